```python
import jax, jax.numpy as jnp
from jax import lax
import numpy as np

D_MODEL = 1024
BATCH = 8
SEQ = 8192
DEPTH = 1

N_MEM = 256
BLOCK = 128
EPS = 1e-6
NEG = -1e30
ROPE_THETA = 500000.0

DIL_GROUPS = ((128, 1), (512, 4), (2048, 16))
N_DIL_GROUPS = 3
A_HEADS = 4
A_HEAD_DIM = 128
A_WIDTH = A_HEADS * A_HEAD_DIM
A_QKV = N_DIL_GROUPS * A_WIDTH
ROT_DIM = A_HEAD_DIM // 4

B_HEADS = 8
B_HEAD_DIM = 64
B_WIDTH = B_HEADS * B_HEAD_DIM

M_HEADS = 4
M_HEAD_DIM = 128
M_WIDTH = M_HEADS * M_HEAD_DIM

N_BRANCH = 3
IN_SIZES = (A_QKV, A_QKV, A_QKV, A_WIDTH,
            B_WIDTH, B_WIDTH, B_WIDTH, B_HEADS, B_WIDTH,
            M_WIDTH, M_WIDTH,
            N_BRANCH * D_MODEL)
IN_COLS = 3 * A_QKV + A_WIDTH + 4 * B_WIDTH + B_HEADS + 2 * M_WIDTH + N_BRANCH * D_MODEL

kernel_name = "hybrid_dilated_fox_memory_gated_block"


def rmsnorm(x, g):
    x32 = x.astype(jnp.float32)
    y = x32 * lax.rsqrt(jnp.mean(x32 * x32, axis=-1, keepdims=True) + EPS)
    return (y * g.astype(jnp.float32)).astype(x.dtype)


def rope_partial(x, pos):
    half = ROT_DIM // 2
    inv = ROPE_THETA ** (-jnp.arange(half, dtype=jnp.float32) / half)
    ang = pos.astype(jnp.float32)[..., None] * inv
    cos = jnp.cos(ang)[:, :, None, :]
    sin = jnp.sin(ang)[:, :, None, :]
    xr = x[..., :ROT_DIM].astype(jnp.float32)
    x1, x2 = xr[..., :half], xr[..., half:]
    rot = jnp.concatenate([x1 * cos - x2 * sin, x2 * cos + x1 * sin], axis=-1).astype(x.dtype)
    return jnp.concatenate([rot, x[..., ROT_DIM:]], axis=-1)


def dilated_window_attention(q, k, v, window, dilation):
    bsz, seq, nh, hd = q.shape
    L = seq // dilation
    steps = window // dilation
    Lp = -(-L // BLOCK) * BLOCK
    nb = Lp // BLOCK

    def to_classes(t):
        t = t.reshape(bsz, L, dilation, nh, hd).transpose(0, 2, 3, 1, 4)
        return jnp.pad(t, ((0, 0), (0, 0), (0, 0), (0, Lp - L), (0, 0)))

    def band(t):
        t = jnp.pad(t, ((0, 0), (0, 0), (0, 0), (BLOCK, 0), (0, 0)))
        t = t.reshape(bsz, dilation, nh, nb + 1, BLOCK, hd)
        return jnp.concatenate([t[:, :, :, :-1], t[:, :, :, 1:]], axis=4)

    qb = to_classes(q).reshape(bsz, dilation, nh, nb, BLOCK, hd)
    kb = band(to_classes(k))
    vb = band(to_classes(v))
    s = jnp.einsum('brhnqd,brhnkd->brhnqk', qb, kb).astype(jnp.float32) * (hd ** -0.5)
    qi = jnp.arange(BLOCK)[:, None]
    ki = jnp.arange(2 * BLOCK)[None, :]
    rel = BLOCK + qi - ki
    blk = jnp.arange(nb)[:, None, None]
    valid = (rel >= 0) & (rel <= steps) & (blk * BLOCK + ki - BLOCK >= 0)
    s = jnp.where(valid, s, NEG)
    m = jnp.max(s, axis=-1, keepdims=True)
    p = jnp.exp(s - m)
    den = jnp.sum(p, axis=-1)
    o = jnp.einsum('brhnqk,brhnkd->brhnqd', p.astype(v.dtype), vb).astype(jnp.float32) / den[..., None]
    lse = m[..., 0] + jnp.log(den)
    o = o.reshape(bsz, dilation, nh, Lp, hd)[:, :, :, :L].transpose(0, 3, 1, 2, 4).reshape(bsz, seq, nh, hd)
    lse = lse.reshape(bsz, dilation, nh, Lp)[..., :L].transpose(0, 3, 1, 2).reshape(bsz, seq, nh)
    return o, lse


def forgetting_attention(q, k, v, log_f):
    bsz, seq, nh, hd = q.shape
    nb = seq // BLOCK
    c = jnp.cumsum(log_f, axis=1).transpose(0, 2, 1)
    kt = k.transpose(0, 2, 1, 3)
    vt = v.transpose(0, 2, 1, 3)
    qb = q.transpose(0, 2, 1, 3).reshape(bsz, nh, nb, BLOCK, hd).transpose(2, 0, 1, 3, 4)
    cb = c.reshape(bsz, nh, nb, BLOCK).transpose(2, 0, 1, 3)
    kpos = jnp.arange(seq)

    def one_block(args):
        n, qn, cn = args
        s = jnp.einsum('bhqd,bhkd->bhqk', qn, kt).astype(jnp.float32) * (hd ** -0.5)
        s = s + cn[..., :, None] - c[:, :, None, :]
        qpos = n * BLOCK + jnp.arange(BLOCK)
        s = jnp.where(kpos[None, :] <= qpos[:, None], s, NEG)
        p = jax.nn.softmax(s, axis=-1)
        return jnp.einsum('bhqk,bhkd->bhqd', p.astype(vt.dtype), vt)

    o = lax.map(one_block, (jnp.arange(nb), qb, cb))
    return o.transpose(1, 0, 3, 2, 4).reshape(bsz, seq, nh, hd)


def memory_attention(q, mem_k, mem_v):
    s = jnp.einsum('bshd,bnhd->bhsn', q, mem_k).astype(jnp.float32) * (q.shape[-1] ** -0.5)
    p = jax.nn.softmax(s, axis=-1)
    return jnp.einsum('bhsn,bnhd->bshd', p.astype(mem_v.dtype), mem_v)


def _fwd_setup_inputs(seed: int = 0) -> dict:
    key = jax.random.key(seed)
    ks = jax.random.split(key, 16)
    f32 = jnp.float32
    x = jax.random.normal(ks[0], (BATCH, SEQ, D_MODEL), f32)
    mem = jax.random.normal(ks[1], (BATCH, N_MEM, D_MODEL), f32)
    start = jax.random.randint(ks[2], (BATCH, 1), 0, 4096, dtype=jnp.int32)
    positions = start + jnp.arange(SEQ, dtype=jnp.int32)[None, :]
    norm_pre_g = 1.0 + 0.05 * jax.random.normal(ks[3], (DEPTH, D_MODEL), f32)
    norm_post_g = 1.0 + 0.05 * jax.random.normal(ks[4], (DEPTH, D_MODEL), f32)
    norm_mem_g = 1.0 + 0.05 * jax.random.normal(ks[5], (DEPTH, D_MODEL), f32)
    w_in = jax.random.normal(ks[6], (DEPTH, D_MODEL, IN_COLS), f32) * D_MODEL ** -0.5
    b_forget = 3.0 + 0.5 * jax.random.normal(ks[7], (DEPTH, B_HEADS), f32)
    b_merge = 0.01 * jax.random.normal(ks[8], (DEPTH, N_BRANCH * D_MODEL), f32)
    w_mem_kv = jax.random.normal(ks[9], (DEPTH, D_MODEL, 2 * M_WIDTH), f32) * D_MODEL ** -0.5
    w_branch_a = jax.random.normal(ks[10], (DEPTH, A_WIDTH, D_MODEL), f32) * A_WIDTH ** -0.5
    w_branch_b = jax.random.normal(ks[11], (DEPTH, B_WIDTH, D_MODEL), f32) * B_WIDTH ** -0.5
    w_branch_m = jax.random.normal(ks[12], (DEPTH, M_WIDTH, D_MODEL), f32) * M_WIDTH ** -0.5
    w_out = jax.random.normal(ks[13], (DEPTH, D_MODEL, D_MODEL), f32) * D_MODEL ** -0.5
    return {"x": x, "mem": mem, "positions": positions,
            "norm_pre_g": norm_pre_g, "norm_post_g": norm_post_g, "norm_mem_g": norm_mem_g,
            "w_in": w_in, "b_forget": b_forget, "b_merge": b_merge, "w_mem_kv": w_mem_kv,
            "w_branch_a": w_branch_a, "w_branch_b": w_branch_b, "w_branch_m": w_branch_m,
            "w_out": w_out}


def _fwd_reference(x, mem, positions, norm_pre_g, norm_post_g, norm_mem_g, w_in, b_forget, b_merge,
              w_mem_kv, w_branch_a, w_branch_b, w_branch_m, w_out):
    bsz, seq, _ = x.shape
    split_at = [int(i) for i in np.cumsum(IN_SIZES)[:-1]]
    for layer in range(DEPTH):
        h = rmsnorm(x, norm_pre_g[layer])
        u = jnp.einsum('bsd,de->bse', h, w_in[layer])
        (qa, ka, va, za, qb, kb, vb, fb, zb, qm, zm, gl) = jnp.split(u, split_at, axis=-1)

        nh_a = N_DIL_GROUPS * A_HEADS
        qa = rope_partial(qa.reshape(bsz, seq, nh_a, A_HEAD_DIM), positions)
        ka = rope_partial(ka.reshape(bsz, seq, nh_a, A_HEAD_DIM), positions)
        va = va.reshape(bsz, seq, nh_a, A_HEAD_DIM)
        outs, lses = [], []
        for g, (window, dilation) in enumerate(DIL_GROUPS):
            sl = slice(g * A_HEADS, (g + 1) * A_HEADS)
            o_g, lse_g = dilated_window_attention(qa[:, :, sl], ka[:, :, sl], va[:, :, sl], window, dilation)
            outs.append(o_g)
            lses.append(lse_g)
        wgt = jax.nn.softmax(jnp.stack(lses, axis=0), axis=0)
        y_a = jnp.sum(wgt[..., None] * jnp.stack(outs, axis=0), axis=0)
        y_a = y_a.astype(x.dtype).reshape(bsz, seq, A_WIDTH) * jax.nn.silu(za)

        log_f = jax.nn.log_sigmoid((fb + b_forget[layer]).astype(jnp.float32))
        y_b = forgetting_attention(qb.reshape(bsz, seq, B_HEADS, B_HEAD_DIM),
                                   kb.reshape(bsz, seq, B_HEADS, B_HEAD_DIM),
                                   vb.reshape(bsz, seq, B_HEADS, B_HEAD_DIM), log_f)
        y_b = y_b.reshape(bsz, seq, B_WIDTH) * jax.nn.silu(zb)

        mkv = jnp.einsum('bnd,de->bne', rmsnorm(mem, norm_mem_g[layer]), w_mem_kv[layer])
        mk, mv = jnp.split(mkv, 2, axis=-1)
        y_m = memory_attention(qm.reshape(bsz, seq, M_HEADS, M_HEAD_DIM),
                               mk.reshape(bsz, N_MEM, M_HEADS, M_HEAD_DIM),
                               mv.reshape(bsz, N_MEM, M_HEADS, M_HEAD_DIM))
        y_m = y_m.reshape(bsz, seq, M_WIDTH) * jax.nn.silu(zm)

        gates = jax.nn.sigmoid(gl + b_merge[layer]).reshape(bsz, seq, N_BRANCH, D_MODEL)
        merged = (gates[:, :, 0] * jnp.einsum('bse,ed->bsd', y_a, w_branch_a[layer])
                  + gates[:, :, 1] * jnp.einsum('bse,ed->bsd', y_b, w_branch_b[layer])
                  + gates[:, :, 2] * jnp.einsum('bse,ed->bsd', y_m, w_branch_m[layer]))
        out = jnp.einsum('bsd,de->bse', merged, w_out[layer])
        x = x + rmsnorm(out, norm_post_g[layer])
    return x


import jax as _jax
import jax.numpy as _jnp

TWIN_FORMAT = 'train_step'
FWD_PARAMS = ['x', 'mem', 'positions', 'norm_pre_g', 'norm_post_g', 'norm_mem_g', 'w_in', 'b_forget', 'b_merge', 'w_mem_kv', 'w_branch_a', 'w_branch_b', 'w_branch_m', 'w_out']
TWIN_WEIGHTS = ['norm_pre_g', 'norm_post_g', 'norm_mem_g', 'w_in', 'b_forget', 'b_merge', 'w_mem_kv', 'w_branch_a', 'w_branch_b', 'w_branch_m', 'w_out']
TWIN_DIFF_INPUT = 'x'
TWIN_INPUTS = ['x', 'mem', 'positions', 'norm_pre_g', 'norm_post_g', 'norm_mem_g', 'w_in', 'b_forget', 'b_merge', 'w_mem_kv', 'w_branch_a', 'w_branch_b', 'w_branch_m', 'w_out', 'loss_target', 'm_norm_pre_g', 'm_norm_post_g', 'm_norm_mem_g', 'm_w_in', 'm_b_forget', 'm_b_merge', 'm_w_mem_kv', 'm_w_branch_a', 'm_w_branch_b', 'm_w_branch_m', 'm_w_out', 'v_norm_pre_g', 'v_norm_post_g', 'v_norm_mem_g', 'v_w_in', 'v_b_forget', 'v_b_merge', 'v_w_mem_kv', 'v_w_branch_a', 'v_w_branch_b', 'v_w_branch_m', 'v_w_out']
TWIN_OUTPUTS = ['loss', 'grad_x', 'grad_norm_pre_g', 'grad_norm_post_g', 'grad_norm_mem_g', 'grad_w_in', 'grad_b_forget', 'grad_b_merge', 'grad_w_mem_kv', 'grad_w_branch_a', 'grad_w_branch_b', 'grad_w_branch_m', 'grad_w_out', 'delta_norm_pre_g', 'delta_norm_post_g', 'delta_norm_mem_g', 'delta_w_in', 'delta_b_forget', 'delta_b_merge', 'delta_w_mem_kv', 'delta_w_branch_a', 'delta_w_branch_b', 'delta_w_branch_m', 'delta_w_out', 'new_m_norm_pre_g', 'new_m_norm_post_g', 'new_m_norm_mem_g', 'new_m_w_in', 'new_m_b_forget', 'new_m_b_merge', 'new_m_w_mem_kv', 'new_m_w_branch_a', 'new_m_w_branch_b', 'new_m_w_branch_m', 'new_m_w_out', 'new_v_norm_pre_g', 'new_v_norm_post_g', 'new_v_norm_mem_g', 'new_v_w_in', 'new_v_b_forget', 'new_v_b_merge', 'new_v_w_mem_kv', 'new_v_w_branch_a', 'new_v_w_branch_b', 'new_v_w_branch_m', 'new_v_w_out']
TWIN_LEAF_KINDS = {'loss': 'loss', 'grad_x': 'grad_x', 'grad_norm_pre_g': 'grad_w', 'grad_norm_post_g': 'grad_w', 'grad_norm_mem_g': 'grad_w', 'grad_w_in': 'grad_w', 'grad_b_forget': 'grad_w', 'grad_b_merge': 'grad_w', 'grad_w_mem_kv': 'grad_w', 'grad_w_branch_a': 'grad_w', 'grad_w_branch_b': 'grad_w', 'grad_w_branch_m': 'grad_w', 'grad_w_out': 'grad_w', 'delta_norm_pre_g': 'delta_w', 'delta_norm_post_g': 'delta_w', 'delta_norm_mem_g': 'delta_w', 'delta_w_in': 'delta_w', 'delta_b_forget': 'delta_w', 'delta_b_merge': 'delta_w', 'delta_w_mem_kv': 'delta_w', 'delta_w_branch_a': 'delta_w', 'delta_w_branch_b': 'delta_w', 'delta_w_branch_m': 'delta_w', 'delta_w_out': 'delta_w', 'new_m_norm_pre_g': 'new_m', 'new_m_norm_post_g': 'new_m', 'new_m_norm_mem_g': 'new_m', 'new_m_w_in': 'new_m', 'new_m_b_forget': 'new_m', 'new_m_b_merge': 'new_m', 'new_m_w_mem_kv': 'new_m', 'new_m_w_branch_a': 'new_m', 'new_m_w_branch_b': 'new_m', 'new_m_w_branch_m': 'new_m', 'new_m_w_out': 'new_m', 'new_v_norm_pre_g': 'new_v', 'new_v_norm_post_g': 'new_v', 'new_v_norm_mem_g': 'new_v', 'new_v_w_in': 'new_v', 'new_v_b_forget': 'new_v', 'new_v_b_merge': 'new_v', 'new_v_w_mem_kv': 'new_v', 'new_v_w_branch_a': 'new_v', 'new_v_w_branch_b': 'new_v', 'new_v_w_branch_m': 'new_v', 'new_v_w_out': 'new_v'}


def _forward(args):
    return _fwd_reference(*[args[k] for k in FWD_PARAMS])


def _output_shape():
    out = _jax.eval_shape(lambda: _forward(_fwd_setup_inputs(0)))
    return out.shape, out.dtype

N_MICROBATCH = 1
ADAM_LR = 0.001
ADAM_B1 = 0.9
ADAM_B2 = 0.999
ADAM_EPS = 1e-08
ADAM_WD = 0.01
ADAM_STEP = 10
PER_EXAMPLE_BATCH_AXIS = {'x': 0, 'mem': 0, 'positions': 0, 'loss_target': 0}
SHARED_INPUTS = []
_WEIGHT_DTYPES = {'norm_pre_g': _jnp.float32, 'norm_post_g': _jnp.float32, 'norm_mem_g': _jnp.float32, 'w_in': _jnp.float32, 'b_forget': _jnp.float32, 'b_merge': _jnp.float32, 'w_mem_kv': _jnp.float32, 'w_branch_a': _jnp.float32, 'w_branch_b': _jnp.float32, 'w_branch_m': _jnp.float32, 'w_out': _jnp.float32}
MOMENT_SCALE = {'norm_pre_g': 7.339919e-01, 'norm_post_g': 6.386230e+01, 'norm_mem_g': 1.829470e-01, 'w_in': 2.043542e-01, 'b_forget': 3.967482e+00, 'b_merge': 9.444482e-02, 'w_mem_kv': 1.806060e-01, 'w_branch_a': 1.491596e-01, 'w_branch_b': 3.654743e-01, 'w_branch_m': 1.402015e-01, 'w_out': 3.878346e-01}


def _to_microbatches(a, axis):
    t = _jnp.moveaxis(a, axis, 0)
    t = t.reshape((N_MICROBATCH, t.shape[0] // N_MICROBATCH) + t.shape[1:])
    return _jnp.moveaxis(t, 1, axis + 1)


def setup_inputs(seed: int = 0) -> dict:
    inp = _fwd_setup_inputs(seed)
    key = _jax.random.fold_in(_jax.random.key(seed), 7919)
    shape, _ = _output_shape()
    out = dict(inp)
    out["loss_target"] = _jax.random.normal(_jax.random.fold_in(key, 0), shape, _jnp.float32)
    for i, name in enumerate(TWIN_WEIGHTS):
        w = inp[name].astype(_jnp.float32)
        if MOMENT_SCALE is None:
            s = _jnp.sqrt(_jnp.mean(_jnp.square(w)) + 1e-30)
        else:
            s = MOMENT_SCALE[name]
        km, kv = _jax.random.split(_jax.random.fold_in(key, i + 1))
        out[name] = w
        out["m_" + name] = s * _jax.random.normal(km, w.shape, _jnp.float32)
        out["v_" + name] = (s * s) * _jax.random.uniform(kv, w.shape, _jnp.float32, 0.5, 1.5)
    if N_MICROBATCH > 1:
        for name, axis in PER_EXAMPLE_BATCH_AXIS.items():
            out[name] = _to_microbatches(out[name], axis)
    return {'x': out['x'], 'mem': out['mem'], 'positions': out['positions'], 'norm_pre_g': out['norm_pre_g'], 'norm_post_g': out['norm_post_g'], 'norm_mem_g': out['norm_mem_g'], 'w_in': out['w_in'], 'b_forget': out['b_forget'], 'b_merge': out['b_merge'], 'w_mem_kv': out['w_mem_kv'], 'w_branch_a': out['w_branch_a'], 'w_branch_b': out['w_branch_b'], 'w_branch_m': out['w_branch_m'], 'w_out': out['w_out'], 'loss_target': out['loss_target'], 'm_norm_pre_g': out['m_norm_pre_g'], 'm_norm_post_g': out['m_norm_post_g'], 'm_norm_mem_g': out['m_norm_mem_g'], 'm_w_in': out['m_w_in'], 'm_b_forget': out['m_b_forget'], 'm_b_merge': out['m_b_merge'], 'm_w_mem_kv': out['m_w_mem_kv'], 'm_w_branch_a': out['m_w_branch_a'], 'm_w_branch_b': out['m_w_branch_b'], 'm_w_branch_m': out['m_w_branch_m'], 'm_w_out': out['m_w_out'], 'v_norm_pre_g': out['v_norm_pre_g'], 'v_norm_post_g': out['v_norm_post_g'], 'v_norm_mem_g': out['v_norm_mem_g'], 'v_w_in': out['v_w_in'], 'v_b_forget': out['v_b_forget'], 'v_b_merge': out['v_b_merge'], 'v_w_mem_kv': out['v_w_mem_kv'], 'v_w_branch_a': out['v_w_branch_a'], 'v_w_branch_b': out['v_w_branch_b'], 'v_w_branch_m': out['v_w_branch_m'], 'v_w_out': out['v_w_out']}


def _loss(weights, diff, rest, loss_target):
    with _jax.named_scope("forward"):
        args = {**rest, TWIN_DIFF_INPUT: diff, **{k: w.astype(_WEIGHT_DTYPES[k]) for k, w in weights.items()}}
        y = _forward(args)
    with _jax.named_scope("loss_head"):
        err = _jnp.square(y.astype(_jnp.float32) - loss_target)
        return 0.5 * _jnp.sum(_jnp.mean(err, axis=-1)) if err.ndim else 0.5 * err


def _adamw(w, g, m, v):
    m = ADAM_B1 * m + (1.0 - ADAM_B1) * g
    v = ADAM_B2 * v + (1.0 - ADAM_B2) * _jnp.square(g)
    m_hat = m / (1.0 - ADAM_B1 ** ADAM_STEP)
    v_hat = v / (1.0 - ADAM_B2 ** ADAM_STEP)
    delta = -ADAM_LR * (m_hat / (_jnp.sqrt(v_hat) + ADAM_EPS) + ADAM_WD * w)
    return delta, m, v


def reference(x, mem, positions, norm_pre_g, norm_post_g, norm_mem_g, w_in, b_forget, b_merge, w_mem_kv, w_branch_a, w_branch_b, w_branch_m, w_out, loss_target, m_norm_pre_g, m_norm_post_g, m_norm_mem_g, m_w_in, m_b_forget, m_b_merge, m_w_mem_kv, m_w_branch_a, m_w_branch_b, m_w_branch_m, m_w_out, v_norm_pre_g, v_norm_post_g, v_norm_mem_g, v_w_in, v_b_forget, v_b_merge, v_w_mem_kv, v_w_branch_a, v_w_branch_b, v_w_branch_m, v_w_out):
    given = dict(x=x, mem=mem, positions=positions, norm_pre_g=norm_pre_g, norm_post_g=norm_post_g, norm_mem_g=norm_mem_g, w_in=w_in, b_forget=b_forget, b_merge=b_merge, w_mem_kv=w_mem_kv, w_branch_a=w_branch_a, w_branch_b=w_branch_b, w_branch_m=w_branch_m, w_out=w_out, loss_target=loss_target, m_norm_pre_g=m_norm_pre_g, m_norm_post_g=m_norm_post_g, m_norm_mem_g=m_norm_mem_g, m_w_in=m_w_in, m_b_forget=m_b_forget, m_b_merge=m_b_merge, m_w_mem_kv=m_w_mem_kv, m_w_branch_a=m_w_branch_a, m_w_branch_b=m_w_branch_b, m_w_branch_m=m_w_branch_m, m_w_out=m_w_out, v_norm_pre_g=v_norm_pre_g, v_norm_post_g=v_norm_post_g, v_norm_mem_g=v_norm_mem_g, v_w_in=v_w_in, v_b_forget=v_b_forget, v_b_merge=v_b_merge, v_w_mem_kv=v_w_mem_kv, v_w_branch_a=v_w_branch_a, v_w_branch_b=v_w_branch_b, v_w_branch_m=v_w_branch_m, v_w_out=v_w_out)
    weights = {n: given[n] for n in TWIN_WEIGHTS}
    shared = {n: given[n] for n in SHARED_INPUTS}
    per_example = {n: given[n] for n in ['x', 'mem', 'positions']}
    grad_fn = _jax.value_and_grad(_loss, argnums=(0, 1))

    def one_microbatch(ex, loss_target):
        ex = dict(ex)
        diff = ex.pop(TWIN_DIFF_INPUT)
        return grad_fn(weights, diff, {**shared, **ex}, loss_target)

    if N_MICROBATCH == 1:
        loss, (grad_w, grad_x) = one_microbatch(per_example, given["loss_target"])
    else:
        def body(carry, xs):
            loss_sum, grad_sum = carry
            l_k, (gw_k, gx_k) = one_microbatch(xs[0], xs[1])
            with _jax.named_scope("update"):
                return (loss_sum + l_k, _jax.tree.map(_jnp.add, grad_sum, gw_k)), gx_k

        init = (_jnp.zeros((), _jnp.float32), _jax.tree.map(_jnp.zeros_like, weights))
        (loss, grad_w), grad_x = _jax.lax.scan(body, init, (per_example, given["loss_target"]))
    with _jax.named_scope("update"):
        delta_w, new_m, new_v = {}, {}, {}
        for n in TWIN_WEIGHTS:
            delta_w[n], new_m[n], new_v[n] = _adamw(weights[n], grad_w[n], given["m_" + n], given["v_" + n])
    return (loss, grad_x, *[grad_w[n] for n in TWIN_WEIGHTS], *[delta_w[n] for n in TWIN_WEIGHTS],
            *[new_m[n] for n in TWIN_WEIGHTS], *[new_v[n] for n in TWIN_WEIGHTS])
```

```python
import functools
import math

import numpy as np
import jax
import jax.numpy as jnp
from jax import lax
from jax.experimental import pallas as pl
from jax.experimental.pallas import tpu as pltpu

F32 = jnp.float32
BF16 = jnp.bfloat16

N_DEV = 8
D_MODEL = 1024
N_MEM = 256
EPS = 1e-6
NEG = -1e30
ROPE_THETA = 500000.0
DIL = (1, 4, 16)
A_HEADS = 4
HEAD = 128
A_WIDTH = 512
B_HEADS = 8
B_HEAD = 64
M_HEADS = 4
ROT = 32

IN_COLS = 11272
C_QA, C_KA, C_VA, C_ZA = 0, 1536, 3072, 4608
C_QB, C_KB, C_VB, C_ZB = 5120, 5632, 6144, 6656
C_QM, C_ZM, C_GL, C_FB = 7168, 7680, 8192, 11264
FB_PAD = 256
NC = C_FB + FB_PAD
FB_ORIG = 6656

ADAM_LR, ADAM_B1, ADAM_B2, ADAM_EPS, ADAM_WD, ADAM_STEP = 0.001, 0.9, 0.999, 1e-08, 0.01, 10

LANES = 128
VMEM_LIMIT = 56 * 1024 * 1024

SH_IN = D_MODEL * (IN_COLS // N_DEV)
SH_KV = (D_MODEL // N_DEV) * D_MODEL
SH_BR = A_WIDTH * (D_MODEL // N_DEV)
SH_OUT = (D_MODEL // N_DEV) * D_MODEL
P_BIG = SH_IN + SH_KV + 3 * SH_BR + SH_OUT
P_PAD = 15 * 131072
O_GPRE, O_GPOST, O_GMEM, O_BM, O_BF, O_LOSS = 0, 1024, 2048, 3072, 6144, 6272
P_SMALL = 6400


def _cp(sem=None):
    return pltpu.CompilerParams(dimension_semantics=sem, vmem_limit_bytes=VMEM_LIMIT)


def _dot(a, b):
    return jnp.dot(a, b, preferred_element_type=F32)


def _dot_nt(a, b):
    return lax.dot_general(a, b, (((1,), (1,)), ((), ())), preferred_element_type=F32)


def _sigmoid(z):
    return 1.0 / (1.0 + jnp.exp(-z))


def _mm(a, b, *, name, bt=False, out_dtype=F32, tm=1024, tn=1024, tk=None):
    M, K = a.shape
    N = b.shape[0] if bt else b.shape[1]
    tm, tn = min(tm, M), min(tn, N)
    tk = K if tk is None else min(tk, K)
    assert M % tm == 0 and N % tn == 0 and K % tk == 0
    nk = K // tk

    def body(a_ref, b_ref, o_ref, acc_ref):
        av = a_ref[...].astype(BF16)
        bv = b_ref[...].astype(BF16)
        p = _dot_nt(av, bv) if bt else _dot(av, bv)
        if nk == 1:
            o_ref[...] = p.astype(out_dtype)
        else:
            k = pl.program_id(2)

            @pl.when(k == 0)
            def _():
                acc_ref[...] = p

            @pl.when(k > 0)
            def _():
                acc_ref[...] += p

            @pl.when(k == nk - 1)
            def _():
                o_ref[...] = acc_ref[...].astype(out_dtype)

    b_spec = (pl.BlockSpec((tn, tk), lambda i, j, k: (j, k)) if bt
              else pl.BlockSpec((tk, tn), lambda i, j, k: (k, j)))
    return pl.pallas_call(
        body, name=name, grid=(M // tm, N // tn, nk),
        in_specs=[pl.BlockSpec((tm, tk), lambda i, j, k: (i, k)), b_spec],
        out_specs=pl.BlockSpec((tm, tn), lambda i, j, k: (i, j)),
        out_shape=jax.ShapeDtypeStruct((M, N), out_dtype),
        scratch_shapes=[pltpu.VMEM((tm, tn) if nk > 1 else (8, LANES), F32)],
        compiler_params=_cp(("parallel", "parallel", "arbitrary")),
    )(a, b)


def _rms_fwd(x, g, *, name):
    S, D = x.shape
    tm = min(512, S)

    def body(x_ref, g_ref, o_ref):
        xv = x_ref[...]
        r = lax.rsqrt(jnp.mean(xv * xv, axis=-1, keepdims=True) + EPS)
        o_ref[...] = (xv * r * g_ref[...]).astype(BF16)

    return pl.pallas_call(
        body, name=name, grid=(S // tm,),
        in_specs=[pl.BlockSpec((tm, D), lambda i: (i, 0)), pl.BlockSpec((1, D), lambda i: (0, 0))],
        out_specs=pl.BlockSpec((tm, D), lambda i: (i, 0)),
        out_shape=jax.ShapeDtypeStruct((S, D), BF16),
        compiler_params=_cp(("parallel",)),
    )(x, g)


def _rms_bwd(x, g, dh, dy, *, name):
    S, D = x.shape
    tm = min(512, S)
    want_dx = dy is not None

    def body(*refs):
        if want_dx:
            x_ref, g_ref, dh_ref, dy_ref, dx_ref, dg_ref = refs
        else:
            x_ref, g_ref, dh_ref, dg_ref = refs
        i = pl.program_id(0)
        xv = x_ref[...]
        r = lax.rsqrt(jnp.mean(xv * xv, axis=-1, keepdims=True) + EPS)
        xh = xv * r
        dhv = dh_ref[...]
        part = jnp.sum(dhv * xh, axis=0, keepdims=True)

        @pl.when(i == 0)
        def _():
            dg_ref[...] = part

        @pl.when(i > 0)
        def _():
            dg_ref[...] += part

        if want_dx:
            dxh = dhv * g_ref[...]
            dx_ref[...] = dy_ref[...] + r * (dxh - xh * jnp.mean(dxh * xh, axis=-1, keepdims=True))

    row = pl.BlockSpec((tm, D), lambda i: (i, 0))
    vec = pl.BlockSpec((1, D), lambda i: (0, 0))
    if want_dx:
        return pl.pallas_call(
            body, name=name, grid=(S // tm,), in_specs=[row, vec, row, row], out_specs=[row, vec],
            out_shape=[jax.ShapeDtypeStruct((S, D), F32), jax.ShapeDtypeStruct((1, D), F32)],
            compiler_params=_cp(("arbitrary",)))(x, g, dh, dy)
    return pl.pallas_call(
        body, name=name, grid=(S // tm,), in_specs=[row, vec, row], out_specs=vec,
        out_shape=jax.ShapeDtypeStruct((1, D), F32),
        compiler_params=_cp(("arbitrary",)))(x, g, dh)


def _post(x, out, tgt, g, *, name):
    S, D = x.shape
    tm = min(512, S)

    def body(x_ref, o_ref, t_ref, g_ref, dy_ref, do_ref, dg_ref, loss_ref):
        i = pl.program_id(0)
        ov = o_ref[...]
        r = lax.rsqrt(jnp.mean(ov * ov, axis=-1, keepdims=True) + EPS)
        n = ov * r
        gv = g_ref[...]
        e = (x_ref[...] + n * gv) - t_ref[...]
        lpart = 0.5 * jnp.sum(jnp.mean(e * e, axis=-1, keepdims=True), axis=0, keepdims=True)
        dy = e * (1.0 / D)
        dy_ref[...] = dy
        dn = dy * gv
        do_ref[...] = (r * (dn - n * jnp.mean(dn * n, axis=-1, keepdims=True))).astype(BF16)
        gpart = jnp.sum(dy * n, axis=0, keepdims=True)
        lrow = jnp.broadcast_to(lpart, (1, LANES))

        @pl.when(i == 0)
        def _():
            dg_ref[...] = gpart
            loss_ref[...] = lrow

        @pl.when(i > 0)
        def _():
            dg_ref[...] += gpart
            loss_ref[...] += lrow

    row = pl.BlockSpec((tm, D), lambda i: (i, 0))
    vec = pl.BlockSpec((1, D), lambda i: (0, 0))
    return pl.pallas_call(
        body, name=name, grid=(S // tm,), in_specs=[row, row, row, vec],
        out_specs=[row, row, vec, pl.BlockSpec((1, LANES), lambda i: (0, 0))],
        out_shape=[jax.ShapeDtypeStruct((S, D), F32), jax.ShapeDtypeStruct((S, D), BF16),
                   jax.ShapeDtypeStruct((1, D), F32), jax.ShapeDtypeStruct((1, LANES), F32)],
        compiler_params=_cp(("arbitrary",)))(x, out, tgt, g)


def _rope(x, c, s1, s2):
    return x * c + pltpu.roll(x, LANES - ROT // 2, 1) * s1 + pltpu.roll(x, ROT // 2, 1) * s2


def _unrope(d, c, s1, s2):
    return d * c + pltpu.roll(d * s1, ROT // 2, 1) + pltpu.roll(d * s2, LANES - ROT // 2, 1)


def _a_masks():
    qi = lax.broadcasted_iota(jnp.int32, (HEAD, HEAD), 0)
    ki = lax.broadcasted_iota(jnp.int32, (HEAD, HEAD), 1)
    return ki >= qi, ki <= qi


A_SCALE = HEAD ** -0.5


def _attn_a_fwd(u, tabs, g, *, name):
    S = u.shape[0]
    d = DIL[g]
    L = S // d
    TQ = min(512, L)
    nsub, nb, ncb = TQ // HEAD, L // TQ, NC // LANES
    u2 = u.reshape(L, d * NC)
    tabs2 = [t.reshape(L, d * LANES) for t in tabs]

    def body(q_ref, kc_ref, kp_ref, vc_ref, vp_ref, c_ref, s1_ref, s2_ref, cp_ref, s1p_ref, s2p_ref,
             o_ref, l_ref):
        n = pl.program_id(1)
        tc = (c_ref[...], s1_ref[...], s2_ref[...])
        q = _rope(q_ref[...], *tc).astype(BF16)
        kc = _rope(kc_ref[...], *tc).astype(BF16)
        kp = _rope(kp_ref[...], cp_ref[...], s1p_ref[...], s2p_ref[...]).astype(BF16)
        vc = vc_ref[...].astype(BF16)
        vp = vp_ref[...].astype(BF16)
        mprev, mcur = _a_masks()
        for a in range(nsub):
            sl = slice(a * HEAD, (a + 1) * HEAD)
            pv = slice((a - 1) * HEAD, a * HEAD)
            qa = q[sl]
            k_prev, v_prev = (kp, vp) if a == 0 else (kc[pv], vc[pv])
            mp = jnp.logical_and(mprev, n > 0) if a == 0 else mprev
            s_p = jnp.where(mp, _dot_nt(qa, k_prev) * A_SCALE, NEG)
            s_c = jnp.where(mcur, _dot_nt(qa, kc[sl]) * A_SCALE, NEG)
            m = jnp.maximum(jnp.max(s_p, axis=-1, keepdims=True), jnp.max(s_c, axis=-1, keepdims=True))
            p_p = jnp.exp(s_p - m)
            p_c = jnp.exp(s_c - m)
            den = jnp.sum(p_p, axis=-1, keepdims=True) + jnp.sum(p_c, axis=-1, keepdims=True)
            o = (_dot(p_p.astype(BF16), v_prev) + _dot(p_c.astype(BF16), vc[sl])) / den
            o_ref[sl, :] = o
            l_ref[sl, :] = jnp.broadcast_to(m + jnp.log(den), (HEAD, HEAD))

    def col(cb, off):
        return (cb // A_HEADS) * ncb + off + g * A_HEADS + cb % A_HEADS

    prev = lambda n: jnp.maximum(n * nsub - 1, 0)
    cur = lambda off: pl.BlockSpec((TQ, HEAD), lambda cb, n: (n, col(cb, off)))
    prv = lambda off: pl.BlockSpec((HEAD, HEAD), lambda cb, n: (prev(n), col(cb, off)))
    tcur = pl.BlockSpec((TQ, LANES), lambda cb, n: (n, cb // A_HEADS))
    tprv = pl.BlockSpec((HEAD, LANES), lambda cb, n: (prev(n), cb // A_HEADS))
    oq, ok, ov = C_QA // LANES, C_KA // LANES, C_VA // LANES
    out = pl.BlockSpec((TQ, HEAD), lambda cb, n: (n, cb))
    o, lse = pl.pallas_call(
        body, name=name, grid=(A_HEADS * d, nb),
        in_specs=[cur(oq), cur(ok), prv(ok), cur(ov), prv(ov), tcur, tcur, tcur, tprv, tprv, tprv],
        out_specs=[out, out],
        out_shape=[jax.ShapeDtypeStruct((L, d * A_WIDTH), F32)] * 2,
        compiler_params=_cp(("parallel", "parallel")),
    )(u2, u2, u2, u2, u2, *tabs2, *tabs2)
    return o.reshape(S, A_WIDTH), lse.reshape(S, A_WIDTH)


def _attn_a_dq(u, tabs, g, do, lse, adj, *, name):
    S = u.shape[0]
    d = DIL[g]
    L = S // d
    TQ = min(512, L)
    nsub, nb, ncb = TQ // HEAD, L // TQ, NC // LANES
    u2 = u.reshape(L, d * NC)
    tabs2 = [t.reshape(L, d * LANES) for t in tabs]
    do2, lse2, adj2 = (t.reshape(L, d * A_WIDTH) for t in (do, lse, adj))

    def body(q_ref, kc_ref, kp_ref, vc_ref, vp_ref, do_ref, l_ref, adj_ref,
             c_ref, s1_ref, s2_ref, cp_ref, s1p_ref, s2p_ref, dq_ref):
        n = pl.program_id(1)
        tc = (c_ref[...], s1_ref[...], s2_ref[...])
        q = _rope(q_ref[...], *tc).astype(BF16)
        kc = _rope(kc_ref[...], *tc).astype(BF16)
        kp = _rope(kp_ref[...], cp_ref[...], s1p_ref[...], s2p_ref[...]).astype(BF16)
        vc = vc_ref[...].astype(BF16)
        vp = vp_ref[...].astype(BF16)
        mprev, mcur = _a_masks()
        for a in range(nsub):
            sl = slice(a * HEAD, (a + 1) * HEAD)
            pv = slice((a - 1) * HEAD, a * HEAD)
            qa = q[sl]
            k_prev, v_prev = (kp, vp) if a == 0 else (kc[pv], vc[pv])
            mp = jnp.logical_and(mprev, n > 0) if a == 0 else mprev
            lse_a = l_ref[sl, :][:, :1]
            adj_a = adj_ref[sl, :][:, :1]
            doa = do_ref[sl, :]
            p_p = jnp.exp(jnp.where(mp, _dot_nt(qa, k_prev) * A_SCALE, NEG) - lse_a)
            p_c = jnp.exp(jnp.where(mcur, _dot_nt(qa, kc[sl]) * A_SCALE, NEG) - lse_a)
            ds_p = p_p * (_dot_nt(doa, v_prev) + adj_a)
            ds_c = p_c * (_dot_nt(doa, vc[sl]) + adj_a)
            dq = (_dot(ds_p.astype(BF16), k_prev) + _dot(ds_c.astype(BF16), kc[sl])) * A_SCALE
            dq_ref[sl, :] = _unrope(dq, c_ref[sl, :], s1_ref[sl, :], s2_ref[sl, :]).astype(BF16)

    def col(cb, off):
        return (cb // A_HEADS) * ncb + off + g * A_HEADS + cb % A_HEADS

    prev = lambda n: jnp.maximum(n * nsub - 1, 0)
    cur = lambda off: pl.BlockSpec((TQ, HEAD), lambda cb, n: (n, col(cb, off)))
    prv = lambda off: pl.BlockSpec((HEAD, HEAD), lambda cb, n: (prev(n), col(cb, off)))
    tcur = pl.BlockSpec((TQ, LANES), lambda cb, n: (n, cb // A_HEADS))
    tprv = pl.BlockSpec((HEAD, LANES), lambda cb, n: (prev(n), cb // A_HEADS))
    oq, ok, ov = C_QA // LANES, C_KA // LANES, C_VA // LANES
    blk = pl.BlockSpec((TQ, HEAD), lambda cb, n: (n, cb))
    dq = pl.pallas_call(
        body, name=name, grid=(A_HEADS * d, nb),
        in_specs=[cur(oq), cur(ok), prv(ok), cur(ov), prv(ov), blk, blk, blk,
                  tcur, tcur, tcur, tprv, tprv, tprv],
        out_specs=blk,
        out_shape=jax.ShapeDtypeStruct((L, d * A_WIDTH), BF16),
        compiler_params=_cp(("parallel", "parallel")),
    )(u2, u2, u2, u2, u2, do2, lse2, adj2, *tabs2, *tabs2)
    return dq.reshape(S, A_WIDTH)


def _attn_a_dkv(u, tabs, g, do, lse, adj, *, name):
    S = u.shape[0]
    d = DIL[g]
    L = S // d
    TQ = min(512, L)
    nsub, nb, ncb = TQ // HEAD, L // TQ, NC // LANES
    nblk = L // HEAD
    u2 = u.reshape(L, d * NC)
    tabs2 = [t.reshape(L, d * LANES) for t in tabs]
    do2, lse2, adj2 = (t.reshape(L, d * A_WIDTH) for t in (do, lse, adj))

    def body(qc_ref, qn_ref, kc_ref, vc_ref, doc_ref, don_ref, lc_ref, ln_ref, ac_ref, an_ref,
             c_ref, s1_ref, s2_ref, cn_ref, s1n_ref, s2n_ref, dk_ref, dv_ref):
        n = pl.program_id(1)
        tc = (c_ref[...], s1_ref[...], s2_ref[...])
        qc = _rope(qc_ref[...], *tc).astype(BF16)
        qn = _rope(qn_ref[...], cn_ref[...], s1n_ref[...], s2n_ref[...]).astype(BF16)
        kc = _rope(kc_ref[...], *tc).astype(BF16)
        vc = vc_ref[...].astype(BF16)
        kr = lax.broadcasted_iota(jnp.int32, (HEAD, HEAD), 0)
        qc_i = lax.broadcasted_iota(jnp.int32, (HEAD, HEAD), 1)
        own_t = kr <= qc_i
        nxt_t = kr >= qc_i
        has_next = n < nb - 1
        for b in range(nsub):
            sl = slice(b * HEAD, (b + 1) * HEAD)
            nx = slice((b + 1) * HEAD, (b + 2) * HEAD)
            kb, vb = kc[sl], vc[sl]
            last = b == nsub - 1
            parts = [(qc[sl], doc_ref[sl, :], lc_ref[sl, :], ac_ref[sl, :], own_t)]
            if last:
                parts.append((qn, don_ref[...], ln_ref[...], an_ref[...], jnp.logical_and(nxt_t, has_next)))
            else:
                parts.append((qc[nx], doc_ref[nx, :], lc_ref[nx, :], ac_ref[nx, :], nxt_t))
            dk = jnp.zeros((HEAD, HEAD), F32)
            dv = jnp.zeros((HEAD, HEAD), F32)
            for qq, dd, ll, aa, msk in parts:
                st = jnp.where(msk, _dot_nt(kb, qq) * A_SCALE, NEG)
                pt = jnp.exp(st - ll.T)
                dv = dv + _dot(pt.astype(BF16), dd)
                dst = pt * (_dot_nt(vb, dd) + aa.T)
                dk = dk + _dot(dst.astype(BF16), qq)
            dk = dk * A_SCALE
            dk_ref[sl, :] = _unrope(dk, c_ref[sl, :], s1_ref[sl, :], s2_ref[sl, :]).astype(BF16)
            dv_ref[sl, :] = dv.astype(BF16)

    def col(cb, off):
        return (cb // A_HEADS) * ncb + off + g * A_HEADS + cb % A_HEADS

    nxt = lambda n: jnp.minimum((n + 1) * nsub, nblk - 1)
    cur = lambda off: pl.BlockSpec((TQ, HEAD), lambda cb, n: (n, col(cb, off)))
    nxu = lambda off: pl.BlockSpec((HEAD, HEAD), lambda cb, n: (nxt(n), col(cb, off)))
    tcur = pl.BlockSpec((TQ, LANES), lambda cb, n: (n, cb // A_HEADS))
    tnxt = pl.BlockSpec((HEAD, LANES), lambda cb, n: (nxt(n), cb // A_HEADS))
    oq, ok, ov = C_QA // LANES, C_KA // LANES, C_VA // LANES
    blk = pl.BlockSpec((TQ, HEAD), lambda cb, n: (n, cb))
    bnx = pl.BlockSpec((HEAD, HEAD), lambda cb, n: (nxt(n), cb))
    dk, dv = pl.pallas_call(
        body, name=name, grid=(A_HEADS * d, nb),
        in_specs=[cur(oq), nxu(oq), cur(ok), cur(ov), blk, bnx, blk, bnx, blk, bnx,
                  tcur, tcur, tcur, tnxt, tnxt, tnxt],
        out_specs=[blk, blk],
        out_shape=[jax.ShapeDtypeStruct((L, d * A_WIDTH), BF16)] * 2,
        compiler_params=_cp(("parallel", "parallel")),
    )(u2, u2, u2, u2, do2, do2, lse2, lse2, adj2, adj2, *tabs2, *tabs2)
    return dk.reshape(S, A_WIDTH), dv.reshape(S, A_WIDTH)


def _silu_parts(z):
    sg = _sigmoid(z)
    return z * sg, sg * (1.0 + z * (1.0 - sg))


def _merge_a_fwd(os_, ls_, u, *, name):
    S = u.shape[0]
    tm = min(512, S)

    def body(o0, o1, o2, l0, l1, l2, z_ref, y_ref):
        ls = [l0[...], l1[...], l2[...]]
        mx = jnp.maximum(jnp.maximum(ls[0], ls[1]), ls[2])
        es = [jnp.exp(l - mx) for l in ls]
        den = es[0] + es[1] + es[2]
        y = (es[0] / den) * o0[...] + (es[1] / den) * o1[...] + (es[2] / den) * o2[...]
        y_ref[...] = (y * _silu_parts(z_ref[...])[0]).astype(BF16)

    blk = pl.BlockSpec((tm, A_WIDTH), lambda i: (i, 0))
    return pl.pallas_call(
        body, name=name, grid=(S // tm,),
        in_specs=[blk] * 6 + [pl.BlockSpec((tm, A_WIDTH), lambda i: (i, C_ZA // A_WIDTH))],
        out_specs=blk, out_shape=jax.ShapeDtypeStruct((S, A_WIDTH), BF16),
        compiler_params=_cp(("parallel",)))(*os_, *ls_, u)


def _merge_a_bwd(os_, ls_, u, dya, *, name):
    S = u.shape[0]
    tm = min(512, S)

    def body(o0, o1, o2, l0, l1, l2, z_ref, dy_ref, d0, d1, d2, a0, a1, a2, dz_ref):
        ls = [l0[...], l1[...], l2[...]]
        ov = [o0[...], o1[...], o2[...]]
        mx = jnp.maximum(jnp.maximum(ls[0], ls[1]), ls[2])
        es = [jnp.exp(l - mx) for l in ls]
        den = es[0] + es[1] + es[2]
        ws = [e / den for e in es]
        y = ws[0] * ov[0] + ws[1] * ov[1] + ws[2] * ov[2]
        sz, dsz = _silu_parts(z_ref[...])
        dyv = dy_ref[...]
        dz_ref[...] = (dyv * y * dsz).astype(BF16)
        dyp = dyv * sz
        for h in range(A_HEADS):
            sl = slice(h * HEAD, (h + 1) * HEAD)
            t = jnp.zeros((tm, 1), F32)
            for gi in range(3):
                t = t + ws[gi][:, sl][:, :1] * jnp.sum(dyp[:, sl] * ov[gi][:, sl], axis=-1, keepdims=True)
            for gi, (dref, aref) in enumerate(((d0, a0), (d1, a1), (d2, a2))):
                wg = ws[gi][:, sl]
                dref[:, sl] = (wg * dyp[:, sl]).astype(BF16)
                aref[:, sl] = -wg * t

    blk = pl.BlockSpec((tm, A_WIDTH), lambda i: (i, 0))
    outs = pl.pallas_call(
        body, name=name, grid=(S // tm,),
        in_specs=[blk] * 6 + [pl.BlockSpec((tm, A_WIDTH), lambda i: (i, C_ZA // A_WIDTH)), blk],
        out_specs=[blk] * 7,
        out_shape=[jax.ShapeDtypeStruct((S, A_WIDTH), BF16)] * 3
        + [jax.ShapeDtypeStruct((S, A_WIDTH), F32)] * 3 + [jax.ShapeDtypeStruct((S, A_WIDTH), BF16)],
        compiler_params=_cp(("parallel",)))(*os_, *ls_, u, dya)
    return outs[0:3], outs[3:6], outs[6]


def _logf(u, bf_pad, *, name):
    S = u.shape[0]
    tm = min(1024, S)

    def body(u_ref, b_ref, o_ref):
        z = u_ref[...] + b_ref[...]
        o_ref[...] = jnp.minimum(z, 0.0) - jnp.log(1.0 + jnp.exp(-jnp.abs(z)))

    return pl.pallas_call(
        body, name=name, grid=(S // tm,),
        in_specs=[pl.BlockSpec((tm, FB_PAD), lambda i: (i, C_FB // FB_PAD)),
                  pl.BlockSpec((1, FB_PAD), lambda i: (0, 0))],
        out_specs=pl.BlockSpec((tm, FB_PAD), lambda i: (i, 0)),
        out_shape=jax.ShapeDtypeStruct((S, FB_PAD), F32),
        compiler_params=_cp(("parallel",)))(u, bf_pad)


def _cumsum_lanes(x, reverse, *, name):
    nt, H, _ = x.shape

    def body(x_ref, o_ref):
        lane = lax.broadcasted_iota(jnp.int32, (H, LANES), 1)

        def tile(t, carry):
            tt = nt - 1 - t if reverse else t
            v = x_ref[tt]
            k = 1
            while k < LANES:
                if reverse:
                    v = v + jnp.where(lane < LANES - k, pltpu.roll(v, LANES - k, 1), 0.0)
                else:
                    v = v + jnp.where(lane >= k, pltpu.roll(v, k, 1), 0.0)
                k *= 2
            v = v + carry
            o_ref[tt] = v
            edge = v[:, :1] if reverse else v[:, LANES - 1:]
            return jnp.broadcast_to(edge, (H, LANES))

        lax.fori_loop(0, nt, tile, jnp.zeros((H, LANES), F32))

    return pl.pallas_call(
        body, name=name, out_shape=jax.ShapeDtypeStruct((nt, H, LANES), F32),
        in_specs=[pl.BlockSpec(memory_space=pltpu.VMEM)], out_specs=pl.BlockSpec(memory_space=pltpu.VMEM),
        compiler_params=_cp())(x)


B_SCALE = B_HEAD ** -0.5


def _fox_masks(T):
    r = lax.broadcasted_iota(jnp.int32, (T, T), 0)
    c = lax.broadcasted_iota(jnp.int32, (T, T), 1)
    return c <= r, r <= c


def _fox_fwd(q, k, v, c, *, name):
    H, S, Dh = q.shape
    T = min(512, S)
    nq = S // T
    k4, v4 = k.reshape(H, nq, T, Dh), v.reshape(H, nq, T, Dh)
    crow, ccol = c.reshape(H, nq, 1, T), c.reshape(H, S, 1)

    def body(q_ref, k_ref, v_ref, cr_ref, cc_ref, o_ref, l_ref, m_s, l_s, acc_s):
        i = pl.program_id(1)
        qv = q_ref[...]
        cq = cc_ref[...]
        m_s[...] = jnp.full((T, 1), NEG, F32)
        l_s[...] = jnp.zeros((T, 1), F32)
        acc_s[...] = jnp.zeros((T, Dh), F32)

        def step(j, masked):
            s = _dot_nt(qv, k_ref[j]) * B_SCALE + (cq - cr_ref[j])
            if masked:
                s = jnp.where(_fox_masks(T)[0], s, NEG)
            m_new = jnp.maximum(m_s[...], jnp.max(s, axis=-1, keepdims=True))
            alpha = jnp.exp(m_s[...] - m_new)
            p = jnp.exp(s - m_new)
            l_s[...] = alpha * l_s[...] + jnp.sum(p, axis=-1, keepdims=True)
            acc_s[...] = alpha * acc_s[...] + _dot(p.astype(BF16), v_ref[j])
            m_s[...] = m_new

        def loop(j, carry):
            step(j, False)
            return carry

        lax.fori_loop(0, i, loop, 0)
        step(i, True)
        o_ref[...] = acc_s[...] / l_s[...]
        l_ref[...] = m_s[...] + jnp.log(l_s[...])

    res = pl.BlockSpec((None, nq, T, Dh), lambda h, i: (h, 0, 0, 0))
    return pl.pallas_call(
        body, name=name, grid=(H, nq),
        in_specs=[pl.BlockSpec((None, T, Dh), lambda h, i: (h, i, 0)), res, res,
                  pl.BlockSpec((None, nq, 1, T), lambda h, i: (h, 0, 0, 0)),
                  pl.BlockSpec((None, T, 1), lambda h, i: (h, i, 0))],
        out_specs=[pl.BlockSpec((None, T, Dh), lambda h, i: (h, i, 0)),
                   pl.BlockSpec((None, T, 1), lambda h, i: (h, i, 0))],
        out_shape=[jax.ShapeDtypeStruct((H, S, Dh), F32), jax.ShapeDtypeStruct((H, S, 1), F32)],
        scratch_shapes=[pltpu.VMEM((T, 1), F32), pltpu.VMEM((T, 1), F32), pltpu.VMEM((T, Dh), F32)],
        compiler_params=_cp(("parallel", "parallel")),
    )(q, k4, v4, crow, ccol)


def _fox_dq(q, k, v, c, o, do, lse, *, name):
    H, S, Dh = q.shape
    T = min(512, S)
    nq = S // T
    k4, v4 = k.reshape(H, nq, T, Dh), v.reshape(H, nq, T, Dh)
    crow, ccol = c.reshape(H, nq, 1, T), c.reshape(H, S, 1)

    def body(q_ref, k_ref, v_ref, cr_ref, cc_ref, o_ref, do_ref, l_ref, dq_ref, dl_ref, dc_ref, acc_s, dc_s):
        i = pl.program_id(1)
        qv = q_ref[...]
        cq = cc_ref[...]
        dov = do_ref[...]
        lse_v = l_ref[...]
        delta = jnp.sum(dov.astype(F32) * o_ref[...], axis=-1, keepdims=True)
        dl_ref[...] = delta
        acc_s[...] = jnp.zeros((T, Dh), F32)
        dc_s[...] = jnp.zeros((T, 1), F32)

        def step(j, masked):
            s = _dot_nt(qv, k_ref[j]) * B_SCALE + (cq - cr_ref[j])
            if masked:
                s = jnp.where(_fox_masks(T)[0], s, NEG)
            p = jnp.exp(s - lse_v)
            ds = p * (_dot_nt(dov, v_ref[j]) - delta)
            acc_s[...] += _dot(ds.astype(BF16), k_ref[j])
            dc_s[...] += jnp.sum(ds, axis=-1, keepdims=True)

        def loop(j, carry):
            step(j, False)
            return carry

        lax.fori_loop(0, i, loop, 0)
        step(i, True)
        dq_ref[...] = acc_s[...] * B_SCALE
        dc_ref[...] = dc_s[...]

    res = pl.BlockSpec((None, nq, T, Dh), lambda h, i: (h, 0, 0, 0))
    tile = pl.BlockSpec((None, T, Dh), lambda h, i: (h, i, 0))
    colv = pl.BlockSpec((None, T, 1), lambda h, i: (h, i, 0))
    return pl.pallas_call(
        body, name=name, grid=(H, nq),
        in_specs=[tile, res, res, pl.BlockSpec((None, nq, 1, T), lambda h, i: (h, 0, 0, 0)), colv,
                  tile, tile, colv],
        out_specs=[tile, colv, colv],
        out_shape=[jax.ShapeDtypeStruct((H, S, Dh), F32)] + [jax.ShapeDtypeStruct((H, S, 1), F32)] * 2,
        scratch_shapes=[pltpu.VMEM((T, Dh), F32), pltpu.VMEM((T, 1), F32)],
        compiler_params=_cp(("parallel", "parallel")),
    )(q, k4, v4, crow, ccol, o, do, lse)


def _fox_dkv(q, k, v, c, do, lse, delta, *, name):
    H, S, Dh = q.shape
    T = min(512, S)
    nq = S // T
    q4, do4 = q.reshape(H, nq, T, Dh), do.reshape(H, nq, T, Dh)
    crow, ccol = c.reshape(H, nq, 1, T), c.reshape(H, S, 1)
    lrow, drow = lse.reshape(H, nq, 1, T), delta.reshape(H, nq, 1, T)

    def body(k_ref, v_ref, q_ref, do_ref, cr_ref, cc_ref, l_ref, dl_ref, dk_ref, dv_ref, dc_ref,
             dk_s, dv_s, dc_s):
        j = pl.program_id(1)
        kv = k_ref[...]
        vv = v_ref[...]
        ck = cc_ref[...]
        dk_s[...] = jnp.zeros((T, Dh), F32)
        dv_s[...] = jnp.zeros((T, Dh), F32)
        dc_s[...] = jnp.zeros((T, 1), F32)

        def step(i, masked):
            st = _dot_nt(kv, q_ref[i]) * B_SCALE + (cr_ref[i] - ck)
            if masked:
                st = jnp.where(_fox_masks(T)[1], st, NEG)
            pt = jnp.exp(st - l_ref[i])
            dv_s[...] += _dot(pt.astype(BF16), do_ref[i])
            dst = pt * (_dot_nt(vv, do_ref[i]) - dl_ref[i])
            dk_s[...] += _dot(dst.astype(BF16), q_ref[i])
            dc_s[...] -= jnp.sum(dst, axis=-1, keepdims=True)

        def loop(i, carry):
            step(i, False)
            return carry

        step(j, True)
        lax.fori_loop(j + 1, nq, loop, 0)
        dk_ref[...] = dk_s[...] * B_SCALE
        dv_ref[...] = dv_s[...]
        dc_ref[...] = dc_s[...]

    res = pl.BlockSpec((None, nq, T, Dh), lambda h, j: (h, 0, 0, 0))
    rowv = pl.BlockSpec((None, nq, 1, T), lambda h, j: (h, 0, 0, 0))
    tile = pl.BlockSpec((None, T, Dh), lambda h, j: (h, j, 0))
    colv = pl.BlockSpec((None, T, 1), lambda h, j: (h, j, 0))
    return pl.pallas_call(
        body, name=name, grid=(H, nq),
        in_specs=[tile, tile, res, res, rowv, colv, rowv, rowv],
        out_specs=[tile, tile, colv],
        out_shape=[jax.ShapeDtypeStruct((H, S, Dh), F32)] * 2 + [jax.ShapeDtypeStruct((H, S, 1), F32)],
        scratch_shapes=[pltpu.VMEM((T, Dh), F32), pltpu.VMEM((T, Dh), F32), pltpu.VMEM((T, 1), F32)],
        compiler_params=_cp(("parallel", "parallel")),
    )(k, v, q4, do4, crow, ccol, lrow, drow)


def _gate_fwd(o, u, zcol, *, name):
    S = u.shape[0]
    tm = min(1024, S)

    def body(o_ref, z_ref, y_ref):
        y_ref[...] = (o_ref[...] * _silu_parts(z_ref[...])[0]).astype(BF16)

    blk = pl.BlockSpec((tm, A_WIDTH), lambda i: (i, 0))
    return pl.pallas_call(
        body, name=name, grid=(S // tm,),
        in_specs=[blk, pl.BlockSpec((tm, A_WIDTH), lambda i: (i, zcol // A_WIDTH))],
        out_specs=blk, out_shape=jax.ShapeDtypeStruct((S, A_WIDTH), BF16),
        compiler_params=_cp(("parallel",)))(o, u)


def _gate_bwd(o, u, zcol, dy, *, name):
    S = u.shape[0]
    tm = min(1024, S)

    def body(o_ref, z_ref, dy_ref, do_ref, dz_ref):
        sz, dsz = _silu_parts(z_ref[...])
        dyv = dy_ref[...]
        do_ref[...] = (dyv * sz).astype(BF16)
        dz_ref[...] = (dyv * o_ref[...] * dsz).astype(BF16)

    blk = pl.BlockSpec((tm, A_WIDTH), lambda i: (i, 0))
    return pl.pallas_call(
        body, name=name, grid=(S // tm,),
        in_specs=[blk, pl.BlockSpec((tm, A_WIDTH), lambda i: (i, zcol // A_WIDTH)), blk],
        out_specs=[blk, blk], out_shape=[jax.ShapeDtypeStruct((S, A_WIDTH), BF16)] * 2,
        compiler_params=_cp(("parallel",)))(o, u, dy)


def _dfb(u, bf_pad, dlogf_pad, *, name):
    S = u.shape[0]
    tm = min(1024, S)

    def body(u_ref, b_ref, d_ref, o_ref, s_ref):
        i = pl.program_id(0)
        dv = d_ref[...] * _sigmoid(-(u_ref[...] + b_ref[...]))
        o_ref[...] = dv.astype(BF16)
        part = jnp.sum(dv, axis=0, keepdims=True)

        @pl.when(i == 0)
        def _():
            s_ref[...] = part

        @pl.when(i > 0)
        def _():
            s_ref[...] += part

    vec = pl.BlockSpec((1, FB_PAD), lambda i: (0, 0))
    blk = pl.BlockSpec((tm, FB_PAD), lambda i: (i, 0))
    return pl.pallas_call(
        body, name=name, grid=(S // tm,),
        in_specs=[pl.BlockSpec((tm, FB_PAD), lambda i: (i, C_FB // FB_PAD)), vec, blk],
        out_specs=[blk, vec],
        out_shape=[jax.ShapeDtypeStruct((S, FB_PAD), BF16), jax.ShapeDtypeStruct((1, FB_PAD), F32)],
        compiler_params=_cp(("arbitrary",)))(u, bf_pad, dlogf_pad)


M_SCALE = HEAD ** -0.5


def _mem_fwd(u, mkv, *, name):
    S = u.shape[0]
    T = min(512, S)

    def body(q_ref, z_ref, k_ref, v_ref, y_ref):
        s = _dot_nt(q_ref[...].astype(BF16), k_ref[...].astype(BF16)) * M_SCALE
        p = jnp.exp(s - jnp.max(s, axis=-1, keepdims=True))
        p = p / jnp.sum(p, axis=-1, keepdims=True)
        o = _dot(p.astype(BF16), v_ref[...].astype(BF16))
        y_ref[...] = (o * _silu_parts(z_ref[...])[0]).astype(BF16)

    return pl.pallas_call(
        body, name=name, grid=(S // T, M_HEADS),
        in_specs=[pl.BlockSpec((T, HEAD), lambda i, h: (i, C_QM // HEAD + h)),
                  pl.BlockSpec((T, HEAD), lambda i, h: (i, C_ZM // HEAD + h)),
                  pl.BlockSpec((N_MEM, HEAD), lambda i, h: (0, h)),
                  pl.BlockSpec((N_MEM, HEAD), lambda i, h: (0, M_HEADS + h))],
        out_specs=pl.BlockSpec((T, HEAD), lambda i, h: (i, h)),
        out_shape=jax.ShapeDtypeStruct((S, A_WIDTH), BF16),
        compiler_params=_cp(("parallel", "parallel")))(u, u, mkv, mkv)


def _mem_bwd(u, mkv, dy, *, name):
    S = u.shape[0]
    T = min(512, S)

    def body(q_ref, z_ref, k_ref, v_ref, dy_ref, dq_ref, dz_ref, dk_ref, dv_ref):
        i = pl.program_id(1)
        qv = q_ref[...].astype(BF16)
        kv = k_ref[...].astype(BF16)
        vv = v_ref[...].astype(BF16)
        s = _dot_nt(qv, kv) * M_SCALE
        p = jnp.exp(s - jnp.max(s, axis=-1, keepdims=True))
        p = p / jnp.sum(p, axis=-1, keepdims=True)
        o = _dot(p.astype(BF16), vv)
        sz, dsz = _silu_parts(z_ref[...])
        dyv = dy_ref[...]
        dz_ref[...] = (dyv * o * dsz).astype(BF16)
        dov = (dyv * sz).astype(BF16)
        dp = _dot_nt(dov, vv)
        ds = p * (dp - jnp.sum(p * dp, axis=-1, keepdims=True))
        dq_ref[...] = (_dot(ds.astype(BF16), kv) * M_SCALE).astype(BF16)
        dvp = _dot(p.T.astype(BF16), dov)
        dkp = _dot(ds.T.astype(BF16), qv) * M_SCALE

        @pl.when(i == 0)
        def _():
            dk_ref[...] = dkp
            dv_ref[...] = dvp

        @pl.when(i > 0)
        def _():
            dk_ref[...] += dkp
            dv_ref[...] += dvp

    tile = pl.BlockSpec((T, HEAD), lambda h, i: (i, h))
    acc = pl.BlockSpec((N_MEM, HEAD), lambda h, i: (0, h))
    return pl.pallas_call(
        body, name=name, grid=(M_HEADS, S // T),
        in_specs=[pl.BlockSpec((T, HEAD), lambda h, i: (i, C_QM // HEAD + h)),
                  pl.BlockSpec((T, HEAD), lambda h, i: (i, C_ZM // HEAD + h)),
                  pl.BlockSpec((N_MEM, HEAD), lambda h, i: (0, h)),
                  pl.BlockSpec((N_MEM, HEAD), lambda h, i: (0, M_HEADS + h)), tile],
        out_specs=[tile, tile, acc, acc],
        out_shape=[jax.ShapeDtypeStruct((S, A_WIDTH), BF16)] * 2
        + [jax.ShapeDtypeStruct((N_MEM, A_WIDTH), F32)] * 2,
        compiler_params=_cp(("parallel", "arbitrary")))(u, u, mkv, mkv, dy)


def _branch_fwd(ys, wbs, u, b_merge, *, name):
    S = u.shape[0]
    tm, tn = min(512, S), 512
    nj = D_MODEL // tn

    def body(ya, yb, ym, wa, wb, wm, g0, g1, g2, b0, b1, b2, mg_ref, p_ref):
        acc = jnp.zeros((tm, tn), F32)
        for i, (y, w, gr, br) in enumerate(((ya, wa, g0, b0), (yb, wb, g1, b1), (ym, wm, g2, b2))):
            pr = _dot(y[...], w[...])
            p_ref[i] = pr
            acc = acc + _sigmoid(gr[...] + br[...]) * pr
        mg_ref[...] = acc.astype(BF16)

    yspec = pl.BlockSpec((tm, A_WIDTH), lambda i, j: (i, 0))
    wspec = pl.BlockSpec((A_WIDTH, tn), lambda i, j: (0, j))
    gspec = lambda b: pl.BlockSpec((tm, tn), lambda i, j: (i, (C_GL + b * D_MODEL) // tn + j))
    bspec = lambda b: pl.BlockSpec((1, tn), lambda i, j: (0, b * nj + j))
    return pl.pallas_call(
        body, name=name, grid=(S // tm, nj),
        in_specs=[yspec] * 3 + [wspec] * 3 + [gspec(0), gspec(1), gspec(2), bspec(0), bspec(1), bspec(2)],
        out_specs=[pl.BlockSpec((tm, tn), lambda i, j: (i, j)),
                   pl.BlockSpec((3, tm, tn), lambda i, j: (0, i, j))],
        out_shape=[jax.ShapeDtypeStruct((S, D_MODEL), BF16), jax.ShapeDtypeStruct((3, S, D_MODEL), F32)],
        compiler_params=_cp(("parallel", "parallel")))(*ys, *wbs, u, u, u, b_merge, b_merge, b_merge)


def _branch_bwd(dm, prods, u, b_merge, *, name):
    S = u.shape[0]
    tm = min(256, S)

    def body(dm_ref, p_ref, g0, g1, g2, b_ref, dp_ref, dgl_ref, db_ref):
        i = pl.program_id(0)
        dmv = dm_ref[...]
        parts = []
        for b, gr in enumerate((g0, g1, g2)):
            sl = slice(b * D_MODEL, (b + 1) * D_MODEL)
            gt = _sigmoid(gr[...] + b_ref[:, sl])
            dp_ref[b] = (dmv * gt).astype(BF16)
            dgl = dmv * p_ref[b] * gt * (1.0 - gt)
            dgl_ref[:, sl] = dgl.astype(BF16)
            parts.append(jnp.sum(dgl, axis=0, keepdims=True))
        part = jnp.concatenate(parts, axis=1)

        @pl.when(i == 0)
        def _():
            db_ref[...] = part

        @pl.when(i > 0)
        def _():
            db_ref[...] += part

    gspec = lambda b: pl.BlockSpec((tm, D_MODEL), lambda i: (i, C_GL // D_MODEL + b))
    vec = pl.BlockSpec((1, 3 * D_MODEL), lambda i: (0, 0))
    return pl.pallas_call(
        body, name=name, grid=(S // tm,),
        in_specs=[pl.BlockSpec((tm, D_MODEL), lambda i: (i, 0)),
                  pl.BlockSpec((3, tm, D_MODEL), lambda i: (0, i, 0)), gspec(0), gspec(1), gspec(2), vec],
        out_specs=[pl.BlockSpec((3, tm, D_MODEL), lambda i: (0, i, 0)),
                   pl.BlockSpec((tm, 3 * D_MODEL), lambda i: (i, 0)), vec],
        out_shape=[jax.ShapeDtypeStruct((3, S, D_MODEL), BF16), jax.ShapeDtypeStruct((S, 3 * D_MODEL), BF16),
                   jax.ShapeDtypeStruct((1, 3 * D_MODEL), F32)],
        compiler_params=_cp(("arbitrary",)))(dm, prods, u, u, u, b_merge)


def _rope_tables(pos):
    half = ROT // 2
    inv = ROPE_THETA ** (-jnp.arange(half, dtype=F32) / half)
    ang = pos.astype(F32)[:, None] * inv
    cos, sin = jnp.cos(ang), jnp.sin(ang)
    S = pos.shape[0]
    one = jnp.ones((S, LANES - ROT), F32)
    zero = jnp.zeros((S, LANES - ROT), F32)
    zh = jnp.zeros((S, half), F32)
    c = jnp.concatenate([cos, cos, one], axis=1)
    s1 = jnp.concatenate([-sin, zh, zero], axis=1)
    s2 = jnp.concatenate([zh, sin, zero], axis=1)
    return c, s1, s2


def _to_tiles(t):
    S, H = t.shape
    return t.reshape(S // LANES, LANES, H).transpose(0, 2, 1)


def _from_tiles(t):
    nt, H, _ = t.shape
    return t.transpose(1, 0, 2).reshape(H, nt * LANES)


def _heads_in(t):
    S = t.shape[0]
    return t.reshape(S, B_HEADS, B_HEAD).transpose(1, 0, 2)


def _heads_out(t):
    S = t.shape[1]
    return t.transpose(1, 0, 2).reshape(S, B_HEADS * B_HEAD)


def _local_step(x, mem, pos, tgt, g_pre, g_post, g_mem, w_cat, bf_pad, b_merge, w_kv, wbs, w_out):
    S = x.shape[0]
    tabs = _rope_tables(pos)

    h = _rms_fwd(x, g_pre, name="rms_pre")
    u = _mm(h, w_cat, name="proj_in", tn=1152)
    outs_a, lses_a = [], []
    for g in range(3):
        o, l = _attn_a_fwd(u, tabs, g, name=f"attn_a_fwd{g}")
        outs_a.append(o)
        lses_a.append(l)
    ya = _merge_a_fwd(outs_a, lses_a, u, name="merge_a_fwd")

    logf = _logf(u, bf_pad, name="logf")
    c = _from_tiles(_cumsum_lanes(_to_tiles(logf[:, :B_HEADS]), False, name="cumsum_fwd"))
    qb = _heads_in(u[:, C_QB:C_QB + 512]).astype(BF16)
    kb = _heads_in(u[:, C_KB:C_KB + 512]).astype(BF16)
    vb = _heads_in(u[:, C_VB:C_VB + 512]).astype(BF16)
    ob, lse_b = _fox_fwd(qb, kb, vb, c, name="fox_fwd")
    ob_t = _heads_out(ob)
    yb = _gate_fwd(ob_t, u, C_ZB, name="gate_b_fwd")

    hm = _rms_fwd(mem, g_mem, name="rms_mem")
    mkv = _mm(hm, w_kv, name="proj_mem")
    ym = _mem_fwd(u, mkv, name="mem_fwd")

    merged, prods = _branch_fwd((ya, yb, ym), wbs, u, b_merge, name="branch_fwd")
    out = _mm(merged, w_out, name="proj_out")
    dy, d_out, dg_post, loss_row = _post(x, out, tgt, g_post, name="post")

    dmerged = _mm(d_out, w_out, bt=True, name="d_merged")
    dw_out = _mm(merged.T, d_out, name="dw_out", tk=2048)
    dprods, dgl, db_merge = _branch_bwd(dmerged, prods, u, b_merge, name="branch_bwd")
    dys, dwbs = [], []
    for i, (y, wb) in enumerate(zip((ya, yb, ym), wbs)):
        dys.append(_mm(dprods[i], wb, bt=True, name=f"d_y{i}"))
        dwbs.append(_mm(y.T, dprods[i], name=f"dw_branch{i}", tk=2048))

    dos_a, adjs_a, dza = _merge_a_bwd(outs_a, lses_a, u, dys[0], name="merge_a_bwd")
    dqa, dka, dva = [], [], []
    for g in range(3):
        dqa.append(_attn_a_dq(u, tabs, g, dos_a[g], lses_a[g], adjs_a[g], name=f"attn_a_dq{g}"))
        dk, dv = _attn_a_dkv(u, tabs, g, dos_a[g], lses_a[g], adjs_a[g], name=f"attn_a_dkv{g}")
        dka.append(dk)
        dva.append(dv)

    dob_t, dzb = _gate_bwd(ob_t, u, C_ZB, dys[1], name="gate_b_bwd")
    dob = _heads_in(dob_t)
    dqb, delta_b, dc_q = _fox_dq(qb, kb, vb, c, ob, dob, lse_b, name="fox_dq")
    dkb, dvb, dc_k = _fox_dkv(qb, kb, vb, c, dob, lse_b, delta_b, name="fox_dkv")
    dc = (dc_q + dc_k).reshape(B_HEADS, S)
    dlogf = _from_tiles(_cumsum_lanes(_to_tiles(dc.T), True, name="cumsum_bwd"))
    dlogf_pad = jnp.pad(dlogf.T, ((0, 0), (0, FB_PAD - B_HEADS)))
    dfb, db_forget = _dfb(u, bf_pad, dlogf_pad, name="dfb")

    dqm, dzm, dmk, dmv = _mem_bwd(u, mkv, dys[2], name="mem_bwd")
    dmkv = jnp.concatenate([dmk, dmv], axis=1).astype(BF16)
    dhm = _mm(dmkv, w_kv, bt=True, name="d_hm")
    dw_kv = _mm(hm.T, dmkv, name="dw_kv")
    dg_mem = _rms_bwd(mem, g_mem, dhm, None, name="rms_mem_bwd")

    du = jnp.concatenate(
        dqa + dka + dva + [dza, _heads_out(dqb).astype(BF16), _heads_out(dkb).astype(BF16),
                           _heads_out(dvb).astype(BF16), dzb, dqm, dzm, dgl, dfb], axis=1)
    dh = _mm(du, w_cat, bt=True, name="d_h", tk=1152)
    dw_cat = _mm(h.T, du, name="dw_in", tn=1152, tk=2048)
    grad_x, dg_pre = _rms_bwd(x, g_pre, dh, dy, name="rms_pre_bwd")

    return dict(loss=loss_row, grad_x=grad_x, dw_cat=dw_cat, dw_kv=dw_kv, dwbs=dwbs, dw_out=dw_out,
                dg_pre=dg_pre, dg_post=dg_post, dg_mem=dg_mem, db_forget=db_forget, db_merge=db_merge)


MESH = pl.DeviceIdType.MESH
ANY = pl.BlockSpec(memory_space=pl.ANY)


def _relations():
    return [(k >> 2 & 1, k >> 1 & 1, k & 1) for k in range(1, N_DEV)]


def _coords():
    return lax.axis_index("x"), lax.axis_index("y"), lax.axis_index("c")


def _all_gather(shard, *, name):
    R, W = shard.shape

    def body(x_ref, out_ref, send_sems, recv_sems, local_sem):
        x, y, c = _coords()
        me = 4 * x + 2 * y + c
        mine = pltpu.make_async_copy(x_ref, out_ref.at[me], local_sem)
        mine.start()
        copies = []
        for k, (fx, fy, fc) in enumerate(_relations()):
            cp = pltpu.make_async_remote_copy(
                src_ref=x_ref, dst_ref=out_ref.at[me], send_sem=send_sems.at[k], recv_sem=recv_sems.at[k],
                device_id=(x ^ fx, y ^ fy, c ^ fc), device_id_type=MESH)
            cp.start()
            copies.append(cp)
        for k, (fx, fy, fc) in enumerate(_relations()):
            peer = 4 * (x ^ fx) + 2 * (y ^ fy) + (c ^ fc)
            pltpu.make_async_remote_copy(
                src_ref=x_ref, dst_ref=out_ref.at[peer], send_sem=send_sems.at[k], recv_sem=recv_sems.at[k],
                device_id=(x ^ fx, y ^ fy, c ^ fc), device_id_type=MESH).wait_recv()
        for cp in copies:
            cp.wait_send()
        mine.wait()

    return pl.pallas_call(
        body, name=name, out_shape=jax.ShapeDtypeStruct((N_DEV, R, W), shard.dtype),
        in_specs=[ANY], out_specs=ANY,
        scratch_shapes=[pltpu.SemaphoreType.DMA((N_DEV - 1,)), pltpu.SemaphoreType.DMA((N_DEV - 1,)),
                        pltpu.SemaphoreType.DMA],
    )(shard)


def _exchange_grads(gbig, gsmall, *, name):
    _, R, W = gbig.shape
    n = N_DEV - 1

    def body(g_ref, s_ref, rb_ref, rs_ref, send_sems, recv_sems, local_sems):
        x, y, c = _coords()
        me = 4 * x + 2 * y + c
        loc = [pltpu.make_async_copy(g_ref.at[me], rb_ref.at[me], local_sems.at[0]),
               pltpu.make_async_copy(s_ref, rs_ref.at[me], local_sems.at[1])]
        for cp in loc:
            cp.start()

        def copies(k, fx, fy, fc, slot_send, slot_recv):
            dev = (x ^ fx, y ^ fy, c ^ fc)
            big = pltpu.make_async_remote_copy(
                src_ref=g_ref.at[slot_send], dst_ref=rb_ref.at[slot_recv], send_sem=send_sems.at[k],
                recv_sem=recv_sems.at[k], device_id=dev, device_id_type=MESH)
            small = pltpu.make_async_remote_copy(
                src_ref=s_ref, dst_ref=rs_ref.at[slot_recv], send_sem=send_sems.at[n + k],
                recv_sem=recv_sems.at[n + k], device_id=dev, device_id_type=MESH)
            return big, small

        started = []
        for k, (fx, fy, fc) in enumerate(_relations()):
            peer = 4 * (x ^ fx) + 2 * (y ^ fy) + (c ^ fc)
            big, small = copies(k, fx, fy, fc, peer, me)
            small.start()
            big.start()
            started += [big, small]
        for k, (fx, fy, fc) in enumerate(_relations()):
            peer = 4 * (x ^ fx) + 2 * (y ^ fy) + (c ^ fc)
            big, small = copies(k, fx, fy, fc, me, peer)
            small.wait_recv()
            big.wait_recv()
        for cp in started:
            cp.wait_send()
        for cp in loc:
            cp.wait()

    return pl.pallas_call(
        body, name=name,
        out_shape=[jax.ShapeDtypeStruct((N_DEV, R, W), gbig.dtype),
                   jax.ShapeDtypeStruct((N_DEV, 1, P_SMALL), gsmall.dtype)],
        in_specs=[ANY, ANY], out_specs=[ANY, ANY],
        scratch_shapes=[pltpu.SemaphoreType.DMA((2 * n,)), pltpu.SemaphoreType.DMA((2 * n,)),
                        pltpu.SemaphoreType.DMA((2,))],
    )(gbig, gsmall)


def _adamw(parts, w, m, v, tr, *, name):
    _, R, W = parts.shape
    assert R % tr == 0

    def body(p_ref, w_ref, m_ref, v_ref, g_ref, d_ref, nm_ref, nv_ref):
        g = p_ref[0]
        for i in range(1, N_DEV):
            g = g + p_ref[i]
        mm = ADAM_B1 * m_ref[...] + (1.0 - ADAM_B1) * g
        vv = ADAM_B2 * v_ref[...] + (1.0 - ADAM_B2) * (g * g)
        m_hat = mm / (1.0 - ADAM_B1 ** ADAM_STEP)
        v_hat = vv / (1.0 - ADAM_B2 ** ADAM_STEP)
        g_ref[...] = g
        d_ref[...] = -ADAM_LR * (m_hat / (jnp.sqrt(v_hat) + ADAM_EPS) + ADAM_WD * w_ref[...])
        nm_ref[...] = mm
        nv_ref[...] = vv

    blk = pl.BlockSpec((tr, W), lambda i: (i, 0))
    return pl.pallas_call(
        body, name=name, grid=(R // tr,),
        in_specs=[pl.BlockSpec((N_DEV, tr, W), lambda i: (0, i, 0)), blk, blk, blk],
        out_specs=[blk] * 4, out_shape=[jax.ShapeDtypeStruct((R, W), F32)] * 4,
        compiler_params=_cp(("parallel",)))(parts, w, m, v)


def _pack_shards(w_in, w_kv, wa, wb, wm, w_out):
    return jnp.concatenate([t.reshape(-1) for t in (w_in, w_kv, wa, wb, wm, w_out)]
                           + [jnp.zeros((P_PAD - P_BIG,), w_in.dtype)])


def _unpack_shards(flat):
    o = 0
    res = []
    for n, shape in ((SH_IN, (1, D_MODEL, IN_COLS // N_DEV)), (SH_KV, (1, D_MODEL // N_DEV, D_MODEL)),
                     (SH_BR, (1, A_WIDTH, D_MODEL // N_DEV)), (SH_BR, (1, A_WIDTH, D_MODEL // N_DEV)),
                     (SH_BR, (1, A_WIDTH, D_MODEL // N_DEV)), (SH_OUT, (1, D_MODEL // N_DEV, D_MODEL))):
        res.append(flat[o:o + n].reshape(shape))
        o += n
    return res


def _full_weights(gathered):
    cs = IN_COLS // N_DEV
    o = 0
    w_in = gathered[:, o:o + SH_IN].reshape(N_DEV, D_MODEL, cs).transpose(1, 0, 2).reshape(D_MODEL, IN_COLS)
    o += SH_IN
    w_kv = gathered[:, o:o + SH_KV].reshape(D_MODEL, D_MODEL)
    o += SH_KV
    wbs = []
    for _ in range(3):
        wbs.append(gathered[:, o:o + SH_BR].reshape(N_DEV, A_WIDTH, D_MODEL // N_DEV)
                   .transpose(1, 0, 2).reshape(A_WIDTH, D_MODEL))
        o += SH_BR
    w_out = gathered[:, o:o + SH_OUT].reshape(D_MODEL, D_MODEL)
    w_cat = jnp.concatenate([w_in[:, :FB_ORIG], w_in[:, FB_ORIG + B_HEADS:], w_in[:, FB_ORIG:FB_ORIG + B_HEADS],
                             jnp.zeros((D_MODEL, FB_PAD - B_HEADS), w_in.dtype)], axis=1)
    return w_cat, w_kv, wbs, w_out


def _pack_grads(dw_cat, dw_kv, dwbs, dw_out):
    cs = IN_COLS // N_DEV
    dw_in = jnp.concatenate([dw_cat[:, :FB_ORIG], dw_cat[:, C_FB:C_FB + B_HEADS], dw_cat[:, FB_ORIG:C_FB]], axis=1)
    parts = [dw_in.reshape(D_MODEL, N_DEV, cs).transpose(1, 0, 2).reshape(N_DEV, SH_IN),
             dw_kv.reshape(N_DEV, SH_KV)]
    for t in dwbs:
        parts.append(t.reshape(A_WIDTH, N_DEV, D_MODEL // N_DEV).transpose(1, 0, 2).reshape(N_DEV, SH_BR))
    parts.append(dw_out.reshape(N_DEV, SH_OUT))
    parts.append(jnp.zeros((N_DEV, P_PAD - P_BIG), F32))
    return jnp.concatenate(parts, axis=1)


def kernel(x, mem, positions, norm_pre_g, norm_post_g, norm_mem_g, w_in, b_forget, b_merge, w_mem_kv, w_branch_a, w_branch_b, w_branch_m, w_out, loss_target, m_norm_pre_g, m_norm_post_g, m_norm_mem_g, m_w_in, m_b_forget, m_b_merge, m_w_mem_kv, m_w_branch_a, m_w_branch_b, m_w_branch_m, m_w_out, v_norm_pre_g, v_norm_post_g, v_norm_mem_g, v_w_in, v_b_forget, v_b_merge, v_w_mem_kv, v_w_branch_a, v_w_branch_b, v_w_branch_m, v_w_out):
    rows = P_PAD // LANES
    big = (w_in, w_mem_kv, w_branch_a, w_branch_b, w_branch_m, w_out)
    w_flat = _pack_shards(*big)
    gathered = _all_gather(w_flat.astype(BF16).reshape(rows, LANES), name="gather_weights")
    w_cat, w_kv, wbs, w_o = _full_weights(gathered.reshape(N_DEV, P_PAD))

    bf_pad = jnp.pad(b_forget, ((0, 0), (0, FB_PAD - B_HEADS)))
    r = _local_step(x[0], mem[0], positions[0], loss_target[0], norm_pre_g, norm_post_g, norm_mem_g,
                    w_cat, bf_pad, b_merge, w_kv, wbs, w_o)

    gbig = _pack_grads(r["dw_cat"], r["dw_kv"], r["dwbs"], r["dw_out"]).reshape(N_DEV, rows, LANES)
    gsmall = jnp.concatenate([r["dg_pre"], r["dg_post"], r["dg_mem"], r["db_merge"],
                              r["db_forget"][:, :LANES], r["loss"]], axis=1)
    rbig, rsmall = _exchange_grads(gbig, gsmall, name="exchange_grads")

    m_flat = _pack_shards(m_w_in, m_w_mem_kv, m_w_branch_a, m_w_branch_b, m_w_branch_m, m_w_out)
    v_flat = _pack_shards(v_w_in, v_w_mem_kv, v_w_branch_a, v_w_branch_b, v_w_branch_m, v_w_out)
    outs_big = _adamw(rbig, w_flat.reshape(rows, LANES), m_flat.reshape(rows, LANES),
                      v_flat.reshape(rows, LANES), 1024, name="adamw_big")
    g_big, d_big, nm_big, nv_big = (_unpack_shards(t.reshape(-1)) for t in outs_big)

    def small_vec(a, b, c, d, e):
        z = jnp.zeros((1, LANES - B_HEADS), F32)
        return jnp.concatenate([a, b, c, d, e, z, jnp.zeros((1, LANES), F32)], axis=1)

    outs_small = _adamw(rsmall, small_vec(norm_pre_g, norm_post_g, norm_mem_g, b_merge, b_forget),
                        small_vec(m_norm_pre_g, m_norm_post_g, m_norm_mem_g, m_b_merge, m_b_forget),
                        small_vec(v_norm_pre_g, v_norm_post_g, v_norm_mem_g, v_b_merge, v_b_forget),
                        1, name="adamw_small")

    def small_parts(t):
        return [t[:, O_GPRE:O_GPRE + D_MODEL], t[:, O_GPOST:O_GPOST + D_MODEL], t[:, O_GMEM:O_GMEM + D_MODEL],
                t[:, O_BF:O_BF + B_HEADS], t[:, O_BM:O_BM + 3 * D_MODEL]]

    loss = outs_small[0][0, O_LOSS]
    result = [loss, r["grad_x"][None]]
    for big_parts, small in zip((g_big, d_big, nm_big, nv_big), outs_small):
        gp, gq, gm, bf, bm = small_parts(small)
        w_i, w_k, w_a, w_b, w_m, w_ot = big_parts
        result += [gp, gq, gm, w_i, bf, bm, w_k, w_a, w_b, w_m, w_ot]
    return tuple(result)
```

```python
import jax
import jax.numpy as jnp
from jax import lax
from jax.experimental import pallas as pl
from jax.experimental.pallas import tpu as pltpu

F32 = jnp.float32
BF16 = jnp.bfloat16

N_DEV = 8
D_MODEL = 1024
N_MEM = 256
EPS = 1e-6
NEG = -1e30
ROPE_THETA = 500000.0
DIL = (1, 4, 16)
A_HEADS = 4
HEAD = 128
A_WIDTH = 512
B_HEADS = 8
B_HEAD = 64
M_HEADS = 4
ROT = 32
IN_COLS = 11272
FB_PAD = 256

SEGS = {
    "A0": ((0, 512), (1536, 2048), (3072, 3584)),
    "A1": ((512, 1024), (2048, 2560), (3584, 4096)),
    "A2": ((1024, 1536), (2560, 3072), (4096, 4608)),
    "B": ((5120, 6656),),
    "R": ((4608, 5120), (6664, 7176), (7176, 7688), (7688, 8200), (8200, 11272), (6656, 6664)),
}
SEG_PAD = {"A0": 0, "A1": 0, "A2": 0, "B": 0, "R": FB_PAD - B_HEADS}
R_ZA, R_ZB, R_QM, R_ZM, R_GL, R_FB = 0, 512, 1024, 1536, 2048, 5120
NR = R_FB + FB_PAD

ADAM_LR, ADAM_B1, ADAM_B2, ADAM_EPS, ADAM_WD, ADAM_STEP = 0.001, 0.9, 0.999, 1e-08, 0.01, 10

LANES = 128
VMEM_LIMIT = 56 * 1024 * 1024

CS = IN_COLS // N_DEV
RO_KV, RO_OUT, RO_BR, RO_IN = 0, 128, 256, 448
IN_ROWS = 1424
ROWS = RO_IN + IN_ROWS
O_GPRE, O_GPOST, O_GMEM, O_BM, O_BF, O_LOSS = 0, 1024, 2048, 3072, 6144, 6272
P_SMALL = 6400


def _cp(sem=None):
    return pltpu.CompilerParams(dimension_semantics=sem, vmem_limit_bytes=VMEM_LIMIT)


def _dot(a, b):
    return jnp.dot(a, b, preferred_element_type=F32)


def _dot_nt(a, b):
    return lax.dot_general(a, b, (((1,), (1,)), ((), ())), preferred_element_type=F32)


def _sigmoid(z):
    return 1.0 / (1.0 + jnp.exp(-z))


def _mm(a, b, *, name, bt=False, out_dtype=F32, tm=1024, tn=1024, tk=None):
    M, K = a.shape
    N = b.shape[0] if bt else b.shape[1]
    tm, tn = min(tm, M), min(tn, N)
    tk = K if tk is None else min(tk, K)
    assert M % tm == 0 and N % tn == 0 and K % tk == 0
    nk = K // tk

    def body(a_ref, b_ref, o_ref, acc_ref):
        av = a_ref[...].astype(BF16)
        bv = b_ref[...].astype(BF16)
        p = _dot_nt(av, bv) if bt else _dot(av, bv)
        if nk == 1:
            o_ref[...] = p.astype(out_dtype)
        else:
            k = pl.program_id(2)

            @pl.when(k == 0)
            def _():
                acc_ref[...] = p

            @pl.when(k > 0)
            def _():
                acc_ref[...] += p

            @pl.when(k == nk - 1)
            def _():
                o_ref[...] = acc_ref[...].astype(out_dtype)

    b_spec = (pl.BlockSpec((tn, tk), lambda i, j, k: (j, k)) if bt
              else pl.BlockSpec((tk, tn), lambda i, j, k: (k, j)))
    return pl.pallas_call(
        body, name=name, grid=(M // tm, N // tn, nk),
        in_specs=[pl.BlockSpec((tm, tk), lambda i, j, k: (i, k)), b_spec],
        out_specs=pl.BlockSpec((tm, tn), lambda i, j, k: (i, j)),
        out_shape=jax.ShapeDtypeStruct((M, N), out_dtype),
        scratch_shapes=[pltpu.VMEM((tm, tn) if nk > 1 else (8, LANES), F32)],
        compiler_params=_cp(("parallel", "parallel", "arbitrary")),
    )(a, b)


def _rms_fwd(x, g, *, name):
    S, D = x.shape
    tm = min(512, S)

    def body(x_ref, g_ref, o_ref):
        xv = x_ref[...]
        r = lax.rsqrt(jnp.mean(xv * xv, axis=-1, keepdims=True) + EPS)
        o_ref[...] = (xv * r * g_ref[...]).astype(BF16)

    return pl.pallas_call(
        body, name=name, grid=(S // tm,),
        in_specs=[pl.BlockSpec((tm, D), lambda i: (i, 0)), pl.BlockSpec((1, D), lambda i: (0, 0))],
        out_specs=pl.BlockSpec((tm, D), lambda i: (i, 0)),
        out_shape=jax.ShapeDtypeStruct((S, D), BF16),
        compiler_params=_cp(("parallel",)),
    )(x, g)


def _rms_bwd(x, g, dh, dy, *, name):
    S, D = x.shape
    tm = min(512, S)
    want_dx = dy is not None

    def body(*refs):
        if want_dx:
            x_ref, g_ref, dh_ref, dy_ref, dx_ref, dg_ref = refs
        else:
            x_ref, g_ref, dh_ref, dg_ref = refs
        i = pl.program_id(0)
        xv = x_ref[...]
        r = lax.rsqrt(jnp.mean(xv * xv, axis=-1, keepdims=True) + EPS)
        xh = xv * r
        dhv = dh_ref[...]
        part = jnp.sum(dhv * xh, axis=0, keepdims=True)

        @pl.when(i == 0)
        def _():
            dg_ref[...] = part

        @pl.when(i > 0)
        def _():
            dg_ref[...] += part

        if want_dx:
            dxh = dhv * g_ref[...]
            dx_ref[...] = dy_ref[...] + r * (dxh - xh * jnp.mean(dxh * xh, axis=-1, keepdims=True))

    row = pl.BlockSpec((tm, D), lambda i: (i, 0))
    vec = pl.BlockSpec((1, D), lambda i: (0, 0))
    if want_dx:
        return pl.pallas_call(
            body, name=name, grid=(S // tm,), in_specs=[row, vec, row, row], out_specs=[row, vec],
            out_shape=[jax.ShapeDtypeStruct((S, D), F32), jax.ShapeDtypeStruct((1, D), F32)],
            compiler_params=_cp(("arbitrary",)))(x, g, dh, dy)
    return pl.pallas_call(
        body, name=name, grid=(S // tm,), in_specs=[row, vec, row], out_specs=vec,
        out_shape=jax.ShapeDtypeStruct((1, D), F32),
        compiler_params=_cp(("arbitrary",)))(x, g, dh)


def _post(x, out, tgt, g, *, name):
    S, D = x.shape
    tm = min(512, S)

    def body(x_ref, o_ref, t_ref, g_ref, dy_ref, do_ref, dg_ref, loss_ref):
        i = pl.program_id(0)
        ov = o_ref[...]
        r = lax.rsqrt(jnp.mean(ov * ov, axis=-1, keepdims=True) + EPS)
        n = ov * r
        gv = g_ref[...]
        e = (x_ref[...] + n * gv) - t_ref[...]
        lpart = 0.5 * jnp.sum(jnp.mean(e * e, axis=-1, keepdims=True), axis=0, keepdims=True)
        dy = e * (1.0 / D)
        dy_ref[...] = dy
        dn = dy * gv
        do_ref[...] = (r * (dn - n * jnp.mean(dn * n, axis=-1, keepdims=True))).astype(BF16)
        gpart = jnp.sum(dy * n, axis=0, keepdims=True)
        lrow = jnp.broadcast_to(lpart, (1, LANES))

        @pl.when(i == 0)
        def _():
            dg_ref[...] = gpart
            loss_ref[...] = lrow

        @pl.when(i > 0)
        def _():
            dg_ref[...] += gpart
            loss_ref[...] += lrow

    row = pl.BlockSpec((tm, D), lambda i: (i, 0))
    vec = pl.BlockSpec((1, D), lambda i: (0, 0))
    return pl.pallas_call(
        body, name=name, grid=(S // tm,), in_specs=[row, row, row, vec],
        out_specs=[row, row, vec, pl.BlockSpec((1, LANES), lambda i: (0, 0))],
        out_shape=[jax.ShapeDtypeStruct((S, D), F32), jax.ShapeDtypeStruct((S, D), BF16),
                   jax.ShapeDtypeStruct((1, D), F32), jax.ShapeDtypeStruct((1, LANES), F32)],
        compiler_params=_cp(("arbitrary",)))(x, out, tgt, g)


def _to_classes(t, d):
    if d == 1:
        return t
    S, C = t.shape
    return t.reshape(S // d, d, C).transpose(1, 0, 2).reshape(S, C)


def _from_classes(t, d):
    if d == 1:
        return t
    S, C = t.shape
    return t.reshape(d, S // d, C).transpose(1, 0, 2).reshape(S, C)


def _rope(x, c, s1, s2):
    return x * c + pltpu.roll(x, LANES - ROT // 2, 1) * s1 + pltpu.roll(x, ROT // 2, 1) * s2


def _unrope(d, c, s1, s2):
    return d * c + pltpu.roll(d * s1, ROT // 2, 1) + pltpu.roll(d * s2, LANES - ROT // 2, 1)


def _a_masks():
    qi = lax.broadcasted_iota(jnp.int32, (HEAD, HEAD), 0)
    ki = lax.broadcasted_iota(jnp.int32, (HEAD, HEAD), 1)
    return ki >= qi, ki <= qi


A_SCALE = HEAD ** -0.5


def _a_geometry(S, g):
    d = DIL[g]
    L = S // d
    TQ = min(512, L)
    return d, L, TQ, TQ // HEAD, L // TQ, L // HEAD


def _attn_a_fwd(ua, tabs, g, *, name):
    S = ua.shape[0]
    d, L, TQ, nsub, nb, nblk = _a_geometry(S, g)

    def body(q_ref, kc_ref, kp_ref, vc_ref, vp_ref, c_ref, s1_ref, s2_ref, cp_ref, s1p_ref, s2p_ref,
             o_ref, l_ref):
        n = pl.program_id(1)
        tc = (c_ref[...], s1_ref[...], s2_ref[...])
        q = _rope(q_ref[...], *tc).astype(BF16)
        kc = _rope(kc_ref[...], *tc).astype(BF16)
        kp = _rope(kp_ref[...], cp_ref[...], s1p_ref[...], s2p_ref[...]).astype(BF16)
        vc = vc_ref[...].astype(BF16)
        vp = vp_ref[...].astype(BF16)
        mprev, mcur = _a_masks()
        for a in range(nsub):
            sl = slice(a * HEAD, (a + 1) * HEAD)
            pv = slice((a - 1) * HEAD, a * HEAD)
            qa = q[sl]
            k_prev, v_prev = (kp, vp) if a == 0 else (kc[pv], vc[pv])
            mp = jnp.logical_and(mprev, n > 0) if a == 0 else mprev
            s_p = jnp.where(mp, _dot_nt(qa, k_prev) * A_SCALE, NEG)
            s_c = jnp.where(mcur, _dot_nt(qa, kc[sl]) * A_SCALE, NEG)
            m = jnp.maximum(jnp.max(s_p, axis=-1, keepdims=True), jnp.max(s_c, axis=-1, keepdims=True))
            p_p = jnp.exp(s_p - m)
            p_c = jnp.exp(s_c - m)
            den = jnp.sum(p_p, axis=-1, keepdims=True) + jnp.sum(p_c, axis=-1, keepdims=True)
            o = (_dot(p_p.astype(BF16), v_prev) + _dot(p_c.astype(BF16), vc[sl])) / den
            o_ref[sl, :] = o
            l_ref[sl, :] = jnp.broadcast_to(m + jnp.log(den), (HEAD, HEAD))

    rcur = lambda cb, n: (cb // A_HEADS) * nb + n
    rprv = lambda cb, n: (cb // A_HEADS) * nblk + jnp.maximum(n * nsub - 1, 0)
    cur = lambda off: pl.BlockSpec((TQ, HEAD), lambda cb, n: (rcur(cb, n), off + cb % A_HEADS))
    prv = lambda off: pl.BlockSpec((HEAD, HEAD), lambda cb, n: (rprv(cb, n), off + cb % A_HEADS))
    tcur = pl.BlockSpec((TQ, LANES), lambda cb, n: (rcur(cb, n), 0))
    tprv = pl.BlockSpec((HEAD, LANES), lambda cb, n: (rprv(cb, n), 0))
    out = pl.BlockSpec((TQ, HEAD), lambda cb, n: (rcur(cb, n), cb % A_HEADS))
    return pl.pallas_call(
        body, name=name, grid=(A_HEADS * d, nb),
        in_specs=[cur(0), cur(4), prv(4), cur(8), prv(8), tcur, tcur, tcur, tprv, tprv, tprv],
        out_specs=[out, out],
        out_shape=[jax.ShapeDtypeStruct((S, A_WIDTH), F32)] * 2,
        compiler_params=_cp(("parallel", "parallel")),
    )(ua, ua, ua, ua, ua, *tabs, *tabs)


def _attn_a_dq(ua, tabs, g, do, lse, adj, *, name):
    S = ua.shape[0]
    d, L, TQ, nsub, nb, nblk = _a_geometry(S, g)

    def body(q_ref, kc_ref, kp_ref, vc_ref, vp_ref, do_ref, l_ref, adj_ref,
             c_ref, s1_ref, s2_ref, cp_ref, s1p_ref, s2p_ref, dq_ref):
        n = pl.program_id(1)
        tc = (c_ref[...], s1_ref[...], s2_ref[...])
        q = _rope(q_ref[...], *tc).astype(BF16)
        kc = _rope(kc_ref[...], *tc).astype(BF16)
        kp = _rope(kp_ref[...], cp_ref[...], s1p_ref[...], s2p_ref[...]).astype(BF16)
        vc = vc_ref[...].astype(BF16)
        vp = vp_ref[...].astype(BF16)
        mprev, mcur = _a_masks()
        for a in range(nsub):
            sl = slice(a * HEAD, (a + 1) * HEAD)
            pv = slice((a - 1) * HEAD, a * HEAD)
            qa = q[sl]
            k_prev, v_prev = (kp, vp) if a == 0 else (kc[pv], vc[pv])
            mp = jnp.logical_and(mprev, n > 0) if a == 0 else mprev
            lse_a = l_ref[sl, :][:, :1]
            adj_a = adj_ref[sl, :][:, :1]
            doa = do_ref[sl, :]
            p_p = jnp.exp(jnp.where(mp, _dot_nt(qa, k_prev) * A_SCALE, NEG) - lse_a)
            p_c = jnp.exp(jnp.where(mcur, _dot_nt(qa, kc[sl]) * A_SCALE, NEG) - lse_a)
            ds_p = p_p * (_dot_nt(doa, v_prev) + adj_a)
            ds_c = p_c * (_dot_nt(doa, vc[sl]) + adj_a)
            dq = (_dot(ds_p.astype(BF16), k_prev) + _dot(ds_c.astype(BF16), kc[sl])) * A_SCALE
            dq_ref[sl, :] = _unrope(dq, c_ref[sl, :], s1_ref[sl, :], s2_ref[sl, :]).astype(BF16)

    rcur = lambda cb, n: (cb // A_HEADS) * nb + n
    rprv = lambda cb, n: (cb // A_HEADS) * nblk + jnp.maximum(n * nsub - 1, 0)
    cur = lambda off: pl.BlockSpec((TQ, HEAD), lambda cb, n: (rcur(cb, n), off + cb % A_HEADS))
    prv = lambda off: pl.BlockSpec((HEAD, HEAD), lambda cb, n: (rprv(cb, n), off + cb % A_HEADS))
    tcur = pl.BlockSpec((TQ, LANES), lambda cb, n: (rcur(cb, n), 0))
    tprv = pl.BlockSpec((HEAD, LANES), lambda cb, n: (rprv(cb, n), 0))
    blk = pl.BlockSpec((TQ, HEAD), lambda cb, n: (rcur(cb, n), cb % A_HEADS))
    return pl.pallas_call(
        body, name=name, grid=(A_HEADS * d, nb),
        in_specs=[cur(0), cur(4), prv(4), cur(8), prv(8), blk, blk, blk,
                  tcur, tcur, tcur, tprv, tprv, tprv],
        out_specs=blk,
        out_shape=jax.ShapeDtypeStruct((S, A_WIDTH), BF16),
        compiler_params=_cp(("parallel", "parallel")),
    )(ua, ua, ua, ua, ua, do, lse, adj, *tabs, *tabs)


def _attn_a_dkv(ua, tabs, g, do, lse, adj, *, name):
    S = ua.shape[0]
    d, L, TQ, nsub, nb, nblk = _a_geometry(S, g)

    def body(qc_ref, qn_ref, kc_ref, vc_ref, doc_ref, don_ref, lc_ref, ln_ref, ac_ref, an_ref,
             c_ref, s1_ref, s2_ref, cn_ref, s1n_ref, s2n_ref, dk_ref, dv_ref):
        n = pl.program_id(1)
        tc = (c_ref[...], s1_ref[...], s2_ref[...])
        qc = _rope(qc_ref[...], *tc).astype(BF16)
        qn = _rope(qn_ref[...], cn_ref[...], s1n_ref[...], s2n_ref[...]).astype(BF16)
        kc = _rope(kc_ref[...], *tc).astype(BF16)
        vc = vc_ref[...].astype(BF16)
        kr = lax.broadcasted_iota(jnp.int32, (HEAD, HEAD), 0)
        qc_i = lax.broadcasted_iota(jnp.int32, (HEAD, HEAD), 1)
        own_t = kr <= qc_i
        nxt_t = kr >= qc_i
        has_next = n < nb - 1
        for b in range(nsub):
            sl = slice(b * HEAD, (b + 1) * HEAD)
            nx = slice((b + 1) * HEAD, (b + 2) * HEAD)
            kb, vb = kc[sl], vc[sl]
            last = b == nsub - 1
            parts = [(qc[sl], doc_ref[sl, :], lc_ref[sl, :], ac_ref[sl, :], own_t)]
            if last:
                parts.append((qn, don_ref[...], ln_ref[...], an_ref[...], jnp.logical_and(nxt_t, has_next)))
            else:
                parts.append((qc[nx], doc_ref[nx, :], lc_ref[nx, :], ac_ref[nx, :], nxt_t))
            dk = jnp.zeros((HEAD, HEAD), F32)
            dv = jnp.zeros((HEAD, HEAD), F32)
            for qq, dd, ll, aa, msk in parts:
                st = jnp.where(msk, _dot_nt(kb, qq) * A_SCALE, NEG)
                pt = jnp.exp(st - ll.T)
                dv = dv + _dot(pt.astype(BF16), dd)
                dst = pt * (_dot_nt(vb, dd) + aa.T)
                dk = dk + _dot(dst.astype(BF16), qq)
            dk = dk * A_SCALE
            dk_ref[sl, :] = _unrope(dk, c_ref[sl, :], s1_ref[sl, :], s2_ref[sl, :]).astype(BF16)
            dv_ref[sl, :] = dv.astype(BF16)

    rcur = lambda cb, n: (cb // A_HEADS) * nb + n
    rnxt = lambda cb, n: (cb // A_HEADS) * nblk + jnp.minimum((n + 1) * nsub, nblk - 1)
    cur = lambda off: pl.BlockSpec((TQ, HEAD), lambda cb, n: (rcur(cb, n), off + cb % A_HEADS))
    nxu = lambda off: pl.BlockSpec((HEAD, HEAD), lambda cb, n: (rnxt(cb, n), off + cb % A_HEADS))
    tcur = pl.BlockSpec((TQ, LANES), lambda cb, n: (rcur(cb, n), 0))
    tnxt = pl.BlockSpec((HEAD, LANES), lambda cb, n: (rnxt(cb, n), 0))
    blk = pl.BlockSpec((TQ, HEAD), lambda cb, n: (rcur(cb, n), cb % A_HEADS))
    bnx = pl.BlockSpec((HEAD, HEAD), lambda cb, n: (rnxt(cb, n), cb % A_HEADS))
    return pl.pallas_call(
        body, name=name, grid=(A_HEADS * d, nb),
        in_specs=[cur(0), nxu(0), cur(4), cur(8), blk, bnx, blk, bnx, blk, bnx,
                  tcur, tcur, tcur, tnxt, tnxt, tnxt],
        out_specs=[blk, blk],
        out_shape=[jax.ShapeDtypeStruct((S, A_WIDTH), BF16)] * 2,
        compiler_params=_cp(("parallel", "parallel")),
    )(ua, ua, ua, ua, do, do, lse, lse, adj, adj, *tabs, *tabs)


def _silu_parts(z):
    sg = _sigmoid(z)
    return z * sg, sg * (1.0 + z * (1.0 - sg))


def _merge_a_fwd(os_, ls_, ur, *, name):
    S = ur.shape[0]
    tm = min(512, S)

    def body(o0, o1, o2, l0, l1, l2, z_ref, y_ref):
        ls = [l0[...], l1[...], l2[...]]
        mx = jnp.maximum(jnp.maximum(ls[0], ls[1]), ls[2])
        es = [jnp.exp(l - mx) for l in ls]
        den = es[0] + es[1] + es[2]
        y = (es[0] / den) * o0[...] + (es[1] / den) * o1[...] + (es[2] / den) * o2[...]
        y_ref[...] = (y * _silu_parts(z_ref[...])[0]).astype(BF16)

    blk = pl.BlockSpec((tm, A_WIDTH), lambda i: (i, 0))
    return pl.pallas_call(
        body, name=name, grid=(S // tm,),
        in_specs=[blk] * 6 + [pl.BlockSpec((tm, A_WIDTH), lambda i: (i, R_ZA // A_WIDTH))],
        out_specs=blk, out_shape=jax.ShapeDtypeStruct((S, A_WIDTH), BF16),
        compiler_params=_cp(("parallel",)))(*os_, *ls_, ur)


def _merge_a_bwd(os_, ls_, ur, dya, *, name):
    S = ur.shape[0]
    tm = min(256, S)

    def body(o0, o1, o2, l0, l1, l2, z_ref, dy_ref, d0, d1, d2, a0, a1, a2, dz_ref):
        ls = [l0[...], l1[...], l2[...]]
        ov = [o0[...], o1[...], o2[...]]
        mx = jnp.maximum(jnp.maximum(ls[0], ls[1]), ls[2])
        es = [jnp.exp(l - mx) for l in ls]
        den = es[0] + es[1] + es[2]
        ws = [e / den for e in es]
        y = ws[0] * ov[0] + ws[1] * ov[1] + ws[2] * ov[2]
        sz, dsz = _silu_parts(z_ref[...])
        dyv = dy_ref[...]
        dz_ref[...] = (dyv * y * dsz).astype(BF16)
        dyp = dyv * sz
        for h in range(A_HEADS):
            sl = slice(h * HEAD, (h + 1) * HEAD)
            t = jnp.zeros((tm, 1), F32)
            for gi in range(3):
                t = t + ws[gi][:, sl][:, :1] * jnp.sum(dyp[:, sl] * ov[gi][:, sl], axis=-1, keepdims=True)
            for gi, (dref, aref) in enumerate(((d0, a0), (d1, a1), (d2, a2))):
                wg = ws[gi][:, sl]
                dref[:, sl] = (wg * dyp[:, sl]).astype(BF16)
                aref[:, sl] = -wg * t

    blk = pl.BlockSpec((tm, A_WIDTH), lambda i: (i, 0))
    outs = pl.pallas_call(
        body, name=name, grid=(S // tm,),
        in_specs=[blk] * 6 + [pl.BlockSpec((tm, A_WIDTH), lambda i: (i, R_ZA // A_WIDTH)), blk],
        out_specs=[blk] * 7,
        out_shape=[jax.ShapeDtypeStruct((S, A_WIDTH), BF16)] * 3
        + [jax.ShapeDtypeStruct((S, A_WIDTH), F32)] * 3 + [jax.ShapeDtypeStruct((S, A_WIDTH), BF16)],
        compiler_params=_cp(("parallel",)))(*os_, *ls_, ur, dya)
    return outs[0:3], outs[3:6], outs[6]


def _logf(ur, bf_pad, *, name):
    S = ur.shape[0]
    tm = min(1024, S)

    def body(u_ref, b_ref, o_ref):
        z = u_ref[...] + b_ref[...]
        o_ref[...] = jnp.minimum(z, 0.0) - jnp.log(1.0 + jnp.exp(-jnp.abs(z)))

    return pl.pallas_call(
        body, name=name, grid=(S // tm,),
        in_specs=[pl.BlockSpec((tm, FB_PAD), lambda i: (i, R_FB // FB_PAD)),
                  pl.BlockSpec((1, FB_PAD), lambda i: (0, 0))],
        out_specs=pl.BlockSpec((tm, FB_PAD), lambda i: (i, 0)),
        out_shape=jax.ShapeDtypeStruct((S, FB_PAD), F32),
        compiler_params=_cp(("parallel",)))(ur, bf_pad)


def _cumsum_lanes(x, reverse, *, name):
    nt, H, _ = x.shape

    def body(x_ref, o_ref):
        lane = lax.broadcasted_iota(jnp.int32, (H, LANES), 1)

        def tile(t, carry):
            tt = nt - 1 - t if reverse else t
            v = x_ref[tt]
            k = 1
            while k < LANES:
                if reverse:
                    v = v + jnp.where(lane < LANES - k, pltpu.roll(v, LANES - k, 1), 0.0)
                else:
                    v = v + jnp.where(lane >= k, pltpu.roll(v, k, 1), 0.0)
                k *= 2
            v = v + carry
            o_ref[tt] = v
            edge = v[:, :1] if reverse else v[:, LANES - 1:]
            return jnp.broadcast_to(edge, (H, LANES))

        lax.fori_loop(0, nt, tile, jnp.zeros((H, LANES), F32))

    return pl.pallas_call(
        body, name=name, out_shape=jax.ShapeDtypeStruct((nt, H, LANES), F32),
        in_specs=[pl.BlockSpec(memory_space=pltpu.VMEM)], out_specs=pl.BlockSpec(memory_space=pltpu.VMEM),
        compiler_params=_cp())(x)


B_SCALE = B_HEAD ** -0.5


def _pair_masks():
    lane = lax.broadcasted_iota(jnp.int32, (1, LANES), 1)
    row = lax.broadcasted_iota(jnp.int32, (LANES, 1), 0)
    return (lane < B_HEAD, lane >= B_HEAD), (row < B_HEAD, row >= B_HEAD)


def _causal_t(T):
    r = lax.broadcasted_iota(jnp.int32, (T, T), 0)
    c = lax.broadcasted_iota(jnp.int32, (T, T), 1)
    return r <= c


def _zero_other(x, keep):
    return jnp.where(keep, x, jnp.zeros_like(x))


def _fox_fwd(ub, vt, crow, ckb, *, name):
    S = ub.shape[0]
    T = min(512, S)
    nq = S // T

    def body(q_ref, k_ref, vt_ref, cr_ref, ck_ref, o_ref, l_ref, m_s, l_s, acc_s):
        i = pl.program_id(1)
        lanes, rows = _pair_masks()
        q = q_ref[...] * B_SCALE
        qm = [_zero_other(q, lanes[0]), _zero_other(q, lanes[1])]
        m_s[...] = jnp.full((2, 1, T), NEG, F32)
        l_s[...] = jnp.zeros((2, 1, T), F32)
        acc_s[...] = jnp.zeros((LANES, T), F32)

        def step(j, masked):
            off = pl.multiple_of(j * T, T)
            kj = k_ref[pl.ds(off, T), :]
            vtj = vt_ref[j]
            upd = jnp.zeros((LANES, T), F32)
            alphas = []
            for a in range(2):
                st = _dot_nt(kj, qm[a]) + (cr_ref[a, i] - jnp.tile(ck_ref[a, pl.ds(off, T), :], (1, T // LANES)))
                if masked:
                    st = jnp.where(_causal_t(T), st, NEG)
                m_old = m_s[a]
                m_new = jnp.maximum(m_old, jnp.max(st, axis=0, keepdims=True))
                alpha = jnp.exp(m_old - m_new)
                pt = jnp.exp(st - m_new)
                l_s[a] = alpha * l_s[a] + jnp.sum(pt, axis=0, keepdims=True)
                m_s[a] = m_new
                upd = upd + _dot(_zero_other(vtj, rows[a]), pt.astype(BF16))
                alphas.append(alpha)
            acc_s[...] = acc_s[...] * jnp.where(rows[0], alphas[0], alphas[1]) + upd

        def loop(j, carry):
            step(j, False)
            return carry

        lax.fori_loop(0, i, loop, 0)
        step(i, True)
        o_ref[...] = (acc_s[...] / jnp.where(rows[0], l_s[0], l_s[1])).T
        l_ref[0] = m_s[0] + jnp.log(l_s[0])
        l_ref[1] = m_s[1] + jnp.log(l_s[1])

    stat = pl.BlockSpec((2, None, 1, T), lambda h, i: (h, i, 0, 0))
    return pl.pallas_call(
        body, name=name, grid=(B_HEADS // 2, nq),
        in_specs=[pl.BlockSpec((T, LANES), lambda h, i: (i, h)),
                  pl.BlockSpec((S, LANES), lambda h, i: (0, 4 + h)),
                  pl.BlockSpec((nq, LANES, T), lambda h, i: (0, h, 0)),
                  pl.BlockSpec((2, nq, 1, T), lambda h, i: (h, 0, 0, 0)),
                  pl.BlockSpec((2, S, LANES), lambda h, i: (h, 0, 0))],
        out_specs=[pl.BlockSpec((T, LANES), lambda h, i: (i, h)), stat],
        out_shape=[jax.ShapeDtypeStruct((S, A_WIDTH), F32), jax.ShapeDtypeStruct((B_HEADS, nq, 1, T), F32)],
        scratch_shapes=[pltpu.VMEM((2, 1, T), F32), pltpu.VMEM((2, 1, T), F32), pltpu.VMEM((LANES, T), F32)],
        compiler_params=_cp(("parallel", "parallel")),
    )(ub, ub, vt, crow, ckb)


def _fox_dq(ub, kt, crow, ckb, o, do, lse, *, name):
    S = ub.shape[0]
    T = min(512, S)
    nq = S // T

    def body(q_ref, k_ref, v_ref, kt_ref, cr_ref, ck_ref, o_ref, do_ref, l_ref, dq_ref, dl_ref, dc_ref, acc_s, dc_s):
        i = pl.program_id(1)
        lanes, rows = _pair_masks()
        q = q_ref[...] * B_SCALE
        dov = do_ref[...]
        qm = [_zero_other(q, lanes[0]), _zero_other(q, lanes[1])]
        dom = [_zero_other(dov, lanes[0]), _zero_other(dov, lanes[1])]
        prod_t = (dov.astype(F32) * o_ref[...]).T
        delta = [jnp.sum(_zero_other(prod_t, rows[a]), axis=0, keepdims=True) for a in range(2)]
        dl_ref[0] = delta[0]
        dl_ref[1] = delta[1]
        acc_s[...] = jnp.zeros((LANES, T), F32)
        dc_s[...] = jnp.zeros((2, 1, T), F32)

        def step(j, masked):
            off = pl.multiple_of(j * T, T)
            kj = k_ref[pl.ds(off, T), :]
            vj = v_ref[pl.ds(off, T), :]
            ktj = kt_ref[j]
            upd = jnp.zeros((LANES, T), F32)
            for a in range(2):
                st = _dot_nt(kj, qm[a]) + (cr_ref[a, i] - jnp.tile(ck_ref[a, pl.ds(off, T), :], (1, T // LANES)))
                if masked:
                    st = jnp.where(_causal_t(T), st, NEG)
                pt = jnp.exp(st - l_ref[a])
                dst = pt * (_dot_nt(vj, dom[a]) - delta[a])
                upd = upd + _dot(_zero_other(ktj, rows[a]), dst.astype(BF16))
                dc_s[a] += jnp.sum(dst, axis=0, keepdims=True)
            acc_s[...] += upd

        def loop(j, carry):
            step(j, False)
            return carry

        lax.fori_loop(0, i, loop, 0)
        step(i, True)
        dq_ref[...] = (acc_s[...] * B_SCALE).T.astype(BF16)
        dc_ref[...] = dc_s[...]

    tile = pl.BlockSpec((T, LANES), lambda h, i: (i, h))
    stat = pl.BlockSpec((2, None, 1, T), lambda h, i: (h, i, 0, 0))
    return pl.pallas_call(
        body, name=name, grid=(B_HEADS // 2, nq),
        in_specs=[tile,
                  pl.BlockSpec((S, LANES), lambda h, i: (0, 4 + h)),
                  pl.BlockSpec((S, LANES), lambda h, i: (0, 8 + h)),
                  pl.BlockSpec((nq, LANES, T), lambda h, i: (0, h, 0)),
                  pl.BlockSpec((2, nq, 1, T), lambda h, i: (h, 0, 0, 0)),
                  pl.BlockSpec((2, S, LANES), lambda h, i: (h, 0, 0)),
                  tile, tile, stat],
        out_specs=[tile, stat, stat],
        out_shape=[jax.ShapeDtypeStruct((S, A_WIDTH), BF16)] + [jax.ShapeDtypeStruct((B_HEADS, nq, 1, T), F32)] * 2,
        scratch_shapes=[pltpu.VMEM((LANES, T), F32), pltpu.VMEM((2, 1, T), F32)],
        compiler_params=_cp(("parallel", "parallel")),
    )(ub, ub, ub, kt, crow, ckb, o, do, lse)


def _fox_dkv(ub, crow, ckb, do, lse, delta, *, name):
    S = ub.shape[0]
    T = min(512, S)
    nq = S // T

    def body(k_ref, v_ref, q_ref, do_ref, cr_ref, ck_ref, l_ref, dl_ref, dk_ref, dv_ref, dc_ref, dk_s, dv_s, dc_s):
        j = pl.program_id(1)
        lanes, _ = _pair_masks()
        kv = k_ref[...]
        vv = v_ref[...]
        km = [_zero_other(kv, lanes[0]), _zero_other(kv, lanes[1])]
        ck = [jnp.tile(ck_ref[a], (1, T // LANES)) for a in range(2)]
        dk_s[...] = jnp.zeros((T, LANES), F32)
        dv_s[...] = jnp.zeros((T, LANES), F32)
        dc_s[...] = jnp.zeros((2, T, 1), F32)

        def step(i, masked):
            off = pl.multiple_of(i * T, T)
            qi = q_ref[pl.ds(off, T), :] * B_SCALE
            doi = do_ref[pl.ds(off, T), :]
            for a in range(2):
                st = _dot_nt(km[a], qi) + (cr_ref[a, i] - ck[a])
                if masked:
                    st = jnp.where(_causal_t(T), st, NEG)
                pt = jnp.exp(st - l_ref[a, i])
                doa = _zero_other(doi, lanes[a])
                dv_s[...] += _dot(pt.astype(BF16), doa)
                dst = pt * (_dot_nt(vv, doa) - dl_ref[a, i])
                dk_s[...] += _dot(dst.astype(BF16), _zero_other(qi, lanes[a]))
                dc_s[a] -= jnp.sum(dst, axis=-1, keepdims=True)

        def loop(i, carry):
            step(i, False)
            return carry

        step(j, True)
        lax.fori_loop(j + 1, nq, loop, 0)
        dk_ref[...] = dk_s[...].astype(BF16)
        dv_ref[...] = dv_s[...].astype(BF16)
        dc_ref[...] = dc_s[...]

    rowv = pl.BlockSpec((2, nq, 1, T), lambda h, j: (h, 0, 0, 0))
    tile = pl.BlockSpec((T, LANES), lambda h, j: (j, h))
    return pl.pallas_call(
        body, name=name, grid=(B_HEADS // 2, nq),
        in_specs=[pl.BlockSpec((T, LANES), lambda h, j: (j, 4 + h)),
                  pl.BlockSpec((T, LANES), lambda h, j: (j, 8 + h)),
                  pl.BlockSpec((S, LANES), lambda h, j: (0, h)),
                  pl.BlockSpec((S, LANES), lambda h, j: (0, h)),
                  rowv,
                  pl.BlockSpec((2, T, LANES), lambda h, j: (h, j, 0)),
                  rowv, rowv],
        out_specs=[tile, tile, pl.BlockSpec((2, T, 1), lambda h, j: (h, j, 0))],
        out_shape=[jax.ShapeDtypeStruct((S, A_WIDTH), BF16)] * 2 + [jax.ShapeDtypeStruct((B_HEADS, S, 1), F32)],
        scratch_shapes=[pltpu.VMEM((T, LANES), F32), pltpu.VMEM((T, LANES), F32), pltpu.VMEM((2, T, 1), F32)],
        compiler_params=_cp(("parallel", "parallel")),
    )(ub, ub, ub, do, crow, ckb, lse, delta)


def _gate_fwd(o, ur, zcol, *, name):
    S = ur.shape[0]
    tm = min(1024, S)

    def body(o_ref, z_ref, y_ref):
        y_ref[...] = (o_ref[...] * _silu_parts(z_ref[...])[0]).astype(BF16)

    blk = pl.BlockSpec((tm, A_WIDTH), lambda i: (i, 0))
    return pl.pallas_call(
        body, name=name, grid=(S // tm,),
        in_specs=[blk, pl.BlockSpec((tm, A_WIDTH), lambda i: (i, zcol // A_WIDTH))],
        out_specs=blk, out_shape=jax.ShapeDtypeStruct((S, A_WIDTH), BF16),
        compiler_params=_cp(("parallel",)))(o, ur)


def _gate_bwd(o, ur, zcol, dy, *, name):
    S = ur.shape[0]
    tm = min(1024, S)

    def body(o_ref, z_ref, dy_ref, do_ref, dz_ref):
        sz, dsz = _silu_parts(z_ref[...])
        dyv = dy_ref[...]
        do_ref[...] = (dyv * sz).astype(BF16)
        dz_ref[...] = (dyv * o_ref[...] * dsz).astype(BF16)

    blk = pl.BlockSpec((tm, A_WIDTH), lambda i: (i, 0))
    return pl.pallas_call(
        body, name=name, grid=(S // tm,),
        in_specs=[blk, pl.BlockSpec((tm, A_WIDTH), lambda i: (i, zcol // A_WIDTH)), blk],
        out_specs=[blk, blk], out_shape=[jax.ShapeDtypeStruct((S, A_WIDTH), BF16)] * 2,
        compiler_params=_cp(("parallel",)))(o, ur, dy)


def _dfb(ur, bf_pad, dlogf_pad, *, name):
    S = ur.shape[0]
    tm = min(1024, S)

    def body(u_ref, b_ref, d_ref, o_ref, s_ref):
        i = pl.program_id(0)
        dv = d_ref[...] * _sigmoid(-(u_ref[...] + b_ref[...]))
        o_ref[...] = dv.astype(BF16)
        part = jnp.sum(dv, axis=0, keepdims=True)

        @pl.when(i == 0)
        def _():
            s_ref[...] = part

        @pl.when(i > 0)
        def _():
            s_ref[...] += part

    vec = pl.BlockSpec((1, FB_PAD), lambda i: (0, 0))
    blk = pl.BlockSpec((tm, FB_PAD), lambda i: (i, 0))
    return pl.pallas_call(
        body, name=name, grid=(S // tm,),
        in_specs=[pl.BlockSpec((tm, FB_PAD), lambda i: (i, R_FB // FB_PAD)), vec, blk],
        out_specs=[blk, vec],
        out_shape=[jax.ShapeDtypeStruct((S, FB_PAD), BF16), jax.ShapeDtypeStruct((1, FB_PAD), F32)],
        compiler_params=_cp(("arbitrary",)))(ur, bf_pad, dlogf_pad)


M_SCALE = HEAD ** -0.5


def _mem_fwd(ur, mkv, *, name):
    S = ur.shape[0]
    T = min(512, S)

    def body(q_ref, z_ref, k_ref, v_ref, y_ref):
        s = _dot_nt(q_ref[...].astype(BF16), k_ref[...].astype(BF16)) * M_SCALE
        p = jnp.exp(s - jnp.max(s, axis=-1, keepdims=True))
        p = p / jnp.sum(p, axis=-1, keepdims=True)
        o = _dot(p.astype(BF16), v_ref[...].astype(BF16))
        y_ref[...] = (o * _silu_parts(z_ref[...])[0]).astype(BF16)

    return pl.pallas_call(
        body, name=name, grid=(S // T, M_HEADS),
        in_specs=[pl.BlockSpec((T, HEAD), lambda i, h: (i, R_QM // HEAD + h)),
                  pl.BlockSpec((T, HEAD), lambda i, h: (i, R_ZM // HEAD + h)),
                  pl.BlockSpec((N_MEM, HEAD), lambda i, h: (0, h)),
                  pl.BlockSpec((N_MEM, HEAD), lambda i, h: (0, M_HEADS + h))],
        out_specs=pl.BlockSpec((T, HEAD), lambda i, h: (i, h)),
        out_shape=jax.ShapeDtypeStruct((S, A_WIDTH), BF16),
        compiler_params=_cp(("parallel", "parallel")))(ur, ur, mkv, mkv)


def _mem_bwd(ur, mkv, dy, *, name):
    S = ur.shape[0]
    T = min(512, S)

    def body(q_ref, z_ref, k_ref, v_ref, dy_ref, dq_ref, dz_ref, dk_ref, dv_ref):
        i = pl.program_id(1)
        qv = q_ref[...].astype(BF16)
        kv = k_ref[...].astype(BF16)
        vv = v_ref[...].astype(BF16)
        s = _dot_nt(qv, kv) * M_SCALE
        p = jnp.exp(s - jnp.max(s, axis=-1, keepdims=True))
        p = p / jnp.sum(p, axis=-1, keepdims=True)
        o = _dot(p.astype(BF16), vv)
        sz, dsz = _silu_parts(z_ref[...])
        dyv = dy_ref[...]
        dz_ref[...] = (dyv * o * dsz).astype(BF16)
        dov = (dyv * sz).astype(BF16)
        dp = _dot_nt(dov, vv)
        ds = p * (dp - jnp.sum(p * dp, axis=-1, keepdims=True))
        dq_ref[...] = (_dot(ds.astype(BF16), kv) * M_SCALE).astype(BF16)
        dvp = _dot(p.T.astype(BF16), dov)
        dkp = _dot(ds.T.astype(BF16), qv) * M_SCALE

        @pl.when(i == 0)
        def _():
            dk_ref[...] = dkp
            dv_ref[...] = dvp

        @pl.when(i > 0)
        def _():
            dk_ref[...] += dkp
            dv_ref[...] += dvp

    tile = pl.BlockSpec((T, HEAD), lambda h, i: (i, h))
    acc = pl.BlockSpec((N_MEM, HEAD), lambda h, i: (0, h))
    return pl.pallas_call(
        body, name=name, grid=(M_HEADS, S // T),
        in_specs=[pl.BlockSpec((T, HEAD), lambda h, i: (i, R_QM // HEAD + h)),
                  pl.BlockSpec((T, HEAD), lambda h, i: (i, R_ZM // HEAD + h)),
                  pl.BlockSpec((N_MEM, HEAD), lambda h, i: (0, h)),
                  pl.BlockSpec((N_MEM, HEAD), lambda h, i: (0, M_HEADS + h)), tile],
        out_specs=[tile, tile, acc, acc],
        out_shape=[jax.ShapeDtypeStruct((S, A_WIDTH), BF16)] * 2
        + [jax.ShapeDtypeStruct((N_MEM, A_WIDTH), F32)] * 2,
        compiler_params=_cp(("parallel", "arbitrary")))(ur, ur, mkv, mkv, dy)


def _branch_fwd(ys, wbs, ur, b_merge, *, name):
    S = ur.shape[0]
    tm, tn = min(512, S), 512
    nj = D_MODEL // tn

    def body(ya, yb, ym, wa, wb, wm, g0, g1, g2, b0, b1, b2, mg_ref, p_ref):
        acc = jnp.zeros((tm, tn), F32)
        for i, (y, w, gr, br) in enumerate(((ya, wa, g0, b0), (yb, wb, g1, b1), (ym, wm, g2, b2))):
            pr = _dot(y[...], w[...])
            p_ref[i] = pr
            acc = acc + _sigmoid(gr[...] + br[...]) * pr
        mg_ref[...] = acc.astype(BF16)

    yspec = pl.BlockSpec((tm, A_WIDTH), lambda i, j: (i, 0))
    wspec = pl.BlockSpec((A_WIDTH, tn), lambda i, j: (0, j))
    gspec = lambda b: pl.BlockSpec((tm, tn), lambda i, j: (i, (R_GL + b * D_MODEL) // tn + j))
    bspec = lambda b: pl.BlockSpec((1, tn), lambda i, j: (0, b * nj + j))
    return pl.pallas_call(
        body, name=name, grid=(S // tm, nj),
        in_specs=[yspec] * 3 + [wspec] * 3 + [gspec(0), gspec(1), gspec(2), bspec(0), bspec(1), bspec(2)],
        out_specs=[pl.BlockSpec((tm, tn), lambda i, j: (i, j)),
                   pl.BlockSpec((3, tm, tn), lambda i, j: (0, i, j))],
        out_shape=[jax.ShapeDtypeStruct((S, D_MODEL), BF16), jax.ShapeDtypeStruct((3, S, D_MODEL), F32)],
        compiler_params=_cp(("parallel", "parallel")))(*ys, *wbs, ur, ur, ur, b_merge, b_merge, b_merge)


def _branch_bwd(dm, prods, ur, b_merge, *, name):
    S = ur.shape[0]
    tm = min(256, S)

    def body(dm_ref, p_ref, g0, g1, g2, b_ref, dp_ref, dgl_ref, db_ref):
        i = pl.program_id(0)
        dmv = dm_ref[...]
        parts = []
        for b, gr in enumerate((g0, g1, g2)):
            sl = slice(b * D_MODEL, (b + 1) * D_MODEL)
            gt = _sigmoid(gr[...] + b_ref[:, sl])
            dp_ref[b] = (dmv * gt).astype(BF16)
            dgl = dmv * p_ref[b] * gt * (1.0 - gt)
            dgl_ref[:, sl] = dgl.astype(BF16)
            parts.append(jnp.sum(dgl, axis=0, keepdims=True))
        part = jnp.concatenate(parts, axis=1)

        @pl.when(i == 0)
        def _():
            db_ref[...] = part

        @pl.when(i > 0)
        def _():
            db_ref[...] += part

    gspec = lambda b: pl.BlockSpec((tm, D_MODEL), lambda i: (i, R_GL // D_MODEL + b))
    vec = pl.BlockSpec((1, 3 * D_MODEL), lambda i: (0, 0))
    return pl.pallas_call(
        body, name=name, grid=(S // tm,),
        in_specs=[pl.BlockSpec((tm, D_MODEL), lambda i: (i, 0)),
                  pl.BlockSpec((3, tm, D_MODEL), lambda i: (0, i, 0)), gspec(0), gspec(1), gspec(2), vec],
        out_specs=[pl.BlockSpec((3, tm, D_MODEL), lambda i: (0, i, 0)),
                   pl.BlockSpec((tm, 3 * D_MODEL), lambda i: (i, 0)), vec],
        out_shape=[jax.ShapeDtypeStruct((3, S, D_MODEL), BF16), jax.ShapeDtypeStruct((S, 3 * D_MODEL), BF16),
                   jax.ShapeDtypeStruct((1, 3 * D_MODEL), F32)],
        compiler_params=_cp(("arbitrary",)))(dm, prods, ur, ur, ur, b_merge)


def _rope_tables(pos):
    half = ROT // 2
    inv = ROPE_THETA ** (-jnp.arange(half, dtype=F32) / half)
    ang = pos.astype(F32)[:, None] * inv
    cos, sin = jnp.cos(ang), jnp.sin(ang)
    S = pos.shape[0]
    one = jnp.ones((S, LANES - ROT), F32)
    zero = jnp.zeros((S, LANES - ROT), F32)
    zh = jnp.zeros((S, half), F32)
    c = jnp.concatenate([cos, cos, one], axis=1)
    s1 = jnp.concatenate([-sin, zh, zero], axis=1)
    s2 = jnp.concatenate([zh, sin, zero], axis=1)
    return c, s1, s2


def _to_tiles(t):
    S, H = t.shape
    return t.reshape(S // LANES, LANES, H).transpose(0, 2, 1)


def _from_tiles(t):
    nt, H, _ = t.shape
    return t.transpose(1, 0, 2).reshape(H, nt * LANES)


def _local_step(x, mem, pos, tgt, g_pre, g_post, g_mem, wt, bf_pad, b_merge, w_kv, wbs, w_out):
    S = x.shape[0]
    T = min(512, S)
    nq = S // T
    tabs = _rope_tables(pos)

    h = _rms_fwd(x, g_pre, name="rms_pre")
    hs = [_to_classes(h, d) for d in DIL]
    tabs_g = [[_to_classes(t, d) for t in tabs] for d in DIL]
    uas = [_mm(hs[g], wt[f"A{g}"], bt=True, name=f"proj_a{g}", tn=1536) for g in range(3)]
    ub = _mm(h, wt["B"], bt=True, out_dtype=BF16, name="proj_b", tn=1536)
    ur = _mm(h, wt["R"], bt=True, name="proj_r", tn=1792)

    outs_c, lses_c = [], []
    for g in range(3):
        o, l = _attn_a_fwd(uas[g], tabs_g[g], g, name=f"attn_a_fwd{g}")
        outs_c.append(o)
        lses_c.append(l)
    outs_a = [_from_classes(o, d) for o, d in zip(outs_c, DIL)]
    lses_a = [_from_classes(l, d) for l, d in zip(lses_c, DIL)]
    ya = _merge_a_fwd(outs_a, lses_a, ur, name="merge_a_fwd")

    logf = _logf(ur, bf_pad, name="logf")
    c = _from_tiles(_cumsum_lanes(_to_tiles(logf[:, :B_HEADS]), False, name="cumsum_fwd"))
    crow = c.reshape(B_HEADS, nq, 1, T)
    ckb = jnp.broadcast_to(c[:, :, None], (B_HEADS, S, LANES))
    kt = ub[:, 512:1024].reshape(nq, T, 512).transpose(0, 2, 1)
    vt = ub[:, 1024:1536].reshape(nq, T, 512).transpose(0, 2, 1)
    ob, lse_b = _fox_fwd(ub, vt, crow, ckb, name="fox_fwd")
    yb = _gate_fwd(ob, ur, R_ZB, name="gate_b_fwd")

    hm = _rms_fwd(mem, g_mem, name="rms_mem")
    mkv = _mm(hm, w_kv, name="proj_mem")
    ym = _mem_fwd(ur, mkv, name="mem_fwd")

    merged, prods = _branch_fwd((ya, yb, ym), wbs, ur, b_merge, name="branch_fwd")
    out = _mm(merged, w_out, name="proj_out")
    dy, d_out, dg_post, loss_row = _post(x, out, tgt, g_post, name="post")

    dmerged = _mm(d_out, w_out, bt=True, name="d_merged")
    dw_out = _mm(merged.T, d_out, name="dw_out", tk=2048)
    dprods, dgl, db_merge = _branch_bwd(dmerged, prods, ur, b_merge, name="branch_bwd")
    dys, dwbs = [], []
    for i, (y, wb) in enumerate(zip((ya, yb, ym), wbs)):
        dys.append(_mm(dprods[i], wb, bt=True, name=f"d_y{i}"))
        dwbs.append(_mm(y.T, dprods[i], name=f"dw_branch{i}", tk=2048))

    dos_a, adjs_a, dza = _merge_a_bwd(outs_a, lses_a, ur, dys[0], name="merge_a_bwd")
    dus_a = []
    for g, d in enumerate(DIL):
        do_c, adj_c = _to_classes(dos_a[g], d), _to_classes(adjs_a[g], d)
        dq = _attn_a_dq(uas[g], tabs_g[g], g, do_c, lses_c[g], adj_c, name=f"attn_a_dq{g}")
        dk, dv = _attn_a_dkv(uas[g], tabs_g[g], g, do_c, lses_c[g], adj_c, name=f"attn_a_dkv{g}")
        dus_a.append(jnp.concatenate([dq, dk, dv], axis=1))

    dob, dzb = _gate_bwd(ob, ur, R_ZB, dys[1], name="gate_b_bwd")
    dqb, delta_b, dc_q = _fox_dq(ub, kt, crow, ckb, ob, dob, lse_b, name="fox_dq")
    dkb, dvb, dc_k = _fox_dkv(ub, crow, ckb, dob, lse_b, delta_b, name="fox_dkv")
    du_b = jnp.concatenate([dqb, dkb, dvb], axis=1)
    dc = dc_q.reshape(B_HEADS, S) + dc_k.reshape(B_HEADS, S)
    dlogf = _from_tiles(_cumsum_lanes(_to_tiles(dc.T), True, name="cumsum_bwd"))
    dlogf_pad = jnp.pad(dlogf.T, ((0, 0), (0, FB_PAD - B_HEADS)))
    dfb, db_forget = _dfb(ur, bf_pad, dlogf_pad, name="dfb")

    dqm, dzm, dmk, dmv = _mem_bwd(ur, mkv, dys[2], name="mem_bwd")
    dmkv = jnp.concatenate([dmk, dmv], axis=1).astype(BF16)
    dhm = _mm(dmkv, w_kv, bt=True, name="d_hm")
    dw_kv = _mm(hm.T, dmkv, name="dw_kv")
    dg_mem = _rms_bwd(mem, g_mem, dhm, None, name="rms_mem_bwd")

    du_r = jnp.concatenate([dza, dzb, dqm, dzm, dgl, dfb], axis=1)
    dh = _mm(du_r, wt["R"], name="d_h_r", tk=1792) + _mm(du_b, wt["B"], name="d_h_b", tk=1536)
    for g, d in enumerate(DIL):
        dh = dh + _from_classes(_mm(dus_a[g], wt[f"A{g}"], name=f"d_h_a{g}", tk=1536), d)
    dwt = {"R": _mm(h.T, du_r, name="dw_in_r", tn=1792, tk=1024).T,
           "B": _mm(h.T, du_b, name="dw_in_b", tn=1536, tk=2048).T}
    for g in range(3):
        dwt[f"A{g}"] = _mm(hs[g].T, dus_a[g], name=f"dw_in_a{g}", tn=1536, tk=2048).T
    grad_x, dg_pre = _rms_bwd(x, g_pre, dh, dy, name="rms_pre_bwd")

    return dict(loss=loss_row, grad_x=grad_x, dwt=dwt, dw_kv=dw_kv, dwbs=dwbs, dw_out=dw_out,
                dg_pre=dg_pre, dg_post=dg_post, dg_mem=dg_mem, db_forget=db_forget, db_merge=db_merge)


MESH = pl.DeviceIdType.MESH
ANY = pl.BlockSpec(memory_space=pl.ANY)


def _relations():
    return [(k >> 2 & 1, k >> 1 & 1, k & 1) for k in range(1, N_DEV)]


def _coords():
    return lax.axis_index("x"), lax.axis_index("y"), lax.axis_index("c")


def _all_gather(shard, *, name):
    R, W = shard.shape

    def body(x_ref, out_ref, send_sems, recv_sems, local_sem):
        x, y, c = _coords()
        me = 4 * x + 2 * y + c
        mine = pltpu.make_async_copy(x_ref, out_ref.at[me], local_sem)
        mine.start()
        copies = []
        for k, (fx, fy, fc) in enumerate(_relations()):
            cp = pltpu.make_async_remote_copy(
                src_ref=x_ref, dst_ref=out_ref.at[me], send_sem=send_sems.at[k], recv_sem=recv_sems.at[k],
                device_id=(x ^ fx, y ^ fy, c ^ fc), device_id_type=MESH)
            cp.start()
            copies.append(cp)
        for k, (fx, fy, fc) in enumerate(_relations()):
            peer = 4 * (x ^ fx) + 2 * (y ^ fy) + (c ^ fc)
            pltpu.make_async_remote_copy(
                src_ref=x_ref, dst_ref=out_ref.at[peer], send_sem=send_sems.at[k], recv_sem=recv_sems.at[k],
                device_id=(x ^ fx, y ^ fy, c ^ fc), device_id_type=MESH).wait_recv()
        for cp in copies:
            cp.wait_send()
        mine.wait()

    return pl.pallas_call(
        body, name=name, out_shape=jax.ShapeDtypeStruct((N_DEV, R, W), shard.dtype),
        in_specs=[ANY], out_specs=ANY,
        scratch_shapes=[pltpu.SemaphoreType.DMA((N_DEV - 1,)), pltpu.SemaphoreType.DMA((N_DEV - 1,)),
                        pltpu.SemaphoreType.DMA],
    )(shard)


def _exchange_grads(gbig, gsmall, *, name):
    _, R, W = gbig.shape
    n = N_DEV - 1

    def body(g_ref, s_ref, rb_ref, rs_ref, send_sems, recv_sems, local_sems):
        x, y, c = _coords()
        me = 4 * x + 2 * y + c
        loc = [pltpu.make_async_copy(g_ref.at[me], rb_ref.at[me], local_sems.at[0]),
               pltpu.make_async_copy(s_ref, rs_ref.at[me], local_sems.at[1])]
        for cp in loc:
            cp.start()

        def copies(k, fx, fy, fc, slot_send, slot_recv):
            dev = (x ^ fx, y ^ fy, c ^ fc)
            big = pltpu.make_async_remote_copy(
                src_ref=g_ref.at[slot_send], dst_ref=rb_ref.at[slot_recv], send_sem=send_sems.at[k],
                recv_sem=recv_sems.at[k], device_id=dev, device_id_type=MESH)
            small = pltpu.make_async_remote_copy(
                src_ref=s_ref, dst_ref=rs_ref.at[slot_recv], send_sem=send_sems.at[n + k],
                recv_sem=recv_sems.at[n + k], device_id=dev, device_id_type=MESH)
            return big, small

        started = []
        for k, (fx, fy, fc) in enumerate(_relations()):
            peer = 4 * (x ^ fx) + 2 * (y ^ fy) + (c ^ fc)
            big, small = copies(k, fx, fy, fc, peer, me)
            small.start()
            big.start()
            started += [big, small]
        for k, (fx, fy, fc) in enumerate(_relations()):
            peer = 4 * (x ^ fx) + 2 * (y ^ fy) + (c ^ fc)
            big, small = copies(k, fx, fy, fc, me, peer)
            small.wait_recv()
            big.wait_recv()
        for cp in started:
            cp.wait_send()
        for cp in loc:
            cp.wait()

    return pl.pallas_call(
        body, name=name,
        out_shape=[jax.ShapeDtypeStruct((N_DEV, R, W), gbig.dtype),
                   jax.ShapeDtypeStruct((N_DEV, 1, P_SMALL), gsmall.dtype)],
        in_specs=[ANY, ANY], out_specs=[ANY, ANY],
        scratch_shapes=[pltpu.SemaphoreType.DMA((2 * n,)), pltpu.SemaphoreType.DMA((2 * n,)),
                        pltpu.SemaphoreType.DMA((2,))],
    )(gbig, gsmall)


def _sum_parts(parts, row0, nrows, tr, *, name):
    n, _, W = parts.shape
    assert row0 % tr == 0 and nrows % tr == 0

    def body(p_ref, o_ref):
        g = p_ref[0]
        for i in range(1, n):
            g = g + p_ref[i]
        o_ref[...] = g

    return pl.pallas_call(
        body, name=name, grid=(nrows // tr,),
        in_specs=[pl.BlockSpec((n, tr, W), lambda i: (0, row0 // tr + i, 0))],
        out_specs=pl.BlockSpec((tr, W), lambda i: (i, 0)),
        out_shape=jax.ShapeDtypeStruct((nrows, W), F32),
        compiler_params=_cp(("parallel",)))(parts)


def _adamw(parts, w, m, v, tr, *, name):
    n = parts.shape[0]
    R, W = w.shape
    assert R % tr == 0

    def body(p_ref, w_ref, m_ref, v_ref, g_ref, d_ref, nm_ref, nv_ref):
        g = p_ref[0]
        for i in range(1, n):
            g = g + p_ref[i]
        mm = ADAM_B1 * m_ref[...] + (1.0 - ADAM_B1) * g
        vv = ADAM_B2 * v_ref[...] + (1.0 - ADAM_B2) * (g * g)
        m_hat = mm / (1.0 - ADAM_B1 ** ADAM_STEP)
        v_hat = vv / (1.0 - ADAM_B2 ** ADAM_STEP)
        g_ref[...] = g
        d_ref[...] = -ADAM_LR * (m_hat / (jnp.sqrt(v_hat) + ADAM_EPS) + ADAM_WD * w_ref[...])
        nm_ref[...] = mm
        nv_ref[...] = vv

    blk = pl.BlockSpec((tr, W), lambda i: (i, 0))
    return pl.pallas_call(
        body, name=name, grid=(R // tr,),
        in_specs=[pl.BlockSpec((n, tr, W), lambda i: (0, i, 0)), blk, blk, blk],
        out_specs=[blk] * 4, out_shape=[jax.ShapeDtypeStruct((R, W), F32)] * 4,
        compiler_params=_cp(("parallel",)))(parts, w, m, v)


def _pack_rest(w_kv, wa, wb, wm, w_out):
    return jnp.concatenate([w_kv[0], w_out[0]] + [t[0].reshape(-1, D_MODEL) for t in (wa, wb, wm)], axis=0)


def _unpack_rest(t):
    br = lambda i: t[RO_BR + 64 * i:RO_BR + 64 * (i + 1)].reshape(1, A_WIDTH, D_MODEL // N_DEV)
    return t[None, RO_KV:RO_OUT], br(0), br(1), br(2), t[None, RO_OUT:RO_BR]


def _orig_rows(gathered, a, b):
    res = []
    while a < b:
        dev, r = divmod(a, CS)
        n = min(b - a, CS - r)
        res.append(gathered[dev, RO_IN + r:RO_IN + r + n])
        a += n
    return res


def _full_weights(gathered):
    wt = {}
    for name, ranges in SEGS.items():
        rows = [p for a, b in ranges for p in _orig_rows(gathered, a, b)]
        if SEG_PAD[name]:
            rows.append(jnp.zeros((SEG_PAD[name], D_MODEL), gathered.dtype))
        wt[name] = jnp.concatenate(rows, axis=0)
    w_kv = gathered[:, RO_KV:RO_OUT].reshape(D_MODEL, D_MODEL)
    w_out = gathered[:, RO_OUT:RO_BR].reshape(D_MODEL, D_MODEL)
    wbs = [gathered[:, RO_BR + 64 * i:RO_BR + 64 * (i + 1)].reshape(N_DEV, A_WIDTH, D_MODEL // N_DEV)
           .transpose(1, 0, 2).reshape(A_WIDTH, D_MODEL) for i in range(3)]
    return wt, w_kv, wbs, w_out


def _orig_order(dwt):
    pieces = []
    for name, ranges in SEGS.items():
        o = 0
        for a, b in ranges:
            pieces.append((a, dwt[name][o:o + b - a]))
            o += b - a
    pieces.sort(key=lambda p: p[0])
    return jnp.concatenate([p[1] for p in pieces], axis=0)


def _pack_grads(dwt, dw_kv, dwbs, dw_out):
    g_in = jnp.pad(_orig_order(dwt).reshape(N_DEV, CS, D_MODEL), ((0, 0), (0, IN_ROWS - CS), (0, 0)))
    br = [t.reshape(A_WIDTH, N_DEV, D_MODEL // N_DEV).transpose(1, 0, 2).reshape(N_DEV, -1, D_MODEL) for t in dwbs]
    return jnp.concatenate([dw_kv.reshape(N_DEV, -1, D_MODEL), dw_out.reshape(N_DEV, -1, D_MODEL)] + br + [g_in],
                           axis=1)


def kernel(x, mem, positions, norm_pre_g, norm_post_g, norm_mem_g, w_in, b_forget, b_merge, w_mem_kv, w_branch_a, w_branch_b, w_branch_m, w_out, loss_target, m_norm_pre_g, m_norm_post_g, m_norm_mem_g, m_w_in, m_b_forget, m_b_merge, m_w_mem_kv, m_w_branch_a, m_w_branch_b, m_w_branch_m, m_w_out, v_norm_pre_g, v_norm_post_g, v_norm_mem_g, v_w_in, v_b_forget, v_b_merge, v_w_mem_kv, v_w_branch_a, v_w_branch_b, v_w_branch_m, v_w_out):
    w_rest = _pack_rest(w_mem_kv, w_branch_a, w_branch_b, w_branch_m, w_out)
    shard = jnp.concatenate([w_rest.astype(BF16), w_in[0].T.astype(BF16),
                             jnp.zeros((IN_ROWS - CS, D_MODEL), BF16)], axis=0)
    gathered = _all_gather(shard, name="gather_weights")
    wt, w_kv, wbs, w_o = _full_weights(gathered)

    bf_pad = jnp.pad(b_forget, ((0, 0), (0, FB_PAD - B_HEADS)))
    r = _local_step(x[0], mem[0], positions[0], loss_target[0], norm_pre_g, norm_post_g, norm_mem_g,
                    wt, bf_pad, b_merge, w_kv, wbs, w_o)

    gbig = _pack_grads(r["dwt"], r["dw_kv"], r["dwbs"], r["dw_out"])
    gsmall = jnp.concatenate([r["dg_pre"], r["dg_post"], r["dg_mem"], r["db_merge"],
                              r["db_forget"][:, :LANES], r["loss"]], axis=1)
    rbig, rsmall = _exchange_grads(gbig, gsmall, name="exchange_grads")

    m_rest = _pack_rest(m_w_mem_kv, m_w_branch_a, m_w_branch_b, m_w_branch_m, m_w_out)
    v_rest = _pack_rest(v_w_mem_kv, v_w_branch_a, v_w_branch_b, v_w_branch_m, v_w_out)
    outs_rest = [_unpack_rest(t) for t in _adamw(rbig, w_rest, m_rest, v_rest, 64, name="adamw_rest")]
    g_in = _sum_parts(rbig, RO_IN, IN_ROWS, 16, name="sum_w_in")[:CS].T
    outs_in = _adamw(g_in[None], w_in[0], m_w_in[0], v_w_in[0], 128, name="adamw_w_in")

    def small_vec(a, b, c, d, e):
        z = jnp.zeros((1, LANES - B_HEADS), F32)
        return jnp.concatenate([a, b, c, d, e, z, jnp.zeros((1, LANES), F32)], axis=1)

    outs_small = _adamw(rsmall, small_vec(norm_pre_g, norm_post_g, norm_mem_g, b_merge, b_forget),
                        small_vec(m_norm_pre_g, m_norm_post_g, m_norm_mem_g, m_b_merge, m_b_forget),
                        small_vec(v_norm_pre_g, v_norm_post_g, v_norm_mem_g, v_b_merge, v_b_forget),
                        1, name="adamw_small")

    def small_parts(t):
        return [t[:, O_GPRE:O_GPRE + D_MODEL], t[:, O_GPOST:O_GPOST + D_MODEL], t[:, O_GMEM:O_GMEM + D_MODEL],
                t[:, O_BF:O_BF + B_HEADS], t[:, O_BM:O_BM + 3 * D_MODEL]]

    loss = outs_small[0][0, O_LOSS]
    result = [loss, r["grad_x"][None]]
    for rest, w_i, small in zip(outs_rest, outs_in, outs_small):
        gp, gq, gm, bf, bm = small_parts(small)
        w_k, w_a, w_b, w_m, w_ot = rest
        result += [gp, gq, gm, w_i[None], bf, bm, w_k, w_a, w_b, w_m, w_ot]
    return tuple(result)
```

```python
import jax
import jax.numpy as jnp
from jax import lax
from jax.experimental import pallas as pl
from jax.experimental.pallas import tpu as pltpu

F32 = jnp.float32
BF16 = jnp.bfloat16

N_DEV = 8
D_MODEL = 1024
N_MEM = 256
EPS = 1e-6
NEG = -1e30
ROPE_THETA = 500000.0
DIL = (1, 4, 16)
A_HEADS = 4
HEAD = 128
A_WIDTH = 512
B_HEADS = 8
B_HEAD = 64
M_HEADS = 4
ROT = 32
IN_COLS = 11272
FB_PAD = 256

SEGS = {
    "A0": ((0, 512), (1536, 2048), (3072, 3584)),
    "A1": ((512, 1024), (2048, 2560), (3584, 4096)),
    "A2": ((1024, 1536), (2560, 3072), (4096, 4608)),
    "B": ((5120, 6656),),
    "R": ((4608, 5120), (6664, 7176), (7176, 7688), (7688, 8200), (8200, 11272), (6656, 6664)),
}
SEG_PAD = {"A0": 0, "A1": 0, "A2": 0, "B": 0, "R": FB_PAD - B_HEADS}
R_ZA, R_ZB, R_QM, R_ZM, R_GL, R_FB = 0, 512, 1024, 1536, 2048, 5120
NR = R_FB + FB_PAD

ADAM_LR, ADAM_B1, ADAM_B2, ADAM_EPS, ADAM_WD, ADAM_STEP = 0.001, 0.9, 0.999, 1e-08, 0.01, 10

LANES = 128
VMEM_LIMIT = 56 * 1024 * 1024

CS = IN_COLS // N_DEV
RO_KV, RO_OUT, RO_BR, RO_IN = 0, 128, 256, 448
IN_ROWS = 1424
ROWS = RO_IN + IN_ROWS
O_GPRE, O_GPOST, O_GMEM, O_BM, O_BF, O_LOSS = 0, 1024, 2048, 3072, 6144, 6272
P_SMALL = 6400


def _cp(sem=None):
    return pltpu.CompilerParams(dimension_semantics=sem, vmem_limit_bytes=VMEM_LIMIT)


def _dot(a, b):
    return jnp.dot(a, b, preferred_element_type=F32)


def _dot_nt(a, b):
    return lax.dot_general(a, b, (((1,), (1,)), ((), ())), preferred_element_type=F32)


def _sigmoid(z):
    return 1.0 / (1.0 + jnp.exp(-z))


def _mm(a, b, *, name, bt=False, out_dtype=F32, tm=1024, tn=1024, tk=None):
    M, K = a.shape
    N = b.shape[0] if bt else b.shape[1]
    tm, tn = min(tm, M), min(tn, N)
    tk = K if tk is None else min(tk, K)
    assert M % tm == 0 and N % tn == 0 and K % tk == 0
    nk = K // tk

    def body(a_ref, b_ref, o_ref, acc_ref):
        av = a_ref[...].astype(BF16)
        bv = b_ref[...].astype(BF16)
        p = _dot_nt(av, bv) if bt else _dot(av, bv)
        if nk == 1:
            o_ref[...] = p.astype(out_dtype)
        else:
            k = pl.program_id(2)

            @pl.when(k == 0)
            def _():
                acc_ref[...] = p

            @pl.when(k > 0)
            def _():
                acc_ref[...] += p

            @pl.when(k == nk - 1)
            def _():
                o_ref[...] = acc_ref[...].astype(out_dtype)

    b_spec = (pl.BlockSpec((tn, tk), lambda i, j, k: (j, k)) if bt
              else pl.BlockSpec((tk, tn), lambda i, j, k: (k, j)))
    return pl.pallas_call(
        body, name=name, grid=(M // tm, N // tn, nk),
        in_specs=[pl.BlockSpec((tm, tk), lambda i, j, k: (i, k)), b_spec],
        out_specs=pl.BlockSpec((tm, tn), lambda i, j, k: (i, j)),
        out_shape=jax.ShapeDtypeStruct((M, N), out_dtype),
        scratch_shapes=[pltpu.VMEM((tm, tn) if nk > 1 else (8, LANES), F32)],
        compiler_params=_cp(("parallel", "parallel", "arbitrary")),
    )(a, b)


def _rms_fwd(x, g, *, name):
    S, D = x.shape
    tm = min(512, S)

    def body(x_ref, g_ref, o_ref):
        xv = x_ref[...]
        r = lax.rsqrt(jnp.mean(xv * xv, axis=-1, keepdims=True) + EPS)
        o_ref[...] = (xv * r * g_ref[...]).astype(BF16)

    return pl.pallas_call(
        body, name=name, grid=(S // tm,),
        in_specs=[pl.BlockSpec((tm, D), lambda i: (i, 0)), pl.BlockSpec((1, D), lambda i: (0, 0))],
        out_specs=pl.BlockSpec((tm, D), lambda i: (i, 0)),
        out_shape=jax.ShapeDtypeStruct((S, D), BF16),
        compiler_params=_cp(("parallel",)),
    )(x, g)


def _rms_bwd(x, g, dh, dy, *, name):
    S, D = x.shape
    tm = min(512, S)
    want_dx = dy is not None

    def body(*refs):
        if want_dx:
            x_ref, g_ref, dh_ref, dy_ref, dx_ref, dg_ref = refs
        else:
            x_ref, g_ref, dh_ref, dg_ref = refs
        i = pl.program_id(0)
        xv = x_ref[...]
        r = lax.rsqrt(jnp.mean(xv * xv, axis=-1, keepdims=True) + EPS)
        xh = xv * r
        dhv = dh_ref[...]
        part = jnp.sum(dhv * xh, axis=0, keepdims=True)

        @pl.when(i == 0)
        def _():
            dg_ref[...] = part

        @pl.when(i > 0)
        def _():
            dg_ref[...] += part

        if want_dx:
            dxh = dhv * g_ref[...]
            dx_ref[...] = dy_ref[...] + r * (dxh - xh * jnp.mean(dxh * xh, axis=-1, keepdims=True))

    row = pl.BlockSpec((tm, D), lambda i: (i, 0))
    vec = pl.BlockSpec((1, D), lambda i: (0, 0))
    if want_dx:
        return pl.pallas_call(
            body, name=name, grid=(S // tm,), in_specs=[row, vec, row, row], out_specs=[row, vec],
            out_shape=[jax.ShapeDtypeStruct((S, D), F32), jax.ShapeDtypeStruct((1, D), F32)],
            compiler_params=_cp(("arbitrary",)))(x, g, dh, dy)
    return pl.pallas_call(
        body, name=name, grid=(S // tm,), in_specs=[row, vec, row], out_specs=vec,
        out_shape=jax.ShapeDtypeStruct((1, D), F32),
        compiler_params=_cp(("arbitrary",)))(x, g, dh)


def _post(x, out, tgt, g, *, name):
    S, D = x.shape
    tm = min(512, S)

    def body(x_ref, o_ref, t_ref, g_ref, dy_ref, do_ref, dg_ref, loss_ref):
        i = pl.program_id(0)
        ov = o_ref[...]
        r = lax.rsqrt(jnp.mean(ov * ov, axis=-1, keepdims=True) + EPS)
        n = ov * r
        gv = g_ref[...]
        e = (x_ref[...] + n * gv) - t_ref[...]
        lpart = 0.5 * jnp.sum(jnp.mean(e * e, axis=-1, keepdims=True), axis=0, keepdims=True)
        dy = e * (1.0 / D)
        dy_ref[...] = dy
        dn = dy * gv
        do_ref[...] = (r * (dn - n * jnp.mean(dn * n, axis=-1, keepdims=True))).astype(BF16)
        gpart = jnp.sum(dy * n, axis=0, keepdims=True)
        lrow = jnp.broadcast_to(lpart, (1, LANES))

        @pl.when(i == 0)
        def _():
            dg_ref[...] = gpart
            loss_ref[...] = lrow

        @pl.when(i > 0)
        def _():
            dg_ref[...] += gpart
            loss_ref[...] += lrow

    row = pl.BlockSpec((tm, D), lambda i: (i, 0))
    vec = pl.BlockSpec((1, D), lambda i: (0, 0))
    return pl.pallas_call(
        body, name=name, grid=(S // tm,), in_specs=[row, row, row, vec],
        out_specs=[row, row, vec, pl.BlockSpec((1, LANES), lambda i: (0, 0))],
        out_shape=[jax.ShapeDtypeStruct((S, D), F32), jax.ShapeDtypeStruct((S, D), BF16),
                   jax.ShapeDtypeStruct((1, D), F32), jax.ShapeDtypeStruct((1, LANES), F32)],
        compiler_params=_cp(("arbitrary",)))(x, out, tgt, g)


def _to_classes(t, d):
    if d == 1:
        return t
    S, C = t.shape
    return t.reshape(S // d, d, C).transpose(1, 0, 2).reshape(S, C)


def _from_classes(t, d):
    if d == 1:
        return t
    S, C = t.shape
    return t.reshape(d, S // d, C).transpose(1, 0, 2).reshape(S, C)


def _rope(x, c, s1, s2):
    return x * c + pltpu.roll(x, LANES - ROT // 2, 1) * s1 + pltpu.roll(x, ROT // 2, 1) * s2


def _unrope(d, c, s1, s2):
    return d * c + pltpu.roll(d * s1, ROT // 2, 1) + pltpu.roll(d * s2, LANES - ROT // 2, 1)


def _a_masks():
    qi = lax.broadcasted_iota(jnp.int32, (HEAD, HEAD), 0)
    ki = lax.broadcasted_iota(jnp.int32, (HEAD, HEAD), 1)
    return ki >= qi, ki <= qi


A_SCALE = HEAD ** -0.5


def _a_geometry(S, g):
    d = DIL[g]
    L = S // d
    TQ = min(512, L)
    return d, L, TQ, TQ // HEAD, L // TQ, L // HEAD


def _attn_a_fwd(ua, tabs, g, *, name):
    S = ua.shape[0]
    d, L, TQ, nsub, nb, nblk = _a_geometry(S, g)

    def body(q_ref, kc_ref, kp_ref, vc_ref, vp_ref, c_ref, s1_ref, s2_ref, cp_ref, s1p_ref, s2p_ref,
             o_ref, l_ref):
        n = pl.program_id(1)
        tc = (c_ref[...], s1_ref[...], s2_ref[...])
        q = _rope(q_ref[...], *tc).astype(BF16)
        kc = _rope(kc_ref[...], *tc).astype(BF16)
        kp = _rope(kp_ref[...], cp_ref[...], s1p_ref[...], s2p_ref[...]).astype(BF16)
        vc = vc_ref[...].astype(BF16)
        vp = vp_ref[...].astype(BF16)
        mprev, mcur = _a_masks()
        for a in range(nsub):
            sl = slice(a * HEAD, (a + 1) * HEAD)
            pv = slice((a - 1) * HEAD, a * HEAD)
            qa = q[sl]
            k_prev, v_prev = (kp, vp) if a == 0 else (kc[pv], vc[pv])
            mp = jnp.logical_and(mprev, n > 0) if a == 0 else mprev
            s_p = jnp.where(mp, _dot_nt(qa, k_prev) * A_SCALE, NEG)
            s_c = jnp.where(mcur, _dot_nt(qa, kc[sl]) * A_SCALE, NEG)
            m = jnp.maximum(jnp.max(s_p, axis=-1, keepdims=True), jnp.max(s_c, axis=-1, keepdims=True))
            p_p = jnp.exp(s_p - m)
            p_c = jnp.exp(s_c - m)
            den = jnp.sum(p_p, axis=-1, keepdims=True) + jnp.sum(p_c, axis=-1, keepdims=True)
            o = (_dot(p_p.astype(BF16), v_prev) + _dot(p_c.astype(BF16), vc[sl])) / den
            o_ref[sl, :] = o
            l_ref[sl, :] = jnp.broadcast_to(m + jnp.log(den), (HEAD, HEAD))

    rcur = lambda cb, n: (cb // A_HEADS) * nb + n
    rprv = lambda cb, n: (cb // A_HEADS) * nblk + jnp.maximum(n * nsub - 1, 0)
    cur = lambda off: pl.BlockSpec((TQ, HEAD), lambda cb, n: (rcur(cb, n), off + cb % A_HEADS))
    prv = lambda off: pl.BlockSpec((HEAD, HEAD), lambda cb, n: (rprv(cb, n), off + cb % A_HEADS))
    tcur = pl.BlockSpec((TQ, LANES), lambda cb, n: (rcur(cb, n), 0))
    tprv = pl.BlockSpec((HEAD, LANES), lambda cb, n: (rprv(cb, n), 0))
    out = pl.BlockSpec((TQ, HEAD), lambda cb, n: (rcur(cb, n), cb % A_HEADS))
    return pl.pallas_call(
        body, name=name, grid=(A_HEADS * d, nb),
        in_specs=[cur(0), cur(4), prv(4), cur(8), prv(8), tcur, tcur, tcur, tprv, tprv, tprv],
        out_specs=[out, out],
        out_shape=[jax.ShapeDtypeStruct((S, A_WIDTH), F32)] * 2,
        compiler_params=_cp(("parallel", "parallel")),
    )(ua, ua, ua, ua, ua, *tabs, *tabs)


def _attn_a_dq(ua, tabs, g, do, lse, adj, *, name):
    S = ua.shape[0]
    d, L, TQ, nsub, nb, nblk = _a_geometry(S, g)

    def body(q_ref, kc_ref, kp_ref, vc_ref, vp_ref, do_ref, l_ref, adj_ref,
             c_ref, s1_ref, s2_ref, cp_ref, s1p_ref, s2p_ref, dq_ref):
        n = pl.program_id(1)
        tc = (c_ref[...], s1_ref[...], s2_ref[...])
        q = _rope(q_ref[...], *tc).astype(BF16)
        kc = _rope(kc_ref[...], *tc).astype(BF16)
        kp = _rope(kp_ref[...], cp_ref[...], s1p_ref[...], s2p_ref[...]).astype(BF16)
        vc = vc_ref[...].astype(BF16)
        vp = vp_ref[...].astype(BF16)
        mprev, mcur = _a_masks()
        for a in range(nsub):
            sl = slice(a * HEAD, (a + 1) * HEAD)
            pv = slice((a - 1) * HEAD, a * HEAD)
            qa = q[sl]
            k_prev, v_prev = (kp, vp) if a == 0 else (kc[pv], vc[pv])
            mp = jnp.logical_and(mprev, n > 0) if a == 0 else mprev
            lse_a = l_ref[sl, :][:, :1]
            adj_a = adj_ref[sl, :][:, :1]
            doa = do_ref[sl, :]
            p_p = jnp.exp(jnp.where(mp, _dot_nt(qa, k_prev) * A_SCALE, NEG) - lse_a)
            p_c = jnp.exp(jnp.where(mcur, _dot_nt(qa, kc[sl]) * A_SCALE, NEG) - lse_a)
            ds_p = p_p * (_dot_nt(doa, v_prev) + adj_a)
            ds_c = p_c * (_dot_nt(doa, vc[sl]) + adj_a)
            dq = (_dot(ds_p.astype(BF16), k_prev) + _dot(ds_c.astype(BF16), kc[sl])) * A_SCALE
            dq_ref[sl, :] = _unrope(dq, c_ref[sl, :], s1_ref[sl, :], s2_ref[sl, :]).astype(BF16)

    rcur = lambda cb, n: (cb // A_HEADS) * nb + n
    rprv = lambda cb, n: (cb // A_HEADS) * nblk + jnp.maximum(n * nsub - 1, 0)
    cur = lambda off: pl.BlockSpec((TQ, HEAD), lambda cb, n: (rcur(cb, n), off + cb % A_HEADS))
    prv = lambda off: pl.BlockSpec((HEAD, HEAD), lambda cb, n: (rprv(cb, n), off + cb % A_HEADS))
    tcur = pl.BlockSpec((TQ, LANES), lambda cb, n: (rcur(cb, n), 0))
    tprv = pl.BlockSpec((HEAD, LANES), lambda cb, n: (rprv(cb, n), 0))
    blk = pl.BlockSpec((TQ, HEAD), lambda cb, n: (rcur(cb, n), cb % A_HEADS))
    return pl.pallas_call(
        body, name=name, grid=(A_HEADS * d, nb),
        in_specs=[cur(0), cur(4), prv(4), cur(8), prv(8), blk, blk, blk,
                  tcur, tcur, tcur, tprv, tprv, tprv],
        out_specs=blk,
        out_shape=jax.ShapeDtypeStruct((S, A_WIDTH), BF16),
        compiler_params=_cp(("parallel", "parallel")),
    )(ua, ua, ua, ua, ua, do, lse, adj, *tabs, *tabs)


def _attn_a_dkv(ua, tabs, g, do, lse, adj, *, name):
    S = ua.shape[0]
    d, L, TQ, nsub, nb, nblk = _a_geometry(S, g)

    def body(qc_ref, qn_ref, kc_ref, vc_ref, doc_ref, don_ref, lc_ref, ln_ref, ac_ref, an_ref,
             c_ref, s1_ref, s2_ref, cn_ref, s1n_ref, s2n_ref, dk_ref, dv_ref):
        n = pl.program_id(1)
        tc = (c_ref[...], s1_ref[...], s2_ref[...])
        qc = _rope(qc_ref[...], *tc).astype(BF16)
        qn = _rope(qn_ref[...], cn_ref[...], s1n_ref[...], s2n_ref[...]).astype(BF16)
        kc = _rope(kc_ref[...], *tc).astype(BF16)
        vc = vc_ref[...].astype(BF16)
        kr = lax.broadcasted_iota(jnp.int32, (HEAD, HEAD), 0)
        qc_i = lax.broadcasted_iota(jnp.int32, (HEAD, HEAD), 1)
        own_t = kr <= qc_i
        nxt_t = kr >= qc_i
        has_next = n < nb - 1
        for b in range(nsub):
            sl = slice(b * HEAD, (b + 1) * HEAD)
            nx = slice((b + 1) * HEAD, (b + 2) * HEAD)
            kb, vb = kc[sl], vc[sl]
            last = b == nsub - 1
            parts = [(qc[sl], doc_ref[sl, :], lc_ref[sl, :], ac_ref[sl, :], own_t)]
            if last:
                parts.append((qn, don_ref[...], ln_ref[...], an_ref[...], jnp.logical_and(nxt_t, has_next)))
            else:
                parts.append((qc[nx], doc_ref[nx, :], lc_ref[nx, :], ac_ref[nx, :], nxt_t))
            dk = jnp.zeros((HEAD, HEAD), F32)
            dv = jnp.zeros((HEAD, HEAD), F32)
            for qq, dd, ll, aa, msk in parts:
                st = jnp.where(msk, _dot_nt(kb, qq) * A_SCALE, NEG)
                pt = jnp.exp(st - ll.T)
                dv = dv + _dot(pt.astype(BF16), dd)
                dst = pt * (_dot_nt(vb, dd) + aa.T)
                dk = dk + _dot(dst.astype(BF16), qq)
            dk = dk * A_SCALE
            dk_ref[sl, :] = _unrope(dk, c_ref[sl, :], s1_ref[sl, :], s2_ref[sl, :]).astype(BF16)
            dv_ref[sl, :] = dv.astype(BF16)

    rcur = lambda cb, n: (cb // A_HEADS) * nb + n
    rnxt = lambda cb, n: (cb // A_HEADS) * nblk + jnp.minimum((n + 1) * nsub, nblk - 1)
    cur = lambda off: pl.BlockSpec((TQ, HEAD), lambda cb, n: (rcur(cb, n), off + cb % A_HEADS))
    nxu = lambda off: pl.BlockSpec((HEAD, HEAD), lambda cb, n: (rnxt(cb, n), off + cb % A_HEADS))
    tcur = pl.BlockSpec((TQ, LANES), lambda cb, n: (rcur(cb, n), 0))
    tnxt = pl.BlockSpec((HEAD, LANES), lambda cb, n: (rnxt(cb, n), 0))
    blk = pl.BlockSpec((TQ, HEAD), lambda cb, n: (rcur(cb, n), cb % A_HEADS))
    bnx = pl.BlockSpec((HEAD, HEAD), lambda cb, n: (rnxt(cb, n), cb % A_HEADS))
    return pl.pallas_call(
        body, name=name, grid=(A_HEADS * d, nb),
        in_specs=[cur(0), nxu(0), cur(4), cur(8), blk, bnx, blk, bnx, blk, bnx,
                  tcur, tcur, tcur, tnxt, tnxt, tnxt],
        out_specs=[blk, blk],
        out_shape=[jax.ShapeDtypeStruct((S, A_WIDTH), BF16)] * 2,
        compiler_params=_cp(("parallel", "parallel")),
    )(ua, ua, ua, ua, do, do, lse, lse, adj, adj, *tabs, *tabs)


def _silu_parts(z):
    sg = _sigmoid(z)
    return z * sg, sg * (1.0 + z * (1.0 - sg))


def _merge_a_fwd(os_, ls_, ur, *, name):
    S = ur.shape[0]
    tm = min(512, S)

    def body(o0, o1, o2, l0, l1, l2, z_ref, y_ref):
        ls = [l0[...], l1[...], l2[...]]
        mx = jnp.maximum(jnp.maximum(ls[0], ls[1]), ls[2])
        es = [jnp.exp(l - mx) for l in ls]
        den = es[0] + es[1] + es[2]
        y = (es[0] / den) * o0[...] + (es[1] / den) * o1[...] + (es[2] / den) * o2[...]
        y_ref[...] = (y * _silu_parts(z_ref[...])[0]).astype(BF16)

    blk = pl.BlockSpec((tm, A_WIDTH), lambda i: (i, 0))
    return pl.pallas_call(
        body, name=name, grid=(S // tm,),
        in_specs=[blk] * 6 + [pl.BlockSpec((tm, A_WIDTH), lambda i: (i, R_ZA // A_WIDTH))],
        out_specs=blk, out_shape=jax.ShapeDtypeStruct((S, A_WIDTH), BF16),
        compiler_params=_cp(("parallel",)))(*os_, *ls_, ur)


def _merge_a_bwd(os_, ls_, ur, dya, *, name):
    S = ur.shape[0]
    tm = min(256, S)

    def body(o0, o1, o2, l0, l1, l2, z_ref, dy_ref, d0, d1, d2, a0, a1, a2, dz_ref):
        ls = [l0[...], l1[...], l2[...]]
        ov = [o0[...], o1[...], o2[...]]
        mx = jnp.maximum(jnp.maximum(ls[0], ls[1]), ls[2])
        es = [jnp.exp(l - mx) for l in ls]
        den = es[0] + es[1] + es[2]
        ws = [e / den for e in es]
        y = ws[0] * ov[0] + ws[1] * ov[1] + ws[2] * ov[2]
        sz, dsz = _silu_parts(z_ref[...])
        dyv = dy_ref[...]
        dz_ref[...] = (dyv * y * dsz).astype(BF16)
        dyp = dyv * sz
        for h in range(A_HEADS):
            sl = slice(h * HEAD, (h + 1) * HEAD)
            t = jnp.zeros((tm, 1), F32)
            for gi in range(3):
                t = t + ws[gi][:, sl][:, :1] * jnp.sum(dyp[:, sl] * ov[gi][:, sl], axis=-1, keepdims=True)
            for gi, (dref, aref) in enumerate(((d0, a0), (d1, a1), (d2, a2))):
                wg = ws[gi][:, sl]
                dref[:, sl] = (wg * dyp[:, sl]).astype(BF16)
                aref[:, sl] = -wg * t

    blk = pl.BlockSpec((tm, A_WIDTH), lambda i: (i, 0))
    outs = pl.pallas_call(
        body, name=name, grid=(S // tm,),
        in_specs=[blk] * 6 + [pl.BlockSpec((tm, A_WIDTH), lambda i: (i, R_ZA // A_WIDTH)), blk],
        out_specs=[blk] * 7,
        out_shape=[jax.ShapeDtypeStruct((S, A_WIDTH), BF16)] * 3
        + [jax.ShapeDtypeStruct((S, A_WIDTH), F32)] * 3 + [jax.ShapeDtypeStruct((S, A_WIDTH), BF16)],
        compiler_params=_cp(("parallel",)))(*os_, *ls_, ur, dya)
    return outs[0:3], outs[3:6], outs[6]


def _logf(ur, bf_pad, *, name):
    S = ur.shape[0]
    tm = min(1024, S)

    def body(u_ref, b_ref, o_ref):
        z = u_ref[...] + b_ref[...]
        o_ref[...] = jnp.minimum(z, 0.0) - jnp.log(1.0 + jnp.exp(-jnp.abs(z)))

    return pl.pallas_call(
        body, name=name, grid=(S // tm,),
        in_specs=[pl.BlockSpec((tm, FB_PAD), lambda i: (i, R_FB // FB_PAD)),
                  pl.BlockSpec((1, FB_PAD), lambda i: (0, 0))],
        out_specs=pl.BlockSpec((tm, FB_PAD), lambda i: (i, 0)),
        out_shape=jax.ShapeDtypeStruct((S, FB_PAD), F32),
        compiler_params=_cp(("parallel",)))(ur, bf_pad)


def _cumsum_lanes(x, reverse, *, name):
    nt, H, _ = x.shape

    def body(x_ref, o_ref):
        lane = lax.broadcasted_iota(jnp.int32, (H, LANES), 1)

        def tile(t, carry):
            tt = nt - 1 - t if reverse else t
            v = x_ref[tt]
            k = 1
            while k < LANES:
                if reverse:
                    v = v + jnp.where(lane < LANES - k, pltpu.roll(v, LANES - k, 1), 0.0)
                else:
                    v = v + jnp.where(lane >= k, pltpu.roll(v, k, 1), 0.0)
                k *= 2
            v = v + carry
            o_ref[tt] = v
            edge = v[:, :1] if reverse else v[:, LANES - 1:]
            return jnp.broadcast_to(edge, (H, LANES))

        lax.fori_loop(0, nt, tile, jnp.zeros((H, LANES), F32))

    return pl.pallas_call(
        body, name=name, out_shape=jax.ShapeDtypeStruct((nt, H, LANES), F32),
        in_specs=[pl.BlockSpec(memory_space=pltpu.VMEM)], out_specs=pl.BlockSpec(memory_space=pltpu.VMEM),
        compiler_params=_cp())(x)


B_SCALE = B_HEAD ** -0.5


def _pair_masks():
    lane = lax.broadcasted_iota(jnp.int32, (1, LANES), 1)
    row = lax.broadcasted_iota(jnp.int32, (LANES, 1), 0)
    return (lane < B_HEAD, lane >= B_HEAD), (row < B_HEAD, row >= B_HEAD)


def _causal_t(T):
    r = lax.broadcasted_iota(jnp.int32, (T, T), 0)
    c = lax.broadcasted_iota(jnp.int32, (T, T), 1)
    return r <= c


def _zero_other(x, keep):
    return jnp.where(keep, x, jnp.zeros_like(x))


def _fox_fwd(ub, vt, crow, ckb, *, name):
    S = ub.shape[0]
    T = min(512, S)
    nq = S // T

    def body(q_ref, k_ref, vt_ref, cr_ref, ck_ref, o_ref, l_ref, m_s, l_s, acc_s):
        i = pl.program_id(1)
        lanes, rows = _pair_masks()
        q = q_ref[...] * B_SCALE
        qm = [_zero_other(q, lanes[0]), _zero_other(q, lanes[1])]
        m_s[...] = jnp.full((2, 1, T), NEG, F32)
        l_s[...] = jnp.zeros((2, 1, T), F32)
        acc_s[...] = jnp.zeros((LANES, T), F32)

        def step(j, masked):
            off = pl.multiple_of(j * T, T)
            kj = k_ref[pl.ds(off, T), :]
            vtj = vt_ref[j]
            upd = jnp.zeros((LANES, T), F32)
            alphas = []
            for a in range(2):
                st = _dot_nt(kj, qm[a]) + (cr_ref[a, i] - jnp.tile(ck_ref[a, pl.ds(off, T), :], (1, T // LANES)))
                if masked:
                    st = jnp.where(_causal_t(T), st, NEG)
                m_old = m_s[a]
                m_new = jnp.maximum(m_old, jnp.max(st, axis=0, keepdims=True))
                alpha = jnp.exp(m_old - m_new)
                pt = jnp.exp(st - m_new)
                l_s[a] = alpha * l_s[a] + jnp.sum(pt, axis=0, keepdims=True)
                m_s[a] = m_new
                upd = upd + _dot(_zero_other(vtj, rows[a]), pt.astype(BF16))
                alphas.append(alpha)
            acc_s[...] = acc_s[...] * jnp.where(rows[0], alphas[0], alphas[1]) + upd

        def loop(j, carry):
            step(j, False)
            return carry

        lax.fori_loop(0, i, loop, 0)
        step(i, True)
        o_ref[...] = (acc_s[...] / jnp.where(rows[0], l_s[0], l_s[1])).T
        l_ref[0] = m_s[0] + jnp.log(l_s[0])
        l_ref[1] = m_s[1] + jnp.log(l_s[1])

    stat = pl.BlockSpec((2, None, 1, T), lambda h, i: (h, i, 0, 0))
    return pl.pallas_call(
        body, name=name, grid=(B_HEADS // 2, nq),
        in_specs=[pl.BlockSpec((T, LANES), lambda h, i: (i, h)),
                  pl.BlockSpec((S, LANES), lambda h, i: (0, 4 + h)),
                  pl.BlockSpec((nq, LANES, T), lambda h, i: (0, h, 0)),
                  pl.BlockSpec((2, nq, 1, T), lambda h, i: (h, 0, 0, 0)),
                  pl.BlockSpec((2, S, LANES), lambda h, i: (h, 0, 0))],
        out_specs=[pl.BlockSpec((T, LANES), lambda h, i: (i, h)), stat],
        out_shape=[jax.ShapeDtypeStruct((S, A_WIDTH), F32), jax.ShapeDtypeStruct((B_HEADS, nq, 1, T), F32)],
        scratch_shapes=[pltpu.VMEM((2, 1, T), F32), pltpu.VMEM((2, 1, T), F32), pltpu.VMEM((LANES, T), F32)],
        compiler_params=_cp(("parallel", "parallel")),
    )(ub, ub, vt, crow, ckb)


def _fox_delta(o, do, *, name):
    S = o.shape[0]
    T = min(512, S)
    nq = S // T

    def body(o_ref, do_ref, d_ref):
        _, rows = _pair_masks()
        prod_t = (do_ref[...].astype(F32) * o_ref[...]).T
        d_ref[0] = jnp.sum(_zero_other(prod_t, rows[0]), axis=0, keepdims=True)
        d_ref[1] = jnp.sum(_zero_other(prod_t, rows[1]), axis=0, keepdims=True)

    tile = pl.BlockSpec((T, LANES), lambda h, i: (i, h))
    return pl.pallas_call(
        body, name=name, grid=(B_HEADS // 2, nq), in_specs=[tile, tile],
        out_specs=pl.BlockSpec((2, None, 1, T), lambda h, i: (h, i, 0, 0)),
        out_shape=jax.ShapeDtypeStruct((B_HEADS, nq, 1, T), F32),
        compiler_params=_cp(("parallel", "parallel")))(o, do)


def _fox_bwd(ub, kt, crow, ckb, do, lse, delta, *, name):
    S = ub.shape[0]
    T = min(512, S)
    nq = S // T

    def body(k_ref, v_ref, kt_ref, q_ref, do_ref, cr_ref, ck_ref, l_ref, dl_ref,
             dk_ref, dv_ref, dck_ref, dqt_ref, dcq_ref, dk_s, dv_s, dc_s):
        j = pl.program_id(1)
        lanes, rows = _pair_masks()
        kv = k_ref[...]
        vv = v_ref[...]
        ktj = kt_ref[...]
        km = [_zero_other(kv, lanes[0]), _zero_other(kv, lanes[1])]
        ktm = [_zero_other(ktj, rows[0]), _zero_other(ktj, rows[1])]
        ck = [jnp.tile(ck_ref[a], (1, T // LANES)) for a in range(2)]
        dk_s[...] = jnp.zeros((T, LANES), F32)
        dv_s[...] = jnp.zeros((T, LANES), F32)
        dc_s[...] = jnp.zeros((2, T, 1), F32)

        @pl.when(j == 0)
        def _():
            dqt_ref[...] = jnp.zeros((nq, LANES, T), F32)
            dcq_ref[...] = jnp.zeros((2, nq, 1, T), F32)

        def step(i, masked):
            off = pl.multiple_of(i * T, T)
            qi = q_ref[pl.ds(off, T), :] * B_SCALE
            doi = do_ref[pl.ds(off, T), :]
            upd = jnp.zeros((LANES, T), F32)
            for a in range(2):
                st = _dot_nt(km[a], qi) + (cr_ref[a, i] - ck[a])
                if masked:
                    st = jnp.where(_causal_t(T), st, NEG)
                pt = jnp.exp(st - l_ref[a, i])
                doa = _zero_other(doi, lanes[a])
                dv_s[...] += _dot(pt.astype(BF16), doa)
                dst = pt * (_dot_nt(vv, doa) - dl_ref[a, i])
                dsb = dst.astype(BF16)
                dk_s[...] += _dot(dsb, _zero_other(qi, lanes[a]))
                upd = upd + _dot(ktm[a], dsb)
                dc_s[a] -= jnp.sum(dst, axis=-1, keepdims=True)
                dcq_ref[a, i] += jnp.sum(dst, axis=0, keepdims=True)
            dqt_ref[i] += upd

        def loop(i, carry):
            step(i, False)
            return carry

        step(j, True)
        lax.fori_loop(j + 1, nq, loop, 0)
        dk_ref[...] = dk_s[...].astype(BF16)
        dv_ref[...] = dv_s[...].astype(BF16)
        dck_ref[...] = dc_s[...]

    rowv = pl.BlockSpec((2, nq, 1, T), lambda h, j: (h, 0, 0, 0))
    tile = pl.BlockSpec((T, LANES), lambda h, j: (j, h))
    return pl.pallas_call(
        body, name=name, grid=(B_HEADS // 2, nq),
        in_specs=[pl.BlockSpec((T, LANES), lambda h, j: (j, 4 + h)),
                  pl.BlockSpec((T, LANES), lambda h, j: (j, 8 + h)),
                  pl.BlockSpec((None, LANES, T), lambda h, j: (j, h, 0)),
                  pl.BlockSpec((S, LANES), lambda h, j: (0, h)),
                  pl.BlockSpec((S, LANES), lambda h, j: (0, h)),
                  rowv,
                  pl.BlockSpec((2, T, LANES), lambda h, j: (h, j, 0)),
                  rowv, rowv],
        out_specs=[tile, tile, pl.BlockSpec((2, T, 1), lambda h, j: (h, j, 0)),
                   pl.BlockSpec((nq, LANES, T), lambda h, j: (0, h, 0)), rowv],
        out_shape=[jax.ShapeDtypeStruct((S, A_WIDTH), BF16)] * 2 + [jax.ShapeDtypeStruct((B_HEADS, S, 1), F32),
                   jax.ShapeDtypeStruct((nq, A_WIDTH, T), F32), jax.ShapeDtypeStruct((B_HEADS, nq, 1, T), F32)],
        scratch_shapes=[pltpu.VMEM((T, LANES), F32), pltpu.VMEM((T, LANES), F32), pltpu.VMEM((2, T, 1), F32)],
        compiler_params=_cp(("parallel", "arbitrary")),
    )(ub, ub, kt, ub, do, crow, ckb, lse, delta)


def _gate_fwd(o, ur, zcol, *, name):
    S = ur.shape[0]
    tm = min(1024, S)

    def body(o_ref, z_ref, y_ref):
        y_ref[...] = (o_ref[...] * _silu_parts(z_ref[...])[0]).astype(BF16)

    blk = pl.BlockSpec((tm, A_WIDTH), lambda i: (i, 0))
    return pl.pallas_call(
        body, name=name, grid=(S // tm,),
        in_specs=[blk, pl.BlockSpec((tm, A_WIDTH), lambda i: (i, zcol // A_WIDTH))],
        out_specs=blk, out_shape=jax.ShapeDtypeStruct((S, A_WIDTH), BF16),
        compiler_params=_cp(("parallel",)))(o, ur)


def _gate_bwd(o, ur, zcol, dy, *, name):
    S = ur.shape[0]
    tm = min(1024, S)

    def body(o_ref, z_ref, dy_ref, do_ref, dz_ref):
        sz, dsz = _silu_parts(z_ref[...])
        dyv = dy_ref[...]
        do_ref[...] = (dyv * sz).astype(BF16)
        dz_ref[...] = (dyv * o_ref[...] * dsz).astype(BF16)

    blk = pl.BlockSpec((tm, A_WIDTH), lambda i: (i, 0))
    return pl.pallas_call(
        body, name=name, grid=(S // tm,),
        in_specs=[blk, pl.BlockSpec((tm, A_WIDTH), lambda i: (i, zcol // A_WIDTH)), blk],
        out_specs=[blk, blk], out_shape=[jax.ShapeDtypeStruct((S, A_WIDTH), BF16)] * 2,
        compiler_params=_cp(("parallel",)))(o, ur, dy)


def _dfb(ur, bf_pad, dlogf_pad, *, name):
    S = ur.shape[0]
    tm = min(1024, S)

    def body(u_ref, b_ref, d_ref, o_ref, s_ref):
        i = pl.program_id(0)
        dv = d_ref[...] * _sigmoid(-(u_ref[...] + b_ref[...]))
        o_ref[...] = dv.astype(BF16)
        part = jnp.sum(dv, axis=0, keepdims=True)

        @pl.when(i == 0)
        def _():
            s_ref[...] = part

        @pl.when(i > 0)
        def _():
            s_ref[...] += part

    vec = pl.BlockSpec((1, FB_PAD), lambda i: (0, 0))
    blk = pl.BlockSpec((tm, FB_PAD), lambda i: (i, 0))
    return pl.pallas_call(
        body, name=name, grid=(S // tm,),
        in_specs=[pl.BlockSpec((tm, FB_PAD), lambda i: (i, R_FB // FB_PAD)), vec, blk],
        out_specs=[blk, vec],
        out_shape=[jax.ShapeDtypeStruct((S, FB_PAD), BF16), jax.ShapeDtypeStruct((1, FB_PAD), F32)],
        compiler_params=_cp(("arbitrary",)))(ur, bf_pad, dlogf_pad)


M_SCALE = HEAD ** -0.5


def _mem_fwd(ur, mkv, *, name):
    S = ur.shape[0]
    T = min(512, S)

    def body(q_ref, z_ref, k_ref, v_ref, y_ref):
        s = _dot_nt(q_ref[...].astype(BF16), k_ref[...].astype(BF16)) * M_SCALE
        p = jnp.exp(s - jnp.max(s, axis=-1, keepdims=True))
        p = p / jnp.sum(p, axis=-1, keepdims=True)
        o = _dot(p.astype(BF16), v_ref[...].astype(BF16))
        y_ref[...] = (o * _silu_parts(z_ref[...])[0]).astype(BF16)

    return pl.pallas_call(
        body, name=name, grid=(S // T, M_HEADS),
        in_specs=[pl.BlockSpec((T, HEAD), lambda i, h: (i, R_QM // HEAD + h)),
                  pl.BlockSpec((T, HEAD), lambda i, h: (i, R_ZM // HEAD + h)),
                  pl.BlockSpec((N_MEM, HEAD), lambda i, h: (0, h)),
                  pl.BlockSpec((N_MEM, HEAD), lambda i, h: (0, M_HEADS + h))],
        out_specs=pl.BlockSpec((T, HEAD), lambda i, h: (i, h)),
        out_shape=jax.ShapeDtypeStruct((S, A_WIDTH), BF16),
        compiler_params=_cp(("parallel", "parallel")))(ur, ur, mkv, mkv)


def _mem_bwd(ur, mkv, dy, *, name):
    S = ur.shape[0]
    T = min(512, S)

    def body(q_ref, z_ref, k_ref, v_ref, dy_ref, dq_ref, dz_ref, dk_ref, dv_ref):
        i = pl.program_id(1)
        qv = q_ref[...].astype(BF16)
        kv = k_ref[...].astype(BF16)
        vv = v_ref[...].astype(BF16)
        s = _dot_nt(qv, kv) * M_SCALE
        p = jnp.exp(s - jnp.max(s, axis=-1, keepdims=True))
        p = p / jnp.sum(p, axis=-1, keepdims=True)
        o = _dot(p.astype(BF16), vv)
        sz, dsz = _silu_parts(z_ref[...])
        dyv = dy_ref[...]
        dz_ref[...] = (dyv * o * dsz).astype(BF16)
        dov = (dyv * sz).astype(BF16)
        dp = _dot_nt(dov, vv)
        ds = p * (dp - jnp.sum(p * dp, axis=-1, keepdims=True))
        dq_ref[...] = (_dot(ds.astype(BF16), kv) * M_SCALE).astype(BF16)
        dvp = _dot(p.T.astype(BF16), dov)
        dkp = _dot(ds.T.astype(BF16), qv) * M_SCALE

        @pl.when(i == 0)
        def _():
            dk_ref[...] = dkp
            dv_ref[...] = dvp

        @pl.when(i > 0)
        def _():
            dk_ref[...] += dkp
            dv_ref[...] += dvp

    tile = pl.BlockSpec((T, HEAD), lambda h, i: (i, h))
    acc = pl.BlockSpec((N_MEM, HEAD), lambda h, i: (0, h))
    return pl.pallas_call(
        body, name=name, grid=(M_HEADS, S // T),
        in_specs=[pl.BlockSpec((T, HEAD), lambda h, i: (i, R_QM // HEAD + h)),
                  pl.BlockSpec((T, HEAD), lambda h, i: (i, R_ZM // HEAD + h)),
                  pl.BlockSpec((N_MEM, HEAD), lambda h, i: (0, h)),
                  pl.BlockSpec((N_MEM, HEAD), lambda h, i: (0, M_HEADS + h)), tile],
        out_specs=[tile, tile, acc, acc],
        out_shape=[jax.ShapeDtypeStruct((S, A_WIDTH), BF16)] * 2
        + [jax.ShapeDtypeStruct((N_MEM, A_WIDTH), F32)] * 2,
        compiler_params=_cp(("parallel", "arbitrary")))(ur, ur, mkv, mkv, dy)


def _branch_fwd(ys, wbs, ur, b_merge, *, name):
    S = ur.shape[0]
    tm, tn = min(512, S), 512
    nj = D_MODEL // tn

    def body(ya, yb, ym, wa, wb, wm, g0, g1, g2, b0, b1, b2, mg_ref, p_ref):
        acc = jnp.zeros((tm, tn), F32)
        for i, (y, w, gr, br) in enumerate(((ya, wa, g0, b0), (yb, wb, g1, b1), (ym, wm, g2, b2))):
            pr = _dot(y[...], w[...])
            p_ref[i] = pr
            acc = acc + _sigmoid(gr[...] + br[...]) * pr
        mg_ref[...] = acc.astype(BF16)

    yspec = pl.BlockSpec((tm, A_WIDTH), lambda i, j: (i, 0))
    wspec = pl.BlockSpec((A_WIDTH, tn), lambda i, j: (0, j))
    gspec = lambda b: pl.BlockSpec((tm, tn), lambda i, j: (i, (R_GL + b * D_MODEL) // tn + j))
    bspec = lambda b: pl.BlockSpec((1, tn), lambda i, j: (0, b * nj + j))
    return pl.pallas_call(
        body, name=name, grid=(S // tm, nj),
        in_specs=[yspec] * 3 + [wspec] * 3 + [gspec(0), gspec(1), gspec(2), bspec(0), bspec(1), bspec(2)],
        out_specs=[pl.BlockSpec((tm, tn), lambda i, j: (i, j)),
                   pl.BlockSpec((3, tm, tn), lambda i, j: (0, i, j))],
        out_shape=[jax.ShapeDtypeStruct((S, D_MODEL), BF16), jax.ShapeDtypeStruct((3, S, D_MODEL), F32)],
        compiler_params=_cp(("parallel", "parallel")))(*ys, *wbs, ur, ur, ur, b_merge, b_merge, b_merge)


def _branch_bwd(dm, prods, ur, b_merge, *, name):
    S = ur.shape[0]
    tm = min(256, S)

    def body(dm_ref, p_ref, g0, g1, g2, b_ref, dp_ref, dgl_ref, db_ref):
        i = pl.program_id(0)
        dmv = dm_ref[...]
        parts = []
        for b, gr in enumerate((g0, g1, g2)):
            sl = slice(b * D_MODEL, (b + 1) * D_MODEL)
            gt = _sigmoid(gr[...] + b_ref[:, sl])
            dp_ref[b] = (dmv * gt).astype(BF16)
            dgl = dmv * p_ref[b] * gt * (1.0 - gt)
            dgl_ref[:, sl] = dgl.astype(BF16)
            parts.append(jnp.sum(dgl, axis=0, keepdims=True))
        part = jnp.concatenate(parts, axis=1)

        @pl.when(i == 0)
        def _():
            db_ref[...] = part

        @pl.when(i > 0)
        def _():
            db_ref[...] += part

    gspec = lambda b: pl.BlockSpec((tm, D_MODEL), lambda i: (i, R_GL // D_MODEL + b))
    vec = pl.BlockSpec((1, 3 * D_MODEL), lambda i: (0, 0))
    return pl.pallas_call(
        body, name=name, grid=(S // tm,),
        in_specs=[pl.BlockSpec((tm, D_MODEL), lambda i: (i, 0)),
                  pl.BlockSpec((3, tm, D_MODEL), lambda i: (0, i, 0)), gspec(0), gspec(1), gspec(2), vec],
        out_specs=[pl.BlockSpec((3, tm, D_MODEL), lambda i: (0, i, 0)),
                   pl.BlockSpec((tm, 3 * D_MODEL), lambda i: (i, 0)), vec],
        out_shape=[jax.ShapeDtypeStruct((3, S, D_MODEL), BF16), jax.ShapeDtypeStruct((S, 3 * D_MODEL), BF16),
                   jax.ShapeDtypeStruct((1, 3 * D_MODEL), F32)],
        compiler_params=_cp(("arbitrary",)))(dm, prods, ur, ur, ur, b_merge)


def _rope_tables(pos):
    half = ROT // 2
    inv = ROPE_THETA ** (-jnp.arange(half, dtype=F32) / half)
    ang = pos.astype(F32)[:, None] * inv
    cos, sin = jnp.cos(ang), jnp.sin(ang)
    S = pos.shape[0]
    one = jnp.ones((S, LANES - ROT), F32)
    zero = jnp.zeros((S, LANES - ROT), F32)
    zh = jnp.zeros((S, half), F32)
    c = jnp.concatenate([cos, cos, one], axis=1)
    s1 = jnp.concatenate([-sin, zh, zero], axis=1)
    s2 = jnp.concatenate([zh, sin, zero], axis=1)
    return c, s1, s2


def _to_tiles(t):
    S, H = t.shape
    return t.reshape(S // LANES, LANES, H).transpose(0, 2, 1)


def _from_tiles(t):
    nt, H, _ = t.shape
    return t.transpose(1, 0, 2).reshape(H, nt * LANES)


def _local_step(x, mem, pos, tgt, g_pre, g_post, g_mem, wt, bf_pad, b_merge, w_kv, wbs, w_out):
    S = x.shape[0]
    T = min(512, S)
    nq = S // T
    tabs = _rope_tables(pos)

    h = _rms_fwd(x, g_pre, name="rms_pre")
    hs = [_to_classes(h, d) for d in DIL]
    tabs_g = [[_to_classes(t, d) for t in tabs] for d in DIL]
    uas = [_mm(hs[g], wt[f"A{g}"], bt=True, name=f"proj_a{g}", tn=1536) for g in range(3)]
    ub = _mm(h, wt["B"], bt=True, out_dtype=BF16, name="proj_b", tn=1536)
    ur = _mm(h, wt["R"], bt=True, name="proj_r", tn=1792)

    outs_c, lses_c = [], []
    for g in range(3):
        o, l = _attn_a_fwd(uas[g], tabs_g[g], g, name=f"attn_a_fwd{g}")
        outs_c.append(o)
        lses_c.append(l)
    outs_a = [_from_classes(o, d) for o, d in zip(outs_c, DIL)]
    lses_a = [_from_classes(l, d) for l, d in zip(lses_c, DIL)]
    ya = _merge_a_fwd(outs_a, lses_a, ur, name="merge_a_fwd")

    logf = _logf(ur, bf_pad, name="logf")
    c = _from_tiles(_cumsum_lanes(_to_tiles(logf[:, :B_HEADS]), False, name="cumsum_fwd"))
    crow = c.reshape(B_HEADS, nq, 1, T)
    ckb = jnp.broadcast_to(c[:, :, None], (B_HEADS, S, LANES))
    kt = ub[:, 512:1024].reshape(nq, T, 512).transpose(0, 2, 1)
    vt = ub[:, 1024:1536].reshape(nq, T, 512).transpose(0, 2, 1)
    ob, lse_b = _fox_fwd(ub, vt, crow, ckb, name="fox_fwd")
    yb = _gate_fwd(ob, ur, R_ZB, name="gate_b_fwd")

    hm = _rms_fwd(mem, g_mem, name="rms_mem")
    mkv = _mm(hm, w_kv, name="proj_mem")
    ym = _mem_fwd(ur, mkv, name="mem_fwd")

    merged, prods = _branch_fwd((ya, yb, ym), wbs, ur, b_merge, name="branch_fwd")
    out = _mm(merged, w_out, name="proj_out")
    dy, d_out, dg_post, loss_row = _post(x, out, tgt, g_post, name="post")

    dmerged = _mm(d_out, w_out, bt=True, name="d_merged")
    dw_out = _mm(merged.T, d_out, name="dw_out", tk=2048)
    dprods, dgl, db_merge = _branch_bwd(dmerged, prods, ur, b_merge, name="branch_bwd")
    dys, dwbs = [], []
    for i, (y, wb) in enumerate(zip((ya, yb, ym), wbs)):
        dys.append(_mm(dprods[i], wb, bt=True, name=f"d_y{i}"))
        dwbs.append(_mm(y.T, dprods[i], name=f"dw_branch{i}", tk=2048))

    dos_a, adjs_a, dza = _merge_a_bwd(outs_a, lses_a, ur, dys[0], name="merge_a_bwd")
    dus_a = []
    for g, d in enumerate(DIL):
        do_c, adj_c = _to_classes(dos_a[g], d), _to_classes(adjs_a[g], d)
        dq = _attn_a_dq(uas[g], tabs_g[g], g, do_c, lses_c[g], adj_c, name=f"attn_a_dq{g}")
        dk, dv = _attn_a_dkv(uas[g], tabs_g[g], g, do_c, lses_c[g], adj_c, name=f"attn_a_dkv{g}")
        dus_a.append(jnp.concatenate([dq, dk, dv], axis=1))

    dob, dzb = _gate_bwd(ob, ur, R_ZB, dys[1], name="gate_b_bwd")
    delta_b = _fox_delta(ob, dob, name="fox_delta")
    dkb, dvb, dc_k, dqt, dc_q = _fox_bwd(ub, kt, crow, ckb, dob, lse_b, delta_b, name="fox_bwd")
    dqb = (dqt.transpose(0, 2, 1).reshape(S, A_WIDTH) * B_SCALE).astype(BF16)
    du_b = jnp.concatenate([dqb, dkb, dvb], axis=1)
    dc = dc_q.reshape(B_HEADS, S) + dc_k.reshape(B_HEADS, S)
    dlogf = _from_tiles(_cumsum_lanes(_to_tiles(dc.T), True, name="cumsum_bwd"))
    dlogf_pad = jnp.pad(dlogf.T, ((0, 0), (0, FB_PAD - B_HEADS)))
    dfb, db_forget = _dfb(ur, bf_pad, dlogf_pad, name="dfb")

    dqm, dzm, dmk, dmv = _mem_bwd(ur, mkv, dys[2], name="mem_bwd")
    dmkv = jnp.concatenate([dmk, dmv], axis=1).astype(BF16)
    dhm = _mm(dmkv, w_kv, bt=True, name="d_hm")
    dw_kv = _mm(hm.T, dmkv, name="dw_kv")
    dg_mem = _rms_bwd(mem, g_mem, dhm, None, name="rms_mem_bwd")

    du_r = jnp.concatenate([dza, dzb, dqm, dzm, dgl, dfb], axis=1)
    dh = _mm(du_r, wt["R"], name="d_h_r", tk=1792) + _mm(du_b, wt["B"], name="d_h_b", tk=1536)
    for g, d in enumerate(DIL):
        dh = dh + _from_classes(_mm(dus_a[g], wt[f"A{g}"], name=f"d_h_a{g}", tk=1536), d)
    dwt = {"R": _mm(h.T, du_r, name="dw_in_r", tn=1792, tk=1024).T,
           "B": _mm(h.T, du_b, name="dw_in_b", tn=1536, tk=2048).T}
    for g in range(3):
        dwt[f"A{g}"] = _mm(hs[g].T, dus_a[g], name=f"dw_in_a{g}", tn=1536, tk=2048).T
    grad_x, dg_pre = _rms_bwd(x, g_pre, dh, dy, name="rms_pre_bwd")

    return dict(loss=loss_row, grad_x=grad_x, dwt=dwt, dw_kv=dw_kv, dwbs=dwbs, dw_out=dw_out,
                dg_pre=dg_pre, dg_post=dg_post, dg_mem=dg_mem, db_forget=db_forget, db_merge=db_merge)


MESH = pl.DeviceIdType.MESH
ANY = pl.BlockSpec(memory_space=pl.ANY)


def _relations():
    return [(k >> 2 & 1, k >> 1 & 1, k & 1) for k in range(1, N_DEV)]


def _coords():
    return lax.axis_index("x"), lax.axis_index("y"), lax.axis_index("c")


def _all_gather(shard, *, name):
    R, W = shard.shape

    def body(x_ref, out_ref, send_sems, recv_sems, local_sem):
        x, y, c = _coords()
        me, sibling = (x, y, c), (x, y, 1 - c)
        chips = [(1 - x, y), (x, 1 - y), (1 - x, 1 - y)]

        def slot(px, py, pc):
            return out_ref.at[4 * px + 2 * py + pc]

        def copy(k, block, to, src=None):
            return pltpu.make_async_remote_copy(
                src_ref=slot(*block) if src is None else src, dst_ref=slot(*block),
                send_sem=send_sems.at[k], recv_sem=recv_sems.at[k], device_id=to, device_id_type=MESH)

        mine = pltpu.make_async_copy(x_ref, slot(*me), local_sem)
        mine.start()
        first = [copy(0, me, sibling, src=x_ref)]
        first += [copy(1 + j, me, (*chip, c), src=x_ref) for j, chip in enumerate(chips)]
        for cp in first:
            cp.start()
        passed = [copy(4 + j, (*chip, c), sibling) for j, chip in enumerate(chips)]
        for j, chip in enumerate(chips):
            copy(1 + j, (*chip, c), me).wait_recv()
            passed[j].start()
        copy(0, sibling, me).wait_recv()
        for j, chip in enumerate(chips):
            copy(4 + j, (*chip, 1 - c), me).wait_recv()
        for cp in first + passed:
            cp.wait_send()
        mine.wait()

    return pl.pallas_call(
        body, name=name, out_shape=jax.ShapeDtypeStruct((N_DEV, R, W), shard.dtype),
        in_specs=[ANY], out_specs=ANY,
        scratch_shapes=[pltpu.SemaphoreType.DMA((N_DEV - 1,)), pltpu.SemaphoreType.DMA((N_DEV - 1,)),
                        pltpu.SemaphoreType.DMA],
    )(shard)


N_CHIP = 4


def _exchange_pair(gbig, *, name):
    _, R, W = gbig.shape

    def body(g_ref, own_ref, sib_ref, send_sems, recv_sems, local_sems):
        x, y, c = _coords()
        sibling = (x, y, 1 - c)
        local, remote = [], []
        for r in range(N_CHIP):
            px, py = x ^ (r >> 1), y ^ (r & 1)
            local.append(pltpu.make_async_copy(g_ref.at[4 * px + 2 * py + c], own_ref.at[r], local_sems.at[r]))
            remote.append(pltpu.make_async_remote_copy(
                src_ref=g_ref.at[4 * px + 2 * py + (1 - c)], dst_ref=sib_ref.at[r],
                send_sem=send_sems.at[r], recv_sem=recv_sems.at[r], device_id=sibling, device_id_type=MESH))
        for cp in remote + local:
            cp.start()
        for cp in remote:
            cp.wait_recv()
        for cp in remote:
            cp.wait_send()
        for cp in local:
            cp.wait()

    return pl.pallas_call(
        body, name=name, out_shape=[jax.ShapeDtypeStruct((N_CHIP, R, W), gbig.dtype)] * 2,
        in_specs=[ANY], out_specs=[ANY, ANY],
        scratch_shapes=[pltpu.SemaphoreType.DMA((N_CHIP,)), pltpu.SemaphoreType.DMA((N_CHIP,)),
                        pltpu.SemaphoreType.DMA((N_CHIP,))],
    )(gbig)


def _pair_sum(own, sib, tr, *, name):
    _, R, W = own.shape

    def body(a_ref, b_ref, o_ref):
        o_ref[...] = (a_ref[...] + b_ref[...]).astype(BF16)

    blk = lambda off: pl.BlockSpec((None, tr, W), lambda r, i: (r + off, i, 0))
    return pl.pallas_call(
        body, name=name, grid=(N_CHIP - 1, R // tr), in_specs=[blk(1), blk(1)], out_specs=blk(0),
        out_shape=jax.ShapeDtypeStruct((N_CHIP - 1, R, W), BF16),
        compiler_params=_cp(("parallel", "parallel")))(own, sib)


def _exchange_chips(send, gsmall, *, name):
    nb, R, W = send.shape
    n = N_DEV - 1

    def body(b_ref, s_ref, rb_ref, rs_ref, send_sems, recv_sems, local_sem):
        x, y, c = _coords()
        me = 4 * x + 2 * y + c
        mine = pltpu.make_async_copy(s_ref, rs_ref.at[me], local_sem)
        mine.start()
        started = []
        for k, (fx, fy, fc) in enumerate(_relations()):
            cp = pltpu.make_async_remote_copy(
                src_ref=s_ref, dst_ref=rs_ref.at[me], send_sem=send_sems.at[k], recv_sem=recv_sems.at[k],
                device_id=(x ^ fx, y ^ fy, c ^ fc), device_id_type=MESH)
            cp.start()
            started.append(cp)
        for r in range(1, N_CHIP):
            cp = pltpu.make_async_remote_copy(
                src_ref=b_ref.at[r - 1], dst_ref=rb_ref.at[r - 1], send_sem=send_sems.at[n + r - 1],
                recv_sem=recv_sems.at[n + r - 1], device_id=(x ^ (r >> 1), y ^ (r & 1), c), device_id_type=MESH)
            cp.start()
            started.append(cp)
        for k, (fx, fy, fc) in enumerate(_relations()):
            peer = 4 * (x ^ fx) + 2 * (y ^ fy) + (c ^ fc)
            pltpu.make_async_remote_copy(
                src_ref=s_ref, dst_ref=rs_ref.at[peer], send_sem=send_sems.at[k], recv_sem=recv_sems.at[k],
                device_id=(x ^ fx, y ^ fy, c ^ fc), device_id_type=MESH).wait_recv()
        for r in range(1, N_CHIP):
            pltpu.make_async_remote_copy(
                src_ref=b_ref.at[r - 1], dst_ref=rb_ref.at[r - 1], send_sem=send_sems.at[n + r - 1],
                recv_sem=recv_sems.at[n + r - 1], device_id=(x ^ (r >> 1), y ^ (r & 1), c),
                device_id_type=MESH).wait_recv()
        for cp in started:
            cp.wait_send()
        mine.wait()

    return pl.pallas_call(
        body, name=name,
        out_shape=[jax.ShapeDtypeStruct((nb, R, W), send.dtype),
                   jax.ShapeDtypeStruct((N_DEV, 1, P_SMALL), gsmall.dtype)],
        in_specs=[ANY, ANY], out_specs=[ANY, ANY],
        scratch_shapes=[pltpu.SemaphoreType.DMA((n + nb,)), pltpu.SemaphoreType.DMA((n + nb,)),
                        pltpu.SemaphoreType.DMA],
    )(send, gsmall)


def _part_specs(parts, tr, row0):
    assert row0 % tr == 0
    return [pl.BlockSpec((n_used, tr, a.shape[2]), lambda i: (0, row0 // tr + i, 0)) for a, n_used in parts]


def _part_total(refs, parts):
    g = None
    for ref, (_, n_used) in zip(refs, parts):
        for k in range(n_used):
            t = ref[k].astype(F32)
            g = t if g is None else g + t
    return g


def _sum_parts(parts, row0, nrows, tr, *, name):
    W = parts[0][0].shape[2]
    assert nrows % tr == 0

    def body(*refs):
        refs[-1][...] = _part_total(refs[:-1], parts)

    return pl.pallas_call(
        body, name=name, grid=(nrows // tr,), in_specs=_part_specs(parts, tr, row0),
        out_specs=pl.BlockSpec((tr, W), lambda i: (i, 0)),
        out_shape=jax.ShapeDtypeStruct((nrows, W), F32),
        compiler_params=_cp(("parallel",)))(*[a for a, _ in parts])


def _adamw(parts, w, m, v, tr, *, name):
    R, W = w.shape
    assert R % tr == 0
    np_ = len(parts)

    def body(*refs):
        w_ref, m_ref, v_ref, g_ref, d_ref, nm_ref, nv_ref = refs[np_:]
        g = _part_total(refs[:np_], parts)
        mm = ADAM_B1 * m_ref[...] + (1.0 - ADAM_B1) * g
        vv = ADAM_B2 * v_ref[...] + (1.0 - ADAM_B2) * (g * g)
        m_hat = mm / (1.0 - ADAM_B1 ** ADAM_STEP)
        v_hat = vv / (1.0 - ADAM_B2 ** ADAM_STEP)
        g_ref[...] = g
        d_ref[...] = -ADAM_LR * (m_hat / (jnp.sqrt(v_hat) + ADAM_EPS) + ADAM_WD * w_ref[...])
        nm_ref[...] = mm
        nv_ref[...] = vv

    blk = pl.BlockSpec((tr, W), lambda i: (i, 0))
    return pl.pallas_call(
        body, name=name, grid=(R // tr,),
        in_specs=_part_specs(parts, tr, 0) + [blk, blk, blk],
        out_specs=[blk] * 4, out_shape=[jax.ShapeDtypeStruct((R, W), F32)] * 4,
        compiler_params=_cp(("parallel",)))(*[a for a, _ in parts], w, m, v)


def _pack_rest(w_kv, wa, wb, wm, w_out):
    return jnp.concatenate([w_kv[0], w_out[0]] + [t[0].reshape(-1, D_MODEL) for t in (wa, wb, wm)], axis=0)


def _unpack_rest(t):
    br = lambda i: t[RO_BR + 64 * i:RO_BR + 64 * (i + 1)].reshape(1, A_WIDTH, D_MODEL // N_DEV)
    return t[None, RO_KV:RO_OUT], br(0), br(1), br(2), t[None, RO_OUT:RO_BR]


def _orig_rows(gathered, a, b):
    res = []
    while a < b:
        dev, r = divmod(a, CS)
        n = min(b - a, CS - r)
        res.append(gathered[dev, RO_IN + r:RO_IN + r + n])
        a += n
    return res


def _full_weights(gathered):
    wt = {}
    for name, ranges in SEGS.items():
        rows = [p for a, b in ranges for p in _orig_rows(gathered, a, b)]
        if SEG_PAD[name]:
            rows.append(jnp.zeros((SEG_PAD[name], D_MODEL), gathered.dtype))
        wt[name] = jnp.concatenate(rows, axis=0)
    w_kv = gathered[:, RO_KV:RO_OUT].reshape(D_MODEL, D_MODEL)
    w_out = gathered[:, RO_OUT:RO_BR].reshape(D_MODEL, D_MODEL)
    wbs = [gathered[:, RO_BR + 64 * i:RO_BR + 64 * (i + 1)].reshape(N_DEV, A_WIDTH, D_MODEL // N_DEV)
           .transpose(1, 0, 2).reshape(A_WIDTH, D_MODEL) for i in range(3)]
    return wt, w_kv, wbs, w_out


def _orig_order(dwt):
    pieces = []
    for name, ranges in SEGS.items():
        o = 0
        for a, b in ranges:
            pieces.append((a, dwt[name][o:o + b - a]))
            o += b - a
    pieces.sort(key=lambda p: p[0])
    return jnp.concatenate([p[1] for p in pieces], axis=0)


def _pack_grads(dwt, dw_kv, dwbs, dw_out):
    g_in = jnp.pad(_orig_order(dwt).reshape(N_DEV, CS, D_MODEL), ((0, 0), (0, IN_ROWS - CS), (0, 0)))
    br = [t.reshape(A_WIDTH, N_DEV, D_MODEL // N_DEV).transpose(1, 0, 2).reshape(N_DEV, -1, D_MODEL) for t in dwbs]
    return jnp.concatenate([dw_kv.reshape(N_DEV, -1, D_MODEL), dw_out.reshape(N_DEV, -1, D_MODEL)] + br + [g_in],
                           axis=1)


def kernel(x, mem, positions, norm_pre_g, norm_post_g, norm_mem_g, w_in, b_forget, b_merge, w_mem_kv, w_branch_a, w_branch_b, w_branch_m, w_out, loss_target, m_norm_pre_g, m_norm_post_g, m_norm_mem_g, m_w_in, m_b_forget, m_b_merge, m_w_mem_kv, m_w_branch_a, m_w_branch_b, m_w_branch_m, m_w_out, v_norm_pre_g, v_norm_post_g, v_norm_mem_g, v_w_in, v_b_forget, v_b_merge, v_w_mem_kv, v_w_branch_a, v_w_branch_b, v_w_branch_m, v_w_out):
    w_rest = _pack_rest(w_mem_kv, w_branch_a, w_branch_b, w_branch_m, w_out)
    shard = jnp.concatenate([w_rest.astype(BF16), w_in[0].T.astype(BF16),
                             jnp.zeros((IN_ROWS - CS, D_MODEL), BF16)], axis=0)
    gathered = _all_gather(shard, name="gather_weights")
    wt, w_kv, wbs, w_o = _full_weights(gathered)

    bf_pad = jnp.pad(b_forget, ((0, 0), (0, FB_PAD - B_HEADS)))
    r = _local_step(x[0], mem[0], positions[0], loss_target[0], norm_pre_g, norm_post_g, norm_mem_g,
                    wt, bf_pad, b_merge, w_kv, wbs, w_o)

    gbig = _pack_grads(r["dwt"], r["dw_kv"], r["dwbs"], r["dw_out"])
    gsmall = jnp.concatenate([r["dg_pre"], r["dg_post"], r["dg_mem"], r["db_merge"],
                              r["db_forget"][:, :LANES], r["loss"]], axis=1)
    own, sib = _exchange_pair(gbig, name="exchange_pair")
    recv, rsmall = _exchange_chips(_pair_sum(own, sib, 208, name="pair_sum"), gsmall, name="exchange_chips")
    parts = [(own, 1), (sib, 1), (recv, N_CHIP - 1)]

    m_rest = _pack_rest(m_w_mem_kv, m_w_branch_a, m_w_branch_b, m_w_branch_m, m_w_out)
    v_rest = _pack_rest(v_w_mem_kv, v_w_branch_a, v_w_branch_b, v_w_branch_m, v_w_out)
    outs_rest = [_unpack_rest(t) for t in _adamw(parts, w_rest, m_rest, v_rest, 64, name="adamw_rest")]
    g_in = _sum_parts(parts, RO_IN, IN_ROWS, 16, name="sum_w_in")[:CS].T
    outs_in = _adamw([(g_in[None], 1)], w_in[0], m_w_in[0], v_w_in[0], 128, name="adamw_w_in")

    def small_vec(a, b, c, d, e):
        z = jnp.zeros((1, LANES - B_HEADS), F32)
        return jnp.concatenate([a, b, c, d, e, z, jnp.zeros((1, LANES), F32)], axis=1)

    outs_small = _adamw([(rsmall, N_DEV)], small_vec(norm_pre_g, norm_post_g, norm_mem_g, b_merge, b_forget),
                        small_vec(m_norm_pre_g, m_norm_post_g, m_norm_mem_g, m_b_merge, m_b_forget),
                        small_vec(v_norm_pre_g, v_norm_post_g, v_norm_mem_g, v_b_merge, v_b_forget),
                        1, name="adamw_small")

    def small_parts(t):
        return [t[:, O_GPRE:O_GPRE + D_MODEL], t[:, O_GPOST:O_GPOST + D_MODEL], t[:, O_GMEM:O_GMEM + D_MODEL],
                t[:, O_BF:O_BF + B_HEADS], t[:, O_BM:O_BM + 3 * D_MODEL]]

    loss = outs_small[0][0, O_LOSS]
    result = [loss, r["grad_x"][None]]
    for rest, w_i, small in zip(outs_rest, outs_in, outs_small):
        gp, gq, gm, bf, bm = small_parts(small)
        w_k, w_a, w_b, w_m, w_ot = rest
        result += [gp, gq, gm, w_i[None], bf, bm, w_k, w_a, w_b, w_m, w_ot]
    return tuple(result)
```

```python
import jax
import jax.numpy as jnp
from jax import lax
from jax.experimental import pallas as pl
from jax.experimental.pallas import tpu as pltpu

F32 = jnp.float32
BF16 = jnp.bfloat16

N_DEV = 8
D_MODEL = 1024
N_MEM = 256
EPS = 1e-6
NEG = -1e30
ROPE_THETA = 500000.0
DIL = (1, 4, 16)
A_HEADS = 4
HEAD = 128
A_WIDTH = 512
B_HEADS = 8
B_HEAD = 64
M_HEADS = 4
ROT = 32
IN_COLS = 11272
FB_PAD = 256

SEGS = {
    "A0": ((0, 512), (1536, 2048), (3072, 3584)),
    "A1": ((512, 1024), (2048, 2560), (3584, 4096)),
    "A2": ((1024, 1536), (2560, 3072), (4096, 4608)),
    "B": ((5120, 6656),),
    "R": ((4608, 5120), (6664, 7176), (7176, 7688), (7688, 8200), (8200, 11272), (6656, 6664)),
}
SEG_PAD = {"A0": 0, "A1": 0, "A2": 0, "B": 0, "R": FB_PAD - B_HEADS}
R_ZA, R_ZB, R_QM, R_ZM, R_GL, R_FB = 0, 512, 1024, 1536, 2048, 5120
NR = R_FB + FB_PAD

ADAM_LR, ADAM_B1, ADAM_B2, ADAM_EPS, ADAM_WD, ADAM_STEP = 0.001, 0.9, 0.999, 1e-08, 0.01, 10

LANES = 128
VMEM_LIMIT = 56 * 1024 * 1024

CS = IN_COLS // N_DEV
RO_KV, RO_OUT, RO_BR, RO_IN = 0, 128, 256, 448
IN_ROWS = 1424
ROWS = RO_IN + IN_ROWS
O_GPRE, O_GPOST, O_GMEM, O_BM, O_BF, O_LOSS = 0, 1024, 2048, 3072, 6144, 6272
P_SMALL = 6400


def _cp(sem=None):
    return pltpu.CompilerParams(dimension_semantics=sem, vmem_limit_bytes=VMEM_LIMIT)


def _dot(a, b):
    return jnp.dot(a, b, preferred_element_type=F32)


def _dot_nt(a, b):
    return lax.dot_general(a, b, (((1,), (1,)), ((), ())), preferred_element_type=F32)


def _sigmoid(z):
    return 1.0 / (1.0 + jnp.exp(-z))


def _mm(a, b, *, name, bt=False, out_dtype=F32, tm=1024, tn=1024, tk=None):
    M, K = a.shape
    N = b.shape[0] if bt else b.shape[1]
    tm, tn = min(tm, M), min(tn, N)
    tk = K if tk is None else min(tk, K)
    assert M % tm == 0 and N % tn == 0 and K % tk == 0
    nk = K // tk

    def body(a_ref, b_ref, o_ref, acc_ref):
        av = a_ref[...].astype(BF16)
        bv = b_ref[...].astype(BF16)
        p = _dot_nt(av, bv) if bt else _dot(av, bv)
        if nk == 1:
            o_ref[...] = p.astype(out_dtype)
        else:
            k = pl.program_id(2)

            @pl.when(k == 0)
            def _():
                acc_ref[...] = p

            @pl.when(k > 0)
            def _():
                acc_ref[...] += p

            @pl.when(k == nk - 1)
            def _():
                o_ref[...] = acc_ref[...].astype(out_dtype)

    b_spec = (pl.BlockSpec((tn, tk), lambda i, j, k: (j, k)) if bt
              else pl.BlockSpec((tk, tn), lambda i, j, k: (k, j)))
    return pl.pallas_call(
        body, name=name, grid=(M // tm, N // tn, nk),
        in_specs=[pl.BlockSpec((tm, tk), lambda i, j, k: (i, k)), b_spec],
        out_specs=pl.BlockSpec((tm, tn), lambda i, j, k: (i, j)),
        out_shape=jax.ShapeDtypeStruct((M, N), out_dtype),
        scratch_shapes=[pltpu.VMEM((tm, tn) if nk > 1 else (8, LANES), F32)],
        compiler_params=_cp(("parallel", "parallel", "arbitrary")),
    )(a, b)


def _rms_fwd(x, g, *, name):
    S, D = x.shape
    tm = min(512, S)

    def body(x_ref, g_ref, o_ref):
        xv = x_ref[...]
        r = lax.rsqrt(jnp.mean(xv * xv, axis=-1, keepdims=True) + EPS)
        o_ref[...] = (xv * r * g_ref[...]).astype(BF16)

    return pl.pallas_call(
        body, name=name, grid=(S // tm,),
        in_specs=[pl.BlockSpec((tm, D), lambda i: (i, 0)), pl.BlockSpec((1, D), lambda i: (0, 0))],
        out_specs=pl.BlockSpec((tm, D), lambda i: (i, 0)),
        out_shape=jax.ShapeDtypeStruct((S, D), BF16),
        compiler_params=_cp(("parallel",)),
    )(x, g)


def _rms_bwd(x, g, dh, dy, *, name):
    S, D = x.shape
    tm = min(512, S)
    want_dx = dy is not None

    def body(*refs):
        if want_dx:
            x_ref, g_ref, dh_ref, dy_ref, dx_ref, dg_ref = refs
        else:
            x_ref, g_ref, dh_ref, dg_ref = refs
        i = pl.program_id(0)
        xv = x_ref[...]
        r = lax.rsqrt(jnp.mean(xv * xv, axis=-1, keepdims=True) + EPS)
        xh = xv * r
        dhv = dh_ref[...]
        part = jnp.sum(dhv * xh, axis=0, keepdims=True)

        @pl.when(i == 0)
        def _():
            dg_ref[...] = part

        @pl.when(i > 0)
        def _():
            dg_ref[...] += part

        if want_dx:
            dxh = dhv * g_ref[...]
            dx_ref[...] = dy_ref[...] + r * (dxh - xh * jnp.mean(dxh * xh, axis=-1, keepdims=True))

    row = pl.BlockSpec((tm, D), lambda i: (i, 0))
    vec = pl.BlockSpec((1, D), lambda i: (0, 0))
    if want_dx:
        return pl.pallas_call(
            body, name=name, grid=(S // tm,), in_specs=[row, vec, row, row], out_specs=[row, vec],
            out_shape=[jax.ShapeDtypeStruct((S, D), F32), jax.ShapeDtypeStruct((1, D), F32)],
            compiler_params=_cp(("arbitrary",)))(x, g, dh, dy)
    return pl.pallas_call(
        body, name=name, grid=(S // tm,), in_specs=[row, vec, row], out_specs=vec,
        out_shape=jax.ShapeDtypeStruct((1, D), F32),
        compiler_params=_cp(("arbitrary",)))(x, g, dh)


def _post(x, out, tgt, g, *, name):
    S, D = x.shape
    tm = min(512, S)

    def body(x_ref, o_ref, t_ref, g_ref, dy_ref, do_ref, dg_ref, loss_ref):
        i = pl.program_id(0)
        ov = o_ref[...]
        r = lax.rsqrt(jnp.mean(ov * ov, axis=-1, keepdims=True) + EPS)
        n = ov * r
        gv = g_ref[...]
        e = (x_ref[...] + n * gv) - t_ref[...]
        lpart = 0.5 * jnp.sum(jnp.mean(e * e, axis=-1, keepdims=True), axis=0, keepdims=True)
        dy = e * (1.0 / D)
        dy_ref[...] = dy
        dn = dy * gv
        do_ref[...] = (r * (dn - n * jnp.mean(dn * n, axis=-1, keepdims=True))).astype(BF16)
        gpart = jnp.sum(dy * n, axis=0, keepdims=True)
        lrow = jnp.broadcast_to(lpart, (1, LANES))

        @pl.when(i == 0)
        def _():
            dg_ref[...] = gpart
            loss_ref[...] = lrow

        @pl.when(i > 0)
        def _():
            dg_ref[...] += gpart
            loss_ref[...] += lrow

    row = pl.BlockSpec((tm, D), lambda i: (i, 0))
    vec = pl.BlockSpec((1, D), lambda i: (0, 0))
    return pl.pallas_call(
        body, name=name, grid=(S // tm,), in_specs=[row, row, row, vec],
        out_specs=[row, row, vec, pl.BlockSpec((1, LANES), lambda i: (0, 0))],
        out_shape=[jax.ShapeDtypeStruct((S, D), F32), jax.ShapeDtypeStruct((S, D), BF16),
                   jax.ShapeDtypeStruct((1, D), F32), jax.ShapeDtypeStruct((1, LANES), F32)],
        compiler_params=_cp(("arbitrary",)))(x, out, tgt, g)


def _to_classes(t, d):
    if d == 1:
        return t
    S, C = t.shape
    return t.reshape(S // d, d, C).transpose(1, 0, 2).reshape(S, C)


def _from_classes(t, d):
    if d == 1:
        return t
    S, C = t.shape
    return t.reshape(d, S // d, C).transpose(1, 0, 2).reshape(S, C)


def _rope(x, c, s1, s2):
    return x * c + pltpu.roll(x, LANES - ROT // 2, 1) * s1 + pltpu.roll(x, ROT // 2, 1) * s2


def _unrope(d, c, s1, s2):
    return d * c + pltpu.roll(d * s1, ROT // 2, 1) + pltpu.roll(d * s2, LANES - ROT // 2, 1)


def _a_masks():
    qi = lax.broadcasted_iota(jnp.int32, (HEAD, HEAD), 0)
    ki = lax.broadcasted_iota(jnp.int32, (HEAD, HEAD), 1)
    return ki >= qi, ki <= qi


A_SCALE = HEAD ** -0.5


def _a_geometry(S, g):
    d = DIL[g]
    L = S // d
    TQ = min(512, L)
    return d, L, TQ, TQ // HEAD, L // TQ, L // HEAD


def _attn_a_fwd(ua, tabs, g, *, name):
    S = ua.shape[0]
    d, L, TQ, nsub, nb, nblk = _a_geometry(S, g)

    def body(q_ref, kc_ref, kp_ref, vc_ref, vp_ref, c_ref, s1_ref, s2_ref, cp_ref, s1p_ref, s2p_ref,
             o_ref, l_ref):
        n = pl.program_id(1)
        tc = (c_ref[...], s1_ref[...], s2_ref[...])
        q = _rope(q_ref[...], *tc).astype(BF16)
        kc = _rope(kc_ref[...], *tc).astype(BF16)
        kp = _rope(kp_ref[...], cp_ref[...], s1p_ref[...], s2p_ref[...]).astype(BF16)
        vc = vc_ref[...].astype(BF16)
        vp = vp_ref[...].astype(BF16)
        mprev, mcur = _a_masks()
        for a in range(nsub):
            sl = slice(a * HEAD, (a + 1) * HEAD)
            pv = slice((a - 1) * HEAD, a * HEAD)
            qa = q[sl]
            k_prev, v_prev = (kp, vp) if a == 0 else (kc[pv], vc[pv])
            mp = jnp.logical_and(mprev, n > 0) if a == 0 else mprev
            s_p = jnp.where(mp, _dot_nt(qa, k_prev) * A_SCALE, NEG)
            s_c = jnp.where(mcur, _dot_nt(qa, kc[sl]) * A_SCALE, NEG)
            m = jnp.maximum(jnp.max(s_p, axis=-1, keepdims=True), jnp.max(s_c, axis=-1, keepdims=True))
            p_p = jnp.exp(s_p - m)
            p_c = jnp.exp(s_c - m)
            den = jnp.sum(p_p, axis=-1, keepdims=True) + jnp.sum(p_c, axis=-1, keepdims=True)
            o = (_dot(p_p.astype(BF16), v_prev) + _dot(p_c.astype(BF16), vc[sl])) / den
            o_ref[sl, :] = o
            l_ref[sl, :] = jnp.broadcast_to(m + jnp.log(den), (HEAD, HEAD))

    rcur = lambda cb, n: (cb // A_HEADS) * nb + n
    rprv = lambda cb, n: (cb // A_HEADS) * nblk + jnp.maximum(n * nsub - 1, 0)
    cur = lambda off: pl.BlockSpec((TQ, HEAD), lambda cb, n: (rcur(cb, n), off + cb % A_HEADS))
    prv = lambda off: pl.BlockSpec((HEAD, HEAD), lambda cb, n: (rprv(cb, n), off + cb % A_HEADS))
    tcur = pl.BlockSpec((TQ, LANES), lambda cb, n: (rcur(cb, n), 0))
    tprv = pl.BlockSpec((HEAD, LANES), lambda cb, n: (rprv(cb, n), 0))
    out = pl.BlockSpec((TQ, HEAD), lambda cb, n: (rcur(cb, n), cb % A_HEADS))
    return pl.pallas_call(
        body, name=name, grid=(A_HEADS * d, nb),
        in_specs=[cur(0), cur(4), prv(4), cur(8), prv(8), tcur, tcur, tcur, tprv, tprv, tprv],
        out_specs=[out, out],
        out_shape=[jax.ShapeDtypeStruct((S, A_WIDTH), F32)] * 2,
        compiler_params=_cp(("parallel", "parallel")),
    )(ua, ua, ua, ua, ua, *tabs, *tabs)


def _attn_a_dq(ua, tabs, g, do, lse, adj, *, name):
    S = ua.shape[0]
    d, L, TQ, nsub, nb, nblk = _a_geometry(S, g)

    def body(q_ref, kc_ref, kp_ref, vc_ref, vp_ref, do_ref, l_ref, adj_ref,
             c_ref, s1_ref, s2_ref, cp_ref, s1p_ref, s2p_ref, dq_ref):
        n = pl.program_id(1)
        tc = (c_ref[...], s1_ref[...], s2_ref[...])
        q = _rope(q_ref[...], *tc).astype(BF16)
        kc = _rope(kc_ref[...], *tc).astype(BF16)
        kp = _rope(kp_ref[...], cp_ref[...], s1p_ref[...], s2p_ref[...]).astype(BF16)
        vc = vc_ref[...].astype(BF16)
        vp = vp_ref[...].astype(BF16)
        mprev, mcur = _a_masks()
        for a in range(nsub):
            sl = slice(a * HEAD, (a + 1) * HEAD)
            pv = slice((a - 1) * HEAD, a * HEAD)
            qa = q[sl]
            k_prev, v_prev = (kp, vp) if a == 0 else (kc[pv], vc[pv])
            mp = jnp.logical_and(mprev, n > 0) if a == 0 else mprev
            lse_a = l_ref[sl, :][:, :1]
            adj_a = adj_ref[sl, :][:, :1]
            doa = do_ref[sl, :]
            p_p = jnp.exp(jnp.where(mp, _dot_nt(qa, k_prev) * A_SCALE, NEG) - lse_a)
            p_c = jnp.exp(jnp.where(mcur, _dot_nt(qa, kc[sl]) * A_SCALE, NEG) - lse_a)
            ds_p = p_p * (_dot_nt(doa, v_prev) + adj_a)
            ds_c = p_c * (_dot_nt(doa, vc[sl]) + adj_a)
            dq = (_dot(ds_p.astype(BF16), k_prev) + _dot(ds_c.astype(BF16), kc[sl])) * A_SCALE
            dq_ref[sl, :] = _unrope(dq, c_ref[sl, :], s1_ref[sl, :], s2_ref[sl, :]).astype(BF16)

    rcur = lambda cb, n: (cb // A_HEADS) * nb + n
    rprv = lambda cb, n: (cb // A_HEADS) * nblk + jnp.maximum(n * nsub - 1, 0)
    cur = lambda off: pl.BlockSpec((TQ, HEAD), lambda cb, n: (rcur(cb, n), off + cb % A_HEADS))
    prv = lambda off: pl.BlockSpec((HEAD, HEAD), lambda cb, n: (rprv(cb, n), off + cb % A_HEADS))
    tcur = pl.BlockSpec((TQ, LANES), lambda cb, n: (rcur(cb, n), 0))
    tprv = pl.BlockSpec((HEAD, LANES), lambda cb, n: (rprv(cb, n), 0))
    blk = pl.BlockSpec((TQ, HEAD), lambda cb, n: (rcur(cb, n), cb % A_HEADS))
    return pl.pallas_call(
        body, name=name, grid=(A_HEADS * d, nb),
        in_specs=[cur(0), cur(4), prv(4), cur(8), prv(8), blk, blk, blk,
                  tcur, tcur, tcur, tprv, tprv, tprv],
        out_specs=blk,
        out_shape=jax.ShapeDtypeStruct((S, A_WIDTH), BF16),
        compiler_params=_cp(("parallel", "parallel")),
    )(ua, ua, ua, ua, ua, do, lse, adj, *tabs, *tabs)


def _attn_a_dkv(ua, tabs, g, do, lse, adj, *, name):
    S = ua.shape[0]
    d, L, TQ, nsub, nb, nblk = _a_geometry(S, g)

    def body(qc_ref, qn_ref, kc_ref, vc_ref, doc_ref, don_ref, lc_ref, ln_ref, ac_ref, an_ref,
             c_ref, s1_ref, s2_ref, cn_ref, s1n_ref, s2n_ref, dk_ref, dv_ref):
        n = pl.program_id(1)
        tc = (c_ref[...], s1_ref[...], s2_ref[...])
        qc = _rope(qc_ref[...], *tc).astype(BF16)
        qn = _rope(qn_ref[...], cn_ref[...], s1n_ref[...], s2n_ref[...]).astype(BF16)
        kc = _rope(kc_ref[...], *tc).astype(BF16)
        vc = vc_ref[...].astype(BF16)
        kr = lax.broadcasted_iota(jnp.int32, (HEAD, HEAD), 0)
        qc_i = lax.broadcasted_iota(jnp.int32, (HEAD, HEAD), 1)
        own_t = kr <= qc_i
        nxt_t = kr >= qc_i
        has_next = n < nb - 1
        for b in range(nsub):
            sl = slice(b * HEAD, (b + 1) * HEAD)
            nx = slice((b + 1) * HEAD, (b + 2) * HEAD)
            kb, vb = kc[sl], vc[sl]
            last = b == nsub - 1
            parts = [(qc[sl], doc_ref[sl, :], lc_ref[sl, :], ac_ref[sl, :], own_t)]
            if last:
                parts.append((qn, don_ref[...], ln_ref[...], an_ref[...], jnp.logical_and(nxt_t, has_next)))
            else:
                parts.append((qc[nx], doc_ref[nx, :], lc_ref[nx, :], ac_ref[nx, :], nxt_t))
            dk = jnp.zeros((HEAD, HEAD), F32)
            dv = jnp.zeros((HEAD, HEAD), F32)
            for qq, dd, ll, aa, msk in parts:
                st = jnp.where(msk, _dot_nt(kb, qq) * A_SCALE, NEG)
                pt = jnp.exp(st - ll.T)
                dv = dv + _dot(pt.astype(BF16), dd)
                dst = pt * (_dot_nt(vb, dd) + aa.T)
                dk = dk + _dot(dst.astype(BF16), qq)
            dk = dk * A_SCALE
            dk_ref[sl, :] = _unrope(dk, c_ref[sl, :], s1_ref[sl, :], s2_ref[sl, :]).astype(BF16)
            dv_ref[sl, :] = dv.astype(BF16)

    rcur = lambda cb, n: (cb // A_HEADS) * nb + n
    rnxt = lambda cb, n: (cb // A_HEADS) * nblk + jnp.minimum((n + 1) * nsub, nblk - 1)
    cur = lambda off: pl.BlockSpec((TQ, HEAD), lambda cb, n: (rcur(cb, n), off + cb % A_HEADS))
    nxu = lambda off: pl.BlockSpec((HEAD, HEAD), lambda cb, n: (rnxt(cb, n), off + cb % A_HEADS))
    tcur = pl.BlockSpec((TQ, LANES), lambda cb, n: (rcur(cb, n), 0))
    tnxt = pl.BlockSpec((HEAD, LANES), lambda cb, n: (rnxt(cb, n), 0))
    blk = pl.BlockSpec((TQ, HEAD), lambda cb, n: (rcur(cb, n), cb % A_HEADS))
    bnx = pl.BlockSpec((HEAD, HEAD), lambda cb, n: (rnxt(cb, n), cb % A_HEADS))
    return pl.pallas_call(
        body, name=name, grid=(A_HEADS * d, nb),
        in_specs=[cur(0), nxu(0), cur(4), cur(8), blk, bnx, blk, bnx, blk, bnx,
                  tcur, tcur, tcur, tnxt, tnxt, tnxt],
        out_specs=[blk, blk],
        out_shape=[jax.ShapeDtypeStruct((S, A_WIDTH), BF16)] * 2,
        compiler_params=_cp(("parallel", "parallel")),
    )(ua, ua, ua, ua, do, do, lse, lse, adj, adj, *tabs, *tabs)


def _silu_parts(z):
    sg = _sigmoid(z)
    return z * sg, sg * (1.0 + z * (1.0 - sg))


def _merge_a_fwd(os_, ls_, ur, *, name):
    S = ur.shape[0]
    tm = min(512, S)

    def body(o0, o1, o2, l0, l1, l2, z_ref, y_ref):
        ls = [l0[...], l1[...], l2[...]]
        mx = jnp.maximum(jnp.maximum(ls[0], ls[1]), ls[2])
        es = [jnp.exp(l - mx) for l in ls]
        den = es[0] + es[1] + es[2]
        y = (es[0] / den) * o0[...] + (es[1] / den) * o1[...] + (es[2] / den) * o2[...]
        y_ref[...] = (y * _silu_parts(z_ref[...])[0]).astype(BF16)

    blk = pl.BlockSpec((tm, A_WIDTH), lambda i: (i, 0))
    return pl.pallas_call(
        body, name=name, grid=(S // tm,),
        in_specs=[blk] * 6 + [pl.BlockSpec((tm, A_WIDTH), lambda i: (i, R_ZA // A_WIDTH))],
        out_specs=blk, out_shape=jax.ShapeDtypeStruct((S, A_WIDTH), BF16),
        compiler_params=_cp(("parallel",)))(*os_, *ls_, ur)


def _merge_a_bwd(os_, ls_, ur, dya, *, name):
    S = ur.shape[0]
    tm = min(256, S)

    def body(o0, o1, o2, l0, l1, l2, z_ref, dy_ref, d0, d1, d2, a0, a1, a2, dz_ref):
        ls = [l0[...], l1[...], l2[...]]
        ov = [o0[...], o1[...], o2[...]]
        mx = jnp.maximum(jnp.maximum(ls[0], ls[1]), ls[2])
        es = [jnp.exp(l - mx) for l in ls]
        den = es[0] + es[1] + es[2]
        ws = [e / den for e in es]
        y = ws[0] * ov[0] + ws[1] * ov[1] + ws[2] * ov[2]
        sz, dsz = _silu_parts(z_ref[...])
        dyv = dy_ref[...]
        dz_ref[...] = (dyv * y * dsz).astype(BF16)
        dyp = dyv * sz
        for h in range(A_HEADS):
            sl = slice(h * HEAD, (h + 1) * HEAD)
            t = jnp.zeros((tm, 1), F32)
            for gi in range(3):
                t = t + ws[gi][:, sl][:, :1] * jnp.sum(dyp[:, sl] * ov[gi][:, sl], axis=-1, keepdims=True)
            for gi, (dref, aref) in enumerate(((d0, a0), (d1, a1), (d2, a2))):
                wg = ws[gi][:, sl]
                dref[:, sl] = (wg * dyp[:, sl]).astype(BF16)
                aref[:, sl] = -wg * t

    blk = pl.BlockSpec((tm, A_WIDTH), lambda i: (i, 0))
    outs = pl.pallas_call(
        body, name=name, grid=(S // tm,),
        in_specs=[blk] * 6 + [pl.BlockSpec((tm, A_WIDTH), lambda i: (i, R_ZA // A_WIDTH)), blk],
        out_specs=[blk] * 7,
        out_shape=[jax.ShapeDtypeStruct((S, A_WIDTH), BF16)] * 3
        + [jax.ShapeDtypeStruct((S, A_WIDTH), F32)] * 3 + [jax.ShapeDtypeStruct((S, A_WIDTH), BF16)],
        compiler_params=_cp(("parallel",)))(*os_, *ls_, ur, dya)
    return outs[0:3], outs[3:6], outs[6]


def _logf(ur, bf_pad, *, name):
    S = ur.shape[0]
    tm = min(1024, S)

    def body(u_ref, b_ref, o_ref):
        z = u_ref[...] + b_ref[...]
        o_ref[...] = jnp.minimum(z, 0.0) - jnp.log(1.0 + jnp.exp(-jnp.abs(z)))

    return pl.pallas_call(
        body, name=name, grid=(S // tm,),
        in_specs=[pl.BlockSpec((tm, FB_PAD), lambda i: (i, R_FB // FB_PAD)),
                  pl.BlockSpec((1, FB_PAD), lambda i: (0, 0))],
        out_specs=pl.BlockSpec((tm, FB_PAD), lambda i: (i, 0)),
        out_shape=jax.ShapeDtypeStruct((S, FB_PAD), F32),
        compiler_params=_cp(("parallel",)))(ur, bf_pad)


def _cumsum_lanes(x, reverse, *, name):
    nt, H, _ = x.shape

    def body(x_ref, o_ref):
        lane = lax.broadcasted_iota(jnp.int32, (H, LANES), 1)

        def tile(t, carry):
            tt = nt - 1 - t if reverse else t
            v = x_ref[tt]
            k = 1
            while k < LANES:
                if reverse:
                    v = v + jnp.where(lane < LANES - k, pltpu.roll(v, LANES - k, 1), 0.0)
                else:
                    v = v + jnp.where(lane >= k, pltpu.roll(v, k, 1), 0.0)
                k *= 2
            v = v + carry
            o_ref[tt] = v
            edge = v[:, :1] if reverse else v[:, LANES - 1:]
            return jnp.broadcast_to(edge, (H, LANES))

        lax.fori_loop(0, nt, tile, jnp.zeros((H, LANES), F32))

    return pl.pallas_call(
        body, name=name, out_shape=jax.ShapeDtypeStruct((nt, H, LANES), F32),
        in_specs=[pl.BlockSpec(memory_space=pltpu.VMEM)], out_specs=pl.BlockSpec(memory_space=pltpu.VMEM),
        compiler_params=_cp())(x)


B_SCALE = B_HEAD ** -0.5


def _pair_masks():
    lane = lax.broadcasted_iota(jnp.int32, (1, LANES), 1)
    row = lax.broadcasted_iota(jnp.int32, (LANES, 1), 0)
    return (lane < B_HEAD, lane >= B_HEAD), (row < B_HEAD, row >= B_HEAD)


def _causal_t(T):
    r = lax.broadcasted_iota(jnp.int32, (T, T), 0)
    c = lax.broadcasted_iota(jnp.int32, (T, T), 1)
    return r <= c


def _zero_other(x, keep):
    return jnp.where(keep, x, jnp.zeros_like(x))


def _fox_fwd(ub, vt, crow, ckb, *, name):
    S = ub.shape[0]
    T = min(512, S)
    nq = S // T

    def body(q_ref, k_ref, vt_ref, cr_ref, ck_ref, o_ref, l_ref, m_s, l_s, acc_s):
        i = pl.program_id(1)
        lanes, rows = _pair_masks()
        q = q_ref[...] * B_SCALE
        qm = [_zero_other(q, lanes[0]), _zero_other(q, lanes[1])]
        m_s[...] = jnp.full((2, 1, T), NEG, F32)
        l_s[...] = jnp.zeros((2, 1, T), F32)
        acc_s[...] = jnp.zeros((LANES, T), F32)

        def step(j, masked):
            off = pl.multiple_of(j * T, T)
            kj = k_ref[pl.ds(off, T), :]
            vtj = vt_ref[j]
            upd = jnp.zeros((LANES, T), F32)
            alphas = []
            for a in range(2):
                st = _dot_nt(kj, qm[a]) + (cr_ref[a, i] - jnp.tile(ck_ref[a, pl.ds(off, T), :], (1, T // LANES)))
                if masked:
                    st = jnp.where(_causal_t(T), st, NEG)
                m_old = m_s[a]
                m_new = jnp.maximum(m_old, jnp.max(st, axis=0, keepdims=True))
                alpha = jnp.exp(m_old - m_new)
                pt = jnp.exp(st - m_new)
                l_s[a] = alpha * l_s[a] + jnp.sum(pt, axis=0, keepdims=True)
                m_s[a] = m_new
                upd = upd + _dot(_zero_other(vtj, rows[a]), pt.astype(BF16))
                alphas.append(alpha)
            acc_s[...] = acc_s[...] * jnp.where(rows[0], alphas[0], alphas[1]) + upd

        def loop(j, carry):
            step(j, False)
            return carry

        lax.fori_loop(0, i, loop, 0)
        step(i, True)
        o_ref[...] = (acc_s[...] / jnp.where(rows[0], l_s[0], l_s[1])).T
        l_ref[0] = m_s[0] + jnp.log(l_s[0])
        l_ref[1] = m_s[1] + jnp.log(l_s[1])

    stat = pl.BlockSpec((2, None, 1, T), lambda h, i: (h, i, 0, 0))
    return pl.pallas_call(
        body, name=name, grid=(B_HEADS // 2, nq),
        in_specs=[pl.BlockSpec((T, LANES), lambda h, i: (i, h)),
                  pl.BlockSpec((S, LANES), lambda h, i: (0, 4 + h)),
                  pl.BlockSpec((nq, LANES, T), lambda h, i: (0, h, 0)),
                  pl.BlockSpec((2, nq, 1, T), lambda h, i: (h, 0, 0, 0)),
                  pl.BlockSpec((2, S, LANES), lambda h, i: (h, 0, 0))],
        out_specs=[pl.BlockSpec((T, LANES), lambda h, i: (i, h)), stat],
        out_shape=[jax.ShapeDtypeStruct((S, A_WIDTH), F32), jax.ShapeDtypeStruct((B_HEADS, nq, 1, T), F32)],
        scratch_shapes=[pltpu.VMEM((2, 1, T), F32), pltpu.VMEM((2, 1, T), F32), pltpu.VMEM((LANES, T), F32)],
        compiler_params=_cp(("parallel", "parallel")),
    )(ub, ub, vt, crow, ckb)


def _fox_delta(o, do, *, name):
    S = o.shape[0]
    T = min(512, S)
    nq = S // T

    def body(o_ref, do_ref, d_ref):
        _, rows = _pair_masks()
        prod_t = (do_ref[...].astype(F32) * o_ref[...]).T
        d_ref[0] = jnp.sum(_zero_other(prod_t, rows[0]), axis=0, keepdims=True)
        d_ref[1] = jnp.sum(_zero_other(prod_t, rows[1]), axis=0, keepdims=True)

    tile = pl.BlockSpec((T, LANES), lambda h, i: (i, h))
    return pl.pallas_call(
        body, name=name, grid=(B_HEADS // 2, nq), in_specs=[tile, tile],
        out_specs=pl.BlockSpec((2, None, 1, T), lambda h, i: (h, i, 0, 0)),
        out_shape=jax.ShapeDtypeStruct((B_HEADS, nq, 1, T), F32),
        compiler_params=_cp(("parallel", "parallel")))(o, do)


def _fox_bwd(ub, kt, crow, ckb, do, lse, delta, *, name):
    S = ub.shape[0]
    T = min(512, S)
    nq = S // T

    def body(k_ref, v_ref, kt_ref, q_ref, do_ref, cr_ref, ck_ref, l_ref, dl_ref,
             dk_ref, dv_ref, dck_ref, dqt_ref, dcq_ref, dk_s, dv_s, dc_s):
        j = pl.program_id(1)
        lanes, rows = _pair_masks()
        kv = k_ref[...]
        vv = v_ref[...]
        ktj = kt_ref[...]
        km = [_zero_other(kv, lanes[0]), _zero_other(kv, lanes[1])]
        ktm = [_zero_other(ktj, rows[0]), _zero_other(ktj, rows[1])]
        ck = [jnp.tile(ck_ref[a], (1, T // LANES)) for a in range(2)]
        dk_s[...] = jnp.zeros((T, LANES), F32)
        dv_s[...] = jnp.zeros((T, LANES), F32)
        dc_s[...] = jnp.zeros((2, T, 1), F32)

        @pl.when(j == 0)
        def _():
            dqt_ref[...] = jnp.zeros((nq, LANES, T), F32)
            dcq_ref[...] = jnp.zeros((2, nq, 1, T), F32)

        def step(i, masked):
            off = pl.multiple_of(i * T, T)
            qi = q_ref[pl.ds(off, T), :] * B_SCALE
            doi = do_ref[pl.ds(off, T), :]
            upd = jnp.zeros((LANES, T), F32)
            for a in range(2):
                st = _dot_nt(km[a], qi) + (cr_ref[a, i] - ck[a])
                if masked:
                    st = jnp.where(_causal_t(T), st, NEG)
                pt = jnp.exp(st - l_ref[a, i])
                doa = _zero_other(doi, lanes[a])
                dv_s[...] += _dot(pt.astype(BF16), doa)
                dst = pt * (_dot_nt(vv, doa) - dl_ref[a, i])
                dsb = dst.astype(BF16)
                dk_s[...] += _dot(dsb, _zero_other(qi, lanes[a]))
                upd = upd + _dot(ktm[a], dsb)
                dc_s[a] -= jnp.sum(dst, axis=-1, keepdims=True)
                dcq_ref[a, i] += jnp.sum(dst, axis=0, keepdims=True)
            dqt_ref[i] += upd

        def loop(i, carry):
            step(i, False)
            return carry

        step(j, True)
        lax.fori_loop(j + 1, nq, loop, 0)
        dk_ref[...] = dk_s[...].astype(BF16)
        dv_ref[...] = dv_s[...].astype(BF16)
        dck_ref[...] = dc_s[...]

    rowv = pl.BlockSpec((2, nq, 1, T), lambda h, j: (h, 0, 0, 0))
    tile = pl.BlockSpec((T, LANES), lambda h, j: (j, h))
    return pl.pallas_call(
        body, name=name, grid=(B_HEADS // 2, nq),
        in_specs=[pl.BlockSpec((T, LANES), lambda h, j: (j, 4 + h)),
                  pl.BlockSpec((T, LANES), lambda h, j: (j, 8 + h)),
                  pl.BlockSpec((None, LANES, T), lambda h, j: (j, h, 0)),
                  pl.BlockSpec((S, LANES), lambda h, j: (0, h)),
                  pl.BlockSpec((S, LANES), lambda h, j: (0, h)),
                  rowv,
                  pl.BlockSpec((2, T, LANES), lambda h, j: (h, j, 0)),
                  rowv, rowv],
        out_specs=[tile, tile, pl.BlockSpec((2, T, 1), lambda h, j: (h, j, 0)),
                   pl.BlockSpec((nq, LANES, T), lambda h, j: (0, h, 0)), rowv],
        out_shape=[jax.ShapeDtypeStruct((S, A_WIDTH), BF16)] * 2 + [jax.ShapeDtypeStruct((B_HEADS, S, 1), F32),
                   jax.ShapeDtypeStruct((nq, A_WIDTH, T), F32), jax.ShapeDtypeStruct((B_HEADS, nq, 1, T), F32)],
        scratch_shapes=[pltpu.VMEM((T, LANES), F32), pltpu.VMEM((T, LANES), F32), pltpu.VMEM((2, T, 1), F32)],
        compiler_params=_cp(("parallel", "arbitrary")),
    )(ub, ub, kt, ub, do, crow, ckb, lse, delta)


def _gate_fwd(o, ur, zcol, *, name):
    S = ur.shape[0]
    tm = min(1024, S)

    def body(o_ref, z_ref, y_ref):
        y_ref[...] = (o_ref[...] * _silu_parts(z_ref[...])[0]).astype(BF16)

    blk = pl.BlockSpec((tm, A_WIDTH), lambda i: (i, 0))
    return pl.pallas_call(
        body, name=name, grid=(S // tm,),
        in_specs=[blk, pl.BlockSpec((tm, A_WIDTH), lambda i: (i, zcol // A_WIDTH))],
        out_specs=blk, out_shape=jax.ShapeDtypeStruct((S, A_WIDTH), BF16),
        compiler_params=_cp(("parallel",)))(o, ur)


def _gate_bwd(o, ur, zcol, dy, *, name):
    S = ur.shape[0]
    tm = min(1024, S)

    def body(o_ref, z_ref, dy_ref, do_ref, dz_ref):
        sz, dsz = _silu_parts(z_ref[...])
        dyv = dy_ref[...]
        do_ref[...] = (dyv * sz).astype(BF16)
        dz_ref[...] = (dyv * o_ref[...] * dsz).astype(BF16)

    blk = pl.BlockSpec((tm, A_WIDTH), lambda i: (i, 0))
    return pl.pallas_call(
        body, name=name, grid=(S // tm,),
        in_specs=[blk, pl.BlockSpec((tm, A_WIDTH), lambda i: (i, zcol // A_WIDTH)), blk],
        out_specs=[blk, blk], out_shape=[jax.ShapeDtypeStruct((S, A_WIDTH), BF16)] * 2,
        compiler_params=_cp(("parallel",)))(o, ur, dy)


def _dfb(ur, bf_pad, dlogf_pad, *, name):
    S = ur.shape[0]
    tm = min(1024, S)

    def body(u_ref, b_ref, d_ref, o_ref, s_ref):
        i = pl.program_id(0)
        dv = d_ref[...] * _sigmoid(-(u_ref[...] + b_ref[...]))
        o_ref[...] = dv.astype(BF16)
        part = jnp.sum(dv, axis=0, keepdims=True)

        @pl.when(i == 0)
        def _():
            s_ref[...] = part

        @pl.when(i > 0)
        def _():
            s_ref[...] += part

    vec = pl.BlockSpec((1, FB_PAD), lambda i: (0, 0))
    blk = pl.BlockSpec((tm, FB_PAD), lambda i: (i, 0))
    return pl.pallas_call(
        body, name=name, grid=(S // tm,),
        in_specs=[pl.BlockSpec((tm, FB_PAD), lambda i: (i, R_FB // FB_PAD)), vec, blk],
        out_specs=[blk, vec],
        out_shape=[jax.ShapeDtypeStruct((S, FB_PAD), BF16), jax.ShapeDtypeStruct((1, FB_PAD), F32)],
        compiler_params=_cp(("arbitrary",)))(ur, bf_pad, dlogf_pad)


M_SCALE = HEAD ** -0.5


def _mem_fwd(ur, mkv, *, name):
    S = ur.shape[0]
    T = min(512, S)

    def body(q_ref, z_ref, k_ref, v_ref, y_ref):
        s = _dot_nt(q_ref[...].astype(BF16), k_ref[...].astype(BF16)) * M_SCALE
        p = jnp.exp(s - jnp.max(s, axis=-1, keepdims=True))
        p = p / jnp.sum(p, axis=-1, keepdims=True)
        o = _dot(p.astype(BF16), v_ref[...].astype(BF16))
        y_ref[...] = (o * _silu_parts(z_ref[...])[0]).astype(BF16)

    return pl.pallas_call(
        body, name=name, grid=(S // T, M_HEADS),
        in_specs=[pl.BlockSpec((T, HEAD), lambda i, h: (i, R_QM // HEAD + h)),
                  pl.BlockSpec((T, HEAD), lambda i, h: (i, R_ZM // HEAD + h)),
                  pl.BlockSpec((N_MEM, HEAD), lambda i, h: (0, h)),
                  pl.BlockSpec((N_MEM, HEAD), lambda i, h: (0, M_HEADS + h))],
        out_specs=pl.BlockSpec((T, HEAD), lambda i, h: (i, h)),
        out_shape=jax.ShapeDtypeStruct((S, A_WIDTH), BF16),
        compiler_params=_cp(("parallel", "parallel")))(ur, ur, mkv, mkv)


def _mem_bwd(ur, mkv, dy, *, name):
    S = ur.shape[0]
    T = min(512, S)

    def body(q_ref, z_ref, k_ref, v_ref, dy_ref, dq_ref, dz_ref, dk_ref, dv_ref):
        i = pl.program_id(1)
        qv = q_ref[...].astype(BF16)
        kv = k_ref[...].astype(BF16)
        vv = v_ref[...].astype(BF16)
        s = _dot_nt(qv, kv) * M_SCALE
        p = jnp.exp(s - jnp.max(s, axis=-1, keepdims=True))
        p = p / jnp.sum(p, axis=-1, keepdims=True)
        o = _dot(p.astype(BF16), vv)
        sz, dsz = _silu_parts(z_ref[...])
        dyv = dy_ref[...]
        dz_ref[...] = (dyv * o * dsz).astype(BF16)
        dov = (dyv * sz).astype(BF16)
        dp = _dot_nt(dov, vv)
        ds = p * (dp - jnp.sum(p * dp, axis=-1, keepdims=True))
        dq_ref[...] = (_dot(ds.astype(BF16), kv) * M_SCALE).astype(BF16)
        dvp = _dot(p.T.astype(BF16), dov)
        dkp = _dot(ds.T.astype(BF16), qv) * M_SCALE

        @pl.when(i == 0)
        def _():
            dk_ref[...] = dkp
            dv_ref[...] = dvp

        @pl.when(i > 0)
        def _():
            dk_ref[...] += dkp
            dv_ref[...] += dvp

    tile = pl.BlockSpec((T, HEAD), lambda h, i: (i, h))
    acc = pl.BlockSpec((N_MEM, HEAD), lambda h, i: (0, h))
    return pl.pallas_call(
        body, name=name, grid=(M_HEADS, S // T),
        in_specs=[pl.BlockSpec((T, HEAD), lambda h, i: (i, R_QM // HEAD + h)),
                  pl.BlockSpec((T, HEAD), lambda h, i: (i, R_ZM // HEAD + h)),
                  pl.BlockSpec((N_MEM, HEAD), lambda h, i: (0, h)),
                  pl.BlockSpec((N_MEM, HEAD), lambda h, i: (0, M_HEADS + h)), tile],
        out_specs=[tile, tile, acc, acc],
        out_shape=[jax.ShapeDtypeStruct((S, A_WIDTH), BF16)] * 2
        + [jax.ShapeDtypeStruct((N_MEM, A_WIDTH), F32)] * 2,
        compiler_params=_cp(("parallel", "arbitrary")))(ur, ur, mkv, mkv, dy)


def _branch_fwd(ys, wbs, ur, b_merge, *, name):
    S = ur.shape[0]
    tm, tn = min(512, S), 512
    nj = D_MODEL // tn

    def body(ya, yb, ym, wa, wb, wm, g0, g1, g2, b0, b1, b2, mg_ref, p_ref):
        acc = jnp.zeros((tm, tn), F32)
        for i, (y, w, gr, br) in enumerate(((ya, wa, g0, b0), (yb, wb, g1, b1), (ym, wm, g2, b2))):
            pr = _dot(y[...], w[...])
            p_ref[i] = pr
            acc = acc + _sigmoid(gr[...] + br[...]) * pr
        mg_ref[...] = acc.astype(BF16)

    yspec = pl.BlockSpec((tm, A_WIDTH), lambda i, j: (i, 0))
    wspec = pl.BlockSpec((A_WIDTH, tn), lambda i, j: (0, j))
    gspec = lambda b: pl.BlockSpec((tm, tn), lambda i, j: (i, (R_GL + b * D_MODEL) // tn + j))
    bspec = lambda b: pl.BlockSpec((1, tn), lambda i, j: (0, b * nj + j))
    return pl.pallas_call(
        body, name=name, grid=(S // tm, nj),
        in_specs=[yspec] * 3 + [wspec] * 3 + [gspec(0), gspec(1), gspec(2), bspec(0), bspec(1), bspec(2)],
        out_specs=[pl.BlockSpec((tm, tn), lambda i, j: (i, j)),
                   pl.BlockSpec((3, tm, tn), lambda i, j: (0, i, j))],
        out_shape=[jax.ShapeDtypeStruct((S, D_MODEL), BF16), jax.ShapeDtypeStruct((3, S, D_MODEL), F32)],
        compiler_params=_cp(("parallel", "parallel")))(*ys, *wbs, ur, ur, ur, b_merge, b_merge, b_merge)


def _branch_bwd(dm, prods, ur, b_merge, *, name):
    S = ur.shape[0]
    tm = min(256, S)

    def body(dm_ref, p_ref, g0, g1, g2, b_ref, dp_ref, dgl_ref, db_ref):
        i = pl.program_id(0)
        dmv = dm_ref[...]
        parts = []
        for b, gr in enumerate((g0, g1, g2)):
            sl = slice(b * D_MODEL, (b + 1) * D_MODEL)
            gt = _sigmoid(gr[...] + b_ref[:, sl])
            dp_ref[b] = (dmv * gt).astype(BF16)
            dgl = dmv * p_ref[b] * gt * (1.0 - gt)
            dgl_ref[:, sl] = dgl.astype(BF16)
            parts.append(jnp.sum(dgl, axis=0, keepdims=True))
        part = jnp.concatenate(parts, axis=1)

        @pl.when(i == 0)
        def _():
            db_ref[...] = part

        @pl.when(i > 0)
        def _():
            db_ref[...] += part

    gspec = lambda b: pl.BlockSpec((tm, D_MODEL), lambda i: (i, R_GL // D_MODEL + b))
    vec = pl.BlockSpec((1, 3 * D_MODEL), lambda i: (0, 0))
    return pl.pallas_call(
        body, name=name, grid=(S // tm,),
        in_specs=[pl.BlockSpec((tm, D_MODEL), lambda i: (i, 0)),
                  pl.BlockSpec((3, tm, D_MODEL), lambda i: (0, i, 0)), gspec(0), gspec(1), gspec(2), vec],
        out_specs=[pl.BlockSpec((3, tm, D_MODEL), lambda i: (0, i, 0)),
                   pl.BlockSpec((tm, 3 * D_MODEL), lambda i: (i, 0)), vec],
        out_shape=[jax.ShapeDtypeStruct((3, S, D_MODEL), BF16), jax.ShapeDtypeStruct((S, 3 * D_MODEL), BF16),
                   jax.ShapeDtypeStruct((1, 3 * D_MODEL), F32)],
        compiler_params=_cp(("arbitrary",)))(dm, prods, ur, ur, ur, b_merge)


def _rope_tables(pos):
    half = ROT // 2
    inv = ROPE_THETA ** (-jnp.arange(half, dtype=F32) / half)
    ang = pos.astype(F32)[:, None] * inv
    cos, sin = jnp.cos(ang), jnp.sin(ang)
    S = pos.shape[0]
    one = jnp.ones((S, LANES - ROT), F32)
    zero = jnp.zeros((S, LANES - ROT), F32)
    zh = jnp.zeros((S, half), F32)
    c = jnp.concatenate([cos, cos, one], axis=1)
    s1 = jnp.concatenate([-sin, zh, zero], axis=1)
    s2 = jnp.concatenate([zh, sin, zero], axis=1)
    return c, s1, s2


def _to_tiles(t):
    S, H = t.shape
    return t.reshape(S // LANES, LANES, H).transpose(0, 2, 1)


def _from_tiles(t):
    nt, H, _ = t.shape
    return t.transpose(1, 0, 2).reshape(H, nt * LANES)


def _local_step(x, mem, pos, tgt, g_pre, g_post, g_mem, wt, bf_pad, b_merge, w_kv, wbs, w_out):
    S = x.shape[0]
    T = min(512, S)
    nq = S // T
    tabs = _rope_tables(pos)

    h = _rms_fwd(x, g_pre, name="rms_pre")
    hs = [_to_classes(h, d) for d in DIL]
    tabs_g = [[_to_classes(t, d) for t in tabs] for d in DIL]
    uas = [_mm(hs[g], wt[f"A{g}"], bt=True, name=f"proj_a{g}", tn=1536) for g in range(3)]
    ub = _mm(h, wt["B"], bt=True, out_dtype=BF16, name="proj_b", tn=1536)
    ur = _mm(h, wt["R"], bt=True, name="proj_r", tn=1792)

    outs_c, lses_c = [], []
    for g in range(3):
        o, l = _attn_a_fwd(uas[g], tabs_g[g], g, name=f"attn_a_fwd{g}")
        outs_c.append(o)
        lses_c.append(l)
    outs_a = [_from_classes(o, d) for o, d in zip(outs_c, DIL)]
    lses_a = [_from_classes(l, d) for l, d in zip(lses_c, DIL)]
    ya = _merge_a_fwd(outs_a, lses_a, ur, name="merge_a_fwd")

    logf = _logf(ur, bf_pad, name="logf")
    c = _from_tiles(_cumsum_lanes(_to_tiles(logf[:, :B_HEADS]), False, name="cumsum_fwd"))
    crow = c.reshape(B_HEADS, nq, 1, T)
    ckb = jnp.broadcast_to(c[:, :, None], (B_HEADS, S, LANES))
    kt = ub[:, 512:1024].reshape(nq, T, 512).transpose(0, 2, 1)
    vt = ub[:, 1024:1536].reshape(nq, T, 512).transpose(0, 2, 1)
    ob, lse_b = _fox_fwd(ub, vt, crow, ckb, name="fox_fwd")
    yb = _gate_fwd(ob, ur, R_ZB, name="gate_b_fwd")

    hm = _rms_fwd(mem, g_mem, name="rms_mem")
    mkv = _mm(hm, w_kv, name="proj_mem")
    ym = _mem_fwd(ur, mkv, name="mem_fwd")

    merged, prods = _branch_fwd((ya, yb, ym), wbs, ur, b_merge, name="branch_fwd")
    out = _mm(merged, w_out, name="proj_out")
    dy, d_out, dg_post, loss_row = _post(x, out, tgt, g_post, name="post")

    dmerged = _mm(d_out, w_out, bt=True, name="d_merged")
    dw_out = _mm(merged.T, d_out, name="dw_out", tk=2048)
    dprods, dgl, db_merge = _branch_bwd(dmerged, prods, ur, b_merge, name="branch_bwd")
    dys, dwbs = [], []
    for i, (y, wb) in enumerate(zip((ya, yb, ym), wbs)):
        dys.append(_mm(dprods[i], wb, bt=True, name=f"d_y{i}"))
        dwbs.append(_mm(y.T, dprods[i], name=f"dw_branch{i}", tk=2048))

    dos_a, adjs_a, dza = _merge_a_bwd(outs_a, lses_a, ur, dys[0], name="merge_a_bwd")
    dus_a = []
    for g, d in enumerate(DIL):
        do_c, adj_c = _to_classes(dos_a[g], d), _to_classes(adjs_a[g], d)
        dq = _attn_a_dq(uas[g], tabs_g[g], g, do_c, lses_c[g], adj_c, name=f"attn_a_dq{g}")
        dk, dv = _attn_a_dkv(uas[g], tabs_g[g], g, do_c, lses_c[g], adj_c, name=f"attn_a_dkv{g}")
        dus_a.append(jnp.concatenate([dq, dk, dv], axis=1))

    dob, dzb = _gate_bwd(ob, ur, R_ZB, dys[1], name="gate_b_bwd")
    delta_b = _fox_delta(ob, dob, name="fox_delta")
    dkb, dvb, dc_k, dqt, dc_q = _fox_bwd(ub, kt, crow, ckb, dob, lse_b, delta_b, name="fox_bwd")
    dqb = (dqt.transpose(0, 2, 1).reshape(S, A_WIDTH) * B_SCALE).astype(BF16)
    du_b = jnp.concatenate([dqb, dkb, dvb], axis=1)
    dc = dc_q.reshape(B_HEADS, S) + dc_k.reshape(B_HEADS, S)
    dlogf = _from_tiles(_cumsum_lanes(_to_tiles(dc.T), True, name="cumsum_bwd"))
    dlogf_pad = jnp.pad(dlogf.T, ((0, 0), (0, FB_PAD - B_HEADS)))
    dfb, db_forget = _dfb(ur, bf_pad, dlogf_pad, name="dfb")

    dqm, dzm, dmk, dmv = _mem_bwd(ur, mkv, dys[2], name="mem_bwd")
    dmkv = jnp.concatenate([dmk, dmv], axis=1).astype(BF16)
    dhm = _mm(dmkv, w_kv, bt=True, name="d_hm")
    dw_kv = _mm(hm.T, dmkv, name="dw_kv")
    dg_mem = _rms_bwd(mem, g_mem, dhm, None, name="rms_mem_bwd")

    du_r = jnp.concatenate([dza, dzb, dqm, dzm, dgl, dfb], axis=1)
    dh = _mm(du_r, wt["R"], name="d_h_r", tk=1792) + _mm(du_b, wt["B"], name="d_h_b", tk=1536)
    for g, d in enumerate(DIL):
        dh = dh + _from_classes(_mm(dus_a[g], wt[f"A{g}"], name=f"d_h_a{g}", tk=1536), d)
    dwt = {"R": _mm(h.T, du_r, name="dw_in_r", tn=1792, tk=1024).T,
           "B": _mm(h.T, du_b, name="dw_in_b", tn=1536, tk=2048).T}
    for g in range(3):
        dwt[f"A{g}"] = _mm(hs[g].T, dus_a[g], name=f"dw_in_a{g}", tn=1536, tk=2048).T
    grad_x, dg_pre = _rms_bwd(x, g_pre, dh, dy, name="rms_pre_bwd")

    return dict(loss=loss_row, grad_x=grad_x, dwt=dwt, dw_kv=dw_kv, dwbs=dwbs, dw_out=dw_out,
                dg_pre=dg_pre, dg_post=dg_post, dg_mem=dg_mem, db_forget=db_forget, db_merge=db_merge)


MESH = pl.DeviceIdType.MESH
ANY = pl.BlockSpec(memory_space=pl.ANY)


def _relations():
    return [(k >> 2 & 1, k >> 1 & 1, k & 1) for k in range(1, N_DEV)]


def _coords():
    return lax.axis_index("x"), lax.axis_index("y"), lax.axis_index("c")


def _all_gather(shard, *, name):
    R, W = shard.shape

    def body(x_ref, out_ref, send_sems, recv_sems, local_sem):
        x, y, c = _coords()
        me, sibling = (x, y, c), (x, y, 1 - c)
        chips = [(1 - x, y), (x, 1 - y), (1 - x, 1 - y)]

        def slot(px, py, pc):
            return out_ref.at[4 * px + 2 * py + pc]

        def copy(k, block, to, src=None):
            return pltpu.make_async_remote_copy(
                src_ref=slot(*block) if src is None else src, dst_ref=slot(*block),
                send_sem=send_sems.at[k], recv_sem=recv_sems.at[k], device_id=to, device_id_type=MESH)

        mine = pltpu.make_async_copy(x_ref, slot(*me), local_sem)
        mine.start()
        first = [copy(0, me, sibling, src=x_ref)]
        first += [copy(1 + j, me, (*chip, c), src=x_ref) for j, chip in enumerate(chips)]
        for cp in first:
            cp.start()
        passed = [copy(4 + j, (*chip, c), sibling) for j, chip in enumerate(chips)]
        for j, chip in enumerate(chips):
            copy(1 + j, (*chip, c), me).wait_recv()
            passed[j].start()
        copy(0, sibling, me).wait_recv()
        for j, chip in enumerate(chips):
            copy(4 + j, (*chip, 1 - c), me).wait_recv()
        for cp in first + passed:
            cp.wait_send()
        mine.wait()

    return pl.pallas_call(
        body, name=name, out_shape=jax.ShapeDtypeStruct((N_DEV, R, W), shard.dtype),
        in_specs=[ANY], out_specs=ANY,
        scratch_shapes=[pltpu.SemaphoreType.DMA((N_DEV - 1,)), pltpu.SemaphoreType.DMA((N_DEV - 1,)),
                        pltpu.SemaphoreType.DMA],
    )(shard)


N_CHIP = 4


def _exchange_pair(gbig, *, name):
    _, R, W = gbig.shape

    def body(g_ref, sib_ref, send_sems, recv_sems):
        x, y, c = _coords()
        copies = []
        for r in range(N_CHIP):
            px, py = x ^ (r >> 1), y ^ (r & 1)
            copies.append(pltpu.make_async_remote_copy(
                src_ref=g_ref.at[4 * px + 2 * py + (1 - c)], dst_ref=sib_ref.at[r],
                send_sem=send_sems.at[r], recv_sem=recv_sems.at[r], device_id=(x, y, 1 - c), device_id_type=MESH))
        for cp in copies:
            cp.start()
        for cp in copies:
            cp.wait_recv()
        for cp in copies:
            cp.wait_send()

    return pl.pallas_call(
        body, name=name, out_shape=jax.ShapeDtypeStruct((N_CHIP, R, W), gbig.dtype),
        in_specs=[ANY], out_specs=ANY,
        scratch_shapes=[pltpu.SemaphoreType.DMA((N_CHIP,)), pltpu.SemaphoreType.DMA((N_CHIP,))],
    )(gbig)


def _own_slabs():
    x, y, c = _coords()
    return jnp.stack([4 * (x ^ (r >> 1)) + 2 * (y ^ (r & 1)) + c for r in range(N_CHIP)]).astype(jnp.int32)


def _pair_sum(gbig, sib, own_idx, tr, *, name):
    _, R, W = gbig.shape

    def body(idx_ref, a_ref, b_ref, o_ref):
        o_ref[...] = (a_ref[...] + b_ref[...]).astype(BF16)

    return pl.pallas_call(
        body, name=name,
        grid_spec=pltpu.PrefetchScalarGridSpec(
            num_scalar_prefetch=1, grid=(N_CHIP - 1, R // tr),
            in_specs=[pl.BlockSpec((None, tr, W), lambda r, i, idx: (idx[r + 1], i, 0)),
                      pl.BlockSpec((None, tr, W), lambda r, i, idx: (r + 1, i, 0))],
            out_specs=pl.BlockSpec((None, tr, W), lambda r, i, idx: (r, i, 0))),
        out_shape=jax.ShapeDtypeStruct((N_CHIP - 1, R, W), BF16),
        compiler_params=_cp(("parallel", "parallel")))(own_idx, gbig, sib)


def _exchange_chips(send, gsmall, *, name):
    nb, R, W = send.shape
    n = N_DEV - 1

    def body(b_ref, s_ref, rb_ref, rs_ref, send_sems, recv_sems, local_sem):
        x, y, c = _coords()
        me = 4 * x + 2 * y + c
        mine = pltpu.make_async_copy(s_ref, rs_ref.at[me], local_sem)
        mine.start()
        started = []
        for k, (fx, fy, fc) in enumerate(_relations()):
            cp = pltpu.make_async_remote_copy(
                src_ref=s_ref, dst_ref=rs_ref.at[me], send_sem=send_sems.at[k], recv_sem=recv_sems.at[k],
                device_id=(x ^ fx, y ^ fy, c ^ fc), device_id_type=MESH)
            cp.start()
            started.append(cp)
        for r in range(1, N_CHIP):
            cp = pltpu.make_async_remote_copy(
                src_ref=b_ref.at[r - 1], dst_ref=rb_ref.at[r - 1], send_sem=send_sems.at[n + r - 1],
                recv_sem=recv_sems.at[n + r - 1], device_id=(x ^ (r >> 1), y ^ (r & 1), c), device_id_type=MESH)
            cp.start()
            started.append(cp)
        for k, (fx, fy, fc) in enumerate(_relations()):
            peer = 4 * (x ^ fx) + 2 * (y ^ fy) + (c ^ fc)
            pltpu.make_async_remote_copy(
                src_ref=s_ref, dst_ref=rs_ref.at[peer], send_sem=send_sems.at[k], recv_sem=recv_sems.at[k],
                device_id=(x ^ fx, y ^ fy, c ^ fc), device_id_type=MESH).wait_recv()
        for r in range(1, N_CHIP):
            pltpu.make_async_remote_copy(
                src_ref=b_ref.at[r - 1], dst_ref=rb_ref.at[r - 1], send_sem=send_sems.at[n + r - 1],
                recv_sem=recv_sems.at[n + r - 1], device_id=(x ^ (r >> 1), y ^ (r & 1), c),
                device_id_type=MESH).wait_recv()
        for cp in started:
            cp.wait_send()
        mine.wait()

    return pl.pallas_call(
        body, name=name,
        out_shape=[jax.ShapeDtypeStruct((nb, R, W), send.dtype),
                   jax.ShapeDtypeStruct((N_DEV, 1, P_SMALL), gsmall.dtype)],
        in_specs=[ANY, ANY], out_specs=[ANY, ANY],
        scratch_shapes=[pltpu.SemaphoreType.DMA((n + nb,)), pltpu.SemaphoreType.DMA((n + nb,)),
                        pltpu.SemaphoreType.DMA],
    )(send, gsmall)


def _part_specs(parts, tr, row0):
    assert row0 % tr == 0
    specs = []
    for a, n_used in parts:
        if n_used is None:
            specs.append(pl.BlockSpec((1, tr, a.shape[2]), lambda i, idx: (idx[0], row0 // tr + i, 0)))
        else:
            specs.append(pl.BlockSpec((n_used, tr, a.shape[2]), lambda i, idx: (0, row0 // tr + i, 0)))
    return specs


def _part_total(refs, parts):
    g = None
    for ref, (_, n_used) in zip(refs, parts):
        for k in range(n_used or 1):
            t = ref[k].astype(F32)
            g = t if g is None else g + t
    return g


def _sum_parts(parts, idx, row0, nrows, tr, *, name):
    W = parts[0][0].shape[2]
    assert nrows % tr == 0

    def body(idx_ref, *refs):
        refs[-1][...] = _part_total(refs[:-1], parts)

    return pl.pallas_call(
        body, name=name,
        grid_spec=pltpu.PrefetchScalarGridSpec(
            num_scalar_prefetch=1, grid=(nrows // tr,), in_specs=_part_specs(parts, tr, row0),
            out_specs=pl.BlockSpec((tr, W), lambda i, idx: (i, 0))),
        out_shape=jax.ShapeDtypeStruct((nrows, W), F32),
        compiler_params=_cp(("parallel",)))(idx, *[a for a, _ in parts])


def _adamw(parts, idx, w, m, v, tr, *, name):
    R, W = w.shape
    assert R % tr == 0
    np_ = len(parts)

    def body(idx_ref, *refs):
        w_ref, m_ref, v_ref, g_ref, d_ref, nm_ref, nv_ref = refs[np_:]
        g = _part_total(refs[:np_], parts)
        mm = ADAM_B1 * m_ref[...] + (1.0 - ADAM_B1) * g
        vv = ADAM_B2 * v_ref[...] + (1.0 - ADAM_B2) * (g * g)
        m_hat = mm / (1.0 - ADAM_B1 ** ADAM_STEP)
        v_hat = vv / (1.0 - ADAM_B2 ** ADAM_STEP)
        g_ref[...] = g
        d_ref[...] = -ADAM_LR * (m_hat / (jnp.sqrt(v_hat) + ADAM_EPS) + ADAM_WD * w_ref[...])
        nm_ref[...] = mm
        nv_ref[...] = vv

    blk = pl.BlockSpec((tr, W), lambda i, idx: (i, 0))
    return pl.pallas_call(
        body, name=name,
        grid_spec=pltpu.PrefetchScalarGridSpec(
            num_scalar_prefetch=1, grid=(R // tr,), in_specs=_part_specs(parts, tr, 0) + [blk, blk, blk],
            out_specs=[blk] * 4),
        out_shape=[jax.ShapeDtypeStruct((R, W), F32)] * 4,
        compiler_params=_cp(("parallel",)))(idx, *[a for a, _ in parts], w, m, v)


def _pack_rest(w_kv, wa, wb, wm, w_out):
    return jnp.concatenate([w_kv[0], w_out[0]] + [t[0].reshape(-1, D_MODEL) for t in (wa, wb, wm)], axis=0)


def _unpack_rest(t):
    br = lambda i: t[RO_BR + 64 * i:RO_BR + 64 * (i + 1)].reshape(1, A_WIDTH, D_MODEL // N_DEV)
    return t[None, RO_KV:RO_OUT], br(0), br(1), br(2), t[None, RO_OUT:RO_BR]


def _orig_rows(gathered, a, b):
    res = []
    while a < b:
        dev, r = divmod(a, CS)
        n = min(b - a, CS - r)
        res.append(gathered[dev, RO_IN + r:RO_IN + r + n])
        a += n
    return res


def _full_weights(gathered):
    wt = {}
    for name, ranges in SEGS.items():
        rows = [p for a, b in ranges for p in _orig_rows(gathered, a, b)]
        if SEG_PAD[name]:
            rows.append(jnp.zeros((SEG_PAD[name], D_MODEL), gathered.dtype))
        wt[name] = jnp.concatenate(rows, axis=0)
    w_kv = gathered[:, RO_KV:RO_OUT].reshape(D_MODEL, D_MODEL)
    w_out = gathered[:, RO_OUT:RO_BR].reshape(D_MODEL, D_MODEL)
    wbs = [gathered[:, RO_BR + 64 * i:RO_BR + 64 * (i + 1)].reshape(N_DEV, A_WIDTH, D_MODEL // N_DEV)
           .transpose(1, 0, 2).reshape(A_WIDTH, D_MODEL) for i in range(3)]
    return wt, w_kv, wbs, w_out


def _orig_order(dwt):
    pieces = []
    for name, ranges in SEGS.items():
        o = 0
        for a, b in ranges:
            pieces.append((a, dwt[name][o:o + b - a]))
            o += b - a
    pieces.sort(key=lambda p: p[0])
    return jnp.concatenate([p[1] for p in pieces], axis=0)


def _pack_grads(dwt, dw_kv, dwbs, dw_out):
    g_in = jnp.pad(_orig_order(dwt).reshape(N_DEV, CS, D_MODEL), ((0, 0), (0, IN_ROWS - CS), (0, 0)))
    br = [t.reshape(A_WIDTH, N_DEV, D_MODEL // N_DEV).transpose(1, 0, 2).reshape(N_DEV, -1, D_MODEL) for t in dwbs]
    return jnp.concatenate([dw_kv.reshape(N_DEV, -1, D_MODEL), dw_out.reshape(N_DEV, -1, D_MODEL)] + br + [g_in],
                           axis=1)


def kernel(x, mem, positions, norm_pre_g, norm_post_g, norm_mem_g, w_in, b_forget, b_merge, w_mem_kv, w_branch_a, w_branch_b, w_branch_m, w_out, loss_target, m_norm_pre_g, m_norm_post_g, m_norm_mem_g, m_w_in, m_b_forget, m_b_merge, m_w_mem_kv, m_w_branch_a, m_w_branch_b, m_w_branch_m, m_w_out, v_norm_pre_g, v_norm_post_g, v_norm_mem_g, v_w_in, v_b_forget, v_b_merge, v_w_mem_kv, v_w_branch_a, v_w_branch_b, v_w_branch_m, v_w_out):
    w_rest = _pack_rest(w_mem_kv, w_branch_a, w_branch_b, w_branch_m, w_out)
    shard = jnp.concatenate([w_rest.astype(BF16), w_in[0].T.astype(BF16),
                             jnp.zeros((IN_ROWS - CS, D_MODEL), BF16)], axis=0)
    gathered = _all_gather(shard, name="gather_weights")
    wt, w_kv, wbs, w_o = _full_weights(gathered)

    bf_pad = jnp.pad(b_forget, ((0, 0), (0, FB_PAD - B_HEADS)))
    r = _local_step(x[0], mem[0], positions[0], loss_target[0], norm_pre_g, norm_post_g, norm_mem_g,
                    wt, bf_pad, b_merge, w_kv, wbs, w_o)

    gbig = _pack_grads(r["dwt"], r["dw_kv"], r["dwbs"], r["dw_out"])
    gsmall = jnp.concatenate([r["dg_pre"], r["dg_post"], r["dg_mem"], r["db_merge"],
                              r["db_forget"][:, :LANES], r["loss"]], axis=1)
    own_idx = _own_slabs()
    sib = _exchange_pair(gbig, name="exchange_pair")
    send = _pair_sum(gbig, sib, own_idx, 208, name="pair_sum")
    recv, rsmall = _exchange_chips(send, gsmall, name="exchange_chips")
    parts = [(gbig, None), (sib, 1), (recv, N_CHIP - 1)]

    m_rest = _pack_rest(m_w_mem_kv, m_w_branch_a, m_w_branch_b, m_w_branch_m, m_w_out)
    v_rest = _pack_rest(v_w_mem_kv, v_w_branch_a, v_w_branch_b, v_w_branch_m, v_w_out)
    outs_rest = [_unpack_rest(t) for t in _adamw(parts, own_idx, w_rest, m_rest, v_rest, 64, name="adamw_rest")]
    g_in = _sum_parts(parts, own_idx, RO_IN, IN_ROWS, 16, name="sum_w_in")[:CS].T
    outs_in = _adamw([(g_in[None], 1)], own_idx, w_in[0], m_w_in[0], v_w_in[0], 128, name="adamw_w_in")

    def small_vec(a, b, c, d, e):
        z = jnp.zeros((1, LANES - B_HEADS), F32)
        return jnp.concatenate([a, b, c, d, e, z, jnp.zeros((1, LANES), F32)], axis=1)

    outs_small = _adamw([(rsmall, N_DEV)], own_idx, small_vec(norm_pre_g, norm_post_g, norm_mem_g, b_merge, b_forget),
                        small_vec(m_norm_pre_g, m_norm_post_g, m_norm_mem_g, m_b_merge, m_b_forget),
                        small_vec(v_norm_pre_g, v_norm_post_g, v_norm_mem_g, v_b_merge, v_b_forget),
                        1, name="adamw_small")

    def small_parts(t):
        return [t[:, O_GPRE:O_GPRE + D_MODEL], t[:, O_GPOST:O_GPOST + D_MODEL], t[:, O_GMEM:O_GMEM + D_MODEL],
                t[:, O_BF:O_BF + B_HEADS], t[:, O_BM:O_BM + 3 * D_MODEL]]

    loss = outs_small[0][0, O_LOSS]
    result = [loss, r["grad_x"][None]]
    for rest, w_i, small in zip(outs_rest, outs_in, outs_small):
        gp, gq, gm, bf, bm = small_parts(small)
        w_k, w_a, w_b, w_m, w_ot = rest
        result += [gp, gq, gm, w_i[None], bf, bm, w_k, w_a, w_b, w_m, w_ot]
    return tuple(result)
```

```python
import jax
import jax.numpy as jnp
from jax import lax
from jax.experimental import pallas as pl
from jax.experimental.pallas import tpu as pltpu

F32 = jnp.float32
BF16 = jnp.bfloat16

N_DEV = 8
D_MODEL = 1024
N_MEM = 256
EPS = 1e-6
NEG = -1e30
ROPE_THETA = 500000.0
DIL = (1, 4, 16)
A_HEADS = 4
HEAD = 128
A_WIDTH = 512
B_HEADS = 8
B_HEAD = 64
M_HEADS = 4
ROT = 32
IN_COLS = 11272
FB_PAD = 256

SEGS = {
    "A0": ((0, 512), (1536, 2048), (3072, 3584)),
    "A1": ((512, 1024), (2048, 2560), (3584, 4096)),
    "A2": ((1024, 1536), (2560, 3072), (4096, 4608)),
    "B": ((5120, 6656),),
    "R": ((4608, 5120), (6664, 7176), (7176, 7688), (7688, 8200), (8200, 11272), (6656, 6664)),
}
SEG_PAD = {"A0": 0, "A1": 0, "A2": 0, "B": 0, "R": FB_PAD - B_HEADS}
R_ZA, R_ZB, R_QM, R_ZM, R_GL, R_FB = 0, 512, 1024, 1536, 2048, 5120
NR = R_FB + FB_PAD

ADAM_LR, ADAM_B1, ADAM_B2, ADAM_EPS, ADAM_WD, ADAM_STEP = 0.001, 0.9, 0.999, 1e-08, 0.01, 10

LANES = 128
VMEM_LIMIT = 56 * 1024 * 1024

CS = IN_COLS // N_DEV
RO_KV, RO_OUT, RO_BR, RO_IN = 0, 128, 256, 448
IN_ROWS = 1424
ROWS = RO_IN + IN_ROWS
O_GPRE, O_GPOST, O_GMEM, O_BM, O_BF, O_LOSS = 0, 1024, 2048, 3072, 6144, 6272
P_SMALL = 6400


def _cp(sem=None):
    return pltpu.CompilerParams(dimension_semantics=sem, vmem_limit_bytes=VMEM_LIMIT)


def _dot(a, b):
    return jnp.dot(a, b, preferred_element_type=F32)


def _dot_nt(a, b):
    return lax.dot_general(a, b, (((1,), (1,)), ((), ())), preferred_element_type=F32)


def _sigmoid(z):
    return 1.0 / (1.0 + jnp.exp(-z))


def _mm(a, b, *, name, bt=False, out_dtype=F32, tm=1024, tn=1024, tk=None):
    M, K = a.shape
    N = b.shape[0] if bt else b.shape[1]
    tm, tn = min(tm, M), min(tn, N)
    tk = K if tk is None else min(tk, K)
    assert M % tm == 0 and N % tn == 0 and K % tk == 0
    nk = K // tk

    def body(a_ref, b_ref, o_ref, acc_ref):
        av = a_ref[...].astype(BF16)
        bv = b_ref[...].astype(BF16)
        p = _dot_nt(av, bv) if bt else _dot(av, bv)
        if nk == 1:
            o_ref[...] = p.astype(out_dtype)
        else:
            k = pl.program_id(2)

            @pl.when(k == 0)
            def _():
                acc_ref[...] = p

            @pl.when(k > 0)
            def _():
                acc_ref[...] += p

            @pl.when(k == nk - 1)
            def _():
                o_ref[...] = acc_ref[...].astype(out_dtype)

    b_spec = (pl.BlockSpec((tn, tk), lambda i, j, k: (j, k)) if bt
              else pl.BlockSpec((tk, tn), lambda i, j, k: (k, j)))
    return pl.pallas_call(
        body, name=name, grid=(M // tm, N // tn, nk),
        in_specs=[pl.BlockSpec((tm, tk), lambda i, j, k: (i, k)), b_spec],
        out_specs=pl.BlockSpec((tm, tn), lambda i, j, k: (i, j)),
        out_shape=jax.ShapeDtypeStruct((M, N), out_dtype),
        scratch_shapes=[pltpu.VMEM((tm, tn) if nk > 1 else (8, LANES), F32)],
        compiler_params=_cp(("parallel", "parallel", "arbitrary")),
    )(a, b)


def _rms_fwd(x, g, *, name):
    S, D = x.shape
    tm = min(512, S)

    def body(x_ref, g_ref, o_ref):
        xv = x_ref[...]
        r = lax.rsqrt(jnp.mean(xv * xv, axis=-1, keepdims=True) + EPS)
        o_ref[...] = (xv * r * g_ref[...]).astype(BF16)

    return pl.pallas_call(
        body, name=name, grid=(S // tm,),
        in_specs=[pl.BlockSpec((tm, D), lambda i: (i, 0)), pl.BlockSpec((1, D), lambda i: (0, 0))],
        out_specs=pl.BlockSpec((tm, D), lambda i: (i, 0)),
        out_shape=jax.ShapeDtypeStruct((S, D), BF16),
        compiler_params=_cp(("parallel",)),
    )(x, g)


def _rms_bwd(x, g, dh, dy, *, name):
    S, D = x.shape
    tm = min(512, S)
    want_dx = dy is not None

    def body(*refs):
        if want_dx:
            x_ref, g_ref, dh_ref, dy_ref, dx_ref, dg_ref = refs
        else:
            x_ref, g_ref, dh_ref, dg_ref = refs
        i = pl.program_id(0)
        xv = x_ref[...]
        r = lax.rsqrt(jnp.mean(xv * xv, axis=-1, keepdims=True) + EPS)
        xh = xv * r
        dhv = dh_ref[...]
        part = jnp.sum(dhv * xh, axis=0, keepdims=True)

        @pl.when(i == 0)
        def _():
            dg_ref[...] = part

        @pl.when(i > 0)
        def _():
            dg_ref[...] += part

        if want_dx:
            dxh = dhv * g_ref[...]
            dx_ref[...] = dy_ref[...] + r * (dxh - xh * jnp.mean(dxh * xh, axis=-1, keepdims=True))

    row = pl.BlockSpec((tm, D), lambda i: (i, 0))
    vec = pl.BlockSpec((1, D), lambda i: (0, 0))
    if want_dx:
        return pl.pallas_call(
            body, name=name, grid=(S // tm,), in_specs=[row, vec, row, row], out_specs=[row, vec],
            out_shape=[jax.ShapeDtypeStruct((S, D), F32), jax.ShapeDtypeStruct((1, D), F32)],
            compiler_params=_cp(("arbitrary",)))(x, g, dh, dy)
    return pl.pallas_call(
        body, name=name, grid=(S // tm,), in_specs=[row, vec, row], out_specs=vec,
        out_shape=jax.ShapeDtypeStruct((1, D), F32),
        compiler_params=_cp(("arbitrary",)))(x, g, dh)


def _post(x, out, tgt, g, *, name):
    S, D = x.shape
    tm = min(512, S)

    def body(x_ref, o_ref, t_ref, g_ref, dy_ref, do_ref, dg_ref, loss_ref):
        i = pl.program_id(0)
        ov = o_ref[...]
        r = lax.rsqrt(jnp.mean(ov * ov, axis=-1, keepdims=True) + EPS)
        n = ov * r
        gv = g_ref[...]
        e = (x_ref[...] + n * gv) - t_ref[...]
        lpart = 0.5 * jnp.sum(jnp.mean(e * e, axis=-1, keepdims=True), axis=0, keepdims=True)
        dy = e * (1.0 / D)
        dy_ref[...] = dy
        dn = dy * gv
        do_ref[...] = (r * (dn - n * jnp.mean(dn * n, axis=-1, keepdims=True))).astype(BF16)
        gpart = jnp.sum(dy * n, axis=0, keepdims=True)
        lrow = jnp.broadcast_to(lpart, (1, LANES))

        @pl.when(i == 0)
        def _():
            dg_ref[...] = gpart
            loss_ref[...] = lrow

        @pl.when(i > 0)
        def _():
            dg_ref[...] += gpart
            loss_ref[...] += lrow

    row = pl.BlockSpec((tm, D), lambda i: (i, 0))
    vec = pl.BlockSpec((1, D), lambda i: (0, 0))
    return pl.pallas_call(
        body, name=name, grid=(S // tm,), in_specs=[row, row, row, vec],
        out_specs=[row, row, vec, pl.BlockSpec((1, LANES), lambda i: (0, 0))],
        out_shape=[jax.ShapeDtypeStruct((S, D), F32), jax.ShapeDtypeStruct((S, D), BF16),
                   jax.ShapeDtypeStruct((1, D), F32), jax.ShapeDtypeStruct((1, LANES), F32)],
        compiler_params=_cp(("arbitrary",)))(x, out, tgt, g)


def _to_classes(t, d):
    if d == 1:
        return t
    S, C = t.shape
    return t.reshape(S // d, d, C).transpose(1, 0, 2).reshape(S, C)


def _from_classes(t, d):
    if d == 1:
        return t
    S, C = t.shape
    return t.reshape(d, S // d, C).transpose(1, 0, 2).reshape(S, C)


def _rope(x, c, s1, s2):
    return x * c + pltpu.roll(x, LANES - ROT // 2, 1) * s1 + pltpu.roll(x, ROT // 2, 1) * s2


def _unrope(d, c, s1, s2):
    return d * c + pltpu.roll(d * s1, ROT // 2, 1) + pltpu.roll(d * s2, LANES - ROT // 2, 1)


def _a_masks():
    qi = lax.broadcasted_iota(jnp.int32, (HEAD, HEAD), 0)
    ki = lax.broadcasted_iota(jnp.int32, (HEAD, HEAD), 1)
    return ki >= qi, ki <= qi


A_SCALE = HEAD ** -0.5


def _a_geometry(S, g):
    d = DIL[g]
    L = S // d
    TQ = min(512, L)
    return d, L, TQ, TQ // HEAD, L // TQ, L // HEAD


def _rope_cast(ua, tabs, *, name):
    S = ua.shape[0]
    tm = min(512, S)

    def body(u_ref, c_ref, s1_ref, s2_ref, o_ref):
        tc = (c_ref[...], s1_ref[...], s2_ref[...])
        for j in range(3 * A_HEADS):
            sl = slice(j * HEAD, (j + 1) * HEAD)
            t = u_ref[:, sl]
            o_ref[:, sl] = (_rope(t, *tc) if j < 2 * A_HEADS else t).astype(BF16)

    blk = pl.BlockSpec((tm, 3 * A_WIDTH), lambda i: (i, 0))
    tab = pl.BlockSpec((tm, LANES), lambda i: (i, 0))
    return pl.pallas_call(
        body, name=name, grid=(S // tm,), in_specs=[blk, tab, tab, tab], out_specs=blk,
        out_shape=jax.ShapeDtypeStruct((S, 3 * A_WIDTH), BF16),
        compiler_params=_cp(("parallel",)))(ua, *tabs)


def _attn_a_fwd(qkv, g, *, name):
    S = qkv.shape[0]
    d, L, TQ, nsub, nb, nblk = _a_geometry(S, g)

    def body(q_ref, kc_ref, kp_ref, vc_ref, vp_ref, o_ref, l_ref):
        n = pl.program_id(1)
        q, kc, kp, vc, vp = q_ref[...], kc_ref[...], kp_ref[...], vc_ref[...], vp_ref[...]
        mprev, mcur = _a_masks()
        for a in range(nsub):
            sl = slice(a * HEAD, (a + 1) * HEAD)
            pv = slice((a - 1) * HEAD, a * HEAD)
            qa = q[sl]
            k_prev, v_prev = (kp, vp) if a == 0 else (kc[pv], vc[pv])
            mp = jnp.logical_and(mprev, n > 0) if a == 0 else mprev
            s_p = jnp.where(mp, _dot_nt(qa, k_prev) * A_SCALE, NEG)
            s_c = jnp.where(mcur, _dot_nt(qa, kc[sl]) * A_SCALE, NEG)
            m = jnp.max(jnp.maximum(s_p, s_c), axis=-1, keepdims=True)
            p_p = jnp.exp(s_p - m)
            p_c = jnp.exp(s_c - m)
            den = jnp.sum(p_p + p_c, axis=-1, keepdims=True)
            o = (_dot(p_p.astype(BF16), v_prev) + _dot(p_c.astype(BF16), vc[sl])) / den
            o_ref[sl, :] = o
            l_ref[sl, :] = jnp.broadcast_to(m + jnp.log(den), (HEAD, HEAD))

    rcur = lambda cb, n: (cb // A_HEADS) * nb + n
    rprv = lambda cb, n: (cb // A_HEADS) * nblk + jnp.maximum(n * nsub - 1, 0)
    cur = lambda off: pl.BlockSpec((TQ, HEAD), lambda cb, n: (rcur(cb, n), off + cb % A_HEADS))
    prv = lambda off: pl.BlockSpec((HEAD, HEAD), lambda cb, n: (rprv(cb, n), off + cb % A_HEADS))
    out = pl.BlockSpec((TQ, HEAD), lambda cb, n: (rcur(cb, n), cb % A_HEADS))
    return pl.pallas_call(
        body, name=name, grid=(A_HEADS * d, nb),
        in_specs=[cur(0), cur(4), prv(4), cur(8), prv(8)],
        out_specs=[out, out],
        out_shape=[jax.ShapeDtypeStruct((S, A_WIDTH), F32)] * 2,
        compiler_params=_cp(("parallel", "parallel")),
    )(qkv, qkv, qkv, qkv, qkv)


def _attn_a_dq(qkv, tabs, g, do, lse, adj, *, name):
    S = qkv.shape[0]
    d, L, TQ, nsub, nb, nblk = _a_geometry(S, g)

    def body(q_ref, kc_ref, kp_ref, vc_ref, vp_ref, do_ref, l_ref, adj_ref, c_ref, s1_ref, s2_ref, dq_ref):
        n = pl.program_id(1)
        q, kc, kp, vc, vp = q_ref[...], kc_ref[...], kp_ref[...], vc_ref[...], vp_ref[...]
        mprev, mcur = _a_masks()
        for a in range(nsub):
            sl = slice(a * HEAD, (a + 1) * HEAD)
            pv = slice((a - 1) * HEAD, a * HEAD)
            qa = q[sl]
            k_prev, v_prev = (kp, vp) if a == 0 else (kc[pv], vc[pv])
            mp = jnp.logical_and(mprev, n > 0) if a == 0 else mprev
            lse_a = l_ref[sl, :][:, :1]
            adj_a = adj_ref[sl, :][:, :1]
            doa = do_ref[sl, :]
            p_p = jnp.exp(jnp.where(mp, _dot_nt(qa, k_prev) * A_SCALE, NEG) - lse_a)
            p_c = jnp.exp(jnp.where(mcur, _dot_nt(qa, kc[sl]) * A_SCALE, NEG) - lse_a)
            ds_p = p_p * (_dot_nt(doa, v_prev) + adj_a)
            ds_c = p_c * (_dot_nt(doa, vc[sl]) + adj_a)
            dq = (_dot(ds_p.astype(BF16), k_prev) + _dot(ds_c.astype(BF16), kc[sl])) * A_SCALE
            dq_ref[sl, :] = _unrope(dq, c_ref[sl, :], s1_ref[sl, :], s2_ref[sl, :]).astype(BF16)

    rcur = lambda cb, n: (cb // A_HEADS) * nb + n
    rprv = lambda cb, n: (cb // A_HEADS) * nblk + jnp.maximum(n * nsub - 1, 0)
    cur = lambda off: pl.BlockSpec((TQ, HEAD), lambda cb, n: (rcur(cb, n), off + cb % A_HEADS))
    prv = lambda off: pl.BlockSpec((HEAD, HEAD), lambda cb, n: (rprv(cb, n), off + cb % A_HEADS))
    tcur = pl.BlockSpec((TQ, LANES), lambda cb, n: (rcur(cb, n), 0))
    blk = pl.BlockSpec((TQ, HEAD), lambda cb, n: (rcur(cb, n), cb % A_HEADS))
    return pl.pallas_call(
        body, name=name, grid=(A_HEADS * d, nb),
        in_specs=[cur(0), cur(4), prv(4), cur(8), prv(8), blk, blk, blk, tcur, tcur, tcur],
        out_specs=blk,
        out_shape=jax.ShapeDtypeStruct((S, A_WIDTH), BF16),
        compiler_params=_cp(("parallel", "parallel")),
    )(qkv, qkv, qkv, qkv, qkv, do, lse, adj, *tabs)


def _attn_a_dkv(qkv, tabs, g, do, lse, adj, *, name):
    S = qkv.shape[0]
    d, L, TQ, nsub, nb, nblk = _a_geometry(S, g)

    def body(qc_ref, qn_ref, kc_ref, vc_ref, doc_ref, don_ref, lc_ref, ln_ref, ac_ref, an_ref,
             c_ref, s1_ref, s2_ref, dk_ref, dv_ref):
        n = pl.program_id(1)
        qc, qn, kc, vc = qc_ref[...], qn_ref[...], kc_ref[...], vc_ref[...]
        kr = lax.broadcasted_iota(jnp.int32, (HEAD, HEAD), 0)
        qc_i = lax.broadcasted_iota(jnp.int32, (HEAD, HEAD), 1)
        own_t = kr <= qc_i
        nxt_t = kr >= qc_i
        has_next = n < nb - 1
        for b in range(nsub):
            sl = slice(b * HEAD, (b + 1) * HEAD)
            nx = slice((b + 1) * HEAD, (b + 2) * HEAD)
            kb, vb = kc[sl], vc[sl]
            last = b == nsub - 1
            parts = [(qc[sl], doc_ref[sl, :], lc_ref[sl, :], ac_ref[sl, :], own_t)]
            if last:
                parts.append((qn, don_ref[...], ln_ref[...], an_ref[...], jnp.logical_and(nxt_t, has_next)))
            else:
                parts.append((qc[nx], doc_ref[nx, :], lc_ref[nx, :], ac_ref[nx, :], nxt_t))
            dk = jnp.zeros((HEAD, HEAD), F32)
            dv = jnp.zeros((HEAD, HEAD), F32)
            for qq, dd, ll, aa, msk in parts:
                st = jnp.where(msk, _dot_nt(kb, qq) * A_SCALE, NEG)
                pt = jnp.exp(st - ll.T)
                dv = dv + _dot(pt.astype(BF16), dd)
                dst = pt * (_dot_nt(vb, dd) + aa.T)
                dk = dk + _dot(dst.astype(BF16), qq)
            dk = dk * A_SCALE
            dk_ref[sl, :] = _unrope(dk, c_ref[sl, :], s1_ref[sl, :], s2_ref[sl, :]).astype(BF16)
            dv_ref[sl, :] = dv.astype(BF16)

    rcur = lambda cb, n: (cb // A_HEADS) * nb + n
    rnxt = lambda cb, n: (cb // A_HEADS) * nblk + jnp.minimum((n + 1) * nsub, nblk - 1)
    cur = lambda off: pl.BlockSpec((TQ, HEAD), lambda cb, n: (rcur(cb, n), off + cb % A_HEADS))
    nxu = lambda off: pl.BlockSpec((HEAD, HEAD), lambda cb, n: (rnxt(cb, n), off + cb % A_HEADS))
    tcur = pl.BlockSpec((TQ, LANES), lambda cb, n: (rcur(cb, n), 0))
    blk = pl.BlockSpec((TQ, HEAD), lambda cb, n: (rcur(cb, n), cb % A_HEADS))
    bnx = pl.BlockSpec((HEAD, HEAD), lambda cb, n: (rnxt(cb, n), cb % A_HEADS))
    return pl.pallas_call(
        body, name=name, grid=(A_HEADS * d, nb),
        in_specs=[cur(0), nxu(0), cur(4), cur(8), blk, bnx, blk, bnx, blk, bnx, tcur, tcur, tcur],
        out_specs=[blk, blk],
        out_shape=[jax.ShapeDtypeStruct((S, A_WIDTH), BF16)] * 2,
        compiler_params=_cp(("parallel", "parallel")),
    )(qkv, qkv, qkv, qkv, do, do, lse, lse, adj, adj, *tabs)


def _silu_parts(z):
    sg = _sigmoid(z)
    return z * sg, sg * (1.0 + z * (1.0 - sg))


def _merge_a_fwd(os_, ls_, ur, *, name):
    S = ur.shape[0]
    tm = min(512, S)

    def body(o0, o1, o2, l0, l1, l2, z_ref, y_ref):
        ls = [l0[...], l1[...], l2[...]]
        mx = jnp.maximum(jnp.maximum(ls[0], ls[1]), ls[2])
        es = [jnp.exp(l - mx) for l in ls]
        den = es[0] + es[1] + es[2]
        y = (es[0] / den) * o0[...] + (es[1] / den) * o1[...] + (es[2] / den) * o2[...]
        y_ref[...] = (y * _silu_parts(z_ref[...])[0]).astype(BF16)

    blk = pl.BlockSpec((tm, A_WIDTH), lambda i: (i, 0))
    return pl.pallas_call(
        body, name=name, grid=(S // tm,),
        in_specs=[blk] * 6 + [pl.BlockSpec((tm, A_WIDTH), lambda i: (i, R_ZA // A_WIDTH))],
        out_specs=blk, out_shape=jax.ShapeDtypeStruct((S, A_WIDTH), BF16),
        compiler_params=_cp(("parallel",)))(*os_, *ls_, ur)


def _merge_a_bwd(os_, ls_, ur, dya, *, name):
    S = ur.shape[0]
    tm = min(256, S)

    def body(o0, o1, o2, l0, l1, l2, z_ref, dy_ref, d0, d1, d2, a0, a1, a2, dz_ref):
        ls = [l0[...], l1[...], l2[...]]
        ov = [o0[...], o1[...], o2[...]]
        mx = jnp.maximum(jnp.maximum(ls[0], ls[1]), ls[2])
        es = [jnp.exp(l - mx) for l in ls]
        den = es[0] + es[1] + es[2]
        ws = [e / den for e in es]
        y = ws[0] * ov[0] + ws[1] * ov[1] + ws[2] * ov[2]
        sz, dsz = _silu_parts(z_ref[...])
        dyv = dy_ref[...]
        dz_ref[...] = (dyv * y * dsz).astype(BF16)
        dyp = dyv * sz
        for h in range(A_HEADS):
            sl = slice(h * HEAD, (h + 1) * HEAD)
            t = jnp.zeros((tm, 1), F32)
            for gi in range(3):
                t = t + ws[gi][:, sl][:, :1] * jnp.sum(dyp[:, sl] * ov[gi][:, sl], axis=-1, keepdims=True)
            for gi, (dref, aref) in enumerate(((d0, a0), (d1, a1), (d2, a2))):
                wg = ws[gi][:, sl]
                dref[:, sl] = (wg * dyp[:, sl]).astype(BF16)
                aref[:, sl] = -wg * t

    blk = pl.BlockSpec((tm, A_WIDTH), lambda i: (i, 0))
    outs = pl.pallas_call(
        body, name=name, grid=(S // tm,),
        in_specs=[blk] * 6 + [pl.BlockSpec((tm, A_WIDTH), lambda i: (i, R_ZA // A_WIDTH)), blk],
        out_specs=[blk] * 7,
        out_shape=[jax.ShapeDtypeStruct((S, A_WIDTH), BF16)] * 3
        + [jax.ShapeDtypeStruct((S, A_WIDTH), F32)] * 3 + [jax.ShapeDtypeStruct((S, A_WIDTH), BF16)],
        compiler_params=_cp(("parallel",)))(*os_, *ls_, ur, dya)
    return outs[0:3], outs[3:6], outs[6]


def _logf(ur, bf_pad, *, name):
    S = ur.shape[0]
    tm = min(1024, S)

    def body(u_ref, b_ref, o_ref):
        z = u_ref[...] + b_ref[...]
        o_ref[...] = jnp.minimum(z, 0.0) - jnp.log(1.0 + jnp.exp(-jnp.abs(z)))

    return pl.pallas_call(
        body, name=name, grid=(S // tm,),
        in_specs=[pl.BlockSpec((tm, FB_PAD), lambda i: (i, R_FB // FB_PAD)),
                  pl.BlockSpec((1, FB_PAD), lambda i: (0, 0))],
        out_specs=pl.BlockSpec((tm, FB_PAD), lambda i: (i, 0)),
        out_shape=jax.ShapeDtypeStruct((S, FB_PAD), F32),
        compiler_params=_cp(("parallel",)))(ur, bf_pad)


def _cumsum_lanes(x, reverse, *, name):
    nt, H, _ = x.shape

    def body(x_ref, o_ref):
        lane = lax.broadcasted_iota(jnp.int32, (H, LANES), 1)

        def tile(t, carry):
            tt = nt - 1 - t if reverse else t
            v = x_ref[tt]
            k = 1
            while k < LANES:
                if reverse:
                    v = v + jnp.where(lane < LANES - k, pltpu.roll(v, LANES - k, 1), 0.0)
                else:
                    v = v + jnp.where(lane >= k, pltpu.roll(v, k, 1), 0.0)
                k *= 2
            v = v + carry
            o_ref[tt] = v
            edge = v[:, :1] if reverse else v[:, LANES - 1:]
            return jnp.broadcast_to(edge, (H, LANES))

        lax.fori_loop(0, nt, tile, jnp.zeros((H, LANES), F32))

    return pl.pallas_call(
        body, name=name, out_shape=jax.ShapeDtypeStruct((nt, H, LANES), F32),
        in_specs=[pl.BlockSpec(memory_space=pltpu.VMEM)], out_specs=pl.BlockSpec(memory_space=pltpu.VMEM),
        compiler_params=_cp())(x)


B_SCALE = B_HEAD ** -0.5


def _pair_masks():
    lane = lax.broadcasted_iota(jnp.int32, (1, LANES), 1)
    row = lax.broadcasted_iota(jnp.int32, (LANES, 1), 0)
    return (lane < B_HEAD, lane >= B_HEAD), (row < B_HEAD, row >= B_HEAD)


def _causal_t(T):
    r = lax.broadcasted_iota(jnp.int32, (T, T), 0)
    c = lax.broadcasted_iota(jnp.int32, (T, T), 1)
    return r <= c


def _zero_other(x, keep):
    return jnp.where(keep, x, jnp.zeros_like(x))


def _fox_fwd(ub, vt, crow, ckb, *, name):
    S = ub.shape[0]
    T = min(512, S)
    nq = S // T

    def body(q_ref, k_ref, vt_ref, cr_ref, ck_ref, o_ref, l_ref, m_s, l_s, acc_s):
        i = pl.program_id(1)
        lanes, rows = _pair_masks()
        q = q_ref[...] * B_SCALE
        qm = [_zero_other(q, lanes[0]), _zero_other(q, lanes[1])]
        m_s[...] = jnp.full((2, 1, T), NEG, F32)
        l_s[...] = jnp.zeros((2, 1, T), F32)
        acc_s[...] = jnp.zeros((LANES, T), F32)

        def step(j, masked):
            off = pl.multiple_of(j * T, T)
            kj = k_ref[pl.ds(off, T), :]
            vtj = vt_ref[j]
            upd = jnp.zeros((LANES, T), F32)
            alphas = []
            for a in range(2):
                st = _dot_nt(kj, qm[a]) + (cr_ref[a, i] - jnp.tile(ck_ref[a, pl.ds(off, T), :], (1, T // LANES)))
                if masked:
                    st = jnp.where(_causal_t(T), st, NEG)
                m_old = m_s[a]
                m_new = jnp.maximum(m_old, jnp.max(st, axis=0, keepdims=True))
                alpha = jnp.exp(m_old - m_new)
                pt = jnp.exp(st - m_new)
                l_s[a] = alpha * l_s[a] + jnp.sum(pt, axis=0, keepdims=True)
                m_s[a] = m_new
                upd = upd + _dot(_zero_other(vtj, rows[a]), pt.astype(BF16))
                alphas.append(alpha)
            acc_s[...] = acc_s[...] * jnp.where(rows[0], alphas[0], alphas[1]) + upd

        def loop(j, carry):
            step(j, False)
            return carry

        lax.fori_loop(0, i, loop, 0)
        step(i, True)
        o_ref[...] = (acc_s[...] / jnp.where(rows[0], l_s[0], l_s[1])).T
        l_ref[0] = m_s[0] + jnp.log(l_s[0])
        l_ref[1] = m_s[1] + jnp.log(l_s[1])

    stat = pl.BlockSpec((2, None, 1, T), lambda h, i: (h, i, 0, 0))
    return pl.pallas_call(
        body, name=name, grid=(B_HEADS // 2, nq),
        in_specs=[pl.BlockSpec((T, LANES), lambda h, i: (i, h)),
                  pl.BlockSpec((S, LANES), lambda h, i: (0, 4 + h)),
                  pl.BlockSpec((nq, LANES, T), lambda h, i: (0, h, 0)),
                  pl.BlockSpec((2, nq, 1, T), lambda h, i: (h, 0, 0, 0)),
                  pl.BlockSpec((2, S, LANES), lambda h, i: (h, 0, 0))],
        out_specs=[pl.BlockSpec((T, LANES), lambda h, i: (i, h)), stat],
        out_shape=[jax.ShapeDtypeStruct((S, A_WIDTH), F32), jax.ShapeDtypeStruct((B_HEADS, nq, 1, T), F32)],
        scratch_shapes=[pltpu.VMEM((2, 1, T), F32), pltpu.VMEM((2, 1, T), F32), pltpu.VMEM((LANES, T), F32)],
        compiler_params=_cp(("parallel", "parallel")),
    )(ub, ub, vt, crow, ckb)


def _fox_delta(o, do, *, name):
    S = o.shape[0]
    T = min(512, S)
    nq = S // T

    def body(o_ref, do_ref, d_ref):
        _, rows = _pair_masks()
        prod_t = (do_ref[...].astype(F32) * o_ref[...]).T
        d_ref[0] = jnp.sum(_zero_other(prod_t, rows[0]), axis=0, keepdims=True)
        d_ref[1] = jnp.sum(_zero_other(prod_t, rows[1]), axis=0, keepdims=True)

    tile = pl.BlockSpec((T, LANES), lambda h, i: (i, h))
    return pl.pallas_call(
        body, name=name, grid=(B_HEADS // 2, nq), in_specs=[tile, tile],
        out_specs=pl.BlockSpec((2, None, 1, T), lambda h, i: (h, i, 0, 0)),
        out_shape=jax.ShapeDtypeStruct((B_HEADS, nq, 1, T), F32),
        compiler_params=_cp(("parallel", "parallel")))(o, do)


def _fox_bwd(ub, kt, crow, ckb, do, lse, delta, *, name):
    S = ub.shape[0]
    T = min(512, S)
    nq = S // T

    def body(k_ref, v_ref, kt_ref, q_ref, do_ref, cr_ref, ck_ref, l_ref, dl_ref,
             dk_ref, dv_ref, dck_ref, dqt_ref, dcq_ref, dk_s, dv_s, dc_s):
        j = pl.program_id(1)
        lanes, rows = _pair_masks()
        kv = k_ref[...]
        vv = v_ref[...]
        ktj = kt_ref[...]
        km = [_zero_other(kv, lanes[0]), _zero_other(kv, lanes[1])]
        ktm = [_zero_other(ktj, rows[0]), _zero_other(ktj, rows[1])]
        ck = [jnp.tile(ck_ref[a], (1, T // LANES)) for a in range(2)]
        dk_s[...] = jnp.zeros((T, LANES), F32)
        dv_s[...] = jnp.zeros((T, LANES), F32)
        dc_s[...] = jnp.zeros((2, T, 1), F32)

        @pl.when(j == 0)
        def _():
            dqt_ref[...] = jnp.zeros((nq, LANES, T), F32)
            dcq_ref[...] = jnp.zeros((2, nq, 1, T), F32)

        def step(i, masked):
            off = pl.multiple_of(i * T, T)
            qi = q_ref[pl.ds(off, T), :] * B_SCALE
            doi = do_ref[pl.ds(off, T), :]
            upd = jnp.zeros((LANES, T), F32)
            for a in range(2):
                st = _dot_nt(km[a], qi) + (cr_ref[a, i] - ck[a])
                if masked:
                    st = jnp.where(_causal_t(T), st, NEG)
                pt = jnp.exp(st - l_ref[a, i])
                doa = _zero_other(doi, lanes[a])
                dv_s[...] += _dot(pt.astype(BF16), doa)
                dst = pt * (_dot_nt(vv, doa) - dl_ref[a, i])
                dsb = dst.astype(BF16)
                dk_s[...] += _dot(dsb, _zero_other(qi, lanes[a]))
                upd = upd + _dot(ktm[a], dsb)
                dc_s[a] -= jnp.sum(dst, axis=-1, keepdims=True)
                dcq_ref[a, i] += jnp.sum(dst, axis=0, keepdims=True)
            dqt_ref[i] += upd

        def loop(i, carry):
            step(i, False)
            return carry

        step(j, True)
        lax.fori_loop(j + 1, nq, loop, 0)
        dk_ref[...] = dk_s[...].astype(BF16)
        dv_ref[...] = dv_s[...].astype(BF16)
        dck_ref[...] = dc_s[...]

    rowv = pl.BlockSpec((2, nq, 1, T), lambda h, j: (h, 0, 0, 0))
    tile = pl.BlockSpec((T, LANES), lambda h, j: (j, h))
    return pl.pallas_call(
        body, name=name, grid=(B_HEADS // 2, nq),
        in_specs=[pl.BlockSpec((T, LANES), lambda h, j: (j, 4 + h)),
                  pl.BlockSpec((T, LANES), lambda h, j: (j, 8 + h)),
                  pl.BlockSpec((None, LANES, T), lambda h, j: (j, h, 0)),
                  pl.BlockSpec((S, LANES), lambda h, j: (0, h)),
                  pl.BlockSpec((S, LANES), lambda h, j: (0, h)),
                  rowv,
                  pl.BlockSpec((2, T, LANES), lambda h, j: (h, j, 0)),
                  rowv, rowv],
        out_specs=[tile, tile, pl.BlockSpec((2, T, 1), lambda h, j: (h, j, 0)),
                   pl.BlockSpec((nq, LANES, T), lambda h, j: (0, h, 0)), rowv],
        out_shape=[jax.ShapeDtypeStruct((S, A_WIDTH), BF16)] * 2 + [jax.ShapeDtypeStruct((B_HEADS, S, 1), F32),
                   jax.ShapeDtypeStruct((nq, A_WIDTH, T), F32), jax.ShapeDtypeStruct((B_HEADS, nq, 1, T), F32)],
        scratch_shapes=[pltpu.VMEM((T, LANES), F32), pltpu.VMEM((T, LANES), F32), pltpu.VMEM((2, T, 1), F32)],
        compiler_params=_cp(("parallel", "arbitrary")),
    )(ub, ub, kt, ub, do, crow, ckb, lse, delta)


def _gate_fwd(o, ur, zcol, *, name):
    S = ur.shape[0]
    tm = min(1024, S)

    def body(o_ref, z_ref, y_ref):
        y_ref[...] = (o_ref[...] * _silu_parts(z_ref[...])[0]).astype(BF16)

    blk = pl.BlockSpec((tm, A_WIDTH), lambda i: (i, 0))
    return pl.pallas_call(
        body, name=name, grid=(S // tm,),
        in_specs=[blk, pl.BlockSpec((tm, A_WIDTH), lambda i: (i, zcol // A_WIDTH))],
        out_specs=blk, out_shape=jax.ShapeDtypeStruct((S, A_WIDTH), BF16),
        compiler_params=_cp(("parallel",)))(o, ur)


def _gate_bwd(o, ur, zcol, dy, *, name):
    S = ur.shape[0]
    tm = min(1024, S)

    def body(o_ref, z_ref, dy_ref, do_ref, dz_ref):
        sz, dsz = _silu_parts(z_ref[...])
        dyv = dy_ref[...]
        do_ref[...] = (dyv * sz).astype(BF16)
        dz_ref[...] = (dyv * o_ref[...] * dsz).astype(BF16)

    blk = pl.BlockSpec((tm, A_WIDTH), lambda i: (i, 0))
    return pl.pallas_call(
        body, name=name, grid=(S // tm,),
        in_specs=[blk, pl.BlockSpec((tm, A_WIDTH), lambda i: (i, zcol // A_WIDTH)), blk],
        out_specs=[blk, blk], out_shape=[jax.ShapeDtypeStruct((S, A_WIDTH), BF16)] * 2,
        compiler_params=_cp(("parallel",)))(o, ur, dy)


def _dfb(ur, bf_pad, dlogf_pad, *, name):
    S = ur.shape[0]
    tm = min(1024, S)

    def body(u_ref, b_ref, d_ref, o_ref, s_ref):
        i = pl.program_id(0)
        dv = d_ref[...] * _sigmoid(-(u_ref[...] + b_ref[...]))
        o_ref[...] = dv.astype(BF16)
        part = jnp.sum(dv, axis=0, keepdims=True)

        @pl.when(i == 0)
        def _():
            s_ref[...] = part

        @pl.when(i > 0)
        def _():
            s_ref[...] += part

    vec = pl.BlockSpec((1, FB_PAD), lambda i: (0, 0))
    blk = pl.BlockSpec((tm, FB_PAD), lambda i: (i, 0))
    return pl.pallas_call(
        body, name=name, grid=(S // tm,),
        in_specs=[pl.BlockSpec((tm, FB_PAD), lambda i: (i, R_FB // FB_PAD)), vec, blk],
        out_specs=[blk, vec],
        out_shape=[jax.ShapeDtypeStruct((S, FB_PAD), BF16), jax.ShapeDtypeStruct((1, FB_PAD), F32)],
        compiler_params=_cp(("arbitrary",)))(ur, bf_pad, dlogf_pad)


M_SCALE = HEAD ** -0.5


def _mem_fwd(ur, mkv, *, name):
    S = ur.shape[0]
    T = min(512, S)

    def body(q_ref, z_ref, k_ref, v_ref, y_ref):
        s = _dot_nt(q_ref[...].astype(BF16), k_ref[...].astype(BF16)) * M_SCALE
        p = jnp.exp(s - jnp.max(s, axis=-1, keepdims=True))
        p = p / jnp.sum(p, axis=-1, keepdims=True)
        o = _dot(p.astype(BF16), v_ref[...].astype(BF16))
        y_ref[...] = (o * _silu_parts(z_ref[...])[0]).astype(BF16)

    return pl.pallas_call(
        body, name=name, grid=(S // T, M_HEADS),
        in_specs=[pl.BlockSpec((T, HEAD), lambda i, h: (i, R_QM // HEAD + h)),
                  pl.BlockSpec((T, HEAD), lambda i, h: (i, R_ZM // HEAD + h)),
                  pl.BlockSpec((N_MEM, HEAD), lambda i, h: (0, h)),
                  pl.BlockSpec((N_MEM, HEAD), lambda i, h: (0, M_HEADS + h))],
        out_specs=pl.BlockSpec((T, HEAD), lambda i, h: (i, h)),
        out_shape=jax.ShapeDtypeStruct((S, A_WIDTH), BF16),
        compiler_params=_cp(("parallel", "parallel")))(ur, ur, mkv, mkv)


def _mem_bwd(ur, mkv, dy, *, name):
    S = ur.shape[0]
    T = min(512, S)

    def body(q_ref, z_ref, k_ref, v_ref, dy_ref, dq_ref, dz_ref, dk_ref, dv_ref):
        i = pl.program_id(1)
        qv = q_ref[...].astype(BF16)
        kv = k_ref[...].astype(BF16)
        vv = v_ref[...].astype(BF16)
        s = _dot_nt(qv, kv) * M_SCALE
        p = jnp.exp(s - jnp.max(s, axis=-1, keepdims=True))
        p = p / jnp.sum(p, axis=-1, keepdims=True)
        o = _dot(p.astype(BF16), vv)
        sz, dsz = _silu_parts(z_ref[...])
        dyv = dy_ref[...]
        dz_ref[...] = (dyv * o * dsz).astype(BF16)
        dov = (dyv * sz).astype(BF16)
        dp = _dot_nt(dov, vv)
        ds = p * (dp - jnp.sum(p * dp, axis=-1, keepdims=True))
        dq_ref[...] = (_dot(ds.astype(BF16), kv) * M_SCALE).astype(BF16)
        dvp = _dot(p.T.astype(BF16), dov)
        dkp = _dot(ds.T.astype(BF16), qv) * M_SCALE

        @pl.when(i == 0)
        def _():
            dk_ref[...] = dkp
            dv_ref[...] = dvp

        @pl.when(i > 0)
        def _():
            dk_ref[...] += dkp
            dv_ref[...] += dvp

    tile = pl.BlockSpec((T, HEAD), lambda h, i: (i, h))
    acc = pl.BlockSpec((N_MEM, HEAD), lambda h, i: (0, h))
    return pl.pallas_call(
        body, name=name, grid=(M_HEADS, S // T),
        in_specs=[pl.BlockSpec((T, HEAD), lambda h, i: (i, R_QM // HEAD + h)),
                  pl.BlockSpec((T, HEAD), lambda h, i: (i, R_ZM // HEAD + h)),
                  pl.BlockSpec((N_MEM, HEAD), lambda h, i: (0, h)),
                  pl.BlockSpec((N_MEM, HEAD), lambda h, i: (0, M_HEADS + h)), tile],
        out_specs=[tile, tile, acc, acc],
        out_shape=[jax.ShapeDtypeStruct((S, A_WIDTH), BF16)] * 2
        + [jax.ShapeDtypeStruct((N_MEM, A_WIDTH), F32)] * 2,
        compiler_params=_cp(("parallel", "arbitrary")))(ur, ur, mkv, mkv, dy)


def _branch_fwd(ys, wbs, ur, b_merge, *, name):
    S = ur.shape[0]
    tm, tn = min(512, S), 512
    nj = D_MODEL // tn

    def body(ya, yb, ym, wa, wb, wm, g0, g1, g2, b0, b1, b2, mg_ref, p_ref):
        acc = jnp.zeros((tm, tn), F32)
        for i, (y, w, gr, br) in enumerate(((ya, wa, g0, b0), (yb, wb, g1, b1), (ym, wm, g2, b2))):
            pr = _dot(y[...], w[...])
            p_ref[i] = pr
            acc = acc + _sigmoid(gr[...] + br[...]) * pr
        mg_ref[...] = acc.astype(BF16)

    yspec = pl.BlockSpec((tm, A_WIDTH), lambda i, j: (i, 0))
    wspec = pl.BlockSpec((A_WIDTH, tn), lambda i, j: (0, j))
    gspec = lambda b: pl.BlockSpec((tm, tn), lambda i, j: (i, (R_GL + b * D_MODEL) // tn + j))
    bspec = lambda b: pl.BlockSpec((1, tn), lambda i, j: (0, b * nj + j))
    return pl.pallas_call(
        body, name=name, grid=(S // tm, nj),
        in_specs=[yspec] * 3 + [wspec] * 3 + [gspec(0), gspec(1), gspec(2), bspec(0), bspec(1), bspec(2)],
        out_specs=[pl.BlockSpec((tm, tn), lambda i, j: (i, j)),
                   pl.BlockSpec((3, tm, tn), lambda i, j: (0, i, j))],
        out_shape=[jax.ShapeDtypeStruct((S, D_MODEL), BF16), jax.ShapeDtypeStruct((3, S, D_MODEL), F32)],
        compiler_params=_cp(("parallel", "parallel")))(*ys, *wbs, ur, ur, ur, b_merge, b_merge, b_merge)


def _branch_bwd(dm, prods, ur, b_merge, *, name):
    S = ur.shape[0]
    tm = min(256, S)

    def body(dm_ref, p_ref, g0, g1, g2, b_ref, dp_ref, dgl_ref, db_ref):
        i = pl.program_id(0)
        dmv = dm_ref[...]
        parts = []
        for b, gr in enumerate((g0, g1, g2)):
            sl = slice(b * D_MODEL, (b + 1) * D_MODEL)
            gt = _sigmoid(gr[...] + b_ref[:, sl])
            dp_ref[b] = (dmv * gt).astype(BF16)
            dgl = dmv * p_ref[b] * gt * (1.0 - gt)
            dgl_ref[:, sl] = dgl.astype(BF16)
            parts.append(jnp.sum(dgl, axis=0, keepdims=True))
        part = jnp.concatenate(parts, axis=1)

        @pl.when(i == 0)
        def _():
            db_ref[...] = part

        @pl.when(i > 0)
        def _():
            db_ref[...] += part

    gspec = lambda b: pl.BlockSpec((tm, D_MODEL), lambda i: (i, R_GL // D_MODEL + b))
    vec = pl.BlockSpec((1, 3 * D_MODEL), lambda i: (0, 0))
    return pl.pallas_call(
        body, name=name, grid=(S // tm,),
        in_specs=[pl.BlockSpec((tm, D_MODEL), lambda i: (i, 0)),
                  pl.BlockSpec((3, tm, D_MODEL), lambda i: (0, i, 0)), gspec(0), gspec(1), gspec(2), vec],
        out_specs=[pl.BlockSpec((3, tm, D_MODEL), lambda i: (0, i, 0)),
                   pl.BlockSpec((tm, 3 * D_MODEL), lambda i: (i, 0)), vec],
        out_shape=[jax.ShapeDtypeStruct((3, S, D_MODEL), BF16), jax.ShapeDtypeStruct((S, 3 * D_MODEL), BF16),
                   jax.ShapeDtypeStruct((1, 3 * D_MODEL), F32)],
        compiler_params=_cp(("arbitrary",)))(dm, prods, ur, ur, ur, b_merge)


def _rope_tables(pos):
    half = ROT // 2
    inv = ROPE_THETA ** (-jnp.arange(half, dtype=F32) / half)
    ang = pos.astype(F32)[:, None] * inv
    cos, sin = jnp.cos(ang), jnp.sin(ang)
    S = pos.shape[0]
    one = jnp.ones((S, LANES - ROT), F32)
    zero = jnp.zeros((S, LANES - ROT), F32)
    zh = jnp.zeros((S, half), F32)
    c = jnp.concatenate([cos, cos, one], axis=1)
    s1 = jnp.concatenate([-sin, zh, zero], axis=1)
    s2 = jnp.concatenate([zh, sin, zero], axis=1)
    return c, s1, s2


def _to_tiles(t):
    S, H = t.shape
    return t.reshape(S // LANES, LANES, H).transpose(0, 2, 1)


def _from_tiles(t):
    nt, H, _ = t.shape
    return t.transpose(1, 0, 2).reshape(H, nt * LANES)


def _local_step(x, mem, pos, tgt, g_pre, g_post, g_mem, wt, bf_pad, b_merge, w_kv, wbs, w_out):
    S = x.shape[0]
    T = min(512, S)
    nq = S // T
    tabs = _rope_tables(pos)

    h = _rms_fwd(x, g_pre, name="rms_pre")
    hs = [_to_classes(h, d) for d in DIL]
    tabs_g = [[_to_classes(t, d) for t in tabs] for d in DIL]
    uas = [_mm(hs[g], wt[f"A{g}"], bt=True, name=f"proj_a{g}", tn=1536) for g in range(3)]
    ub = _mm(h, wt["B"], bt=True, out_dtype=BF16, name="proj_b", tn=1536)
    ur = _mm(h, wt["R"], bt=True, name="proj_r", tn=1792)

    qkvs = [_rope_cast(uas[g], tabs_g[g], name=f"rope_a{g}") for g in range(3)]
    outs_c, lses_c = [], []
    for g in range(3):
        o, l = _attn_a_fwd(qkvs[g], g, name=f"attn_a_fwd{g}")
        outs_c.append(o)
        lses_c.append(l)
    outs_a = [_from_classes(o, d) for o, d in zip(outs_c, DIL)]
    lses_a = [_from_classes(l, d) for l, d in zip(lses_c, DIL)]
    ya = _merge_a_fwd(outs_a, lses_a, ur, name="merge_a_fwd")

    logf = _logf(ur, bf_pad, name="logf")
    c = _from_tiles(_cumsum_lanes(_to_tiles(logf[:, :B_HEADS]), False, name="cumsum_fwd"))
    crow = c.reshape(B_HEADS, nq, 1, T)
    ckb = jnp.broadcast_to(c[:, :, None], (B_HEADS, S, LANES))
    kt = ub[:, 512:1024].reshape(nq, T, 512).transpose(0, 2, 1)
    vt = ub[:, 1024:1536].reshape(nq, T, 512).transpose(0, 2, 1)
    ob, lse_b = _fox_fwd(ub, vt, crow, ckb, name="fox_fwd")
    yb = _gate_fwd(ob, ur, R_ZB, name="gate_b_fwd")

    hm = _rms_fwd(mem, g_mem, name="rms_mem")
    mkv = _mm(hm, w_kv, name="proj_mem")
    ym = _mem_fwd(ur, mkv, name="mem_fwd")

    merged, prods = _branch_fwd((ya, yb, ym), wbs, ur, b_merge, name="branch_fwd")
    out = _mm(merged, w_out, name="proj_out")
    dy, d_out, dg_post, loss_row = _post(x, out, tgt, g_post, name="post")

    dmerged = _mm(d_out, w_out, bt=True, name="d_merged")
    dw_out = _mm(merged.T, d_out, name="dw_out", tk=2048)
    dprods, dgl, db_merge = _branch_bwd(dmerged, prods, ur, b_merge, name="branch_bwd")
    dys, dwbs = [], []
    for i, (y, wb) in enumerate(zip((ya, yb, ym), wbs)):
        dys.append(_mm(dprods[i], wb, bt=True, name=f"d_y{i}"))
        dwbs.append(_mm(y.T, dprods[i], name=f"dw_branch{i}", tk=2048))

    dos_a, adjs_a, dza = _merge_a_bwd(outs_a, lses_a, ur, dys[0], name="merge_a_bwd")
    dus_a = []
    for g, d in enumerate(DIL):
        do_c, adj_c = _to_classes(dos_a[g], d), _to_classes(adjs_a[g], d)
        dq = _attn_a_dq(qkvs[g], tabs_g[g], g, do_c, lses_c[g], adj_c, name=f"attn_a_dq{g}")
        dk, dv = _attn_a_dkv(qkvs[g], tabs_g[g], g, do_c, lses_c[g], adj_c, name=f"attn_a_dkv{g}")
        dus_a.append(jnp.concatenate([dq, dk, dv], axis=1))

    dob, dzb = _gate_bwd(ob, ur, R_ZB, dys[1], name="gate_b_bwd")
    delta_b = _fox_delta(ob, dob, name="fox_delta")
    dkb, dvb, dc_k, dqt, dc_q = _fox_bwd(ub, kt, crow, ckb, dob, lse_b, delta_b, name="fox_bwd")
    dqb = (dqt.transpose(0, 2, 1).reshape(S, A_WIDTH) * B_SCALE).astype(BF16)
    du_b = jnp.concatenate([dqb, dkb, dvb], axis=1)
    dc = dc_q.reshape(B_HEADS, S) + dc_k.reshape(B_HEADS, S)
    dlogf = _from_tiles(_cumsum_lanes(_to_tiles(dc.T), True, name="cumsum_bwd"))
    dlogf_pad = jnp.pad(dlogf.T, ((0, 0), (0, FB_PAD - B_HEADS)))
    dfb, db_forget = _dfb(ur, bf_pad, dlogf_pad, name="dfb")

    dqm, dzm, dmk, dmv = _mem_bwd(ur, mkv, dys[2], name="mem_bwd")
    dmkv = jnp.concatenate([dmk, dmv], axis=1).astype(BF16)
    dhm = _mm(dmkv, w_kv, bt=True, name="d_hm")
    dw_kv = _mm(hm.T, dmkv, name="dw_kv")
    dg_mem = _rms_bwd(mem, g_mem, dhm, None, name="rms_mem_bwd")

    du_r = jnp.concatenate([dza, dzb, dqm, dzm, dgl, dfb], axis=1)
    dh = _mm(du_r, wt["R"], name="d_h_r", tk=1792) + _mm(du_b, wt["B"], name="d_h_b", tk=1536)
    for g, d in enumerate(DIL):
        dh = dh + _from_classes(_mm(dus_a[g], wt[f"A{g}"], name=f"d_h_a{g}", tk=1536), d)
    dwt = {"R": _mm(h.T, du_r, name="dw_in_r", tn=1792, tk=1024).T,
           "B": _mm(h.T, du_b, name="dw_in_b", tn=1536, tk=2048).T}
    for g in range(3):
        dwt[f"A{g}"] = _mm(hs[g].T, dus_a[g], name=f"dw_in_a{g}", tn=1536, tk=2048).T
    grad_x, dg_pre = _rms_bwd(x, g_pre, dh, dy, name="rms_pre_bwd")

    return dict(loss=loss_row, grad_x=grad_x, dwt=dwt, dw_kv=dw_kv, dwbs=dwbs, dw_out=dw_out,
                dg_pre=dg_pre, dg_post=dg_post, dg_mem=dg_mem, db_forget=db_forget, db_merge=db_merge)


MESH = pl.DeviceIdType.MESH
ANY = pl.BlockSpec(memory_space=pl.ANY)


def _relations():
    return [(k >> 2 & 1, k >> 1 & 1, k & 1) for k in range(1, N_DEV)]


def _coords():
    return lax.axis_index("x"), lax.axis_index("y"), lax.axis_index("c")


def _all_gather(shard, *, name):
    R, W = shard.shape

    def body(x_ref, out_ref, send_sems, recv_sems, local_sem):
        x, y, c = _coords()
        me, sibling = (x, y, c), (x, y, 1 - c)
        chips = [(1 - x, y), (x, 1 - y), (1 - x, 1 - y)]

        def slot(px, py, pc):
            return out_ref.at[4 * px + 2 * py + pc]

        def copy(k, block, to, src=None):
            return pltpu.make_async_remote_copy(
                src_ref=slot(*block) if src is None else src, dst_ref=slot(*block),
                send_sem=send_sems.at[k], recv_sem=recv_sems.at[k], device_id=to, device_id_type=MESH)

        mine = pltpu.make_async_copy(x_ref, slot(*me), local_sem)
        mine.start()
        first = [copy(0, me, sibling, src=x_ref)]
        first += [copy(1 + j, me, (*chip, c), src=x_ref) for j, chip in enumerate(chips)]
        for cp in first:
            cp.start()
        passed = [copy(4 + j, (*chip, c), sibling) for j, chip in enumerate(chips)]
        for j, chip in enumerate(chips):
            copy(1 + j, (*chip, c), me).wait_recv()
            passed[j].start()
        copy(0, sibling, me).wait_recv()
        for j, chip in enumerate(chips):
            copy(4 + j, (*chip, 1 - c), me).wait_recv()
        for cp in first + passed:
            cp.wait_send()
        mine.wait()

    return pl.pallas_call(
        body, name=name, out_shape=jax.ShapeDtypeStruct((N_DEV, R, W), shard.dtype),
        in_specs=[ANY], out_specs=ANY,
        scratch_shapes=[pltpu.SemaphoreType.DMA((N_DEV - 1,)), pltpu.SemaphoreType.DMA((N_DEV - 1,)),
                        pltpu.SemaphoreType.DMA],
    )(shard)


N_CHIP = 4


def _exchange_pair(gbig, *, name):
    _, R, W = gbig.shape

    def body(g_ref, sib_ref, send_sems, recv_sems):
        x, y, c = _coords()
        copies = []
        for r in range(N_CHIP):
            px, py = x ^ (r >> 1), y ^ (r & 1)
            copies.append(pltpu.make_async_remote_copy(
                src_ref=g_ref.at[4 * px + 2 * py + (1 - c)], dst_ref=sib_ref.at[r],
                send_sem=send_sems.at[r], recv_sem=recv_sems.at[r], device_id=(x, y, 1 - c), device_id_type=MESH))
        for cp in copies:
            cp.start()
        for cp in copies:
            cp.wait_recv()
        for cp in copies:
            cp.wait_send()

    return pl.pallas_call(
        body, name=name, out_shape=jax.ShapeDtypeStruct((N_CHIP, R, W), gbig.dtype),
        in_specs=[ANY], out_specs=ANY,
        scratch_shapes=[pltpu.SemaphoreType.DMA((N_CHIP,)), pltpu.SemaphoreType.DMA((N_CHIP,))],
    )(gbig)


def _own_slabs():
    x, y, c = _coords()
    return jnp.stack([4 * (x ^ (r >> 1)) + 2 * (y ^ (r & 1)) + c for r in range(N_CHIP)]).astype(jnp.int32)


def _pair_sum(gbig, sib, own_idx, tr, *, name):
    _, R, W = gbig.shape

    def body(idx_ref, a_ref, b_ref, o_ref):
        o_ref[...] = (a_ref[...] + b_ref[...]).astype(BF16)

    return pl.pallas_call(
        body, name=name,
        grid_spec=pltpu.PrefetchScalarGridSpec(
            num_scalar_prefetch=1, grid=(N_CHIP - 1, R // tr),
            in_specs=[pl.BlockSpec((None, tr, W), lambda r, i, idx: (idx[r + 1], i, 0)),
                      pl.BlockSpec((None, tr, W), lambda r, i, idx: (r + 1, i, 0))],
            out_specs=pl.BlockSpec((None, tr, W), lambda r, i, idx: (r, i, 0))),
        out_shape=jax.ShapeDtypeStruct((N_CHIP - 1, R, W), BF16),
        compiler_params=_cp(("parallel", "parallel")))(own_idx, gbig, sib)


def _exchange_chips(send, gsmall, *, name):
    nb, R, W = send.shape
    n = N_DEV - 1

    def body(b_ref, s_ref, rb_ref, rs_ref, send_sems, recv_sems, local_sem):
        x, y, c = _coords()
        me = 4 * x + 2 * y + c
        mine = pltpu.make_async_copy(s_ref, rs_ref.at[me], local_sem)
        mine.start()
        started = []
        for k, (fx, fy, fc) in enumerate(_relations()):
            cp = pltpu.make_async_remote_copy(
                src_ref=s_ref, dst_ref=rs_ref.at[me], send_sem=send_sems.at[k], recv_sem=recv_sems.at[k],
                device_id=(x ^ fx, y ^ fy, c ^ fc), device_id_type=MESH)
            cp.start()
            started.append(cp)
        for r in range(1, N_CHIP):
            cp = pltpu.make_async_remote_copy(
                src_ref=b_ref.at[r - 1], dst_ref=rb_ref.at[r - 1], send_sem=send_sems.at[n + r - 1],
                recv_sem=recv_sems.at[n + r - 1], device_id=(x ^ (r >> 1), y ^ (r & 1), c), device_id_type=MESH)
            cp.start()
            started.append(cp)
        for k, (fx, fy, fc) in enumerate(_relations()):
            peer = 4 * (x ^ fx) + 2 * (y ^ fy) + (c ^ fc)
            pltpu.make_async_remote_copy(
                src_ref=s_ref, dst_ref=rs_ref.at[peer], send_sem=send_sems.at[k], recv_sem=recv_sems.at[k],
                device_id=(x ^ fx, y ^ fy, c ^ fc), device_id_type=MESH).wait_recv()
        for r in range(1, N_CHIP):
            pltpu.make_async_remote_copy(
                src_ref=b_ref.at[r - 1], dst_ref=rb_ref.at[r - 1], send_sem=send_sems.at[n + r - 1],
                recv_sem=recv_sems.at[n + r - 1], device_id=(x ^ (r >> 1), y ^ (r & 1), c),
                device_id_type=MESH).wait_recv()
        for cp in started:
            cp.wait_send()
        mine.wait()

    return pl.pallas_call(
        body, name=name,
        out_shape=[jax.ShapeDtypeStruct((nb, R, W), send.dtype),
                   jax.ShapeDtypeStruct((N_DEV, 1, P_SMALL), gsmall.dtype)],
        in_specs=[ANY, ANY], out_specs=[ANY, ANY],
        scratch_shapes=[pltpu.SemaphoreType.DMA((n + nb,)), pltpu.SemaphoreType.DMA((n + nb,)),
                        pltpu.SemaphoreType.DMA],
    )(send, gsmall)


def _part_specs(parts, tr, row0):
    assert row0 % tr == 0
    specs = []
    for a, n_used in parts:
        if n_used is None:
            specs.append(pl.BlockSpec((1, tr, a.shape[2]), lambda i, idx: (idx[0], row0 // tr + i, 0)))
        else:
            specs.append(pl.BlockSpec((n_used, tr, a.shape[2]), lambda i, idx: (0, row0 // tr + i, 0)))
    return specs


def _part_total(refs, parts):
    g = None
    for ref, (_, n_used) in zip(refs, parts):
        for k in range(n_used or 1):
            t = ref[k].astype(F32)
            g = t if g is None else g + t
    return g


def _sum_parts(parts, idx, row0, nrows, tr, *, name):
    W = parts[0][0].shape[2]
    assert nrows % tr == 0

    def body(idx_ref, *refs):
        refs[-1][...] = _part_total(refs[:-1], parts)

    return pl.pallas_call(
        body, name=name,
        grid_spec=pltpu.PrefetchScalarGridSpec(
            num_scalar_prefetch=1, grid=(nrows // tr,), in_specs=_part_specs(parts, tr, row0),
            out_specs=pl.BlockSpec((tr, W), lambda i, idx: (i, 0))),
        out_shape=jax.ShapeDtypeStruct((nrows, W), F32),
        compiler_params=_cp(("parallel",)))(idx, *[a for a, _ in parts])


def _adamw(parts, idx, w, m, v, tr, *, name):
    R, W = w.shape
    assert R % tr == 0
    np_ = len(parts)

    def body(idx_ref, *refs):
        w_ref, m_ref, v_ref, g_ref, d_ref, nm_ref, nv_ref = refs[np_:]
        g = _part_total(refs[:np_], parts)
        mm = ADAM_B1 * m_ref[...] + (1.0 - ADAM_B1) * g
        vv = ADAM_B2 * v_ref[...] + (1.0 - ADAM_B2) * (g * g)
        m_hat = mm / (1.0 - ADAM_B1 ** ADAM_STEP)
        v_hat = vv / (1.0 - ADAM_B2 ** ADAM_STEP)
        g_ref[...] = g
        d_ref[...] = -ADAM_LR * (m_hat / (jnp.sqrt(v_hat) + ADAM_EPS) + ADAM_WD * w_ref[...])
        nm_ref[...] = mm
        nv_ref[...] = vv

    blk = pl.BlockSpec((tr, W), lambda i, idx: (i, 0))
    return pl.pallas_call(
        body, name=name,
        grid_spec=pltpu.PrefetchScalarGridSpec(
            num_scalar_prefetch=1, grid=(R // tr,), in_specs=_part_specs(parts, tr, 0) + [blk, blk, blk],
            out_specs=[blk] * 4),
        out_shape=[jax.ShapeDtypeStruct((R, W), F32)] * 4,
        compiler_params=_cp(("parallel",)))(idx, *[a for a, _ in parts], w, m, v)


def _pack_rest(w_kv, wa, wb, wm, w_out):
    return jnp.concatenate([w_kv[0], w_out[0]] + [t[0].reshape(-1, D_MODEL) for t in (wa, wb, wm)], axis=0)


def _unpack_rest(t):
    br = lambda i: t[RO_BR + 64 * i:RO_BR + 64 * (i + 1)].reshape(1, A_WIDTH, D_MODEL // N_DEV)
    return t[None, RO_KV:RO_OUT], br(0), br(1), br(2), t[None, RO_OUT:RO_BR]


def _orig_rows(gathered, a, b):
    res = []
    while a < b:
        dev, r = divmod(a, CS)
        n = min(b - a, CS - r)
        res.append(gathered[dev, RO_IN + r:RO_IN + r + n])
        a += n
    return res


def _full_weights(gathered):
    wt = {}
    for name, ranges in SEGS.items():
        rows = [p for a, b in ranges for p in _orig_rows(gathered, a, b)]
        if SEG_PAD[name]:
            rows.append(jnp.zeros((SEG_PAD[name], D_MODEL), gathered.dtype))
        wt[name] = jnp.concatenate(rows, axis=0)
    w_kv = gathered[:, RO_KV:RO_OUT].reshape(D_MODEL, D_MODEL)
    w_out = gathered[:, RO_OUT:RO_BR].reshape(D_MODEL, D_MODEL)
    wbs = [gathered[:, RO_BR + 64 * i:RO_BR + 64 * (i + 1)].reshape(N_DEV, A_WIDTH, D_MODEL // N_DEV)
           .transpose(1, 0, 2).reshape(A_WIDTH, D_MODEL) for i in range(3)]
    return wt, w_kv, wbs, w_out


def _orig_order(dwt):
    pieces = []
    for name, ranges in SEGS.items():
        o = 0
        for a, b in ranges:
            pieces.append((a, dwt[name][o:o + b - a]))
            o += b - a
    pieces.sort(key=lambda p: p[0])
    return jnp.concatenate([p[1] for p in pieces], axis=0)


def _pack_grads(dwt, dw_kv, dwbs, dw_out):
    g_in = jnp.pad(_orig_order(dwt).reshape(N_DEV, CS, D_MODEL), ((0, 0), (0, IN_ROWS - CS), (0, 0)))
    br = [t.reshape(A_WIDTH, N_DEV, D_MODEL // N_DEV).transpose(1, 0, 2).reshape(N_DEV, -1, D_MODEL) for t in dwbs]
    return jnp.concatenate([dw_kv.reshape(N_DEV, -1, D_MODEL), dw_out.reshape(N_DEV, -1, D_MODEL)] + br + [g_in],
                           axis=1)


def kernel(x, mem, positions, norm_pre_g, norm_post_g, norm_mem_g, w_in, b_forget, b_merge, w_mem_kv, w_branch_a, w_branch_b, w_branch_m, w_out, loss_target, m_norm_pre_g, m_norm_post_g, m_norm_mem_g, m_w_in, m_b_forget, m_b_merge, m_w_mem_kv, m_w_branch_a, m_w_branch_b, m_w_branch_m, m_w_out, v_norm_pre_g, v_norm_post_g, v_norm_mem_g, v_w_in, v_b_forget, v_b_merge, v_w_mem_kv, v_w_branch_a, v_w_branch_b, v_w_branch_m, v_w_out):
    w_rest = _pack_rest(w_mem_kv, w_branch_a, w_branch_b, w_branch_m, w_out)
    shard = jnp.concatenate([w_rest.astype(BF16), w_in[0].T.astype(BF16),
                             jnp.zeros((IN_ROWS - CS, D_MODEL), BF16)], axis=0)
    gathered = _all_gather(shard, name="gather_weights")
    wt, w_kv, wbs, w_o = _full_weights(gathered)

    bf_pad = jnp.pad(b_forget, ((0, 0), (0, FB_PAD - B_HEADS)))
    r = _local_step(x[0], mem[0], positions[0], loss_target[0], norm_pre_g, norm_post_g, norm_mem_g,
                    wt, bf_pad, b_merge, w_kv, wbs, w_o)

    gbig = _pack_grads(r["dwt"], r["dw_kv"], r["dwbs"], r["dw_out"])
    gsmall = jnp.concatenate([r["dg_pre"], r["dg_post"], r["dg_mem"], r["db_merge"],
                              r["db_forget"][:, :LANES], r["loss"]], axis=1)
    own_idx = _own_slabs()
    sib = _exchange_pair(gbig, name="exchange_pair")
    send = _pair_sum(gbig, sib, own_idx, 208, name="pair_sum")
    recv, rsmall = _exchange_chips(send, gsmall, name="exchange_chips")
    parts = [(gbig, None), (sib, 1), (recv, N_CHIP - 1)]

    m_rest = _pack_rest(m_w_mem_kv, m_w_branch_a, m_w_branch_b, m_w_branch_m, m_w_out)
    v_rest = _pack_rest(v_w_mem_kv, v_w_branch_a, v_w_branch_b, v_w_branch_m, v_w_out)
    outs_rest = [_unpack_rest(t) for t in _adamw(parts, own_idx, w_rest, m_rest, v_rest, 64, name="adamw_rest")]
    g_in = _sum_parts(parts, own_idx, RO_IN, IN_ROWS, 16, name="sum_w_in")[:CS].T
    outs_in = _adamw([(g_in[None], 1)], own_idx, w_in[0], m_w_in[0], v_w_in[0], 128, name="adamw_w_in")

    def small_vec(a, b, c, d, e):
        z = jnp.zeros((1, LANES - B_HEADS), F32)
        return jnp.concatenate([a, b, c, d, e, z, jnp.zeros((1, LANES), F32)], axis=1)

    outs_small = _adamw([(rsmall, N_DEV)], own_idx, small_vec(norm_pre_g, norm_post_g, norm_mem_g, b_merge, b_forget),
                        small_vec(m_norm_pre_g, m_norm_post_g, m_norm_mem_g, m_b_merge, m_b_forget),
                        small_vec(v_norm_pre_g, v_norm_post_g, v_norm_mem_g, v_b_merge, v_b_forget),
                        1, name="adamw_small")

    def small_parts(t):
        return [t[:, O_GPRE:O_GPRE + D_MODEL], t[:, O_GPOST:O_GPOST + D_MODEL], t[:, O_GMEM:O_GMEM + D_MODEL],
                t[:, O_BF:O_BF + B_HEADS], t[:, O_BM:O_BM + 3 * D_MODEL]]

    loss = outs_small[0][0, O_LOSS]
    result = [loss, r["grad_x"][None]]
    for rest, w_i, small in zip(outs_rest, outs_in, outs_small):
        gp, gq, gm, bf, bm = small_parts(small)
        w_k, w_a, w_b, w_m, w_ot = rest
        result += [gp, gq, gm, w_i[None], bf, bm, w_k, w_a, w_b, w_m, w_ot]
    return tuple(result)
```

```python
import jax
import jax.numpy as jnp
from jax import lax
from jax.experimental import pallas as pl
from jax.experimental.pallas import tpu as pltpu

F32 = jnp.float32
BF16 = jnp.bfloat16

N_DEV = 8
D_MODEL = 1024
N_MEM = 256
EPS = 1e-6
NEG = -1e30
ROPE_THETA = 500000.0
DIL = (1, 4, 16)
A_HEADS = 4
HEAD = 128
A_WIDTH = 512
B_HEADS = 8
B_HEAD = 64
M_HEADS = 4
ROT = 32
IN_COLS = 11272
FB_PAD = 256

SEGS = {
    "A0": ((0, 512), (1536, 2048), (3072, 3584)),
    "A1": ((512, 1024), (2048, 2560), (3584, 4096)),
    "A2": ((1024, 1536), (2560, 3072), (4096, 4608)),
    "B": ((5120, 6656),),
    "R": ((4608, 5120), (6664, 7176), (7176, 7688), (7688, 8200), (8200, 11272), (6656, 6664)),
}
SEG_PAD = {"A0": 0, "A1": 0, "A2": 0, "B": 0, "R": FB_PAD - B_HEADS}
R_ZA, R_ZB, R_QM, R_ZM, R_GL, R_FB = 0, 512, 1024, 1536, 2048, 5120
NR = R_FB + FB_PAD

ADAM_LR, ADAM_B1, ADAM_B2, ADAM_EPS, ADAM_WD, ADAM_STEP = 0.001, 0.9, 0.999, 1e-08, 0.01, 10

LANES = 128
VMEM_LIMIT = 56 * 1024 * 1024

CS = IN_COLS // N_DEV
RO_KV, RO_OUT, RO_BR, RO_IN = 0, 128, 256, 448
IN_ROWS = 1424
ROWS = RO_IN + IN_ROWS
O_GPRE, O_GPOST, O_GMEM, O_BM, O_BF, O_LOSS = 0, 1024, 2048, 3072, 6144, 6272
P_SMALL = 6400


def _cp(sem=None):
    return pltpu.CompilerParams(dimension_semantics=sem, vmem_limit_bytes=VMEM_LIMIT)


def _dot(a, b):
    return jnp.dot(a, b, preferred_element_type=F32)


def _dot_nt(a, b):
    return lax.dot_general(a, b, (((1,), (1,)), ((), ())), preferred_element_type=F32)


def _sigmoid(z):
    return 1.0 / (1.0 + jnp.exp(-z))


def _mm(a, b, *, name, at=False, bt=False, out_dtype=F32, tm=1024, tn=1024, tk=None):
    assert not (at and bt)
    K, M = a.shape if at else a.shape[::-1]
    N = b.shape[0] if bt else b.shape[1]
    tm, tn = min(tm, M), min(tn, N)
    tk = K if tk is None else min(tk, K)
    assert M % tm == 0 and N % tn == 0 and K % tk == 0
    nk = K // tk

    def body(a_ref, b_ref, o_ref, acc_ref):
        av = a_ref[...].astype(BF16)
        bv = b_ref[...].astype(BF16)
        if at:
            p = lax.dot_general(av, bv, (((0,), (0,)), ((), ())), preferred_element_type=F32)
        else:
            p = _dot_nt(av, bv) if bt else _dot(av, bv)
        if nk == 1:
            o_ref[...] = p.astype(out_dtype)
        else:
            k = pl.program_id(2)

            @pl.when(k == 0)
            def _():
                acc_ref[...] = p

            @pl.when(k > 0)
            def _():
                acc_ref[...] += p

            @pl.when(k == nk - 1)
            def _():
                o_ref[...] = acc_ref[...].astype(out_dtype)

    b_spec = (pl.BlockSpec((tn, tk), lambda i, j, k: (j, k)) if bt
              else pl.BlockSpec((tk, tn), lambda i, j, k: (k, j)))
    a_spec = (pl.BlockSpec((tk, tm), lambda i, j, k: (k, i)) if at
              else pl.BlockSpec((tm, tk), lambda i, j, k: (i, k)))
    return pl.pallas_call(
        body, name=name, grid=(M // tm, N // tn, nk),
        in_specs=[a_spec, b_spec],
        out_specs=pl.BlockSpec((tm, tn), lambda i, j, k: (i, j)),
        out_shape=jax.ShapeDtypeStruct((M, N), out_dtype),
        scratch_shapes=[pltpu.VMEM((tm, tn) if nk > 1 else (8, LANES), F32)],
        compiler_params=_cp(("parallel", "parallel", "arbitrary")),
    )(a, b)


def _rms_fwd(x, g, *, name):
    S, D = x.shape
    tm = min(512, S)

    def body(x_ref, g_ref, o_ref):
        xv = x_ref[...]
        r = lax.rsqrt(jnp.mean(xv * xv, axis=-1, keepdims=True) + EPS)
        o_ref[...] = (xv * r * g_ref[...]).astype(BF16)

    return pl.pallas_call(
        body, name=name, grid=(S // tm,),
        in_specs=[pl.BlockSpec((tm, D), lambda i: (i, 0)), pl.BlockSpec((1, D), lambda i: (0, 0))],
        out_specs=pl.BlockSpec((tm, D), lambda i: (i, 0)),
        out_shape=jax.ShapeDtypeStruct((S, D), BF16),
        compiler_params=_cp(("parallel",)),
    )(x, g)


def _rms_bwd(x, g, dh, dy, *, name):
    S, D = x.shape
    tm = min(512, S)
    want_dx = dy is not None

    def body(*refs):
        if want_dx:
            x_ref, g_ref, dh_ref, dy_ref, dx_ref, dg_ref = refs
        else:
            x_ref, g_ref, dh_ref, dg_ref = refs
        i = pl.program_id(0)
        xv = x_ref[...]
        r = lax.rsqrt(jnp.mean(xv * xv, axis=-1, keepdims=True) + EPS)
        xh = xv * r
        dhv = dh_ref[...]
        part = jnp.sum(dhv * xh, axis=0, keepdims=True)

        @pl.when(i == 0)
        def _():
            dg_ref[...] = part

        @pl.when(i > 0)
        def _():
            dg_ref[...] += part

        if want_dx:
            dxh = dhv * g_ref[...]
            dx_ref[...] = dy_ref[...] + r * (dxh - xh * jnp.mean(dxh * xh, axis=-1, keepdims=True))

    row = pl.BlockSpec((tm, D), lambda i: (i, 0))
    vec = pl.BlockSpec((1, D), lambda i: (0, 0))
    if want_dx:
        return pl.pallas_call(
            body, name=name, grid=(S // tm,), in_specs=[row, vec, row, row], out_specs=[row, vec],
            out_shape=[jax.ShapeDtypeStruct((S, D), F32), jax.ShapeDtypeStruct((1, D), F32)],
            compiler_params=_cp(("arbitrary",)))(x, g, dh, dy)
    return pl.pallas_call(
        body, name=name, grid=(S // tm,), in_specs=[row, vec, row], out_specs=vec,
        out_shape=jax.ShapeDtypeStruct((1, D), F32),
        compiler_params=_cp(("arbitrary",)))(x, g, dh)


def _post(x, out, tgt, g, *, name):
    S, D = x.shape
    tm = min(512, S)

    def body(x_ref, o_ref, t_ref, g_ref, dy_ref, do_ref, dg_ref, loss_ref):
        i = pl.program_id(0)
        ov = o_ref[...]
        r = lax.rsqrt(jnp.mean(ov * ov, axis=-1, keepdims=True) + EPS)
        n = ov * r
        gv = g_ref[...]
        e = (x_ref[...] + n * gv) - t_ref[...]
        lpart = 0.5 * jnp.sum(jnp.mean(e * e, axis=-1, keepdims=True), axis=0, keepdims=True)
        dy = e * (1.0 / D)
        dy_ref[...] = dy
        dn = dy * gv
        do_ref[...] = (r * (dn - n * jnp.mean(dn * n, axis=-1, keepdims=True))).astype(BF16)
        gpart = jnp.sum(dy * n, axis=0, keepdims=True)
        lrow = jnp.broadcast_to(lpart, (1, LANES))

        @pl.when(i == 0)
        def _():
            dg_ref[...] = gpart
            loss_ref[...] = lrow

        @pl.when(i > 0)
        def _():
            dg_ref[...] += gpart
            loss_ref[...] += lrow

    row = pl.BlockSpec((tm, D), lambda i: (i, 0))
    vec = pl.BlockSpec((1, D), lambda i: (0, 0))
    return pl.pallas_call(
        body, name=name, grid=(S // tm,), in_specs=[row, row, row, vec],
        out_specs=[row, row, vec, pl.BlockSpec((1, LANES), lambda i: (0, 0))],
        out_shape=[jax.ShapeDtypeStruct((S, D), F32), jax.ShapeDtypeStruct((S, D), BF16),
                   jax.ShapeDtypeStruct((1, D), F32), jax.ShapeDtypeStruct((1, LANES), F32)],
        compiler_params=_cp(("arbitrary",)))(x, out, tgt, g)


def _to_classes(t, d):
    if d == 1:
        return t
    S, C = t.shape
    return t.reshape(S // d, d, C).transpose(1, 0, 2).reshape(S, C)


def _from_classes(t, d):
    if d == 1:
        return t
    S, C = t.shape
    return t.reshape(d, S // d, C).transpose(1, 0, 2).reshape(S, C)


def _rope(x, c, s1, s2):
    return x * c + pltpu.roll(x, LANES - ROT // 2, 1) * s1 + pltpu.roll(x, ROT // 2, 1) * s2


def _unrope(d, c, s1, s2):
    return d * c + pltpu.roll(d * s1, ROT // 2, 1) + pltpu.roll(d * s2, LANES - ROT // 2, 1)


def _a_band(qb):
    r = lax.broadcasted_iota(jnp.int32, (qb, qb + HEAD), 0)
    c = lax.broadcasted_iota(jnp.int32, (qb, qb + HEAD), 1)
    return jnp.logical_and(c >= r, c <= r + HEAD)


def _a_first_ok(qb, n):
    c = lax.broadcasted_iota(jnp.int32, (qb, qb + HEAD), 1)
    return jnp.logical_or(c >= HEAD, n > 0)


def _a_last_ok(qb, has_next):
    c = lax.broadcasted_iota(jnp.int32, (qb, qb + HEAD), 1)
    return jnp.logical_or(c < qb, has_next)


A_SCALE = HEAD ** -0.5


def _a_geometry(S, g):
    d = DIL[g]
    L = S // d
    TQ = min(512, L)
    return d, L, TQ, TQ // HEAD, L // TQ, L // HEAD


def _rope_cast(ua, tabs, *, name):
    S = ua.shape[0]
    tm = min(512, S)

    def body(u_ref, c_ref, s1_ref, s2_ref, o_ref):
        tc = (c_ref[...], s1_ref[...], s2_ref[...])
        for j in range(3 * A_HEADS):
            sl = slice(j * HEAD, (j + 1) * HEAD)
            t = u_ref[:, sl]
            o_ref[:, sl] = (_rope(t, *tc) if j < 2 * A_HEADS else t).astype(BF16)

    blk = pl.BlockSpec((tm, 3 * A_WIDTH), lambda i: (i, 0))
    tab = pl.BlockSpec((tm, LANES), lambda i: (i, 0))
    return pl.pallas_call(
        body, name=name, grid=(S // tm,), in_specs=[blk, tab, tab, tab], out_specs=blk,
        out_shape=jax.ShapeDtypeStruct((S, 3 * A_WIDTH), BF16),
        compiler_params=_cp(("parallel",)))(ua, *tabs)


def _attn_a_fwd(qkv, g, *, name):
    S = qkv.shape[0]
    d, L, TQ, nsub, nb, nblk = _a_geometry(S, g)

    def body(q_ref, kc_ref, kp_ref, vc_ref, vp_ref, o_ref, l_ref):
        n = pl.program_id(1)
        q, kc, kp, vc, vp = q_ref[...], kc_ref[...], kp_ref[...], vc_ref[...], vp_ref[...]
        QB = min(2 * HEAD, TQ)
        band = _a_band(QB)
        for hh in range(TQ // QB):
            sl = slice(hh * QB, (hh + 1) * QB)
            pv = slice(hh * QB - HEAD, hh * QB)
            kcat = jnp.concatenate([kp if hh == 0 else kc[pv], kc[sl]], axis=0)
            vcat = jnp.concatenate([vp if hh == 0 else vc[pv], vc[sl]], axis=0)
            msk = jnp.logical_and(band, _a_first_ok(QB, n)) if hh == 0 else band
            s = jnp.where(msk, _dot_nt(q[sl], kcat) * A_SCALE, NEG)
            m = jnp.max(s, axis=-1, keepdims=True)
            p = jnp.exp(s - m)
            den = jnp.sum(p, axis=-1, keepdims=True)
            o_ref[sl, :] = _dot(p.astype(BF16), vcat) / den
            l_ref[sl, :] = jnp.broadcast_to(m + jnp.log(den), (QB, HEAD))

    rcur = lambda cb, n: (cb // A_HEADS) * nb + n
    rprv = lambda cb, n: (cb // A_HEADS) * nblk + jnp.maximum(n * nsub - 1, 0)
    cur = lambda off: pl.BlockSpec((TQ, HEAD), lambda cb, n: (rcur(cb, n), off + cb % A_HEADS))
    prv = lambda off: pl.BlockSpec((HEAD, HEAD), lambda cb, n: (rprv(cb, n), off + cb % A_HEADS))
    out = pl.BlockSpec((TQ, HEAD), lambda cb, n: (rcur(cb, n), cb % A_HEADS))
    return pl.pallas_call(
        body, name=name, grid=(A_HEADS * d, nb),
        in_specs=[cur(0), cur(4), prv(4), cur(8), prv(8)],
        out_specs=[out, out],
        out_shape=[jax.ShapeDtypeStruct((S, A_WIDTH), F32)] * 2,
        compiler_params=_cp(("parallel", "parallel")),
    )(qkv, qkv, qkv, qkv, qkv)


def _attn_a_dq(qkv, tabs, g, do, lse, adj, *, name):
    S = qkv.shape[0]
    d, L, TQ, nsub, nb, nblk = _a_geometry(S, g)

    def body(q_ref, kc_ref, kp_ref, vc_ref, vp_ref, do_ref, l_ref, adj_ref, c_ref, s1_ref, s2_ref, dq_ref):
        n = pl.program_id(1)
        q, kc, kp, vc, vp = q_ref[...], kc_ref[...], kp_ref[...], vc_ref[...], vp_ref[...]
        QB = min(2 * HEAD, TQ)
        band = _a_band(QB)
        for hh in range(TQ // QB):
            sl = slice(hh * QB, (hh + 1) * QB)
            pv = slice(hh * QB - HEAD, hh * QB)
            kcat = jnp.concatenate([kp if hh == 0 else kc[pv], kc[sl]], axis=0)
            vcat = jnp.concatenate([vp if hh == 0 else vc[pv], vc[sl]], axis=0)
            msk = jnp.logical_and(band, _a_first_ok(QB, n)) if hh == 0 else band
            doh = do_ref[sl, :]
            p = jnp.exp(jnp.where(msk, _dot_nt(q[sl], kcat) * A_SCALE, NEG) - l_ref[sl, :][:, :1])
            ds = p * (_dot_nt(doh, vcat) + adj_ref[sl, :][:, :1])
            dq = _dot(ds.astype(BF16), kcat) * A_SCALE
            dq_ref[sl, :] = _unrope(dq, c_ref[sl, :], s1_ref[sl, :], s2_ref[sl, :]).astype(BF16)

    rcur = lambda cb, n: (cb // A_HEADS) * nb + n
    rprv = lambda cb, n: (cb // A_HEADS) * nblk + jnp.maximum(n * nsub - 1, 0)
    cur = lambda off: pl.BlockSpec((TQ, HEAD), lambda cb, n: (rcur(cb, n), off + cb % A_HEADS))
    prv = lambda off: pl.BlockSpec((HEAD, HEAD), lambda cb, n: (rprv(cb, n), off + cb % A_HEADS))
    tcur = pl.BlockSpec((TQ, LANES), lambda cb, n: (rcur(cb, n), 0))
    blk = pl.BlockSpec((TQ, HEAD), lambda cb, n: (rcur(cb, n), cb % A_HEADS))
    return pl.pallas_call(
        body, name=name, grid=(A_HEADS * d, nb),
        in_specs=[cur(0), cur(4), prv(4), cur(8), prv(8), blk, blk, blk, tcur, tcur, tcur],
        out_specs=blk,
        out_shape=jax.ShapeDtypeStruct((S, A_WIDTH), BF16),
        compiler_params=_cp(("parallel", "parallel")),
    )(qkv, qkv, qkv, qkv, qkv, do, lse, adj, *tabs)


def _attn_a_dkv(qkv, tabs, g, do, lse, adj, *, name):
    S = qkv.shape[0]
    d, L, TQ, nsub, nb, nblk = _a_geometry(S, g)

    def body(qc_ref, qn_ref, kc_ref, vc_ref, doc_ref, don_ref, lc_ref, ln_ref, ac_ref, an_ref,
             c_ref, s1_ref, s2_ref, dk_ref, dv_ref):
        n = pl.program_id(1)
        qc, qn, kc, vc = qc_ref[...], qn_ref[...], kc_ref[...], vc_ref[...]
        QB = min(2 * HEAD, TQ)
        nh = TQ // QB
        band = _a_band(QB)
        for kh in range(nh):
            sl = slice(kh * QB, (kh + 1) * QB)
            nx = slice((kh + 1) * QB, (kh + 1) * QB + HEAD)
            last = kh == nh - 1
            cat = lambda cur, nxt: jnp.concatenate([cur[sl], nxt[...] if last else cur[nx]], axis=0)
            qcat = cat(qc, qn)
            docat = cat(doc_ref, don_ref)
            lt = cat(lc_ref, ln_ref).T[:1, :]
            at = cat(ac_ref, an_ref).T[:1, :]
            msk = jnp.logical_and(band, _a_last_ok(QB, n < nb - 1)) if last else band
            st = jnp.where(msk, _dot_nt(kc[sl], qcat) * A_SCALE, NEG)
            pt = jnp.exp(st - lt)
            dv_ref[sl, :] = _dot(pt.astype(BF16), docat).astype(BF16)
            dst = pt * (_dot_nt(vc[sl], docat) + at)
            dk = _dot(dst.astype(BF16), qcat) * A_SCALE
            dk_ref[sl, :] = _unrope(dk, c_ref[sl, :], s1_ref[sl, :], s2_ref[sl, :]).astype(BF16)

    rcur = lambda cb, n: (cb // A_HEADS) * nb + n
    rnxt = lambda cb, n: (cb // A_HEADS) * nblk + jnp.minimum((n + 1) * nsub, nblk - 1)
    cur = lambda off: pl.BlockSpec((TQ, HEAD), lambda cb, n: (rcur(cb, n), off + cb % A_HEADS))
    nxu = lambda off: pl.BlockSpec((HEAD, HEAD), lambda cb, n: (rnxt(cb, n), off + cb % A_HEADS))
    tcur = pl.BlockSpec((TQ, LANES), lambda cb, n: (rcur(cb, n), 0))
    blk = pl.BlockSpec((TQ, HEAD), lambda cb, n: (rcur(cb, n), cb % A_HEADS))
    bnx = pl.BlockSpec((HEAD, HEAD), lambda cb, n: (rnxt(cb, n), cb % A_HEADS))
    return pl.pallas_call(
        body, name=name, grid=(A_HEADS * d, nb),
        in_specs=[cur(0), nxu(0), cur(4), cur(8), blk, bnx, blk, bnx, blk, bnx, tcur, tcur, tcur],
        out_specs=[blk, blk],
        out_shape=[jax.ShapeDtypeStruct((S, A_WIDTH), BF16)] * 2,
        compiler_params=_cp(("parallel", "parallel")),
    )(qkv, qkv, qkv, qkv, do, do, lse, lse, adj, adj, *tabs)


def _silu_parts(z):
    sg = _sigmoid(z)
    return z * sg, sg * (1.0 + z * (1.0 - sg))


def _merge_a_fwd(os_, ls_, ur, *, name):
    S = ur.shape[0]
    tm = min(512, S)

    def body(o0, o1, o2, l0, l1, l2, z_ref, y_ref):
        ls = [l0[...], l1[...], l2[...]]
        mx = jnp.maximum(jnp.maximum(ls[0], ls[1]), ls[2])
        es = [jnp.exp(l - mx) for l in ls]
        den = es[0] + es[1] + es[2]
        y = (es[0] / den) * o0[...] + (es[1] / den) * o1[...] + (es[2] / den) * o2[...]
        y_ref[...] = (y * _silu_parts(z_ref[...])[0]).astype(BF16)

    blk = pl.BlockSpec((tm, A_WIDTH), lambda i: (i, 0))
    return pl.pallas_call(
        body, name=name, grid=(S // tm,),
        in_specs=[blk] * 6 + [pl.BlockSpec((tm, A_WIDTH), lambda i: (i, R_ZA // A_WIDTH))],
        out_specs=blk, out_shape=jax.ShapeDtypeStruct((S, A_WIDTH), BF16),
        compiler_params=_cp(("parallel",)))(*os_, *ls_, ur)


def _merge_a_bwd(os_, ls_, ur, dya, *, name):
    S = ur.shape[0]
    tm = min(256, S)

    def body(o0, o1, o2, l0, l1, l2, z_ref, dy_ref, d0, d1, d2, a0, a1, a2, dz_ref):
        ls = [l0[...], l1[...], l2[...]]
        ov = [o0[...], o1[...], o2[...]]
        mx = jnp.maximum(jnp.maximum(ls[0], ls[1]), ls[2])
        es = [jnp.exp(l - mx) for l in ls]
        den = es[0] + es[1] + es[2]
        ws = [e / den for e in es]
        y = ws[0] * ov[0] + ws[1] * ov[1] + ws[2] * ov[2]
        sz, dsz = _silu_parts(z_ref[...])
        dyv = dy_ref[...]
        dz_ref[...] = (dyv * y * dsz).astype(BF16)
        dyp = dyv * sz
        for h in range(A_HEADS):
            sl = slice(h * HEAD, (h + 1) * HEAD)
            t = jnp.zeros((tm, 1), F32)
            for gi in range(3):
                t = t + ws[gi][:, sl][:, :1] * jnp.sum(dyp[:, sl] * ov[gi][:, sl], axis=-1, keepdims=True)
            for gi, (dref, aref) in enumerate(((d0, a0), (d1, a1), (d2, a2))):
                wg = ws[gi][:, sl]
                dref[:, sl] = (wg * dyp[:, sl]).astype(BF16)
                aref[:, sl] = -wg * t

    blk = pl.BlockSpec((tm, A_WIDTH), lambda i: (i, 0))
    outs = pl.pallas_call(
        body, name=name, grid=(S // tm,),
        in_specs=[blk] * 6 + [pl.BlockSpec((tm, A_WIDTH), lambda i: (i, R_ZA // A_WIDTH)), blk],
        out_specs=[blk] * 7,
        out_shape=[jax.ShapeDtypeStruct((S, A_WIDTH), BF16)] * 3
        + [jax.ShapeDtypeStruct((S, A_WIDTH), F32)] * 3 + [jax.ShapeDtypeStruct((S, A_WIDTH), BF16)],
        compiler_params=_cp(("parallel",)))(*os_, *ls_, ur, dya)
    return outs[0:3], outs[3:6], outs[6]


def _logf(ur, bf_pad, *, name):
    S = ur.shape[0]
    tm = min(1024, S)

    def body(u_ref, b_ref, o_ref):
        z = u_ref[...] + b_ref[...]
        o_ref[...] = jnp.minimum(z, 0.0) - jnp.log(1.0 + jnp.exp(-jnp.abs(z)))

    return pl.pallas_call(
        body, name=name, grid=(S // tm,),
        in_specs=[pl.BlockSpec((tm, FB_PAD), lambda i: (i, R_FB // FB_PAD)),
                  pl.BlockSpec((1, FB_PAD), lambda i: (0, 0))],
        out_specs=pl.BlockSpec((tm, FB_PAD), lambda i: (i, 0)),
        out_shape=jax.ShapeDtypeStruct((S, FB_PAD), F32),
        compiler_params=_cp(("parallel",)))(ur, bf_pad)


def _cumsum_lanes(x, reverse, *, name):
    nt, H, _ = x.shape

    def body(x_ref, o_ref):
        lane = lax.broadcasted_iota(jnp.int32, (H, LANES), 1)

        def tile(t, carry):
            tt = nt - 1 - t if reverse else t
            v = x_ref[tt]
            k = 1
            while k < LANES:
                if reverse:
                    v = v + jnp.where(lane < LANES - k, pltpu.roll(v, LANES - k, 1), 0.0)
                else:
                    v = v + jnp.where(lane >= k, pltpu.roll(v, k, 1), 0.0)
                k *= 2
            v = v + carry
            o_ref[tt] = v
            edge = v[:, :1] if reverse else v[:, LANES - 1:]
            return jnp.broadcast_to(edge, (H, LANES))

        lax.fori_loop(0, nt, tile, jnp.zeros((H, LANES), F32))

    return pl.pallas_call(
        body, name=name, out_shape=jax.ShapeDtypeStruct((nt, H, LANES), F32),
        in_specs=[pl.BlockSpec(memory_space=pltpu.VMEM)], out_specs=pl.BlockSpec(memory_space=pltpu.VMEM),
        compiler_params=_cp())(x)


B_SCALE = B_HEAD ** -0.5


def _pair_masks():
    lane = lax.broadcasted_iota(jnp.int32, (1, LANES), 1)
    row = lax.broadcasted_iota(jnp.int32, (LANES, 1), 0)
    return (lane < B_HEAD, lane >= B_HEAD), (row < B_HEAD, row >= B_HEAD)


def _causal_t(T):
    r = lax.broadcasted_iota(jnp.int32, (T, T), 0)
    c = lax.broadcasted_iota(jnp.int32, (T, T), 1)
    return r <= c


def _zero_other(x, keep):
    return jnp.where(keep, x, jnp.zeros_like(x))


def _fox_fwd(ub, vt, crow, ckb, *, name):
    S = ub.shape[0]
    T = min(512, S)
    nq = S // T

    def body(q_ref, k_ref, vt_ref, cr_ref, ck_ref, o_ref, l_ref, m_s, l_s, acc_s):
        i = pl.program_id(1)
        lanes, rows = _pair_masks()
        q = q_ref[...] * B_SCALE
        qm = [_zero_other(q, lanes[0]), _zero_other(q, lanes[1])]
        m_s[...] = jnp.full((2, 1, T), NEG, F32)
        l_s[...] = jnp.zeros((2, 1, T), F32)
        acc_s[...] = jnp.zeros((LANES, T), F32)

        def step(j, masked):
            off = pl.multiple_of(j * T, T)
            kj = k_ref[pl.ds(off, T), :]
            vtj = vt_ref[j]
            upd = jnp.zeros((LANES, T), F32)
            alphas = []
            for a in range(2):
                st = _dot_nt(kj, qm[a]) + (cr_ref[a, i] - jnp.tile(ck_ref[a, pl.ds(off, T), :], (1, T // LANES)))
                if masked:
                    st = jnp.where(_causal_t(T), st, NEG)
                m_old = m_s[a]
                m_new = jnp.maximum(m_old, jnp.max(st, axis=0, keepdims=True))
                alpha = jnp.exp(m_old - m_new)
                pt = jnp.exp(st - m_new)
                l_s[a] = alpha * l_s[a] + jnp.sum(pt, axis=0, keepdims=True)
                m_s[a] = m_new
                upd = upd + _dot(_zero_other(vtj, rows[a]), pt.astype(BF16))
                alphas.append(alpha)
            acc_s[...] = acc_s[...] * jnp.where(rows[0], alphas[0], alphas[1]) + upd

        def loop(j, carry):
            step(j, False)
            return carry

        lax.fori_loop(0, i, loop, 0)
        step(i, True)
        o_ref[...] = (acc_s[...] / jnp.where(rows[0], l_s[0], l_s[1])).T
        l_ref[0] = m_s[0] + jnp.log(l_s[0])
        l_ref[1] = m_s[1] + jnp.log(l_s[1])

    stat = pl.BlockSpec((2, None, 1, T), lambda h, i: (h, i, 0, 0))
    return pl.pallas_call(
        body, name=name, grid=(B_HEADS // 2, nq),
        in_specs=[pl.BlockSpec((T, LANES), lambda h, i: (i, h)),
                  pl.BlockSpec((S, LANES), lambda h, i: (0, 4 + h)),
                  pl.BlockSpec((nq, LANES, T), lambda h, i: (0, h, 0)),
                  pl.BlockSpec((2, nq, 1, T), lambda h, i: (h, 0, 0, 0)),
                  pl.BlockSpec((2, S, LANES), lambda h, i: (h, 0, 0))],
        out_specs=[pl.BlockSpec((T, LANES), lambda h, i: (i, h)), stat],
        out_shape=[jax.ShapeDtypeStruct((S, A_WIDTH), F32), jax.ShapeDtypeStruct((B_HEADS, nq, 1, T), F32)],
        scratch_shapes=[pltpu.VMEM((2, 1, T), F32), pltpu.VMEM((2, 1, T), F32), pltpu.VMEM((LANES, T), F32)],
        compiler_params=_cp(("parallel", "parallel")),
    )(ub, ub, vt, crow, ckb)


def _fox_delta(o, do, *, name):
    S = o.shape[0]
    T = min(512, S)
    nq = S // T

    def body(o_ref, do_ref, d_ref):
        _, rows = _pair_masks()
        prod_t = (do_ref[...].astype(F32) * o_ref[...]).T
        d_ref[0] = jnp.sum(_zero_other(prod_t, rows[0]), axis=0, keepdims=True)
        d_ref[1] = jnp.sum(_zero_other(prod_t, rows[1]), axis=0, keepdims=True)

    tile = pl.BlockSpec((T, LANES), lambda h, i: (i, h))
    return pl.pallas_call(
        body, name=name, grid=(B_HEADS // 2, nq), in_specs=[tile, tile],
        out_specs=pl.BlockSpec((2, None, 1, T), lambda h, i: (h, i, 0, 0)),
        out_shape=jax.ShapeDtypeStruct((B_HEADS, nq, 1, T), F32),
        compiler_params=_cp(("parallel", "parallel")))(o, do)


def _fox_bwd(ub, kt, crow, ckb, do, lse, delta, *, name):
    S = ub.shape[0]
    T = min(512, S)
    nq = S // T

    def body(k_ref, v_ref, kt_ref, q_ref, do_ref, cr_ref, ck_ref, l_ref, dl_ref,
             dk_ref, dv_ref, dck_ref, dqt_ref, dcq_ref, dk_s, dv_s, dc_s):
        j = pl.program_id(1)
        lanes, rows = _pair_masks()
        kv = k_ref[...]
        vv = v_ref[...]
        ktj = kt_ref[...]
        km = [_zero_other(kv, lanes[0]), _zero_other(kv, lanes[1])]
        ktm = [_zero_other(ktj, rows[0]), _zero_other(ktj, rows[1])]
        ck = [jnp.tile(ck_ref[a], (1, T // LANES)) for a in range(2)]
        dk_s[...] = jnp.zeros((T, LANES), F32)
        dv_s[...] = jnp.zeros((T, LANES), F32)
        dc_s[...] = jnp.zeros((2, T, 1), F32)

        @pl.when(j == 0)
        def _():
            dqt_ref[...] = jnp.zeros((nq, LANES, T), F32)
            dcq_ref[...] = jnp.zeros((2, nq, 1, T), F32)

        def step(i, masked):
            off = pl.multiple_of(i * T, T)
            qi = q_ref[pl.ds(off, T), :] * B_SCALE
            doi = do_ref[pl.ds(off, T), :]
            upd = jnp.zeros((LANES, T), F32)
            for a in range(2):
                st = _dot_nt(km[a], qi) + (cr_ref[a, i] - ck[a])
                if masked:
                    st = jnp.where(_causal_t(T), st, NEG)
                pt = jnp.exp(st - l_ref[a, i])
                doa = _zero_other(doi, lanes[a])
                dv_s[...] += _dot(pt.astype(BF16), doa)
                dst = pt * (_dot_nt(vv, doa) - dl_ref[a, i])
                dsb = dst.astype(BF16)
                dk_s[...] += _dot(dsb, _zero_other(qi, lanes[a]))
                upd = upd + _dot(ktm[a], dsb)
                dc_s[a] -= jnp.sum(dst, axis=-1, keepdims=True)
                dcq_ref[a, i] += jnp.sum(dst, axis=0, keepdims=True)
            dqt_ref[i] += upd

        def loop(i, carry):
            step(i, False)
            return carry

        step(j, True)
        lax.fori_loop(j + 1, nq, loop, 0)
        dk_ref[...] = dk_s[...].astype(BF16)
        dv_ref[...] = dv_s[...].astype(BF16)
        dck_ref[...] = dc_s[...]

    rowv = pl.BlockSpec((2, nq, 1, T), lambda h, j: (h, 0, 0, 0))
    tile = pl.BlockSpec((T, LANES), lambda h, j: (j, h))
    return pl.pallas_call(
        body, name=name, grid=(B_HEADS // 2, nq),
        in_specs=[pl.BlockSpec((T, LANES), lambda h, j: (j, 4 + h)),
                  pl.BlockSpec((T, LANES), lambda h, j: (j, 8 + h)),
                  pl.BlockSpec((None, LANES, T), lambda h, j: (j, h, 0)),
                  pl.BlockSpec((S, LANES), lambda h, j: (0, h)),
                  pl.BlockSpec((S, LANES), lambda h, j: (0, h)),
                  rowv,
                  pl.BlockSpec((2, T, LANES), lambda h, j: (h, j, 0)),
                  rowv, rowv],
        out_specs=[tile, tile, pl.BlockSpec((2, T, 1), lambda h, j: (h, j, 0)),
                   pl.BlockSpec((nq, LANES, T), lambda h, j: (0, h, 0)), rowv],
        out_shape=[jax.ShapeDtypeStruct((S, A_WIDTH), BF16)] * 2 + [jax.ShapeDtypeStruct((B_HEADS, S, 1), F32),
                   jax.ShapeDtypeStruct((nq, A_WIDTH, T), F32), jax.ShapeDtypeStruct((B_HEADS, nq, 1, T), F32)],
        scratch_shapes=[pltpu.VMEM((T, LANES), F32), pltpu.VMEM((T, LANES), F32), pltpu.VMEM((2, T, 1), F32)],
        compiler_params=_cp(("parallel", "arbitrary")),
    )(ub, ub, kt, ub, do, crow, ckb, lse, delta)


def _gate_fwd(o, ur, zcol, *, name):
    S = ur.shape[0]
    tm = min(1024, S)

    def body(o_ref, z_ref, y_ref):
        y_ref[...] = (o_ref[...] * _silu_parts(z_ref[...])[0]).astype(BF16)

    blk = pl.BlockSpec((tm, A_WIDTH), lambda i: (i, 0))
    return pl.pallas_call(
        body, name=name, grid=(S // tm,),
        in_specs=[blk, pl.BlockSpec((tm, A_WIDTH), lambda i: (i, zcol // A_WIDTH))],
        out_specs=blk, out_shape=jax.ShapeDtypeStruct((S, A_WIDTH), BF16),
        compiler_params=_cp(("parallel",)))(o, ur)


def _gate_bwd(o, ur, zcol, dy, *, name):
    S = ur.shape[0]
    tm = min(1024, S)

    def body(o_ref, z_ref, dy_ref, do_ref, dz_ref):
        sz, dsz = _silu_parts(z_ref[...])
        dyv = dy_ref[...]
        do_ref[...] = (dyv * sz).astype(BF16)
        dz_ref[...] = (dyv * o_ref[...] * dsz).astype(BF16)

    blk = pl.BlockSpec((tm, A_WIDTH), lambda i: (i, 0))
    return pl.pallas_call(
        body, name=name, grid=(S // tm,),
        in_specs=[blk, pl.BlockSpec((tm, A_WIDTH), lambda i: (i, zcol // A_WIDTH)), blk],
        out_specs=[blk, blk], out_shape=[jax.ShapeDtypeStruct((S, A_WIDTH), BF16)] * 2,
        compiler_params=_cp(("parallel",)))(o, ur, dy)


def _dfb(ur, bf_pad, dlogf_pad, *, name):
    S = ur.shape[0]
    tm = min(1024, S)

    def body(u_ref, b_ref, d_ref, o_ref, s_ref):
        i = pl.program_id(0)
        dv = d_ref[...] * _sigmoid(-(u_ref[...] + b_ref[...]))
        o_ref[...] = dv.astype(BF16)
        part = jnp.sum(dv, axis=0, keepdims=True)

        @pl.when(i == 0)
        def _():
            s_ref[...] = part

        @pl.when(i > 0)
        def _():
            s_ref[...] += part

    vec = pl.BlockSpec((1, FB_PAD), lambda i: (0, 0))
    blk = pl.BlockSpec((tm, FB_PAD), lambda i: (i, 0))
    return pl.pallas_call(
        body, name=name, grid=(S // tm,),
        in_specs=[pl.BlockSpec((tm, FB_PAD), lambda i: (i, R_FB // FB_PAD)), vec, blk],
        out_specs=[blk, vec],
        out_shape=[jax.ShapeDtypeStruct((S, FB_PAD), BF16), jax.ShapeDtypeStruct((1, FB_PAD), F32)],
        compiler_params=_cp(("arbitrary",)))(ur, bf_pad, dlogf_pad)


M_SCALE = HEAD ** -0.5


def _mem_fwd(ur, mkv, *, name):
    S = ur.shape[0]
    T = min(512, S)

    def body(q_ref, z_ref, k_ref, v_ref, y_ref):
        s = _dot_nt(q_ref[...].astype(BF16), k_ref[...].astype(BF16)) * M_SCALE
        p = jnp.exp(s - jnp.max(s, axis=-1, keepdims=True))
        p = p / jnp.sum(p, axis=-1, keepdims=True)
        o = _dot(p.astype(BF16), v_ref[...].astype(BF16))
        y_ref[...] = (o * _silu_parts(z_ref[...])[0]).astype(BF16)

    return pl.pallas_call(
        body, name=name, grid=(S // T, M_HEADS),
        in_specs=[pl.BlockSpec((T, HEAD), lambda i, h: (i, R_QM // HEAD + h)),
                  pl.BlockSpec((T, HEAD), lambda i, h: (i, R_ZM // HEAD + h)),
                  pl.BlockSpec((N_MEM, HEAD), lambda i, h: (0, h)),
                  pl.BlockSpec((N_MEM, HEAD), lambda i, h: (0, M_HEADS + h))],
        out_specs=pl.BlockSpec((T, HEAD), lambda i, h: (i, h)),
        out_shape=jax.ShapeDtypeStruct((S, A_WIDTH), BF16),
        compiler_params=_cp(("parallel", "parallel")))(ur, ur, mkv, mkv)


def _mem_bwd(ur, mkv, dy, *, name):
    S = ur.shape[0]
    T = min(512, S)

    def body(q_ref, z_ref, k_ref, v_ref, dy_ref, dq_ref, dz_ref, dk_ref, dv_ref):
        i = pl.program_id(1)
        qv = q_ref[...].astype(BF16)
        kv = k_ref[...].astype(BF16)
        vv = v_ref[...].astype(BF16)
        s = _dot_nt(qv, kv) * M_SCALE
        p = jnp.exp(s - jnp.max(s, axis=-1, keepdims=True))
        p = p / jnp.sum(p, axis=-1, keepdims=True)
        o = _dot(p.astype(BF16), vv)
        sz, dsz = _silu_parts(z_ref[...])
        dyv = dy_ref[...]
        dz_ref[...] = (dyv * o * dsz).astype(BF16)
        dov = (dyv * sz).astype(BF16)
        dp = _dot_nt(dov, vv)
        ds = p * (dp - jnp.sum(p * dp, axis=-1, keepdims=True))
        dq_ref[...] = (_dot(ds.astype(BF16), kv) * M_SCALE).astype(BF16)
        dvp = _dot(p.T.astype(BF16), dov)
        dkp = _dot(ds.T.astype(BF16), qv) * M_SCALE

        @pl.when(i == 0)
        def _():
            dk_ref[...] = dkp
            dv_ref[...] = dvp

        @pl.when(i > 0)
        def _():
            dk_ref[...] += dkp
            dv_ref[...] += dvp

    tile = pl.BlockSpec((T, HEAD), lambda h, i: (i, h))
    acc = pl.BlockSpec((N_MEM, HEAD), lambda h, i: (0, h))
    return pl.pallas_call(
        body, name=name, grid=(M_HEADS, S // T),
        in_specs=[pl.BlockSpec((T, HEAD), lambda h, i: (i, R_QM // HEAD + h)),
                  pl.BlockSpec((T, HEAD), lambda h, i: (i, R_ZM // HEAD + h)),
                  pl.BlockSpec((N_MEM, HEAD), lambda h, i: (0, h)),
                  pl.BlockSpec((N_MEM, HEAD), lambda h, i: (0, M_HEADS + h)), tile],
        out_specs=[tile, tile, acc, acc],
        out_shape=[jax.ShapeDtypeStruct((S, A_WIDTH), BF16)] * 2
        + [jax.ShapeDtypeStruct((N_MEM, A_WIDTH), F32)] * 2,
        compiler_params=_cp(("parallel", "arbitrary")))(ur, ur, mkv, mkv, dy)


def _branch_fwd(ys, wbs, ur, b_merge, *, name):
    S = ur.shape[0]
    tm, tn = min(512, S), 512
    nj = D_MODEL // tn

    def body(ya, yb, ym, wa, wb, wm, g0, g1, g2, b0, b1, b2, mg_ref, p_ref):
        acc = jnp.zeros((tm, tn), F32)
        for i, (y, w, gr, br) in enumerate(((ya, wa, g0, b0), (yb, wb, g1, b1), (ym, wm, g2, b2))):
            pr = _dot(y[...], w[...])
            p_ref[i] = pr
            acc = acc + _sigmoid(gr[...] + br[...]) * pr
        mg_ref[...] = acc.astype(BF16)

    yspec = pl.BlockSpec((tm, A_WIDTH), lambda i, j: (i, 0))
    wspec = pl.BlockSpec((A_WIDTH, tn), lambda i, j: (0, j))
    gspec = lambda b: pl.BlockSpec((tm, tn), lambda i, j: (i, (R_GL + b * D_MODEL) // tn + j))
    bspec = lambda b: pl.BlockSpec((1, tn), lambda i, j: (0, b * nj + j))
    return pl.pallas_call(
        body, name=name, grid=(S // tm, nj),
        in_specs=[yspec] * 3 + [wspec] * 3 + [gspec(0), gspec(1), gspec(2), bspec(0), bspec(1), bspec(2)],
        out_specs=[pl.BlockSpec((tm, tn), lambda i, j: (i, j)),
                   pl.BlockSpec((3, tm, tn), lambda i, j: (0, i, j))],
        out_shape=[jax.ShapeDtypeStruct((S, D_MODEL), BF16), jax.ShapeDtypeStruct((3, S, D_MODEL), F32)],
        compiler_params=_cp(("parallel", "parallel")))(*ys, *wbs, ur, ur, ur, b_merge, b_merge, b_merge)


def _branch_bwd(dm, prods, ur, b_merge, *, name):
    S = ur.shape[0]
    tm = min(256, S)

    def body(dm_ref, p_ref, g0, g1, g2, b_ref, dp_ref, dgl_ref, db_ref):
        i = pl.program_id(0)
        dmv = dm_ref[...]
        parts = []
        for b, gr in enumerate((g0, g1, g2)):
            sl = slice(b * D_MODEL, (b + 1) * D_MODEL)
            gt = _sigmoid(gr[...] + b_ref[:, sl])
            dp_ref[b] = (dmv * gt).astype(BF16)
            dgl = dmv * p_ref[b] * gt * (1.0 - gt)
            dgl_ref[:, sl] = dgl.astype(BF16)
            parts.append(jnp.sum(dgl, axis=0, keepdims=True))
        part = jnp.concatenate(parts, axis=1)

        @pl.when(i == 0)
        def _():
            db_ref[...] = part

        @pl.when(i > 0)
        def _():
            db_ref[...] += part

    gspec = lambda b: pl.BlockSpec((tm, D_MODEL), lambda i: (i, R_GL // D_MODEL + b))
    vec = pl.BlockSpec((1, 3 * D_MODEL), lambda i: (0, 0))
    return pl.pallas_call(
        body, name=name, grid=(S // tm,),
        in_specs=[pl.BlockSpec((tm, D_MODEL), lambda i: (i, 0)),
                  pl.BlockSpec((3, tm, D_MODEL), lambda i: (0, i, 0)), gspec(0), gspec(1), gspec(2), vec],
        out_specs=[pl.BlockSpec((3, tm, D_MODEL), lambda i: (0, i, 0)),
                   pl.BlockSpec((tm, 3 * D_MODEL), lambda i: (i, 0)), vec],
        out_shape=[jax.ShapeDtypeStruct((3, S, D_MODEL), BF16), jax.ShapeDtypeStruct((S, 3 * D_MODEL), BF16),
                   jax.ShapeDtypeStruct((1, 3 * D_MODEL), F32)],
        compiler_params=_cp(("arbitrary",)))(dm, prods, ur, ur, ur, b_merge)


def _rope_tables(pos):
    half = ROT // 2
    inv = ROPE_THETA ** (-jnp.arange(half, dtype=F32) / half)
    ang = pos.astype(F32)[:, None] * inv
    cos, sin = jnp.cos(ang), jnp.sin(ang)
    S = pos.shape[0]
    one = jnp.ones((S, LANES - ROT), F32)
    zero = jnp.zeros((S, LANES - ROT), F32)
    zh = jnp.zeros((S, half), F32)
    c = jnp.concatenate([cos, cos, one], axis=1)
    s1 = jnp.concatenate([-sin, zh, zero], axis=1)
    s2 = jnp.concatenate([zh, sin, zero], axis=1)
    return c, s1, s2


def _to_tiles(t):
    S, H = t.shape
    return t.reshape(S // LANES, LANES, H).transpose(0, 2, 1)


def _from_tiles(t):
    nt, H, _ = t.shape
    return t.transpose(1, 0, 2).reshape(H, nt * LANES)


def _local_step(x, mem, pos, tgt, g_pre, g_post, g_mem, wt, bf_pad, b_merge, w_kv, wbs, w_out):
    S = x.shape[0]
    T = min(512, S)
    nq = S // T
    tabs = _rope_tables(pos)

    h = _rms_fwd(x, g_pre, name="rms_pre")
    hs = [_to_classes(h, d) for d in DIL]
    tabs_g = [[_to_classes(t, d) for t in tabs] for d in DIL]
    uas = [_mm(hs[g], wt[f"A{g}"], bt=True, name=f"proj_a{g}", tn=1536) for g in range(3)]
    ub = _mm(h, wt["B"], bt=True, out_dtype=BF16, name="proj_b", tn=1536)
    ur = _mm(h, wt["R"], bt=True, name="proj_r", tn=1792)

    qkvs = [_rope_cast(uas[g], tabs_g[g], name=f"rope_a{g}") for g in range(3)]
    outs_c, lses_c = [], []
    for g in range(3):
        o, l = _attn_a_fwd(qkvs[g], g, name=f"attn_a_fwd{g}")
        outs_c.append(o)
        lses_c.append(l)
    outs_a = [_from_classes(o, d) for o, d in zip(outs_c, DIL)]
    lses_a = [_from_classes(l, d) for l, d in zip(lses_c, DIL)]
    ya = _merge_a_fwd(outs_a, lses_a, ur, name="merge_a_fwd")

    logf = _logf(ur, bf_pad, name="logf")
    c = _from_tiles(_cumsum_lanes(_to_tiles(logf[:, :B_HEADS]), False, name="cumsum_fwd"))
    crow = c.reshape(B_HEADS, nq, 1, T)
    ckb = jnp.broadcast_to(c[:, :, None], (B_HEADS, S, LANES))
    kt = ub[:, 512:1024].reshape(nq, T, 512).transpose(0, 2, 1)
    vt = ub[:, 1024:1536].reshape(nq, T, 512).transpose(0, 2, 1)
    ob, lse_b = _fox_fwd(ub, vt, crow, ckb, name="fox_fwd")
    yb = _gate_fwd(ob, ur, R_ZB, name="gate_b_fwd")

    hm = _rms_fwd(mem, g_mem, name="rms_mem")
    mkv = _mm(hm, w_kv, name="proj_mem")
    ym = _mem_fwd(ur, mkv, name="mem_fwd")

    merged, prods = _branch_fwd((ya, yb, ym), wbs, ur, b_merge, name="branch_fwd")
    out = _mm(merged, w_out, name="proj_out")
    dy, d_out, dg_post, loss_row = _post(x, out, tgt, g_post, name="post")

    dmerged = _mm(d_out, w_out, bt=True, name="d_merged")
    dw_out = _mm(merged, d_out, at=True, name="dw_out", tk=2048)
    dprods, dgl, db_merge = _branch_bwd(dmerged, prods, ur, b_merge, name="branch_bwd")
    dys, dwbs = [], []
    for i, (y, wb) in enumerate(zip((ya, yb, ym), wbs)):
        dys.append(_mm(dprods[i], wb, bt=True, name=f"d_y{i}"))
        dwbs.append(_mm(y, dprods[i], at=True, name=f"dw_branch{i}", tk=2048))

    dos_a, adjs_a, dza = _merge_a_bwd(outs_a, lses_a, ur, dys[0], name="merge_a_bwd")
    dus_a = []
    for g, d in enumerate(DIL):
        do_c, adj_c = _to_classes(dos_a[g], d), _to_classes(adjs_a[g], d)
        dq = _attn_a_dq(qkvs[g], tabs_g[g], g, do_c, lses_c[g], adj_c, name=f"attn_a_dq{g}")
        dk, dv = _attn_a_dkv(qkvs[g], tabs_g[g], g, do_c, lses_c[g], adj_c, name=f"attn_a_dkv{g}")
        dus_a.append(jnp.concatenate([dq, dk, dv], axis=1))

    dob, dzb = _gate_bwd(ob, ur, R_ZB, dys[1], name="gate_b_bwd")
    delta_b = _fox_delta(ob, dob, name="fox_delta")
    dkb, dvb, dc_k, dqt, dc_q = _fox_bwd(ub, kt, crow, ckb, dob, lse_b, delta_b, name="fox_bwd")
    dqb = (dqt.transpose(0, 2, 1).reshape(S, A_WIDTH) * B_SCALE).astype(BF16)
    du_b = jnp.concatenate([dqb, dkb, dvb], axis=1)
    dc = dc_q.reshape(B_HEADS, S) + dc_k.reshape(B_HEADS, S)
    dlogf = _from_tiles(_cumsum_lanes(_to_tiles(dc.T), True, name="cumsum_bwd"))
    dlogf_pad = jnp.pad(dlogf.T, ((0, 0), (0, FB_PAD - B_HEADS)))
    dfb, db_forget = _dfb(ur, bf_pad, dlogf_pad, name="dfb")

    dqm, dzm, dmk, dmv = _mem_bwd(ur, mkv, dys[2], name="mem_bwd")
    dmkv = jnp.concatenate([dmk, dmv], axis=1).astype(BF16)
    dhm = _mm(dmkv, w_kv, bt=True, name="d_hm")
    dw_kv = _mm(hm, dmkv, at=True, name="dw_kv")
    dg_mem = _rms_bwd(mem, g_mem, dhm, None, name="rms_mem_bwd")

    du_r = jnp.concatenate([dza, dzb, dqm, dzm, dgl, dfb], axis=1)
    dh = _mm(du_r, wt["R"], name="d_h_r", tk=1792) + _mm(du_b, wt["B"], name="d_h_b", tk=1536)
    for g, d in enumerate(DIL):
        dh = dh + _from_classes(_mm(dus_a[g], wt[f"A{g}"], name=f"d_h_a{g}", tk=1536), d)
    dwt = {"R": _mm(du_r, h, at=True, name="dw_in_r", tm=1792, tk=1024),
           "B": _mm(du_b, h, at=True, name="dw_in_b", tm=1536, tk=2048)}
    for g in range(3):
        dwt[f"A{g}"] = _mm(dus_a[g], hs[g], at=True, name=f"dw_in_a{g}", tm=1536, tk=2048)
    grad_x, dg_pre = _rms_bwd(x, g_pre, dh, dy, name="rms_pre_bwd")

    return dict(loss=loss_row, grad_x=grad_x, dwt=dwt, dw_kv=dw_kv, dwbs=dwbs, dw_out=dw_out,
                dg_pre=dg_pre, dg_post=dg_post, dg_mem=dg_mem, db_forget=db_forget, db_merge=db_merge)


MESH = pl.DeviceIdType.MESH
ANY = pl.BlockSpec(memory_space=pl.ANY)


def _relations():
    return [(k >> 2 & 1, k >> 1 & 1, k & 1) for k in range(1, N_DEV)]


def _coords():
    return lax.axis_index("x"), lax.axis_index("y"), lax.axis_index("c")


def _all_gather(shard, *, name):
    R, W = shard.shape

    def body(x_ref, out_ref, send_sems, recv_sems, local_sem):
        x, y, c = _coords()
        me, sibling = (x, y, c), (x, y, 1 - c)
        chips = [(1 - x, y), (x, 1 - y), (1 - x, 1 - y)]

        def slot(px, py, pc):
            return out_ref.at[4 * px + 2 * py + pc]

        def copy(k, block, to, src=None):
            return pltpu.make_async_remote_copy(
                src_ref=slot(*block) if src is None else src, dst_ref=slot(*block),
                send_sem=send_sems.at[k], recv_sem=recv_sems.at[k], device_id=to, device_id_type=MESH)

        mine = pltpu.make_async_copy(x_ref, slot(*me), local_sem)
        mine.start()
        first = [copy(0, me, sibling, src=x_ref)]
        first += [copy(1 + j, me, (*chip, c), src=x_ref) for j, chip in enumerate(chips)]
        for cp in first:
            cp.start()
        passed = [copy(4 + j, (*chip, c), sibling) for j, chip in enumerate(chips)]
        for j, chip in enumerate(chips):
            copy(1 + j, (*chip, c), me).wait_recv()
            passed[j].start()
        copy(0, sibling, me).wait_recv()
        for j, chip in enumerate(chips):
            copy(4 + j, (*chip, 1 - c), me).wait_recv()
        for cp in first + passed:
            cp.wait_send()
        mine.wait()

    return pl.pallas_call(
        body, name=name, out_shape=jax.ShapeDtypeStruct((N_DEV, R, W), shard.dtype),
        in_specs=[ANY], out_specs=ANY,
        scratch_shapes=[pltpu.SemaphoreType.DMA((N_DEV - 1,)), pltpu.SemaphoreType.DMA((N_DEV - 1,)),
                        pltpu.SemaphoreType.DMA],
    )(shard)


N_CHIP = 4


def _exchange_pair(gbig, *, name):
    _, R, W = gbig.shape

    def body(g_ref, sib_ref, send_sems, recv_sems):
        x, y, c = _coords()
        copies = []
        for r in range(N_CHIP):
            px, py = x ^ (r >> 1), y ^ (r & 1)
            copies.append(pltpu.make_async_remote_copy(
                src_ref=g_ref.at[4 * px + 2 * py + (1 - c)], dst_ref=sib_ref.at[r],
                send_sem=send_sems.at[r], recv_sem=recv_sems.at[r], device_id=(x, y, 1 - c), device_id_type=MESH))
        for cp in copies:
            cp.start()
        for cp in copies:
            cp.wait_recv()
        for cp in copies:
            cp.wait_send()

    return pl.pallas_call(
        body, name=name, out_shape=jax.ShapeDtypeStruct((N_CHIP, R, W), gbig.dtype),
        in_specs=[ANY], out_specs=ANY,
        scratch_shapes=[pltpu.SemaphoreType.DMA((N_CHIP,)), pltpu.SemaphoreType.DMA((N_CHIP,))],
    )(gbig)


def _own_slabs():
    x, y, c = _coords()
    return jnp.stack([4 * (x ^ (r >> 1)) + 2 * (y ^ (r & 1)) + c for r in range(N_CHIP)]).astype(jnp.int32)


def _pair_sum(gbig, sib, own_idx, tr, *, name):
    _, R, W = gbig.shape

    def body(idx_ref, a_ref, b_ref, o_ref):
        o_ref[...] = (a_ref[...] + b_ref[...]).astype(BF16)

    return pl.pallas_call(
        body, name=name,
        grid_spec=pltpu.PrefetchScalarGridSpec(
            num_scalar_prefetch=1, grid=(N_CHIP - 1, R // tr),
            in_specs=[pl.BlockSpec((None, tr, W), lambda r, i, idx: (idx[r + 1], i, 0)),
                      pl.BlockSpec((None, tr, W), lambda r, i, idx: (r + 1, i, 0))],
            out_specs=pl.BlockSpec((None, tr, W), lambda r, i, idx: (r, i, 0))),
        out_shape=jax.ShapeDtypeStruct((N_CHIP - 1, R, W), BF16),
        compiler_params=_cp(("parallel", "parallel")))(own_idx, gbig, sib)


def _exchange_chips(send, gsmall, *, name):
    nb, R, W = send.shape
    n = N_DEV - 1

    def body(b_ref, s_ref, rb_ref, rs_ref, send_sems, recv_sems, local_sem):
        x, y, c = _coords()
        me = 4 * x + 2 * y + c
        mine = pltpu.make_async_copy(s_ref, rs_ref.at[me], local_sem)
        mine.start()
        started = []
        for k, (fx, fy, fc) in enumerate(_relations()):
            cp = pltpu.make_async_remote_copy(
                src_ref=s_ref, dst_ref=rs_ref.at[me], send_sem=send_sems.at[k], recv_sem=recv_sems.at[k],
                device_id=(x ^ fx, y ^ fy, c ^ fc), device_id_type=MESH)
            cp.start()
            started.append(cp)
        for r in range(1, N_CHIP):
            cp = pltpu.make_async_remote_copy(
                src_ref=b_ref.at[r - 1], dst_ref=rb_ref.at[r - 1], send_sem=send_sems.at[n + r - 1],
                recv_sem=recv_sems.at[n + r - 1], device_id=(x ^ (r >> 1), y ^ (r & 1), c), device_id_type=MESH)
            cp.start()
            started.append(cp)
        for k, (fx, fy, fc) in enumerate(_relations()):
            peer = 4 * (x ^ fx) + 2 * (y ^ fy) + (c ^ fc)
            pltpu.make_async_remote_copy(
                src_ref=s_ref, dst_ref=rs_ref.at[peer], send_sem=send_sems.at[k], recv_sem=recv_sems.at[k],
                device_id=(x ^ fx, y ^ fy, c ^ fc), device_id_type=MESH).wait_recv()
        for r in range(1, N_CHIP):
            pltpu.make_async_remote_copy(
                src_ref=b_ref.at[r - 1], dst_ref=rb_ref.at[r - 1], send_sem=send_sems.at[n + r - 1],
                recv_sem=recv_sems.at[n + r - 1], device_id=(x ^ (r >> 1), y ^ (r & 1), c),
                device_id_type=MESH).wait_recv()
        for cp in started:
            cp.wait_send()
        mine.wait()

    return pl.pallas_call(
        body, name=name,
        out_shape=[jax.ShapeDtypeStruct((nb, R, W), send.dtype),
                   jax.ShapeDtypeStruct((N_DEV, 1, P_SMALL), gsmall.dtype)],
        in_specs=[ANY, ANY], out_specs=[ANY, ANY],
        scratch_shapes=[pltpu.SemaphoreType.DMA((n + nb,)), pltpu.SemaphoreType.DMA((n + nb,)),
                        pltpu.SemaphoreType.DMA],
    )(send, gsmall)


def _part_specs(parts, tr, row0):
    assert row0 % tr == 0
    specs = []
    for a, n_used in parts:
        if n_used is None:
            specs.append(pl.BlockSpec((1, tr, a.shape[2]), lambda i, idx: (idx[0], row0 // tr + i, 0)))
        else:
            specs.append(pl.BlockSpec((n_used, tr, a.shape[2]), lambda i, idx: (0, row0 // tr + i, 0)))
    return specs


def _part_total(refs, parts):
    g = None
    for ref, (_, n_used) in zip(refs, parts):
        for k in range(n_used or 1):
            t = ref[k].astype(F32)
            g = t if g is None else g + t
    return g


def _sum_parts(parts, idx, row0, nrows, tr, *, name):
    W = parts[0][0].shape[2]
    assert nrows % tr == 0

    def body(idx_ref, *refs):
        refs[-1][...] = _part_total(refs[:-1], parts)

    return pl.pallas_call(
        body, name=name,
        grid_spec=pltpu.PrefetchScalarGridSpec(
            num_scalar_prefetch=1, grid=(nrows // tr,), in_specs=_part_specs(parts, tr, row0),
            out_specs=pl.BlockSpec((tr, W), lambda i, idx: (i, 0))),
        out_shape=jax.ShapeDtypeStruct((nrows, W), F32),
        compiler_params=_cp(("parallel",)))(idx, *[a for a, _ in parts])


def _adamw(parts, idx, w, m, v, tr, *, name):
    R, W = w.shape
    assert R % tr == 0
    np_ = len(parts)

    def body(idx_ref, *refs):
        w_ref, m_ref, v_ref, g_ref, d_ref, nm_ref, nv_ref = refs[np_:]
        g = _part_total(refs[:np_], parts)
        mm = ADAM_B1 * m_ref[...] + (1.0 - ADAM_B1) * g
        vv = ADAM_B2 * v_ref[...] + (1.0 - ADAM_B2) * (g * g)
        m_hat = mm / (1.0 - ADAM_B1 ** ADAM_STEP)
        v_hat = vv / (1.0 - ADAM_B2 ** ADAM_STEP)
        g_ref[...] = g
        d_ref[...] = -ADAM_LR * (m_hat / (jnp.sqrt(v_hat) + ADAM_EPS) + ADAM_WD * w_ref[...])
        nm_ref[...] = mm
        nv_ref[...] = vv

    blk = pl.BlockSpec((tr, W), lambda i, idx: (i, 0))
    return pl.pallas_call(
        body, name=name,
        grid_spec=pltpu.PrefetchScalarGridSpec(
            num_scalar_prefetch=1, grid=(R // tr,), in_specs=_part_specs(parts, tr, 0) + [blk, blk, blk],
            out_specs=[blk] * 4),
        out_shape=[jax.ShapeDtypeStruct((R, W), F32)] * 4,
        compiler_params=_cp(("parallel",)))(idx, *[a for a, _ in parts], w, m, v)


def _pack_rest(w_kv, wa, wb, wm, w_out):
    return jnp.concatenate([w_kv[0], w_out[0]] + [t[0].reshape(-1, D_MODEL) for t in (wa, wb, wm)], axis=0)


def _unpack_rest(t):
    br = lambda i: t[RO_BR + 64 * i:RO_BR + 64 * (i + 1)].reshape(1, A_WIDTH, D_MODEL // N_DEV)
    return t[None, RO_KV:RO_OUT], br(0), br(1), br(2), t[None, RO_OUT:RO_BR]


def _orig_rows(gathered, a, b):
    res = []
    while a < b:
        dev, r = divmod(a, CS)
        n = min(b - a, CS - r)
        res.append(gathered[dev, RO_IN + r:RO_IN + r + n])
        a += n
    return res


def _full_weights(gathered):
    wt = {}
    for name, ranges in SEGS.items():
        rows = [p for a, b in ranges for p in _orig_rows(gathered, a, b)]
        if SEG_PAD[name]:
            rows.append(jnp.zeros((SEG_PAD[name], D_MODEL), gathered.dtype))
        wt[name] = jnp.concatenate(rows, axis=0)
    w_kv = gathered[:, RO_KV:RO_OUT].reshape(D_MODEL, D_MODEL)
    w_out = gathered[:, RO_OUT:RO_BR].reshape(D_MODEL, D_MODEL)
    wbs = [gathered[:, RO_BR + 64 * i:RO_BR + 64 * (i + 1)].reshape(N_DEV, A_WIDTH, D_MODEL // N_DEV)
           .transpose(1, 0, 2).reshape(A_WIDTH, D_MODEL) for i in range(3)]
    return wt, w_kv, wbs, w_out


def _orig_order(dwt):
    pieces = []
    for name, ranges in SEGS.items():
        o = 0
        for a, b in ranges:
            pieces.append((a, dwt[name][o:o + b - a]))
            o += b - a
    pieces.sort(key=lambda p: p[0])
    return jnp.concatenate([p[1] for p in pieces], axis=0)


def _pack_grads(dwt, dw_kv, dwbs, dw_out):
    g_in = jnp.pad(_orig_order(dwt).reshape(N_DEV, CS, D_MODEL), ((0, 0), (0, IN_ROWS - CS), (0, 0)))
    br = [t.reshape(A_WIDTH, N_DEV, D_MODEL // N_DEV).transpose(1, 0, 2).reshape(N_DEV, -1, D_MODEL) for t in dwbs]
    return jnp.concatenate([dw_kv.reshape(N_DEV, -1, D_MODEL), dw_out.reshape(N_DEV, -1, D_MODEL)] + br + [g_in],
                           axis=1)


def kernel(x, mem, positions, norm_pre_g, norm_post_g, norm_mem_g, w_in, b_forget, b_merge, w_mem_kv, w_branch_a, w_branch_b, w_branch_m, w_out, loss_target, m_norm_pre_g, m_norm_post_g, m_norm_mem_g, m_w_in, m_b_forget, m_b_merge, m_w_mem_kv, m_w_branch_a, m_w_branch_b, m_w_branch_m, m_w_out, v_norm_pre_g, v_norm_post_g, v_norm_mem_g, v_w_in, v_b_forget, v_b_merge, v_w_mem_kv, v_w_branch_a, v_w_branch_b, v_w_branch_m, v_w_out):
    w_rest = _pack_rest(w_mem_kv, w_branch_a, w_branch_b, w_branch_m, w_out)
    shard = jnp.concatenate([w_rest.astype(BF16), w_in[0].T.astype(BF16),
                             jnp.zeros((IN_ROWS - CS, D_MODEL), BF16)], axis=0)
    gathered = _all_gather(shard, name="gather_weights")
    wt, w_kv, wbs, w_o = _full_weights(gathered)

    bf_pad = jnp.pad(b_forget, ((0, 0), (0, FB_PAD - B_HEADS)))
    r = _local_step(x[0], mem[0], positions[0], loss_target[0], norm_pre_g, norm_post_g, norm_mem_g,
                    wt, bf_pad, b_merge, w_kv, wbs, w_o)

    gbig = _pack_grads(r["dwt"], r["dw_kv"], r["dwbs"], r["dw_out"])
    gsmall = jnp.concatenate([r["dg_pre"], r["dg_post"], r["dg_mem"], r["db_merge"],
                              r["db_forget"][:, :LANES], r["loss"]], axis=1)
    own_idx = _own_slabs()
    sib = _exchange_pair(gbig, name="exchange_pair")
    send = _pair_sum(gbig, sib, own_idx, 208, name="pair_sum")
    recv, rsmall = _exchange_chips(send, gsmall, name="exchange_chips")
    parts = [(gbig, None), (sib, 1), (recv, N_CHIP - 1)]

    m_rest = _pack_rest(m_w_mem_kv, m_w_branch_a, m_w_branch_b, m_w_branch_m, m_w_out)
    v_rest = _pack_rest(v_w_mem_kv, v_w_branch_a, v_w_branch_b, v_w_branch_m, v_w_out)
    outs_rest = [_unpack_rest(t) for t in _adamw(parts, own_idx, w_rest, m_rest, v_rest, 64, name="adamw_rest")]
    g_in = _sum_parts(parts, own_idx, RO_IN, IN_ROWS, 16, name="sum_w_in")[:CS].T
    outs_in = _adamw([(g_in[None], 1)], own_idx, w_in[0], m_w_in[0], v_w_in[0], 128, name="adamw_w_in")

    def small_vec(a, b, c, d, e):
        z = jnp.zeros((1, LANES - B_HEADS), F32)
        return jnp.concatenate([a, b, c, d, e, z, jnp.zeros((1, LANES), F32)], axis=1)

    outs_small = _adamw([(rsmall, N_DEV)], own_idx, small_vec(norm_pre_g, norm_post_g, norm_mem_g, b_merge, b_forget),
                        small_vec(m_norm_pre_g, m_norm_post_g, m_norm_mem_g, m_b_merge, m_b_forget),
                        small_vec(v_norm_pre_g, v_norm_post_g, v_norm_mem_g, v_b_merge, v_b_forget),
                        1, name="adamw_small")

    def small_parts(t):
        return [t[:, O_GPRE:O_GPRE + D_MODEL], t[:, O_GPOST:O_GPOST + D_MODEL], t[:, O_GMEM:O_GMEM + D_MODEL],
                t[:, O_BF:O_BF + B_HEADS], t[:, O_BM:O_BM + 3 * D_MODEL]]

    loss = outs_small[0][0, O_LOSS]
    result = [loss, r["grad_x"][None]]
    for rest, w_i, small in zip(outs_rest, outs_in, outs_small):
        gp, gq, gm, bf, bm = small_parts(small)
        w_k, w_a, w_b, w_m, w_ot = rest
        result += [gp, gq, gm, w_i[None], bf, bm, w_k, w_a, w_b, w_m, w_ot]
    return tuple(result)
```

```python
import jax
import jax.numpy as jnp
from jax import lax
from jax.experimental import pallas as pl
from jax.experimental.pallas import tpu as pltpu

F32 = jnp.float32
BF16 = jnp.bfloat16

N_DEV = 8
D_MODEL = 1024
N_MEM = 256
EPS = 1e-6
NEG = -1e30
ROPE_THETA = 500000.0
DIL = (1, 4, 16)
A_HEADS = 4
HEAD = 128
A_WIDTH = 512
B_HEADS = 8
B_HEAD = 64
M_HEADS = 4
ROT = 32
IN_COLS = 11272
FB_PAD = 256

SEGS = {
    "A0": ((0, 512), (1536, 2048), (3072, 3584)),
    "A1": ((512, 1024), (2048, 2560), (3584, 4096)),
    "A2": ((1024, 1536), (2560, 3072), (4096, 4608)),
    "B": ((5120, 6656),),
    "R": ((4608, 5120), (6664, 7176), (7176, 7688), (7688, 8200), (8200, 11272), (6656, 6664)),
}
SEG_PAD = {"A0": 0, "A1": 0, "A2": 0, "B": 0, "R": FB_PAD - B_HEADS}
R_ZA, R_ZB, R_QM, R_ZM, R_GL, R_FB = 0, 512, 1024, 1536, 2048, 5120
NR = R_FB + FB_PAD

ADAM_LR, ADAM_B1, ADAM_B2, ADAM_EPS, ADAM_WD, ADAM_STEP = 0.001, 0.9, 0.999, 1e-08, 0.01, 10

LANES = 128
VMEM_LIMIT = 56 * 1024 * 1024

CS = IN_COLS // N_DEV
RO_KV, RO_OUT, RO_BR, RO_IN = 0, 128, 256, 448
IN_ROWS = 1424
ROWS = RO_IN + IN_ROWS
O_GPRE, O_GPOST, O_GMEM, O_BM, O_BF, O_LOSS = 0, 1024, 2048, 3072, 6144, 6272
P_SMALL = 6400


def _cp(sem=None):
    return pltpu.CompilerParams(dimension_semantics=sem, vmem_limit_bytes=VMEM_LIMIT)


def _dot(a, b):
    return jnp.dot(a, b, preferred_element_type=F32)


def _dot_nt(a, b):
    return lax.dot_general(a, b, (((1,), (1,)), ((), ())), preferred_element_type=F32)


def _sigmoid(z):
    return 1.0 / (1.0 + jnp.exp(-z))


def _mm(a, b, *, name, at=False, bt=False, out_dtype=F32, tm=1024, tn=1024, tk=None):
    assert not (at and bt)
    K, M = a.shape if at else a.shape[::-1]
    N = b.shape[0] if bt else b.shape[1]
    tm, tn = min(tm, M), min(tn, N)
    tk = K if tk is None else min(tk, K)
    assert M % tm == 0 and N % tn == 0 and K % tk == 0
    nk = K // tk

    def body(a_ref, b_ref, o_ref, acc_ref):
        av = a_ref[...].astype(BF16)
        bv = b_ref[...].astype(BF16)
        if at:
            p = lax.dot_general(av, bv, (((0,), (0,)), ((), ())), preferred_element_type=F32)
        else:
            p = _dot_nt(av, bv) if bt else _dot(av, bv)
        if nk == 1:
            o_ref[...] = p.astype(out_dtype)
        else:
            k = pl.program_id(2)

            @pl.when(k == 0)
            def _():
                acc_ref[...] = p

            @pl.when(k > 0)
            def _():
                acc_ref[...] += p

            @pl.when(k == nk - 1)
            def _():
                o_ref[...] = acc_ref[...].astype(out_dtype)

    b_spec = (pl.BlockSpec((tn, tk), lambda i, j, k: (j, k)) if bt
              else pl.BlockSpec((tk, tn), lambda i, j, k: (k, j)))
    a_spec = (pl.BlockSpec((tk, tm), lambda i, j, k: (k, i)) if at
              else pl.BlockSpec((tm, tk), lambda i, j, k: (i, k)))
    return pl.pallas_call(
        body, name=name, grid=(M // tm, N // tn, nk),
        in_specs=[a_spec, b_spec],
        out_specs=pl.BlockSpec((tm, tn), lambda i, j, k: (i, j)),
        out_shape=jax.ShapeDtypeStruct((M, N), out_dtype),
        scratch_shapes=[pltpu.VMEM((tm, tn) if nk > 1 else (8, LANES), F32)],
        compiler_params=_cp(("parallel", "parallel", "arbitrary")),
    )(a, b)


def _rms_fwd(x, g, *, name):
    S, D = x.shape
    tm = min(512, S)

    def body(x_ref, g_ref, o_ref):
        xv = x_ref[...]
        r = lax.rsqrt(jnp.mean(xv * xv, axis=-1, keepdims=True) + EPS)
        o_ref[...] = (xv * r * g_ref[...]).astype(BF16)

    return pl.pallas_call(
        body, name=name, grid=(S // tm,),
        in_specs=[pl.BlockSpec((tm, D), lambda i: (i, 0)), pl.BlockSpec((1, D), lambda i: (0, 0))],
        out_specs=pl.BlockSpec((tm, D), lambda i: (i, 0)),
        out_shape=jax.ShapeDtypeStruct((S, D), BF16),
        compiler_params=_cp(("parallel",)),
    )(x, g)


def _rms_bwd(x, g, dh, dy, *, name):
    S, D = x.shape
    tm = min(512, S)
    want_dx = dy is not None

    def body(*refs):
        if want_dx:
            x_ref, g_ref, dh_ref, dy_ref, dx_ref, dg_ref = refs
        else:
            x_ref, g_ref, dh_ref, dg_ref = refs
        i = pl.program_id(0)
        xv = x_ref[...]
        r = lax.rsqrt(jnp.mean(xv * xv, axis=-1, keepdims=True) + EPS)
        xh = xv * r
        dhv = dh_ref[...]
        part = jnp.sum(dhv * xh, axis=0, keepdims=True)

        @pl.when(i == 0)
        def _():
            dg_ref[...] = part

        @pl.when(i > 0)
        def _():
            dg_ref[...] += part

        if want_dx:
            dxh = dhv * g_ref[...]
            dx_ref[...] = dy_ref[...] + r * (dxh - xh * jnp.mean(dxh * xh, axis=-1, keepdims=True))

    row = pl.BlockSpec((tm, D), lambda i: (i, 0))
    vec = pl.BlockSpec((1, D), lambda i: (0, 0))
    if want_dx:
        return pl.pallas_call(
            body, name=name, grid=(S // tm,), in_specs=[row, vec, row, row], out_specs=[row, vec],
            out_shape=[jax.ShapeDtypeStruct((S, D), F32), jax.ShapeDtypeStruct((1, D), F32)],
            compiler_params=_cp(("arbitrary",)))(x, g, dh, dy)
    return pl.pallas_call(
        body, name=name, grid=(S // tm,), in_specs=[row, vec, row], out_specs=vec,
        out_shape=jax.ShapeDtypeStruct((1, D), F32),
        compiler_params=_cp(("arbitrary",)))(x, g, dh)


def _post(x, out, tgt, g, *, name):
    S, D = x.shape
    tm = min(512, S)

    def body(x_ref, o_ref, t_ref, g_ref, dy_ref, do_ref, dg_ref, loss_ref):
        i = pl.program_id(0)
        ov = o_ref[...]
        r = lax.rsqrt(jnp.mean(ov * ov, axis=-1, keepdims=True) + EPS)
        n = ov * r
        gv = g_ref[...]
        e = (x_ref[...] + n * gv) - t_ref[...]
        lpart = 0.5 * jnp.sum(jnp.mean(e * e, axis=-1, keepdims=True), axis=0, keepdims=True)
        dy = e * (1.0 / D)
        dy_ref[...] = dy
        dn = dy * gv
        do_ref[...] = (r * (dn - n * jnp.mean(dn * n, axis=-1, keepdims=True))).astype(BF16)
        gpart = jnp.sum(dy * n, axis=0, keepdims=True)
        lrow = jnp.broadcast_to(lpart, (1, LANES))

        @pl.when(i == 0)
        def _():
            dg_ref[...] = gpart
            loss_ref[...] = lrow

        @pl.when(i > 0)
        def _():
            dg_ref[...] += gpart
            loss_ref[...] += lrow

    row = pl.BlockSpec((tm, D), lambda i: (i, 0))
    vec = pl.BlockSpec((1, D), lambda i: (0, 0))
    return pl.pallas_call(
        body, name=name, grid=(S // tm,), in_specs=[row, row, row, vec],
        out_specs=[row, row, vec, pl.BlockSpec((1, LANES), lambda i: (0, 0))],
        out_shape=[jax.ShapeDtypeStruct((S, D), F32), jax.ShapeDtypeStruct((S, D), BF16),
                   jax.ShapeDtypeStruct((1, D), F32), jax.ShapeDtypeStruct((1, LANES), F32)],
        compiler_params=_cp(("arbitrary",)))(x, out, tgt, g)


def _to_classes(t, d):
    if d == 1:
        return t
    S, C = t.shape
    return t.reshape(S // d, d, C).transpose(1, 0, 2).reshape(S, C)


def _from_classes(t, d):
    if d == 1:
        return t
    S, C = t.shape
    return t.reshape(d, S // d, C).transpose(1, 0, 2).reshape(S, C)


def _rope(x, c, s1, s2):
    return x * c + pltpu.roll(x, LANES - ROT // 2, 1) * s1 + pltpu.roll(x, ROT // 2, 1) * s2


def _unrope(d, c, s1, s2):
    return d * c + pltpu.roll(d * s1, ROT // 2, 1) + pltpu.roll(d * s2, LANES - ROT // 2, 1)


def _a_band(qb):
    r = lax.broadcasted_iota(jnp.int32, (qb, qb + HEAD), 0)
    c = lax.broadcasted_iota(jnp.int32, (qb, qb + HEAD), 1)
    return jnp.logical_and(c >= r, c <= r + HEAD)


def _a_first_ok(qb, n):
    c = lax.broadcasted_iota(jnp.int32, (qb, qb + HEAD), 1)
    return jnp.logical_or(c >= HEAD, n > 0)


def _a_last_ok(qb, has_next):
    c = lax.broadcasted_iota(jnp.int32, (qb, qb + HEAD), 1)
    return jnp.logical_or(c < qb, has_next)


A_SCALE = HEAD ** -0.5


def _a_geometry(S, g):
    d = DIL[g]
    L = S // d
    TQ = min(512, L)
    return d, L, TQ, TQ // HEAD, L // TQ, L // HEAD


def _rope_cast(ua, tabs, *, name):
    S = ua.shape[0]
    tm = min(512, S)

    def body(u_ref, c_ref, s1_ref, s2_ref, o_ref):
        tc = (c_ref[...], s1_ref[...], s2_ref[...])
        for j in range(3 * A_HEADS):
            sl = slice(j * HEAD, (j + 1) * HEAD)
            t = u_ref[:, sl]
            o_ref[:, sl] = (_rope(t, *tc) if j < 2 * A_HEADS else t).astype(BF16)

    blk = pl.BlockSpec((tm, 3 * A_WIDTH), lambda i: (i, 0))
    tab = pl.BlockSpec((tm, LANES), lambda i: (i, 0))
    return pl.pallas_call(
        body, name=name, grid=(S // tm,), in_specs=[blk, tab, tab, tab], out_specs=blk,
        out_shape=jax.ShapeDtypeStruct((S, 3 * A_WIDTH), BF16),
        compiler_params=_cp(("parallel",)))(ua, *tabs)


def _attn_a_fwd(qkv, g, *, name):
    S = qkv.shape[0]
    d, L, TQ, nsub, nb, nblk = _a_geometry(S, g)

    def body(q_ref, kc_ref, kp_ref, vc_ref, vp_ref, o_ref, l_ref):
        n = pl.program_id(1)
        q, kc, kp, vc, vp = q_ref[...], kc_ref[...], kp_ref[...], vc_ref[...], vp_ref[...]
        QB = min(2 * HEAD, TQ)
        band = _a_band(QB)
        for hh in range(TQ // QB):
            sl = slice(hh * QB, (hh + 1) * QB)
            pv = slice(hh * QB - HEAD, hh * QB)
            kcat = jnp.concatenate([kp if hh == 0 else kc[pv], kc[sl]], axis=0)
            vcat = jnp.concatenate([vp if hh == 0 else vc[pv], vc[sl]], axis=0)
            msk = jnp.logical_and(band, _a_first_ok(QB, n)) if hh == 0 else band
            s = jnp.where(msk, _dot_nt(q[sl], kcat) * A_SCALE, NEG)
            m = jnp.max(s, axis=-1, keepdims=True)
            p = jnp.exp(s - m)
            den = jnp.sum(p, axis=-1, keepdims=True)
            o_ref[sl, :] = _dot(p.astype(BF16), vcat) / den
            l_ref[sl, :] = jnp.broadcast_to(m + jnp.log(den), (QB, HEAD))

    rcur = lambda cb, n: (cb // A_HEADS) * nb + n
    rprv = lambda cb, n: (cb // A_HEADS) * nblk + jnp.maximum(n * nsub - 1, 0)
    cur = lambda off: pl.BlockSpec((TQ, HEAD), lambda cb, n: (rcur(cb, n), off + cb % A_HEADS))
    prv = lambda off: pl.BlockSpec((HEAD, HEAD), lambda cb, n: (rprv(cb, n), off + cb % A_HEADS))
    out = pl.BlockSpec((TQ, HEAD), lambda cb, n: (rcur(cb, n), cb % A_HEADS))
    return pl.pallas_call(
        body, name=name, grid=(A_HEADS * d, nb),
        in_specs=[cur(0), cur(4), prv(4), cur(8), prv(8)],
        out_specs=[out, out],
        out_shape=[jax.ShapeDtypeStruct((S, A_WIDTH), F32)] * 2,
        compiler_params=_cp(("parallel", "parallel")),
    )(qkv, qkv, qkv, qkv, qkv)


def _attn_a_dq(qkv, tabs, g, do, lse, adj, *, name):
    S = qkv.shape[0]
    d, L, TQ, nsub, nb, nblk = _a_geometry(S, g)

    def body(q_ref, kc_ref, kp_ref, vc_ref, vp_ref, do_ref, l_ref, adj_ref, c_ref, s1_ref, s2_ref, dq_ref):
        n = pl.program_id(1)
        q, kc, kp, vc, vp = q_ref[...], kc_ref[...], kp_ref[...], vc_ref[...], vp_ref[...]
        QB = min(2 * HEAD, TQ)
        band = _a_band(QB)
        for hh in range(TQ // QB):
            sl = slice(hh * QB, (hh + 1) * QB)
            pv = slice(hh * QB - HEAD, hh * QB)
            kcat = jnp.concatenate([kp if hh == 0 else kc[pv], kc[sl]], axis=0)
            vcat = jnp.concatenate([vp if hh == 0 else vc[pv], vc[sl]], axis=0)
            msk = jnp.logical_and(band, _a_first_ok(QB, n)) if hh == 0 else band
            doh = do_ref[sl, :]
            p = jnp.exp(jnp.where(msk, _dot_nt(q[sl], kcat) * A_SCALE, NEG) - l_ref[sl, :][:, :1])
            ds = p * (_dot_nt(doh, vcat) + adj_ref[sl, :][:, :1])
            dq = _dot(ds.astype(BF16), kcat) * A_SCALE
            dq_ref[sl, :] = _unrope(dq, c_ref[sl, :], s1_ref[sl, :], s2_ref[sl, :]).astype(BF16)

    rcur = lambda cb, n: (cb // A_HEADS) * nb + n
    rprv = lambda cb, n: (cb // A_HEADS) * nblk + jnp.maximum(n * nsub - 1, 0)
    cur = lambda off: pl.BlockSpec((TQ, HEAD), lambda cb, n: (rcur(cb, n), off + cb % A_HEADS))
    prv = lambda off: pl.BlockSpec((HEAD, HEAD), lambda cb, n: (rprv(cb, n), off + cb % A_HEADS))
    tcur = pl.BlockSpec((TQ, LANES), lambda cb, n: (rcur(cb, n), 0))
    blk = pl.BlockSpec((TQ, HEAD), lambda cb, n: (rcur(cb, n), cb % A_HEADS))
    return pl.pallas_call(
        body, name=name, grid=(A_HEADS * d, nb),
        in_specs=[cur(0), cur(4), prv(4), cur(8), prv(8), blk, blk, blk, tcur, tcur, tcur],
        out_specs=blk,
        out_shape=jax.ShapeDtypeStruct((S, A_WIDTH), BF16),
        compiler_params=_cp(("parallel", "parallel")),
    )(qkv, qkv, qkv, qkv, qkv, do, lse, adj, *tabs)


def _attn_a_dkv(qkv, tabs, g, do, lse, adj, *, name):
    S = qkv.shape[0]
    d, L, TQ, nsub, nb, nblk = _a_geometry(S, g)

    def body(qc_ref, qn_ref, kc_ref, vc_ref, doc_ref, don_ref, lc_ref, ln_ref, ac_ref, an_ref,
             c_ref, s1_ref, s2_ref, dk_ref, dv_ref):
        n = pl.program_id(1)
        qc, qn, kc, vc = qc_ref[...], qn_ref[...], kc_ref[...], vc_ref[...]
        QB = min(2 * HEAD, TQ)
        nh = TQ // QB
        band = _a_band(QB)
        for kh in range(nh):
            sl = slice(kh * QB, (kh + 1) * QB)
            nx = slice((kh + 1) * QB, (kh + 1) * QB + HEAD)
            last = kh == nh - 1
            cat = lambda cur, nxt: jnp.concatenate([cur[sl], nxt[...] if last else cur[nx]], axis=0)
            qcat = cat(qc, qn)
            docat = cat(doc_ref, don_ref)
            lt = cat(lc_ref, ln_ref).T[:1, :]
            at = cat(ac_ref, an_ref).T[:1, :]
            msk = jnp.logical_and(band, _a_last_ok(QB, n < nb - 1)) if last else band
            st = jnp.where(msk, _dot_nt(kc[sl], qcat) * A_SCALE, NEG)
            pt = jnp.exp(st - lt)
            dv_ref[sl, :] = _dot(pt.astype(BF16), docat).astype(BF16)
            dst = pt * (_dot_nt(vc[sl], docat) + at)
            dk = _dot(dst.astype(BF16), qcat) * A_SCALE
            dk_ref[sl, :] = _unrope(dk, c_ref[sl, :], s1_ref[sl, :], s2_ref[sl, :]).astype(BF16)

    rcur = lambda cb, n: (cb // A_HEADS) * nb + n
    rnxt = lambda cb, n: (cb // A_HEADS) * nblk + jnp.minimum((n + 1) * nsub, nblk - 1)
    cur = lambda off: pl.BlockSpec((TQ, HEAD), lambda cb, n: (rcur(cb, n), off + cb % A_HEADS))
    nxu = lambda off: pl.BlockSpec((HEAD, HEAD), lambda cb, n: (rnxt(cb, n), off + cb % A_HEADS))
    tcur = pl.BlockSpec((TQ, LANES), lambda cb, n: (rcur(cb, n), 0))
    blk = pl.BlockSpec((TQ, HEAD), lambda cb, n: (rcur(cb, n), cb % A_HEADS))
    bnx = pl.BlockSpec((HEAD, HEAD), lambda cb, n: (rnxt(cb, n), cb % A_HEADS))
    return pl.pallas_call(
        body, name=name, grid=(A_HEADS * d, nb),
        in_specs=[cur(0), nxu(0), cur(4), cur(8), blk, bnx, blk, bnx, blk, bnx, tcur, tcur, tcur],
        out_specs=[blk, blk],
        out_shape=[jax.ShapeDtypeStruct((S, A_WIDTH), BF16)] * 2,
        compiler_params=_cp(("parallel", "parallel")),
    )(qkv, qkv, qkv, qkv, do, do, lse, lse, adj, adj, *tabs)


def _silu_parts(z):
    sg = _sigmoid(z)
    return z * sg, sg * (1.0 + z * (1.0 - sg))


def _merge_a_fwd(os_, ls_, ur, *, name):
    S = ur.shape[0]
    tm = min(512, S)

    def body(o0, o1, o2, l0, l1, l2, z_ref, y_ref):
        ls = [l0[...], l1[...], l2[...]]
        mx = jnp.maximum(jnp.maximum(ls[0], ls[1]), ls[2])
        es = [jnp.exp(l - mx) for l in ls]
        den = es[0] + es[1] + es[2]
        y = (es[0] / den) * o0[...] + (es[1] / den) * o1[...] + (es[2] / den) * o2[...]
        y_ref[...] = (y * _silu_parts(z_ref[...])[0]).astype(BF16)

    blk = pl.BlockSpec((tm, A_WIDTH), lambda i: (i, 0))
    return pl.pallas_call(
        body, name=name, grid=(S // tm,),
        in_specs=[blk] * 6 + [pl.BlockSpec((tm, A_WIDTH), lambda i: (i, R_ZA // A_WIDTH))],
        out_specs=blk, out_shape=jax.ShapeDtypeStruct((S, A_WIDTH), BF16),
        compiler_params=_cp(("parallel",)))(*os_, *ls_, ur)


def _merge_a_bwd(os_, ls_, ur, dya, *, name):
    S = ur.shape[0]
    tm = min(256, S)

    def body(o0, o1, o2, l0, l1, l2, z_ref, dy_ref, d0, d1, d2, a0, a1, a2, dz_ref):
        ls = [l0[...], l1[...], l2[...]]
        ov = [o0[...], o1[...], o2[...]]
        mx = jnp.maximum(jnp.maximum(ls[0], ls[1]), ls[2])
        es = [jnp.exp(l - mx) for l in ls]
        den = es[0] + es[1] + es[2]
        ws = [e / den for e in es]
        y = ws[0] * ov[0] + ws[1] * ov[1] + ws[2] * ov[2]
        sz, dsz = _silu_parts(z_ref[...])
        dyv = dy_ref[...]
        dz_ref[...] = (dyv * y * dsz).astype(BF16)
        dyp = dyv * sz
        for h in range(A_HEADS):
            sl = slice(h * HEAD, (h + 1) * HEAD)
            t = jnp.zeros((tm, 1), F32)
            for gi in range(3):
                t = t + ws[gi][:, sl][:, :1] * jnp.sum(dyp[:, sl] * ov[gi][:, sl], axis=-1, keepdims=True)
            for gi, (dref, aref) in enumerate(((d0, a0), (d1, a1), (d2, a2))):
                wg = ws[gi][:, sl]
                dref[:, sl] = (wg * dyp[:, sl]).astype(BF16)
                aref[:, sl] = -wg * t

    blk = pl.BlockSpec((tm, A_WIDTH), lambda i: (i, 0))
    outs = pl.pallas_call(
        body, name=name, grid=(S // tm,),
        in_specs=[blk] * 6 + [pl.BlockSpec((tm, A_WIDTH), lambda i: (i, R_ZA // A_WIDTH)), blk],
        out_specs=[blk] * 7,
        out_shape=[jax.ShapeDtypeStruct((S, A_WIDTH), BF16)] * 3
        + [jax.ShapeDtypeStruct((S, A_WIDTH), F32)] * 3 + [jax.ShapeDtypeStruct((S, A_WIDTH), BF16)],
        compiler_params=_cp(("parallel",)))(*os_, *ls_, ur, dya)
    return outs[0:3], outs[3:6], outs[6]


def _logf(ur, bf_pad, *, name):
    S = ur.shape[0]
    tm = min(1024, S)

    def body(u_ref, b_ref, o_ref):
        z = u_ref[...] + b_ref[...]
        o_ref[...] = jnp.minimum(z, 0.0) - jnp.log(1.0 + jnp.exp(-jnp.abs(z)))

    return pl.pallas_call(
        body, name=name, grid=(S // tm,),
        in_specs=[pl.BlockSpec((tm, FB_PAD), lambda i: (i, R_FB // FB_PAD)),
                  pl.BlockSpec((1, FB_PAD), lambda i: (0, 0))],
        out_specs=pl.BlockSpec((tm, FB_PAD), lambda i: (i, 0)),
        out_shape=jax.ShapeDtypeStruct((S, FB_PAD), F32),
        compiler_params=_cp(("parallel",)))(ur, bf_pad)


def _cumsum_lanes(x, reverse, *, name):
    nt, H, _ = x.shape

    def body(x_ref, o_ref):
        lane = lax.broadcasted_iota(jnp.int32, (H, LANES), 1)

        def tile(t, carry):
            tt = nt - 1 - t if reverse else t
            v = x_ref[tt]
            k = 1
            while k < LANES:
                if reverse:
                    v = v + jnp.where(lane < LANES - k, pltpu.roll(v, LANES - k, 1), 0.0)
                else:
                    v = v + jnp.where(lane >= k, pltpu.roll(v, k, 1), 0.0)
                k *= 2
            v = v + carry
            o_ref[tt] = v
            edge = v[:, :1] if reverse else v[:, LANES - 1:]
            return jnp.broadcast_to(edge, (H, LANES))

        lax.fori_loop(0, nt, tile, jnp.zeros((H, LANES), F32))

    return pl.pallas_call(
        body, name=name, out_shape=jax.ShapeDtypeStruct((nt, H, LANES), F32),
        in_specs=[pl.BlockSpec(memory_space=pltpu.VMEM)], out_specs=pl.BlockSpec(memory_space=pltpu.VMEM),
        compiler_params=_cp())(x)


B_SCALE = B_HEAD ** -0.5


def _pair_masks():
    lane = lax.broadcasted_iota(jnp.int32, (1, LANES), 1)
    row = lax.broadcasted_iota(jnp.int32, (LANES, 1), 0)
    return (lane < B_HEAD, lane >= B_HEAD), (row < B_HEAD, row >= B_HEAD)


def _causal_t(T):
    r = lax.broadcasted_iota(jnp.int32, (T, T), 0)
    c = lax.broadcasted_iota(jnp.int32, (T, T), 1)
    return r <= c


def _zero_other(x, keep):
    return jnp.where(keep, x, jnp.zeros_like(x))


def _fox_aug(ub, ckb, *, name):
    S = ub.shape[0]
    T = min(512, S)

    def body(q_ref, k_ref, c_ref, qa_ref, ka_ref):
        lane = lax.broadcasted_iota(jnp.int32, (1, LANES), 1)
        q = q_ref[...] * B_SCALE
        k = k_ref[...]
        for a in range(2):
            own = (lane < B_HEAD) if a == 0 else (lane >= B_HEAD)
            o = B_HEAD if a == 0 else 0
            c = c_ref[a]
            hi = c.astype(BF16)
            r1 = c - hi.astype(F32)
            mid = r1.astype(BF16)
            lo = (r1 - mid.astype(F32)).astype(BF16)
            pieces = (hi, mid, lo)
            one = jnp.ones((T, LANES), BF16)
            qa = jnp.where(own, q, jnp.zeros_like(q))
            ka = jnp.where(own, k, jnp.zeros_like(k))
            for t in range(3):
                qa = jnp.where(lane == o + t, pieces[t], qa)
                qa = jnp.where(lane == o + 3 + t, one, qa)
                ka = jnp.where(lane == o + t, one, ka)
                ka = jnp.where(lane == o + 3 + t, -pieces[t], ka)
            qa_ref[a] = qa
            ka_ref[a] = ka

    out = pl.BlockSpec((2, T, LANES), lambda h, i: (h, i, 0))
    return pl.pallas_call(
        body, name=name, grid=(B_HEADS // 2, S // T),
        in_specs=[pl.BlockSpec((T, LANES), lambda h, i: (i, h)), pl.BlockSpec((T, LANES), lambda h, i: (i, 4 + h)), out],
        out_specs=[out, out], out_shape=[jax.ShapeDtypeStruct((B_HEADS, S, LANES), BF16)] * 2,
        compiler_params=_cp(("parallel", "parallel")))(ub, ub, ckb)


def _fox_fwd(qaug, kaug, vt, *, name):
    S = qaug.shape[1]
    T = min(512, S)
    nq = S // T

    def body(q_ref, k_ref, vt_ref, o_ref, l_ref, m_s, l_s, acc_s):
        i = pl.program_id(1)
        _, rows = _pair_masks()
        qm = [q_ref[0], q_ref[1]]
        m_s[...] = jnp.full((2, 1, T), NEG, F32)
        l_s[...] = jnp.zeros((2, 1, T), F32)
        acc_s[...] = jnp.zeros((LANES, T), F32)

        def step(j, masked):
            off = pl.multiple_of(j * T, T)
            vtj = vt_ref[j]
            upd = jnp.zeros((LANES, T), F32)
            alphas = []
            for a in range(2):
                st = _dot_nt(k_ref[a, pl.ds(off, T), :], qm[a])
                if masked:
                    st = jnp.where(_causal_t(T), st, NEG)
                m_old = m_s[a]
                m_new = jnp.maximum(m_old, jnp.max(st, axis=0, keepdims=True))
                alpha = jnp.exp(m_old - m_new)
                pt = jnp.exp(st - m_new)
                l_s[a] = alpha * l_s[a] + jnp.sum(pt, axis=0, keepdims=True)
                m_s[a] = m_new
                upd = upd + _dot(_zero_other(vtj, rows[a]), pt.astype(BF16))
                alphas.append(alpha)
            acc_s[...] = acc_s[...] * jnp.where(rows[0], alphas[0], alphas[1]) + upd

        def loop(j, carry):
            step(j, False)
            return carry

        lax.fori_loop(0, i, loop, 0)
        step(i, True)
        o_ref[...] = (acc_s[...] / jnp.where(rows[0], l_s[0], l_s[1])).T
        l_ref[0] = m_s[0] + jnp.log(l_s[0])
        l_ref[1] = m_s[1] + jnp.log(l_s[1])

    stat = pl.BlockSpec((2, None, 1, T), lambda h, i: (h, i, 0, 0))
    return pl.pallas_call(
        body, name=name, grid=(B_HEADS // 2, nq),
        in_specs=[pl.BlockSpec((2, T, LANES), lambda h, i: (h, i, 0)),
                  pl.BlockSpec((2, S, LANES), lambda h, i: (h, 0, 0)),
                  pl.BlockSpec((nq, LANES, T), lambda h, i: (0, h, 0))],
        out_specs=[pl.BlockSpec((T, LANES), lambda h, i: (i, h)), stat],
        out_shape=[jax.ShapeDtypeStruct((S, A_WIDTH), F32), jax.ShapeDtypeStruct((B_HEADS, nq, 1, T), F32)],
        scratch_shapes=[pltpu.VMEM((2, 1, T), F32), pltpu.VMEM((2, 1, T), F32), pltpu.VMEM((LANES, T), F32)],
        compiler_params=_cp(("parallel", "parallel")),
    )(qaug, kaug, vt)


def _fox_delta(o, do, *, name):
    S = o.shape[0]
    T = min(512, S)
    nq = S // T

    def body(o_ref, do_ref, d_ref):
        _, rows = _pair_masks()
        prod_t = (do_ref[...].astype(F32) * o_ref[...]).T
        d_ref[0] = jnp.sum(_zero_other(prod_t, rows[0]), axis=0, keepdims=True)
        d_ref[1] = jnp.sum(_zero_other(prod_t, rows[1]), axis=0, keepdims=True)

    tile = pl.BlockSpec((T, LANES), lambda h, i: (i, h))
    return pl.pallas_call(
        body, name=name, grid=(B_HEADS // 2, nq), in_specs=[tile, tile],
        out_specs=pl.BlockSpec((2, None, 1, T), lambda h, i: (h, i, 0, 0)),
        out_shape=jax.ShapeDtypeStruct((B_HEADS, nq, 1, T), F32),
        compiler_params=_cp(("parallel", "parallel")))(o, do)


def _fox_bwd(ub, qaug, kaug, kt, do, lse, delta, *, name):
    S = ub.shape[0]
    T = min(512, S)
    nq = S // T

    def body(k_ref, v_ref, kt_ref, q_ref, do_ref, l_ref, dl_ref,
             dk_ref, dv_ref, dck_ref, dqt_ref, dcq_ref, dk_s, dv_s, dc_s):
        j = pl.program_id(1)
        lanes, rows = _pair_masks()
        vv = v_ref[...]
        ktj = kt_ref[...]
        km = [k_ref[0], k_ref[1]]
        ktm = [_zero_other(ktj, rows[0]), _zero_other(ktj, rows[1])]
        dk_s[...] = jnp.zeros((2, T, LANES), F32)
        dv_s[...] = jnp.zeros((T, LANES), F32)
        dc_s[...] = jnp.zeros((2, T, 1), F32)

        @pl.when(j == 0)
        def _():
            dqt_ref[...] = jnp.zeros((nq, LANES, T), F32)
            dcq_ref[...] = jnp.zeros((2, nq, 1, T), F32)

        def step(i, masked):
            off = pl.multiple_of(i * T, T)
            doi = do_ref[pl.ds(off, T), :]
            upd = jnp.zeros((LANES, T), F32)
            for a in range(2):
                qi = q_ref[a, pl.ds(off, T), :]
                st = _dot_nt(km[a], qi)
                if masked:
                    st = jnp.where(_causal_t(T), st, NEG)
                pt = jnp.exp(st - l_ref[a, i])
                doa = _zero_other(doi, lanes[a])
                dv_s[...] += _dot(pt.astype(BF16), doa)
                dst = pt * (_dot_nt(vv, doa) - dl_ref[a, i])
                dsb = dst.astype(BF16)
                dk_s[a] += _dot(dsb, qi)
                upd = upd + _dot(ktm[a], dsb)
                dc_s[a] -= jnp.sum(dst, axis=-1, keepdims=True)
                dcq_ref[a, i] += jnp.sum(dst, axis=0, keepdims=True)
            dqt_ref[i] += upd

        def loop(i, carry):
            step(i, False)
            return carry

        step(j, True)
        lax.fori_loop(j + 1, nq, loop, 0)
        dk_ref[...] = jnp.where(lanes[0], dk_s[0], dk_s[1]).astype(BF16)
        dv_ref[...] = dv_s[...].astype(BF16)
        dck_ref[...] = dc_s[...]

    rowv = pl.BlockSpec((2, nq, 1, T), lambda h, j: (h, 0, 0, 0))
    tile = pl.BlockSpec((T, LANES), lambda h, j: (j, h))
    return pl.pallas_call(
        body, name=name, grid=(B_HEADS // 2, nq),
        in_specs=[pl.BlockSpec((2, T, LANES), lambda h, j: (h, j, 0)),
                  pl.BlockSpec((T, LANES), lambda h, j: (j, 8 + h)),
                  pl.BlockSpec((None, LANES, T), lambda h, j: (j, h, 0)),
                  pl.BlockSpec((2, S, LANES), lambda h, j: (h, 0, 0)),
                  pl.BlockSpec((S, LANES), lambda h, j: (0, h)),
                  rowv, rowv],
        out_specs=[tile, tile, pl.BlockSpec((2, T, 1), lambda h, j: (h, j, 0)),
                   pl.BlockSpec((nq, LANES, T), lambda h, j: (0, h, 0)), rowv],
        out_shape=[jax.ShapeDtypeStruct((S, A_WIDTH), BF16)] * 2 + [jax.ShapeDtypeStruct((B_HEADS, S, 1), F32),
                   jax.ShapeDtypeStruct((nq, A_WIDTH, T), F32), jax.ShapeDtypeStruct((B_HEADS, nq, 1, T), F32)],
        scratch_shapes=[pltpu.VMEM((2, T, LANES), F32), pltpu.VMEM((T, LANES), F32), pltpu.VMEM((2, T, 1), F32)],
        compiler_params=_cp(("parallel", "arbitrary")),
    )(kaug, ub, kt, qaug, do, lse, delta)


def _gate_fwd(o, ur, zcol, *, name):
    S = ur.shape[0]
    tm = min(1024, S)

    def body(o_ref, z_ref, y_ref):
        y_ref[...] = (o_ref[...] * _silu_parts(z_ref[...])[0]).astype(BF16)

    blk = pl.BlockSpec((tm, A_WIDTH), lambda i: (i, 0))
    return pl.pallas_call(
        body, name=name, grid=(S // tm,),
        in_specs=[blk, pl.BlockSpec((tm, A_WIDTH), lambda i: (i, zcol // A_WIDTH))],
        out_specs=blk, out_shape=jax.ShapeDtypeStruct((S, A_WIDTH), BF16),
        compiler_params=_cp(("parallel",)))(o, ur)


def _gate_bwd(o, ur, zcol, dy, *, name):
    S = ur.shape[0]
    tm = min(1024, S)

    def body(o_ref, z_ref, dy_ref, do_ref, dz_ref):
        sz, dsz = _silu_parts(z_ref[...])
        dyv = dy_ref[...]
        do_ref[...] = (dyv * sz).astype(BF16)
        dz_ref[...] = (dyv * o_ref[...] * dsz).astype(BF16)

    blk = pl.BlockSpec((tm, A_WIDTH), lambda i: (i, 0))
    return pl.pallas_call(
        body, name=name, grid=(S // tm,),
        in_specs=[blk, pl.BlockSpec((tm, A_WIDTH), lambda i: (i, zcol // A_WIDTH)), blk],
        out_specs=[blk, blk], out_shape=[jax.ShapeDtypeStruct((S, A_WIDTH), BF16)] * 2,
        compiler_params=_cp(("parallel",)))(o, ur, dy)


def _dfb(ur, bf_pad, dlogf_pad, *, name):
    S = ur.shape[0]
    tm = min(1024, S)

    def body(u_ref, b_ref, d_ref, o_ref, s_ref):
        i = pl.program_id(0)
        dv = d_ref[...] * _sigmoid(-(u_ref[...] + b_ref[...]))
        o_ref[...] = dv.astype(BF16)
        part = jnp.sum(dv, axis=0, keepdims=True)

        @pl.when(i == 0)
        def _():
            s_ref[...] = part

        @pl.when(i > 0)
        def _():
            s_ref[...] += part

    vec = pl.BlockSpec((1, FB_PAD), lambda i: (0, 0))
    blk = pl.BlockSpec((tm, FB_PAD), lambda i: (i, 0))
    return pl.pallas_call(
        body, name=name, grid=(S // tm,),
        in_specs=[pl.BlockSpec((tm, FB_PAD), lambda i: (i, R_FB // FB_PAD)), vec, blk],
        out_specs=[blk, vec],
        out_shape=[jax.ShapeDtypeStruct((S, FB_PAD), BF16), jax.ShapeDtypeStruct((1, FB_PAD), F32)],
        compiler_params=_cp(("arbitrary",)))(ur, bf_pad, dlogf_pad)


M_SCALE = HEAD ** -0.5


def _mem_fwd(ur, mkv, *, name):
    S = ur.shape[0]
    T = min(512, S)

    def body(q_ref, z_ref, k_ref, v_ref, y_ref):
        s = _dot_nt(q_ref[...].astype(BF16), k_ref[...].astype(BF16)) * M_SCALE
        p = jnp.exp(s - jnp.max(s, axis=-1, keepdims=True))
        p = p / jnp.sum(p, axis=-1, keepdims=True)
        o = _dot(p.astype(BF16), v_ref[...].astype(BF16))
        y_ref[...] = (o * _silu_parts(z_ref[...])[0]).astype(BF16)

    return pl.pallas_call(
        body, name=name, grid=(S // T, M_HEADS),
        in_specs=[pl.BlockSpec((T, HEAD), lambda i, h: (i, R_QM // HEAD + h)),
                  pl.BlockSpec((T, HEAD), lambda i, h: (i, R_ZM // HEAD + h)),
                  pl.BlockSpec((N_MEM, HEAD), lambda i, h: (0, h)),
                  pl.BlockSpec((N_MEM, HEAD), lambda i, h: (0, M_HEADS + h))],
        out_specs=pl.BlockSpec((T, HEAD), lambda i, h: (i, h)),
        out_shape=jax.ShapeDtypeStruct((S, A_WIDTH), BF16),
        compiler_params=_cp(("parallel", "parallel")))(ur, ur, mkv, mkv)


def _mem_bwd(ur, mkv, dy, *, name):
    S = ur.shape[0]
    T = min(512, S)

    def body(q_ref, z_ref, k_ref, v_ref, dy_ref, dq_ref, dz_ref, dk_ref, dv_ref):
        i = pl.program_id(1)
        qv = q_ref[...].astype(BF16)
        kv = k_ref[...].astype(BF16)
        vv = v_ref[...].astype(BF16)
        s = _dot_nt(qv, kv) * M_SCALE
        p = jnp.exp(s - jnp.max(s, axis=-1, keepdims=True))
        p = p / jnp.sum(p, axis=-1, keepdims=True)
        o = _dot(p.astype(BF16), vv)
        sz, dsz = _silu_parts(z_ref[...])
        dyv = dy_ref[...]
        dz_ref[...] = (dyv * o * dsz).astype(BF16)
        dov = (dyv * sz).astype(BF16)
        dp = _dot_nt(dov, vv)
        ds = p * (dp - jnp.sum(p * dp, axis=-1, keepdims=True))
        dq_ref[...] = (_dot(ds.astype(BF16), kv) * M_SCALE).astype(BF16)
        dvp = _dot(p.T.astype(BF16), dov)
        dkp = _dot(ds.T.astype(BF16), qv) * M_SCALE

        @pl.when(i == 0)
        def _():
            dk_ref[...] = dkp
            dv_ref[...] = dvp

        @pl.when(i > 0)
        def _():
            dk_ref[...] += dkp
            dv_ref[...] += dvp

    tile = pl.BlockSpec((T, HEAD), lambda h, i: (i, h))
    acc = pl.BlockSpec((N_MEM, HEAD), lambda h, i: (0, h))
    return pl.pallas_call(
        body, name=name, grid=(M_HEADS, S // T),
        in_specs=[pl.BlockSpec((T, HEAD), lambda h, i: (i, R_QM // HEAD + h)),
                  pl.BlockSpec((T, HEAD), lambda h, i: (i, R_ZM // HEAD + h)),
                  pl.BlockSpec((N_MEM, HEAD), lambda h, i: (0, h)),
                  pl.BlockSpec((N_MEM, HEAD), lambda h, i: (0, M_HEADS + h)), tile],
        out_specs=[tile, tile, acc, acc],
        out_shape=[jax.ShapeDtypeStruct((S, A_WIDTH), BF16)] * 2
        + [jax.ShapeDtypeStruct((N_MEM, A_WIDTH), F32)] * 2,
        compiler_params=_cp(("parallel", "arbitrary")))(ur, ur, mkv, mkv, dy)


def _branch_fwd(ys, wbs, ur, b_merge, *, name):
    S = ur.shape[0]
    tm, tn = min(512, S), 512
    nj = D_MODEL // tn

    def body(ya, yb, ym, wa, wb, wm, g0, g1, g2, b0, b1, b2, mg_ref, p_ref):
        acc = jnp.zeros((tm, tn), F32)
        for i, (y, w, gr, br) in enumerate(((ya, wa, g0, b0), (yb, wb, g1, b1), (ym, wm, g2, b2))):
            pr = _dot(y[...], w[...])
            p_ref[i] = pr
            acc = acc + _sigmoid(gr[...] + br[...]) * pr
        mg_ref[...] = acc.astype(BF16)

    yspec = pl.BlockSpec((tm, A_WIDTH), lambda i, j: (i, 0))
    wspec = pl.BlockSpec((A_WIDTH, tn), lambda i, j: (0, j))
    gspec = lambda b: pl.BlockSpec((tm, tn), lambda i, j: (i, (R_GL + b * D_MODEL) // tn + j))
    bspec = lambda b: pl.BlockSpec((1, tn), lambda i, j: (0, b * nj + j))
    return pl.pallas_call(
        body, name=name, grid=(S // tm, nj),
        in_specs=[yspec] * 3 + [wspec] * 3 + [gspec(0), gspec(1), gspec(2), bspec(0), bspec(1), bspec(2)],
        out_specs=[pl.BlockSpec((tm, tn), lambda i, j: (i, j)),
                   pl.BlockSpec((3, tm, tn), lambda i, j: (0, i, j))],
        out_shape=[jax.ShapeDtypeStruct((S, D_MODEL), BF16), jax.ShapeDtypeStruct((3, S, D_MODEL), F32)],
        compiler_params=_cp(("parallel", "parallel")))(*ys, *wbs, ur, ur, ur, b_merge, b_merge, b_merge)


def _branch_bwd(dm, prods, ur, b_merge, *, name):
    S = ur.shape[0]
    tm = min(256, S)

    def body(dm_ref, p_ref, g0, g1, g2, b_ref, dp_ref, dgl_ref, db_ref):
        i = pl.program_id(0)
        dmv = dm_ref[...]
        parts = []
        for b, gr in enumerate((g0, g1, g2)):
            sl = slice(b * D_MODEL, (b + 1) * D_MODEL)
            gt = _sigmoid(gr[...] + b_ref[:, sl])
            dp_ref[b] = (dmv * gt).astype(BF16)
            dgl = dmv * p_ref[b] * gt * (1.0 - gt)
            dgl_ref[:, sl] = dgl.astype(BF16)
            parts.append(jnp.sum(dgl, axis=0, keepdims=True))
        part = jnp.concatenate(parts, axis=1)

        @pl.when(i == 0)
        def _():
            db_ref[...] = part

        @pl.when(i > 0)
        def _():
            db_ref[...] += part

    gspec = lambda b: pl.BlockSpec((tm, D_MODEL), lambda i: (i, R_GL // D_MODEL + b))
    vec = pl.BlockSpec((1, 3 * D_MODEL), lambda i: (0, 0))
    return pl.pallas_call(
        body, name=name, grid=(S // tm,),
        in_specs=[pl.BlockSpec((tm, D_MODEL), lambda i: (i, 0)),
                  pl.BlockSpec((3, tm, D_MODEL), lambda i: (0, i, 0)), gspec(0), gspec(1), gspec(2), vec],
        out_specs=[pl.BlockSpec((3, tm, D_MODEL), lambda i: (0, i, 0)),
                   pl.BlockSpec((tm, 3 * D_MODEL), lambda i: (i, 0)), vec],
        out_shape=[jax.ShapeDtypeStruct((3, S, D_MODEL), BF16), jax.ShapeDtypeStruct((S, 3 * D_MODEL), BF16),
                   jax.ShapeDtypeStruct((1, 3 * D_MODEL), F32)],
        compiler_params=_cp(("arbitrary",)))(dm, prods, ur, ur, ur, b_merge)


def _rope_tables(pos):
    half = ROT // 2
    inv = ROPE_THETA ** (-jnp.arange(half, dtype=F32) / half)
    ang = pos.astype(F32)[:, None] * inv
    cos, sin = jnp.cos(ang), jnp.sin(ang)
    S = pos.shape[0]
    one = jnp.ones((S, LANES - ROT), F32)
    zero = jnp.zeros((S, LANES - ROT), F32)
    zh = jnp.zeros((S, half), F32)
    c = jnp.concatenate([cos, cos, one], axis=1)
    s1 = jnp.concatenate([-sin, zh, zero], axis=1)
    s2 = jnp.concatenate([zh, sin, zero], axis=1)
    return c, s1, s2


def _to_tiles(t):
    S, H = t.shape
    return t.reshape(S // LANES, LANES, H).transpose(0, 2, 1)


def _from_tiles(t):
    nt, H, _ = t.shape
    return t.transpose(1, 0, 2).reshape(H, nt * LANES)


def _local_step(x, mem, pos, tgt, g_pre, g_post, g_mem, wt, bf_pad, b_merge, w_kv, wbs, w_out):
    S = x.shape[0]
    T = min(512, S)
    nq = S // T
    tabs = _rope_tables(pos)

    h = _rms_fwd(x, g_pre, name="rms_pre")
    hs = [_to_classes(h, d) for d in DIL]
    tabs_g = [[_to_classes(t, d) for t in tabs] for d in DIL]
    uas = [_mm(hs[g], wt[f"A{g}"], bt=True, name=f"proj_a{g}", tn=1536) for g in range(3)]
    ub = _mm(h, wt["B"], bt=True, out_dtype=BF16, name="proj_b", tn=1536)
    ur = _mm(h, wt["R"], bt=True, name="proj_r", tn=1792)

    qkvs = [_rope_cast(uas[g], tabs_g[g], name=f"rope_a{g}") for g in range(3)]
    outs_c, lses_c = [], []
    for g in range(3):
        o, l = _attn_a_fwd(qkvs[g], g, name=f"attn_a_fwd{g}")
        outs_c.append(o)
        lses_c.append(l)
    outs_a = [_from_classes(o, d) for o, d in zip(outs_c, DIL)]
    lses_a = [_from_classes(l, d) for l, d in zip(lses_c, DIL)]
    ya = _merge_a_fwd(outs_a, lses_a, ur, name="merge_a_fwd")

    logf = _logf(ur, bf_pad, name="logf")
    c = _from_tiles(_cumsum_lanes(_to_tiles(logf[:, :B_HEADS]), False, name="cumsum_fwd"))
    ckb = jnp.broadcast_to(c[:, :, None], (B_HEADS, S, LANES))
    qaug, kaug = _fox_aug(ub, ckb, name="fox_aug")
    kt = ub[:, 512:1024].reshape(nq, T, 512).transpose(0, 2, 1)
    vt = ub[:, 1024:1536].reshape(nq, T, 512).transpose(0, 2, 1)
    ob, lse_b = _fox_fwd(qaug, kaug, vt, name="fox_fwd")
    yb = _gate_fwd(ob, ur, R_ZB, name="gate_b_fwd")

    hm = _rms_fwd(mem, g_mem, name="rms_mem")
    mkv = _mm(hm, w_kv, name="proj_mem")
    ym = _mem_fwd(ur, mkv, name="mem_fwd")

    merged, prods = _branch_fwd((ya, yb, ym), wbs, ur, b_merge, name="branch_fwd")
    out = _mm(merged, w_out, name="proj_out")
    dy, d_out, dg_post, loss_row = _post(x, out, tgt, g_post, name="post")

    dmerged = _mm(d_out, w_out, bt=True, name="d_merged")
    dw_out = _mm(merged, d_out, at=True, name="dw_out", tk=2048)
    dprods, dgl, db_merge = _branch_bwd(dmerged, prods, ur, b_merge, name="branch_bwd")
    dys, dwbs = [], []
    for i, (y, wb) in enumerate(zip((ya, yb, ym), wbs)):
        dys.append(_mm(dprods[i], wb, bt=True, name=f"d_y{i}"))
        dwbs.append(_mm(y, dprods[i], at=True, name=f"dw_branch{i}", tk=2048))

    dos_a, adjs_a, dza = _merge_a_bwd(outs_a, lses_a, ur, dys[0], name="merge_a_bwd")
    dus_a = []
    for g, d in enumerate(DIL):
        do_c, adj_c = _to_classes(dos_a[g], d), _to_classes(adjs_a[g], d)
        dq = _attn_a_dq(qkvs[g], tabs_g[g], g, do_c, lses_c[g], adj_c, name=f"attn_a_dq{g}")
        dk, dv = _attn_a_dkv(qkvs[g], tabs_g[g], g, do_c, lses_c[g], adj_c, name=f"attn_a_dkv{g}")
        dus_a.append(jnp.concatenate([dq, dk, dv], axis=1))

    dob, dzb = _gate_bwd(ob, ur, R_ZB, dys[1], name="gate_b_bwd")
    delta_b = _fox_delta(ob, dob, name="fox_delta")
    dkb, dvb, dc_k, dqt, dc_q = _fox_bwd(ub, qaug, kaug, kt, dob, lse_b, delta_b, name="fox_bwd")
    dqb = (dqt.transpose(0, 2, 1).reshape(S, A_WIDTH) * B_SCALE).astype(BF16)
    du_b = jnp.concatenate([dqb, dkb, dvb], axis=1)
    dc = dc_q.reshape(B_HEADS, S) + dc_k.reshape(B_HEADS, S)
    dlogf = _from_tiles(_cumsum_lanes(_to_tiles(dc.T), True, name="cumsum_bwd"))
    dlogf_pad = jnp.pad(dlogf.T, ((0, 0), (0, FB_PAD - B_HEADS)))
    dfb, db_forget = _dfb(ur, bf_pad, dlogf_pad, name="dfb")

    dqm, dzm, dmk, dmv = _mem_bwd(ur, mkv, dys[2], name="mem_bwd")
    dmkv = jnp.concatenate([dmk, dmv], axis=1).astype(BF16)
    dhm = _mm(dmkv, w_kv, bt=True, name="d_hm")
    dw_kv = _mm(hm, dmkv, at=True, name="dw_kv")
    dg_mem = _rms_bwd(mem, g_mem, dhm, None, name="rms_mem_bwd")

    du_r = jnp.concatenate([dza, dzb, dqm, dzm, dgl, dfb], axis=1)
    dh = _mm(du_r, wt["R"], name="d_h_r", tk=1792) + _mm(du_b, wt["B"], name="d_h_b", tk=1536)
    for g, d in enumerate(DIL):
        dh = dh + _from_classes(_mm(dus_a[g], wt[f"A{g}"], name=f"d_h_a{g}", tk=1536), d)
    dwt = {"R": _mm(du_r, h, at=True, name="dw_in_r", tm=1792, tk=1024),
           "B": _mm(du_b, h, at=True, name="dw_in_b", tm=1536, tk=2048)}
    for g in range(3):
        dwt[f"A{g}"] = _mm(dus_a[g], hs[g], at=True, name=f"dw_in_a{g}", tm=1536, tk=2048)
    grad_x, dg_pre = _rms_bwd(x, g_pre, dh, dy, name="rms_pre_bwd")

    return dict(loss=loss_row, grad_x=grad_x, dwt=dwt, dw_kv=dw_kv, dwbs=dwbs, dw_out=dw_out,
                dg_pre=dg_pre, dg_post=dg_post, dg_mem=dg_mem, db_forget=db_forget, db_merge=db_merge)


MESH = pl.DeviceIdType.MESH
ANY = pl.BlockSpec(memory_space=pl.ANY)


def _relations():
    return [(k >> 2 & 1, k >> 1 & 1, k & 1) for k in range(1, N_DEV)]


def _coords():
    return lax.axis_index("x"), lax.axis_index("y"), lax.axis_index("c")


def _all_gather(shard, *, name):
    R, W = shard.shape

    def body(x_ref, out_ref, send_sems, recv_sems, local_sem):
        x, y, c = _coords()
        me, sibling = (x, y, c), (x, y, 1 - c)
        chips = [(1 - x, y), (x, 1 - y), (1 - x, 1 - y)]

        def slot(px, py, pc):
            return out_ref.at[4 * px + 2 * py + pc]

        def copy(k, block, to, src=None):
            return pltpu.make_async_remote_copy(
                src_ref=slot(*block) if src is None else src, dst_ref=slot(*block),
                send_sem=send_sems.at[k], recv_sem=recv_sems.at[k], device_id=to, device_id_type=MESH)

        mine = pltpu.make_async_copy(x_ref, slot(*me), local_sem)
        mine.start()
        first = [copy(0, me, sibling, src=x_ref)]
        first += [copy(1 + j, me, (*chip, c), src=x_ref) for j, chip in enumerate(chips)]
        for cp in first:
            cp.start()
        passed = [copy(4 + j, (*chip, c), sibling) for j, chip in enumerate(chips)]
        for j, chip in enumerate(chips):
            copy(1 + j, (*chip, c), me).wait_recv()
            passed[j].start()
        copy(0, sibling, me).wait_recv()
        for j, chip in enumerate(chips):
            copy(4 + j, (*chip, 1 - c), me).wait_recv()
        for cp in first + passed:
            cp.wait_send()
        mine.wait()

    return pl.pallas_call(
        body, name=name, out_shape=jax.ShapeDtypeStruct((N_DEV, R, W), shard.dtype),
        in_specs=[ANY], out_specs=ANY,
        scratch_shapes=[pltpu.SemaphoreType.DMA((N_DEV - 1,)), pltpu.SemaphoreType.DMA((N_DEV - 1,)),
                        pltpu.SemaphoreType.DMA],
    )(shard)


N_CHIP = 4


def _exchange_pair(gbig, *, name):
    _, R, W = gbig.shape

    def body(g_ref, sib_ref, send_sems, recv_sems):
        x, y, c = _coords()
        copies = []
        for r in range(N_CHIP):
            px, py = x ^ (r >> 1), y ^ (r & 1)
            copies.append(pltpu.make_async_remote_copy(
                src_ref=g_ref.at[4 * px + 2 * py + (1 - c)], dst_ref=sib_ref.at[r],
                send_sem=send_sems.at[r], recv_sem=recv_sems.at[r], device_id=(x, y, 1 - c), device_id_type=MESH))
        for cp in copies:
            cp.start()
        for cp in copies:
            cp.wait_recv()
        for cp in copies:
            cp.wait_send()

    return pl.pallas_call(
        body, name=name, out_shape=jax.ShapeDtypeStruct((N_CHIP, R, W), gbig.dtype),
        in_specs=[ANY], out_specs=ANY,
        scratch_shapes=[pltpu.SemaphoreType.DMA((N_CHIP,)), pltpu.SemaphoreType.DMA((N_CHIP,))],
    )(gbig)


def _own_slabs():
    x, y, c = _coords()
    return jnp.stack([4 * (x ^ (r >> 1)) + 2 * (y ^ (r & 1)) + c for r in range(N_CHIP)]).astype(jnp.int32)


def _pair_sum(gbig, sib, own_idx, tr, *, name):
    _, R, W = gbig.shape

    def body(idx_ref, a_ref, b_ref, o_ref):
        o_ref[...] = (a_ref[...] + b_ref[...]).astype(BF16)

    return pl.pallas_call(
        body, name=name,
        grid_spec=pltpu.PrefetchScalarGridSpec(
            num_scalar_prefetch=1, grid=(N_CHIP - 1, R // tr),
            in_specs=[pl.BlockSpec((None, tr, W), lambda r, i, idx: (idx[r + 1], i, 0)),
                      pl.BlockSpec((None, tr, W), lambda r, i, idx: (r + 1, i, 0))],
            out_specs=pl.BlockSpec((None, tr, W), lambda r, i, idx: (r, i, 0))),
        out_shape=jax.ShapeDtypeStruct((N_CHIP - 1, R, W), BF16),
        compiler_params=_cp(("parallel", "parallel")))(own_idx, gbig, sib)


def _exchange_chips(send, gsmall, *, name):
    nb, R, W = send.shape
    n = N_DEV - 1

    def body(b_ref, s_ref, rb_ref, rs_ref, send_sems, recv_sems, local_sem):
        x, y, c = _coords()
        me = 4 * x + 2 * y + c
        mine = pltpu.make_async_copy(s_ref, rs_ref.at[me], local_sem)
        mine.start()
        started = []
        for k, (fx, fy, fc) in enumerate(_relations()):
            cp = pltpu.make_async_remote_copy(
                src_ref=s_ref, dst_ref=rs_ref.at[me], send_sem=send_sems.at[k], recv_sem=recv_sems.at[k],
                device_id=(x ^ fx, y ^ fy, c ^ fc), device_id_type=MESH)
            cp.start()
            started.append(cp)
        for r in range(1, N_CHIP):
            cp = pltpu.make_async_remote_copy(
                src_ref=b_ref.at[r - 1], dst_ref=rb_ref.at[r - 1], send_sem=send_sems.at[n + r - 1],
                recv_sem=recv_sems.at[n + r - 1], device_id=(x ^ (r >> 1), y ^ (r & 1), c), device_id_type=MESH)
            cp.start()
            started.append(cp)
        for k, (fx, fy, fc) in enumerate(_relations()):
            peer = 4 * (x ^ fx) + 2 * (y ^ fy) + (c ^ fc)
            pltpu.make_async_remote_copy(
                src_ref=s_ref, dst_ref=rs_ref.at[peer], send_sem=send_sems.at[k], recv_sem=recv_sems.at[k],
                device_id=(x ^ fx, y ^ fy, c ^ fc), device_id_type=MESH).wait_recv()
        for r in range(1, N_CHIP):
            pltpu.make_async_remote_copy(
                src_ref=b_ref.at[r - 1], dst_ref=rb_ref.at[r - 1], send_sem=send_sems.at[n + r - 1],
                recv_sem=recv_sems.at[n + r - 1], device_id=(x ^ (r >> 1), y ^ (r & 1), c),
                device_id_type=MESH).wait_recv()
        for cp in started:
            cp.wait_send()
        mine.wait()

    return pl.pallas_call(
        body, name=name,
        out_shape=[jax.ShapeDtypeStruct((nb, R, W), send.dtype),
                   jax.ShapeDtypeStruct((N_DEV, 1, P_SMALL), gsmall.dtype)],
        in_specs=[ANY, ANY], out_specs=[ANY, ANY],
        scratch_shapes=[pltpu.SemaphoreType.DMA((n + nb,)), pltpu.SemaphoreType.DMA((n + nb,)),
                        pltpu.SemaphoreType.DMA],
    )(send, gsmall)


def _part_specs(parts, tr, row0):
    assert row0 % tr == 0
    specs = []
    for a, n_used in parts:
        if n_used is None:
            specs.append(pl.BlockSpec((1, tr, a.shape[2]), lambda i, idx: (idx[0], row0 // tr + i, 0)))
        else:
            specs.append(pl.BlockSpec((n_used, tr, a.shape[2]), lambda i, idx: (0, row0 // tr + i, 0)))
    return specs


def _part_total(refs, parts):
    g = None
    for ref, (_, n_used) in zip(refs, parts):
        for k in range(n_used or 1):
            t = ref[k].astype(F32)
            g = t if g is None else g + t
    return g


def _sum_parts(parts, idx, row0, nrows, tr, *, name):
    W = parts[0][0].shape[2]
    assert nrows % tr == 0

    def body(idx_ref, *refs):
        refs[-1][...] = _part_total(refs[:-1], parts)

    return pl.pallas_call(
        body, name=name,
        grid_spec=pltpu.PrefetchScalarGridSpec(
            num_scalar_prefetch=1, grid=(nrows // tr,), in_specs=_part_specs(parts, tr, row0),
            out_specs=pl.BlockSpec((tr, W), lambda i, idx: (i, 0))),
        out_shape=jax.ShapeDtypeStruct((nrows, W), F32),
        compiler_params=_cp(("parallel",)))(idx, *[a for a, _ in parts])


def _adamw(parts, idx, w, m, v, tr, *, name):
    R, W = w.shape
    assert R % tr == 0
    np_ = len(parts)

    def body(idx_ref, *refs):
        w_ref, m_ref, v_ref, g_ref, d_ref, nm_ref, nv_ref = refs[np_:]
        g = _part_total(refs[:np_], parts)
        mm = ADAM_B1 * m_ref[...] + (1.0 - ADAM_B1) * g
        vv = ADAM_B2 * v_ref[...] + (1.0 - ADAM_B2) * (g * g)
        m_hat = mm / (1.0 - ADAM_B1 ** ADAM_STEP)
        v_hat = vv / (1.0 - ADAM_B2 ** ADAM_STEP)
        g_ref[...] = g
        d_ref[...] = -ADAM_LR * (m_hat / (jnp.sqrt(v_hat) + ADAM_EPS) + ADAM_WD * w_ref[...])
        nm_ref[...] = mm
        nv_ref[...] = vv

    blk = pl.BlockSpec((tr, W), lambda i, idx: (i, 0))
    return pl.pallas_call(
        body, name=name,
        grid_spec=pltpu.PrefetchScalarGridSpec(
            num_scalar_prefetch=1, grid=(R // tr,), in_specs=_part_specs(parts, tr, 0) + [blk, blk, blk],
            out_specs=[blk] * 4),
        out_shape=[jax.ShapeDtypeStruct((R, W), F32)] * 4,
        compiler_params=_cp(("parallel",)))(idx, *[a for a, _ in parts], w, m, v)


def _pack_rest(w_kv, wa, wb, wm, w_out):
    return jnp.concatenate([w_kv[0], w_out[0]] + [t[0].reshape(-1, D_MODEL) for t in (wa, wb, wm)], axis=0)


def _unpack_rest(t):
    br = lambda i: t[RO_BR + 64 * i:RO_BR + 64 * (i + 1)].reshape(1, A_WIDTH, D_MODEL // N_DEV)
    return t[None, RO_KV:RO_OUT], br(0), br(1), br(2), t[None, RO_OUT:RO_BR]


def _orig_rows(gathered, a, b):
    res = []
    while a < b:
        dev, r = divmod(a, CS)
        n = min(b - a, CS - r)
        res.append(gathered[dev, RO_IN + r:RO_IN + r + n])
        a += n
    return res


def _full_weights(gathered):
    wt = {}
    for name, ranges in SEGS.items():
        rows = [p for a, b in ranges for p in _orig_rows(gathered, a, b)]
        if SEG_PAD[name]:
            rows.append(jnp.zeros((SEG_PAD[name], D_MODEL), gathered.dtype))
        wt[name] = jnp.concatenate(rows, axis=0)
    w_kv = gathered[:, RO_KV:RO_OUT].reshape(D_MODEL, D_MODEL)
    w_out = gathered[:, RO_OUT:RO_BR].reshape(D_MODEL, D_MODEL)
    wbs = [gathered[:, RO_BR + 64 * i:RO_BR + 64 * (i + 1)].reshape(N_DEV, A_WIDTH, D_MODEL // N_DEV)
           .transpose(1, 0, 2).reshape(A_WIDTH, D_MODEL) for i in range(3)]
    return wt, w_kv, wbs, w_out


def _orig_order(dwt):
    pieces = []
    for name, ranges in SEGS.items():
        o = 0
        for a, b in ranges:
            pieces.append((a, dwt[name][o:o + b - a]))
            o += b - a
    pieces.sort(key=lambda p: p[0])
    return jnp.concatenate([p[1] for p in pieces], axis=0)


def _pack_grads(dwt, dw_kv, dwbs, dw_out):
    g_in = jnp.pad(_orig_order(dwt).reshape(N_DEV, CS, D_MODEL), ((0, 0), (0, IN_ROWS - CS), (0, 0)))
    br = [t.reshape(A_WIDTH, N_DEV, D_MODEL // N_DEV).transpose(1, 0, 2).reshape(N_DEV, -1, D_MODEL) for t in dwbs]
    return jnp.concatenate([dw_kv.reshape(N_DEV, -1, D_MODEL), dw_out.reshape(N_DEV, -1, D_MODEL)] + br + [g_in],
                           axis=1)


def kernel(x, mem, positions, norm_pre_g, norm_post_g, norm_mem_g, w_in, b_forget, b_merge, w_mem_kv, w_branch_a, w_branch_b, w_branch_m, w_out, loss_target, m_norm_pre_g, m_norm_post_g, m_norm_mem_g, m_w_in, m_b_forget, m_b_merge, m_w_mem_kv, m_w_branch_a, m_w_branch_b, m_w_branch_m, m_w_out, v_norm_pre_g, v_norm_post_g, v_norm_mem_g, v_w_in, v_b_forget, v_b_merge, v_w_mem_kv, v_w_branch_a, v_w_branch_b, v_w_branch_m, v_w_out):
    w_rest = _pack_rest(w_mem_kv, w_branch_a, w_branch_b, w_branch_m, w_out)
    shard = jnp.concatenate([w_rest.astype(BF16), w_in[0].T.astype(BF16),
                             jnp.zeros((IN_ROWS - CS, D_MODEL), BF16)], axis=0)
    gathered = _all_gather(shard, name="gather_weights")
    wt, w_kv, wbs, w_o = _full_weights(gathered)

    bf_pad = jnp.pad(b_forget, ((0, 0), (0, FB_PAD - B_HEADS)))
    r = _local_step(x[0], mem[0], positions[0], loss_target[0], norm_pre_g, norm_post_g, norm_mem_g,
                    wt, bf_pad, b_merge, w_kv, wbs, w_o)

    gbig = _pack_grads(r["dwt"], r["dw_kv"], r["dwbs"], r["dw_out"])
    gsmall = jnp.concatenate([r["dg_pre"], r["dg_post"], r["dg_mem"], r["db_merge"],
                              r["db_forget"][:, :LANES], r["loss"]], axis=1)
    own_idx = _own_slabs()
    sib = _exchange_pair(gbig, name="exchange_pair")
    send = _pair_sum(gbig, sib, own_idx, 208, name="pair_sum")
    recv, rsmall = _exchange_chips(send, gsmall, name="exchange_chips")
    parts = [(gbig, None), (sib, 1), (recv, N_CHIP - 1)]

    m_rest = _pack_rest(m_w_mem_kv, m_w_branch_a, m_w_branch_b, m_w_branch_m, m_w_out)
    v_rest = _pack_rest(v_w_mem_kv, v_w_branch_a, v_w_branch_b, v_w_branch_m, v_w_out)
    outs_rest = [_unpack_rest(t) for t in _adamw(parts, own_idx, w_rest, m_rest, v_rest, 64, name="adamw_rest")]
    g_in = _sum_parts(parts, own_idx, RO_IN, IN_ROWS, 16, name="sum_w_in")[:CS].T
    outs_in = _adamw([(g_in[None], 1)], own_idx, w_in[0], m_w_in[0], v_w_in[0], 128, name="adamw_w_in")

    def small_vec(a, b, c, d, e):
        z = jnp.zeros((1, LANES - B_HEADS), F32)
        return jnp.concatenate([a, b, c, d, e, z, jnp.zeros((1, LANES), F32)], axis=1)

    outs_small = _adamw([(rsmall, N_DEV)], own_idx, small_vec(norm_pre_g, norm_post_g, norm_mem_g, b_merge, b_forget),
                        small_vec(m_norm_pre_g, m_norm_post_g, m_norm_mem_g, m_b_merge, m_b_forget),
                        small_vec(v_norm_pre_g, v_norm_post_g, v_norm_mem_g, v_b_merge, v_b_forget),
                        1, name="adamw_small")

    def small_parts(t):
        return [t[:, O_GPRE:O_GPRE + D_MODEL], t[:, O_GPOST:O_GPOST + D_MODEL], t[:, O_GMEM:O_GMEM + D_MODEL],
                t[:, O_BF:O_BF + B_HEADS], t[:, O_BM:O_BM + 3 * D_MODEL]]

    loss = outs_small[0][0, O_LOSS]
    result = [loss, r["grad_x"][None]]
    for rest, w_i, small in zip(outs_rest, outs_in, outs_small):
        gp, gq, gm, bf, bm = small_parts(small)
        w_k, w_a, w_b, w_m, w_ot = rest
        result += [gp, gq, gm, w_i[None], bf, bm, w_k, w_a, w_b, w_m, w_ot]
    return tuple(result)
```

```python
import jax
import jax.numpy as jnp
from jax import lax
from jax.experimental import pallas as pl
from jax.experimental.pallas import tpu as pltpu

F32 = jnp.float32
BF16 = jnp.bfloat16

N_DEV = 8
D_MODEL = 1024
N_MEM = 256
EPS = 1e-6
NEG = -1e30
ROPE_THETA = 500000.0
DIL = (1, 4, 16)
A_HEADS = 4
HEAD = 128
A_WIDTH = 512
B_HEADS = 8
B_HEAD = 64
M_HEADS = 4
ROT = 32
IN_COLS = 11272
FB_PAD = 256

SEGS = {
    "A0": ((0, 512), (1536, 2048), (3072, 3584)),
    "A1": ((512, 1024), (2048, 2560), (3584, 4096)),
    "A2": ((1024, 1536), (2560, 3072), (4096, 4608)),
    "B": ((5120, 6656),),
    "R": ((4608, 5120), (6664, 7176), (7176, 7688), (7688, 8200), (8200, 11272), (6656, 6664)),
}
SEG_PAD = {"A0": 0, "A1": 0, "A2": 0, "B": 0, "R": FB_PAD - B_HEADS}
R_ZA, R_ZB, R_QM, R_ZM, R_GL, R_FB = 0, 512, 1024, 1536, 2048, 5120
NR = R_FB + FB_PAD

ADAM_LR, ADAM_B1, ADAM_B2, ADAM_EPS, ADAM_WD, ADAM_STEP = 0.001, 0.9, 0.999, 1e-08, 0.01, 10

LANES = 128
VMEM_LIMIT = 56 * 1024 * 1024

CS = IN_COLS // N_DEV
RO_KV, RO_OUT, RO_BR, RO_IN = 0, 128, 256, 448
IN_ROWS = 1424
ROWS = RO_IN + IN_ROWS
O_GPRE, O_GPOST, O_GMEM, O_BM, O_BF, O_LOSS = 0, 1024, 2048, 3072, 6144, 6272
P_SMALL = 6400


def _cp(sem=None):
    return pltpu.CompilerParams(dimension_semantics=sem, vmem_limit_bytes=VMEM_LIMIT)


def _dot(a, b):
    return jnp.dot(a, b, preferred_element_type=F32)


def _dot_nt(a, b):
    return lax.dot_general(a, b, (((1,), (1,)), ((), ())), preferred_element_type=F32)


def _sigmoid(z):
    return 1.0 / (1.0 + jnp.exp(-z))


def _mm(a, b, *, name, at=False, bt=False, out_dtype=F32, tm=1024, tn=1024, tk=None):
    assert not (at and bt)
    K, M = a.shape if at else a.shape[::-1]
    N = b.shape[0] if bt else b.shape[1]
    tm, tn = min(tm, M), min(tn, N)
    tk = K if tk is None else min(tk, K)
    assert M % tm == 0 and N % tn == 0 and K % tk == 0
    nk = K // tk

    def body(a_ref, b_ref, o_ref, acc_ref):
        av = a_ref[...].astype(BF16)
        bv = b_ref[...].astype(BF16)
        if at:
            p = lax.dot_general(av, bv, (((0,), (0,)), ((), ())), preferred_element_type=F32)
        else:
            p = _dot_nt(av, bv) if bt else _dot(av, bv)
        if nk == 1:
            o_ref[...] = p.astype(out_dtype)
        else:
            k = pl.program_id(2)

            @pl.when(k == 0)
            def _():
                acc_ref[...] = p

            @pl.when(k > 0)
            def _():
                acc_ref[...] += p

            @pl.when(k == nk - 1)
            def _():
                o_ref[...] = acc_ref[...].astype(out_dtype)

    b_spec = (pl.BlockSpec((tn, tk), lambda i, j, k: (j, k)) if bt
              else pl.BlockSpec((tk, tn), lambda i, j, k: (k, j)))
    a_spec = (pl.BlockSpec((tk, tm), lambda i, j, k: (k, i)) if at
              else pl.BlockSpec((tm, tk), lambda i, j, k: (i, k)))
    return pl.pallas_call(
        body, name=name, grid=(M // tm, N // tn, nk),
        in_specs=[a_spec, b_spec],
        out_specs=pl.BlockSpec((tm, tn), lambda i, j, k: (i, j)),
        out_shape=jax.ShapeDtypeStruct((M, N), out_dtype),
        scratch_shapes=[pltpu.VMEM((tm, tn) if nk > 1 else (8, LANES), F32)],
        compiler_params=_cp(("parallel", "parallel", "arbitrary")),
    )(a, b)


def _rms_fwd(x, g, *, name):
    S, D = x.shape
    tm = min(512, S)

    def body(x_ref, g_ref, o_ref):
        xv = x_ref[...]
        r = lax.rsqrt(jnp.mean(xv * xv, axis=-1, keepdims=True) + EPS)
        o_ref[...] = (xv * r * g_ref[...]).astype(BF16)

    return pl.pallas_call(
        body, name=name, grid=(S // tm,),
        in_specs=[pl.BlockSpec((tm, D), lambda i: (i, 0)), pl.BlockSpec((1, D), lambda i: (0, 0))],
        out_specs=pl.BlockSpec((tm, D), lambda i: (i, 0)),
        out_shape=jax.ShapeDtypeStruct((S, D), BF16),
        compiler_params=_cp(("parallel",)),
    )(x, g)


def _rms_bwd(x, g, dh, dy, *, name):
    S, D = x.shape
    tm = min(512, S)
    want_dx = dy is not None

    def body(*refs):
        if want_dx:
            x_ref, g_ref, dh_ref, dy_ref, dx_ref, dg_ref = refs
        else:
            x_ref, g_ref, dh_ref, dg_ref = refs
        i = pl.program_id(0)
        xv = x_ref[...]
        r = lax.rsqrt(jnp.mean(xv * xv, axis=-1, keepdims=True) + EPS)
        xh = xv * r
        dhv = dh_ref[...]
        part = jnp.sum(dhv * xh, axis=0, keepdims=True)

        @pl.when(i == 0)
        def _():
            dg_ref[...] = part

        @pl.when(i > 0)
        def _():
            dg_ref[...] += part

        if want_dx:
            dxh = dhv * g_ref[...]
            dx_ref[...] = dy_ref[...] + r * (dxh - xh * jnp.mean(dxh * xh, axis=-1, keepdims=True))

    row = pl.BlockSpec((tm, D), lambda i: (i, 0))
    vec = pl.BlockSpec((1, D), lambda i: (0, 0))
    if want_dx:
        return pl.pallas_call(
            body, name=name, grid=(S // tm,), in_specs=[row, vec, row, row], out_specs=[row, vec],
            out_shape=[jax.ShapeDtypeStruct((S, D), F32), jax.ShapeDtypeStruct((1, D), F32)],
            compiler_params=_cp(("arbitrary",)))(x, g, dh, dy)
    return pl.pallas_call(
        body, name=name, grid=(S // tm,), in_specs=[row, vec, row], out_specs=vec,
        out_shape=jax.ShapeDtypeStruct((1, D), F32),
        compiler_params=_cp(("arbitrary",)))(x, g, dh)


def _post(x, out, tgt, g, *, name):
    S, D = x.shape
    tm = min(512, S)

    def body(x_ref, o_ref, t_ref, g_ref, dy_ref, do_ref, dg_ref, loss_ref):
        i = pl.program_id(0)
        ov = o_ref[...]
        r = lax.rsqrt(jnp.mean(ov * ov, axis=-1, keepdims=True) + EPS)
        n = ov * r
        gv = g_ref[...]
        e = (x_ref[...] + n * gv) - t_ref[...]
        lpart = 0.5 * jnp.sum(jnp.mean(e * e, axis=-1, keepdims=True), axis=0, keepdims=True)
        dy = e * (1.0 / D)
        dy_ref[...] = dy
        dn = dy * gv
        do_ref[...] = (r * (dn - n * jnp.mean(dn * n, axis=-1, keepdims=True))).astype(BF16)
        gpart = jnp.sum(dy * n, axis=0, keepdims=True)
        lrow = jnp.broadcast_to(lpart, (1, LANES))

        @pl.when(i == 0)
        def _():
            dg_ref[...] = gpart
            loss_ref[...] = lrow

        @pl.when(i > 0)
        def _():
            dg_ref[...] += gpart
            loss_ref[...] += lrow

    row = pl.BlockSpec((tm, D), lambda i: (i, 0))
    vec = pl.BlockSpec((1, D), lambda i: (0, 0))
    return pl.pallas_call(
        body, name=name, grid=(S // tm,), in_specs=[row, row, row, vec],
        out_specs=[row, row, vec, pl.BlockSpec((1, LANES), lambda i: (0, 0))],
        out_shape=[jax.ShapeDtypeStruct((S, D), F32), jax.ShapeDtypeStruct((S, D), BF16),
                   jax.ShapeDtypeStruct((1, D), F32), jax.ShapeDtypeStruct((1, LANES), F32)],
        compiler_params=_cp(("arbitrary",)))(x, out, tgt, g)


def _to_classes(t, d):
    if d == 1:
        return t
    S, C = t.shape
    return t.reshape(S // d, d, C).transpose(1, 0, 2).reshape(S, C)


def _from_classes(t, d):
    if d == 1:
        return t
    S, C = t.shape
    return t.reshape(d, S // d, C).transpose(1, 0, 2).reshape(S, C)


def _rope(x, c, s1, s2):
    return x * c + pltpu.roll(x, LANES - ROT // 2, 1) * s1 + pltpu.roll(x, ROT // 2, 1) * s2


def _unrope(d, c, s1, s2):
    return d * c + pltpu.roll(d * s1, ROT // 2, 1) + pltpu.roll(d * s2, LANES - ROT // 2, 1)


def _a_band(qb):
    r = lax.broadcasted_iota(jnp.int32, (qb, qb + HEAD), 0)
    c = lax.broadcasted_iota(jnp.int32, (qb, qb + HEAD), 1)
    return jnp.logical_and(c >= r, c <= r + HEAD)


def _a_first_ok(qb, n):
    c = lax.broadcasted_iota(jnp.int32, (qb, qb + HEAD), 1)
    return jnp.logical_or(c >= HEAD, n > 0)


def _a_last_ok(qb, has_next):
    c = lax.broadcasted_iota(jnp.int32, (qb, qb + HEAD), 1)
    return jnp.logical_or(c < qb, has_next)


A_SCALE = HEAD ** -0.5


def _a_geometry(S, g):
    d = DIL[g]
    L = S // d
    TQ = min(512, L)
    return d, L, TQ, TQ // HEAD, L // TQ, L // HEAD


def _proj_rope(h, w, tabs, *, name):
    S, D = h.shape
    tm = min(512, S)

    def body(h_ref, w_ref, c_ref, s1_ref, s2_ref, o_ref):
        tc = (c_ref[...], s1_ref[...], s2_ref[...])
        u = _dot_nt(h_ref[...], w_ref[...])
        for j in range(3 * A_HEADS):
            sl = slice(j * HEAD, (j + 1) * HEAD)
            o_ref[:, sl] = (_rope(u[:, sl], *tc) if j < 2 * A_HEADS else u[:, sl]).astype(BF16)

    tab = pl.BlockSpec((tm, LANES), lambda i: (i, 0))
    return pl.pallas_call(
        body, name=name, grid=(S // tm,),
        in_specs=[pl.BlockSpec((tm, D), lambda i: (i, 0)), pl.BlockSpec((3 * A_WIDTH, D), lambda i: (0, 0)),
                  tab, tab, tab],
        out_specs=pl.BlockSpec((tm, 3 * A_WIDTH), lambda i: (i, 0)),
        out_shape=jax.ShapeDtypeStruct((S, 3 * A_WIDTH), BF16),
        compiler_params=_cp(("parallel",)))(h, w, *tabs)


def _attn_a_fwd(qkv, g, *, name):
    S = qkv.shape[0]
    d, L, TQ, nsub, nb, nblk = _a_geometry(S, g)

    def body(q_ref, kc_ref, kp_ref, vc_ref, vp_ref, o_ref, l_ref):
        n = pl.program_id(1)
        QB = min(2 * HEAD, TQ)
        band = _a_band(QB)
        first = jnp.logical_and(band, _a_first_ok(QB, n))
        for h in range(A_HEADS):
            hs = slice(h * HEAD, (h + 1) * HEAD)
            for hh in range(TQ // QB):
                sl = slice(hh * QB, (hh + 1) * QB)
                pv = slice(hh * QB - HEAD, hh * QB)
                kcat = jnp.concatenate([kp_ref[:, hs] if hh == 0 else kc_ref[pv, hs], kc_ref[sl, hs]], axis=0)
                vcat = jnp.concatenate([vp_ref[:, hs] if hh == 0 else vc_ref[pv, hs], vc_ref[sl, hs]], axis=0)
                s = jnp.where(first if hh == 0 else band, _dot_nt(q_ref[sl, hs], kcat) * A_SCALE, NEG)
                m = jnp.max(s, axis=-1, keepdims=True)
                p = jnp.exp(s - m)
                den = jnp.sum(p, axis=-1, keepdims=True)
                o_ref[sl, hs] = _dot(p.astype(BF16), vcat) / den
                l_ref[sl, hs] = jnp.broadcast_to(m + jnp.log(den), (QB, HEAD))

    rcur = lambda r, n: r * nb + n
    rprv = lambda r, n: r * nblk + jnp.maximum(n * nsub - 1, 0)
    cur = lambda off: pl.BlockSpec((TQ, A_WIDTH), lambda r, n: (rcur(r, n), off))
    prv = lambda off: pl.BlockSpec((HEAD, A_WIDTH), lambda r, n: (rprv(r, n), off))
    out = pl.BlockSpec((TQ, A_WIDTH), lambda r, n: (rcur(r, n), 0))
    return pl.pallas_call(
        body, name=name, grid=(d, nb),
        in_specs=[cur(0), cur(1), prv(1), cur(2), prv(2)],
        out_specs=[out, out],
        out_shape=[jax.ShapeDtypeStruct((S, A_WIDTH), F32)] * 2,
        compiler_params=_cp(("parallel", "parallel")),
    )(qkv, qkv, qkv, qkv, qkv)


def _attn_a_dq(qkv, tabs, g, do, lse, adj, *, name):
    S = qkv.shape[0]
    d, L, TQ, nsub, nb, nblk = _a_geometry(S, g)

    def body(q_ref, kc_ref, kp_ref, vc_ref, vp_ref, do_ref, l_ref, adj_ref, c_ref, s1_ref, s2_ref, dq_ref):
        n = pl.program_id(1)
        QB = min(2 * HEAD, TQ)
        band = _a_band(QB)
        first = jnp.logical_and(band, _a_first_ok(QB, n))
        for h in range(A_HEADS):
            hs = slice(h * HEAD, (h + 1) * HEAD)
            for hh in range(TQ // QB):
                sl = slice(hh * QB, (hh + 1) * QB)
                pv = slice(hh * QB - HEAD, hh * QB)
                kcat = jnp.concatenate([kp_ref[:, hs] if hh == 0 else kc_ref[pv, hs], kc_ref[sl, hs]], axis=0)
                vcat = jnp.concatenate([vp_ref[:, hs] if hh == 0 else vc_ref[pv, hs], vc_ref[sl, hs]], axis=0)
                s = jnp.where(first if hh == 0 else band, _dot_nt(q_ref[sl, hs], kcat) * A_SCALE, NEG)
                p = jnp.exp(s - l_ref[sl, hs][:, :1])
                ds = p * (_dot_nt(do_ref[sl, hs], vcat) + adj_ref[sl, hs][:, :1])
                dq = _dot(ds.astype(BF16), kcat) * A_SCALE
                dq_ref[sl, hs] = _unrope(dq, c_ref[sl, :], s1_ref[sl, :], s2_ref[sl, :]).astype(BF16)

    rcur = lambda r, n: r * nb + n
    rprv = lambda r, n: r * nblk + jnp.maximum(n * nsub - 1, 0)
    cur = lambda off: pl.BlockSpec((TQ, A_WIDTH), lambda r, n: (rcur(r, n), off))
    prv = lambda off: pl.BlockSpec((HEAD, A_WIDTH), lambda r, n: (rprv(r, n), off))
    tcur = pl.BlockSpec((TQ, LANES), lambda r, n: (rcur(r, n), 0))
    blk = cur(0)
    return pl.pallas_call(
        body, name=name, grid=(d, nb),
        in_specs=[cur(0), cur(1), prv(1), cur(2), prv(2), blk, blk, blk, tcur, tcur, tcur],
        out_specs=blk,
        out_shape=jax.ShapeDtypeStruct((S, A_WIDTH), BF16),
        compiler_params=_cp(("parallel", "parallel")),
    )(qkv, qkv, qkv, qkv, qkv, do, lse, adj, *tabs)


def _attn_a_dkv(qkv, tabs, g, do, lse, adj, *, name):
    S = qkv.shape[0]
    d, L, TQ, nsub, nb, nblk = _a_geometry(S, g)

    def body(qc_ref, qn_ref, kc_ref, vc_ref, doc_ref, don_ref, lc_ref, ln_ref, ac_ref, an_ref,
             c_ref, s1_ref, s2_ref, dk_ref, dv_ref):
        n = pl.program_id(1)
        QB = min(2 * HEAD, TQ)
        nh = TQ // QB
        band = _a_band(QB)
        end = jnp.logical_and(band, _a_last_ok(QB, n < nb - 1))
        for h in range(A_HEADS):
            hs = slice(h * HEAD, (h + 1) * HEAD)
            for kh in range(nh):
                sl = slice(kh * QB, (kh + 1) * QB)
                nx = slice((kh + 1) * QB, (kh + 1) * QB + HEAD)
                last = kh == nh - 1
                cat = lambda cur, nxt: jnp.concatenate([cur[sl, hs], nxt[:, hs] if last else cur[nx, hs]], axis=0)
                qcat = cat(qc_ref, qn_ref)
                docat = cat(doc_ref, don_ref)
                lt = cat(lc_ref, ln_ref).T[:1, :]
                at = cat(ac_ref, an_ref).T[:1, :]
                st = jnp.where(end if last else band, _dot_nt(kc_ref[sl, hs], qcat) * A_SCALE, NEG)
                pt = jnp.exp(st - lt)
                dv_ref[sl, hs] = _dot(pt.astype(BF16), docat).astype(BF16)
                dst = pt * (_dot_nt(vc_ref[sl, hs], docat) + at)
                dk = _dot(dst.astype(BF16), qcat) * A_SCALE
                dk_ref[sl, hs] = _unrope(dk, c_ref[sl, :], s1_ref[sl, :], s2_ref[sl, :]).astype(BF16)

    rcur = lambda r, n: r * nb + n
    rnxt = lambda r, n: r * nblk + jnp.minimum((n + 1) * nsub, nblk - 1)
    cur = lambda off: pl.BlockSpec((TQ, A_WIDTH), lambda r, n: (rcur(r, n), off))
    nxu = lambda off: pl.BlockSpec((HEAD, A_WIDTH), lambda r, n: (rnxt(r, n), off))
    tcur = pl.BlockSpec((TQ, LANES), lambda r, n: (rcur(r, n), 0))
    blk, bnx = cur(0), nxu(0)
    return pl.pallas_call(
        body, name=name, grid=(d, nb),
        in_specs=[cur(0), nxu(0), cur(1), cur(2), blk, bnx, blk, bnx, blk, bnx, tcur, tcur, tcur],
        out_specs=[blk, blk],
        out_shape=[jax.ShapeDtypeStruct((S, A_WIDTH), BF16)] * 2,
        compiler_params=_cp(("parallel", "parallel")),
    )(qkv, qkv, qkv, qkv, do, do, lse, lse, adj, adj, *tabs)


def _silu_parts(z):
    sg = _sigmoid(z)
    return z * sg, sg * (1.0 + z * (1.0 - sg))


def _merge_a_fwd(os_, ls_, ur, *, name):
    S = ur.shape[0]
    tm = min(512, S)

    def body(o0, o1, o2, l0, l1, l2, z_ref, y_ref):
        ls = [l0[...], l1[...], l2[...]]
        mx = jnp.maximum(jnp.maximum(ls[0], ls[1]), ls[2])
        es = [jnp.exp(l - mx) for l in ls]
        den = es[0] + es[1] + es[2]
        y = (es[0] / den) * o0[...] + (es[1] / den) * o1[...] + (es[2] / den) * o2[...]
        y_ref[...] = (y * _silu_parts(z_ref[...])[0]).astype(BF16)

    blk = pl.BlockSpec((tm, A_WIDTH), lambda i: (i, 0))
    return pl.pallas_call(
        body, name=name, grid=(S // tm,),
        in_specs=[blk] * 6 + [pl.BlockSpec((tm, A_WIDTH), lambda i: (i, R_ZA // A_WIDTH))],
        out_specs=blk, out_shape=jax.ShapeDtypeStruct((S, A_WIDTH), BF16),
        compiler_params=_cp(("parallel",)))(*os_, *ls_, ur)


def _merge_a_bwd(os_, ls_, ur, dya, *, name):
    S = ur.shape[0]
    tm = min(256, S)

    def body(o0, o1, o2, l0, l1, l2, z_ref, dy_ref, d0, d1, d2, a0, a1, a2, dz_ref):
        ls = [l0[...], l1[...], l2[...]]
        ov = [o0[...], o1[...], o2[...]]
        mx = jnp.maximum(jnp.maximum(ls[0], ls[1]), ls[2])
        es = [jnp.exp(l - mx) for l in ls]
        den = es[0] + es[1] + es[2]
        ws = [e / den for e in es]
        y = ws[0] * ov[0] + ws[1] * ov[1] + ws[2] * ov[2]
        sz, dsz = _silu_parts(z_ref[...])
        dyv = dy_ref[...]
        dz_ref[...] = (dyv * y * dsz).astype(BF16)
        dyp = dyv * sz
        for h in range(A_HEADS):
            sl = slice(h * HEAD, (h + 1) * HEAD)
            t = jnp.zeros((tm, 1), F32)
            for gi in range(3):
                t = t + ws[gi][:, sl][:, :1] * jnp.sum(dyp[:, sl] * ov[gi][:, sl], axis=-1, keepdims=True)
            for gi, (dref, aref) in enumerate(((d0, a0), (d1, a1), (d2, a2))):
                wg = ws[gi][:, sl]
                dref[:, sl] = (wg * dyp[:, sl]).astype(BF16)
                aref[:, sl] = -wg * t

    blk = pl.BlockSpec((tm, A_WIDTH), lambda i: (i, 0))
    outs = pl.pallas_call(
        body, name=name, grid=(S // tm,),
        in_specs=[blk] * 6 + [pl.BlockSpec((tm, A_WIDTH), lambda i: (i, R_ZA // A_WIDTH)), blk],
        out_specs=[blk] * 7,
        out_shape=[jax.ShapeDtypeStruct((S, A_WIDTH), BF16)] * 3
        + [jax.ShapeDtypeStruct((S, A_WIDTH), F32)] * 3 + [jax.ShapeDtypeStruct((S, A_WIDTH), BF16)],
        compiler_params=_cp(("parallel",)))(*os_, *ls_, ur, dya)
    return outs[0:3], outs[3:6], outs[6]


def _logf(ur, bf_pad, *, name):
    S = ur.shape[0]
    tm = min(1024, S)

    def body(u_ref, b_ref, o_ref):
        z = u_ref[...] + b_ref[...]
        o_ref[...] = jnp.minimum(z, 0.0) - jnp.log(1.0 + jnp.exp(-jnp.abs(z)))

    return pl.pallas_call(
        body, name=name, grid=(S // tm,),
        in_specs=[pl.BlockSpec((tm, FB_PAD), lambda i: (i, R_FB // FB_PAD)),
                  pl.BlockSpec((1, FB_PAD), lambda i: (0, 0))],
        out_specs=pl.BlockSpec((tm, FB_PAD), lambda i: (i, 0)),
        out_shape=jax.ShapeDtypeStruct((S, FB_PAD), F32),
        compiler_params=_cp(("parallel",)))(ur, bf_pad)


def _cumsum_lanes(x, reverse, *, name):
    nt, H, _ = x.shape

    def body(x_ref, o_ref):
        lane = lax.broadcasted_iota(jnp.int32, (H, LANES), 1)

        def tile(t, carry):
            tt = nt - 1 - t if reverse else t
            v = x_ref[tt]
            k = 1
            while k < LANES:
                if reverse:
                    v = v + jnp.where(lane < LANES - k, pltpu.roll(v, LANES - k, 1), 0.0)
                else:
                    v = v + jnp.where(lane >= k, pltpu.roll(v, k, 1), 0.0)
                k *= 2
            v = v + carry
            o_ref[tt] = v
            edge = v[:, :1] if reverse else v[:, LANES - 1:]
            return jnp.broadcast_to(edge, (H, LANES))

        lax.fori_loop(0, nt, tile, jnp.zeros((H, LANES), F32))

    return pl.pallas_call(
        body, name=name, out_shape=jax.ShapeDtypeStruct((nt, H, LANES), F32),
        in_specs=[pl.BlockSpec(memory_space=pltpu.VMEM)], out_specs=pl.BlockSpec(memory_space=pltpu.VMEM),
        compiler_params=_cp())(x)


B_SCALE = B_HEAD ** -0.5


def _pair_masks():
    lane = lax.broadcasted_iota(jnp.int32, (1, LANES), 1)
    row = lax.broadcasted_iota(jnp.int32, (LANES, 1), 0)
    return (lane < B_HEAD, lane >= B_HEAD), (row < B_HEAD, row >= B_HEAD)


def _causal_t(T):
    r = lax.broadcasted_iota(jnp.int32, (T, T), 0)
    c = lax.broadcasted_iota(jnp.int32, (T, T), 1)
    return r <= c


def _zero_other(x, keep):
    return jnp.where(keep, x, jnp.zeros_like(x))


def _fox_aug(ub, ckb, *, name):
    S = ub.shape[0]
    T = min(512, S)

    def body(q_ref, k_ref, c_ref, qa_ref, ka_ref):
        lane = lax.broadcasted_iota(jnp.int32, (1, LANES), 1)
        q = q_ref[...] * B_SCALE
        k = k_ref[...]
        for a in range(2):
            own = (lane < B_HEAD) if a == 0 else (lane >= B_HEAD)
            o = B_HEAD if a == 0 else 0
            c = c_ref[a]
            hi = c.astype(BF16)
            r1 = c - hi.astype(F32)
            mid = r1.astype(BF16)
            lo = (r1 - mid.astype(F32)).astype(BF16)
            pieces = (hi, mid, lo)
            one = jnp.ones((T, LANES), BF16)
            qa = jnp.where(own, q, jnp.zeros_like(q))
            ka = jnp.where(own, k, jnp.zeros_like(k))
            for t in range(3):
                qa = jnp.where(lane == o + t, pieces[t], qa)
                qa = jnp.where(lane == o + 3 + t, one, qa)
                ka = jnp.where(lane == o + t, one, ka)
                ka = jnp.where(lane == o + 3 + t, -pieces[t], ka)
            qa_ref[a] = qa
            ka_ref[a] = ka

    out = pl.BlockSpec((2, T, LANES), lambda h, i: (h, i, 0))
    return pl.pallas_call(
        body, name=name, grid=(B_HEADS // 2, S // T),
        in_specs=[pl.BlockSpec((T, LANES), lambda h, i: (i, h)), pl.BlockSpec((T, LANES), lambda h, i: (i, 4 + h)), out],
        out_specs=[out, out], out_shape=[jax.ShapeDtypeStruct((B_HEADS, S, LANES), BF16)] * 2,
        compiler_params=_cp(("parallel", "parallel")))(ub, ub, ckb)


def _fox_fwd(qaug, kaug, vt, *, name):
    S = qaug.shape[1]
    T = min(512, S)
    nq = S // T

    def body(q_ref, k_ref, vt_ref, o_ref, l_ref, m_s, l_s, acc_s):
        i = pl.program_id(1)
        _, rows = _pair_masks()
        qm = [q_ref[0], q_ref[1]]
        m_s[...] = jnp.full((2, 1, T), NEG, F32)
        l_s[...] = jnp.zeros((2, 1, T), F32)
        acc_s[...] = jnp.zeros((LANES, T), F32)

        def step(j, masked):
            off = pl.multiple_of(j * T, T)
            vtj = vt_ref[j]
            upd = jnp.zeros((LANES, T), F32)
            alphas = []
            for a in range(2):
                st = _dot_nt(k_ref[a, pl.ds(off, T), :], qm[a])
                if masked:
                    st = jnp.where(_causal_t(T), st, NEG)
                m_old = m_s[a]
                m_new = jnp.maximum(m_old, jnp.max(st, axis=0, keepdims=True))
                alpha = jnp.exp(m_old - m_new)
                pt = jnp.exp(st - m_new)
                l_s[a] = alpha * l_s[a] + jnp.sum(pt, axis=0, keepdims=True)
                m_s[a] = m_new
                upd = upd + _dot(_zero_other(vtj, rows[a]), pt.astype(BF16))
                alphas.append(alpha)
            acc_s[...] = acc_s[...] * jnp.where(rows[0], alphas[0], alphas[1]) + upd

        def loop(j, carry):
            step(j, False)
            return carry

        lax.fori_loop(0, i, loop, 0)
        step(i, True)
        o_ref[...] = (acc_s[...] / jnp.where(rows[0], l_s[0], l_s[1])).T
        l_ref[0] = m_s[0] + jnp.log(l_s[0])
        l_ref[1] = m_s[1] + jnp.log(l_s[1])

    stat = pl.BlockSpec((2, None, 1, T), lambda h, i: (h, i, 0, 0))
    return pl.pallas_call(
        body, name=name, grid=(B_HEADS // 2, nq),
        in_specs=[pl.BlockSpec((2, T, LANES), lambda h, i: (h, i, 0)),
                  pl.BlockSpec((2, S, LANES), lambda h, i: (h, 0, 0)),
                  pl.BlockSpec((nq, LANES, T), lambda h, i: (0, h, 0))],
        out_specs=[pl.BlockSpec((T, LANES), lambda h, i: (i, h)), stat],
        out_shape=[jax.ShapeDtypeStruct((S, A_WIDTH), F32), jax.ShapeDtypeStruct((B_HEADS, nq, 1, T), F32)],
        scratch_shapes=[pltpu.VMEM((2, 1, T), F32), pltpu.VMEM((2, 1, T), F32), pltpu.VMEM((LANES, T), F32)],
        compiler_params=_cp(("parallel", "parallel")),
    )(qaug, kaug, vt)


def _fox_delta(o, do, *, name):
    S = o.shape[0]
    T = min(512, S)
    nq = S // T

    def body(o_ref, do_ref, d_ref):
        _, rows = _pair_masks()
        prod_t = (do_ref[...].astype(F32) * o_ref[...]).T
        d_ref[0] = jnp.sum(_zero_other(prod_t, rows[0]), axis=0, keepdims=True)
        d_ref[1] = jnp.sum(_zero_other(prod_t, rows[1]), axis=0, keepdims=True)

    tile = pl.BlockSpec((T, LANES), lambda h, i: (i, h))
    return pl.pallas_call(
        body, name=name, grid=(B_HEADS // 2, nq), in_specs=[tile, tile],
        out_specs=pl.BlockSpec((2, None, 1, T), lambda h, i: (h, i, 0, 0)),
        out_shape=jax.ShapeDtypeStruct((B_HEADS, nq, 1, T), F32),
        compiler_params=_cp(("parallel", "parallel")))(o, do)


def _fox_bwd(ub, qaug, kaug, kt, do, lse, delta, *, name):
    S = ub.shape[0]
    T = min(512, S)
    nq = S // T

    def body(k_ref, v_ref, kt_ref, q_ref, do_ref, l_ref, dl_ref,
             dk_ref, dv_ref, dck_ref, dqt_ref, dcq_ref, dk_s, dv_s, dc_s):
        j = pl.program_id(1)
        lanes, rows = _pair_masks()
        vv = v_ref[...]
        ktj = kt_ref[...]
        km = [k_ref[0], k_ref[1]]
        ktm = [_zero_other(ktj, rows[0]), _zero_other(ktj, rows[1])]
        dk_s[...] = jnp.zeros((2, T, LANES), F32)
        dv_s[...] = jnp.zeros((T, LANES), F32)
        dc_s[...] = jnp.zeros((2, T, 1), F32)

        @pl.when(j == 0)
        def _():
            dqt_ref[...] = jnp.zeros((nq, LANES, T), F32)
            dcq_ref[...] = jnp.zeros((2, nq, 1, T), F32)

        def step(i, masked):
            off = pl.multiple_of(i * T, T)
            doi = do_ref[pl.ds(off, T), :]
            upd = jnp.zeros((LANES, T), F32)
            for a in range(2):
                qi = q_ref[a, pl.ds(off, T), :]
                st = _dot_nt(km[a], qi)
                if masked:
                    st = jnp.where(_causal_t(T), st, NEG)
                pt = jnp.exp(st - l_ref[a, i])
                doa = _zero_other(doi, lanes[a])
                dv_s[...] += _dot(pt.astype(BF16), doa)
                dst = pt * (_dot_nt(vv, doa) - dl_ref[a, i])
                dsb = dst.astype(BF16)
                dk_s[a] += _dot(dsb, qi)
                upd = upd + _dot(ktm[a], dsb)
                dc_s[a] -= jnp.sum(dst, axis=-1, keepdims=True)
                dcq_ref[a, i] += jnp.sum(dst, axis=0, keepdims=True)
            dqt_ref[i] += upd

        def loop(i, carry):
            step(i, False)
            return carry

        step(j, True)
        lax.fori_loop(j + 1, nq, loop, 0)
        dk_ref[...] = jnp.where(lanes[0], dk_s[0], dk_s[1]).astype(BF16)
        dv_ref[...] = dv_s[...].astype(BF16)
        dck_ref[...] = dc_s[...]

    rowv = pl.BlockSpec((2, nq, 1, T), lambda h, j: (h, 0, 0, 0))
    tile = pl.BlockSpec((T, LANES), lambda h, j: (j, h))
    return pl.pallas_call(
        body, name=name, grid=(B_HEADS // 2, nq),
        in_specs=[pl.BlockSpec((2, T, LANES), lambda h, j: (h, j, 0)),
                  pl.BlockSpec((T, LANES), lambda h, j: (j, 8 + h)),
                  pl.BlockSpec((None, LANES, T), lambda h, j: (j, h, 0)),
                  pl.BlockSpec((2, S, LANES), lambda h, j: (h, 0, 0)),
                  pl.BlockSpec((S, LANES), lambda h, j: (0, h)),
                  rowv, rowv],
        out_specs=[tile, tile, pl.BlockSpec((2, T, 1), lambda h, j: (h, j, 0)),
                   pl.BlockSpec((nq, LANES, T), lambda h, j: (0, h, 0)), rowv],
        out_shape=[jax.ShapeDtypeStruct((S, A_WIDTH), BF16)] * 2 + [jax.ShapeDtypeStruct((B_HEADS, S, 1), F32),
                   jax.ShapeDtypeStruct((nq, A_WIDTH, T), F32), jax.ShapeDtypeStruct((B_HEADS, nq, 1, T), F32)],
        scratch_shapes=[pltpu.VMEM((2, T, LANES), F32), pltpu.VMEM((T, LANES), F32), pltpu.VMEM((2, T, 1), F32)],
        compiler_params=_cp(("parallel", "arbitrary")),
    )(kaug, ub, kt, qaug, do, lse, delta)


def _gate_fwd(o, ur, zcol, *, name):
    S = ur.shape[0]
    tm = min(1024, S)

    def body(o_ref, z_ref, y_ref):
        y_ref[...] = (o_ref[...] * _silu_parts(z_ref[...])[0]).astype(BF16)

    blk = pl.BlockSpec((tm, A_WIDTH), lambda i: (i, 0))
    return pl.pallas_call(
        body, name=name, grid=(S // tm,),
        in_specs=[blk, pl.BlockSpec((tm, A_WIDTH), lambda i: (i, zcol // A_WIDTH))],
        out_specs=blk, out_shape=jax.ShapeDtypeStruct((S, A_WIDTH), BF16),
        compiler_params=_cp(("parallel",)))(o, ur)


def _gate_bwd(o, ur, zcol, dy, *, name):
    S = ur.shape[0]
    tm = min(1024, S)

    def body(o_ref, z_ref, dy_ref, do_ref, dz_ref):
        sz, dsz = _silu_parts(z_ref[...])
        dyv = dy_ref[...]
        do_ref[...] = (dyv * sz).astype(BF16)
        dz_ref[...] = (dyv * o_ref[...] * dsz).astype(BF16)

    blk = pl.BlockSpec((tm, A_WIDTH), lambda i: (i, 0))
    return pl.pallas_call(
        body, name=name, grid=(S // tm,),
        in_specs=[blk, pl.BlockSpec((tm, A_WIDTH), lambda i: (i, zcol // A_WIDTH)), blk],
        out_specs=[blk, blk], out_shape=[jax.ShapeDtypeStruct((S, A_WIDTH), BF16)] * 2,
        compiler_params=_cp(("parallel",)))(o, ur, dy)


def _dfb(ur, bf_pad, dlogf_pad, *, name):
    S = ur.shape[0]
    tm = min(1024, S)

    def body(u_ref, b_ref, d_ref, o_ref, s_ref):
        i = pl.program_id(0)
        dv = d_ref[...] * _sigmoid(-(u_ref[...] + b_ref[...]))
        o_ref[...] = dv.astype(BF16)
        part = jnp.sum(dv, axis=0, keepdims=True)

        @pl.when(i == 0)
        def _():
            s_ref[...] = part

        @pl.when(i > 0)
        def _():
            s_ref[...] += part

    vec = pl.BlockSpec((1, FB_PAD), lambda i: (0, 0))
    blk = pl.BlockSpec((tm, FB_PAD), lambda i: (i, 0))
    return pl.pallas_call(
        body, name=name, grid=(S // tm,),
        in_specs=[pl.BlockSpec((tm, FB_PAD), lambda i: (i, R_FB // FB_PAD)), vec, blk],
        out_specs=[blk, vec],
        out_shape=[jax.ShapeDtypeStruct((S, FB_PAD), BF16), jax.ShapeDtypeStruct((1, FB_PAD), F32)],
        compiler_params=_cp(("arbitrary",)))(ur, bf_pad, dlogf_pad)


M_SCALE = HEAD ** -0.5


def _mem_fwd(ur, mkv, *, name):
    S = ur.shape[0]
    T = min(512, S)

    def body(q_ref, z_ref, k_ref, v_ref, y_ref):
        s = _dot_nt(q_ref[...].astype(BF16), k_ref[...].astype(BF16)) * M_SCALE
        p = jnp.exp(s - jnp.max(s, axis=-1, keepdims=True))
        p = p / jnp.sum(p, axis=-1, keepdims=True)
        o = _dot(p.astype(BF16), v_ref[...].astype(BF16))
        y_ref[...] = (o * _silu_parts(z_ref[...])[0]).astype(BF16)

    return pl.pallas_call(
        body, name=name, grid=(S // T, M_HEADS),
        in_specs=[pl.BlockSpec((T, HEAD), lambda i, h: (i, R_QM // HEAD + h)),
                  pl.BlockSpec((T, HEAD), lambda i, h: (i, R_ZM // HEAD + h)),
                  pl.BlockSpec((N_MEM, HEAD), lambda i, h: (0, h)),
                  pl.BlockSpec((N_MEM, HEAD), lambda i, h: (0, M_HEADS + h))],
        out_specs=pl.BlockSpec((T, HEAD), lambda i, h: (i, h)),
        out_shape=jax.ShapeDtypeStruct((S, A_WIDTH), BF16),
        compiler_params=_cp(("parallel", "parallel")))(ur, ur, mkv, mkv)


def _mem_bwd(ur, mkv, dy, *, name):
    S = ur.shape[0]
    T = min(512, S)

    def body(q_ref, z_ref, k_ref, v_ref, dy_ref, dq_ref, dz_ref, dk_ref, dv_ref):
        i = pl.program_id(1)
        qv = q_ref[...].astype(BF16)
        kv = k_ref[...].astype(BF16)
        vv = v_ref[...].astype(BF16)
        s = _dot_nt(qv, kv) * M_SCALE
        p = jnp.exp(s - jnp.max(s, axis=-1, keepdims=True))
        p = p / jnp.sum(p, axis=-1, keepdims=True)
        o = _dot(p.astype(BF16), vv)
        sz, dsz = _silu_parts(z_ref[...])
        dyv = dy_ref[...]
        dz_ref[...] = (dyv * o * dsz).astype(BF16)
        dov = (dyv * sz).astype(BF16)
        dp = _dot_nt(dov, vv)
        ds = p * (dp - jnp.sum(p * dp, axis=-1, keepdims=True))
        dq_ref[...] = (_dot(ds.astype(BF16), kv) * M_SCALE).astype(BF16)
        dvp = _dot(p.T.astype(BF16), dov)
        dkp = _dot(ds.T.astype(BF16), qv) * M_SCALE

        @pl.when(i == 0)
        def _():
            dk_ref[...] = dkp
            dv_ref[...] = dvp

        @pl.when(i > 0)
        def _():
            dk_ref[...] += dkp
            dv_ref[...] += dvp

    tile = pl.BlockSpec((T, HEAD), lambda h, i: (i, h))
    acc = pl.BlockSpec((N_MEM, HEAD), lambda h, i: (0, h))
    return pl.pallas_call(
        body, name=name, grid=(M_HEADS, S // T),
        in_specs=[pl.BlockSpec((T, HEAD), lambda h, i: (i, R_QM // HEAD + h)),
                  pl.BlockSpec((T, HEAD), lambda h, i: (i, R_ZM // HEAD + h)),
                  pl.BlockSpec((N_MEM, HEAD), lambda h, i: (0, h)),
                  pl.BlockSpec((N_MEM, HEAD), lambda h, i: (0, M_HEADS + h)), tile],
        out_specs=[tile, tile, acc, acc],
        out_shape=[jax.ShapeDtypeStruct((S, A_WIDTH), BF16)] * 2
        + [jax.ShapeDtypeStruct((N_MEM, A_WIDTH), F32)] * 2,
        compiler_params=_cp(("parallel", "arbitrary")))(ur, ur, mkv, mkv, dy)


def _branch_fwd(ys, wbs, ur, b_merge, *, name):
    S = ur.shape[0]
    tm, tn = min(512, S), 512
    nj = D_MODEL // tn

    def body(ya, yb, ym, wa, wb, wm, g0, g1, g2, b0, b1, b2, mg_ref, p_ref):
        acc = jnp.zeros((tm, tn), F32)
        for i, (y, w, gr, br) in enumerate(((ya, wa, g0, b0), (yb, wb, g1, b1), (ym, wm, g2, b2))):
            pr = _dot(y[...], w[...])
            p_ref[i] = pr
            acc = acc + _sigmoid(gr[...] + br[...]) * pr
        mg_ref[...] = acc.astype(BF16)

    yspec = pl.BlockSpec((tm, A_WIDTH), lambda i, j: (i, 0))
    wspec = pl.BlockSpec((A_WIDTH, tn), lambda i, j: (0, j))
    gspec = lambda b: pl.BlockSpec((tm, tn), lambda i, j: (i, (R_GL + b * D_MODEL) // tn + j))
    bspec = lambda b: pl.BlockSpec((1, tn), lambda i, j: (0, b * nj + j))
    return pl.pallas_call(
        body, name=name, grid=(S // tm, nj),
        in_specs=[yspec] * 3 + [wspec] * 3 + [gspec(0), gspec(1), gspec(2), bspec(0), bspec(1), bspec(2)],
        out_specs=[pl.BlockSpec((tm, tn), lambda i, j: (i, j)),
                   pl.BlockSpec((3, tm, tn), lambda i, j: (0, i, j))],
        out_shape=[jax.ShapeDtypeStruct((S, D_MODEL), BF16), jax.ShapeDtypeStruct((3, S, D_MODEL), F32)],
        compiler_params=_cp(("parallel", "parallel")))(*ys, *wbs, ur, ur, ur, b_merge, b_merge, b_merge)


def _branch_bwd(dm, prods, ur, b_merge, *, name):
    S = ur.shape[0]
    tm = min(256, S)

    def body(dm_ref, p_ref, g0, g1, g2, b_ref, dp_ref, dgl_ref, db_ref):
        i = pl.program_id(0)
        dmv = dm_ref[...]
        parts = []
        for b, gr in enumerate((g0, g1, g2)):
            sl = slice(b * D_MODEL, (b + 1) * D_MODEL)
            gt = _sigmoid(gr[...] + b_ref[:, sl])
            dp_ref[b] = (dmv * gt).astype(BF16)
            dgl = dmv * p_ref[b] * gt * (1.0 - gt)
            dgl_ref[:, sl] = dgl.astype(BF16)
            parts.append(jnp.sum(dgl, axis=0, keepdims=True))
        part = jnp.concatenate(parts, axis=1)

        @pl.when(i == 0)
        def _():
            db_ref[...] = part

        @pl.when(i > 0)
        def _():
            db_ref[...] += part

    gspec = lambda b: pl.BlockSpec((tm, D_MODEL), lambda i: (i, R_GL // D_MODEL + b))
    vec = pl.BlockSpec((1, 3 * D_MODEL), lambda i: (0, 0))
    return pl.pallas_call(
        body, name=name, grid=(S // tm,),
        in_specs=[pl.BlockSpec((tm, D_MODEL), lambda i: (i, 0)),
                  pl.BlockSpec((3, tm, D_MODEL), lambda i: (0, i, 0)), gspec(0), gspec(1), gspec(2), vec],
        out_specs=[pl.BlockSpec((3, tm, D_MODEL), lambda i: (0, i, 0)),
                   pl.BlockSpec((tm, 3 * D_MODEL), lambda i: (i, 0)), vec],
        out_shape=[jax.ShapeDtypeStruct((3, S, D_MODEL), BF16), jax.ShapeDtypeStruct((S, 3 * D_MODEL), BF16),
                   jax.ShapeDtypeStruct((1, 3 * D_MODEL), F32)],
        compiler_params=_cp(("arbitrary",)))(dm, prods, ur, ur, ur, b_merge)


def _rope_tables(pos):
    half = ROT // 2
    S = pos.shape[0]
    inv = ROPE_THETA ** (-jnp.arange(half, dtype=F32) / half)
    per_row = LANES // half
    ang = jnp.repeat(pos.astype(F32).reshape(S // per_row, per_row), half, axis=1) * jnp.tile(inv, per_row)
    cos, sin = jnp.cos(ang).reshape(S, half), jnp.sin(ang).reshape(S, half)
    one = jnp.ones((S, LANES - ROT), F32)
    zero = jnp.zeros((S, LANES - ROT), F32)
    zh = jnp.zeros((S, half), F32)
    c = jnp.concatenate([cos, cos, one], axis=1)
    s1 = jnp.concatenate([-sin, zh, zero], axis=1)
    s2 = jnp.concatenate([zh, sin, zero], axis=1)
    return c, s1, s2


def _to_tiles(t):
    S, H = t.shape
    return t.reshape(S // LANES, LANES, H).transpose(0, 2, 1)


def _from_tiles(t):
    nt, H, _ = t.shape
    return t.transpose(1, 0, 2).reshape(H, nt * LANES)


def _local_step(x, mem, pos, tgt, g_pre, g_post, g_mem, wt, bf_pad, b_merge, w_kv, wbs, w_out):
    S = x.shape[0]
    T = min(512, S)
    nq = S // T
    tabs = _rope_tables(pos)

    h = _rms_fwd(x, g_pre, name="rms_pre")
    hs = [_to_classes(h, d) for d in DIL]
    tabs_g = [[_to_classes(t, d) for t in tabs] for d in DIL]
    qkvs = [_proj_rope(hs[g], wt[f"A{g}"], tabs_g[g], name=f"proj_a{g}") for g in range(3)]
    ub = _mm(h, wt["B"], bt=True, out_dtype=BF16, name="proj_b", tn=1536)
    ur = _mm(h, wt["R"], bt=True, name="proj_r", tn=1792)

    outs_c, lses_c = [], []
    for g in range(3):
        o, l = _attn_a_fwd(qkvs[g], g, name=f"attn_a_fwd{g}")
        outs_c.append(o)
        lses_c.append(l)
    outs_a = [_from_classes(o, d) for o, d in zip(outs_c, DIL)]
    lses_a = [_from_classes(l, d) for l, d in zip(lses_c, DIL)]
    ya = _merge_a_fwd(outs_a, lses_a, ur, name="merge_a_fwd")

    logf = _logf(ur, bf_pad, name="logf")
    c = _from_tiles(_cumsum_lanes(_to_tiles(logf[:, :B_HEADS]), False, name="cumsum_fwd"))
    ckb = jnp.broadcast_to(c[:, :, None], (B_HEADS, S, LANES))
    qaug, kaug = _fox_aug(ub, ckb, name="fox_aug")
    kt = ub[:, 512:1024].reshape(nq, T, 512).transpose(0, 2, 1)
    vt = ub[:, 1024:1536].reshape(nq, T, 512).transpose(0, 2, 1)
    ob, lse_b = _fox_fwd(qaug, kaug, vt, name="fox_fwd")
    yb = _gate_fwd(ob, ur, R_ZB, name="gate_b_fwd")

    hm = _rms_fwd(mem, g_mem, name="rms_mem")
    mkv = _mm(hm, w_kv, name="proj_mem")
    ym = _mem_fwd(ur, mkv, name="mem_fwd")

    merged, prods = _branch_fwd((ya, yb, ym), wbs, ur, b_merge, name="branch_fwd")
    out = _mm(merged, w_out, name="proj_out")
    dy, d_out, dg_post, loss_row = _post(x, out, tgt, g_post, name="post")

    dmerged = _mm(d_out, w_out, bt=True, name="d_merged")
    dw_out = _mm(merged, d_out, at=True, name="dw_out", tk=2048)
    dprods, dgl, db_merge = _branch_bwd(dmerged, prods, ur, b_merge, name="branch_bwd")
    dys, dwbs = [], []
    for i, (y, wb) in enumerate(zip((ya, yb, ym), wbs)):
        dys.append(_mm(dprods[i], wb, bt=True, name=f"d_y{i}"))
        dwbs.append(_mm(y, dprods[i], at=True, name=f"dw_branch{i}", tk=2048))

    dos_a, adjs_a, dza = _merge_a_bwd(outs_a, lses_a, ur, dys[0], name="merge_a_bwd")
    dus_a = []
    for g, d in enumerate(DIL):
        do_c, adj_c = _to_classes(dos_a[g], d), _to_classes(adjs_a[g], d)
        dq = _attn_a_dq(qkvs[g], tabs_g[g], g, do_c, lses_c[g], adj_c, name=f"attn_a_dq{g}")
        dk, dv = _attn_a_dkv(qkvs[g], tabs_g[g], g, do_c, lses_c[g], adj_c, name=f"attn_a_dkv{g}")
        dus_a.append(jnp.concatenate([dq, dk, dv], axis=1))

    dob, dzb = _gate_bwd(ob, ur, R_ZB, dys[1], name="gate_b_bwd")
    delta_b = _fox_delta(ob, dob, name="fox_delta")
    dkb, dvb, dc_k, dqt, dc_q = _fox_bwd(ub, qaug, kaug, kt, dob, lse_b, delta_b, name="fox_bwd")
    dqb = (dqt.transpose(0, 2, 1).reshape(S, A_WIDTH) * B_SCALE).astype(BF16)
    du_b = jnp.concatenate([dqb, dkb, dvb], axis=1)
    dc = dc_q.reshape(B_HEADS, S) + dc_k.reshape(B_HEADS, S)
    dlogf = _from_tiles(_cumsum_lanes(_to_tiles(dc.T), True, name="cumsum_bwd"))
    dlogf_pad = jnp.pad(dlogf.T, ((0, 0), (0, FB_PAD - B_HEADS)))
    dfb, db_forget = _dfb(ur, bf_pad, dlogf_pad, name="dfb")

    dqm, dzm, dmk, dmv = _mem_bwd(ur, mkv, dys[2], name="mem_bwd")
    dmkv = jnp.concatenate([dmk, dmv], axis=1).astype(BF16)
    dhm = _mm(dmkv, w_kv, bt=True, name="d_hm")
    dw_kv = _mm(hm, dmkv, at=True, name="dw_kv")
    dg_mem = _rms_bwd(mem, g_mem, dhm, None, name="rms_mem_bwd")

    du_r = jnp.concatenate([dza, dzb, dqm, dzm, dgl, dfb], axis=1)
    dh = _mm(du_r, wt["R"], name="d_h_r", tk=1792) + _mm(du_b, wt["B"], name="d_h_b", tk=1536)
    for g, d in enumerate(DIL):
        dh = dh + _from_classes(_mm(dus_a[g], wt[f"A{g}"], name=f"d_h_a{g}", tk=1536), d)
    dwt = {"R": _mm(du_r, h, at=True, name="dw_in_r", tm=1792, tk=1024),
           "B": _mm(du_b, h, at=True, name="dw_in_b", tm=1536, tk=2048)}
    for g in range(3):
        dwt[f"A{g}"] = _mm(dus_a[g], hs[g], at=True, name=f"dw_in_a{g}", tm=1536, tk=2048)
    grad_x, dg_pre = _rms_bwd(x, g_pre, dh, dy, name="rms_pre_bwd")

    return dict(loss=loss_row, grad_x=grad_x, dwt=dwt, dw_kv=dw_kv, dwbs=dwbs, dw_out=dw_out,
                dg_pre=dg_pre, dg_post=dg_post, dg_mem=dg_mem, db_forget=db_forget, db_merge=db_merge)


MESH = pl.DeviceIdType.MESH
ANY = pl.BlockSpec(memory_space=pl.ANY)


def _relations():
    return [(k >> 2 & 1, k >> 1 & 1, k & 1) for k in range(1, N_DEV)]


def _coords():
    return lax.axis_index("x"), lax.axis_index("y"), lax.axis_index("c")


def _all_gather(shard, *, name):
    R, W = shard.shape

    def body(x_ref, out_ref, send_sems, recv_sems, local_sem):
        x, y, c = _coords()
        me, sibling = (x, y, c), (x, y, 1 - c)
        chips = [(1 - x, y), (x, 1 - y), (1 - x, 1 - y)]

        def slot(px, py, pc):
            return out_ref.at[4 * px + 2 * py + pc]

        def copy(k, block, to, src=None):
            return pltpu.make_async_remote_copy(
                src_ref=slot(*block) if src is None else src, dst_ref=slot(*block),
                send_sem=send_sems.at[k], recv_sem=recv_sems.at[k], device_id=to, device_id_type=MESH)

        mine = pltpu.make_async_copy(x_ref, slot(*me), local_sem)
        mine.start()
        first = [copy(0, me, sibling, src=x_ref)]
        first += [copy(1 + j, me, (*chip, c), src=x_ref) for j, chip in enumerate(chips)]
        for cp in first:
            cp.start()
        passed = [copy(4 + j, (*chip, c), sibling) for j, chip in enumerate(chips)]
        for j, chip in enumerate(chips):
            copy(1 + j, (*chip, c), me).wait_recv()
            passed[j].start()
        copy(0, sibling, me).wait_recv()
        for j, chip in enumerate(chips):
            copy(4 + j, (*chip, 1 - c), me).wait_recv()
        for cp in first + passed:
            cp.wait_send()
        mine.wait()

    return pl.pallas_call(
        body, name=name, out_shape=jax.ShapeDtypeStruct((N_DEV, R, W), shard.dtype),
        in_specs=[ANY], out_specs=ANY,
        scratch_shapes=[pltpu.SemaphoreType.DMA((N_DEV - 1,)), pltpu.SemaphoreType.DMA((N_DEV - 1,)),
                        pltpu.SemaphoreType.DMA],
    )(shard)


N_CHIP = 4


def _exchange_pair(gbig, *, name):
    _, R, W = gbig.shape

    def body(g_ref, sib_ref, send_sems, recv_sems):
        x, y, c = _coords()
        copies = []
        for r in range(N_CHIP):
            px, py = x ^ (r >> 1), y ^ (r & 1)
            copies.append(pltpu.make_async_remote_copy(
                src_ref=g_ref.at[4 * px + 2 * py + (1 - c)], dst_ref=sib_ref.at[r],
                send_sem=send_sems.at[r], recv_sem=recv_sems.at[r], device_id=(x, y, 1 - c), device_id_type=MESH))
        for cp in copies:
            cp.start()
        for cp in copies:
            cp.wait_recv()
        for cp in copies:
            cp.wait_send()

    return pl.pallas_call(
        body, name=name, out_shape=jax.ShapeDtypeStruct((N_CHIP, R, W), gbig.dtype),
        in_specs=[ANY], out_specs=ANY,
        scratch_shapes=[pltpu.SemaphoreType.DMA((N_CHIP,)), pltpu.SemaphoreType.DMA((N_CHIP,))],
    )(gbig)


def _own_slabs():
    x, y, c = _coords()
    return jnp.stack([4 * (x ^ (r >> 1)) + 2 * (y ^ (r & 1)) + c for r in range(N_CHIP)]).astype(jnp.int32)


def _pair_sum(gbig, sib, own_idx, tr, *, name):
    _, R, W = gbig.shape

    def body(idx_ref, a_ref, b_ref, o_ref):
        o_ref[...] = (a_ref[...] + b_ref[...]).astype(BF16)

    return pl.pallas_call(
        body, name=name,
        grid_spec=pltpu.PrefetchScalarGridSpec(
            num_scalar_prefetch=1, grid=(N_CHIP - 1, R // tr),
            in_specs=[pl.BlockSpec((None, tr, W), lambda r, i, idx: (idx[r + 1], i, 0)),
                      pl.BlockSpec((None, tr, W), lambda r, i, idx: (r + 1, i, 0))],
            out_specs=pl.BlockSpec((None, tr, W), lambda r, i, idx: (r, i, 0))),
        out_shape=jax.ShapeDtypeStruct((N_CHIP - 1, R, W), BF16),
        compiler_params=_cp(("parallel", "parallel")))(own_idx, gbig, sib)


def _exchange_chips(send, gsmall, *, name):
    nb, R, W = send.shape
    n = N_DEV - 1

    def body(b_ref, s_ref, rb_ref, rs_ref, send_sems, recv_sems, local_sem):
        x, y, c = _coords()
        me = 4 * x + 2 * y + c
        mine = pltpu.make_async_copy(s_ref, rs_ref.at[me], local_sem)
        mine.start()
        started = []
        for k, (fx, fy, fc) in enumerate(_relations()):
            cp = pltpu.make_async_remote_copy(
                src_ref=s_ref, dst_ref=rs_ref.at[me], send_sem=send_sems.at[k], recv_sem=recv_sems.at[k],
                device_id=(x ^ fx, y ^ fy, c ^ fc), device_id_type=MESH)
            cp.start()
            started.append(cp)
        for r in range(1, N_CHIP):
            cp = pltpu.make_async_remote_copy(
                src_ref=b_ref.at[r - 1], dst_ref=rb_ref.at[r - 1], send_sem=send_sems.at[n + r - 1],
                recv_sem=recv_sems.at[n + r - 1], device_id=(x ^ (r >> 1), y ^ (r & 1), c), device_id_type=MESH)
            cp.start()
            started.append(cp)
        for k, (fx, fy, fc) in enumerate(_relations()):
            peer = 4 * (x ^ fx) + 2 * (y ^ fy) + (c ^ fc)
            pltpu.make_async_remote_copy(
                src_ref=s_ref, dst_ref=rs_ref.at[peer], send_sem=send_sems.at[k], recv_sem=recv_sems.at[k],
                device_id=(x ^ fx, y ^ fy, c ^ fc), device_id_type=MESH).wait_recv()
        for r in range(1, N_CHIP):
            pltpu.make_async_remote_copy(
                src_ref=b_ref.at[r - 1], dst_ref=rb_ref.at[r - 1], send_sem=send_sems.at[n + r - 1],
                recv_sem=recv_sems.at[n + r - 1], device_id=(x ^ (r >> 1), y ^ (r & 1), c),
                device_id_type=MESH).wait_recv()
        for cp in started:
            cp.wait_send()
        mine.wait()

    return pl.pallas_call(
        body, name=name,
        out_shape=[jax.ShapeDtypeStruct((nb, R, W), send.dtype),
                   jax.ShapeDtypeStruct((N_DEV, 1, P_SMALL), gsmall.dtype)],
        in_specs=[ANY, ANY], out_specs=[ANY, ANY],
        scratch_shapes=[pltpu.SemaphoreType.DMA((n + nb,)), pltpu.SemaphoreType.DMA((n + nb,)),
                        pltpu.SemaphoreType.DMA],
    )(send, gsmall)


def _part_specs(parts, tr, row0):
    assert row0 % tr == 0
    specs = []
    for a, n_used in parts:
        if n_used is None:
            specs.append(pl.BlockSpec((1, tr, a.shape[2]), lambda i, idx: (idx[0], row0 // tr + i, 0)))
        else:
            specs.append(pl.BlockSpec((n_used, tr, a.shape[2]), lambda i, idx: (0, row0 // tr + i, 0)))
    return specs


def _part_total(refs, parts):
    g = None
    for ref, (_, n_used) in zip(refs, parts):
        for k in range(n_used or 1):
            t = ref[k].astype(F32)
            g = t if g is None else g + t
    return g


def _sum_parts(parts, idx, row0, nrows, tr, *, name):
    W = parts[0][0].shape[2]
    assert nrows % tr == 0

    def body(idx_ref, *refs):
        refs[-1][...] = _part_total(refs[:-1], parts)

    return pl.pallas_call(
        body, name=name,
        grid_spec=pltpu.PrefetchScalarGridSpec(
            num_scalar_prefetch=1, grid=(nrows // tr,), in_specs=_part_specs(parts, tr, row0),
            out_specs=pl.BlockSpec((tr, W), lambda i, idx: (i, 0))),
        out_shape=jax.ShapeDtypeStruct((nrows, W), F32),
        compiler_params=_cp(("parallel",)))(idx, *[a for a, _ in parts])


def _adamw(parts, idx, w, m, v, tr, *, name):
    R, W = w.shape
    assert R % tr == 0
    np_ = len(parts)

    def body(idx_ref, *refs):
        w_ref, m_ref, v_ref, g_ref, d_ref, nm_ref, nv_ref = refs[np_:]
        g = _part_total(refs[:np_], parts)
        mm = ADAM_B1 * m_ref[...] + (1.0 - ADAM_B1) * g
        vv = ADAM_B2 * v_ref[...] + (1.0 - ADAM_B2) * (g * g)
        m_hat = mm / (1.0 - ADAM_B1 ** ADAM_STEP)
        v_hat = vv / (1.0 - ADAM_B2 ** ADAM_STEP)
        g_ref[...] = g
        d_ref[...] = -ADAM_LR * (m_hat / (jnp.sqrt(v_hat) + ADAM_EPS) + ADAM_WD * w_ref[...])
        nm_ref[...] = mm
        nv_ref[...] = vv

    blk = pl.BlockSpec((tr, W), lambda i, idx: (i, 0))
    return pl.pallas_call(
        body, name=name,
        grid_spec=pltpu.PrefetchScalarGridSpec(
            num_scalar_prefetch=1, grid=(R // tr,), in_specs=_part_specs(parts, tr, 0) + [blk, blk, blk],
            out_specs=[blk] * 4),
        out_shape=[jax.ShapeDtypeStruct((R, W), F32)] * 4,
        compiler_params=_cp(("parallel",)))(idx, *[a for a, _ in parts], w, m, v)


def _pack_rest(w_kv, wa, wb, wm, w_out):
    return jnp.concatenate([w_kv[0], w_out[0]] + [t[0].reshape(-1, D_MODEL) for t in (wa, wb, wm)], axis=0)


def _unpack_rest(t):
    br = lambda i: t[RO_BR + 64 * i:RO_BR + 64 * (i + 1)].reshape(1, A_WIDTH, D_MODEL // N_DEV)
    return t[None, RO_KV:RO_OUT], br(0), br(1), br(2), t[None, RO_OUT:RO_BR]


def _orig_rows(gathered, a, b):
    res = []
    while a < b:
        dev, r = divmod(a, CS)
        n = min(b - a, CS - r)
        res.append(gathered[dev, RO_IN + r:RO_IN + r + n])
        a += n
    return res


def _full_weights(gathered):
    wt = {}
    for name, ranges in SEGS.items():
        rows = [p for a, b in ranges for p in _orig_rows(gathered, a, b)]
        if SEG_PAD[name]:
            rows.append(jnp.zeros((SEG_PAD[name], D_MODEL), gathered.dtype))
        wt[name] = jnp.concatenate(rows, axis=0)
    w_kv = gathered[:, RO_KV:RO_OUT].reshape(D_MODEL, D_MODEL)
    w_out = gathered[:, RO_OUT:RO_BR].reshape(D_MODEL, D_MODEL)
    wbs = [gathered[:, RO_BR + 64 * i:RO_BR + 64 * (i + 1)].reshape(N_DEV, A_WIDTH, D_MODEL // N_DEV)
           .transpose(1, 0, 2).reshape(A_WIDTH, D_MODEL) for i in range(3)]
    return wt, w_kv, wbs, w_out


def _orig_order(dwt):
    pieces = []
    for name, ranges in SEGS.items():
        o = 0
        for a, b in ranges:
            pieces.append((a, dwt[name][o:o + b - a]))
            o += b - a
    pieces.sort(key=lambda p: p[0])
    return jnp.concatenate([p[1] for p in pieces], axis=0)


def _pack_grads(dwt, dw_kv, dwbs, dw_out):
    g_in = jnp.pad(_orig_order(dwt).reshape(N_DEV, CS, D_MODEL), ((0, 0), (0, IN_ROWS - CS), (0, 0)))
    br = [t.reshape(A_WIDTH, N_DEV, D_MODEL // N_DEV).transpose(1, 0, 2).reshape(N_DEV, -1, D_MODEL) for t in dwbs]
    return jnp.concatenate([dw_kv.reshape(N_DEV, -1, D_MODEL), dw_out.reshape(N_DEV, -1, D_MODEL)] + br + [g_in],
                           axis=1)


def kernel(x, mem, positions, norm_pre_g, norm_post_g, norm_mem_g, w_in, b_forget, b_merge, w_mem_kv, w_branch_a, w_branch_b, w_branch_m, w_out, loss_target, m_norm_pre_g, m_norm_post_g, m_norm_mem_g, m_w_in, m_b_forget, m_b_merge, m_w_mem_kv, m_w_branch_a, m_w_branch_b, m_w_branch_m, m_w_out, v_norm_pre_g, v_norm_post_g, v_norm_mem_g, v_w_in, v_b_forget, v_b_merge, v_w_mem_kv, v_w_branch_a, v_w_branch_b, v_w_branch_m, v_w_out):
    w_rest = _pack_rest(w_mem_kv, w_branch_a, w_branch_b, w_branch_m, w_out)
    shard = jnp.concatenate([w_rest.astype(BF16), w_in[0].T.astype(BF16),
                             jnp.zeros((IN_ROWS - CS, D_MODEL), BF16)], axis=0)
    gathered = _all_gather(shard, name="gather_weights")
    wt, w_kv, wbs, w_o = _full_weights(gathered)

    bf_pad = jnp.pad(b_forget, ((0, 0), (0, FB_PAD - B_HEADS)))
    r = _local_step(x[0], mem[0], positions[0], loss_target[0], norm_pre_g, norm_post_g, norm_mem_g,
                    wt, bf_pad, b_merge, w_kv, wbs, w_o)

    gbig = _pack_grads(r["dwt"], r["dw_kv"], r["dwbs"], r["dw_out"])
    gsmall = jnp.concatenate([r["dg_pre"], r["dg_post"], r["dg_mem"], r["db_merge"],
                              r["db_forget"][:, :LANES], r["loss"]], axis=1)
    own_idx = _own_slabs()
    sib = _exchange_pair(gbig, name="exchange_pair")
    send = _pair_sum(gbig, sib, own_idx, 208, name="pair_sum")
    recv, rsmall = _exchange_chips(send, gsmall, name="exchange_chips")
    parts = [(gbig, None), (sib, 1), (recv, N_CHIP - 1)]

    m_rest = _pack_rest(m_w_mem_kv, m_w_branch_a, m_w_branch_b, m_w_branch_m, m_w_out)
    v_rest = _pack_rest(v_w_mem_kv, v_w_branch_a, v_w_branch_b, v_w_branch_m, v_w_out)
    outs_rest = [_unpack_rest(t) for t in _adamw(parts, own_idx, w_rest, m_rest, v_rest, 64, name="adamw_rest")]
    g_in = _sum_parts(parts, own_idx, RO_IN, IN_ROWS, 16, name="sum_w_in")[:CS].T
    outs_in = _adamw([(g_in[None], 1)], own_idx, w_in[0], m_w_in[0], v_w_in[0], 128, name="adamw_w_in")

    def small_vec(a, b, c, d, e):
        z = jnp.zeros((1, LANES - B_HEADS), F32)
        return jnp.concatenate([a, b, c, d, e, z, jnp.zeros((1, LANES), F32)], axis=1)

    outs_small = _adamw([(rsmall, N_DEV)], own_idx, small_vec(norm_pre_g, norm_post_g, norm_mem_g, b_merge, b_forget),
                        small_vec(m_norm_pre_g, m_norm_post_g, m_norm_mem_g, m_b_merge, m_b_forget),
                        small_vec(v_norm_pre_g, v_norm_post_g, v_norm_mem_g, v_b_merge, v_b_forget),
                        1, name="adamw_small")

    def small_parts(t):
        return [t[:, O_GPRE:O_GPRE + D_MODEL], t[:, O_GPOST:O_GPOST + D_MODEL], t[:, O_GMEM:O_GMEM + D_MODEL],
                t[:, O_BF:O_BF + B_HEADS], t[:, O_BM:O_BM + 3 * D_MODEL]]

    loss = outs_small[0][0, O_LOSS]
    result = [loss, r["grad_x"][None]]
    for rest, w_i, small in zip(outs_rest, outs_in, outs_small):
        gp, gq, gm, bf, bm = small_parts(small)
        w_k, w_a, w_b, w_m, w_ot = rest
        result += [gp, gq, gm, w_i[None], bf, bm, w_k, w_a, w_b, w_m, w_ot]
    return tuple(result)
```

```python
import jax
import jax.numpy as jnp
from jax import lax
from jax.experimental import pallas as pl
from jax.experimental.pallas import tpu as pltpu

F32 = jnp.float32
BF16 = jnp.bfloat16

N_DEV = 8
D_MODEL = 1024
N_MEM = 256
EPS = 1e-6
NEG = -1e30
ROPE_THETA = 500000.0
DIL = (1, 4, 16)
A_HEADS = 4
HEAD = 128
A_WIDTH = 512
B_HEADS = 8
B_HEAD = 64
M_HEADS = 4
ROT = 32
IN_COLS = 11272
FB_PAD = 256

SEGS = {
    "A0": ((0, 512), (1536, 2048), (3072, 3584)),
    "A1": ((512, 1024), (2048, 2560), (3584, 4096)),
    "A2": ((1024, 1536), (2560, 3072), (4096, 4608)),
    "B": ((5120, 6656),),
    "R": ((4608, 5120), (6664, 7176), (7176, 7688), (7688, 8200), (8200, 11272), (6656, 6664)),
}
SEG_PAD = {"A0": 0, "A1": 0, "A2": 0, "B": 0, "R": FB_PAD - B_HEADS}
R_ZA, R_ZB, R_QM, R_ZM, R_GL, R_FB = 0, 512, 1024, 1536, 2048, 5120
NR = R_FB + FB_PAD

ADAM_LR, ADAM_B1, ADAM_B2, ADAM_EPS, ADAM_WD, ADAM_STEP = 0.001, 0.9, 0.999, 1e-08, 0.01, 10

LANES = 128
VMEM_LIMIT = 56 * 1024 * 1024

CS = IN_COLS // N_DEV
RO_KV, RO_OUT, RO_BR, RO_IN = 0, 128, 256, 448
IN_ROWS = 1424
ROWS = RO_IN + IN_ROWS
O_GPRE, O_GPOST, O_GMEM, O_BM, O_BF, O_LOSS = 0, 1024, 2048, 3072, 6144, 6272
P_SMALL = 6400


def _cp(sem=None):
    return pltpu.CompilerParams(dimension_semantics=sem, vmem_limit_bytes=VMEM_LIMIT)


def _dot(a, b):
    return jnp.dot(a, b, preferred_element_type=F32)


def _dot_nt(a, b):
    return lax.dot_general(a, b, (((1,), (1,)), ((), ())), preferred_element_type=F32)


def _sigmoid(z):
    return 1.0 / (1.0 + jnp.exp(-z))


def _mm(a, b, *, name, at=False, bt=False, out_dtype=F32, tm=1024, tn=1024, tk=None, comm=None):
    assert not (at and bt)
    K, M = a.shape if at else a.shape[::-1]
    N = b.shape[0] if bt else b.shape[1]
    tm, tn = min(tm, M), min(tn, N)
    tk = K if tk is None else min(tk, K)
    assert M % tm == 0 and N % tn == 0 and K % tk == 0
    nk = K // tk
    grid = (M // tm, N // tn, nk)
    n_in = len(comm["inputs"]) if comm else 0
    n_out = len(comm["out_shape"]) if comm else 0

    def body(a_ref, b_ref, *rest):
        c_in, o_ref, c_out = rest[:n_in], rest[n_in], rest[n_in + 1:n_in + 1 + n_out]
        acc_ref, sems = rest[n_in + 1 + n_out], rest[n_in + 2 + n_out:]
        if comm:
            step = (pl.program_id(0) * grid[1] + pl.program_id(1)) * grid[2] + pl.program_id(2)

            @pl.when(step == 0)
            def _():
                comm["start"](*c_in, *c_out, *sems)

        av = a_ref[...].astype(BF16)
        bv = b_ref[...].astype(BF16)
        if at:
            p = lax.dot_general(av, bv, (((0,), (0,)), ((), ())), preferred_element_type=F32)
        else:
            p = _dot_nt(av, bv) if bt else _dot(av, bv)
        if nk == 1:
            o_ref[...] = p.astype(out_dtype)
        else:
            k = pl.program_id(2)

            @pl.when(k == 0)
            def _():
                acc_ref[...] = p

            @pl.when(k > 0)
            def _():
                acc_ref[...] += p

            @pl.when(k == nk - 1)
            def _():
                o_ref[...] = acc_ref[...].astype(out_dtype)

        if comm:
            @pl.when(step == grid[0] * grid[1] * grid[2] - 1)
            def _():
                comm["wait"](*c_in, *c_out, *sems)

    b_spec = (pl.BlockSpec((tn, tk), lambda i, j, k: (j, k)) if bt
              else pl.BlockSpec((tk, tn), lambda i, j, k: (k, j)))
    a_spec = (pl.BlockSpec((tk, tm), lambda i, j, k: (k, i)) if at
              else pl.BlockSpec((tm, tk), lambda i, j, k: (i, k)))
    out_spec = pl.BlockSpec((tm, tn), lambda i, j, k: (i, j))
    out_shape = jax.ShapeDtypeStruct((M, N), out_dtype)
    acc = pltpu.VMEM((tm, tn) if nk > 1 else (8, LANES), F32)
    if not comm:
        return pl.pallas_call(
            body, name=name, grid=grid, in_specs=[a_spec, b_spec], out_specs=out_spec, out_shape=out_shape,
            scratch_shapes=[acc], compiler_params=_cp(("parallel", "parallel", "arbitrary")))(a, b)
    return pl.pallas_call(
        body, name=name, grid=grid, in_specs=[a_spec, b_spec] + [ANY] * n_in,
        out_specs=[out_spec] + [ANY] * n_out, out_shape=[out_shape] + comm["out_shape"],
        scratch_shapes=[acc] + comm["sems"],
        compiler_params=_cp(("arbitrary", "arbitrary", "arbitrary")))(a, b, *comm["inputs"])


def _rms_fwd(x, g, *, name):
    S, D = x.shape
    tm = min(512, S)

    def body(x_ref, g_ref, o_ref):
        xv = x_ref[...]
        r = lax.rsqrt(jnp.mean(xv * xv, axis=-1, keepdims=True) + EPS)
        o_ref[...] = (xv * r * g_ref[...]).astype(BF16)

    return pl.pallas_call(
        body, name=name, grid=(S // tm,),
        in_specs=[pl.BlockSpec((tm, D), lambda i: (i, 0)), pl.BlockSpec((1, D), lambda i: (0, 0))],
        out_specs=pl.BlockSpec((tm, D), lambda i: (i, 0)),
        out_shape=jax.ShapeDtypeStruct((S, D), BF16),
        compiler_params=_cp(("parallel",)),
    )(x, g)


def _rms_bwd(x, g, dh, dy, *, name):
    S, D = x.shape
    tm = min(512, S)
    want_dx = dy is not None

    def body(*refs):
        if want_dx:
            x_ref, g_ref, dh_ref, dy_ref, dx_ref, dg_ref = refs
        else:
            x_ref, g_ref, dh_ref, dg_ref = refs
        i = pl.program_id(0)
        xv = x_ref[...]
        r = lax.rsqrt(jnp.mean(xv * xv, axis=-1, keepdims=True) + EPS)
        xh = xv * r
        dhv = dh_ref[...]
        part = jnp.sum(dhv * xh, axis=0, keepdims=True)

        @pl.when(i == 0)
        def _():
            dg_ref[...] = part

        @pl.when(i > 0)
        def _():
            dg_ref[...] += part

        if want_dx:
            dxh = dhv * g_ref[...]
            dx_ref[...] = dy_ref[...] + r * (dxh - xh * jnp.mean(dxh * xh, axis=-1, keepdims=True))

    row = pl.BlockSpec((tm, D), lambda i: (i, 0))
    vec = pl.BlockSpec((1, D), lambda i: (0, 0))
    if want_dx:
        return pl.pallas_call(
            body, name=name, grid=(S // tm,), in_specs=[row, vec, row, row], out_specs=[row, vec],
            out_shape=[jax.ShapeDtypeStruct((S, D), F32), jax.ShapeDtypeStruct((1, D), F32)],
            compiler_params=_cp(("arbitrary",)))(x, g, dh, dy)
    return pl.pallas_call(
        body, name=name, grid=(S // tm,), in_specs=[row, vec, row], out_specs=vec,
        out_shape=jax.ShapeDtypeStruct((1, D), F32),
        compiler_params=_cp(("arbitrary",)))(x, g, dh)


def _post(x, out, tgt, g, *, name):
    S, D = x.shape
    tm = min(512, S)

    def body(x_ref, o_ref, t_ref, g_ref, dy_ref, do_ref, dg_ref, loss_ref):
        i = pl.program_id(0)
        ov = o_ref[...]
        r = lax.rsqrt(jnp.mean(ov * ov, axis=-1, keepdims=True) + EPS)
        n = ov * r
        gv = g_ref[...]
        e = (x_ref[...] + n * gv) - t_ref[...]
        lpart = 0.5 * jnp.sum(jnp.mean(e * e, axis=-1, keepdims=True), axis=0, keepdims=True)
        dy = e * (1.0 / D)
        dy_ref[...] = dy
        dn = dy * gv
        do_ref[...] = (r * (dn - n * jnp.mean(dn * n, axis=-1, keepdims=True))).astype(BF16)
        gpart = jnp.sum(dy * n, axis=0, keepdims=True)
        lrow = jnp.broadcast_to(lpart, (1, LANES))

        @pl.when(i == 0)
        def _():
            dg_ref[...] = gpart
            loss_ref[...] = lrow

        @pl.when(i > 0)
        def _():
            dg_ref[...] += gpart
            loss_ref[...] += lrow

    row = pl.BlockSpec((tm, D), lambda i: (i, 0))
    vec = pl.BlockSpec((1, D), lambda i: (0, 0))
    return pl.pallas_call(
        body, name=name, grid=(S // tm,), in_specs=[row, row, row, vec],
        out_specs=[row, row, vec, pl.BlockSpec((1, LANES), lambda i: (0, 0))],
        out_shape=[jax.ShapeDtypeStruct((S, D), F32), jax.ShapeDtypeStruct((S, D), BF16),
                   jax.ShapeDtypeStruct((1, D), F32), jax.ShapeDtypeStruct((1, LANES), F32)],
        compiler_params=_cp(("arbitrary",)))(x, out, tgt, g)


def _to_classes(t, d):
    if d == 1:
        return t
    S, C = t.shape
    return t.reshape(S // d, d, C).transpose(1, 0, 2).reshape(S, C)


def _from_classes(t, d):
    if d == 1:
        return t
    S, C = t.shape
    return t.reshape(d, S // d, C).transpose(1, 0, 2).reshape(S, C)


def _rope(x, c, s1, s2):
    return x * c + pltpu.roll(x, LANES - ROT // 2, 1) * s1 + pltpu.roll(x, ROT // 2, 1) * s2


def _unrope(d, c, s1, s2):
    return d * c + pltpu.roll(d * s1, ROT // 2, 1) + pltpu.roll(d * s2, LANES - ROT // 2, 1)


def _a_band(qb):
    r = lax.broadcasted_iota(jnp.int32, (qb, qb + HEAD), 0)
    c = lax.broadcasted_iota(jnp.int32, (qb, qb + HEAD), 1)
    return jnp.logical_and(c >= r, c <= r + HEAD)


def _a_first_ok(qb, n):
    c = lax.broadcasted_iota(jnp.int32, (qb, qb + HEAD), 1)
    return jnp.logical_or(c >= HEAD, n > 0)


def _a_last_ok(qb, has_next):
    c = lax.broadcasted_iota(jnp.int32, (qb, qb + HEAD), 1)
    return jnp.logical_or(c < qb, has_next)


A_SCALE = HEAD ** -0.5


def _a_geometry(S, g):
    d = DIL[g]
    L = S // d
    TQ = min(512, L)
    return d, L, TQ, TQ // HEAD, L // TQ, L // HEAD


def _proj_rope(h, w, tabs, *, name):
    S, D = h.shape
    tm = min(512, S)

    def body(h_ref, w_ref, c_ref, s1_ref, s2_ref, o_ref):
        tc = (c_ref[...], s1_ref[...], s2_ref[...])
        u = _dot_nt(h_ref[...], w_ref[...])
        for j in range(3 * A_HEADS):
            sl = slice(j * HEAD, (j + 1) * HEAD)
            o_ref[:, sl] = (_rope(u[:, sl], *tc) if j < 2 * A_HEADS else u[:, sl]).astype(BF16)

    tab = pl.BlockSpec((tm, LANES), lambda i: (i, 0))
    return pl.pallas_call(
        body, name=name, grid=(S // tm,),
        in_specs=[pl.BlockSpec((tm, D), lambda i: (i, 0)), pl.BlockSpec((3 * A_WIDTH, D), lambda i: (0, 0)),
                  tab, tab, tab],
        out_specs=pl.BlockSpec((tm, 3 * A_WIDTH), lambda i: (i, 0)),
        out_shape=jax.ShapeDtypeStruct((S, 3 * A_WIDTH), BF16),
        compiler_params=_cp(("parallel",)))(h, w, *tabs)


def _attn_a_fwd(qkv, g, *, name):
    S = qkv.shape[0]
    d, L, TQ, nsub, nb, nblk = _a_geometry(S, g)

    def body(q_ref, kc_ref, kp_ref, vc_ref, vp_ref, o_ref, l_ref):
        n = pl.program_id(1)
        QB = min(2 * HEAD, TQ)
        band = _a_band(QB)
        first = jnp.logical_and(band, _a_first_ok(QB, n))
        for h in range(A_HEADS):
            hs = slice(h * HEAD, (h + 1) * HEAD)
            for hh in range(TQ // QB):
                sl = slice(hh * QB, (hh + 1) * QB)
                pv = slice(hh * QB - HEAD, hh * QB)
                kcat = jnp.concatenate([kp_ref[:, hs] if hh == 0 else kc_ref[pv, hs], kc_ref[sl, hs]], axis=0)
                vcat = jnp.concatenate([vp_ref[:, hs] if hh == 0 else vc_ref[pv, hs], vc_ref[sl, hs]], axis=0)
                s = jnp.where(first if hh == 0 else band, _dot_nt(q_ref[sl, hs], kcat) * A_SCALE, NEG)
                m = jnp.max(s, axis=-1, keepdims=True)
                p = jnp.exp(s - m)
                den = jnp.sum(p, axis=-1, keepdims=True)
                o_ref[sl, hs] = _dot(p.astype(BF16), vcat) / den
                l_ref[sl, hs] = jnp.broadcast_to(m + jnp.log(den), (QB, HEAD))

    rcur = lambda r, n: r * nb + n
    rprv = lambda r, n: r * nblk + jnp.maximum(n * nsub - 1, 0)
    cur = lambda off: pl.BlockSpec((TQ, A_WIDTH), lambda r, n: (rcur(r, n), off))
    prv = lambda off: pl.BlockSpec((HEAD, A_WIDTH), lambda r, n: (rprv(r, n), off))
    out = pl.BlockSpec((TQ, A_WIDTH), lambda r, n: (rcur(r, n), 0))
    return pl.pallas_call(
        body, name=name, grid=(d, nb),
        in_specs=[cur(0), cur(1), prv(1), cur(2), prv(2)],
        out_specs=[out, out],
        out_shape=[jax.ShapeDtypeStruct((S, A_WIDTH), F32)] * 2,
        compiler_params=_cp(("parallel", "parallel")),
    )(qkv, qkv, qkv, qkv, qkv)


def _attn_a_dq(qkv, tabs, g, do, lse, adj, *, name):
    S = qkv.shape[0]
    d, L, TQ, nsub, nb, nblk = _a_geometry(S, g)

    def body(q_ref, kc_ref, kp_ref, vc_ref, vp_ref, do_ref, l_ref, adj_ref, c_ref, s1_ref, s2_ref, dq_ref):
        n = pl.program_id(1)
        QB = min(2 * HEAD, TQ)
        band = _a_band(QB)
        first = jnp.logical_and(band, _a_first_ok(QB, n))
        for h in range(A_HEADS):
            hs = slice(h * HEAD, (h + 1) * HEAD)
            for hh in range(TQ // QB):
                sl = slice(hh * QB, (hh + 1) * QB)
                pv = slice(hh * QB - HEAD, hh * QB)
                kcat = jnp.concatenate([kp_ref[:, hs] if hh == 0 else kc_ref[pv, hs], kc_ref[sl, hs]], axis=0)
                vcat = jnp.concatenate([vp_ref[:, hs] if hh == 0 else vc_ref[pv, hs], vc_ref[sl, hs]], axis=0)
                s = jnp.where(first if hh == 0 else band, _dot_nt(q_ref[sl, hs], kcat) * A_SCALE, NEG)
                p = jnp.exp(s - l_ref[sl, hs][:, :1])
                ds = p * (_dot_nt(do_ref[sl, hs], vcat) + adj_ref[sl, hs][:, :1])
                dq = _dot(ds.astype(BF16), kcat) * A_SCALE
                dq_ref[sl, hs] = _unrope(dq, c_ref[sl, :], s1_ref[sl, :], s2_ref[sl, :]).astype(BF16)

    rcur = lambda r, n: r * nb + n
    rprv = lambda r, n: r * nblk + jnp.maximum(n * nsub - 1, 0)
    cur = lambda off: pl.BlockSpec((TQ, A_WIDTH), lambda r, n: (rcur(r, n), off))
    prv = lambda off: pl.BlockSpec((HEAD, A_WIDTH), lambda r, n: (rprv(r, n), off))
    tcur = pl.BlockSpec((TQ, LANES), lambda r, n: (rcur(r, n), 0))
    blk = cur(0)
    return pl.pallas_call(
        body, name=name, grid=(d, nb),
        in_specs=[cur(0), cur(1), prv(1), cur(2), prv(2), blk, blk, blk, tcur, tcur, tcur],
        out_specs=blk,
        out_shape=jax.ShapeDtypeStruct((S, A_WIDTH), BF16),
        compiler_params=_cp(("parallel", "parallel")),
    )(qkv, qkv, qkv, qkv, qkv, do, lse, adj, *tabs)


def _attn_a_dkv(qkv, tabs, g, do, lse, adj, *, name):
    S = qkv.shape[0]
    d, L, TQ, nsub, nb, nblk = _a_geometry(S, g)

    def body(qc_ref, qn_ref, kc_ref, vc_ref, doc_ref, don_ref, lc_ref, ln_ref, ac_ref, an_ref,
             c_ref, s1_ref, s2_ref, dk_ref, dv_ref):
        n = pl.program_id(1)
        QB = min(2 * HEAD, TQ)
        nh = TQ // QB
        band = _a_band(QB)
        end = jnp.logical_and(band, _a_last_ok(QB, n < nb - 1))
        for h in range(A_HEADS):
            hs = slice(h * HEAD, (h + 1) * HEAD)
            for kh in range(nh):
                sl = slice(kh * QB, (kh + 1) * QB)
                nx = slice((kh + 1) * QB, (kh + 1) * QB + HEAD)
                last = kh == nh - 1
                cat = lambda cur, nxt: jnp.concatenate([cur[sl, hs], nxt[:, hs] if last else cur[nx, hs]], axis=0)
                qcat = cat(qc_ref, qn_ref)
                docat = cat(doc_ref, don_ref)
                lt = cat(lc_ref, ln_ref).T[:1, :]
                at = cat(ac_ref, an_ref).T[:1, :]
                st = jnp.where(end if last else band, _dot_nt(kc_ref[sl, hs], qcat) * A_SCALE, NEG)
                pt = jnp.exp(st - lt)
                dv_ref[sl, hs] = _dot(pt.astype(BF16), docat).astype(BF16)
                dst = pt * (_dot_nt(vc_ref[sl, hs], docat) + at)
                dk = _dot(dst.astype(BF16), qcat) * A_SCALE
                dk_ref[sl, hs] = _unrope(dk, c_ref[sl, :], s1_ref[sl, :], s2_ref[sl, :]).astype(BF16)

    rcur = lambda r, n: r * nb + n
    rnxt = lambda r, n: r * nblk + jnp.minimum((n + 1) * nsub, nblk - 1)
    cur = lambda off: pl.BlockSpec((TQ, A_WIDTH), lambda r, n: (rcur(r, n), off))
    nxu = lambda off: pl.BlockSpec((HEAD, A_WIDTH), lambda r, n: (rnxt(r, n), off))
    tcur = pl.BlockSpec((TQ, LANES), lambda r, n: (rcur(r, n), 0))
    blk, bnx = cur(0), nxu(0)
    return pl.pallas_call(
        body, name=name, grid=(d, nb),
        in_specs=[cur(0), nxu(0), cur(1), cur(2), blk, bnx, blk, bnx, blk, bnx, tcur, tcur, tcur],
        out_specs=[blk, blk],
        out_shape=[jax.ShapeDtypeStruct((S, A_WIDTH), BF16)] * 2,
        compiler_params=_cp(("parallel", "parallel")),
    )(qkv, qkv, qkv, qkv, do, do, lse, lse, adj, adj, *tabs)


def _silu_parts(z):
    sg = _sigmoid(z)
    return z * sg, sg * (1.0 + z * (1.0 - sg))


def _merge_a_fwd(os_, ls_, ur, *, name):
    S = ur.shape[0]
    tm = min(512, S)

    def body(o0, o1, o2, l0, l1, l2, z_ref, y_ref):
        ls = [l0[...], l1[...], l2[...]]
        mx = jnp.maximum(jnp.maximum(ls[0], ls[1]), ls[2])
        es = [jnp.exp(l - mx) for l in ls]
        den = es[0] + es[1] + es[2]
        y = (es[0] / den) * o0[...] + (es[1] / den) * o1[...] + (es[2] / den) * o2[...]
        y_ref[...] = (y * _silu_parts(z_ref[...])[0]).astype(BF16)

    blk = pl.BlockSpec((tm, A_WIDTH), lambda i: (i, 0))
    return pl.pallas_call(
        body, name=name, grid=(S // tm,),
        in_specs=[blk] * 6 + [pl.BlockSpec((tm, A_WIDTH), lambda i: (i, R_ZA // A_WIDTH))],
        out_specs=blk, out_shape=jax.ShapeDtypeStruct((S, A_WIDTH), BF16),
        compiler_params=_cp(("parallel",)))(*os_, *ls_, ur)


def _merge_a_bwd(os_, ls_, ur, dya, *, name):
    S = ur.shape[0]
    tm = min(256, S)

    def body(o0, o1, o2, l0, l1, l2, z_ref, dy_ref, d0, d1, d2, a0, a1, a2, dz_ref):
        ls = [l0[...], l1[...], l2[...]]
        ov = [o0[...], o1[...], o2[...]]
        mx = jnp.maximum(jnp.maximum(ls[0], ls[1]), ls[2])
        es = [jnp.exp(l - mx) for l in ls]
        den = es[0] + es[1] + es[2]
        ws = [e / den for e in es]
        y = ws[0] * ov[0] + ws[1] * ov[1] + ws[2] * ov[2]
        sz, dsz = _silu_parts(z_ref[...])
        dyv = dy_ref[...]
        dz_ref[...] = (dyv * y * dsz).astype(BF16)
        dyp = dyv * sz
        for h in range(A_HEADS):
            sl = slice(h * HEAD, (h + 1) * HEAD)
            t = jnp.zeros((tm, 1), F32)
            for gi in range(3):
                t = t + ws[gi][:, sl][:, :1] * jnp.sum(dyp[:, sl] * ov[gi][:, sl], axis=-1, keepdims=True)
            for gi, (dref, aref) in enumerate(((d0, a0), (d1, a1), (d2, a2))):
                wg = ws[gi][:, sl]
                dref[:, sl] = (wg * dyp[:, sl]).astype(BF16)
                aref[:, sl] = -wg * t

    blk = pl.BlockSpec((tm, A_WIDTH), lambda i: (i, 0))
    outs = pl.pallas_call(
        body, name=name, grid=(S // tm,),
        in_specs=[blk] * 6 + [pl.BlockSpec((tm, A_WIDTH), lambda i: (i, R_ZA // A_WIDTH)), blk],
        out_specs=[blk] * 7,
        out_shape=[jax.ShapeDtypeStruct((S, A_WIDTH), BF16)] * 3
        + [jax.ShapeDtypeStruct((S, A_WIDTH), F32)] * 3 + [jax.ShapeDtypeStruct((S, A_WIDTH), BF16)],
        compiler_params=_cp(("parallel",)))(*os_, *ls_, ur, dya)
    return outs[0:3], outs[3:6], outs[6]


def _logf(ur, bf_pad, *, name):
    S = ur.shape[0]
    tm = min(1024, S)

    def body(u_ref, b_ref, o_ref):
        z = u_ref[...] + b_ref[...]
        o_ref[...] = jnp.minimum(z, 0.0) - jnp.log(1.0 + jnp.exp(-jnp.abs(z)))

    return pl.pallas_call(
        body, name=name, grid=(S // tm,),
        in_specs=[pl.BlockSpec((tm, FB_PAD), lambda i: (i, R_FB // FB_PAD)),
                  pl.BlockSpec((1, FB_PAD), lambda i: (0, 0))],
        out_specs=pl.BlockSpec((tm, FB_PAD), lambda i: (i, 0)),
        out_shape=jax.ShapeDtypeStruct((S, FB_PAD), F32),
        compiler_params=_cp(("parallel",)))(ur, bf_pad)


def _cumsum_lanes(x, reverse, *, name):
    nt, H, _ = x.shape

    def body(x_ref, o_ref):
        lane = lax.broadcasted_iota(jnp.int32, (H, LANES), 1)

        def tile(t, carry):
            tt = nt - 1 - t if reverse else t
            v = x_ref[tt]
            k = 1
            while k < LANES:
                if reverse:
                    v = v + jnp.where(lane < LANES - k, pltpu.roll(v, LANES - k, 1), 0.0)
                else:
                    v = v + jnp.where(lane >= k, pltpu.roll(v, k, 1), 0.0)
                k *= 2
            v = v + carry
            o_ref[tt] = v
            edge = v[:, :1] if reverse else v[:, LANES - 1:]
            return jnp.broadcast_to(edge, (H, LANES))

        lax.fori_loop(0, nt, tile, jnp.zeros((H, LANES), F32))

    return pl.pallas_call(
        body, name=name, out_shape=jax.ShapeDtypeStruct((nt, H, LANES), F32),
        in_specs=[pl.BlockSpec(memory_space=pltpu.VMEM)], out_specs=pl.BlockSpec(memory_space=pltpu.VMEM),
        compiler_params=_cp())(x)


B_SCALE = B_HEAD ** -0.5


def _pair_masks():
    lane = lax.broadcasted_iota(jnp.int32, (1, LANES), 1)
    row = lax.broadcasted_iota(jnp.int32, (LANES, 1), 0)
    return (lane < B_HEAD, lane >= B_HEAD), (row < B_HEAD, row >= B_HEAD)


def _causal_t(T):
    r = lax.broadcasted_iota(jnp.int32, (T, T), 0)
    c = lax.broadcasted_iota(jnp.int32, (T, T), 1)
    return r <= c


def _zero_other(x, keep):
    return jnp.where(keep, x, jnp.zeros_like(x))


def _fox_aug(ub, ckb, *, name):
    S = ub.shape[0]
    T = min(512, S)

    def body(q_ref, k_ref, c_ref, qa_ref, ka_ref):
        lane = lax.broadcasted_iota(jnp.int32, (1, LANES), 1)
        q = q_ref[...] * B_SCALE
        k = k_ref[...]
        for a in range(2):
            own = (lane < B_HEAD) if a == 0 else (lane >= B_HEAD)
            o = B_HEAD if a == 0 else 0
            c = c_ref[a]
            hi = c.astype(BF16)
            r1 = c - hi.astype(F32)
            mid = r1.astype(BF16)
            lo = (r1 - mid.astype(F32)).astype(BF16)
            pieces = (hi, mid, lo)
            one = jnp.ones((T, LANES), BF16)
            qa = jnp.where(own, q, jnp.zeros_like(q))
            ka = jnp.where(own, k, jnp.zeros_like(k))
            for t in range(3):
                qa = jnp.where(lane == o + t, pieces[t], qa)
                qa = jnp.where(lane == o + 3 + t, one, qa)
                ka = jnp.where(lane == o + t, one, ka)
                ka = jnp.where(lane == o + 3 + t, -pieces[t], ka)
            qa_ref[a] = qa
            ka_ref[a] = ka

    out = pl.BlockSpec((2, T, LANES), lambda h, i: (h, i, 0))
    return pl.pallas_call(
        body, name=name, grid=(B_HEADS // 2, S // T),
        in_specs=[pl.BlockSpec((T, LANES), lambda h, i: (i, h)), pl.BlockSpec((T, LANES), lambda h, i: (i, 4 + h)), out],
        out_specs=[out, out], out_shape=[jax.ShapeDtypeStruct((B_HEADS, S, LANES), BF16)] * 2,
        compiler_params=_cp(("parallel", "parallel")))(ub, ub, ckb)


def _fox_fwd(qaug, kaug, vt, *, name):
    S = qaug.shape[1]
    T = min(512, S)
    nq = S // T

    def body(q_ref, k_ref, vt_ref, o_ref, l_ref, m_s, l_s, acc_s):
        i = pl.program_id(1)
        _, rows = _pair_masks()
        qm = [q_ref[0], q_ref[1]]
        m_s[...] = jnp.full((2, 1, T), NEG, F32)
        l_s[...] = jnp.zeros((2, 1, T), F32)
        acc_s[...] = jnp.zeros((LANES, T), F32)

        def step(j, masked):
            off = pl.multiple_of(j * T, T)
            vtj = vt_ref[j]
            upd = jnp.zeros((LANES, T), F32)
            alphas = []
            for a in range(2):
                st = _dot_nt(k_ref[a, pl.ds(off, T), :], qm[a])
                if masked:
                    st = jnp.where(_causal_t(T), st, NEG)
                m_old = m_s[a]
                m_new = jnp.maximum(m_old, jnp.max(st, axis=0, keepdims=True))
                alpha = jnp.exp(m_old - m_new)
                pt = jnp.exp(st - m_new)
                l_s[a] = alpha * l_s[a] + jnp.sum(pt, axis=0, keepdims=True)
                m_s[a] = m_new
                upd = upd + _dot(_zero_other(vtj, rows[a]), pt.astype(BF16))
                alphas.append(alpha)
            acc_s[...] = acc_s[...] * jnp.where(rows[0], alphas[0], alphas[1]) + upd

        def loop(j, carry):
            step(j, False)
            return carry

        lax.fori_loop(0, i, loop, 0)
        step(i, True)
        o_ref[...] = (acc_s[...] / jnp.where(rows[0], l_s[0], l_s[1])).T
        l_ref[0] = m_s[0] + jnp.log(l_s[0])
        l_ref[1] = m_s[1] + jnp.log(l_s[1])

    stat = pl.BlockSpec((2, None, 1, T), lambda h, i: (h, i, 0, 0))
    return pl.pallas_call(
        body, name=name, grid=(B_HEADS // 2, nq),
        in_specs=[pl.BlockSpec((2, T, LANES), lambda h, i: (h, i, 0)),
                  pl.BlockSpec((2, S, LANES), lambda h, i: (h, 0, 0)),
                  pl.BlockSpec((nq, LANES, T), lambda h, i: (0, h, 0))],
        out_specs=[pl.BlockSpec((T, LANES), lambda h, i: (i, h)), stat],
        out_shape=[jax.ShapeDtypeStruct((S, A_WIDTH), F32), jax.ShapeDtypeStruct((B_HEADS, nq, 1, T), F32)],
        scratch_shapes=[pltpu.VMEM((2, 1, T), F32), pltpu.VMEM((2, 1, T), F32), pltpu.VMEM((LANES, T), F32)],
        compiler_params=_cp(("parallel", "parallel")),
    )(qaug, kaug, vt)


def _fox_delta(o, do, *, name):
    S = o.shape[0]
    T = min(512, S)
    nq = S // T

    def body(o_ref, do_ref, d_ref):
        _, rows = _pair_masks()
        prod_t = (do_ref[...].astype(F32) * o_ref[...]).T
        d_ref[0] = jnp.sum(_zero_other(prod_t, rows[0]), axis=0, keepdims=True)
        d_ref[1] = jnp.sum(_zero_other(prod_t, rows[1]), axis=0, keepdims=True)

    tile = pl.BlockSpec((T, LANES), lambda h, i: (i, h))
    return pl.pallas_call(
        body, name=name, grid=(B_HEADS // 2, nq), in_specs=[tile, tile],
        out_specs=pl.BlockSpec((2, None, 1, T), lambda h, i: (h, i, 0, 0)),
        out_shape=jax.ShapeDtypeStruct((B_HEADS, nq, 1, T), F32),
        compiler_params=_cp(("parallel", "parallel")))(o, do)


def _fox_bwd(ub, qaug, kaug, kt, do, lse, delta, *, name):
    S = ub.shape[0]
    T = min(512, S)
    nq = S // T

    def body(k_ref, v_ref, kt_ref, q_ref, do_ref, l_ref, dl_ref,
             dk_ref, dv_ref, dck_ref, dqt_ref, dcq_ref, dk_s, dv_s, dc_s):
        j = pl.program_id(1)
        lanes, rows = _pair_masks()
        vv = v_ref[...]
        ktj = kt_ref[...]
        km = [k_ref[0], k_ref[1]]
        ktm = [_zero_other(ktj, rows[0]), _zero_other(ktj, rows[1])]
        dk_s[...] = jnp.zeros((2, T, LANES), F32)
        dv_s[...] = jnp.zeros((T, LANES), F32)
        dc_s[...] = jnp.zeros((2, T, 1), F32)

        @pl.when(j == 0)
        def _():
            dqt_ref[...] = jnp.zeros((nq, LANES, T), F32)
            dcq_ref[...] = jnp.zeros((2, nq, 1, T), F32)

        def step(i, masked):
            off = pl.multiple_of(i * T, T)
            doi = do_ref[pl.ds(off, T), :]
            upd = jnp.zeros((LANES, T), F32)
            for a in range(2):
                qi = q_ref[a, pl.ds(off, T), :]
                st = _dot_nt(km[a], qi)
                if masked:
                    st = jnp.where(_causal_t(T), st, NEG)
                pt = jnp.exp(st - l_ref[a, i])
                doa = _zero_other(doi, lanes[a])
                dv_s[...] += _dot(pt.astype(BF16), doa)
                dst = pt * (_dot_nt(vv, doa) - dl_ref[a, i])
                dsb = dst.astype(BF16)
                dk_s[a] += _dot(dsb, qi)
                upd = upd + _dot(ktm[a], dsb)
                dc_s[a] -= jnp.sum(dst, axis=-1, keepdims=True)
                dcq_ref[a, i] += jnp.sum(dst, axis=0, keepdims=True)
            dqt_ref[i] += upd

        def loop(i, carry):
            step(i, False)
            return carry

        step(j, True)
        lax.fori_loop(j + 1, nq, loop, 0)
        dk_ref[...] = jnp.where(lanes[0], dk_s[0], dk_s[1]).astype(BF16)
        dv_ref[...] = dv_s[...].astype(BF16)
        dck_ref[...] = dc_s[...]

    rowv = pl.BlockSpec((2, nq, 1, T), lambda h, j: (h, 0, 0, 0))
    tile = pl.BlockSpec((T, LANES), lambda h, j: (j, h))
    return pl.pallas_call(
        body, name=name, grid=(B_HEADS // 2, nq),
        in_specs=[pl.BlockSpec((2, T, LANES), lambda h, j: (h, j, 0)),
                  pl.BlockSpec((T, LANES), lambda h, j: (j, 8 + h)),
                  pl.BlockSpec((None, LANES, T), lambda h, j: (j, h, 0)),
                  pl.BlockSpec((2, S, LANES), lambda h, j: (h, 0, 0)),
                  pl.BlockSpec((S, LANES), lambda h, j: (0, h)),
                  rowv, rowv],
        out_specs=[tile, tile, pl.BlockSpec((2, T, 1), lambda h, j: (h, j, 0)),
                   pl.BlockSpec((nq, LANES, T), lambda h, j: (0, h, 0)), rowv],
        out_shape=[jax.ShapeDtypeStruct((S, A_WIDTH), BF16)] * 2 + [jax.ShapeDtypeStruct((B_HEADS, S, 1), F32),
                   jax.ShapeDtypeStruct((nq, A_WIDTH, T), F32), jax.ShapeDtypeStruct((B_HEADS, nq, 1, T), F32)],
        scratch_shapes=[pltpu.VMEM((2, T, LANES), F32), pltpu.VMEM((T, LANES), F32), pltpu.VMEM((2, T, 1), F32)],
        compiler_params=_cp(("parallel", "arbitrary")),
    )(kaug, ub, kt, qaug, do, lse, delta)


def _gate_fwd(o, ur, zcol, *, name):
    S = ur.shape[0]
    tm = min(1024, S)

    def body(o_ref, z_ref, y_ref):
        y_ref[...] = (o_ref[...] * _silu_parts(z_ref[...])[0]).astype(BF16)

    blk = pl.BlockSpec((tm, A_WIDTH), lambda i: (i, 0))
    return pl.pallas_call(
        body, name=name, grid=(S // tm,),
        in_specs=[blk, pl.BlockSpec((tm, A_WIDTH), lambda i: (i, zcol // A_WIDTH))],
        out_specs=blk, out_shape=jax.ShapeDtypeStruct((S, A_WIDTH), BF16),
        compiler_params=_cp(("parallel",)))(o, ur)


def _gate_bwd(o, ur, zcol, dy, *, name):
    S = ur.shape[0]
    tm = min(1024, S)

    def body(o_ref, z_ref, dy_ref, do_ref, dz_ref):
        sz, dsz = _silu_parts(z_ref[...])
        dyv = dy_ref[...]
        do_ref[...] = (dyv * sz).astype(BF16)
        dz_ref[...] = (dyv * o_ref[...] * dsz).astype(BF16)

    blk = pl.BlockSpec((tm, A_WIDTH), lambda i: (i, 0))
    return pl.pallas_call(
        body, name=name, grid=(S // tm,),
        in_specs=[blk, pl.BlockSpec((tm, A_WIDTH), lambda i: (i, zcol // A_WIDTH)), blk],
        out_specs=[blk, blk], out_shape=[jax.ShapeDtypeStruct((S, A_WIDTH), BF16)] * 2,
        compiler_params=_cp(("parallel",)))(o, ur, dy)


def _dfb(ur, bf_pad, dlogf_pad, *, name):
    S = ur.shape[0]
    tm = min(1024, S)

    def body(u_ref, b_ref, d_ref, o_ref, s_ref):
        i = pl.program_id(0)
        dv = d_ref[...] * _sigmoid(-(u_ref[...] + b_ref[...]))
        o_ref[...] = dv.astype(BF16)
        part = jnp.sum(dv, axis=0, keepdims=True)

        @pl.when(i == 0)
        def _():
            s_ref[...] = part

        @pl.when(i > 0)
        def _():
            s_ref[...] += part

    vec = pl.BlockSpec((1, FB_PAD), lambda i: (0, 0))
    blk = pl.BlockSpec((tm, FB_PAD), lambda i: (i, 0))
    return pl.pallas_call(
        body, name=name, grid=(S // tm,),
        in_specs=[pl.BlockSpec((tm, FB_PAD), lambda i: (i, R_FB // FB_PAD)), vec, blk],
        out_specs=[blk, vec],
        out_shape=[jax.ShapeDtypeStruct((S, FB_PAD), BF16), jax.ShapeDtypeStruct((1, FB_PAD), F32)],
        compiler_params=_cp(("arbitrary",)))(ur, bf_pad, dlogf_pad)


M_SCALE = HEAD ** -0.5


def _mem_fwd(ur, mkv, *, name):
    S = ur.shape[0]
    T = min(512, S)

    def body(q_ref, z_ref, k_ref, v_ref, y_ref):
        s = _dot_nt(q_ref[...].astype(BF16), k_ref[...].astype(BF16)) * M_SCALE
        p = jnp.exp(s - jnp.max(s, axis=-1, keepdims=True))
        p = p / jnp.sum(p, axis=-1, keepdims=True)
        o = _dot(p.astype(BF16), v_ref[...].astype(BF16))
        y_ref[...] = (o * _silu_parts(z_ref[...])[0]).astype(BF16)

    return pl.pallas_call(
        body, name=name, grid=(S // T, M_HEADS),
        in_specs=[pl.BlockSpec((T, HEAD), lambda i, h: (i, R_QM // HEAD + h)),
                  pl.BlockSpec((T, HEAD), lambda i, h: (i, R_ZM // HEAD + h)),
                  pl.BlockSpec((N_MEM, HEAD), lambda i, h: (0, h)),
                  pl.BlockSpec((N_MEM, HEAD), lambda i, h: (0, M_HEADS + h))],
        out_specs=pl.BlockSpec((T, HEAD), lambda i, h: (i, h)),
        out_shape=jax.ShapeDtypeStruct((S, A_WIDTH), BF16),
        compiler_params=_cp(("parallel", "parallel")))(ur, ur, mkv, mkv)


def _mem_bwd(ur, mkv, dy, *, name):
    S = ur.shape[0]
    T = min(512, S)

    def body(q_ref, z_ref, k_ref, v_ref, dy_ref, dq_ref, dz_ref, dk_ref, dv_ref):
        i = pl.program_id(1)
        qv = q_ref[...].astype(BF16)
        kv = k_ref[...].astype(BF16)
        vv = v_ref[...].astype(BF16)
        s = _dot_nt(qv, kv) * M_SCALE
        p = jnp.exp(s - jnp.max(s, axis=-1, keepdims=True))
        p = p / jnp.sum(p, axis=-1, keepdims=True)
        o = _dot(p.astype(BF16), vv)
        sz, dsz = _silu_parts(z_ref[...])
        dyv = dy_ref[...]
        dz_ref[...] = (dyv * o * dsz).astype(BF16)
        dov = (dyv * sz).astype(BF16)
        dp = _dot_nt(dov, vv)
        ds = p * (dp - jnp.sum(p * dp, axis=-1, keepdims=True))
        dq_ref[...] = (_dot(ds.astype(BF16), kv) * M_SCALE).astype(BF16)
        dvp = _dot(p.T.astype(BF16), dov)
        dkp = _dot(ds.T.astype(BF16), qv) * M_SCALE

        @pl.when(i == 0)
        def _():
            dk_ref[...] = dkp
            dv_ref[...] = dvp

        @pl.when(i > 0)
        def _():
            dk_ref[...] += dkp
            dv_ref[...] += dvp

    tile = pl.BlockSpec((T, HEAD), lambda h, i: (i, h))
    acc = pl.BlockSpec((N_MEM, HEAD), lambda h, i: (0, h))
    return pl.pallas_call(
        body, name=name, grid=(M_HEADS, S // T),
        in_specs=[pl.BlockSpec((T, HEAD), lambda h, i: (i, R_QM // HEAD + h)),
                  pl.BlockSpec((T, HEAD), lambda h, i: (i, R_ZM // HEAD + h)),
                  pl.BlockSpec((N_MEM, HEAD), lambda h, i: (0, h)),
                  pl.BlockSpec((N_MEM, HEAD), lambda h, i: (0, M_HEADS + h)), tile],
        out_specs=[tile, tile, acc, acc],
        out_shape=[jax.ShapeDtypeStruct((S, A_WIDTH), BF16)] * 2
        + [jax.ShapeDtypeStruct((N_MEM, A_WIDTH), F32)] * 2,
        compiler_params=_cp(("parallel", "arbitrary")))(ur, ur, mkv, mkv, dy)


def _branch_fwd(ys, wbs, ur, b_merge, *, name):
    S = ur.shape[0]
    tm, tn = min(512, S), 512
    nj = D_MODEL // tn

    def body(ya, yb, ym, wa, wb, wm, g0, g1, g2, b0, b1, b2, mg_ref, p_ref):
        acc = jnp.zeros((tm, tn), F32)
        for i, (y, w, gr, br) in enumerate(((ya, wa, g0, b0), (yb, wb, g1, b1), (ym, wm, g2, b2))):
            pr = _dot(y[...], w[...])
            p_ref[i] = pr
            acc = acc + _sigmoid(gr[...] + br[...]) * pr
        mg_ref[...] = acc.astype(BF16)

    yspec = pl.BlockSpec((tm, A_WIDTH), lambda i, j: (i, 0))
    wspec = pl.BlockSpec((A_WIDTH, tn), lambda i, j: (0, j))
    gspec = lambda b: pl.BlockSpec((tm, tn), lambda i, j: (i, (R_GL + b * D_MODEL) // tn + j))
    bspec = lambda b: pl.BlockSpec((1, tn), lambda i, j: (0, b * nj + j))
    return pl.pallas_call(
        body, name=name, grid=(S // tm, nj),
        in_specs=[yspec] * 3 + [wspec] * 3 + [gspec(0), gspec(1), gspec(2), bspec(0), bspec(1), bspec(2)],
        out_specs=[pl.BlockSpec((tm, tn), lambda i, j: (i, j)),
                   pl.BlockSpec((3, tm, tn), lambda i, j: (0, i, j))],
        out_shape=[jax.ShapeDtypeStruct((S, D_MODEL), BF16), jax.ShapeDtypeStruct((3, S, D_MODEL), F32)],
        compiler_params=_cp(("parallel", "parallel")))(*ys, *wbs, ur, ur, ur, b_merge, b_merge, b_merge)


def _branch_bwd(dm, prods, ur, b_merge, *, name):
    S = ur.shape[0]
    tm = min(256, S)

    def body(dm_ref, p_ref, g0, g1, g2, b_ref, dp_ref, dgl_ref, db_ref):
        i = pl.program_id(0)
        dmv = dm_ref[...]
        parts = []
        for b, gr in enumerate((g0, g1, g2)):
            sl = slice(b * D_MODEL, (b + 1) * D_MODEL)
            gt = _sigmoid(gr[...] + b_ref[:, sl])
            dp_ref[b] = (dmv * gt).astype(BF16)
            dgl = dmv * p_ref[b] * gt * (1.0 - gt)
            dgl_ref[:, sl] = dgl.astype(BF16)
            parts.append(jnp.sum(dgl, axis=0, keepdims=True))
        part = jnp.concatenate(parts, axis=1)

        @pl.when(i == 0)
        def _():
            db_ref[...] = part

        @pl.when(i > 0)
        def _():
            db_ref[...] += part

    gspec = lambda b: pl.BlockSpec((tm, D_MODEL), lambda i: (i, R_GL // D_MODEL + b))
    vec = pl.BlockSpec((1, 3 * D_MODEL), lambda i: (0, 0))
    return pl.pallas_call(
        body, name=name, grid=(S // tm,),
        in_specs=[pl.BlockSpec((tm, D_MODEL), lambda i: (i, 0)),
                  pl.BlockSpec((3, tm, D_MODEL), lambda i: (0, i, 0)), gspec(0), gspec(1), gspec(2), vec],
        out_specs=[pl.BlockSpec((3, tm, D_MODEL), lambda i: (0, i, 0)),
                   pl.BlockSpec((tm, 3 * D_MODEL), lambda i: (i, 0)), vec],
        out_shape=[jax.ShapeDtypeStruct((3, S, D_MODEL), BF16), jax.ShapeDtypeStruct((S, 3 * D_MODEL), BF16),
                   jax.ShapeDtypeStruct((1, 3 * D_MODEL), F32)],
        compiler_params=_cp(("arbitrary",)))(dm, prods, ur, ur, ur, b_merge)


def _rope_tables(pos):
    half = ROT // 2
    S = pos.shape[0]
    inv = ROPE_THETA ** (-jnp.arange(half, dtype=F32) / half)
    per_row = LANES // half
    ang = jnp.repeat(pos.astype(F32).reshape(S // per_row, per_row), half, axis=1) * jnp.tile(inv, per_row)
    cos, sin = jnp.cos(ang).reshape(S, half), jnp.sin(ang).reshape(S, half)
    one = jnp.ones((S, LANES - ROT), F32)
    zero = jnp.zeros((S, LANES - ROT), F32)
    zh = jnp.zeros((S, half), F32)
    c = jnp.concatenate([cos, cos, one], axis=1)
    s1 = jnp.concatenate([-sin, zh, zero], axis=1)
    s2 = jnp.concatenate([zh, sin, zero], axis=1)
    return c, s1, s2


def _to_tiles(t):
    S, H = t.shape
    return t.reshape(S // LANES, LANES, H).transpose(0, 2, 1)


def _from_tiles(t):
    nt, H, _ = t.shape
    return t.transpose(1, 0, 2).reshape(H, nt * LANES)


def _local_step(x, mem, pos, tgt, g_pre, g_post, g_mem, wt, bf_pad, b_merge, w_kv, wbs, w_out, pack=None):
    S = x.shape[0]
    T = min(512, S)
    nq = S // T
    tabs = _rope_tables(pos)

    h = _rms_fwd(x, g_pre, name="rms_pre")
    hs = [_to_classes(h, d) for d in DIL]
    tabs_g = [[_to_classes(t, d) for t in tabs] for d in DIL]
    qkvs = [_proj_rope(hs[g], wt[f"A{g}"], tabs_g[g], name=f"proj_a{g}") for g in range(3)]
    ub = _mm(h, wt["B"], bt=True, out_dtype=BF16, name="proj_b", tn=1536)
    ur = _mm(h, wt["R"], bt=True, name="proj_r", tn=1792)

    outs_c, lses_c = [], []
    for g in range(3):
        o, l = _attn_a_fwd(qkvs[g], g, name=f"attn_a_fwd{g}")
        outs_c.append(o)
        lses_c.append(l)
    outs_a = [_from_classes(o, d) for o, d in zip(outs_c, DIL)]
    lses_a = [_from_classes(l, d) for l, d in zip(lses_c, DIL)]
    ya = _merge_a_fwd(outs_a, lses_a, ur, name="merge_a_fwd")

    logf = _logf(ur, bf_pad, name="logf")
    c = _from_tiles(_cumsum_lanes(_to_tiles(logf[:, :B_HEADS]), False, name="cumsum_fwd"))
    ckb = jnp.broadcast_to(c[:, :, None], (B_HEADS, S, LANES))
    qaug, kaug = _fox_aug(ub, ckb, name="fox_aug")
    kt = ub[:, 512:1024].reshape(nq, T, 512).transpose(0, 2, 1)
    vt = ub[:, 1024:1536].reshape(nq, T, 512).transpose(0, 2, 1)
    ob, lse_b = _fox_fwd(qaug, kaug, vt, name="fox_fwd")
    yb = _gate_fwd(ob, ur, R_ZB, name="gate_b_fwd")

    hm = _rms_fwd(mem, g_mem, name="rms_mem")
    mkv = _mm(hm, w_kv, name="proj_mem")
    ym = _mem_fwd(ur, mkv, name="mem_fwd")

    merged, prods = _branch_fwd((ya, yb, ym), wbs, ur, b_merge, name="branch_fwd")
    out = _mm(merged, w_out, name="proj_out")
    dy, d_out, dg_post, loss_row = _post(x, out, tgt, g_post, name="post")

    dmerged = _mm(d_out, w_out, bt=True, name="d_merged")
    dw_out = _mm(merged, d_out, at=True, name="dw_out", tk=2048)
    dprods, dgl, db_merge = _branch_bwd(dmerged, prods, ur, b_merge, name="branch_bwd")
    dys, dwbs = [], []
    for i, (y, wb) in enumerate(zip((ya, yb, ym), wbs)):
        dys.append(_mm(dprods[i], wb, bt=True, name=f"d_y{i}"))
        dwbs.append(_mm(y, dprods[i], at=True, name=f"dw_branch{i}", tk=2048))

    dos_a, adjs_a, dza = _merge_a_bwd(outs_a, lses_a, ur, dys[0], name="merge_a_bwd")
    dus_a = []
    for g, d in enumerate(DIL):
        do_c, adj_c = _to_classes(dos_a[g], d), _to_classes(adjs_a[g], d)
        dq = _attn_a_dq(qkvs[g], tabs_g[g], g, do_c, lses_c[g], adj_c, name=f"attn_a_dq{g}")
        dk, dv = _attn_a_dkv(qkvs[g], tabs_g[g], g, do_c, lses_c[g], adj_c, name=f"attn_a_dkv{g}")
        dus_a.append(jnp.concatenate([dq, dk, dv], axis=1))

    dob, dzb = _gate_bwd(ob, ur, R_ZB, dys[1], name="gate_b_bwd")
    delta_b = _fox_delta(ob, dob, name="fox_delta")
    dkb, dvb, dc_k, dqt, dc_q = _fox_bwd(ub, qaug, kaug, kt, dob, lse_b, delta_b, name="fox_bwd")
    dqb = (dqt.transpose(0, 2, 1).reshape(S, A_WIDTH) * B_SCALE).astype(BF16)
    du_b = jnp.concatenate([dqb, dkb, dvb], axis=1)
    dc = dc_q.reshape(B_HEADS, S) + dc_k.reshape(B_HEADS, S)
    dlogf = _from_tiles(_cumsum_lanes(_to_tiles(dc.T), True, name="cumsum_bwd"))
    dlogf_pad = jnp.pad(dlogf.T, ((0, 0), (0, FB_PAD - B_HEADS)))
    dfb, db_forget = _dfb(ur, bf_pad, dlogf_pad, name="dfb")

    dqm, dzm, dmk, dmv = _mem_bwd(ur, mkv, dys[2], name="mem_bwd")
    dmkv = jnp.concatenate([dmk, dmv], axis=1).astype(BF16)
    dhm = _mm(dmkv, w_kv, bt=True, name="d_hm")
    dw_kv = _mm(hm, dmkv, at=True, name="dw_kv")
    dg_mem = _rms_bwd(mem, g_mem, dhm, None, name="rms_mem_bwd")

    du_r = jnp.concatenate([dza, dzb, dqm, dzm, dgl, dfb], axis=1)
    dwt = {"R": _mm(du_r, h, at=True, name="dw_in_r", tm=1792, tk=1024),
           "B": _mm(du_b, h, at=True, name="dw_in_b", tm=1536, tk=2048)}
    for g in range(3):
        dwt[f"A{g}"] = _mm(dus_a[g], hs[g], at=True, name=f"dw_in_a{g}", tm=1536, tk=2048)
    res = dict(dwt=dwt, dw_kv=dw_kv, dwbs=dwbs, dw_out=dw_out)
    if pack is None:
        dh_b = _mm(du_b, wt["B"], name="d_h_b", tk=1536)
        dh_r = _mm(du_r, wt["R"], name="d_h_r", tk=1792)
    else:
        gbig = pack(dwt, dw_kv, dwbs, dw_out)
        own_idx = _own_slabs()
        dh_b, sib = _mm(du_b, wt["B"], name="d_h_b", tk=1536, comm=_pair_comm(gbig))
        send = _pair_sum(gbig, sib, own_idx, 208, name="pair_sum")
        dh_r, recv = _mm(du_r, wt["R"], name="d_h_r", tk=1792, comm=_chips_comm(send))
        res = dict(parts=[(gbig, None), (sib, 1), (recv, N_CHIP - 1)], own_idx=own_idx)
    dh = dh_r + dh_b
    for g, d in enumerate(DIL):
        dh = dh + _from_classes(_mm(dus_a[g], wt[f"A{g}"], name=f"d_h_a{g}", tk=1536), d)
    grad_x, dg_pre = _rms_bwd(x, g_pre, dh, dy, name="rms_pre_bwd")

    return dict(res, loss=loss_row, grad_x=grad_x, dg_pre=dg_pre, dg_post=dg_post, dg_mem=dg_mem,
                db_forget=db_forget, db_merge=db_merge)


MESH = pl.DeviceIdType.MESH
ANY = pl.BlockSpec(memory_space=pl.ANY)


def _relations():
    return [(k >> 2 & 1, k >> 1 & 1, k & 1) for k in range(1, N_DEV)]


def _coords():
    return lax.axis_index("x"), lax.axis_index("y"), lax.axis_index("c")


def _all_gather(shard, *, name):
    R, W = shard.shape

    def body(x_ref, out_ref, send_sems, recv_sems, local_sem):
        x, y, c = _coords()
        me, sibling = (x, y, c), (x, y, 1 - c)
        chips = [(1 - x, y), (x, 1 - y), (1 - x, 1 - y)]

        def slot(px, py, pc):
            return out_ref.at[4 * px + 2 * py + pc]

        def copy(k, block, to, src=None):
            return pltpu.make_async_remote_copy(
                src_ref=slot(*block) if src is None else src, dst_ref=slot(*block),
                send_sem=send_sems.at[k], recv_sem=recv_sems.at[k], device_id=to, device_id_type=MESH)

        mine = pltpu.make_async_copy(x_ref, slot(*me), local_sem)
        mine.start()
        first = [copy(0, me, sibling, src=x_ref)]
        first += [copy(1 + j, me, (*chip, c), src=x_ref) for j, chip in enumerate(chips)]
        for cp in first:
            cp.start()
        passed = [copy(4 + j, (*chip, c), sibling) for j, chip in enumerate(chips)]
        for j, chip in enumerate(chips):
            copy(1 + j, (*chip, c), me).wait_recv()
            passed[j].start()
        copy(0, sibling, me).wait_recv()
        for j, chip in enumerate(chips):
            copy(4 + j, (*chip, 1 - c), me).wait_recv()
        for cp in first + passed:
            cp.wait_send()
        mine.wait()

    return pl.pallas_call(
        body, name=name, out_shape=jax.ShapeDtypeStruct((N_DEV, R, W), shard.dtype),
        in_specs=[ANY], out_specs=ANY,
        scratch_shapes=[pltpu.SemaphoreType.DMA((N_DEV - 1,)), pltpu.SemaphoreType.DMA((N_DEV - 1,)),
                        pltpu.SemaphoreType.DMA],
    )(shard)


N_CHIP = 4


def _pair_comm(gbig):
    _, R, W = gbig.shape

    def copies(g_ref, sib_ref, send_sems, recv_sems):
        x, y, c = _coords()
        return [pltpu.make_async_remote_copy(
            src_ref=g_ref.at[4 * (x ^ (r >> 1)) + 2 * (y ^ (r & 1)) + (1 - c)], dst_ref=sib_ref.at[r],
            send_sem=send_sems.at[r], recv_sem=recv_sems.at[r], device_id=(x, y, 1 - c), device_id_type=MESH)
            for r in range(N_CHIP)]

    def start(*refs):
        for cp in copies(*refs):
            cp.start()

    def wait(*refs):
        cps = copies(*refs)
        for cp in cps:
            cp.wait_recv()
        for cp in cps:
            cp.wait_send()

    return dict(inputs=[gbig], out_shape=[jax.ShapeDtypeStruct((N_CHIP, R, W), gbig.dtype)],
                sems=[pltpu.SemaphoreType.DMA((N_CHIP,)), pltpu.SemaphoreType.DMA((N_CHIP,))],
                start=start, wait=wait)


def _own_slabs():
    x, y, c = _coords()
    return jnp.stack([4 * (x ^ (r >> 1)) + 2 * (y ^ (r & 1)) + c for r in range(N_CHIP)]).astype(jnp.int32)


def _pair_sum(gbig, sib, own_idx, tr, *, name):
    _, R, W = gbig.shape

    def body(idx_ref, a_ref, b_ref, o_ref):
        o_ref[...] = (a_ref[...] + b_ref[...]).astype(BF16)

    return pl.pallas_call(
        body, name=name,
        grid_spec=pltpu.PrefetchScalarGridSpec(
            num_scalar_prefetch=1, grid=(N_CHIP - 1, R // tr),
            in_specs=[pl.BlockSpec((None, tr, W), lambda r, i, idx: (idx[r + 1], i, 0)),
                      pl.BlockSpec((None, tr, W), lambda r, i, idx: (r + 1, i, 0))],
            out_specs=pl.BlockSpec((None, tr, W), lambda r, i, idx: (r, i, 0))),
        out_shape=jax.ShapeDtypeStruct((N_CHIP - 1, R, W), BF16),
        compiler_params=_cp(("parallel", "parallel")))(own_idx, gbig, sib)


def _chips_comm(send):
    nb, R, W = send.shape

    def copies(b_ref, rb_ref, send_sems, recv_sems):
        x, y, c = _coords()
        return [pltpu.make_async_remote_copy(
            src_ref=b_ref.at[r - 1], dst_ref=rb_ref.at[r - 1], send_sem=send_sems.at[r - 1],
            recv_sem=recv_sems.at[r - 1], device_id=(x ^ (r >> 1), y ^ (r & 1), c), device_id_type=MESH)
            for r in range(1, N_CHIP)]

    def start(*refs):
        for cp in copies(*refs):
            cp.start()

    def wait(*refs):
        cps = copies(*refs)
        for cp in cps:
            cp.wait_recv()
        for cp in cps:
            cp.wait_send()

    return dict(inputs=[send], out_shape=[jax.ShapeDtypeStruct((nb, R, W), send.dtype)],
                sems=[pltpu.SemaphoreType.DMA((nb,)), pltpu.SemaphoreType.DMA((nb,))],
                start=start, wait=wait)


def _gather_small(gsmall, *, name):
    n = N_DEV - 1

    def body(s_ref, rs_ref, send_sems, recv_sems, local_sem):
        x, y, c = _coords()
        me = 4 * x + 2 * y + c
        mine = pltpu.make_async_copy(s_ref, rs_ref.at[me], local_sem)
        mine.start()

        def copy(k, fx, fy, fc, slot):
            return pltpu.make_async_remote_copy(
                src_ref=s_ref, dst_ref=rs_ref.at[slot], send_sem=send_sems.at[k], recv_sem=recv_sems.at[k],
                device_id=(x ^ fx, y ^ fy, c ^ fc), device_id_type=MESH)

        started = [copy(k, *rel, me) for k, rel in enumerate(_relations())]
        for cp in started:
            cp.start()
        for k, (fx, fy, fc) in enumerate(_relations()):
            copy(k, fx, fy, fc, 4 * (x ^ fx) + 2 * (y ^ fy) + (c ^ fc)).wait_recv()
        for cp in started:
            cp.wait_send()
        mine.wait()

    return pl.pallas_call(
        body, name=name, out_shape=jax.ShapeDtypeStruct((N_DEV, 1, P_SMALL), gsmall.dtype),
        in_specs=[ANY], out_specs=ANY,
        scratch_shapes=[pltpu.SemaphoreType.DMA((n,)), pltpu.SemaphoreType.DMA((n,)), pltpu.SemaphoreType.DMA],
    )(gsmall)


def _part_specs(parts, tr, row0):
    assert row0 % tr == 0
    specs = []
    for a, n_used in parts:
        if n_used is None:
            specs.append(pl.BlockSpec((1, tr, a.shape[2]), lambda i, idx: (idx[0], row0 // tr + i, 0)))
        else:
            specs.append(pl.BlockSpec((n_used, tr, a.shape[2]), lambda i, idx: (0, row0 // tr + i, 0)))
    return specs


def _part_total(refs, parts):
    g = None
    for ref, (_, n_used) in zip(refs, parts):
        for k in range(n_used or 1):
            t = ref[k].astype(F32)
            g = t if g is None else g + t
    return g


def _sum_parts(parts, idx, row0, nrows, tr, *, name):
    W = parts[0][0].shape[2]
    assert nrows % tr == 0

    def body(idx_ref, *refs):
        refs[-1][...] = _part_total(refs[:-1], parts)

    return pl.pallas_call(
        body, name=name,
        grid_spec=pltpu.PrefetchScalarGridSpec(
            num_scalar_prefetch=1, grid=(nrows // tr,), in_specs=_part_specs(parts, tr, row0),
            out_specs=pl.BlockSpec((tr, W), lambda i, idx: (i, 0))),
        out_shape=jax.ShapeDtypeStruct((nrows, W), F32),
        compiler_params=_cp(("parallel",)))(idx, *[a for a, _ in parts])


def _adamw(parts, idx, w, m, v, tr, *, name):
    R, W = w.shape
    assert R % tr == 0
    np_ = len(parts)

    def body(idx_ref, *refs):
        w_ref, m_ref, v_ref, g_ref, d_ref, nm_ref, nv_ref = refs[np_:]
        g = _part_total(refs[:np_], parts)
        mm = ADAM_B1 * m_ref[...] + (1.0 - ADAM_B1) * g
        vv = ADAM_B2 * v_ref[...] + (1.0 - ADAM_B2) * (g * g)
        m_hat = mm / (1.0 - ADAM_B1 ** ADAM_STEP)
        v_hat = vv / (1.0 - ADAM_B2 ** ADAM_STEP)
        g_ref[...] = g
        d_ref[...] = -ADAM_LR * (m_hat / (jnp.sqrt(v_hat) + ADAM_EPS) + ADAM_WD * w_ref[...])
        nm_ref[...] = mm
        nv_ref[...] = vv

    blk = pl.BlockSpec((tr, W), lambda i, idx: (i, 0))
    return pl.pallas_call(
        body, name=name,
        grid_spec=pltpu.PrefetchScalarGridSpec(
            num_scalar_prefetch=1, grid=(R // tr,), in_specs=_part_specs(parts, tr, 0) + [blk, blk, blk],
            out_specs=[blk] * 4),
        out_shape=[jax.ShapeDtypeStruct((R, W), F32)] * 4,
        compiler_params=_cp(("parallel",)))(idx, *[a for a, _ in parts], w, m, v)


def _pack_rest(w_kv, wa, wb, wm, w_out):
    return jnp.concatenate([w_kv[0], w_out[0]] + [t[0].reshape(-1, D_MODEL) for t in (wa, wb, wm)], axis=0)


def _unpack_rest(t):
    br = lambda i: t[RO_BR + 64 * i:RO_BR + 64 * (i + 1)].reshape(1, A_WIDTH, D_MODEL // N_DEV)
    return t[None, RO_KV:RO_OUT], br(0), br(1), br(2), t[None, RO_OUT:RO_BR]


def _orig_rows(gathered, a, b):
    res = []
    while a < b:
        dev, r = divmod(a, CS)
        n = min(b - a, CS - r)
        res.append(gathered[dev, RO_IN + r:RO_IN + r + n])
        a += n
    return res


def _full_weights(gathered):
    wt = {}
    for name, ranges in SEGS.items():
        rows = [p for a, b in ranges for p in _orig_rows(gathered, a, b)]
        if SEG_PAD[name]:
            rows.append(jnp.zeros((SEG_PAD[name], D_MODEL), gathered.dtype))
        wt[name] = jnp.concatenate(rows, axis=0)
    w_kv = gathered[:, RO_KV:RO_OUT].reshape(D_MODEL, D_MODEL)
    w_out = gathered[:, RO_OUT:RO_BR].reshape(D_MODEL, D_MODEL)
    wbs = [gathered[:, RO_BR + 64 * i:RO_BR + 64 * (i + 1)].reshape(N_DEV, A_WIDTH, D_MODEL // N_DEV)
           .transpose(1, 0, 2).reshape(A_WIDTH, D_MODEL) for i in range(3)]
    return wt, w_kv, wbs, w_out


def _orig_order(dwt):
    pieces = []
    for name, ranges in SEGS.items():
        o = 0
        for a, b in ranges:
            pieces.append((a, dwt[name][o:o + b - a]))
            o += b - a
    pieces.sort(key=lambda p: p[0])
    return jnp.concatenate([p[1] for p in pieces], axis=0)


def _pack_grads(dwt, dw_kv, dwbs, dw_out):
    g_in = jnp.pad(_orig_order(dwt).reshape(N_DEV, CS, D_MODEL), ((0, 0), (0, IN_ROWS - CS), (0, 0)))
    br = [t.reshape(A_WIDTH, N_DEV, D_MODEL // N_DEV).transpose(1, 0, 2).reshape(N_DEV, -1, D_MODEL) for t in dwbs]
    return jnp.concatenate([dw_kv.reshape(N_DEV, -1, D_MODEL), dw_out.reshape(N_DEV, -1, D_MODEL)] + br + [g_in],
                           axis=1)


def kernel(x, mem, positions, norm_pre_g, norm_post_g, norm_mem_g, w_in, b_forget, b_merge, w_mem_kv, w_branch_a, w_branch_b, w_branch_m, w_out, loss_target, m_norm_pre_g, m_norm_post_g, m_norm_mem_g, m_w_in, m_b_forget, m_b_merge, m_w_mem_kv, m_w_branch_a, m_w_branch_b, m_w_branch_m, m_w_out, v_norm_pre_g, v_norm_post_g, v_norm_mem_g, v_w_in, v_b_forget, v_b_merge, v_w_mem_kv, v_w_branch_a, v_w_branch_b, v_w_branch_m, v_w_out):
    w_rest = _pack_rest(w_mem_kv, w_branch_a, w_branch_b, w_branch_m, w_out)
    shard = jnp.concatenate([w_rest.astype(BF16), w_in[0].T.astype(BF16),
                             jnp.zeros((IN_ROWS - CS, D_MODEL), BF16)], axis=0)
    gathered = _all_gather(shard, name="gather_weights")
    wt, w_kv, wbs, w_o = _full_weights(gathered)

    bf_pad = jnp.pad(b_forget, ((0, 0), (0, FB_PAD - B_HEADS)))
    r = _local_step(x[0], mem[0], positions[0], loss_target[0], norm_pre_g, norm_post_g, norm_mem_g,
                    wt, bf_pad, b_merge, w_kv, wbs, w_o, pack=_pack_grads)

    gsmall = jnp.concatenate([r["dg_pre"], r["dg_post"], r["dg_mem"], r["db_merge"],
                              r["db_forget"][:, :LANES], r["loss"]], axis=1)
    rsmall = _gather_small(gsmall, name="gather_small")
    parts, own_idx = r["parts"], r["own_idx"]

    m_rest = _pack_rest(m_w_mem_kv, m_w_branch_a, m_w_branch_b, m_w_branch_m, m_w_out)
    v_rest = _pack_rest(v_w_mem_kv, v_w_branch_a, v_w_branch_b, v_w_branch_m, v_w_out)
    outs_rest = [_unpack_rest(t) for t in _adamw(parts, own_idx, w_rest, m_rest, v_rest, 64, name="adamw_rest")]
    g_in = _sum_parts(parts, own_idx, RO_IN, IN_ROWS, 16, name="sum_w_in")[:CS].T
    outs_in = _adamw([(g_in[None], 1)], own_idx, w_in[0], m_w_in[0], v_w_in[0], 128, name="adamw_w_in")

    def small_vec(a, b, c, d, e):
        z = jnp.zeros((1, LANES - B_HEADS), F32)
        return jnp.concatenate([a, b, c, d, e, z, jnp.zeros((1, LANES), F32)], axis=1)

    outs_small = _adamw([(rsmall, N_DEV)], own_idx, small_vec(norm_pre_g, norm_post_g, norm_mem_g, b_merge, b_forget),
                        small_vec(m_norm_pre_g, m_norm_post_g, m_norm_mem_g, m_b_merge, m_b_forget),
                        small_vec(v_norm_pre_g, v_norm_post_g, v_norm_mem_g, v_b_merge, v_b_forget),
                        1, name="adamw_small")

    def small_parts(t):
        return [t[:, O_GPRE:O_GPRE + D_MODEL], t[:, O_GPOST:O_GPOST + D_MODEL], t[:, O_GMEM:O_GMEM + D_MODEL],
                t[:, O_BF:O_BF + B_HEADS], t[:, O_BM:O_BM + 3 * D_MODEL]]

    loss = outs_small[0][0, O_LOSS]
    result = [loss, r["grad_x"][None]]
    for rest, w_i, small in zip(outs_rest, outs_in, outs_small):
        gp, gq, gm, bf, bm = small_parts(small)
        w_k, w_a, w_b, w_m, w_ot = rest
        result += [gp, gq, gm, w_i[None], bf, bm, w_k, w_a, w_b, w_m, w_ot]
    return tuple(result)
```

```python
import jax
import jax.numpy as jnp
from jax import lax
from jax.experimental import pallas as pl
from jax.experimental.pallas import tpu as pltpu

F32 = jnp.float32
BF16 = jnp.bfloat16

N_DEV = 8
D_MODEL = 1024
N_MEM = 256
EPS = 1e-6
NEG = -1e30
ROPE_THETA = 500000.0
DIL = (1, 4, 16)
A_HEADS = 4
HEAD = 128
A_WIDTH = 512
B_HEADS = 8
B_HEAD = 64
M_HEADS = 4
ROT = 32
IN_COLS = 11272
FB_PAD = 256

SEGS = {
    "A0": ((0, 512), (1536, 2048), (3072, 3584)),
    "A1": ((512, 1024), (2048, 2560), (3584, 4096)),
    "A2": ((1024, 1536), (2560, 3072), (4096, 4608)),
    "B": ((5120, 6656),),
    "R": ((4608, 5120), (6664, 7176), (7176, 7688), (7688, 8200), (8200, 11272), (6656, 6664)),
}
SEG_PAD = {"A0": 0, "A1": 0, "A2": 0, "B": 0, "R": FB_PAD - B_HEADS}
R_ZA, R_ZB, R_QM, R_ZM, R_GL, R_FB = 0, 512, 1024, 1536, 2048, 5120
NR = R_FB + FB_PAD

ADAM_LR, ADAM_B1, ADAM_B2, ADAM_EPS, ADAM_WD, ADAM_STEP = 0.001, 0.9, 0.999, 1e-08, 0.01, 10

LANES = 128
VMEM_LIMIT = 56 * 1024 * 1024

CS = IN_COLS // N_DEV
RO_KV, RO_OUT, RO_BR, RO_IN = 0, 128, 256, 448
IN_ROWS = 1424
ROWS = RO_IN + IN_ROWS
O_GPRE, O_GPOST, O_GMEM, O_BM, O_BF, O_LOSS = 0, 1024, 2048, 3072, 6144, 6272
P_SMALL = 6400


def _cp(sem=None):
    return pltpu.CompilerParams(dimension_semantics=sem, vmem_limit_bytes=VMEM_LIMIT)


def _dot(a, b):
    return jnp.dot(a, b, preferred_element_type=F32)


def _dot_nt(a, b):
    return lax.dot_general(a, b, (((1,), (1,)), ((), ())), preferred_element_type=F32)


def _sigmoid(z):
    return 1.0 / (1.0 + jnp.exp(-z))


def _mm(a, b, *, name, at=False, bt=False, out_dtype=F32, tm=1024, tn=1024, tk=None, comm=None):
    assert not (at and bt)
    K, M = a.shape if at else a.shape[::-1]
    N = b.shape[0] if bt else b.shape[1]
    tm, tn = min(tm, M), min(tn, N)
    tk = K if tk is None else min(tk, K)
    assert M % tm == 0 and N % tn == 0 and K % tk == 0
    nk = K // tk
    grid = (M // tm, N // tn, nk)
    n_in = len(comm["inputs"]) if comm else 0
    n_out = len(comm["out_shape"]) if comm else 0

    def body(a_ref, b_ref, *rest):
        c_in, o_ref, c_out = rest[:n_in], rest[n_in], rest[n_in + 1:n_in + 1 + n_out]
        acc_ref, sems = rest[n_in + 1 + n_out], rest[n_in + 2 + n_out:]
        if comm:
            step = (pl.program_id(0) * grid[1] + pl.program_id(1)) * grid[2] + pl.program_id(2)

            @pl.when(step == 0)
            def _():
                comm["start"](*c_in, *c_out, *sems)

        av = a_ref[...].astype(BF16)
        bv = b_ref[...].astype(BF16)
        if at:
            p = lax.dot_general(av, bv, (((0,), (0,)), ((), ())), preferred_element_type=F32)
        else:
            p = _dot_nt(av, bv) if bt else _dot(av, bv)
        if nk == 1:
            o_ref[...] = p.astype(out_dtype)
        else:
            k = pl.program_id(2)

            @pl.when(k == 0)
            def _():
                acc_ref[...] = p

            @pl.when(k > 0)
            def _():
                acc_ref[...] += p

            @pl.when(k == nk - 1)
            def _():
                o_ref[...] = acc_ref[...].astype(out_dtype)

        if comm:
            @pl.when(step == grid[0] * grid[1] * grid[2] - 1)
            def _():
                comm["wait"](*c_in, *c_out, *sems)

    b_spec = (pl.BlockSpec((tn, tk), lambda i, j, k: (j, k)) if bt
              else pl.BlockSpec((tk, tn), lambda i, j, k: (k, j)))
    a_spec = (pl.BlockSpec((tk, tm), lambda i, j, k: (k, i)) if at
              else pl.BlockSpec((tm, tk), lambda i, j, k: (i, k)))
    out_spec = pl.BlockSpec((tm, tn), lambda i, j, k: (i, j))
    out_shape = jax.ShapeDtypeStruct((M, N), out_dtype)
    acc = pltpu.VMEM((tm, tn) if nk > 1 else (8, LANES), F32)
    if not comm:
        return pl.pallas_call(
            body, name=name, grid=grid, in_specs=[a_spec, b_spec], out_specs=out_spec, out_shape=out_shape,
            scratch_shapes=[acc], compiler_params=_cp(("parallel", "parallel", "arbitrary")))(a, b)
    return pl.pallas_call(
        body, name=name, grid=grid, in_specs=[a_spec, b_spec] + [ANY] * n_in,
        out_specs=[out_spec] + [ANY] * n_out, out_shape=[out_shape] + comm["out_shape"],
        scratch_shapes=[acc] + comm["sems"],
        compiler_params=_cp(("arbitrary", "arbitrary", "arbitrary")))(a, b, *comm["inputs"])


def _rms_fwd(x, g, *, name):
    S, D = x.shape
    tm = min(512, S)

    def body(x_ref, g_ref, o_ref):
        xv = x_ref[...]
        r = lax.rsqrt(jnp.mean(xv * xv, axis=-1, keepdims=True) + EPS)
        o_ref[...] = (xv * r * g_ref[...]).astype(BF16)

    return pl.pallas_call(
        body, name=name, grid=(S // tm,),
        in_specs=[pl.BlockSpec((tm, D), lambda i: (i, 0)), pl.BlockSpec((1, D), lambda i: (0, 0))],
        out_specs=pl.BlockSpec((tm, D), lambda i: (i, 0)),
        out_shape=jax.ShapeDtypeStruct((S, D), BF16),
        compiler_params=_cp(("parallel",)),
    )(x, g)


def _rms_bwd(x, g, dh, dy, *, name):
    S, D = x.shape
    tm = min(512, S)
    want_dx = dy is not None

    def body(*refs):
        if want_dx:
            x_ref, g_ref, dh_ref, dy_ref, dx_ref, dg_ref = refs
        else:
            x_ref, g_ref, dh_ref, dg_ref = refs
        i = pl.program_id(0)
        xv = x_ref[...]
        r = lax.rsqrt(jnp.mean(xv * xv, axis=-1, keepdims=True) + EPS)
        xh = xv * r
        dhv = dh_ref[...]
        part = jnp.sum(dhv * xh, axis=0, keepdims=True)

        @pl.when(i == 0)
        def _():
            dg_ref[...] = part

        @pl.when(i > 0)
        def _():
            dg_ref[...] += part

        if want_dx:
            dxh = dhv * g_ref[...]
            dx_ref[...] = dy_ref[...] + r * (dxh - xh * jnp.mean(dxh * xh, axis=-1, keepdims=True))

    row = pl.BlockSpec((tm, D), lambda i: (i, 0))
    vec = pl.BlockSpec((1, D), lambda i: (0, 0))
    if want_dx:
        return pl.pallas_call(
            body, name=name, grid=(S // tm,), in_specs=[row, vec, row, row], out_specs=[row, vec],
            out_shape=[jax.ShapeDtypeStruct((S, D), F32), jax.ShapeDtypeStruct((1, D), F32)],
            compiler_params=_cp(("arbitrary",)))(x, g, dh, dy)
    return pl.pallas_call(
        body, name=name, grid=(S // tm,), in_specs=[row, vec, row], out_specs=vec,
        out_shape=jax.ShapeDtypeStruct((1, D), F32),
        compiler_params=_cp(("arbitrary",)))(x, g, dh)


def _post(x, out, tgt, g, *, name):
    S, D = x.shape
    tm = min(512, S)

    def body(x_ref, o_ref, t_ref, g_ref, dy_ref, do_ref, dg_ref, loss_ref):
        i = pl.program_id(0)
        ov = o_ref[...]
        r = lax.rsqrt(jnp.mean(ov * ov, axis=-1, keepdims=True) + EPS)
        n = ov * r
        gv = g_ref[...]
        e = (x_ref[...] + n * gv) - t_ref[...]
        lpart = 0.5 * jnp.sum(jnp.mean(e * e, axis=-1, keepdims=True), axis=0, keepdims=True)
        dy = e * (1.0 / D)
        dy_ref[...] = dy
        dn = dy * gv
        do_ref[...] = (r * (dn - n * jnp.mean(dn * n, axis=-1, keepdims=True))).astype(BF16)
        gpart = jnp.sum(dy * n, axis=0, keepdims=True)
        lrow = jnp.broadcast_to(lpart, (1, LANES))

        @pl.when(i == 0)
        def _():
            dg_ref[...] = gpart
            loss_ref[...] = lrow

        @pl.when(i > 0)
        def _():
            dg_ref[...] += gpart
            loss_ref[...] += lrow

    row = pl.BlockSpec((tm, D), lambda i: (i, 0))
    vec = pl.BlockSpec((1, D), lambda i: (0, 0))
    return pl.pallas_call(
        body, name=name, grid=(S // tm,), in_specs=[row, row, row, vec],
        out_specs=[row, row, vec, pl.BlockSpec((1, LANES), lambda i: (0, 0))],
        out_shape=[jax.ShapeDtypeStruct((S, D), F32), jax.ShapeDtypeStruct((S, D), BF16),
                   jax.ShapeDtypeStruct((1, D), F32), jax.ShapeDtypeStruct((1, LANES), F32)],
        compiler_params=_cp(("arbitrary",)))(x, out, tgt, g)


def _to_classes(t, d):
    if d == 1:
        return t
    S, C = t.shape
    return t.reshape(S // d, d, C).transpose(1, 0, 2).reshape(S, C)


def _from_classes(t, d):
    if d == 1:
        return t
    S, C = t.shape
    return t.reshape(d, S // d, C).transpose(1, 0, 2).reshape(S, C)


def _rope(x, c, s1, s2):
    return x * c + pltpu.roll(x, LANES - ROT // 2, 1) * s1 + pltpu.roll(x, ROT // 2, 1) * s2


def _unrope(d, c, s1, s2):
    return d * c + pltpu.roll(d * s1, ROT // 2, 1) + pltpu.roll(d * s2, LANES - ROT // 2, 1)


def _a_band(qb):
    r = lax.broadcasted_iota(jnp.int32, (qb, qb + HEAD), 0)
    c = lax.broadcasted_iota(jnp.int32, (qb, qb + HEAD), 1)
    return jnp.logical_and(c >= r, c <= r + HEAD)


def _a_first_ok(qb, n):
    c = lax.broadcasted_iota(jnp.int32, (qb, qb + HEAD), 1)
    return jnp.logical_or(c >= HEAD, n > 0)


def _a_last_ok(qb, has_next):
    c = lax.broadcasted_iota(jnp.int32, (qb, qb + HEAD), 1)
    return jnp.logical_or(c < qb, has_next)


A_SCALE = HEAD ** -0.5


def _a_geometry(S, g):
    d = DIL[g]
    L = S // d
    TQ = min(512, L)
    return d, L, TQ, TQ // HEAD, L // TQ, L // HEAD


def _proj_rope(h, w, tabs, *, name):
    S, D = h.shape
    tm = min(512, S)

    def body(h_ref, w_ref, c_ref, s1_ref, s2_ref, o_ref):
        tc = (c_ref[...], s1_ref[...], s2_ref[...])
        u = _dot_nt(h_ref[...], w_ref[...])
        for j in range(3 * A_HEADS):
            sl = slice(j * HEAD, (j + 1) * HEAD)
            o_ref[:, sl] = (_rope(u[:, sl], *tc) if j < 2 * A_HEADS else u[:, sl]).astype(BF16)

    tab = pl.BlockSpec((tm, LANES), lambda i: (i, 0))
    return pl.pallas_call(
        body, name=name, grid=(S // tm,),
        in_specs=[pl.BlockSpec((tm, D), lambda i: (i, 0)), pl.BlockSpec((3 * A_WIDTH, D), lambda i: (0, 0)),
                  tab, tab, tab],
        out_specs=pl.BlockSpec((tm, 3 * A_WIDTH), lambda i: (i, 0)),
        out_shape=jax.ShapeDtypeStruct((S, 3 * A_WIDTH), BF16),
        compiler_params=_cp(("parallel",)))(h, w, *tabs)


def _attn_a_fwd(qkv, g, *, name):
    S = qkv.shape[0]
    d, L, TQ, nsub, nb, nblk = _a_geometry(S, g)

    def body(q_ref, kc_ref, kp_ref, vc_ref, vp_ref, o_ref, l_ref):
        n = pl.program_id(1)
        QB = min(2 * HEAD, TQ)
        band = _a_band(QB)
        first = jnp.logical_and(band, _a_first_ok(QB, n))
        for h in range(A_HEADS):
            hs = slice(h * HEAD, (h + 1) * HEAD)
            for hh in range(TQ // QB):
                sl = slice(hh * QB, (hh + 1) * QB)
                pv = slice(hh * QB - HEAD, hh * QB)
                kcat = jnp.concatenate([kp_ref[:, hs] if hh == 0 else kc_ref[pv, hs], kc_ref[sl, hs]], axis=0)
                vcat = jnp.concatenate([vp_ref[:, hs] if hh == 0 else vc_ref[pv, hs], vc_ref[sl, hs]], axis=0)
                s = jnp.where(first if hh == 0 else band, _dot_nt(q_ref[sl, hs], kcat) * A_SCALE, NEG)
                m = jnp.max(s, axis=-1, keepdims=True)
                p = jnp.exp(s - m)
                den = jnp.sum(p, axis=-1, keepdims=True)
                o_ref[sl, hs] = _dot(p.astype(BF16), vcat) / den
                l_ref[sl, hs] = jnp.broadcast_to(m + jnp.log(den), (QB, HEAD))

    rcur = lambda r, n: r * nb + n
    rprv = lambda r, n: r * nblk + jnp.maximum(n * nsub - 1, 0)
    cur = lambda off: pl.BlockSpec((TQ, A_WIDTH), lambda r, n: (rcur(r, n), off))
    prv = lambda off: pl.BlockSpec((HEAD, A_WIDTH), lambda r, n: (rprv(r, n), off))
    out = pl.BlockSpec((TQ, A_WIDTH), lambda r, n: (rcur(r, n), 0))
    return pl.pallas_call(
        body, name=name, grid=(d, nb),
        in_specs=[cur(0), cur(1), prv(1), cur(2), prv(2)],
        out_specs=[out, out],
        out_shape=[jax.ShapeDtypeStruct((S, A_WIDTH), F32)] * 2,
        compiler_params=_cp(("parallel", "parallel")),
    )(qkv, qkv, qkv, qkv, qkv)


def _attn_a_dq(qkv, tabs, g, do, lse, adj, *, name):
    S = qkv.shape[0]
    d, L, TQ, nsub, nb, nblk = _a_geometry(S, g)

    def body(q_ref, kc_ref, kp_ref, vc_ref, vp_ref, do_ref, l_ref, adj_ref, c_ref, s1_ref, s2_ref, dq_ref):
        n = pl.program_id(1)
        QB = min(2 * HEAD, TQ)
        band = _a_band(QB)
        first = jnp.logical_and(band, _a_first_ok(QB, n))
        for h in range(A_HEADS):
            hs = slice(h * HEAD, (h + 1) * HEAD)
            for hh in range(TQ // QB):
                sl = slice(hh * QB, (hh + 1) * QB)
                pv = slice(hh * QB - HEAD, hh * QB)
                kcat = jnp.concatenate([kp_ref[:, hs] if hh == 0 else kc_ref[pv, hs], kc_ref[sl, hs]], axis=0)
                vcat = jnp.concatenate([vp_ref[:, hs] if hh == 0 else vc_ref[pv, hs], vc_ref[sl, hs]], axis=0)
                s = jnp.where(first if hh == 0 else band, _dot_nt(q_ref[sl, hs], kcat) * A_SCALE, NEG)
                p = jnp.exp(s - l_ref[sl, hs][:, :1])
                ds = p * (_dot_nt(do_ref[sl, hs], vcat) + adj_ref[sl, hs][:, :1])
                dq = _dot(ds.astype(BF16), kcat) * A_SCALE
                dq_ref[sl, hs] = _unrope(dq, c_ref[sl, :], s1_ref[sl, :], s2_ref[sl, :]).astype(BF16)

    rcur = lambda r, n: r * nb + n
    rprv = lambda r, n: r * nblk + jnp.maximum(n * nsub - 1, 0)
    cur = lambda off: pl.BlockSpec((TQ, A_WIDTH), lambda r, n: (rcur(r, n), off))
    prv = lambda off: pl.BlockSpec((HEAD, A_WIDTH), lambda r, n: (rprv(r, n), off))
    tcur = pl.BlockSpec((TQ, LANES), lambda r, n: (rcur(r, n), 0))
    blk = cur(0)
    return pl.pallas_call(
        body, name=name, grid=(d, nb),
        in_specs=[cur(0), cur(1), prv(1), cur(2), prv(2), blk, blk, blk, tcur, tcur, tcur],
        out_specs=blk,
        out_shape=jax.ShapeDtypeStruct((S, A_WIDTH), BF16),
        compiler_params=_cp(("parallel", "parallel")),
    )(qkv, qkv, qkv, qkv, qkv, do, lse, adj, *tabs)


def _attn_a_dkv(qkv, tabs, g, do, lse, adj, *, name):
    S = qkv.shape[0]
    d, L, TQ, nsub, nb, nblk = _a_geometry(S, g)

    def body(qc_ref, qn_ref, kc_ref, vc_ref, doc_ref, don_ref, lc_ref, ln_ref, ac_ref, an_ref,
             c_ref, s1_ref, s2_ref, dk_ref, dv_ref):
        n = pl.program_id(1)
        QB = min(2 * HEAD, TQ)
        nh = TQ // QB
        band = _a_band(QB)
        end = jnp.logical_and(band, _a_last_ok(QB, n < nb - 1))
        for h in range(A_HEADS):
            hs = slice(h * HEAD, (h + 1) * HEAD)
            for kh in range(nh):
                sl = slice(kh * QB, (kh + 1) * QB)
                nx = slice((kh + 1) * QB, (kh + 1) * QB + HEAD)
                last = kh == nh - 1
                cat = lambda cur, nxt: jnp.concatenate([cur[sl, hs], nxt[:, hs] if last else cur[nx, hs]], axis=0)
                qcat = cat(qc_ref, qn_ref)
                docat = cat(doc_ref, don_ref)
                lt = cat(lc_ref, ln_ref).T[:1, :]
                at = cat(ac_ref, an_ref).T[:1, :]
                st = jnp.where(end if last else band, _dot_nt(kc_ref[sl, hs], qcat) * A_SCALE, NEG)
                pt = jnp.exp(st - lt)
                dv_ref[sl, hs] = _dot(pt.astype(BF16), docat).astype(BF16)
                dst = pt * (_dot_nt(vc_ref[sl, hs], docat) + at)
                dk = _dot(dst.astype(BF16), qcat) * A_SCALE
                dk_ref[sl, hs] = _unrope(dk, c_ref[sl, :], s1_ref[sl, :], s2_ref[sl, :]).astype(BF16)

    rcur = lambda r, n: r * nb + n
    rnxt = lambda r, n: r * nblk + jnp.minimum((n + 1) * nsub, nblk - 1)
    cur = lambda off: pl.BlockSpec((TQ, A_WIDTH), lambda r, n: (rcur(r, n), off))
    nxu = lambda off: pl.BlockSpec((HEAD, A_WIDTH), lambda r, n: (rnxt(r, n), off))
    tcur = pl.BlockSpec((TQ, LANES), lambda r, n: (rcur(r, n), 0))
    blk, bnx = cur(0), nxu(0)
    return pl.pallas_call(
        body, name=name, grid=(d, nb),
        in_specs=[cur(0), nxu(0), cur(1), cur(2), blk, bnx, blk, bnx, blk, bnx, tcur, tcur, tcur],
        out_specs=[blk, blk],
        out_shape=[jax.ShapeDtypeStruct((S, A_WIDTH), BF16)] * 2,
        compiler_params=_cp(("parallel", "parallel")),
    )(qkv, qkv, qkv, qkv, do, do, lse, lse, adj, adj, *tabs)


def _silu_parts(z):
    sg = _sigmoid(z)
    return z * sg, sg * (1.0 + z * (1.0 - sg))


def _merge_a_fwd(os_, ls_, ur, *, name):
    S = ur.shape[0]
    tm = min(512, S)

    def body(o0, o1, o2, l0, l1, l2, z_ref, y_ref):
        ls = [l0[...], l1[...], l2[...]]
        mx = jnp.maximum(jnp.maximum(ls[0], ls[1]), ls[2])
        es = [jnp.exp(l - mx) for l in ls]
        den = es[0] + es[1] + es[2]
        y = (es[0] / den) * o0[...] + (es[1] / den) * o1[...] + (es[2] / den) * o2[...]
        y_ref[...] = (y * _silu_parts(z_ref[...])[0]).astype(BF16)

    blk = pl.BlockSpec((tm, A_WIDTH), lambda i: (i, 0))
    return pl.pallas_call(
        body, name=name, grid=(S // tm,),
        in_specs=[blk] * 6 + [pl.BlockSpec((tm, A_WIDTH), lambda i: (i, R_ZA // A_WIDTH))],
        out_specs=blk, out_shape=jax.ShapeDtypeStruct((S, A_WIDTH), BF16),
        compiler_params=_cp(("parallel",)))(*os_, *ls_, ur)


def _merge_a_bwd(os_, ls_, ur, dya, *, name):
    S = ur.shape[0]
    tm = min(256, S)

    def body(o0, o1, o2, l0, l1, l2, z_ref, dy_ref, d0, d1, d2, a0, a1, a2, dz_ref):
        ls = [l0[...], l1[...], l2[...]]
        ov = [o0[...], o1[...], o2[...]]
        mx = jnp.maximum(jnp.maximum(ls[0], ls[1]), ls[2])
        es = [jnp.exp(l - mx) for l in ls]
        den = es[0] + es[1] + es[2]
        ws = [e / den for e in es]
        y = ws[0] * ov[0] + ws[1] * ov[1] + ws[2] * ov[2]
        sz, dsz = _silu_parts(z_ref[...])
        dyv = dy_ref[...]
        dz_ref[...] = (dyv * y * dsz).astype(BF16)
        dyp = dyv * sz
        for h in range(A_HEADS):
            sl = slice(h * HEAD, (h + 1) * HEAD)
            t = jnp.zeros((tm, 1), F32)
            for gi in range(3):
                t = t + ws[gi][:, sl][:, :1] * jnp.sum(dyp[:, sl] * ov[gi][:, sl], axis=-1, keepdims=True)
            for gi, (dref, aref) in enumerate(((d0, a0), (d1, a1), (d2, a2))):
                wg = ws[gi][:, sl]
                dref[:, sl] = (wg * dyp[:, sl]).astype(BF16)
                aref[:, sl] = -wg * t

    blk = pl.BlockSpec((tm, A_WIDTH), lambda i: (i, 0))
    outs = pl.pallas_call(
        body, name=name, grid=(S // tm,),
        in_specs=[blk] * 6 + [pl.BlockSpec((tm, A_WIDTH), lambda i: (i, R_ZA // A_WIDTH)), blk],
        out_specs=[blk] * 7,
        out_shape=[jax.ShapeDtypeStruct((S, A_WIDTH), BF16)] * 3
        + [jax.ShapeDtypeStruct((S, A_WIDTH), F32)] * 3 + [jax.ShapeDtypeStruct((S, A_WIDTH), BF16)],
        compiler_params=_cp(("parallel",)))(*os_, *ls_, ur, dya)
    return outs[0:3], outs[3:6], outs[6]


def _logf(ur, bf_pad, *, name):
    S = ur.shape[0]
    tm = min(1024, S)

    def body(u_ref, b_ref, o_ref):
        z = u_ref[...] + b_ref[...]
        o_ref[...] = jnp.minimum(z, 0.0) - jnp.log(1.0 + jnp.exp(-jnp.abs(z)))

    return pl.pallas_call(
        body, name=name, grid=(S // tm,),
        in_specs=[pl.BlockSpec((tm, FB_PAD), lambda i: (i, R_FB // FB_PAD)),
                  pl.BlockSpec((1, FB_PAD), lambda i: (0, 0))],
        out_specs=pl.BlockSpec((tm, FB_PAD), lambda i: (i, 0)),
        out_shape=jax.ShapeDtypeStruct((S, FB_PAD), F32),
        compiler_params=_cp(("parallel",)))(ur, bf_pad)


def _cumsum_lanes(x, reverse, *, name):
    nt, H, _ = x.shape
    R = nt * H

    def body(x_ref, o_ref):
        v = x_ref[...].reshape(R, LANES)
        lane = lax.broadcasted_iota(jnp.int32, (R, LANES), 1)
        row = lax.broadcasted_iota(jnp.int32, (R, LANES), 0)

        def scan(t, step, idx, n, axis):
            while step < n:
                if reverse:
                    t = t + jnp.where(idx < n - step, pltpu.roll(t, n - step, axis), 0.0)
                else:
                    t = t + jnp.where(idx >= step, pltpu.roll(t, step, axis), 0.0)
                step *= 2
            return t

        v = scan(v, 1, lane, LANES, 1)
        total = jnp.broadcast_to(v[:, :1] if reverse else v[:, LANES - 1:], (R, LANES))
        carry = scan(total, H, row, R, 0) - total
        o_ref[...] = (v + carry).reshape(nt, H, LANES)

    return pl.pallas_call(
        body, name=name, out_shape=jax.ShapeDtypeStruct((nt, H, LANES), F32),
        in_specs=[pl.BlockSpec(memory_space=pltpu.VMEM)], out_specs=pl.BlockSpec(memory_space=pltpu.VMEM),
        compiler_params=_cp())(x)


B_SCALE = B_HEAD ** -0.5


def _pair_masks():
    lane = lax.broadcasted_iota(jnp.int32, (1, LANES), 1)
    row = lax.broadcasted_iota(jnp.int32, (LANES, 1), 0)
    return (lane < B_HEAD, lane >= B_HEAD), (row < B_HEAD, row >= B_HEAD)


def _causal_t(T):
    r = lax.broadcasted_iota(jnp.int32, (T, T), 0)
    c = lax.broadcasted_iota(jnp.int32, (T, T), 1)
    return r <= c


def _zero_other(x, keep):
    return jnp.where(keep, x, jnp.zeros_like(x))


def _fox_aug(ub, ckb, *, name):
    S = ub.shape[0]
    T = min(2048, S)

    def body(q_ref, k_ref, c_ref, qa_ref, ka_ref):
        lane = lax.broadcasted_iota(jnp.int32, (1, LANES), 1)
        q = q_ref[...] * B_SCALE
        k = k_ref[...]
        for a in range(2):
            own = (lane < B_HEAD) if a == 0 else (lane >= B_HEAD)
            o = B_HEAD if a == 0 else 0
            c = c_ref[a]
            hi = c.astype(BF16)
            r1 = c - hi.astype(F32)
            mid = r1.astype(BF16)
            lo = (r1 - mid.astype(F32)).astype(BF16)
            pieces = (hi, mid, lo)
            one = jnp.ones((T, LANES), BF16)
            qa = jnp.where(own, q, jnp.zeros_like(q))
            ka = jnp.where(own, k, jnp.zeros_like(k))
            for t in range(3):
                qa = jnp.where(lane == o + t, pieces[t], qa)
                qa = jnp.where(lane == o + 3 + t, one, qa)
                ka = jnp.where(lane == o + t, one, ka)
                ka = jnp.where(lane == o + 3 + t, -pieces[t], ka)
            qa_ref[a] = qa
            ka_ref[a] = ka

    out = pl.BlockSpec((2, T, LANES), lambda h, i: (h, i, 0))
    return pl.pallas_call(
        body, name=name, grid=(B_HEADS // 2, S // T),
        in_specs=[pl.BlockSpec((T, LANES), lambda h, i: (i, h)), pl.BlockSpec((T, LANES), lambda h, i: (i, 4 + h)), out],
        out_specs=[out, out], out_shape=[jax.ShapeDtypeStruct((B_HEADS, S, LANES), BF16)] * 2,
        compiler_params=_cp(("parallel", "parallel")))(ub, ub, ckb)


def _fox_fwd(qaug, kaug, vt, *, name):
    S = qaug.shape[1]
    T = min(512, S)
    nq = S // T

    def body(q_ref, k_ref, vt_ref, o_ref, l_ref, m_s, l_s, acc_s):
        i = pl.program_id(1)
        _, rows = _pair_masks()
        qm = [q_ref[0], q_ref[1]]
        m_s[...] = jnp.full((2, 1, T), NEG, F32)
        l_s[...] = jnp.zeros((2, 1, T), F32)
        acc_s[...] = jnp.zeros((LANES, T), F32)

        def step(j, masked):
            off = pl.multiple_of(j * T, T)
            vtj = vt_ref[j]
            upd = jnp.zeros((LANES, T), F32)
            alphas = []
            for a in range(2):
                st = _dot_nt(k_ref[a, pl.ds(off, T), :], qm[a])
                if masked:
                    st = jnp.where(_causal_t(T), st, NEG)
                m_old = m_s[a]
                m_new = jnp.maximum(m_old, jnp.max(st, axis=0, keepdims=True))
                alpha = jnp.exp(m_old - m_new)
                pt = jnp.exp(st - m_new)
                l_s[a] = alpha * l_s[a] + jnp.sum(pt, axis=0, keepdims=True)
                m_s[a] = m_new
                upd = upd + _dot(_zero_other(vtj, rows[a]), pt.astype(BF16))
                alphas.append(alpha)
            acc_s[...] = acc_s[...] * jnp.where(rows[0], alphas[0], alphas[1]) + upd

        def loop(j, carry):
            step(j, False)
            return carry

        lax.fori_loop(0, i, loop, 0)
        step(i, True)
        o_ref[...] = (acc_s[...] / jnp.where(rows[0], l_s[0], l_s[1])).T
        l_ref[0] = m_s[0] + jnp.log(l_s[0])
        l_ref[1] = m_s[1] + jnp.log(l_s[1])

    stat = pl.BlockSpec((2, None, 1, T), lambda h, i: (h, i, 0, 0))
    return pl.pallas_call(
        body, name=name, grid=(B_HEADS // 2, nq),
        in_specs=[pl.BlockSpec((2, T, LANES), lambda h, i: (h, i, 0)),
                  pl.BlockSpec((2, S, LANES), lambda h, i: (h, 0, 0)),
                  pl.BlockSpec((nq, LANES, T), lambda h, i: (0, h, 0))],
        out_specs=[pl.BlockSpec((T, LANES), lambda h, i: (i, h)), stat],
        out_shape=[jax.ShapeDtypeStruct((S, A_WIDTH), F32), jax.ShapeDtypeStruct((B_HEADS, nq, 1, T), F32)],
        scratch_shapes=[pltpu.VMEM((2, 1, T), F32), pltpu.VMEM((2, 1, T), F32), pltpu.VMEM((LANES, T), F32)],
        compiler_params=_cp(("parallel", "parallel")),
    )(qaug, kaug, vt)


def _fox_delta(o, do, *, name):
    S = o.shape[0]
    T = min(512, S)
    nq = S // T

    def body(o_ref, do_ref, d_ref):
        _, rows = _pair_masks()
        prod_t = (do_ref[...].astype(F32) * o_ref[...]).T
        d_ref[0] = jnp.sum(_zero_other(prod_t, rows[0]), axis=0, keepdims=True)
        d_ref[1] = jnp.sum(_zero_other(prod_t, rows[1]), axis=0, keepdims=True)

    tile = pl.BlockSpec((T, LANES), lambda h, i: (i, h))
    return pl.pallas_call(
        body, name=name, grid=(B_HEADS // 2, nq), in_specs=[tile, tile],
        out_specs=pl.BlockSpec((2, None, 1, T), lambda h, i: (h, i, 0, 0)),
        out_shape=jax.ShapeDtypeStruct((B_HEADS, nq, 1, T), F32),
        compiler_params=_cp(("parallel", "parallel")))(o, do)


def _fox_bwd(ub, qaug, kaug, kt, do, lse, delta, *, name):
    S = ub.shape[0]
    T = min(512, S)
    nq = S // T

    def body(k_ref, v_ref, kt_ref, q_ref, do_ref, l_ref, dl_ref,
             dk_ref, dv_ref, dck_ref, dqt_ref, dcq_ref, dk_s, dv_s, dc_s):
        j = pl.program_id(1)
        lanes, rows = _pair_masks()
        vv = v_ref[...]
        ktj = kt_ref[...]
        km = [k_ref[0], k_ref[1]]
        ktm = [_zero_other(ktj, rows[0]), _zero_other(ktj, rows[1])]
        dk_s[...] = jnp.zeros((2, T, LANES), F32)
        dv_s[...] = jnp.zeros((T, LANES), F32)
        dc_s[...] = jnp.zeros((2, T, 1), F32)

        @pl.when(j == 0)
        def _():
            dqt_ref[...] = jnp.zeros((nq, LANES, T), F32)
            dcq_ref[...] = jnp.zeros((2, nq, 1, T), F32)

        def step(i, masked):
            off = pl.multiple_of(i * T, T)
            doi = do_ref[pl.ds(off, T), :]
            upd = jnp.zeros((LANES, T), F32)
            for a in range(2):
                qi = q_ref[a, pl.ds(off, T), :]
                st = _dot_nt(km[a], qi)
                if masked:
                    st = jnp.where(_causal_t(T), st, NEG)
                pt = jnp.exp(st - l_ref[a, i])
                doa = _zero_other(doi, lanes[a])
                dv_s[...] += _dot(pt.astype(BF16), doa)
                dst = pt * (_dot_nt(vv, doa) - dl_ref[a, i])
                dsb = dst.astype(BF16)
                dk_s[a] += _dot(dsb, qi)
                upd = upd + _dot(ktm[a], dsb)
                dc_s[a] -= jnp.sum(dst, axis=-1, keepdims=True)
                dcq_ref[a, i] += jnp.sum(dst, axis=0, keepdims=True)
            dqt_ref[i] += upd

        def loop(i, carry):
            step(i, False)
            return carry

        step(j, True)
        lax.fori_loop(j + 1, nq, loop, 0)
        dk_ref[...] = jnp.where(lanes[0], dk_s[0], dk_s[1]).astype(BF16)
        dv_ref[...] = dv_s[...].astype(BF16)
        dck_ref[...] = dc_s[...]

    rowv = pl.BlockSpec((2, nq, 1, T), lambda h, j: (h, 0, 0, 0))
    tile = pl.BlockSpec((T, LANES), lambda h, j: (j, h))
    return pl.pallas_call(
        body, name=name, grid=(B_HEADS // 2, nq),
        in_specs=[pl.BlockSpec((2, T, LANES), lambda h, j: (h, j, 0)),
                  pl.BlockSpec((T, LANES), lambda h, j: (j, 8 + h)),
                  pl.BlockSpec((None, LANES, T), lambda h, j: (j, h, 0)),
                  pl.BlockSpec((2, S, LANES), lambda h, j: (h, 0, 0)),
                  pl.BlockSpec((S, LANES), lambda h, j: (0, h)),
                  rowv, rowv],
        out_specs=[tile, tile, pl.BlockSpec((2, T, 1), lambda h, j: (h, j, 0)),
                   pl.BlockSpec((nq, LANES, T), lambda h, j: (0, h, 0)), rowv],
        out_shape=[jax.ShapeDtypeStruct((S, A_WIDTH), BF16)] * 2 + [jax.ShapeDtypeStruct((B_HEADS, S, 1), F32),
                   jax.ShapeDtypeStruct((nq, A_WIDTH, T), F32), jax.ShapeDtypeStruct((B_HEADS, nq, 1, T), F32)],
        scratch_shapes=[pltpu.VMEM((2, T, LANES), F32), pltpu.VMEM((T, LANES), F32), pltpu.VMEM((2, T, 1), F32)],
        compiler_params=_cp(("parallel", "arbitrary")),
    )(kaug, ub, kt, qaug, do, lse, delta)


def _gate_fwd(o, ur, zcol, *, name):
    S = ur.shape[0]
    tm = min(1024, S)

    def body(o_ref, z_ref, y_ref):
        y_ref[...] = (o_ref[...] * _silu_parts(z_ref[...])[0]).astype(BF16)

    blk = pl.BlockSpec((tm, A_WIDTH), lambda i: (i, 0))
    return pl.pallas_call(
        body, name=name, grid=(S // tm,),
        in_specs=[blk, pl.BlockSpec((tm, A_WIDTH), lambda i: (i, zcol // A_WIDTH))],
        out_specs=blk, out_shape=jax.ShapeDtypeStruct((S, A_WIDTH), BF16),
        compiler_params=_cp(("parallel",)))(o, ur)


def _gate_bwd(o, ur, zcol, dy, *, name):
    S = ur.shape[0]
    tm = min(1024, S)

    def body(o_ref, z_ref, dy_ref, do_ref, dz_ref):
        sz, dsz = _silu_parts(z_ref[...])
        dyv = dy_ref[...]
        do_ref[...] = (dyv * sz).astype(BF16)
        dz_ref[...] = (dyv * o_ref[...] * dsz).astype(BF16)

    blk = pl.BlockSpec((tm, A_WIDTH), lambda i: (i, 0))
    return pl.pallas_call(
        body, name=name, grid=(S // tm,),
        in_specs=[blk, pl.BlockSpec((tm, A_WIDTH), lambda i: (i, zcol // A_WIDTH)), blk],
        out_specs=[blk, blk], out_shape=[jax.ShapeDtypeStruct((S, A_WIDTH), BF16)] * 2,
        compiler_params=_cp(("parallel",)))(o, ur, dy)


def _dfb(ur, bf_pad, dlogf_pad, *, name):
    S = ur.shape[0]
    tm = min(1024, S)

    def body(u_ref, b_ref, d_ref, o_ref, s_ref):
        i = pl.program_id(0)
        dv = d_ref[...] * _sigmoid(-(u_ref[...] + b_ref[...]))
        o_ref[...] = dv.astype(BF16)
        part = jnp.sum(dv, axis=0, keepdims=True)

        @pl.when(i == 0)
        def _():
            s_ref[...] = part

        @pl.when(i > 0)
        def _():
            s_ref[...] += part

    vec = pl.BlockSpec((1, FB_PAD), lambda i: (0, 0))
    blk = pl.BlockSpec((tm, FB_PAD), lambda i: (i, 0))
    return pl.pallas_call(
        body, name=name, grid=(S // tm,),
        in_specs=[pl.BlockSpec((tm, FB_PAD), lambda i: (i, R_FB // FB_PAD)), vec, blk],
        out_specs=[blk, vec],
        out_shape=[jax.ShapeDtypeStruct((S, FB_PAD), BF16), jax.ShapeDtypeStruct((1, FB_PAD), F32)],
        compiler_params=_cp(("arbitrary",)))(ur, bf_pad, dlogf_pad)


M_SCALE = HEAD ** -0.5


def _mem_fwd(ur, mkv, *, name):
    S = ur.shape[0]
    T = min(512, S)

    def body(q_ref, z_ref, k_ref, v_ref, y_ref):
        for h in range(M_HEADS):
            hs = slice(h * HEAD, (h + 1) * HEAD)
            s = _dot_nt(q_ref[:, hs].astype(BF16), k_ref[:, hs].astype(BF16)) * M_SCALE
            p = jnp.exp(s - jnp.max(s, axis=-1, keepdims=True))
            p = p / jnp.sum(p, axis=-1, keepdims=True)
            o = _dot(p.astype(BF16), v_ref[:, hs].astype(BF16))
            y_ref[:, hs] = (o * _silu_parts(z_ref[:, hs])[0]).astype(BF16)

    wide = lambda col: pl.BlockSpec((T, A_WIDTH), lambda i: (i, col // A_WIDTH))
    kv = lambda half: pl.BlockSpec((N_MEM, A_WIDTH), lambda i: (0, half))
    return pl.pallas_call(
        body, name=name, grid=(S // T,),
        in_specs=[wide(R_QM), wide(R_ZM), kv(0), kv(1)],
        out_specs=pl.BlockSpec((T, A_WIDTH), lambda i: (i, 0)),
        out_shape=jax.ShapeDtypeStruct((S, A_WIDTH), BF16),
        compiler_params=_cp(("parallel",)))(ur, ur, mkv, mkv)


def _mem_bwd(ur, mkv, dy, *, name):
    S = ur.shape[0]
    T = min(512, S)

    def body(q_ref, z_ref, k_ref, v_ref, dy_ref, dq_ref, dz_ref, dk_ref, dv_ref):
        i = pl.program_id(0)

        @pl.when(i == 0)
        def _():
            dk_ref[...] = jnp.zeros((N_MEM, A_WIDTH), F32)
            dv_ref[...] = jnp.zeros((N_MEM, A_WIDTH), F32)

        for h in range(M_HEADS):
            hs = slice(h * HEAD, (h + 1) * HEAD)
            qv = q_ref[:, hs].astype(BF16)
            kv = k_ref[:, hs].astype(BF16)
            vv = v_ref[:, hs].astype(BF16)
            s = _dot_nt(qv, kv) * M_SCALE
            p = jnp.exp(s - jnp.max(s, axis=-1, keepdims=True))
            p = p / jnp.sum(p, axis=-1, keepdims=True)
            o = _dot(p.astype(BF16), vv)
            sz, dsz = _silu_parts(z_ref[:, hs])
            dyv = dy_ref[:, hs]
            dz_ref[:, hs] = (dyv * o * dsz).astype(BF16)
            dov = (dyv * sz).astype(BF16)
            dp = _dot_nt(dov, vv)
            ds = p * (dp - jnp.sum(p * dp, axis=-1, keepdims=True))
            dq_ref[:, hs] = (_dot(ds.astype(BF16), kv) * M_SCALE).astype(BF16)
            dv_ref[:, hs] += _dot(p.T.astype(BF16), dov)
            dk_ref[:, hs] += _dot(ds.T.astype(BF16), qv) * M_SCALE

    wide = lambda col: pl.BlockSpec((T, A_WIDTH), lambda i: (i, col // A_WIDTH))
    kv = lambda half: pl.BlockSpec((N_MEM, A_WIDTH), lambda i: (0, half))
    tile = pl.BlockSpec((T, A_WIDTH), lambda i: (i, 0))
    acc = pl.BlockSpec((N_MEM, A_WIDTH), lambda i: (0, 0))
    return pl.pallas_call(
        body, name=name, grid=(S // T,),
        in_specs=[wide(R_QM), wide(R_ZM), kv(0), kv(1), tile],
        out_specs=[tile, tile, acc, acc],
        out_shape=[jax.ShapeDtypeStruct((S, A_WIDTH), BF16)] * 2
        + [jax.ShapeDtypeStruct((N_MEM, A_WIDTH), F32)] * 2,
        compiler_params=_cp(("arbitrary",)))(ur, ur, mkv, mkv, dy)


def _branch_fwd(ys, wbs, ur, b_merge, *, name):
    S = ur.shape[0]
    tm, tn = min(512, S), 512
    nj = D_MODEL // tn

    def body(ya, yb, ym, wa, wb, wm, g0, g1, g2, b0, b1, b2, mg_ref, p_ref):
        acc = jnp.zeros((tm, tn), F32)
        for i, (y, w, gr, br) in enumerate(((ya, wa, g0, b0), (yb, wb, g1, b1), (ym, wm, g2, b2))):
            pr = _dot(y[...], w[...])
            p_ref[i] = pr.astype(BF16)
            acc = acc + _sigmoid(gr[...] + br[...]) * pr
        mg_ref[...] = acc.astype(BF16)

    yspec = pl.BlockSpec((tm, A_WIDTH), lambda i, j: (i, 0))
    wspec = pl.BlockSpec((A_WIDTH, tn), lambda i, j: (0, j))
    gspec = lambda b: pl.BlockSpec((tm, tn), lambda i, j: (i, (R_GL + b * D_MODEL) // tn + j))
    bspec = lambda b: pl.BlockSpec((1, tn), lambda i, j: (0, b * nj + j))
    return pl.pallas_call(
        body, name=name, grid=(S // tm, nj),
        in_specs=[yspec] * 3 + [wspec] * 3 + [gspec(0), gspec(1), gspec(2), bspec(0), bspec(1), bspec(2)],
        out_specs=[pl.BlockSpec((tm, tn), lambda i, j: (i, j)),
                   pl.BlockSpec((3, tm, tn), lambda i, j: (0, i, j))],
        out_shape=[jax.ShapeDtypeStruct((S, D_MODEL), BF16), jax.ShapeDtypeStruct((3, S, D_MODEL), BF16)],
        compiler_params=_cp(("parallel", "parallel")))(*ys, *wbs, ur, ur, ur, b_merge, b_merge, b_merge)


def _branch_bwd(dm, prods, ur, b_merge, *, name):
    S = ur.shape[0]
    tm = min(256, S)

    def body(dm_ref, p_ref, g0, g1, g2, b_ref, dp_ref, dgl_ref, db_ref):
        i = pl.program_id(0)
        dmv = dm_ref[...]
        parts = []
        for b, gr in enumerate((g0, g1, g2)):
            sl = slice(b * D_MODEL, (b + 1) * D_MODEL)
            gt = _sigmoid(gr[...] + b_ref[:, sl])
            dp_ref[b] = (dmv * gt).astype(BF16)
            dgl = dmv * p_ref[b].astype(F32) * gt * (1.0 - gt)
            dgl_ref[:, sl] = dgl.astype(BF16)
            parts.append(jnp.sum(dgl, axis=0, keepdims=True))
        part = jnp.concatenate(parts, axis=1)

        @pl.when(i == 0)
        def _():
            db_ref[...] = part

        @pl.when(i > 0)
        def _():
            db_ref[...] += part

    gspec = lambda b: pl.BlockSpec((tm, D_MODEL), lambda i: (i, R_GL // D_MODEL + b))
    vec = pl.BlockSpec((1, 3 * D_MODEL), lambda i: (0, 0))
    return pl.pallas_call(
        body, name=name, grid=(S // tm,),
        in_specs=[pl.BlockSpec((tm, D_MODEL), lambda i: (i, 0)),
                  pl.BlockSpec((3, tm, D_MODEL), lambda i: (0, i, 0)), gspec(0), gspec(1), gspec(2), vec],
        out_specs=[pl.BlockSpec((3, tm, D_MODEL), lambda i: (0, i, 0)),
                   pl.BlockSpec((tm, 3 * D_MODEL), lambda i: (i, 0)), vec],
        out_shape=[jax.ShapeDtypeStruct((3, S, D_MODEL), BF16), jax.ShapeDtypeStruct((S, 3 * D_MODEL), BF16),
                   jax.ShapeDtypeStruct((1, 3 * D_MODEL), F32)],
        compiler_params=_cp(("arbitrary",)))(dm, prods, ur, ur, ur, b_merge)


def _rope_tables(pos):
    half = ROT // 2
    S = pos.shape[0]
    inv = ROPE_THETA ** (-jnp.arange(half, dtype=F32) / half)
    per_row = LANES // half
    ang = jnp.repeat(pos.astype(F32).reshape(S // per_row, per_row), half, axis=1) * jnp.tile(inv, per_row)
    cos, sin = jnp.cos(ang).reshape(S, half), jnp.sin(ang).reshape(S, half)
    one = jnp.ones((S, LANES - ROT), F32)
    zero = jnp.zeros((S, LANES - ROT), F32)
    zh = jnp.zeros((S, half), F32)
    c = jnp.concatenate([cos, cos, one], axis=1)
    s1 = jnp.concatenate([-sin, zh, zero], axis=1)
    s2 = jnp.concatenate([zh, sin, zero], axis=1)
    return c, s1, s2


def _to_tiles(t):
    S, H = t.shape
    return t.reshape(S // LANES, LANES, H).transpose(0, 2, 1)


def _from_tiles(t):
    nt, H, _ = t.shape
    return t.transpose(1, 0, 2).reshape(H, nt * LANES)


def _local_step(x, mem, pos, tgt, g_pre, g_post, g_mem, wt, bf_pad, b_merge, w_kv, wbs, w_out, pack=None):
    S = x.shape[0]
    T = min(512, S)
    nq = S // T
    tabs = _rope_tables(pos)

    h = _rms_fwd(x, g_pre, name="rms_pre")
    hs = [_to_classes(h, d) for d in DIL]
    tabs_g = [[_to_classes(t, d) for t in tabs] for d in DIL]
    qkvs = [_proj_rope(hs[g], wt[f"A{g}"], tabs_g[g], name=f"proj_a{g}") for g in range(3)]
    ub = _mm(h, wt["B"], bt=True, out_dtype=BF16, name="proj_b", tn=1536)
    ur = _mm(h, wt["R"], bt=True, name="proj_r", tn=1792)

    outs_c, lses_c = [], []
    for g in range(3):
        o, l = _attn_a_fwd(qkvs[g], g, name=f"attn_a_fwd{g}")
        outs_c.append(o)
        lses_c.append(l)
    outs_a = [_from_classes(o, d) for o, d in zip(outs_c, DIL)]
    lses_a = [_from_classes(l, d) for l, d in zip(lses_c, DIL)]
    ya = _merge_a_fwd(outs_a, lses_a, ur, name="merge_a_fwd")

    logf = _logf(ur, bf_pad, name="logf")
    c = _from_tiles(_cumsum_lanes(_to_tiles(logf[:, :B_HEADS]), False, name="cumsum_fwd"))
    ckb = jnp.broadcast_to(c[:, :, None], (B_HEADS, S, LANES))
    qaug, kaug = _fox_aug(ub, ckb, name="fox_aug")
    kt = ub[:, 512:1024].reshape(nq, T, 512).transpose(0, 2, 1)
    vt = ub[:, 1024:1536].reshape(nq, T, 512).transpose(0, 2, 1)
    ob, lse_b = _fox_fwd(qaug, kaug, vt, name="fox_fwd")
    yb = _gate_fwd(ob, ur, R_ZB, name="gate_b_fwd")

    hm = _rms_fwd(mem, g_mem, name="rms_mem")
    mkv = _mm(hm, w_kv, name="proj_mem")
    ym = _mem_fwd(ur, mkv, name="mem_fwd")

    merged, prods = _branch_fwd((ya, yb, ym), wbs, ur, b_merge, name="branch_fwd")
    out = _mm(merged, w_out, name="proj_out")
    dy, d_out, dg_post, loss_row = _post(x, out, tgt, g_post, name="post")

    dmerged = _mm(d_out, w_out, bt=True, name="d_merged")
    dw_out = _mm(merged, d_out, at=True, name="dw_out", tk=2048)
    dprods, dgl, db_merge = _branch_bwd(dmerged, prods, ur, b_merge, name="branch_bwd")
    dys, dwbs = [], []
    for i, (y, wb) in enumerate(zip((ya, yb, ym), wbs)):
        dys.append(_mm(dprods[i], wb, bt=True, name=f"d_y{i}"))
        dwbs.append(_mm(y, dprods[i], at=True, name=f"dw_branch{i}", tk=2048))

    dos_a, adjs_a, dza = _merge_a_bwd(outs_a, lses_a, ur, dys[0], name="merge_a_bwd")
    dus_a = []
    for g, d in enumerate(DIL):
        do_c, adj_c = _to_classes(dos_a[g], d), _to_classes(adjs_a[g], d)
        dq = _attn_a_dq(qkvs[g], tabs_g[g], g, do_c, lses_c[g], adj_c, name=f"attn_a_dq{g}")
        dk, dv = _attn_a_dkv(qkvs[g], tabs_g[g], g, do_c, lses_c[g], adj_c, name=f"attn_a_dkv{g}")
        dus_a.append(jnp.concatenate([dq, dk, dv], axis=1))

    dob, dzb = _gate_bwd(ob, ur, R_ZB, dys[1], name="gate_b_bwd")
    delta_b = _fox_delta(ob, dob, name="fox_delta")
    dkb, dvb, dc_k, dqt, dc_q = _fox_bwd(ub, qaug, kaug, kt, dob, lse_b, delta_b, name="fox_bwd")
    dqb = (dqt.transpose(0, 2, 1).reshape(S, A_WIDTH) * B_SCALE).astype(BF16)
    du_b = jnp.concatenate([dqb, dkb, dvb], axis=1)
    dc = dc_q.reshape(B_HEADS, S) + dc_k.reshape(B_HEADS, S)
    dlogf = _from_tiles(_cumsum_lanes(_to_tiles(dc.T), True, name="cumsum_bwd"))
    dlogf_pad = jnp.pad(dlogf.T, ((0, 0), (0, FB_PAD - B_HEADS)))
    dfb, db_forget = _dfb(ur, bf_pad, dlogf_pad, name="dfb")

    dqm, dzm, dmk, dmv = _mem_bwd(ur, mkv, dys[2], name="mem_bwd")
    dmkv = jnp.concatenate([dmk, dmv], axis=1).astype(BF16)
    dhm = _mm(dmkv, w_kv, bt=True, name="d_hm")
    dw_kv = _mm(hm, dmkv, at=True, name="dw_kv")
    dg_mem = _rms_bwd(mem, g_mem, dhm, None, name="rms_mem_bwd")

    du_r = jnp.concatenate([dza, dzb, dqm, dzm, dgl, dfb], axis=1)
    dwt = {"R": _mm(du_r, h, at=True, name="dw_in_r", tm=1792, tk=1024),
           "B": _mm(du_b, h, at=True, name="dw_in_b", tm=1536, tk=2048)}
    for g in range(3):
        dwt[f"A{g}"] = _mm(dus_a[g], hs[g], at=True, name=f"dw_in_a{g}", tm=1536, tk=2048)
    res = dict(dwt=dwt, dw_kv=dw_kv, dwbs=dwbs, dw_out=dw_out)
    if pack is None:
        dh_b = _mm(du_b, wt["B"], name="d_h_b", tk=1536)
        dh_r = _mm(du_r, wt["R"], name="d_h_r", tk=1792)
    else:
        gbig = pack(dwt, dw_kv, dwbs, dw_out)
        own_idx = _own_slabs()
        dh_b, sib = _mm(du_b, wt["B"], name="d_h_b", tk=1536, comm=_pair_comm(gbig))
        send = _pair_sum(gbig, sib, own_idx, 208, name="pair_sum")
        dh_r, recv = _mm(du_r, wt["R"], name="d_h_r", tk=1792, comm=_chips_comm(send))
        res = dict(parts=[(gbig, None), (sib, 1), (recv, N_CHIP - 1)], own_idx=own_idx)
    dh = dh_r + dh_b
    for g, d in enumerate(DIL):
        dh = dh + _from_classes(_mm(dus_a[g], wt[f"A{g}"], name=f"d_h_a{g}", tk=1536), d)
    grad_x, dg_pre = _rms_bwd(x, g_pre, dh, dy, name="rms_pre_bwd")

    return dict(res, loss=loss_row, grad_x=grad_x, dg_pre=dg_pre, dg_post=dg_post, dg_mem=dg_mem,
                db_forget=db_forget, db_merge=db_merge)


MESH = pl.DeviceIdType.MESH
ANY = pl.BlockSpec(memory_space=pl.ANY)


def _relations():
    return [(k >> 2 & 1, k >> 1 & 1, k & 1) for k in range(1, N_DEV)]


def _coords():
    return lax.axis_index("x"), lax.axis_index("y"), lax.axis_index("c")


def _all_gather(shard, *, name):
    R, W = shard.shape

    def body(x_ref, out_ref, send_sems, recv_sems, local_sem):
        x, y, c = _coords()
        me, sibling = (x, y, c), (x, y, 1 - c)
        chips = [(1 - x, y), (x, 1 - y), (1 - x, 1 - y)]

        def slot(px, py, pc):
            return out_ref.at[4 * px + 2 * py + pc]

        def copy(k, block, to, src=None):
            return pltpu.make_async_remote_copy(
                src_ref=slot(*block) if src is None else src, dst_ref=slot(*block),
                send_sem=send_sems.at[k], recv_sem=recv_sems.at[k], device_id=to, device_id_type=MESH)

        mine = pltpu.make_async_copy(x_ref, slot(*me), local_sem)
        mine.start()
        first = [copy(0, me, sibling, src=x_ref)]
        first += [copy(1 + j, me, (*chip, c), src=x_ref) for j, chip in enumerate(chips)]
        for cp in first:
            cp.start()
        passed = [copy(4 + j, (*chip, c), sibling) for j, chip in enumerate(chips)]
        for j, chip in enumerate(chips):
            copy(1 + j, (*chip, c), me).wait_recv()
            passed[j].start()
        copy(0, sibling, me).wait_recv()
        for j, chip in enumerate(chips):
            copy(4 + j, (*chip, 1 - c), me).wait_recv()
        for cp in first + passed:
            cp.wait_send()
        mine.wait()

    return pl.pallas_call(
        body, name=name, out_shape=jax.ShapeDtypeStruct((N_DEV, R, W), shard.dtype),
        in_specs=[ANY], out_specs=ANY,
        scratch_shapes=[pltpu.SemaphoreType.DMA((N_DEV - 1,)), pltpu.SemaphoreType.DMA((N_DEV - 1,)),
                        pltpu.SemaphoreType.DMA],
    )(shard)


N_CHIP = 4


def _pair_comm(gbig):
    _, R, W = gbig.shape

    def copies(g_ref, sib_ref, send_sems, recv_sems):
        x, y, c = _coords()
        return [pltpu.make_async_remote_copy(
            src_ref=g_ref.at[4 * (x ^ (r >> 1)) + 2 * (y ^ (r & 1)) + (1 - c)], dst_ref=sib_ref.at[r],
            send_sem=send_sems.at[r], recv_sem=recv_sems.at[r], device_id=(x, y, 1 - c), device_id_type=MESH)
            for r in range(N_CHIP)]

    def start(*refs):
        for cp in copies(*refs):
            cp.start()

    def wait(*refs):
        cps = copies(*refs)
        for cp in cps:
            cp.wait_recv()
        for cp in cps:
            cp.wait_send()

    return dict(inputs=[gbig], out_shape=[jax.ShapeDtypeStruct((N_CHIP, R, W), gbig.dtype)],
                sems=[pltpu.SemaphoreType.DMA((N_CHIP,)), pltpu.SemaphoreType.DMA((N_CHIP,))],
                start=start, wait=wait)


def _own_slabs():
    x, y, c = _coords()
    return jnp.stack([4 * (x ^ (r >> 1)) + 2 * (y ^ (r & 1)) + c for r in range(N_CHIP)]).astype(jnp.int32)


def _pair_sum(gbig, sib, own_idx, tr, *, name):
    _, R, W = gbig.shape

    def body(idx_ref, a_ref, b_ref, o_ref):
        o_ref[...] = (a_ref[...] + b_ref[...]).astype(BF16)

    return pl.pallas_call(
        body, name=name,
        grid_spec=pltpu.PrefetchScalarGridSpec(
            num_scalar_prefetch=1, grid=(N_CHIP - 1, R // tr),
            in_specs=[pl.BlockSpec((None, tr, W), lambda r, i, idx: (idx[r + 1], i, 0)),
                      pl.BlockSpec((None, tr, W), lambda r, i, idx: (r + 1, i, 0))],
            out_specs=pl.BlockSpec((None, tr, W), lambda r, i, idx: (r, i, 0))),
        out_shape=jax.ShapeDtypeStruct((N_CHIP - 1, R, W), BF16),
        compiler_params=_cp(("parallel", "parallel")))(own_idx, gbig, sib)


def _chips_comm(send):
    nb, R, W = send.shape

    def copies(b_ref, rb_ref, send_sems, recv_sems):
        x, y, c = _coords()
        return [pltpu.make_async_remote_copy(
            src_ref=b_ref.at[r - 1], dst_ref=rb_ref.at[r - 1], send_sem=send_sems.at[r - 1],
            recv_sem=recv_sems.at[r - 1], device_id=(x ^ (r >> 1), y ^ (r & 1), c), device_id_type=MESH)
            for r in range(1, N_CHIP)]

    def start(*refs):
        for cp in copies(*refs):
            cp.start()

    def wait(*refs):
        cps = copies(*refs)
        for cp in cps:
            cp.wait_recv()
        for cp in cps:
            cp.wait_send()

    return dict(inputs=[send], out_shape=[jax.ShapeDtypeStruct((nb, R, W), send.dtype)],
                sems=[pltpu.SemaphoreType.DMA((nb,)), pltpu.SemaphoreType.DMA((nb,))],
                start=start, wait=wait)


def _gather_small(gsmall, *, name):
    n = N_DEV - 1

    def body(s_ref, rs_ref, send_sems, recv_sems, local_sem):
        x, y, c = _coords()
        me = 4 * x + 2 * y + c
        mine = pltpu.make_async_copy(s_ref, rs_ref.at[me], local_sem)
        mine.start()

        def copy(k, fx, fy, fc, slot):
            return pltpu.make_async_remote_copy(
                src_ref=s_ref, dst_ref=rs_ref.at[slot], send_sem=send_sems.at[k], recv_sem=recv_sems.at[k],
                device_id=(x ^ fx, y ^ fy, c ^ fc), device_id_type=MESH)

        started = [copy(k, *rel, me) for k, rel in enumerate(_relations())]
        for cp in started:
            cp.start()
        for k, (fx, fy, fc) in enumerate(_relations()):
            copy(k, fx, fy, fc, 4 * (x ^ fx) + 2 * (y ^ fy) + (c ^ fc)).wait_recv()
        for cp in started:
            cp.wait_send()
        mine.wait()

    return pl.pallas_call(
        body, name=name, out_shape=jax.ShapeDtypeStruct((N_DEV, 1, P_SMALL), gsmall.dtype),
        in_specs=[ANY], out_specs=ANY,
        scratch_shapes=[pltpu.SemaphoreType.DMA((n,)), pltpu.SemaphoreType.DMA((n,)), pltpu.SemaphoreType.DMA],
    )(gsmall)


def _part_specs(parts, tr, row0):
    assert row0 % tr == 0
    specs = []
    for a, n_used in parts:
        if n_used is None:
            specs.append(pl.BlockSpec((1, tr, a.shape[2]), lambda i, idx: (idx[0], row0 // tr + i, 0)))
        else:
            specs.append(pl.BlockSpec((n_used, tr, a.shape[2]), lambda i, idx: (0, row0 // tr + i, 0)))
    return specs


def _part_total(refs, parts):
    g = None
    for ref, (_, n_used) in zip(refs, parts):
        for k in range(n_used or 1):
            t = ref[k].astype(F32)
            g = t if g is None else g + t
    return g


def _sum_parts(parts, idx, row0, nrows, tr, *, name):
    W = parts[0][0].shape[2]
    assert nrows % tr == 0

    def body(idx_ref, *refs):
        refs[-1][...] = _part_total(refs[:-1], parts)

    return pl.pallas_call(
        body, name=name,
        grid_spec=pltpu.PrefetchScalarGridSpec(
            num_scalar_prefetch=1, grid=(nrows // tr,), in_specs=_part_specs(parts, tr, row0),
            out_specs=pl.BlockSpec((tr, W), lambda i, idx: (i, 0))),
        out_shape=jax.ShapeDtypeStruct((nrows, W), F32),
        compiler_params=_cp(("parallel",)))(idx, *[a for a, _ in parts])


def _adamw(parts, idx, w, m, v, tr, *, name):
    R, W = w.shape
    assert R % tr == 0
    np_ = len(parts)

    def body(idx_ref, *refs):
        w_ref, m_ref, v_ref, g_ref, d_ref, nm_ref, nv_ref = refs[np_:]
        g = _part_total(refs[:np_], parts)
        mm = ADAM_B1 * m_ref[...] + (1.0 - ADAM_B1) * g
        vv = ADAM_B2 * v_ref[...] + (1.0 - ADAM_B2) * (g * g)
        m_hat = mm / (1.0 - ADAM_B1 ** ADAM_STEP)
        v_hat = vv / (1.0 - ADAM_B2 ** ADAM_STEP)
        g_ref[...] = g
        d_ref[...] = -ADAM_LR * (m_hat / (jnp.sqrt(v_hat) + ADAM_EPS) + ADAM_WD * w_ref[...])
        nm_ref[...] = mm
        nv_ref[...] = vv

    blk = pl.BlockSpec((tr, W), lambda i, idx: (i, 0))
    return pl.pallas_call(
        body, name=name,
        grid_spec=pltpu.PrefetchScalarGridSpec(
            num_scalar_prefetch=1, grid=(R // tr,), in_specs=_part_specs(parts, tr, 0) + [blk, blk, blk],
            out_specs=[blk] * 4),
        out_shape=[jax.ShapeDtypeStruct((R, W), F32)] * 4,
        compiler_params=_cp(("parallel",)))(idx, *[a for a, _ in parts], w, m, v)


def _pack_rest(w_kv, wa, wb, wm, w_out):
    return jnp.concatenate([w_kv[0], w_out[0]] + [t[0].reshape(-1, D_MODEL) for t in (wa, wb, wm)], axis=0)


def _unpack_rest(t):
    br = lambda i: t[RO_BR + 64 * i:RO_BR + 64 * (i + 1)].reshape(1, A_WIDTH, D_MODEL // N_DEV)
    return t[None, RO_KV:RO_OUT], br(0), br(1), br(2), t[None, RO_OUT:RO_BR]


def _orig_rows(gathered, a, b):
    res = []
    while a < b:
        dev, r = divmod(a, CS)
        n = min(b - a, CS - r)
        res.append(gathered[dev, RO_IN + r:RO_IN + r + n])
        a += n
    return res


def _full_weights(gathered):
    wt = {}
    for name, ranges in SEGS.items():
        rows = [p for a, b in ranges for p in _orig_rows(gathered, a, b)]
        if SEG_PAD[name]:
            rows.append(jnp.zeros((SEG_PAD[name], D_MODEL), gathered.dtype))
        wt[name] = jnp.concatenate(rows, axis=0)
    w_kv = gathered[:, RO_KV:RO_OUT].reshape(D_MODEL, D_MODEL)
    w_out = gathered[:, RO_OUT:RO_BR].reshape(D_MODEL, D_MODEL)
    wbs = [gathered[:, RO_BR + 64 * i:RO_BR + 64 * (i + 1)].reshape(N_DEV, A_WIDTH, D_MODEL // N_DEV)
           .transpose(1, 0, 2).reshape(A_WIDTH, D_MODEL) for i in range(3)]
    return wt, w_kv, wbs, w_out


def _orig_order(dwt):
    pieces = []
    for name, ranges in SEGS.items():
        o = 0
        for a, b in ranges:
            pieces.append((a, dwt[name][o:o + b - a]))
            o += b - a
    pieces.sort(key=lambda p: p[0])
    return jnp.concatenate([p[1] for p in pieces], axis=0)


def _pack_grads(dwt, dw_kv, dwbs, dw_out):
    g_in = jnp.pad(_orig_order(dwt).reshape(N_DEV, CS, D_MODEL), ((0, 0), (0, IN_ROWS - CS), (0, 0)))
    br = [t.reshape(A_WIDTH, N_DEV, D_MODEL // N_DEV).transpose(1, 0, 2).reshape(N_DEV, -1, D_MODEL) for t in dwbs]
    return jnp.concatenate([dw_kv.reshape(N_DEV, -1, D_MODEL), dw_out.reshape(N_DEV, -1, D_MODEL)] + br + [g_in],
                           axis=1)


def kernel(x, mem, positions, norm_pre_g, norm_post_g, norm_mem_g, w_in, b_forget, b_merge, w_mem_kv, w_branch_a, w_branch_b, w_branch_m, w_out, loss_target, m_norm_pre_g, m_norm_post_g, m_norm_mem_g, m_w_in, m_b_forget, m_b_merge, m_w_mem_kv, m_w_branch_a, m_w_branch_b, m_w_branch_m, m_w_out, v_norm_pre_g, v_norm_post_g, v_norm_mem_g, v_w_in, v_b_forget, v_b_merge, v_w_mem_kv, v_w_branch_a, v_w_branch_b, v_w_branch_m, v_w_out):
    w_rest = _pack_rest(w_mem_kv, w_branch_a, w_branch_b, w_branch_m, w_out)
    shard = jnp.concatenate([w_rest.astype(BF16), w_in[0].T.astype(BF16),
                             jnp.zeros((IN_ROWS - CS, D_MODEL), BF16)], axis=0)
    gathered = _all_gather(shard, name="gather_weights")
    wt, w_kv, wbs, w_o = _full_weights(gathered)

    bf_pad = jnp.pad(b_forget, ((0, 0), (0, FB_PAD - B_HEADS)))
    r = _local_step(x[0], mem[0], positions[0], loss_target[0], norm_pre_g, norm_post_g, norm_mem_g,
                    wt, bf_pad, b_merge, w_kv, wbs, w_o, pack=_pack_grads)

    gsmall = jnp.concatenate([r["dg_pre"], r["dg_post"], r["dg_mem"], r["db_merge"],
                              r["db_forget"][:, :LANES], r["loss"]], axis=1)
    rsmall = _gather_small(gsmall, name="gather_small")
    parts, own_idx = r["parts"], r["own_idx"]

    m_rest = _pack_rest(m_w_mem_kv, m_w_branch_a, m_w_branch_b, m_w_branch_m, m_w_out)
    v_rest = _pack_rest(v_w_mem_kv, v_w_branch_a, v_w_branch_b, v_w_branch_m, v_w_out)
    outs_rest = [_unpack_rest(t) for t in _adamw(parts, own_idx, w_rest, m_rest, v_rest, 64, name="adamw_rest")]
    g_in = _sum_parts(parts, own_idx, RO_IN, IN_ROWS, 16, name="sum_w_in")[:CS].T
    outs_in = _adamw([(g_in[None], 1)], own_idx, w_in[0], m_w_in[0], v_w_in[0], 128, name="adamw_w_in")

    def small_vec(a, b, c, d, e):
        z = jnp.zeros((1, LANES - B_HEADS), F32)
        return jnp.concatenate([a, b, c, d, e, z, jnp.zeros((1, LANES), F32)], axis=1)

    outs_small = _adamw([(rsmall, N_DEV)], own_idx, small_vec(norm_pre_g, norm_post_g, norm_mem_g, b_merge, b_forget),
                        small_vec(m_norm_pre_g, m_norm_post_g, m_norm_mem_g, m_b_merge, m_b_forget),
                        small_vec(v_norm_pre_g, v_norm_post_g, v_norm_mem_g, v_b_merge, v_b_forget),
                        1, name="adamw_small")

    def small_parts(t):
        return [t[:, O_GPRE:O_GPRE + D_MODEL], t[:, O_GPOST:O_GPOST + D_MODEL], t[:, O_GMEM:O_GMEM + D_MODEL],
                t[:, O_BF:O_BF + B_HEADS], t[:, O_BM:O_BM + 3 * D_MODEL]]

    loss = outs_small[0][0, O_LOSS]
    result = [loss, r["grad_x"][None]]
    for rest, w_i, small in zip(outs_rest, outs_in, outs_small):
        gp, gq, gm, bf, bm = small_parts(small)
        w_k, w_a, w_b, w_m, w_ot = rest
        result += [gp, gq, gm, w_i[None], bf, bm, w_k, w_a, w_b, w_m, w_ot]
    return tuple(result)
```

```python
import jax
import jax.numpy as jnp
from jax import lax
from jax.experimental import pallas as pl
from jax.experimental.pallas import tpu as pltpu

F32 = jnp.float32
BF16 = jnp.bfloat16

N_DEV = 8
D_MODEL = 1024
N_MEM = 256
EPS = 1e-6
NEG = -1e30
ROPE_THETA = 500000.0
DIL = (1, 4, 16)
A_HEADS = 4
HEAD = 128
A_WIDTH = 512
B_HEADS = 8
B_HEAD = 64
M_HEADS = 4
ROT = 32
IN_COLS = 11272
FB_PAD = 256

SEGS = {
    "A0": ((0, 512), (1536, 2048), (3072, 3584)),
    "A1": ((512, 1024), (2048, 2560), (3584, 4096)),
    "A2": ((1024, 1536), (2560, 3072), (4096, 4608)),
    "B": ((5120, 6656),),
    "R": ((4608, 5120), (6664, 7176), (7176, 7688), (7688, 8200), (8200, 11272), (6656, 6664)),
}
SEG_PAD = {"A0": 0, "A1": 0, "A2": 0, "B": 0, "R": FB_PAD - B_HEADS}
R_ZA, R_ZB, R_QM, R_ZM, R_GL, R_FB = 0, 512, 1024, 1536, 2048, 5120
NR = R_FB + FB_PAD

ADAM_LR, ADAM_B1, ADAM_B2, ADAM_EPS, ADAM_WD, ADAM_STEP = 0.001, 0.9, 0.999, 1e-08, 0.01, 10

LANES = 128
VMEM_LIMIT = 56 * 1024 * 1024

CS = IN_COLS // N_DEV
RO_KV, RO_OUT, RO_BR, RO_IN = 0, 128, 256, 448
IN_ROWS = 1424
ROWS = RO_IN + IN_ROWS
O_GPRE, O_GPOST, O_GMEM, O_BM, O_BF, O_LOSS = 0, 1024, 2048, 3072, 6144, 6272
P_SMALL = 6400


def _cp(sem=None):
    return pltpu.CompilerParams(dimension_semantics=sem, vmem_limit_bytes=VMEM_LIMIT)


def _dot(a, b):
    return jnp.dot(a, b, preferred_element_type=F32)


def _dot_nt(a, b):
    return lax.dot_general(a, b, (((1,), (1,)), ((), ())), preferred_element_type=F32)


def _sigmoid(z):
    return 1.0 / (1.0 + jnp.exp(-z))


def _mm(a, b, *, name, at=False, bt=False, out_dtype=F32, tm=1024, tn=1024, tk=None, comm=None):
    assert not (at and bt)
    K, M = a.shape if at else a.shape[::-1]
    N = b.shape[0] if bt else b.shape[1]
    tm, tn = min(tm, M), min(tn, N)
    tk = K if tk is None else min(tk, K)
    assert M % tm == 0 and N % tn == 0 and K % tk == 0
    nk = K // tk
    grid = (M // tm, N // tn, nk)
    n_in = len(comm["inputs"]) if comm else 0
    n_out = len(comm["out_shape"]) if comm else 0

    def body(a_ref, b_ref, *rest):
        c_in, o_ref, c_out = rest[:n_in], rest[n_in], rest[n_in + 1:n_in + 1 + n_out]
        acc_ref, sems = rest[n_in + 1 + n_out], rest[n_in + 2 + n_out:]
        if comm:
            step = (pl.program_id(0) * grid[1] + pl.program_id(1)) * grid[2] + pl.program_id(2)

            @pl.when(step == 0)
            def _():
                comm["start"](*c_in, *c_out, *sems)

        av = a_ref[...].astype(BF16)
        bv = b_ref[...].astype(BF16)
        if at:
            p = lax.dot_general(av, bv, (((0,), (0,)), ((), ())), preferred_element_type=F32)
        else:
            p = _dot_nt(av, bv) if bt else _dot(av, bv)
        if nk == 1:
            o_ref[...] = p.astype(out_dtype)
        else:
            k = pl.program_id(2)

            @pl.when(k == 0)
            def _():
                acc_ref[...] = p

            @pl.when(k > 0)
            def _():
                acc_ref[...] += p

            @pl.when(k == nk - 1)
            def _():
                o_ref[...] = acc_ref[...].astype(out_dtype)

        if comm:
            @pl.when(step == grid[0] * grid[1] * grid[2] - 1)
            def _():
                comm["wait"](*c_in, *c_out, *sems)

    b_spec = (pl.BlockSpec((tn, tk), lambda i, j, k: (j, k)) if bt
              else pl.BlockSpec((tk, tn), lambda i, j, k: (k, j)))
    a_spec = (pl.BlockSpec((tk, tm), lambda i, j, k: (k, i)) if at
              else pl.BlockSpec((tm, tk), lambda i, j, k: (i, k)))
    out_spec = pl.BlockSpec((tm, tn), lambda i, j, k: (i, j))
    out_shape = jax.ShapeDtypeStruct((M, N), out_dtype)
    acc = pltpu.VMEM((tm, tn) if nk > 1 else (8, LANES), F32)
    if not comm:
        return pl.pallas_call(
            body, name=name, grid=grid, in_specs=[a_spec, b_spec], out_specs=out_spec, out_shape=out_shape,
            scratch_shapes=[acc], compiler_params=_cp(("parallel", "parallel", "arbitrary")))(a, b)
    return pl.pallas_call(
        body, name=name, grid=grid, in_specs=[a_spec, b_spec] + [ANY] * n_in,
        out_specs=[out_spec] + [ANY] * n_out, out_shape=[out_shape] + comm["out_shape"],
        scratch_shapes=[acc] + comm["sems"],
        compiler_params=_cp(("arbitrary", "arbitrary", "arbitrary")))(a, b, *comm["inputs"])


def _mm_sum(pairs, *, name, tm=1024, tk=768, comm=None):
    M, N = pairs[0][0].shape[0], pairs[0][1].shape[1]
    tm = min(tm, M)
    steps = [a.shape[1] // tk for a, _ in pairs]
    assert M % tm == 0 and all(a.shape[1] % tk == 0 for a, _ in pairs)
    first = [sum(steps[:p]) for p in range(len(pairs))]
    total = sum(steps)
    grid = (M // tm, total)
    n_in = len(comm["inputs"]) if comm else 0
    n_out = len(comm["out_shape"]) if comm else 0
    npair = len(pairs)

    def body(*refs):
        ab, rest = refs[:2 * npair], refs[2 * npair:]
        c_in, o_ref, c_out = rest[:n_in], rest[n_in], rest[n_in + 1:n_in + 1 + n_out]
        acc_ref, sems = rest[n_in + 1 + n_out], rest[n_in + 2 + n_out:]
        k = pl.program_id(1)
        if comm:
            step = pl.program_id(0) * total + k

            @pl.when(step == 0)
            def _():
                comm["start"](*c_in, *c_out, *sems)

        @pl.when(k == 0)
        def _():
            acc_ref[...] = jnp.zeros((tm, N), F32)

        for p in range(npair):
            @pl.when(jnp.logical_and(k >= first[p], k < first[p] + steps[p]))
            def _(p=p):
                acc_ref[...] += _dot(ab[2 * p][...], ab[2 * p + 1][...])

        @pl.when(k == total - 1)
        def _():
            o_ref[...] = acc_ref[...]

        if comm:
            @pl.when(step == grid[0] * total - 1)
            def _():
                comm["wait"](*c_in, *c_out, *sems)

    def local(p):
        return lambda k: jnp.clip(k - first[p], 0, steps[p] - 1)

    in_specs = []
    for p in range(npair):
        in_specs += [pl.BlockSpec((tm, tk), lambda i, k, f=local(p): (i, f(k))),
                     pl.BlockSpec((tk, N), lambda i, k, f=local(p): (f(k), 0))]
    out_spec = pl.BlockSpec((tm, N), lambda i, k: (i, 0))
    out_shape = jax.ShapeDtypeStruct((M, N), F32)
    args = [t for pair in pairs for t in pair]
    if not comm:
        return pl.pallas_call(
            body, name=name, grid=grid, in_specs=in_specs, out_specs=out_spec, out_shape=out_shape,
            scratch_shapes=[pltpu.VMEM((tm, N), F32)], compiler_params=_cp(("parallel", "arbitrary")))(*args)
    return pl.pallas_call(
        body, name=name, grid=grid, in_specs=in_specs + [ANY] * n_in,
        out_specs=[out_spec] + [ANY] * n_out, out_shape=[out_shape] + comm["out_shape"],
        scratch_shapes=[pltpu.VMEM((tm, N), F32)] + comm["sems"],
        compiler_params=_cp(("arbitrary", "arbitrary")))(*args, *comm["inputs"])


def _rms_fwd(x, g, *, name):
    S, D = x.shape
    tm = min(512, S)

    def body(x_ref, g_ref, o_ref):
        xv = x_ref[...]
        r = lax.rsqrt(jnp.mean(xv * xv, axis=-1, keepdims=True) + EPS)
        o_ref[...] = (xv * r * g_ref[...]).astype(BF16)

    return pl.pallas_call(
        body, name=name, grid=(S // tm,),
        in_specs=[pl.BlockSpec((tm, D), lambda i: (i, 0)), pl.BlockSpec((1, D), lambda i: (0, 0))],
        out_specs=pl.BlockSpec((tm, D), lambda i: (i, 0)),
        out_shape=jax.ShapeDtypeStruct((S, D), BF16),
        compiler_params=_cp(("parallel",)),
    )(x, g)


def _rms_bwd(x, g, dh, dy, *, name):
    S, D = x.shape
    tm = min(512, S)
    want_dx = dy is not None

    def body(*refs):
        if want_dx:
            x_ref, g_ref, dh_ref, dy_ref, dx_ref, dg_ref = refs
        else:
            x_ref, g_ref, dh_ref, dg_ref = refs
        i = pl.program_id(0)
        xv = x_ref[...]
        r = lax.rsqrt(jnp.mean(xv * xv, axis=-1, keepdims=True) + EPS)
        xh = xv * r
        dhv = dh_ref[...]
        part = jnp.sum(dhv * xh, axis=0, keepdims=True)

        @pl.when(i == 0)
        def _():
            dg_ref[...] = part

        @pl.when(i > 0)
        def _():
            dg_ref[...] += part

        if want_dx:
            dxh = dhv * g_ref[...]
            dx_ref[...] = dy_ref[...] + r * (dxh - xh * jnp.mean(dxh * xh, axis=-1, keepdims=True))

    row = pl.BlockSpec((tm, D), lambda i: (i, 0))
    vec = pl.BlockSpec((1, D), lambda i: (0, 0))
    if want_dx:
        return pl.pallas_call(
            body, name=name, grid=(S // tm,), in_specs=[row, vec, row, row], out_specs=[row, vec],
            out_shape=[jax.ShapeDtypeStruct((S, D), F32), jax.ShapeDtypeStruct((1, D), F32)],
            compiler_params=_cp(("arbitrary",)))(x, g, dh, dy)
    return pl.pallas_call(
        body, name=name, grid=(S // tm,), in_specs=[row, vec, row], out_specs=vec,
        out_shape=jax.ShapeDtypeStruct((1, D), F32),
        compiler_params=_cp(("arbitrary",)))(x, g, dh)


def _post(x, out, tgt, g, *, name):
    S, D = x.shape
    tm = min(512, S)

    def body(x_ref, o_ref, t_ref, g_ref, dy_ref, do_ref, dg_ref, loss_ref):
        i = pl.program_id(0)
        ov = o_ref[...]
        r = lax.rsqrt(jnp.mean(ov * ov, axis=-1, keepdims=True) + EPS)
        n = ov * r
        gv = g_ref[...]
        e = (x_ref[...] + n * gv) - t_ref[...]
        lpart = 0.5 * jnp.sum(jnp.mean(e * e, axis=-1, keepdims=True), axis=0, keepdims=True)
        dy = e * (1.0 / D)
        dy_ref[...] = dy
        dn = dy * gv
        do_ref[...] = (r * (dn - n * jnp.mean(dn * n, axis=-1, keepdims=True))).astype(BF16)
        gpart = jnp.sum(dy * n, axis=0, keepdims=True)
        lrow = jnp.broadcast_to(lpart, (1, LANES))

        @pl.when(i == 0)
        def _():
            dg_ref[...] = gpart
            loss_ref[...] = lrow

        @pl.when(i > 0)
        def _():
            dg_ref[...] += gpart
            loss_ref[...] += lrow

    row = pl.BlockSpec((tm, D), lambda i: (i, 0))
    vec = pl.BlockSpec((1, D), lambda i: (0, 0))
    return pl.pallas_call(
        body, name=name, grid=(S // tm,), in_specs=[row, row, row, vec],
        out_specs=[row, row, vec, pl.BlockSpec((1, LANES), lambda i: (0, 0))],
        out_shape=[jax.ShapeDtypeStruct((S, D), F32), jax.ShapeDtypeStruct((S, D), BF16),
                   jax.ShapeDtypeStruct((1, D), F32), jax.ShapeDtypeStruct((1, LANES), F32)],
        compiler_params=_cp(("arbitrary",)))(x, out, tgt, g)


def _to_classes(t, d):
    if d == 1:
        return t
    S, C = t.shape
    return t.reshape(S // d, d, C).transpose(1, 0, 2).reshape(S, C)


def _from_classes(t, d):
    if d == 1:
        return t
    S, C = t.shape
    return t.reshape(d, S // d, C).transpose(1, 0, 2).reshape(S, C)


def _rope(x, c, s1, s2):
    return x * c + pltpu.roll(x, LANES - ROT // 2, 1) * s1 + pltpu.roll(x, ROT // 2, 1) * s2


def _unrope(d, c, s1, s2):
    return d * c + pltpu.roll(d * s1, ROT // 2, 1) + pltpu.roll(d * s2, LANES - ROT // 2, 1)


def _a_band(qb):
    r = lax.broadcasted_iota(jnp.int32, (qb, qb + HEAD), 0)
    c = lax.broadcasted_iota(jnp.int32, (qb, qb + HEAD), 1)
    return jnp.logical_and(c >= r, c <= r + HEAD)


def _a_first_ok(qb, n):
    c = lax.broadcasted_iota(jnp.int32, (qb, qb + HEAD), 1)
    return jnp.logical_or(c >= HEAD, n > 0)


def _a_last_ok(qb, has_next):
    c = lax.broadcasted_iota(jnp.int32, (qb, qb + HEAD), 1)
    return jnp.logical_or(c < qb, has_next)


A_SCALE = HEAD ** -0.5


def _a_geometry(S, g):
    d = DIL[g]
    L = S // d
    TQ = min(512, L)
    return d, L, TQ, TQ // HEAD, L // TQ, L // HEAD


def _proj_rope(h, w, tabs, *, name):
    S, D = h.shape
    tm = min(512, S)

    def body(h_ref, w_ref, c_ref, s1_ref, s2_ref, o_ref):
        tc = (c_ref[...], s1_ref[...], s2_ref[...])
        u = _dot_nt(h_ref[...], w_ref[...])
        for j in range(3 * A_HEADS):
            sl = slice(j * HEAD, (j + 1) * HEAD)
            o_ref[:, sl] = (_rope(u[:, sl], *tc) if j < 2 * A_HEADS else u[:, sl]).astype(BF16)

    tab = pl.BlockSpec((tm, LANES), lambda i: (i, 0))
    return pl.pallas_call(
        body, name=name, grid=(S // tm,),
        in_specs=[pl.BlockSpec((tm, D), lambda i: (i, 0)), pl.BlockSpec((3 * A_WIDTH, D), lambda i: (0, 0)),
                  tab, tab, tab],
        out_specs=pl.BlockSpec((tm, 3 * A_WIDTH), lambda i: (i, 0)),
        out_shape=jax.ShapeDtypeStruct((S, 3 * A_WIDTH), BF16),
        compiler_params=_cp(("parallel",)))(h, w, *tabs)


def _attn_a_fwd(qkv, g, *, name):
    S = qkv.shape[0]
    d, L, TQ, nsub, nb, nblk = _a_geometry(S, g)

    def body(q_ref, kc_ref, kp_ref, vc_ref, vp_ref, o_ref, l_ref):
        n = pl.program_id(1)
        QB = min(2 * HEAD, TQ)
        band = _a_band(QB)
        first = jnp.logical_and(band, _a_first_ok(QB, n))
        for h in range(A_HEADS):
            hs = slice(h * HEAD, (h + 1) * HEAD)
            for hh in range(TQ // QB):
                sl = slice(hh * QB, (hh + 1) * QB)
                pv = slice(hh * QB - HEAD, hh * QB)
                kcat = jnp.concatenate([kp_ref[:, hs] if hh == 0 else kc_ref[pv, hs], kc_ref[sl, hs]], axis=0)
                vcat = jnp.concatenate([vp_ref[:, hs] if hh == 0 else vc_ref[pv, hs], vc_ref[sl, hs]], axis=0)
                s = jnp.where(first if hh == 0 else band, _dot_nt(q_ref[sl, hs], kcat) * A_SCALE, NEG)
                m = jnp.max(s, axis=-1, keepdims=True)
                p = jnp.exp(s - m)
                den = jnp.sum(p, axis=-1, keepdims=True)
                o_ref[sl, hs] = _dot(p.astype(BF16), vcat) / den
                l_ref[sl, hs] = jnp.broadcast_to(m + jnp.log(den), (QB, HEAD))

    rcur = lambda r, n: r * nb + n
    rprv = lambda r, n: r * nblk + jnp.maximum(n * nsub - 1, 0)
    cur = lambda off: pl.BlockSpec((TQ, A_WIDTH), lambda r, n: (rcur(r, n), off))
    prv = lambda off: pl.BlockSpec((HEAD, A_WIDTH), lambda r, n: (rprv(r, n), off))
    out = pl.BlockSpec((TQ, A_WIDTH), lambda r, n: (rcur(r, n), 0))
    return pl.pallas_call(
        body, name=name, grid=(d, nb),
        in_specs=[cur(0), cur(1), prv(1), cur(2), prv(2)],
        out_specs=[out, out],
        out_shape=[jax.ShapeDtypeStruct((S, A_WIDTH), F32)] * 2,
        compiler_params=_cp(("parallel", "parallel")),
    )(qkv, qkv, qkv, qkv, qkv)


def _attn_a_dq(qkv, tabs, g, do, lse, adj, *, name):
    S = qkv.shape[0]
    d, L, TQ, nsub, nb, nblk = _a_geometry(S, g)

    def body(q_ref, kc_ref, kp_ref, vc_ref, vp_ref, do_ref, l_ref, adj_ref, c_ref, s1_ref, s2_ref, dq_ref):
        n = pl.program_id(1)
        QB = min(2 * HEAD, TQ)
        band = _a_band(QB)
        first = jnp.logical_and(band, _a_first_ok(QB, n))
        for h in range(A_HEADS):
            hs = slice(h * HEAD, (h + 1) * HEAD)
            for hh in range(TQ // QB):
                sl = slice(hh * QB, (hh + 1) * QB)
                pv = slice(hh * QB - HEAD, hh * QB)
                kcat = jnp.concatenate([kp_ref[:, hs] if hh == 0 else kc_ref[pv, hs], kc_ref[sl, hs]], axis=0)
                vcat = jnp.concatenate([vp_ref[:, hs] if hh == 0 else vc_ref[pv, hs], vc_ref[sl, hs]], axis=0)
                s = jnp.where(first if hh == 0 else band, _dot_nt(q_ref[sl, hs], kcat) * A_SCALE, NEG)
                p = jnp.exp(s - l_ref[sl, hs][:, :1])
                ds = p * (_dot_nt(do_ref[sl, hs], vcat) + adj_ref[sl, hs][:, :1])
                dq = _dot(ds.astype(BF16), kcat) * A_SCALE
                dq_ref[sl, hs] = _unrope(dq, c_ref[sl, :], s1_ref[sl, :], s2_ref[sl, :]).astype(BF16)

    rcur = lambda r, n: r * nb + n
    rprv = lambda r, n: r * nblk + jnp.maximum(n * nsub - 1, 0)
    cur = lambda off: pl.BlockSpec((TQ, A_WIDTH), lambda r, n: (rcur(r, n), off))
    prv = lambda off: pl.BlockSpec((HEAD, A_WIDTH), lambda r, n: (rprv(r, n), off))
    tcur = pl.BlockSpec((TQ, LANES), lambda r, n: (rcur(r, n), 0))
    blk = cur(0)
    return pl.pallas_call(
        body, name=name, grid=(d, nb),
        in_specs=[cur(0), cur(1), prv(1), cur(2), prv(2), blk, blk, blk, tcur, tcur, tcur],
        out_specs=blk,
        out_shape=jax.ShapeDtypeStruct((S, A_WIDTH), BF16),
        compiler_params=_cp(("parallel", "parallel")),
    )(qkv, qkv, qkv, qkv, qkv, do, lse, adj, *tabs)


def _attn_a_dkv(qkv, tabs, g, do, lse, adj, *, name):
    S = qkv.shape[0]
    d, L, TQ, nsub, nb, nblk = _a_geometry(S, g)

    def body(qc_ref, qn_ref, kc_ref, vc_ref, doc_ref, don_ref, lc_ref, ln_ref, ac_ref, an_ref,
             c_ref, s1_ref, s2_ref, dk_ref, dv_ref):
        n = pl.program_id(1)
        QB = min(2 * HEAD, TQ)
        nh = TQ // QB
        band = _a_band(QB)
        end = jnp.logical_and(band, _a_last_ok(QB, n < nb - 1))
        for h in range(A_HEADS):
            hs = slice(h * HEAD, (h + 1) * HEAD)
            for kh in range(nh):
                sl = slice(kh * QB, (kh + 1) * QB)
                nx = slice((kh + 1) * QB, (kh + 1) * QB + HEAD)
                last = kh == nh - 1
                cat = lambda cur, nxt: jnp.concatenate([cur[sl, hs], nxt[:, hs] if last else cur[nx, hs]], axis=0)
                qcat = cat(qc_ref, qn_ref)
                docat = cat(doc_ref, don_ref)
                lt = cat(lc_ref, ln_ref).T[:1, :]
                at = cat(ac_ref, an_ref).T[:1, :]
                st = jnp.where(end if last else band, _dot_nt(kc_ref[sl, hs], qcat) * A_SCALE, NEG)
                pt = jnp.exp(st - lt)
                dv_ref[sl, hs] = _dot(pt.astype(BF16), docat).astype(BF16)
                dst = pt * (_dot_nt(vc_ref[sl, hs], docat) + at)
                dk = _dot(dst.astype(BF16), qcat) * A_SCALE
                dk_ref[sl, hs] = _unrope(dk, c_ref[sl, :], s1_ref[sl, :], s2_ref[sl, :]).astype(BF16)

    rcur = lambda r, n: r * nb + n
    rnxt = lambda r, n: r * nblk + jnp.minimum((n + 1) * nsub, nblk - 1)
    cur = lambda off: pl.BlockSpec((TQ, A_WIDTH), lambda r, n: (rcur(r, n), off))
    nxu = lambda off: pl.BlockSpec((HEAD, A_WIDTH), lambda r, n: (rnxt(r, n), off))
    tcur = pl.BlockSpec((TQ, LANES), lambda r, n: (rcur(r, n), 0))
    blk, bnx = cur(0), nxu(0)
    return pl.pallas_call(
        body, name=name, grid=(d, nb),
        in_specs=[cur(0), nxu(0), cur(1), cur(2), blk, bnx, blk, bnx, blk, bnx, tcur, tcur, tcur],
        out_specs=[blk, blk],
        out_shape=[jax.ShapeDtypeStruct((S, A_WIDTH), BF16)] * 2,
        compiler_params=_cp(("parallel", "parallel")),
    )(qkv, qkv, qkv, qkv, do, do, lse, lse, adj, adj, *tabs)


def _silu_parts(z):
    sg = _sigmoid(z)
    return z * sg, sg * (1.0 + z * (1.0 - sg))


def _merge_a_fwd(os_, ls_, ur, *, name):
    S = ur.shape[0]
    tm = min(512, S)

    def body(o0, o1, o2, l0, l1, l2, z_ref, y_ref):
        ls = [l0[...], l1[...], l2[...]]
        mx = jnp.maximum(jnp.maximum(ls[0], ls[1]), ls[2])
        es = [jnp.exp(l - mx) for l in ls]
        den = es[0] + es[1] + es[2]
        y = (es[0] / den) * o0[...] + (es[1] / den) * o1[...] + (es[2] / den) * o2[...]
        y_ref[...] = (y * _silu_parts(z_ref[...])[0]).astype(BF16)

    blk = pl.BlockSpec((tm, A_WIDTH), lambda i: (i, 0))
    return pl.pallas_call(
        body, name=name, grid=(S // tm,),
        in_specs=[blk] * 6 + [pl.BlockSpec((tm, A_WIDTH), lambda i: (i, R_ZA // A_WIDTH))],
        out_specs=blk, out_shape=jax.ShapeDtypeStruct((S, A_WIDTH), BF16),
        compiler_params=_cp(("parallel",)))(*os_, *ls_, ur)


def _merge_a_bwd(os_, ls_, ur, dya, *, name):
    S = ur.shape[0]
    tm = min(256, S)

    def body(o0, o1, o2, l0, l1, l2, z_ref, dy_ref, d0, d1, d2, a0, a1, a2, dz_ref):
        ls = [l0[...], l1[...], l2[...]]
        ov = [o0[...], o1[...], o2[...]]
        mx = jnp.maximum(jnp.maximum(ls[0], ls[1]), ls[2])
        es = [jnp.exp(l - mx) for l in ls]
        den = es[0] + es[1] + es[2]
        ws = [e / den for e in es]
        y = ws[0] * ov[0] + ws[1] * ov[1] + ws[2] * ov[2]
        sz, dsz = _silu_parts(z_ref[...])
        dyv = dy_ref[...]
        dz_ref[...] = (dyv * y * dsz).astype(BF16)
        dyp = dyv * sz
        for h in range(A_HEADS):
            sl = slice(h * HEAD, (h + 1) * HEAD)
            t = jnp.zeros((tm, 1), F32)
            for gi in range(3):
                t = t + ws[gi][:, sl][:, :1] * jnp.sum(dyp[:, sl] * ov[gi][:, sl], axis=-1, keepdims=True)
            for gi, (dref, aref) in enumerate(((d0, a0), (d1, a1), (d2, a2))):
                wg = ws[gi][:, sl]
                dref[:, sl] = (wg * dyp[:, sl]).astype(BF16)
                aref[:, sl] = -wg * t

    blk = pl.BlockSpec((tm, A_WIDTH), lambda i: (i, 0))
    outs = pl.pallas_call(
        body, name=name, grid=(S // tm,),
        in_specs=[blk] * 6 + [pl.BlockSpec((tm, A_WIDTH), lambda i: (i, R_ZA // A_WIDTH)), blk],
        out_specs=[blk] * 7,
        out_shape=[jax.ShapeDtypeStruct((S, A_WIDTH), BF16)] * 3
        + [jax.ShapeDtypeStruct((S, A_WIDTH), F32)] * 3 + [jax.ShapeDtypeStruct((S, A_WIDTH), BF16)],
        compiler_params=_cp(("parallel",)))(*os_, *ls_, ur, dya)
    return outs[0:3], outs[3:6], outs[6]


def _logf(ur, bf_pad, *, name):
    S = ur.shape[0]
    tm = min(1024, S)

    def body(u_ref, b_ref, o_ref):
        z = u_ref[...] + b_ref[...]
        o_ref[...] = jnp.minimum(z, 0.0) - jnp.log(1.0 + jnp.exp(-jnp.abs(z)))

    return pl.pallas_call(
        body, name=name, grid=(S // tm,),
        in_specs=[pl.BlockSpec((tm, FB_PAD), lambda i: (i, R_FB // FB_PAD)),
                  pl.BlockSpec((1, FB_PAD), lambda i: (0, 0))],
        out_specs=pl.BlockSpec((tm, FB_PAD), lambda i: (i, 0)),
        out_shape=jax.ShapeDtypeStruct((S, FB_PAD), F32),
        compiler_params=_cp(("parallel",)))(ur, bf_pad)


def _cumsum_lanes(x, reverse, *, name):
    nt, H, _ = x.shape
    R = nt * H

    def body(x_ref, o_ref):
        v = x_ref[...].reshape(R, LANES)
        lane = lax.broadcasted_iota(jnp.int32, (R, LANES), 1)
        row = lax.broadcasted_iota(jnp.int32, (R, LANES), 0)

        def scan(t, step, idx, n, axis):
            while step < n:
                if reverse:
                    t = t + jnp.where(idx < n - step, pltpu.roll(t, n - step, axis), 0.0)
                else:
                    t = t + jnp.where(idx >= step, pltpu.roll(t, step, axis), 0.0)
                step *= 2
            return t

        v = scan(v, 1, lane, LANES, 1)
        total = jnp.broadcast_to(v[:, :1] if reverse else v[:, LANES - 1:], (R, LANES))
        carry = scan(total, H, row, R, 0) - total
        o_ref[...] = (v + carry).reshape(nt, H, LANES)

    return pl.pallas_call(
        body, name=name, out_shape=jax.ShapeDtypeStruct((nt, H, LANES), F32),
        in_specs=[pl.BlockSpec(memory_space=pltpu.VMEM)], out_specs=pl.BlockSpec(memory_space=pltpu.VMEM),
        compiler_params=_cp())(x)


B_SCALE = B_HEAD ** -0.5


def _pair_masks():
    lane = lax.broadcasted_iota(jnp.int32, (1, LANES), 1)
    row = lax.broadcasted_iota(jnp.int32, (LANES, 1), 0)
    return (lane < B_HEAD, lane >= B_HEAD), (row < B_HEAD, row >= B_HEAD)


def _causal_t(T):
    r = lax.broadcasted_iota(jnp.int32, (T, T), 0)
    c = lax.broadcasted_iota(jnp.int32, (T, T), 1)
    return r <= c


def _zero_other(x, keep):
    return jnp.where(keep, x, jnp.zeros_like(x))


def _fox_aug(ub, ckb, *, name):
    S = ub.shape[0]
    T = min(2048, S)

    def body(q_ref, k_ref, c_ref, qa_ref, ka_ref):
        lane = lax.broadcasted_iota(jnp.int32, (1, LANES), 1)
        q = q_ref[...] * B_SCALE
        k = k_ref[...]
        for a in range(2):
            own = (lane < B_HEAD) if a == 0 else (lane >= B_HEAD)
            o = B_HEAD if a == 0 else 0
            c = c_ref[a]
            hi = c.astype(BF16)
            r1 = c - hi.astype(F32)
            mid = r1.astype(BF16)
            lo = (r1 - mid.astype(F32)).astype(BF16)
            pieces = (hi, mid, lo)
            one = jnp.ones((T, LANES), BF16)
            qa = jnp.where(own, q, jnp.zeros_like(q))
            ka = jnp.where(own, k, jnp.zeros_like(k))
            for t in range(3):
                qa = jnp.where(lane == o + t, pieces[t], qa)
                qa = jnp.where(lane == o + 3 + t, one, qa)
                ka = jnp.where(lane == o + t, one, ka)
                ka = jnp.where(lane == o + 3 + t, -pieces[t], ka)
            qa_ref[a] = qa
            ka_ref[a] = ka

    out = pl.BlockSpec((2, T, LANES), lambda h, i: (h, i, 0))
    return pl.pallas_call(
        body, name=name, grid=(B_HEADS // 2, S // T),
        in_specs=[pl.BlockSpec((T, LANES), lambda h, i: (i, h)), pl.BlockSpec((T, LANES), lambda h, i: (i, 4 + h)), out],
        out_specs=[out, out], out_shape=[jax.ShapeDtypeStruct((B_HEADS, S, LANES), BF16)] * 2,
        compiler_params=_cp(("parallel", "parallel")))(ub, ub, ckb)


def _fox_fwd(qaug, kaug, vt, *, name):
    S = qaug.shape[1]
    T = min(512, S)
    nq = S // T

    def body(q_ref, k_ref, vt_ref, o_ref, l_ref, m_s, l_s, acc_s, st_s):
        i = pl.program_id(1)
        _, rows = _pair_masks()
        qm = [q_ref[0], q_ref[1]]
        m_s[...] = jnp.full((2, 1, T), NEG, F32)
        l_s[...] = jnp.zeros((2, 1, T), F32)
        acc_s[...] = jnp.zeros((LANES, T), F32)

        def logits(j):
            off = pl.multiple_of(j * T, T)
            return [_dot_nt(k_ref[a, pl.ds(off, T), :], qm[a]) for a in range(2)]

        def step(j, masked, prefetch):
            nxt = logits(j + 1) if prefetch else None
            vtj = vt_ref[j]
            upd = jnp.zeros((LANES, T), F32)
            alphas = []
            for a in range(2):
                st = st_s[a]
                if masked:
                    st = jnp.where(_causal_t(T), st, NEG)
                m_old = m_s[a]
                m_new = jnp.maximum(m_old, jnp.max(st, axis=0, keepdims=True))
                alpha = jnp.exp(m_old - m_new)
                pt = jnp.exp(st - m_new)
                l_s[a] = alpha * l_s[a] + jnp.sum(pt, axis=0, keepdims=True)
                m_s[a] = m_new
                upd = upd + _dot(_zero_other(vtj, rows[a]), pt.astype(BF16))
                alphas.append(alpha)
            acc_s[...] = acc_s[...] * jnp.where(rows[0], alphas[0], alphas[1]) + upd
            if prefetch:
                st_s[0] = nxt[0]
                st_s[1] = nxt[1]

        def loop(j, carry):
            step(j, False, True)
            return carry

        first = logits(0)
        st_s[0] = first[0]
        st_s[1] = first[1]
        lax.fori_loop(0, i, loop, 0)
        step(i, True, False)
        o_ref[...] = (acc_s[...] / jnp.where(rows[0], l_s[0], l_s[1])).T
        l_ref[0] = m_s[0] + jnp.log(l_s[0])
        l_ref[1] = m_s[1] + jnp.log(l_s[1])

    stat = pl.BlockSpec((2, None, 1, T), lambda h, i: (h, i, 0, 0))
    return pl.pallas_call(
        body, name=name, grid=(B_HEADS // 2, nq),
        in_specs=[pl.BlockSpec((2, T, LANES), lambda h, i: (h, i, 0)),
                  pl.BlockSpec((2, S, LANES), lambda h, i: (h, 0, 0)),
                  pl.BlockSpec((nq, LANES, T), lambda h, i: (0, h, 0))],
        out_specs=[pl.BlockSpec((T, LANES), lambda h, i: (i, h)), stat],
        out_shape=[jax.ShapeDtypeStruct((S, A_WIDTH), F32), jax.ShapeDtypeStruct((B_HEADS, nq, 1, T), F32)],
        scratch_shapes=[pltpu.VMEM((2, 1, T), F32), pltpu.VMEM((2, 1, T), F32), pltpu.VMEM((LANES, T), F32),
                        pltpu.VMEM((2, T, T), F32)],
        compiler_params=_cp(("parallel", "parallel")),
    )(qaug, kaug, vt)


def _fox_delta(o, do, *, name):
    S = o.shape[0]
    T = min(512, S)
    nq = S // T

    per = min(4, nq)

    def body(o_ref, do_ref, d_ref):
        _, rows = _pair_masks()
        for t in range(per):
            sl = slice(t * T, (t + 1) * T)
            prod_t = (do_ref[sl, :].astype(F32) * o_ref[sl, :]).T
            d_ref[0, t] = jnp.sum(_zero_other(prod_t, rows[0]), axis=0, keepdims=True)
            d_ref[1, t] = jnp.sum(_zero_other(prod_t, rows[1]), axis=0, keepdims=True)

    tile = pl.BlockSpec((per * T, LANES), lambda h, i: (i, h))
    return pl.pallas_call(
        body, name=name, grid=(B_HEADS // 2, nq // per), in_specs=[tile, tile],
        out_specs=pl.BlockSpec((2, per, 1, T), lambda h, i: (h, i, 0, 0)),
        out_shape=jax.ShapeDtypeStruct((B_HEADS, nq, 1, T), F32),
        compiler_params=_cp(("parallel", "parallel")))(o, do)


def _fox_bwd(ub, qaug, kaug, kt, do, lse, delta, *, name):
    S = ub.shape[0]
    T = min(512, S)
    nq = S // T

    def body(k_ref, v_ref, kt_ref, q_ref, do_ref, l_ref, dl_ref,
             dk_ref, dv_ref, dck_ref, dqt_ref, dcq_ref, dk_s, dv_s, dc_s):
        j = pl.program_id(1)
        lanes, rows = _pair_masks()
        vv = v_ref[...]
        ktj = kt_ref[...]
        km = [k_ref[0], k_ref[1]]
        ktm = [_zero_other(ktj, rows[0]), _zero_other(ktj, rows[1])]
        dk_s[...] = jnp.zeros((2, T, LANES), F32)
        dv_s[...] = jnp.zeros((T, LANES), F32)
        dc_s[...] = jnp.zeros((2, T, 1), F32)

        @pl.when(j == 0)
        def _():
            dqt_ref[...] = jnp.zeros((nq, LANES, T), F32)
            dcq_ref[...] = jnp.zeros((2, nq, 1, T), F32)

        def step(i, masked):
            off = pl.multiple_of(i * T, T)
            doi = do_ref[pl.ds(off, T), :]
            upd = jnp.zeros((LANES, T), F32)
            for a in range(2):
                qi = q_ref[a, pl.ds(off, T), :]
                st = _dot_nt(km[a], qi)
                if masked:
                    st = jnp.where(_causal_t(T), st, NEG)
                pt = jnp.exp(st - l_ref[a, i])
                doa = _zero_other(doi, lanes[a])
                dv_s[...] += _dot(pt.astype(BF16), doa)
                dst = pt * (_dot_nt(vv, doa) - dl_ref[a, i])
                dsb = dst.astype(BF16)
                dk_s[a] += _dot(dsb, qi)
                upd = upd + _dot(ktm[a], dsb)
                dc_s[a] -= jnp.sum(dst, axis=-1, keepdims=True)
                dcq_ref[a, i] += jnp.sum(dst, axis=0, keepdims=True)
            dqt_ref[i] += upd

        def loop(i, carry):
            step(i, False)
            return carry

        step(j, True)
        lax.fori_loop(j + 1, nq, loop, 0)
        dk_ref[...] = jnp.where(lanes[0], dk_s[0], dk_s[1]).astype(BF16)
        dv_ref[...] = dv_s[...].astype(BF16)
        dck_ref[...] = dc_s[...]

    rowv = pl.BlockSpec((2, nq, 1, T), lambda h, j: (h, 0, 0, 0))
    tile = pl.BlockSpec((T, LANES), lambda h, j: (j, h))
    return pl.pallas_call(
        body, name=name, grid=(B_HEADS // 2, nq),
        in_specs=[pl.BlockSpec((2, T, LANES), lambda h, j: (h, j, 0)),
                  pl.BlockSpec((T, LANES), lambda h, j: (j, 8 + h)),
                  pl.BlockSpec((None, LANES, T), lambda h, j: (j, h, 0)),
                  pl.BlockSpec((2, S, LANES), lambda h, j: (h, 0, 0)),
                  pl.BlockSpec((S, LANES), lambda h, j: (0, h)),
                  rowv, rowv],
        out_specs=[tile, tile, pl.BlockSpec((2, T, 1), lambda h, j: (h, j, 0)),
                   pl.BlockSpec((nq, LANES, T), lambda h, j: (0, h, 0)), rowv],
        out_shape=[jax.ShapeDtypeStruct((S, A_WIDTH), BF16)] * 2 + [jax.ShapeDtypeStruct((B_HEADS, S, 1), F32),
                   jax.ShapeDtypeStruct((nq, A_WIDTH, T), F32), jax.ShapeDtypeStruct((B_HEADS, nq, 1, T), F32)],
        scratch_shapes=[pltpu.VMEM((2, T, LANES), F32), pltpu.VMEM((T, LANES), F32), pltpu.VMEM((2, T, 1), F32)],
        compiler_params=_cp(("parallel", "arbitrary")),
    )(kaug, ub, kt, qaug, do, lse, delta)


def _gate_fwd(o, ur, zcol, *, name):
    S = ur.shape[0]
    tm = min(1024, S)

    def body(o_ref, z_ref, y_ref):
        y_ref[...] = (o_ref[...] * _silu_parts(z_ref[...])[0]).astype(BF16)

    blk = pl.BlockSpec((tm, A_WIDTH), lambda i: (i, 0))
    return pl.pallas_call(
        body, name=name, grid=(S // tm,),
        in_specs=[blk, pl.BlockSpec((tm, A_WIDTH), lambda i: (i, zcol // A_WIDTH))],
        out_specs=blk, out_shape=jax.ShapeDtypeStruct((S, A_WIDTH), BF16),
        compiler_params=_cp(("parallel",)))(o, ur)


def _gate_bwd(o, ur, zcol, dy, *, name):
    S = ur.shape[0]
    tm = min(1024, S)

    def body(o_ref, z_ref, dy_ref, do_ref, dz_ref):
        sz, dsz = _silu_parts(z_ref[...])
        dyv = dy_ref[...]
        do_ref[...] = (dyv * sz).astype(BF16)
        dz_ref[...] = (dyv * o_ref[...] * dsz).astype(BF16)

    blk = pl.BlockSpec((tm, A_WIDTH), lambda i: (i, 0))
    return pl.pallas_call(
        body, name=name, grid=(S // tm,),
        in_specs=[blk, pl.BlockSpec((tm, A_WIDTH), lambda i: (i, zcol // A_WIDTH)), blk],
        out_specs=[blk, blk], out_shape=[jax.ShapeDtypeStruct((S, A_WIDTH), BF16)] * 2,
        compiler_params=_cp(("parallel",)))(o, ur, dy)


def _dfb(ur, bf_pad, dlogf_pad, *, name):
    S = ur.shape[0]
    tm = min(1024, S)

    def body(u_ref, b_ref, d_ref, o_ref, s_ref):
        i = pl.program_id(0)
        dv = d_ref[...] * _sigmoid(-(u_ref[...] + b_ref[...]))
        o_ref[...] = dv.astype(BF16)
        part = jnp.sum(dv, axis=0, keepdims=True)

        @pl.when(i == 0)
        def _():
            s_ref[...] = part

        @pl.when(i > 0)
        def _():
            s_ref[...] += part

    vec = pl.BlockSpec((1, FB_PAD), lambda i: (0, 0))
    blk = pl.BlockSpec((tm, FB_PAD), lambda i: (i, 0))
    return pl.pallas_call(
        body, name=name, grid=(S // tm,),
        in_specs=[pl.BlockSpec((tm, FB_PAD), lambda i: (i, R_FB // FB_PAD)), vec, blk],
        out_specs=[blk, vec],
        out_shape=[jax.ShapeDtypeStruct((S, FB_PAD), BF16), jax.ShapeDtypeStruct((1, FB_PAD), F32)],
        compiler_params=_cp(("arbitrary",)))(ur, bf_pad, dlogf_pad)


M_SCALE = HEAD ** -0.5


def _mem_fwd(ur, mkv, *, name):
    S = ur.shape[0]
    T = min(512, S)

    def body(q_ref, z_ref, k_ref, v_ref, y_ref):
        for h in range(M_HEADS):
            hs = slice(h * HEAD, (h + 1) * HEAD)
            s = _dot_nt(q_ref[:, hs].astype(BF16), k_ref[:, hs].astype(BF16)) * M_SCALE
            p = jnp.exp(s - jnp.max(s, axis=-1, keepdims=True))
            p = p / jnp.sum(p, axis=-1, keepdims=True)
            o = _dot(p.astype(BF16), v_ref[:, hs].astype(BF16))
            y_ref[:, hs] = (o * _silu_parts(z_ref[:, hs])[0]).astype(BF16)

    wide = lambda col: pl.BlockSpec((T, A_WIDTH), lambda i: (i, col // A_WIDTH))
    kv = lambda half: pl.BlockSpec((N_MEM, A_WIDTH), lambda i: (0, half))
    return pl.pallas_call(
        body, name=name, grid=(S // T,),
        in_specs=[wide(R_QM), wide(R_ZM), kv(0), kv(1)],
        out_specs=pl.BlockSpec((T, A_WIDTH), lambda i: (i, 0)),
        out_shape=jax.ShapeDtypeStruct((S, A_WIDTH), BF16),
        compiler_params=_cp(("parallel",)))(ur, ur, mkv, mkv)


def _mem_bwd(ur, mkv, dy, *, name):
    S = ur.shape[0]
    T = min(512, S)

    def body(q_ref, z_ref, k_ref, v_ref, dy_ref, dq_ref, dz_ref, dk_ref, dv_ref):
        i = pl.program_id(0)

        @pl.when(i == 0)
        def _():
            dk_ref[...] = jnp.zeros((N_MEM, A_WIDTH), F32)
            dv_ref[...] = jnp.zeros((N_MEM, A_WIDTH), F32)

        for h in range(M_HEADS):
            hs = slice(h * HEAD, (h + 1) * HEAD)
            qv = q_ref[:, hs].astype(BF16)
            kv = k_ref[:, hs].astype(BF16)
            vv = v_ref[:, hs].astype(BF16)
            s = _dot_nt(qv, kv) * M_SCALE
            p = jnp.exp(s - jnp.max(s, axis=-1, keepdims=True))
            p = p / jnp.sum(p, axis=-1, keepdims=True)
            o = _dot(p.astype(BF16), vv)
            sz, dsz = _silu_parts(z_ref[:, hs])
            dyv = dy_ref[:, hs]
            dz_ref[:, hs] = (dyv * o * dsz).astype(BF16)
            dov = (dyv * sz).astype(BF16)
            dp = _dot_nt(dov, vv)
            ds = p * (dp - jnp.sum(p * dp, axis=-1, keepdims=True))
            dq_ref[:, hs] = (_dot(ds.astype(BF16), kv) * M_SCALE).astype(BF16)
            dv_ref[:, hs] += _dot(p.T.astype(BF16), dov)
            dk_ref[:, hs] += _dot(ds.T.astype(BF16), qv) * M_SCALE

    wide = lambda col: pl.BlockSpec((T, A_WIDTH), lambda i: (i, col // A_WIDTH))
    kv = lambda half: pl.BlockSpec((N_MEM, A_WIDTH), lambda i: (0, half))
    tile = pl.BlockSpec((T, A_WIDTH), lambda i: (i, 0))
    acc = pl.BlockSpec((N_MEM, A_WIDTH), lambda i: (0, 0))
    return pl.pallas_call(
        body, name=name, grid=(S // T,),
        in_specs=[wide(R_QM), wide(R_ZM), kv(0), kv(1), tile],
        out_specs=[tile, tile, acc, acc],
        out_shape=[jax.ShapeDtypeStruct((S, A_WIDTH), BF16)] * 2
        + [jax.ShapeDtypeStruct((N_MEM, A_WIDTH), F32)] * 2,
        compiler_params=_cp(("arbitrary",)))(ur, ur, mkv, mkv, dy)


def _branch_fwd(ys, wbs, ur, b_merge, *, name):
    S = ur.shape[0]
    tm, tn = min(512, S), 512
    nj = D_MODEL // tn

    def body(ya, yb, ym, wa, wb, wm, g0, g1, g2, b0, b1, b2, mg_ref, p_ref):
        acc = jnp.zeros((tm, tn), F32)
        for i, (y, w, gr, br) in enumerate(((ya, wa, g0, b0), (yb, wb, g1, b1), (ym, wm, g2, b2))):
            pr = _dot(y[...], w[...])
            p_ref[i] = pr.astype(BF16)
            acc = acc + _sigmoid(gr[...] + br[...]) * pr
        mg_ref[...] = acc.astype(BF16)

    yspec = pl.BlockSpec((tm, A_WIDTH), lambda i, j: (i, 0))
    wspec = pl.BlockSpec((A_WIDTH, tn), lambda i, j: (0, j))
    gspec = lambda b: pl.BlockSpec((tm, tn), lambda i, j: (i, (R_GL + b * D_MODEL) // tn + j))
    bspec = lambda b: pl.BlockSpec((1, tn), lambda i, j: (0, b * nj + j))
    return pl.pallas_call(
        body, name=name, grid=(S // tm, nj),
        in_specs=[yspec] * 3 + [wspec] * 3 + [gspec(0), gspec(1), gspec(2), bspec(0), bspec(1), bspec(2)],
        out_specs=[pl.BlockSpec((tm, tn), lambda i, j: (i, j)),
                   pl.BlockSpec((3, tm, tn), lambda i, j: (0, i, j))],
        out_shape=[jax.ShapeDtypeStruct((S, D_MODEL), BF16), jax.ShapeDtypeStruct((3, S, D_MODEL), BF16)],
        compiler_params=_cp(("parallel", "parallel")))(*ys, *wbs, ur, ur, ur, b_merge, b_merge, b_merge)


def _branch_bwd(dm, prods, ur, b_merge, *, name):
    S = ur.shape[0]
    tm = min(256, S)

    def body(dm_ref, p_ref, g0, g1, g2, b_ref, dp_ref, dgl_ref, db_ref):
        i = pl.program_id(0)
        dmv = dm_ref[...]
        parts = []
        for b, gr in enumerate((g0, g1, g2)):
            sl = slice(b * D_MODEL, (b + 1) * D_MODEL)
            gt = _sigmoid(gr[...] + b_ref[:, sl])
            dp_ref[b] = (dmv * gt).astype(BF16)
            dgl = dmv * p_ref[b].astype(F32) * gt * (1.0 - gt)
            dgl_ref[:, sl] = dgl.astype(BF16)
            parts.append(jnp.sum(dgl, axis=0, keepdims=True))
        part = jnp.concatenate(parts, axis=1)

        @pl.when(i == 0)
        def _():
            db_ref[...] = part

        @pl.when(i > 0)
        def _():
            db_ref[...] += part

    gspec = lambda b: pl.BlockSpec((tm, D_MODEL), lambda i: (i, R_GL // D_MODEL + b))
    vec = pl.BlockSpec((1, 3 * D_MODEL), lambda i: (0, 0))
    return pl.pallas_call(
        body, name=name, grid=(S // tm,),
        in_specs=[pl.BlockSpec((tm, D_MODEL), lambda i: (i, 0)),
                  pl.BlockSpec((3, tm, D_MODEL), lambda i: (0, i, 0)), gspec(0), gspec(1), gspec(2), vec],
        out_specs=[pl.BlockSpec((3, tm, D_MODEL), lambda i: (0, i, 0)),
                   pl.BlockSpec((tm, 3 * D_MODEL), lambda i: (i, 0)), vec],
        out_shape=[jax.ShapeDtypeStruct((3, S, D_MODEL), BF16), jax.ShapeDtypeStruct((S, 3 * D_MODEL), BF16),
                   jax.ShapeDtypeStruct((1, 3 * D_MODEL), F32)],
        compiler_params=_cp(("arbitrary",)))(dm, prods, ur, ur, ur, b_merge)


def _rope_tables(pos):
    half = ROT // 2
    S = pos.shape[0]
    inv = ROPE_THETA ** (-jnp.arange(half, dtype=F32) / half)
    per_row = LANES // half
    ang = jnp.repeat(pos.astype(F32).reshape(S // per_row, per_row), half, axis=1) * jnp.tile(inv, per_row)
    cos, sin = jnp.cos(ang).reshape(S, half), jnp.sin(ang).reshape(S, half)
    one = jnp.ones((S, LANES - ROT), F32)
    zero = jnp.zeros((S, LANES - ROT), F32)
    zh = jnp.zeros((S, half), F32)
    c = jnp.concatenate([cos, cos, one], axis=1)
    s1 = jnp.concatenate([-sin, zh, zero], axis=1)
    s2 = jnp.concatenate([zh, sin, zero], axis=1)
    return c, s1, s2


def _to_tiles(t):
    S, H = t.shape
    return t.reshape(S // LANES, LANES, H).transpose(0, 2, 1)


def _from_tiles(t):
    nt, H, _ = t.shape
    return t.transpose(1, 0, 2).reshape(H, nt * LANES)


def _local_step(x, mem, pos, tgt, g_pre, g_post, g_mem, wt, bf_pad, b_merge, w_kv, wbs, w_out, pack=None):
    S = x.shape[0]
    T = min(512, S)
    nq = S // T
    tabs = _rope_tables(pos)

    h = _rms_fwd(x, g_pre, name="rms_pre")
    hs = [_to_classes(h, d) for d in DIL]
    tabs_g = [[_to_classes(t, d) for t in tabs] for d in DIL]
    qkvs = [_proj_rope(hs[g], wt[f"A{g}"], tabs_g[g], name=f"proj_a{g}") for g in range(3)]
    ub = _mm(h, wt["B"], bt=True, out_dtype=BF16, name="proj_b", tn=1536)
    ur = _mm(h, wt["R"], bt=True, name="proj_r", tn=1792)

    outs_c, lses_c = [], []
    for g in range(3):
        o, l = _attn_a_fwd(qkvs[g], g, name=f"attn_a_fwd{g}")
        outs_c.append(o)
        lses_c.append(l)
    outs_a = [_from_classes(o, d) for o, d in zip(outs_c, DIL)]
    lses_a = [_from_classes(l, d) for l, d in zip(lses_c, DIL)]
    ya = _merge_a_fwd(outs_a, lses_a, ur, name="merge_a_fwd")

    logf = _logf(ur, bf_pad, name="logf")
    c = _from_tiles(_cumsum_lanes(_to_tiles(logf[:, :B_HEADS]), False, name="cumsum_fwd"))
    ckb = jnp.broadcast_to(c[:, :, None], (B_HEADS, S, LANES))
    qaug, kaug = _fox_aug(ub, ckb, name="fox_aug")
    kt = ub[:, 512:1024].reshape(nq, T, 512).transpose(0, 2, 1)
    vt = ub[:, 1024:1536].reshape(nq, T, 512).transpose(0, 2, 1)
    ob, lse_b = _fox_fwd(qaug, kaug, vt, name="fox_fwd")
    yb = _gate_fwd(ob, ur, R_ZB, name="gate_b_fwd")

    hm = _rms_fwd(mem, g_mem, name="rms_mem")
    mkv = _mm(hm, w_kv, name="proj_mem")
    ym = _mem_fwd(ur, mkv, name="mem_fwd")

    merged, prods = _branch_fwd((ya, yb, ym), wbs, ur, b_merge, name="branch_fwd")
    out = _mm(merged, w_out, name="proj_out")
    dy, d_out, dg_post, loss_row = _post(x, out, tgt, g_post, name="post")

    dmerged = _mm(d_out, w_out, bt=True, name="d_merged")
    dw_out = _mm(merged, d_out, at=True, name="dw_out", tk=2048)
    dprods, dgl, db_merge = _branch_bwd(dmerged, prods, ur, b_merge, name="branch_bwd")
    dys, dwbs = [], []
    for i, (y, wb) in enumerate(zip((ya, yb, ym), wbs)):
        dys.append(_mm(dprods[i], wb, bt=True, name=f"d_y{i}"))
        dwbs.append(_mm(y, dprods[i], at=True, name=f"dw_branch{i}", tk=2048))

    dos_a, adjs_a, dza = _merge_a_bwd(outs_a, lses_a, ur, dys[0], name="merge_a_bwd")
    dus_a = []
    for g, d in enumerate(DIL):
        do_c, adj_c = _to_classes(dos_a[g], d), _to_classes(adjs_a[g], d)
        dq = _attn_a_dq(qkvs[g], tabs_g[g], g, do_c, lses_c[g], adj_c, name=f"attn_a_dq{g}")
        dk, dv = _attn_a_dkv(qkvs[g], tabs_g[g], g, do_c, lses_c[g], adj_c, name=f"attn_a_dkv{g}")
        dus_a.append(jnp.concatenate([dq, dk, dv], axis=1))

    dob, dzb = _gate_bwd(ob, ur, R_ZB, dys[1], name="gate_b_bwd")
    delta_b = _fox_delta(ob, dob, name="fox_delta")
    dkb, dvb, dc_k, dqt, dc_q = _fox_bwd(ub, qaug, kaug, kt, dob, lse_b, delta_b, name="fox_bwd")
    dqb = (dqt.transpose(0, 2, 1).reshape(S, A_WIDTH) * B_SCALE).astype(BF16)
    du_b = jnp.concatenate([dqb, dkb, dvb], axis=1)
    dc = dc_q.reshape(B_HEADS, S) + dc_k.reshape(B_HEADS, S)
    dlogf = _from_tiles(_cumsum_lanes(_to_tiles(dc.T), True, name="cumsum_bwd"))
    dlogf_pad = jnp.pad(dlogf.T, ((0, 0), (0, FB_PAD - B_HEADS)))
    dfb, db_forget = _dfb(ur, bf_pad, dlogf_pad, name="dfb")

    dqm, dzm, dmk, dmv = _mem_bwd(ur, mkv, dys[2], name="mem_bwd")
    dmkv = jnp.concatenate([dmk, dmv], axis=1).astype(BF16)
    dhm = _mm(dmkv, w_kv, bt=True, name="d_hm")
    dw_kv = _mm(hm, dmkv, at=True, name="dw_kv")
    dg_mem = _rms_bwd(mem, g_mem, dhm, None, name="rms_mem_bwd")

    du_r = jnp.concatenate([dza, dzb, dqm, dzm, dgl, dfb], axis=1)
    dwt = {"R": _mm(du_r, h, at=True, name="dw_in_r", tm=1792, tk=1024),
           "B": _mm(du_b, h, at=True, name="dw_in_b", tm=1536, tk=2048)}
    for g in range(3):
        dwt[f"A{g}"] = _mm(dus_a[g], hs[g], at=True, name=f"dw_in_a{g}", tm=1536, tk=2048)
    res = dict(dwt=dwt, dw_kv=dw_kv, dwbs=dwbs, dw_out=dw_out)
    token_major = [(du_r, wt["R"]), (du_b, wt["B"]), (dus_a[0], wt["A0"])]
    if pack is None:
        dh_1 = _mm(dus_a[1], wt["A1"], name="d_h_a1", tk=1536)
        dh = _mm_sum(token_major, name="d_h_main")
    else:
        gbig = pack(dwt, dw_kv, dwbs, dw_out)
        own_idx = _own_slabs()
        dh_1, sib = _mm(dus_a[1], wt["A1"], name="d_h_a1", tk=1536, comm=_pair_comm(gbig))
        send = _pair_sum(gbig, sib, own_idx, 208, name="pair_sum")
        dh, recv = _mm_sum(token_major, name="d_h_main", comm=_chips_comm(send))
        res = dict(parts=[(gbig, None), (sib, 1), (recv, N_CHIP - 1)], own_idx=own_idx)
    dh = dh + _from_classes(dh_1, DIL[1]) + _from_classes(_mm(dus_a[2], wt["A2"], name="d_h_a2", tk=1536), DIL[2])
    grad_x, dg_pre = _rms_bwd(x, g_pre, dh, dy, name="rms_pre_bwd")

    return dict(res, loss=loss_row, grad_x=grad_x, dg_pre=dg_pre, dg_post=dg_post, dg_mem=dg_mem,
                db_forget=db_forget, db_merge=db_merge)


MESH = pl.DeviceIdType.MESH
ANY = pl.BlockSpec(memory_space=pl.ANY)


def _relations():
    return [(k >> 2 & 1, k >> 1 & 1, k & 1) for k in range(1, N_DEV)]


def _coords():
    return lax.axis_index("x"), lax.axis_index("y"), lax.axis_index("c")


def _all_gather(shard, *, name):
    R, W = shard.shape

    def body(x_ref, out_ref, send_sems, recv_sems, local_sem):
        x, y, c = _coords()
        me, sibling = (x, y, c), (x, y, 1 - c)
        chips = [(1 - x, y), (x, 1 - y), (1 - x, 1 - y)]

        def slot(px, py, pc):
            return out_ref.at[4 * px + 2 * py + pc]

        def copy(k, block, to, src=None):
            return pltpu.make_async_remote_copy(
                src_ref=slot(*block) if src is None else src, dst_ref=slot(*block),
                send_sem=send_sems.at[k], recv_sem=recv_sems.at[k], device_id=to, device_id_type=MESH)

        mine = pltpu.make_async_copy(x_ref, slot(*me), local_sem)
        mine.start()
        first = [copy(0, me, sibling, src=x_ref)]
        first += [copy(1 + j, me, (*chip, c), src=x_ref) for j, chip in enumerate(chips)]
        for cp in first:
            cp.start()
        passed = [copy(4 + j, (*chip, c), sibling) for j, chip in enumerate(chips)]
        for j, chip in enumerate(chips):
            copy(1 + j, (*chip, c), me).wait_recv()
            passed[j].start()
        copy(0, sibling, me).wait_recv()
        for j, chip in enumerate(chips):
            copy(4 + j, (*chip, 1 - c), me).wait_recv()
        for cp in first + passed:
            cp.wait_send()
        mine.wait()

    return pl.pallas_call(
        body, name=name, out_shape=jax.ShapeDtypeStruct((N_DEV, R, W), shard.dtype),
        in_specs=[ANY], out_specs=ANY,
        scratch_shapes=[pltpu.SemaphoreType.DMA((N_DEV - 1,)), pltpu.SemaphoreType.DMA((N_DEV - 1,)),
                        pltpu.SemaphoreType.DMA],
    )(shard)


N_CHIP = 4


def _pair_comm(gbig):
    _, R, W = gbig.shape

    def copies(g_ref, sib_ref, send_sems, recv_sems):
        x, y, c = _coords()
        return [pltpu.make_async_remote_copy(
            src_ref=g_ref.at[4 * (x ^ (r >> 1)) + 2 * (y ^ (r & 1)) + (1 - c)], dst_ref=sib_ref.at[r],
            send_sem=send_sems.at[r], recv_sem=recv_sems.at[r], device_id=(x, y, 1 - c), device_id_type=MESH)
            for r in range(N_CHIP)]

    def start(*refs):
        for cp in copies(*refs):
            cp.start()

    def wait(*refs):
        cps = copies(*refs)
        for cp in cps:
            cp.wait_recv()
        for cp in cps:
            cp.wait_send()

    return dict(inputs=[gbig], out_shape=[jax.ShapeDtypeStruct((N_CHIP, R, W), gbig.dtype)],
                sems=[pltpu.SemaphoreType.DMA((N_CHIP,)), pltpu.SemaphoreType.DMA((N_CHIP,))],
                start=start, wait=wait)


def _own_slabs():
    x, y, c = _coords()
    return jnp.stack([4 * (x ^ (r >> 1)) + 2 * (y ^ (r & 1)) + c for r in range(N_CHIP)]).astype(jnp.int32)


def _pair_sum(gbig, sib, own_idx, tr, *, name):
    _, R, W = gbig.shape

    def body(idx_ref, a_ref, b_ref, o_ref):
        o_ref[...] = (a_ref[...] + b_ref[...]).astype(BF16)

    return pl.pallas_call(
        body, name=name,
        grid_spec=pltpu.PrefetchScalarGridSpec(
            num_scalar_prefetch=1, grid=(N_CHIP - 1, R // tr),
            in_specs=[pl.BlockSpec((None, tr, W), lambda r, i, idx: (idx[r + 1], i, 0)),
                      pl.BlockSpec((None, tr, W), lambda r, i, idx: (r + 1, i, 0))],
            out_specs=pl.BlockSpec((None, tr, W), lambda r, i, idx: (r, i, 0))),
        out_shape=jax.ShapeDtypeStruct((N_CHIP - 1, R, W), BF16),
        compiler_params=_cp(("parallel", "parallel")))(own_idx, gbig, sib)


def _chips_comm(send):
    nb, R, W = send.shape

    def copies(b_ref, rb_ref, send_sems, recv_sems):
        x, y, c = _coords()
        return [pltpu.make_async_remote_copy(
            src_ref=b_ref.at[r - 1], dst_ref=rb_ref.at[r - 1], send_sem=send_sems.at[r - 1],
            recv_sem=recv_sems.at[r - 1], device_id=(x ^ (r >> 1), y ^ (r & 1), c), device_id_type=MESH)
            for r in range(1, N_CHIP)]

    def start(*refs):
        for cp in copies(*refs):
            cp.start()

    def wait(*refs):
        cps = copies(*refs)
        for cp in cps:
            cp.wait_recv()
        for cp in cps:
            cp.wait_send()

    return dict(inputs=[send], out_shape=[jax.ShapeDtypeStruct((nb, R, W), send.dtype)],
                sems=[pltpu.SemaphoreType.DMA((nb,)), pltpu.SemaphoreType.DMA((nb,))],
                start=start, wait=wait)


def _gather_small(gsmall, *, name):
    n = N_DEV - 1

    def body(s_ref, rs_ref, send_sems, recv_sems, local_sem):
        x, y, c = _coords()
        me = 4 * x + 2 * y + c
        mine = pltpu.make_async_copy(s_ref, rs_ref.at[me], local_sem)
        mine.start()

        def copy(k, fx, fy, fc, slot):
            return pltpu.make_async_remote_copy(
                src_ref=s_ref, dst_ref=rs_ref.at[slot], send_sem=send_sems.at[k], recv_sem=recv_sems.at[k],
                device_id=(x ^ fx, y ^ fy, c ^ fc), device_id_type=MESH)

        started = [copy(k, *rel, me) for k, rel in enumerate(_relations())]
        for cp in started:
            cp.start()
        for k, (fx, fy, fc) in enumerate(_relations()):
            copy(k, fx, fy, fc, 4 * (x ^ fx) + 2 * (y ^ fy) + (c ^ fc)).wait_recv()
        for cp in started:
            cp.wait_send()
        mine.wait()

    return pl.pallas_call(
        body, name=name, out_shape=jax.ShapeDtypeStruct((N_DEV, 1, P_SMALL), gsmall.dtype),
        in_specs=[ANY], out_specs=ANY,
        scratch_shapes=[pltpu.SemaphoreType.DMA((n,)), pltpu.SemaphoreType.DMA((n,)), pltpu.SemaphoreType.DMA],
    )(gsmall)


def _part_specs(parts, tr, row0):
    assert row0 % tr == 0
    specs = []
    for a, n_used in parts:
        if n_used is None:
            specs.append(pl.BlockSpec((1, tr, a.shape[2]), lambda i, idx: (idx[0], row0 // tr + i, 0)))
        else:
            specs.append(pl.BlockSpec((n_used, tr, a.shape[2]), lambda i, idx: (0, row0 // tr + i, 0)))
    return specs


def _part_total(refs, parts):
    g = None
    for ref, (_, n_used) in zip(refs, parts):
        for k in range(n_used or 1):
            t = ref[k].astype(F32)
            g = t if g is None else g + t
    return g


def _sum_parts(parts, idx, row0, nrows, tr, *, name):
    W = parts[0][0].shape[2]
    assert nrows % tr == 0

    def body(idx_ref, *refs):
        refs[-1][...] = _part_total(refs[:-1], parts)

    return pl.pallas_call(
        body, name=name,
        grid_spec=pltpu.PrefetchScalarGridSpec(
            num_scalar_prefetch=1, grid=(nrows // tr,), in_specs=_part_specs(parts, tr, row0),
            out_specs=pl.BlockSpec((tr, W), lambda i, idx: (i, 0))),
        out_shape=jax.ShapeDtypeStruct((nrows, W), F32),
        compiler_params=_cp(("parallel",)))(idx, *[a for a, _ in parts])


def _adamw(parts, idx, w, m, v, tr, *, name):
    R, W = w.shape
    assert R % tr == 0
    np_ = len(parts)

    def body(idx_ref, *refs):
        w_ref, m_ref, v_ref, g_ref, d_ref, nm_ref, nv_ref = refs[np_:]
        g = _part_total(refs[:np_], parts)
        mm = ADAM_B1 * m_ref[...] + (1.0 - ADAM_B1) * g
        vv = ADAM_B2 * v_ref[...] + (1.0 - ADAM_B2) * (g * g)
        m_hat = mm / (1.0 - ADAM_B1 ** ADAM_STEP)
        v_hat = vv / (1.0 - ADAM_B2 ** ADAM_STEP)
        g_ref[...] = g
        d_ref[...] = -ADAM_LR * (m_hat / (jnp.sqrt(v_hat) + ADAM_EPS) + ADAM_WD * w_ref[...])
        nm_ref[...] = mm
        nv_ref[...] = vv

    blk = pl.BlockSpec((tr, W), lambda i, idx: (i, 0))
    return pl.pallas_call(
        body, name=name,
        grid_spec=pltpu.PrefetchScalarGridSpec(
            num_scalar_prefetch=1, grid=(R // tr,), in_specs=_part_specs(parts, tr, 0) + [blk, blk, blk],
            out_specs=[blk] * 4),
        out_shape=[jax.ShapeDtypeStruct((R, W), F32)] * 4,
        compiler_params=_cp(("parallel",)))(idx, *[a for a, _ in parts], w, m, v)


def _pack_rest(w_kv, wa, wb, wm, w_out):
    return jnp.concatenate([w_kv[0], w_out[0]] + [t[0].reshape(-1, D_MODEL) for t in (wa, wb, wm)], axis=0)


def _unpack_rest(t):
    br = lambda i: t[RO_BR + 64 * i:RO_BR + 64 * (i + 1)].reshape(1, A_WIDTH, D_MODEL // N_DEV)
    return t[None, RO_KV:RO_OUT], br(0), br(1), br(2), t[None, RO_OUT:RO_BR]


def _orig_rows(gathered, a, b):
    res = []
    while a < b:
        dev, r = divmod(a, CS)
        n = min(b - a, CS - r)
        res.append(gathered[dev, RO_IN + r:RO_IN + r + n])
        a += n
    return res


def _full_weights(gathered):
    wt = {}
    for name, ranges in SEGS.items():
        rows = [p for a, b in ranges for p in _orig_rows(gathered, a, b)]
        if SEG_PAD[name]:
            rows.append(jnp.zeros((SEG_PAD[name], D_MODEL), gathered.dtype))
        wt[name] = jnp.concatenate(rows, axis=0)
    w_kv = gathered[:, RO_KV:RO_OUT].reshape(D_MODEL, D_MODEL)
    w_out = gathered[:, RO_OUT:RO_BR].reshape(D_MODEL, D_MODEL)
    wbs = [gathered[:, RO_BR + 64 * i:RO_BR + 64 * (i + 1)].reshape(N_DEV, A_WIDTH, D_MODEL // N_DEV)
           .transpose(1, 0, 2).reshape(A_WIDTH, D_MODEL) for i in range(3)]
    return wt, w_kv, wbs, w_out


def _orig_order(dwt):
    pieces = []
    for name, ranges in SEGS.items():
        o = 0
        for a, b in ranges:
            pieces.append((a, dwt[name][o:o + b - a]))
            o += b - a
    pieces.sort(key=lambda p: p[0])
    return jnp.concatenate([p[1] for p in pieces], axis=0)


def _pack_grads(dwt, dw_kv, dwbs, dw_out):
    g_in = jnp.pad(_orig_order(dwt).reshape(N_DEV, CS, D_MODEL), ((0, 0), (0, IN_ROWS - CS), (0, 0)))
    br = [t.reshape(A_WIDTH, N_DEV, D_MODEL // N_DEV).transpose(1, 0, 2).reshape(N_DEV, -1, D_MODEL) for t in dwbs]
    return jnp.concatenate([dw_kv.reshape(N_DEV, -1, D_MODEL), dw_out.reshape(N_DEV, -1, D_MODEL)] + br + [g_in],
                           axis=1)


def kernel(x, mem, positions, norm_pre_g, norm_post_g, norm_mem_g, w_in, b_forget, b_merge, w_mem_kv, w_branch_a, w_branch_b, w_branch_m, w_out, loss_target, m_norm_pre_g, m_norm_post_g, m_norm_mem_g, m_w_in, m_b_forget, m_b_merge, m_w_mem_kv, m_w_branch_a, m_w_branch_b, m_w_branch_m, m_w_out, v_norm_pre_g, v_norm_post_g, v_norm_mem_g, v_w_in, v_b_forget, v_b_merge, v_w_mem_kv, v_w_branch_a, v_w_branch_b, v_w_branch_m, v_w_out):
    w_rest = _pack_rest(w_mem_kv, w_branch_a, w_branch_b, w_branch_m, w_out)
    shard = jnp.concatenate([w_rest.astype(BF16), w_in[0].T.astype(BF16),
                             jnp.zeros((IN_ROWS - CS, D_MODEL), BF16)], axis=0)
    gathered = _all_gather(shard, name="gather_weights")
    wt, w_kv, wbs, w_o = _full_weights(gathered)

    bf_pad = jnp.pad(b_forget, ((0, 0), (0, FB_PAD - B_HEADS)))
    r = _local_step(x[0], mem[0], positions[0], loss_target[0], norm_pre_g, norm_post_g, norm_mem_g,
                    wt, bf_pad, b_merge, w_kv, wbs, w_o, pack=_pack_grads)

    gsmall = jnp.concatenate([r["dg_pre"], r["dg_post"], r["dg_mem"], r["db_merge"],
                              r["db_forget"][:, :LANES], r["loss"]], axis=1)
    rsmall = _gather_small(gsmall, name="gather_small")
    parts, own_idx = r["parts"], r["own_idx"]

    m_rest = _pack_rest(m_w_mem_kv, m_w_branch_a, m_w_branch_b, m_w_branch_m, m_w_out)
    v_rest = _pack_rest(v_w_mem_kv, v_w_branch_a, v_w_branch_b, v_w_branch_m, v_w_out)
    outs_rest = [_unpack_rest(t) for t in _adamw(parts, own_idx, w_rest, m_rest, v_rest, 64, name="adamw_rest")]
    g_in = _sum_parts(parts, own_idx, RO_IN, IN_ROWS, 16, name="sum_w_in")[:CS].T
    outs_in = _adamw([(g_in[None], 1)], own_idx, w_in[0], m_w_in[0], v_w_in[0], 128, name="adamw_w_in")

    def small_vec(a, b, c, d, e):
        z = jnp.zeros((1, LANES - B_HEADS), F32)
        return jnp.concatenate([a, b, c, d, e, z, jnp.zeros((1, LANES), F32)], axis=1)

    outs_small = _adamw([(rsmall, N_DEV)], own_idx, small_vec(norm_pre_g, norm_post_g, norm_mem_g, b_merge, b_forget),
                        small_vec(m_norm_pre_g, m_norm_post_g, m_norm_mem_g, m_b_merge, m_b_forget),
                        small_vec(v_norm_pre_g, v_norm_post_g, v_norm_mem_g, v_b_merge, v_b_forget),
                        1, name="adamw_small")

    def small_parts(t):
        return [t[:, O_GPRE:O_GPRE + D_MODEL], t[:, O_GPOST:O_GPOST + D_MODEL], t[:, O_GMEM:O_GMEM + D_MODEL],
                t[:, O_BF:O_BF + B_HEADS], t[:, O_BM:O_BM + 3 * D_MODEL]]

    loss = outs_small[0][0, O_LOSS]
    result = [loss, r["grad_x"][None]]
    for rest, w_i, small in zip(outs_rest, outs_in, outs_small):
        gp, gq, gm, bf, bm = small_parts(small)
        w_k, w_a, w_b, w_m, w_ot = rest
        result += [gp, gq, gm, w_i[None], bf, bm, w_k, w_a, w_b, w_m, w_ot]
    return tuple(result)
```

```python
import jax
import jax.numpy as jnp
from jax import lax
from jax.experimental import pallas as pl
from jax.experimental.pallas import tpu as pltpu

F32 = jnp.float32
BF16 = jnp.bfloat16

N_DEV = 8
D_MODEL = 1024
N_MEM = 256
EPS = 1e-6
NEG = -1e30
ROPE_THETA = 500000.0
DIL = (1, 4, 16)
A_HEADS = 4
HEAD = 128
A_WIDTH = 512
B_HEADS = 8
B_HEAD = 64
M_HEADS = 4
ROT = 32
IN_COLS = 11272
FB_PAD = 256

SEGS = {
    "A0": ((0, 512), (1536, 2048), (3072, 3584)),
    "A1": ((512, 1024), (2048, 2560), (3584, 4096)),
    "A2": ((1024, 1536), (2560, 3072), (4096, 4608)),
    "B": ((5120, 6656),),
    "R": ((4608, 5120), (6664, 7176), (7176, 7688), (7688, 8200), (8200, 11272), (6656, 6664)),
}
SEG_PAD = {"A0": 0, "A1": 0, "A2": 0, "B": 0, "R": FB_PAD - B_HEADS}
R_ZA, R_ZB, R_QM, R_ZM, R_GL, R_FB = 0, 512, 1024, 1536, 2048, 5120
NR = R_FB + FB_PAD

ADAM_LR, ADAM_B1, ADAM_B2, ADAM_EPS, ADAM_WD, ADAM_STEP = 0.001, 0.9, 0.999, 1e-08, 0.01, 10

LANES = 128
VMEM_LIMIT = 56 * 1024 * 1024

CS = IN_COLS // N_DEV
RO_KV, RO_OUT, RO_BR, RO_IN = 0, 128, 256, 448
IN_ROWS = 1424
ROWS = RO_IN + IN_ROWS
O_GPRE, O_GPOST, O_GMEM, O_BM, O_BF, O_LOSS = 0, 1024, 2048, 3072, 6144, 6272
P_SMALL = 6400


def _cp(sem=None):
    return pltpu.CompilerParams(dimension_semantics=sem, vmem_limit_bytes=VMEM_LIMIT)


def _dot(a, b):
    return jnp.dot(a, b, preferred_element_type=F32)


def _dot_nt(a, b):
    return lax.dot_general(a, b, (((1,), (1,)), ((), ())), preferred_element_type=F32)


def _sigmoid(z):
    return 1.0 / (1.0 + jnp.exp(-z))


def _mm(a, b, *, name, at=False, bt=False, out_dtype=F32, tm=1024, tn=1024, tk=None, comm=None):
    assert not (at and bt)
    K, M = a.shape if at else a.shape[::-1]
    N = b.shape[0] if bt else b.shape[1]
    tm, tn = min(tm, M), min(tn, N)
    tk = K if tk is None else min(tk, K)
    assert M % tm == 0 and N % tn == 0 and K % tk == 0
    nk = K // tk
    grid = (M // tm, N // tn, nk)
    n_in = len(comm["inputs"]) if comm else 0
    n_out = len(comm["out_shape"]) if comm else 0

    def body(a_ref, b_ref, *rest):
        c_in, o_ref, c_out = rest[:n_in], rest[n_in], rest[n_in + 1:n_in + 1 + n_out]
        acc_ref, sems = rest[n_in + 1 + n_out], rest[n_in + 2 + n_out:]
        if comm:
            step = (pl.program_id(0) * grid[1] + pl.program_id(1)) * grid[2] + pl.program_id(2)

            @pl.when(step == 0)
            def _():
                comm["start"](*c_in, *c_out, *sems)

        av = a_ref[...].astype(BF16)
        bv = b_ref[...].astype(BF16)
        if at:
            p = lax.dot_general(av, bv, (((0,), (0,)), ((), ())), preferred_element_type=F32)
        else:
            p = _dot_nt(av, bv) if bt else _dot(av, bv)
        if nk == 1:
            o_ref[...] = p.astype(out_dtype)
        else:
            k = pl.program_id(2)

            @pl.when(k == 0)
            def _():
                acc_ref[...] = p

            @pl.when(k > 0)
            def _():
                acc_ref[...] += p

            @pl.when(k == nk - 1)
            def _():
                o_ref[...] = acc_ref[...].astype(out_dtype)

        if comm:
            @pl.when(step == grid[0] * grid[1] * grid[2] - 1)
            def _():
                comm["wait"](*c_in, *c_out, *sems)

    b_spec = (pl.BlockSpec((tn, tk), lambda i, j, k: (j, k)) if bt
              else pl.BlockSpec((tk, tn), lambda i, j, k: (k, j)))
    a_spec = (pl.BlockSpec((tk, tm), lambda i, j, k: (k, i)) if at
              else pl.BlockSpec((tm, tk), lambda i, j, k: (i, k)))
    out_spec = pl.BlockSpec((tm, tn), lambda i, j, k: (i, j))
    out_shape = jax.ShapeDtypeStruct((M, N), out_dtype)
    acc = pltpu.VMEM((tm, tn) if nk > 1 else (8, LANES), F32)
    if not comm:
        return pl.pallas_call(
            body, name=name, grid=grid, in_specs=[a_spec, b_spec], out_specs=out_spec, out_shape=out_shape,
            scratch_shapes=[acc], compiler_params=_cp(("parallel", "parallel", "arbitrary")))(a, b)
    return pl.pallas_call(
        body, name=name, grid=grid, in_specs=[a_spec, b_spec] + [ANY] * n_in,
        out_specs=[out_spec] + [ANY] * n_out, out_shape=[out_shape] + comm["out_shape"],
        scratch_shapes=[acc] + comm["sems"],
        compiler_params=_cp(("arbitrary", "arbitrary", "arbitrary")))(a, b, *comm["inputs"])


def _mm_sum(pairs, *, name, tm=1024, tk=768, comm=None):
    M, N = pairs[0][0].shape[0], pairs[0][1].shape[1]
    tm = min(tm, M)
    steps = [a.shape[1] // tk for a, _ in pairs]
    assert M % tm == 0 and all(a.shape[1] % tk == 0 for a, _ in pairs)
    first = [sum(steps[:p]) for p in range(len(pairs))]
    total = sum(steps)
    grid = (M // tm, total)
    n_in = len(comm["inputs"]) if comm else 0
    n_out = len(comm["out_shape"]) if comm else 0
    npair = len(pairs)

    def body(*refs):
        ab, rest = refs[:2 * npair], refs[2 * npair:]
        c_in, o_ref, c_out = rest[:n_in], rest[n_in], rest[n_in + 1:n_in + 1 + n_out]
        acc_ref, sems = rest[n_in + 1 + n_out], rest[n_in + 2 + n_out:]
        k = pl.program_id(1)
        if comm:
            step = pl.program_id(0) * total + k

            @pl.when(step == 0)
            def _():
                comm["start"](*c_in, *c_out, *sems)

        @pl.when(k == 0)
        def _():
            acc_ref[...] = jnp.zeros((tm, N), F32)

        for p in range(npair):
            @pl.when(jnp.logical_and(k >= first[p], k < first[p] + steps[p]))
            def _(p=p):
                acc_ref[...] += _dot(ab[2 * p][...], ab[2 * p + 1][...])

        @pl.when(k == total - 1)
        def _():
            o_ref[...] = acc_ref[...]

        if comm:
            @pl.when(step == grid[0] * total - 1)
            def _():
                comm["wait"](*c_in, *c_out, *sems)

    def local(p):
        return lambda k: jnp.clip(k - first[p], 0, steps[p] - 1)

    in_specs = []
    for p in range(npair):
        in_specs += [pl.BlockSpec((tm, tk), lambda i, k, f=local(p): (i, f(k))),
                     pl.BlockSpec((tk, N), lambda i, k, f=local(p): (f(k), 0))]
    out_spec = pl.BlockSpec((tm, N), lambda i, k: (i, 0))
    out_shape = jax.ShapeDtypeStruct((M, N), F32)
    args = [t for pair in pairs for t in pair]
    if not comm:
        return pl.pallas_call(
            body, name=name, grid=grid, in_specs=in_specs, out_specs=out_spec, out_shape=out_shape,
            scratch_shapes=[pltpu.VMEM((tm, N), F32)], compiler_params=_cp(("parallel", "arbitrary")))(*args)
    return pl.pallas_call(
        body, name=name, grid=grid, in_specs=in_specs + [ANY] * n_in,
        out_specs=[out_spec] + [ANY] * n_out, out_shape=[out_shape] + comm["out_shape"],
        scratch_shapes=[pltpu.VMEM((tm, N), F32)] + comm["sems"],
        compiler_params=_cp(("arbitrary", "arbitrary")))(*args, *comm["inputs"])


def _class_spec(S, d, tm, width):
    return pl.BlockSpec((d, tm // d, width), lambda i: (0, i, 0))


def _rms_fwd(x, g, *, name, dilations=()):
    S, D = x.shape
    tm = min(512, S)
    ds = [d for d in dilations if d > 1]

    def body(x_ref, g_ref, o_ref, *rest):
        xv = x_ref[...]
        r = lax.rsqrt(jnp.mean(xv * xv, axis=-1, keepdims=True) + EPS)
        hv = xv * r * g_ref[...]
        o_ref[...] = hv.astype(BF16)
        if ds:
            tmps = rest[len(ds):]
            for c, tmp in enumerate(tmps):
                tmp[...] = hv[:, c * LANES:(c + 1) * LANES]
            for c_ref, d in zip(rest, ds):
                for k in range(d):
                    c_ref[k] = jnp.concatenate([tmp[pl.ds(k, tm // d, stride=d), :] for tmp in tmps],
                                               axis=1).astype(BF16)

    row = pl.BlockSpec((tm, D), lambda i: (i, 0))
    outs = pl.pallas_call(
        body, name=name, grid=(S // tm,),
        in_specs=[row, pl.BlockSpec((1, D), lambda i: (0, 0))],
        out_specs=[row] + [_class_spec(S, d, tm, D) for d in ds],
        out_shape=[jax.ShapeDtypeStruct((S, D), BF16)] + [jax.ShapeDtypeStruct((d, S // d, D), BF16) for d in ds],
        scratch_shapes=[pltpu.VMEM((tm, LANES), F32)] * (D // LANES) if ds else [],
        compiler_params=_cp(("parallel",)),
    )(x, g)
    return [outs[0]] + [o.reshape(S, D) for o in outs[1:]] if ds else outs[0]


def _rms_bwd(x, g, dh, dy, *, name, dh_classes=()):
    S, D = x.shape
    tm = min(512, S)
    want_dx = dy is not None
    nc = len(dh_classes)

    def body(*refs):
        c_refs, refs = refs[:nc], refs[nc:]
        if want_dx:
            x_ref, g_ref, dh_ref, dy_ref, dx_ref, dg_ref = refs[:6]
        else:
            x_ref, g_ref, dh_ref, dg_ref = refs[:4]
        i = pl.program_id(0)
        xv = x_ref[...]
        r = lax.rsqrt(jnp.mean(xv * xv, axis=-1, keepdims=True) + EPS)
        xh = xv * r
        if nc:
            tmps = refs[-(D // LANES):]
            cols = [slice(c * LANES, (c + 1) * LANES) for c in range(D // LANES)]
            for tmp, cs in zip(tmps, cols):
                tmp[...] = dh_ref[:, cs]
            for c_ref, (_, d) in zip(c_refs, dh_classes):
                for k in range(d):
                    for tmp, cs in zip(tmps, cols):
                        tmp[pl.ds(k, tm // d, stride=d), :] += c_ref[k, :, cs]
            dhv = jnp.concatenate([tmp[...] for tmp in tmps], axis=1)
        else:
            dhv = dh_ref[...]
        part = jnp.sum(dhv * xh, axis=0, keepdims=True)

        @pl.when(i == 0)
        def _():
            dg_ref[...] = part

        @pl.when(i > 0)
        def _():
            dg_ref[...] += part

        if want_dx:
            dxh = dhv * g_ref[...]
            dx_ref[...] = dy_ref[...] + r * (dxh - xh * jnp.mean(dxh * xh, axis=-1, keepdims=True))

    row = pl.BlockSpec((tm, D), lambda i: (i, 0))
    vec = pl.BlockSpec((1, D), lambda i: (0, 0))
    c_specs = [_class_spec(S, d, tm, D) for _, d in dh_classes]
    c_args = [a.reshape(d, S // d, D) for a, d in dh_classes]
    scratch = [pltpu.VMEM((tm, LANES), F32)] * (D // LANES) if nc else []
    if want_dx:
        return pl.pallas_call(
            body, name=name, grid=(S // tm,), in_specs=c_specs + [row, vec, row, row], out_specs=[row, vec],
            out_shape=[jax.ShapeDtypeStruct((S, D), F32), jax.ShapeDtypeStruct((1, D), F32)],
            scratch_shapes=scratch, compiler_params=_cp(("arbitrary",)))(*c_args, x, g, dh, dy)
    return pl.pallas_call(
        body, name=name, grid=(S // tm,), in_specs=c_specs + [row, vec, row], out_specs=vec,
        out_shape=jax.ShapeDtypeStruct((1, D), F32),
        scratch_shapes=scratch, compiler_params=_cp(("arbitrary",)))(*c_args, x, g, dh)


def _post(x, out, tgt, g, *, name):
    S, D = x.shape
    tm = min(512, S)

    def body(x_ref, o_ref, t_ref, g_ref, dy_ref, do_ref, dg_ref, loss_ref):
        i = pl.program_id(0)
        ov = o_ref[...]
        r = lax.rsqrt(jnp.mean(ov * ov, axis=-1, keepdims=True) + EPS)
        n = ov * r
        gv = g_ref[...]
        e = (x_ref[...] + n * gv) - t_ref[...]
        lpart = 0.5 * jnp.sum(jnp.mean(e * e, axis=-1, keepdims=True), axis=0, keepdims=True)
        dy = e * (1.0 / D)
        dy_ref[...] = dy
        dn = dy * gv
        do_ref[...] = (r * (dn - n * jnp.mean(dn * n, axis=-1, keepdims=True))).astype(BF16)
        gpart = jnp.sum(dy * n, axis=0, keepdims=True)
        lrow = jnp.broadcast_to(lpart, (1, LANES))

        @pl.when(i == 0)
        def _():
            dg_ref[...] = gpart
            loss_ref[...] = lrow

        @pl.when(i > 0)
        def _():
            dg_ref[...] += gpart
            loss_ref[...] += lrow

    row = pl.BlockSpec((tm, D), lambda i: (i, 0))
    vec = pl.BlockSpec((1, D), lambda i: (0, 0))
    return pl.pallas_call(
        body, name=name, grid=(S // tm,), in_specs=[row, row, row, vec],
        out_specs=[row, row, vec, pl.BlockSpec((1, LANES), lambda i: (0, 0))],
        out_shape=[jax.ShapeDtypeStruct((S, D), F32), jax.ShapeDtypeStruct((S, D), BF16),
                   jax.ShapeDtypeStruct((1, D), F32), jax.ShapeDtypeStruct((1, LANES), F32)],
        compiler_params=_cp(("arbitrary",)))(x, out, tgt, g)


def _to_classes(t, d):
    if d == 1:
        return t
    S, C = t.shape
    return t.reshape(S // d, d, C).transpose(1, 0, 2).reshape(S, C)


def _from_classes(t, d):
    if d == 1:
        return t
    S, C = t.shape
    return t.reshape(d, S // d, C).transpose(1, 0, 2).reshape(S, C)


def _rope(x, c, s1, s2):
    return x * c + pltpu.roll(x, LANES - ROT // 2, 1) * s1 + pltpu.roll(x, ROT // 2, 1) * s2


def _unrope(d, c, s1, s2):
    return d * c + pltpu.roll(d * s1, ROT // 2, 1) + pltpu.roll(d * s2, LANES - ROT // 2, 1)


def _a_band(qb):
    r = lax.broadcasted_iota(jnp.int32, (qb, qb + HEAD), 0)
    c = lax.broadcasted_iota(jnp.int32, (qb, qb + HEAD), 1)
    return jnp.logical_and(c >= r, c <= r + HEAD)


def _a_first_ok(qb, n):
    c = lax.broadcasted_iota(jnp.int32, (qb, qb + HEAD), 1)
    return jnp.logical_or(c >= HEAD, n > 0)


def _a_last_ok(qb, has_next):
    c = lax.broadcasted_iota(jnp.int32, (qb, qb + HEAD), 1)
    return jnp.logical_or(c < qb, has_next)


A_SCALE = HEAD ** -0.5


def _a_geometry(S, g):
    d = DIL[g]
    L = S // d
    TQ = min(512, L)
    return d, L, TQ, TQ // HEAD, L // TQ, L // HEAD


def _proj_rope(h, w, tabs, *, name):
    S, D = h.shape
    tm = min(512, S)

    def body(h_ref, w_ref, c_ref, s1_ref, s2_ref, o_ref):
        tc = (c_ref[...], s1_ref[...], s2_ref[...])
        u = _dot_nt(h_ref[...], w_ref[...])
        for j in range(3 * A_HEADS):
            sl = slice(j * HEAD, (j + 1) * HEAD)
            o_ref[:, sl] = (_rope(u[:, sl], *tc) if j < 2 * A_HEADS else u[:, sl]).astype(BF16)

    tab = pl.BlockSpec((tm, LANES), lambda i: (i, 0))
    return pl.pallas_call(
        body, name=name, grid=(S // tm,),
        in_specs=[pl.BlockSpec((tm, D), lambda i: (i, 0)), pl.BlockSpec((3 * A_WIDTH, D), lambda i: (0, 0)),
                  tab, tab, tab],
        out_specs=pl.BlockSpec((tm, 3 * A_WIDTH), lambda i: (i, 0)),
        out_shape=jax.ShapeDtypeStruct((S, 3 * A_WIDTH), BF16),
        compiler_params=_cp(("parallel",)))(h, w, *tabs)


def _attn_a_fwd(qkv, g, *, name):
    S = qkv.shape[0]
    d, L, TQ, nsub, nb, nblk = _a_geometry(S, g)

    def body(q_ref, kc_ref, kp_ref, vc_ref, vp_ref, o_ref, l_ref):
        n = pl.program_id(1)
        QB = min(2 * HEAD, TQ)
        band = _a_band(QB)
        first = jnp.logical_and(band, _a_first_ok(QB, n))
        for h in range(A_HEADS):
            hs = slice(h * HEAD, (h + 1) * HEAD)
            for hh in range(TQ // QB):
                sl = slice(hh * QB, (hh + 1) * QB)
                pv = slice(hh * QB - HEAD, hh * QB)
                kcat = jnp.concatenate([kp_ref[:, hs] if hh == 0 else kc_ref[pv, hs], kc_ref[sl, hs]], axis=0)
                vcat = jnp.concatenate([vp_ref[:, hs] if hh == 0 else vc_ref[pv, hs], vc_ref[sl, hs]], axis=0)
                s = jnp.where(first if hh == 0 else band, _dot_nt(q_ref[sl, hs], kcat) * A_SCALE, NEG)
                m = jnp.max(s, axis=-1, keepdims=True)
                p = jnp.exp(s - m)
                den = jnp.sum(p, axis=-1, keepdims=True)
                o_ref[sl, hs] = _dot(p.astype(BF16), vcat) / den
                l_ref[sl, hs] = jnp.broadcast_to(m + jnp.log(den), (QB, HEAD))

    rcur = lambda r, n: r * nb + n
    rprv = lambda r, n: r * nblk + jnp.maximum(n * nsub - 1, 0)
    cur = lambda off: pl.BlockSpec((TQ, A_WIDTH), lambda r, n: (rcur(r, n), off))
    prv = lambda off: pl.BlockSpec((HEAD, A_WIDTH), lambda r, n: (rprv(r, n), off))
    out = pl.BlockSpec((TQ, A_WIDTH), lambda r, n: (rcur(r, n), 0))
    return pl.pallas_call(
        body, name=name, grid=(d, nb),
        in_specs=[cur(0), cur(1), prv(1), cur(2), prv(2)],
        out_specs=[out, out],
        out_shape=[jax.ShapeDtypeStruct((S, A_WIDTH), F32)] * 2,
        compiler_params=_cp(("parallel", "parallel")),
    )(qkv, qkv, qkv, qkv, qkv)


def _attn_a_dq(qkv, tabs, g, do, lse, adj, *, name):
    S = qkv.shape[0]
    d, L, TQ, nsub, nb, nblk = _a_geometry(S, g)

    def body(q_ref, kc_ref, kp_ref, vc_ref, vp_ref, do_ref, l_ref, adj_ref, c_ref, s1_ref, s2_ref, dq_ref):
        n = pl.program_id(1)
        QB = min(2 * HEAD, TQ)
        band = _a_band(QB)
        first = jnp.logical_and(band, _a_first_ok(QB, n))
        for h in range(A_HEADS):
            hs = slice(h * HEAD, (h + 1) * HEAD)
            for hh in range(TQ // QB):
                sl = slice(hh * QB, (hh + 1) * QB)
                pv = slice(hh * QB - HEAD, hh * QB)
                kcat = jnp.concatenate([kp_ref[:, hs] if hh == 0 else kc_ref[pv, hs], kc_ref[sl, hs]], axis=0)
                vcat = jnp.concatenate([vp_ref[:, hs] if hh == 0 else vc_ref[pv, hs], vc_ref[sl, hs]], axis=0)
                s = jnp.where(first if hh == 0 else band, _dot_nt(q_ref[sl, hs], kcat) * A_SCALE, NEG)
                p = jnp.exp(s - l_ref[sl, hs][:, :1])
                ds = p * (_dot_nt(do_ref[sl, hs], vcat) + adj_ref[sl, hs][:, :1])
                dq = _dot(ds.astype(BF16), kcat) * A_SCALE
                dq_ref[sl, hs] = _unrope(dq, c_ref[sl, :], s1_ref[sl, :], s2_ref[sl, :]).astype(BF16)

    rcur = lambda r, n: r * nb + n
    rprv = lambda r, n: r * nblk + jnp.maximum(n * nsub - 1, 0)
    cur = lambda off: pl.BlockSpec((TQ, A_WIDTH), lambda r, n: (rcur(r, n), off))
    prv = lambda off: pl.BlockSpec((HEAD, A_WIDTH), lambda r, n: (rprv(r, n), off))
    tcur = pl.BlockSpec((TQ, LANES), lambda r, n: (rcur(r, n), 0))
    blk = cur(0)
    return pl.pallas_call(
        body, name=name, grid=(d, nb),
        in_specs=[cur(0), cur(1), prv(1), cur(2), prv(2), blk, blk, blk, tcur, tcur, tcur],
        out_specs=blk,
        out_shape=jax.ShapeDtypeStruct((S, A_WIDTH), BF16),
        compiler_params=_cp(("parallel", "parallel")),
    )(qkv, qkv, qkv, qkv, qkv, do, lse, adj, *tabs)


def _attn_a_dkv(qkv, tabs, g, do, lse, adj, *, name):
    S = qkv.shape[0]
    d, L, TQ, nsub, nb, nblk = _a_geometry(S, g)

    def body(qc_ref, qn_ref, kc_ref, vc_ref, doc_ref, don_ref, lc_ref, ln_ref, ac_ref, an_ref,
             c_ref, s1_ref, s2_ref, dk_ref, dv_ref):
        n = pl.program_id(1)
        QB = min(2 * HEAD, TQ)
        nh = TQ // QB
        band = _a_band(QB)
        end = jnp.logical_and(band, _a_last_ok(QB, n < nb - 1))
        for h in range(A_HEADS):
            hs = slice(h * HEAD, (h + 1) * HEAD)
            for kh in range(nh):
                sl = slice(kh * QB, (kh + 1) * QB)
                nx = slice((kh + 1) * QB, (kh + 1) * QB + HEAD)
                last = kh == nh - 1
                cat = lambda cur, nxt: jnp.concatenate([cur[sl, hs], nxt[:, hs] if last else cur[nx, hs]], axis=0)
                qcat = cat(qc_ref, qn_ref)
                docat = cat(doc_ref, don_ref)
                lt = cat(lc_ref, ln_ref).T[:1, :]
                at = cat(ac_ref, an_ref).T[:1, :]
                st = jnp.where(end if last else band, _dot_nt(kc_ref[sl, hs], qcat) * A_SCALE, NEG)
                pt = jnp.exp(st - lt)
                dv_ref[sl, hs] = _dot(pt.astype(BF16), docat).astype(BF16)
                dst = pt * (_dot_nt(vc_ref[sl, hs], docat) + at)
                dk = _dot(dst.astype(BF16), qcat) * A_SCALE
                dk_ref[sl, hs] = _unrope(dk, c_ref[sl, :], s1_ref[sl, :], s2_ref[sl, :]).astype(BF16)

    rcur = lambda r, n: r * nb + n
    rnxt = lambda r, n: r * nblk + jnp.minimum((n + 1) * nsub, nblk - 1)
    cur = lambda off: pl.BlockSpec((TQ, A_WIDTH), lambda r, n: (rcur(r, n), off))
    nxu = lambda off: pl.BlockSpec((HEAD, A_WIDTH), lambda r, n: (rnxt(r, n), off))
    tcur = pl.BlockSpec((TQ, LANES), lambda r, n: (rcur(r, n), 0))
    blk, bnx = cur(0), nxu(0)
    return pl.pallas_call(
        body, name=name, grid=(d, nb),
        in_specs=[cur(0), nxu(0), cur(1), cur(2), blk, bnx, blk, bnx, blk, bnx, tcur, tcur, tcur],
        out_specs=[blk, blk],
        out_shape=[jax.ShapeDtypeStruct((S, A_WIDTH), BF16)] * 2,
        compiler_params=_cp(("parallel", "parallel")),
    )(qkv, qkv, qkv, qkv, do, do, lse, lse, adj, adj, *tabs)


def _silu_parts(z):
    sg = _sigmoid(z)
    return z * sg, sg * (1.0 + z * (1.0 - sg))


def _merge_a_fwd(os_, ls_, ur, *, name):
    S = ur.shape[0]
    tm = min(512, S)

    def body(o0, o1, o2, l0, l1, l2, z_ref, y_ref):
        ls = [l0[...], l1[...], l2[...]]
        mx = jnp.maximum(jnp.maximum(ls[0], ls[1]), ls[2])
        es = [jnp.exp(l - mx) for l in ls]
        den = es[0] + es[1] + es[2]
        y = (es[0] / den) * o0[...] + (es[1] / den) * o1[...] + (es[2] / den) * o2[...]
        y_ref[...] = (y * _silu_parts(z_ref[...])[0]).astype(BF16)

    blk = pl.BlockSpec((tm, A_WIDTH), lambda i: (i, 0))
    return pl.pallas_call(
        body, name=name, grid=(S // tm,),
        in_specs=[blk] * 6 + [pl.BlockSpec((tm, A_WIDTH), lambda i: (i, R_ZA // A_WIDTH))],
        out_specs=blk, out_shape=jax.ShapeDtypeStruct((S, A_WIDTH), BF16),
        compiler_params=_cp(("parallel",)))(*os_, *ls_, ur)


def _merge_a_bwd(os_, ls_, ur, dya, *, name):
    S = ur.shape[0]
    tm = min(256, S)

    def body(o0, o1, o2, l0, l1, l2, z_ref, dy_ref, d0, d1, d2, a0, a1, a2, dz_ref):
        ls = [l0[...], l1[...], l2[...]]
        ov = [o0[...], o1[...], o2[...]]
        mx = jnp.maximum(jnp.maximum(ls[0], ls[1]), ls[2])
        es = [jnp.exp(l - mx) for l in ls]
        den = es[0] + es[1] + es[2]
        ws = [e / den for e in es]
        y = ws[0] * ov[0] + ws[1] * ov[1] + ws[2] * ov[2]
        sz, dsz = _silu_parts(z_ref[...])
        dyv = dy_ref[...]
        dz_ref[...] = (dyv * y * dsz).astype(BF16)
        dyp = dyv * sz
        for h in range(A_HEADS):
            sl = slice(h * HEAD, (h + 1) * HEAD)
            t = jnp.zeros((tm, 1), F32)
            for gi in range(3):
                t = t + ws[gi][:, sl][:, :1] * jnp.sum(dyp[:, sl] * ov[gi][:, sl], axis=-1, keepdims=True)
            for gi, (dref, aref) in enumerate(((d0, a0), (d1, a1), (d2, a2))):
                wg = ws[gi][:, sl]
                dref[:, sl] = (wg * dyp[:, sl]).astype(BF16)
                aref[:, sl] = -wg * t

    blk = pl.BlockSpec((tm, A_WIDTH), lambda i: (i, 0))
    outs = pl.pallas_call(
        body, name=name, grid=(S // tm,),
        in_specs=[blk] * 6 + [pl.BlockSpec((tm, A_WIDTH), lambda i: (i, R_ZA // A_WIDTH)), blk],
        out_specs=[blk] * 7,
        out_shape=[jax.ShapeDtypeStruct((S, A_WIDTH), BF16)] * 3
        + [jax.ShapeDtypeStruct((S, A_WIDTH), F32)] * 3 + [jax.ShapeDtypeStruct((S, A_WIDTH), BF16)],
        compiler_params=_cp(("parallel",)))(*os_, *ls_, ur, dya)
    return outs[0:3], outs[3:6], outs[6]


def _logf(ur, bf_pad, *, name):
    S = ur.shape[0]
    tm = min(1024, S)

    def body(u_ref, b_ref, o_ref):
        z = u_ref[...] + b_ref[...]
        o_ref[...] = jnp.minimum(z, 0.0) - jnp.log(1.0 + jnp.exp(-jnp.abs(z)))

    return pl.pallas_call(
        body, name=name, grid=(S // tm,),
        in_specs=[pl.BlockSpec((tm, FB_PAD), lambda i: (i, R_FB // FB_PAD)),
                  pl.BlockSpec((1, FB_PAD), lambda i: (0, 0))],
        out_specs=pl.BlockSpec((tm, FB_PAD), lambda i: (i, 0)),
        out_shape=jax.ShapeDtypeStruct((S, FB_PAD), F32),
        compiler_params=_cp(("parallel",)))(ur, bf_pad)


def _cumsum_lanes(x, reverse, *, name):
    nt, H, _ = x.shape
    R = nt * H

    def body(x_ref, o_ref):
        v = x_ref[...].reshape(R, LANES)
        lane = lax.broadcasted_iota(jnp.int32, (R, LANES), 1)
        row = lax.broadcasted_iota(jnp.int32, (R, LANES), 0)

        def scan(t, step, idx, n, axis):
            while step < n:
                if reverse:
                    t = t + jnp.where(idx < n - step, pltpu.roll(t, n - step, axis), 0.0)
                else:
                    t = t + jnp.where(idx >= step, pltpu.roll(t, step, axis), 0.0)
                step *= 2
            return t

        v = scan(v, 1, lane, LANES, 1)
        total = jnp.broadcast_to(v[:, :1] if reverse else v[:, LANES - 1:], (R, LANES))
        carry = scan(total, H, row, R, 0) - total
        o_ref[...] = (v + carry).reshape(nt, H, LANES)

    return pl.pallas_call(
        body, name=name, out_shape=jax.ShapeDtypeStruct((nt, H, LANES), F32),
        in_specs=[pl.BlockSpec(memory_space=pltpu.VMEM)], out_specs=pl.BlockSpec(memory_space=pltpu.VMEM),
        compiler_params=_cp())(x)


B_SCALE = B_HEAD ** -0.5


def _pair_masks():
    lane = lax.broadcasted_iota(jnp.int32, (1, LANES), 1)
    row = lax.broadcasted_iota(jnp.int32, (LANES, 1), 0)
    return (lane < B_HEAD, lane >= B_HEAD), (row < B_HEAD, row >= B_HEAD)


def _causal_t(T):
    r = lax.broadcasted_iota(jnp.int32, (T, T), 0)
    c = lax.broadcasted_iota(jnp.int32, (T, T), 1)
    return r <= c


def _zero_other(x, keep):
    return jnp.where(keep, x, jnp.zeros_like(x))


def _fox_aug(ub, ckb, *, name):
    S = ub.shape[0]
    T = min(2048, S)

    def body(q_ref, k_ref, c_ref, qa_ref, ka_ref):
        lane = lax.broadcasted_iota(jnp.int32, (1, LANES), 1)
        q = q_ref[...] * B_SCALE
        k = k_ref[...]
        for a in range(2):
            own = (lane < B_HEAD) if a == 0 else (lane >= B_HEAD)
            o = B_HEAD if a == 0 else 0
            c = c_ref[a]
            hi = c.astype(BF16)
            r1 = c - hi.astype(F32)
            mid = r1.astype(BF16)
            lo = (r1 - mid.astype(F32)).astype(BF16)
            pieces = (hi, mid, lo)
            one = jnp.ones((T, LANES), BF16)
            qa = jnp.where(own, q, jnp.zeros_like(q))
            ka = jnp.where(own, k, jnp.zeros_like(k))
            for t in range(3):
                qa = jnp.where(lane == o + t, pieces[t], qa)
                qa = jnp.where(lane == o + 3 + t, one, qa)
                ka = jnp.where(lane == o + t, one, ka)
                ka = jnp.where(lane == o + 3 + t, -pieces[t], ka)
            qa_ref[a] = qa
            ka_ref[a] = ka

    out = pl.BlockSpec((2, T, LANES), lambda h, i: (h, i, 0))
    return pl.pallas_call(
        body, name=name, grid=(B_HEADS // 2, S // T),
        in_specs=[pl.BlockSpec((T, LANES), lambda h, i: (i, h)), pl.BlockSpec((T, LANES), lambda h, i: (i, 4 + h)), out],
        out_specs=[out, out], out_shape=[jax.ShapeDtypeStruct((B_HEADS, S, LANES), BF16)] * 2,
        compiler_params=_cp(("parallel", "parallel")))(ub, ub, ckb)


def _fox_fwd(qaug, kaug, vt, *, name):
    S = qaug.shape[1]
    T = min(512, S)
    nq = S // T

    def body(q_ref, k_ref, vt_ref, o_ref, l_ref, m_s, l_s, acc_s, st_s):
        i = pl.program_id(1)
        _, rows = _pair_masks()
        qm = [q_ref[0], q_ref[1]]
        m_s[...] = jnp.full((2, 1, T), NEG, F32)
        l_s[...] = jnp.zeros((2, 1, T), F32)
        acc_s[...] = jnp.zeros((LANES, T), F32)

        def logits(j):
            off = pl.multiple_of(j * T, T)
            return [_dot_nt(k_ref[a, pl.ds(off, T), :], qm[a]) for a in range(2)]

        def step(j, masked, prefetch):
            nxt = logits(j + 1) if prefetch else None
            vtj = vt_ref[j]
            upd = jnp.zeros((LANES, T), F32)
            alphas = []
            for a in range(2):
                st = st_s[a]
                if masked:
                    st = jnp.where(_causal_t(T), st, NEG)
                m_old = m_s[a]
                m_new = jnp.maximum(m_old, jnp.max(st, axis=0, keepdims=True))
                alpha = jnp.exp(m_old - m_new)
                pt = jnp.exp(st - m_new)
                l_s[a] = alpha * l_s[a] + jnp.sum(pt, axis=0, keepdims=True)
                m_s[a] = m_new
                upd = upd + _dot(_zero_other(vtj, rows[a]), pt.astype(BF16))
                alphas.append(alpha)
            acc_s[...] = acc_s[...] * jnp.where(rows[0], alphas[0], alphas[1]) + upd
            if prefetch:
                st_s[0] = nxt[0]
                st_s[1] = nxt[1]

        def loop(j, carry):
            step(j, False, True)
            return carry

        first = logits(0)
        st_s[0] = first[0]
        st_s[1] = first[1]
        lax.fori_loop(0, i, loop, 0)
        step(i, True, False)
        o_ref[...] = (acc_s[...] / jnp.where(rows[0], l_s[0], l_s[1])).T
        l_ref[0] = m_s[0] + jnp.log(l_s[0])
        l_ref[1] = m_s[1] + jnp.log(l_s[1])

    stat = pl.BlockSpec((2, None, 1, T), lambda h, i: (h, i, 0, 0))
    return pl.pallas_call(
        body, name=name, grid=(B_HEADS // 2, nq),
        in_specs=[pl.BlockSpec((2, T, LANES), lambda h, i: (h, i, 0)),
                  pl.BlockSpec((2, S, LANES), lambda h, i: (h, 0, 0)),
                  pl.BlockSpec((nq, LANES, T), lambda h, i: (0, h, 0))],
        out_specs=[pl.BlockSpec((T, LANES), lambda h, i: (i, h)), stat],
        out_shape=[jax.ShapeDtypeStruct((S, A_WIDTH), F32), jax.ShapeDtypeStruct((B_HEADS, nq, 1, T), F32)],
        scratch_shapes=[pltpu.VMEM((2, 1, T), F32), pltpu.VMEM((2, 1, T), F32), pltpu.VMEM((LANES, T), F32),
                        pltpu.VMEM((2, T, T), F32)],
        compiler_params=_cp(("parallel", "parallel")),
    )(qaug, kaug, vt)


def _fox_delta(o, do, *, name):
    S = o.shape[0]
    T = min(512, S)
    nq = S // T

    per = min(4, nq)

    def body(o_ref, do_ref, d_ref):
        _, rows = _pair_masks()
        for t in range(per):
            sl = slice(t * T, (t + 1) * T)
            prod_t = (do_ref[sl, :].astype(F32) * o_ref[sl, :]).T
            d_ref[0, t] = jnp.sum(_zero_other(prod_t, rows[0]), axis=0, keepdims=True)
            d_ref[1, t] = jnp.sum(_zero_other(prod_t, rows[1]), axis=0, keepdims=True)

    tile = pl.BlockSpec((per * T, LANES), lambda h, i: (i, h))
    return pl.pallas_call(
        body, name=name, grid=(B_HEADS // 2, nq // per), in_specs=[tile, tile],
        out_specs=pl.BlockSpec((2, per, 1, T), lambda h, i: (h, i, 0, 0)),
        out_shape=jax.ShapeDtypeStruct((B_HEADS, nq, 1, T), F32),
        compiler_params=_cp(("parallel", "parallel")))(o, do)


def _fox_bwd(ub, qaug, kaug, kt, do, lse, delta, *, name):
    S = ub.shape[0]
    T = min(512, S)
    nq = S // T

    def body(k_ref, v_ref, kt_ref, q_ref, do_ref, l_ref, dl_ref,
             dk_ref, dv_ref, dck_ref, dqt_ref, dcq_ref, dk_s, dv_s, dc_s):
        j = pl.program_id(1)
        lanes, rows = _pair_masks()
        vv = v_ref[...]
        ktj = kt_ref[...]
        km = [k_ref[0], k_ref[1]]
        ktm = [_zero_other(ktj, rows[0]), _zero_other(ktj, rows[1])]
        dk_s[...] = jnp.zeros((2, T, LANES), F32)
        dv_s[...] = jnp.zeros((T, LANES), F32)
        dc_s[...] = jnp.zeros((2, T, 1), F32)

        @pl.when(j == 0)
        def _():
            dqt_ref[...] = jnp.zeros((nq, LANES, T), F32)
            dcq_ref[...] = jnp.zeros((2, nq, 1, T), F32)

        def step(i, masked):
            off = pl.multiple_of(i * T, T)
            doi = do_ref[pl.ds(off, T), :]
            upd = jnp.zeros((LANES, T), F32)
            for a in range(2):
                qi = q_ref[a, pl.ds(off, T), :]
                st = _dot_nt(km[a], qi)
                if masked:
                    st = jnp.where(_causal_t(T), st, NEG)
                pt = jnp.exp(st - l_ref[a, i])
                doa = _zero_other(doi, lanes[a])
                dv_s[...] += _dot(pt.astype(BF16), doa)
                dst = pt * (_dot_nt(vv, doa) - dl_ref[a, i])
                dsb = dst.astype(BF16)
                dk_s[a] += _dot(dsb, qi)
                upd = upd + _dot(ktm[a], dsb)
                dc_s[a] -= jnp.sum(dst, axis=-1, keepdims=True)
                dcq_ref[a, i] += jnp.sum(dst, axis=0, keepdims=True)
            dqt_ref[i] += upd

        def loop(i, carry):
            step(i, False)
            return carry

        step(j, True)
        lax.fori_loop(j + 1, nq, loop, 0)
        dk_ref[...] = jnp.where(lanes[0], dk_s[0], dk_s[1]).astype(BF16)
        dv_ref[...] = dv_s[...].astype(BF16)
        dck_ref[...] = dc_s[...]

    rowv = pl.BlockSpec((2, nq, 1, T), lambda h, j: (h, 0, 0, 0))
    tile = pl.BlockSpec((T, LANES), lambda h, j: (j, h))
    return pl.pallas_call(
        body, name=name, grid=(B_HEADS // 2, nq),
        in_specs=[pl.BlockSpec((2, T, LANES), lambda h, j: (h, j, 0)),
                  pl.BlockSpec((T, LANES), lambda h, j: (j, 8 + h)),
                  pl.BlockSpec((None, LANES, T), lambda h, j: (j, h, 0)),
                  pl.BlockSpec((2, S, LANES), lambda h, j: (h, 0, 0)),
                  pl.BlockSpec((S, LANES), lambda h, j: (0, h)),
                  rowv, rowv],
        out_specs=[tile, tile, pl.BlockSpec((2, T, 1), lambda h, j: (h, j, 0)),
                   pl.BlockSpec((nq, LANES, T), lambda h, j: (0, h, 0)), rowv],
        out_shape=[jax.ShapeDtypeStruct((S, A_WIDTH), BF16)] * 2 + [jax.ShapeDtypeStruct((B_HEADS, S, 1), F32),
                   jax.ShapeDtypeStruct((nq, A_WIDTH, T), F32), jax.ShapeDtypeStruct((B_HEADS, nq, 1, T), F32)],
        scratch_shapes=[pltpu.VMEM((2, T, LANES), F32), pltpu.VMEM((T, LANES), F32), pltpu.VMEM((2, T, 1), F32)],
        compiler_params=_cp(("parallel", "arbitrary")),
    )(kaug, ub, kt, qaug, do, lse, delta)


def _gate_fwd(o, ur, zcol, *, name):
    S = ur.shape[0]
    tm = min(1024, S)

    def body(o_ref, z_ref, y_ref):
        y_ref[...] = (o_ref[...] * _silu_parts(z_ref[...])[0]).astype(BF16)

    blk = pl.BlockSpec((tm, A_WIDTH), lambda i: (i, 0))
    return pl.pallas_call(
        body, name=name, grid=(S // tm,),
        in_specs=[blk, pl.BlockSpec((tm, A_WIDTH), lambda i: (i, zcol // A_WIDTH))],
        out_specs=blk, out_shape=jax.ShapeDtypeStruct((S, A_WIDTH), BF16),
        compiler_params=_cp(("parallel",)))(o, ur)


def _gate_bwd(o, ur, zcol, dy, *, name):
    S = ur.shape[0]
    tm = min(1024, S)

    def body(o_ref, z_ref, dy_ref, do_ref, dz_ref):
        sz, dsz = _silu_parts(z_ref[...])
        dyv = dy_ref[...]
        do_ref[...] = (dyv * sz).astype(BF16)
        dz_ref[...] = (dyv * o_ref[...] * dsz).astype(BF16)

    blk = pl.BlockSpec((tm, A_WIDTH), lambda i: (i, 0))
    return pl.pallas_call(
        body, name=name, grid=(S // tm,),
        in_specs=[blk, pl.BlockSpec((tm, A_WIDTH), lambda i: (i, zcol // A_WIDTH)), blk],
        out_specs=[blk, blk], out_shape=[jax.ShapeDtypeStruct((S, A_WIDTH), BF16)] * 2,
        compiler_params=_cp(("parallel",)))(o, ur, dy)


def _dfb(ur, bf_pad, dlogf_pad, *, name):
    S = ur.shape[0]
    tm = min(1024, S)

    def body(u_ref, b_ref, d_ref, o_ref, s_ref):
        i = pl.program_id(0)
        dv = d_ref[...] * _sigmoid(-(u_ref[...] + b_ref[...]))
        o_ref[...] = dv.astype(BF16)
        part = jnp.sum(dv, axis=0, keepdims=True)

        @pl.when(i == 0)
        def _():
            s_ref[...] = part

        @pl.when(i > 0)
        def _():
            s_ref[...] += part

    vec = pl.BlockSpec((1, FB_PAD), lambda i: (0, 0))
    blk = pl.BlockSpec((tm, FB_PAD), lambda i: (i, 0))
    return pl.pallas_call(
        body, name=name, grid=(S // tm,),
        in_specs=[pl.BlockSpec((tm, FB_PAD), lambda i: (i, R_FB // FB_PAD)), vec, blk],
        out_specs=[blk, vec],
        out_shape=[jax.ShapeDtypeStruct((S, FB_PAD), BF16), jax.ShapeDtypeStruct((1, FB_PAD), F32)],
        compiler_params=_cp(("arbitrary",)))(ur, bf_pad, dlogf_pad)


M_SCALE = HEAD ** -0.5


def _mem_fwd(ur, mkv, *, name):
    S = ur.shape[0]
    T = min(512, S)

    def body(q_ref, z_ref, k_ref, v_ref, y_ref):
        for h in range(M_HEADS):
            hs = slice(h * HEAD, (h + 1) * HEAD)
            s = _dot_nt(q_ref[:, hs].astype(BF16), k_ref[:, hs].astype(BF16)) * M_SCALE
            p = jnp.exp(s - jnp.max(s, axis=-1, keepdims=True))
            p = p / jnp.sum(p, axis=-1, keepdims=True)
            o = _dot(p.astype(BF16), v_ref[:, hs].astype(BF16))
            y_ref[:, hs] = (o * _silu_parts(z_ref[:, hs])[0]).astype(BF16)

    wide = lambda col: pl.BlockSpec((T, A_WIDTH), lambda i: (i, col // A_WIDTH))
    kv = lambda half: pl.BlockSpec((N_MEM, A_WIDTH), lambda i: (0, half))
    return pl.pallas_call(
        body, name=name, grid=(S // T,),
        in_specs=[wide(R_QM), wide(R_ZM), kv(0), kv(1)],
        out_specs=pl.BlockSpec((T, A_WIDTH), lambda i: (i, 0)),
        out_shape=jax.ShapeDtypeStruct((S, A_WIDTH), BF16),
        compiler_params=_cp(("parallel",)))(ur, ur, mkv, mkv)


def _mem_bwd(ur, mkv, dy, *, name):
    S = ur.shape[0]
    T = min(512, S)

    def body(q_ref, z_ref, k_ref, v_ref, dy_ref, dq_ref, dz_ref, dk_ref, dv_ref):
        i = pl.program_id(0)

        @pl.when(i == 0)
        def _():
            dk_ref[...] = jnp.zeros((N_MEM, A_WIDTH), F32)
            dv_ref[...] = jnp.zeros((N_MEM, A_WIDTH), F32)

        for h in range(M_HEADS):
            hs = slice(h * HEAD, (h + 1) * HEAD)
            qv = q_ref[:, hs].astype(BF16)
            kv = k_ref[:, hs].astype(BF16)
            vv = v_ref[:, hs].astype(BF16)
            s = _dot_nt(qv, kv) * M_SCALE
            p = jnp.exp(s - jnp.max(s, axis=-1, keepdims=True))
            p = p / jnp.sum(p, axis=-1, keepdims=True)
            o = _dot(p.astype(BF16), vv)
            sz, dsz = _silu_parts(z_ref[:, hs])
            dyv = dy_ref[:, hs]
            dz_ref[:, hs] = (dyv * o * dsz).astype(BF16)
            dov = (dyv * sz).astype(BF16)
            dp = _dot_nt(dov, vv)
            ds = p * (dp - jnp.sum(p * dp, axis=-1, keepdims=True))
            dq_ref[:, hs] = (_dot(ds.astype(BF16), kv) * M_SCALE).astype(BF16)
            dv_ref[:, hs] += _dot(p.T.astype(BF16), dov)
            dk_ref[:, hs] += _dot(ds.T.astype(BF16), qv) * M_SCALE

    wide = lambda col: pl.BlockSpec((T, A_WIDTH), lambda i: (i, col // A_WIDTH))
    kv = lambda half: pl.BlockSpec((N_MEM, A_WIDTH), lambda i: (0, half))
    tile = pl.BlockSpec((T, A_WIDTH), lambda i: (i, 0))
    acc = pl.BlockSpec((N_MEM, A_WIDTH), lambda i: (0, 0))
    return pl.pallas_call(
        body, name=name, grid=(S // T,),
        in_specs=[wide(R_QM), wide(R_ZM), kv(0), kv(1), tile],
        out_specs=[tile, tile, acc, acc],
        out_shape=[jax.ShapeDtypeStruct((S, A_WIDTH), BF16)] * 2
        + [jax.ShapeDtypeStruct((N_MEM, A_WIDTH), F32)] * 2,
        compiler_params=_cp(("arbitrary",)))(ur, ur, mkv, mkv, dy)


def _branch_fwd(ys, wbs, ur, b_merge, *, name):
    S = ur.shape[0]
    tm, tn = min(512, S), 512
    nj = D_MODEL // tn

    def body(ya, yb, ym, wa, wb, wm, g0, g1, g2, b0, b1, b2, mg_ref, p_ref):
        acc = jnp.zeros((tm, tn), F32)
        for i, (y, w, gr, br) in enumerate(((ya, wa, g0, b0), (yb, wb, g1, b1), (ym, wm, g2, b2))):
            pr = _dot(y[...], w[...])
            p_ref[i] = pr.astype(BF16)
            acc = acc + _sigmoid(gr[...] + br[...]) * pr
        mg_ref[...] = acc.astype(BF16)

    yspec = pl.BlockSpec((tm, A_WIDTH), lambda i, j: (i, 0))
    wspec = pl.BlockSpec((A_WIDTH, tn), lambda i, j: (0, j))
    gspec = lambda b: pl.BlockSpec((tm, tn), lambda i, j: (i, (R_GL + b * D_MODEL) // tn + j))
    bspec = lambda b: pl.BlockSpec((1, tn), lambda i, j: (0, b * nj + j))
    return pl.pallas_call(
        body, name=name, grid=(S // tm, nj),
        in_specs=[yspec] * 3 + [wspec] * 3 + [gspec(0), gspec(1), gspec(2), bspec(0), bspec(1), bspec(2)],
        out_specs=[pl.BlockSpec((tm, tn), lambda i, j: (i, j)),
                   pl.BlockSpec((3, tm, tn), lambda i, j: (0, i, j))],
        out_shape=[jax.ShapeDtypeStruct((S, D_MODEL), BF16), jax.ShapeDtypeStruct((3, S, D_MODEL), BF16)],
        compiler_params=_cp(("parallel", "parallel")))(*ys, *wbs, ur, ur, ur, b_merge, b_merge, b_merge)


def _branch_bwd(dm, prods, ur, b_merge, *, name):
    S = ur.shape[0]
    tm = min(256, S)

    def body(dm_ref, p_ref, g0, g1, g2, b_ref, dp_ref, dgl_ref, db_ref):
        i = pl.program_id(0)
        dmv = dm_ref[...]
        parts = []
        for b, gr in enumerate((g0, g1, g2)):
            sl = slice(b * D_MODEL, (b + 1) * D_MODEL)
            gt = _sigmoid(gr[...] + b_ref[:, sl])
            dp_ref[b] = (dmv * gt).astype(BF16)
            dgl = dmv * p_ref[b].astype(F32) * gt * (1.0 - gt)
            dgl_ref[:, sl] = dgl.astype(BF16)
            parts.append(jnp.sum(dgl, axis=0, keepdims=True))
        part = jnp.concatenate(parts, axis=1)

        @pl.when(i == 0)
        def _():
            db_ref[...] = part

        @pl.when(i > 0)
        def _():
            db_ref[...] += part

    gspec = lambda b: pl.BlockSpec((tm, D_MODEL), lambda i: (i, R_GL // D_MODEL + b))
    vec = pl.BlockSpec((1, 3 * D_MODEL), lambda i: (0, 0))
    return pl.pallas_call(
        body, name=name, grid=(S // tm,),
        in_specs=[pl.BlockSpec((tm, D_MODEL), lambda i: (i, 0)),
                  pl.BlockSpec((3, tm, D_MODEL), lambda i: (0, i, 0)), gspec(0), gspec(1), gspec(2), vec],
        out_specs=[pl.BlockSpec((3, tm, D_MODEL), lambda i: (0, i, 0)),
                   pl.BlockSpec((tm, 3 * D_MODEL), lambda i: (i, 0)), vec],
        out_shape=[jax.ShapeDtypeStruct((3, S, D_MODEL), BF16), jax.ShapeDtypeStruct((S, 3 * D_MODEL), BF16),
                   jax.ShapeDtypeStruct((1, 3 * D_MODEL), F32)],
        compiler_params=_cp(("arbitrary",)))(dm, prods, ur, ur, ur, b_merge)


def _rope_tables(pos):
    half = ROT // 2
    S = pos.shape[0]
    inv = ROPE_THETA ** (-jnp.arange(half, dtype=F32) / half)
    per_row = LANES // half
    ang = jnp.repeat(pos.astype(F32).reshape(S // per_row, per_row), half, axis=1) * jnp.tile(inv, per_row)
    cos, sin = jnp.cos(ang).reshape(S, half), jnp.sin(ang).reshape(S, half)
    one = jnp.ones((S, LANES - ROT), F32)
    zero = jnp.zeros((S, LANES - ROT), F32)
    zh = jnp.zeros((S, half), F32)
    c = jnp.concatenate([cos, cos, one], axis=1)
    s1 = jnp.concatenate([-sin, zh, zero], axis=1)
    s2 = jnp.concatenate([zh, sin, zero], axis=1)
    return c, s1, s2


def _to_tiles(t):
    S, H = t.shape
    return t.reshape(S // LANES, LANES, H).transpose(0, 2, 1)


def _from_tiles(t):
    nt, H, _ = t.shape
    return t.transpose(1, 0, 2).reshape(H, nt * LANES)


def _local_step(x, mem, pos, tgt, g_pre, g_post, g_mem, wt, bf_pad, b_merge, w_kv, wbs, w_out, pack=None):
    S = x.shape[0]
    T = min(512, S)
    nq = S // T
    tabs = _rope_tables(pos)

    hs = _rms_fwd(x, g_pre, name="rms_pre", dilations=DIL)
    h = hs[0]
    tabs_g = [[_to_classes(t, d) for t in tabs] for d in DIL]
    qkvs = [_proj_rope(hs[g], wt[f"A{g}"], tabs_g[g], name=f"proj_a{g}") for g in range(3)]
    ub = _mm(h, wt["B"], bt=True, out_dtype=BF16, name="proj_b", tn=1536)
    ur = _mm(h, wt["R"], bt=True, name="proj_r", tn=1792)

    outs_c, lses_c = [], []
    for g in range(3):
        o, l = _attn_a_fwd(qkvs[g], g, name=f"attn_a_fwd{g}")
        outs_c.append(o)
        lses_c.append(l)
    outs_a = [_from_classes(o, d) for o, d in zip(outs_c, DIL)]
    lses_a = [_from_classes(l, d) for l, d in zip(lses_c, DIL)]
    ya = _merge_a_fwd(outs_a, lses_a, ur, name="merge_a_fwd")

    logf = _logf(ur, bf_pad, name="logf")
    c = _from_tiles(_cumsum_lanes(_to_tiles(logf[:, :B_HEADS]), False, name="cumsum_fwd"))
    ckb = jnp.broadcast_to(c[:, :, None], (B_HEADS, S, LANES))
    qaug, kaug = _fox_aug(ub, ckb, name="fox_aug")
    kt = ub[:, 512:1024].reshape(nq, T, 512).transpose(0, 2, 1)
    vt = ub[:, 1024:1536].reshape(nq, T, 512).transpose(0, 2, 1)
    ob, lse_b = _fox_fwd(qaug, kaug, vt, name="fox_fwd")
    yb = _gate_fwd(ob, ur, R_ZB, name="gate_b_fwd")

    hm = _rms_fwd(mem, g_mem, name="rms_mem")
    mkv = _mm(hm, w_kv, name="proj_mem")
    ym = _mem_fwd(ur, mkv, name="mem_fwd")

    merged, prods = _branch_fwd((ya, yb, ym), wbs, ur, b_merge, name="branch_fwd")
    out = _mm(merged, w_out, name="proj_out")
    dy, d_out, dg_post, loss_row = _post(x, out, tgt, g_post, name="post")

    dmerged = _mm(d_out, w_out, bt=True, name="d_merged")
    dw_out = _mm(merged, d_out, at=True, name="dw_out", tk=2048)
    dprods, dgl, db_merge = _branch_bwd(dmerged, prods, ur, b_merge, name="branch_bwd")
    dys, dwbs = [], []
    for i, (y, wb) in enumerate(zip((ya, yb, ym), wbs)):
        dys.append(_mm(dprods[i], wb, bt=True, name=f"d_y{i}"))
        dwbs.append(_mm(y, dprods[i], at=True, name=f"dw_branch{i}", tk=2048))

    dos_a, adjs_a, dza = _merge_a_bwd(outs_a, lses_a, ur, dys[0], name="merge_a_bwd")
    dus_a = []
    for g, d in enumerate(DIL):
        do_c, adj_c = _to_classes(dos_a[g], d), _to_classes(adjs_a[g], d)
        dq = _attn_a_dq(qkvs[g], tabs_g[g], g, do_c, lses_c[g], adj_c, name=f"attn_a_dq{g}")
        dk, dv = _attn_a_dkv(qkvs[g], tabs_g[g], g, do_c, lses_c[g], adj_c, name=f"attn_a_dkv{g}")
        dus_a.append(jnp.concatenate([dq, dk, dv], axis=1))

    dob, dzb = _gate_bwd(ob, ur, R_ZB, dys[1], name="gate_b_bwd")
    delta_b = _fox_delta(ob, dob, name="fox_delta")
    dkb, dvb, dc_k, dqt, dc_q = _fox_bwd(ub, qaug, kaug, kt, dob, lse_b, delta_b, name="fox_bwd")
    dqb = (dqt.transpose(0, 2, 1).reshape(S, A_WIDTH) * B_SCALE).astype(BF16)
    du_b = jnp.concatenate([dqb, dkb, dvb], axis=1)
    dc = dc_q.reshape(B_HEADS, S) + dc_k.reshape(B_HEADS, S)
    dlogf = _from_tiles(_cumsum_lanes(_to_tiles(dc.T), True, name="cumsum_bwd"))
    dlogf_pad = jnp.pad(dlogf.T, ((0, 0), (0, FB_PAD - B_HEADS)))
    dfb, db_forget = _dfb(ur, bf_pad, dlogf_pad, name="dfb")

    dqm, dzm, dmk, dmv = _mem_bwd(ur, mkv, dys[2], name="mem_bwd")
    dmkv = jnp.concatenate([dmk, dmv], axis=1).astype(BF16)
    dhm = _mm(dmkv, w_kv, bt=True, name="d_hm")
    dw_kv = _mm(hm, dmkv, at=True, name="dw_kv")
    dg_mem = _rms_bwd(mem, g_mem, dhm, None, name="rms_mem_bwd")

    du_r = jnp.concatenate([dza, dzb, dqm, dzm, dgl, dfb], axis=1)
    dwt = {"R": _mm(du_r, h, at=True, name="dw_in_r", tm=1792, tk=1024),
           "B": _mm(du_b, h, at=True, name="dw_in_b", tm=1536, tk=2048)}
    for g in range(3):
        dwt[f"A{g}"] = _mm(dus_a[g], hs[g], at=True, name=f"dw_in_a{g}", tm=1536, tk=2048)
    res = dict(dwt=dwt, dw_kv=dw_kv, dwbs=dwbs, dw_out=dw_out)
    token_major = [(du_r, wt["R"]), (du_b, wt["B"]), (dus_a[0], wt["A0"])]
    if pack is None:
        dh_1 = _mm(dus_a[1], wt["A1"], name="d_h_a1", tk=1536)
        dh = _mm_sum(token_major, name="d_h_main")
    else:
        gbig = pack(dwt, dw_kv, dwbs, dw_out)
        own_idx = _own_slabs()
        dh_1, sib = _mm(dus_a[1], wt["A1"], name="d_h_a1", tk=1536, comm=_pair_comm(gbig))
        send = _pair_sum(gbig, sib, own_idx, 208, name="pair_sum")
        dh, recv = _mm_sum(token_major, name="d_h_main", comm=_chips_comm(send))
        res = dict(parts=[(gbig, None), (sib, 1), (recv, N_CHIP - 1)], own_idx=own_idx)
    dh_2 = _mm(dus_a[2], wt["A2"], name="d_h_a2", tk=1536)
    grad_x, dg_pre = _rms_bwd(x, g_pre, dh, dy, name="rms_pre_bwd", dh_classes=[(dh_1, DIL[1]), (dh_2, DIL[2])])

    return dict(res, loss=loss_row, grad_x=grad_x, dg_pre=dg_pre, dg_post=dg_post, dg_mem=dg_mem,
                db_forget=db_forget, db_merge=db_merge)


MESH = pl.DeviceIdType.MESH
ANY = pl.BlockSpec(memory_space=pl.ANY)


def _relations():
    return [(k >> 2 & 1, k >> 1 & 1, k & 1) for k in range(1, N_DEV)]


def _coords():
    return lax.axis_index("x"), lax.axis_index("y"), lax.axis_index("c")


def _all_gather(shard, *, name):
    R, W = shard.shape

    def body(x_ref, out_ref, send_sems, recv_sems, local_sem):
        x, y, c = _coords()
        me, sibling = (x, y, c), (x, y, 1 - c)
        chips = [(1 - x, y), (x, 1 - y), (1 - x, 1 - y)]

        def slot(px, py, pc):
            return out_ref.at[4 * px + 2 * py + pc]

        def copy(k, block, to, src=None):
            return pltpu.make_async_remote_copy(
                src_ref=slot(*block) if src is None else src, dst_ref=slot(*block),
                send_sem=send_sems.at[k], recv_sem=recv_sems.at[k], device_id=to, device_id_type=MESH)

        mine = pltpu.make_async_copy(x_ref, slot(*me), local_sem)
        mine.start()
        first = [copy(0, me, sibling, src=x_ref)]
        first += [copy(1 + j, me, (*chip, c), src=x_ref) for j, chip in enumerate(chips)]
        for cp in first:
            cp.start()
        passed = [copy(4 + j, (*chip, c), sibling) for j, chip in enumerate(chips)]
        for j, chip in enumerate(chips):
            copy(1 + j, (*chip, c), me).wait_recv()
            passed[j].start()
        copy(0, sibling, me).wait_recv()
        for j, chip in enumerate(chips):
            copy(4 + j, (*chip, 1 - c), me).wait_recv()
        for cp in first + passed:
            cp.wait_send()
        mine.wait()

    return pl.pallas_call(
        body, name=name, out_shape=jax.ShapeDtypeStruct((N_DEV, R, W), shard.dtype),
        in_specs=[ANY], out_specs=ANY,
        scratch_shapes=[pltpu.SemaphoreType.DMA((N_DEV - 1,)), pltpu.SemaphoreType.DMA((N_DEV - 1,)),
                        pltpu.SemaphoreType.DMA],
    )(shard)


N_CHIP = 4


def _pair_comm(gbig):
    _, R, W = gbig.shape

    def copies(g_ref, sib_ref, send_sems, recv_sems):
        x, y, c = _coords()
        return [pltpu.make_async_remote_copy(
            src_ref=g_ref.at[4 * (x ^ (r >> 1)) + 2 * (y ^ (r & 1)) + (1 - c)], dst_ref=sib_ref.at[r],
            send_sem=send_sems.at[r], recv_sem=recv_sems.at[r], device_id=(x, y, 1 - c), device_id_type=MESH)
            for r in range(N_CHIP)]

    def start(*refs):
        for cp in copies(*refs):
            cp.start()

    def wait(*refs):
        cps = copies(*refs)
        for cp in cps:
            cp.wait_recv()
        for cp in cps:
            cp.wait_send()

    return dict(inputs=[gbig], out_shape=[jax.ShapeDtypeStruct((N_CHIP, R, W), gbig.dtype)],
                sems=[pltpu.SemaphoreType.DMA((N_CHIP,)), pltpu.SemaphoreType.DMA((N_CHIP,))],
                start=start, wait=wait)


def _own_slabs():
    x, y, c = _coords()
    return jnp.stack([4 * (x ^ (r >> 1)) + 2 * (y ^ (r & 1)) + c for r in range(N_CHIP)]).astype(jnp.int32)


def _pair_sum(gbig, sib, own_idx, tr, *, name):
    _, R, W = gbig.shape

    def body(idx_ref, a_ref, b_ref, o_ref):
        o_ref[...] = (a_ref[...] + b_ref[...]).astype(BF16)

    return pl.pallas_call(
        body, name=name,
        grid_spec=pltpu.PrefetchScalarGridSpec(
            num_scalar_prefetch=1, grid=(N_CHIP - 1, R // tr),
            in_specs=[pl.BlockSpec((None, tr, W), lambda r, i, idx: (idx[r + 1], i, 0)),
                      pl.BlockSpec((None, tr, W), lambda r, i, idx: (r + 1, i, 0))],
            out_specs=pl.BlockSpec((None, tr, W), lambda r, i, idx: (r, i, 0))),
        out_shape=jax.ShapeDtypeStruct((N_CHIP - 1, R, W), BF16),
        compiler_params=_cp(("parallel", "parallel")))(own_idx, gbig, sib)


def _chips_comm(send):
    nb, R, W = send.shape

    def copies(b_ref, rb_ref, send_sems, recv_sems):
        x, y, c = _coords()
        return [pltpu.make_async_remote_copy(
            src_ref=b_ref.at[r - 1], dst_ref=rb_ref.at[r - 1], send_sem=send_sems.at[r - 1],
            recv_sem=recv_sems.at[r - 1], device_id=(x ^ (r >> 1), y ^ (r & 1), c), device_id_type=MESH)
            for r in range(1, N_CHIP)]

    def start(*refs):
        for cp in copies(*refs):
            cp.start()

    def wait(*refs):
        cps = copies(*refs)
        for cp in cps:
            cp.wait_recv()
        for cp in cps:
            cp.wait_send()

    return dict(inputs=[send], out_shape=[jax.ShapeDtypeStruct((nb, R, W), send.dtype)],
                sems=[pltpu.SemaphoreType.DMA((nb,)), pltpu.SemaphoreType.DMA((nb,))],
                start=start, wait=wait)


def _gather_small(gsmall, *, name):
    n = N_DEV - 1

    def body(s_ref, rs_ref, send_sems, recv_sems, local_sem):
        x, y, c = _coords()
        me = 4 * x + 2 * y + c
        mine = pltpu.make_async_copy(s_ref, rs_ref.at[me], local_sem)
        mine.start()

        def copy(k, fx, fy, fc, slot):
            return pltpu.make_async_remote_copy(
                src_ref=s_ref, dst_ref=rs_ref.at[slot], send_sem=send_sems.at[k], recv_sem=recv_sems.at[k],
                device_id=(x ^ fx, y ^ fy, c ^ fc), device_id_type=MESH)

        started = [copy(k, *rel, me) for k, rel in enumerate(_relations())]
        for cp in started:
            cp.start()
        for k, (fx, fy, fc) in enumerate(_relations()):
            copy(k, fx, fy, fc, 4 * (x ^ fx) + 2 * (y ^ fy) + (c ^ fc)).wait_recv()
        for cp in started:
            cp.wait_send()
        mine.wait()

    return pl.pallas_call(
        body, name=name, out_shape=jax.ShapeDtypeStruct((N_DEV, 1, P_SMALL), gsmall.dtype),
        in_specs=[ANY], out_specs=ANY,
        scratch_shapes=[pltpu.SemaphoreType.DMA((n,)), pltpu.SemaphoreType.DMA((n,)), pltpu.SemaphoreType.DMA],
    )(gsmall)


def _part_specs(parts, tr, row0):
    assert row0 % tr == 0
    specs = []
    for a, n_used in parts:
        if n_used is None:
            specs.append(pl.BlockSpec((1, tr, a.shape[2]), lambda i, idx: (idx[0], row0 // tr + i, 0)))
        else:
            specs.append(pl.BlockSpec((n_used, tr, a.shape[2]), lambda i, idx: (0, row0 // tr + i, 0)))
    return specs


def _part_total(refs, parts):
    g = None
    for ref, (_, n_used) in zip(refs, parts):
        for k in range(n_used or 1):
            t = ref[k].astype(F32)
            g = t if g is None else g + t
    return g


def _sum_parts(parts, idx, row0, nrows, tr, *, name):
    W = parts[0][0].shape[2]
    assert nrows % tr == 0

    def body(idx_ref, *refs):
        refs[-1][...] = _part_total(refs[:-1], parts)

    return pl.pallas_call(
        body, name=name,
        grid_spec=pltpu.PrefetchScalarGridSpec(
            num_scalar_prefetch=1, grid=(nrows // tr,), in_specs=_part_specs(parts, tr, row0),
            out_specs=pl.BlockSpec((tr, W), lambda i, idx: (i, 0))),
        out_shape=jax.ShapeDtypeStruct((nrows, W), F32),
        compiler_params=_cp(("parallel",)))(idx, *[a for a, _ in parts])


def _adamw(parts, idx, w, m, v, tr, *, name):
    R, W = w.shape
    assert R % tr == 0
    np_ = len(parts)

    def body(idx_ref, *refs):
        w_ref, m_ref, v_ref, g_ref, d_ref, nm_ref, nv_ref = refs[np_:]
        g = _part_total(refs[:np_], parts)
        mm = ADAM_B1 * m_ref[...] + (1.0 - ADAM_B1) * g
        vv = ADAM_B2 * v_ref[...] + (1.0 - ADAM_B2) * (g * g)
        m_hat = mm / (1.0 - ADAM_B1 ** ADAM_STEP)
        v_hat = vv / (1.0 - ADAM_B2 ** ADAM_STEP)
        g_ref[...] = g
        d_ref[...] = -ADAM_LR * (m_hat / (jnp.sqrt(v_hat) + ADAM_EPS) + ADAM_WD * w_ref[...])
        nm_ref[...] = mm
        nv_ref[...] = vv

    blk = pl.BlockSpec((tr, W), lambda i, idx: (i, 0))
    return pl.pallas_call(
        body, name=name,
        grid_spec=pltpu.PrefetchScalarGridSpec(
            num_scalar_prefetch=1, grid=(R // tr,), in_specs=_part_specs(parts, tr, 0) + [blk, blk, blk],
            out_specs=[blk] * 4),
        out_shape=[jax.ShapeDtypeStruct((R, W), F32)] * 4,
        compiler_params=_cp(("parallel",)))(idx, *[a for a, _ in parts], w, m, v)


def _pack_rest(w_kv, wa, wb, wm, w_out):
    return jnp.concatenate([w_kv[0], w_out[0]] + [t[0].reshape(-1, D_MODEL) for t in (wa, wb, wm)], axis=0)


def _unpack_rest(t):
    br = lambda i: t[RO_BR + 64 * i:RO_BR + 64 * (i + 1)].reshape(1, A_WIDTH, D_MODEL // N_DEV)
    return t[None, RO_KV:RO_OUT], br(0), br(1), br(2), t[None, RO_OUT:RO_BR]


def _orig_rows(gathered, a, b):
    res = []
    while a < b:
        dev, r = divmod(a, CS)
        n = min(b - a, CS - r)
        res.append(gathered[dev, RO_IN + r:RO_IN + r + n])
        a += n
    return res


def _full_weights(gathered):
    wt = {}
    for name, ranges in SEGS.items():
        rows = [p for a, b in ranges for p in _orig_rows(gathered, a, b)]
        if SEG_PAD[name]:
            rows.append(jnp.zeros((SEG_PAD[name], D_MODEL), gathered.dtype))
        wt[name] = jnp.concatenate(rows, axis=0)
    w_kv = gathered[:, RO_KV:RO_OUT].reshape(D_MODEL, D_MODEL)
    w_out = gathered[:, RO_OUT:RO_BR].reshape(D_MODEL, D_MODEL)
    wbs = [gathered[:, RO_BR + 64 * i:RO_BR + 64 * (i + 1)].reshape(N_DEV, A_WIDTH, D_MODEL // N_DEV)
           .transpose(1, 0, 2).reshape(A_WIDTH, D_MODEL) for i in range(3)]
    return wt, w_kv, wbs, w_out


def _orig_order(dwt):
    pieces = []
    for name, ranges in SEGS.items():
        o = 0
        for a, b in ranges:
            pieces.append((a, dwt[name][o:o + b - a]))
            o += b - a
    pieces.sort(key=lambda p: p[0])
    return jnp.concatenate([p[1] for p in pieces], axis=0)


def _pack_grads(dwt, dw_kv, dwbs, dw_out):
    g_in = jnp.pad(_orig_order(dwt).reshape(N_DEV, CS, D_MODEL), ((0, 0), (0, IN_ROWS - CS), (0, 0)))
    br = [t.reshape(A_WIDTH, N_DEV, D_MODEL // N_DEV).transpose(1, 0, 2).reshape(N_DEV, -1, D_MODEL) for t in dwbs]
    return jnp.concatenate([dw_kv.reshape(N_DEV, -1, D_MODEL), dw_out.reshape(N_DEV, -1, D_MODEL)] + br + [g_in],
                           axis=1)


def kernel(x, mem, positions, norm_pre_g, norm_post_g, norm_mem_g, w_in, b_forget, b_merge, w_mem_kv, w_branch_a, w_branch_b, w_branch_m, w_out, loss_target, m_norm_pre_g, m_norm_post_g, m_norm_mem_g, m_w_in, m_b_forget, m_b_merge, m_w_mem_kv, m_w_branch_a, m_w_branch_b, m_w_branch_m, m_w_out, v_norm_pre_g, v_norm_post_g, v_norm_mem_g, v_w_in, v_b_forget, v_b_merge, v_w_mem_kv, v_w_branch_a, v_w_branch_b, v_w_branch_m, v_w_out):
    w_rest = _pack_rest(w_mem_kv, w_branch_a, w_branch_b, w_branch_m, w_out)
    shard = jnp.concatenate([w_rest.astype(BF16), w_in[0].T.astype(BF16),
                             jnp.zeros((IN_ROWS - CS, D_MODEL), BF16)], axis=0)
    gathered = _all_gather(shard, name="gather_weights")
    wt, w_kv, wbs, w_o = _full_weights(gathered)

    bf_pad = jnp.pad(b_forget, ((0, 0), (0, FB_PAD - B_HEADS)))
    r = _local_step(x[0], mem[0], positions[0], loss_target[0], norm_pre_g, norm_post_g, norm_mem_g,
                    wt, bf_pad, b_merge, w_kv, wbs, w_o, pack=_pack_grads)

    gsmall = jnp.concatenate([r["dg_pre"], r["dg_post"], r["dg_mem"], r["db_merge"],
                              r["db_forget"][:, :LANES], r["loss"]], axis=1)
    rsmall = _gather_small(gsmall, name="gather_small")
    parts, own_idx = r["parts"], r["own_idx"]

    m_rest = _pack_rest(m_w_mem_kv, m_w_branch_a, m_w_branch_b, m_w_branch_m, m_w_out)
    v_rest = _pack_rest(v_w_mem_kv, v_w_branch_a, v_w_branch_b, v_w_branch_m, v_w_out)
    outs_rest = [_unpack_rest(t) for t in _adamw(parts, own_idx, w_rest, m_rest, v_rest, 64, name="adamw_rest")]
    g_in = _sum_parts(parts, own_idx, RO_IN, IN_ROWS, 16, name="sum_w_in")[:CS].T
    outs_in = _adamw([(g_in[None], 1)], own_idx, w_in[0], m_w_in[0], v_w_in[0], 128, name="adamw_w_in")

    def small_vec(a, b, c, d, e):
        z = jnp.zeros((1, LANES - B_HEADS), F32)
        return jnp.concatenate([a, b, c, d, e, z, jnp.zeros((1, LANES), F32)], axis=1)

    outs_small = _adamw([(rsmall, N_DEV)], own_idx, small_vec(norm_pre_g, norm_post_g, norm_mem_g, b_merge, b_forget),
                        small_vec(m_norm_pre_g, m_norm_post_g, m_norm_mem_g, m_b_merge, m_b_forget),
                        small_vec(v_norm_pre_g, v_norm_post_g, v_norm_mem_g, v_b_merge, v_b_forget),
                        1, name="adamw_small")

    def small_parts(t):
        return [t[:, O_GPRE:O_GPRE + D_MODEL], t[:, O_GPOST:O_GPOST + D_MODEL], t[:, O_GMEM:O_GMEM + D_MODEL],
                t[:, O_BF:O_BF + B_HEADS], t[:, O_BM:O_BM + 3 * D_MODEL]]

    loss = outs_small[0][0, O_LOSS]
    result = [loss, r["grad_x"][None]]
    for rest, w_i, small in zip(outs_rest, outs_in, outs_small):
        gp, gq, gm, bf, bm = small_parts(small)
        w_k, w_a, w_b, w_m, w_ot = rest
        result += [gp, gq, gm, w_i[None], bf, bm, w_k, w_a, w_b, w_m, w_ot]
    return tuple(result)
```

```python
import jax
import jax.numpy as jnp
from jax import lax
from jax.experimental import pallas as pl
from jax.experimental.pallas import tpu as pltpu

F32 = jnp.float32
BF16 = jnp.bfloat16

N_DEV = 8
D_MODEL = 1024
N_MEM = 256
EPS = 1e-6
NEG = -1e30
ROPE_THETA = 500000.0
DIL = (1, 4, 16)
A_HEADS = 4
HEAD = 128
A_WIDTH = 512
B_HEADS = 8
B_HEAD = 64
M_HEADS = 4
ROT = 32
IN_COLS = 11272
FB_PAD = 256

SEGS = {
    "A0": ((0, 512), (1536, 2048), (3072, 3584)),
    "A1": ((512, 1024), (2048, 2560), (3584, 4096)),
    "A2": ((1024, 1536), (2560, 3072), (4096, 4608)),
    "B": ((5120, 6656),),
    "R": ((4608, 5120), (6664, 7176), (7176, 7688), (7688, 8200), (8200, 11272), (6656, 6664)),
}
SEG_PAD = {"A0": 0, "A1": 0, "A2": 0, "B": 0, "R": FB_PAD - B_HEADS}
R_ZA, R_ZB, R_QM, R_ZM, R_GL, R_FB = 0, 512, 1024, 1536, 2048, 5120
NR = R_FB + FB_PAD

ADAM_LR, ADAM_B1, ADAM_B2, ADAM_EPS, ADAM_WD, ADAM_STEP = 0.001, 0.9, 0.999, 1e-08, 0.01, 10

LANES = 128
VMEM_LIMIT = 56 * 1024 * 1024

CS = IN_COLS // N_DEV
RO_KV, RO_OUT, RO_BR, RO_IN = 0, 128, 256, 448
IN_ROWS = 1424
ROWS = RO_IN + IN_ROWS
O_GPRE, O_GPOST, O_GMEM, O_BM, O_BF, O_LOSS = 0, 1024, 2048, 3072, 6144, 6272
P_SMALL = 6400


def _cp(sem=None):
    return pltpu.CompilerParams(dimension_semantics=sem, vmem_limit_bytes=VMEM_LIMIT)


def _dot(a, b):
    return jnp.dot(a, b, preferred_element_type=F32)


def _dot_nt(a, b):
    return lax.dot_general(a, b, (((1,), (1,)), ((), ())), preferred_element_type=F32)


def _sigmoid(z):
    return 1.0 / (1.0 + jnp.exp(-z))


def _mm(a, b, *, name, at=False, bt=False, out_dtype=F32, tm=1024, tn=1024, tk=None, comm=None):
    assert not (at and bt)
    K, M = a.shape if at else a.shape[::-1]
    N = b.shape[0] if bt else b.shape[1]
    tm, tn = min(tm, M), min(tn, N)
    tk = K if tk is None else min(tk, K)
    assert M % tm == 0 and N % tn == 0 and K % tk == 0
    nk = K // tk
    grid = (M // tm, N // tn, nk)
    n_in = len(comm["inputs"]) if comm else 0
    n_out = len(comm["out_shape"]) if comm else 0

    def body(a_ref, b_ref, *rest):
        c_in, o_ref, c_out = rest[:n_in], rest[n_in], rest[n_in + 1:n_in + 1 + n_out]
        acc_ref, sems = rest[n_in + 1 + n_out], rest[n_in + 2 + n_out:]
        if comm:
            step = (pl.program_id(0) * grid[1] + pl.program_id(1)) * grid[2] + pl.program_id(2)

            @pl.when(step == 0)
            def _():
                comm["start"](*c_in, *c_out, *sems)

        av = a_ref[...].astype(BF16)
        bv = b_ref[...].astype(BF16)
        if at:
            p = lax.dot_general(av, bv, (((0,), (0,)), ((), ())), preferred_element_type=F32)
        else:
            p = _dot_nt(av, bv) if bt else _dot(av, bv)
        if nk == 1:
            o_ref[...] = p.astype(out_dtype)
        else:
            k = pl.program_id(2)

            @pl.when(k == 0)
            def _():
                acc_ref[...] = p

            @pl.when(k > 0)
            def _():
                acc_ref[...] += p

            @pl.when(k == nk - 1)
            def _():
                o_ref[...] = acc_ref[...].astype(out_dtype)

        if comm:
            @pl.when(step == grid[0] * grid[1] * grid[2] - 1)
            def _():
                comm["wait"](*c_in, *c_out, *sems)

    b_spec = (pl.BlockSpec((tn, tk), lambda i, j, k: (j, k)) if bt
              else pl.BlockSpec((tk, tn), lambda i, j, k: (k, j)))
    a_spec = (pl.BlockSpec((tk, tm), lambda i, j, k: (k, i)) if at
              else pl.BlockSpec((tm, tk), lambda i, j, k: (i, k)))
    out_spec = pl.BlockSpec((tm, tn), lambda i, j, k: (i, j))
    out_shape = jax.ShapeDtypeStruct((M, N), out_dtype)
    acc = pltpu.VMEM((tm, tn) if nk > 1 else (8, LANES), F32)
    if not comm:
        return pl.pallas_call(
            body, name=name, grid=grid, in_specs=[a_spec, b_spec], out_specs=out_spec, out_shape=out_shape,
            scratch_shapes=[acc], compiler_params=_cp(("parallel", "parallel", "arbitrary")))(a, b)
    return pl.pallas_call(
        body, name=name, grid=grid, in_specs=[a_spec, b_spec] + [ANY] * n_in,
        out_specs=[out_spec] + [ANY] * n_out, out_shape=[out_shape] + comm["out_shape"],
        scratch_shapes=[acc] + comm["sems"],
        compiler_params=_cp(("arbitrary", "arbitrary", "arbitrary")))(a, b, *comm["inputs"])


def _mm_sum(pairs, *, name, tm=1024, tk=768, comm=None):
    M, N = pairs[0][0].shape[0], pairs[0][1].shape[1]
    tm = min(tm, M)
    steps = [a.shape[1] // tk for a, _ in pairs]
    assert M % tm == 0 and all(a.shape[1] % tk == 0 for a, _ in pairs)
    first = [sum(steps[:p]) for p in range(len(pairs))]
    total = sum(steps)
    grid = (M // tm, total)
    n_in = len(comm["inputs"]) if comm else 0
    n_out = len(comm["out_shape"]) if comm else 0
    npair = len(pairs)

    def body(*refs):
        ab, rest = refs[:2 * npair], refs[2 * npair:]
        c_in, o_ref, c_out = rest[:n_in], rest[n_in], rest[n_in + 1:n_in + 1 + n_out]
        acc_ref, sems = rest[n_in + 1 + n_out], rest[n_in + 2 + n_out:]
        k = pl.program_id(1)
        if comm:
            step = pl.program_id(0) * total + k

            @pl.when(step == 0)
            def _():
                comm["start"](*c_in, *c_out, *sems)

        @pl.when(k == 0)
        def _():
            acc_ref[...] = jnp.zeros((tm, N), F32)

        for p in range(npair):
            @pl.when(jnp.logical_and(k >= first[p], k < first[p] + steps[p]))
            def _(p=p):
                acc_ref[...] += _dot(ab[2 * p][...], ab[2 * p + 1][...])

        @pl.when(k == total - 1)
        def _():
            o_ref[...] = acc_ref[...]

        if comm:
            @pl.when(step == grid[0] * total - 1)
            def _():
                comm["wait"](*c_in, *c_out, *sems)

    def local(p):
        return lambda k: jnp.clip(k - first[p], 0, steps[p] - 1)

    in_specs = []
    for p in range(npair):
        in_specs += [pl.BlockSpec((tm, tk), lambda i, k, f=local(p): (i, f(k))),
                     pl.BlockSpec((tk, N), lambda i, k, f=local(p): (f(k), 0))]
    out_spec = pl.BlockSpec((tm, N), lambda i, k: (i, 0))
    out_shape = jax.ShapeDtypeStruct((M, N), F32)
    args = [t for pair in pairs for t in pair]
    if not comm:
        return pl.pallas_call(
            body, name=name, grid=grid, in_specs=in_specs, out_specs=out_spec, out_shape=out_shape,
            scratch_shapes=[pltpu.VMEM((tm, N), F32)], compiler_params=_cp(("parallel", "arbitrary")))(*args)
    return pl.pallas_call(
        body, name=name, grid=grid, in_specs=in_specs + [ANY] * n_in,
        out_specs=[out_spec] + [ANY] * n_out, out_shape=[out_shape] + comm["out_shape"],
        scratch_shapes=[pltpu.VMEM((tm, N), F32)] + comm["sems"],
        compiler_params=_cp(("arbitrary", "arbitrary")))(*args, *comm["inputs"])


def _class_spec(S, d, tm, width):
    return pl.BlockSpec((d, tm // d, width), lambda i: (0, i, 0))


def _rms_fwd(x, g, *, name, dilations=()):
    S, D = x.shape
    tm = min(512, S)
    ds = [d for d in dilations if d > 1]

    def body(x_ref, g_ref, o_ref, *rest):
        xv = x_ref[...]
        r = lax.rsqrt(jnp.mean(xv * xv, axis=-1, keepdims=True) + EPS)
        hv = xv * r * g_ref[...]
        o_ref[...] = hv.astype(BF16)
        if ds:
            tmps = rest[len(ds):]
            for c, tmp in enumerate(tmps):
                tmp[...] = hv[:, c * LANES:(c + 1) * LANES]
            for c_ref, d in zip(rest, ds):
                for k in range(d):
                    c_ref[k] = jnp.concatenate([tmp[pl.ds(k, tm // d, stride=d), :] for tmp in tmps],
                                               axis=1).astype(BF16)

    row = pl.BlockSpec((tm, D), lambda i: (i, 0))
    outs = pl.pallas_call(
        body, name=name, grid=(S // tm,),
        in_specs=[row, pl.BlockSpec((1, D), lambda i: (0, 0))],
        out_specs=[row] + [_class_spec(S, d, tm, D) for d in ds],
        out_shape=[jax.ShapeDtypeStruct((S, D), BF16)] + [jax.ShapeDtypeStruct((d, S // d, D), BF16) for d in ds],
        scratch_shapes=[pltpu.VMEM((tm, LANES), F32)] * (D // LANES) if ds else [],
        compiler_params=_cp(("parallel",)),
    )(x, g)
    return [outs[0]] + [o.reshape(S, D) for o in outs[1:]] if ds else outs[0]


def _rms_bwd(x, g, dh, dy, *, name, dh_classes=()):
    S, D = x.shape
    tm = min(512, S)
    want_dx = dy is not None
    nc = len(dh_classes)

    def body(*refs):
        c_refs, refs = refs[:nc], refs[nc:]
        if want_dx:
            x_ref, g_ref, dh_ref, dy_ref, dx_ref, dg_ref = refs[:6]
        else:
            x_ref, g_ref, dh_ref, dg_ref = refs[:4]
        i = pl.program_id(0)
        xv = x_ref[...]
        r = lax.rsqrt(jnp.mean(xv * xv, axis=-1, keepdims=True) + EPS)
        xh = xv * r
        if nc:
            tmps = refs[-(D // LANES):]
            cols = [slice(c * LANES, (c + 1) * LANES) for c in range(D // LANES)]
            for tmp, cs in zip(tmps, cols):
                tmp[...] = dh_ref[:, cs]
            for c_ref, (_, d) in zip(c_refs, dh_classes):
                for k in range(d):
                    for tmp, cs in zip(tmps, cols):
                        tmp[pl.ds(k, tm // d, stride=d), :] += c_ref[k, :, cs]
            dhv = jnp.concatenate([tmp[...] for tmp in tmps], axis=1)
        else:
            dhv = dh_ref[...]
        part = jnp.sum(dhv * xh, axis=0, keepdims=True)

        @pl.when(i == 0)
        def _():
            dg_ref[...] = part

        @pl.when(i > 0)
        def _():
            dg_ref[...] += part

        if want_dx:
            dxh = dhv * g_ref[...]
            dx_ref[...] = dy_ref[...] + r * (dxh - xh * jnp.mean(dxh * xh, axis=-1, keepdims=True))

    row = pl.BlockSpec((tm, D), lambda i: (i, 0))
    vec = pl.BlockSpec((1, D), lambda i: (0, 0))
    c_specs = [_class_spec(S, d, tm, D) for _, d in dh_classes]
    c_args = [a.reshape(d, S // d, D) for a, d in dh_classes]
    scratch = [pltpu.VMEM((tm, LANES), F32)] * (D // LANES) if nc else []
    if want_dx:
        return pl.pallas_call(
            body, name=name, grid=(S // tm,), in_specs=c_specs + [row, vec, row, row], out_specs=[row, vec],
            out_shape=[jax.ShapeDtypeStruct((S, D), F32), jax.ShapeDtypeStruct((1, D), F32)],
            scratch_shapes=scratch, compiler_params=_cp(("arbitrary",)))(*c_args, x, g, dh, dy)
    return pl.pallas_call(
        body, name=name, grid=(S // tm,), in_specs=c_specs + [row, vec, row], out_specs=vec,
        out_shape=jax.ShapeDtypeStruct((1, D), F32),
        scratch_shapes=scratch, compiler_params=_cp(("arbitrary",)))(*c_args, x, g, dh)


def _post(x, out, tgt, g, *, name):
    S, D = x.shape
    tm = min(512, S)

    def body(x_ref, o_ref, t_ref, g_ref, dy_ref, do_ref, dg_ref, loss_ref):
        i = pl.program_id(0)
        ov = o_ref[...]
        r = lax.rsqrt(jnp.mean(ov * ov, axis=-1, keepdims=True) + EPS)
        n = ov * r
        gv = g_ref[...]
        e = (x_ref[...] + n * gv) - t_ref[...]
        lpart = 0.5 * jnp.sum(jnp.mean(e * e, axis=-1, keepdims=True), axis=0, keepdims=True)
        dy = e * (1.0 / D)
        dy_ref[...] = dy
        dn = dy * gv
        do_ref[...] = (r * (dn - n * jnp.mean(dn * n, axis=-1, keepdims=True))).astype(BF16)
        gpart = jnp.sum(dy * n, axis=0, keepdims=True)
        lrow = jnp.broadcast_to(lpart, (1, LANES))

        @pl.when(i == 0)
        def _():
            dg_ref[...] = gpart
            loss_ref[...] = lrow

        @pl.when(i > 0)
        def _():
            dg_ref[...] += gpart
            loss_ref[...] += lrow

    row = pl.BlockSpec((tm, D), lambda i: (i, 0))
    vec = pl.BlockSpec((1, D), lambda i: (0, 0))
    return pl.pallas_call(
        body, name=name, grid=(S // tm,), in_specs=[row, row, row, vec],
        out_specs=[row, row, vec, pl.BlockSpec((1, LANES), lambda i: (0, 0))],
        out_shape=[jax.ShapeDtypeStruct((S, D), F32), jax.ShapeDtypeStruct((S, D), BF16),
                   jax.ShapeDtypeStruct((1, D), F32), jax.ShapeDtypeStruct((1, LANES), F32)],
        compiler_params=_cp(("arbitrary",)))(x, out, tgt, g)


def _to_classes(t, d):
    if d == 1:
        return t
    S, C = t.shape
    return t.reshape(S // d, d, C).transpose(1, 0, 2).reshape(S, C)


def _rope(x, c, s1, s2):
    return x * c + pltpu.roll(x, LANES - ROT // 2, 1) * s1 + pltpu.roll(x, ROT // 2, 1) * s2


def _unrope(d, c, s1, s2):
    return d * c + pltpu.roll(d * s1, ROT // 2, 1) + pltpu.roll(d * s2, LANES - ROT // 2, 1)


def _a_band(qb):
    r = lax.broadcasted_iota(jnp.int32, (qb, qb + HEAD), 0)
    c = lax.broadcasted_iota(jnp.int32, (qb, qb + HEAD), 1)
    return jnp.logical_and(c >= r, c <= r + HEAD)


def _a_first_ok(qb, n):
    c = lax.broadcasted_iota(jnp.int32, (qb, qb + HEAD), 1)
    return jnp.logical_or(c >= HEAD, n > 0)


def _a_last_ok(qb, has_next):
    c = lax.broadcasted_iota(jnp.int32, (qb, qb + HEAD), 1)
    return jnp.logical_or(c < qb, has_next)


A_SCALE = HEAD ** -0.5


def _a_geometry(S, g):
    d = DIL[g]
    L = S // d
    TQ = min(512, L)
    return d, L, TQ, TQ // HEAD, L // TQ, L // HEAD


def _proj_rope(h, w, tabs, *, name):
    S, D = h.shape
    tm = min(512, S)

    def body(h_ref, w_ref, c_ref, s1_ref, s2_ref, o_ref):
        tc = (c_ref[...], s1_ref[...], s2_ref[...])
        u = _dot_nt(h_ref[...], w_ref[...])
        for j in range(3 * A_HEADS):
            sl = slice(j * HEAD, (j + 1) * HEAD)
            o_ref[:, sl] = (_rope(u[:, sl], *tc) if j < 2 * A_HEADS else u[:, sl]).astype(BF16)

    tab = pl.BlockSpec((tm, LANES), lambda i: (i, 0))
    return pl.pallas_call(
        body, name=name, grid=(S // tm,),
        in_specs=[pl.BlockSpec((tm, D), lambda i: (i, 0)), pl.BlockSpec((3 * A_WIDTH, D), lambda i: (0, 0)),
                  tab, tab, tab],
        out_specs=pl.BlockSpec((tm, 3 * A_WIDTH), lambda i: (i, 0)),
        out_shape=jax.ShapeDtypeStruct((S, 3 * A_WIDTH), BF16),
        compiler_params=_cp(("parallel",)))(h, w, *tabs)


def _attn_a_fwd(qkv, g, *, name):
    S = qkv.shape[0]
    d, L, TQ, nsub, nb, nblk = _a_geometry(S, g)

    def body(q_ref, kc_ref, kp_ref, vc_ref, vp_ref, o_ref, l_ref):
        n = pl.program_id(1)
        QB = min(2 * HEAD, TQ)
        band = _a_band(QB)
        first = jnp.logical_and(band, _a_first_ok(QB, n))
        for h in range(A_HEADS):
            hs = slice(h * HEAD, (h + 1) * HEAD)
            for hh in range(TQ // QB):
                sl = slice(hh * QB, (hh + 1) * QB)
                pv = slice(hh * QB - HEAD, hh * QB)
                kcat = jnp.concatenate([kp_ref[:, hs] if hh == 0 else kc_ref[pv, hs], kc_ref[sl, hs]], axis=0)
                vcat = jnp.concatenate([vp_ref[:, hs] if hh == 0 else vc_ref[pv, hs], vc_ref[sl, hs]], axis=0)
                s = jnp.where(first if hh == 0 else band, _dot_nt(q_ref[sl, hs], kcat) * A_SCALE, NEG)
                m = jnp.max(s, axis=-1, keepdims=True)
                p = jnp.exp(s - m)
                den = jnp.sum(p, axis=-1, keepdims=True)
                o_ref[sl, hs] = _dot(p.astype(BF16), vcat) / den
                l_ref[sl, hs] = jnp.broadcast_to(m + jnp.log(den), (QB, HEAD))

    rcur = lambda r, n: r * nb + n
    rprv = lambda r, n: r * nblk + jnp.maximum(n * nsub - 1, 0)
    cur = lambda off: pl.BlockSpec((TQ, A_WIDTH), lambda r, n: (rcur(r, n), off))
    prv = lambda off: pl.BlockSpec((HEAD, A_WIDTH), lambda r, n: (rprv(r, n), off))
    out = pl.BlockSpec((TQ, A_WIDTH), lambda r, n: (rcur(r, n), 0))
    return pl.pallas_call(
        body, name=name, grid=(d, nb),
        in_specs=[cur(0), cur(1), prv(1), cur(2), prv(2)],
        out_specs=[out, out],
        out_shape=[jax.ShapeDtypeStruct((S, A_WIDTH), F32)] * 2,
        compiler_params=_cp(("parallel", "parallel")),
    )(qkv, qkv, qkv, qkv, qkv)


def _attn_a_dq(qkv, tabs, g, do, lse, adj, *, name):
    S = qkv.shape[0]
    d, L, TQ, nsub, nb, nblk = _a_geometry(S, g)

    def body(q_ref, kc_ref, kp_ref, vc_ref, vp_ref, do_ref, l_ref, adj_ref, c_ref, s1_ref, s2_ref, dq_ref):
        n = pl.program_id(1)
        QB = min(2 * HEAD, TQ)
        band = _a_band(QB)
        first = jnp.logical_and(band, _a_first_ok(QB, n))
        for h in range(A_HEADS):
            hs = slice(h * HEAD, (h + 1) * HEAD)
            for hh in range(TQ // QB):
                sl = slice(hh * QB, (hh + 1) * QB)
                pv = slice(hh * QB - HEAD, hh * QB)
                kcat = jnp.concatenate([kp_ref[:, hs] if hh == 0 else kc_ref[pv, hs], kc_ref[sl, hs]], axis=0)
                vcat = jnp.concatenate([vp_ref[:, hs] if hh == 0 else vc_ref[pv, hs], vc_ref[sl, hs]], axis=0)
                s = jnp.where(first if hh == 0 else band, _dot_nt(q_ref[sl, hs], kcat) * A_SCALE, NEG)
                p = jnp.exp(s - l_ref[sl, hs][:, :1])
                ds = p * (_dot_nt(do_ref[sl, hs], vcat) + adj_ref[sl, hs][:, :1])
                dq = _dot(ds.astype(BF16), kcat) * A_SCALE
                dq_ref[sl, hs] = _unrope(dq, c_ref[sl, :], s1_ref[sl, :], s2_ref[sl, :]).astype(BF16)

    rcur = lambda r, n: r * nb + n
    rprv = lambda r, n: r * nblk + jnp.maximum(n * nsub - 1, 0)
    cur = lambda off: pl.BlockSpec((TQ, A_WIDTH), lambda r, n: (rcur(r, n), off))
    prv = lambda off: pl.BlockSpec((HEAD, A_WIDTH), lambda r, n: (rprv(r, n), off))
    tcur = pl.BlockSpec((TQ, LANES), lambda r, n: (rcur(r, n), 0))
    blk = cur(0)
    return pl.pallas_call(
        body, name=name, grid=(d, nb),
        in_specs=[cur(0), cur(1), prv(1), cur(2), prv(2), blk, blk, blk, tcur, tcur, tcur],
        out_specs=blk,
        out_shape=jax.ShapeDtypeStruct((S, A_WIDTH), BF16),
        compiler_params=_cp(("parallel", "parallel")),
    )(qkv, qkv, qkv, qkv, qkv, do, lse, adj, *tabs)


def _attn_a_dkv(qkv, tabs, g, do, lse, adj, *, name):
    S = qkv.shape[0]
    d, L, TQ, nsub, nb, nblk = _a_geometry(S, g)

    def body(qc_ref, qn_ref, kc_ref, vc_ref, doc_ref, don_ref, lc_ref, ln_ref, ac_ref, an_ref,
             c_ref, s1_ref, s2_ref, dk_ref, dv_ref):
        n = pl.program_id(1)
        QB = min(2 * HEAD, TQ)
        nh = TQ // QB
        band = _a_band(QB)
        end = jnp.logical_and(band, _a_last_ok(QB, n < nb - 1))
        for h in range(A_HEADS):
            hs = slice(h * HEAD, (h + 1) * HEAD)
            for kh in range(nh):
                sl = slice(kh * QB, (kh + 1) * QB)
                nx = slice((kh + 1) * QB, (kh + 1) * QB + HEAD)
                last = kh == nh - 1
                cat = lambda cur, nxt: jnp.concatenate([cur[sl, hs], nxt[:, hs] if last else cur[nx, hs]], axis=0)
                qcat = cat(qc_ref, qn_ref)
                docat = cat(doc_ref, don_ref)
                lt = cat(lc_ref, ln_ref).T[:1, :]
                at = cat(ac_ref, an_ref).T[:1, :]
                st = jnp.where(end if last else band, _dot_nt(kc_ref[sl, hs], qcat) * A_SCALE, NEG)
                pt = jnp.exp(st - lt)
                dv_ref[sl, hs] = _dot(pt.astype(BF16), docat).astype(BF16)
                dst = pt * (_dot_nt(vc_ref[sl, hs], docat) + at)
                dk = _dot(dst.astype(BF16), qcat) * A_SCALE
                dk_ref[sl, hs] = _unrope(dk, c_ref[sl, :], s1_ref[sl, :], s2_ref[sl, :]).astype(BF16)

    rcur = lambda r, n: r * nb + n
    rnxt = lambda r, n: r * nblk + jnp.minimum((n + 1) * nsub, nblk - 1)
    cur = lambda off: pl.BlockSpec((TQ, A_WIDTH), lambda r, n: (rcur(r, n), off))
    nxu = lambda off: pl.BlockSpec((HEAD, A_WIDTH), lambda r, n: (rnxt(r, n), off))
    tcur = pl.BlockSpec((TQ, LANES), lambda r, n: (rcur(r, n), 0))
    blk, bnx = cur(0), nxu(0)
    return pl.pallas_call(
        body, name=name, grid=(d, nb),
        in_specs=[cur(0), nxu(0), cur(1), cur(2), blk, bnx, blk, bnx, blk, bnx, tcur, tcur, tcur],
        out_specs=[blk, blk],
        out_shape=[jax.ShapeDtypeStruct((S, A_WIDTH), BF16)] * 2,
        compiler_params=_cp(("parallel", "parallel")),
    )(qkv, qkv, qkv, qkv, do, do, lse, lse, adj, adj, *tabs)


def _silu_parts(z):
    sg = _sigmoid(z)
    return z * sg, sg * (1.0 + z * (1.0 - sg))


def _classes_to_tokens(c_ref, d, tm, tmps):
    if d == 1:
        return c_ref[...].astype(F32)
    for k in range(d):
        for c, tmp in enumerate(tmps):
            tmp[pl.ds(k, tm // d, stride=d), :] = c_ref[k, :, c * LANES:(c + 1) * LANES].astype(F32)
    return jnp.concatenate([tmp[...] for tmp in tmps], axis=1)


def _tokens_to_classes(val, c_ref, d, tm, tmps):
    if d == 1:
        c_ref[...] = val.astype(c_ref.dtype)
        return
    for c, tmp in enumerate(tmps):
        tmp[...] = val[:, c * LANES:(c + 1) * LANES]
    for k in range(d):
        c_ref[k] = jnp.concatenate([tmp[pl.ds(k, tm // d, stride=d), :] for tmp in tmps], axis=1).astype(c_ref.dtype)


def _group_spec(S, d, tm):
    if d == 1:
        return pl.BlockSpec((tm, A_WIDTH), lambda i: (i, 0))
    return _class_spec(S, d, tm, A_WIDTH)


def _group_view(t, d):
    return t if d == 1 else t.reshape(d, t.shape[0] // d, t.shape[1])


def _merge_a_fwd(os_, ls_, ur, *, name):
    S = ur.shape[0]
    tm = min(512, S)

    def body(o0, o1, o2, l0, l1, l2, z_ref, y_ref, *tmps):
        ls = [_classes_to_tokens(r, d, tm, tmps) for r, d in zip((l0, l1, l2), DIL)]
        ov = [_classes_to_tokens(r, d, tm, tmps) for r, d in zip((o0, o1, o2), DIL)]
        mx = jnp.maximum(jnp.maximum(ls[0], ls[1]), ls[2])
        es = [jnp.exp(l - mx) for l in ls]
        den = es[0] + es[1] + es[2]
        y = (es[0] / den) * ov[0] + (es[1] / den) * ov[1] + (es[2] / den) * ov[2]
        y_ref[...] = (y * _silu_parts(z_ref[...])[0]).astype(BF16)

    blk = pl.BlockSpec((tm, A_WIDTH), lambda i: (i, 0))
    groups = [_group_spec(S, d, tm) for d in DIL]
    return pl.pallas_call(
        body, name=name, grid=(S // tm,),
        in_specs=groups + groups + [pl.BlockSpec((tm, A_WIDTH), lambda i: (i, R_ZA // A_WIDTH))],
        out_specs=blk, out_shape=jax.ShapeDtypeStruct((S, A_WIDTH), BF16),
        scratch_shapes=[pltpu.VMEM((tm, LANES), F32)] * (A_WIDTH // LANES),
        compiler_params=_cp(("parallel",)))(*[_group_view(t, d) for t, d in zip(os_, DIL)],
                                            *[_group_view(t, d) for t, d in zip(ls_, DIL)], ur)


def _merge_a_bwd(os_, ls_, ur, dya, *, name):
    S = ur.shape[0]
    tm = min(256, S)

    def body(o0, o1, o2, l0, l1, l2, z_ref, dy_ref, d0, d1, d2, a0, a1, a2, dz_ref, *tmps):
        ls = [_classes_to_tokens(r, d, tm, tmps) for r, d in zip((l0, l1, l2), DIL)]
        ov = [_classes_to_tokens(r, d, tm, tmps) for r, d in zip((o0, o1, o2), DIL)]
        mx = jnp.maximum(jnp.maximum(ls[0], ls[1]), ls[2])
        es = [jnp.exp(l - mx) for l in ls]
        den = es[0] + es[1] + es[2]
        ws = [e / den for e in es]
        y = ws[0] * ov[0] + ws[1] * ov[1] + ws[2] * ov[2]
        sz, dsz = _silu_parts(z_ref[...])
        dyv = dy_ref[...]
        dz_ref[...] = (dyv * y * dsz).astype(BF16)
        dyp = dyv * sz
        ts = []
        for h in range(A_HEADS):
            sl = slice(h * HEAD, (h + 1) * HEAD)
            t = jnp.zeros((tm, 1), F32)
            for gi in range(3):
                t = t + ws[gi][:, sl][:, :1] * jnp.sum(dyp[:, sl] * ov[gi][:, sl], axis=-1, keepdims=True)
            ts.append(jnp.broadcast_to(t, (tm, HEAD)))
        tb = jnp.concatenate(ts, axis=1)
        for gi, (dref, aref) in enumerate(((d0, a0), (d1, a1), (d2, a2))):
            _tokens_to_classes(ws[gi] * dyp, dref, DIL[gi], tm, tmps)
            _tokens_to_classes(-ws[gi] * tb, aref, DIL[gi], tm, tmps)

    blk = pl.BlockSpec((tm, A_WIDTH), lambda i: (i, 0))
    groups = [_group_spec(S, d, tm) for d in DIL]
    shaped = lambda dt: [jax.ShapeDtypeStruct((S, A_WIDTH) if d == 1 else (d, S // d, A_WIDTH), dt) for d in DIL]
    outs = pl.pallas_call(
        body, name=name, grid=(S // tm,),
        in_specs=groups + groups + [pl.BlockSpec((tm, A_WIDTH), lambda i: (i, R_ZA // A_WIDTH)), blk],
        out_specs=groups + groups + [blk],
        out_shape=shaped(BF16) + shaped(F32) + [jax.ShapeDtypeStruct((S, A_WIDTH), BF16)],
        scratch_shapes=[pltpu.VMEM((tm, LANES), F32)] * (A_WIDTH // LANES),
        compiler_params=_cp(("parallel",)))(*[_group_view(t, d) for t, d in zip(os_, DIL)],
                                            *[_group_view(t, d) for t, d in zip(ls_, DIL)], ur, dya)
    flat = [t.reshape(S, A_WIDTH) for t in outs[:6]]
    return flat[0:3], flat[3:6], outs[6]


def _logf(ur, bf_pad, *, name):
    S = ur.shape[0]
    tm = min(1024, S)

    def body(u_ref, b_ref, o_ref):
        z = u_ref[...] + b_ref[...]
        o_ref[...] = jnp.minimum(z, 0.0) - jnp.log(1.0 + jnp.exp(-jnp.abs(z)))

    return pl.pallas_call(
        body, name=name, grid=(S // tm,),
        in_specs=[pl.BlockSpec((tm, FB_PAD), lambda i: (i, R_FB // FB_PAD)),
                  pl.BlockSpec((1, FB_PAD), lambda i: (0, 0))],
        out_specs=pl.BlockSpec((tm, FB_PAD), lambda i: (i, 0)),
        out_shape=jax.ShapeDtypeStruct((S, FB_PAD), F32),
        compiler_params=_cp(("parallel",)))(ur, bf_pad)


def _cumsum_lanes(x, reverse, *, name):
    nt, H, _ = x.shape
    R = nt * H

    def body(x_ref, o_ref):
        v = x_ref[...].reshape(R, LANES)
        lane = lax.broadcasted_iota(jnp.int32, (R, LANES), 1)
        row = lax.broadcasted_iota(jnp.int32, (R, LANES), 0)

        def scan(t, step, idx, n, axis):
            while step < n:
                if reverse:
                    t = t + jnp.where(idx < n - step, pltpu.roll(t, n - step, axis), 0.0)
                else:
                    t = t + jnp.where(idx >= step, pltpu.roll(t, step, axis), 0.0)
                step *= 2
            return t

        v = scan(v, 1, lane, LANES, 1)
        total = jnp.broadcast_to(v[:, :1] if reverse else v[:, LANES - 1:], (R, LANES))
        carry = scan(total, H, row, R, 0) - total
        o_ref[...] = (v + carry).reshape(nt, H, LANES)

    return pl.pallas_call(
        body, name=name, out_shape=jax.ShapeDtypeStruct((nt, H, LANES), F32),
        in_specs=[pl.BlockSpec(memory_space=pltpu.VMEM)], out_specs=pl.BlockSpec(memory_space=pltpu.VMEM),
        compiler_params=_cp())(x)


B_SCALE = B_HEAD ** -0.5


def _pair_masks():
    lane = lax.broadcasted_iota(jnp.int32, (1, LANES), 1)
    row = lax.broadcasted_iota(jnp.int32, (LANES, 1), 0)
    return (lane < B_HEAD, lane >= B_HEAD), (row < B_HEAD, row >= B_HEAD)


def _causal_t(T):
    r = lax.broadcasted_iota(jnp.int32, (T, T), 0)
    c = lax.broadcasted_iota(jnp.int32, (T, T), 1)
    return r <= c


def _zero_other(x, keep):
    return jnp.where(keep, x, jnp.zeros_like(x))


def _fox_aug(ub, ckb, *, name):
    S = ub.shape[0]
    T = min(2048, S)

    def body(q_ref, k_ref, c_ref, qa_ref, ka_ref):
        lane = lax.broadcasted_iota(jnp.int32, (1, LANES), 1)
        q = q_ref[...] * B_SCALE
        k = k_ref[...]
        for a in range(2):
            own = (lane < B_HEAD) if a == 0 else (lane >= B_HEAD)
            o = B_HEAD if a == 0 else 0
            c = c_ref[a]
            hi = c.astype(BF16)
            r1 = c - hi.astype(F32)
            mid = r1.astype(BF16)
            lo = (r1 - mid.astype(F32)).astype(BF16)
            pieces = (hi, mid, lo)
            one = jnp.ones((T, LANES), BF16)
            qa = jnp.where(own, q, jnp.zeros_like(q))
            ka = jnp.where(own, k, jnp.zeros_like(k))
            for t in range(3):
                qa = jnp.where(lane == o + t, pieces[t], qa)
                qa = jnp.where(lane == o + 3 + t, one, qa)
                ka = jnp.where(lane == o + t, one, ka)
                ka = jnp.where(lane == o + 3 + t, -pieces[t], ka)
            qa_ref[a] = qa
            ka_ref[a] = ka

    out = pl.BlockSpec((2, T, LANES), lambda h, i: (h, i, 0))
    return pl.pallas_call(
        body, name=name, grid=(B_HEADS // 2, S // T),
        in_specs=[pl.BlockSpec((T, LANES), lambda h, i: (i, h)), pl.BlockSpec((T, LANES), lambda h, i: (i, 4 + h)), out],
        out_specs=[out, out], out_shape=[jax.ShapeDtypeStruct((B_HEADS, S, LANES), BF16)] * 2,
        compiler_params=_cp(("parallel", "parallel")))(ub, ub, ckb)


def _fox_fwd(qaug, kaug, vt, *, name):
    S = qaug.shape[1]
    T = min(512, S)
    nq = S // T

    def body(q_ref, k_ref, vt_ref, o_ref, l_ref, m_s, l_s, acc_s, st_s):
        i = pl.program_id(1)
        _, rows = _pair_masks()
        qm = [q_ref[0], q_ref[1]]
        m_s[...] = jnp.full((2, 1, T), NEG, F32)
        l_s[...] = jnp.zeros((2, 1, T), F32)
        acc_s[...] = jnp.zeros((LANES, T), F32)

        def logits(j):
            off = pl.multiple_of(j * T, T)
            return [_dot_nt(k_ref[a, pl.ds(off, T), :], qm[a]) for a in range(2)]

        def step(j, masked, prefetch):
            nxt = logits(j + 1) if prefetch else None
            vtj = vt_ref[j]
            upd = jnp.zeros((LANES, T), F32)
            alphas = []
            for a in range(2):
                st = st_s[a]
                if masked:
                    st = jnp.where(_causal_t(T), st, NEG)
                m_old = m_s[a]
                m_new = jnp.maximum(m_old, jnp.max(st, axis=0, keepdims=True))
                alpha = jnp.exp(m_old - m_new)
                pt = jnp.exp(st - m_new)
                l_s[a] = alpha * l_s[a] + jnp.sum(pt, axis=0, keepdims=True)
                m_s[a] = m_new
                upd = upd + _dot(_zero_other(vtj, rows[a]), pt.astype(BF16))
                alphas.append(alpha)
            acc_s[...] = acc_s[...] * jnp.where(rows[0], alphas[0], alphas[1]) + upd
            if prefetch:
                st_s[0] = nxt[0]
                st_s[1] = nxt[1]

        def loop(j, carry):
            step(j, False, True)
            return carry

        first = logits(0)
        st_s[0] = first[0]
        st_s[1] = first[1]
        lax.fori_loop(0, i, loop, 0)
        step(i, True, False)
        o_ref[...] = (acc_s[...] / jnp.where(rows[0], l_s[0], l_s[1])).T
        l_ref[0] = m_s[0] + jnp.log(l_s[0])
        l_ref[1] = m_s[1] + jnp.log(l_s[1])

    stat = pl.BlockSpec((2, None, 1, T), lambda h, i: (h, i, 0, 0))
    return pl.pallas_call(
        body, name=name, grid=(B_HEADS // 2, nq),
        in_specs=[pl.BlockSpec((2, T, LANES), lambda h, i: (h, i, 0)),
                  pl.BlockSpec((2, S, LANES), lambda h, i: (h, 0, 0)),
                  pl.BlockSpec((nq, LANES, T), lambda h, i: (0, h, 0))],
        out_specs=[pl.BlockSpec((T, LANES), lambda h, i: (i, h)), stat],
        out_shape=[jax.ShapeDtypeStruct((S, A_WIDTH), F32), jax.ShapeDtypeStruct((B_HEADS, nq, 1, T), F32)],
        scratch_shapes=[pltpu.VMEM((2, 1, T), F32), pltpu.VMEM((2, 1, T), F32), pltpu.VMEM((LANES, T), F32),
                        pltpu.VMEM((2, T, T), F32)],
        compiler_params=_cp(("parallel", "parallel")),
    )(qaug, kaug, vt)


def _fox_delta(o, do, *, name):
    S = o.shape[0]
    T = min(512, S)
    nq = S // T

    per = min(4, nq)

    def body(o_ref, do_ref, d_ref):
        _, rows = _pair_masks()
        for t in range(per):
            sl = slice(t * T, (t + 1) * T)
            prod_t = (do_ref[sl, :].astype(F32) * o_ref[sl, :]).T
            d_ref[0, t] = jnp.sum(_zero_other(prod_t, rows[0]), axis=0, keepdims=True)
            d_ref[1, t] = jnp.sum(_zero_other(prod_t, rows[1]), axis=0, keepdims=True)

    tile = pl.BlockSpec((per * T, LANES), lambda h, i: (i, h))
    return pl.pallas_call(
        body, name=name, grid=(B_HEADS // 2, nq // per), in_specs=[tile, tile],
        out_specs=pl.BlockSpec((2, per, 1, T), lambda h, i: (h, i, 0, 0)),
        out_shape=jax.ShapeDtypeStruct((B_HEADS, nq, 1, T), F32),
        compiler_params=_cp(("parallel", "parallel")))(o, do)


def _fox_bwd(ub, qaug, kaug, kt, do, lse, delta, *, name):
    S = ub.shape[0]
    T = min(512, S)
    nq = S // T

    def body(k_ref, v_ref, kt_ref, q_ref, do_ref, l_ref, dl_ref,
             dk_ref, dv_ref, dck_ref, dqt_ref, dcq_ref, dk_s, dv_s, dc_s):
        j = pl.program_id(1)
        lanes, rows = _pair_masks()
        vv = v_ref[...]
        ktj = kt_ref[...]
        km = [k_ref[0], k_ref[1]]
        ktm = [_zero_other(ktj, rows[0]), _zero_other(ktj, rows[1])]
        dk_s[...] = jnp.zeros((2, T, LANES), F32)
        dv_s[...] = jnp.zeros((T, LANES), F32)
        dc_s[...] = jnp.zeros((2, T, 1), F32)

        @pl.when(j == 0)
        def _():
            dqt_ref[...] = jnp.zeros((nq, LANES, T), F32)
            dcq_ref[...] = jnp.zeros((2, nq, 1, T), F32)

        def step(i, masked):
            off = pl.multiple_of(i * T, T)
            doi = do_ref[pl.ds(off, T), :]
            upd = jnp.zeros((LANES, T), F32)
            for a in range(2):
                qi = q_ref[a, pl.ds(off, T), :]
                st = _dot_nt(km[a], qi)
                if masked:
                    st = jnp.where(_causal_t(T), st, NEG)
                pt = jnp.exp(st - l_ref[a, i])
                doa = _zero_other(doi, lanes[a])
                dv_s[...] += _dot(pt.astype(BF16), doa)
                dst = pt * (_dot_nt(vv, doa) - dl_ref[a, i])
                dsb = dst.astype(BF16)
                dk_s[a] += _dot(dsb, qi)
                upd = upd + _dot(ktm[a], dsb)
                dc_s[a] -= jnp.sum(dst, axis=-1, keepdims=True)
                dcq_ref[a, i] += jnp.sum(dst, axis=0, keepdims=True)
            dqt_ref[i] += upd

        def loop(i, carry):
            step(i, False)
            return carry

        step(j, True)
        lax.fori_loop(j + 1, nq, loop, 0)
        dk_ref[...] = jnp.where(lanes[0], dk_s[0], dk_s[1]).astype(BF16)
        dv_ref[...] = dv_s[...].astype(BF16)
        dck_ref[...] = dc_s[...]

    rowv = pl.BlockSpec((2, nq, 1, T), lambda h, j: (h, 0, 0, 0))
    tile = pl.BlockSpec((T, LANES), lambda h, j: (j, h))
    return pl.pallas_call(
        body, name=name, grid=(B_HEADS // 2, nq),
        in_specs=[pl.BlockSpec((2, T, LANES), lambda h, j: (h, j, 0)),
                  pl.BlockSpec((T, LANES), lambda h, j: (j, 8 + h)),
                  pl.BlockSpec((None, LANES, T), lambda h, j: (j, h, 0)),
                  pl.BlockSpec((2, S, LANES), lambda h, j: (h, 0, 0)),
                  pl.BlockSpec((S, LANES), lambda h, j: (0, h)),
                  rowv, rowv],
        out_specs=[tile, tile, pl.BlockSpec((2, T, 1), lambda h, j: (h, j, 0)),
                   pl.BlockSpec((nq, LANES, T), lambda h, j: (0, h, 0)), rowv],
        out_shape=[jax.ShapeDtypeStruct((S, A_WIDTH), BF16)] * 2 + [jax.ShapeDtypeStruct((B_HEADS, S, 1), F32),
                   jax.ShapeDtypeStruct((nq, A_WIDTH, T), F32), jax.ShapeDtypeStruct((B_HEADS, nq, 1, T), F32)],
        scratch_shapes=[pltpu.VMEM((2, T, LANES), F32), pltpu.VMEM((T, LANES), F32), pltpu.VMEM((2, T, 1), F32)],
        compiler_params=_cp(("parallel", "arbitrary")),
    )(kaug, ub, kt, qaug, do, lse, delta)


def _gate_fwd(o, ur, zcol, *, name):
    S = ur.shape[0]
    tm = min(1024, S)

    def body(o_ref, z_ref, y_ref):
        y_ref[...] = (o_ref[...] * _silu_parts(z_ref[...])[0]).astype(BF16)

    blk = pl.BlockSpec((tm, A_WIDTH), lambda i: (i, 0))
    return pl.pallas_call(
        body, name=name, grid=(S // tm,),
        in_specs=[blk, pl.BlockSpec((tm, A_WIDTH), lambda i: (i, zcol // A_WIDTH))],
        out_specs=blk, out_shape=jax.ShapeDtypeStruct((S, A_WIDTH), BF16),
        compiler_params=_cp(("parallel",)))(o, ur)


def _gate_bwd(o, ur, zcol, dy, *, name):
    S = ur.shape[0]
    tm = min(1024, S)

    def body(o_ref, z_ref, dy_ref, do_ref, dz_ref):
        sz, dsz = _silu_parts(z_ref[...])
        dyv = dy_ref[...]
        do_ref[...] = (dyv * sz).astype(BF16)
        dz_ref[...] = (dyv * o_ref[...] * dsz).astype(BF16)

    blk = pl.BlockSpec((tm, A_WIDTH), lambda i: (i, 0))
    return pl.pallas_call(
        body, name=name, grid=(S // tm,),
        in_specs=[blk, pl.BlockSpec((tm, A_WIDTH), lambda i: (i, zcol // A_WIDTH)), blk],
        out_specs=[blk, blk], out_shape=[jax.ShapeDtypeStruct((S, A_WIDTH), BF16)] * 2,
        compiler_params=_cp(("parallel",)))(o, ur, dy)


def _dfb(ur, bf_pad, dlogf_pad, *, name):
    S = ur.shape[0]
    tm = min(1024, S)

    def body(u_ref, b_ref, d_ref, o_ref, s_ref):
        i = pl.program_id(0)
        dv = d_ref[...] * _sigmoid(-(u_ref[...] + b_ref[...]))
        o_ref[...] = dv.astype(BF16)
        part = jnp.sum(dv, axis=0, keepdims=True)

        @pl.when(i == 0)
        def _():
            s_ref[...] = part

        @pl.when(i > 0)
        def _():
            s_ref[...] += part

    vec = pl.BlockSpec((1, FB_PAD), lambda i: (0, 0))
    blk = pl.BlockSpec((tm, FB_PAD), lambda i: (i, 0))
    return pl.pallas_call(
        body, name=name, grid=(S // tm,),
        in_specs=[pl.BlockSpec((tm, FB_PAD), lambda i: (i, R_FB // FB_PAD)), vec, blk],
        out_specs=[blk, vec],
        out_shape=[jax.ShapeDtypeStruct((S, FB_PAD), BF16), jax.ShapeDtypeStruct((1, FB_PAD), F32)],
        compiler_params=_cp(("arbitrary",)))(ur, bf_pad, dlogf_pad)


M_SCALE = HEAD ** -0.5


def _mem_fwd(ur, mkv, *, name):
    S = ur.shape[0]
    T = min(512, S)

    def body(q_ref, z_ref, k_ref, v_ref, y_ref):
        for h in range(M_HEADS):
            hs = slice(h * HEAD, (h + 1) * HEAD)
            s = _dot_nt(q_ref[:, hs].astype(BF16), k_ref[:, hs].astype(BF16)) * M_SCALE
            p = jnp.exp(s - jnp.max(s, axis=-1, keepdims=True))
            p = p / jnp.sum(p, axis=-1, keepdims=True)
            o = _dot(p.astype(BF16), v_ref[:, hs].astype(BF16))
            y_ref[:, hs] = (o * _silu_parts(z_ref[:, hs])[0]).astype(BF16)

    wide = lambda col: pl.BlockSpec((T, A_WIDTH), lambda i: (i, col // A_WIDTH))
    kv = lambda half: pl.BlockSpec((N_MEM, A_WIDTH), lambda i: (0, half))
    return pl.pallas_call(
        body, name=name, grid=(S // T,),
        in_specs=[wide(R_QM), wide(R_ZM), kv(0), kv(1)],
        out_specs=pl.BlockSpec((T, A_WIDTH), lambda i: (i, 0)),
        out_shape=jax.ShapeDtypeStruct((S, A_WIDTH), BF16),
        compiler_params=_cp(("parallel",)))(ur, ur, mkv, mkv)


def _mem_bwd(ur, mkv, dy, *, name):
    S = ur.shape[0]
    T = min(512, S)

    def body(q_ref, z_ref, k_ref, v_ref, dy_ref, dq_ref, dz_ref, dk_ref, dv_ref):
        i = pl.program_id(0)

        @pl.when(i == 0)
        def _():
            dk_ref[...] = jnp.zeros((N_MEM, A_WIDTH), F32)
            dv_ref[...] = jnp.zeros((N_MEM, A_WIDTH), F32)

        for h in range(M_HEADS):
            hs = slice(h * HEAD, (h + 1) * HEAD)
            qv = q_ref[:, hs].astype(BF16)
            kv = k_ref[:, hs].astype(BF16)
            vv = v_ref[:, hs].astype(BF16)
            s = _dot_nt(qv, kv) * M_SCALE
            p = jnp.exp(s - jnp.max(s, axis=-1, keepdims=True))
            p = p / jnp.sum(p, axis=-1, keepdims=True)
            o = _dot(p.astype(BF16), vv)
            sz, dsz = _silu_parts(z_ref[:, hs])
            dyv = dy_ref[:, hs]
            dz_ref[:, hs] = (dyv * o * dsz).astype(BF16)
            dov = (dyv * sz).astype(BF16)
            dp = _dot_nt(dov, vv)
            ds = p * (dp - jnp.sum(p * dp, axis=-1, keepdims=True))
            dq_ref[:, hs] = (_dot(ds.astype(BF16), kv) * M_SCALE).astype(BF16)
            dv_ref[:, hs] += _dot(p.T.astype(BF16), dov)
            dk_ref[:, hs] += _dot(ds.T.astype(BF16), qv) * M_SCALE

    wide = lambda col: pl.BlockSpec((T, A_WIDTH), lambda i: (i, col // A_WIDTH))
    kv = lambda half: pl.BlockSpec((N_MEM, A_WIDTH), lambda i: (0, half))
    tile = pl.BlockSpec((T, A_WIDTH), lambda i: (i, 0))
    acc = pl.BlockSpec((N_MEM, A_WIDTH), lambda i: (0, 0))
    return pl.pallas_call(
        body, name=name, grid=(S // T,),
        in_specs=[wide(R_QM), wide(R_ZM), kv(0), kv(1), tile],
        out_specs=[tile, tile, acc, acc],
        out_shape=[jax.ShapeDtypeStruct((S, A_WIDTH), BF16)] * 2
        + [jax.ShapeDtypeStruct((N_MEM, A_WIDTH), F32)] * 2,
        compiler_params=_cp(("arbitrary",)))(ur, ur, mkv, mkv, dy)


def _branch_fwd(ys, wbs, ur, b_merge, *, name):
    S = ur.shape[0]
    tm, tn = min(512, S), 512
    nj = D_MODEL // tn

    def body(ya, yb, ym, wa, wb, wm, g0, g1, g2, b0, b1, b2, mg_ref, p_ref):
        acc = jnp.zeros((tm, tn), F32)
        for i, (y, w, gr, br) in enumerate(((ya, wa, g0, b0), (yb, wb, g1, b1), (ym, wm, g2, b2))):
            pr = _dot(y[...], w[...])
            p_ref[i] = pr.astype(BF16)
            acc = acc + _sigmoid(gr[...] + br[...]) * pr
        mg_ref[...] = acc.astype(BF16)

    yspec = pl.BlockSpec((tm, A_WIDTH), lambda i, j: (i, 0))
    wspec = pl.BlockSpec((A_WIDTH, tn), lambda i, j: (0, j))
    gspec = lambda b: pl.BlockSpec((tm, tn), lambda i, j: (i, (R_GL + b * D_MODEL) // tn + j))
    bspec = lambda b: pl.BlockSpec((1, tn), lambda i, j: (0, b * nj + j))
    return pl.pallas_call(
        body, name=name, grid=(S // tm, nj),
        in_specs=[yspec] * 3 + [wspec] * 3 + [gspec(0), gspec(1), gspec(2), bspec(0), bspec(1), bspec(2)],
        out_specs=[pl.BlockSpec((tm, tn), lambda i, j: (i, j)),
                   pl.BlockSpec((3, tm, tn), lambda i, j: (0, i, j))],
        out_shape=[jax.ShapeDtypeStruct((S, D_MODEL), BF16), jax.ShapeDtypeStruct((3, S, D_MODEL), BF16)],
        compiler_params=_cp(("parallel", "parallel")))(*ys, *wbs, ur, ur, ur, b_merge, b_merge, b_merge)


def _branch_bwd(dm, prods, ur, b_merge, *, name):
    S = ur.shape[0]
    tm = min(256, S)

    def body(dm_ref, p_ref, g0, g1, g2, b_ref, dp_ref, dgl_ref, db_ref):
        i = pl.program_id(0)
        dmv = dm_ref[...]
        parts = []
        for b, gr in enumerate((g0, g1, g2)):
            sl = slice(b * D_MODEL, (b + 1) * D_MODEL)
            gt = _sigmoid(gr[...] + b_ref[:, sl])
            dp_ref[b] = (dmv * gt).astype(BF16)
            dgl = dmv * p_ref[b].astype(F32) * gt * (1.0 - gt)
            dgl_ref[:, sl] = dgl.astype(BF16)
            parts.append(jnp.sum(dgl, axis=0, keepdims=True))
        part = jnp.concatenate(parts, axis=1)

        @pl.when(i == 0)
        def _():
            db_ref[...] = part

        @pl.when(i > 0)
        def _():
            db_ref[...] += part

    gspec = lambda b: pl.BlockSpec((tm, D_MODEL), lambda i: (i, R_GL // D_MODEL + b))
    vec = pl.BlockSpec((1, 3 * D_MODEL), lambda i: (0, 0))
    return pl.pallas_call(
        body, name=name, grid=(S // tm,),
        in_specs=[pl.BlockSpec((tm, D_MODEL), lambda i: (i, 0)),
                  pl.BlockSpec((3, tm, D_MODEL), lambda i: (0, i, 0)), gspec(0), gspec(1), gspec(2), vec],
        out_specs=[pl.BlockSpec((3, tm, D_MODEL), lambda i: (0, i, 0)),
                   pl.BlockSpec((tm, 3 * D_MODEL), lambda i: (i, 0)), vec],
        out_shape=[jax.ShapeDtypeStruct((3, S, D_MODEL), BF16), jax.ShapeDtypeStruct((S, 3 * D_MODEL), BF16),
                   jax.ShapeDtypeStruct((1, 3 * D_MODEL), F32)],
        compiler_params=_cp(("arbitrary",)))(dm, prods, ur, ur, ur, b_merge)


def _rope_tables(pos):
    half = ROT // 2
    S = pos.shape[0]
    inv = ROPE_THETA ** (-jnp.arange(half, dtype=F32) / half)
    per_row = LANES // half
    ang = jnp.repeat(pos.astype(F32).reshape(S // per_row, per_row), half, axis=1) * jnp.tile(inv, per_row)
    cos, sin = jnp.cos(ang).reshape(S, half), jnp.sin(ang).reshape(S, half)
    one = jnp.ones((S, LANES - ROT), F32)
    zero = jnp.zeros((S, LANES - ROT), F32)
    zh = jnp.zeros((S, half), F32)
    c = jnp.concatenate([cos, cos, one], axis=1)
    s1 = jnp.concatenate([-sin, zh, zero], axis=1)
    s2 = jnp.concatenate([zh, sin, zero], axis=1)
    return c, s1, s2


def _to_tiles(t):
    S, H = t.shape
    return t.reshape(S // LANES, LANES, H).transpose(0, 2, 1)


def _from_tiles(t):
    nt, H, _ = t.shape
    return t.transpose(1, 0, 2).reshape(H, nt * LANES)


def _local_step(x, mem, pos, tgt, g_pre, g_post, g_mem, wt, bf_pad, b_merge, w_kv, wbs, w_out, pack=None):
    S = x.shape[0]
    T = min(512, S)
    nq = S // T
    tabs = _rope_tables(pos)

    hs = _rms_fwd(x, g_pre, name="rms_pre", dilations=DIL)
    h = hs[0]
    tabs_g = [[_to_classes(t, d) for t in tabs] for d in DIL]
    qkvs = [_proj_rope(hs[g], wt[f"A{g}"], tabs_g[g], name=f"proj_a{g}") for g in range(3)]
    ub = _mm(h, wt["B"], bt=True, out_dtype=BF16, name="proj_b", tn=1536)
    ur = _mm(h, wt["R"], bt=True, name="proj_r", tn=1792)

    outs_c, lses_c = [], []
    for g in range(3):
        o, l = _attn_a_fwd(qkvs[g], g, name=f"attn_a_fwd{g}")
        outs_c.append(o)
        lses_c.append(l)
    ya = _merge_a_fwd(outs_c, lses_c, ur, name="merge_a_fwd")

    logf = _logf(ur, bf_pad, name="logf")
    c = _from_tiles(_cumsum_lanes(_to_tiles(logf[:, :B_HEADS]), False, name="cumsum_fwd"))
    ckb = jnp.broadcast_to(c[:, :, None], (B_HEADS, S, LANES))
    qaug, kaug = _fox_aug(ub, ckb, name="fox_aug")
    kt = ub[:, 512:1024].reshape(nq, T, 512).transpose(0, 2, 1)
    vt = ub[:, 1024:1536].reshape(nq, T, 512).transpose(0, 2, 1)
    ob, lse_b = _fox_fwd(qaug, kaug, vt, name="fox_fwd")
    yb = _gate_fwd(ob, ur, R_ZB, name="gate_b_fwd")

    hm = _rms_fwd(mem, g_mem, name="rms_mem")
    mkv = _mm(hm, w_kv, name="proj_mem")
    ym = _mem_fwd(ur, mkv, name="mem_fwd")

    merged, prods = _branch_fwd((ya, yb, ym), wbs, ur, b_merge, name="branch_fwd")
    out = _mm(merged, w_out, name="proj_out")
    dy, d_out, dg_post, loss_row = _post(x, out, tgt, g_post, name="post")

    dmerged = _mm(d_out, w_out, bt=True, name="d_merged")
    dw_out = _mm(merged, d_out, at=True, name="dw_out", tk=2048)
    dprods, dgl, db_merge = _branch_bwd(dmerged, prods, ur, b_merge, name="branch_bwd")
    dys, dwbs = [], []
    for i, (y, wb) in enumerate(zip((ya, yb, ym), wbs)):
        dys.append(_mm(dprods[i], wb, bt=True, name=f"d_y{i}"))
        dwbs.append(_mm(y, dprods[i], at=True, name=f"dw_branch{i}", tk=2048))

    dos_c, adjs_c, dza = _merge_a_bwd(outs_c, lses_c, ur, dys[0], name="merge_a_bwd")
    dus_a = []
    for g, d in enumerate(DIL):
        do_c, adj_c = dos_c[g], adjs_c[g]
        dq =_attn_a_dq(qkvs[g], tabs_g[g], g, do_c, lses_c[g], adj_c, name=f"attn_a_dq{g}")
        dk, dv = _attn_a_dkv(qkvs[g], tabs_g[g], g, do_c, lses_c[g], adj_c, name=f"attn_a_dkv{g}")
        dus_a.append(jnp.concatenate([dq, dk, dv], axis=1))

    dob, dzb = _gate_bwd(ob, ur, R_ZB, dys[1], name="gate_b_bwd")
    delta_b = _fox_delta(ob, dob, name="fox_delta")
    dkb, dvb, dc_k, dqt, dc_q = _fox_bwd(ub, qaug, kaug, kt, dob, lse_b, delta_b, name="fox_bwd")
    dqb = (dqt.transpose(0, 2, 1).reshape(S, A_WIDTH) * B_SCALE).astype(BF16)
    du_b = jnp.concatenate([dqb, dkb, dvb], axis=1)
    dc = dc_q.reshape(B_HEADS, S) + dc_k.reshape(B_HEADS, S)
    dlogf = _from_tiles(_cumsum_lanes(_to_tiles(dc.T), True, name="cumsum_bwd"))
    dlogf_pad = jnp.pad(dlogf.T, ((0, 0), (0, FB_PAD - B_HEADS)))
    dfb, db_forget = _dfb(ur, bf_pad, dlogf_pad, name="dfb")

    dqm, dzm, dmk, dmv = _mem_bwd(ur, mkv, dys[2], name="mem_bwd")
    dmkv = jnp.concatenate([dmk, dmv], axis=1).astype(BF16)
    dhm = _mm(dmkv, w_kv, bt=True, name="d_hm")
    dw_kv = _mm(hm, dmkv, at=True, name="dw_kv")
    dg_mem = _rms_bwd(mem, g_mem, dhm, None, name="rms_mem_bwd")

    du_r = jnp.concatenate([dza, dzb, dqm, dzm, dgl, dfb], axis=1)
    dwt = {"R": _mm(du_r, h, at=True, name="dw_in_r", tm=1792, tk=1024),
           "B": _mm(du_b, h, at=True, name="dw_in_b", tm=1536, tk=2048)}
    for g in range(3):
        dwt[f"A{g}"] = _mm(dus_a[g], hs[g], at=True, name=f"dw_in_a{g}", tm=1536, tk=2048)
    res = dict(dwt=dwt, dw_kv=dw_kv, dwbs=dwbs, dw_out=dw_out)
    token_major = [(du_r, wt["R"]), (du_b, wt["B"]), (dus_a[0], wt["A0"])]
    if pack is None:
        dh_1 = _mm(dus_a[1], wt["A1"], name="d_h_a1", tk=1536)
        dh = _mm_sum(token_major, name="d_h_main")
    else:
        gbig = pack(dwt, dw_kv, dwbs, dw_out)
        own_idx = _own_slabs()
        dh_1, sib = _mm(dus_a[1], wt["A1"], name="d_h_a1", tk=1536, comm=_pair_comm(gbig))
        send = _pair_sum(gbig, sib, own_idx, 208, name="pair_sum")
        dh, recv = _mm_sum(token_major, name="d_h_main", comm=_chips_comm(send))
        res = dict(parts=[(gbig, None), (sib, 1), (recv, N_CHIP - 1)], own_idx=own_idx)
    dh_2 = _mm(dus_a[2], wt["A2"], name="d_h_a2", tk=1536)
    grad_x, dg_pre = _rms_bwd(x, g_pre, dh, dy, name="rms_pre_bwd", dh_classes=[(dh_1, DIL[1]), (dh_2, DIL[2])])

    return dict(res, loss=loss_row, grad_x=grad_x, dg_pre=dg_pre, dg_post=dg_post, dg_mem=dg_mem,
                db_forget=db_forget, db_merge=db_merge)


MESH = pl.DeviceIdType.MESH
ANY = pl.BlockSpec(memory_space=pl.ANY)


def _relations():
    return [(k >> 2 & 1, k >> 1 & 1, k & 1) for k in range(1, N_DEV)]


def _coords():
    return lax.axis_index("x"), lax.axis_index("y"), lax.axis_index("c")


def _all_gather(shard, *, name):
    R, W = shard.shape

    def body(x_ref, out_ref, send_sems, recv_sems, local_sem):
        x, y, c = _coords()
        me, sibling = (x, y, c), (x, y, 1 - c)
        chips = [(1 - x, y), (x, 1 - y), (1 - x, 1 - y)]

        def slot(px, py, pc):
            return out_ref.at[4 * px + 2 * py + pc]

        def copy(k, block, to, src=None):
            return pltpu.make_async_remote_copy(
                src_ref=slot(*block) if src is None else src, dst_ref=slot(*block),
                send_sem=send_sems.at[k], recv_sem=recv_sems.at[k], device_id=to, device_id_type=MESH)

        mine = pltpu.make_async_copy(x_ref, slot(*me), local_sem)
        mine.start()
        first = [copy(0, me, sibling, src=x_ref)]
        first += [copy(1 + j, me, (*chip, c), src=x_ref) for j, chip in enumerate(chips)]
        for cp in first:
            cp.start()
        passed = [copy(4 + j, (*chip, c), sibling) for j, chip in enumerate(chips)]
        for j, chip in enumerate(chips):
            copy(1 + j, (*chip, c), me).wait_recv()
            passed[j].start()
        copy(0, sibling, me).wait_recv()
        for j, chip in enumerate(chips):
            copy(4 + j, (*chip, 1 - c), me).wait_recv()
        for cp in first + passed:
            cp.wait_send()
        mine.wait()

    return pl.pallas_call(
        body, name=name, out_shape=jax.ShapeDtypeStruct((N_DEV, R, W), shard.dtype),
        in_specs=[ANY], out_specs=ANY,
        scratch_shapes=[pltpu.SemaphoreType.DMA((N_DEV - 1,)), pltpu.SemaphoreType.DMA((N_DEV - 1,)),
                        pltpu.SemaphoreType.DMA],
    )(shard)


N_CHIP = 4


def _pair_comm(gbig):
    _, R, W = gbig.shape

    def copies(g_ref, sib_ref, send_sems, recv_sems):
        x, y, c = _coords()
        return [pltpu.make_async_remote_copy(
            src_ref=g_ref.at[4 * (x ^ (r >> 1)) + 2 * (y ^ (r & 1)) + (1 - c)], dst_ref=sib_ref.at[r],
            send_sem=send_sems.at[r], recv_sem=recv_sems.at[r], device_id=(x, y, 1 - c), device_id_type=MESH)
            for r in range(N_CHIP)]

    def start(*refs):
        for cp in copies(*refs):
            cp.start()

    def wait(*refs):
        cps = copies(*refs)
        for cp in cps:
            cp.wait_recv()
        for cp in cps:
            cp.wait_send()

    return dict(inputs=[gbig], out_shape=[jax.ShapeDtypeStruct((N_CHIP, R, W), gbig.dtype)],
                sems=[pltpu.SemaphoreType.DMA((N_CHIP,)), pltpu.SemaphoreType.DMA((N_CHIP,))],
                start=start, wait=wait)


def _own_slabs():
    x, y, c = _coords()
    return jnp.stack([4 * (x ^ (r >> 1)) + 2 * (y ^ (r & 1)) + c for r in range(N_CHIP)]).astype(jnp.int32)


def _pair_sum(gbig, sib, own_idx, tr, *, name):
    _, R, W = gbig.shape

    def body(idx_ref, a_ref, b_ref, o_ref):
        o_ref[...] = (a_ref[...] + b_ref[...]).astype(BF16)

    return pl.pallas_call(
        body, name=name,
        grid_spec=pltpu.PrefetchScalarGridSpec(
            num_scalar_prefetch=1, grid=(N_CHIP - 1, R // tr),
            in_specs=[pl.BlockSpec((None, tr, W), lambda r, i, idx: (idx[r + 1], i, 0)),
                      pl.BlockSpec((None, tr, W), lambda r, i, idx: (r + 1, i, 0))],
            out_specs=pl.BlockSpec((None, tr, W), lambda r, i, idx: (r, i, 0))),
        out_shape=jax.ShapeDtypeStruct((N_CHIP - 1, R, W), BF16),
        compiler_params=_cp(("parallel", "parallel")))(own_idx, gbig, sib)


def _chips_comm(send):
    nb, R, W = send.shape

    def copies(b_ref, rb_ref, send_sems, recv_sems):
        x, y, c = _coords()
        return [pltpu.make_async_remote_copy(
            src_ref=b_ref.at[r - 1], dst_ref=rb_ref.at[r - 1], send_sem=send_sems.at[r - 1],
            recv_sem=recv_sems.at[r - 1], device_id=(x ^ (r >> 1), y ^ (r & 1), c), device_id_type=MESH)
            for r in range(1, N_CHIP)]

    def start(*refs):
        for cp in copies(*refs):
            cp.start()

    def wait(*refs):
        cps = copies(*refs)
        for cp in cps:
            cp.wait_recv()
        for cp in cps:
            cp.wait_send()

    return dict(inputs=[send], out_shape=[jax.ShapeDtypeStruct((nb, R, W), send.dtype)],
                sems=[pltpu.SemaphoreType.DMA((nb,)), pltpu.SemaphoreType.DMA((nb,))],
                start=start, wait=wait)


def _gather_small(gsmall, *, name):
    n = N_DEV - 1

    def body(s_ref, rs_ref, send_sems, recv_sems, local_sem):
        x, y, c = _coords()
        me = 4 * x + 2 * y + c
        mine = pltpu.make_async_copy(s_ref, rs_ref.at[me], local_sem)
        mine.start()

        def copy(k, fx, fy, fc, slot):
            return pltpu.make_async_remote_copy(
                src_ref=s_ref, dst_ref=rs_ref.at[slot], send_sem=send_sems.at[k], recv_sem=recv_sems.at[k],
                device_id=(x ^ fx, y ^ fy, c ^ fc), device_id_type=MESH)

        started = [copy(k, *rel, me) for k, rel in enumerate(_relations())]
        for cp in started:
            cp.start()
        for k, (fx, fy, fc) in enumerate(_relations()):
            copy(k, fx, fy, fc, 4 * (x ^ fx) + 2 * (y ^ fy) + (c ^ fc)).wait_recv()
        for cp in started:
            cp.wait_send()
        mine.wait()

    return pl.pallas_call(
        body, name=name, out_shape=jax.ShapeDtypeStruct((N_DEV, 1, P_SMALL), gsmall.dtype),
        in_specs=[ANY], out_specs=ANY,
        scratch_shapes=[pltpu.SemaphoreType.DMA((n,)), pltpu.SemaphoreType.DMA((n,)), pltpu.SemaphoreType.DMA],
    )(gsmall)


def _part_specs(parts, tr, row0):
    assert row0 % tr == 0
    specs = []
    for a, n_used in parts:
        if n_used is None:
            specs.append(pl.BlockSpec((1, tr, a.shape[2]), lambda i, idx: (idx[0], row0 // tr + i, 0)))
        else:
            specs.append(pl.BlockSpec((n_used, tr, a.shape[2]), lambda i, idx: (0, row0 // tr + i, 0)))
    return specs


def _part_total(refs, parts):
    g = None
    for ref, (_, n_used) in zip(refs, parts):
        for k in range(n_used or 1):
            t = ref[k].astype(F32)
            g = t if g is None else g + t
    return g


def _sum_parts(parts, idx, row0, nrows, tr, *, name):
    W = parts[0][0].shape[2]
    assert nrows % tr == 0

    def body(idx_ref, *refs):
        refs[-1][...] = _part_total(refs[:-1], parts)

    return pl.pallas_call(
        body, name=name,
        grid_spec=pltpu.PrefetchScalarGridSpec(
            num_scalar_prefetch=1, grid=(nrows // tr,), in_specs=_part_specs(parts, tr, row0),
            out_specs=pl.BlockSpec((tr, W), lambda i, idx: (i, 0))),
        out_shape=jax.ShapeDtypeStruct((nrows, W), F32),
        compiler_params=_cp(("parallel",)))(idx, *[a for a, _ in parts])


def _adamw(parts, idx, w, m, v, tr, *, name):
    R, W = w.shape
    assert R % tr == 0
    np_ = len(parts)

    def body(idx_ref, *refs):
        w_ref, m_ref, v_ref, g_ref, d_ref, nm_ref, nv_ref = refs[np_:]
        g = _part_total(refs[:np_], parts)
        mm = ADAM_B1 * m_ref[...] + (1.0 - ADAM_B1) * g
        vv = ADAM_B2 * v_ref[...] + (1.0 - ADAM_B2) * (g * g)
        m_hat = mm / (1.0 - ADAM_B1 ** ADAM_STEP)
        v_hat = vv / (1.0 - ADAM_B2 ** ADAM_STEP)
        g_ref[...] = g
        d_ref[...] = -ADAM_LR * (m_hat / (jnp.sqrt(v_hat) + ADAM_EPS) + ADAM_WD * w_ref[...])
        nm_ref[...] = mm
        nv_ref[...] = vv

    blk = pl.BlockSpec((tr, W), lambda i, idx: (i, 0))
    return pl.pallas_call(
        body, name=name,
        grid_spec=pltpu.PrefetchScalarGridSpec(
            num_scalar_prefetch=1, grid=(R // tr,), in_specs=_part_specs(parts, tr, 0) + [blk, blk, blk],
            out_specs=[blk] * 4),
        out_shape=[jax.ShapeDtypeStruct((R, W), F32)] * 4,
        compiler_params=_cp(("parallel",)))(idx, *[a for a, _ in parts], w, m, v)


def _pack_rest(w_kv, wa, wb, wm, w_out):
    return jnp.concatenate([w_kv[0], w_out[0]] + [t[0].reshape(-1, D_MODEL) for t in (wa, wb, wm)], axis=0)


def _unpack_rest(t):
    br = lambda i: t[RO_BR + 64 * i:RO_BR + 64 * (i + 1)].reshape(1, A_WIDTH, D_MODEL // N_DEV)
    return t[None, RO_KV:RO_OUT], br(0), br(1), br(2), t[None, RO_OUT:RO_BR]


def _orig_rows(gathered, a, b):
    res = []
    while a < b:
        dev, r = divmod(a, CS)
        n = min(b - a, CS - r)
        res.append(gathered[dev, RO_IN + r:RO_IN + r + n])
        a += n
    return res


def _full_weights(gathered):
    wt = {}
    for name, ranges in SEGS.items():
        rows = [p for a, b in ranges for p in _orig_rows(gathered, a, b)]
        if SEG_PAD[name]:
            rows.append(jnp.zeros((SEG_PAD[name], D_MODEL), gathered.dtype))
        wt[name] = jnp.concatenate(rows, axis=0)
    w_kv = gathered[:, RO_KV:RO_OUT].reshape(D_MODEL, D_MODEL)
    w_out = gathered[:, RO_OUT:RO_BR].reshape(D_MODEL, D_MODEL)
    wbs = [gathered[:, RO_BR + 64 * i:RO_BR + 64 * (i + 1)].reshape(N_DEV, A_WIDTH, D_MODEL // N_DEV)
           .transpose(1, 0, 2).reshape(A_WIDTH, D_MODEL) for i in range(3)]
    return wt, w_kv, wbs, w_out


def _orig_order(dwt):
    pieces = []
    for name, ranges in SEGS.items():
        o = 0
        for a, b in ranges:
            pieces.append((a, dwt[name][o:o + b - a]))
            o += b - a
    pieces.sort(key=lambda p: p[0])
    return jnp.concatenate([p[1] for p in pieces], axis=0)


def _pack_grads(dwt, dw_kv, dwbs, dw_out):
    g_in = jnp.pad(_orig_order(dwt).reshape(N_DEV, CS, D_MODEL), ((0, 0), (0, IN_ROWS - CS), (0, 0)))
    br = [t.reshape(A_WIDTH, N_DEV, D_MODEL // N_DEV).transpose(1, 0, 2).reshape(N_DEV, -1, D_MODEL) for t in dwbs]
    return jnp.concatenate([dw_kv.reshape(N_DEV, -1, D_MODEL), dw_out.reshape(N_DEV, -1, D_MODEL)] + br + [g_in],
                           axis=1)


def kernel(x, mem, positions, norm_pre_g, norm_post_g, norm_mem_g, w_in, b_forget, b_merge, w_mem_kv, w_branch_a, w_branch_b, w_branch_m, w_out, loss_target, m_norm_pre_g, m_norm_post_g, m_norm_mem_g, m_w_in, m_b_forget, m_b_merge, m_w_mem_kv, m_w_branch_a, m_w_branch_b, m_w_branch_m, m_w_out, v_norm_pre_g, v_norm_post_g, v_norm_mem_g, v_w_in, v_b_forget, v_b_merge, v_w_mem_kv, v_w_branch_a, v_w_branch_b, v_w_branch_m, v_w_out):
    w_rest = _pack_rest(w_mem_kv, w_branch_a, w_branch_b, w_branch_m, w_out)
    shard = jnp.concatenate([w_rest.astype(BF16), w_in[0].T.astype(BF16),
                             jnp.zeros((IN_ROWS - CS, D_MODEL), BF16)], axis=0)
    gathered = _all_gather(shard, name="gather_weights")
    wt, w_kv, wbs, w_o = _full_weights(gathered)

    bf_pad = jnp.pad(b_forget, ((0, 0), (0, FB_PAD - B_HEADS)))
    r = _local_step(x[0], mem[0], positions[0], loss_target[0], norm_pre_g, norm_post_g, norm_mem_g,
                    wt, bf_pad, b_merge, w_kv, wbs, w_o, pack=_pack_grads)

    gsmall = jnp.concatenate([r["dg_pre"], r["dg_post"], r["dg_mem"], r["db_merge"],
                              r["db_forget"][:, :LANES], r["loss"]], axis=1)
    rsmall = _gather_small(gsmall, name="gather_small")
    parts, own_idx = r["parts"], r["own_idx"]

    m_rest = _pack_rest(m_w_mem_kv, m_w_branch_a, m_w_branch_b, m_w_branch_m, m_w_out)
    v_rest = _pack_rest(v_w_mem_kv, v_w_branch_a, v_w_branch_b, v_w_branch_m, v_w_out)
    outs_rest = [_unpack_rest(t) for t in _adamw(parts, own_idx, w_rest, m_rest, v_rest, 64, name="adamw_rest")]
    g_in = _sum_parts(parts, own_idx, RO_IN, IN_ROWS, 16, name="sum_w_in")[:CS].T
    outs_in = _adamw([(g_in[None], 1)], own_idx, w_in[0], m_w_in[0], v_w_in[0], 128, name="adamw_w_in")

    def small_vec(a, b, c, d, e):
        z = jnp.zeros((1, LANES - B_HEADS), F32)
        return jnp.concatenate([a, b, c, d, e, z, jnp.zeros((1, LANES), F32)], axis=1)

    outs_small = _adamw([(rsmall, N_DEV)], own_idx, small_vec(norm_pre_g, norm_post_g, norm_mem_g, b_merge, b_forget),
                        small_vec(m_norm_pre_g, m_norm_post_g, m_norm_mem_g, m_b_merge, m_b_forget),
                        small_vec(v_norm_pre_g, v_norm_post_g, v_norm_mem_g, v_b_merge, v_b_forget),
                        1, name="adamw_small")

    def small_parts(t):
        return [t[:, O_GPRE:O_GPRE + D_MODEL], t[:, O_GPOST:O_GPOST + D_MODEL], t[:, O_GMEM:O_GMEM + D_MODEL],
                t[:, O_BF:O_BF + B_HEADS], t[:, O_BM:O_BM + 3 * D_MODEL]]

    loss = outs_small[0][0, O_LOSS]
    result = [loss, r["grad_x"][None]]
    for rest, w_i, small in zip(outs_rest, outs_in, outs_small):
        gp, gq, gm, bf, bm = small_parts(small)
        w_k, w_a, w_b, w_m, w_ot = rest
        result += [gp, gq, gm, w_i[None], bf, bm, w_k, w_a, w_b, w_m, w_ot]
    return tuple(result)
```

```python
import jax
import jax.numpy as jnp
from jax import lax
from jax.experimental import pallas as pl
from jax.experimental.pallas import tpu as pltpu

F32 = jnp.float32
BF16 = jnp.bfloat16

N_DEV = 8
D_MODEL = 1024
N_MEM = 256
EPS = 1e-6
NEG = -1e30
ROPE_THETA = 500000.0
DIL = (1, 4, 16)
A_HEADS = 4
HEAD = 128
A_WIDTH = 512
B_HEADS = 8
B_HEAD = 64
M_HEADS = 4
ROT = 32
IN_COLS = 11272
FB_PAD = 256

SEGS = {
    "A0": ((0, 512), (1536, 2048), (3072, 3584)),
    "A1": ((512, 1024), (2048, 2560), (3584, 4096)),
    "A2": ((1024, 1536), (2560, 3072), (4096, 4608)),
    "B": ((5120, 6656),),
    "R": ((4608, 5120), (6664, 7176), (7176, 7688), (7688, 8200), (8200, 11272), (6656, 6664)),
}
SEG_PAD = {"A0": 0, "A1": 0, "A2": 0, "B": 0, "R": FB_PAD - B_HEADS}
R_ZA, R_ZB, R_QM, R_ZM, R_GL, R_FB = 0, 512, 1024, 1536, 2048, 5120
NR = R_FB + FB_PAD

ADAM_LR, ADAM_B1, ADAM_B2, ADAM_EPS, ADAM_WD, ADAM_STEP = 0.001, 0.9, 0.999, 1e-08, 0.01, 10

LANES = 128
VMEM_LIMIT = 56 * 1024 * 1024

CS = IN_COLS // N_DEV
RO_KV, RO_OUT, RO_BR, RO_IN = 0, 128, 256, 448
IN_ROWS = 1424
ROWS = RO_IN + IN_ROWS
O_GPRE, O_GPOST, O_GMEM, O_BM, O_BF, O_LOSS = 0, 1024, 2048, 3072, 6144, 6272
P_SMALL = 6400


def _cp(sem=None):
    return pltpu.CompilerParams(dimension_semantics=sem, vmem_limit_bytes=VMEM_LIMIT)


def _dot(a, b):
    return jnp.dot(a, b, preferred_element_type=F32)


def _dot_nt(a, b):
    return lax.dot_general(a, b, (((1,), (1,)), ((), ())), preferred_element_type=F32)


def _sigmoid(z):
    return 1.0 / (1.0 + jnp.exp(-z))


def _mm(a, b, *, name, at=False, bt=False, out_dtype=F32, tm=1024, tn=1024, tk=None, comm=None):
    assert not (at and bt)
    K, M = a.shape if at else a.shape[::-1]
    N = b.shape[0] if bt else b.shape[1]
    tm, tn = min(tm, M), min(tn, N)
    tk = K if tk is None else min(tk, K)
    assert M % tm == 0 and N % tn == 0 and K % tk == 0
    nk = K // tk
    grid = (M // tm, N // tn, nk)
    n_in = len(comm["inputs"]) if comm else 0
    n_out = len(comm["out_shape"]) if comm else 0

    def body(a_ref, b_ref, *rest):
        c_in, o_ref, c_out = rest[:n_in], rest[n_in], rest[n_in + 1:n_in + 1 + n_out]
        acc_ref, sems = rest[n_in + 1 + n_out], rest[n_in + 2 + n_out:]
        if comm:
            step = (pl.program_id(0) * grid[1] + pl.program_id(1)) * grid[2] + pl.program_id(2)

            @pl.when(step == 0)
            def _():
                comm["start"](*c_in, *c_out, *sems)

        av = a_ref[...].astype(BF16)
        bv = b_ref[...].astype(BF16)
        if at:
            p = lax.dot_general(av, bv, (((0,), (0,)), ((), ())), preferred_element_type=F32)
        else:
            p = _dot_nt(av, bv) if bt else _dot(av, bv)
        if nk == 1:
            o_ref[...] = p.astype(out_dtype)
        else:
            k = pl.program_id(2)

            @pl.when(k == 0)
            def _():
                acc_ref[...] = p

            @pl.when(k > 0)
            def _():
                acc_ref[...] += p

            @pl.when(k == nk - 1)
            def _():
                o_ref[...] = acc_ref[...].astype(out_dtype)

        if comm:
            @pl.when(step == grid[0] * grid[1] * grid[2] - 1)
            def _():
                comm["wait"](*c_in, *c_out, *sems)

    b_spec = (pl.BlockSpec((tn, tk), lambda i, j, k: (j, k)) if bt
              else pl.BlockSpec((tk, tn), lambda i, j, k: (k, j)))
    a_spec = (pl.BlockSpec((tk, tm), lambda i, j, k: (k, i)) if at
              else pl.BlockSpec((tm, tk), lambda i, j, k: (i, k)))
    out_spec = pl.BlockSpec((tm, tn), lambda i, j, k: (i, j))
    out_shape = jax.ShapeDtypeStruct((M, N), out_dtype)
    acc = pltpu.VMEM((tm, tn) if nk > 1 else (8, LANES), F32)
    if not comm:
        return pl.pallas_call(
            body, name=name, grid=grid, in_specs=[a_spec, b_spec], out_specs=out_spec, out_shape=out_shape,
            scratch_shapes=[acc], compiler_params=_cp(("parallel", "parallel", "arbitrary")))(a, b)
    return pl.pallas_call(
        body, name=name, grid=grid, in_specs=[a_spec, b_spec] + [ANY] * n_in,
        out_specs=[out_spec] + [ANY] * n_out, out_shape=[out_shape] + comm["out_shape"],
        scratch_shapes=[acc] + comm["sems"],
        compiler_params=_cp(("arbitrary", "arbitrary", "arbitrary")))(a, b, *comm["inputs"])


def _mm_sum(pairs, *, name, tm=1024, tk=768, comm=None):
    M, N = pairs[0][0].shape[0], pairs[0][1].shape[1]
    tm = min(tm, M)
    steps = [a.shape[1] // tk for a, _ in pairs]
    assert M % tm == 0 and all(a.shape[1] % tk == 0 for a, _ in pairs)
    first = [sum(steps[:p]) for p in range(len(pairs))]
    total = sum(steps)
    grid = (M // tm, total)
    n_in = len(comm["inputs"]) if comm else 0
    n_out = len(comm["out_shape"]) if comm else 0
    npair = len(pairs)

    def body(*refs):
        ab, rest = refs[:2 * npair], refs[2 * npair:]
        c_in, o_ref, c_out = rest[:n_in], rest[n_in], rest[n_in + 1:n_in + 1 + n_out]
        acc_ref, sems = rest[n_in + 1 + n_out], rest[n_in + 2 + n_out:]
        k = pl.program_id(1)
        if comm:
            step = pl.program_id(0) * total + k

            @pl.when(step == 0)
            def _():
                comm["start"](*c_in, *c_out, *sems)

        @pl.when(k == 0)
        def _():
            acc_ref[...] = jnp.zeros((tm, N), F32)

        for p in range(npair):
            @pl.when(jnp.logical_and(k >= first[p], k < first[p] + steps[p]))
            def _(p=p):
                acc_ref[...] += _dot(ab[2 * p][...], ab[2 * p + 1][...])

        @pl.when(k == total - 1)
        def _():
            o_ref[...] = acc_ref[...]

        if comm:
            @pl.when(step == grid[0] * total - 1)
            def _():
                comm["wait"](*c_in, *c_out, *sems)

    def local(p):
        return lambda k: jnp.clip(k - first[p], 0, steps[p] - 1)

    in_specs = []
    for p in range(npair):
        in_specs += [pl.BlockSpec((tm, tk), lambda i, k, f=local(p): (i, f(k))),
                     pl.BlockSpec((tk, N), lambda i, k, f=local(p): (f(k), 0))]
    out_spec = pl.BlockSpec((tm, N), lambda i, k: (i, 0))
    out_shape = jax.ShapeDtypeStruct((M, N), F32)
    args = [t for pair in pairs for t in pair]
    if not comm:
        return pl.pallas_call(
            body, name=name, grid=grid, in_specs=in_specs, out_specs=out_spec, out_shape=out_shape,
            scratch_shapes=[pltpu.VMEM((tm, N), F32)], compiler_params=_cp(("parallel", "arbitrary")))(*args)
    return pl.pallas_call(
        body, name=name, grid=grid, in_specs=in_specs + [ANY] * n_in,
        out_specs=[out_spec] + [ANY] * n_out, out_shape=[out_shape] + comm["out_shape"],
        scratch_shapes=[pltpu.VMEM((tm, N), F32)] + comm["sems"],
        compiler_params=_cp(("arbitrary", "arbitrary")))(*args, *comm["inputs"])


def _class_spec(S, d, tm, width):
    return pl.BlockSpec((d, tm // d, width), lambda i: (0, i, 0))


def _rms_fwd(x, g, *, name, dilations=(), comm=None):
    S, D = x.shape
    tm = min(512, S)
    ds = [d for d in dilations if d > 1]
    nsteps = S // tm
    n_in = len(comm["inputs"]) if comm else 0
    n_out = len(comm["out_shape"]) if comm else 0
    n_tmp = D // LANES if ds else 0

    def body(x_ref, g_ref, *rest):
        c_in, o_ref, rest = rest[:n_in], rest[n_in], rest[n_in + 1:]
        cls, c_out, rest = rest[:len(ds)], rest[len(ds):len(ds) + n_out], rest[len(ds) + n_out:]
        tmps, sems = rest[:n_tmp], rest[n_tmp:]
        if comm:
            @pl.when(pl.program_id(0) == 0)
            def _():
                comm["start"](*c_in, *c_out, *sems)

        xv = x_ref[...]
        r = lax.rsqrt(jnp.mean(xv * xv, axis=-1, keepdims=True) + EPS)
        hv = xv * r * g_ref[...]
        o_ref[...] = hv.astype(BF16)
        if ds:
            for c, tmp in enumerate(tmps):
                tmp[...] = hv[:, c * LANES:(c + 1) * LANES]
            for c_ref, d in zip(cls, ds):
                for k in range(d):
                    c_ref[k] = jnp.concatenate([tmp[pl.ds(k, tm // d, stride=d), :] for tmp in tmps],
                                               axis=1).astype(BF16)
        if comm:
            @pl.when(pl.program_id(0) == nsteps - 1)
            def _():
                comm["wait"](*c_in, *c_out, *sems)

    row = pl.BlockSpec((tm, D), lambda i: (i, 0))
    outs = pl.pallas_call(
        body, name=name, grid=(nsteps,),
        in_specs=[row, pl.BlockSpec((1, D), lambda i: (0, 0))] + [ANY] * n_in,
        out_specs=[row] + [_class_spec(S, d, tm, D) for d in ds] + [ANY] * n_out,
        out_shape=[jax.ShapeDtypeStruct((S, D), BF16)] + [jax.ShapeDtypeStruct((d, S // d, D), BF16) for d in ds]
        + (comm["out_shape"] if comm else []),
        scratch_shapes=[pltpu.VMEM((tm, LANES), F32)] * n_tmp + (comm["sems"] if comm else []),
        compiler_params=_cp(("arbitrary",) if comm else ("parallel",)),
    )(x, g, *(comm["inputs"] if comm else []))
    rows = [outs[0]] + [o.reshape(S, D) for o in outs[1:1 + len(ds)]]
    if comm:
        return rows, list(outs[1 + len(ds):])
    return rows if ds else rows[0]


def _rms_bwd(x, g, dh, dy, *, name, dh_classes=()):
    S, D = x.shape
    tm = min(512, S)
    want_dx = dy is not None
    nc = len(dh_classes)

    def body(*refs):
        c_refs, refs = refs[:nc], refs[nc:]
        if want_dx:
            x_ref, g_ref, dh_ref, dy_ref, dx_ref, dg_ref = refs[:6]
        else:
            x_ref, g_ref, dh_ref, dg_ref = refs[:4]
        i = pl.program_id(0)
        xv = x_ref[...]
        r = lax.rsqrt(jnp.mean(xv * xv, axis=-1, keepdims=True) + EPS)
        xh = xv * r
        if nc:
            tmps = refs[-(D // LANES):]
            cols = [slice(c * LANES, (c + 1) * LANES) for c in range(D // LANES)]
            for tmp, cs in zip(tmps, cols):
                tmp[...] = dh_ref[:, cs]
            for c_ref, (_, d) in zip(c_refs, dh_classes):
                for k in range(d):
                    for tmp, cs in zip(tmps, cols):
                        tmp[pl.ds(k, tm // d, stride=d), :] += c_ref[k, :, cs]
            dhv = jnp.concatenate([tmp[...] for tmp in tmps], axis=1)
        else:
            dhv = dh_ref[...]
        part = jnp.sum(dhv * xh, axis=0, keepdims=True)

        @pl.when(i == 0)
        def _():
            dg_ref[...] = part

        @pl.when(i > 0)
        def _():
            dg_ref[...] += part

        if want_dx:
            dxh = dhv * g_ref[...]
            dx_ref[...] = dy_ref[...] + r * (dxh - xh * jnp.mean(dxh * xh, axis=-1, keepdims=True))

    row = pl.BlockSpec((tm, D), lambda i: (i, 0))
    vec = pl.BlockSpec((1, D), lambda i: (0, 0))
    c_specs = [_class_spec(S, d, tm, D) for _, d in dh_classes]
    c_args = [a.reshape(d, S // d, D) for a, d in dh_classes]
    scratch = [pltpu.VMEM((tm, LANES), F32)] * (D // LANES) if nc else []
    if want_dx:
        return pl.pallas_call(
            body, name=name, grid=(S // tm,), in_specs=c_specs + [row, vec, row, row], out_specs=[row, vec],
            out_shape=[jax.ShapeDtypeStruct((S, D), F32), jax.ShapeDtypeStruct((1, D), F32)],
            scratch_shapes=scratch, compiler_params=_cp(("arbitrary",)))(*c_args, x, g, dh, dy)
    return pl.pallas_call(
        body, name=name, grid=(S // tm,), in_specs=c_specs + [row, vec, row], out_specs=vec,
        out_shape=jax.ShapeDtypeStruct((1, D), F32),
        scratch_shapes=scratch, compiler_params=_cp(("arbitrary",)))(*c_args, x, g, dh)


def _post(x, out, tgt, g, *, name):
    S, D = x.shape
    tm = min(512, S)

    def body(x_ref, o_ref, t_ref, g_ref, dy_ref, do_ref, dg_ref, loss_ref):
        i = pl.program_id(0)
        ov = o_ref[...]
        r = lax.rsqrt(jnp.mean(ov * ov, axis=-1, keepdims=True) + EPS)
        n = ov * r
        gv = g_ref[...]
        e = (x_ref[...] + n * gv) - t_ref[...]
        lpart = 0.5 * jnp.sum(jnp.mean(e * e, axis=-1, keepdims=True), axis=0, keepdims=True)
        dy = e * (1.0 / D)
        dy_ref[...] = dy
        dn = dy * gv
        do_ref[...] = (r * (dn - n * jnp.mean(dn * n, axis=-1, keepdims=True))).astype(BF16)
        gpart = jnp.sum(dy * n, axis=0, keepdims=True)
        lrow = jnp.broadcast_to(lpart, (1, LANES))

        @pl.when(i == 0)
        def _():
            dg_ref[...] = gpart
            loss_ref[...] = lrow

        @pl.when(i > 0)
        def _():
            dg_ref[...] += gpart
            loss_ref[...] += lrow

    row = pl.BlockSpec((tm, D), lambda i: (i, 0))
    vec = pl.BlockSpec((1, D), lambda i: (0, 0))
    return pl.pallas_call(
        body, name=name, grid=(S // tm,), in_specs=[row, row, row, vec],
        out_specs=[row, row, vec, pl.BlockSpec((1, LANES), lambda i: (0, 0))],
        out_shape=[jax.ShapeDtypeStruct((S, D), F32), jax.ShapeDtypeStruct((S, D), BF16),
                   jax.ShapeDtypeStruct((1, D), F32), jax.ShapeDtypeStruct((1, LANES), F32)],
        compiler_params=_cp(("arbitrary",)))(x, out, tgt, g)


def _to_classes(t, d):
    if d == 1:
        return t
    S, C = t.shape
    return t.reshape(S // d, d, C).transpose(1, 0, 2).reshape(S, C)


def _rope(x, c, s1, s2):
    return x * c + pltpu.roll(x, LANES - ROT // 2, 1) * s1 + pltpu.roll(x, ROT // 2, 1) * s2


def _unrope(d, c, s1, s2):
    return d * c + pltpu.roll(d * s1, ROT // 2, 1) + pltpu.roll(d * s2, LANES - ROT // 2, 1)


def _a_band(qb):
    r = lax.broadcasted_iota(jnp.int32, (qb, qb + HEAD), 0)
    c = lax.broadcasted_iota(jnp.int32, (qb, qb + HEAD), 1)
    return jnp.logical_and(c >= r, c <= r + HEAD)


def _a_first_ok(qb, n):
    c = lax.broadcasted_iota(jnp.int32, (qb, qb + HEAD), 1)
    return jnp.logical_or(c >= HEAD, n > 0)


def _a_last_ok(qb, has_next):
    c = lax.broadcasted_iota(jnp.int32, (qb, qb + HEAD), 1)
    return jnp.logical_or(c < qb, has_next)


A_SCALE = HEAD ** -0.5


def _a_geometry(S, g):
    d = DIL[g]
    L = S // d
    TQ = min(512, L)
    return d, L, TQ, TQ // HEAD, L // TQ, L // HEAD


def _proj_rope(h, w, tabs, *, name):
    S, D = h.shape
    tm = min(512, S)

    def body(h_ref, w_ref, c_ref, s1_ref, s2_ref, o_ref):
        tc = (c_ref[...], s1_ref[...], s2_ref[...])
        u = _dot_nt(h_ref[...], w_ref[...])
        for j in range(3 * A_HEADS):
            sl = slice(j * HEAD, (j + 1) * HEAD)
            o_ref[:, sl] = (_rope(u[:, sl], *tc) if j < 2 * A_HEADS else u[:, sl]).astype(BF16)

    tab = pl.BlockSpec((tm, LANES), lambda i: (i, 0))
    return pl.pallas_call(
        body, name=name, grid=(S // tm,),
        in_specs=[pl.BlockSpec((tm, D), lambda i: (i, 0)), pl.BlockSpec((3 * A_WIDTH, D), lambda i: (0, 0)),
                  tab, tab, tab],
        out_specs=pl.BlockSpec((tm, 3 * A_WIDTH), lambda i: (i, 0)),
        out_shape=jax.ShapeDtypeStruct((S, 3 * A_WIDTH), BF16),
        compiler_params=_cp(("parallel",)))(h, w, *tabs)


def _attn_a_fwd(qkv, g, *, name):
    S = qkv.shape[0]
    d, L, TQ, nsub, nb, nblk = _a_geometry(S, g)

    def body(q_ref, kc_ref, kp_ref, vc_ref, vp_ref, o_ref, l_ref):
        n = pl.program_id(1)
        QB = min(2 * HEAD, TQ)
        band = _a_band(QB)
        first = jnp.logical_and(band, _a_first_ok(QB, n))
        for h in range(A_HEADS):
            hs = slice(h * HEAD, (h + 1) * HEAD)
            for hh in range(TQ // QB):
                sl = slice(hh * QB, (hh + 1) * QB)
                pv = slice(hh * QB - HEAD, hh * QB)
                kcat = jnp.concatenate([kp_ref[:, hs] if hh == 0 else kc_ref[pv, hs], kc_ref[sl, hs]], axis=0)
                vcat = jnp.concatenate([vp_ref[:, hs] if hh == 0 else vc_ref[pv, hs], vc_ref[sl, hs]], axis=0)
                s = jnp.where(first if hh == 0 else band, _dot_nt(q_ref[sl, hs], kcat) * A_SCALE, NEG)
                m = jnp.max(s, axis=-1, keepdims=True)
                p = jnp.exp(s - m)
                den = jnp.sum(p, axis=-1, keepdims=True)
                o_ref[sl, hs] = _dot(p.astype(BF16), vcat) / den
                l_ref[sl, hs] = jnp.broadcast_to(m + jnp.log(den), (QB, HEAD))

    rcur = lambda r, n: r * nb + n
    rprv = lambda r, n: r * nblk + jnp.maximum(n * nsub - 1, 0)
    cur = lambda off: pl.BlockSpec((TQ, A_WIDTH), lambda r, n: (rcur(r, n), off))
    prv = lambda off: pl.BlockSpec((HEAD, A_WIDTH), lambda r, n: (rprv(r, n), off))
    out = pl.BlockSpec((TQ, A_WIDTH), lambda r, n: (rcur(r, n), 0))
    return pl.pallas_call(
        body, name=name, grid=(d, nb),
        in_specs=[cur(0), cur(1), prv(1), cur(2), prv(2)],
        out_specs=[out, out],
        out_shape=[jax.ShapeDtypeStruct((S, A_WIDTH), F32)] * 2,
        compiler_params=_cp(("parallel", "parallel")),
    )(qkv, qkv, qkv, qkv, qkv)


def _attn_a_dq(qkv, tabs, g, do, lse, adj, *, name):
    S = qkv.shape[0]
    d, L, TQ, nsub, nb, nblk = _a_geometry(S, g)

    def body(q_ref, kc_ref, kp_ref, vc_ref, vp_ref, do_ref, l_ref, adj_ref, c_ref, s1_ref, s2_ref, dq_ref):
        n = pl.program_id(1)
        QB = min(2 * HEAD, TQ)
        band = _a_band(QB)
        first = jnp.logical_and(band, _a_first_ok(QB, n))
        for h in range(A_HEADS):
            hs = slice(h * HEAD, (h + 1) * HEAD)
            for hh in range(TQ // QB):
                sl = slice(hh * QB, (hh + 1) * QB)
                pv = slice(hh * QB - HEAD, hh * QB)
                kcat = jnp.concatenate([kp_ref[:, hs] if hh == 0 else kc_ref[pv, hs], kc_ref[sl, hs]], axis=0)
                vcat = jnp.concatenate([vp_ref[:, hs] if hh == 0 else vc_ref[pv, hs], vc_ref[sl, hs]], axis=0)
                s = jnp.where(first if hh == 0 else band, _dot_nt(q_ref[sl, hs], kcat) * A_SCALE, NEG)
                p = jnp.exp(s - l_ref[sl, hs][:, :1])
                ds = p * (_dot_nt(do_ref[sl, hs], vcat) + adj_ref[sl, hs][:, :1])
                dq = _dot(ds.astype(BF16), kcat) * A_SCALE
                dq_ref[sl, hs] = _unrope(dq, c_ref[sl, :], s1_ref[sl, :], s2_ref[sl, :]).astype(BF16)

    rcur = lambda r, n: r * nb + n
    rprv = lambda r, n: r * nblk + jnp.maximum(n * nsub - 1, 0)
    cur = lambda off: pl.BlockSpec((TQ, A_WIDTH), lambda r, n: (rcur(r, n), off))
    prv = lambda off: pl.BlockSpec((HEAD, A_WIDTH), lambda r, n: (rprv(r, n), off))
    tcur = pl.BlockSpec((TQ, LANES), lambda r, n: (rcur(r, n), 0))
    blk = cur(0)
    return pl.pallas_call(
        body, name=name, grid=(d, nb),
        in_specs=[cur(0), cur(1), prv(1), cur(2), prv(2), blk, blk, blk, tcur, tcur, tcur],
        out_specs=blk,
        out_shape=jax.ShapeDtypeStruct((S, A_WIDTH), BF16),
        compiler_params=_cp(("parallel", "parallel")),
    )(qkv, qkv, qkv, qkv, qkv, do, lse, adj, *tabs)


def _attn_a_dkv(qkv, tabs, g, do, lse, adj, *, name):
    S = qkv.shape[0]
    d, L, TQ, nsub, nb, nblk = _a_geometry(S, g)

    def body(qc_ref, qn_ref, kc_ref, vc_ref, doc_ref, don_ref, lc_ref, ln_ref, ac_ref, an_ref,
             c_ref, s1_ref, s2_ref, dk_ref, dv_ref):
        n = pl.program_id(1)
        QB = min(2 * HEAD, TQ)
        nh = TQ // QB
        band = _a_band(QB)
        end = jnp.logical_and(band, _a_last_ok(QB, n < nb - 1))
        for h in range(A_HEADS):
            hs = slice(h * HEAD, (h + 1) * HEAD)
            for kh in range(nh):
                sl = slice(kh * QB, (kh + 1) * QB)
                nx = slice((kh + 1) * QB, (kh + 1) * QB + HEAD)
                last = kh == nh - 1
                cat = lambda cur, nxt: jnp.concatenate([cur[sl, hs], nxt[:, hs] if last else cur[nx, hs]], axis=0)
                qcat = cat(qc_ref, qn_ref)
                docat = cat(doc_ref, don_ref)
                lt = cat(lc_ref, ln_ref).T[:1, :]
                at = cat(ac_ref, an_ref).T[:1, :]
                st = jnp.where(end if last else band, _dot_nt(kc_ref[sl, hs], qcat) * A_SCALE, NEG)
                pt = jnp.exp(st - lt)
                dv_ref[sl, hs] = _dot(pt.astype(BF16), docat).astype(BF16)
                dst = pt * (_dot_nt(vc_ref[sl, hs], docat) + at)
                dk = _dot(dst.astype(BF16), qcat) * A_SCALE
                dk_ref[sl, hs] = _unrope(dk, c_ref[sl, :], s1_ref[sl, :], s2_ref[sl, :]).astype(BF16)

    rcur = lambda r, n: r * nb + n
    rnxt = lambda r, n: r * nblk + jnp.minimum((n + 1) * nsub, nblk - 1)
    cur = lambda off: pl.BlockSpec((TQ, A_WIDTH), lambda r, n: (rcur(r, n), off))
    nxu = lambda off: pl.BlockSpec((HEAD, A_WIDTH), lambda r, n: (rnxt(r, n), off))
    tcur = pl.BlockSpec((TQ, LANES), lambda r, n: (rcur(r, n), 0))
    blk, bnx = cur(0), nxu(0)
    return pl.pallas_call(
        body, name=name, grid=(d, nb),
        in_specs=[cur(0), nxu(0), cur(1), cur(2), blk, bnx, blk, bnx, blk, bnx, tcur, tcur, tcur],
        out_specs=[blk, blk],
        out_shape=[jax.ShapeDtypeStruct((S, A_WIDTH), BF16)] * 2,
        compiler_params=_cp(("parallel", "parallel")),
    )(qkv, qkv, qkv, qkv, do, do, lse, lse, adj, adj, *tabs)


def _silu_parts(z):
    sg = _sigmoid(z)
    return z * sg, sg * (1.0 + z * (1.0 - sg))


def _classes_to_tokens(c_ref, d, tm, tmps):
    if d == 1:
        return c_ref[...].astype(F32)
    for k in range(d):
        for c, tmp in enumerate(tmps):
            tmp[pl.ds(k, tm // d, stride=d), :] = c_ref[k, :, c * LANES:(c + 1) * LANES].astype(F32)
    return jnp.concatenate([tmp[...] for tmp in tmps], axis=1)


def _tokens_to_classes(val, c_ref, d, tm, tmps):
    if d == 1:
        c_ref[...] = val.astype(c_ref.dtype)
        return
    for c, tmp in enumerate(tmps):
        tmp[...] = val[:, c * LANES:(c + 1) * LANES]
    for k in range(d):
        c_ref[k] = jnp.concatenate([tmp[pl.ds(k, tm // d, stride=d), :] for tmp in tmps], axis=1).astype(c_ref.dtype)


def _group_spec(S, d, tm):
    if d == 1:
        return pl.BlockSpec((tm, A_WIDTH), lambda i: (i, 0))
    return _class_spec(S, d, tm, A_WIDTH)


def _group_view(t, d):
    return t if d == 1 else t.reshape(d, t.shape[0] // d, t.shape[1])


def _merge_a_fwd(os_, ls_, ur, *, name):
    S = ur.shape[0]
    tm = min(512, S)

    def body(o0, o1, o2, l0, l1, l2, z_ref, y_ref, *tmps):
        ls = [_classes_to_tokens(r, d, tm, tmps) for r, d in zip((l0, l1, l2), DIL)]
        ov = [_classes_to_tokens(r, d, tm, tmps) for r, d in zip((o0, o1, o2), DIL)]
        mx = jnp.maximum(jnp.maximum(ls[0], ls[1]), ls[2])
        es = [jnp.exp(l - mx) for l in ls]
        den = es[0] + es[1] + es[2]
        y = (es[0] / den) * ov[0] + (es[1] / den) * ov[1] + (es[2] / den) * ov[2]
        y_ref[...] = (y * _silu_parts(z_ref[...])[0]).astype(BF16)

    blk = pl.BlockSpec((tm, A_WIDTH), lambda i: (i, 0))
    groups = [_group_spec(S, d, tm) for d in DIL]
    return pl.pallas_call(
        body, name=name, grid=(S // tm,),
        in_specs=groups + groups + [pl.BlockSpec((tm, A_WIDTH), lambda i: (i, R_ZA // A_WIDTH))],
        out_specs=blk, out_shape=jax.ShapeDtypeStruct((S, A_WIDTH), BF16),
        scratch_shapes=[pltpu.VMEM((tm, LANES), F32)] * (A_WIDTH // LANES),
        compiler_params=_cp(("parallel",)))(*[_group_view(t, d) for t, d in zip(os_, DIL)],
                                            *[_group_view(t, d) for t, d in zip(ls_, DIL)], ur)


def _merge_a_bwd(os_, ls_, ur, dya, *, name):
    S = ur.shape[0]
    tm = min(256, S)

    def body(o0, o1, o2, l0, l1, l2, z_ref, dy_ref, d0, d1, d2, a0, a1, a2, dz_ref, *tmps):
        ls = [_classes_to_tokens(r, d, tm, tmps) for r, d in zip((l0, l1, l2), DIL)]
        ov = [_classes_to_tokens(r, d, tm, tmps) for r, d in zip((o0, o1, o2), DIL)]
        mx = jnp.maximum(jnp.maximum(ls[0], ls[1]), ls[2])
        es = [jnp.exp(l - mx) for l in ls]
        den = es[0] + es[1] + es[2]
        ws = [e / den for e in es]
        y = ws[0] * ov[0] + ws[1] * ov[1] + ws[2] * ov[2]
        sz, dsz = _silu_parts(z_ref[...])
        dyv = dy_ref[...]
        dz_ref[...] = (dyv * y * dsz).astype(BF16)
        dyp = dyv * sz
        ts = []
        for h in range(A_HEADS):
            sl = slice(h * HEAD, (h + 1) * HEAD)
            t = jnp.zeros((tm, 1), F32)
            for gi in range(3):
                t = t + ws[gi][:, sl][:, :1] * jnp.sum(dyp[:, sl] * ov[gi][:, sl], axis=-1, keepdims=True)
            ts.append(jnp.broadcast_to(t, (tm, HEAD)))
        tb = jnp.concatenate(ts, axis=1)
        for gi, (dref, aref) in enumerate(((d0, a0), (d1, a1), (d2, a2))):
            _tokens_to_classes(ws[gi] * dyp, dref, DIL[gi], tm, tmps)
            _tokens_to_classes(-ws[gi] * tb, aref, DIL[gi], tm, tmps)

    blk = pl.BlockSpec((tm, A_WIDTH), lambda i: (i, 0))
    groups = [_group_spec(S, d, tm) for d in DIL]
    shaped = lambda dt: [jax.ShapeDtypeStruct((S, A_WIDTH) if d == 1 else (d, S // d, A_WIDTH), dt) for d in DIL]
    outs = pl.pallas_call(
        body, name=name, grid=(S // tm,),
        in_specs=groups + groups + [pl.BlockSpec((tm, A_WIDTH), lambda i: (i, R_ZA // A_WIDTH)), blk],
        out_specs=groups + groups + [blk],
        out_shape=shaped(BF16) + shaped(F32) + [jax.ShapeDtypeStruct((S, A_WIDTH), BF16)],
        scratch_shapes=[pltpu.VMEM((tm, LANES), F32)] * (A_WIDTH // LANES),
        compiler_params=_cp(("parallel",)))(*[_group_view(t, d) for t, d in zip(os_, DIL)],
                                            *[_group_view(t, d) for t, d in zip(ls_, DIL)], ur, dya)
    flat = [t.reshape(S, A_WIDTH) for t in outs[:6]]
    return flat[0:3], flat[3:6], outs[6]


def _logf(ur, bf_pad, *, name):
    S = ur.shape[0]
    tm = min(1024, S)

    def body(u_ref, b_ref, o_ref):
        z = u_ref[...] + b_ref[...]
        o_ref[...] = jnp.minimum(z, 0.0) - jnp.log(1.0 + jnp.exp(-jnp.abs(z)))

    return pl.pallas_call(
        body, name=name, grid=(S // tm,),
        in_specs=[pl.BlockSpec((tm, FB_PAD), lambda i: (i, R_FB // FB_PAD)),
                  pl.BlockSpec((1, FB_PAD), lambda i: (0, 0))],
        out_specs=pl.BlockSpec((tm, FB_PAD), lambda i: (i, 0)),
        out_shape=jax.ShapeDtypeStruct((S, FB_PAD), F32),
        compiler_params=_cp(("parallel",)))(ur, bf_pad)


def _cumsum_lanes(x, reverse, *, name):
    nt, H, _ = x.shape
    R = nt * H

    def body(x_ref, o_ref):
        v = x_ref[...].reshape(R, LANES)
        lane = lax.broadcasted_iota(jnp.int32, (R, LANES), 1)
        row = lax.broadcasted_iota(jnp.int32, (R, LANES), 0)

        def scan(t, step, idx, n, axis):
            while step < n:
                if reverse:
                    t = t + jnp.where(idx < n - step, pltpu.roll(t, n - step, axis), 0.0)
                else:
                    t = t + jnp.where(idx >= step, pltpu.roll(t, step, axis), 0.0)
                step *= 2
            return t

        v = scan(v, 1, lane, LANES, 1)
        total = jnp.broadcast_to(v[:, :1] if reverse else v[:, LANES - 1:], (R, LANES))
        carry = scan(total, H, row, R, 0) - total
        o_ref[...] = (v + carry).reshape(nt, H, LANES)

    return pl.pallas_call(
        body, name=name, out_shape=jax.ShapeDtypeStruct((nt, H, LANES), F32),
        in_specs=[pl.BlockSpec(memory_space=pltpu.VMEM)], out_specs=pl.BlockSpec(memory_space=pltpu.VMEM),
        compiler_params=_cp())(x)


B_SCALE = B_HEAD ** -0.5


def _pair_masks():
    lane = lax.broadcasted_iota(jnp.int32, (1, LANES), 1)
    row = lax.broadcasted_iota(jnp.int32, (LANES, 1), 0)
    return (lane < B_HEAD, lane >= B_HEAD), (row < B_HEAD, row >= B_HEAD)


def _causal_t(T):
    r = lax.broadcasted_iota(jnp.int32, (T, T), 0)
    c = lax.broadcasted_iota(jnp.int32, (T, T), 1)
    return r <= c


def _zero_other(x, keep):
    return jnp.where(keep, x, jnp.zeros_like(x))


def _fox_aug(ub, ckb, *, name):
    S = ub.shape[0]
    T = min(2048, S)

    def body(q_ref, k_ref, c_ref, qa_ref, ka_ref):
        lane = lax.broadcasted_iota(jnp.int32, (1, LANES), 1)
        q = q_ref[...] * B_SCALE
        k = k_ref[...]
        for a in range(2):
            own = (lane < B_HEAD) if a == 0 else (lane >= B_HEAD)
            o = B_HEAD if a == 0 else 0
            c = c_ref[a]
            hi = c.astype(BF16)
            r1 = c - hi.astype(F32)
            mid = r1.astype(BF16)
            lo = (r1 - mid.astype(F32)).astype(BF16)
            pieces = (hi, mid, lo)
            one = jnp.ones((T, LANES), BF16)
            qa = jnp.where(own, q, jnp.zeros_like(q))
            ka = jnp.where(own, k, jnp.zeros_like(k))
            for t in range(3):
                qa = jnp.where(lane == o + t, pieces[t], qa)
                qa = jnp.where(lane == o + 3 + t, one, qa)
                ka = jnp.where(lane == o + t, one, ka)
                ka = jnp.where(lane == o + 3 + t, -pieces[t], ka)
            qa_ref[a] = qa
            ka_ref[a] = ka

    out = pl.BlockSpec((2, T, LANES), lambda h, i: (h, i, 0))
    return pl.pallas_call(
        body, name=name, grid=(B_HEADS // 2, S // T),
        in_specs=[pl.BlockSpec((T, LANES), lambda h, i: (i, h)), pl.BlockSpec((T, LANES), lambda h, i: (i, 4 + h)), out],
        out_specs=[out, out], out_shape=[jax.ShapeDtypeStruct((B_HEADS, S, LANES), BF16)] * 2,
        compiler_params=_cp(("parallel", "parallel")))(ub, ub, ckb)


def _fox_fwd(qaug, kaug, vt, *, name):
    S = qaug.shape[1]
    T = min(512, S)
    nq = S // T

    def body(q_ref, k_ref, vt_ref, o_ref, l_ref, m_s, l_s, acc_s, st_s):
        i = pl.program_id(1)
        _, rows = _pair_masks()
        qm = [q_ref[0], q_ref[1]]
        m_s[...] = jnp.full((2, 1, T), NEG, F32)
        l_s[...] = jnp.zeros((2, 1, T), F32)
        acc_s[...] = jnp.zeros((LANES, T), F32)

        def logits(j):
            off = pl.multiple_of(j * T, T)
            return [_dot_nt(k_ref[a, pl.ds(off, T), :], qm[a]) for a in range(2)]

        def step(j, masked, prefetch):
            nxt = logits(j + 1) if prefetch else None
            vtj = vt_ref[j]
            upd = jnp.zeros((LANES, T), F32)
            alphas = []
            for a in range(2):
                st = st_s[a]
                if masked:
                    st = jnp.where(_causal_t(T), st, NEG)
                m_old = m_s[a]
                m_new = jnp.maximum(m_old, jnp.max(st, axis=0, keepdims=True))
                alpha = jnp.exp(m_old - m_new)
                pt = jnp.exp(st - m_new)
                l_s[a] = alpha * l_s[a] + jnp.sum(pt, axis=0, keepdims=True)
                m_s[a] = m_new
                upd = upd + _dot(_zero_other(vtj, rows[a]), pt.astype(BF16))
                alphas.append(alpha)
            acc_s[...] = acc_s[...] * jnp.where(rows[0], alphas[0], alphas[1]) + upd
            if prefetch:
                st_s[0] = nxt[0]
                st_s[1] = nxt[1]

        def loop(j, carry):
            step(j, False, True)
            return carry

        first = logits(0)
        st_s[0] = first[0]
        st_s[1] = first[1]
        lax.fori_loop(0, i, loop, 0)
        step(i, True, False)
        o_ref[...] = (acc_s[...] / jnp.where(rows[0], l_s[0], l_s[1])).T
        l_ref[0] = m_s[0] + jnp.log(l_s[0])
        l_ref[1] = m_s[1] + jnp.log(l_s[1])

    stat = pl.BlockSpec((2, None, 1, T), lambda h, i: (h, i, 0, 0))
    return pl.pallas_call(
        body, name=name, grid=(B_HEADS // 2, nq),
        in_specs=[pl.BlockSpec((2, T, LANES), lambda h, i: (h, i, 0)),
                  pl.BlockSpec((2, S, LANES), lambda h, i: (h, 0, 0)),
                  pl.BlockSpec((nq, LANES, T), lambda h, i: (0, h, 0))],
        out_specs=[pl.BlockSpec((T, LANES), lambda h, i: (i, h)), stat],
        out_shape=[jax.ShapeDtypeStruct((S, A_WIDTH), F32), jax.ShapeDtypeStruct((B_HEADS, nq, 1, T), F32)],
        scratch_shapes=[pltpu.VMEM((2, 1, T), F32), pltpu.VMEM((2, 1, T), F32), pltpu.VMEM((LANES, T), F32),
                        pltpu.VMEM((2, T, T), F32)],
        compiler_params=_cp(("parallel", "parallel")),
    )(qaug, kaug, vt)


def _fox_delta(o, do, *, name):
    S = o.shape[0]
    T = min(512, S)
    nq = S // T

    per = min(4, nq)

    def body(o_ref, do_ref, d_ref):
        _, rows = _pair_masks()
        for t in range(per):
            sl = slice(t * T, (t + 1) * T)
            prod_t = (do_ref[sl, :].astype(F32) * o_ref[sl, :]).T
            d_ref[0, t] = jnp.sum(_zero_other(prod_t, rows[0]), axis=0, keepdims=True)
            d_ref[1, t] = jnp.sum(_zero_other(prod_t, rows[1]), axis=0, keepdims=True)

    tile = pl.BlockSpec((per * T, LANES), lambda h, i: (i, h))
    return pl.pallas_call(
        body, name=name, grid=(B_HEADS // 2, nq // per), in_specs=[tile, tile],
        out_specs=pl.BlockSpec((2, per, 1, T), lambda h, i: (h, i, 0, 0)),
        out_shape=jax.ShapeDtypeStruct((B_HEADS, nq, 1, T), F32),
        compiler_params=_cp(("parallel", "parallel")))(o, do)


def _fox_bwd(ub, qaug, kaug, kt, do, lse, delta, *, name):
    S = ub.shape[0]
    T = min(512, S)
    nq = S // T

    def body(k_ref, v_ref, kt_ref, q_ref, do_ref, l_ref, dl_ref,
             dk_ref, dv_ref, dck_ref, dqt_ref, dcq_ref, dk_s, dv_s, dc_s):
        j = pl.program_id(1)
        lanes, rows = _pair_masks()
        vv = v_ref[...]
        ktj = kt_ref[...]
        km = [k_ref[0], k_ref[1]]
        ktm = [_zero_other(ktj, rows[0]), _zero_other(ktj, rows[1])]
        dk_s[...] = jnp.zeros((2, T, LANES), F32)
        dv_s[...] = jnp.zeros((T, LANES), F32)
        dc_s[...] = jnp.zeros((2, T, 1), F32)

        @pl.when(j == 0)
        def _():
            dqt_ref[...] = jnp.zeros((nq, LANES, T), F32)
            dcq_ref[...] = jnp.zeros((2, nq, 1, T), F32)

        def step(i, masked):
            off = pl.multiple_of(i * T, T)
            doi = do_ref[pl.ds(off, T), :]
            upd = jnp.zeros((LANES, T), F32)
            for a in range(2):
                qi = q_ref[a, pl.ds(off, T), :]
                st = _dot_nt(km[a], qi)
                if masked:
                    st = jnp.where(_causal_t(T), st, NEG)
                pt = jnp.exp(st - l_ref[a, i])
                doa = _zero_other(doi, lanes[a])
                dv_s[...] += _dot(pt.astype(BF16), doa)
                dst = pt * (_dot_nt(vv, doa) - dl_ref[a, i])
                dsb = dst.astype(BF16)
                dk_s[a] += _dot(dsb, qi)
                upd = upd + _dot(ktm[a], dsb)
                dc_s[a] -= jnp.sum(dst, axis=-1, keepdims=True)
                dcq_ref[a, i] += jnp.sum(dst, axis=0, keepdims=True)
            dqt_ref[i] += upd

        def loop(i, carry):
            step(i, False)
            return carry

        step(j, True)
        lax.fori_loop(j + 1, nq, loop, 0)
        dk_ref[...] = jnp.where(lanes[0], dk_s[0], dk_s[1]).astype(BF16)
        dv_ref[...] = dv_s[...].astype(BF16)
        dck_ref[...] = dc_s[...]

    rowv = pl.BlockSpec((2, nq, 1, T), lambda h, j: (h, 0, 0, 0))
    tile = pl.BlockSpec((T, LANES), lambda h, j: (j, h))
    return pl.pallas_call(
        body, name=name, grid=(B_HEADS // 2, nq),
        in_specs=[pl.BlockSpec((2, T, LANES), lambda h, j: (h, j, 0)),
                  pl.BlockSpec((T, LANES), lambda h, j: (j, 8 + h)),
                  pl.BlockSpec((None, LANES, T), lambda h, j: (j, h, 0)),
                  pl.BlockSpec((2, S, LANES), lambda h, j: (h, 0, 0)),
                  pl.BlockSpec((S, LANES), lambda h, j: (0, h)),
                  rowv, rowv],
        out_specs=[tile, tile, pl.BlockSpec((2, T, 1), lambda h, j: (h, j, 0)),
                   pl.BlockSpec((nq, LANES, T), lambda h, j: (0, h, 0)), rowv],
        out_shape=[jax.ShapeDtypeStruct((S, A_WIDTH), BF16)] * 2 + [jax.ShapeDtypeStruct((B_HEADS, S, 1), F32),
                   jax.ShapeDtypeStruct((nq, A_WIDTH, T), F32), jax.ShapeDtypeStruct((B_HEADS, nq, 1, T), F32)],
        scratch_shapes=[pltpu.VMEM((2, T, LANES), F32), pltpu.VMEM((T, LANES), F32), pltpu.VMEM((2, T, 1), F32)],
        compiler_params=_cp(("parallel", "arbitrary")),
    )(kaug, ub, kt, qaug, do, lse, delta)


def _gate_fwd(o, ur, zcol, *, name):
    S = ur.shape[0]
    tm = min(1024, S)

    def body(o_ref, z_ref, y_ref):
        y_ref[...] = (o_ref[...] * _silu_parts(z_ref[...])[0]).astype(BF16)

    blk = pl.BlockSpec((tm, A_WIDTH), lambda i: (i, 0))
    return pl.pallas_call(
        body, name=name, grid=(S // tm,),
        in_specs=[blk, pl.BlockSpec((tm, A_WIDTH), lambda i: (i, zcol // A_WIDTH))],
        out_specs=blk, out_shape=jax.ShapeDtypeStruct((S, A_WIDTH), BF16),
        compiler_params=_cp(("parallel",)))(o, ur)


def _gate_bwd(o, ur, zcol, dy, *, name):
    S = ur.shape[0]
    tm = min(1024, S)

    def body(o_ref, z_ref, dy_ref, do_ref, dz_ref):
        sz, dsz = _silu_parts(z_ref[...])
        dyv = dy_ref[...]
        do_ref[...] = (dyv * sz).astype(BF16)
        dz_ref[...] = (dyv * o_ref[...] * dsz).astype(BF16)

    blk = pl.BlockSpec((tm, A_WIDTH), lambda i: (i, 0))
    return pl.pallas_call(
        body, name=name, grid=(S // tm,),
        in_specs=[blk, pl.BlockSpec((tm, A_WIDTH), lambda i: (i, zcol // A_WIDTH)), blk],
        out_specs=[blk, blk], out_shape=[jax.ShapeDtypeStruct((S, A_WIDTH), BF16)] * 2,
        compiler_params=_cp(("parallel",)))(o, ur, dy)


def _dfb(ur, bf_pad, dlogf_pad, *, name):
    S = ur.shape[0]
    tm = min(1024, S)

    def body(u_ref, b_ref, d_ref, o_ref, s_ref):
        i = pl.program_id(0)
        dv = d_ref[...] * _sigmoid(-(u_ref[...] + b_ref[...]))
        o_ref[...] = dv.astype(BF16)
        part = jnp.sum(dv, axis=0, keepdims=True)

        @pl.when(i == 0)
        def _():
            s_ref[...] = part

        @pl.when(i > 0)
        def _():
            s_ref[...] += part

    vec = pl.BlockSpec((1, FB_PAD), lambda i: (0, 0))
    blk = pl.BlockSpec((tm, FB_PAD), lambda i: (i, 0))
    return pl.pallas_call(
        body, name=name, grid=(S // tm,),
        in_specs=[pl.BlockSpec((tm, FB_PAD), lambda i: (i, R_FB // FB_PAD)), vec, blk],
        out_specs=[blk, vec],
        out_shape=[jax.ShapeDtypeStruct((S, FB_PAD), BF16), jax.ShapeDtypeStruct((1, FB_PAD), F32)],
        compiler_params=_cp(("arbitrary",)))(ur, bf_pad, dlogf_pad)


M_SCALE = HEAD ** -0.5


def _mem_fwd(ur, mkv, *, name):
    S = ur.shape[0]
    T = min(512, S)

    def body(q_ref, z_ref, k_ref, v_ref, y_ref):
        for h in range(M_HEADS):
            hs = slice(h * HEAD, (h + 1) * HEAD)
            s = _dot_nt(q_ref[:, hs].astype(BF16), k_ref[:, hs].astype(BF16)) * M_SCALE
            p = jnp.exp(s - jnp.max(s, axis=-1, keepdims=True))
            p = p / jnp.sum(p, axis=-1, keepdims=True)
            o = _dot(p.astype(BF16), v_ref[:, hs].astype(BF16))
            y_ref[:, hs] = (o * _silu_parts(z_ref[:, hs])[0]).astype(BF16)

    wide = lambda col: pl.BlockSpec((T, A_WIDTH), lambda i: (i, col // A_WIDTH))
    kv = lambda half: pl.BlockSpec((N_MEM, A_WIDTH), lambda i: (0, half))
    return pl.pallas_call(
        body, name=name, grid=(S // T,),
        in_specs=[wide(R_QM), wide(R_ZM), kv(0), kv(1)],
        out_specs=pl.BlockSpec((T, A_WIDTH), lambda i: (i, 0)),
        out_shape=jax.ShapeDtypeStruct((S, A_WIDTH), BF16),
        compiler_params=_cp(("parallel",)))(ur, ur, mkv, mkv)


def _mem_bwd(ur, mkv, dy, *, name):
    S = ur.shape[0]
    T = min(512, S)

    def body(q_ref, z_ref, k_ref, v_ref, dy_ref, dq_ref, dz_ref, dk_ref, dv_ref):
        i = pl.program_id(0)

        @pl.when(i == 0)
        def _():
            dk_ref[...] = jnp.zeros((N_MEM, A_WIDTH), F32)
            dv_ref[...] = jnp.zeros((N_MEM, A_WIDTH), F32)

        for h in range(M_HEADS):
            hs = slice(h * HEAD, (h + 1) * HEAD)
            qv = q_ref[:, hs].astype(BF16)
            kv = k_ref[:, hs].astype(BF16)
            vv = v_ref[:, hs].astype(BF16)
            s = _dot_nt(qv, kv) * M_SCALE
            p = jnp.exp(s - jnp.max(s, axis=-1, keepdims=True))
            p = p / jnp.sum(p, axis=-1, keepdims=True)
            o = _dot(p.astype(BF16), vv)
            sz, dsz = _silu_parts(z_ref[:, hs])
            dyv = dy_ref[:, hs]
            dz_ref[:, hs] = (dyv * o * dsz).astype(BF16)
            dov = (dyv * sz).astype(BF16)
            dp = _dot_nt(dov, vv)
            ds = p * (dp - jnp.sum(p * dp, axis=-1, keepdims=True))
            dq_ref[:, hs] = (_dot(ds.astype(BF16), kv) * M_SCALE).astype(BF16)
            dv_ref[:, hs] += _dot(p.T.astype(BF16), dov)
            dk_ref[:, hs] += _dot(ds.T.astype(BF16), qv) * M_SCALE

    wide = lambda col: pl.BlockSpec((T, A_WIDTH), lambda i: (i, col // A_WIDTH))
    kv = lambda half: pl.BlockSpec((N_MEM, A_WIDTH), lambda i: (0, half))
    tile = pl.BlockSpec((T, A_WIDTH), lambda i: (i, 0))
    acc = pl.BlockSpec((N_MEM, A_WIDTH), lambda i: (0, 0))
    return pl.pallas_call(
        body, name=name, grid=(S // T,),
        in_specs=[wide(R_QM), wide(R_ZM), kv(0), kv(1), tile],
        out_specs=[tile, tile, acc, acc],
        out_shape=[jax.ShapeDtypeStruct((S, A_WIDTH), BF16)] * 2
        + [jax.ShapeDtypeStruct((N_MEM, A_WIDTH), F32)] * 2,
        compiler_params=_cp(("arbitrary",)))(ur, ur, mkv, mkv, dy)


def _branch_fwd(ys, wbs, ur, b_merge, *, name):
    S = ur.shape[0]
    tm, tn = min(512, S), 512
    nj = D_MODEL // tn

    def body(ya, yb, ym, wa, wb, wm, g0, g1, g2, b0, b1, b2, mg_ref, p_ref):
        acc = jnp.zeros((tm, tn), F32)
        for i, (y, w, gr, br) in enumerate(((ya, wa, g0, b0), (yb, wb, g1, b1), (ym, wm, g2, b2))):
            pr = _dot(y[...], w[...])
            p_ref[i] = pr.astype(BF16)
            acc = acc + _sigmoid(gr[...] + br[...]) * pr
        mg_ref[...] = acc.astype(BF16)

    yspec = pl.BlockSpec((tm, A_WIDTH), lambda i, j: (i, 0))
    wspec = pl.BlockSpec((A_WIDTH, tn), lambda i, j: (0, j))
    gspec = lambda b: pl.BlockSpec((tm, tn), lambda i, j: (i, (R_GL + b * D_MODEL) // tn + j))
    bspec = lambda b: pl.BlockSpec((1, tn), lambda i, j: (0, b * nj + j))
    return pl.pallas_call(
        body, name=name, grid=(S // tm, nj),
        in_specs=[yspec] * 3 + [wspec] * 3 + [gspec(0), gspec(1), gspec(2), bspec(0), bspec(1), bspec(2)],
        out_specs=[pl.BlockSpec((tm, tn), lambda i, j: (i, j)),
                   pl.BlockSpec((3, tm, tn), lambda i, j: (0, i, j))],
        out_shape=[jax.ShapeDtypeStruct((S, D_MODEL), BF16), jax.ShapeDtypeStruct((3, S, D_MODEL), BF16)],
        compiler_params=_cp(("parallel", "parallel")))(*ys, *wbs, ur, ur, ur, b_merge, b_merge, b_merge)


def _branch_bwd(dm, prods, ur, b_merge, *, name):
    S = ur.shape[0]
    tm = min(256, S)

    def body(dm_ref, p_ref, g0, g1, g2, b_ref, dp_ref, dgl_ref, db_ref):
        i = pl.program_id(0)
        dmv = dm_ref[...]
        parts = []
        for b, gr in enumerate((g0, g1, g2)):
            sl = slice(b * D_MODEL, (b + 1) * D_MODEL)
            gt = _sigmoid(gr[...] + b_ref[:, sl])
            dp_ref[b] = (dmv * gt).astype(BF16)
            dgl = dmv * p_ref[b].astype(F32) * gt * (1.0 - gt)
            dgl_ref[:, sl] = dgl.astype(BF16)
            parts.append(jnp.sum(dgl, axis=0, keepdims=True))
        part = jnp.concatenate(parts, axis=1)

        @pl.when(i == 0)
        def _():
            db_ref[...] = part

        @pl.when(i > 0)
        def _():
            db_ref[...] += part

    gspec = lambda b: pl.BlockSpec((tm, D_MODEL), lambda i: (i, R_GL // D_MODEL + b))
    vec = pl.BlockSpec((1, 3 * D_MODEL), lambda i: (0, 0))
    return pl.pallas_call(
        body, name=name, grid=(S // tm,),
        in_specs=[pl.BlockSpec((tm, D_MODEL), lambda i: (i, 0)),
                  pl.BlockSpec((3, tm, D_MODEL), lambda i: (0, i, 0)), gspec(0), gspec(1), gspec(2), vec],
        out_specs=[pl.BlockSpec((3, tm, D_MODEL), lambda i: (0, i, 0)),
                   pl.BlockSpec((tm, 3 * D_MODEL), lambda i: (i, 0)), vec],
        out_shape=[jax.ShapeDtypeStruct((3, S, D_MODEL), BF16), jax.ShapeDtypeStruct((S, 3 * D_MODEL), BF16),
                   jax.ShapeDtypeStruct((1, 3 * D_MODEL), F32)],
        compiler_params=_cp(("arbitrary",)))(dm, prods, ur, ur, ur, b_merge)


def _rope_tables(pos):
    half = ROT // 2
    S = pos.shape[0]
    inv = ROPE_THETA ** (-jnp.arange(half, dtype=F32) / half)
    per_row = LANES // half
    ang = jnp.repeat(pos.astype(F32).reshape(S // per_row, per_row), half, axis=1) * jnp.tile(inv, per_row)
    cos, sin = jnp.cos(ang).reshape(S, half), jnp.sin(ang).reshape(S, half)
    one = jnp.ones((S, LANES - ROT), F32)
    zero = jnp.zeros((S, LANES - ROT), F32)
    zh = jnp.zeros((S, half), F32)
    c = jnp.concatenate([cos, cos, one], axis=1)
    s1 = jnp.concatenate([-sin, zh, zero], axis=1)
    s2 = jnp.concatenate([zh, sin, zero], axis=1)
    return c, s1, s2


def _to_tiles(t):
    S, H = t.shape
    return t.reshape(S // LANES, LANES, H).transpose(0, 2, 1)


def _from_tiles(t):
    nt, H, _ = t.shape
    return t.transpose(1, 0, 2).reshape(H, nt * LANES)


def _local_step(x, mem, pos, tgt, g_pre, g_post, g_mem, wt, bf_pad, b_merge, w_kv, wbs, w_out, pack=None, hs=None):
    S = x.shape[0]
    T = min(512, S)
    nq = S // T
    tabs = _rope_tables(pos)

    if hs is None:
        hs = _rms_fwd(x, g_pre, name="rms_pre", dilations=DIL)
    h = hs[0]
    tabs_g = [[_to_classes(t, d) for t in tabs] for d in DIL]
    qkvs = [_proj_rope(hs[g], wt[f"A{g}"], tabs_g[g], name=f"proj_a{g}") for g in range(3)]
    ub = _mm(h, wt["B"], bt=True, out_dtype=BF16, name="proj_b", tn=1536)
    ur = _mm(h, wt["R"], bt=True, name="proj_r", tn=1792)

    outs_c, lses_c = [], []
    for g in range(3):
        o, l = _attn_a_fwd(qkvs[g], g, name=f"attn_a_fwd{g}")
        outs_c.append(o)
        lses_c.append(l)
    ya = _merge_a_fwd(outs_c, lses_c, ur, name="merge_a_fwd")

    logf = _logf(ur, bf_pad, name="logf")
    c = _from_tiles(_cumsum_lanes(_to_tiles(logf[:, :B_HEADS]), False, name="cumsum_fwd"))
    ckb = jnp.broadcast_to(c[:, :, None], (B_HEADS, S, LANES))
    qaug, kaug = _fox_aug(ub, ckb, name="fox_aug")
    kt = ub[:, 512:1024].reshape(nq, T, 512).transpose(0, 2, 1)
    vt = ub[:, 1024:1536].reshape(nq, T, 512).transpose(0, 2, 1)
    ob, lse_b = _fox_fwd(qaug, kaug, vt, name="fox_fwd")
    yb = _gate_fwd(ob, ur, R_ZB, name="gate_b_fwd")

    hm = _rms_fwd(mem, g_mem, name="rms_mem")
    mkv = _mm(hm, w_kv, name="proj_mem")
    ym = _mem_fwd(ur, mkv, name="mem_fwd")

    merged, prods = _branch_fwd((ya, yb, ym), wbs, ur, b_merge, name="branch_fwd")
    out = _mm(merged, w_out, name="proj_out")
    dy, d_out, dg_post, loss_row = _post(x, out, tgt, g_post, name="post")

    dmerged = _mm(d_out, w_out, bt=True, name="d_merged")
    dw_out = _mm(merged, d_out, at=True, name="dw_out", tk=2048)
    dprods, dgl, db_merge = _branch_bwd(dmerged, prods, ur, b_merge, name="branch_bwd")
    dys, dwbs = [], []
    for i, (y, wb) in enumerate(zip((ya, yb, ym), wbs)):
        dys.append(_mm(dprods[i], wb, bt=True, name=f"d_y{i}"))
        dwbs.append(_mm(y, dprods[i], at=True, name=f"dw_branch{i}", tk=2048))

    dos_c, adjs_c, dza = _merge_a_bwd(outs_c, lses_c, ur, dys[0], name="merge_a_bwd")
    dus_a = []
    for g, d in enumerate(DIL):
        do_c, adj_c = dos_c[g], adjs_c[g]
        dq =_attn_a_dq(qkvs[g], tabs_g[g], g, do_c, lses_c[g], adj_c, name=f"attn_a_dq{g}")
        dk, dv = _attn_a_dkv(qkvs[g], tabs_g[g], g, do_c, lses_c[g], adj_c, name=f"attn_a_dkv{g}")
        dus_a.append(jnp.concatenate([dq, dk, dv], axis=1))

    dob, dzb = _gate_bwd(ob, ur, R_ZB, dys[1], name="gate_b_bwd")
    delta_b = _fox_delta(ob, dob, name="fox_delta")
    dkb, dvb, dc_k, dqt, dc_q = _fox_bwd(ub, qaug, kaug, kt, dob, lse_b, delta_b, name="fox_bwd")
    dqb = (dqt.transpose(0, 2, 1).reshape(S, A_WIDTH) * B_SCALE).astype(BF16)
    du_b = jnp.concatenate([dqb, dkb, dvb], axis=1)
    dc = dc_q.reshape(B_HEADS, S) + dc_k.reshape(B_HEADS, S)
    dlogf = _from_tiles(_cumsum_lanes(_to_tiles(dc.T), True, name="cumsum_bwd"))
    dlogf_pad = jnp.pad(dlogf.T, ((0, 0), (0, FB_PAD - B_HEADS)))
    dfb, db_forget = _dfb(ur, bf_pad, dlogf_pad, name="dfb")

    dqm, dzm, dmk, dmv = _mem_bwd(ur, mkv, dys[2], name="mem_bwd")
    dmkv = jnp.concatenate([dmk, dmv], axis=1).astype(BF16)
    dhm = _mm(dmkv, w_kv, bt=True, name="d_hm")
    dw_kv = _mm(hm, dmkv, at=True, name="dw_kv")
    dg_mem = _rms_bwd(mem, g_mem, dhm, None, name="rms_mem_bwd")

    du_r = jnp.concatenate([dza, dzb, dqm, dzm, dgl, dfb], axis=1)
    dwt = {"R": _mm(du_r, h, at=True, name="dw_in_r", tm=1792, tk=1024),
           "B": _mm(du_b, h, at=True, name="dw_in_b", tm=1536, tk=2048)}
    for g in range(3):
        dwt[f"A{g}"] = _mm(dus_a[g], hs[g], at=True, name=f"dw_in_a{g}", tm=1536, tk=2048)
    res = dict(dwt=dwt, dw_kv=dw_kv, dwbs=dwbs, dw_out=dw_out)
    token_major = [(du_r, wt["R"]), (du_b, wt["B"]), (dus_a[0], wt["A0"])]
    if pack is None:
        dh_1 = _mm(dus_a[1], wt["A1"], name="d_h_a1", tk=1536)
        dh = _mm_sum(token_major, name="d_h_main")
    else:
        gbig = pack(dwt, dw_kv, dwbs, dw_out)
        own_idx = _own_slabs()
        dh_1, sib = _mm(dus_a[1], wt["A1"], name="d_h_a1", tk=1536, comm=_pair_comm(gbig))
        send = _pair_sum(gbig, sib, own_idx, 208, name="pair_sum")
        dh, recv = _mm_sum(token_major, name="d_h_main", comm=_chips_comm(send))
        res = dict(parts=[(gbig, None), (sib, 1), (recv, N_CHIP - 1)], own_idx=own_idx)
    dh_2 = _mm(dus_a[2], wt["A2"], name="d_h_a2", tk=1536)
    grad_x, dg_pre = _rms_bwd(x, g_pre, dh, dy, name="rms_pre_bwd", dh_classes=[(dh_1, DIL[1]), (dh_2, DIL[2])])

    return dict(res, loss=loss_row, grad_x=grad_x, dg_pre=dg_pre, dg_post=dg_post, dg_mem=dg_mem,
                db_forget=db_forget, db_merge=db_merge)


MESH = pl.DeviceIdType.MESH
ANY = pl.BlockSpec(memory_space=pl.ANY)


def _relations():
    return [(k >> 2 & 1, k >> 1 & 1, k & 1) for k in range(1, N_DEV)]


def _coords():
    return lax.axis_index("x"), lax.axis_index("y"), lax.axis_index("c")


def _gather_comm(shard):
    R, W = shard.shape

    def plan(x_ref, out_ref, send_sems, recv_sems, local_sem):
        x, y, c = _coords()
        me, sibling = (x, y, c), (x, y, 1 - c)
        chips = [(1 - x, y), (x, 1 - y), (1 - x, 1 - y)]

        def slot(px, py, pc):
            return out_ref.at[4 * px + 2 * py + pc]

        def copy(k, block, to, src=None):
            return pltpu.make_async_remote_copy(
                src_ref=slot(*block) if src is None else src, dst_ref=slot(*block),
                send_sem=send_sems.at[k], recv_sem=recv_sems.at[k], device_id=to, device_id_type=MESH)

        mine = pltpu.make_async_copy(x_ref, slot(*me), local_sem)
        first = [copy(0, me, sibling, src=x_ref)]
        first += [copy(1 + j, me, (*chip, c), src=x_ref) for j, chip in enumerate(chips)]
        return me, sibling, chips, c, copy, mine, first

    def start(*refs):
        _, _, _, _, _, mine, first = plan(*refs)
        mine.start()
        for cp in first:
            cp.start()

    def wait(*refs):
        me, sibling, chips, c, copy, mine, first = plan(*refs)
        passed = [copy(4 + j, (*chip, c), sibling) for j, chip in enumerate(chips)]
        for j, chip in enumerate(chips):
            copy(1 + j, (*chip, c), me).wait_recv()
            passed[j].start()
        copy(0, sibling, me).wait_recv()
        for j, chip in enumerate(chips):
            copy(4 + j, (*chip, 1 - c), me).wait_recv()
        for cp in first + passed:
            cp.wait_send()
        mine.wait()

    return dict(inputs=[shard], out_shape=[jax.ShapeDtypeStruct((N_DEV, R, W), shard.dtype)],
                sems=[pltpu.SemaphoreType.DMA((N_DEV - 1,)), pltpu.SemaphoreType.DMA((N_DEV - 1,)),
                      pltpu.SemaphoreType.DMA],
                start=start, wait=wait)


N_CHIP = 4


def _pair_comm(gbig):
    _, R, W = gbig.shape

    def copies(g_ref, sib_ref, send_sems, recv_sems):
        x, y, c = _coords()
        return [pltpu.make_async_remote_copy(
            src_ref=g_ref.at[4 * (x ^ (r >> 1)) + 2 * (y ^ (r & 1)) + (1 - c)], dst_ref=sib_ref.at[r],
            send_sem=send_sems.at[r], recv_sem=recv_sems.at[r], device_id=(x, y, 1 - c), device_id_type=MESH)
            for r in range(N_CHIP)]

    def start(*refs):
        for cp in copies(*refs):
            cp.start()

    def wait(*refs):
        cps = copies(*refs)
        for cp in cps:
            cp.wait_recv()
        for cp in cps:
            cp.wait_send()

    return dict(inputs=[gbig], out_shape=[jax.ShapeDtypeStruct((N_CHIP, R, W), gbig.dtype)],
                sems=[pltpu.SemaphoreType.DMA((N_CHIP,)), pltpu.SemaphoreType.DMA((N_CHIP,))],
                start=start, wait=wait)


def _own_slabs():
    x, y, c = _coords()
    return jnp.stack([4 * (x ^ (r >> 1)) + 2 * (y ^ (r & 1)) + c for r in range(N_CHIP)]).astype(jnp.int32)


def _pair_sum(gbig, sib, own_idx, tr, *, name):
    _, R, W = gbig.shape

    def body(idx_ref, a_ref, b_ref, o_ref):
        o_ref[...] = (a_ref[...] + b_ref[...]).astype(BF16)

    return pl.pallas_call(
        body, name=name,
        grid_spec=pltpu.PrefetchScalarGridSpec(
            num_scalar_prefetch=1, grid=(N_CHIP - 1, R // tr),
            in_specs=[pl.BlockSpec((None, tr, W), lambda r, i, idx: (idx[r + 1], i, 0)),
                      pl.BlockSpec((None, tr, W), lambda r, i, idx: (r + 1, i, 0))],
            out_specs=pl.BlockSpec((None, tr, W), lambda r, i, idx: (r, i, 0))),
        out_shape=jax.ShapeDtypeStruct((N_CHIP - 1, R, W), BF16),
        compiler_params=_cp(("parallel", "parallel")))(own_idx, gbig, sib)


def _chips_comm(send):
    nb, R, W = send.shape

    def copies(b_ref, rb_ref, send_sems, recv_sems):
        x, y, c = _coords()
        return [pltpu.make_async_remote_copy(
            src_ref=b_ref.at[r - 1], dst_ref=rb_ref.at[r - 1], send_sem=send_sems.at[r - 1],
            recv_sem=recv_sems.at[r - 1], device_id=(x ^ (r >> 1), y ^ (r & 1), c), device_id_type=MESH)
            for r in range(1, N_CHIP)]

    def start(*refs):
        for cp in copies(*refs):
            cp.start()

    def wait(*refs):
        cps = copies(*refs)
        for cp in cps:
            cp.wait_recv()
        for cp in cps:
            cp.wait_send()

    return dict(inputs=[send], out_shape=[jax.ShapeDtypeStruct((nb, R, W), send.dtype)],
                sems=[pltpu.SemaphoreType.DMA((nb,)), pltpu.SemaphoreType.DMA((nb,))],
                start=start, wait=wait)


def _gather_small(gsmall, *, name):
    n = N_DEV - 1

    def body(s_ref, rs_ref, send_sems, recv_sems, local_sem):
        x, y, c = _coords()
        me = 4 * x + 2 * y + c
        mine = pltpu.make_async_copy(s_ref, rs_ref.at[me], local_sem)
        mine.start()

        def copy(k, fx, fy, fc, slot):
            return pltpu.make_async_remote_copy(
                src_ref=s_ref, dst_ref=rs_ref.at[slot], send_sem=send_sems.at[k], recv_sem=recv_sems.at[k],
                device_id=(x ^ fx, y ^ fy, c ^ fc), device_id_type=MESH)

        started = [copy(k, *rel, me) for k, rel in enumerate(_relations())]
        for cp in started:
            cp.start()
        for k, (fx, fy, fc) in enumerate(_relations()):
            copy(k, fx, fy, fc, 4 * (x ^ fx) + 2 * (y ^ fy) + (c ^ fc)).wait_recv()
        for cp in started:
            cp.wait_send()
        mine.wait()

    return pl.pallas_call(
        body, name=name, out_shape=jax.ShapeDtypeStruct((N_DEV, 1, P_SMALL), gsmall.dtype),
        in_specs=[ANY], out_specs=ANY,
        scratch_shapes=[pltpu.SemaphoreType.DMA((n,)), pltpu.SemaphoreType.DMA((n,)), pltpu.SemaphoreType.DMA],
    )(gsmall)


def _part_specs(parts, tr, row0):
    assert row0 % tr == 0
    specs = []
    for a, n_used in parts:
        if n_used is None:
            specs.append(pl.BlockSpec((1, tr, a.shape[2]), lambda i, idx: (idx[0], row0 // tr + i, 0)))
        else:
            specs.append(pl.BlockSpec((n_used, tr, a.shape[2]), lambda i, idx: (0, row0 // tr + i, 0)))
    return specs


def _part_total(refs, parts):
    g = None
    for ref, (_, n_used) in zip(refs, parts):
        for k in range(n_used or 1):
            t = ref[k].astype(F32)
            g = t if g is None else g + t
    return g


def _sum_parts(parts, idx, row0, nrows, tr, *, name):
    W = parts[0][0].shape[2]
    assert nrows % tr == 0

    def body(idx_ref, *refs):
        refs[-1][...] = _part_total(refs[:-1], parts)

    return pl.pallas_call(
        body, name=name,
        grid_spec=pltpu.PrefetchScalarGridSpec(
            num_scalar_prefetch=1, grid=(nrows // tr,), in_specs=_part_specs(parts, tr, row0),
            out_specs=pl.BlockSpec((tr, W), lambda i, idx: (i, 0))),
        out_shape=jax.ShapeDtypeStruct((nrows, W), F32),
        compiler_params=_cp(("parallel",)))(idx, *[a for a, _ in parts])


def _adamw(parts, idx, w, m, v, tr, *, name):
    R, W = w.shape
    assert R % tr == 0
    np_ = len(parts)

    def body(idx_ref, *refs):
        w_ref, m_ref, v_ref, g_ref, d_ref, nm_ref, nv_ref = refs[np_:]
        g = _part_total(refs[:np_], parts)
        mm = ADAM_B1 * m_ref[...] + (1.0 - ADAM_B1) * g
        vv = ADAM_B2 * v_ref[...] + (1.0 - ADAM_B2) * (g * g)
        m_hat = mm / (1.0 - ADAM_B1 ** ADAM_STEP)
        v_hat = vv / (1.0 - ADAM_B2 ** ADAM_STEP)
        g_ref[...] = g
        d_ref[...] = -ADAM_LR * (m_hat / (jnp.sqrt(v_hat) + ADAM_EPS) + ADAM_WD * w_ref[...])
        nm_ref[...] = mm
        nv_ref[...] = vv

    blk = pl.BlockSpec((tr, W), lambda i, idx: (i, 0))
    return pl.pallas_call(
        body, name=name,
        grid_spec=pltpu.PrefetchScalarGridSpec(
            num_scalar_prefetch=1, grid=(R // tr,), in_specs=_part_specs(parts, tr, 0) + [blk, blk, blk],
            out_specs=[blk] * 4),
        out_shape=[jax.ShapeDtypeStruct((R, W), F32)] * 4,
        compiler_params=_cp(("parallel",)))(idx, *[a for a, _ in parts], w, m, v)


def _pack_rest(w_kv, wa, wb, wm, w_out):
    return jnp.concatenate([w_kv[0], w_out[0]] + [t[0].reshape(-1, D_MODEL) for t in (wa, wb, wm)], axis=0)


def _unpack_rest(t):
    br = lambda i: t[RO_BR + 64 * i:RO_BR + 64 * (i + 1)].reshape(1, A_WIDTH, D_MODEL // N_DEV)
    return t[None, RO_KV:RO_OUT], br(0), br(1), br(2), t[None, RO_OUT:RO_BR]


def _orig_rows(gathered, a, b):
    res = []
    while a < b:
        dev, r = divmod(a, CS)
        n = min(b - a, CS - r)
        res.append(gathered[dev, RO_IN + r:RO_IN + r + n])
        a += n
    return res


def _full_weights(gathered):
    wt = {}
    for name, ranges in SEGS.items():
        rows = [p for a, b in ranges for p in _orig_rows(gathered, a, b)]
        if SEG_PAD[name]:
            rows.append(jnp.zeros((SEG_PAD[name], D_MODEL), gathered.dtype))
        wt[name] = jnp.concatenate(rows, axis=0)
    w_kv = gathered[:, RO_KV:RO_OUT].reshape(D_MODEL, D_MODEL)
    w_out = gathered[:, RO_OUT:RO_BR].reshape(D_MODEL, D_MODEL)
    wbs = [gathered[:, RO_BR + 64 * i:RO_BR + 64 * (i + 1)].reshape(N_DEV, A_WIDTH, D_MODEL // N_DEV)
           .transpose(1, 0, 2).reshape(A_WIDTH, D_MODEL) for i in range(3)]
    return wt, w_kv, wbs, w_out


def _orig_order(dwt):
    pieces = []
    for name, ranges in SEGS.items():
        o = 0
        for a, b in ranges:
            pieces.append((a, dwt[name][o:o + b - a]))
            o += b - a
    pieces.sort(key=lambda p: p[0])
    return jnp.concatenate([p[1] for p in pieces], axis=0)


def _pack_grads(dwt, dw_kv, dwbs, dw_out):
    g_in = jnp.pad(_orig_order(dwt).reshape(N_DEV, CS, D_MODEL), ((0, 0), (0, IN_ROWS - CS), (0, 0)))
    br = [t.reshape(A_WIDTH, N_DEV, D_MODEL // N_DEV).transpose(1, 0, 2).reshape(N_DEV, -1, D_MODEL) for t in dwbs]
    return jnp.concatenate([dw_kv.reshape(N_DEV, -1, D_MODEL), dw_out.reshape(N_DEV, -1, D_MODEL)] + br + [g_in],
                           axis=1)


def kernel(x, mem, positions, norm_pre_g, norm_post_g, norm_mem_g, w_in, b_forget, b_merge, w_mem_kv, w_branch_a, w_branch_b, w_branch_m, w_out, loss_target, m_norm_pre_g, m_norm_post_g, m_norm_mem_g, m_w_in, m_b_forget, m_b_merge, m_w_mem_kv, m_w_branch_a, m_w_branch_b, m_w_branch_m, m_w_out, v_norm_pre_g, v_norm_post_g, v_norm_mem_g, v_w_in, v_b_forget, v_b_merge, v_w_mem_kv, v_w_branch_a, v_w_branch_b, v_w_branch_m, v_w_out):
    w_rest = _pack_rest(w_mem_kv, w_branch_a, w_branch_b, w_branch_m, w_out)
    shard = jnp.concatenate([w_rest.astype(BF16), w_in[0].T.astype(BF16),
                             jnp.zeros((IN_ROWS - CS, D_MODEL), BF16)], axis=0)
    hs, (gathered,) = _rms_fwd(x[0], norm_pre_g, name="rms_pre_gather", dilations=DIL, comm=_gather_comm(shard))
    wt, w_kv, wbs, w_o = _full_weights(gathered)

    bf_pad = jnp.pad(b_forget, ((0, 0), (0, FB_PAD - B_HEADS)))
    r = _local_step(x[0], mem[0], positions[0], loss_target[0], norm_pre_g, norm_post_g, norm_mem_g,
                    wt, bf_pad, b_merge, w_kv, wbs, w_o, pack=_pack_grads, hs=hs)

    gsmall = jnp.concatenate([r["dg_pre"], r["dg_post"], r["dg_mem"], r["db_merge"],
                              r["db_forget"][:, :LANES], r["loss"]], axis=1)
    rsmall = _gather_small(gsmall, name="gather_small")
    parts, own_idx = r["parts"], r["own_idx"]

    m_rest = _pack_rest(m_w_mem_kv, m_w_branch_a, m_w_branch_b, m_w_branch_m, m_w_out)
    v_rest = _pack_rest(v_w_mem_kv, v_w_branch_a, v_w_branch_b, v_w_branch_m, v_w_out)
    outs_rest = [_unpack_rest(t) for t in _adamw(parts, own_idx, w_rest, m_rest, v_rest, 64, name="adamw_rest")]
    g_in = _sum_parts(parts, own_idx, RO_IN, IN_ROWS, 16, name="sum_w_in")[:CS].T
    outs_in = _adamw([(g_in[None], 1)], own_idx, w_in[0], m_w_in[0], v_w_in[0], 128, name="adamw_w_in")

    def small_vec(a, b, c, d, e):
        z = jnp.zeros((1, LANES - B_HEADS), F32)
        return jnp.concatenate([a, b, c, d, e, z, jnp.zeros((1, LANES), F32)], axis=1)

    outs_small = _adamw([(rsmall, N_DEV)], own_idx, small_vec(norm_pre_g, norm_post_g, norm_mem_g, b_merge, b_forget),
                        small_vec(m_norm_pre_g, m_norm_post_g, m_norm_mem_g, m_b_merge, m_b_forget),
                        small_vec(v_norm_pre_g, v_norm_post_g, v_norm_mem_g, v_b_merge, v_b_forget),
                        1, name="adamw_small")

    def small_parts(t):
        return [t[:, O_GPRE:O_GPRE + D_MODEL], t[:, O_GPOST:O_GPOST + D_MODEL], t[:, O_GMEM:O_GMEM + D_MODEL],
                t[:, O_BF:O_BF + B_HEADS], t[:, O_BM:O_BM + 3 * D_MODEL]]

    loss = outs_small[0][0, O_LOSS]
    result = [loss, r["grad_x"][None]]
    for rest, w_i, small in zip(outs_rest, outs_in, outs_small):
        gp, gq, gm, bf, bm = small_parts(small)
        w_k, w_a, w_b, w_m, w_ot = rest
        result += [gp, gq, gm, w_i[None], bf, bm, w_k, w_a, w_b, w_m, w_ot]
    return tuple(result)
```

```python
import jax
import jax.numpy as jnp
from jax import lax
from jax.experimental import pallas as pl
from jax.experimental.pallas import tpu as pltpu

F32 = jnp.float32
BF16 = jnp.bfloat16

N_DEV = 8
D_MODEL = 1024
N_MEM = 256
EPS = 1e-6
NEG = -1e30
ROPE_THETA = 500000.0
DIL = (1, 4, 16)
A_HEADS = 4
HEAD = 128
A_WIDTH = 512
B_HEADS = 8
B_HEAD = 64
M_HEADS = 4
ROT = 32
IN_COLS = 11272
FB_PAD = 256

SEGS = {
    "A0": ((0, 512), (1536, 2048), (3072, 3584)),
    "A1": ((512, 1024), (2048, 2560), (3584, 4096)),
    "A2": ((1024, 1536), (2560, 3072), (4096, 4608)),
    "B": ((5120, 6656),),
    "R": ((4608, 5120), (6664, 7176), (7176, 7688), (7688, 8200), (8200, 11272), (6656, 6664)),
}
SEG_PAD = {"A0": 0, "A1": 0, "A2": 0, "B": 0, "R": FB_PAD - B_HEADS}
R_ZA, R_ZB, R_QM, R_ZM, R_GL, R_FB = 0, 512, 1024, 1536, 2048, 5120
NR = R_FB + FB_PAD

ADAM_LR, ADAM_B1, ADAM_B2, ADAM_EPS, ADAM_WD, ADAM_STEP = 0.001, 0.9, 0.999, 1e-08, 0.01, 10

LANES = 128
VMEM_LIMIT = 56 * 1024 * 1024

CS = IN_COLS // N_DEV
RO_KV, RO_OUT, RO_BR, RO_IN = 0, 128, 256, 448
IN_ROWS = 1424
ROWS = RO_IN + IN_ROWS
O_GPRE, O_GPOST, O_GMEM, O_BM, O_BF, O_LOSS = 0, 1024, 2048, 3072, 6144, 6272
P_SMALL = 6400


def _cp(sem=None):
    return pltpu.CompilerParams(dimension_semantics=sem, vmem_limit_bytes=VMEM_LIMIT)


def _dot(a, b):
    return jnp.dot(a, b, preferred_element_type=F32)


def _dot_nt(a, b):
    return lax.dot_general(a, b, (((1,), (1,)), ((), ())), preferred_element_type=F32)


def _sigmoid(z):
    return 1.0 / (1.0 + jnp.exp(-z))


def _mm(a, b, *, name, at=False, bt=False, out_dtype=F32, tm=1024, tn=1024, tk=None, comm=None):
    assert not (at and bt)
    K, M = a.shape if at else a.shape[::-1]
    N = b.shape[0] if bt else b.shape[1]
    tm, tn = min(tm, M), min(tn, N)
    tk = K if tk is None else min(tk, K)
    assert M % tm == 0 and N % tn == 0 and K % tk == 0
    nk = K // tk
    grid = (M // tm, N // tn, nk)
    n_in = len(comm["inputs"]) if comm else 0
    n_out = len(comm["out_shape"]) if comm else 0

    def body(a_ref, b_ref, *rest):
        c_in, o_ref, c_out = rest[:n_in], rest[n_in], rest[n_in + 1:n_in + 1 + n_out]
        acc_ref, sems = rest[n_in + 1 + n_out], rest[n_in + 2 + n_out:]
        if comm:
            step = (pl.program_id(0) * grid[1] + pl.program_id(1)) * grid[2] + pl.program_id(2)

            @pl.when(step == 0)
            def _():
                comm["start"](*c_in, *c_out, *sems)

        av = a_ref[...].astype(BF16)
        bv = b_ref[...].astype(BF16)
        if at:
            p = lax.dot_general(av, bv, (((0,), (0,)), ((), ())), preferred_element_type=F32)
        else:
            p = _dot_nt(av, bv) if bt else _dot(av, bv)
        if nk == 1:
            o_ref[...] = p.astype(out_dtype)
        else:
            k = pl.program_id(2)

            @pl.when(k == 0)
            def _():
                acc_ref[...] = p

            @pl.when(k > 0)
            def _():
                acc_ref[...] += p

            @pl.when(k == nk - 1)
            def _():
                o_ref[...] = acc_ref[...].astype(out_dtype)

        if comm:
            @pl.when(step == grid[0] * grid[1] * grid[2] - 1)
            def _():
                comm["wait"](*c_in, *c_out, *sems)

    b_spec = (pl.BlockSpec((tn, tk), lambda i, j, k: (j, k)) if bt
              else pl.BlockSpec((tk, tn), lambda i, j, k: (k, j)))
    a_spec = (pl.BlockSpec((tk, tm), lambda i, j, k: (k, i)) if at
              else pl.BlockSpec((tm, tk), lambda i, j, k: (i, k)))
    out_spec = pl.BlockSpec((tm, tn), lambda i, j, k: (i, j))
    out_shape = jax.ShapeDtypeStruct((M, N), out_dtype)
    acc = pltpu.VMEM((tm, tn) if nk > 1 else (8, LANES), F32)
    if not comm:
        return pl.pallas_call(
            body, name=name, grid=grid, in_specs=[a_spec, b_spec], out_specs=out_spec, out_shape=out_shape,
            scratch_shapes=[acc], compiler_params=_cp(("parallel", "parallel", "arbitrary")))(a, b)
    return pl.pallas_call(
        body, name=name, grid=grid, in_specs=[a_spec, b_spec] + [ANY] * n_in,
        out_specs=[out_spec] + [ANY] * n_out, out_shape=[out_shape] + comm["out_shape"],
        scratch_shapes=[acc] + comm["sems"],
        compiler_params=_cp(("arbitrary", "arbitrary", "arbitrary")))(a, b, *comm["inputs"])


def _mm_sum(pairs, *, name, tm=1024, tk=768, comm=None):
    M, N = pairs[0][0].shape[0], pairs[0][1].shape[1]
    tm = min(tm, M)
    steps = [a.shape[1] // tk for a, _ in pairs]
    assert M % tm == 0 and all(a.shape[1] % tk == 0 for a, _ in pairs)
    first = [sum(steps[:p]) for p in range(len(pairs))]
    total = sum(steps)
    grid = (M // tm, total)
    n_in = len(comm["inputs"]) if comm else 0
    n_out = len(comm["out_shape"]) if comm else 0
    npair = len(pairs)

    def body(*refs):
        ab, rest = refs[:2 * npair], refs[2 * npair:]
        c_in, o_ref, c_out = rest[:n_in], rest[n_in], rest[n_in + 1:n_in + 1 + n_out]
        acc_ref, sems = rest[n_in + 1 + n_out], rest[n_in + 2 + n_out:]
        k = pl.program_id(1)
        if comm:
            step = pl.program_id(0) * total + k

            @pl.when(step == 0)
            def _():
                comm["start"](*c_in, *c_out, *sems)

        @pl.when(k == 0)
        def _():
            acc_ref[...] = jnp.zeros((tm, N), F32)

        for p in range(npair):
            @pl.when(jnp.logical_and(k >= first[p], k < first[p] + steps[p]))
            def _(p=p):
                acc_ref[...] += _dot(ab[2 * p][...], ab[2 * p + 1][...])

        @pl.when(k == total - 1)
        def _():
            o_ref[...] = acc_ref[...]

        if comm:
            @pl.when(step == grid[0] * total - 1)
            def _():
                comm["wait"](*c_in, *c_out, *sems)

    def local(p):
        return lambda k: jnp.clip(k - first[p], 0, steps[p] - 1)

    in_specs = []
    for p in range(npair):
        in_specs += [pl.BlockSpec((tm, tk), lambda i, k, f=local(p): (i, f(k))),
                     pl.BlockSpec((tk, N), lambda i, k, f=local(p): (f(k), 0))]
    out_spec = pl.BlockSpec((tm, N), lambda i, k: (i, 0))
    out_shape = jax.ShapeDtypeStruct((M, N), F32)
    args = [t for pair in pairs for t in pair]
    if not comm:
        return pl.pallas_call(
            body, name=name, grid=grid, in_specs=in_specs, out_specs=out_spec, out_shape=out_shape,
            scratch_shapes=[pltpu.VMEM((tm, N), F32)], compiler_params=_cp(("parallel", "arbitrary")))(*args)
    return pl.pallas_call(
        body, name=name, grid=grid, in_specs=in_specs + [ANY] * n_in,
        out_specs=[out_spec] + [ANY] * n_out, out_shape=[out_shape] + comm["out_shape"],
        scratch_shapes=[pltpu.VMEM((tm, N), F32)] + comm["sems"],
        compiler_params=_cp(("arbitrary", "arbitrary")))(*args, *comm["inputs"])


def _class_spec(S, d, tm, width):
    return pl.BlockSpec((d, tm // d, width), lambda i: (0, i, 0))


def _rms_fwd(x, g, *, name, dilations=(), comm=None):
    S, D = x.shape
    tm = min(512, S)
    ds = [d for d in dilations if d > 1]
    nsteps = S // tm
    n_in = len(comm["inputs"]) if comm else 0
    n_out = len(comm["out_shape"]) if comm else 0
    n_tmp = D // LANES if ds else 0

    def body(x_ref, g_ref, *rest):
        c_in, o_ref, rest = rest[:n_in], rest[n_in], rest[n_in + 1:]
        cls, c_out, rest = rest[:len(ds)], rest[len(ds):len(ds) + n_out], rest[len(ds) + n_out:]
        tmps, sems = rest[:n_tmp], rest[n_tmp:]
        if comm:
            @pl.when(pl.program_id(0) == 0)
            def _():
                comm["start"](*c_in, *c_out, *sems)

        xv = x_ref[...]
        r = lax.rsqrt(jnp.mean(xv * xv, axis=-1, keepdims=True) + EPS)
        hv = xv * r * g_ref[...]
        o_ref[...] = hv.astype(BF16)
        if ds:
            for c, tmp in enumerate(tmps):
                tmp[...] = hv[:, c * LANES:(c + 1) * LANES]
            for c_ref, d in zip(cls, ds):
                for k in range(d):
                    c_ref[k] = jnp.concatenate([tmp[pl.ds(k, tm // d, stride=d), :] for tmp in tmps],
                                               axis=1).astype(BF16)
        if comm:
            @pl.when(pl.program_id(0) == nsteps - 1)
            def _():
                comm["wait"](*c_in, *c_out, *sems)

    row = pl.BlockSpec((tm, D), lambda i: (i, 0))
    outs = pl.pallas_call(
        body, name=name, grid=(nsteps,),
        in_specs=[row, pl.BlockSpec((1, D), lambda i: (0, 0))] + [ANY] * n_in,
        out_specs=[row] + [_class_spec(S, d, tm, D) for d in ds] + [ANY] * n_out,
        out_shape=[jax.ShapeDtypeStruct((S, D), BF16)] + [jax.ShapeDtypeStruct((d, S // d, D), BF16) for d in ds]
        + (comm["out_shape"] if comm else []),
        scratch_shapes=[pltpu.VMEM((tm, LANES), F32)] * n_tmp + (comm["sems"] if comm else []),
        compiler_params=_cp(("arbitrary",) if comm else ("parallel",)),
    )(x, g, *(comm["inputs"] if comm else []))
    rows = [outs[0]] + [o.reshape(S, D) for o in outs[1:1 + len(ds)]]
    if comm:
        return rows, list(outs[1 + len(ds):])
    return rows if ds else rows[0]


def _rms_bwd(x, g, dh, dy, *, name, dh_classes=()):
    S, D = x.shape
    tm = min(512, S)
    want_dx = dy is not None
    nc = len(dh_classes)

    def body(*refs):
        c_refs, refs = refs[:nc], refs[nc:]
        if want_dx:
            x_ref, g_ref, dh_ref, dy_ref, dx_ref, dg_ref = refs[:6]
        else:
            x_ref, g_ref, dh_ref, dg_ref = refs[:4]
        i = pl.program_id(0)
        xv = x_ref[...]
        r = lax.rsqrt(jnp.mean(xv * xv, axis=-1, keepdims=True) + EPS)
        xh = xv * r
        if nc:
            tmps = refs[-(D // LANES):]
            cols = [slice(c * LANES, (c + 1) * LANES) for c in range(D // LANES)]
            for tmp, cs in zip(tmps, cols):
                tmp[...] = dh_ref[:, cs]
            for c_ref, (_, d) in zip(c_refs, dh_classes):
                for k in range(d):
                    for tmp, cs in zip(tmps, cols):
                        tmp[pl.ds(k, tm // d, stride=d), :] += c_ref[k, :, cs]
            dhv = jnp.concatenate([tmp[...] for tmp in tmps], axis=1)
        else:
            dhv = dh_ref[...]
        part = jnp.sum(dhv * xh, axis=0, keepdims=True)

        @pl.when(i == 0)
        def _():
            dg_ref[...] = part

        @pl.when(i > 0)
        def _():
            dg_ref[...] += part

        if want_dx:
            dxh = dhv * g_ref[...]
            dx_ref[...] = dy_ref[...] + r * (dxh - xh * jnp.mean(dxh * xh, axis=-1, keepdims=True))

    row = pl.BlockSpec((tm, D), lambda i: (i, 0))
    vec = pl.BlockSpec((1, D), lambda i: (0, 0))
    c_specs = [_class_spec(S, d, tm, D) for _, d in dh_classes]
    c_args = [a.reshape(d, S // d, D) for a, d in dh_classes]
    scratch = [pltpu.VMEM((tm, LANES), F32)] * (D // LANES) if nc else []
    if want_dx:
        return pl.pallas_call(
            body, name=name, grid=(S // tm,), in_specs=c_specs + [row, vec, row, row], out_specs=[row, vec],
            out_shape=[jax.ShapeDtypeStruct((S, D), F32), jax.ShapeDtypeStruct((1, D), F32)],
            scratch_shapes=scratch, compiler_params=_cp(("arbitrary",)))(*c_args, x, g, dh, dy)
    return pl.pallas_call(
        body, name=name, grid=(S // tm,), in_specs=c_specs + [row, vec, row], out_specs=vec,
        out_shape=jax.ShapeDtypeStruct((1, D), F32),
        scratch_shapes=scratch, compiler_params=_cp(("arbitrary",)))(*c_args, x, g, dh)


def _post(x, out, tgt, g, *, name):
    S, D = x.shape
    tm = min(512, S)

    def body(x_ref, o_ref, t_ref, g_ref, dy_ref, do_ref, dg_ref, loss_ref):
        i = pl.program_id(0)
        ov = o_ref[...]
        r = lax.rsqrt(jnp.mean(ov * ov, axis=-1, keepdims=True) + EPS)
        n = ov * r
        gv = g_ref[...]
        e = (x_ref[...] + n * gv) - t_ref[...]
        lpart = 0.5 * jnp.sum(jnp.mean(e * e, axis=-1, keepdims=True), axis=0, keepdims=True)
        dy = e * (1.0 / D)
        dy_ref[...] = dy
        dn = dy * gv
        do_ref[...] = (r * (dn - n * jnp.mean(dn * n, axis=-1, keepdims=True))).astype(BF16)
        gpart = jnp.sum(dy * n, axis=0, keepdims=True)
        lrow = jnp.broadcast_to(lpart, (1, LANES))

        @pl.when(i == 0)
        def _():
            dg_ref[...] = gpart
            loss_ref[...] = lrow

        @pl.when(i > 0)
        def _():
            dg_ref[...] += gpart
            loss_ref[...] += lrow

    row = pl.BlockSpec((tm, D), lambda i: (i, 0))
    vec = pl.BlockSpec((1, D), lambda i: (0, 0))
    return pl.pallas_call(
        body, name=name, grid=(S // tm,), in_specs=[row, row, row, vec],
        out_specs=[row, row, vec, pl.BlockSpec((1, LANES), lambda i: (0, 0))],
        out_shape=[jax.ShapeDtypeStruct((S, D), F32), jax.ShapeDtypeStruct((S, D), BF16),
                   jax.ShapeDtypeStruct((1, D), F32), jax.ShapeDtypeStruct((1, LANES), F32)],
        compiler_params=_cp(("arbitrary",)))(x, out, tgt, g)


def _to_classes(t, d):
    if d == 1:
        return t
    S, C = t.shape
    return t.reshape(S // d, d, C).transpose(1, 0, 2).reshape(S, C)


def _rope(x, c, s1, s2):
    return x * c + pltpu.roll(x, LANES - ROT // 2, 1) * s1 + pltpu.roll(x, ROT // 2, 1) * s2


def _unrope(d, c, s1, s2):
    return d * c + pltpu.roll(d * s1, ROT // 2, 1) + pltpu.roll(d * s2, LANES - ROT // 2, 1)


def _a_band(qb):
    r = lax.broadcasted_iota(jnp.int32, (qb, qb + HEAD), 0)
    c = lax.broadcasted_iota(jnp.int32, (qb, qb + HEAD), 1)
    return jnp.logical_and(c >= r, c <= r + HEAD)


def _a_first_ok(qb, n):
    c = lax.broadcasted_iota(jnp.int32, (qb, qb + HEAD), 1)
    return jnp.logical_or(c >= HEAD, n > 0)


def _a_last_ok(qb, has_next):
    c = lax.broadcasted_iota(jnp.int32, (qb, qb + HEAD), 1)
    return jnp.logical_or(c < qb, has_next)


A_SCALE = HEAD ** -0.5


def _a_geometry(S, g):
    d = DIL[g]
    L = S // d
    TQ = min(512, L)
    return d, L, TQ, TQ // HEAD, L // TQ, L // HEAD


def _proj_rope(h, w, tabs, *, name):
    S, D = h.shape
    tm = min(512, S)

    def body(h_ref, w_ref, c_ref, s1_ref, s2_ref, o_ref):
        tc = (c_ref[...], s1_ref[...], s2_ref[...])
        u = _dot_nt(h_ref[...], w_ref[...])
        for j in range(3 * A_HEADS):
            sl = slice(j * HEAD, (j + 1) * HEAD)
            o_ref[:, sl] = (_rope(u[:, sl], *tc) if j < 2 * A_HEADS else u[:, sl]).astype(BF16)

    tab = pl.BlockSpec((tm, LANES), lambda i: (i, 0))
    return pl.pallas_call(
        body, name=name, grid=(S // tm,),
        in_specs=[pl.BlockSpec((tm, D), lambda i: (i, 0)), pl.BlockSpec((3 * A_WIDTH, D), lambda i: (0, 0)),
                  tab, tab, tab],
        out_specs=pl.BlockSpec((tm, 3 * A_WIDTH), lambda i: (i, 0)),
        out_shape=jax.ShapeDtypeStruct((S, 3 * A_WIDTH), BF16),
        compiler_params=_cp(("parallel",)))(h, w, *tabs)


def _attn_a_fwd(qkv, g, *, name):
    S = qkv.shape[0]
    d, L, TQ, nsub, nb, nblk = _a_geometry(S, g)

    def body(q_ref, kc_ref, kp_ref, vc_ref, vp_ref, o_ref, l_ref):
        n = pl.program_id(1)
        QB = min(2 * HEAD, TQ)
        band = _a_band(QB)
        first = jnp.logical_and(band, _a_first_ok(QB, n))
        for h in range(A_HEADS):
            hs = slice(h * HEAD, (h + 1) * HEAD)
            for hh in range(TQ // QB):
                sl = slice(hh * QB, (hh + 1) * QB)
                pv = slice(hh * QB - HEAD, hh * QB)
                kcat = jnp.concatenate([kp_ref[:, hs] if hh == 0 else kc_ref[pv, hs], kc_ref[sl, hs]], axis=0)
                vcat = jnp.concatenate([vp_ref[:, hs] if hh == 0 else vc_ref[pv, hs], vc_ref[sl, hs]], axis=0)
                s = jnp.where(first if hh == 0 else band, _dot_nt(q_ref[sl, hs], kcat) * A_SCALE, NEG)
                m = jnp.max(s, axis=-1, keepdims=True)
                p = jnp.exp(s - m)
                den = jnp.sum(p, axis=-1, keepdims=True)
                o_ref[sl, hs] = _dot(p.astype(BF16), vcat) / den
                l_ref[sl, hs] = jnp.broadcast_to(m + jnp.log(den), (QB, HEAD))

    rcur = lambda r, n: r * nb + n
    rprv = lambda r, n: r * nblk + jnp.maximum(n * nsub - 1, 0)
    cur = lambda off: pl.BlockSpec((TQ, A_WIDTH), lambda r, n: (rcur(r, n), off))
    prv = lambda off: pl.BlockSpec((HEAD, A_WIDTH), lambda r, n: (rprv(r, n), off))
    out = pl.BlockSpec((TQ, A_WIDTH), lambda r, n: (rcur(r, n), 0))
    return pl.pallas_call(
        body, name=name, grid=(d, nb),
        in_specs=[cur(0), cur(1), prv(1), cur(2), prv(2)],
        out_specs=[out, out],
        out_shape=[jax.ShapeDtypeStruct((S, A_WIDTH), F32)] * 2,
        compiler_params=_cp(("parallel", "parallel")),
    )(qkv, qkv, qkv, qkv, qkv)


def _attn_a_dq(qkv, tabs, g, do, lse, adj, *, name):
    S = qkv.shape[0]
    d, L, TQ, nsub, nb, nblk = _a_geometry(S, g)

    def body(q_ref, kc_ref, kp_ref, vc_ref, vp_ref, do_ref, l_ref, adj_ref, c_ref, s1_ref, s2_ref, dq_ref):
        n = pl.program_id(1)
        QB = min(2 * HEAD, TQ)
        band = _a_band(QB)
        first = jnp.logical_and(band, _a_first_ok(QB, n))
        for h in range(A_HEADS):
            hs = slice(h * HEAD, (h + 1) * HEAD)
            for hh in range(TQ // QB):
                sl = slice(hh * QB, (hh + 1) * QB)
                pv = slice(hh * QB - HEAD, hh * QB)
                kcat = jnp.concatenate([kp_ref[:, hs] if hh == 0 else kc_ref[pv, hs], kc_ref[sl, hs]], axis=0)
                vcat = jnp.concatenate([vp_ref[:, hs] if hh == 0 else vc_ref[pv, hs], vc_ref[sl, hs]], axis=0)
                s = jnp.where(first if hh == 0 else band, _dot_nt(q_ref[sl, hs], kcat) * A_SCALE, NEG)
                p = jnp.exp(s - l_ref[sl, hs][:, :1])
                ds = p * (_dot_nt(do_ref[sl, hs], vcat) + adj_ref[sl, hs][:, :1])
                dq = _dot(ds.astype(BF16), kcat) * A_SCALE
                dq_ref[sl, hs] = _unrope(dq, c_ref[sl, :], s1_ref[sl, :], s2_ref[sl, :]).astype(BF16)

    rcur = lambda r, n: r * nb + n
    rprv = lambda r, n: r * nblk + jnp.maximum(n * nsub - 1, 0)
    cur = lambda off: pl.BlockSpec((TQ, A_WIDTH), lambda r, n: (rcur(r, n), off))
    prv = lambda off: pl.BlockSpec((HEAD, A_WIDTH), lambda r, n: (rprv(r, n), off))
    tcur = pl.BlockSpec((TQ, LANES), lambda r, n: (rcur(r, n), 0))
    blk = cur(0)
    return pl.pallas_call(
        body, name=name, grid=(d, nb),
        in_specs=[cur(0), cur(1), prv(1), cur(2), prv(2), blk, blk, blk, tcur, tcur, tcur],
        out_specs=blk,
        out_shape=jax.ShapeDtypeStruct((S, A_WIDTH), BF16),
        compiler_params=_cp(("parallel", "parallel")),
    )(qkv, qkv, qkv, qkv, qkv, do, lse, adj, *tabs)


def _attn_a_dkv(qkv, tabs, g, do, lse, adj, *, name):
    S = qkv.shape[0]
    d, L, TQ, nsub, nb, nblk = _a_geometry(S, g)

    def body(qc_ref, qn_ref, kc_ref, vc_ref, doc_ref, don_ref, lc_ref, ln_ref, ac_ref, an_ref,
             c_ref, s1_ref, s2_ref, dk_ref, dv_ref):
        n = pl.program_id(1)
        QB = min(2 * HEAD, TQ)
        nh = TQ // QB
        band = _a_band(QB)
        end = jnp.logical_and(band, _a_last_ok(QB, n < nb - 1))
        for h in range(A_HEADS):
            hs = slice(h * HEAD, (h + 1) * HEAD)
            for kh in range(nh):
                sl = slice(kh * QB, (kh + 1) * QB)
                nx = slice((kh + 1) * QB, (kh + 1) * QB + HEAD)
                last = kh == nh - 1
                cat = lambda cur, nxt: jnp.concatenate([cur[sl, hs], nxt[:, hs] if last else cur[nx, hs]], axis=0)
                qcat = cat(qc_ref, qn_ref)
                docat = cat(doc_ref, don_ref)
                lt = cat(lc_ref, ln_ref).T[:1, :]
                at = cat(ac_ref, an_ref).T[:1, :]
                st = jnp.where(end if last else band, _dot_nt(kc_ref[sl, hs], qcat) * A_SCALE, NEG)
                pt = jnp.exp(st - lt)
                dv_ref[sl, hs] = _dot(pt.astype(BF16), docat).astype(BF16)
                dst = pt * (_dot_nt(vc_ref[sl, hs], docat) + at)
                dk = _dot(dst.astype(BF16), qcat) * A_SCALE
                dk_ref[sl, hs] = _unrope(dk, c_ref[sl, :], s1_ref[sl, :], s2_ref[sl, :]).astype(BF16)

    rcur = lambda r, n: r * nb + n
    rnxt = lambda r, n: r * nblk + jnp.minimum((n + 1) * nsub, nblk - 1)
    cur = lambda off: pl.BlockSpec((TQ, A_WIDTH), lambda r, n: (rcur(r, n), off))
    nxu = lambda off: pl.BlockSpec((HEAD, A_WIDTH), lambda r, n: (rnxt(r, n), off))
    tcur = pl.BlockSpec((TQ, LANES), lambda r, n: (rcur(r, n), 0))
    blk, bnx = cur(0), nxu(0)
    return pl.pallas_call(
        body, name=name, grid=(d, nb),
        in_specs=[cur(0), nxu(0), cur(1), cur(2), blk, bnx, blk, bnx, blk, bnx, tcur, tcur, tcur],
        out_specs=[blk, blk],
        out_shape=[jax.ShapeDtypeStruct((S, A_WIDTH), BF16)] * 2,
        compiler_params=_cp(("parallel", "parallel")),
    )(qkv, qkv, qkv, qkv, do, do, lse, lse, adj, adj, *tabs)


def _silu_parts(z):
    sg = _sigmoid(z)
    return z * sg, sg * (1.0 + z * (1.0 - sg))


def _classes_to_tokens(c_ref, d, tm, tmps):
    if d == 1:
        return c_ref[...].astype(F32)
    for k in range(d):
        for c, tmp in enumerate(tmps):
            tmp[pl.ds(k, tm // d, stride=d), :] = c_ref[k, :, c * LANES:(c + 1) * LANES].astype(F32)
    return jnp.concatenate([tmp[...] for tmp in tmps], axis=1)


def _tokens_to_classes(val, c_ref, d, tm, tmps):
    if d == 1:
        c_ref[...] = val.astype(c_ref.dtype)
        return
    for c, tmp in enumerate(tmps):
        tmp[...] = val[:, c * LANES:(c + 1) * LANES]
    for k in range(d):
        c_ref[k] = jnp.concatenate([tmp[pl.ds(k, tm // d, stride=d), :] for tmp in tmps], axis=1).astype(c_ref.dtype)


def _group_spec(S, d, tm):
    if d == 1:
        return pl.BlockSpec((tm, A_WIDTH), lambda i: (i, 0))
    return _class_spec(S, d, tm, A_WIDTH)


def _group_view(t, d):
    return t if d == 1 else t.reshape(d, t.shape[0] // d, t.shape[1])


def _merge_a_fwd(os_, ls_, ur, *, name):
    S = ur.shape[0]
    tm = min(512, S)

    def body(o0, o1, o2, l0, l1, l2, z_ref, y_ref, *tmps):
        ls = [_classes_to_tokens(r, d, tm, tmps) for r, d in zip((l0, l1, l2), DIL)]
        ov = [_classes_to_tokens(r, d, tm, tmps) for r, d in zip((o0, o1, o2), DIL)]
        mx = jnp.maximum(jnp.maximum(ls[0], ls[1]), ls[2])
        es = [jnp.exp(l - mx) for l in ls]
        den = es[0] + es[1] + es[2]
        y = (es[0] / den) * ov[0] + (es[1] / den) * ov[1] + (es[2] / den) * ov[2]
        y_ref[...] = (y * _silu_parts(z_ref[...])[0]).astype(BF16)

    blk = pl.BlockSpec((tm, A_WIDTH), lambda i: (i, 0))
    groups = [_group_spec(S, d, tm) for d in DIL]
    return pl.pallas_call(
        body, name=name, grid=(S // tm,),
        in_specs=groups + groups + [pl.BlockSpec((tm, A_WIDTH), lambda i: (i, R_ZA // A_WIDTH))],
        out_specs=blk, out_shape=jax.ShapeDtypeStruct((S, A_WIDTH), BF16),
        scratch_shapes=[pltpu.VMEM((tm, LANES), F32)] * (A_WIDTH // LANES),
        compiler_params=_cp(("parallel",)))(*[_group_view(t, d) for t, d in zip(os_, DIL)],
                                            *[_group_view(t, d) for t, d in zip(ls_, DIL)], ur)


def _merge_a_bwd(os_, ls_, ur, dya, *, name):
    S = ur.shape[0]
    tm = min(256, S)

    def body(o0, o1, o2, l0, l1, l2, z_ref, dy_ref, d0, d1, d2, a0, a1, a2, dz_ref, *tmps):
        ls = [_classes_to_tokens(r, d, tm, tmps) for r, d in zip((l0, l1, l2), DIL)]
        ov = [_classes_to_tokens(r, d, tm, tmps) for r, d in zip((o0, o1, o2), DIL)]
        mx = jnp.maximum(jnp.maximum(ls[0], ls[1]), ls[2])
        es = [jnp.exp(l - mx) for l in ls]
        den = es[0] + es[1] + es[2]
        ws = [e / den for e in es]
        y = ws[0] * ov[0] + ws[1] * ov[1] + ws[2] * ov[2]
        sz, dsz = _silu_parts(z_ref[...])
        dyv = dy_ref[...]
        dz_ref[...] = (dyv * y * dsz).astype(BF16)
        dyp = dyv * sz
        ts = []
        for h in range(A_HEADS):
            sl = slice(h * HEAD, (h + 1) * HEAD)
            t = jnp.zeros((tm, 1), F32)
            for gi in range(3):
                t = t + ws[gi][:, sl][:, :1] * jnp.sum(dyp[:, sl] * ov[gi][:, sl], axis=-1, keepdims=True)
            ts.append(jnp.broadcast_to(t, (tm, HEAD)))
        tb = jnp.concatenate(ts, axis=1)
        for gi, (dref, aref) in enumerate(((d0, a0), (d1, a1), (d2, a2))):
            _tokens_to_classes(ws[gi] * dyp, dref, DIL[gi], tm, tmps)
            _tokens_to_classes(-ws[gi] * tb, aref, DIL[gi], tm, tmps)

    blk = pl.BlockSpec((tm, A_WIDTH), lambda i: (i, 0))
    groups = [_group_spec(S, d, tm) for d in DIL]
    shaped = lambda dt: [jax.ShapeDtypeStruct((S, A_WIDTH) if d == 1 else (d, S // d, A_WIDTH), dt) for d in DIL]
    outs = pl.pallas_call(
        body, name=name, grid=(S // tm,),
        in_specs=groups + groups + [pl.BlockSpec((tm, A_WIDTH), lambda i: (i, R_ZA // A_WIDTH)), blk],
        out_specs=groups + groups + [blk],
        out_shape=shaped(BF16) + shaped(F32) + [jax.ShapeDtypeStruct((S, A_WIDTH), BF16)],
        scratch_shapes=[pltpu.VMEM((tm, LANES), F32)] * (A_WIDTH // LANES),
        compiler_params=_cp(("parallel",)))(*[_group_view(t, d) for t, d in zip(os_, DIL)],
                                            *[_group_view(t, d) for t, d in zip(ls_, DIL)], ur, dya)
    flat = [t.reshape(S, A_WIDTH) for t in outs[:6]]
    return flat[0:3], flat[3:6], outs[6]


def _logf(ur, bf_pad, *, name):
    S = ur.shape[0]
    tm = min(1024, S)

    def body(u_ref, b_ref, o_ref):
        z = u_ref[...] + b_ref[...]
        o_ref[...] = jnp.minimum(z, 0.0) - jnp.log(1.0 + jnp.exp(-jnp.abs(z)))

    return pl.pallas_call(
        body, name=name, grid=(S // tm,),
        in_specs=[pl.BlockSpec((tm, FB_PAD), lambda i: (i, R_FB // FB_PAD)),
                  pl.BlockSpec((1, FB_PAD), lambda i: (0, 0))],
        out_specs=pl.BlockSpec((tm, FB_PAD), lambda i: (i, 0)),
        out_shape=jax.ShapeDtypeStruct((S, FB_PAD), F32),
        compiler_params=_cp(("parallel",)))(ur, bf_pad)


def _cumsum_lanes(x, reverse, *, name):
    nt, H, _ = x.shape
    R = nt * H

    def body(x_ref, o_ref):
        v = x_ref[...].reshape(R, LANES)
        lane = lax.broadcasted_iota(jnp.int32, (R, LANES), 1)
        row = lax.broadcasted_iota(jnp.int32, (R, LANES), 0)

        def scan(t, step, idx, n, axis):
            while step < n:
                if reverse:
                    t = t + jnp.where(idx < n - step, pltpu.roll(t, n - step, axis), 0.0)
                else:
                    t = t + jnp.where(idx >= step, pltpu.roll(t, step, axis), 0.0)
                step *= 2
            return t

        v = scan(v, 1, lane, LANES, 1)
        total = jnp.broadcast_to(v[:, :1] if reverse else v[:, LANES - 1:], (R, LANES))
        carry = scan(total, H, row, R, 0) - total
        o_ref[...] = (v + carry).reshape(nt, H, LANES)

    return pl.pallas_call(
        body, name=name, out_shape=jax.ShapeDtypeStruct((nt, H, LANES), F32),
        in_specs=[pl.BlockSpec(memory_space=pltpu.VMEM)], out_specs=pl.BlockSpec(memory_space=pltpu.VMEM),
        compiler_params=_cp())(x)


B_SCALE = B_HEAD ** -0.5


def _pair_masks():
    lane = lax.broadcasted_iota(jnp.int32, (1, LANES), 1)
    row = lax.broadcasted_iota(jnp.int32, (LANES, 1), 0)
    return (lane < B_HEAD, lane >= B_HEAD), (row < B_HEAD, row >= B_HEAD)


def _causal_t(T):
    r = lax.broadcasted_iota(jnp.int32, (T, T), 0)
    c = lax.broadcasted_iota(jnp.int32, (T, T), 1)
    return r <= c


def _zero_other(x, keep):
    return jnp.where(keep, x, jnp.zeros_like(x))


def _fox_aug(ub, c, *, name):
    S = ub.shape[0]
    T = min(2048, S)

    def body(q_ref, k_ref, c_ref, qa_ref, ka_ref):
        lane = lax.broadcasted_iota(jnp.int32, (1, LANES), 1)
        q = q_ref[...] * B_SCALE
        k = k_ref[...]
        for a in range(2):
            own = (lane < B_HEAD) if a == 0 else (lane >= B_HEAD)
            o = B_HEAD if a == 0 else 0
            cv = jnp.broadcast_to(c_ref[:, a:a + 1], (T, LANES))
            hi = cv.astype(BF16)
            r1 = cv - hi.astype(F32)
            mid = r1.astype(BF16)
            lo = (r1 - mid.astype(F32)).astype(BF16)
            pieces = (hi, mid, lo)
            one = jnp.ones((T, LANES), BF16)
            qa = jnp.where(own, q, jnp.zeros_like(q))
            ka = jnp.where(own, k, jnp.zeros_like(k))
            for t in range(3):
                qa = jnp.where(lane == o + t, pieces[t], qa)
                qa = jnp.where(lane == o + 3 + t, one, qa)
                ka = jnp.where(lane == o + t, one, ka)
                ka = jnp.where(lane == o + 3 + t, -pieces[t], ka)
            qa_ref[a] = qa
            ka_ref[a] = ka

    out = pl.BlockSpec((2, T, LANES), lambda h, i: (h, i, 0))
    c_pairs = c.reshape(B_HEADS // 2, 2, S).transpose(0, 2, 1)
    return pl.pallas_call(
        body, name=name, grid=(B_HEADS // 2, S // T),
        in_specs=[pl.BlockSpec((T, LANES), lambda h, i: (i, h)), pl.BlockSpec((T, LANES), lambda h, i: (i, 4 + h)),
                  pl.BlockSpec((None, T, 2), lambda h, i: (h, i, 0))],
        out_specs=[out, out], out_shape=[jax.ShapeDtypeStruct((B_HEADS, S, LANES), BF16)] * 2,
        compiler_params=_cp(("parallel", "parallel")))(ub, ub, c_pairs)


def _fox_fwd(qaug, kaug, vt, *, name):
    S = qaug.shape[1]
    T = min(512, S)
    nq = S // T

    def body(q_ref, k_ref, vt_ref, o_ref, l_ref, m_s, l_s, acc_s, st_s):
        i = pl.program_id(1)
        _, rows = _pair_masks()
        qm = [q_ref[0], q_ref[1]]
        m_s[...] = jnp.full((2, 1, T), NEG, F32)
        l_s[...] = jnp.zeros((2, 1, T), F32)
        acc_s[...] = jnp.zeros((LANES, T), F32)

        def logits(j):
            off = pl.multiple_of(j * T, T)
            return [_dot_nt(k_ref[a, pl.ds(off, T), :], qm[a]) for a in range(2)]

        def step(j, masked, prefetch):
            nxt = logits(j + 1) if prefetch else None
            vtj = vt_ref[j]
            upd = jnp.zeros((LANES, T), F32)
            alphas = []
            for a in range(2):
                st = st_s[a]
                if masked:
                    st = jnp.where(_causal_t(T), st, NEG)
                m_old = m_s[a]
                m_new = jnp.maximum(m_old, jnp.max(st, axis=0, keepdims=True))
                alpha = jnp.exp(m_old - m_new)
                pt = jnp.exp(st - m_new)
                l_s[a] = alpha * l_s[a] + jnp.sum(pt, axis=0, keepdims=True)
                m_s[a] = m_new
                upd = upd + _dot(_zero_other(vtj, rows[a]), pt.astype(BF16))
                alphas.append(alpha)
            acc_s[...] = acc_s[...] * jnp.where(rows[0], alphas[0], alphas[1]) + upd
            if prefetch:
                st_s[0] = nxt[0]
                st_s[1] = nxt[1]

        def loop(j, carry):
            step(j, False, True)
            return carry

        first = logits(0)
        st_s[0] = first[0]
        st_s[1] = first[1]
        lax.fori_loop(0, i, loop, 0)
        step(i, True, False)
        o_ref[...] = (acc_s[...] / jnp.where(rows[0], l_s[0], l_s[1])).T
        l_ref[0] = m_s[0] + jnp.log(l_s[0])
        l_ref[1] = m_s[1] + jnp.log(l_s[1])

    stat = pl.BlockSpec((2, None, 1, T), lambda h, i: (h, i, 0, 0))
    return pl.pallas_call(
        body, name=name, grid=(B_HEADS // 2, nq),
        in_specs=[pl.BlockSpec((2, T, LANES), lambda h, i: (h, i, 0)),
                  pl.BlockSpec((2, S, LANES), lambda h, i: (h, 0, 0)),
                  pl.BlockSpec((nq, LANES, T), lambda h, i: (0, h, 0))],
        out_specs=[pl.BlockSpec((T, LANES), lambda h, i: (i, h)), stat],
        out_shape=[jax.ShapeDtypeStruct((S, A_WIDTH), F32), jax.ShapeDtypeStruct((B_HEADS, nq, 1, T), F32)],
        scratch_shapes=[pltpu.VMEM((2, 1, T), F32), pltpu.VMEM((2, 1, T), F32), pltpu.VMEM((LANES, T), F32),
                        pltpu.VMEM((2, T, T), F32)],
        compiler_params=_cp(("parallel", "parallel")),
    )(qaug, kaug, vt)


def _fox_delta(o, do, *, name):
    S = o.shape[0]
    T = min(512, S)
    nq = S // T

    per = min(4, nq)

    def body(o_ref, do_ref, d_ref):
        _, rows = _pair_masks()
        for t in range(per):
            sl = slice(t * T, (t + 1) * T)
            prod_t = (do_ref[sl, :].astype(F32) * o_ref[sl, :]).T
            d_ref[0, t] = jnp.sum(_zero_other(prod_t, rows[0]), axis=0, keepdims=True)
            d_ref[1, t] = jnp.sum(_zero_other(prod_t, rows[1]), axis=0, keepdims=True)

    tile = pl.BlockSpec((per * T, LANES), lambda h, i: (i, h))
    return pl.pallas_call(
        body, name=name, grid=(B_HEADS // 2, nq // per), in_specs=[tile, tile],
        out_specs=pl.BlockSpec((2, per, 1, T), lambda h, i: (h, i, 0, 0)),
        out_shape=jax.ShapeDtypeStruct((B_HEADS, nq, 1, T), F32),
        compiler_params=_cp(("parallel", "parallel")))(o, do)


def _fox_bwd(ub, qaug, kaug, kt, do, lse, delta, *, name):
    S = ub.shape[0]
    T = min(512, S)
    nq = S // T

    def body(k_ref, v_ref, kt_ref, q_ref, do_ref, l_ref, dl_ref,
             dk_ref, dv_ref, dck_ref, dqt_ref, dcq_ref, dk_s, dv_s, dc_s):
        j = pl.program_id(1)
        lanes, rows = _pair_masks()
        vv = v_ref[...]
        ktj = kt_ref[...]
        km = [k_ref[0], k_ref[1]]
        ktm = [_zero_other(ktj, rows[0]), _zero_other(ktj, rows[1])]
        dk_s[...] = jnp.zeros((2, T, LANES), F32)
        dv_s[...] = jnp.zeros((T, LANES), F32)
        dc_s[...] = jnp.zeros((2, T, 1), F32)

        @pl.when(j == 0)
        def _():
            dqt_ref[...] = jnp.zeros((nq, LANES, T), F32)
            dcq_ref[...] = jnp.zeros((2, nq, 1, T), F32)

        def step(i, masked):
            off = pl.multiple_of(i * T, T)
            doi = do_ref[pl.ds(off, T), :]
            upd = jnp.zeros((LANES, T), F32)
            for a in range(2):
                qi = q_ref[a, pl.ds(off, T), :]
                st = _dot_nt(km[a], qi)
                if masked:
                    st = jnp.where(_causal_t(T), st, NEG)
                pt = jnp.exp(st - l_ref[a, i])
                doa = _zero_other(doi, lanes[a])
                dv_s[...] += _dot(pt.astype(BF16), doa)
                dst = pt * (_dot_nt(vv, doa) - dl_ref[a, i])
                dsb = dst.astype(BF16)
                dk_s[a] += _dot(dsb, qi)
                upd = upd + _dot(ktm[a], dsb)
                dc_s[a] -= jnp.sum(dst, axis=-1, keepdims=True)
                dcq_ref[a, i] += jnp.sum(dst, axis=0, keepdims=True)
            dqt_ref[i] += upd

        def loop(i, carry):
            step(i, False)
            return carry

        step(j, True)
        lax.fori_loop(j + 1, nq, loop, 0)
        dk_ref[...] = jnp.where(lanes[0], dk_s[0], dk_s[1]).astype(BF16)
        dv_ref[...] = dv_s[...].astype(BF16)
        dck_ref[...] = dc_s[...]

    rowv = pl.BlockSpec((2, nq, 1, T), lambda h, j: (h, 0, 0, 0))
    tile = pl.BlockSpec((T, LANES), lambda h, j: (j, h))
    return pl.pallas_call(
        body, name=name, grid=(B_HEADS // 2, nq),
        in_specs=[pl.BlockSpec((2, T, LANES), lambda h, j: (h, j, 0)),
                  pl.BlockSpec((T, LANES), lambda h, j: (j, 8 + h)),
                  pl.BlockSpec((None, LANES, T), lambda h, j: (j, h, 0)),
                  pl.BlockSpec((2, S, LANES), lambda h, j: (h, 0, 0)),
                  pl.BlockSpec((S, LANES), lambda h, j: (0, h)),
                  rowv, rowv],
        out_specs=[tile, tile, pl.BlockSpec((2, T, 1), lambda h, j: (h, j, 0)),
                   pl.BlockSpec((nq, LANES, T), lambda h, j: (0, h, 0)), rowv],
        out_shape=[jax.ShapeDtypeStruct((S, A_WIDTH), BF16)] * 2 + [jax.ShapeDtypeStruct((B_HEADS, S, 1), F32),
                   jax.ShapeDtypeStruct((nq, A_WIDTH, T), F32), jax.ShapeDtypeStruct((B_HEADS, nq, 1, T), F32)],
        scratch_shapes=[pltpu.VMEM((2, T, LANES), F32), pltpu.VMEM((T, LANES), F32), pltpu.VMEM((2, T, 1), F32)],
        compiler_params=_cp(("parallel", "arbitrary")),
    )(kaug, ub, kt, qaug, do, lse, delta)


def _gate_fwd(o, ur, zcol, *, name):
    S = ur.shape[0]
    tm = min(1024, S)

    def body(o_ref, z_ref, y_ref):
        y_ref[...] = (o_ref[...] * _silu_parts(z_ref[...])[0]).astype(BF16)

    blk = pl.BlockSpec((tm, A_WIDTH), lambda i: (i, 0))
    return pl.pallas_call(
        body, name=name, grid=(S // tm,),
        in_specs=[blk, pl.BlockSpec((tm, A_WIDTH), lambda i: (i, zcol // A_WIDTH))],
        out_specs=blk, out_shape=jax.ShapeDtypeStruct((S, A_WIDTH), BF16),
        compiler_params=_cp(("parallel",)))(o, ur)


def _gate_bwd(o, ur, zcol, dy, *, name):
    S = ur.shape[0]
    tm = min(1024, S)

    def body(o_ref, z_ref, dy_ref, do_ref, dz_ref):
        sz, dsz = _silu_parts(z_ref[...])
        dyv = dy_ref[...]
        do_ref[...] = (dyv * sz).astype(BF16)
        dz_ref[...] = (dyv * o_ref[...] * dsz).astype(BF16)

    blk = pl.BlockSpec((tm, A_WIDTH), lambda i: (i, 0))
    return pl.pallas_call(
        body, name=name, grid=(S // tm,),
        in_specs=[blk, pl.BlockSpec((tm, A_WIDTH), lambda i: (i, zcol // A_WIDTH)), blk],
        out_specs=[blk, blk], out_shape=[jax.ShapeDtypeStruct((S, A_WIDTH), BF16)] * 2,
        compiler_params=_cp(("parallel",)))(o, ur, dy)


def _dfb(ur, bf_pad, dlogf_pad, *, name):
    S = ur.shape[0]
    tm = min(1024, S)

    def body(u_ref, b_ref, d_ref, o_ref, s_ref):
        i = pl.program_id(0)
        dv = d_ref[...] * _sigmoid(-(u_ref[...] + b_ref[...]))
        o_ref[...] = dv.astype(BF16)
        part = jnp.sum(dv, axis=0, keepdims=True)

        @pl.when(i == 0)
        def _():
            s_ref[...] = part

        @pl.when(i > 0)
        def _():
            s_ref[...] += part

    vec = pl.BlockSpec((1, FB_PAD), lambda i: (0, 0))
    blk = pl.BlockSpec((tm, FB_PAD), lambda i: (i, 0))
    return pl.pallas_call(
        body, name=name, grid=(S // tm,),
        in_specs=[pl.BlockSpec((tm, FB_PAD), lambda i: (i, R_FB // FB_PAD)), vec, blk],
        out_specs=[blk, vec],
        out_shape=[jax.ShapeDtypeStruct((S, FB_PAD), BF16), jax.ShapeDtypeStruct((1, FB_PAD), F32)],
        compiler_params=_cp(("arbitrary",)))(ur, bf_pad, dlogf_pad)


M_SCALE = HEAD ** -0.5


def _mem_fwd(ur, mkv, *, name):
    S = ur.shape[0]
    T = min(512, S)

    def body(q_ref, z_ref, k_ref, v_ref, y_ref):
        for h in range(M_HEADS):
            hs = slice(h * HEAD, (h + 1) * HEAD)
            s = _dot_nt(q_ref[:, hs].astype(BF16), k_ref[:, hs].astype(BF16)) * M_SCALE
            p = jnp.exp(s - jnp.max(s, axis=-1, keepdims=True))
            p = p / jnp.sum(p, axis=-1, keepdims=True)
            o = _dot(p.astype(BF16), v_ref[:, hs].astype(BF16))
            y_ref[:, hs] = (o * _silu_parts(z_ref[:, hs])[0]).astype(BF16)

    wide = lambda col: pl.BlockSpec((T, A_WIDTH), lambda i: (i, col // A_WIDTH))
    kv = lambda half: pl.BlockSpec((N_MEM, A_WIDTH), lambda i: (0, half))
    return pl.pallas_call(
        body, name=name, grid=(S // T,),
        in_specs=[wide(R_QM), wide(R_ZM), kv(0), kv(1)],
        out_specs=pl.BlockSpec((T, A_WIDTH), lambda i: (i, 0)),
        out_shape=jax.ShapeDtypeStruct((S, A_WIDTH), BF16),
        compiler_params=_cp(("parallel",)))(ur, ur, mkv, mkv)


def _mem_bwd(ur, mkv, dy, *, name):
    S = ur.shape[0]
    T = min(512, S)

    def body(q_ref, z_ref, k_ref, v_ref, dy_ref, dq_ref, dz_ref, dk_ref, dv_ref):
        i = pl.program_id(0)

        @pl.when(i == 0)
        def _():
            dk_ref[...] = jnp.zeros((N_MEM, A_WIDTH), F32)
            dv_ref[...] = jnp.zeros((N_MEM, A_WIDTH), F32)

        for h in range(M_HEADS):
            hs = slice(h * HEAD, (h + 1) * HEAD)
            qv = q_ref[:, hs].astype(BF16)
            kv = k_ref[:, hs].astype(BF16)
            vv = v_ref[:, hs].astype(BF16)
            s = _dot_nt(qv, kv) * M_SCALE
            p = jnp.exp(s - jnp.max(s, axis=-1, keepdims=True))
            p = p / jnp.sum(p, axis=-1, keepdims=True)
            o = _dot(p.astype(BF16), vv)
            sz, dsz = _silu_parts(z_ref[:, hs])
            dyv = dy_ref[:, hs]
            dz_ref[:, hs] = (dyv * o * dsz).astype(BF16)
            dov = (dyv * sz).astype(BF16)
            dp = _dot_nt(dov, vv)
            ds = p * (dp - jnp.sum(p * dp, axis=-1, keepdims=True))
            dq_ref[:, hs] = (_dot(ds.astype(BF16), kv) * M_SCALE).astype(BF16)
            dv_ref[:, hs] += _dot(p.T.astype(BF16), dov)
            dk_ref[:, hs] += _dot(ds.T.astype(BF16), qv) * M_SCALE

    wide = lambda col: pl.BlockSpec((T, A_WIDTH), lambda i: (i, col // A_WIDTH))
    kv = lambda half: pl.BlockSpec((N_MEM, A_WIDTH), lambda i: (0, half))
    tile = pl.BlockSpec((T, A_WIDTH), lambda i: (i, 0))
    acc = pl.BlockSpec((N_MEM, A_WIDTH), lambda i: (0, 0))
    return pl.pallas_call(
        body, name=name, grid=(S // T,),
        in_specs=[wide(R_QM), wide(R_ZM), kv(0), kv(1), tile],
        out_specs=[tile, tile, acc, acc],
        out_shape=[jax.ShapeDtypeStruct((S, A_WIDTH), BF16)] * 2
        + [jax.ShapeDtypeStruct((N_MEM, A_WIDTH), F32)] * 2,
        compiler_params=_cp(("arbitrary",)))(ur, ur, mkv, mkv, dy)


def _branch_fwd(ys, wbs, ur, b_merge, *, name):
    S = ur.shape[0]
    tm, tn = min(512, S), 512
    nj = D_MODEL // tn

    def body(ya, yb, ym, wa, wb, wm, g0, g1, g2, b0, b1, b2, mg_ref, p_ref):
        acc = jnp.zeros((tm, tn), F32)
        for i, (y, w, gr, br) in enumerate(((ya, wa, g0, b0), (yb, wb, g1, b1), (ym, wm, g2, b2))):
            pr = _dot(y[...], w[...])
            p_ref[i] = pr.astype(BF16)
            acc = acc + _sigmoid(gr[...] + br[...]) * pr
        mg_ref[...] = acc.astype(BF16)

    yspec = pl.BlockSpec((tm, A_WIDTH), lambda i, j: (i, 0))
    wspec = pl.BlockSpec((A_WIDTH, tn), lambda i, j: (0, j))
    gspec = lambda b: pl.BlockSpec((tm, tn), lambda i, j: (i, (R_GL + b * D_MODEL) // tn + j))
    bspec = lambda b: pl.BlockSpec((1, tn), lambda i, j: (0, b * nj + j))
    return pl.pallas_call(
        body, name=name, grid=(S // tm, nj),
        in_specs=[yspec] * 3 + [wspec] * 3 + [gspec(0), gspec(1), gspec(2), bspec(0), bspec(1), bspec(2)],
        out_specs=[pl.BlockSpec((tm, tn), lambda i, j: (i, j)),
                   pl.BlockSpec((3, tm, tn), lambda i, j: (0, i, j))],
        out_shape=[jax.ShapeDtypeStruct((S, D_MODEL), BF16), jax.ShapeDtypeStruct((3, S, D_MODEL), BF16)],
        compiler_params=_cp(("parallel", "parallel")))(*ys, *wbs, ur, ur, ur, b_merge, b_merge, b_merge)


def _branch_bwd(dm, prods, ur, b_merge, *, name):
    S = ur.shape[0]
    tm = min(256, S)

    def body(dm_ref, p_ref, g0, g1, g2, b_ref, dp0, dp1, dp2, dgl_ref, db_ref):
        i = pl.program_id(0)
        dmv = dm_ref[...]
        parts = []
        for b, (gr, dp_ref) in enumerate(((g0, dp0), (g1, dp1), (g2, dp2))):
            sl = slice(b * D_MODEL, (b + 1) * D_MODEL)
            gt = _sigmoid(gr[...] + b_ref[:, sl])
            dp_ref[...] = (dmv * gt).astype(BF16)
            dgl = dmv * p_ref[b].astype(F32) * gt * (1.0 - gt)
            dgl_ref[:, sl] = dgl.astype(BF16)
            parts.append(jnp.sum(dgl, axis=0, keepdims=True))
        part = jnp.concatenate(parts, axis=1)

        @pl.when(i == 0)
        def _():
            db_ref[...] = part

        @pl.when(i > 0)
        def _():
            db_ref[...] += part

    gspec = lambda b: pl.BlockSpec((tm, D_MODEL), lambda i: (i, R_GL // D_MODEL + b))
    vec = pl.BlockSpec((1, 3 * D_MODEL), lambda i: (0, 0))
    row = pl.BlockSpec((tm, D_MODEL), lambda i: (i, 0))
    outs = pl.pallas_call(
        body, name=name, grid=(S // tm,),
        in_specs=[row, pl.BlockSpec((3, tm, D_MODEL), lambda i: (0, i, 0)), gspec(0), gspec(1), gspec(2), vec],
        out_specs=[row, row, row, pl.BlockSpec((tm, 3 * D_MODEL), lambda i: (i, 0)), vec],
        out_shape=[jax.ShapeDtypeStruct((S, D_MODEL), BF16)] * 3
        + [jax.ShapeDtypeStruct((S, 3 * D_MODEL), BF16), jax.ShapeDtypeStruct((1, 3 * D_MODEL), F32)],
        compiler_params=_cp(("arbitrary",)))(dm, prods, ur, ur, ur, b_merge)
    return outs[0:3], outs[3], outs[4]


def _rope_tables(pos):
    half = ROT // 2
    S = pos.shape[0]
    inv = ROPE_THETA ** (-jnp.arange(half, dtype=F32) / half)
    per_row = LANES // half
    ang = jnp.repeat(pos.astype(F32).reshape(S // per_row, per_row), half, axis=1) * jnp.tile(inv, per_row)
    cos, sin = lax.optimization_barrier((jnp.cos(ang).reshape(S, half), jnp.sin(ang).reshape(S, half)))
    one = jnp.ones((S, LANES - ROT), F32)
    zero = jnp.zeros((S, LANES - ROT), F32)
    zh = jnp.zeros((S, half), F32)
    c = jnp.concatenate([cos, cos, one], axis=1)
    s1 = jnp.concatenate([-sin, zh, zero], axis=1)
    s2 = jnp.concatenate([zh, sin, zero], axis=1)
    return c, s1, s2


def _to_tiles(t):
    S, H = t.shape
    return t.reshape(S // LANES, LANES, H).transpose(0, 2, 1)


def _from_tiles(t):
    nt, H, _ = t.shape
    return t.transpose(1, 0, 2).reshape(H, nt * LANES)


def _local_step(x, mem, pos, tgt, g_pre, g_post, g_mem, wt, bf_pad, b_merge, w_kv, wbs, w_out, pack=None, hs=None):
    S = x.shape[0]
    T = min(512, S)
    nq = S // T
    tabs = _rope_tables(pos)

    if hs is None:
        hs = _rms_fwd(x, g_pre, name="rms_pre", dilations=DIL)
    h = hs[0]
    tabs_g = [[_to_classes(t, d) for t in tabs] for d in DIL]
    qkvs = [_proj_rope(hs[g], wt[f"A{g}"], tabs_g[g], name=f"proj_a{g}") for g in range(3)]
    ub = _mm(h, wt["B"], bt=True, out_dtype=BF16, name="proj_b", tn=1536)
    ur = _mm(h, wt["R"], bt=True, name="proj_r", tn=1792)

    outs_c, lses_c = [], []
    for g in range(3):
        o, l = _attn_a_fwd(qkvs[g], g, name=f"attn_a_fwd{g}")
        outs_c.append(o)
        lses_c.append(l)
    ya = _merge_a_fwd(outs_c, lses_c, ur, name="merge_a_fwd")

    logf = _logf(ur, bf_pad, name="logf")
    c = _from_tiles(_cumsum_lanes(_to_tiles(logf[:, :B_HEADS]), False, name="cumsum_fwd"))
    qaug, kaug = _fox_aug(ub, c, name="fox_aug")
    kt = ub[:, 512:1024].reshape(nq, T, 512).transpose(0, 2, 1)
    vt = ub[:, 1024:1536].reshape(nq, T, 512).transpose(0, 2, 1)
    ob, lse_b = _fox_fwd(qaug, kaug, vt, name="fox_fwd")
    yb = _gate_fwd(ob, ur, R_ZB, name="gate_b_fwd")

    hm = _rms_fwd(mem, g_mem, name="rms_mem")
    mkv = _mm(hm, w_kv, name="proj_mem")
    ym = _mem_fwd(ur, mkv, name="mem_fwd")

    merged, prods = _branch_fwd((ya, yb, ym), wbs, ur, b_merge, name="branch_fwd")
    out = _mm(merged, w_out, name="proj_out")
    dy, d_out, dg_post, loss_row = _post(x, out, tgt, g_post, name="post")

    dmerged = _mm(d_out, w_out, bt=True, name="d_merged")
    dw_out = _mm(merged, d_out, at=True, name="dw_out", tk=2048)
    dprods, dgl, db_merge = _branch_bwd(dmerged, prods, ur, b_merge, name="branch_bwd")
    dys, dwbs = [], []
    for i, (y, wb) in enumerate(zip((ya, yb, ym), wbs)):
        dys.append(_mm(dprods[i], wb, bt=True, name=f"d_y{i}"))
        dwbs.append(_mm(y, dprods[i], at=True, name=f"dw_branch{i}", tk=2048))

    dos_c, adjs_c, dza = _merge_a_bwd(outs_c, lses_c, ur, dys[0], name="merge_a_bwd")
    dus_a = []
    for g, d in enumerate(DIL):
        do_c, adj_c = dos_c[g], adjs_c[g]
        dq =_attn_a_dq(qkvs[g], tabs_g[g], g, do_c, lses_c[g], adj_c, name=f"attn_a_dq{g}")
        dk, dv = _attn_a_dkv(qkvs[g], tabs_g[g], g, do_c, lses_c[g], adj_c, name=f"attn_a_dkv{g}")
        dus_a.append(jnp.concatenate([dq, dk, dv], axis=1))

    dob, dzb = _gate_bwd(ob, ur, R_ZB, dys[1], name="gate_b_bwd")
    delta_b = _fox_delta(ob, dob, name="fox_delta")
    dkb, dvb, dc_k, dqt, dc_q = _fox_bwd(ub, qaug, kaug, kt, dob, lse_b, delta_b, name="fox_bwd")
    dqb = (dqt.transpose(0, 2, 1).reshape(S, A_WIDTH) * B_SCALE).astype(BF16)
    du_b = jnp.concatenate([dqb, dkb, dvb], axis=1)
    dc = dc_q.reshape(B_HEADS, S) + dc_k.reshape(B_HEADS, S)
    dlogf = _from_tiles(_cumsum_lanes(_to_tiles(dc.T), True, name="cumsum_bwd"))
    dlogf_pad = jnp.pad(dlogf.T, ((0, 0), (0, FB_PAD - B_HEADS)))
    dfb, db_forget = _dfb(ur, bf_pad, dlogf_pad, name="dfb")

    dqm, dzm, dmk, dmv = _mem_bwd(ur, mkv, dys[2], name="mem_bwd")
    dmkv = jnp.concatenate([dmk, dmv], axis=1).astype(BF16)
    dhm = _mm(dmkv, w_kv, bt=True, name="d_hm")
    dw_kv = _mm(hm, dmkv, at=True, name="dw_kv")
    dg_mem = _rms_bwd(mem, g_mem, dhm, None, name="rms_mem_bwd")

    du_r = jnp.concatenate([dza, dzb, dqm, dzm, dgl, dfb], axis=1)
    dwt = {"R": _mm(du_r, h, at=True, name="dw_in_r", tm=1792, tk=1024),
           "B": _mm(du_b, h, at=True, name="dw_in_b", tm=1536, tk=2048)}
    for g in range(3):
        dwt[f"A{g}"] = _mm(dus_a[g], hs[g], at=True, name=f"dw_in_a{g}", tm=1536, tk=2048)
    res = dict(dwt=dwt, dw_kv=dw_kv, dwbs=dwbs, dw_out=dw_out)
    token_major = [(du_r, wt["R"]), (du_b, wt["B"]), (dus_a[0], wt["A0"])]
    if pack is None:
        dh_1 = _mm(dus_a[1], wt["A1"], name="d_h_a1", tk=1536)
        dh = _mm_sum(token_major, name="d_h_main")
    else:
        gbig = pack(dwt, dw_kv, dwbs, dw_out)
        own_idx = _own_slabs()
        dh_1, sib = _mm(dus_a[1], wt["A1"], name="d_h_a1", tk=1536, comm=_pair_comm(gbig))
        send = _pair_sum(gbig, sib, own_idx, 208, name="pair_sum")
        dh, recv = _mm_sum(token_major, name="d_h_main", comm=_chips_comm(send))
        res = dict(parts=[(gbig, None), (sib, 1), (recv, N_CHIP - 1)], own_idx=own_idx)
    dh_2 = _mm(dus_a[2], wt["A2"], name="d_h_a2", tk=1536)
    grad_x, dg_pre = _rms_bwd(x, g_pre, dh, dy, name="rms_pre_bwd", dh_classes=[(dh_1, DIL[1]), (dh_2, DIL[2])])

    return dict(res, loss=loss_row, grad_x=grad_x, dg_pre=dg_pre, dg_post=dg_post, dg_mem=dg_mem,
                db_forget=db_forget, db_merge=db_merge)


MESH = pl.DeviceIdType.MESH
ANY = pl.BlockSpec(memory_space=pl.ANY)


def _relations():
    return [(k >> 2 & 1, k >> 1 & 1, k & 1) for k in range(1, N_DEV)]


def _coords():
    return lax.axis_index("x"), lax.axis_index("y"), lax.axis_index("c")


def _gather_comm(shard):
    R, W = shard.shape

    def plan(x_ref, out_ref, send_sems, recv_sems, local_sem):
        x, y, c = _coords()
        me, sibling = (x, y, c), (x, y, 1 - c)
        chips = [(1 - x, y), (x, 1 - y), (1 - x, 1 - y)]

        def slot(px, py, pc):
            return out_ref.at[4 * px + 2 * py + pc]

        def copy(k, block, to, src=None):
            return pltpu.make_async_remote_copy(
                src_ref=slot(*block) if src is None else src, dst_ref=slot(*block),
                send_sem=send_sems.at[k], recv_sem=recv_sems.at[k], device_id=to, device_id_type=MESH)

        mine = pltpu.make_async_copy(x_ref, slot(*me), local_sem)
        first = [copy(0, me, sibling, src=x_ref)]
        first += [copy(1 + j, me, (*chip, c), src=x_ref) for j, chip in enumerate(chips)]
        return me, sibling, chips, c, copy, mine, first

    def start(*refs):
        _, _, _, _, _, mine, first = plan(*refs)
        mine.start()
        for cp in first:
            cp.start()

    def wait(*refs):
        me, sibling, chips, c, copy, mine, first = plan(*refs)
        passed = [copy(4 + j, (*chip, c), sibling) for j, chip in enumerate(chips)]
        for j, chip in enumerate(chips):
            copy(1 + j, (*chip, c), me).wait_recv()
            passed[j].start()
        copy(0, sibling, me).wait_recv()
        for j, chip in enumerate(chips):
            copy(4 + j, (*chip, 1 - c), me).wait_recv()
        for cp in first + passed:
            cp.wait_send()
        mine.wait()

    return dict(inputs=[shard], out_shape=[jax.ShapeDtypeStruct((N_DEV, R, W), shard.dtype)],
                sems=[pltpu.SemaphoreType.DMA((N_DEV - 1,)), pltpu.SemaphoreType.DMA((N_DEV - 1,)),
                      pltpu.SemaphoreType.DMA],
                start=start, wait=wait)


N_CHIP = 4


def _pair_comm(gbig):
    _, R, W = gbig.shape

    def copies(g_ref, sib_ref, send_sems, recv_sems):
        x, y, c = _coords()
        return [pltpu.make_async_remote_copy(
            src_ref=g_ref.at[4 * (x ^ (r >> 1)) + 2 * (y ^ (r & 1)) + (1 - c)], dst_ref=sib_ref.at[r],
            send_sem=send_sems.at[r], recv_sem=recv_sems.at[r], device_id=(x, y, 1 - c), device_id_type=MESH)
            for r in range(N_CHIP)]

    def start(*refs):
        for cp in copies(*refs):
            cp.start()

    def wait(*refs):
        cps = copies(*refs)
        for cp in cps:
            cp.wait_recv()
        for cp in cps:
            cp.wait_send()

    return dict(inputs=[gbig], out_shape=[jax.ShapeDtypeStruct((N_CHIP, R, W), gbig.dtype)],
                sems=[pltpu.SemaphoreType.DMA((N_CHIP,)), pltpu.SemaphoreType.DMA((N_CHIP,))],
                start=start, wait=wait)


def _own_slabs():
    x, y, c = _coords()
    return jnp.stack([4 * (x ^ (r >> 1)) + 2 * (y ^ (r & 1)) + c for r in range(N_CHIP)]).astype(jnp.int32)


def _pair_sum(gbig, sib, own_idx, tr, *, name):
    _, R, W = gbig.shape

    def body(idx_ref, a_ref, b_ref, o_ref):
        o_ref[...] = (a_ref[...] + b_ref[...]).astype(BF16)

    return pl.pallas_call(
        body, name=name,
        grid_spec=pltpu.PrefetchScalarGridSpec(
            num_scalar_prefetch=1, grid=(N_CHIP - 1, R // tr),
            in_specs=[pl.BlockSpec((None, tr, W), lambda r, i, idx: (idx[r + 1], i, 0)),
                      pl.BlockSpec((None, tr, W), lambda r, i, idx: (r + 1, i, 0))],
            out_specs=pl.BlockSpec((None, tr, W), lambda r, i, idx: (r, i, 0))),
        out_shape=jax.ShapeDtypeStruct((N_CHIP - 1, R, W), BF16),
        compiler_params=_cp(("parallel", "parallel")))(own_idx, gbig, sib)


def _chips_comm(send):
    nb, R, W = send.shape

    def copies(b_ref, rb_ref, send_sems, recv_sems):
        x, y, c = _coords()
        return [pltpu.make_async_remote_copy(
            src_ref=b_ref.at[r - 1], dst_ref=rb_ref.at[r - 1], send_sem=send_sems.at[r - 1],
            recv_sem=recv_sems.at[r - 1], device_id=(x ^ (r >> 1), y ^ (r & 1), c), device_id_type=MESH)
            for r in range(1, N_CHIP)]

    def start(*refs):
        for cp in copies(*refs):
            cp.start()

    def wait(*refs):
        cps = copies(*refs)
        for cp in cps:
            cp.wait_recv()
        for cp in cps:
            cp.wait_send()

    return dict(inputs=[send], out_shape=[jax.ShapeDtypeStruct((nb, R, W), send.dtype)],
                sems=[pltpu.SemaphoreType.DMA((nb,)), pltpu.SemaphoreType.DMA((nb,))],
                start=start, wait=wait)


def _gather_small(gsmall, *, name):
    n = N_DEV - 1

    def body(s_ref, rs_ref, send_sems, recv_sems, local_sem):
        x, y, c = _coords()
        me = 4 * x + 2 * y + c
        mine = pltpu.make_async_copy(s_ref, rs_ref.at[me], local_sem)
        mine.start()

        def copy(k, fx, fy, fc, slot):
            return pltpu.make_async_remote_copy(
                src_ref=s_ref, dst_ref=rs_ref.at[slot], send_sem=send_sems.at[k], recv_sem=recv_sems.at[k],
                device_id=(x ^ fx, y ^ fy, c ^ fc), device_id_type=MESH)

        started = [copy(k, *rel, me) for k, rel in enumerate(_relations())]
        for cp in started:
            cp.start()
        for k, (fx, fy, fc) in enumerate(_relations()):
            copy(k, fx, fy, fc, 4 * (x ^ fx) + 2 * (y ^ fy) + (c ^ fc)).wait_recv()
        for cp in started:
            cp.wait_send()
        mine.wait()

    return pl.pallas_call(
        body, name=name, out_shape=jax.ShapeDtypeStruct((N_DEV, 1, P_SMALL), gsmall.dtype),
        in_specs=[ANY], out_specs=ANY,
        scratch_shapes=[pltpu.SemaphoreType.DMA((n,)), pltpu.SemaphoreType.DMA((n,)), pltpu.SemaphoreType.DMA],
    )(gsmall)


def _part_specs(parts, tr, row0):
    assert row0 % tr == 0
    specs = []
    for a, n_used in parts:
        if n_used is None:
            specs.append(pl.BlockSpec((1, tr, a.shape[2]), lambda i, idx: (idx[0], row0 // tr + i, 0)))
        else:
            specs.append(pl.BlockSpec((n_used, tr, a.shape[2]), lambda i, idx: (0, row0 // tr + i, 0)))
    return specs


def _part_total(refs, parts):
    g = None
    for ref, (_, n_used) in zip(refs, parts):
        for k in range(n_used or 1):
            t = ref[k].astype(F32)
            g = t if g is None else g + t
    return g


def _sum_parts(parts, idx, row0, nrows, tr, *, name):
    W = parts[0][0].shape[2]
    assert nrows % tr == 0

    def body(idx_ref, *refs):
        refs[-1][...] = _part_total(refs[:-1], parts)

    return pl.pallas_call(
        body, name=name,
        grid_spec=pltpu.PrefetchScalarGridSpec(
            num_scalar_prefetch=1, grid=(nrows // tr,), in_specs=_part_specs(parts, tr, row0),
            out_specs=pl.BlockSpec((tr, W), lambda i, idx: (i, 0))),
        out_shape=jax.ShapeDtypeStruct((nrows, W), F32),
        compiler_params=_cp(("parallel",)))(idx, *[a for a, _ in parts])


def _adamw(parts, idx, w, m, v, tr, *, name):
    R, W = w.shape
    assert R % tr == 0
    np_ = len(parts)

    def body(idx_ref, *refs):
        w_ref, m_ref, v_ref, g_ref, d_ref, nm_ref, nv_ref = refs[np_:]
        g = _part_total(refs[:np_], parts)
        mm = ADAM_B1 * m_ref[...] + (1.0 - ADAM_B1) * g
        vv = ADAM_B2 * v_ref[...] + (1.0 - ADAM_B2) * (g * g)
        m_hat = mm / (1.0 - ADAM_B1 ** ADAM_STEP)
        v_hat = vv / (1.0 - ADAM_B2 ** ADAM_STEP)
        g_ref[...] = g
        d_ref[...] = -ADAM_LR * (m_hat / (jnp.sqrt(v_hat) + ADAM_EPS) + ADAM_WD * w_ref[...])
        nm_ref[...] = mm
        nv_ref[...] = vv

    blk = pl.BlockSpec((tr, W), lambda i, idx: (i, 0))
    return pl.pallas_call(
        body, name=name,
        grid_spec=pltpu.PrefetchScalarGridSpec(
            num_scalar_prefetch=1, grid=(R // tr,), in_specs=_part_specs(parts, tr, 0) + [blk, blk, blk],
            out_specs=[blk] * 4),
        out_shape=[jax.ShapeDtypeStruct((R, W), F32)] * 4,
        compiler_params=_cp(("parallel",)))(idx, *[a for a, _ in parts], w, m, v)


def _pack_rest(w_kv, wa, wb, wm, w_out):
    return jnp.concatenate([w_kv[0], w_out[0]] + [t[0].reshape(-1, D_MODEL) for t in (wa, wb, wm)], axis=0)


def _unpack_rest(t):
    br = lambda i: t[RO_BR + 64 * i:RO_BR + 64 * (i + 1)].reshape(1, A_WIDTH, D_MODEL // N_DEV)
    return t[None, RO_KV:RO_OUT], br(0), br(1), br(2), t[None, RO_OUT:RO_BR]


def _orig_rows(gathered, a, b):
    res = []
    while a < b:
        dev, r = divmod(a, CS)
        n = min(b - a, CS - r)
        res.append(gathered[dev, RO_IN + r:RO_IN + r + n])
        a += n
    return res


def _full_weights(gathered):
    wt = {}
    for name, ranges in SEGS.items():
        rows = [p for a, b in ranges for p in _orig_rows(gathered, a, b)]
        if SEG_PAD[name]:
            rows.append(jnp.zeros((SEG_PAD[name], D_MODEL), gathered.dtype))
        wt[name] = jnp.concatenate(rows, axis=0)
    w_kv = gathered[:, RO_KV:RO_OUT].reshape(D_MODEL, D_MODEL)
    w_out = gathered[:, RO_OUT:RO_BR].reshape(D_MODEL, D_MODEL)
    wbs = [gathered[:, RO_BR + 64 * i:RO_BR + 64 * (i + 1)].reshape(N_DEV, A_WIDTH, D_MODEL // N_DEV)
           .transpose(1, 0, 2).reshape(A_WIDTH, D_MODEL) for i in range(3)]
    return wt, w_kv, wbs, w_out


def _orig_order(dwt):
    pieces = []
    for name, ranges in SEGS.items():
        o = 0
        for a, b in ranges:
            pieces.append((a, dwt[name][o:o + b - a]))
            o += b - a
    pieces.sort(key=lambda p: p[0])
    return jnp.concatenate([p[1] for p in pieces], axis=0)


def _pack_grads(dwt, dw_kv, dwbs, dw_out):
    g_in = jnp.pad(_orig_order(dwt).reshape(N_DEV, CS, D_MODEL), ((0, 0), (0, IN_ROWS - CS), (0, 0)))
    br = [t.reshape(A_WIDTH, N_DEV, D_MODEL // N_DEV).transpose(1, 0, 2).reshape(N_DEV, -1, D_MODEL) for t in dwbs]
    return jnp.concatenate([dw_kv.reshape(N_DEV, -1, D_MODEL), dw_out.reshape(N_DEV, -1, D_MODEL)] + br + [g_in],
                           axis=1)


def kernel(x, mem, positions, norm_pre_g, norm_post_g, norm_mem_g, w_in, b_forget, b_merge, w_mem_kv, w_branch_a, w_branch_b, w_branch_m, w_out, loss_target, m_norm_pre_g, m_norm_post_g, m_norm_mem_g, m_w_in, m_b_forget, m_b_merge, m_w_mem_kv, m_w_branch_a, m_w_branch_b, m_w_branch_m, m_w_out, v_norm_pre_g, v_norm_post_g, v_norm_mem_g, v_w_in, v_b_forget, v_b_merge, v_w_mem_kv, v_w_branch_a, v_w_branch_b, v_w_branch_m, v_w_out):
    w_rest = _pack_rest(w_mem_kv, w_branch_a, w_branch_b, w_branch_m, w_out)
    shard = jnp.concatenate([w_rest.astype(BF16), w_in[0].T.astype(BF16),
                             jnp.zeros((IN_ROWS - CS, D_MODEL), BF16)], axis=0)
    hs, (gathered,) = _rms_fwd(x[0], norm_pre_g, name="rms_pre_gather", dilations=DIL, comm=_gather_comm(shard))
    wt, w_kv, wbs, w_o = _full_weights(gathered)

    bf_pad = jnp.pad(b_forget, ((0, 0), (0, FB_PAD - B_HEADS)))
    r = _local_step(x[0], mem[0], positions[0], loss_target[0], norm_pre_g, norm_post_g, norm_mem_g,
                    wt, bf_pad, b_merge, w_kv, wbs, w_o, pack=_pack_grads, hs=hs)

    gsmall = jnp.concatenate([r["dg_pre"], r["dg_post"], r["dg_mem"], r["db_merge"],
                              r["db_forget"][:, :LANES], r["loss"]], axis=1)
    rsmall = _gather_small(gsmall, name="gather_small")
    parts, own_idx = r["parts"], r["own_idx"]

    m_rest = _pack_rest(m_w_mem_kv, m_w_branch_a, m_w_branch_b, m_w_branch_m, m_w_out)
    v_rest = _pack_rest(v_w_mem_kv, v_w_branch_a, v_w_branch_b, v_w_branch_m, v_w_out)
    outs_rest = [_unpack_rest(t) for t in _adamw(parts, own_idx, w_rest, m_rest, v_rest, 64, name="adamw_rest")]
    g_in = _sum_parts(parts, own_idx, RO_IN, IN_ROWS, 16, name="sum_w_in")[:CS].T
    outs_in = _adamw([(g_in[None], 1)], own_idx, w_in[0], m_w_in[0], v_w_in[0], 128, name="adamw_w_in")

    def small_vec(a, b, c, d, e):
        z = jnp.zeros((1, LANES - B_HEADS), F32)
        return jnp.concatenate([a, b, c, d, e, z, jnp.zeros((1, LANES), F32)], axis=1)

    outs_small = _adamw([(rsmall, N_DEV)], own_idx, small_vec(norm_pre_g, norm_post_g, norm_mem_g, b_merge, b_forget),
                        small_vec(m_norm_pre_g, m_norm_post_g, m_norm_mem_g, m_b_merge, m_b_forget),
                        small_vec(v_norm_pre_g, v_norm_post_g, v_norm_mem_g, v_b_merge, v_b_forget),
                        1, name="adamw_small")

    def small_parts(t):
        return [t[:, O_GPRE:O_GPRE + D_MODEL], t[:, O_GPOST:O_GPOST + D_MODEL], t[:, O_GMEM:O_GMEM + D_MODEL],
                t[:, O_BF:O_BF + B_HEADS], t[:, O_BM:O_BM + 3 * D_MODEL]]

    loss = outs_small[0][0, O_LOSS]
    result = [loss, r["grad_x"][None]]
    for rest, w_i, small in zip(outs_rest, outs_in, outs_small):
        gp, gq, gm, bf, bm = small_parts(small)
        w_k, w_a, w_b, w_m, w_ot = rest
        result += [gp, gq, gm, w_i[None], bf, bm, w_k, w_a, w_b, w_m, w_ot]
    return tuple(result)
```

```python
import jax
import jax.numpy as jnp
from jax import lax
from jax.experimental import pallas as pl
from jax.experimental.pallas import tpu as pltpu

F32 = jnp.float32
BF16 = jnp.bfloat16

N_DEV = 8
D_MODEL = 1024
N_MEM = 256
EPS = 1e-6
NEG = -1e30
ROPE_THETA = 500000.0
DIL = (1, 4, 16)
A_HEADS = 4
HEAD = 128
A_WIDTH = 512
B_HEADS = 8
B_HEAD = 64
M_HEADS = 4
ROT = 32
IN_COLS = 11272
FB_PAD = 256

SEGS = {
    "A0": ((0, 512), (1536, 2048), (3072, 3584)),
    "A1": ((512, 1024), (2048, 2560), (3584, 4096)),
    "A2": ((1024, 1536), (2560, 3072), (4096, 4608)),
    "B": ((5120, 6656),),
    "R": ((4608, 5120), (6664, 7176), (7176, 7688), (7688, 8200), (8200, 11272), (6656, 6664)),
}
SEG_PAD = {"A0": 0, "A1": 0, "A2": 0, "B": 0, "R": FB_PAD - B_HEADS}
R_ZA, R_ZB, R_QM, R_ZM, R_GL, R_FB = 0, 512, 1024, 1536, 2048, 5120
NR = R_FB + FB_PAD

ADAM_LR, ADAM_B1, ADAM_B2, ADAM_EPS, ADAM_WD, ADAM_STEP = 0.001, 0.9, 0.999, 1e-08, 0.01, 10

LANES = 128
VMEM_LIMIT = 56 * 1024 * 1024

CS = IN_COLS // N_DEV
RO_KV, RO_OUT, RO_BR, RO_IN = 0, 128, 256, 448
IN_ROWS = 1424
ROWS = RO_IN + IN_ROWS
O_GPRE, O_GPOST, O_GMEM, O_BM, O_BF, O_LOSS = 0, 1024, 2048, 3072, 6144, 6272
P_SMALL = 6400


def _cp(sem=None):
    return pltpu.CompilerParams(dimension_semantics=sem, vmem_limit_bytes=VMEM_LIMIT)


def _dot(a, b):
    return jnp.dot(a, b, preferred_element_type=F32)


def _dot_nt(a, b):
    return lax.dot_general(a, b, (((1,), (1,)), ((), ())), preferred_element_type=F32)


def _sigmoid(z):
    return 1.0 / (1.0 + jnp.exp(-z))


def _mm(a, b, *, name, at=False, bt=False, out_dtype=F32, tm=1024, tn=1024, tk=None, comm=None):
    assert not (at and bt)
    K, M = a.shape if at else a.shape[::-1]
    N = b.shape[0] if bt else b.shape[1]
    tm, tn = min(tm, M), min(tn, N)
    tk = K if tk is None else min(tk, K)
    assert M % tm == 0 and N % tn == 0 and K % tk == 0
    nk = K // tk
    grid = (M // tm, N // tn, nk)
    n_in = len(comm["inputs"]) if comm else 0
    n_out = len(comm["out_shape"]) if comm else 0

    def body(a_ref, b_ref, *rest):
        c_in, o_ref, c_out = rest[:n_in], rest[n_in], rest[n_in + 1:n_in + 1 + n_out]
        acc_ref, sems = rest[n_in + 1 + n_out], rest[n_in + 2 + n_out:]
        if comm:
            step = (pl.program_id(0) * grid[1] + pl.program_id(1)) * grid[2] + pl.program_id(2)

            @pl.when(step == 0)
            def _():
                comm["start"](*c_in, *c_out, *sems)

        av = a_ref[...].astype(BF16)
        bv = b_ref[...].astype(BF16)
        if at:
            p = lax.dot_general(av, bv, (((0,), (0,)), ((), ())), preferred_element_type=F32)
        else:
            p = _dot_nt(av, bv) if bt else _dot(av, bv)
        if nk == 1:
            o_ref[...] = p.astype(out_dtype)
        else:
            k = pl.program_id(2)

            @pl.when(k == 0)
            def _():
                acc_ref[...] = p

            @pl.when(k > 0)
            def _():
                acc_ref[...] += p

            @pl.when(k == nk - 1)
            def _():
                o_ref[...] = acc_ref[...].astype(out_dtype)

        if comm:
            @pl.when(step == grid[0] * grid[1] * grid[2] - 1)
            def _():
                comm["wait"](*c_in, *c_out, *sems)

    b_spec = (pl.BlockSpec((tn, tk), lambda i, j, k: (j, k)) if bt
              else pl.BlockSpec((tk, tn), lambda i, j, k: (k, j)))
    a_spec = (pl.BlockSpec((tk, tm), lambda i, j, k: (k, i)) if at
              else pl.BlockSpec((tm, tk), lambda i, j, k: (i, k)))
    out_spec = pl.BlockSpec((tm, tn), lambda i, j, k: (i, j))
    out_shape = jax.ShapeDtypeStruct((M, N), out_dtype)
    acc = pltpu.VMEM((tm, tn) if nk > 1 else (8, LANES), F32)
    if not comm:
        return pl.pallas_call(
            body, name=name, grid=grid, in_specs=[a_spec, b_spec], out_specs=out_spec, out_shape=out_shape,
            scratch_shapes=[acc], compiler_params=_cp(("parallel", "parallel", "arbitrary")))(a, b)
    return pl.pallas_call(
        body, name=name, grid=grid, in_specs=[a_spec, b_spec] + [ANY] * n_in,
        out_specs=[out_spec] + [ANY] * n_out, out_shape=[out_shape] + comm["out_shape"],
        scratch_shapes=[acc] + comm["sems"],
        compiler_params=_cp(("arbitrary", "arbitrary", "arbitrary")))(a, b, *comm["inputs"])


def _mm_sum(pairs, *, name, tm=1024, tk=768, comm=None):
    M, N = pairs[0][0].shape[0], pairs[0][1].shape[1]
    tm = min(tm, M)
    steps = [a.shape[1] // tk for a, _ in pairs]
    assert M % tm == 0 and all(a.shape[1] % tk == 0 for a, _ in pairs)
    first = [sum(steps[:p]) for p in range(len(pairs))]
    total = sum(steps)
    grid = (M // tm, total)
    n_in = len(comm["inputs"]) if comm else 0
    n_out = len(comm["out_shape"]) if comm else 0
    npair = len(pairs)

    def body(*refs):
        ab, rest = refs[:2 * npair], refs[2 * npair:]
        c_in, o_ref, c_out = rest[:n_in], rest[n_in], rest[n_in + 1:n_in + 1 + n_out]
        acc_ref, sems = rest[n_in + 1 + n_out], rest[n_in + 2 + n_out:]
        k = pl.program_id(1)
        if comm:
            step = pl.program_id(0) * total + k

            @pl.when(step == 0)
            def _():
                comm["start"](*c_in, *c_out, *sems)

        @pl.when(k == 0)
        def _():
            acc_ref[...] = jnp.zeros((tm, N), F32)

        for p in range(npair):
            @pl.when(jnp.logical_and(k >= first[p], k < first[p] + steps[p]))
            def _(p=p):
                acc_ref[...] += _dot(ab[2 * p][...], ab[2 * p + 1][...])

        @pl.when(k == total - 1)
        def _():
            o_ref[...] = acc_ref[...]

        if comm:
            @pl.when(step == grid[0] * total - 1)
            def _():
                comm["wait"](*c_in, *c_out, *sems)

    def local(p):
        return lambda k: jnp.clip(k - first[p], 0, steps[p] - 1)

    in_specs = []
    for p in range(npair):
        in_specs += [pl.BlockSpec((tm, tk), lambda i, k, f=local(p): (i, f(k))),
                     pl.BlockSpec((tk, N), lambda i, k, f=local(p): (f(k), 0))]
    out_spec = pl.BlockSpec((tm, N), lambda i, k: (i, 0))
    out_shape = jax.ShapeDtypeStruct((M, N), F32)
    args = [t for pair in pairs for t in pair]
    if not comm:
        return pl.pallas_call(
            body, name=name, grid=grid, in_specs=in_specs, out_specs=out_spec, out_shape=out_shape,
            scratch_shapes=[pltpu.VMEM((tm, N), F32)], compiler_params=_cp(("parallel", "arbitrary")))(*args)
    return pl.pallas_call(
        body, name=name, grid=grid, in_specs=in_specs + [ANY] * n_in,
        out_specs=[out_spec] + [ANY] * n_out, out_shape=[out_shape] + comm["out_shape"],
        scratch_shapes=[pltpu.VMEM((tm, N), F32)] + comm["sems"],
        compiler_params=_cp(("arbitrary", "arbitrary")))(*args, *comm["inputs"])


def _class_spec(S, d, tm, width):
    return pl.BlockSpec((d, tm // d, width), lambda i: (0, i, 0))


def _rms_fwd(x, g, *, name, dilations=(), comm=None):
    S, D = x.shape
    tm = min(512, S)
    ds = [d for d in dilations if d > 1]
    nsteps = S // tm
    n_in = len(comm["inputs"]) if comm else 0
    n_out = len(comm["out_shape"]) if comm else 0
    n_tmp = D // LANES if ds else 0

    def body(x_ref, g_ref, *rest):
        c_in, o_ref, rest = rest[:n_in], rest[n_in], rest[n_in + 1:]
        cls, c_out, rest = rest[:len(ds)], rest[len(ds):len(ds) + n_out], rest[len(ds) + n_out:]
        tmps, sems = rest[:n_tmp], rest[n_tmp:]
        if comm:
            @pl.when(pl.program_id(0) == 0)
            def _():
                comm["start"](*c_in, *c_out, *sems)

        xv = x_ref[...]
        r = lax.rsqrt(jnp.mean(xv * xv, axis=-1, keepdims=True) + EPS)
        hv = xv * r * g_ref[...]
        o_ref[...] = hv.astype(BF16)
        if ds:
            for c, tmp in enumerate(tmps):
                tmp[...] = hv[:, c * LANES:(c + 1) * LANES]
            for c_ref, d in zip(cls, ds):
                for k in range(d):
                    c_ref[k] = jnp.concatenate([tmp[pl.ds(k, tm // d, stride=d), :] for tmp in tmps],
                                               axis=1).astype(BF16)
        if comm:
            @pl.when(pl.program_id(0) == nsteps - 1)
            def _():
                comm["wait"](*c_in, *c_out, *sems)

    row = pl.BlockSpec((tm, D), lambda i: (i, 0))
    outs = pl.pallas_call(
        body, name=name, grid=(nsteps,),
        in_specs=[row, pl.BlockSpec((1, D), lambda i: (0, 0))] + [ANY] * n_in,
        out_specs=[row] + [_class_spec(S, d, tm, D) for d in ds] + [ANY] * n_out,
        out_shape=[jax.ShapeDtypeStruct((S, D), BF16)] + [jax.ShapeDtypeStruct((d, S // d, D), BF16) for d in ds]
        + (comm["out_shape"] if comm else []),
        scratch_shapes=[pltpu.VMEM((tm, LANES), F32)] * n_tmp + (comm["sems"] if comm else []),
        compiler_params=_cp(("arbitrary",) if comm else ("parallel",)),
    )(x, g, *(comm["inputs"] if comm else []))
    rows = [outs[0]] + [o.reshape(S, D) for o in outs[1:1 + len(ds)]]
    if comm:
        return rows, list(outs[1 + len(ds):])
    return rows if ds else rows[0]


def _rms_bwd(x, g, dh, dy, *, name, dh_classes=()):
    S, D = x.shape
    tm = min(512, S)
    want_dx = dy is not None
    nc = len(dh_classes)

    def body(*refs):
        c_refs, refs = refs[:nc], refs[nc:]
        if want_dx:
            x_ref, g_ref, dh_ref, dy_ref, dx_ref, dg_ref = refs[:6]
        else:
            x_ref, g_ref, dh_ref, dg_ref = refs[:4]
        i = pl.program_id(0)
        xv = x_ref[...]
        r = lax.rsqrt(jnp.mean(xv * xv, axis=-1, keepdims=True) + EPS)
        xh = xv * r
        if nc:
            tmps = refs[-(D // LANES):]
            cols = [slice(c * LANES, (c + 1) * LANES) for c in range(D // LANES)]
            for tmp, cs in zip(tmps, cols):
                tmp[...] = dh_ref[:, cs]
            for c_ref, (_, d) in zip(c_refs, dh_classes):
                for k in range(d):
                    for tmp, cs in zip(tmps, cols):
                        tmp[pl.ds(k, tm // d, stride=d), :] += c_ref[k, :, cs]
            dhv = jnp.concatenate([tmp[...] for tmp in tmps], axis=1)
        else:
            dhv = dh_ref[...]
        part = jnp.sum(dhv * xh, axis=0, keepdims=True)

        @pl.when(i == 0)
        def _():
            dg_ref[...] = part

        @pl.when(i > 0)
        def _():
            dg_ref[...] += part

        if want_dx:
            dxh = dhv * g_ref[...]
            dx_ref[...] = dy_ref[...] + r * (dxh - xh * jnp.mean(dxh * xh, axis=-1, keepdims=True))

    row = pl.BlockSpec((tm, D), lambda i: (i, 0))
    vec = pl.BlockSpec((1, D), lambda i: (0, 0))
    c_specs = [_class_spec(S, d, tm, D) for _, d in dh_classes]
    c_args = [a.reshape(d, S // d, D) for a, d in dh_classes]
    scratch = [pltpu.VMEM((tm, LANES), F32)] * (D // LANES) if nc else []
    if want_dx:
        return pl.pallas_call(
            body, name=name, grid=(S // tm,), in_specs=c_specs + [row, vec, row, row], out_specs=[row, vec],
            out_shape=[jax.ShapeDtypeStruct((S, D), F32), jax.ShapeDtypeStruct((1, D), F32)],
            scratch_shapes=scratch, compiler_params=_cp(("arbitrary",)))(*c_args, x, g, dh, dy)
    return pl.pallas_call(
        body, name=name, grid=(S // tm,), in_specs=c_specs + [row, vec, row], out_specs=vec,
        out_shape=jax.ShapeDtypeStruct((1, D), F32),
        scratch_shapes=scratch, compiler_params=_cp(("arbitrary",)))(*c_args, x, g, dh)


def _post(x, out, tgt, g, *, name):
    S, D = x.shape
    tm = min(512, S)

    def body(x_ref, o_ref, t_ref, g_ref, dy_ref, do_ref, dg_ref, loss_ref):
        i = pl.program_id(0)
        ov = o_ref[...]
        r = lax.rsqrt(jnp.mean(ov * ov, axis=-1, keepdims=True) + EPS)
        n = ov * r
        gv = g_ref[...]
        e = (x_ref[...] + n * gv) - t_ref[...]
        lpart = 0.5 * jnp.sum(jnp.mean(e * e, axis=-1, keepdims=True), axis=0, keepdims=True)
        dy = e * (1.0 / D)
        dy_ref[...] = dy
        dn = dy * gv
        do_ref[...] = (r * (dn - n * jnp.mean(dn * n, axis=-1, keepdims=True))).astype(BF16)
        gpart = jnp.sum(dy * n, axis=0, keepdims=True)
        lrow = jnp.broadcast_to(lpart, (1, LANES))

        @pl.when(i == 0)
        def _():
            dg_ref[...] = gpart
            loss_ref[...] = lrow

        @pl.when(i > 0)
        def _():
            dg_ref[...] += gpart
            loss_ref[...] += lrow

    row = pl.BlockSpec((tm, D), lambda i: (i, 0))
    vec = pl.BlockSpec((1, D), lambda i: (0, 0))
    return pl.pallas_call(
        body, name=name, grid=(S // tm,), in_specs=[row, row, row, vec],
        out_specs=[row, row, vec, pl.BlockSpec((1, LANES), lambda i: (0, 0))],
        out_shape=[jax.ShapeDtypeStruct((S, D), F32), jax.ShapeDtypeStruct((S, D), BF16),
                   jax.ShapeDtypeStruct((1, D), F32), jax.ShapeDtypeStruct((1, LANES), F32)],
        compiler_params=_cp(("arbitrary",)))(x, out, tgt, g)


def _to_classes(t, d):
    if d == 1:
        return t
    S, C = t.shape
    return t.reshape(S // d, d, C).transpose(1, 0, 2).reshape(S, C)


def _rope(x, c, s1, s2):
    return x * c + pltpu.roll(x, LANES - ROT // 2, 1) * s1 + pltpu.roll(x, ROT // 2, 1) * s2


def _unrope(d, c, s1, s2):
    return d * c + pltpu.roll(d * s1, ROT // 2, 1) + pltpu.roll(d * s2, LANES - ROT // 2, 1)


def _a_band(qb):
    r = lax.broadcasted_iota(jnp.int32, (qb, qb + HEAD), 0)
    c = lax.broadcasted_iota(jnp.int32, (qb, qb + HEAD), 1)
    return jnp.logical_and(c >= r, c <= r + HEAD)


def _a_first_ok(qb, n):
    c = lax.broadcasted_iota(jnp.int32, (qb, qb + HEAD), 1)
    return jnp.logical_or(c >= HEAD, n > 0)


def _a_last_ok(qb, has_next):
    c = lax.broadcasted_iota(jnp.int32, (qb, qb + HEAD), 1)
    return jnp.logical_or(c < qb, has_next)


A_SCALE = HEAD ** -0.5


def _a_geometry(S, g):
    d = DIL[g]
    L = S // d
    TQ = min(512, L)
    return d, L, TQ, TQ // HEAD, L // TQ, L // HEAD


def _proj_rope(h, w, tabs, *, name):
    S, D = h.shape
    tm = min(512, S)

    def body(h_ref, w_ref, c_ref, s1_ref, s2_ref, o_ref):
        tc = (c_ref[...], s1_ref[...], s2_ref[...])
        u = _dot_nt(h_ref[...], w_ref[...])
        for j in range(3 * A_HEADS):
            sl = slice(j * HEAD, (j + 1) * HEAD)
            o_ref[:, sl] = (_rope(u[:, sl], *tc) if j < 2 * A_HEADS else u[:, sl]).astype(BF16)

    tab = pl.BlockSpec((tm, LANES), lambda i: (i, 0))
    return pl.pallas_call(
        body, name=name, grid=(S // tm,),
        in_specs=[pl.BlockSpec((tm, D), lambda i: (i, 0)), pl.BlockSpec((3 * A_WIDTH, D), lambda i: (0, 0)),
                  tab, tab, tab],
        out_specs=pl.BlockSpec((tm, 3 * A_WIDTH), lambda i: (i, 0)),
        out_shape=jax.ShapeDtypeStruct((S, 3 * A_WIDTH), BF16),
        compiler_params=_cp(("parallel",)))(h, w, *tabs)


def _attn_a_fwd(qkv, g, *, name):
    S = qkv.shape[0]
    d, L, TQ, nsub, nb, nblk = _a_geometry(S, g)

    def body(q_ref, kc_ref, kp_ref, vc_ref, vp_ref, o_ref, l_ref):
        n = pl.program_id(1)
        QB = min(2 * HEAD, TQ)
        band = _a_band(QB)
        first = jnp.logical_and(band, _a_first_ok(QB, n))
        for h in range(A_HEADS):
            hs = slice(h * HEAD, (h + 1) * HEAD)
            for hh in range(TQ // QB):
                sl = slice(hh * QB, (hh + 1) * QB)
                pv = slice(hh * QB - HEAD, hh * QB)
                kcat = jnp.concatenate([kp_ref[:, hs] if hh == 0 else kc_ref[pv, hs], kc_ref[sl, hs]], axis=0)
                vcat = jnp.concatenate([vp_ref[:, hs] if hh == 0 else vc_ref[pv, hs], vc_ref[sl, hs]], axis=0)
                s = jnp.where(first if hh == 0 else band, _dot_nt(q_ref[sl, hs], kcat) * A_SCALE, NEG)
                m = jnp.max(s, axis=-1, keepdims=True)
                p = jnp.exp(s - m)
                den = jnp.sum(p, axis=-1, keepdims=True)
                o_ref[sl, hs] = _dot(p.astype(BF16), vcat) / den
                l_ref[sl, hs] = jnp.broadcast_to(m + jnp.log(den), (QB, HEAD))

    rcur = lambda r, n: r * nb + n
    rprv = lambda r, n: r * nblk + jnp.maximum(n * nsub - 1, 0)
    cur = lambda off: pl.BlockSpec((TQ, A_WIDTH), lambda r, n: (rcur(r, n), off))
    prv = lambda off: pl.BlockSpec((HEAD, A_WIDTH), lambda r, n: (rprv(r, n), off))
    out = pl.BlockSpec((TQ, A_WIDTH), lambda r, n: (rcur(r, n), 0))
    return pl.pallas_call(
        body, name=name, grid=(d, nb),
        in_specs=[cur(0), cur(1), prv(1), cur(2), prv(2)],
        out_specs=[out, out],
        out_shape=[jax.ShapeDtypeStruct((S, A_WIDTH), F32)] * 2,
        compiler_params=_cp(("parallel", "parallel")),
    )(qkv, qkv, qkv, qkv, qkv)


def _attn_a_dq(qkv, tabs, g, do, lse, adj, du, *, name):
    S = qkv.shape[0]
    d, L, TQ, nsub, nb, nblk = _a_geometry(S, g)

    def body(q_ref, kc_ref, kp_ref, vc_ref, vp_ref, do_ref, l_ref, adj_ref, c_ref, s1_ref, s2_ref, du_ref, dq_ref):
        n = pl.program_id(1)
        QB = min(2 * HEAD, TQ)
        band = _a_band(QB)
        first = jnp.logical_and(band, _a_first_ok(QB, n))
        for h in range(A_HEADS):
            hs = slice(h * HEAD, (h + 1) * HEAD)
            for hh in range(TQ // QB):
                sl = slice(hh * QB, (hh + 1) * QB)
                pv = slice(hh * QB - HEAD, hh * QB)
                kcat = jnp.concatenate([kp_ref[:, hs] if hh == 0 else kc_ref[pv, hs], kc_ref[sl, hs]], axis=0)
                vcat = jnp.concatenate([vp_ref[:, hs] if hh == 0 else vc_ref[pv, hs], vc_ref[sl, hs]], axis=0)
                s = jnp.where(first if hh == 0 else band, _dot_nt(q_ref[sl, hs], kcat) * A_SCALE, NEG)
                p = jnp.exp(s - l_ref[sl, hs][:, :1])
                ds = p * (_dot_nt(do_ref[sl, hs], vcat) + adj_ref[sl, hs][:, :1])
                dq = _dot(ds.astype(BF16), kcat) * A_SCALE
                dq_ref[sl, hs] = _unrope(dq, c_ref[sl, :], s1_ref[sl, :], s2_ref[sl, :]).astype(BF16)

    rcur = lambda r, n: r * nb + n
    rprv = lambda r, n: r * nblk + jnp.maximum(n * nsub - 1, 0)
    cur = lambda off: pl.BlockSpec((TQ, A_WIDTH), lambda r, n: (rcur(r, n), off))
    prv = lambda off: pl.BlockSpec((HEAD, A_WIDTH), lambda r, n: (rprv(r, n), off))
    tcur = pl.BlockSpec((TQ, LANES), lambda r, n: (rcur(r, n), 0))
    blk = cur(0)
    return pl.pallas_call(
        body, name=name, grid=(d, nb),
        in_specs=[cur(0), cur(1), prv(1), cur(2), prv(2), blk, blk, blk, tcur, tcur, tcur, ANY],
        out_specs=blk,
        out_shape=jax.ShapeDtypeStruct((S, 3 * A_WIDTH), BF16),
        input_output_aliases={11: 0},
        compiler_params=_cp(("parallel", "parallel")),
    )(qkv, qkv, qkv, qkv, qkv, do, lse, adj, *tabs, du)


def _attn_a_dkv(qkv, tabs, g, do, lse, adj, *, name):
    S = qkv.shape[0]
    d, L, TQ, nsub, nb, nblk = _a_geometry(S, g)

    def body(qc_ref, qn_ref, kc_ref, vc_ref, doc_ref, don_ref, lc_ref, ln_ref, ac_ref, an_ref,
             c_ref, s1_ref, s2_ref, du_ref):
        n = pl.program_id(1)
        QB = min(2 * HEAD, TQ)
        nh = TQ // QB
        band = _a_band(QB)
        end = jnp.logical_and(band, _a_last_ok(QB, n < nb - 1))
        for h in range(A_HEADS):
            hs = slice(h * HEAD, (h + 1) * HEAD)
            for kh in range(nh):
                sl = slice(kh * QB, (kh + 1) * QB)
                nx = slice((kh + 1) * QB, (kh + 1) * QB + HEAD)
                last = kh == nh - 1
                cat = lambda cur, nxt: jnp.concatenate([cur[sl, hs], nxt[:, hs] if last else cur[nx, hs]], axis=0)
                qcat = cat(qc_ref, qn_ref)
                docat = cat(doc_ref, don_ref)
                lt = cat(lc_ref, ln_ref).T[:1, :]
                at = cat(ac_ref, an_ref).T[:1, :]
                st = jnp.where(end if last else band, _dot_nt(kc_ref[sl, hs], qcat) * A_SCALE, NEG)
                pt = jnp.exp(st - lt)
                dv_cols = slice(2 * A_WIDTH + h * HEAD, 2 * A_WIDTH + (h + 1) * HEAD)
                dk_cols = slice(A_WIDTH + h * HEAD, A_WIDTH + (h + 1) * HEAD)
                du_ref[sl, dv_cols] = _dot(pt.astype(BF16), docat).astype(BF16)
                dst = pt * (_dot_nt(vc_ref[sl, hs], docat) + at)
                dk = _dot(dst.astype(BF16), qcat) * A_SCALE
                du_ref[sl, dk_cols] = _unrope(dk, c_ref[sl, :], s1_ref[sl, :], s2_ref[sl, :]).astype(BF16)

    rcur = lambda r, n: r * nb + n
    rnxt = lambda r, n: r * nblk + jnp.minimum((n + 1) * nsub, nblk - 1)
    cur = lambda off: pl.BlockSpec((TQ, A_WIDTH), lambda r, n: (rcur(r, n), off))
    nxu = lambda off: pl.BlockSpec((HEAD, A_WIDTH), lambda r, n: (rnxt(r, n), off))
    tcur = pl.BlockSpec((TQ, LANES), lambda r, n: (rcur(r, n), 0))
    blk, bnx = cur(0), nxu(0)
    return pl.pallas_call(
        body, name=name, grid=(d, nb),
        in_specs=[cur(0), nxu(0), cur(1), cur(2), blk, bnx, blk, bnx, blk, bnx, tcur, tcur, tcur],
        out_specs=pl.BlockSpec((TQ, 3 * A_WIDTH), lambda r, n: (rcur(r, n), 0)),
        out_shape=jax.ShapeDtypeStruct((S, 3 * A_WIDTH), BF16),
        compiler_params=_cp(("parallel", "parallel")),
    )(qkv, qkv, qkv, qkv, do, do, lse, lse, adj, adj, *tabs)


def _silu_parts(z):
    sg = _sigmoid(z)
    return z * sg, sg * (1.0 + z * (1.0 - sg))


def _classes_to_tokens(c_ref, d, tm, tmps):
    if d == 1:
        return c_ref[...].astype(F32)
    for k in range(d):
        for c, tmp in enumerate(tmps):
            tmp[pl.ds(k, tm // d, stride=d), :] = c_ref[k, :, c * LANES:(c + 1) * LANES].astype(F32)
    return jnp.concatenate([tmp[...] for tmp in tmps], axis=1)


def _tokens_to_classes(val, c_ref, d, tm, tmps):
    if d == 1:
        c_ref[...] = val.astype(c_ref.dtype)
        return
    for c, tmp in enumerate(tmps):
        tmp[...] = val[:, c * LANES:(c + 1) * LANES]
    for k in range(d):
        c_ref[k] = jnp.concatenate([tmp[pl.ds(k, tm // d, stride=d), :] for tmp in tmps], axis=1).astype(c_ref.dtype)


def _group_spec(S, d, tm):
    if d == 1:
        return pl.BlockSpec((tm, A_WIDTH), lambda i: (i, 0))
    return _class_spec(S, d, tm, A_WIDTH)


def _group_view(t, d):
    return t if d == 1 else t.reshape(d, t.shape[0] // d, t.shape[1])


def _merge_a_fwd(os_, ls_, ur, *, name):
    S = ur.shape[0]
    tm = min(512, S)

    def body(o0, o1, o2, l0, l1, l2, z_ref, y_ref, *tmps):
        ls = [_classes_to_tokens(r, d, tm, tmps) for r, d in zip((l0, l1, l2), DIL)]
        ov = [_classes_to_tokens(r, d, tm, tmps) for r, d in zip((o0, o1, o2), DIL)]
        mx = jnp.maximum(jnp.maximum(ls[0], ls[1]), ls[2])
        es = [jnp.exp(l - mx) for l in ls]
        den = es[0] + es[1] + es[2]
        y = (es[0] / den) * ov[0] + (es[1] / den) * ov[1] + (es[2] / den) * ov[2]
        y_ref[...] = (y * _silu_parts(z_ref[...])[0]).astype(BF16)

    blk = pl.BlockSpec((tm, A_WIDTH), lambda i: (i, 0))
    groups = [_group_spec(S, d, tm) for d in DIL]
    return pl.pallas_call(
        body, name=name, grid=(S // tm,),
        in_specs=groups + groups + [pl.BlockSpec((tm, A_WIDTH), lambda i: (i, R_ZA // A_WIDTH))],
        out_specs=blk, out_shape=jax.ShapeDtypeStruct((S, A_WIDTH), BF16),
        scratch_shapes=[pltpu.VMEM((tm, LANES), F32)] * (A_WIDTH // LANES),
        compiler_params=_cp(("parallel",)))(*[_group_view(t, d) for t, d in zip(os_, DIL)],
                                            *[_group_view(t, d) for t, d in zip(ls_, DIL)], ur)


def _merge_a_bwd(os_, ls_, ur, dya, *, name):
    S = ur.shape[0]
    tm = min(256, S)

    def body(o0, o1, o2, l0, l1, l2, z_ref, dy_ref, d0, d1, d2, a0, a1, a2, dz_ref, *tmps):
        ls = [_classes_to_tokens(r, d, tm, tmps) for r, d in zip((l0, l1, l2), DIL)]
        ov = [_classes_to_tokens(r, d, tm, tmps) for r, d in zip((o0, o1, o2), DIL)]
        mx = jnp.maximum(jnp.maximum(ls[0], ls[1]), ls[2])
        es = [jnp.exp(l - mx) for l in ls]
        den = es[0] + es[1] + es[2]
        ws = [e / den for e in es]
        y = ws[0] * ov[0] + ws[1] * ov[1] + ws[2] * ov[2]
        sz, dsz = _silu_parts(z_ref[...])
        dyv = dy_ref[...]
        dz_ref[...] = (dyv * y * dsz).astype(BF16)
        dyp = dyv * sz
        ts = []
        for h in range(A_HEADS):
            sl = slice(h * HEAD, (h + 1) * HEAD)
            t = jnp.zeros((tm, 1), F32)
            for gi in range(3):
                t = t + ws[gi][:, sl][:, :1] * jnp.sum(dyp[:, sl] * ov[gi][:, sl], axis=-1, keepdims=True)
            ts.append(jnp.broadcast_to(t, (tm, HEAD)))
        tb = jnp.concatenate(ts, axis=1)
        for gi, (dref, aref) in enumerate(((d0, a0), (d1, a1), (d2, a2))):
            _tokens_to_classes(ws[gi] * dyp, dref, DIL[gi], tm, tmps)
            _tokens_to_classes(-ws[gi] * tb, aref, DIL[gi], tm, tmps)

    blk = pl.BlockSpec((tm, A_WIDTH), lambda i: (i, 0))
    groups = [_group_spec(S, d, tm) for d in DIL]
    shaped = lambda dt: [jax.ShapeDtypeStruct((S, A_WIDTH) if d == 1 else (d, S // d, A_WIDTH), dt) for d in DIL]
    outs = pl.pallas_call(
        body, name=name, grid=(S // tm,),
        in_specs=groups + groups + [pl.BlockSpec((tm, A_WIDTH), lambda i: (i, R_ZA // A_WIDTH)), blk],
        out_specs=groups + groups + [blk],
        out_shape=shaped(BF16) + shaped(F32) + [jax.ShapeDtypeStruct((S, A_WIDTH), BF16)],
        scratch_shapes=[pltpu.VMEM((tm, LANES), F32)] * (A_WIDTH // LANES),
        compiler_params=_cp(("parallel",)))(*[_group_view(t, d) for t, d in zip(os_, DIL)],
                                            *[_group_view(t, d) for t, d in zip(ls_, DIL)], ur, dya)
    flat = [t.reshape(S, A_WIDTH) for t in outs[:6]]
    return flat[0:3], flat[3:6], outs[6]


def _logf(ur, bf_pad, *, name):
    S = ur.shape[0]
    tm = min(1024, S)

    def body(u_ref, b_ref, o_ref):
        z = u_ref[...] + b_ref[...]
        o_ref[...] = jnp.minimum(z, 0.0) - jnp.log(1.0 + jnp.exp(-jnp.abs(z)))

    return pl.pallas_call(
        body, name=name, grid=(S // tm,),
        in_specs=[pl.BlockSpec((tm, FB_PAD), lambda i: (i, R_FB // FB_PAD)),
                  pl.BlockSpec((1, FB_PAD), lambda i: (0, 0))],
        out_specs=pl.BlockSpec((tm, FB_PAD), lambda i: (i, 0)),
        out_shape=jax.ShapeDtypeStruct((S, FB_PAD), F32),
        compiler_params=_cp(("parallel",)))(ur, bf_pad)


def _cumsum_lanes(x, reverse, *, name):
    nt, H, _ = x.shape
    R = nt * H

    def body(x_ref, o_ref):
        v = x_ref[...].reshape(R, LANES)
        lane = lax.broadcasted_iota(jnp.int32, (R, LANES), 1)
        row = lax.broadcasted_iota(jnp.int32, (R, LANES), 0)

        def scan(t, step, idx, n, axis):
            while step < n:
                if reverse:
                    t = t + jnp.where(idx < n - step, pltpu.roll(t, n - step, axis), 0.0)
                else:
                    t = t + jnp.where(idx >= step, pltpu.roll(t, step, axis), 0.0)
                step *= 2
            return t

        v = scan(v, 1, lane, LANES, 1)
        total = jnp.broadcast_to(v[:, :1] if reverse else v[:, LANES - 1:], (R, LANES))
        carry = scan(total, H, row, R, 0) - total
        o_ref[...] = (v + carry).reshape(nt, H, LANES)

    return pl.pallas_call(
        body, name=name, out_shape=jax.ShapeDtypeStruct((nt, H, LANES), F32),
        in_specs=[pl.BlockSpec(memory_space=pltpu.VMEM)], out_specs=pl.BlockSpec(memory_space=pltpu.VMEM),
        compiler_params=_cp())(x)


B_SCALE = B_HEAD ** -0.5


def _pair_masks():
    lane = lax.broadcasted_iota(jnp.int32, (1, LANES), 1)
    row = lax.broadcasted_iota(jnp.int32, (LANES, 1), 0)
    return (lane < B_HEAD, lane >= B_HEAD), (row < B_HEAD, row >= B_HEAD)


def _causal_t(T):
    r = lax.broadcasted_iota(jnp.int32, (T, T), 0)
    c = lax.broadcasted_iota(jnp.int32, (T, T), 1)
    return r <= c


def _zero_other(x, keep):
    return jnp.where(keep, x, jnp.zeros_like(x))


def _fox_aug(ub, c, *, name):
    S = ub.shape[0]
    T = min(2048, S)

    def body(q_ref, k_ref, c_ref, qa_ref, ka_ref):
        lane = lax.broadcasted_iota(jnp.int32, (1, LANES), 1)
        q = q_ref[...] * B_SCALE
        k = k_ref[...]
        for a in range(2):
            own = (lane < B_HEAD) if a == 0 else (lane >= B_HEAD)
            o = B_HEAD if a == 0 else 0
            cv = jnp.broadcast_to(c_ref[:, a:a + 1], (T, LANES))
            hi = cv.astype(BF16)
            r1 = cv - hi.astype(F32)
            mid = r1.astype(BF16)
            lo = (r1 - mid.astype(F32)).astype(BF16)
            pieces = (hi, mid, lo)
            one = jnp.ones((T, LANES), BF16)
            qa = jnp.where(own, q, jnp.zeros_like(q))
            ka = jnp.where(own, k, jnp.zeros_like(k))
            for t in range(3):
                qa = jnp.where(lane == o + t, pieces[t], qa)
                qa = jnp.where(lane == o + 3 + t, one, qa)
                ka = jnp.where(lane == o + t, one, ka)
                ka = jnp.where(lane == o + 3 + t, -pieces[t], ka)
            qa_ref[a] = qa
            ka_ref[a] = ka

    out = pl.BlockSpec((2, T, LANES), lambda h, i: (h, i, 0))
    c_pairs = c.reshape(B_HEADS // 2, 2, S).transpose(0, 2, 1)
    return pl.pallas_call(
        body, name=name, grid=(B_HEADS // 2, S // T),
        in_specs=[pl.BlockSpec((T, LANES), lambda h, i: (i, h)), pl.BlockSpec((T, LANES), lambda h, i: (i, 4 + h)),
                  pl.BlockSpec((None, T, 2), lambda h, i: (h, i, 0))],
        out_specs=[out, out], out_shape=[jax.ShapeDtypeStruct((B_HEADS, S, LANES), BF16)] * 2,
        compiler_params=_cp(("parallel", "parallel")))(ub, ub, c_pairs)


def _fox_fwd(qaug, kaug, vt, *, name):
    S = qaug.shape[1]
    T = min(512, S)
    nq = S // T

    def body(q_ref, k_ref, vt_ref, o_ref, l_ref, m_s, l_s, acc_s, st_s):
        i = pl.program_id(1)
        _, rows = _pair_masks()
        qm = [q_ref[0], q_ref[1]]
        m_s[...] = jnp.full((2, 1, T), NEG, F32)
        l_s[...] = jnp.zeros((2, 1, T), F32)
        acc_s[...] = jnp.zeros((LANES, T), F32)

        def logits(j):
            off = pl.multiple_of(j * T, T)
            return [_dot_nt(k_ref[a, pl.ds(off, T), :], qm[a]) for a in range(2)]

        def step(j, masked, prefetch):
            nxt = logits(j + 1) if prefetch else None
            vtj = vt_ref[j]
            upd = jnp.zeros((LANES, T), F32)
            alphas = []
            for a in range(2):
                st = st_s[a]
                if masked:
                    st = jnp.where(_causal_t(T), st, NEG)
                m_old = m_s[a]
                m_new = jnp.maximum(m_old, jnp.max(st, axis=0, keepdims=True))
                alpha = jnp.exp(m_old - m_new)
                pt = jnp.exp(st - m_new)
                l_s[a] = alpha * l_s[a] + jnp.sum(pt, axis=0, keepdims=True)
                m_s[a] = m_new
                upd = upd + _dot(_zero_other(vtj, rows[a]), pt.astype(BF16))
                alphas.append(alpha)
            acc_s[...] = acc_s[...] * jnp.where(rows[0], alphas[0], alphas[1]) + upd
            if prefetch:
                st_s[0] = nxt[0]
                st_s[1] = nxt[1]

        def loop(j, carry):
            step(j, False, True)
            return carry

        first = logits(0)
        st_s[0] = first[0]
        st_s[1] = first[1]
        lax.fori_loop(0, i, loop, 0)
        step(i, True, False)
        o_ref[...] = (acc_s[...] / jnp.where(rows[0], l_s[0], l_s[1])).T
        l_ref[0] = m_s[0] + jnp.log(l_s[0])
        l_ref[1] = m_s[1] + jnp.log(l_s[1])

    stat = pl.BlockSpec((2, None, 1, T), lambda h, i: (h, i, 0, 0))
    return pl.pallas_call(
        body, name=name, grid=(B_HEADS // 2, nq),
        in_specs=[pl.BlockSpec((2, T, LANES), lambda h, i: (h, i, 0)),
                  pl.BlockSpec((2, S, LANES), lambda h, i: (h, 0, 0)),
                  pl.BlockSpec((nq, LANES, T), lambda h, i: (0, h, 0))],
        out_specs=[pl.BlockSpec((T, LANES), lambda h, i: (i, h)), stat],
        out_shape=[jax.ShapeDtypeStruct((S, A_WIDTH), F32), jax.ShapeDtypeStruct((B_HEADS, nq, 1, T), F32)],
        scratch_shapes=[pltpu.VMEM((2, 1, T), F32), pltpu.VMEM((2, 1, T), F32), pltpu.VMEM((LANES, T), F32),
                        pltpu.VMEM((2, T, T), F32)],
        compiler_params=_cp(("parallel", "parallel")),
    )(qaug, kaug, vt)


def _fox_delta(o, do, *, name):
    S = o.shape[0]
    T = min(512, S)
    nq = S // T

    per = min(4, nq)

    def body(o_ref, do_ref, d_ref):
        _, rows = _pair_masks()
        for t in range(per):
            sl = slice(t * T, (t + 1) * T)
            prod_t = (do_ref[sl, :].astype(F32) * o_ref[sl, :]).T
            d_ref[0, t] = jnp.sum(_zero_other(prod_t, rows[0]), axis=0, keepdims=True)
            d_ref[1, t] = jnp.sum(_zero_other(prod_t, rows[1]), axis=0, keepdims=True)

    tile = pl.BlockSpec((per * T, LANES), lambda h, i: (i, h))
    return pl.pallas_call(
        body, name=name, grid=(B_HEADS // 2, nq // per), in_specs=[tile, tile],
        out_specs=pl.BlockSpec((2, per, 1, T), lambda h, i: (h, i, 0, 0)),
        out_shape=jax.ShapeDtypeStruct((B_HEADS, nq, 1, T), F32),
        compiler_params=_cp(("parallel", "parallel")))(o, do)


def _fox_bwd(ub, qaug, kaug, kt, do, lse, delta, *, name):
    S = ub.shape[0]
    T = min(512, S)
    nq = S // T

    def body(k_ref, v_ref, kt_ref, q_ref, do_ref, l_ref, dl_ref,
             dk_ref, dv_ref, dck_ref, dqt_ref, dcq_ref, dk_s, dv_s, dc_s):
        j = pl.program_id(1)
        lanes, rows = _pair_masks()
        vv = v_ref[...]
        ktj = kt_ref[...]
        km = [k_ref[0], k_ref[1]]
        ktm = [_zero_other(ktj, rows[0]), _zero_other(ktj, rows[1])]
        dk_s[...] = jnp.zeros((2, T, LANES), F32)
        dv_s[...] = jnp.zeros((T, LANES), F32)
        dc_s[...] = jnp.zeros((2, T, 1), F32)

        @pl.when(j == 0)
        def _():
            dqt_ref[...] = jnp.zeros((nq, LANES, T), F32)
            dcq_ref[...] = jnp.zeros((2, nq, 1, T), F32)

        def step(i, masked):
            off = pl.multiple_of(i * T, T)
            doi = do_ref[pl.ds(off, T), :]
            upd = jnp.zeros((LANES, T), F32)
            for a in range(2):
                qi = q_ref[a, pl.ds(off, T), :]
                st = _dot_nt(km[a], qi)
                if masked:
                    st = jnp.where(_causal_t(T), st, NEG)
                pt = jnp.exp(st - l_ref[a, i])
                doa = _zero_other(doi, lanes[a])
                dv_s[...] += _dot(pt.astype(BF16), doa)
                dst = pt * (_dot_nt(vv, doa) - dl_ref[a, i])
                dsb = dst.astype(BF16)
                dk_s[a] += _dot(dsb, qi)
                upd = upd + _dot(ktm[a], dsb)
                dc_s[a] -= jnp.sum(dst, axis=-1, keepdims=True)
                dcq_ref[a, i] += jnp.sum(dst, axis=0, keepdims=True)
            dqt_ref[i] += upd

        def loop(i, carry):
            step(i, False)
            return carry

        step(j, True)
        lax.fori_loop(j + 1, nq, loop, 0)
        dk_ref[...] = jnp.where(lanes[0], dk_s[0], dk_s[1]).astype(BF16)
        dv_ref[...] = dv_s[...].astype(BF16)
        dck_ref[...] = dc_s[...]

    rowv = pl.BlockSpec((2, nq, 1, T), lambda h, j: (h, 0, 0, 0))
    tile = pl.BlockSpec((T, LANES), lambda h, j: (j, h))
    return pl.pallas_call(
        body, name=name, grid=(B_HEADS // 2, nq),
        in_specs=[pl.BlockSpec((2, T, LANES), lambda h, j: (h, j, 0)),
                  pl.BlockSpec((T, LANES), lambda h, j: (j, 8 + h)),
                  pl.BlockSpec((None, LANES, T), lambda h, j: (j, h, 0)),
                  pl.BlockSpec((2, S, LANES), lambda h, j: (h, 0, 0)),
                  pl.BlockSpec((S, LANES), lambda h, j: (0, h)),
                  rowv, rowv],
        out_specs=[tile, tile, pl.BlockSpec((2, T, 1), lambda h, j: (h, j, 0)),
                   pl.BlockSpec((nq, LANES, T), lambda h, j: (0, h, 0)), rowv],
        out_shape=[jax.ShapeDtypeStruct((S, A_WIDTH), BF16)] * 2 + [jax.ShapeDtypeStruct((B_HEADS, S, 1), F32),
                   jax.ShapeDtypeStruct((nq, A_WIDTH, T), F32), jax.ShapeDtypeStruct((B_HEADS, nq, 1, T), F32)],
        scratch_shapes=[pltpu.VMEM((2, T, LANES), F32), pltpu.VMEM((T, LANES), F32), pltpu.VMEM((2, T, 1), F32)],
        compiler_params=_cp(("parallel", "arbitrary")),
    )(kaug, ub, kt, qaug, do, lse, delta)


def _gate_fwd(o, ur, zcol, *, name):
    S = ur.shape[0]
    tm = min(1024, S)

    def body(o_ref, z_ref, y_ref):
        y_ref[...] = (o_ref[...] * _silu_parts(z_ref[...])[0]).astype(BF16)

    blk = pl.BlockSpec((tm, A_WIDTH), lambda i: (i, 0))
    return pl.pallas_call(
        body, name=name, grid=(S // tm,),
        in_specs=[blk, pl.BlockSpec((tm, A_WIDTH), lambda i: (i, zcol // A_WIDTH))],
        out_specs=blk, out_shape=jax.ShapeDtypeStruct((S, A_WIDTH), BF16),
        compiler_params=_cp(("parallel",)))(o, ur)


def _gate_bwd(o, ur, zcol, dy, *, name):
    S = ur.shape[0]
    tm = min(1024, S)

    def body(o_ref, z_ref, dy_ref, do_ref, dz_ref):
        sz, dsz = _silu_parts(z_ref[...])
        dyv = dy_ref[...]
        do_ref[...] = (dyv * sz).astype(BF16)
        dz_ref[...] = (dyv * o_ref[...] * dsz).astype(BF16)

    blk = pl.BlockSpec((tm, A_WIDTH), lambda i: (i, 0))
    return pl.pallas_call(
        body, name=name, grid=(S // tm,),
        in_specs=[blk, pl.BlockSpec((tm, A_WIDTH), lambda i: (i, zcol // A_WIDTH)), blk],
        out_specs=[blk, blk], out_shape=[jax.ShapeDtypeStruct((S, A_WIDTH), BF16)] * 2,
        compiler_params=_cp(("parallel",)))(o, ur, dy)


def _dfb(ur, bf_pad, dlogf_pad, *, name):
    S = ur.shape[0]
    tm = min(1024, S)

    def body(u_ref, b_ref, d_ref, o_ref, s_ref):
        i = pl.program_id(0)
        dv = d_ref[...] * _sigmoid(-(u_ref[...] + b_ref[...]))
        o_ref[...] = dv.astype(BF16)
        part = jnp.sum(dv, axis=0, keepdims=True)

        @pl.when(i == 0)
        def _():
            s_ref[...] = part

        @pl.when(i > 0)
        def _():
            s_ref[...] += part

    vec = pl.BlockSpec((1, FB_PAD), lambda i: (0, 0))
    blk = pl.BlockSpec((tm, FB_PAD), lambda i: (i, 0))
    return pl.pallas_call(
        body, name=name, grid=(S // tm,),
        in_specs=[pl.BlockSpec((tm, FB_PAD), lambda i: (i, R_FB // FB_PAD)), vec, blk],
        out_specs=[blk, vec],
        out_shape=[jax.ShapeDtypeStruct((S, FB_PAD), BF16), jax.ShapeDtypeStruct((1, FB_PAD), F32)],
        compiler_params=_cp(("arbitrary",)))(ur, bf_pad, dlogf_pad)


M_SCALE = HEAD ** -0.5


def _mem_fwd(ur, mkv, *, name):
    S = ur.shape[0]
    T = min(512, S)

    def body(q_ref, z_ref, k_ref, v_ref, y_ref):
        for h in range(M_HEADS):
            hs = slice(h * HEAD, (h + 1) * HEAD)
            s = _dot_nt(q_ref[:, hs].astype(BF16), k_ref[:, hs].astype(BF16)) * M_SCALE
            p = jnp.exp(s - jnp.max(s, axis=-1, keepdims=True))
            p = p / jnp.sum(p, axis=-1, keepdims=True)
            o = _dot(p.astype(BF16), v_ref[:, hs].astype(BF16))
            y_ref[:, hs] = (o * _silu_parts(z_ref[:, hs])[0]).astype(BF16)

    wide = lambda col: pl.BlockSpec((T, A_WIDTH), lambda i: (i, col // A_WIDTH))
    kv = lambda half: pl.BlockSpec((N_MEM, A_WIDTH), lambda i: (0, half))
    return pl.pallas_call(
        body, name=name, grid=(S // T,),
        in_specs=[wide(R_QM), wide(R_ZM), kv(0), kv(1)],
        out_specs=pl.BlockSpec((T, A_WIDTH), lambda i: (i, 0)),
        out_shape=jax.ShapeDtypeStruct((S, A_WIDTH), BF16),
        compiler_params=_cp(("parallel",)))(ur, ur, mkv, mkv)


def _mem_bwd(ur, mkv, dy, *, name):
    S = ur.shape[0]
    T = min(512, S)

    def body(q_ref, z_ref, k_ref, v_ref, dy_ref, dq_ref, dz_ref, dk_ref, dv_ref):
        i = pl.program_id(0)

        @pl.when(i == 0)
        def _():
            dk_ref[...] = jnp.zeros((N_MEM, A_WIDTH), F32)
            dv_ref[...] = jnp.zeros((N_MEM, A_WIDTH), F32)

        for h in range(M_HEADS):
            hs = slice(h * HEAD, (h + 1) * HEAD)
            qv = q_ref[:, hs].astype(BF16)
            kv = k_ref[:, hs].astype(BF16)
            vv = v_ref[:, hs].astype(BF16)
            s = _dot_nt(qv, kv) * M_SCALE
            p = jnp.exp(s - jnp.max(s, axis=-1, keepdims=True))
            p = p / jnp.sum(p, axis=-1, keepdims=True)
            o = _dot(p.astype(BF16), vv)
            sz, dsz = _silu_parts(z_ref[:, hs])
            dyv = dy_ref[:, hs]
            dz_ref[:, hs] = (dyv * o * dsz).astype(BF16)
            dov = (dyv * sz).astype(BF16)
            dp = _dot_nt(dov, vv)
            ds = p * (dp - jnp.sum(p * dp, axis=-1, keepdims=True))
            dq_ref[:, hs] = (_dot(ds.astype(BF16), kv) * M_SCALE).astype(BF16)
            dv_ref[:, hs] += _dot(p.T.astype(BF16), dov)
            dk_ref[:, hs] += _dot(ds.T.astype(BF16), qv) * M_SCALE

    wide = lambda col: pl.BlockSpec((T, A_WIDTH), lambda i: (i, col // A_WIDTH))
    kv = lambda half: pl.BlockSpec((N_MEM, A_WIDTH), lambda i: (0, half))
    tile = pl.BlockSpec((T, A_WIDTH), lambda i: (i, 0))
    acc = pl.BlockSpec((N_MEM, A_WIDTH), lambda i: (0, 0))
    return pl.pallas_call(
        body, name=name, grid=(S // T,),
        in_specs=[wide(R_QM), wide(R_ZM), kv(0), kv(1), tile],
        out_specs=[tile, tile, acc, acc],
        out_shape=[jax.ShapeDtypeStruct((S, A_WIDTH), BF16)] * 2
        + [jax.ShapeDtypeStruct((N_MEM, A_WIDTH), F32)] * 2,
        compiler_params=_cp(("arbitrary",)))(ur, ur, mkv, mkv, dy)


def _branch_fwd(ys, wbs, ur, b_merge, *, name):
    S = ur.shape[0]
    tm, tn = min(512, S), 512
    nj = D_MODEL // tn

    def body(ya, yb, ym, wa, wb, wm, g0, g1, g2, b0, b1, b2, mg_ref, p_ref):
        acc = jnp.zeros((tm, tn), F32)
        for i, (y, w, gr, br) in enumerate(((ya, wa, g0, b0), (yb, wb, g1, b1), (ym, wm, g2, b2))):
            pr = _dot(y[...], w[...])
            p_ref[i] = pr.astype(BF16)
            acc = acc + _sigmoid(gr[...] + br[...]) * pr
        mg_ref[...] = acc.astype(BF16)

    yspec = pl.BlockSpec((tm, A_WIDTH), lambda i, j: (i, 0))
    wspec = pl.BlockSpec((A_WIDTH, tn), lambda i, j: (0, j))
    gspec = lambda b: pl.BlockSpec((tm, tn), lambda i, j: (i, (R_GL + b * D_MODEL) // tn + j))
    bspec = lambda b: pl.BlockSpec((1, tn), lambda i, j: (0, b * nj + j))
    return pl.pallas_call(
        body, name=name, grid=(S // tm, nj),
        in_specs=[yspec] * 3 + [wspec] * 3 + [gspec(0), gspec(1), gspec(2), bspec(0), bspec(1), bspec(2)],
        out_specs=[pl.BlockSpec((tm, tn), lambda i, j: (i, j)),
                   pl.BlockSpec((3, tm, tn), lambda i, j: (0, i, j))],
        out_shape=[jax.ShapeDtypeStruct((S, D_MODEL), BF16), jax.ShapeDtypeStruct((3, S, D_MODEL), BF16)],
        compiler_params=_cp(("parallel", "parallel")))(*ys, *wbs, ur, ur, ur, b_merge, b_merge, b_merge)


def _branch_bwd(dm, prods, ur, b_merge, *, name):
    S = ur.shape[0]
    tm = min(256, S)

    def body(dm_ref, p_ref, g0, g1, g2, b_ref, dp0, dp1, dp2, dgl_ref, db_ref):
        i = pl.program_id(0)
        dmv = dm_ref[...]
        parts = []
        for b, (gr, dp_ref) in enumerate(((g0, dp0), (g1, dp1), (g2, dp2))):
            sl = slice(b * D_MODEL, (b + 1) * D_MODEL)
            gt = _sigmoid(gr[...] + b_ref[:, sl])
            dp_ref[...] = (dmv * gt).astype(BF16)
            dgl = dmv * p_ref[b].astype(F32) * gt * (1.0 - gt)
            dgl_ref[:, sl] = dgl.astype(BF16)
            parts.append(jnp.sum(dgl, axis=0, keepdims=True))
        part = jnp.concatenate(parts, axis=1)

        @pl.when(i == 0)
        def _():
            db_ref[...] = part

        @pl.when(i > 0)
        def _():
            db_ref[...] += part

    gspec = lambda b: pl.BlockSpec((tm, D_MODEL), lambda i: (i, R_GL // D_MODEL + b))
    vec = pl.BlockSpec((1, 3 * D_MODEL), lambda i: (0, 0))
    row = pl.BlockSpec((tm, D_MODEL), lambda i: (i, 0))
    outs = pl.pallas_call(
        body, name=name, grid=(S // tm,),
        in_specs=[row, pl.BlockSpec((3, tm, D_MODEL), lambda i: (0, i, 0)), gspec(0), gspec(1), gspec(2), vec],
        out_specs=[row, row, row, pl.BlockSpec((tm, 3 * D_MODEL), lambda i: (i, 0)), vec],
        out_shape=[jax.ShapeDtypeStruct((S, D_MODEL), BF16)] * 3
        + [jax.ShapeDtypeStruct((S, 3 * D_MODEL), BF16), jax.ShapeDtypeStruct((1, 3 * D_MODEL), F32)],
        compiler_params=_cp(("arbitrary",)))(dm, prods, ur, ur, ur, b_merge)
    return outs[0:3], outs[3], outs[4]


def _rope_tables(pos):
    half = ROT // 2
    S = pos.shape[0]
    inv = ROPE_THETA ** (-jnp.arange(half, dtype=F32) / half)
    per_row = LANES // half
    ang = jnp.repeat(pos.astype(F32).reshape(S // per_row, per_row), half, axis=1) * jnp.tile(inv, per_row)
    cos, sin = lax.optimization_barrier((jnp.cos(ang).reshape(S, half), jnp.sin(ang).reshape(S, half)))
    one = jnp.ones((S, LANES - ROT), F32)
    zero = jnp.zeros((S, LANES - ROT), F32)
    zh = jnp.zeros((S, half), F32)
    c = jnp.concatenate([cos, cos, one], axis=1)
    s1 = jnp.concatenate([-sin, zh, zero], axis=1)
    s2 = jnp.concatenate([zh, sin, zero], axis=1)
    return c, s1, s2


def _to_tiles(t):
    S, H = t.shape
    return t.reshape(S // LANES, LANES, H).transpose(0, 2, 1)


def _from_tiles(t):
    nt, H, _ = t.shape
    return t.transpose(1, 0, 2).reshape(H, nt * LANES)


def _local_step(x, mem, pos, tgt, g_pre, g_post, g_mem, wt, bf_pad, b_merge, w_kv, wbs, w_out, pack=None, hs=None):
    S = x.shape[0]
    T = min(512, S)
    nq = S // T
    tabs = _rope_tables(pos)

    if hs is None:
        hs = _rms_fwd(x, g_pre, name="rms_pre", dilations=DIL)
    h = hs[0]
    tabs_g = [[_to_classes(t, d) for t in tabs] for d in DIL]
    qkvs = [_proj_rope(hs[g], wt[f"A{g}"], tabs_g[g], name=f"proj_a{g}") for g in range(3)]
    ub = _mm(h, wt["B"], bt=True, out_dtype=BF16, name="proj_b", tn=1536)
    ur = _mm(h, wt["R"], bt=True, name="proj_r", tn=1792)

    outs_c, lses_c = [], []
    for g in range(3):
        o, l = _attn_a_fwd(qkvs[g], g, name=f"attn_a_fwd{g}")
        outs_c.append(o)
        lses_c.append(l)
    ya = _merge_a_fwd(outs_c, lses_c, ur, name="merge_a_fwd")

    logf = _logf(ur, bf_pad, name="logf")
    c = _from_tiles(_cumsum_lanes(_to_tiles(logf[:, :B_HEADS]), False, name="cumsum_fwd"))
    qaug, kaug = _fox_aug(ub, c, name="fox_aug")
    kt = ub[:, 512:1024].reshape(nq, T, 512).transpose(0, 2, 1)
    vt = ub[:, 1024:1536].reshape(nq, T, 512).transpose(0, 2, 1)
    ob, lse_b = _fox_fwd(qaug, kaug, vt, name="fox_fwd")
    yb = _gate_fwd(ob, ur, R_ZB, name="gate_b_fwd")

    hm = _rms_fwd(mem, g_mem, name="rms_mem")
    mkv = _mm(hm, w_kv, name="proj_mem")
    ym = _mem_fwd(ur, mkv, name="mem_fwd")

    merged, prods = _branch_fwd((ya, yb, ym), wbs, ur, b_merge, name="branch_fwd")
    out = _mm(merged, w_out, name="proj_out")
    dy, d_out, dg_post, loss_row = _post(x, out, tgt, g_post, name="post")

    dmerged = _mm(d_out, w_out, bt=True, name="d_merged")
    dw_out = _mm(merged, d_out, at=True, name="dw_out", tk=2048)
    dprods, dgl, db_merge = _branch_bwd(dmerged, prods, ur, b_merge, name="branch_bwd")
    dys, dwbs = [], []
    for i, (y, wb) in enumerate(zip((ya, yb, ym), wbs)):
        dys.append(_mm(dprods[i], wb, bt=True, name=f"d_y{i}"))
        dwbs.append(_mm(y, dprods[i], at=True, name=f"dw_branch{i}", tk=2048))

    dos_c, adjs_c, dza = _merge_a_bwd(outs_c, lses_c, ur, dys[0], name="merge_a_bwd")
    dus_a = []
    for g, d in enumerate(DIL):
        do_c, adj_c = dos_c[g], adjs_c[g]
        du = _attn_a_dkv(qkvs[g], tabs_g[g], g, do_c, lses_c[g], adj_c, name=f"attn_a_dkv{g}")
        dus_a.append(_attn_a_dq(qkvs[g], tabs_g[g], g, do_c, lses_c[g], adj_c, du, name=f"attn_a_dq{g}"))

    dob, dzb = _gate_bwd(ob, ur, R_ZB, dys[1], name="gate_b_bwd")
    delta_b = _fox_delta(ob, dob, name="fox_delta")
    dkb, dvb, dc_k, dqt, dc_q = _fox_bwd(ub, qaug, kaug, kt, dob, lse_b, delta_b, name="fox_bwd")
    dqb = (dqt.transpose(0, 2, 1).reshape(S, A_WIDTH) * B_SCALE).astype(BF16)
    du_b = jnp.concatenate([dqb, dkb, dvb], axis=1)
    dc = dc_q.reshape(B_HEADS, S) + dc_k.reshape(B_HEADS, S)
    dlogf = _from_tiles(_cumsum_lanes(_to_tiles(dc.T), True, name="cumsum_bwd"))
    dlogf_pad = jnp.pad(dlogf.T, ((0, 0), (0, FB_PAD - B_HEADS)))
    dfb, db_forget = _dfb(ur, bf_pad, dlogf_pad, name="dfb")

    dqm, dzm, dmk, dmv = _mem_bwd(ur, mkv, dys[2], name="mem_bwd")
    dmkv = jnp.concatenate([dmk, dmv], axis=1).astype(BF16)
    dhm = _mm(dmkv, w_kv, bt=True, name="d_hm")
    dw_kv = _mm(hm, dmkv, at=True, name="dw_kv")
    dg_mem = _rms_bwd(mem, g_mem, dhm, None, name="rms_mem_bwd")

    du_r = jnp.concatenate([dza, dzb, dqm, dzm, dgl, dfb], axis=1)
    dwt = {"R": _mm(du_r, h, at=True, name="dw_in_r", tm=1792, tk=1024),
           "B": _mm(du_b, h, at=True, name="dw_in_b", tm=1536, tk=2048)}
    for g in range(3):
        dwt[f"A{g}"] = _mm(dus_a[g], hs[g], at=True, name=f"dw_in_a{g}", tm=1536, tk=2048)
    res = dict(dwt=dwt, dw_kv=dw_kv, dwbs=dwbs, dw_out=dw_out)
    token_major = [(du_r, wt["R"]), (du_b, wt["B"]), (dus_a[0], wt["A0"])]
    if pack is None:
        dh_1 = _mm(dus_a[1], wt["A1"], name="d_h_a1", tk=1536)
        dh = _mm_sum(token_major, name="d_h_main")
    else:
        gbig = pack(dwt, dw_kv, dwbs, dw_out)
        own_idx = _own_slabs()
        dh_1, sib = _mm(dus_a[1], wt["A1"], name="d_h_a1", tk=1536, comm=_pair_comm(gbig))
        send = _pair_sum(gbig, sib, own_idx, 208, name="pair_sum")
        dh, recv = _mm_sum(token_major, name="d_h_main", comm=_chips_comm(send))
        res = dict(parts=[(gbig, None), (sib, 1), (recv, N_CHIP - 1)], own_idx=own_idx)
    dh_2 = _mm(dus_a[2], wt["A2"], name="d_h_a2", tk=1536)
    grad_x, dg_pre = _rms_bwd(x, g_pre, dh, dy, name="rms_pre_bwd", dh_classes=[(dh_1, DIL[1]), (dh_2, DIL[2])])

    return dict(res, loss=loss_row, grad_x=grad_x, dg_pre=dg_pre, dg_post=dg_post, dg_mem=dg_mem,
                db_forget=db_forget, db_merge=db_merge)


MESH = pl.DeviceIdType.MESH
ANY = pl.BlockSpec(memory_space=pl.ANY)


def _relations():
    return [(k >> 2 & 1, k >> 1 & 1, k & 1) for k in range(1, N_DEV)]


def _coords():
    return lax.axis_index("x"), lax.axis_index("y"), lax.axis_index("c")


def _gather_comm(shard):
    R, W = shard.shape

    def plan(x_ref, out_ref, send_sems, recv_sems, local_sem):
        x, y, c = _coords()
        me, sibling = (x, y, c), (x, y, 1 - c)
        chips = [(1 - x, y), (x, 1 - y), (1 - x, 1 - y)]

        def slot(px, py, pc):
            return out_ref.at[4 * px + 2 * py + pc]

        def copy(k, block, to, src=None):
            return pltpu.make_async_remote_copy(
                src_ref=slot(*block) if src is None else src, dst_ref=slot(*block),
                send_sem=send_sems.at[k], recv_sem=recv_sems.at[k], device_id=to, device_id_type=MESH)

        mine = pltpu.make_async_copy(x_ref, slot(*me), local_sem)
        first = [copy(0, me, sibling, src=x_ref)]
        first += [copy(1 + j, me, (*chip, c), src=x_ref) for j, chip in enumerate(chips)]
        return me, sibling, chips, c, copy, mine, first

    def start(*refs):
        _, _, _, _, _, mine, first = plan(*refs)
        mine.start()
        for cp in first:
            cp.start()

    def wait(*refs):
        me, sibling, chips, c, copy, mine, first = plan(*refs)
        passed = [copy(4 + j, (*chip, c), sibling) for j, chip in enumerate(chips)]
        for j, chip in enumerate(chips):
            copy(1 + j, (*chip, c), me).wait_recv()
            passed[j].start()
        copy(0, sibling, me).wait_recv()
        for j, chip in enumerate(chips):
            copy(4 + j, (*chip, 1 - c), me).wait_recv()
        for cp in first + passed:
            cp.wait_send()
        mine.wait()

    return dict(inputs=[shard], out_shape=[jax.ShapeDtypeStruct((N_DEV, R, W), shard.dtype)],
                sems=[pltpu.SemaphoreType.DMA((N_DEV - 1,)), pltpu.SemaphoreType.DMA((N_DEV - 1,)),
                      pltpu.SemaphoreType.DMA],
                start=start, wait=wait)


N_CHIP = 4


def _pair_comm(gbig):
    _, R, W = gbig.shape

    def copies(g_ref, sib_ref, send_sems, recv_sems):
        x, y, c = _coords()
        return [pltpu.make_async_remote_copy(
            src_ref=g_ref.at[4 * (x ^ (r >> 1)) + 2 * (y ^ (r & 1)) + (1 - c)], dst_ref=sib_ref.at[r],
            send_sem=send_sems.at[r], recv_sem=recv_sems.at[r], device_id=(x, y, 1 - c), device_id_type=MESH)
            for r in range(N_CHIP)]

    def start(*refs):
        for cp in copies(*refs):
            cp.start()

    def wait(*refs):
        cps = copies(*refs)
        for cp in cps:
            cp.wait_recv()
        for cp in cps:
            cp.wait_send()

    return dict(inputs=[gbig], out_shape=[jax.ShapeDtypeStruct((N_CHIP, R, W), gbig.dtype)],
                sems=[pltpu.SemaphoreType.DMA((N_CHIP,)), pltpu.SemaphoreType.DMA((N_CHIP,))],
                start=start, wait=wait)


def _own_slabs():
    x, y, c = _coords()
    return jnp.stack([4 * (x ^ (r >> 1)) + 2 * (y ^ (r & 1)) + c for r in range(N_CHIP)]).astype(jnp.int32)


def _pair_sum(gbig, sib, own_idx, tr, *, name):
    _, R, W = gbig.shape

    def body(idx_ref, a_ref, b_ref, o_ref):
        o_ref[...] = (a_ref[...] + b_ref[...]).astype(BF16)

    return pl.pallas_call(
        body, name=name,
        grid_spec=pltpu.PrefetchScalarGridSpec(
            num_scalar_prefetch=1, grid=(N_CHIP - 1, R // tr),
            in_specs=[pl.BlockSpec((None, tr, W), lambda r, i, idx: (idx[r + 1], i, 0)),
                      pl.BlockSpec((None, tr, W), lambda r, i, idx: (r + 1, i, 0))],
            out_specs=pl.BlockSpec((None, tr, W), lambda r, i, idx: (r, i, 0))),
        out_shape=jax.ShapeDtypeStruct((N_CHIP - 1, R, W), BF16),
        compiler_params=_cp(("parallel", "parallel")))(own_idx, gbig, sib)


def _chips_comm(send):
    nb, R, W = send.shape

    def copies(b_ref, rb_ref, send_sems, recv_sems):
        x, y, c = _coords()
        return [pltpu.make_async_remote_copy(
            src_ref=b_ref.at[r - 1], dst_ref=rb_ref.at[r - 1], send_sem=send_sems.at[r - 1],
            recv_sem=recv_sems.at[r - 1], device_id=(x ^ (r >> 1), y ^ (r & 1), c), device_id_type=MESH)
            for r in range(1, N_CHIP)]

    def start(*refs):
        for cp in copies(*refs):
            cp.start()

    def wait(*refs):
        cps = copies(*refs)
        for cp in cps:
            cp.wait_recv()
        for cp in cps:
            cp.wait_send()

    return dict(inputs=[send], out_shape=[jax.ShapeDtypeStruct((nb, R, W), send.dtype)],
                sems=[pltpu.SemaphoreType.DMA((nb,)), pltpu.SemaphoreType.DMA((nb,))],
                start=start, wait=wait)


def _gather_small(gsmall, *, name):
    n = N_DEV - 1

    def body(s_ref, rs_ref, send_sems, recv_sems, local_sem):
        x, y, c = _coords()
        me = 4 * x + 2 * y + c
        mine = pltpu.make_async_copy(s_ref, rs_ref.at[me], local_sem)
        mine.start()

        def copy(k, fx, fy, fc, slot):
            return pltpu.make_async_remote_copy(
                src_ref=s_ref, dst_ref=rs_ref.at[slot], send_sem=send_sems.at[k], recv_sem=recv_sems.at[k],
                device_id=(x ^ fx, y ^ fy, c ^ fc), device_id_type=MESH)

        started = [copy(k, *rel, me) for k, rel in enumerate(_relations())]
        for cp in started:
            cp.start()
        for k, (fx, fy, fc) in enumerate(_relations()):
            copy(k, fx, fy, fc, 4 * (x ^ fx) + 2 * (y ^ fy) + (c ^ fc)).wait_recv()
        for cp in started:
            cp.wait_send()
        mine.wait()

    return pl.pallas_call(
        body, name=name, out_shape=jax.ShapeDtypeStruct((N_DEV, 1, P_SMALL), gsmall.dtype),
        in_specs=[ANY], out_specs=ANY,
        scratch_shapes=[pltpu.SemaphoreType.DMA((n,)), pltpu.SemaphoreType.DMA((n,)), pltpu.SemaphoreType.DMA],
    )(gsmall)


def _part_specs(parts, tr, row0):
    assert row0 % tr == 0
    specs = []
    for a, n_used in parts:
        if n_used is None:
            specs.append(pl.BlockSpec((1, tr, a.shape[2]), lambda i, idx: (idx[0], row0 // tr + i, 0)))
        else:
            specs.append(pl.BlockSpec((n_used, tr, a.shape[2]), lambda i, idx: (0, row0 // tr + i, 0)))
    return specs


def _part_total(refs, parts):
    g = None
    for ref, (_, n_used) in zip(refs, parts):
        for k in range(n_used or 1):
            t = ref[k].astype(F32)
            g = t if g is None else g + t
    return g


def _sum_parts(parts, idx, row0, nrows, tr, *, name):
    W = parts[0][0].shape[2]
    assert nrows % tr == 0

    def body(idx_ref, *refs):
        refs[-1][...] = _part_total(refs[:-1], parts)

    return pl.pallas_call(
        body, name=name,
        grid_spec=pltpu.PrefetchScalarGridSpec(
            num_scalar_prefetch=1, grid=(nrows // tr,), in_specs=_part_specs(parts, tr, row0),
            out_specs=pl.BlockSpec((tr, W), lambda i, idx: (i, 0))),
        out_shape=jax.ShapeDtypeStruct((nrows, W), F32),
        compiler_params=_cp(("parallel",)))(idx, *[a for a, _ in parts])


def _adamw(parts, idx, w, m, v, tr, *, name):
    R, W = w.shape
    assert R % tr == 0
    np_ = len(parts)

    def body(idx_ref, *refs):
        w_ref, m_ref, v_ref, g_ref, d_ref, nm_ref, nv_ref = refs[np_:]
        g = _part_total(refs[:np_], parts)
        mm = ADAM_B1 * m_ref[...] + (1.0 - ADAM_B1) * g
        vv = ADAM_B2 * v_ref[...] + (1.0 - ADAM_B2) * (g * g)
        m_hat = mm / (1.0 - ADAM_B1 ** ADAM_STEP)
        v_hat = vv / (1.0 - ADAM_B2 ** ADAM_STEP)
        g_ref[...] = g
        d_ref[...] = -ADAM_LR * (m_hat / (jnp.sqrt(v_hat) + ADAM_EPS) + ADAM_WD * w_ref[...])
        nm_ref[...] = mm
        nv_ref[...] = vv

    blk = pl.BlockSpec((tr, W), lambda i, idx: (i, 0))
    return pl.pallas_call(
        body, name=name,
        grid_spec=pltpu.PrefetchScalarGridSpec(
            num_scalar_prefetch=1, grid=(R // tr,), in_specs=_part_specs(parts, tr, 0) + [blk, blk, blk],
            out_specs=[blk] * 4),
        out_shape=[jax.ShapeDtypeStruct((R, W), F32)] * 4,
        compiler_params=_cp(("parallel",)))(idx, *[a for a, _ in parts], w, m, v)


def _pack_rest(w_kv, wa, wb, wm, w_out):
    return jnp.concatenate([w_kv[0], w_out[0]] + [t[0].reshape(-1, D_MODEL) for t in (wa, wb, wm)], axis=0)


def _unpack_rest(t):
    br = lambda i: t[RO_BR + 64 * i:RO_BR + 64 * (i + 1)].reshape(1, A_WIDTH, D_MODEL // N_DEV)
    return t[None, RO_KV:RO_OUT], br(0), br(1), br(2), t[None, RO_OUT:RO_BR]


def _orig_rows(gathered, a, b):
    res = []
    while a < b:
        dev, r = divmod(a, CS)
        n = min(b - a, CS - r)
        res.append(gathered[dev, RO_IN + r:RO_IN + r + n])
        a += n
    return res


def _full_weights(gathered):
    wt = {}
    for name, ranges in SEGS.items():
        rows = [p for a, b in ranges for p in _orig_rows(gathered, a, b)]
        if SEG_PAD[name]:
            rows.append(jnp.zeros((SEG_PAD[name], D_MODEL), gathered.dtype))
        wt[name] = jnp.concatenate(rows, axis=0)
    w_kv = gathered[:, RO_KV:RO_OUT].reshape(D_MODEL, D_MODEL)
    w_out = gathered[:, RO_OUT:RO_BR].reshape(D_MODEL, D_MODEL)
    wbs = [gathered[:, RO_BR + 64 * i:RO_BR + 64 * (i + 1)].reshape(N_DEV, A_WIDTH, D_MODEL // N_DEV)
           .transpose(1, 0, 2).reshape(A_WIDTH, D_MODEL) for i in range(3)]
    return wt, w_kv, wbs, w_out


def _orig_order(dwt):
    pieces = []
    for name, ranges in SEGS.items():
        o = 0
        for a, b in ranges:
            pieces.append((a, dwt[name][o:o + b - a]))
            o += b - a
    pieces.sort(key=lambda p: p[0])
    return jnp.concatenate([p[1] for p in pieces], axis=0)


def _pack_grads(dwt, dw_kv, dwbs, dw_out):
    g_in = jnp.pad(_orig_order(dwt).reshape(N_DEV, CS, D_MODEL), ((0, 0), (0, IN_ROWS - CS), (0, 0)))
    br = [t.reshape(A_WIDTH, N_DEV, D_MODEL // N_DEV).transpose(1, 0, 2).reshape(N_DEV, -1, D_MODEL) for t in dwbs]
    return jnp.concatenate([dw_kv.reshape(N_DEV, -1, D_MODEL), dw_out.reshape(N_DEV, -1, D_MODEL)] + br + [g_in],
                           axis=1)


def kernel(x, mem, positions, norm_pre_g, norm_post_g, norm_mem_g, w_in, b_forget, b_merge, w_mem_kv, w_branch_a, w_branch_b, w_branch_m, w_out, loss_target, m_norm_pre_g, m_norm_post_g, m_norm_mem_g, m_w_in, m_b_forget, m_b_merge, m_w_mem_kv, m_w_branch_a, m_w_branch_b, m_w_branch_m, m_w_out, v_norm_pre_g, v_norm_post_g, v_norm_mem_g, v_w_in, v_b_forget, v_b_merge, v_w_mem_kv, v_w_branch_a, v_w_branch_b, v_w_branch_m, v_w_out):
    w_rest = _pack_rest(w_mem_kv, w_branch_a, w_branch_b, w_branch_m, w_out)
    shard = jnp.concatenate([w_rest.astype(BF16), w_in[0].T.astype(BF16),
                             jnp.zeros((IN_ROWS - CS, D_MODEL), BF16)], axis=0)
    hs, (gathered,) = _rms_fwd(x[0], norm_pre_g, name="rms_pre_gather", dilations=DIL, comm=_gather_comm(shard))
    wt, w_kv, wbs, w_o = _full_weights(gathered)

    bf_pad = jnp.pad(b_forget, ((0, 0), (0, FB_PAD - B_HEADS)))
    r = _local_step(x[0], mem[0], positions[0], loss_target[0], norm_pre_g, norm_post_g, norm_mem_g,
                    wt, bf_pad, b_merge, w_kv, wbs, w_o, pack=_pack_grads, hs=hs)

    gsmall = jnp.concatenate([r["dg_pre"], r["dg_post"], r["dg_mem"], r["db_merge"],
                              r["db_forget"][:, :LANES], r["loss"]], axis=1)
    rsmall = _gather_small(gsmall, name="gather_small")
    parts, own_idx = r["parts"], r["own_idx"]

    m_rest = _pack_rest(m_w_mem_kv, m_w_branch_a, m_w_branch_b, m_w_branch_m, m_w_out)
    v_rest = _pack_rest(v_w_mem_kv, v_w_branch_a, v_w_branch_b, v_w_branch_m, v_w_out)
    outs_rest = [_unpack_rest(t) for t in _adamw(parts, own_idx, w_rest, m_rest, v_rest, 64, name="adamw_rest")]
    g_in = _sum_parts(parts, own_idx, RO_IN, IN_ROWS, 16, name="sum_w_in")[:CS].T
    outs_in = _adamw([(g_in[None], 1)], own_idx, w_in[0], m_w_in[0], v_w_in[0], 128, name="adamw_w_in")

    def small_vec(a, b, c, d, e):
        z = jnp.zeros((1, LANES - B_HEADS), F32)
        return jnp.concatenate([a, b, c, d, e, z, jnp.zeros((1, LANES), F32)], axis=1)

    outs_small = _adamw([(rsmall, N_DEV)], own_idx, small_vec(norm_pre_g, norm_post_g, norm_mem_g, b_merge, b_forget),
                        small_vec(m_norm_pre_g, m_norm_post_g, m_norm_mem_g, m_b_merge, m_b_forget),
                        small_vec(v_norm_pre_g, v_norm_post_g, v_norm_mem_g, v_b_merge, v_b_forget),
                        1, name="adamw_small")

    def small_parts(t):
        return [t[:, O_GPRE:O_GPRE + D_MODEL], t[:, O_GPOST:O_GPOST + D_MODEL], t[:, O_GMEM:O_GMEM + D_MODEL],
                t[:, O_BF:O_BF + B_HEADS], t[:, O_BM:O_BM + 3 * D_MODEL]]

    loss = outs_small[0][0, O_LOSS]
    result = [loss, r["grad_x"][None]]
    for rest, w_i, small in zip(outs_rest, outs_in, outs_small):
        gp, gq, gm, bf, bm = small_parts(small)
        w_k, w_a, w_b, w_m, w_ot = rest
        result += [gp, gq, gm, w_i[None], bf, bm, w_k, w_a, w_b, w_m, w_ot]
    return tuple(result)
```

```python
import jax
import jax.numpy as jnp
from jax import lax
from jax.experimental import pallas as pl
from jax.experimental.pallas import tpu as pltpu

F32 = jnp.float32
BF16 = jnp.bfloat16

N_DEV = 8
D_MODEL = 1024
N_MEM = 256
EPS = 1e-6
NEG = -1e30
ROPE_THETA = 500000.0
DIL = (1, 4, 16)
A_HEADS = 4
HEAD = 128
A_WIDTH = 512
B_HEADS = 8
B_HEAD = 64
M_HEADS = 4
ROT = 32
IN_COLS = 11272
FB_PAD = 256

SEGS = {
    "A0": ((0, 512), (1536, 2048), (3072, 3584)),
    "A1": ((512, 1024), (2048, 2560), (3584, 4096)),
    "A2": ((1024, 1536), (2560, 3072), (4096, 4608)),
    "B": ((5120, 6656),),
    "R": ((4608, 5120), (6664, 7176), (7176, 7688), (7688, 8200), (8200, 11272), (6656, 6664)),
}
SEG_PAD = {"A0": 0, "A1": 0, "A2": 0, "B": 0, "R": FB_PAD - B_HEADS}
R_ZA, R_ZB, R_QM, R_ZM, R_GL, R_FB = 0, 512, 1024, 1536, 2048, 5120
NR = R_FB + FB_PAD

ADAM_LR, ADAM_B1, ADAM_B2, ADAM_EPS, ADAM_WD, ADAM_STEP = 0.001, 0.9, 0.999, 1e-08, 0.01, 10

LANES = 128
VMEM_LIMIT = 56 * 1024 * 1024

CS = IN_COLS // N_DEV
RO_KV, RO_OUT, RO_BR, RO_IN = 0, 128, 256, 448
IN_ROWS = 1424
ROWS = RO_IN + IN_ROWS
O_GPRE, O_GPOST, O_GMEM, O_BM, O_BF, O_LOSS = 0, 1024, 2048, 3072, 6144, 6272
P_SMALL = 6400


def _cp(sem=None):
    return pltpu.CompilerParams(dimension_semantics=sem, vmem_limit_bytes=VMEM_LIMIT)


def _dot(a, b):
    return jnp.dot(a, b, preferred_element_type=F32)


def _dot_nt(a, b):
    return lax.dot_general(a, b, (((1,), (1,)), ((), ())), preferred_element_type=F32)


def _sigmoid(z):
    return 1.0 / (1.0 + jnp.exp(-z))


def _mm(a, b, *, name, at=False, bt=False, out_dtype=F32, tm=1024, tn=1024, tk=None, comm=None):
    assert not (at and bt)
    K, M = a.shape if at else a.shape[::-1]
    N = b.shape[0] if bt else b.shape[1]
    tm, tn = min(tm, M), min(tn, N)
    tk = K if tk is None else min(tk, K)
    assert M % tm == 0 and N % tn == 0 and K % tk == 0
    nk = K // tk
    grid = (M // tm, N // tn, nk)
    n_in = len(comm["inputs"]) if comm else 0
    n_out = len(comm["out_shape"]) if comm else 0

    def body(a_ref, b_ref, *rest):
        c_in, o_ref, c_out = rest[:n_in], rest[n_in], rest[n_in + 1:n_in + 1 + n_out]
        acc_ref, sems = rest[n_in + 1 + n_out], rest[n_in + 2 + n_out:]
        if comm:
            step = (pl.program_id(0) * grid[1] + pl.program_id(1)) * grid[2] + pl.program_id(2)

            @pl.when(step == 0)
            def _():
                comm["start"](*c_in, *c_out, *sems)

        av = a_ref[...].astype(BF16)
        bv = b_ref[...].astype(BF16)
        if at:
            p = lax.dot_general(av, bv, (((0,), (0,)), ((), ())), preferred_element_type=F32)
        else:
            p = _dot_nt(av, bv) if bt else _dot(av, bv)
        if nk == 1:
            o_ref[...] = p.astype(out_dtype)
        else:
            k = pl.program_id(2)

            @pl.when(k == 0)
            def _():
                acc_ref[...] = p

            @pl.when(k > 0)
            def _():
                acc_ref[...] += p

            @pl.when(k == nk - 1)
            def _():
                o_ref[...] = acc_ref[...].astype(out_dtype)

        if comm:
            @pl.when(step == grid[0] * grid[1] * grid[2] - 1)
            def _():
                comm["wait"](*c_in, *c_out, *sems)

    b_spec = (pl.BlockSpec((tn, tk), lambda i, j, k: (j, k)) if bt
              else pl.BlockSpec((tk, tn), lambda i, j, k: (k, j)))
    a_spec = (pl.BlockSpec((tk, tm), lambda i, j, k: (k, i)) if at
              else pl.BlockSpec((tm, tk), lambda i, j, k: (i, k)))
    out_spec = pl.BlockSpec((tm, tn), lambda i, j, k: (i, j))
    out_shape = jax.ShapeDtypeStruct((M, N), out_dtype)
    acc = pltpu.VMEM((tm, tn) if nk > 1 else (8, LANES), F32)
    if not comm:
        return pl.pallas_call(
            body, name=name, grid=grid, in_specs=[a_spec, b_spec], out_specs=out_spec, out_shape=out_shape,
            scratch_shapes=[acc], compiler_params=_cp(("parallel", "parallel", "arbitrary")))(a, b)
    return pl.pallas_call(
        body, name=name, grid=grid, in_specs=[a_spec, b_spec] + [ANY] * n_in,
        out_specs=[out_spec] + [ANY] * n_out, out_shape=[out_shape] + comm["out_shape"],
        scratch_shapes=[acc] + comm["sems"],
        compiler_params=_cp(("arbitrary", "arbitrary", "arbitrary")))(a, b, *comm["inputs"])


def _mm_sum(pairs, *, name, tm=1024, tk=768, comm=None):
    M, N = pairs[0][0].shape[0], pairs[0][1].shape[1]
    tm = min(tm, M)
    steps = [a.shape[1] // tk for a, _ in pairs]
    assert M % tm == 0 and all(a.shape[1] % tk == 0 for a, _ in pairs)
    first = [sum(steps[:p]) for p in range(len(pairs))]
    total = sum(steps)
    grid = (M // tm, total)
    n_in = len(comm["inputs"]) if comm else 0
    n_out = len(comm["out_shape"]) if comm else 0
    npair = len(pairs)

    def body(*refs):
        ab, rest = refs[:2 * npair], refs[2 * npair:]
        c_in, o_ref, c_out = rest[:n_in], rest[n_in], rest[n_in + 1:n_in + 1 + n_out]
        acc_ref, sems = rest[n_in + 1 + n_out], rest[n_in + 2 + n_out:]
        k = pl.program_id(1)
        if comm:
            step = pl.program_id(0) * total + k

            @pl.when(step == 0)
            def _():
                comm["start"](*c_in, *c_out, *sems)

        @pl.when(k == 0)
        def _():
            acc_ref[...] = jnp.zeros((tm, N), F32)

        for p in range(npair):
            @pl.when(jnp.logical_and(k >= first[p], k < first[p] + steps[p]))
            def _(p=p):
                acc_ref[...] += _dot(ab[2 * p][...], ab[2 * p + 1][...])

        @pl.when(k == total - 1)
        def _():
            o_ref[...] = acc_ref[...]

        if comm:
            @pl.when(step == grid[0] * total - 1)
            def _():
                comm["wait"](*c_in, *c_out, *sems)

    def local(p):
        return lambda k: jnp.clip(k - first[p], 0, steps[p] - 1)

    in_specs = []
    for p in range(npair):
        in_specs += [pl.BlockSpec((tm, tk), lambda i, k, f=local(p): (i, f(k))),
                     pl.BlockSpec((tk, N), lambda i, k, f=local(p): (f(k), 0))]
    out_spec = pl.BlockSpec((tm, N), lambda i, k: (i, 0))
    out_shape = jax.ShapeDtypeStruct((M, N), F32)
    args = [t for pair in pairs for t in pair]
    if not comm:
        return pl.pallas_call(
            body, name=name, grid=grid, in_specs=in_specs, out_specs=out_spec, out_shape=out_shape,
            scratch_shapes=[pltpu.VMEM((tm, N), F32)], compiler_params=_cp(("parallel", "arbitrary")))(*args)
    return pl.pallas_call(
        body, name=name, grid=grid, in_specs=in_specs + [ANY] * n_in,
        out_specs=[out_spec] + [ANY] * n_out, out_shape=[out_shape] + comm["out_shape"],
        scratch_shapes=[pltpu.VMEM((tm, N), F32)] + comm["sems"],
        compiler_params=_cp(("arbitrary", "arbitrary")))(*args, *comm["inputs"])


def _class_spec(S, d, tm, width):
    return pl.BlockSpec((d, tm // d, width), lambda i: (0, i, 0))


def _rms_fwd(x, g, *, name, dilations=(), comm=None):
    S, D = x.shape
    tm = min(512, S)
    ds = [d for d in dilations if d > 1]
    nsteps = S // tm
    n_in = len(comm["inputs"]) if comm else 0
    n_out = len(comm["out_shape"]) if comm else 0
    n_tmp = D // LANES if ds else 0

    def body(x_ref, g_ref, *rest):
        c_in, o_ref, rest = rest[:n_in], rest[n_in], rest[n_in + 1:]
        cls, c_out, rest = rest[:len(ds)], rest[len(ds):len(ds) + n_out], rest[len(ds) + n_out:]
        tmps, sems = rest[:n_tmp], rest[n_tmp:]
        if comm:
            @pl.when(pl.program_id(0) == 0)
            def _():
                comm["start"](*c_in, *c_out, *sems)

        xv = x_ref[...]
        r = lax.rsqrt(jnp.mean(xv * xv, axis=-1, keepdims=True) + EPS)
        hv = xv * r * g_ref[...]
        o_ref[...] = hv.astype(BF16)
        if ds:
            for c, tmp in enumerate(tmps):
                tmp[...] = hv[:, c * LANES:(c + 1) * LANES]
            for c_ref, d in zip(cls, ds):
                for k in range(d):
                    c_ref[k] = jnp.concatenate([tmp[pl.ds(k, tm // d, stride=d), :] for tmp in tmps],
                                               axis=1).astype(BF16)
        if comm:
            @pl.when(pl.program_id(0) == nsteps - 1)
            def _():
                comm["wait"](*c_in, *c_out, *sems)

    row = pl.BlockSpec((tm, D), lambda i: (i, 0))
    outs = pl.pallas_call(
        body, name=name, grid=(nsteps,),
        in_specs=[row, pl.BlockSpec((1, D), lambda i: (0, 0))] + [ANY] * n_in,
        out_specs=[row] + [_class_spec(S, d, tm, D) for d in ds] + [ANY] * n_out,
        out_shape=[jax.ShapeDtypeStruct((S, D), BF16)] + [jax.ShapeDtypeStruct((d, S // d, D), BF16) for d in ds]
        + (comm["out_shape"] if comm else []),
        scratch_shapes=[pltpu.VMEM((tm, LANES), F32)] * n_tmp + (comm["sems"] if comm else []),
        compiler_params=_cp(("arbitrary",) if comm else ("parallel",)),
    )(x, g, *(comm["inputs"] if comm else []))
    rows = [outs[0]] + [o.reshape(S, D) for o in outs[1:1 + len(ds)]]
    if comm:
        return rows, list(outs[1 + len(ds):])
    return rows if ds else rows[0]


def _rms_bwd(x, g, dh, dy, *, name, dh_classes=()):
    S, D = x.shape
    tm = min(512, S)
    want_dx = dy is not None
    nc = len(dh_classes)

    def body(*refs):
        c_refs, refs = refs[:nc], refs[nc:]
        if want_dx:
            x_ref, g_ref, dh_ref, dy_ref, dx_ref, dg_ref = refs[:6]
        else:
            x_ref, g_ref, dh_ref, dg_ref = refs[:4]
        i = pl.program_id(0)
        xv = x_ref[...]
        r = lax.rsqrt(jnp.mean(xv * xv, axis=-1, keepdims=True) + EPS)
        xh = xv * r
        if nc:
            tmps = refs[-(D // LANES):]
            cols = [slice(c * LANES, (c + 1) * LANES) for c in range(D // LANES)]
            for tmp, cs in zip(tmps, cols):
                tmp[...] = dh_ref[:, cs]
            for c_ref, (_, d) in zip(c_refs, dh_classes):
                for k in range(d):
                    for tmp, cs in zip(tmps, cols):
                        tmp[pl.ds(k, tm // d, stride=d), :] += c_ref[k, :, cs]
            dhv = jnp.concatenate([tmp[...] for tmp in tmps], axis=1)
        else:
            dhv = dh_ref[...]
        part = jnp.sum(dhv * xh, axis=0, keepdims=True)

        @pl.when(i == 0)
        def _():
            dg_ref[...] = part

        @pl.when(i > 0)
        def _():
            dg_ref[...] += part

        if want_dx:
            dxh = dhv * g_ref[...]
            dx_ref[...] = dy_ref[...] + r * (dxh - xh * jnp.mean(dxh * xh, axis=-1, keepdims=True))

    row = pl.BlockSpec((tm, D), lambda i: (i, 0))
    vec = pl.BlockSpec((1, D), lambda i: (0, 0))
    c_specs = [_class_spec(S, d, tm, D) for _, d in dh_classes]
    c_args = [a.reshape(d, S // d, D) for a, d in dh_classes]
    scratch = [pltpu.VMEM((tm, LANES), F32)] * (D // LANES) if nc else []
    if want_dx:
        return pl.pallas_call(
            body, name=name, grid=(S // tm,), in_specs=c_specs + [row, vec, row, row], out_specs=[row, vec],
            out_shape=[jax.ShapeDtypeStruct((S, D), F32), jax.ShapeDtypeStruct((1, D), F32)],
            scratch_shapes=scratch, compiler_params=_cp(("arbitrary",)))(*c_args, x, g, dh, dy)
    return pl.pallas_call(
        body, name=name, grid=(S // tm,), in_specs=c_specs + [row, vec, row], out_specs=vec,
        out_shape=jax.ShapeDtypeStruct((1, D), F32),
        scratch_shapes=scratch, compiler_params=_cp(("arbitrary",)))(*c_args, x, g, dh)


def _post(x, out, tgt, g, *, name):
    S, D = x.shape
    tm = min(512, S)

    def body(x_ref, o_ref, t_ref, g_ref, dy_ref, do_ref, dg_ref, loss_ref):
        i = pl.program_id(0)
        ov = o_ref[...]
        r = lax.rsqrt(jnp.mean(ov * ov, axis=-1, keepdims=True) + EPS)
        n = ov * r
        gv = g_ref[...]
        e = (x_ref[...] + n * gv) - t_ref[...]
        lpart = 0.5 * jnp.sum(jnp.mean(e * e, axis=-1, keepdims=True), axis=0, keepdims=True)
        dy = e * (1.0 / D)
        dy_ref[...] = dy
        dn = dy * gv
        do_ref[...] = (r * (dn - n * jnp.mean(dn * n, axis=-1, keepdims=True))).astype(BF16)
        gpart = jnp.sum(dy * n, axis=0, keepdims=True)
        lrow = jnp.broadcast_to(lpart, (1, LANES))

        @pl.when(i == 0)
        def _():
            dg_ref[...] = gpart
            loss_ref[...] = lrow

        @pl.when(i > 0)
        def _():
            dg_ref[...] += gpart
            loss_ref[...] += lrow

    row = pl.BlockSpec((tm, D), lambda i: (i, 0))
    vec = pl.BlockSpec((1, D), lambda i: (0, 0))
    return pl.pallas_call(
        body, name=name, grid=(S // tm,), in_specs=[row, row, row, vec],
        out_specs=[row, row, vec, pl.BlockSpec((1, LANES), lambda i: (0, 0))],
        out_shape=[jax.ShapeDtypeStruct((S, D), F32), jax.ShapeDtypeStruct((S, D), BF16),
                   jax.ShapeDtypeStruct((1, D), F32), jax.ShapeDtypeStruct((1, LANES), F32)],
        compiler_params=_cp(("arbitrary",)))(x, out, tgt, g)


def _to_classes(t, d):
    if d == 1:
        return t
    S, C = t.shape
    return t.reshape(S // d, d, C).transpose(1, 0, 2).reshape(S, C)


def _rope(x, c, s1, s2):
    return x * c + pltpu.roll(x, LANES - ROT // 2, 1) * s1 + pltpu.roll(x, ROT // 2, 1) * s2


def _unrope(d, c, s1, s2):
    return d * c + pltpu.roll(d * s1, ROT // 2, 1) + pltpu.roll(d * s2, LANES - ROT // 2, 1)


def _a_band(qb):
    r = lax.broadcasted_iota(jnp.int32, (qb, qb + HEAD), 0)
    c = lax.broadcasted_iota(jnp.int32, (qb, qb + HEAD), 1)
    return jnp.logical_and(c >= r, c <= r + HEAD)


def _a_first_ok(qb, n):
    c = lax.broadcasted_iota(jnp.int32, (qb, qb + HEAD), 1)
    return jnp.logical_or(c >= HEAD, n > 0)


def _a_last_ok(qb, has_next):
    c = lax.broadcasted_iota(jnp.int32, (qb, qb + HEAD), 1)
    return jnp.logical_or(c < qb, has_next)


A_SCALE = HEAD ** -0.5


def _a_geometry(S, g):
    d = DIL[g]
    L = S // d
    TQ = min(512, L)
    return d, L, TQ, TQ // HEAD, L // TQ, L // HEAD


def _proj_rope(h, w, tabs, *, name):
    S, D = h.shape
    tm = min(512, S)

    def body(h_ref, w_ref, c_ref, s1_ref, s2_ref, o_ref):
        tc = (c_ref[...], s1_ref[...], s2_ref[...])
        u = _dot_nt(h_ref[...], w_ref[...])
        for j in range(3 * A_HEADS):
            sl = slice(j * HEAD, (j + 1) * HEAD)
            o_ref[:, sl] = (_rope(u[:, sl], *tc) if j < 2 * A_HEADS else u[:, sl]).astype(BF16)

    tab = pl.BlockSpec((tm, LANES), lambda i: (i, 0))
    return pl.pallas_call(
        body, name=name, grid=(S // tm,),
        in_specs=[pl.BlockSpec((tm, D), lambda i: (i, 0)), pl.BlockSpec((3 * A_WIDTH, D), lambda i: (0, 0)),
                  tab, tab, tab],
        out_specs=pl.BlockSpec((tm, 3 * A_WIDTH), lambda i: (i, 0)),
        out_shape=jax.ShapeDtypeStruct((S, 3 * A_WIDTH), BF16),
        compiler_params=_cp(("parallel",)))(h, w, *tabs)


def _attn_a_fwd(qkv, g, *, name):
    S = qkv.shape[0]
    d, L, TQ, nsub, nb, nblk = _a_geometry(S, g)

    def body(q_ref, kc_ref, kp_ref, vc_ref, vp_ref, o_ref, l_ref):
        n = pl.program_id(1)
        QB = min(2 * HEAD, TQ)
        band = _a_band(QB)
        first = jnp.logical_and(band, _a_first_ok(QB, n))
        for h in range(A_HEADS):
            hs = slice(h * HEAD, (h + 1) * HEAD)
            for hh in range(TQ // QB):
                sl = slice(hh * QB, (hh + 1) * QB)
                pv = slice(hh * QB - HEAD, hh * QB)
                kcat = jnp.concatenate([kp_ref[:, hs] if hh == 0 else kc_ref[pv, hs], kc_ref[sl, hs]], axis=0)
                vcat = jnp.concatenate([vp_ref[:, hs] if hh == 0 else vc_ref[pv, hs], vc_ref[sl, hs]], axis=0)
                s = jnp.where(first if hh == 0 else band, _dot_nt(q_ref[sl, hs], kcat) * A_SCALE, NEG)
                m = jnp.max(s, axis=-1, keepdims=True)
                p = jnp.exp(s - m)
                den = jnp.sum(p, axis=-1, keepdims=True)
                o_ref[sl, hs] = _dot(p.astype(BF16), vcat) / den
                l_ref[sl, hs] = jnp.broadcast_to(m + jnp.log(den), (QB, HEAD))

    rcur = lambda r, n: r * nb + n
    rprv = lambda r, n: r * nblk + jnp.maximum(n * nsub - 1, 0)
    cur = lambda off: pl.BlockSpec((TQ, A_WIDTH), lambda r, n: (rcur(r, n), off))
    prv = lambda off: pl.BlockSpec((HEAD, A_WIDTH), lambda r, n: (rprv(r, n), off))
    out = pl.BlockSpec((TQ, A_WIDTH), lambda r, n: (rcur(r, n), 0))
    return pl.pallas_call(
        body, name=name, grid=(d, nb),
        in_specs=[cur(0), cur(1), prv(1), cur(2), prv(2)],
        out_specs=[out, out],
        out_shape=[jax.ShapeDtypeStruct((S, A_WIDTH), F32)] * 2,
        compiler_params=_cp(("parallel", "parallel")),
    )(qkv, qkv, qkv, qkv, qkv)


def _attn_a_dq(qkv, tabs, g, do, lse, adj, du, *, name):
    S = qkv.shape[0]
    d, L, TQ, nsub, nb, nblk = _a_geometry(S, g)

    def body(q_ref, kc_ref, kp_ref, vc_ref, vp_ref, do_ref, l_ref, adj_ref, c_ref, s1_ref, s2_ref, du_ref, dq_ref):
        n = pl.program_id(1)
        QB = min(2 * HEAD, TQ)
        band = _a_band(QB)
        first = jnp.logical_and(band, _a_first_ok(QB, n))
        for h in range(A_HEADS):
            hs = slice(h * HEAD, (h + 1) * HEAD)
            for hh in range(TQ // QB):
                sl = slice(hh * QB, (hh + 1) * QB)
                pv = slice(hh * QB - HEAD, hh * QB)
                kcat = jnp.concatenate([kp_ref[:, hs] if hh == 0 else kc_ref[pv, hs], kc_ref[sl, hs]], axis=0)
                vcat = jnp.concatenate([vp_ref[:, hs] if hh == 0 else vc_ref[pv, hs], vc_ref[sl, hs]], axis=0)
                s = jnp.where(first if hh == 0 else band, _dot_nt(q_ref[sl, hs], kcat) * A_SCALE, NEG)
                p = jnp.exp(s - l_ref[sl, hs][:, :1])
                ds = p * (_dot_nt(do_ref[sl, hs], vcat) + adj_ref[sl, hs][:, :1])
                dq = _dot(ds.astype(BF16), kcat) * A_SCALE
                dq_ref[sl, hs] = _unrope(dq, c_ref[sl, :], s1_ref[sl, :], s2_ref[sl, :]).astype(BF16)

    rcur = lambda r, n: r * nb + n
    rprv = lambda r, n: r * nblk + jnp.maximum(n * nsub - 1, 0)
    cur = lambda off: pl.BlockSpec((TQ, A_WIDTH), lambda r, n: (rcur(r, n), off))
    prv = lambda off: pl.BlockSpec((HEAD, A_WIDTH), lambda r, n: (rprv(r, n), off))
    tcur = pl.BlockSpec((TQ, LANES), lambda r, n: (rcur(r, n), 0))
    blk = cur(0)
    return pl.pallas_call(
        body, name=name, grid=(d, nb),
        in_specs=[cur(0), cur(1), prv(1), cur(2), prv(2), blk, blk, blk, tcur, tcur, tcur, ANY],
        out_specs=blk,
        out_shape=jax.ShapeDtypeStruct((S, 3 * A_WIDTH), BF16),
        input_output_aliases={11: 0},
        compiler_params=_cp(("parallel", "parallel")),
    )(qkv, qkv, qkv, qkv, qkv, do, lse, adj, *tabs, du)


def _attn_a_dkv(qkv, tabs, g, do, lse, adj, *, name):
    S = qkv.shape[0]
    d, L, TQ, nsub, nb, nblk = _a_geometry(S, g)

    def body(qc_ref, qn_ref, kc_ref, vc_ref, doc_ref, don_ref, lc_ref, ln_ref, ac_ref, an_ref,
             c_ref, s1_ref, s2_ref, du_ref):
        n = pl.program_id(1)
        QB = min(2 * HEAD, TQ)
        nh = TQ // QB
        band = _a_band(QB)
        end = jnp.logical_and(band, _a_last_ok(QB, n < nb - 1))
        for h in range(A_HEADS):
            hs = slice(h * HEAD, (h + 1) * HEAD)
            for kh in range(nh):
                sl = slice(kh * QB, (kh + 1) * QB)
                nx = slice((kh + 1) * QB, (kh + 1) * QB + HEAD)
                last = kh == nh - 1
                cat = lambda cur, nxt: jnp.concatenate([cur[sl, hs], nxt[:, hs] if last else cur[nx, hs]], axis=0)
                qcat = cat(qc_ref, qn_ref)
                docat = cat(doc_ref, don_ref)
                lt = cat(lc_ref, ln_ref).T[:1, :]
                at = cat(ac_ref, an_ref).T[:1, :]
                st = jnp.where(end if last else band, _dot_nt(kc_ref[sl, hs], qcat) * A_SCALE, NEG)
                pt = jnp.exp(st - lt)
                dv_cols = slice(2 * A_WIDTH + h * HEAD, 2 * A_WIDTH + (h + 1) * HEAD)
                dk_cols = slice(A_WIDTH + h * HEAD, A_WIDTH + (h + 1) * HEAD)
                du_ref[sl, dv_cols] = _dot(pt.astype(BF16), docat).astype(BF16)
                dst = pt * (_dot_nt(vc_ref[sl, hs], docat) + at)
                dk = _dot(dst.astype(BF16), qcat) * A_SCALE
                du_ref[sl, dk_cols] = _unrope(dk, c_ref[sl, :], s1_ref[sl, :], s2_ref[sl, :]).astype(BF16)

    rcur = lambda r, n: r * nb + n
    rnxt = lambda r, n: r * nblk + jnp.minimum((n + 1) * nsub, nblk - 1)
    cur = lambda off: pl.BlockSpec((TQ, A_WIDTH), lambda r, n: (rcur(r, n), off))
    nxu = lambda off: pl.BlockSpec((HEAD, A_WIDTH), lambda r, n: (rnxt(r, n), off))
    tcur = pl.BlockSpec((TQ, LANES), lambda r, n: (rcur(r, n), 0))
    blk, bnx = cur(0), nxu(0)
    return pl.pallas_call(
        body, name=name, grid=(d, nb),
        in_specs=[cur(0), nxu(0), cur(1), cur(2), blk, bnx, blk, bnx, blk, bnx, tcur, tcur, tcur],
        out_specs=pl.BlockSpec((TQ, 3 * A_WIDTH), lambda r, n: (rcur(r, n), 0)),
        out_shape=jax.ShapeDtypeStruct((S, 3 * A_WIDTH), BF16),
        compiler_params=_cp(("parallel", "parallel")),
    )(qkv, qkv, qkv, qkv, do, do, lse, lse, adj, adj, *tabs)


def _silu_parts(z):
    sg = _sigmoid(z)
    return z * sg, sg * (1.0 + z * (1.0 - sg))


def _classes_to_tokens(c_ref, d, tm, tmps):
    if d == 1:
        return c_ref[...].astype(F32)
    for k in range(d):
        for c, tmp in enumerate(tmps):
            tmp[pl.ds(k, tm // d, stride=d), :] = c_ref[k, :, c * LANES:(c + 1) * LANES].astype(F32)
    return jnp.concatenate([tmp[...] for tmp in tmps], axis=1)


def _tokens_to_classes(val, c_ref, d, tm, tmps):
    if d == 1:
        c_ref[...] = val.astype(c_ref.dtype)
        return
    for c, tmp in enumerate(tmps):
        tmp[...] = val[:, c * LANES:(c + 1) * LANES]
    for k in range(d):
        c_ref[k] = jnp.concatenate([tmp[pl.ds(k, tm // d, stride=d), :] for tmp in tmps], axis=1).astype(c_ref.dtype)


def _group_spec(S, d, tm):
    if d == 1:
        return pl.BlockSpec((tm, A_WIDTH), lambda i: (i, 0))
    return _class_spec(S, d, tm, A_WIDTH)


def _group_view(t, d):
    return t if d == 1 else t.reshape(d, t.shape[0] // d, t.shape[1])


def _merge_a_fwd(os_, ls_, ur, *, name):
    S = ur.shape[0]
    tm = min(512, S)

    def body(o0, o1, o2, l0, l1, l2, z_ref, y_ref, *tmps):
        ls = [_classes_to_tokens(r, d, tm, tmps) for r, d in zip((l0, l1, l2), DIL)]
        ov = [_classes_to_tokens(r, d, tm, tmps) for r, d in zip((o0, o1, o2), DIL)]
        mx = jnp.maximum(jnp.maximum(ls[0], ls[1]), ls[2])
        es = [jnp.exp(l - mx) for l in ls]
        den = es[0] + es[1] + es[2]
        y = (es[0] / den) * ov[0] + (es[1] / den) * ov[1] + (es[2] / den) * ov[2]
        y_ref[...] = (y * _silu_parts(z_ref[...])[0]).astype(BF16)

    blk = pl.BlockSpec((tm, A_WIDTH), lambda i: (i, 0))
    groups = [_group_spec(S, d, tm) for d in DIL]
    return pl.pallas_call(
        body, name=name, grid=(S // tm,),
        in_specs=groups + groups + [pl.BlockSpec((tm, A_WIDTH), lambda i: (i, R_ZA // A_WIDTH))],
        out_specs=blk, out_shape=jax.ShapeDtypeStruct((S, A_WIDTH), BF16),
        scratch_shapes=[pltpu.VMEM((tm, LANES), F32)] * (A_WIDTH // LANES),
        compiler_params=_cp(("parallel",)))(*[_group_view(t, d) for t, d in zip(os_, DIL)],
                                            *[_group_view(t, d) for t, d in zip(ls_, DIL)], ur)


def _merge_a_bwd(os_, ls_, ur, dya, du_r, *, name):
    S = ur.shape[0]
    tm = min(256, S)

    def body(o0, o1, o2, l0, l1, l2, z_ref, dy_ref, du_in, d0, d1, d2, a0, a1, a2, dz_ref, *tmps):
        ls = [_classes_to_tokens(r, d, tm, tmps) for r, d in zip((l0, l1, l2), DIL)]
        ov = [_classes_to_tokens(r, d, tm, tmps) for r, d in zip((o0, o1, o2), DIL)]
        mx = jnp.maximum(jnp.maximum(ls[0], ls[1]), ls[2])
        es = [jnp.exp(l - mx) for l in ls]
        den = es[0] + es[1] + es[2]
        ws = [e / den for e in es]
        y = ws[0] * ov[0] + ws[1] * ov[1] + ws[2] * ov[2]
        sz, dsz = _silu_parts(z_ref[...])
        dyv = dy_ref[...]
        dz_ref[...] = (dyv * y * dsz).astype(BF16)
        dyp = dyv * sz
        ts = []
        for h in range(A_HEADS):
            sl = slice(h * HEAD, (h + 1) * HEAD)
            t = jnp.zeros((tm, 1), F32)
            for gi in range(3):
                t = t + ws[gi][:, sl][:, :1] * jnp.sum(dyp[:, sl] * ov[gi][:, sl], axis=-1, keepdims=True)
            ts.append(jnp.broadcast_to(t, (tm, HEAD)))
        tb = jnp.concatenate(ts, axis=1)
        for gi, (dref, aref) in enumerate(((d0, a0), (d1, a1), (d2, a2))):
            _tokens_to_classes(ws[gi] * dyp, dref, DIL[gi], tm, tmps)
            _tokens_to_classes(-ws[gi] * tb, aref, DIL[gi], tm, tmps)

    blk = pl.BlockSpec((tm, A_WIDTH), lambda i: (i, 0))
    groups = [_group_spec(S, d, tm) for d in DIL]
    shaped = lambda dt: [jax.ShapeDtypeStruct((S, A_WIDTH) if d == 1 else (d, S // d, A_WIDTH), dt) for d in DIL]
    outs = pl.pallas_call(
        body, name=name, grid=(S // tm,),
        in_specs=groups + groups + [pl.BlockSpec((tm, A_WIDTH), lambda i: (i, R_ZA // A_WIDTH)), blk, ANY],
        out_specs=groups + groups + [pl.BlockSpec((tm, A_WIDTH), lambda i: (i, R_ZA // A_WIDTH))],
        out_shape=shaped(BF16) + shaped(F32) + [jax.ShapeDtypeStruct(du_r.shape, BF16)],
        input_output_aliases={8: 6},
        scratch_shapes=[pltpu.VMEM((tm, LANES), F32)] * (A_WIDTH // LANES),
        compiler_params=_cp(("parallel",)))(*[_group_view(t, d) for t, d in zip(os_, DIL)],
                                            *[_group_view(t, d) for t, d in zip(ls_, DIL)], ur, dya, du_r)
    flat = [t.reshape(S, A_WIDTH) for t in outs[:6]]
    return flat[0:3], flat[3:6], outs[6]


def _logf(ur, bf_pad, *, name):
    S = ur.shape[0]
    tm = min(1024, S)

    def body(u_ref, b_ref, o_ref):
        z = u_ref[...] + b_ref[...]
        o_ref[...] = jnp.minimum(z, 0.0) - jnp.log(1.0 + jnp.exp(-jnp.abs(z)))

    return pl.pallas_call(
        body, name=name, grid=(S // tm,),
        in_specs=[pl.BlockSpec((tm, FB_PAD), lambda i: (i, R_FB // FB_PAD)),
                  pl.BlockSpec((1, FB_PAD), lambda i: (0, 0))],
        out_specs=pl.BlockSpec((tm, FB_PAD), lambda i: (i, 0)),
        out_shape=jax.ShapeDtypeStruct((S, FB_PAD), F32),
        compiler_params=_cp(("parallel",)))(ur, bf_pad)


def _cumsum_lanes(x, reverse, *, name):
    nt, H, _ = x.shape
    R = nt * H

    def body(x_ref, o_ref):
        v = x_ref[...].reshape(R, LANES)
        lane = lax.broadcasted_iota(jnp.int32, (R, LANES), 1)
        row = lax.broadcasted_iota(jnp.int32, (R, LANES), 0)

        def scan(t, step, idx, n, axis):
            while step < n:
                if reverse:
                    t = t + jnp.where(idx < n - step, pltpu.roll(t, n - step, axis), 0.0)
                else:
                    t = t + jnp.where(idx >= step, pltpu.roll(t, step, axis), 0.0)
                step *= 2
            return t

        v = scan(v, 1, lane, LANES, 1)
        total = jnp.broadcast_to(v[:, :1] if reverse else v[:, LANES - 1:], (R, LANES))
        carry = scan(total, H, row, R, 0) - total
        o_ref[...] = (v + carry).reshape(nt, H, LANES)

    return pl.pallas_call(
        body, name=name, out_shape=jax.ShapeDtypeStruct((nt, H, LANES), F32),
        in_specs=[pl.BlockSpec(memory_space=pltpu.VMEM)], out_specs=pl.BlockSpec(memory_space=pltpu.VMEM),
        compiler_params=_cp())(x)


B_SCALE = B_HEAD ** -0.5


def _pair_masks():
    lane = lax.broadcasted_iota(jnp.int32, (1, LANES), 1)
    row = lax.broadcasted_iota(jnp.int32, (LANES, 1), 0)
    return (lane < B_HEAD, lane >= B_HEAD), (row < B_HEAD, row >= B_HEAD)


def _causal_t(T):
    r = lax.broadcasted_iota(jnp.int32, (T, T), 0)
    c = lax.broadcasted_iota(jnp.int32, (T, T), 1)
    return r <= c


def _zero_other(x, keep):
    return jnp.where(keep, x, jnp.zeros_like(x))


def _fox_aug(ub, c, *, name):
    S = ub.shape[0]
    T = min(2048, S)

    def body(q_ref, k_ref, c_ref, qa_ref, ka_ref):
        lane = lax.broadcasted_iota(jnp.int32, (1, LANES), 1)
        q = q_ref[...] * B_SCALE
        k = k_ref[...]
        for a in range(2):
            own = (lane < B_HEAD) if a == 0 else (lane >= B_HEAD)
            o = B_HEAD if a == 0 else 0
            cv = jnp.broadcast_to(c_ref[:, a:a + 1], (T, LANES))
            hi = cv.astype(BF16)
            r1 = cv - hi.astype(F32)
            mid = r1.astype(BF16)
            lo = (r1 - mid.astype(F32)).astype(BF16)
            pieces = (hi, mid, lo)
            one = jnp.ones((T, LANES), BF16)
            qa = jnp.where(own, q, jnp.zeros_like(q))
            ka = jnp.where(own, k, jnp.zeros_like(k))
            for t in range(3):
                qa = jnp.where(lane == o + t, pieces[t], qa)
                qa = jnp.where(lane == o + 3 + t, one, qa)
                ka = jnp.where(lane == o + t, one, ka)
                ka = jnp.where(lane == o + 3 + t, -pieces[t], ka)
            qa_ref[a] = qa
            ka_ref[a] = ka

    out = pl.BlockSpec((2, T, LANES), lambda h, i: (h, i, 0))
    c_pairs = c.reshape(B_HEADS // 2, 2, S).transpose(0, 2, 1)
    return pl.pallas_call(
        body, name=name, grid=(B_HEADS // 2, S // T),
        in_specs=[pl.BlockSpec((T, LANES), lambda h, i: (i, h)), pl.BlockSpec((T, LANES), lambda h, i: (i, 4 + h)),
                  pl.BlockSpec((None, T, 2), lambda h, i: (h, i, 0))],
        out_specs=[out, out], out_shape=[jax.ShapeDtypeStruct((B_HEADS, S, LANES), BF16)] * 2,
        compiler_params=_cp(("parallel", "parallel")))(ub, ub, c_pairs)


def _fox_fwd(qaug, kaug, vt, *, name):
    S = qaug.shape[1]
    T = min(512, S)
    nq = S // T

    def body(q_ref, k_ref, vt_ref, o_ref, l_ref, m_s, l_s, acc_s, st_s):
        i = pl.program_id(1)
        _, rows = _pair_masks()
        qm = [q_ref[0], q_ref[1]]
        m_s[...] = jnp.full((2, 1, T), NEG, F32)
        l_s[...] = jnp.zeros((2, 1, T), F32)
        acc_s[...] = jnp.zeros((LANES, T), F32)

        def logits(j):
            off = pl.multiple_of(j * T, T)
            return [_dot_nt(k_ref[a, pl.ds(off, T), :], qm[a]) for a in range(2)]

        def step(j, masked, prefetch):
            nxt = logits(j + 1) if prefetch else None
            vtj = vt_ref[j]
            upd = jnp.zeros((LANES, T), F32)
            alphas = []
            for a in range(2):
                st = st_s[a]
                if masked:
                    st = jnp.where(_causal_t(T), st, NEG)
                m_old = m_s[a]
                m_new = jnp.maximum(m_old, jnp.max(st, axis=0, keepdims=True))
                alpha = jnp.exp(m_old - m_new)
                pt = jnp.exp(st - m_new)
                l_s[a] = alpha * l_s[a] + jnp.sum(pt, axis=0, keepdims=True)
                m_s[a] = m_new
                upd = upd + _dot(_zero_other(vtj, rows[a]), pt.astype(BF16))
                alphas.append(alpha)
            acc_s[...] = acc_s[...] * jnp.where(rows[0], alphas[0], alphas[1]) + upd
            if prefetch:
                st_s[0] = nxt[0]
                st_s[1] = nxt[1]

        def loop(j, carry):
            step(j, False, True)
            return carry

        first = logits(0)
        st_s[0] = first[0]
        st_s[1] = first[1]
        lax.fori_loop(0, i, loop, 0)
        step(i, True, False)
        o_ref[...] = (acc_s[...] / jnp.where(rows[0], l_s[0], l_s[1])).T
        l_ref[0] = m_s[0] + jnp.log(l_s[0])
        l_ref[1] = m_s[1] + jnp.log(l_s[1])

    stat = pl.BlockSpec((2, None, 1, T), lambda h, i: (h, i, 0, 0))
    return pl.pallas_call(
        body, name=name, grid=(B_HEADS // 2, nq),
        in_specs=[pl.BlockSpec((2, T, LANES), lambda h, i: (h, i, 0)),
                  pl.BlockSpec((2, S, LANES), lambda h, i: (h, 0, 0)),
                  pl.BlockSpec((nq, LANES, T), lambda h, i: (0, h, 0))],
        out_specs=[pl.BlockSpec((T, LANES), lambda h, i: (i, h)), stat],
        out_shape=[jax.ShapeDtypeStruct((S, A_WIDTH), F32), jax.ShapeDtypeStruct((B_HEADS, nq, 1, T), F32)],
        scratch_shapes=[pltpu.VMEM((2, 1, T), F32), pltpu.VMEM((2, 1, T), F32), pltpu.VMEM((LANES, T), F32),
                        pltpu.VMEM((2, T, T), F32)],
        compiler_params=_cp(("parallel", "parallel")),
    )(qaug, kaug, vt)


def _fox_delta(o, do, *, name):
    S = o.shape[0]
    T = min(512, S)
    nq = S // T

    per = min(4, nq)

    def body(o_ref, do_ref, d_ref):
        _, rows = _pair_masks()
        for t in range(per):
            sl = slice(t * T, (t + 1) * T)
            prod_t = (do_ref[sl, :].astype(F32) * o_ref[sl, :]).T
            d_ref[0, t] = jnp.sum(_zero_other(prod_t, rows[0]), axis=0, keepdims=True)
            d_ref[1, t] = jnp.sum(_zero_other(prod_t, rows[1]), axis=0, keepdims=True)

    tile = pl.BlockSpec((per * T, LANES), lambda h, i: (i, h))
    return pl.pallas_call(
        body, name=name, grid=(B_HEADS // 2, nq // per), in_specs=[tile, tile],
        out_specs=pl.BlockSpec((2, per, 1, T), lambda h, i: (h, i, 0, 0)),
        out_shape=jax.ShapeDtypeStruct((B_HEADS, nq, 1, T), F32),
        compiler_params=_cp(("parallel", "parallel")))(o, do)


def _fox_bwd(ub, qaug, kaug, kt, do, lse, delta, *, name):
    S = ub.shape[0]
    T = min(512, S)
    nq = S // T

    def body(k_ref, v_ref, kt_ref, q_ref, do_ref, l_ref, dl_ref,
             dk_ref, dv_ref, dck_ref, dqt_ref, dcq_ref, dk_s, dv_s, dc_s):
        j = pl.program_id(1)
        lanes, rows = _pair_masks()
        vv = v_ref[...]
        ktj = kt_ref[...]
        km = [k_ref[0], k_ref[1]]
        ktm = [_zero_other(ktj, rows[0]), _zero_other(ktj, rows[1])]
        dk_s[...] = jnp.zeros((2, T, LANES), F32)
        dv_s[...] = jnp.zeros((T, LANES), F32)
        dc_s[...] = jnp.zeros((2, T, 1), F32)

        @pl.when(j == 0)
        def _():
            dqt_ref[...] = jnp.zeros((nq, LANES, T), F32)
            dcq_ref[...] = jnp.zeros((2, nq, 1, T), F32)

        def step(i, masked):
            off = pl.multiple_of(i * T, T)
            doi = do_ref[pl.ds(off, T), :]
            upd = jnp.zeros((LANES, T), F32)
            for a in range(2):
                qi = q_ref[a, pl.ds(off, T), :]
                st = _dot_nt(km[a], qi)
                if masked:
                    st = jnp.where(_causal_t(T), st, NEG)
                pt = jnp.exp(st - l_ref[a, i])
                doa = _zero_other(doi, lanes[a])
                dv_s[...] += _dot(pt.astype(BF16), doa)
                dst = pt * (_dot_nt(vv, doa) - dl_ref[a, i])
                dsb = dst.astype(BF16)
                dk_s[a] += _dot(dsb, qi)
                upd = upd + _dot(ktm[a], dsb)
                dc_s[a] -= jnp.sum(dst, axis=-1, keepdims=True)
                dcq_ref[a, i] += jnp.sum(dst, axis=0, keepdims=True)
            dqt_ref[i] += upd

        def loop(i, carry):
            step(i, False)
            return carry

        step(j, True)
        lax.fori_loop(j + 1, nq, loop, 0)
        dk_ref[...] = jnp.where(lanes[0], dk_s[0], dk_s[1]).astype(BF16)
        dv_ref[...] = dv_s[...].astype(BF16)
        dck_ref[...] = dc_s[...]

    rowv = pl.BlockSpec((2, nq, 1, T), lambda h, j: (h, 0, 0, 0))
    tile = pl.BlockSpec((T, LANES), lambda h, j: (j, h))
    return pl.pallas_call(
        body, name=name, grid=(B_HEADS // 2, nq),
        in_specs=[pl.BlockSpec((2, T, LANES), lambda h, j: (h, j, 0)),
                  pl.BlockSpec((T, LANES), lambda h, j: (j, 8 + h)),
                  pl.BlockSpec((None, LANES, T), lambda h, j: (j, h, 0)),
                  pl.BlockSpec((2, S, LANES), lambda h, j: (h, 0, 0)),
                  pl.BlockSpec((S, LANES), lambda h, j: (0, h)),
                  rowv, rowv],
        out_specs=[tile, tile, pl.BlockSpec((2, T, 1), lambda h, j: (h, j, 0)),
                   pl.BlockSpec((nq, LANES, T), lambda h, j: (0, h, 0)), rowv],
        out_shape=[jax.ShapeDtypeStruct((S, A_WIDTH), BF16)] * 2 + [jax.ShapeDtypeStruct((B_HEADS, S, 1), F32),
                   jax.ShapeDtypeStruct((nq, A_WIDTH, T), F32), jax.ShapeDtypeStruct((B_HEADS, nq, 1, T), F32)],
        scratch_shapes=[pltpu.VMEM((2, T, LANES), F32), pltpu.VMEM((T, LANES), F32), pltpu.VMEM((2, T, 1), F32)],
        compiler_params=_cp(("parallel", "arbitrary")),
    )(kaug, ub, kt, qaug, do, lse, delta)


def _gate_fwd(o, ur, zcol, *, name):
    S = ur.shape[0]
    tm = min(1024, S)

    def body(o_ref, z_ref, y_ref):
        y_ref[...] = (o_ref[...] * _silu_parts(z_ref[...])[0]).astype(BF16)

    blk = pl.BlockSpec((tm, A_WIDTH), lambda i: (i, 0))
    return pl.pallas_call(
        body, name=name, grid=(S // tm,),
        in_specs=[blk, pl.BlockSpec((tm, A_WIDTH), lambda i: (i, zcol // A_WIDTH))],
        out_specs=blk, out_shape=jax.ShapeDtypeStruct((S, A_WIDTH), BF16),
        compiler_params=_cp(("parallel",)))(o, ur)


def _gate_bwd(o, ur, zcol, dy, du_r, *, name):
    S = ur.shape[0]
    tm = min(1024, S)

    def body(o_ref, z_ref, dy_ref, du_in, do_ref, dz_ref):
        sz, dsz = _silu_parts(z_ref[...])
        dyv = dy_ref[...]
        do_ref[...] = (dyv * sz).astype(BF16)
        dz_ref[...] = (dyv * o_ref[...] * dsz).astype(BF16)

    blk = pl.BlockSpec((tm, A_WIDTH), lambda i: (i, 0))
    gate = pl.BlockSpec((tm, A_WIDTH), lambda i: (i, zcol // A_WIDTH))
    return pl.pallas_call(
        body, name=name, grid=(S // tm,),
        in_specs=[blk, gate, blk, ANY],
        out_specs=[blk, gate],
        out_shape=[jax.ShapeDtypeStruct((S, A_WIDTH), BF16), jax.ShapeDtypeStruct(du_r.shape, BF16)],
        input_output_aliases={3: 1},
        compiler_params=_cp(("parallel",)))(o, ur, dy, du_r)


def _dfb(ur, bf_pad, dlogf_pad, du_r, *, name):
    S = ur.shape[0]
    tm = min(1024, S)

    def body(u_ref, b_ref, d_ref, du_in, o_ref, s_ref):
        i = pl.program_id(0)
        dv = d_ref[...] * _sigmoid(-(u_ref[...] + b_ref[...]))
        o_ref[...] = dv.astype(BF16)
        part = jnp.sum(dv, axis=0, keepdims=True)

        @pl.when(i == 0)
        def _():
            s_ref[...] = part

        @pl.when(i > 0)
        def _():
            s_ref[...] += part

    vec = pl.BlockSpec((1, FB_PAD), lambda i: (0, 0))
    blk = pl.BlockSpec((tm, FB_PAD), lambda i: (i, 0))
    fb = pl.BlockSpec((tm, FB_PAD), lambda i: (i, R_FB // FB_PAD))
    return pl.pallas_call(
        body, name=name, grid=(S // tm,),
        in_specs=[fb, vec, blk, ANY],
        out_specs=[fb, vec],
        out_shape=[jax.ShapeDtypeStruct(du_r.shape, BF16), jax.ShapeDtypeStruct((1, FB_PAD), F32)],
        input_output_aliases={3: 0},
        compiler_params=_cp(("arbitrary",)))(ur, bf_pad, dlogf_pad, du_r)


M_SCALE = HEAD ** -0.5


def _mem_fwd(ur, mkv, *, name):
    S = ur.shape[0]
    T = min(512, S)

    def body(q_ref, z_ref, k_ref, v_ref, y_ref):
        for h in range(M_HEADS):
            hs = slice(h * HEAD, (h + 1) * HEAD)
            s = _dot_nt(q_ref[:, hs].astype(BF16), k_ref[:, hs].astype(BF16)) * M_SCALE
            p = jnp.exp(s - jnp.max(s, axis=-1, keepdims=True))
            p = p / jnp.sum(p, axis=-1, keepdims=True)
            o = _dot(p.astype(BF16), v_ref[:, hs].astype(BF16))
            y_ref[:, hs] = (o * _silu_parts(z_ref[:, hs])[0]).astype(BF16)

    wide = lambda col: pl.BlockSpec((T, A_WIDTH), lambda i: (i, col // A_WIDTH))
    kv = lambda half: pl.BlockSpec((N_MEM, A_WIDTH), lambda i: (0, half))
    return pl.pallas_call(
        body, name=name, grid=(S // T,),
        in_specs=[wide(R_QM), wide(R_ZM), kv(0), kv(1)],
        out_specs=pl.BlockSpec((T, A_WIDTH), lambda i: (i, 0)),
        out_shape=jax.ShapeDtypeStruct((S, A_WIDTH), BF16),
        compiler_params=_cp(("parallel",)))(ur, ur, mkv, mkv)


def _mem_bwd(ur, mkv, dy, du_r, *, name):
    S = ur.shape[0]
    T = min(512, S)

    def body(q_ref, z_ref, k_ref, v_ref, dy_ref, du_in, du_ref, dk_ref, dv_ref):
        i = pl.program_id(0)

        @pl.when(i == 0)
        def _():
            dk_ref[...] = jnp.zeros((N_MEM, A_WIDTH), F32)
            dv_ref[...] = jnp.zeros((N_MEM, A_WIDTH), F32)

        for h in range(M_HEADS):
            hs = slice(h * HEAD, (h + 1) * HEAD)
            qv = q_ref[:, hs].astype(BF16)
            kv = k_ref[:, hs].astype(BF16)
            vv = v_ref[:, hs].astype(BF16)
            s = _dot_nt(qv, kv) * M_SCALE
            p = jnp.exp(s - jnp.max(s, axis=-1, keepdims=True))
            p = p / jnp.sum(p, axis=-1, keepdims=True)
            o = _dot(p.astype(BF16), vv)
            sz, dsz = _silu_parts(z_ref[:, hs])
            dyv = dy_ref[:, hs]
            du_ref[:, A_WIDTH + h * HEAD:A_WIDTH + (h + 1) * HEAD] = (dyv * o * dsz).astype(BF16)
            dov = (dyv * sz).astype(BF16)
            dp = _dot_nt(dov, vv)
            ds = p * (dp - jnp.sum(p * dp, axis=-1, keepdims=True))
            du_ref[:, hs] = (_dot(ds.astype(BF16), kv) * M_SCALE).astype(BF16)
            dv_ref[:, hs] += _dot(p.T.astype(BF16), dov)
            dk_ref[:, hs] += _dot(ds.T.astype(BF16), qv) * M_SCALE

    wide = lambda col: pl.BlockSpec((T, A_WIDTH), lambda i: (i, col // A_WIDTH))
    kv = lambda half: pl.BlockSpec((N_MEM, A_WIDTH), lambda i: (0, half))
    tile = pl.BlockSpec((T, A_WIDTH), lambda i: (i, 0))
    acc = pl.BlockSpec((N_MEM, A_WIDTH), lambda i: (0, 0))
    assert R_ZM == R_QM + A_WIDTH and R_QM % (2 * A_WIDTH) == 0
    return pl.pallas_call(
        body, name=name, grid=(S // T,),
        in_specs=[wide(R_QM), wide(R_ZM), kv(0), kv(1), tile, ANY],
        out_specs=[pl.BlockSpec((T, 2 * A_WIDTH), lambda i: (i, R_QM // (2 * A_WIDTH))), acc, acc],
        out_shape=[jax.ShapeDtypeStruct(du_r.shape, BF16)] + [jax.ShapeDtypeStruct((N_MEM, A_WIDTH), F32)] * 2,
        input_output_aliases={5: 0},
        compiler_params=_cp(("arbitrary",)))(ur, ur, mkv, mkv, dy, du_r)


def _branch_fwd(ys, wbs, ur, b_merge, *, name):
    S = ur.shape[0]
    tm, tn = min(512, S), 512
    nj = D_MODEL // tn

    def body(ya, yb, ym, wa, wb, wm, g0, g1, g2, b0, b1, b2, mg_ref, p_ref):
        acc = jnp.zeros((tm, tn), F32)
        for i, (y, w, gr, br) in enumerate(((ya, wa, g0, b0), (yb, wb, g1, b1), (ym, wm, g2, b2))):
            pr = _dot(y[...], w[...])
            p_ref[i] = pr.astype(BF16)
            acc = acc + _sigmoid(gr[...] + br[...]) * pr
        mg_ref[...] = acc.astype(BF16)

    yspec = pl.BlockSpec((tm, A_WIDTH), lambda i, j: (i, 0))
    wspec = pl.BlockSpec((A_WIDTH, tn), lambda i, j: (0, j))
    gspec = lambda b: pl.BlockSpec((tm, tn), lambda i, j: (i, (R_GL + b * D_MODEL) // tn + j))
    bspec = lambda b: pl.BlockSpec((1, tn), lambda i, j: (0, b * nj + j))
    return pl.pallas_call(
        body, name=name, grid=(S // tm, nj),
        in_specs=[yspec] * 3 + [wspec] * 3 + [gspec(0), gspec(1), gspec(2), bspec(0), bspec(1), bspec(2)],
        out_specs=[pl.BlockSpec((tm, tn), lambda i, j: (i, j)),
                   pl.BlockSpec((3, tm, tn), lambda i, j: (0, i, j))],
        out_shape=[jax.ShapeDtypeStruct((S, D_MODEL), BF16), jax.ShapeDtypeStruct((3, S, D_MODEL), BF16)],
        compiler_params=_cp(("parallel", "parallel")))(*ys, *wbs, ur, ur, ur, b_merge, b_merge, b_merge)


def _branch_bwd(dm, prods, ur, b_merge, *, name):
    S = ur.shape[0]
    tm = min(256, S)

    def body(dm_ref, p_ref, g0, g1, g2, b_ref, dp0, dp1, dp2, dgl_ref, db_ref):
        i = pl.program_id(0)
        dmv = dm_ref[...]
        parts = []
        for b, (gr, dp_ref) in enumerate(((g0, dp0), (g1, dp1), (g2, dp2))):
            sl = slice(b * D_MODEL, (b + 1) * D_MODEL)
            gt = _sigmoid(gr[...] + b_ref[:, sl])
            dp_ref[...] = (dmv * gt).astype(BF16)
            dgl = dmv * p_ref[b].astype(F32) * gt * (1.0 - gt)
            dgl_ref[:, R_GL + b * D_MODEL:R_GL + (b + 1) * D_MODEL] = dgl.astype(BF16)
            parts.append(jnp.sum(dgl, axis=0, keepdims=True))
        part = jnp.concatenate(parts, axis=1)

        @pl.when(i == 0)
        def _():
            db_ref[...] = part

        @pl.when(i > 0)
        def _():
            db_ref[...] += part

    gspec = lambda b: pl.BlockSpec((tm, D_MODEL), lambda i: (i, R_GL // D_MODEL + b))
    vec = pl.BlockSpec((1, 3 * D_MODEL), lambda i: (0, 0))
    row = pl.BlockSpec((tm, D_MODEL), lambda i: (i, 0))
    outs = pl.pallas_call(
        body, name=name, grid=(S // tm,),
        in_specs=[row, pl.BlockSpec((3, tm, D_MODEL), lambda i: (0, i, 0)), gspec(0), gspec(1), gspec(2), vec],
        out_specs=[row, row, row, pl.BlockSpec((tm, NR), lambda i: (i, 0)), vec],
        out_shape=[jax.ShapeDtypeStruct((S, D_MODEL), BF16)] * 3
        + [jax.ShapeDtypeStruct((S, NR), BF16), jax.ShapeDtypeStruct((1, 3 * D_MODEL), F32)],
        compiler_params=_cp(("arbitrary",)))(dm, prods, ur, ur, ur, b_merge)
    return outs[0:3], outs[3], outs[4]


def _rope_tables(pos):
    half = ROT // 2
    S = pos.shape[0]
    inv = ROPE_THETA ** (-jnp.arange(half, dtype=F32) / half)
    per_row = LANES // half
    ang = jnp.repeat(pos.astype(F32).reshape(S // per_row, per_row), half, axis=1) * jnp.tile(inv, per_row)
    cos, sin = lax.optimization_barrier((jnp.cos(ang).reshape(S, half), jnp.sin(ang).reshape(S, half)))
    one = jnp.ones((S, LANES - ROT), F32)
    zero = jnp.zeros((S, LANES - ROT), F32)
    zh = jnp.zeros((S, half), F32)
    c = jnp.concatenate([cos, cos, one], axis=1)
    s1 = jnp.concatenate([-sin, zh, zero], axis=1)
    s2 = jnp.concatenate([zh, sin, zero], axis=1)
    return c, s1, s2


def _to_tiles(t):
    S, H = t.shape
    return t.reshape(S // LANES, LANES, H).transpose(0, 2, 1)


def _from_tiles(t):
    nt, H, _ = t.shape
    return t.transpose(1, 0, 2).reshape(H, nt * LANES)


def _local_step(x, mem, pos, tgt, g_pre, g_post, g_mem, wt, bf_pad, b_merge, w_kv, wbs, w_out, pack=None, hs=None):
    S = x.shape[0]
    T = min(512, S)
    nq = S // T
    tabs = _rope_tables(pos)

    if hs is None:
        hs = _rms_fwd(x, g_pre, name="rms_pre", dilations=DIL)
    h = hs[0]
    tabs_g = [[_to_classes(t, d) for t in tabs] for d in DIL]
    qkvs = [_proj_rope(hs[g], wt[f"A{g}"], tabs_g[g], name=f"proj_a{g}") for g in range(3)]
    ub = _mm(h, wt["B"], bt=True, out_dtype=BF16, name="proj_b", tn=1536)
    ur = _mm(h, wt["R"], bt=True, name="proj_r", tn=1792)

    outs_c, lses_c = [], []
    for g in range(3):
        o, l = _attn_a_fwd(qkvs[g], g, name=f"attn_a_fwd{g}")
        outs_c.append(o)
        lses_c.append(l)
    ya = _merge_a_fwd(outs_c, lses_c, ur, name="merge_a_fwd")

    logf = _logf(ur, bf_pad, name="logf")
    c = _from_tiles(_cumsum_lanes(_to_tiles(logf[:, :B_HEADS]), False, name="cumsum_fwd"))
    qaug, kaug = _fox_aug(ub, c, name="fox_aug")
    kt = ub[:, 512:1024].reshape(nq, T, 512).transpose(0, 2, 1)
    vt = ub[:, 1024:1536].reshape(nq, T, 512).transpose(0, 2, 1)
    ob, lse_b = _fox_fwd(qaug, kaug, vt, name="fox_fwd")
    yb = _gate_fwd(ob, ur, R_ZB, name="gate_b_fwd")

    hm = _rms_fwd(mem, g_mem, name="rms_mem")
    mkv = _mm(hm, w_kv, name="proj_mem")
    ym = _mem_fwd(ur, mkv, name="mem_fwd")

    merged, prods = _branch_fwd((ya, yb, ym), wbs, ur, b_merge, name="branch_fwd")
    out = _mm(merged, w_out, name="proj_out")
    dy, d_out, dg_post, loss_row = _post(x, out, tgt, g_post, name="post")

    dmerged = _mm(d_out, w_out, bt=True, name="d_merged")
    dw_out = _mm(merged, d_out, at=True, name="dw_out", tk=2048)
    dprods, du_r, db_merge = _branch_bwd(dmerged, prods, ur, b_merge, name="branch_bwd")
    dys, dwbs = [], []
    for i, (y, wb) in enumerate(zip((ya, yb, ym), wbs)):
        dys.append(_mm(dprods[i], wb, bt=True, name=f"d_y{i}"))
        dwbs.append(_mm(y, dprods[i], at=True, name=f"dw_branch{i}", tk=2048))

    dos_c, adjs_c, du_r = _merge_a_bwd(outs_c, lses_c, ur, dys[0], du_r, name="merge_a_bwd")
    dus_a = []
    for g, d in enumerate(DIL):
        do_c, adj_c = dos_c[g], adjs_c[g]
        du = _attn_a_dkv(qkvs[g], tabs_g[g], g, do_c, lses_c[g], adj_c, name=f"attn_a_dkv{g}")
        dus_a.append(_attn_a_dq(qkvs[g], tabs_g[g], g, do_c, lses_c[g], adj_c, du, name=f"attn_a_dq{g}"))

    dob, du_r = _gate_bwd(ob, ur, R_ZB, dys[1], du_r, name="gate_b_bwd")
    delta_b = _fox_delta(ob, dob, name="fox_delta")
    dkb, dvb, dc_k, dqt, dc_q = _fox_bwd(ub, qaug, kaug, kt, dob, lse_b, delta_b, name="fox_bwd")
    dqb = (dqt.transpose(0, 2, 1).reshape(S, A_WIDTH) * B_SCALE).astype(BF16)
    du_b = jnp.concatenate([dqb, dkb, dvb], axis=1)
    dc = dc_q.reshape(B_HEADS, S) + dc_k.reshape(B_HEADS, S)
    dlogf = _from_tiles(_cumsum_lanes(_to_tiles(dc.T), True, name="cumsum_bwd"))
    dlogf_pad = jnp.pad(dlogf.T, ((0, 0), (0, FB_PAD - B_HEADS)))
    du_r, db_forget = _dfb(ur, bf_pad, dlogf_pad, du_r, name="dfb")

    du_r, dmk, dmv = _mem_bwd(ur, mkv, dys[2], du_r, name="mem_bwd")
    dmkv = jnp.concatenate([dmk, dmv], axis=1).astype(BF16)
    dhm = _mm(dmkv, w_kv, bt=True, name="d_hm")
    dw_kv = _mm(hm, dmkv, at=True, name="dw_kv")
    dg_mem = _rms_bwd(mem, g_mem, dhm, None, name="rms_mem_bwd")

    dwt ={"R": _mm(du_r, h, at=True, name="dw_in_r", tm=1792, tk=1024),
           "B": _mm(du_b, h, at=True, name="dw_in_b", tm=1536, tk=2048)}
    for g in range(3):
        dwt[f"A{g}"] = _mm(dus_a[g], hs[g], at=True, name=f"dw_in_a{g}", tm=1536, tk=2048)
    res = dict(dwt=dwt, dw_kv=dw_kv, dwbs=dwbs, dw_out=dw_out)
    token_major = [(du_r, wt["R"]), (du_b, wt["B"]), (dus_a[0], wt["A0"])]
    if pack is None:
        dh_1 = _mm(dus_a[1], wt["A1"], name="d_h_a1", tk=1536)
        dh = _mm_sum(token_major, name="d_h_main")
    else:
        gbig = pack(dwt, dw_kv, dwbs, dw_out)
        own_idx = _own_slabs()
        dh_1, sib = _mm(dus_a[1], wt["A1"], name="d_h_a1", tk=1536, comm=_pair_comm(gbig))
        send = _pair_sum(gbig, sib, own_idx, 208, name="pair_sum")
        dh, recv = _mm_sum(token_major, name="d_h_main", comm=_chips_comm(send))
        res = dict(parts=[(gbig, None), (sib, 1), (recv, N_CHIP - 1)], own_idx=own_idx)
    dh_2 = _mm(dus_a[2], wt["A2"], name="d_h_a2", tk=1536)
    grad_x, dg_pre = _rms_bwd(x, g_pre, dh, dy, name="rms_pre_bwd", dh_classes=[(dh_1, DIL[1]), (dh_2, DIL[2])])

    return dict(res, loss=loss_row, grad_x=grad_x, dg_pre=dg_pre, dg_post=dg_post, dg_mem=dg_mem,
                db_forget=db_forget, db_merge=db_merge)


MESH = pl.DeviceIdType.MESH
ANY = pl.BlockSpec(memory_space=pl.ANY)


def _relations():
    return [(k >> 2 & 1, k >> 1 & 1, k & 1) for k in range(1, N_DEV)]


def _coords():
    return lax.axis_index("x"), lax.axis_index("y"), lax.axis_index("c")


def _gather_comm(shard):
    R, W = shard.shape

    def plan(x_ref, out_ref, send_sems, recv_sems, local_sem):
        x, y, c = _coords()
        me, sibling = (x, y, c), (x, y, 1 - c)
        chips = [(1 - x, y), (x, 1 - y), (1 - x, 1 - y)]

        def slot(px, py, pc):
            return out_ref.at[4 * px + 2 * py + pc]

        def copy(k, block, to, src=None):
            return pltpu.make_async_remote_copy(
                src_ref=slot(*block) if src is None else src, dst_ref=slot(*block),
                send_sem=send_sems.at[k], recv_sem=recv_sems.at[k], device_id=to, device_id_type=MESH)

        mine = pltpu.make_async_copy(x_ref, slot(*me), local_sem)
        first = [copy(0, me, sibling, src=x_ref)]
        first += [copy(1 + j, me, (*chip, c), src=x_ref) for j, chip in enumerate(chips)]
        return me, sibling, chips, c, copy, mine, first

    def start(*refs):
        _, _, _, _, _, mine, first = plan(*refs)
        mine.start()
        for cp in first:
            cp.start()

    def wait(*refs):
        me, sibling, chips, c, copy, mine, first = plan(*refs)
        passed = [copy(4 + j, (*chip, c), sibling) for j, chip in enumerate(chips)]
        for j, chip in enumerate(chips):
            copy(1 + j, (*chip, c), me).wait_recv()
            passed[j].start()
        copy(0, sibling, me).wait_recv()
        for j, chip in enumerate(chips):
            copy(4 + j, (*chip, 1 - c), me).wait_recv()
        for cp in first + passed:
            cp.wait_send()
        mine.wait()

    return dict(inputs=[shard], out_shape=[jax.ShapeDtypeStruct((N_DEV, R, W), shard.dtype)],
                sems=[pltpu.SemaphoreType.DMA((N_DEV - 1,)), pltpu.SemaphoreType.DMA((N_DEV - 1,)),
                      pltpu.SemaphoreType.DMA],
                start=start, wait=wait)


N_CHIP = 4


def _pair_comm(gbig):
    _, R, W = gbig.shape

    def copies(g_ref, sib_ref, send_sems, recv_sems):
        x, y, c = _coords()
        return [pltpu.make_async_remote_copy(
            src_ref=g_ref.at[4 * (x ^ (r >> 1)) + 2 * (y ^ (r & 1)) + (1 - c)], dst_ref=sib_ref.at[r],
            send_sem=send_sems.at[r], recv_sem=recv_sems.at[r], device_id=(x, y, 1 - c), device_id_type=MESH)
            for r in range(N_CHIP)]

    def start(*refs):
        for cp in copies(*refs):
            cp.start()

    def wait(*refs):
        cps = copies(*refs)
        for cp in cps:
            cp.wait_recv()
        for cp in cps:
            cp.wait_send()

    return dict(inputs=[gbig], out_shape=[jax.ShapeDtypeStruct((N_CHIP, R, W), gbig.dtype)],
                sems=[pltpu.SemaphoreType.DMA((N_CHIP,)), pltpu.SemaphoreType.DMA((N_CHIP,))],
                start=start, wait=wait)


def _own_slabs():
    x, y, c = _coords()
    return jnp.stack([4 * (x ^ (r >> 1)) + 2 * (y ^ (r & 1)) + c for r in range(N_CHIP)]).astype(jnp.int32)


def _pair_sum(gbig, sib, own_idx, tr, *, name):
    _, R, W = gbig.shape

    def body(idx_ref, a_ref, b_ref, o_ref):
        o_ref[...] = (a_ref[...] + b_ref[...]).astype(BF16)

    return pl.pallas_call(
        body, name=name,
        grid_spec=pltpu.PrefetchScalarGridSpec(
            num_scalar_prefetch=1, grid=(N_CHIP - 1, R // tr),
            in_specs=[pl.BlockSpec((None, tr, W), lambda r, i, idx: (idx[r + 1], i, 0)),
                      pl.BlockSpec((None, tr, W), lambda r, i, idx: (r + 1, i, 0))],
            out_specs=pl.BlockSpec((None, tr, W), lambda r, i, idx: (r, i, 0))),
        out_shape=jax.ShapeDtypeStruct((N_CHIP - 1, R, W), BF16),
        compiler_params=_cp(("parallel", "parallel")))(own_idx, gbig, sib)


def _chips_comm(send):
    nb, R, W = send.shape

    def copies(b_ref, rb_ref, send_sems, recv_sems):
        x, y, c = _coords()
        return [pltpu.make_async_remote_copy(
            src_ref=b_ref.at[r - 1], dst_ref=rb_ref.at[r - 1], send_sem=send_sems.at[r - 1],
            recv_sem=recv_sems.at[r - 1], device_id=(x ^ (r >> 1), y ^ (r & 1), c), device_id_type=MESH)
            for r in range(1, N_CHIP)]

    def start(*refs):
        for cp in copies(*refs):
            cp.start()

    def wait(*refs):
        cps = copies(*refs)
        for cp in cps:
            cp.wait_recv()
        for cp in cps:
            cp.wait_send()

    return dict(inputs=[send], out_shape=[jax.ShapeDtypeStruct((nb, R, W), send.dtype)],
                sems=[pltpu.SemaphoreType.DMA((nb,)), pltpu.SemaphoreType.DMA((nb,))],
                start=start, wait=wait)


def _gather_small(gsmall, *, name):
    n = N_DEV - 1

    def body(s_ref, rs_ref, send_sems, recv_sems, local_sem):
        x, y, c = _coords()
        me = 4 * x + 2 * y + c
        mine = pltpu.make_async_copy(s_ref, rs_ref.at[me], local_sem)
        mine.start()

        def copy(k, fx, fy, fc, slot):
            return pltpu.make_async_remote_copy(
                src_ref=s_ref, dst_ref=rs_ref.at[slot], send_sem=send_sems.at[k], recv_sem=recv_sems.at[k],
                device_id=(x ^ fx, y ^ fy, c ^ fc), device_id_type=MESH)

        started = [copy(k, *rel, me) for k, rel in enumerate(_relations())]
        for cp in started:
            cp.start()
        for k, (fx, fy, fc) in enumerate(_relations()):
            copy(k, fx, fy, fc, 4 * (x ^ fx) + 2 * (y ^ fy) + (c ^ fc)).wait_recv()
        for cp in started:
            cp.wait_send()
        mine.wait()

    return pl.pallas_call(
        body, name=name, out_shape=jax.ShapeDtypeStruct((N_DEV, 1, P_SMALL), gsmall.dtype),
        in_specs=[ANY], out_specs=ANY,
        scratch_shapes=[pltpu.SemaphoreType.DMA((n,)), pltpu.SemaphoreType.DMA((n,)), pltpu.SemaphoreType.DMA],
    )(gsmall)


def _part_specs(parts, tr, row0):
    assert row0 % tr == 0
    specs = []
    for a, n_used in parts:
        if n_used is None:
            specs.append(pl.BlockSpec((1, tr, a.shape[2]), lambda i, idx: (idx[0], row0 // tr + i, 0)))
        else:
            specs.append(pl.BlockSpec((n_used, tr, a.shape[2]), lambda i, idx: (0, row0 // tr + i, 0)))
    return specs


def _part_total(refs, parts):
    g = None
    for ref, (_, n_used) in zip(refs, parts):
        for k in range(n_used or 1):
            t = ref[k].astype(F32)
            g = t if g is None else g + t
    return g


def _sum_parts(parts, idx, row0, nrows, tr, *, name):
    W = parts[0][0].shape[2]
    assert nrows % tr == 0

    def body(idx_ref, *refs):
        refs[-1][...] = _part_total(refs[:-1], parts)

    return pl.pallas_call(
        body, name=name,
        grid_spec=pltpu.PrefetchScalarGridSpec(
            num_scalar_prefetch=1, grid=(nrows // tr,), in_specs=_part_specs(parts, tr, row0),
            out_specs=pl.BlockSpec((tr, W), lambda i, idx: (i, 0))),
        out_shape=jax.ShapeDtypeStruct((nrows, W), F32),
        compiler_params=_cp(("parallel",)))(idx, *[a for a, _ in parts])


def _adamw(parts, idx, w, m, v, tr, *, name):
    R, W = w.shape
    assert R % tr == 0
    np_ = len(parts)

    def body(idx_ref, *refs):
        w_ref, m_ref, v_ref, g_ref, d_ref, nm_ref, nv_ref = refs[np_:]
        g = _part_total(refs[:np_], parts)
        mm = ADAM_B1 * m_ref[...] + (1.0 - ADAM_B1) * g
        vv = ADAM_B2 * v_ref[...] + (1.0 - ADAM_B2) * (g * g)
        m_hat = mm / (1.0 - ADAM_B1 ** ADAM_STEP)
        v_hat = vv / (1.0 - ADAM_B2 ** ADAM_STEP)
        g_ref[...] = g
        d_ref[...] = -ADAM_LR * (m_hat / (jnp.sqrt(v_hat) + ADAM_EPS) + ADAM_WD * w_ref[...])
        nm_ref[...] = mm
        nv_ref[...] = vv

    blk = pl.BlockSpec((tr, W), lambda i, idx: (i, 0))
    return pl.pallas_call(
        body, name=name,
        grid_spec=pltpu.PrefetchScalarGridSpec(
            num_scalar_prefetch=1, grid=(R // tr,), in_specs=_part_specs(parts, tr, 0) + [blk, blk, blk],
            out_specs=[blk] * 4),
        out_shape=[jax.ShapeDtypeStruct((R, W), F32)] * 4,
        compiler_params=_cp(("parallel",)))(idx, *[a for a, _ in parts], w, m, v)


def _pack_rest(w_kv, wa, wb, wm, w_out):
    return jnp.concatenate([w_kv[0], w_out[0]] + [t[0].reshape(-1, D_MODEL) for t in (wa, wb, wm)], axis=0)


def _unpack_rest(t):
    br = lambda i: t[RO_BR + 64 * i:RO_BR + 64 * (i + 1)].reshape(1, A_WIDTH, D_MODEL // N_DEV)
    return t[None, RO_KV:RO_OUT], br(0), br(1), br(2), t[None, RO_OUT:RO_BR]


def _orig_rows(gathered, a, b):
    res = []
    while a < b:
        dev, r = divmod(a, CS)
        n = min(b - a, CS - r)
        res.append(gathered[dev, RO_IN + r:RO_IN + r + n])
        a += n
    return res


def _full_weights(gathered):
    wt = {}
    for name, ranges in SEGS.items():
        rows = [p for a, b in ranges for p in _orig_rows(gathered, a, b)]
        if SEG_PAD[name]:
            rows.append(jnp.zeros((SEG_PAD[name], D_MODEL), gathered.dtype))
        wt[name] = jnp.concatenate(rows, axis=0)
    w_kv = gathered[:, RO_KV:RO_OUT].reshape(D_MODEL, D_MODEL)
    w_out = gathered[:, RO_OUT:RO_BR].reshape(D_MODEL, D_MODEL)
    wbs = [gathered[:, RO_BR + 64 * i:RO_BR + 64 * (i + 1)].reshape(N_DEV, A_WIDTH, D_MODEL // N_DEV)
           .transpose(1, 0, 2).reshape(A_WIDTH, D_MODEL) for i in range(3)]
    return wt, w_kv, wbs, w_out


def _orig_order(dwt):
    pieces = []
    for name, ranges in SEGS.items():
        o = 0
        for a, b in ranges:
            pieces.append((a, dwt[name][o:o + b - a]))
            o += b - a
    pieces.sort(key=lambda p: p[0])
    return jnp.concatenate([p[1] for p in pieces], axis=0)


def _pack_grads(dwt, dw_kv, dwbs, dw_out):
    g_in = jnp.pad(_orig_order(dwt).reshape(N_DEV, CS, D_MODEL), ((0, 0), (0, IN_ROWS - CS), (0, 0)))
    br = [t.reshape(A_WIDTH, N_DEV, D_MODEL // N_DEV).transpose(1, 0, 2).reshape(N_DEV, -1, D_MODEL) for t in dwbs]
    return jnp.concatenate([dw_kv.reshape(N_DEV, -1, D_MODEL), dw_out.reshape(N_DEV, -1, D_MODEL)] + br + [g_in],
                           axis=1)


def kernel(x, mem, positions, norm_pre_g, norm_post_g, norm_mem_g, w_in, b_forget, b_merge, w_mem_kv, w_branch_a, w_branch_b, w_branch_m, w_out, loss_target, m_norm_pre_g, m_norm_post_g, m_norm_mem_g, m_w_in, m_b_forget, m_b_merge, m_w_mem_kv, m_w_branch_a, m_w_branch_b, m_w_branch_m, m_w_out, v_norm_pre_g, v_norm_post_g, v_norm_mem_g, v_w_in, v_b_forget, v_b_merge, v_w_mem_kv, v_w_branch_a, v_w_branch_b, v_w_branch_m, v_w_out):
    w_rest = _pack_rest(w_mem_kv, w_branch_a, w_branch_b, w_branch_m, w_out)
    shard = jnp.concatenate([w_rest.astype(BF16), w_in[0].T.astype(BF16),
                             jnp.zeros((IN_ROWS - CS, D_MODEL), BF16)], axis=0)
    hs, (gathered,) = _rms_fwd(x[0], norm_pre_g, name="rms_pre_gather", dilations=DIL, comm=_gather_comm(shard))
    wt, w_kv, wbs, w_o = _full_weights(gathered)

    bf_pad = jnp.pad(b_forget, ((0, 0), (0, FB_PAD - B_HEADS)))
    r = _local_step(x[0], mem[0], positions[0], loss_target[0], norm_pre_g, norm_post_g, norm_mem_g,
                    wt, bf_pad, b_merge, w_kv, wbs, w_o, pack=_pack_grads, hs=hs)

    gsmall = jnp.concatenate([r["dg_pre"], r["dg_post"], r["dg_mem"], r["db_merge"],
                              r["db_forget"][:, :LANES], r["loss"]], axis=1)
    rsmall = _gather_small(gsmall, name="gather_small")
    parts, own_idx = r["parts"], r["own_idx"]

    m_rest = _pack_rest(m_w_mem_kv, m_w_branch_a, m_w_branch_b, m_w_branch_m, m_w_out)
    v_rest = _pack_rest(v_w_mem_kv, v_w_branch_a, v_w_branch_b, v_w_branch_m, v_w_out)
    outs_rest = [_unpack_rest(t) for t in _adamw(parts, own_idx, w_rest, m_rest, v_rest, 64, name="adamw_rest")]
    g_in = _sum_parts(parts, own_idx, RO_IN, IN_ROWS, 16, name="sum_w_in")[:CS].T
    outs_in = _adamw([(g_in[None], 1)], own_idx, w_in[0], m_w_in[0], v_w_in[0], 128, name="adamw_w_in")

    def small_vec(a, b, c, d, e):
        z = jnp.zeros((1, LANES - B_HEADS), F32)
        return jnp.concatenate([a, b, c, d, e, z, jnp.zeros((1, LANES), F32)], axis=1)

    outs_small = _adamw([(rsmall, N_DEV)], own_idx, small_vec(norm_pre_g, norm_post_g, norm_mem_g, b_merge, b_forget),
                        small_vec(m_norm_pre_g, m_norm_post_g, m_norm_mem_g, m_b_merge, m_b_forget),
                        small_vec(v_norm_pre_g, v_norm_post_g, v_norm_mem_g, v_b_merge, v_b_forget),
                        1, name="adamw_small")

    def small_parts(t):
        return [t[:, O_GPRE:O_GPRE + D_MODEL], t[:, O_GPOST:O_GPOST + D_MODEL], t[:, O_GMEM:O_GMEM + D_MODEL],
                t[:, O_BF:O_BF + B_HEADS], t[:, O_BM:O_BM + 3 * D_MODEL]]

    loss = outs_small[0][0, O_LOSS]
    result = [loss, r["grad_x"][None]]
    for rest, w_i, small in zip(outs_rest, outs_in, outs_small):
        gp, gq, gm, bf, bm = small_parts(small)
        w_k, w_a, w_b, w_m, w_ot = rest
        result += [gp, gq, gm, w_i[None], bf, bm, w_k, w_a, w_b, w_m, w_ot]
    return tuple(result)
```

```python
import jax
import jax.numpy as jnp
from jax import lax
from jax.experimental import pallas as pl
from jax.experimental.pallas import tpu as pltpu

F32 = jnp.float32
BF16 = jnp.bfloat16

N_DEV = 8
D_MODEL = 1024
N_MEM = 256
EPS = 1e-6
NEG = -1e30
ROPE_THETA = 500000.0
DIL = (1, 4, 16)
A_HEADS = 4
HEAD = 128
A_WIDTH = 512
B_HEADS = 8
B_HEAD = 64
M_HEADS = 4
ROT = 32
IN_COLS = 11272
FB_PAD = 256

SEGS = {
    "A0": ((0, 512), (1536, 2048), (3072, 3584)),
    "A1": ((512, 1024), (2048, 2560), (3584, 4096)),
    "A2": ((1024, 1536), (2560, 3072), (4096, 4608)),
    "B": ((5120, 6656),),
    "R": ((4608, 5120), (6664, 7176), (7176, 7688), (7688, 8200), (8200, 11272), (6656, 6664)),
}
SEG_PAD = {"A0": 0, "A1": 0, "A2": 0, "B": 0, "R": FB_PAD - B_HEADS}
R_ZA, R_ZB, R_QM, R_ZM, R_GL, R_FB = 0, 512, 1024, 1536, 2048, 5120
NR = R_FB + FB_PAD

ADAM_LR, ADAM_B1, ADAM_B2, ADAM_EPS, ADAM_WD, ADAM_STEP = 0.001, 0.9, 0.999, 1e-08, 0.01, 10

LANES = 128
VMEM_LIMIT = 56 * 1024 * 1024

CS = IN_COLS // N_DEV
RO_KV, RO_OUT, RO_BR, RO_IN = 0, 128, 256, 448
IN_ROWS = 1424
ROWS = RO_IN + IN_ROWS
O_GPRE, O_GPOST, O_GMEM, O_BM, O_BF, O_LOSS = 0, 1024, 2048, 3072, 6144, 6272
P_SMALL = 6400


def _cp(sem=None):
    return pltpu.CompilerParams(dimension_semantics=sem, vmem_limit_bytes=VMEM_LIMIT)


def _dot(a, b):
    return jnp.dot(a, b, preferred_element_type=F32)


def _dot_nt(a, b):
    return lax.dot_general(a, b, (((1,), (1,)), ((), ())), preferred_element_type=F32)


def _sigmoid(z):
    return 1.0 / (1.0 + jnp.exp(-z))


def _mm(a, b, *, name, at=False, bt=False, out_dtype=F32, tm=1024, tn=1024, tk=None, comm=None):
    assert not (at and bt)
    K, M = a.shape if at else a.shape[::-1]
    N = b.shape[0] if bt else b.shape[1]
    tm, tn = min(tm, M), min(tn, N)
    tk = K if tk is None else min(tk, K)
    assert M % tm == 0 and N % tn == 0 and K % tk == 0
    nk = K // tk
    grid = (M // tm, N // tn, nk)
    n_in = len(comm["inputs"]) if comm else 0
    n_out = len(comm["out_shape"]) if comm else 0

    def body(a_ref, b_ref, *rest):
        c_in, o_ref, c_out = rest[:n_in], rest[n_in], rest[n_in + 1:n_in + 1 + n_out]
        acc_ref, sems = rest[n_in + 1 + n_out], rest[n_in + 2 + n_out:]
        if comm:
            step = (pl.program_id(0) * grid[1] + pl.program_id(1)) * grid[2] + pl.program_id(2)

            @pl.when(step == 0)
            def _():
                comm["start"](*c_in, *c_out, *sems)

        av = a_ref[...].astype(BF16)
        bv = b_ref[...].astype(BF16)
        if at:
            p = lax.dot_general(av, bv, (((0,), (0,)), ((), ())), preferred_element_type=F32)
        else:
            p = _dot_nt(av, bv) if bt else _dot(av, bv)
        if nk == 1:
            o_ref[...] = p.astype(out_dtype)
        else:
            k = pl.program_id(2)

            @pl.when(k == 0)
            def _():
                acc_ref[...] = p

            @pl.when(k > 0)
            def _():
                acc_ref[...] += p

            @pl.when(k == nk - 1)
            def _():
                o_ref[...] = acc_ref[...].astype(out_dtype)

        if comm:
            @pl.when(step == grid[0] * grid[1] * grid[2] - 1)
            def _():
                comm["wait"](*c_in, *c_out, *sems)

    b_spec = (pl.BlockSpec((tn, tk), lambda i, j, k: (j, k)) if bt
              else pl.BlockSpec((tk, tn), lambda i, j, k: (k, j)))
    a_spec = (pl.BlockSpec((tk, tm), lambda i, j, k: (k, i)) if at
              else pl.BlockSpec((tm, tk), lambda i, j, k: (i, k)))
    out_spec = pl.BlockSpec((tm, tn), lambda i, j, k: (i, j))
    out_shape = jax.ShapeDtypeStruct((M, N), out_dtype)
    acc = pltpu.VMEM((tm, tn) if nk > 1 else (8, LANES), F32)
    if not comm:
        return pl.pallas_call(
            body, name=name, grid=grid, in_specs=[a_spec, b_spec], out_specs=out_spec, out_shape=out_shape,
            scratch_shapes=[acc], compiler_params=_cp(("parallel", "parallel", "arbitrary")))(a, b)
    return pl.pallas_call(
        body, name=name, grid=grid, in_specs=[a_spec, b_spec] + [ANY] * n_in,
        out_specs=[out_spec] + [ANY] * n_out, out_shape=[out_shape] + comm["out_shape"],
        scratch_shapes=[acc] + comm["sems"],
        compiler_params=_cp(("arbitrary", "arbitrary", "arbitrary")))(a, b, *comm["inputs"])


def _mm_sum(pairs, *, name, tm=1024, tk=768, comm=None):
    M, N = pairs[0][0].shape[0], pairs[0][1].shape[1]
    tm = min(tm, M)
    steps = [a.shape[1] // tk for a, _ in pairs]
    assert M % tm == 0 and all(a.shape[1] % tk == 0 for a, _ in pairs)
    first = [sum(steps[:p]) for p in range(len(pairs))]
    total = sum(steps)
    grid = (M // tm, total)
    n_in = len(comm["inputs"]) if comm else 0
    n_out = len(comm["out_shape"]) if comm else 0
    npair = len(pairs)

    def body(*refs):
        ab, rest = refs[:2 * npair], refs[2 * npair:]
        c_in, o_ref, c_out = rest[:n_in], rest[n_in], rest[n_in + 1:n_in + 1 + n_out]
        acc_ref, sems = rest[n_in + 1 + n_out], rest[n_in + 2 + n_out:]
        k = pl.program_id(1)
        if comm:
            step = pl.program_id(0) * total + k

            @pl.when(step == 0)
            def _():
                comm["start"](*c_in, *c_out, *sems)

        @pl.when(k == 0)
        def _():
            acc_ref[...] = jnp.zeros((tm, N), F32)

        for p in range(npair):
            @pl.when(jnp.logical_and(k >= first[p], k < first[p] + steps[p]))
            def _(p=p):
                acc_ref[...] += _dot(ab[2 * p][...], ab[2 * p + 1][...])

        @pl.when(k == total - 1)
        def _():
            o_ref[...] = acc_ref[...]

        if comm:
            @pl.when(step == grid[0] * total - 1)
            def _():
                comm["wait"](*c_in, *c_out, *sems)

    def local(p):
        return lambda k: jnp.clip(k - first[p], 0, steps[p] - 1)

    in_specs = []
    for p in range(npair):
        in_specs += [pl.BlockSpec((tm, tk), lambda i, k, f=local(p): (i, f(k))),
                     pl.BlockSpec((tk, N), lambda i, k, f=local(p): (f(k), 0))]
    out_spec = pl.BlockSpec((tm, N), lambda i, k: (i, 0))
    out_shape = jax.ShapeDtypeStruct((M, N), F32)
    args = [t for pair in pairs for t in pair]
    if not comm:
        return pl.pallas_call(
            body, name=name, grid=grid, in_specs=in_specs, out_specs=out_spec, out_shape=out_shape,
            scratch_shapes=[pltpu.VMEM((tm, N), F32)], compiler_params=_cp(("parallel", "arbitrary")))(*args)
    return pl.pallas_call(
        body, name=name, grid=grid, in_specs=in_specs + [ANY] * n_in,
        out_specs=[out_spec] + [ANY] * n_out, out_shape=[out_shape] + comm["out_shape"],
        scratch_shapes=[pltpu.VMEM((tm, N), F32)] + comm["sems"],
        compiler_params=_cp(("arbitrary", "arbitrary")))(*args, *comm["inputs"])


def _class_spec(S, d, tm, width):
    return pl.BlockSpec((d, tm // d, width), lambda i: (0, i, 0))


def _rms_fwd(x, g, *, name, dilations=(), comm=None):
    S, D = x.shape
    tm = min(512, S)
    ds = [d for d in dilations if d > 1]
    nsteps = S // tm
    n_in = len(comm["inputs"]) if comm else 0
    n_out = len(comm["out_shape"]) if comm else 0
    n_tmp = D // LANES if ds else 0

    def body(x_ref, g_ref, *rest):
        c_in, o_ref, rest = rest[:n_in], rest[n_in], rest[n_in + 1:]
        cls, c_out, rest = rest[:len(ds)], rest[len(ds):len(ds) + n_out], rest[len(ds) + n_out:]
        tmps, sems = rest[:n_tmp], rest[n_tmp:]
        if comm:
            @pl.when(pl.program_id(0) == 0)
            def _():
                comm["start"](*c_in, *c_out, *sems)

        xv = x_ref[...]
        r = lax.rsqrt(jnp.mean(xv * xv, axis=-1, keepdims=True) + EPS)
        hv = xv * r * g_ref[...]
        o_ref[...] = hv.astype(BF16)
        if ds:
            for c, tmp in enumerate(tmps):
                tmp[...] = hv[:, c * LANES:(c + 1) * LANES]
            for c_ref, d in zip(cls, ds):
                for k in range(d):
                    c_ref[k] = jnp.concatenate([tmp[pl.ds(k, tm // d, stride=d), :] for tmp in tmps],
                                               axis=1).astype(BF16)
        if comm:
            @pl.when(pl.program_id(0) == nsteps - 1)
            def _():
                comm["wait"](*c_in, *c_out, *sems)

    row = pl.BlockSpec((tm, D), lambda i: (i, 0))
    outs = pl.pallas_call(
        body, name=name, grid=(nsteps,),
        in_specs=[row, pl.BlockSpec((1, D), lambda i: (0, 0))] + [ANY] * n_in,
        out_specs=[row] + [_class_spec(S, d, tm, D) for d in ds] + [ANY] * n_out,
        out_shape=[jax.ShapeDtypeStruct((S, D), BF16)] + [jax.ShapeDtypeStruct((d, S // d, D), BF16) for d in ds]
        + (comm["out_shape"] if comm else []),
        scratch_shapes=[pltpu.VMEM((tm, LANES), F32)] * n_tmp + (comm["sems"] if comm else []),
        compiler_params=_cp(("arbitrary",) if comm else ("parallel",)),
    )(x, g, *(comm["inputs"] if comm else []))
    rows = [outs[0]] + [o.reshape(S, D) for o in outs[1:1 + len(ds)]]
    if comm:
        return rows, list(outs[1 + len(ds):])
    return rows if ds else rows[0]


def _rms_bwd(x, g, dh, dy, *, name, dh_classes=()):
    S, D = x.shape
    tm = min(512, S)
    want_dx = dy is not None
    nc = len(dh_classes)

    def body(*refs):
        c_refs, refs = refs[:nc], refs[nc:]
        if want_dx:
            x_ref, g_ref, dh_ref, dy_ref, dx_ref, dg_ref = refs[:6]
        else:
            x_ref, g_ref, dh_ref, dg_ref = refs[:4]
        i = pl.program_id(0)
        xv = x_ref[...]
        r = lax.rsqrt(jnp.mean(xv * xv, axis=-1, keepdims=True) + EPS)
        xh = xv * r
        if nc:
            tmps = refs[-(D // LANES):]
            cols = [slice(c * LANES, (c + 1) * LANES) for c in range(D // LANES)]
            for tmp, cs in zip(tmps, cols):
                tmp[...] = dh_ref[:, cs]
            for c_ref, (_, d) in zip(c_refs, dh_classes):
                for k in range(d):
                    for tmp, cs in zip(tmps, cols):
                        tmp[pl.ds(k, tm // d, stride=d), :] += c_ref[k, :, cs]
            dhv = jnp.concatenate([tmp[...] for tmp in tmps], axis=1)
        else:
            dhv = dh_ref[...]
        part = jnp.sum(dhv * xh, axis=0, keepdims=True)

        @pl.when(i == 0)
        def _():
            dg_ref[...] = part

        @pl.when(i > 0)
        def _():
            dg_ref[...] += part

        if want_dx:
            dxh = dhv * g_ref[...]
            dx_ref[...] = dy_ref[...] + r * (dxh - xh * jnp.mean(dxh * xh, axis=-1, keepdims=True))

    row = pl.BlockSpec((tm, D), lambda i: (i, 0))
    vec = pl.BlockSpec((1, D), lambda i: (0, 0))
    c_specs = [_class_spec(S, d, tm, D) for _, d in dh_classes]
    c_args = [a.reshape(d, S // d, D) for a, d in dh_classes]
    scratch = [pltpu.VMEM((tm, LANES), F32)] * (D // LANES) if nc else []
    if want_dx:
        return pl.pallas_call(
            body, name=name, grid=(S // tm,), in_specs=c_specs + [row, vec, row, row], out_specs=[row, vec],
            out_shape=[jax.ShapeDtypeStruct((S, D), F32), jax.ShapeDtypeStruct((1, D), F32)],
            scratch_shapes=scratch, compiler_params=_cp(("arbitrary",)))(*c_args, x, g, dh, dy)
    return pl.pallas_call(
        body, name=name, grid=(S // tm,), in_specs=c_specs + [row, vec, row], out_specs=vec,
        out_shape=jax.ShapeDtypeStruct((1, D), F32),
        scratch_shapes=scratch, compiler_params=_cp(("arbitrary",)))(*c_args, x, g, dh)


def _post(x, out, tgt, g, *, name):
    S, D = x.shape
    tm = min(512, S)

    def body(x_ref, o_ref, t_ref, g_ref, dy_ref, do_ref, dg_ref, loss_ref):
        i = pl.program_id(0)
        ov = o_ref[...]
        r = lax.rsqrt(jnp.mean(ov * ov, axis=-1, keepdims=True) + EPS)
        n = ov * r
        gv = g_ref[...]
        e = (x_ref[...] + n * gv) - t_ref[...]
        lpart = 0.5 * jnp.sum(jnp.mean(e * e, axis=-1, keepdims=True), axis=0, keepdims=True)
        dy = e * (1.0 / D)
        dy_ref[...] = dy
        dn = dy * gv
        do_ref[...] = (r * (dn - n * jnp.mean(dn * n, axis=-1, keepdims=True))).astype(BF16)
        gpart = jnp.sum(dy * n, axis=0, keepdims=True)
        lrow = jnp.broadcast_to(lpart, (1, LANES))

        @pl.when(i == 0)
        def _():
            dg_ref[...] = gpart
            loss_ref[...] = lrow

        @pl.when(i > 0)
        def _():
            dg_ref[...] += gpart
            loss_ref[...] += lrow

    row = pl.BlockSpec((tm, D), lambda i: (i, 0))
    vec = pl.BlockSpec((1, D), lambda i: (0, 0))
    return pl.pallas_call(
        body, name=name, grid=(S // tm,), in_specs=[row, row, row, vec],
        out_specs=[row, row, vec, pl.BlockSpec((1, LANES), lambda i: (0, 0))],
        out_shape=[jax.ShapeDtypeStruct((S, D), F32), jax.ShapeDtypeStruct((S, D), BF16),
                   jax.ShapeDtypeStruct((1, D), F32), jax.ShapeDtypeStruct((1, LANES), F32)],
        compiler_params=_cp(("arbitrary",)))(x, out, tgt, g)


def _to_classes(t, d):
    if d == 1:
        return t
    S, C = t.shape
    return t.reshape(S // d, d, C).transpose(1, 0, 2).reshape(S, C)


def _rope(x, c, s1, s2):
    return x * c + pltpu.roll(x, LANES - ROT // 2, 1) * s1 + pltpu.roll(x, ROT // 2, 1) * s2


def _unrope(d, c, s1, s2):
    return d * c + pltpu.roll(d * s1, ROT // 2, 1) + pltpu.roll(d * s2, LANES - ROT // 2, 1)


def _a_band(qb):
    r = lax.broadcasted_iota(jnp.int32, (qb, qb + HEAD), 0)
    c = lax.broadcasted_iota(jnp.int32, (qb, qb + HEAD), 1)
    return jnp.logical_and(c >= r, c <= r + HEAD)


def _a_first_ok(qb, n):
    c = lax.broadcasted_iota(jnp.int32, (qb, qb + HEAD), 1)
    return jnp.logical_or(c >= HEAD, n > 0)


def _a_last_ok(qb, has_next):
    c = lax.broadcasted_iota(jnp.int32, (qb, qb + HEAD), 1)
    return jnp.logical_or(c < qb, has_next)


A_SCALE = HEAD ** -0.5


def _a_geometry(S, g):
    d = DIL[g]
    L = S // d
    TQ = min(512, L)
    return d, L, TQ, TQ // HEAD, L // TQ, L // HEAD


def _proj_rope(h, w, tabs, *, name):
    S, D = h.shape
    tm = min(512, S)

    def body(h_ref, w_ref, c_ref, s1_ref, s2_ref, o_ref):
        tc = (c_ref[...], s1_ref[...], s2_ref[...])
        u = _dot_nt(h_ref[...], w_ref[...])
        for j in range(3 * A_HEADS):
            sl = slice(j * HEAD, (j + 1) * HEAD)
            o_ref[:, sl] = (_rope(u[:, sl], *tc) if j < 2 * A_HEADS else u[:, sl]).astype(BF16)

    tab = pl.BlockSpec((tm, LANES), lambda i: (i, 0))
    return pl.pallas_call(
        body, name=name, grid=(S // tm,),
        in_specs=[pl.BlockSpec((tm, D), lambda i: (i, 0)), pl.BlockSpec((3 * A_WIDTH, D), lambda i: (0, 0)),
                  tab, tab, tab],
        out_specs=pl.BlockSpec((tm, 3 * A_WIDTH), lambda i: (i, 0)),
        out_shape=jax.ShapeDtypeStruct((S, 3 * A_WIDTH), BF16),
        compiler_params=_cp(("parallel",)))(h, w, *tabs)


def _attn_a_fwd(qkv, g, *, name):
    S = qkv.shape[0]
    d, L, TQ, nsub, nb, nblk = _a_geometry(S, g)

    def body(q_ref, kc_ref, kp_ref, vc_ref, vp_ref, o_ref, l_ref):
        n = pl.program_id(1)
        QB = min(2 * HEAD, TQ)
        band = _a_band(QB)
        first = jnp.logical_and(band, _a_first_ok(QB, n))
        for h in range(A_HEADS):
            hs = slice(h * HEAD, (h + 1) * HEAD)
            for hh in range(TQ // QB):
                sl = slice(hh * QB, (hh + 1) * QB)
                pv = slice(hh * QB - HEAD, hh * QB)
                kcat = jnp.concatenate([kp_ref[:, hs] if hh == 0 else kc_ref[pv, hs], kc_ref[sl, hs]], axis=0)
                vcat = jnp.concatenate([vp_ref[:, hs] if hh == 0 else vc_ref[pv, hs], vc_ref[sl, hs]], axis=0)
                s = jnp.where(first if hh == 0 else band, _dot_nt(q_ref[sl, hs], kcat) * A_SCALE, NEG)
                m = jnp.max(s, axis=-1, keepdims=True)
                p = jnp.exp(s - m)
                den = jnp.sum(p, axis=-1, keepdims=True)
                o_ref[sl, hs] = _dot(p.astype(BF16), vcat) / den
                l_ref[sl, hs] = jnp.broadcast_to(m + jnp.log(den), (QB, HEAD))

    rcur = lambda r, n: r * nb + n
    rprv = lambda r, n: r * nblk + jnp.maximum(n * nsub - 1, 0)
    cur = lambda off: pl.BlockSpec((TQ, A_WIDTH), lambda r, n: (rcur(r, n), off))
    prv = lambda off: pl.BlockSpec((HEAD, A_WIDTH), lambda r, n: (rprv(r, n), off))
    out = pl.BlockSpec((TQ, A_WIDTH), lambda r, n: (rcur(r, n), 0))
    return pl.pallas_call(
        body, name=name, grid=(d, nb),
        in_specs=[cur(0), cur(1), prv(1), cur(2), prv(2)],
        out_specs=[out, out],
        out_shape=[jax.ShapeDtypeStruct((S, A_WIDTH), F32)] * 2,
        compiler_params=_cp(("parallel", "parallel")),
    )(qkv, qkv, qkv, qkv, qkv)


def _attn_a_dq(qkv, tabs, g, do, lse, adj, du, *, name):
    S = qkv.shape[0]
    d, L, TQ, nsub, nb, nblk = _a_geometry(S, g)

    def body(q_ref, kc_ref, kp_ref, vc_ref, vp_ref, do_ref, l_ref, adj_ref, c_ref, s1_ref, s2_ref, du_ref, dq_ref):
        n = pl.program_id(1)
        QB = min(2 * HEAD, TQ)
        band = _a_band(QB)
        first = jnp.logical_and(band, _a_first_ok(QB, n))
        for h in range(A_HEADS):
            hs = slice(h * HEAD, (h + 1) * HEAD)
            for hh in range(TQ // QB):
                sl = slice(hh * QB, (hh + 1) * QB)
                pv = slice(hh * QB - HEAD, hh * QB)
                kcat = jnp.concatenate([kp_ref[:, hs] if hh == 0 else kc_ref[pv, hs], kc_ref[sl, hs]], axis=0)
                vcat = jnp.concatenate([vp_ref[:, hs] if hh == 0 else vc_ref[pv, hs], vc_ref[sl, hs]], axis=0)
                s = jnp.where(first if hh == 0 else band, _dot_nt(q_ref[sl, hs], kcat) * A_SCALE, NEG)
                p = jnp.exp(s - l_ref[sl, hs][:, :1])
                ds = p * (_dot_nt(do_ref[sl, hs], vcat) + adj_ref[sl, hs][:, :1])
                dq = _dot(ds.astype(BF16), kcat) * A_SCALE
                dq_ref[sl, hs] = _unrope(dq, c_ref[sl, :], s1_ref[sl, :], s2_ref[sl, :]).astype(BF16)

    rcur = lambda r, n: r * nb + n
    rprv = lambda r, n: r * nblk + jnp.maximum(n * nsub - 1, 0)
    cur = lambda off: pl.BlockSpec((TQ, A_WIDTH), lambda r, n: (rcur(r, n), off))
    prv = lambda off: pl.BlockSpec((HEAD, A_WIDTH), lambda r, n: (rprv(r, n), off))
    tcur = pl.BlockSpec((TQ, LANES), lambda r, n: (rcur(r, n), 0))
    blk = cur(0)
    return pl.pallas_call(
        body, name=name, grid=(d, nb),
        in_specs=[cur(0), cur(1), prv(1), cur(2), prv(2), blk, blk, blk, tcur, tcur, tcur, ANY],
        out_specs=blk,
        out_shape=jax.ShapeDtypeStruct((S, 3 * A_WIDTH), BF16),
        input_output_aliases={11: 0},
        compiler_params=_cp(("parallel", "parallel")),
    )(qkv, qkv, qkv, qkv, qkv, do, lse, adj, *tabs, du)


def _attn_a_dkv(qkv, tabs, g, do, lse, adj, *, name):
    S = qkv.shape[0]
    d, L, TQ, nsub, nb, nblk = _a_geometry(S, g)

    def body(qc_ref, qn_ref, kc_ref, vc_ref, doc_ref, don_ref, lc_ref, ln_ref, ac_ref, an_ref,
             c_ref, s1_ref, s2_ref, du_ref):
        n = pl.program_id(1)
        QB = min(2 * HEAD, TQ)
        nh = TQ // QB
        band = _a_band(QB)
        end = jnp.logical_and(band, _a_last_ok(QB, n < nb - 1))
        for h in range(A_HEADS):
            hs = slice(h * HEAD, (h + 1) * HEAD)
            for kh in range(nh):
                sl = slice(kh * QB, (kh + 1) * QB)
                nx = slice((kh + 1) * QB, (kh + 1) * QB + HEAD)
                last = kh == nh - 1
                cat = lambda cur, nxt: jnp.concatenate([cur[sl, hs], nxt[:, hs] if last else cur[nx, hs]], axis=0)
                qcat = cat(qc_ref, qn_ref)
                docat = cat(doc_ref, don_ref)
                lt = cat(lc_ref, ln_ref).T[:1, :]
                at = cat(ac_ref, an_ref).T[:1, :]
                st = jnp.where(end if last else band, _dot_nt(kc_ref[sl, hs], qcat) * A_SCALE, NEG)
                pt = jnp.exp(st - lt)
                dv_cols = slice(2 * A_WIDTH + h * HEAD, 2 * A_WIDTH + (h + 1) * HEAD)
                dk_cols = slice(A_WIDTH + h * HEAD, A_WIDTH + (h + 1) * HEAD)
                du_ref[sl, dv_cols] = _dot(pt.astype(BF16), docat).astype(BF16)
                dst = pt * (_dot_nt(vc_ref[sl, hs], docat) + at)
                dk = _dot(dst.astype(BF16), qcat) * A_SCALE
                du_ref[sl, dk_cols] = _unrope(dk, c_ref[sl, :], s1_ref[sl, :], s2_ref[sl, :]).astype(BF16)

    rcur = lambda r, n: r * nb + n
    rnxt = lambda r, n: r * nblk + jnp.minimum((n + 1) * nsub, nblk - 1)
    cur = lambda off: pl.BlockSpec((TQ, A_WIDTH), lambda r, n: (rcur(r, n), off))
    nxu = lambda off: pl.BlockSpec((HEAD, A_WIDTH), lambda r, n: (rnxt(r, n), off))
    tcur = pl.BlockSpec((TQ, LANES), lambda r, n: (rcur(r, n), 0))
    blk, bnx = cur(0), nxu(0)
    return pl.pallas_call(
        body, name=name, grid=(d, nb),
        in_specs=[cur(0), nxu(0), cur(1), cur(2), blk, bnx, blk, bnx, blk, bnx, tcur, tcur, tcur],
        out_specs=pl.BlockSpec((TQ, 3 * A_WIDTH), lambda r, n: (rcur(r, n), 0)),
        out_shape=jax.ShapeDtypeStruct((S, 3 * A_WIDTH), BF16),
        compiler_params=_cp(("parallel", "parallel")),
    )(qkv, qkv, qkv, qkv, do, do, lse, lse, adj, adj, *tabs)


def _silu_parts(z):
    sg = _sigmoid(z)
    return z * sg, sg * (1.0 + z * (1.0 - sg))


def _classes_to_tokens(c_ref, d, tm, tmps):
    if d == 1:
        return c_ref[...].astype(F32)
    for k in range(d):
        for c, tmp in enumerate(tmps):
            tmp[pl.ds(k, tm // d, stride=d), :] = c_ref[k, :, c * LANES:(c + 1) * LANES].astype(F32)
    return jnp.concatenate([tmp[...] for tmp in tmps], axis=1)


def _tokens_to_classes(val, c_ref, d, tm, tmps):
    if d == 1:
        c_ref[...] = val.astype(c_ref.dtype)
        return
    for c, tmp in enumerate(tmps):
        tmp[...] = val[:, c * LANES:(c + 1) * LANES]
    for k in range(d):
        c_ref[k] = jnp.concatenate([tmp[pl.ds(k, tm // d, stride=d), :] for tmp in tmps], axis=1).astype(c_ref.dtype)


def _group_spec(S, d, tm):
    if d == 1:
        return pl.BlockSpec((tm, A_WIDTH), lambda i: (i, 0))
    return _class_spec(S, d, tm, A_WIDTH)


def _group_view(t, d):
    return t if d == 1 else t.reshape(d, t.shape[0] // d, t.shape[1])


def _merge_a_fwd(os_, ls_, ur, *, name):
    S = ur.shape[0]
    tm = min(512, S)

    def body(o0, o1, o2, l0, l1, l2, z_ref, y_ref, *tmps):
        ls = [_classes_to_tokens(r, d, tm, tmps) for r, d in zip((l0, l1, l2), DIL)]
        ov = [_classes_to_tokens(r, d, tm, tmps) for r, d in zip((o0, o1, o2), DIL)]
        mx = jnp.maximum(jnp.maximum(ls[0], ls[1]), ls[2])
        es = [jnp.exp(l - mx) for l in ls]
        den = es[0] + es[1] + es[2]
        y = (es[0] / den) * ov[0] + (es[1] / den) * ov[1] + (es[2] / den) * ov[2]
        y_ref[...] = (y * _silu_parts(z_ref[...])[0]).astype(BF16)

    blk = pl.BlockSpec((tm, A_WIDTH), lambda i: (i, 0))
    groups = [_group_spec(S, d, tm) for d in DIL]
    return pl.pallas_call(
        body, name=name, grid=(S // tm,),
        in_specs=groups + groups + [pl.BlockSpec((tm, A_WIDTH), lambda i: (i, R_ZA // A_WIDTH))],
        out_specs=blk, out_shape=jax.ShapeDtypeStruct((S, A_WIDTH), BF16),
        scratch_shapes=[pltpu.VMEM((tm, LANES), F32)] * (A_WIDTH // LANES),
        compiler_params=_cp(("parallel",)))(*[_group_view(t, d) for t, d in zip(os_, DIL)],
                                            *[_group_view(t, d) for t, d in zip(ls_, DIL)], ur)


def _merge_a_bwd(os_, ls_, ur, dya, du_r, *, name):
    S = ur.shape[0]
    tm = min(256, S)

    def body(o0, o1, o2, l0, l1, l2, z_ref, dy_ref, du_in, d0, d1, d2, a0, a1, a2, dz_ref, *tmps):
        ls = [_classes_to_tokens(r, d, tm, tmps) for r, d in zip((l0, l1, l2), DIL)]
        ov = [_classes_to_tokens(r, d, tm, tmps) for r, d in zip((o0, o1, o2), DIL)]
        mx = jnp.maximum(jnp.maximum(ls[0], ls[1]), ls[2])
        es = [jnp.exp(l - mx) for l in ls]
        den = es[0] + es[1] + es[2]
        ws = [e / den for e in es]
        y = ws[0] * ov[0] + ws[1] * ov[1] + ws[2] * ov[2]
        sz, dsz = _silu_parts(z_ref[...])
        dyv = dy_ref[...]
        dz_ref[...] = (dyv * y * dsz).astype(BF16)
        dyp = dyv * sz
        ts = []
        for h in range(A_HEADS):
            sl = slice(h * HEAD, (h + 1) * HEAD)
            t = jnp.zeros((tm, 1), F32)
            for gi in range(3):
                t = t + ws[gi][:, sl][:, :1] * jnp.sum(dyp[:, sl] * ov[gi][:, sl], axis=-1, keepdims=True)
            ts.append(jnp.broadcast_to(t, (tm, HEAD)))
        tb = jnp.concatenate(ts, axis=1)
        for gi, (dref, aref) in enumerate(((d0, a0), (d1, a1), (d2, a2))):
            _tokens_to_classes(ws[gi] * dyp, dref, DIL[gi], tm, tmps)
            _tokens_to_classes(-ws[gi] * tb, aref, DIL[gi], tm, tmps)

    blk = pl.BlockSpec((tm, A_WIDTH), lambda i: (i, 0))
    groups = [_group_spec(S, d, tm) for d in DIL]
    shaped = lambda dt: [jax.ShapeDtypeStruct((S, A_WIDTH) if d == 1 else (d, S // d, A_WIDTH), dt) for d in DIL]
    outs = pl.pallas_call(
        body, name=name, grid=(S // tm,),
        in_specs=groups + groups + [pl.BlockSpec((tm, A_WIDTH), lambda i: (i, R_ZA // A_WIDTH)), blk, ANY],
        out_specs=groups + groups + [pl.BlockSpec((tm, A_WIDTH), lambda i: (i, R_ZA // A_WIDTH))],
        out_shape=shaped(BF16) + shaped(F32) + [jax.ShapeDtypeStruct(du_r.shape, BF16)],
        input_output_aliases={8: 6},
        scratch_shapes=[pltpu.VMEM((tm, LANES), F32)] * (A_WIDTH // LANES),
        compiler_params=_cp(("parallel",)))(*[_group_view(t, d) for t, d in zip(os_, DIL)],
                                            *[_group_view(t, d) for t, d in zip(ls_, DIL)], ur, dya, du_r)
    flat = [t.reshape(S, A_WIDTH) for t in outs[:6]]
    return flat[0:3], flat[3:6], outs[6]


def _logf(ur, bf_pad, *, name):
    S = ur.shape[0]
    tm = min(1024, S)

    def body(u_ref, b_ref, o_ref):
        z = u_ref[...] + b_ref[...]
        o_ref[...] = jnp.minimum(z, 0.0) - jnp.log(1.0 + jnp.exp(-jnp.abs(z)))

    return pl.pallas_call(
        body, name=name, grid=(S // tm,),
        in_specs=[pl.BlockSpec((tm, FB_PAD), lambda i: (i, R_FB // FB_PAD)),
                  pl.BlockSpec((1, FB_PAD), lambda i: (0, 0))],
        out_specs=pl.BlockSpec((tm, FB_PAD), lambda i: (i, 0)),
        out_shape=jax.ShapeDtypeStruct((S, FB_PAD), F32),
        compiler_params=_cp(("parallel",)))(ur, bf_pad)


def _cumsum_lanes(x, reverse, *, name):
    nt, H, _ = x.shape
    R = nt * H

    def body(x_ref, o_ref):
        v = x_ref[...].reshape(R, LANES)
        lane = lax.broadcasted_iota(jnp.int32, (R, LANES), 1)
        row = lax.broadcasted_iota(jnp.int32, (R, LANES), 0)

        def scan(t, step, idx, n, axis):
            while step < n:
                if reverse:
                    t = t + jnp.where(idx < n - step, pltpu.roll(t, n - step, axis), 0.0)
                else:
                    t = t + jnp.where(idx >= step, pltpu.roll(t, step, axis), 0.0)
                step *= 2
            return t

        v = scan(v, 1, lane, LANES, 1)
        total = jnp.broadcast_to(v[:, :1] if reverse else v[:, LANES - 1:], (R, LANES))
        carry = scan(total, H, row, R, 0) - total
        o_ref[...] = (v + carry).reshape(nt, H, LANES)

    return pl.pallas_call(
        body, name=name, out_shape=jax.ShapeDtypeStruct((nt, H, LANES), F32),
        in_specs=[pl.BlockSpec(memory_space=pltpu.VMEM)], out_specs=pl.BlockSpec(memory_space=pltpu.VMEM),
        compiler_params=_cp())(x)


B_SCALE = B_HEAD ** -0.5


def _pair_masks():
    lane = lax.broadcasted_iota(jnp.int32, (1, LANES), 1)
    row = lax.broadcasted_iota(jnp.int32, (LANES, 1), 0)
    return (lane < B_HEAD, lane >= B_HEAD), (row < B_HEAD, row >= B_HEAD)


def _causal_t(T):
    r = lax.broadcasted_iota(jnp.int32, (T, T), 0)
    c = lax.broadcasted_iota(jnp.int32, (T, T), 1)
    return r <= c


def _zero_other(x, keep):
    return jnp.where(keep, x, jnp.zeros_like(x))


def _fox_aug(ub, c, *, name):
    S = ub.shape[0]
    T = min(2048, S)

    def body(q_ref, k_ref, c_ref, qa_ref, ka_ref):
        lane = lax.broadcasted_iota(jnp.int32, (1, LANES), 1)
        q = q_ref[...] * B_SCALE
        k = k_ref[...]
        for a in range(2):
            own = (lane < B_HEAD) if a == 0 else (lane >= B_HEAD)
            o = B_HEAD if a == 0 else 0
            cv = jnp.broadcast_to(c_ref[:, a:a + 1], (T, LANES))
            hi = cv.astype(BF16)
            r1 = cv - hi.astype(F32)
            mid = r1.astype(BF16)
            lo = (r1 - mid.astype(F32)).astype(BF16)
            pieces = (hi, mid, lo)
            one = jnp.ones((T, LANES), BF16)
            qa = jnp.where(own, q, jnp.zeros_like(q))
            ka = jnp.where(own, k, jnp.zeros_like(k))
            for t in range(3):
                qa = jnp.where(lane == o + t, pieces[t], qa)
                qa = jnp.where(lane == o + 3 + t, one, qa)
                ka = jnp.where(lane == o + t, one, ka)
                ka = jnp.where(lane == o + 3 + t, -pieces[t], ka)
            qa_ref[a] = qa
            ka_ref[a] = ka

    out = pl.BlockSpec((2, T, LANES), lambda h, i: (h, i, 0))
    c_pairs = c.reshape(B_HEADS // 2, 2, S).transpose(0, 2, 1)
    return pl.pallas_call(
        body, name=name, grid=(B_HEADS // 2, S // T),
        in_specs=[pl.BlockSpec((T, LANES), lambda h, i: (i, h)), pl.BlockSpec((T, LANES), lambda h, i: (i, 4 + h)),
                  pl.BlockSpec((None, T, 2), lambda h, i: (h, i, 0))],
        out_specs=[out, out], out_shape=[jax.ShapeDtypeStruct((B_HEADS, S, LANES), BF16)] * 2,
        compiler_params=_cp(("parallel", "parallel")))(ub, ub, c_pairs)


def _fox_fwd(qaug, kaug, vt, *, name):
    S = qaug.shape[1]
    T = min(512, S)
    nq = S // T

    def body(q_ref, k_ref, vt_ref, o_ref, l_ref, m_s, l_s, acc_s, st_s):
        i = pl.program_id(1)
        _, rows = _pair_masks()
        qm = [q_ref[0], q_ref[1]]
        m_s[...] = jnp.full((2, 1, T), NEG, F32)
        l_s[...] = jnp.zeros((2, 1, T), F32)
        acc_s[...] = jnp.zeros((LANES, T), F32)

        def logits(j):
            off = pl.multiple_of(j * T, T)
            return [_dot_nt(k_ref[a, pl.ds(off, T), :], qm[a]) for a in range(2)]

        def step(j, masked, prefetch):
            nxt = logits(j + 1) if prefetch else None
            vtj = vt_ref[j]
            upd = jnp.zeros((LANES, T), F32)
            alphas = []
            for a in range(2):
                st = st_s[a]
                if masked:
                    st = jnp.where(_causal_t(T), st, NEG)
                m_old = m_s[a]
                m_new = jnp.maximum(m_old, jnp.max(st, axis=0, keepdims=True))
                alpha = jnp.exp(m_old - m_new)
                pt = jnp.exp(st - m_new)
                l_s[a] = alpha * l_s[a] + jnp.sum(pt, axis=0, keepdims=True)
                m_s[a] = m_new
                upd = upd + _dot(_zero_other(vtj, rows[a]), pt.astype(BF16))
                alphas.append(alpha)
            acc_s[...] = acc_s[...] * jnp.where(rows[0], alphas[0], alphas[1]) + upd
            if prefetch:
                st_s[0] = nxt[0]
                st_s[1] = nxt[1]

        def loop(j, carry):
            step(j, False, True)
            return carry

        first = logits(0)
        st_s[0] = first[0]
        st_s[1] = first[1]
        lax.fori_loop(0, i, loop, 0)
        step(i, True, False)
        o_ref[...] = (acc_s[...] / jnp.where(rows[0], l_s[0], l_s[1])).T
        l_ref[0] = m_s[0] + jnp.log(l_s[0])
        l_ref[1] = m_s[1] + jnp.log(l_s[1])

    stat = pl.BlockSpec((2, None, 1, T), lambda h, i: (h, i, 0, 0))
    return pl.pallas_call(
        body, name=name, grid=(B_HEADS // 2, nq),
        in_specs=[pl.BlockSpec((2, T, LANES), lambda h, i: (h, i, 0)),
                  pl.BlockSpec((2, S, LANES), lambda h, i: (h, 0, 0)),
                  pl.BlockSpec((nq, LANES, T), lambda h, i: (0, h, 0))],
        out_specs=[pl.BlockSpec((T, LANES), lambda h, i: (i, h)), stat],
        out_shape=[jax.ShapeDtypeStruct((S, A_WIDTH), F32), jax.ShapeDtypeStruct((B_HEADS, nq, 1, T), F32)],
        scratch_shapes=[pltpu.VMEM((2, 1, T), F32), pltpu.VMEM((2, 1, T), F32), pltpu.VMEM((LANES, T), F32),
                        pltpu.VMEM((2, T, T), F32)],
        compiler_params=_cp(("parallel", "parallel")),
    )(qaug, kaug, vt)


def _fox_delta(o, do, *, name):
    S = o.shape[0]
    T = min(512, S)
    nq = S // T

    per = min(4, nq)

    def body(o_ref, do_ref, d_ref):
        _, rows = _pair_masks()
        for t in range(per):
            sl = slice(t * T, (t + 1) * T)
            prod_t = (do_ref[sl, :].astype(F32) * o_ref[sl, :]).T
            d_ref[0, t] = jnp.sum(_zero_other(prod_t, rows[0]), axis=0, keepdims=True)
            d_ref[1, t] = jnp.sum(_zero_other(prod_t, rows[1]), axis=0, keepdims=True)

    tile = pl.BlockSpec((per * T, LANES), lambda h, i: (i, h))
    return pl.pallas_call(
        body, name=name, grid=(B_HEADS // 2, nq // per), in_specs=[tile, tile],
        out_specs=pl.BlockSpec((2, per, 1, T), lambda h, i: (h, i, 0, 0)),
        out_shape=jax.ShapeDtypeStruct((B_HEADS, nq, 1, T), F32),
        compiler_params=_cp(("parallel", "parallel")))(o, do)


def _fox_bwd(ub, qaug, kaug, kt, do, lse, delta, *, name):
    S = ub.shape[0]
    T = min(512, S)
    nq = S // T

    def body(k_ref, v_ref, kt_ref, q_ref, do_ref, l_ref, dl_ref,
             dk_ref, dv_ref, dck_ref, dqt_ref, dcq_ref, dk_s, dv_s, dc_s):
        j = pl.program_id(1)
        lanes, rows = _pair_masks()
        vv = v_ref[...]
        ktj = kt_ref[...]
        km = [k_ref[0], k_ref[1]]
        ktm = [_zero_other(ktj, rows[0]), _zero_other(ktj, rows[1])]
        dk_s[...] = jnp.zeros((2, T, LANES), F32)
        dv_s[...] = jnp.zeros((T, LANES), F32)
        dc_s[...] = jnp.zeros((2, T, 1), F32)

        @pl.when(j == 0)
        def _():
            dqt_ref[...] = jnp.zeros((nq, LANES, T), F32)
            dcq_ref[...] = jnp.zeros((2, nq, 1, T), F32)

        def step(i, masked):
            off = pl.multiple_of(i * T, T)
            doi = do_ref[pl.ds(off, T), :]
            upd = jnp.zeros((LANES, T), F32)
            for a in range(2):
                qi = q_ref[a, pl.ds(off, T), :]
                st = _dot_nt(km[a], qi)
                if masked:
                    st = jnp.where(_causal_t(T), st, NEG)
                pt = jnp.exp(st - l_ref[a, i])
                doa = _zero_other(doi, lanes[a])
                dv_s[...] += _dot(pt.astype(BF16), doa)
                dst = pt * (_dot_nt(vv, doa) - dl_ref[a, i])
                dsb = dst.astype(BF16)
                dk_s[a] += _dot(dsb, qi)
                upd = upd + _dot(ktm[a], dsb)
                dc_s[a] -= jnp.sum(dst, axis=-1, keepdims=True)
                dcq_ref[a, i] += jnp.sum(dst, axis=0, keepdims=True)
            dqt_ref[i] += upd

        def loop(i, carry):
            step(i, False)
            return carry

        step(j, True)
        lax.fori_loop(j + 1, nq, loop, 0)
        dk_ref[...] = jnp.where(lanes[0], dk_s[0], dk_s[1]).astype(BF16)
        dv_ref[...] = dv_s[...].astype(BF16)
        dck_ref[...] = dc_s[...]

    rowv = pl.BlockSpec((2, nq, 1, T), lambda h, j: (h, 0, 0, 0))
    tile = pl.BlockSpec((T, LANES), lambda h, j: (j, h))
    return pl.pallas_call(
        body, name=name, grid=(B_HEADS // 2, nq),
        in_specs=[pl.BlockSpec((2, T, LANES), lambda h, j: (h, j, 0)),
                  pl.BlockSpec((T, LANES), lambda h, j: (j, 8 + h)),
                  pl.BlockSpec((None, LANES, T), lambda h, j: (j, h, 0)),
                  pl.BlockSpec((2, S, LANES), lambda h, j: (h, 0, 0)),
                  pl.BlockSpec((S, LANES), lambda h, j: (0, h)),
                  rowv, rowv],
        out_specs=[tile, tile, pl.BlockSpec((2, T, 1), lambda h, j: (h, j, 0)),
                   pl.BlockSpec((nq, LANES, T), lambda h, j: (0, h, 0)), rowv],
        out_shape=[jax.ShapeDtypeStruct((S, A_WIDTH), BF16)] * 2 + [jax.ShapeDtypeStruct((B_HEADS, S, 1), F32),
                   jax.ShapeDtypeStruct((nq, A_WIDTH, T), F32), jax.ShapeDtypeStruct((B_HEADS, nq, 1, T), F32)],
        scratch_shapes=[pltpu.VMEM((2, T, LANES), F32), pltpu.VMEM((T, LANES), F32), pltpu.VMEM((2, T, 1), F32)],
        compiler_params=_cp(("parallel", "arbitrary")),
    )(kaug, ub, kt, qaug, do, lse, delta)


def _gate_fwd(o, ur, zcol, *, name):
    S = ur.shape[0]
    tm = min(1024, S)

    def body(o_ref, z_ref, y_ref):
        y_ref[...] = (o_ref[...] * _silu_parts(z_ref[...])[0]).astype(BF16)

    blk = pl.BlockSpec((tm, A_WIDTH), lambda i: (i, 0))
    return pl.pallas_call(
        body, name=name, grid=(S // tm,),
        in_specs=[blk, pl.BlockSpec((tm, A_WIDTH), lambda i: (i, zcol // A_WIDTH))],
        out_specs=blk, out_shape=jax.ShapeDtypeStruct((S, A_WIDTH), BF16),
        compiler_params=_cp(("parallel",)))(o, ur)


def _gate_bwd(o, ur, zcol, dy, du_r, *, name):
    S = ur.shape[0]
    tm = min(1024, S)

    def body(o_ref, z_ref, dy_ref, du_in, do_ref, dz_ref):
        sz, dsz = _silu_parts(z_ref[...])
        dyv = dy_ref[...]
        do_ref[...] = (dyv * sz).astype(BF16)
        dz_ref[...] = (dyv * o_ref[...] * dsz).astype(BF16)

    blk = pl.BlockSpec((tm, A_WIDTH), lambda i: (i, 0))
    gate = pl.BlockSpec((tm, A_WIDTH), lambda i: (i, zcol // A_WIDTH))
    return pl.pallas_call(
        body, name=name, grid=(S // tm,),
        in_specs=[blk, gate, blk, ANY],
        out_specs=[blk, gate],
        out_shape=[jax.ShapeDtypeStruct((S, A_WIDTH), BF16), jax.ShapeDtypeStruct(du_r.shape, BF16)],
        input_output_aliases={3: 1},
        compiler_params=_cp(("parallel",)))(o, ur, dy, du_r)


def _dfb(ur, bf_pad, dlogf_pad, du_r, *, name):
    S = ur.shape[0]
    tm = min(1024, S)

    def body(u_ref, b_ref, d_ref, du_in, o_ref, s_ref):
        i = pl.program_id(0)
        dv = d_ref[...] * _sigmoid(-(u_ref[...] + b_ref[...]))
        o_ref[...] = dv.astype(BF16)
        part = jnp.sum(dv, axis=0, keepdims=True)

        @pl.when(i == 0)
        def _():
            s_ref[...] = part

        @pl.when(i > 0)
        def _():
            s_ref[...] += part

    vec = pl.BlockSpec((1, FB_PAD), lambda i: (0, 0))
    blk = pl.BlockSpec((tm, FB_PAD), lambda i: (i, 0))
    fb = pl.BlockSpec((tm, FB_PAD), lambda i: (i, R_FB // FB_PAD))
    return pl.pallas_call(
        body, name=name, grid=(S // tm,),
        in_specs=[fb, vec, blk, ANY],
        out_specs=[fb, vec],
        out_shape=[jax.ShapeDtypeStruct(du_r.shape, BF16), jax.ShapeDtypeStruct((1, FB_PAD), F32)],
        input_output_aliases={3: 0},
        compiler_params=_cp(("arbitrary",)))(ur, bf_pad, dlogf_pad, du_r)


M_SCALE = HEAD ** -0.5


def _mem_fwd(ur, mkv, *, name):
    S = ur.shape[0]
    T = min(512, S)

    def body(q_ref, z_ref, k_ref, v_ref, y_ref):
        for h in range(M_HEADS):
            hs = slice(h * HEAD, (h + 1) * HEAD)
            s = _dot_nt(q_ref[:, hs].astype(BF16), k_ref[:, hs].astype(BF16)) * M_SCALE
            p = jnp.exp(s - jnp.max(s, axis=-1, keepdims=True))
            p = p / jnp.sum(p, axis=-1, keepdims=True)
            o = _dot(p.astype(BF16), v_ref[:, hs].astype(BF16))
            y_ref[:, hs] = (o * _silu_parts(z_ref[:, hs])[0]).astype(BF16)

    wide = lambda col: pl.BlockSpec((T, A_WIDTH), lambda i: (i, col // A_WIDTH))
    kv = lambda half: pl.BlockSpec((N_MEM, A_WIDTH), lambda i: (0, half))
    return pl.pallas_call(
        body, name=name, grid=(S // T,),
        in_specs=[wide(R_QM), wide(R_ZM), kv(0), kv(1)],
        out_specs=pl.BlockSpec((T, A_WIDTH), lambda i: (i, 0)),
        out_shape=jax.ShapeDtypeStruct((S, A_WIDTH), BF16),
        compiler_params=_cp(("parallel",)))(ur, ur, mkv, mkv)


def _mem_bwd(ur, mkv, dy, du_r, *, name):
    S = ur.shape[0]
    T = min(512, S)

    def body(q_ref, z_ref, k_ref, v_ref, dy_ref, du_in, du_ref, dk_ref, dv_ref):
        i = pl.program_id(0)

        @pl.when(i == 0)
        def _():
            dk_ref[...] = jnp.zeros((N_MEM, A_WIDTH), F32)
            dv_ref[...] = jnp.zeros((N_MEM, A_WIDTH), F32)

        for h in range(M_HEADS):
            hs = slice(h * HEAD, (h + 1) * HEAD)
            qv = q_ref[:, hs].astype(BF16)
            kv = k_ref[:, hs].astype(BF16)
            vv = v_ref[:, hs].astype(BF16)
            s = _dot_nt(qv, kv) * M_SCALE
            p = jnp.exp(s - jnp.max(s, axis=-1, keepdims=True))
            p = p / jnp.sum(p, axis=-1, keepdims=True)
            o = _dot(p.astype(BF16), vv)
            sz, dsz = _silu_parts(z_ref[:, hs])
            dyv = dy_ref[:, hs]
            du_ref[:, A_WIDTH + h * HEAD:A_WIDTH + (h + 1) * HEAD] = (dyv * o * dsz).astype(BF16)
            dov = (dyv * sz).astype(BF16)
            dp = _dot_nt(dov, vv)
            ds = p * (dp - jnp.sum(p * dp, axis=-1, keepdims=True))
            du_ref[:, hs] = (_dot(ds.astype(BF16), kv) * M_SCALE).astype(BF16)
            dv_ref[:, hs] += _dot(p.T.astype(BF16), dov)
            dk_ref[:, hs] += _dot(ds.T.astype(BF16), qv) * M_SCALE

    wide = lambda col: pl.BlockSpec((T, A_WIDTH), lambda i: (i, col // A_WIDTH))
    kv = lambda half: pl.BlockSpec((N_MEM, A_WIDTH), lambda i: (0, half))
    tile = pl.BlockSpec((T, A_WIDTH), lambda i: (i, 0))
    acc = pl.BlockSpec((N_MEM, A_WIDTH), lambda i: (0, 0))
    assert R_ZM == R_QM + A_WIDTH and R_QM % (2 * A_WIDTH) == 0
    return pl.pallas_call(
        body, name=name, grid=(S // T,),
        in_specs=[wide(R_QM), wide(R_ZM), kv(0), kv(1), tile, ANY],
        out_specs=[pl.BlockSpec((T, 2 * A_WIDTH), lambda i: (i, R_QM // (2 * A_WIDTH))), acc, acc],
        out_shape=[jax.ShapeDtypeStruct(du_r.shape, BF16)] + [jax.ShapeDtypeStruct((N_MEM, A_WIDTH), F32)] * 2,
        input_output_aliases={5: 0},
        compiler_params=_cp(("arbitrary",)))(ur, ur, mkv, mkv, dy, du_r)


def _branch_fwd(ys, wbs, ur, b_merge, *, name):
    S = ur.shape[0]
    tm, tn = min(512, S), 512
    nj = D_MODEL // tn

    def body(ya, yb, ym, wa, wb, wm, g0, g1, g2, b0, b1, b2, mg_ref, p_ref):
        acc = jnp.zeros((tm, tn), F32)
        for i, (y, w, gr, br) in enumerate(((ya, wa, g0, b0), (yb, wb, g1, b1), (ym, wm, g2, b2))):
            pr = _dot(y[...], w[...])
            p_ref[i] = pr.astype(BF16)
            acc = acc + _sigmoid(gr[...] + br[...]) * pr
        mg_ref[...] = acc.astype(BF16)

    yspec = pl.BlockSpec((tm, A_WIDTH), lambda i, j: (i, 0))
    wspec = pl.BlockSpec((A_WIDTH, tn), lambda i, j: (0, j))
    gspec = lambda b: pl.BlockSpec((tm, tn), lambda i, j: (i, (R_GL + b * D_MODEL) // tn + j))
    bspec = lambda b: pl.BlockSpec((1, tn), lambda i, j: (0, b * nj + j))
    return pl.pallas_call(
        body, name=name, grid=(S // tm, nj),
        in_specs=[yspec] * 3 + [wspec] * 3 + [gspec(0), gspec(1), gspec(2), bspec(0), bspec(1), bspec(2)],
        out_specs=[pl.BlockSpec((tm, tn), lambda i, j: (i, j)),
                   pl.BlockSpec((3, tm, tn), lambda i, j: (0, i, j))],
        out_shape=[jax.ShapeDtypeStruct((S, D_MODEL), BF16), jax.ShapeDtypeStruct((3, S, D_MODEL), BF16)],
        compiler_params=_cp(("parallel", "parallel")))(*ys, *wbs, ur, ur, ur, b_merge, b_merge, b_merge)


def _branch_bwd(dm, prods, ur, b_merge, *, name):
    S = ur.shape[0]
    tm = min(256, S)

    def body(dm_ref, p_ref, g0, g1, g2, b_ref, dp0, dp1, dp2, dgl_ref, db_ref):
        i = pl.program_id(0)
        dmv = dm_ref[...]
        parts = []
        for b, (gr, dp_ref) in enumerate(((g0, dp0), (g1, dp1), (g2, dp2))):
            sl = slice(b * D_MODEL, (b + 1) * D_MODEL)
            gt = _sigmoid(gr[...] + b_ref[:, sl])
            dp_ref[...] = (dmv * gt).astype(BF16)
            dgl = dmv * p_ref[b].astype(F32) * gt * (1.0 - gt)
            dgl_ref[:, R_GL + b * D_MODEL:R_GL + (b + 1) * D_MODEL] = dgl.astype(BF16)
            parts.append(jnp.sum(dgl, axis=0, keepdims=True))
        part = jnp.concatenate(parts, axis=1)

        @pl.when(i == 0)
        def _():
            db_ref[...] = part

        @pl.when(i > 0)
        def _():
            db_ref[...] += part

    gspec = lambda b: pl.BlockSpec((tm, D_MODEL), lambda i: (i, R_GL // D_MODEL + b))
    vec = pl.BlockSpec((1, 3 * D_MODEL), lambda i: (0, 0))
    row = pl.BlockSpec((tm, D_MODEL), lambda i: (i, 0))
    outs = pl.pallas_call(
        body, name=name, grid=(S // tm,),
        in_specs=[row, pl.BlockSpec((3, tm, D_MODEL), lambda i: (0, i, 0)), gspec(0), gspec(1), gspec(2), vec],
        out_specs=[row, row, row, pl.BlockSpec((tm, NR), lambda i: (i, 0)), vec],
        out_shape=[jax.ShapeDtypeStruct((S, D_MODEL), BF16)] * 3
        + [jax.ShapeDtypeStruct((S, NR), BF16), jax.ShapeDtypeStruct((1, 3 * D_MODEL), F32)],
        compiler_params=_cp(("arbitrary",)))(dm, prods, ur, ur, ur, b_merge)
    return outs[0:3], outs[3], outs[4]


def _rope_tables(pos):
    half = ROT // 2
    S = pos.shape[0]
    inv = ROPE_THETA ** (-jnp.arange(half, dtype=F32) / half)
    per_row = LANES // half
    ang = jnp.repeat(pos.astype(F32).reshape(S // per_row, per_row), half, axis=1) * jnp.tile(inv, per_row)
    cos, sin = lax.optimization_barrier((jnp.cos(ang).reshape(S, half), jnp.sin(ang).reshape(S, half)))
    one = jnp.ones((S, LANES - ROT), F32)
    zero = jnp.zeros((S, LANES - ROT), F32)
    zh = jnp.zeros((S, half), F32)
    c = jnp.concatenate([cos, cos, one], axis=1)
    s1 = jnp.concatenate([-sin, zh, zero], axis=1)
    s2 = jnp.concatenate([zh, sin, zero], axis=1)
    return c, s1, s2


def _to_tiles(t):
    S, H = t.shape
    return t.reshape(S // LANES, LANES, H).transpose(0, 2, 1)


def _from_tiles(t):
    nt, H, _ = t.shape
    return t.transpose(1, 0, 2).reshape(H, nt * LANES)


def _local_step(x, mem, pos, tgt, g_pre, g_post, g_mem, wt, bf_pad, b_merge, w_kv, wbs, w_out, pack=None, hs=None):
    S = x.shape[0]
    T = min(512, S)
    nq = S // T
    tabs = _rope_tables(pos)

    if hs is None:
        hs = _rms_fwd(x, g_pre, name="rms_pre", dilations=DIL)
    h = hs[0]
    tabs_g = [[_to_classes(t, d) for t in tabs] for d in DIL]
    qkvs = [_proj_rope(hs[g], wt[f"A{g}"], tabs_g[g], name=f"proj_a{g}") for g in range(3)]
    ub = _mm(h, wt["B"], bt=True, out_dtype=BF16, name="proj_b", tn=1536)
    ur = _mm(h, wt["R"], bt=True, name="proj_r", tn=1792)

    outs_c, lses_c = [], []
    for g in range(3):
        o, l = _attn_a_fwd(qkvs[g], g, name=f"attn_a_fwd{g}")
        outs_c.append(o)
        lses_c.append(l)
    ya = _merge_a_fwd(outs_c, lses_c, ur, name="merge_a_fwd")

    logf = _logf(ur, bf_pad, name="logf")
    c = _from_tiles(_cumsum_lanes(_to_tiles(logf[:, :B_HEADS]), False, name="cumsum_fwd"))
    qaug, kaug = _fox_aug(ub, c, name="fox_aug")
    kt = ub[:, 512:1024].reshape(nq, T, 512).transpose(0, 2, 1)
    vt = ub[:, 1024:1536].reshape(nq, T, 512).transpose(0, 2, 1)
    ob, lse_b = _fox_fwd(qaug, kaug, vt, name="fox_fwd")
    yb = _gate_fwd(ob, ur, R_ZB, name="gate_b_fwd")

    hm = _rms_fwd(mem, g_mem, name="rms_mem")
    mkv = _mm(hm, w_kv, name="proj_mem")
    ym = _mem_fwd(ur, mkv, name="mem_fwd")

    merged, prods = _branch_fwd((ya, yb, ym), wbs, ur, b_merge, name="branch_fwd")
    out = _mm(merged, w_out, name="proj_out")
    dy, d_out, dg_post, loss_row = _post(x, out, tgt, g_post, name="post")

    dmerged = _mm(d_out, w_out, bt=True, name="d_merged")
    dw_out = _mm(merged, d_out, at=True, name="dw_out", tk=2048)
    dprods, du_r, db_merge = _branch_bwd(dmerged, prods, ur, b_merge, name="branch_bwd")
    dys, dwbs = [], []
    for i, (y, wb) in enumerate(zip((ya, yb, ym), wbs)):
        dys.append(_mm(dprods[i], wb, bt=True, name=f"d_y{i}"))
        dwbs.append(_mm(y, dprods[i], at=True, name=f"dw_branch{i}", tk=2048))

    dos_c, adjs_c, du_r = _merge_a_bwd(outs_c, lses_c, ur, dys[0], du_r, name="merge_a_bwd")
    dus_a = []
    for g, d in enumerate(DIL):
        do_c, adj_c = dos_c[g], adjs_c[g]
        du = _attn_a_dkv(qkvs[g], tabs_g[g], g, do_c, lses_c[g], adj_c, name=f"attn_a_dkv{g}")
        dus_a.append(_attn_a_dq(qkvs[g], tabs_g[g], g, do_c, lses_c[g], adj_c, du, name=f"attn_a_dq{g}"))

    dob, du_r = _gate_bwd(ob, ur, R_ZB, dys[1], du_r, name="gate_b_bwd")
    delta_b = _fox_delta(ob, dob, name="fox_delta")
    dkb, dvb, dc_k, dqt, dc_q = _fox_bwd(ub, qaug, kaug, kt, dob, lse_b, delta_b, name="fox_bwd")
    dqb = (dqt.transpose(0, 2, 1).reshape(S, A_WIDTH) * B_SCALE).astype(BF16)
    du_b = jnp.concatenate([dqb, dkb, dvb], axis=1)
    dc = dc_q.reshape(B_HEADS, S) + dc_k.reshape(B_HEADS, S)
    dlogf = _from_tiles(_cumsum_lanes(_to_tiles(dc.T), True, name="cumsum_bwd"))
    dlogf_pad = jnp.pad(dlogf.T, ((0, 0), (0, FB_PAD - B_HEADS)))
    du_r, db_forget = _dfb(ur, bf_pad, dlogf_pad, du_r, name="dfb")

    du_r, dmk, dmv = _mem_bwd(ur, mkv, dys[2], du_r, name="mem_bwd")
    dmkv = jnp.concatenate([dmk, dmv], axis=1).astype(BF16)
    dhm = _mm(dmkv, w_kv, bt=True, name="d_hm")
    dw_kv = _mm(hm, dmkv, at=True, name="dw_kv")
    dg_mem = _rms_bwd(mem, g_mem, dhm, None, name="rms_mem_bwd")

    dwt ={"R": _mm(du_r, h, at=True, name="dw_in_r", tm=1792, tk=1024),
           "B": _mm(du_b, h, at=True, name="dw_in_b", tm=1536, tk=2048)}
    for g in range(3):
        dwt[f"A{g}"] = _mm(dus_a[g], hs[g], at=True, name=f"dw_in_a{g}", tm=1536, tk=2048)
    res = dict(dwt=dwt, dw_kv=dw_kv, dwbs=dwbs, dw_out=dw_out)
    token_major = [(du_r, wt["R"]), (du_b, wt["B"]), (dus_a[0], wt["A0"])]
    if pack is None:
        dh_1 = _mm(dus_a[1], wt["A1"], name="d_h_a1", tk=1536)
        dh = _mm_sum(token_major, name="d_h_main")
    else:
        gbig = pack(dwt, dw_kv, dwbs, dw_out)
        own_idx = _own_slabs()
        dh_1, sib = _mm(dus_a[1], wt["A1"], name="d_h_a1", tk=1536, comm=_pair_comm(gbig))
        send = _pair_sum(gbig, sib, own_idx, 208, name="pair_sum")
        dh, recv = _mm_sum(token_major, name="d_h_main", comm=_chips_comm(send))
        res = dict(parts=[(gbig, None), (sib, 1), (recv, N_CHIP - 1)], own_idx=own_idx)
    dh_2 = _mm(dus_a[2], wt["A2"], name="d_h_a2", tk=1536)
    grad_x, dg_pre = _rms_bwd(x, g_pre, dh, dy, name="rms_pre_bwd", dh_classes=[(dh_1, DIL[1]), (dh_2, DIL[2])])

    return dict(res, loss=loss_row, grad_x=grad_x, dg_pre=dg_pre, dg_post=dg_post, dg_mem=dg_mem,
                db_forget=db_forget, db_merge=db_merge)


MESH = pl.DeviceIdType.MESH
ANY = pl.BlockSpec(memory_space=pl.ANY)


def _relations():
    return [(k >> 2 & 1, k >> 1 & 1, k & 1) for k in range(1, N_DEV)]


def _coords():
    return lax.axis_index("x"), lax.axis_index("y"), lax.axis_index("c")


def _gather_comm(shard):
    R, W = shard.shape

    def plan(x_ref, out_ref, send_sems, recv_sems, local_sem):
        x, y, c = _coords()
        me, sibling = (x, y, c), (x, y, 1 - c)
        chips = [(1 - x, y), (x, 1 - y), (1 - x, 1 - y)]

        def slot(px, py, pc):
            return out_ref.at[4 * px + 2 * py + pc]

        def copy(k, block, to, src=None):
            return pltpu.make_async_remote_copy(
                src_ref=slot(*block) if src is None else src, dst_ref=slot(*block),
                send_sem=send_sems.at[k], recv_sem=recv_sems.at[k], device_id=to, device_id_type=MESH)

        mine = pltpu.make_async_copy(x_ref, slot(*me), local_sem)
        first = [copy(0, me, sibling, src=x_ref)]
        first += [copy(1 + j, me, (*chip, c), src=x_ref) for j, chip in enumerate(chips)]
        return me, sibling, chips, c, copy, mine, first

    def start(*refs):
        _, _, _, _, _, mine, first = plan(*refs)
        mine.start()
        for cp in first:
            cp.start()

    def wait(*refs):
        me, sibling, chips, c, copy, mine, first = plan(*refs)
        passed = [copy(4 + j, (*chip, c), sibling) for j, chip in enumerate(chips)]
        for j, chip in enumerate(chips):
            copy(1 + j, (*chip, c), me).wait_recv()
            passed[j].start()
        copy(0, sibling, me).wait_recv()
        for j, chip in enumerate(chips):
            copy(4 + j, (*chip, 1 - c), me).wait_recv()
        for cp in first + passed:
            cp.wait_send()
        mine.wait()

    return dict(inputs=[shard], out_shape=[jax.ShapeDtypeStruct((N_DEV, R, W), shard.dtype)],
                sems=[pltpu.SemaphoreType.DMA((N_DEV - 1,)), pltpu.SemaphoreType.DMA((N_DEV - 1,)),
                      pltpu.SemaphoreType.DMA],
                start=start, wait=wait)


N_CHIP = 4


def _pair_comm(gbig):
    _, R, W = gbig.shape

    def copies(g_ref, sib_ref, send_sems, recv_sems):
        x, y, c = _coords()
        return [pltpu.make_async_remote_copy(
            src_ref=g_ref.at[4 * (x ^ (r >> 1)) + 2 * (y ^ (r & 1)) + (1 - c)], dst_ref=sib_ref.at[r],
            send_sem=send_sems.at[r], recv_sem=recv_sems.at[r], device_id=(x, y, 1 - c), device_id_type=MESH)
            for r in range(N_CHIP)]

    def start(*refs):
        for cp in copies(*refs):
            cp.start()

    def wait(*refs):
        cps = copies(*refs)
        for cp in cps:
            cp.wait_recv()
        for cp in cps:
            cp.wait_send()

    return dict(inputs=[gbig], out_shape=[jax.ShapeDtypeStruct((N_CHIP, R, W), gbig.dtype)],
                sems=[pltpu.SemaphoreType.DMA((N_CHIP,)), pltpu.SemaphoreType.DMA((N_CHIP,))],
                start=start, wait=wait)


def _own_slabs():
    x, y, c = _coords()
    return jnp.stack([4 * (x ^ (r >> 1)) + 2 * (y ^ (r & 1)) + c for r in range(N_CHIP)]).astype(jnp.int32)


def _pair_sum(gbig, sib, own_idx, tr, *, name):
    _, R, W = gbig.shape

    def body(idx_ref, a_ref, b_ref, o_ref):
        o_ref[...] = (a_ref[...] + b_ref[...]).astype(BF16)

    return pl.pallas_call(
        body, name=name,
        grid_spec=pltpu.PrefetchScalarGridSpec(
            num_scalar_prefetch=1, grid=(N_CHIP - 1, R // tr),
            in_specs=[pl.BlockSpec((None, tr, W), lambda r, i, idx: (idx[r + 1], i, 0)),
                      pl.BlockSpec((None, tr, W), lambda r, i, idx: (r + 1, i, 0))],
            out_specs=pl.BlockSpec((None, tr, W), lambda r, i, idx: (r, i, 0))),
        out_shape=jax.ShapeDtypeStruct((N_CHIP - 1, R, W), BF16),
        compiler_params=_cp(("parallel", "parallel")))(own_idx, gbig, sib)


def _chips_comm(send):
    nb, R, W = send.shape

    def copies(b_ref, rb_ref, send_sems, recv_sems):
        x, y, c = _coords()
        return [pltpu.make_async_remote_copy(
            src_ref=b_ref.at[r - 1], dst_ref=rb_ref.at[r - 1], send_sem=send_sems.at[r - 1],
            recv_sem=recv_sems.at[r - 1], device_id=(x ^ (r >> 1), y ^ (r & 1), c), device_id_type=MESH)
            for r in range(1, N_CHIP)]

    def start(*refs):
        for cp in copies(*refs):
            cp.start()

    def wait(*refs):
        cps = copies(*refs)
        for cp in cps:
            cp.wait_recv()
        for cp in cps:
            cp.wait_send()

    return dict(inputs=[send], out_shape=[jax.ShapeDtypeStruct((nb, R, W), send.dtype)],
                sems=[pltpu.SemaphoreType.DMA((nb,)), pltpu.SemaphoreType.DMA((nb,))],
                start=start, wait=wait)


def _gather_small(gsmall, *, name):
    n = N_DEV - 1

    def body(s_ref, rs_ref, send_sems, recv_sems, local_sem):
        x, y, c = _coords()
        me = 4 * x + 2 * y + c
        mine = pltpu.make_async_copy(s_ref, rs_ref.at[me], local_sem)
        mine.start()

        def copy(k, fx, fy, fc, slot):
            return pltpu.make_async_remote_copy(
                src_ref=s_ref, dst_ref=rs_ref.at[slot], send_sem=send_sems.at[k], recv_sem=recv_sems.at[k],
                device_id=(x ^ fx, y ^ fy, c ^ fc), device_id_type=MESH)

        started = [copy(k, *rel, me) for k, rel in enumerate(_relations())]
        for cp in started:
            cp.start()
        for k, (fx, fy, fc) in enumerate(_relations()):
            copy(k, fx, fy, fc, 4 * (x ^ fx) + 2 * (y ^ fy) + (c ^ fc)).wait_recv()
        for cp in started:
            cp.wait_send()
        mine.wait()

    return pl.pallas_call(
        body, name=name, out_shape=jax.ShapeDtypeStruct((N_DEV, 1, P_SMALL), gsmall.dtype),
        in_specs=[ANY], out_specs=ANY,
        scratch_shapes=[pltpu.SemaphoreType.DMA((n,)), pltpu.SemaphoreType.DMA((n,)), pltpu.SemaphoreType.DMA],
    )(gsmall)


def _part_specs(parts, tr, row0):
    assert row0 % tr == 0
    specs = []
    for a, n_used in parts:
        if n_used is None:
            specs.append(pl.BlockSpec((1, tr, a.shape[2]), lambda i, idx: (idx[0], row0 // tr + i, 0)))
        else:
            specs.append(pl.BlockSpec((n_used, tr, a.shape[2]), lambda i, idx: (0, row0 // tr + i, 0)))
    return specs


def _part_total(refs, parts):
    g = None
    for ref, (_, n_used) in zip(refs, parts):
        for k in range(n_used or 1):
            t = ref[k].astype(F32)
            g = t if g is None else g + t
    return g


def _sum_parts(parts, idx, row0, nrows, tr, *, name):
    W = parts[0][0].shape[2]
    assert nrows % tr == 0

    def body(idx_ref, *refs):
        refs[-1][...] = _part_total(refs[:-1], parts)

    return pl.pallas_call(
        body, name=name,
        grid_spec=pltpu.PrefetchScalarGridSpec(
            num_scalar_prefetch=1, grid=(nrows // tr,), in_specs=_part_specs(parts, tr, row0),
            out_specs=pl.BlockSpec((tr, W), lambda i, idx: (i, 0))),
        out_shape=jax.ShapeDtypeStruct((nrows, W), F32),
        compiler_params=_cp(("parallel",)))(idx, *[a for a, _ in parts])


def _adamw(parts, idx, w, m, v, tr, *, name):
    R, W = w.shape
    assert R % tr == 0
    np_ = len(parts)

    def body(idx_ref, *refs):
        w_ref, m_ref, v_ref, g_ref, d_ref, nm_ref, nv_ref = refs[np_:]
        g = _part_total(refs[:np_], parts)
        mm = ADAM_B1 * m_ref[...] + (1.0 - ADAM_B1) * g
        vv = ADAM_B2 * v_ref[...] + (1.0 - ADAM_B2) * (g * g)
        m_hat = mm / (1.0 - ADAM_B1 ** ADAM_STEP)
        v_hat = vv / (1.0 - ADAM_B2 ** ADAM_STEP)
        g_ref[...] = g
        d_ref[...] = -ADAM_LR * (m_hat / (jnp.sqrt(v_hat) + ADAM_EPS) + ADAM_WD * w_ref[...])
        nm_ref[...] = mm
        nv_ref[...] = vv

    blk = pl.BlockSpec((tr, W), lambda i, idx: (i, 0))
    return pl.pallas_call(
        body, name=name,
        grid_spec=pltpu.PrefetchScalarGridSpec(
            num_scalar_prefetch=1, grid=(R // tr,), in_specs=_part_specs(parts, tr, 0) + [blk, blk, blk],
            out_specs=[blk] * 4),
        out_shape=[jax.ShapeDtypeStruct((R, W), F32)] * 4,
        compiler_params=_cp(("parallel",)))(idx, *[a for a, _ in parts], w, m, v)


def _pack_rest(w_kv, wa, wb, wm, w_out):
    return jnp.concatenate([w_kv[0], w_out[0]] + [t[0].reshape(-1, D_MODEL) for t in (wa, wb, wm)], axis=0)


def _unpack_rest(t):
    br = lambda i: t[RO_BR + 64 * i:RO_BR + 64 * (i + 1)].reshape(1, A_WIDTH, D_MODEL // N_DEV)
    return t[None, RO_KV:RO_OUT], br(0), br(1), br(2), t[None, RO_OUT:RO_BR]


def _orig_rows(gathered, a, b):
    res = []
    while a < b:
        dev, r = divmod(a, CS)
        n = min(b - a, CS - r)
        res.append(gathered[dev, RO_IN + r:RO_IN + r + n])
        a += n
    return res


def _full_weights(gathered):
    wt = {}
    for name, ranges in SEGS.items():
        rows = [p for a, b in ranges for p in _orig_rows(gathered, a, b)]
        if SEG_PAD[name]:
            rows.append(jnp.zeros((SEG_PAD[name], D_MODEL), gathered.dtype))
        wt[name] = jnp.concatenate(rows, axis=0)
    w_kv = gathered[:, RO_KV:RO_OUT].reshape(D_MODEL, D_MODEL)
    w_out = gathered[:, RO_OUT:RO_BR].reshape(D_MODEL, D_MODEL)
    wbs = [gathered[:, RO_BR + 64 * i:RO_BR + 64 * (i + 1)].reshape(N_DEV, A_WIDTH, D_MODEL // N_DEV)
           .transpose(1, 0, 2).reshape(A_WIDTH, D_MODEL) for i in range(3)]
    return wt, w_kv, wbs, w_out


def _orig_pieces(dwt, lo, hi):
    pieces = []
    for name, ranges in SEGS.items():
        o = 0
        for a, b in ranges:
            s, e = max(a, lo), min(b, hi)
            if s < e:
                pieces.append((s, dwt[name][o + s - a:o + e - a]))
            o += b - a
    pieces.sort(key=lambda p: p[0])
    return [p[1] for p in pieces]


def _orig_order(dwt):
    return jnp.concatenate(_orig_pieces(dwt, 0, IN_COLS), axis=0)


def _pack_grads(dwt, dw_kv, dwbs, dw_out):
    kv, out = dw_kv.reshape(N_DEV, -1, D_MODEL), dw_out.reshape(N_DEV, -1, D_MODEL)
    br = [t.reshape(A_WIDTH, N_DEV, D_MODEL // N_DEV).transpose(1, 0, 2).reshape(N_DEV, -1, D_MODEL) for t in dwbs]
    pad = jnp.zeros((IN_ROWS - CS, D_MODEL), F32)
    rows = []
    for j in range(N_DEV):
        rows += [kv[j], out[j]] + [b[j] for b in br] + _orig_pieces(dwt, j * CS, (j + 1) * CS) + [pad]
    return jnp.concatenate(rows, axis=0).reshape(N_DEV, ROWS, D_MODEL)


def kernel(x, mem, positions, norm_pre_g, norm_post_g, norm_mem_g, w_in, b_forget, b_merge, w_mem_kv, w_branch_a, w_branch_b, w_branch_m, w_out, loss_target, m_norm_pre_g, m_norm_post_g, m_norm_mem_g, m_w_in, m_b_forget, m_b_merge, m_w_mem_kv, m_w_branch_a, m_w_branch_b, m_w_branch_m, m_w_out, v_norm_pre_g, v_norm_post_g, v_norm_mem_g, v_w_in, v_b_forget, v_b_merge, v_w_mem_kv, v_w_branch_a, v_w_branch_b, v_w_branch_m, v_w_out):
    w_rest = _pack_rest(w_mem_kv, w_branch_a, w_branch_b, w_branch_m, w_out)
    shard = jnp.concatenate([w_rest.astype(BF16), w_in[0].T.astype(BF16),
                             jnp.zeros((IN_ROWS - CS, D_MODEL), BF16)], axis=0)
    hs, (gathered,) = _rms_fwd(x[0], norm_pre_g, name="rms_pre_gather", dilations=DIL, comm=_gather_comm(shard))
    wt, w_kv, wbs, w_o = _full_weights(gathered)

    bf_pad = jnp.pad(b_forget, ((0, 0), (0, FB_PAD - B_HEADS)))
    r = _local_step(x[0], mem[0], positions[0], loss_target[0], norm_pre_g, norm_post_g, norm_mem_g,
                    wt, bf_pad, b_merge, w_kv, wbs, w_o, pack=_pack_grads, hs=hs)

    gsmall = jnp.concatenate([r["dg_pre"], r["dg_post"], r["dg_mem"], r["db_merge"],
                              r["db_forget"][:, :LANES], r["loss"]], axis=1)
    rsmall = _gather_small(gsmall, name="gather_small")
    parts, own_idx = r["parts"], r["own_idx"]

    m_rest = _pack_rest(m_w_mem_kv, m_w_branch_a, m_w_branch_b, m_w_branch_m, m_w_out)
    v_rest = _pack_rest(v_w_mem_kv, v_w_branch_a, v_w_branch_b, v_w_branch_m, v_w_out)
    outs_rest = [_unpack_rest(t) for t in _adamw(parts, own_idx, w_rest, m_rest, v_rest, 64, name="adamw_rest")]
    g_in = _sum_parts(parts, own_idx, RO_IN, IN_ROWS, 16, name="sum_w_in")[:CS].T
    outs_in = _adamw([(g_in[None], 1)], own_idx, w_in[0], m_w_in[0], v_w_in[0], 128, name="adamw_w_in")

    def small_vec(a, b, c, d, e):
        z = jnp.zeros((1, LANES - B_HEADS), F32)
        return jnp.concatenate([a, b, c, d, e, z, jnp.zeros((1, LANES), F32)], axis=1)

    outs_small = _adamw([(rsmall, N_DEV)], own_idx, small_vec(norm_pre_g, norm_post_g, norm_mem_g, b_merge, b_forget),
                        small_vec(m_norm_pre_g, m_norm_post_g, m_norm_mem_g, m_b_merge, m_b_forget),
                        small_vec(v_norm_pre_g, v_norm_post_g, v_norm_mem_g, v_b_merge, v_b_forget),
                        1, name="adamw_small")

    def small_parts(t):
        return [t[:, O_GPRE:O_GPRE + D_MODEL], t[:, O_GPOST:O_GPOST + D_MODEL], t[:, O_GMEM:O_GMEM + D_MODEL],
                t[:, O_BF:O_BF + B_HEADS], t[:, O_BM:O_BM + 3 * D_MODEL]]

    loss = outs_small[0][0, O_LOSS]
    result = [loss, r["grad_x"][None]]
    for rest, w_i, small in zip(outs_rest, outs_in, outs_small):
        gp, gq, gm, bf, bm = small_parts(small)
        w_k, w_a, w_b, w_m, w_ot = rest
        result += [gp, gq, gm, w_i[None], bf, bm, w_k, w_a, w_b, w_m, w_ot]
    return tuple(result)
```

```python
import jax
import jax.numpy as jnp
from jax import lax
from jax.experimental import pallas as pl
from jax.experimental.pallas import tpu as pltpu

F32 = jnp.float32
BF16 = jnp.bfloat16

N_DEV = 8
D_MODEL = 1024
N_MEM = 256
EPS = 1e-6
NEG = -1e30
ROPE_THETA = 500000.0
DIL = (1, 4, 16)
A_HEADS = 4
HEAD = 128
A_WIDTH = 512
B_HEADS = 8
B_HEAD = 64
M_HEADS = 4
ROT = 32
IN_COLS = 11272
FB_PAD = 256

SEGS = {
    "A0": ((0, 512), (1536, 2048), (3072, 3584)),
    "A1": ((512, 1024), (2048, 2560), (3584, 4096)),
    "A2": ((1024, 1536), (2560, 3072), (4096, 4608)),
    "B": ((5120, 6656),),
    "R": ((4608, 5120), (6664, 7176), (7176, 7688), (7688, 8200), (8200, 11272), (6656, 6664)),
}
SEG_PAD = {"A0": 0, "A1": 0, "A2": 0, "B": 0, "R": FB_PAD - B_HEADS}
R_ZA, R_ZB, R_QM, R_ZM, R_GL, R_FB = 0, 512, 1024, 1536, 2048, 5120
NR = R_FB + FB_PAD

ADAM_LR, ADAM_B1, ADAM_B2, ADAM_EPS, ADAM_WD, ADAM_STEP = 0.001, 0.9, 0.999, 1e-08, 0.01, 10

LANES = 128
VMEM_LIMIT = 56 * 1024 * 1024

CS = IN_COLS // N_DEV
RO_KV, RO_OUT, RO_BR, RO_IN = 0, 128, 256, 448
IN_ROWS = 1424
ROWS = RO_IN + IN_ROWS
O_GPRE, O_GPOST, O_GMEM, O_BM, O_BF, O_LOSS = 0, 1024, 2048, 3072, 6144, 6272
P_SMALL = 6400


def _cp(sem=None):
    return pltpu.CompilerParams(dimension_semantics=sem, vmem_limit_bytes=VMEM_LIMIT)


def _dot(a, b):
    return jnp.dot(a, b, preferred_element_type=F32)


def _dot_nt(a, b):
    return lax.dot_general(a, b, (((1,), (1,)), ((), ())), preferred_element_type=F32)


def _sigmoid(z):
    return 1.0 / (1.0 + jnp.exp(-z))


def _mm(a, b, *, name, at=False, bt=False, out_dtype=F32, tm=1024, tn=1024, tk=None, comm=None):
    assert not (at and bt)
    K, M = a.shape if at else a.shape[::-1]
    N = b.shape[0] if bt else b.shape[1]
    tm, tn = min(tm, M), min(tn, N)
    tk = K if tk is None else min(tk, K)
    assert M % tm == 0 and N % tn == 0 and K % tk == 0
    nk = K // tk
    grid = (M // tm, N // tn, nk)
    n_in = len(comm["inputs"]) if comm else 0
    n_out = len(comm["out_shape"]) if comm else 0

    def body(a_ref, b_ref, *rest):
        c_in, o_ref, c_out = rest[:n_in], rest[n_in], rest[n_in + 1:n_in + 1 + n_out]
        acc_ref, sems = rest[n_in + 1 + n_out], rest[n_in + 2 + n_out:]
        if comm:
            step = (pl.program_id(0) * grid[1] + pl.program_id(1)) * grid[2] + pl.program_id(2)

            @pl.when(step == 0)
            def _():
                comm["start"](*c_in, *c_out, *sems)

        av = a_ref[...].astype(BF16)
        bv = b_ref[...].astype(BF16)
        if at:
            p = lax.dot_general(av, bv, (((0,), (0,)), ((), ())), preferred_element_type=F32)
        else:
            p = _dot_nt(av, bv) if bt else _dot(av, bv)
        if nk == 1:
            o_ref[...] = p.astype(out_dtype)
        else:
            k = pl.program_id(2)

            @pl.when(k == 0)
            def _():
                acc_ref[...] = p

            @pl.when(k > 0)
            def _():
                acc_ref[...] += p

            @pl.when(k == nk - 1)
            def _():
                o_ref[...] = acc_ref[...].astype(out_dtype)

        if comm:
            @pl.when(step == grid[0] * grid[1] * grid[2] - 1)
            def _():
                comm["wait"](*c_in, *c_out, *sems)

    b_spec = (pl.BlockSpec((tn, tk), lambda i, j, k: (j, k)) if bt
              else pl.BlockSpec((tk, tn), lambda i, j, k: (k, j)))
    a_spec = (pl.BlockSpec((tk, tm), lambda i, j, k: (k, i)) if at
              else pl.BlockSpec((tm, tk), lambda i, j, k: (i, k)))
    out_spec = pl.BlockSpec((tm, tn), lambda i, j, k: (i, j))
    out_shape = jax.ShapeDtypeStruct((M, N), out_dtype)
    acc = pltpu.VMEM((tm, tn) if nk > 1 else (8, LANES), F32)
    if not comm:
        return pl.pallas_call(
            body, name=name, grid=grid, in_specs=[a_spec, b_spec], out_specs=out_spec, out_shape=out_shape,
            scratch_shapes=[acc], compiler_params=_cp(("parallel", "parallel", "arbitrary")))(a, b)
    return pl.pallas_call(
        body, name=name, grid=grid, in_specs=[a_spec, b_spec] + [ANY] * n_in,
        out_specs=[out_spec] + [ANY] * n_out, out_shape=[out_shape] + comm["out_shape"],
        scratch_shapes=[acc] + comm["sems"],
        compiler_params=_cp(("arbitrary", "arbitrary", "arbitrary")))(a, b, *comm["inputs"])


def _mm_sum(pairs, *, name, tm=1024, tk=768, comm=None):
    M, N = pairs[0][0].shape[0], pairs[0][1].shape[1]
    tm = min(tm, M)
    steps = [a.shape[1] // tk for a, _ in pairs]
    assert M % tm == 0 and all(a.shape[1] % tk == 0 for a, _ in pairs)
    first = [sum(steps[:p]) for p in range(len(pairs))]
    total = sum(steps)
    grid = (M // tm, total)
    n_in = len(comm["inputs"]) if comm else 0
    n_out = len(comm["out_shape"]) if comm else 0
    npair = len(pairs)

    def body(*refs):
        ab, rest = refs[:2 * npair], refs[2 * npair:]
        c_in, o_ref, c_out = rest[:n_in], rest[n_in], rest[n_in + 1:n_in + 1 + n_out]
        acc_ref, sems = rest[n_in + 1 + n_out], rest[n_in + 2 + n_out:]
        k = pl.program_id(1)
        if comm:
            step = pl.program_id(0) * total + k

            @pl.when(step == 0)
            def _():
                comm["start"](*c_in, *c_out, *sems)

        @pl.when(k == 0)
        def _():
            acc_ref[...] = jnp.zeros((tm, N), F32)

        for p in range(npair):
            @pl.when(jnp.logical_and(k >= first[p], k < first[p] + steps[p]))
            def _(p=p):
                acc_ref[...] += _dot(ab[2 * p][...], ab[2 * p + 1][...])

        @pl.when(k == total - 1)
        def _():
            o_ref[...] = acc_ref[...]

        if comm:
            @pl.when(step == grid[0] * total - 1)
            def _():
                comm["wait"](*c_in, *c_out, *sems)

    def local(p):
        return lambda k: jnp.clip(k - first[p], 0, steps[p] - 1)

    in_specs = []
    for p in range(npair):
        in_specs += [pl.BlockSpec((tm, tk), lambda i, k, f=local(p): (i, f(k))),
                     pl.BlockSpec((tk, N), lambda i, k, f=local(p): (f(k), 0))]
    out_spec = pl.BlockSpec((tm, N), lambda i, k: (i, 0))
    out_shape = jax.ShapeDtypeStruct((M, N), F32)
    args = [t for pair in pairs for t in pair]
    if not comm:
        return pl.pallas_call(
            body, name=name, grid=grid, in_specs=in_specs, out_specs=out_spec, out_shape=out_shape,
            scratch_shapes=[pltpu.VMEM((tm, N), F32)], compiler_params=_cp(("parallel", "arbitrary")))(*args)
    return pl.pallas_call(
        body, name=name, grid=grid, in_specs=in_specs + [ANY] * n_in,
        out_specs=[out_spec] + [ANY] * n_out, out_shape=[out_shape] + comm["out_shape"],
        scratch_shapes=[pltpu.VMEM((tm, N), F32)] + comm["sems"],
        compiler_params=_cp(("arbitrary", "arbitrary")))(*args, *comm["inputs"])


def _class_spec(S, d, tm, width):
    return pl.BlockSpec((d, tm // d, width), lambda i: (0, i, 0))


def _rms_fwd(x, g, *, name, dilations=(), comm=None):
    S, D = x.shape
    tm = min(512, S)
    ds = [d for d in dilations if d > 1]
    nsteps = S // tm
    n_in = len(comm["inputs"]) if comm else 0
    n_out = len(comm["out_shape"]) if comm else 0
    n_tmp = D // LANES if ds else 0

    def body(x_ref, g_ref, *rest):
        c_in, o_ref, rest = rest[:n_in], rest[n_in], rest[n_in + 1:]
        cls, c_out, rest = rest[:len(ds)], rest[len(ds):len(ds) + n_out], rest[len(ds) + n_out:]
        tmps, sems = rest[:n_tmp], rest[n_tmp:]
        if comm:
            @pl.when(pl.program_id(0) == 0)
            def _():
                comm["start"](*c_in, *c_out, *sems)

        xv = x_ref[...]
        r = lax.rsqrt(jnp.mean(xv * xv, axis=-1, keepdims=True) + EPS)
        hv = xv * r * g_ref[...]
        o_ref[...] = hv.astype(BF16)
        if ds:
            for c, tmp in enumerate(tmps):
                tmp[...] = hv[:, c * LANES:(c + 1) * LANES]
            for c_ref, d in zip(cls, ds):
                for k in range(d):
                    c_ref[k] = jnp.concatenate([tmp[pl.ds(k, tm // d, stride=d), :] for tmp in tmps],
                                               axis=1).astype(BF16)
        if comm:
            @pl.when(pl.program_id(0) == nsteps - 1)
            def _():
                comm["wait"](*c_in, *c_out, *sems)

    row = pl.BlockSpec((tm, D), lambda i: (i, 0))
    outs = pl.pallas_call(
        body, name=name, grid=(nsteps,),
        in_specs=[row, pl.BlockSpec((1, D), lambda i: (0, 0))] + [ANY] * n_in,
        out_specs=[row] + [_class_spec(S, d, tm, D) for d in ds] + [ANY] * n_out,
        out_shape=[jax.ShapeDtypeStruct((S, D), BF16)] + [jax.ShapeDtypeStruct((d, S // d, D), BF16) for d in ds]
        + (comm["out_shape"] if comm else []),
        scratch_shapes=[pltpu.VMEM((tm, LANES), F32)] * n_tmp + (comm["sems"] if comm else []),
        compiler_params=_cp(("arbitrary",) if comm else ("parallel",)),
    )(x, g, *(comm["inputs"] if comm else []))
    rows = [outs[0]] + [o.reshape(S, D) for o in outs[1:1 + len(ds)]]
    if comm:
        return rows, list(outs[1 + len(ds):])
    return rows if ds else rows[0]


def _rms_bwd(x, g, dh, dy, *, name, dh_classes=()):
    S, D = x.shape
    tm = min(512, S)
    want_dx = dy is not None
    nc = len(dh_classes)

    def body(*refs):
        c_refs, refs = refs[:nc], refs[nc:]
        if want_dx:
            x_ref, g_ref, dh_ref, dy_ref, dx_ref, dg_ref = refs[:6]
        else:
            x_ref, g_ref, dh_ref, dg_ref = refs[:4]
        i = pl.program_id(0)
        xv = x_ref[...]
        r = lax.rsqrt(jnp.mean(xv * xv, axis=-1, keepdims=True) + EPS)
        xh = xv * r
        if nc:
            tmps = refs[-(D // LANES):]
            cols = [slice(c * LANES, (c + 1) * LANES) for c in range(D // LANES)]
            for tmp, cs in zip(tmps, cols):
                tmp[...] = dh_ref[:, cs]
            for c_ref, (_, d) in zip(c_refs, dh_classes):
                for k in range(d):
                    for tmp, cs in zip(tmps, cols):
                        tmp[pl.ds(k, tm // d, stride=d), :] += c_ref[k, :, cs]
            dhv = jnp.concatenate([tmp[...] for tmp in tmps], axis=1)
        else:
            dhv = dh_ref[...]
        part = jnp.sum(dhv * xh, axis=0, keepdims=True)

        @pl.when(i == 0)
        def _():
            dg_ref[...] = part

        @pl.when(i > 0)
        def _():
            dg_ref[...] += part

        if want_dx:
            dxh = dhv * g_ref[...]
            dx_ref[...] = dy_ref[...] + r * (dxh - xh * jnp.mean(dxh * xh, axis=-1, keepdims=True))

    row = pl.BlockSpec((tm, D), lambda i: (i, 0))
    vec = pl.BlockSpec((1, D), lambda i: (0, 0))
    c_specs = [_class_spec(S, d, tm, D) for _, d in dh_classes]
    c_args = [a.reshape(d, S // d, D) for a, d in dh_classes]
    scratch = [pltpu.VMEM((tm, LANES), F32)] * (D // LANES) if nc else []
    if want_dx:
        return pl.pallas_call(
            body, name=name, grid=(S // tm,), in_specs=c_specs + [row, vec, row, row], out_specs=[row, vec],
            out_shape=[jax.ShapeDtypeStruct((S, D), F32), jax.ShapeDtypeStruct((1, D), F32)],
            scratch_shapes=scratch, compiler_params=_cp(("arbitrary",)))(*c_args, x, g, dh, dy)
    return pl.pallas_call(
        body, name=name, grid=(S // tm,), in_specs=c_specs + [row, vec, row], out_specs=vec,
        out_shape=jax.ShapeDtypeStruct((1, D), F32),
        scratch_shapes=scratch, compiler_params=_cp(("arbitrary",)))(*c_args, x, g, dh)


def _post(x, out, tgt, g, *, name):
    S, D = x.shape
    tm = min(512, S)

    def body(x_ref, o_ref, t_ref, g_ref, dy_ref, do_ref, dg_ref, loss_ref):
        i = pl.program_id(0)
        ov = o_ref[...]
        r = lax.rsqrt(jnp.mean(ov * ov, axis=-1, keepdims=True) + EPS)
        n = ov * r
        gv = g_ref[...]
        e = (x_ref[...] + n * gv) - t_ref[...]
        lpart = 0.5 * jnp.sum(jnp.mean(e * e, axis=-1, keepdims=True), axis=0, keepdims=True)
        dy = e * (1.0 / D)
        dy_ref[...] = dy
        dn = dy * gv
        do_ref[...] = (r * (dn - n * jnp.mean(dn * n, axis=-1, keepdims=True))).astype(BF16)
        gpart = jnp.sum(dy * n, axis=0, keepdims=True)
        lrow = jnp.broadcast_to(lpart, (1, LANES))

        @pl.when(i == 0)
        def _():
            dg_ref[...] = gpart
            loss_ref[...] = lrow

        @pl.when(i > 0)
        def _():
            dg_ref[...] += gpart
            loss_ref[...] += lrow

    row = pl.BlockSpec((tm, D), lambda i: (i, 0))
    vec = pl.BlockSpec((1, D), lambda i: (0, 0))
    return pl.pallas_call(
        body, name=name, grid=(S // tm,), in_specs=[row, row, row, vec],
        out_specs=[row, row, vec, pl.BlockSpec((1, LANES), lambda i: (0, 0))],
        out_shape=[jax.ShapeDtypeStruct((S, D), F32), jax.ShapeDtypeStruct((S, D), BF16),
                   jax.ShapeDtypeStruct((1, D), F32), jax.ShapeDtypeStruct((1, LANES), F32)],
        compiler_params=_cp(("arbitrary",)))(x, out, tgt, g)


def _to_classes(t, d):
    if d == 1:
        return t
    S, C = t.shape
    return t.reshape(S // d, d, C).transpose(1, 0, 2).reshape(S, C)


def _rope(x, c, s1, s2):
    return x * c + pltpu.roll(x, LANES - ROT // 2, 1) * s1 + pltpu.roll(x, ROT // 2, 1) * s2


def _unrope(d, c, s1, s2):
    return d * c + pltpu.roll(d * s1, ROT // 2, 1) + pltpu.roll(d * s2, LANES - ROT // 2, 1)


def _a_band(qb):
    r = lax.broadcasted_iota(jnp.int32, (qb, qb + HEAD), 0)
    c = lax.broadcasted_iota(jnp.int32, (qb, qb + HEAD), 1)
    return jnp.logical_and(c >= r, c <= r + HEAD)


def _a_first_ok(qb, n):
    c = lax.broadcasted_iota(jnp.int32, (qb, qb + HEAD), 1)
    return jnp.logical_or(c >= HEAD, n > 0)


def _a_last_ok(qb, has_next):
    c = lax.broadcasted_iota(jnp.int32, (qb, qb + HEAD), 1)
    return jnp.logical_or(c < qb, has_next)


A_SCALE = HEAD ** -0.5


def _a_geometry(S, g):
    d = DIL[g]
    L = S // d
    TQ = min(512, L)
    return d, L, TQ, TQ // HEAD, L // TQ, L // HEAD


def _proj_rope(h, w, tabs, *, name):
    S, D = h.shape
    tm = min(512, S)

    def body(h_ref, w_ref, c_ref, s1_ref, s2_ref, o_ref):
        tc = (c_ref[...], s1_ref[...], s2_ref[...])
        u = _dot_nt(h_ref[...], w_ref[...])
        for j in range(3 * A_HEADS):
            sl = slice(j * HEAD, (j + 1) * HEAD)
            o_ref[:, sl] = (_rope(u[:, sl], *tc) if j < 2 * A_HEADS else u[:, sl]).astype(BF16)

    tab = pl.BlockSpec((tm, LANES), lambda i: (i, 0))
    return pl.pallas_call(
        body, name=name, grid=(S // tm,),
        in_specs=[pl.BlockSpec((tm, D), lambda i: (i, 0)), pl.BlockSpec((3 * A_WIDTH, D), lambda i: (0, 0)),
                  tab, tab, tab],
        out_specs=pl.BlockSpec((tm, 3 * A_WIDTH), lambda i: (i, 0)),
        out_shape=jax.ShapeDtypeStruct((S, 3 * A_WIDTH), BF16),
        compiler_params=_cp(("parallel",)))(h, w, *tabs)


def _attn_a_fwd(qkv, g, *, name):
    S = qkv.shape[0]
    d, L, TQ, nsub, nb, nblk = _a_geometry(S, g)

    def body(q_ref, kc_ref, kp_ref, vc_ref, vp_ref, o_ref, l_ref):
        n = pl.program_id(1)
        QB = min(2 * HEAD, TQ)
        band = _a_band(QB)
        first = jnp.logical_and(band, _a_first_ok(QB, n))
        for h in range(A_HEADS):
            hs = slice(h * HEAD, (h + 1) * HEAD)
            for hh in range(TQ // QB):
                sl = slice(hh * QB, (hh + 1) * QB)
                pv = slice(hh * QB - HEAD, hh * QB)
                kcat = jnp.concatenate([kp_ref[:, hs] if hh == 0 else kc_ref[pv, hs], kc_ref[sl, hs]], axis=0)
                vcat = jnp.concatenate([vp_ref[:, hs] if hh == 0 else vc_ref[pv, hs], vc_ref[sl, hs]], axis=0)
                s = jnp.where(first if hh == 0 else band, _dot_nt(q_ref[sl, hs], kcat) * A_SCALE, NEG)
                m = jnp.max(s, axis=-1, keepdims=True)
                p = jnp.exp(s - m)
                den = jnp.sum(p, axis=-1, keepdims=True)
                o_ref[sl, hs] = _dot(p.astype(BF16), vcat) / den
                l_ref[sl, hs] = jnp.broadcast_to(m + jnp.log(den), (QB, HEAD))

    rcur = lambda r, n: r * nb + n
    rprv = lambda r, n: r * nblk + jnp.maximum(n * nsub - 1, 0)
    cur = lambda off: pl.BlockSpec((TQ, A_WIDTH), lambda r, n: (rcur(r, n), off))
    prv = lambda off: pl.BlockSpec((HEAD, A_WIDTH), lambda r, n: (rprv(r, n), off))
    out = pl.BlockSpec((TQ, A_WIDTH), lambda r, n: (rcur(r, n), 0))
    return pl.pallas_call(
        body, name=name, grid=(d, nb),
        in_specs=[cur(0), cur(1), prv(1), cur(2), prv(2)],
        out_specs=[out, out],
        out_shape=[jax.ShapeDtypeStruct((S, A_WIDTH), F32)] * 2,
        compiler_params=_cp(("parallel", "parallel")),
    )(qkv, qkv, qkv, qkv, qkv)


def _attn_a_dq(qkv, tabs, g, do, lse, adj, du, *, name):
    S = qkv.shape[0]
    d, L, TQ, nsub, nb, nblk = _a_geometry(S, g)

    def body(q_ref, kc_ref, kp_ref, vc_ref, vp_ref, do_ref, l_ref, adj_ref, c_ref, s1_ref, s2_ref, du_ref, dq_ref):
        n = pl.program_id(1)
        QB = min(2 * HEAD, TQ)
        band = _a_band(QB)
        first = jnp.logical_and(band, _a_first_ok(QB, n))
        for h in range(A_HEADS):
            hs = slice(h * HEAD, (h + 1) * HEAD)
            for hh in range(TQ // QB):
                sl = slice(hh * QB, (hh + 1) * QB)
                pv = slice(hh * QB - HEAD, hh * QB)
                kcat = jnp.concatenate([kp_ref[:, hs] if hh == 0 else kc_ref[pv, hs], kc_ref[sl, hs]], axis=0)
                vcat = jnp.concatenate([vp_ref[:, hs] if hh == 0 else vc_ref[pv, hs], vc_ref[sl, hs]], axis=0)
                s = jnp.where(first if hh == 0 else band, _dot_nt(q_ref[sl, hs], kcat) * A_SCALE, NEG)
                p = jnp.exp(s - l_ref[sl, hs][:, :1])
                ds = p * (_dot_nt(do_ref[sl, hs], vcat) + adj_ref[sl, hs][:, :1])
                dq = _dot(ds.astype(BF16), kcat) * A_SCALE
                dq_ref[sl, hs] = _unrope(dq, c_ref[sl, :], s1_ref[sl, :], s2_ref[sl, :]).astype(BF16)

    rcur = lambda r, n: r * nb + n
    rprv = lambda r, n: r * nblk + jnp.maximum(n * nsub - 1, 0)
    cur = lambda off: pl.BlockSpec((TQ, A_WIDTH), lambda r, n: (rcur(r, n), off))
    prv = lambda off: pl.BlockSpec((HEAD, A_WIDTH), lambda r, n: (rprv(r, n), off))
    tcur = pl.BlockSpec((TQ, LANES), lambda r, n: (rcur(r, n), 0))
    blk = cur(0)
    return pl.pallas_call(
        body, name=name, grid=(d, nb),
        in_specs=[cur(0), cur(1), prv(1), cur(2), prv(2), blk, blk, blk, tcur, tcur, tcur, ANY],
        out_specs=blk,
        out_shape=jax.ShapeDtypeStruct((S, 3 * A_WIDTH), BF16),
        input_output_aliases={11: 0},
        compiler_params=_cp(("parallel", "parallel")),
    )(qkv, qkv, qkv, qkv, qkv, do, lse, adj, *tabs, du)


def _attn_a_dkv(qkv, tabs, g, do, lse, adj, *, name):
    S = qkv.shape[0]
    d, L, TQ, nsub, nb, nblk = _a_geometry(S, g)

    def body(qc_ref, qn_ref, kc_ref, vc_ref, doc_ref, don_ref, lc_ref, ln_ref, ac_ref, an_ref,
             c_ref, s1_ref, s2_ref, du_ref):
        n = pl.program_id(1)
        QB = min(2 * HEAD, TQ)
        nh = TQ // QB
        band = _a_band(QB)
        end = jnp.logical_and(band, _a_last_ok(QB, n < nb - 1))
        for h in range(A_HEADS):
            hs = slice(h * HEAD, (h + 1) * HEAD)
            for kh in range(nh):
                sl = slice(kh * QB, (kh + 1) * QB)
                nx = slice((kh + 1) * QB, (kh + 1) * QB + HEAD)
                last = kh == nh - 1
                cat = lambda cur, nxt: jnp.concatenate([cur[sl, hs], nxt[:, hs] if last else cur[nx, hs]], axis=0)
                qcat = cat(qc_ref, qn_ref)
                docat = cat(doc_ref, don_ref)
                lt = cat(lc_ref, ln_ref).T[:1, :]
                at = cat(ac_ref, an_ref).T[:1, :]
                st = jnp.where(end if last else band, _dot_nt(kc_ref[sl, hs], qcat) * A_SCALE, NEG)
                pt = jnp.exp(st - lt)
                dv_cols = slice(2 * A_WIDTH + h * HEAD, 2 * A_WIDTH + (h + 1) * HEAD)
                dk_cols = slice(A_WIDTH + h * HEAD, A_WIDTH + (h + 1) * HEAD)
                du_ref[sl, dv_cols] = _dot(pt.astype(BF16), docat).astype(BF16)
                dst = pt * (_dot_nt(vc_ref[sl, hs], docat) + at)
                dk = _dot(dst.astype(BF16), qcat) * A_SCALE
                du_ref[sl, dk_cols] = _unrope(dk, c_ref[sl, :], s1_ref[sl, :], s2_ref[sl, :]).astype(BF16)

    rcur = lambda r, n: r * nb + n
    rnxt = lambda r, n: r * nblk + jnp.minimum((n + 1) * nsub, nblk - 1)
    cur = lambda off: pl.BlockSpec((TQ, A_WIDTH), lambda r, n: (rcur(r, n), off))
    nxu = lambda off: pl.BlockSpec((HEAD, A_WIDTH), lambda r, n: (rnxt(r, n), off))
    tcur = pl.BlockSpec((TQ, LANES), lambda r, n: (rcur(r, n), 0))
    blk, bnx = cur(0), nxu(0)
    return pl.pallas_call(
        body, name=name, grid=(d, nb),
        in_specs=[cur(0), nxu(0), cur(1), cur(2), blk, bnx, blk, bnx, blk, bnx, tcur, tcur, tcur],
        out_specs=pl.BlockSpec((TQ, 3 * A_WIDTH), lambda r, n: (rcur(r, n), 0)),
        out_shape=jax.ShapeDtypeStruct((S, 3 * A_WIDTH), BF16),
        compiler_params=_cp(("parallel", "parallel")),
    )(qkv, qkv, qkv, qkv, do, do, lse, lse, adj, adj, *tabs)


def _silu_parts(z):
    sg = _sigmoid(z)
    return z * sg, sg * (1.0 + z * (1.0 - sg))


def _classes_to_tokens(c_ref, d, tm, tmps):
    if d == 1:
        return c_ref[...].astype(F32)
    for k in range(d):
        for c, tmp in enumerate(tmps):
            tmp[pl.ds(k, tm // d, stride=d), :] = c_ref[k, :, c * LANES:(c + 1) * LANES].astype(F32)
    return jnp.concatenate([tmp[...] for tmp in tmps], axis=1)


def _tokens_to_classes(val, c_ref, d, tm, tmps):
    if d == 1:
        c_ref[...] = val.astype(c_ref.dtype)
        return
    for c, tmp in enumerate(tmps):
        tmp[...] = val[:, c * LANES:(c + 1) * LANES]
    for k in range(d):
        c_ref[k] = jnp.concatenate([tmp[pl.ds(k, tm // d, stride=d), :] for tmp in tmps], axis=1).astype(c_ref.dtype)


def _group_spec(S, d, tm):
    if d == 1:
        return pl.BlockSpec((tm, A_WIDTH), lambda i: (i, 0))
    return _class_spec(S, d, tm, A_WIDTH)


def _group_view(t, d):
    return t if d == 1 else t.reshape(d, t.shape[0] // d, t.shape[1])


def _merge_a_fwd(os_, ls_, ur, *, name):
    S = ur.shape[0]
    tm = min(512, S)

    def body(o0, o1, o2, l0, l1, l2, z_ref, y_ref, *tmps):
        ls = [_classes_to_tokens(r, d, tm, tmps) for r, d in zip((l0, l1, l2), DIL)]
        ov = [_classes_to_tokens(r, d, tm, tmps) for r, d in zip((o0, o1, o2), DIL)]
        mx = jnp.maximum(jnp.maximum(ls[0], ls[1]), ls[2])
        es = [jnp.exp(l - mx) for l in ls]
        den = es[0] + es[1] + es[2]
        y = (es[0] / den) * ov[0] + (es[1] / den) * ov[1] + (es[2] / den) * ov[2]
        y_ref[...] = (y * _silu_parts(z_ref[...])[0]).astype(BF16)

    blk = pl.BlockSpec((tm, A_WIDTH), lambda i: (i, 0))
    groups = [_group_spec(S, d, tm) for d in DIL]
    return pl.pallas_call(
        body, name=name, grid=(S // tm,),
        in_specs=groups + groups + [pl.BlockSpec((tm, A_WIDTH), lambda i: (i, R_ZA // A_WIDTH))],
        out_specs=blk, out_shape=jax.ShapeDtypeStruct((S, A_WIDTH), BF16),
        scratch_shapes=[pltpu.VMEM((tm, LANES), F32)] * (A_WIDTH // LANES),
        compiler_params=_cp(("parallel",)))(*[_group_view(t, d) for t, d in zip(os_, DIL)],
                                            *[_group_view(t, d) for t, d in zip(ls_, DIL)], ur)


def _merge_a_bwd(os_, ls_, ur, dya, du_r, *, name):
    S = ur.shape[0]
    tm = min(256, S)

    def body(o0, o1, o2, l0, l1, l2, z_ref, dy_ref, du_in, d0, d1, d2, a0, a1, a2, dz_ref, *tmps):
        ls = [_classes_to_tokens(r, d, tm, tmps) for r, d in zip((l0, l1, l2), DIL)]
        ov = [_classes_to_tokens(r, d, tm, tmps) for r, d in zip((o0, o1, o2), DIL)]
        mx = jnp.maximum(jnp.maximum(ls[0], ls[1]), ls[2])
        es = [jnp.exp(l - mx) for l in ls]
        den = es[0] + es[1] + es[2]
        ws = [e / den for e in es]
        y = ws[0] * ov[0] + ws[1] * ov[1] + ws[2] * ov[2]
        sz, dsz = _silu_parts(z_ref[...])
        dyv = dy_ref[...]
        dz_ref[...] = (dyv * y * dsz).astype(BF16)
        dyp = dyv * sz
        ts = []
        for h in range(A_HEADS):
            sl = slice(h * HEAD, (h + 1) * HEAD)
            t = jnp.zeros((tm, 1), F32)
            for gi in range(3):
                t = t + ws[gi][:, sl][:, :1] * jnp.sum(dyp[:, sl] * ov[gi][:, sl], axis=-1, keepdims=True)
            ts.append(jnp.broadcast_to(t, (tm, HEAD)))
        tb = jnp.concatenate(ts, axis=1)
        for gi, (dref, aref) in enumerate(((d0, a0), (d1, a1), (d2, a2))):
            _tokens_to_classes(ws[gi] * dyp, dref, DIL[gi], tm, tmps)
            _tokens_to_classes(-ws[gi] * tb, aref, DIL[gi], tm, tmps)

    blk = pl.BlockSpec((tm, A_WIDTH), lambda i: (i, 0))
    groups = [_group_spec(S, d, tm) for d in DIL]
    shaped = lambda dt: [jax.ShapeDtypeStruct((S, A_WIDTH) if d == 1 else (d, S // d, A_WIDTH), dt) for d in DIL]
    outs = pl.pallas_call(
        body, name=name, grid=(S // tm,),
        in_specs=groups + groups + [pl.BlockSpec((tm, A_WIDTH), lambda i: (i, R_ZA // A_WIDTH)), blk, ANY],
        out_specs=groups + groups + [pl.BlockSpec((tm, A_WIDTH), lambda i: (i, R_ZA // A_WIDTH))],
        out_shape=shaped(BF16) + shaped(F32) + [jax.ShapeDtypeStruct(du_r.shape, BF16)],
        input_output_aliases={8: 6},
        scratch_shapes=[pltpu.VMEM((tm, LANES), F32)] * (A_WIDTH // LANES),
        compiler_params=_cp(("parallel",)))(*[_group_view(t, d) for t, d in zip(os_, DIL)],
                                            *[_group_view(t, d) for t, d in zip(ls_, DIL)], ur, dya, du_r)
    flat = [t.reshape(S, A_WIDTH) for t in outs[:6]]
    return flat[0:3], flat[3:6], outs[6]


def _logf(ur, bf_pad, *, name):
    S = ur.shape[0]
    tm = min(1024, S)

    def body(u_ref, b_ref, o_ref):
        z = u_ref[...] + b_ref[...]
        o_ref[...] = jnp.minimum(z, 0.0) - jnp.log(1.0 + jnp.exp(-jnp.abs(z)))

    return pl.pallas_call(
        body, name=name, grid=(S // tm,),
        in_specs=[pl.BlockSpec((tm, FB_PAD), lambda i: (i, R_FB // FB_PAD)),
                  pl.BlockSpec((1, FB_PAD), lambda i: (0, 0))],
        out_specs=pl.BlockSpec((tm, FB_PAD), lambda i: (i, 0)),
        out_shape=jax.ShapeDtypeStruct((S, FB_PAD), F32),
        compiler_params=_cp(("parallel",)))(ur, bf_pad)


def _cumsum_lanes(x, reverse, *, name):
    nt, H, _ = x.shape
    R = nt * H

    def body(x_ref, o_ref):
        v = x_ref[...].reshape(R, LANES)
        lane = lax.broadcasted_iota(jnp.int32, (R, LANES), 1)
        row = lax.broadcasted_iota(jnp.int32, (R, LANES), 0)

        def scan(t, step, idx, n, axis):
            while step < n:
                if reverse:
                    t = t + jnp.where(idx < n - step, pltpu.roll(t, n - step, axis), 0.0)
                else:
                    t = t + jnp.where(idx >= step, pltpu.roll(t, step, axis), 0.0)
                step *= 2
            return t

        v = scan(v, 1, lane, LANES, 1)
        total = jnp.broadcast_to(v[:, :1] if reverse else v[:, LANES - 1:], (R, LANES))
        carry = scan(total, H, row, R, 0) - total
        o_ref[...] = (v + carry).reshape(nt, H, LANES)

    return pl.pallas_call(
        body, name=name, out_shape=jax.ShapeDtypeStruct((nt, H, LANES), F32),
        in_specs=[pl.BlockSpec(memory_space=pltpu.VMEM)], out_specs=pl.BlockSpec(memory_space=pltpu.VMEM),
        compiler_params=_cp())(x)


B_SCALE = B_HEAD ** -0.5


def _pair_masks():
    lane = lax.broadcasted_iota(jnp.int32, (1, LANES), 1)
    row = lax.broadcasted_iota(jnp.int32, (LANES, 1), 0)
    return (lane < B_HEAD, lane >= B_HEAD), (row < B_HEAD, row >= B_HEAD)


def _causal_t(T):
    r = lax.broadcasted_iota(jnp.int32, (T, T), 0)
    c = lax.broadcasted_iota(jnp.int32, (T, T), 1)
    return r <= c


def _zero_other(x, keep):
    return jnp.where(keep, x, jnp.zeros_like(x))


def _fox_aug(ub, c, *, name):
    S = ub.shape[0]
    T = min(2048, S)

    def body(q_ref, k_ref, c_ref, qa_ref, ka_ref):
        lane = lax.broadcasted_iota(jnp.int32, (1, LANES), 1)
        q = q_ref[...] * B_SCALE
        k = k_ref[...]
        for a in range(2):
            own = (lane < B_HEAD) if a == 0 else (lane >= B_HEAD)
            o = B_HEAD if a == 0 else 0
            cv = jnp.broadcast_to(c_ref[:, a:a + 1], (T, LANES))
            hi = cv.astype(BF16)
            r1 = cv - hi.astype(F32)
            mid = r1.astype(BF16)
            lo = (r1 - mid.astype(F32)).astype(BF16)
            pieces = (hi, mid, lo)
            one = jnp.ones((T, LANES), BF16)
            qa = jnp.where(own, q, jnp.zeros_like(q))
            ka = jnp.where(own, k, jnp.zeros_like(k))
            for t in range(3):
                qa = jnp.where(lane == o + t, pieces[t], qa)
                qa = jnp.where(lane == o + 3 + t, one, qa)
                ka = jnp.where(lane == o + t, one, ka)
                ka = jnp.where(lane == o + 3 + t, -pieces[t], ka)
            qa_ref[a] = qa
            ka_ref[a] = ka

    out = pl.BlockSpec((2, T, LANES), lambda h, i: (h, i, 0))
    c_pairs = c.reshape(B_HEADS // 2, 2, S).transpose(0, 2, 1)
    return pl.pallas_call(
        body, name=name, grid=(B_HEADS // 2, S // T),
        in_specs=[pl.BlockSpec((T, LANES), lambda h, i: (i, h)), pl.BlockSpec((T, LANES), lambda h, i: (i, 4 + h)),
                  pl.BlockSpec((None, T, 2), lambda h, i: (h, i, 0))],
        out_specs=[out, out], out_shape=[jax.ShapeDtypeStruct((B_HEADS, S, LANES), BF16)] * 2,
        compiler_params=_cp(("parallel", "parallel")))(ub, ub, c_pairs)


def _fox_fwd(qaug, kaug, vt, *, name):
    S = qaug.shape[1]
    T = min(512, S)
    nq = S // T

    def body(q_ref, k_ref, vt_ref, o_ref, l_ref, m_s, l_s, acc_s, st_s):
        i = pl.program_id(1)
        _, rows = _pair_masks()
        qm = [q_ref[0], q_ref[1]]
        m_s[...] = jnp.full((2, 1, T), NEG, F32)
        l_s[...] = jnp.zeros((2, 1, T), F32)
        acc_s[...] = jnp.zeros((LANES, T), F32)

        def logits(j):
            off = pl.multiple_of(j * T, T)
            return [_dot_nt(k_ref[a, pl.ds(off, T), :], qm[a]) for a in range(2)]

        def step(j, masked, prefetch):
            nxt = logits(j + 1) if prefetch else None
            vtj = vt_ref[j]
            upd = jnp.zeros((LANES, T), F32)
            alphas = []
            for a in range(2):
                st = st_s[a]
                if masked:
                    st = jnp.where(_causal_t(T), st, NEG)
                m_old = m_s[a]
                m_new = jnp.maximum(m_old, jnp.max(st, axis=0, keepdims=True))
                alpha = jnp.exp(m_old - m_new)
                pt = jnp.exp(st - m_new)
                l_s[a] = alpha * l_s[a] + jnp.sum(pt, axis=0, keepdims=True)
                m_s[a] = m_new
                upd = upd + _dot(_zero_other(vtj, rows[a]), pt.astype(BF16))
                alphas.append(alpha)
            acc_s[...] = acc_s[...] * jnp.where(rows[0], alphas[0], alphas[1]) + upd
            if prefetch:
                st_s[0] = nxt[0]
                st_s[1] = nxt[1]

        def loop(j, carry):
            step(j, False, True)
            return carry

        first = logits(0)
        st_s[0] = first[0]
        st_s[1] = first[1]
        lax.fori_loop(0, i, loop, 0)
        step(i, True, False)
        o_ref[...] = (acc_s[...] / jnp.where(rows[0], l_s[0], l_s[1])).T
        l_ref[0] = m_s[0] + jnp.log(l_s[0])
        l_ref[1] = m_s[1] + jnp.log(l_s[1])

    stat = pl.BlockSpec((2, None, 1, T), lambda h, i: (h, i, 0, 0))
    return pl.pallas_call(
        body, name=name, grid=(B_HEADS // 2, nq),
        in_specs=[pl.BlockSpec((2, T, LANES), lambda h, i: (h, i, 0)),
                  pl.BlockSpec((2, S, LANES), lambda h, i: (h, 0, 0)),
                  pl.BlockSpec((nq, LANES, T), lambda h, i: (0, h, 0))],
        out_specs=[pl.BlockSpec((T, LANES), lambda h, i: (i, h)), stat],
        out_shape=[jax.ShapeDtypeStruct((S, A_WIDTH), F32), jax.ShapeDtypeStruct((B_HEADS, nq, 1, T), F32)],
        scratch_shapes=[pltpu.VMEM((2, 1, T), F32), pltpu.VMEM((2, 1, T), F32), pltpu.VMEM((LANES, T), F32),
                        pltpu.VMEM((2, T, T), F32)],
        compiler_params=_cp(("parallel", "parallel")),
    )(qaug, kaug, vt)


def _fox_delta(o, do, *, name):
    S = o.shape[0]
    T = min(512, S)
    nq = S // T

    per = min(4, nq)

    def body(o_ref, do_ref, d_ref):
        _, rows = _pair_masks()
        for t in range(per):
            sl = slice(t * T, (t + 1) * T)
            prod_t = (do_ref[sl, :].astype(F32) * o_ref[sl, :]).T
            d_ref[0, t] = jnp.sum(_zero_other(prod_t, rows[0]), axis=0, keepdims=True)
            d_ref[1, t] = jnp.sum(_zero_other(prod_t, rows[1]), axis=0, keepdims=True)

    tile = pl.BlockSpec((per * T, LANES), lambda h, i: (i, h))
    return pl.pallas_call(
        body, name=name, grid=(B_HEADS // 2, nq // per), in_specs=[tile, tile],
        out_specs=pl.BlockSpec((2, per, 1, T), lambda h, i: (h, i, 0, 0)),
        out_shape=jax.ShapeDtypeStruct((B_HEADS, nq, 1, T), F32),
        compiler_params=_cp(("parallel", "parallel")))(o, do)


def _fox_bwd(ub, qaug, kaug, kt, do, lse, delta, *, name):
    S = ub.shape[0]
    T = min(512, S)
    nq = S // T

    def body(k_ref, v_ref, kt_ref, q_ref, do_ref, l_ref, dl_ref,
             dk_ref, dv_ref, dck_ref, dqt_ref, dcq_ref, dk_s, dv_s, dc_s):
        j = pl.program_id(1)
        lanes, rows = _pair_masks()
        vv = v_ref[...]
        ktj = kt_ref[...]
        km = [k_ref[0], k_ref[1]]
        ktm = [_zero_other(ktj, rows[0]), _zero_other(ktj, rows[1])]
        dk_s[...] = jnp.zeros((2, T, LANES), F32)
        dv_s[...] = jnp.zeros((T, LANES), F32)
        dc_s[...] = jnp.zeros((2, T, 1), F32)

        @pl.when(j == 0)
        def _():
            dqt_ref[...] = jnp.zeros((nq, LANES, T), F32)
            dcq_ref[...] = jnp.zeros((2, nq, 1, T), F32)

        def step(i, masked):
            off = pl.multiple_of(i * T, T)
            doi = do_ref[pl.ds(off, T), :]
            upd = jnp.zeros((LANES, T), F32)
            for a in range(2):
                qi = q_ref[a, pl.ds(off, T), :]
                st = _dot_nt(km[a], qi)
                if masked:
                    st = jnp.where(_causal_t(T), st, NEG)
                pt = jnp.exp(st - l_ref[a, i])
                doa = _zero_other(doi, lanes[a])
                dv_s[...] += _dot(pt.astype(BF16), doa)
                dst = pt * (_dot_nt(vv, doa) - dl_ref[a, i])
                dsb = dst.astype(BF16)
                dk_s[a] += _dot(dsb, qi)
                upd = upd + _dot(ktm[a], dsb)
                dc_s[a] -= jnp.sum(dst, axis=-1, keepdims=True)
                dcq_ref[a, i] += jnp.sum(dst, axis=0, keepdims=True)
            dqt_ref[i] += upd

        def loop(i, carry):
            step(i, False)
            return carry

        step(j, True)
        lax.fori_loop(j + 1, nq, loop, 0)
        dk_ref[...] = jnp.where(lanes[0], dk_s[0], dk_s[1]).astype(BF16)
        dv_ref[...] = dv_s[...].astype(BF16)
        dck_ref[...] = dc_s[...]

    rowv = pl.BlockSpec((2, nq, 1, T), lambda h, j: (h, 0, 0, 0))
    tile = pl.BlockSpec((T, LANES), lambda h, j: (j, h))
    return pl.pallas_call(
        body, name=name, grid=(B_HEADS // 2, nq),
        in_specs=[pl.BlockSpec((2, T, LANES), lambda h, j: (h, j, 0)),
                  pl.BlockSpec((T, LANES), lambda h, j: (j, 8 + h)),
                  pl.BlockSpec((None, LANES, T), lambda h, j: (j, h, 0)),
                  pl.BlockSpec((2, S, LANES), lambda h, j: (h, 0, 0)),
                  pl.BlockSpec((S, LANES), lambda h, j: (0, h)),
                  rowv, rowv],
        out_specs=[tile, tile, pl.BlockSpec((2, T, 1), lambda h, j: (h, j, 0)),
                   pl.BlockSpec((nq, LANES, T), lambda h, j: (0, h, 0)), rowv],
        out_shape=[jax.ShapeDtypeStruct((S, A_WIDTH), BF16)] * 2 + [jax.ShapeDtypeStruct((B_HEADS, S, 1), F32),
                   jax.ShapeDtypeStruct((nq, A_WIDTH, T), F32), jax.ShapeDtypeStruct((B_HEADS, nq, 1, T), F32)],
        scratch_shapes=[pltpu.VMEM((2, T, LANES), F32), pltpu.VMEM((T, LANES), F32), pltpu.VMEM((2, T, 1), F32)],
        compiler_params=_cp(("parallel", "arbitrary")),
    )(kaug, ub, kt, qaug, do, lse, delta)


def _gate_fwd(o, ur, zcol, *, name):
    S = ur.shape[0]
    tm = min(1024, S)

    def body(o_ref, z_ref, y_ref):
        y_ref[...] = (o_ref[...] * _silu_parts(z_ref[...])[0]).astype(BF16)

    blk = pl.BlockSpec((tm, A_WIDTH), lambda i: (i, 0))
    return pl.pallas_call(
        body, name=name, grid=(S // tm,),
        in_specs=[blk, pl.BlockSpec((tm, A_WIDTH), lambda i: (i, zcol // A_WIDTH))],
        out_specs=blk, out_shape=jax.ShapeDtypeStruct((S, A_WIDTH), BF16),
        compiler_params=_cp(("parallel",)))(o, ur)


def _gate_bwd(o, ur, zcol, dy, du_r, *, name):
    S = ur.shape[0]
    tm = min(1024, S)

    def body(o_ref, z_ref, dy_ref, du_in, do_ref, dz_ref):
        sz, dsz = _silu_parts(z_ref[...])
        dyv = dy_ref[...]
        do_ref[...] = (dyv * sz).astype(BF16)
        dz_ref[...] = (dyv * o_ref[...] * dsz).astype(BF16)

    blk = pl.BlockSpec((tm, A_WIDTH), lambda i: (i, 0))
    gate = pl.BlockSpec((tm, A_WIDTH), lambda i: (i, zcol // A_WIDTH))
    return pl.pallas_call(
        body, name=name, grid=(S // tm,),
        in_specs=[blk, gate, blk, ANY],
        out_specs=[blk, gate],
        out_shape=[jax.ShapeDtypeStruct((S, A_WIDTH), BF16), jax.ShapeDtypeStruct(du_r.shape, BF16)],
        input_output_aliases={3: 1},
        compiler_params=_cp(("parallel",)))(o, ur, dy, du_r)


def _dfb(ur, bf_pad, dlogf_pad, du_r, *, name):
    S = ur.shape[0]
    tm = min(1024, S)

    def body(u_ref, b_ref, d_ref, du_in, o_ref, s_ref):
        i = pl.program_id(0)
        dv = d_ref[...] * _sigmoid(-(u_ref[...] + b_ref[...]))
        o_ref[...] = dv.astype(BF16)
        part = jnp.sum(dv, axis=0, keepdims=True)

        @pl.when(i == 0)
        def _():
            s_ref[...] = part

        @pl.when(i > 0)
        def _():
            s_ref[...] += part

    vec = pl.BlockSpec((1, FB_PAD), lambda i: (0, 0))
    blk = pl.BlockSpec((tm, FB_PAD), lambda i: (i, 0))
    fb = pl.BlockSpec((tm, FB_PAD), lambda i: (i, R_FB // FB_PAD))
    return pl.pallas_call(
        body, name=name, grid=(S // tm,),
        in_specs=[fb, vec, blk, ANY],
        out_specs=[fb, vec],
        out_shape=[jax.ShapeDtypeStruct(du_r.shape, BF16), jax.ShapeDtypeStruct((1, FB_PAD), F32)],
        input_output_aliases={3: 0},
        compiler_params=_cp(("arbitrary",)))(ur, bf_pad, dlogf_pad, du_r)


M_SCALE = HEAD ** -0.5


def _mem_fwd(ur, mkv, *, name):
    S = ur.shape[0]
    T = min(512, S)

    def body(q_ref, z_ref, k_ref, v_ref, y_ref):
        for h in range(M_HEADS):
            hs = slice(h * HEAD, (h + 1) * HEAD)
            s = _dot_nt(q_ref[:, hs].astype(BF16), k_ref[:, hs].astype(BF16)) * M_SCALE
            p = jnp.exp(s - jnp.max(s, axis=-1, keepdims=True))
            p = p / jnp.sum(p, axis=-1, keepdims=True)
            o = _dot(p.astype(BF16), v_ref[:, hs].astype(BF16))
            y_ref[:, hs] = (o * _silu_parts(z_ref[:, hs])[0]).astype(BF16)

    wide = lambda col: pl.BlockSpec((T, A_WIDTH), lambda i: (i, col // A_WIDTH))
    kv = lambda half: pl.BlockSpec((N_MEM, A_WIDTH), lambda i: (0, half))
    return pl.pallas_call(
        body, name=name, grid=(S // T,),
        in_specs=[wide(R_QM), wide(R_ZM), kv(0), kv(1)],
        out_specs=pl.BlockSpec((T, A_WIDTH), lambda i: (i, 0)),
        out_shape=jax.ShapeDtypeStruct((S, A_WIDTH), BF16),
        compiler_params=_cp(("parallel",)))(ur, ur, mkv, mkv)


def _mem_bwd(ur, mkv, dy, du_r, *, name):
    S = ur.shape[0]
    T = min(512, S)

    def body(q_ref, z_ref, k_ref, v_ref, dy_ref, du_in, du_ref, dk_ref, dv_ref):
        i = pl.program_id(0)

        @pl.when(i == 0)
        def _():
            dk_ref[...] = jnp.zeros((N_MEM, A_WIDTH), F32)
            dv_ref[...] = jnp.zeros((N_MEM, A_WIDTH), F32)

        for h in range(M_HEADS):
            hs = slice(h * HEAD, (h + 1) * HEAD)
            qv = q_ref[:, hs].astype(BF16)
            kv = k_ref[:, hs].astype(BF16)
            vv = v_ref[:, hs].astype(BF16)
            s = _dot_nt(qv, kv) * M_SCALE
            p = jnp.exp(s - jnp.max(s, axis=-1, keepdims=True))
            p = p / jnp.sum(p, axis=-1, keepdims=True)
            o = _dot(p.astype(BF16), vv)
            sz, dsz = _silu_parts(z_ref[:, hs])
            dyv = dy_ref[:, hs]
            du_ref[:, A_WIDTH + h * HEAD:A_WIDTH + (h + 1) * HEAD] = (dyv * o * dsz).astype(BF16)
            dov = (dyv * sz).astype(BF16)
            dp = _dot_nt(dov, vv)
            ds = p * (dp - jnp.sum(p * dp, axis=-1, keepdims=True))
            du_ref[:, hs] = (_dot(ds.astype(BF16), kv) * M_SCALE).astype(BF16)
            dv_ref[:, hs] += _dot(p.T.astype(BF16), dov)
            dk_ref[:, hs] += _dot(ds.T.astype(BF16), qv) * M_SCALE

    wide = lambda col: pl.BlockSpec((T, A_WIDTH), lambda i: (i, col // A_WIDTH))
    kv = lambda half: pl.BlockSpec((N_MEM, A_WIDTH), lambda i: (0, half))
    tile = pl.BlockSpec((T, A_WIDTH), lambda i: (i, 0))
    acc = pl.BlockSpec((N_MEM, A_WIDTH), lambda i: (0, 0))
    assert R_ZM == R_QM + A_WIDTH and R_QM % (2 * A_WIDTH) == 0
    return pl.pallas_call(
        body, name=name, grid=(S // T,),
        in_specs=[wide(R_QM), wide(R_ZM), kv(0), kv(1), tile, ANY],
        out_specs=[pl.BlockSpec((T, 2 * A_WIDTH), lambda i: (i, R_QM // (2 * A_WIDTH))), acc, acc],
        out_shape=[jax.ShapeDtypeStruct(du_r.shape, BF16)] + [jax.ShapeDtypeStruct((N_MEM, A_WIDTH), F32)] * 2,
        input_output_aliases={5: 0},
        compiler_params=_cp(("arbitrary",)))(ur, ur, mkv, mkv, dy, du_r)


def _branch_fwd(ys, wbs, ur, b_merge, *, name):
    S = ur.shape[0]
    tm, tn = min(512, S), 512
    nj = D_MODEL // tn

    def body(ya, yb, ym, wa, wb, wm, g0, g1, g2, b0, b1, b2, mg_ref, p_ref):
        acc = jnp.zeros((tm, tn), F32)
        for i, (y, w, gr, br) in enumerate(((ya, wa, g0, b0), (yb, wb, g1, b1), (ym, wm, g2, b2))):
            pr = _dot(y[...], w[...])
            p_ref[i] = pr.astype(BF16)
            acc = acc + _sigmoid(gr[...] + br[...]) * pr
        mg_ref[...] = acc.astype(BF16)

    yspec = pl.BlockSpec((tm, A_WIDTH), lambda i, j: (i, 0))
    wspec = pl.BlockSpec((A_WIDTH, tn), lambda i, j: (0, j))
    gspec = lambda b: pl.BlockSpec((tm, tn), lambda i, j: (i, (R_GL + b * D_MODEL) // tn + j))
    bspec = lambda b: pl.BlockSpec((1, tn), lambda i, j: (0, b * nj + j))
    return pl.pallas_call(
        body, name=name, grid=(S // tm, nj),
        in_specs=[yspec] * 3 + [wspec] * 3 + [gspec(0), gspec(1), gspec(2), bspec(0), bspec(1), bspec(2)],
        out_specs=[pl.BlockSpec((tm, tn), lambda i, j: (i, j)),
                   pl.BlockSpec((3, tm, tn), lambda i, j: (0, i, j))],
        out_shape=[jax.ShapeDtypeStruct((S, D_MODEL), BF16), jax.ShapeDtypeStruct((3, S, D_MODEL), BF16)],
        compiler_params=_cp(("parallel", "parallel")))(*ys, *wbs, ur, ur, ur, b_merge, b_merge, b_merge)


def _branch_bwd(dm, prods, ur, b_merge, *, name):
    S = ur.shape[0]
    tm = min(256, S)

    def body(dm_ref, p_ref, g0, g1, g2, b_ref, dp0, dp1, dp2, dgl_ref, db_ref):
        i = pl.program_id(0)
        dmv = dm_ref[...]
        parts = []
        for b, (gr, dp_ref) in enumerate(((g0, dp0), (g1, dp1), (g2, dp2))):
            sl = slice(b * D_MODEL, (b + 1) * D_MODEL)
            gt = _sigmoid(gr[...] + b_ref[:, sl])
            dp_ref[...] = (dmv * gt).astype(BF16)
            dgl = dmv * p_ref[b].astype(F32) * gt * (1.0 - gt)
            dgl_ref[:, R_GL + b * D_MODEL:R_GL + (b + 1) * D_MODEL] = dgl.astype(BF16)
            parts.append(jnp.sum(dgl, axis=0, keepdims=True))
        part = jnp.concatenate(parts, axis=1)

        @pl.when(i == 0)
        def _():
            db_ref[...] = part

        @pl.when(i > 0)
        def _():
            db_ref[...] += part

    gspec = lambda b: pl.BlockSpec((tm, D_MODEL), lambda i: (i, R_GL // D_MODEL + b))
    vec = pl.BlockSpec((1, 3 * D_MODEL), lambda i: (0, 0))
    row = pl.BlockSpec((tm, D_MODEL), lambda i: (i, 0))
    outs = pl.pallas_call(
        body, name=name, grid=(S // tm,),
        in_specs=[row, pl.BlockSpec((3, tm, D_MODEL), lambda i: (0, i, 0)), gspec(0), gspec(1), gspec(2), vec],
        out_specs=[row, row, row, pl.BlockSpec((tm, NR), lambda i: (i, 0)), vec],
        out_shape=[jax.ShapeDtypeStruct((S, D_MODEL), BF16)] * 3
        + [jax.ShapeDtypeStruct((S, NR), BF16), jax.ShapeDtypeStruct((1, 3 * D_MODEL), F32)],
        compiler_params=_cp(("arbitrary",)))(dm, prods, ur, ur, ur, b_merge)
    return outs[0:3], outs[3], outs[4]


def _rope_tables(pos):
    half = ROT // 2
    S = pos.shape[0]
    inv = ROPE_THETA ** (-jnp.arange(half, dtype=F32) / half)
    per_row = LANES // half
    ang = jnp.repeat(pos.astype(F32).reshape(S // per_row, per_row), half, axis=1) * jnp.tile(inv, per_row)
    cos, sin = lax.optimization_barrier((jnp.cos(ang).reshape(S, half), jnp.sin(ang).reshape(S, half)))
    one = jnp.ones((S, LANES - ROT), F32)
    zero = jnp.zeros((S, LANES - ROT), F32)
    zh = jnp.zeros((S, half), F32)
    c = jnp.concatenate([cos, cos, one], axis=1)
    s1 = jnp.concatenate([-sin, zh, zero], axis=1)
    s2 = jnp.concatenate([zh, sin, zero], axis=1)
    return c, s1, s2


def _to_tiles(t):
    S, H = t.shape
    return t.reshape(S // LANES, LANES, H).transpose(0, 2, 1)


def _from_tiles(t):
    nt, H, _ = t.shape
    return t.transpose(1, 0, 2).reshape(H, nt * LANES)


def _local_step(x, mem, pos, tgt, g_pre, g_post, g_mem, wt, bf_pad, b_merge, w_kv, wbs, w_out, pack=None, hs=None):
    S = x.shape[0]
    T = min(512, S)
    nq = S // T
    tabs = _rope_tables(pos)

    if hs is None:
        hs = _rms_fwd(x, g_pre, name="rms_pre", dilations=DIL)
    h = hs[0]
    tabs_g = [[_to_classes(t, d) for t in tabs] for d in DIL]
    qkvs = [_proj_rope(hs[g], wt[f"A{g}"], tabs_g[g], name=f"proj_a{g}") for g in range(3)]
    ub = _mm(h, wt["B"], bt=True, out_dtype=BF16, name="proj_b", tn=1536)
    ur = _mm(h, wt["R"], bt=True, name="proj_r", tn=1792)

    outs_c, lses_c = [], []
    for g in range(3):
        o, l = _attn_a_fwd(qkvs[g], g, name=f"attn_a_fwd{g}")
        outs_c.append(o)
        lses_c.append(l)
    ya = _merge_a_fwd(outs_c, lses_c, ur, name="merge_a_fwd")

    logf = _logf(ur, bf_pad, name="logf")
    c = _from_tiles(_cumsum_lanes(_to_tiles(logf[:, :B_HEADS]), False, name="cumsum_fwd"))
    qaug, kaug = _fox_aug(ub, c, name="fox_aug")
    kt = ub[:, 512:1024].reshape(nq, T, 512).transpose(0, 2, 1)
    vt = ub[:, 1024:1536].reshape(nq, T, 512).transpose(0, 2, 1)
    ob, lse_b = _fox_fwd(qaug, kaug, vt, name="fox_fwd")
    yb = _gate_fwd(ob, ur, R_ZB, name="gate_b_fwd")

    hm = _rms_fwd(mem, g_mem, name="rms_mem")
    mkv = _mm(hm, w_kv, name="proj_mem")
    ym = _mem_fwd(ur, mkv, name="mem_fwd")

    merged, prods = _branch_fwd((ya, yb, ym), wbs, ur, b_merge, name="branch_fwd")
    out = _mm(merged, w_out, name="proj_out")
    dy, d_out, dg_post, loss_row = _post(x, out, tgt, g_post, name="post")

    dmerged = _mm(d_out, w_out, bt=True, name="d_merged")
    dw_out = _mm(merged, d_out, at=True, name="dw_out", tk=2048)
    dprods, du_r, db_merge = _branch_bwd(dmerged, prods, ur, b_merge, name="branch_bwd")
    dys, dwbs = [], []
    for i, (y, wb) in enumerate(zip((ya, yb, ym), wbs)):
        dys.append(_mm(dprods[i], wb, bt=True, name=f"d_y{i}"))
        dwbs.append(_mm(y, dprods[i], at=True, name=f"dw_branch{i}", tk=2048))

    dos_c, adjs_c, du_r = _merge_a_bwd(outs_c, lses_c, ur, dys[0], du_r, name="merge_a_bwd")
    dus_a = []
    for g, d in enumerate(DIL):
        do_c, adj_c = dos_c[g], adjs_c[g]
        du = _attn_a_dkv(qkvs[g], tabs_g[g], g, do_c, lses_c[g], adj_c, name=f"attn_a_dkv{g}")
        dus_a.append(_attn_a_dq(qkvs[g], tabs_g[g], g, do_c, lses_c[g], adj_c, du, name=f"attn_a_dq{g}"))

    dob, du_r = _gate_bwd(ob, ur, R_ZB, dys[1], du_r, name="gate_b_bwd")
    delta_b = _fox_delta(ob, dob, name="fox_delta")
    dkb, dvb, dc_k, dqt, dc_q = _fox_bwd(ub, qaug, kaug, kt, dob, lse_b, delta_b, name="fox_bwd")
    dqb = (dqt.transpose(0, 2, 1).reshape(S, A_WIDTH) * B_SCALE).astype(BF16)
    du_b = jnp.concatenate([dqb, dkb, dvb], axis=1)
    dc = dc_q.reshape(B_HEADS, S) + dc_k.reshape(B_HEADS, S)
    dlogf = _from_tiles(_cumsum_lanes(_to_tiles(dc.T), True, name="cumsum_bwd"))
    dlogf_pad = jnp.pad(dlogf.T, ((0, 0), (0, FB_PAD - B_HEADS)))
    du_r, db_forget = _dfb(ur, bf_pad, dlogf_pad, du_r, name="dfb")

    du_r, dmk, dmv = _mem_bwd(ur, mkv, dys[2], du_r, name="mem_bwd")
    dmkv = jnp.concatenate([dmk, dmv], axis=1).astype(BF16)
    dhm = _mm(dmkv, w_kv, bt=True, name="d_hm")
    dw_kv = _mm(hm, dmkv, at=True, name="dw_kv")
    dg_mem = _rms_bwd(mem, g_mem, dhm, None, name="rms_mem_bwd")

    dwt ={"R": _mm(du_r, h, at=True, name="dw_in_r", tm=1792, tk=1024),
           "B": _mm(du_b, h, at=True, name="dw_in_b", tm=1536, tk=2048)}
    for g in range(3):
        dwt[f"A{g}"] = _mm(dus_a[g], hs[g], at=True, name=f"dw_in_a{g}", tm=1536, tk=2048)
    res = dict(dwt=dwt, dw_kv=dw_kv, dwbs=dwbs, dw_out=dw_out)
    token_major = [(du_r, wt["R"]), (du_b, wt["B"]), (dus_a[0], wt["A0"])]
    if pack is None:
        dh_1 = _mm(dus_a[1], wt["A1"], name="d_h_a1", tk=1536)
        dh = _mm_sum(token_major, name="d_h_main")
    else:
        gbig = pack(dwt, dw_kv, dwbs, dw_out)
        own_idx = _own_slabs()
        dh_1, sib = _mm(dus_a[1], wt["A1"], name="d_h_a1", tk=1536, comm=_pair_comm(gbig))
        send = _pair_sum(gbig, sib, own_idx, 208, name="pair_sum")
        dh, recv = _mm_sum(token_major, name="d_h_main", comm=_chips_comm(send))
        res = dict(parts=[(gbig, None), (sib, 1), (recv, N_CHIP - 1)], own_idx=own_idx)
    dh_2 = _mm(dus_a[2], wt["A2"], name="d_h_a2", tk=1536)
    grad_x, dg_pre = _rms_bwd(x, g_pre, dh, dy, name="rms_pre_bwd", dh_classes=[(dh_1, DIL[1]), (dh_2, DIL[2])])

    return dict(res, loss=loss_row, grad_x=grad_x, dg_pre=dg_pre, dg_post=dg_post, dg_mem=dg_mem,
                db_forget=db_forget, db_merge=db_merge)


MESH = pl.DeviceIdType.MESH
ANY = pl.BlockSpec(memory_space=pl.ANY)


def _relations():
    return [(k >> 2 & 1, k >> 1 & 1, k & 1) for k in range(1, N_DEV)]


def _coords():
    return lax.axis_index("x"), lax.axis_index("y"), lax.axis_index("c")


def _gather_comm(shard):
    R, W = shard.shape

    def plan(x_ref, out_ref, send_sems, recv_sems, local_sem):
        x, y, c = _coords()
        me, sibling = (x, y, c), (x, y, 1 - c)
        chips = [(1 - x, y), (x, 1 - y), (1 - x, 1 - y)]

        def slot(px, py, pc):
            return out_ref.at[4 * px + 2 * py + pc]

        def copy(k, block, to, src=None):
            return pltpu.make_async_remote_copy(
                src_ref=slot(*block) if src is None else src, dst_ref=slot(*block),
                send_sem=send_sems.at[k], recv_sem=recv_sems.at[k], device_id=to, device_id_type=MESH)

        mine = pltpu.make_async_copy(x_ref, slot(*me), local_sem)
        first = [copy(0, me, sibling, src=x_ref)]
        first += [copy(1 + j, me, (*chip, c), src=x_ref) for j, chip in enumerate(chips)]
        return me, sibling, chips, c, copy, mine, first

    def start(*refs):
        _, _, _, _, _, mine, first = plan(*refs)
        mine.start()
        for cp in first:
            cp.start()

    def wait(*refs):
        me, sibling, chips, c, copy, mine, first = plan(*refs)
        passed = [copy(4 + j, (*chip, c), sibling) for j, chip in enumerate(chips)]
        for j, chip in enumerate(chips):
            copy(1 + j, (*chip, c), me).wait_recv()
            passed[j].start()
        copy(0, sibling, me).wait_recv()
        for j, chip in enumerate(chips):
            copy(4 + j, (*chip, 1 - c), me).wait_recv()
        for cp in first + passed:
            cp.wait_send()
        mine.wait()

    return dict(inputs=[shard], out_shape=[jax.ShapeDtypeStruct((N_DEV, R, W), shard.dtype)],
                sems=[pltpu.SemaphoreType.DMA((N_DEV - 1,)), pltpu.SemaphoreType.DMA((N_DEV - 1,)),
                      pltpu.SemaphoreType.DMA],
                start=start, wait=wait)


N_CHIP = 4


def _pair_comm(gbig):
    _, R, W = gbig.shape

    def copies(g_ref, sib_ref, send_sems, recv_sems):
        x, y, c = _coords()
        return [pltpu.make_async_remote_copy(
            src_ref=g_ref.at[4 * (x ^ (r >> 1)) + 2 * (y ^ (r & 1)) + (1 - c)], dst_ref=sib_ref.at[r],
            send_sem=send_sems.at[r], recv_sem=recv_sems.at[r], device_id=(x, y, 1 - c), device_id_type=MESH)
            for r in range(N_CHIP)]

    def start(*refs):
        for cp in copies(*refs):
            cp.start()

    def wait(*refs):
        cps = copies(*refs)
        for cp in cps:
            cp.wait_recv()
        for cp in cps:
            cp.wait_send()

    return dict(inputs=[gbig], out_shape=[jax.ShapeDtypeStruct((N_CHIP, R, W), gbig.dtype)],
                sems=[pltpu.SemaphoreType.DMA((N_CHIP,)), pltpu.SemaphoreType.DMA((N_CHIP,))],
                start=start, wait=wait)


def _own_slabs():
    x, y, c = _coords()
    return jnp.stack([4 * (x ^ (r >> 1)) + 2 * (y ^ (r & 1)) + c for r in range(N_CHIP)]).astype(jnp.int32)


def _pair_sum(gbig, sib, own_idx, tr, *, name):
    _, R, W = gbig.shape

    def body(idx_ref, a_ref, b_ref, o_ref):
        o_ref[...] = (a_ref[...] + b_ref[...]).astype(BF16)

    return pl.pallas_call(
        body, name=name,
        grid_spec=pltpu.PrefetchScalarGridSpec(
            num_scalar_prefetch=1, grid=(N_CHIP - 1, R // tr),
            in_specs=[pl.BlockSpec((None, tr, W), lambda r, i, idx: (idx[r + 1], i, 0)),
                      pl.BlockSpec((None, tr, W), lambda r, i, idx: (r + 1, i, 0))],
            out_specs=pl.BlockSpec((None, tr, W), lambda r, i, idx: (r, i, 0))),
        out_shape=jax.ShapeDtypeStruct((N_CHIP - 1, R, W), BF16),
        compiler_params=_cp(("parallel", "parallel")))(own_idx, gbig, sib)


def _chips_comm(send):
    nb, R, W = send.shape

    def copies(b_ref, rb_ref, send_sems, recv_sems):
        x, y, c = _coords()
        return [pltpu.make_async_remote_copy(
            src_ref=b_ref.at[r - 1], dst_ref=rb_ref.at[r - 1], send_sem=send_sems.at[r - 1],
            recv_sem=recv_sems.at[r - 1], device_id=(x ^ (r >> 1), y ^ (r & 1), c), device_id_type=MESH)
            for r in range(1, N_CHIP)]

    def start(*refs):
        for cp in copies(*refs):
            cp.start()

    def wait(*refs):
        cps = copies(*refs)
        for cp in cps:
            cp.wait_recv()
        for cp in cps:
            cp.wait_send()

    return dict(inputs=[send], out_shape=[jax.ShapeDtypeStruct((nb, R, W), send.dtype)],
                sems=[pltpu.SemaphoreType.DMA((nb,)), pltpu.SemaphoreType.DMA((nb,))],
                start=start, wait=wait)


def _gather_small(gsmall, *, name):
    n = N_DEV - 1

    def body(s_ref, rs_ref, send_sems, recv_sems, local_sem):
        x, y, c = _coords()
        me = 4 * x + 2 * y + c
        mine = pltpu.make_async_copy(s_ref, rs_ref.at[me], local_sem)
        mine.start()

        def copy(k, fx, fy, fc, slot):
            return pltpu.make_async_remote_copy(
                src_ref=s_ref, dst_ref=rs_ref.at[slot], send_sem=send_sems.at[k], recv_sem=recv_sems.at[k],
                device_id=(x ^ fx, y ^ fy, c ^ fc), device_id_type=MESH)

        started = [copy(k, *rel, me) for k, rel in enumerate(_relations())]
        for cp in started:
            cp.start()
        for k, (fx, fy, fc) in enumerate(_relations()):
            copy(k, fx, fy, fc, 4 * (x ^ fx) + 2 * (y ^ fy) + (c ^ fc)).wait_recv()
        for cp in started:
            cp.wait_send()
        mine.wait()

    return pl.pallas_call(
        body, name=name, out_shape=jax.ShapeDtypeStruct((N_DEV, 1, P_SMALL), gsmall.dtype),
        in_specs=[ANY], out_specs=ANY,
        scratch_shapes=[pltpu.SemaphoreType.DMA((n,)), pltpu.SemaphoreType.DMA((n,)), pltpu.SemaphoreType.DMA],
    )(gsmall)


def _part_specs(parts, tr, row0):
    assert row0 % tr == 0
    specs = []
    for a, n_used in parts:
        if n_used is None:
            specs.append(pl.BlockSpec((1, tr, a.shape[2]), lambda i, idx: (idx[0], row0 // tr + i, 0)))
        else:
            specs.append(pl.BlockSpec((n_used, tr, a.shape[2]), lambda i, idx: (0, row0 // tr + i, 0)))
    return specs


def _part_total(refs, parts):
    g = None
    for ref, (_, n_used) in zip(refs, parts):
        for k in range(n_used or 1):
            t = ref[k].astype(F32)
            g = t if g is None else g + t
    return g


def _sum_parts(parts, idx, row0, nrows, tr, *, name):
    W = parts[0][0].shape[2]
    assert nrows % tr == 0

    def body(idx_ref, *refs):
        refs[-1][...] = _part_total(refs[:-1], parts)

    return pl.pallas_call(
        body, name=name,
        grid_spec=pltpu.PrefetchScalarGridSpec(
            num_scalar_prefetch=1, grid=(nrows // tr,), in_specs=_part_specs(parts, tr, row0),
            out_specs=pl.BlockSpec((tr, W), lambda i, idx: (i, 0))),
        out_shape=jax.ShapeDtypeStruct((nrows, W), F32),
        compiler_params=_cp(("parallel",)))(idx, *[a for a, _ in parts])


def _adamw(parts, idx, w, m, v, tr, *, name):
    R, W = w.shape
    assert R % tr == 0
    np_ = len(parts)

    def body(idx_ref, *refs):
        w_ref, m_ref, v_ref, g_ref, d_ref, nm_ref, nv_ref = refs[np_:]
        g = _part_total(refs[:np_], parts)
        mm = ADAM_B1 * m_ref[...] + (1.0 - ADAM_B1) * g
        vv = ADAM_B2 * v_ref[...] + (1.0 - ADAM_B2) * (g * g)
        m_hat = mm / (1.0 - ADAM_B1 ** ADAM_STEP)
        v_hat = vv / (1.0 - ADAM_B2 ** ADAM_STEP)
        g_ref[...] = g
        d_ref[...] = -ADAM_LR * (m_hat / (jnp.sqrt(v_hat) + ADAM_EPS) + ADAM_WD * w_ref[...])
        nm_ref[...] = mm
        nv_ref[...] = vv

    blk = pl.BlockSpec((tr, W), lambda i, idx: (i, 0))
    return pl.pallas_call(
        body, name=name,
        grid_spec=pltpu.PrefetchScalarGridSpec(
            num_scalar_prefetch=1, grid=(R // tr,), in_specs=_part_specs(parts, tr, 0) + [blk, blk, blk],
            out_specs=[blk] * 4),
        out_shape=[jax.ShapeDtypeStruct((R, W), F32)] * 4,
        compiler_params=_cp(("parallel",)))(idx, *[a for a, _ in parts], w, m, v)


def _pack_rest(w_kv, wa, wb, wm, w_out):
    return jnp.concatenate([w_kv[0], w_out[0]] + [t[0].reshape(-1, D_MODEL) for t in (wa, wb, wm)], axis=0)


def _unpack_rest(t):
    br = lambda i: t[RO_BR + 64 * i:RO_BR + 64 * (i + 1)].reshape(1, A_WIDTH, D_MODEL // N_DEV)
    return t[None, RO_KV:RO_OUT], br(0), br(1), br(2), t[None, RO_OUT:RO_BR]


def _orig_rows(gathered, a, b):
    res = []
    while a < b:
        dev, r = divmod(a, CS)
        n = min(b - a, CS - r)
        res.append(gathered[dev, RO_IN + r:RO_IN + r + n])
        a += n
    return res


def _full_weights(gathered):
    wt = {}
    for name, ranges in SEGS.items():
        rows = [p for a, b in ranges for p in _orig_rows(gathered, a, b)]
        if SEG_PAD[name]:
            rows.append(jnp.zeros((SEG_PAD[name], D_MODEL), gathered.dtype))
        wt[name] = jnp.concatenate(rows, axis=0)
    w_kv = gathered[:, RO_KV:RO_OUT].reshape(D_MODEL, D_MODEL)
    w_out = gathered[:, RO_OUT:RO_BR].reshape(D_MODEL, D_MODEL)
    wbs = [gathered[:, RO_BR + 64 * i:RO_BR + 64 * (i + 1)].reshape(N_DEV, A_WIDTH, D_MODEL // N_DEV)
           .transpose(1, 0, 2).reshape(A_WIDTH, D_MODEL) for i in range(3)]
    return wt, w_kv, wbs, w_out


def _orig_order(dwt):
    pieces = []
    for name, ranges in SEGS.items():
        o = 0
        for a, b in ranges:
            pieces.append((a, dwt[name][o:o + b - a]))
            o += b - a
    pieces.sort(key=lambda p: p[0])
    return jnp.concatenate([p[1] for p in pieces], axis=0)


def _pack_grads(dwt, dw_kv, dwbs, dw_out):
    g_in = jnp.pad(_orig_order(dwt).reshape(N_DEV, CS, D_MODEL), ((0, 0), (0, IN_ROWS - CS), (0, 0)))
    br = [t.reshape(A_WIDTH, N_DEV, D_MODEL // N_DEV).transpose(1, 0, 2).reshape(N_DEV, -1, D_MODEL) for t in dwbs]
    return jnp.concatenate([dw_kv.reshape(N_DEV, -1, D_MODEL), dw_out.reshape(N_DEV, -1, D_MODEL)] + br + [g_in],
                           axis=1)


def kernel(x, mem, positions, norm_pre_g, norm_post_g, norm_mem_g, w_in, b_forget, b_merge, w_mem_kv, w_branch_a, w_branch_b, w_branch_m, w_out, loss_target, m_norm_pre_g, m_norm_post_g, m_norm_mem_g, m_w_in, m_b_forget, m_b_merge, m_w_mem_kv, m_w_branch_a, m_w_branch_b, m_w_branch_m, m_w_out, v_norm_pre_g, v_norm_post_g, v_norm_mem_g, v_w_in, v_b_forget, v_b_merge, v_w_mem_kv, v_w_branch_a, v_w_branch_b, v_w_branch_m, v_w_out):
    w_rest = _pack_rest(w_mem_kv, w_branch_a, w_branch_b, w_branch_m, w_out)
    shard = jnp.concatenate([w_rest.astype(BF16), w_in[0].T.astype(BF16),
                             jnp.zeros((IN_ROWS - CS, D_MODEL), BF16)], axis=0)
    hs, (gathered,) = _rms_fwd(x[0], norm_pre_g, name="rms_pre_gather", dilations=DIL, comm=_gather_comm(shard))
    wt, w_kv, wbs, w_o = _full_weights(gathered)

    bf_pad = jnp.pad(b_forget, ((0, 0), (0, FB_PAD - B_HEADS)))
    r = _local_step(x[0], mem[0], positions[0], loss_target[0], norm_pre_g, norm_post_g, norm_mem_g,
                    wt, bf_pad, b_merge, w_kv, wbs, w_o, pack=_pack_grads, hs=hs)

    gsmall = jnp.concatenate([r["dg_pre"], r["dg_post"], r["dg_mem"], r["db_merge"],
                              r["db_forget"][:, :LANES], r["loss"]], axis=1)
    rsmall = _gather_small(gsmall, name="gather_small")
    parts, own_idx = r["parts"], r["own_idx"]

    m_rest = _pack_rest(m_w_mem_kv, m_w_branch_a, m_w_branch_b, m_w_branch_m, m_w_out)
    v_rest = _pack_rest(v_w_mem_kv, v_w_branch_a, v_w_branch_b, v_w_branch_m, v_w_out)
    gsum = _sum_parts(parts, own_idx, 0, ROWS, 208, name="sum_grads")
    outs_rest = [_unpack_rest(t) for t in
                 _adamw([(gsum[None], 1)], own_idx, w_rest, m_rest, v_rest, 64, name="adamw_rest")]
    g_in = gsum[RO_IN:RO_IN + CS].T
    outs_in = _adamw([(g_in[None], 1)], own_idx, w_in[0], m_w_in[0], v_w_in[0], 128, name="adamw_w_in")

    def small_vec(a, b, c, d, e):
        z = jnp.zeros((1, LANES - B_HEADS), F32)
        return jnp.concatenate([a, b, c, d, e, z, jnp.zeros((1, LANES), F32)], axis=1)

    outs_small = _adamw([(rsmall, N_DEV)], own_idx, small_vec(norm_pre_g, norm_post_g, norm_mem_g, b_merge, b_forget),
                        small_vec(m_norm_pre_g, m_norm_post_g, m_norm_mem_g, m_b_merge, m_b_forget),
                        small_vec(v_norm_pre_g, v_norm_post_g, v_norm_mem_g, v_b_merge, v_b_forget),
                        1, name="adamw_small")

    def small_parts(t):
        return [t[:, O_GPRE:O_GPRE + D_MODEL], t[:, O_GPOST:O_GPOST + D_MODEL], t[:, O_GMEM:O_GMEM + D_MODEL],
                t[:, O_BF:O_BF + B_HEADS], t[:, O_BM:O_BM + 3 * D_MODEL]]

    loss = outs_small[0][0, O_LOSS]
    result = [loss, r["grad_x"][None]]
    for rest, w_i, small in zip(outs_rest, outs_in, outs_small):
        gp, gq, gm, bf, bm = small_parts(small)
        w_k, w_a, w_b, w_m, w_ot = rest
        result += [gp, gq, gm, w_i[None], bf, bm, w_k, w_a, w_b, w_m, w_ot]
    return tuple(result)
```

```python
import jax
import jax.numpy as jnp
from jax import lax
from jax.experimental import pallas as pl
from jax.experimental.pallas import tpu as pltpu

F32 = jnp.float32
BF16 = jnp.bfloat16

N_DEV = 8
D_MODEL = 1024
N_MEM = 256
EPS = 1e-6
NEG = -1e30
ROPE_THETA = 500000.0
DIL = (1, 4, 16)
A_HEADS = 4
HEAD = 128
A_WIDTH = 512
B_HEADS = 8
B_HEAD = 64
M_HEADS = 4
ROT = 32
IN_COLS = 11272
FB_PAD = 256

SEGS = {
    "A0": ((0, 512), (1536, 2048), (3072, 3584)),
    "A1": ((512, 1024), (2048, 2560), (3584, 4096)),
    "A2": ((1024, 1536), (2560, 3072), (4096, 4608)),
    "B": ((5120, 6656),),
    "R": ((4608, 5120), (6664, 7176), (7176, 7688), (7688, 8200), (8200, 11272), (6656, 6664)),
}
SEG_PAD = {"A0": 0, "A1": 0, "A2": 0, "B": 0, "R": FB_PAD - B_HEADS}
R_ZA, R_ZB, R_QM, R_ZM, R_GL, R_FB = 0, 512, 1024, 1536, 2048, 5120
NR = R_FB + FB_PAD

ADAM_LR, ADAM_B1, ADAM_B2, ADAM_EPS, ADAM_WD, ADAM_STEP = 0.001, 0.9, 0.999, 1e-08, 0.01, 10

LANES = 128
VMEM_LIMIT = 56 * 1024 * 1024

CS = IN_COLS // N_DEV
RO_KV, RO_OUT, RO_BR, RO_IN = 0, 128, 256, 448
IN_ROWS = 1424
ROWS = RO_IN + IN_ROWS
O_GPRE, O_GPOST, O_GMEM, O_BM, O_BF, O_LOSS = 0, 1024, 2048, 3072, 6144, 6272
P_SMALL = 6400


def _cp(sem=None):
    return pltpu.CompilerParams(dimension_semantics=sem, vmem_limit_bytes=VMEM_LIMIT)


def _dot(a, b):
    return jnp.dot(a, b, preferred_element_type=F32)


def _dot_nt(a, b):
    return lax.dot_general(a, b, (((1,), (1,)), ((), ())), preferred_element_type=F32)


def _sigmoid(z):
    return 1.0 / (1.0 + jnp.exp(-z))


def _mm(a, b, *, name, at=False, bt=False, out_dtype=F32, tm=1024, tn=1024, tk=None, comm=None):
    assert not (at and bt)
    K, M = a.shape if at else a.shape[::-1]
    N = b.shape[0] if bt else b.shape[1]
    tm, tn = min(tm, M), min(tn, N)
    tk = K if tk is None else min(tk, K)
    assert M % tm == 0 and N % tn == 0 and K % tk == 0
    nk = K // tk
    grid = (M // tm, N // tn, nk)
    n_in = len(comm["inputs"]) if comm else 0
    n_out = len(comm["out_shape"]) if comm else 0

    def body(a_ref, b_ref, *rest):
        c_in, o_ref, c_out = rest[:n_in], rest[n_in], rest[n_in + 1:n_in + 1 + n_out]
        acc_ref, sems = rest[n_in + 1 + n_out], rest[n_in + 2 + n_out:]
        if comm:
            step = (pl.program_id(0) * grid[1] + pl.program_id(1)) * grid[2] + pl.program_id(2)

            @pl.when(step == 0)
            def _():
                comm["start"](*c_in, *c_out, *sems)

        av = a_ref[...].astype(BF16)
        bv = b_ref[...].astype(BF16)
        if at:
            p = lax.dot_general(av, bv, (((0,), (0,)), ((), ())), preferred_element_type=F32)
        else:
            p = _dot_nt(av, bv) if bt else _dot(av, bv)
        if nk == 1:
            o_ref[...] = p.astype(out_dtype)
        else:
            k = pl.program_id(2)

            @pl.when(k == 0)
            def _():
                acc_ref[...] = p

            @pl.when(k > 0)
            def _():
                acc_ref[...] += p

            @pl.when(k == nk - 1)
            def _():
                o_ref[...] = acc_ref[...].astype(out_dtype)

        if comm:
            @pl.when(step == grid[0] * grid[1] * grid[2] - 1)
            def _():
                comm["wait"](*c_in, *c_out, *sems)

    b_spec = (pl.BlockSpec((tn, tk), lambda i, j, k: (j, k)) if bt
              else pl.BlockSpec((tk, tn), lambda i, j, k: (k, j)))
    a_spec = (pl.BlockSpec((tk, tm), lambda i, j, k: (k, i)) if at
              else pl.BlockSpec((tm, tk), lambda i, j, k: (i, k)))
    out_spec = pl.BlockSpec((tm, tn), lambda i, j, k: (i, j))
    out_shape = jax.ShapeDtypeStruct((M, N), out_dtype)
    acc = pltpu.VMEM((tm, tn) if nk > 1 else (8, LANES), F32)
    if not comm:
        return pl.pallas_call(
            body, name=name, grid=grid, in_specs=[a_spec, b_spec], out_specs=out_spec, out_shape=out_shape,
            scratch_shapes=[acc], compiler_params=_cp(("parallel", "parallel", "arbitrary")))(a, b)
    return pl.pallas_call(
        body, name=name, grid=grid, in_specs=[a_spec, b_spec] + [ANY] * n_in,
        out_specs=[out_spec] + [ANY] * n_out, out_shape=[out_shape] + comm["out_shape"],
        input_output_aliases={2 + i: 1 + o for i, o in comm.get("alias", {}).items()},
        scratch_shapes=[acc] + comm["sems"],
        compiler_params=_cp(("arbitrary", "arbitrary", "arbitrary")))(a, b, *comm["inputs"])


def _mm_sum(pairs, *, name, tm=1024, tk=768, comm=None):
    M, N = pairs[0][0].shape[0], pairs[0][1].shape[1]
    tm = min(tm, M)
    steps = [a.shape[1] // tk for a, _ in pairs]
    assert M % tm == 0 and all(a.shape[1] % tk == 0 for a, _ in pairs)
    first = [sum(steps[:p]) for p in range(len(pairs))]
    total = sum(steps)
    grid = (M // tm, total)
    n_in = len(comm["inputs"]) if comm else 0
    n_out = len(comm["out_shape"]) if comm else 0
    npair = len(pairs)

    def body(*refs):
        ab, rest = refs[:2 * npair], refs[2 * npair:]
        c_in, o_ref, c_out = rest[:n_in], rest[n_in], rest[n_in + 1:n_in + 1 + n_out]
        acc_ref, sems = rest[n_in + 1 + n_out], rest[n_in + 2 + n_out:]
        k = pl.program_id(1)
        if comm:
            step = pl.program_id(0) * total + k

            @pl.when(step == 0)
            def _():
                comm["start"](*c_in, *c_out, *sems)

        @pl.when(k == 0)
        def _():
            acc_ref[...] = jnp.zeros((tm, N), F32)

        for p in range(npair):
            @pl.when(jnp.logical_and(k >= first[p], k < first[p] + steps[p]))
            def _(p=p):
                acc_ref[...] += _dot(ab[2 * p][...], ab[2 * p + 1][...])

        @pl.when(k == total - 1)
        def _():
            o_ref[...] = acc_ref[...]

        if comm:
            @pl.when(step == grid[0] * total - 1)
            def _():
                comm["wait"](*c_in, *c_out, *sems)

    def local(p):
        return lambda k: jnp.clip(k - first[p], 0, steps[p] - 1)

    in_specs = []
    for p in range(npair):
        in_specs += [pl.BlockSpec((tm, tk), lambda i, k, f=local(p): (i, f(k))),
                     pl.BlockSpec((tk, N), lambda i, k, f=local(p): (f(k), 0))]
    out_spec = pl.BlockSpec((tm, N), lambda i, k: (i, 0))
    out_shape = jax.ShapeDtypeStruct((M, N), F32)
    args = [t for pair in pairs for t in pair]
    if not comm:
        return pl.pallas_call(
            body, name=name, grid=grid, in_specs=in_specs, out_specs=out_spec, out_shape=out_shape,
            scratch_shapes=[pltpu.VMEM((tm, N), F32)], compiler_params=_cp(("parallel", "arbitrary")))(*args)
    return pl.pallas_call(
        body, name=name, grid=grid, in_specs=in_specs + [ANY] * n_in,
        out_specs=[out_spec] + [ANY] * n_out, out_shape=[out_shape] + comm["out_shape"],
        scratch_shapes=[pltpu.VMEM((tm, N), F32)] + comm["sems"],
        compiler_params=_cp(("arbitrary", "arbitrary")))(*args, *comm["inputs"])


def _class_spec(S, d, tm, width):
    return pl.BlockSpec((d, tm // d, width), lambda i: (0, i, 0))


def _rms_fwd(x, g, *, name, dilations=(), comm=None):
    S, D = x.shape
    tm = min(512, S)
    ds = [d for d in dilations if d > 1]
    nsteps = S // tm
    n_in = len(comm["inputs"]) if comm else 0
    n_out = len(comm["out_shape"]) if comm else 0
    n_tmp = D // LANES if ds else 0

    def body(x_ref, g_ref, *rest):
        c_in, o_ref, rest = rest[:n_in], rest[n_in], rest[n_in + 1:]
        cls, c_out, rest = rest[:len(ds)], rest[len(ds):len(ds) + n_out], rest[len(ds) + n_out:]
        tmps, sems = rest[:n_tmp], rest[n_tmp:]
        if comm:
            @pl.when(pl.program_id(0) == 0)
            def _():
                comm["start"](*c_in, *c_out, *sems)

        xv = x_ref[...]
        r = lax.rsqrt(jnp.mean(xv * xv, axis=-1, keepdims=True) + EPS)
        hv = xv * r * g_ref[...]
        o_ref[...] = hv.astype(BF16)
        if ds:
            for c, tmp in enumerate(tmps):
                tmp[...] = hv[:, c * LANES:(c + 1) * LANES]
            for c_ref, d in zip(cls, ds):
                for k in range(d):
                    c_ref[k] = jnp.concatenate([tmp[pl.ds(k, tm // d, stride=d), :] for tmp in tmps],
                                               axis=1).astype(BF16)
        if comm:
            @pl.when(pl.program_id(0) == nsteps - 1)
            def _():
                comm["wait"](*c_in, *c_out, *sems)

    row = pl.BlockSpec((tm, D), lambda i: (i, 0))
    outs = pl.pallas_call(
        body, name=name, grid=(nsteps,),
        in_specs=[row, pl.BlockSpec((1, D), lambda i: (0, 0))] + [ANY] * n_in,
        out_specs=[row] + [_class_spec(S, d, tm, D) for d in ds] + [ANY] * n_out,
        out_shape=[jax.ShapeDtypeStruct((S, D), BF16)] + [jax.ShapeDtypeStruct((d, S // d, D), BF16) for d in ds]
        + (comm["out_shape"] if comm else []),
        scratch_shapes=[pltpu.VMEM((tm, LANES), F32)] * n_tmp + (comm["sems"] if comm else []),
        compiler_params=_cp(("arbitrary",) if comm else ("parallel",)),
    )(x, g, *(comm["inputs"] if comm else []))
    rows = [outs[0]] + [o.reshape(S, D) for o in outs[1:1 + len(ds)]]
    if comm:
        return rows, list(outs[1 + len(ds):])
    return rows if ds else rows[0]


def _rms_bwd(x, g, dh, dy, *, name, dh_classes=()):
    S, D = x.shape
    tm = min(512, S)
    want_dx = dy is not None
    nc = len(dh_classes)

    def body(*refs):
        c_refs, refs = refs[:nc], refs[nc:]
        if want_dx:
            x_ref, g_ref, dh_ref, dy_ref, dx_ref, dg_ref = refs[:6]
        else:
            x_ref, g_ref, dh_ref, dg_ref = refs[:4]
        i = pl.program_id(0)
        xv = x_ref[...]
        r = lax.rsqrt(jnp.mean(xv * xv, axis=-1, keepdims=True) + EPS)
        xh = xv * r
        if nc:
            tmps = refs[-(D // LANES):]
            cols = [slice(c * LANES, (c + 1) * LANES) for c in range(D // LANES)]
            for tmp, cs in zip(tmps, cols):
                tmp[...] = dh_ref[:, cs]
            for c_ref, (_, d) in zip(c_refs, dh_classes):
                for k in range(d):
                    for tmp, cs in zip(tmps, cols):
                        tmp[pl.ds(k, tm // d, stride=d), :] += c_ref[k, :, cs]
            dhv = jnp.concatenate([tmp[...] for tmp in tmps], axis=1)
        else:
            dhv = dh_ref[...]
        part = jnp.sum(dhv * xh, axis=0, keepdims=True)

        @pl.when(i == 0)
        def _():
            dg_ref[...] = part

        @pl.when(i > 0)
        def _():
            dg_ref[...] += part

        if want_dx:
            dxh = dhv * g_ref[...]
            dx_ref[...] = dy_ref[...] + r * (dxh - xh * jnp.mean(dxh * xh, axis=-1, keepdims=True))

    row = pl.BlockSpec((tm, D), lambda i: (i, 0))
    vec = pl.BlockSpec((1, D), lambda i: (0, 0))
    c_specs = [_class_spec(S, d, tm, D) for _, d in dh_classes]
    c_args = [a.reshape(d, S // d, D) for a, d in dh_classes]
    scratch = [pltpu.VMEM((tm, LANES), F32)] * (D // LANES) if nc else []
    if want_dx:
        return pl.pallas_call(
            body, name=name, grid=(S // tm,), in_specs=c_specs + [row, vec, row, row], out_specs=[row, vec],
            out_shape=[jax.ShapeDtypeStruct((S, D), F32), jax.ShapeDtypeStruct((1, D), F32)],
            scratch_shapes=scratch, compiler_params=_cp(("arbitrary",)))(*c_args, x, g, dh, dy)
    return pl.pallas_call(
        body, name=name, grid=(S // tm,), in_specs=c_specs + [row, vec, row], out_specs=vec,
        out_shape=jax.ShapeDtypeStruct((1, D), F32),
        scratch_shapes=scratch, compiler_params=_cp(("arbitrary",)))(*c_args, x, g, dh)


def _post(x, out, tgt, g, *, name):
    S, D = x.shape
    tm = min(512, S)

    def body(x_ref, o_ref, t_ref, g_ref, dy_ref, do_ref, dg_ref, loss_ref):
        i = pl.program_id(0)
        ov = o_ref[...]
        r = lax.rsqrt(jnp.mean(ov * ov, axis=-1, keepdims=True) + EPS)
        n = ov * r
        gv = g_ref[...]
        e = (x_ref[...] + n * gv) - t_ref[...]
        lpart = 0.5 * jnp.sum(jnp.mean(e * e, axis=-1, keepdims=True), axis=0, keepdims=True)
        dy = e * (1.0 / D)
        dy_ref[...] = dy
        dn = dy * gv
        do_ref[...] = (r * (dn - n * jnp.mean(dn * n, axis=-1, keepdims=True))).astype(BF16)
        gpart = jnp.sum(dy * n, axis=0, keepdims=True)
        lrow = jnp.broadcast_to(lpart, (1, LANES))

        @pl.when(i == 0)
        def _():
            dg_ref[...] = gpart
            loss_ref[...] = lrow

        @pl.when(i > 0)
        def _():
            dg_ref[...] += gpart
            loss_ref[...] += lrow

    row = pl.BlockSpec((tm, D), lambda i: (i, 0))
    vec = pl.BlockSpec((1, D), lambda i: (0, 0))
    return pl.pallas_call(
        body, name=name, grid=(S // tm,), in_specs=[row, row, row, vec],
        out_specs=[row, row, vec, pl.BlockSpec((1, LANES), lambda i: (0, 0))],
        out_shape=[jax.ShapeDtypeStruct((S, D), F32), jax.ShapeDtypeStruct((S, D), BF16),
                   jax.ShapeDtypeStruct((1, D), F32), jax.ShapeDtypeStruct((1, LANES), F32)],
        compiler_params=_cp(("arbitrary",)))(x, out, tgt, g)


def _to_classes(t, d):
    if d == 1:
        return t
    S, C = t.shape
    return t.reshape(S // d, d, C).transpose(1, 0, 2).reshape(S, C)


def _rope(x, c, s1, s2):
    return x * c + pltpu.roll(x, LANES - ROT // 2, 1) * s1 + pltpu.roll(x, ROT // 2, 1) * s2


def _unrope(d, c, s1, s2):
    return d * c + pltpu.roll(d * s1, ROT // 2, 1) + pltpu.roll(d * s2, LANES - ROT // 2, 1)


def _a_band(qb):
    r = lax.broadcasted_iota(jnp.int32, (qb, qb + HEAD), 0)
    c = lax.broadcasted_iota(jnp.int32, (qb, qb + HEAD), 1)
    return jnp.logical_and(c >= r, c <= r + HEAD)


def _a_first_ok(qb, n):
    c = lax.broadcasted_iota(jnp.int32, (qb, qb + HEAD), 1)
    return jnp.logical_or(c >= HEAD, n > 0)


def _a_last_ok(qb, has_next):
    c = lax.broadcasted_iota(jnp.int32, (qb, qb + HEAD), 1)
    return jnp.logical_or(c < qb, has_next)


A_SCALE = HEAD ** -0.5


def _a_geometry(S, g):
    d = DIL[g]
    L = S // d
    TQ = min(512, L)
    return d, L, TQ, TQ // HEAD, L // TQ, L // HEAD


def _proj_rope(h, w, tabs, *, name):
    S, D = h.shape
    tm = min(512, S)

    def body(h_ref, w_ref, c_ref, s1_ref, s2_ref, o_ref):
        tc = (c_ref[...], s1_ref[...], s2_ref[...])
        u = _dot_nt(h_ref[...], w_ref[...])
        for j in range(3 * A_HEADS):
            sl = slice(j * HEAD, (j + 1) * HEAD)
            o_ref[:, sl] = (_rope(u[:, sl], *tc) if j < 2 * A_HEADS else u[:, sl]).astype(BF16)

    tab = pl.BlockSpec((tm, LANES), lambda i: (i, 0))
    return pl.pallas_call(
        body, name=name, grid=(S // tm,),
        in_specs=[pl.BlockSpec((tm, D), lambda i: (i, 0)), pl.BlockSpec((3 * A_WIDTH, D), lambda i: (0, 0)),
                  tab, tab, tab],
        out_specs=pl.BlockSpec((tm, 3 * A_WIDTH), lambda i: (i, 0)),
        out_shape=jax.ShapeDtypeStruct((S, 3 * A_WIDTH), BF16),
        compiler_params=_cp(("parallel",)))(h, w, *tabs)


def _attn_a_fwd(qkv, g, *, name):
    S = qkv.shape[0]
    d, L, TQ, nsub, nb, nblk = _a_geometry(S, g)

    def body(q_ref, kc_ref, kp_ref, vc_ref, vp_ref, o_ref, l_ref):
        n = pl.program_id(1)
        QB = min(2 * HEAD, TQ)
        band = _a_band(QB)
        first = jnp.logical_and(band, _a_first_ok(QB, n))
        for h in range(A_HEADS):
            hs = slice(h * HEAD, (h + 1) * HEAD)
            for hh in range(TQ // QB):
                sl = slice(hh * QB, (hh + 1) * QB)
                pv = slice(hh * QB - HEAD, hh * QB)
                kcat = jnp.concatenate([kp_ref[:, hs] if hh == 0 else kc_ref[pv, hs], kc_ref[sl, hs]], axis=0)
                vcat = jnp.concatenate([vp_ref[:, hs] if hh == 0 else vc_ref[pv, hs], vc_ref[sl, hs]], axis=0)
                s = jnp.where(first if hh == 0 else band, _dot_nt(q_ref[sl, hs], kcat) * A_SCALE, NEG)
                m = jnp.max(s, axis=-1, keepdims=True)
                p = jnp.exp(s - m)
                den = jnp.sum(p, axis=-1, keepdims=True)
                o_ref[sl, hs] = _dot(p.astype(BF16), vcat) / den
                l_ref[sl, hs] = jnp.broadcast_to(m + jnp.log(den), (QB, HEAD))

    rcur = lambda r, n: r * nb + n
    rprv = lambda r, n: r * nblk + jnp.maximum(n * nsub - 1, 0)
    cur = lambda off: pl.BlockSpec((TQ, A_WIDTH), lambda r, n: (rcur(r, n), off))
    prv = lambda off: pl.BlockSpec((HEAD, A_WIDTH), lambda r, n: (rprv(r, n), off))
    out = pl.BlockSpec((TQ, A_WIDTH), lambda r, n: (rcur(r, n), 0))
    return pl.pallas_call(
        body, name=name, grid=(d, nb),
        in_specs=[cur(0), cur(1), prv(1), cur(2), prv(2)],
        out_specs=[out, out],
        out_shape=[jax.ShapeDtypeStruct((S, A_WIDTH), F32)] * 2,
        compiler_params=_cp(("parallel", "parallel")),
    )(qkv, qkv, qkv, qkv, qkv)


def _attn_a_dq(qkv, tabs, g, do, lse, adj, du, *, name):
    S = qkv.shape[0]
    d, L, TQ, nsub, nb, nblk = _a_geometry(S, g)

    def body(q_ref, kc_ref, kp_ref, vc_ref, vp_ref, do_ref, l_ref, adj_ref, c_ref, s1_ref, s2_ref, du_ref, dq_ref):
        n = pl.program_id(1)
        QB = min(2 * HEAD, TQ)
        band = _a_band(QB)
        first = jnp.logical_and(band, _a_first_ok(QB, n))
        for h in range(A_HEADS):
            hs = slice(h * HEAD, (h + 1) * HEAD)
            for hh in range(TQ // QB):
                sl = slice(hh * QB, (hh + 1) * QB)
                pv = slice(hh * QB - HEAD, hh * QB)
                kcat = jnp.concatenate([kp_ref[:, hs] if hh == 0 else kc_ref[pv, hs], kc_ref[sl, hs]], axis=0)
                vcat = jnp.concatenate([vp_ref[:, hs] if hh == 0 else vc_ref[pv, hs], vc_ref[sl, hs]], axis=0)
                s = jnp.where(first if hh == 0 else band, _dot_nt(q_ref[sl, hs], kcat) * A_SCALE, NEG)
                p = jnp.exp(s - l_ref[sl, hs][:, :1])
                ds = p * (_dot_nt(do_ref[sl, hs], vcat) + adj_ref[sl, hs][:, :1])
                dq = _dot(ds.astype(BF16), kcat) * A_SCALE
                dq_ref[sl, hs] = _unrope(dq, c_ref[sl, :], s1_ref[sl, :], s2_ref[sl, :]).astype(BF16)

    rcur = lambda r, n: r * nb + n
    rprv = lambda r, n: r * nblk + jnp.maximum(n * nsub - 1, 0)
    cur = lambda off: pl.BlockSpec((TQ, A_WIDTH), lambda r, n: (rcur(r, n), off))
    prv = lambda off: pl.BlockSpec((HEAD, A_WIDTH), lambda r, n: (rprv(r, n), off))
    tcur = pl.BlockSpec((TQ, LANES), lambda r, n: (rcur(r, n), 0))
    blk = cur(0)
    return pl.pallas_call(
        body, name=name, grid=(d, nb),
        in_specs=[cur(0), cur(1), prv(1), cur(2), prv(2), blk, blk, blk, tcur, tcur, tcur, ANY],
        out_specs=blk,
        out_shape=jax.ShapeDtypeStruct((S, 3 * A_WIDTH), BF16),
        input_output_aliases={11: 0},
        compiler_params=_cp(("parallel", "parallel")),
    )(qkv, qkv, qkv, qkv, qkv, do, lse, adj, *tabs, du)


def _attn_a_dkv(qkv, tabs, g, do, lse, adj, *, name):
    S = qkv.shape[0]
    d, L, TQ, nsub, nb, nblk = _a_geometry(S, g)

    def body(qc_ref, qn_ref, kc_ref, vc_ref, doc_ref, don_ref, lc_ref, ln_ref, ac_ref, an_ref,
             c_ref, s1_ref, s2_ref, du_ref):
        n = pl.program_id(1)
        QB = min(2 * HEAD, TQ)
        nh = TQ // QB
        band = _a_band(QB)
        end = jnp.logical_and(band, _a_last_ok(QB, n < nb - 1))
        for h in range(A_HEADS):
            hs = slice(h * HEAD, (h + 1) * HEAD)
            for kh in range(nh):
                sl = slice(kh * QB, (kh + 1) * QB)
                nx = slice((kh + 1) * QB, (kh + 1) * QB + HEAD)
                last = kh == nh - 1
                cat = lambda cur, nxt: jnp.concatenate([cur[sl, hs], nxt[:, hs] if last else cur[nx, hs]], axis=0)
                qcat = cat(qc_ref, qn_ref)
                docat = cat(doc_ref, don_ref)
                lt = cat(lc_ref, ln_ref).T[:1, :]
                at = cat(ac_ref, an_ref).T[:1, :]
                st = jnp.where(end if last else band, _dot_nt(kc_ref[sl, hs], qcat) * A_SCALE, NEG)
                pt = jnp.exp(st - lt)
                dv_cols = slice(2 * A_WIDTH + h * HEAD, 2 * A_WIDTH + (h + 1) * HEAD)
                dk_cols = slice(A_WIDTH + h * HEAD, A_WIDTH + (h + 1) * HEAD)
                du_ref[sl, dv_cols] = _dot(pt.astype(BF16), docat).astype(BF16)
                dst = pt * (_dot_nt(vc_ref[sl, hs], docat) + at)
                dk = _dot(dst.astype(BF16), qcat) * A_SCALE
                du_ref[sl, dk_cols] = _unrope(dk, c_ref[sl, :], s1_ref[sl, :], s2_ref[sl, :]).astype(BF16)

    rcur = lambda r, n: r * nb + n
    rnxt = lambda r, n: r * nblk + jnp.minimum((n + 1) * nsub, nblk - 1)
    cur = lambda off: pl.BlockSpec((TQ, A_WIDTH), lambda r, n: (rcur(r, n), off))
    nxu = lambda off: pl.BlockSpec((HEAD, A_WIDTH), lambda r, n: (rnxt(r, n), off))
    tcur = pl.BlockSpec((TQ, LANES), lambda r, n: (rcur(r, n), 0))
    blk, bnx = cur(0), nxu(0)
    return pl.pallas_call(
        body, name=name, grid=(d, nb),
        in_specs=[cur(0), nxu(0), cur(1), cur(2), blk, bnx, blk, bnx, blk, bnx, tcur, tcur, tcur],
        out_specs=pl.BlockSpec((TQ, 3 * A_WIDTH), lambda r, n: (rcur(r, n), 0)),
        out_shape=jax.ShapeDtypeStruct((S, 3 * A_WIDTH), BF16),
        compiler_params=_cp(("parallel", "parallel")),
    )(qkv, qkv, qkv, qkv, do, do, lse, lse, adj, adj, *tabs)


def _silu_parts(z):
    sg = _sigmoid(z)
    return z * sg, sg * (1.0 + z * (1.0 - sg))


def _classes_to_tokens(c_ref, d, tm, tmps):
    if d == 1:
        return c_ref[...].astype(F32)
    for k in range(d):
        for c, tmp in enumerate(tmps):
            tmp[pl.ds(k, tm // d, stride=d), :] = c_ref[k, :, c * LANES:(c + 1) * LANES].astype(F32)
    return jnp.concatenate([tmp[...] for tmp in tmps], axis=1)


def _tokens_to_classes(val, c_ref, d, tm, tmps):
    if d == 1:
        c_ref[...] = val.astype(c_ref.dtype)
        return
    for c, tmp in enumerate(tmps):
        tmp[...] = val[:, c * LANES:(c + 1) * LANES]
    for k in range(d):
        c_ref[k] = jnp.concatenate([tmp[pl.ds(k, tm // d, stride=d), :] for tmp in tmps], axis=1).astype(c_ref.dtype)


def _group_spec(S, d, tm):
    if d == 1:
        return pl.BlockSpec((tm, A_WIDTH), lambda i: (i, 0))
    return _class_spec(S, d, tm, A_WIDTH)


def _group_view(t, d):
    return t if d == 1 else t.reshape(d, t.shape[0] // d, t.shape[1])


def _merge_a_fwd(os_, ls_, ur, *, name):
    S = ur.shape[0]
    tm = min(512, S)

    def body(o0, o1, o2, l0, l1, l2, z_ref, y_ref, *tmps):
        ls = [_classes_to_tokens(r, d, tm, tmps) for r, d in zip((l0, l1, l2), DIL)]
        ov = [_classes_to_tokens(r, d, tm, tmps) for r, d in zip((o0, o1, o2), DIL)]
        mx = jnp.maximum(jnp.maximum(ls[0], ls[1]), ls[2])
        es = [jnp.exp(l - mx) for l in ls]
        den = es[0] + es[1] + es[2]
        y = (es[0] / den) * ov[0] + (es[1] / den) * ov[1] + (es[2] / den) * ov[2]
        y_ref[...] = (y * _silu_parts(z_ref[...])[0]).astype(BF16)

    blk = pl.BlockSpec((tm, A_WIDTH), lambda i: (i, 0))
    groups = [_group_spec(S, d, tm) for d in DIL]
    return pl.pallas_call(
        body, name=name, grid=(S // tm,),
        in_specs=groups + groups + [pl.BlockSpec((tm, A_WIDTH), lambda i: (i, R_ZA // A_WIDTH))],
        out_specs=blk, out_shape=jax.ShapeDtypeStruct((S, A_WIDTH), BF16),
        scratch_shapes=[pltpu.VMEM((tm, LANES), F32)] * (A_WIDTH // LANES),
        compiler_params=_cp(("parallel",)))(*[_group_view(t, d) for t, d in zip(os_, DIL)],
                                            *[_group_view(t, d) for t, d in zip(ls_, DIL)], ur)


def _merge_a_bwd(os_, ls_, ur, dya, du_r, *, name):
    S = ur.shape[0]
    tm = min(256, S)

    def body(o0, o1, o2, l0, l1, l2, z_ref, dy_ref, du_in, d0, d1, d2, a0, a1, a2, dz_ref, *tmps):
        ls = [_classes_to_tokens(r, d, tm, tmps) for r, d in zip((l0, l1, l2), DIL)]
        ov = [_classes_to_tokens(r, d, tm, tmps) for r, d in zip((o0, o1, o2), DIL)]
        mx = jnp.maximum(jnp.maximum(ls[0], ls[1]), ls[2])
        es = [jnp.exp(l - mx) for l in ls]
        den = es[0] + es[1] + es[2]
        ws = [e / den for e in es]
        y = ws[0] * ov[0] + ws[1] * ov[1] + ws[2] * ov[2]
        sz, dsz = _silu_parts(z_ref[...])
        dyv = dy_ref[...]
        dz_ref[...] = (dyv * y * dsz).astype(BF16)
        dyp = dyv * sz
        ts = []
        for h in range(A_HEADS):
            sl = slice(h * HEAD, (h + 1) * HEAD)
            t = jnp.zeros((tm, 1), F32)
            for gi in range(3):
                t = t + ws[gi][:, sl][:, :1] * jnp.sum(dyp[:, sl] * ov[gi][:, sl], axis=-1, keepdims=True)
            ts.append(jnp.broadcast_to(t, (tm, HEAD)))
        tb = jnp.concatenate(ts, axis=1)
        for gi, (dref, aref) in enumerate(((d0, a0), (d1, a1), (d2, a2))):
            _tokens_to_classes(ws[gi] * dyp, dref, DIL[gi], tm, tmps)
            _tokens_to_classes(-ws[gi] * tb, aref, DIL[gi], tm, tmps)

    blk = pl.BlockSpec((tm, A_WIDTH), lambda i: (i, 0))
    groups = [_group_spec(S, d, tm) for d in DIL]
    shaped = lambda dt: [jax.ShapeDtypeStruct((S, A_WIDTH) if d == 1 else (d, S // d, A_WIDTH), dt) for d in DIL]
    outs = pl.pallas_call(
        body, name=name, grid=(S // tm,),
        in_specs=groups + groups + [pl.BlockSpec((tm, A_WIDTH), lambda i: (i, R_ZA // A_WIDTH)), blk, ANY],
        out_specs=groups + groups + [pl.BlockSpec((tm, A_WIDTH), lambda i: (i, R_ZA // A_WIDTH))],
        out_shape=shaped(BF16) + shaped(F32) + [jax.ShapeDtypeStruct(du_r.shape, BF16)],
        input_output_aliases={8: 6},
        scratch_shapes=[pltpu.VMEM((tm, LANES), F32)] * (A_WIDTH // LANES),
        compiler_params=_cp(("parallel",)))(*[_group_view(t, d) for t, d in zip(os_, DIL)],
                                            *[_group_view(t, d) for t, d in zip(ls_, DIL)], ur, dya, du_r)
    flat = [t.reshape(S, A_WIDTH) for t in outs[:6]]
    return flat[0:3], flat[3:6], outs[6]


def _logf(ur, bf_pad, *, name):
    S = ur.shape[0]
    tm = min(1024, S)

    def body(u_ref, b_ref, o_ref):
        z = u_ref[...] + b_ref[...]
        o_ref[...] = jnp.minimum(z, 0.0) - jnp.log(1.0 + jnp.exp(-jnp.abs(z)))

    return pl.pallas_call(
        body, name=name, grid=(S // tm,),
        in_specs=[pl.BlockSpec((tm, FB_PAD), lambda i: (i, R_FB // FB_PAD)),
                  pl.BlockSpec((1, FB_PAD), lambda i: (0, 0))],
        out_specs=pl.BlockSpec((tm, FB_PAD), lambda i: (i, 0)),
        out_shape=jax.ShapeDtypeStruct((S, FB_PAD), F32),
        compiler_params=_cp(("parallel",)))(ur, bf_pad)


def _cumsum_lanes(x, reverse, *, name):
    nt, H, _ = x.shape
    R = nt * H

    def body(x_ref, o_ref):
        v = x_ref[...].reshape(R, LANES)
        lane = lax.broadcasted_iota(jnp.int32, (R, LANES), 1)
        row = lax.broadcasted_iota(jnp.int32, (R, LANES), 0)

        def scan(t, step, idx, n, axis):
            while step < n:
                if reverse:
                    t = t + jnp.where(idx < n - step, pltpu.roll(t, n - step, axis), 0.0)
                else:
                    t = t + jnp.where(idx >= step, pltpu.roll(t, step, axis), 0.0)
                step *= 2
            return t

        v = scan(v, 1, lane, LANES, 1)
        total = jnp.broadcast_to(v[:, :1] if reverse else v[:, LANES - 1:], (R, LANES))
        carry = scan(total, H, row, R, 0) - total
        o_ref[...] = (v + carry).reshape(nt, H, LANES)

    return pl.pallas_call(
        body, name=name, out_shape=jax.ShapeDtypeStruct((nt, H, LANES), F32),
        in_specs=[pl.BlockSpec(memory_space=pltpu.VMEM)], out_specs=pl.BlockSpec(memory_space=pltpu.VMEM),
        compiler_params=_cp())(x)


B_SCALE = B_HEAD ** -0.5


def _pair_masks():
    lane = lax.broadcasted_iota(jnp.int32, (1, LANES), 1)
    row = lax.broadcasted_iota(jnp.int32, (LANES, 1), 0)
    return (lane < B_HEAD, lane >= B_HEAD), (row < B_HEAD, row >= B_HEAD)


def _causal_t(T):
    r = lax.broadcasted_iota(jnp.int32, (T, T), 0)
    c = lax.broadcasted_iota(jnp.int32, (T, T), 1)
    return r <= c


def _zero_other(x, keep):
    return jnp.where(keep, x, jnp.zeros_like(x))


def _fox_aug(ub, c, *, name):
    S = ub.shape[0]
    T = min(2048, S)

    def body(q_ref, k_ref, c_ref, qa_ref, ka_ref):
        lane = lax.broadcasted_iota(jnp.int32, (1, LANES), 1)
        q = q_ref[...] * B_SCALE
        k = k_ref[...]
        for a in range(2):
            own = (lane < B_HEAD) if a == 0 else (lane >= B_HEAD)
            o = B_HEAD if a == 0 else 0
            cv = jnp.broadcast_to(c_ref[:, a:a + 1], (T, LANES))
            hi = cv.astype(BF16)
            r1 = cv - hi.astype(F32)
            mid = r1.astype(BF16)
            lo = (r1 - mid.astype(F32)).astype(BF16)
            pieces = (hi, mid, lo)
            one = jnp.ones((T, LANES), BF16)
            qa = jnp.where(own, q, jnp.zeros_like(q))
            ka = jnp.where(own, k, jnp.zeros_like(k))
            for t in range(3):
                qa = jnp.where(lane == o + t, pieces[t], qa)
                qa = jnp.where(lane == o + 3 + t, one, qa)
                ka = jnp.where(lane == o + t, one, ka)
                ka = jnp.where(lane == o + 3 + t, -pieces[t], ka)
            qa_ref[a] = qa
            ka_ref[a] = ka

    out = pl.BlockSpec((2, T, LANES), lambda h, i: (h, i, 0))
    c_pairs = c.reshape(B_HEADS // 2, 2, S).transpose(0, 2, 1)
    return pl.pallas_call(
        body, name=name, grid=(B_HEADS // 2, S // T),
        in_specs=[pl.BlockSpec((T, LANES), lambda h, i: (i, h)), pl.BlockSpec((T, LANES), lambda h, i: (i, 4 + h)),
                  pl.BlockSpec((None, T, 2), lambda h, i: (h, i, 0))],
        out_specs=[out, out], out_shape=[jax.ShapeDtypeStruct((B_HEADS, S, LANES), BF16)] * 2,
        compiler_params=_cp(("parallel", "parallel")))(ub, ub, c_pairs)


def _fox_fwd(qaug, kaug, vt, *, name):
    S = qaug.shape[1]
    T = min(512, S)
    nq = S // T

    def body(q_ref, k_ref, vt_ref, o_ref, l_ref, m_s, l_s, acc_s, st_s):
        i = pl.program_id(1)
        _, rows = _pair_masks()
        qm = [q_ref[0], q_ref[1]]
        m_s[...] = jnp.full((2, 1, T), NEG, F32)
        l_s[...] = jnp.zeros((2, 1, T), F32)
        acc_s[...] = jnp.zeros((LANES, T), F32)

        def logits(j):
            off = pl.multiple_of(j * T, T)
            return [_dot_nt(k_ref[a, pl.ds(off, T), :], qm[a]) for a in range(2)]

        def step(j, masked, prefetch):
            nxt = logits(j + 1) if prefetch else None
            vtj = vt_ref[j]
            upd = jnp.zeros((LANES, T), F32)
            alphas = []
            for a in range(2):
                st = st_s[a]
                if masked:
                    st = jnp.where(_causal_t(T), st, NEG)
                m_old = m_s[a]
                m_new = jnp.maximum(m_old, jnp.max(st, axis=0, keepdims=True))
                alpha = jnp.exp(m_old - m_new)
                pt = jnp.exp(st - m_new)
                l_s[a] = alpha * l_s[a] + jnp.sum(pt, axis=0, keepdims=True)
                m_s[a] = m_new
                upd = upd + _dot(_zero_other(vtj, rows[a]), pt.astype(BF16))
                alphas.append(alpha)
            acc_s[...] = acc_s[...] * jnp.where(rows[0], alphas[0], alphas[1]) + upd
            if prefetch:
                st_s[0] = nxt[0]
                st_s[1] = nxt[1]

        def loop(j, carry):
            step(j, False, True)
            return carry

        first = logits(0)
        st_s[0] = first[0]
        st_s[1] = first[1]
        lax.fori_loop(0, i, loop, 0)
        step(i, True, False)
        o_ref[...] = (acc_s[...] / jnp.where(rows[0], l_s[0], l_s[1])).T
        l_ref[0] = m_s[0] + jnp.log(l_s[0])
        l_ref[1] = m_s[1] + jnp.log(l_s[1])

    stat = pl.BlockSpec((2, None, 1, T), lambda h, i: (h, i, 0, 0))
    return pl.pallas_call(
        body, name=name, grid=(B_HEADS // 2, nq),
        in_specs=[pl.BlockSpec((2, T, LANES), lambda h, i: (h, i, 0)),
                  pl.BlockSpec((2, S, LANES), lambda h, i: (h, 0, 0)),
                  pl.BlockSpec((nq, LANES, T), lambda h, i: (0, h, 0))],
        out_specs=[pl.BlockSpec((T, LANES), lambda h, i: (i, h)), stat],
        out_shape=[jax.ShapeDtypeStruct((S, A_WIDTH), F32), jax.ShapeDtypeStruct((B_HEADS, nq, 1, T), F32)],
        scratch_shapes=[pltpu.VMEM((2, 1, T), F32), pltpu.VMEM((2, 1, T), F32), pltpu.VMEM((LANES, T), F32),
                        pltpu.VMEM((2, T, T), F32)],
        compiler_params=_cp(("parallel", "parallel")),
    )(qaug, kaug, vt)


def _fox_delta(o, do, *, name):
    S = o.shape[0]
    T = min(512, S)
    nq = S // T

    per = min(4, nq)

    def body(o_ref, do_ref, d_ref):
        _, rows = _pair_masks()
        for t in range(per):
            sl = slice(t * T, (t + 1) * T)
            prod_t = (do_ref[sl, :].astype(F32) * o_ref[sl, :]).T
            d_ref[0, t] = jnp.sum(_zero_other(prod_t, rows[0]), axis=0, keepdims=True)
            d_ref[1, t] = jnp.sum(_zero_other(prod_t, rows[1]), axis=0, keepdims=True)

    tile = pl.BlockSpec((per * T, LANES), lambda h, i: (i, h))
    return pl.pallas_call(
        body, name=name, grid=(B_HEADS // 2, nq // per), in_specs=[tile, tile],
        out_specs=pl.BlockSpec((2, per, 1, T), lambda h, i: (h, i, 0, 0)),
        out_shape=jax.ShapeDtypeStruct((B_HEADS, nq, 1, T), F32),
        compiler_params=_cp(("parallel", "parallel")))(o, do)


def _fox_bwd(ub, qaug, kaug, kt, do, lse, delta, *, name):
    S = ub.shape[0]
    T = min(512, S)
    nq = S // T

    def body(k_ref, v_ref, kt_ref, q_ref, do_ref, l_ref, dl_ref,
             dk_ref, dv_ref, dck_ref, dqt_ref, dcq_ref, dk_s, dv_s, dc_s):
        j = pl.program_id(1)
        lanes, rows = _pair_masks()
        vv = v_ref[...]
        ktj = kt_ref[...]
        km = [k_ref[0], k_ref[1]]
        ktm = [_zero_other(ktj, rows[0]), _zero_other(ktj, rows[1])]
        dk_s[...] = jnp.zeros((2, T, LANES), F32)
        dv_s[...] = jnp.zeros((T, LANES), F32)
        dc_s[...] = jnp.zeros((2, T, 1), F32)

        @pl.when(j == 0)
        def _():
            dqt_ref[...] = jnp.zeros((nq, LANES, T), F32)
            dcq_ref[...] = jnp.zeros((2, nq, 1, T), F32)

        def step(i, masked):
            off = pl.multiple_of(i * T, T)
            doi = do_ref[pl.ds(off, T), :]
            upd = jnp.zeros((LANES, T), F32)
            for a in range(2):
                qi = q_ref[a, pl.ds(off, T), :]
                st = _dot_nt(km[a], qi)
                if masked:
                    st = jnp.where(_causal_t(T), st, NEG)
                pt = jnp.exp(st - l_ref[a, i])
                doa = _zero_other(doi, lanes[a])
                dv_s[...] += _dot(pt.astype(BF16), doa)
                dst = pt * (_dot_nt(vv, doa) - dl_ref[a, i])
                dsb = dst.astype(BF16)
                dk_s[a] += _dot(dsb, qi)
                upd = upd + _dot(ktm[a], dsb)
                dc_s[a] -= jnp.sum(dst, axis=-1, keepdims=True)
                dcq_ref[a, i] += jnp.sum(dst, axis=0, keepdims=True)
            dqt_ref[i] += upd

        def loop(i, carry):
            step(i, False)
            return carry

        step(j, True)
        lax.fori_loop(j + 1, nq, loop, 0)
        dk_ref[...] = jnp.where(lanes[0], dk_s[0], dk_s[1]).astype(BF16)
        dv_ref[...] = dv_s[...].astype(BF16)
        dck_ref[...] = dc_s[...]

    rowv = pl.BlockSpec((2, nq, 1, T), lambda h, j: (h, 0, 0, 0))
    tile = pl.BlockSpec((T, LANES), lambda h, j: (j, h))
    return pl.pallas_call(
        body, name=name, grid=(B_HEADS // 2, nq),
        in_specs=[pl.BlockSpec((2, T, LANES), lambda h, j: (h, j, 0)),
                  pl.BlockSpec((T, LANES), lambda h, j: (j, 8 + h)),
                  pl.BlockSpec((None, LANES, T), lambda h, j: (j, h, 0)),
                  pl.BlockSpec((2, S, LANES), lambda h, j: (h, 0, 0)),
                  pl.BlockSpec((S, LANES), lambda h, j: (0, h)),
                  rowv, rowv],
        out_specs=[tile, tile, pl.BlockSpec((2, T, 1), lambda h, j: (h, j, 0)),
                   pl.BlockSpec((nq, LANES, T), lambda h, j: (0, h, 0)), rowv],
        out_shape=[jax.ShapeDtypeStruct((S, A_WIDTH), BF16)] * 2 + [jax.ShapeDtypeStruct((B_HEADS, S, 1), F32),
                   jax.ShapeDtypeStruct((nq, A_WIDTH, T), F32), jax.ShapeDtypeStruct((B_HEADS, nq, 1, T), F32)],
        scratch_shapes=[pltpu.VMEM((2, T, LANES), F32), pltpu.VMEM((T, LANES), F32), pltpu.VMEM((2, T, 1), F32)],
        compiler_params=_cp(("parallel", "arbitrary")),
    )(kaug, ub, kt, qaug, do, lse, delta)


def _gate_fwd(o, ur, zcol, *, name):
    S = ur.shape[0]
    tm = min(1024, S)

    def body(o_ref, z_ref, y_ref):
        y_ref[...] = (o_ref[...] * _silu_parts(z_ref[...])[0]).astype(BF16)

    blk = pl.BlockSpec((tm, A_WIDTH), lambda i: (i, 0))
    return pl.pallas_call(
        body, name=name, grid=(S // tm,),
        in_specs=[blk, pl.BlockSpec((tm, A_WIDTH), lambda i: (i, zcol // A_WIDTH))],
        out_specs=blk, out_shape=jax.ShapeDtypeStruct((S, A_WIDTH), BF16),
        compiler_params=_cp(("parallel",)))(o, ur)


def _gate_bwd(o, ur, zcol, dy, du_r, *, name):
    S = ur.shape[0]
    tm = min(1024, S)

    def body(o_ref, z_ref, dy_ref, du_in, do_ref, dz_ref):
        sz, dsz = _silu_parts(z_ref[...])
        dyv = dy_ref[...]
        do_ref[...] = (dyv * sz).astype(BF16)
        dz_ref[...] = (dyv * o_ref[...] * dsz).astype(BF16)

    blk = pl.BlockSpec((tm, A_WIDTH), lambda i: (i, 0))
    gate = pl.BlockSpec((tm, A_WIDTH), lambda i: (i, zcol // A_WIDTH))
    return pl.pallas_call(
        body, name=name, grid=(S // tm,),
        in_specs=[blk, gate, blk, ANY],
        out_specs=[blk, gate],
        out_shape=[jax.ShapeDtypeStruct((S, A_WIDTH), BF16), jax.ShapeDtypeStruct(du_r.shape, BF16)],
        input_output_aliases={3: 1},
        compiler_params=_cp(("parallel",)))(o, ur, dy, du_r)


def _dfb(ur, bf_pad, dlogf_pad, du_r, *, name):
    S = ur.shape[0]
    tm = min(1024, S)

    def body(u_ref, b_ref, d_ref, du_in, o_ref, s_ref):
        i = pl.program_id(0)
        dv = d_ref[...] * _sigmoid(-(u_ref[...] + b_ref[...]))
        o_ref[...] = dv.astype(BF16)
        part = jnp.sum(dv, axis=0, keepdims=True)

        @pl.when(i == 0)
        def _():
            s_ref[...] = part

        @pl.when(i > 0)
        def _():
            s_ref[...] += part

    vec = pl.BlockSpec((1, FB_PAD), lambda i: (0, 0))
    blk = pl.BlockSpec((tm, FB_PAD), lambda i: (i, 0))
    fb = pl.BlockSpec((tm, FB_PAD), lambda i: (i, R_FB // FB_PAD))
    return pl.pallas_call(
        body, name=name, grid=(S // tm,),
        in_specs=[fb, vec, blk, ANY],
        out_specs=[fb, vec],
        out_shape=[jax.ShapeDtypeStruct(du_r.shape, BF16), jax.ShapeDtypeStruct((1, FB_PAD), F32)],
        input_output_aliases={3: 0},
        compiler_params=_cp(("arbitrary",)))(ur, bf_pad, dlogf_pad, du_r)


M_SCALE = HEAD ** -0.5


def _mem_fwd(ur, mkv, *, name):
    S = ur.shape[0]
    T = min(512, S)

    def body(q_ref, z_ref, k_ref, v_ref, y_ref):
        for h in range(M_HEADS):
            hs = slice(h * HEAD, (h + 1) * HEAD)
            s = _dot_nt(q_ref[:, hs].astype(BF16), k_ref[:, hs].astype(BF16)) * M_SCALE
            p = jnp.exp(s - jnp.max(s, axis=-1, keepdims=True))
            p = p / jnp.sum(p, axis=-1, keepdims=True)
            o = _dot(p.astype(BF16), v_ref[:, hs].astype(BF16))
            y_ref[:, hs] = (o * _silu_parts(z_ref[:, hs])[0]).astype(BF16)

    wide = lambda col: pl.BlockSpec((T, A_WIDTH), lambda i: (i, col // A_WIDTH))
    kv = lambda half: pl.BlockSpec((N_MEM, A_WIDTH), lambda i: (0, half))
    return pl.pallas_call(
        body, name=name, grid=(S // T,),
        in_specs=[wide(R_QM), wide(R_ZM), kv(0), kv(1)],
        out_specs=pl.BlockSpec((T, A_WIDTH), lambda i: (i, 0)),
        out_shape=jax.ShapeDtypeStruct((S, A_WIDTH), BF16),
        compiler_params=_cp(("parallel",)))(ur, ur, mkv, mkv)


def _mem_bwd(ur, mkv, dy, du_r, *, name):
    S = ur.shape[0]
    T = min(512, S)

    def body(q_ref, z_ref, k_ref, v_ref, dy_ref, du_in, du_ref, dk_ref, dv_ref):
        i = pl.program_id(0)

        @pl.when(i == 0)
        def _():
            dk_ref[...] = jnp.zeros((N_MEM, A_WIDTH), F32)
            dv_ref[...] = jnp.zeros((N_MEM, A_WIDTH), F32)

        for h in range(M_HEADS):
            hs = slice(h * HEAD, (h + 1) * HEAD)
            qv = q_ref[:, hs].astype(BF16)
            kv = k_ref[:, hs].astype(BF16)
            vv = v_ref[:, hs].astype(BF16)
            s = _dot_nt(qv, kv) * M_SCALE
            p = jnp.exp(s - jnp.max(s, axis=-1, keepdims=True))
            p = p / jnp.sum(p, axis=-1, keepdims=True)
            o = _dot(p.astype(BF16), vv)
            sz, dsz = _silu_parts(z_ref[:, hs])
            dyv = dy_ref[:, hs]
            du_ref[:, A_WIDTH + h * HEAD:A_WIDTH + (h + 1) * HEAD] = (dyv * o * dsz).astype(BF16)
            dov = (dyv * sz).astype(BF16)
            dp = _dot_nt(dov, vv)
            ds = p * (dp - jnp.sum(p * dp, axis=-1, keepdims=True))
            du_ref[:, hs] = (_dot(ds.astype(BF16), kv) * M_SCALE).astype(BF16)
            dv_ref[:, hs] += _dot(p.T.astype(BF16), dov)
            dk_ref[:, hs] += _dot(ds.T.astype(BF16), qv) * M_SCALE

    wide = lambda col: pl.BlockSpec((T, A_WIDTH), lambda i: (i, col // A_WIDTH))
    kv = lambda half: pl.BlockSpec((N_MEM, A_WIDTH), lambda i: (0, half))
    tile = pl.BlockSpec((T, A_WIDTH), lambda i: (i, 0))
    acc = pl.BlockSpec((N_MEM, A_WIDTH), lambda i: (0, 0))
    assert R_ZM == R_QM + A_WIDTH and R_QM % (2 * A_WIDTH) == 0
    return pl.pallas_call(
        body, name=name, grid=(S // T,),
        in_specs=[wide(R_QM), wide(R_ZM), kv(0), kv(1), tile, ANY],
        out_specs=[pl.BlockSpec((T, 2 * A_WIDTH), lambda i: (i, R_QM // (2 * A_WIDTH))), acc, acc],
        out_shape=[jax.ShapeDtypeStruct(du_r.shape, BF16)] + [jax.ShapeDtypeStruct((N_MEM, A_WIDTH), F32)] * 2,
        input_output_aliases={5: 0},
        compiler_params=_cp(("arbitrary",)))(ur, ur, mkv, mkv, dy, du_r)


def _branch_fwd(ys, wbs, ur, b_merge, *, name):
    S = ur.shape[0]
    tm, tn = min(512, S), 512
    nj = D_MODEL // tn

    def body(ya, yb, ym, wa, wb, wm, g0, g1, g2, b0, b1, b2, mg_ref, p_ref):
        acc = jnp.zeros((tm, tn), F32)
        for i, (y, w, gr, br) in enumerate(((ya, wa, g0, b0), (yb, wb, g1, b1), (ym, wm, g2, b2))):
            pr = _dot(y[...], w[...])
            p_ref[i] = pr.astype(BF16)
            acc = acc + _sigmoid(gr[...] + br[...]) * pr
        mg_ref[...] = acc.astype(BF16)

    yspec = pl.BlockSpec((tm, A_WIDTH), lambda i, j: (i, 0))
    wspec = pl.BlockSpec((A_WIDTH, tn), lambda i, j: (0, j))
    gspec = lambda b: pl.BlockSpec((tm, tn), lambda i, j: (i, (R_GL + b * D_MODEL) // tn + j))
    bspec = lambda b: pl.BlockSpec((1, tn), lambda i, j: (0, b * nj + j))
    return pl.pallas_call(
        body, name=name, grid=(S // tm, nj),
        in_specs=[yspec] * 3 + [wspec] * 3 + [gspec(0), gspec(1), gspec(2), bspec(0), bspec(1), bspec(2)],
        out_specs=[pl.BlockSpec((tm, tn), lambda i, j: (i, j)),
                   pl.BlockSpec((3, tm, tn), lambda i, j: (0, i, j))],
        out_shape=[jax.ShapeDtypeStruct((S, D_MODEL), BF16), jax.ShapeDtypeStruct((3, S, D_MODEL), BF16)],
        compiler_params=_cp(("parallel", "parallel")))(*ys, *wbs, ur, ur, ur, b_merge, b_merge, b_merge)


def _branch_bwd(dm, prods, ur, b_merge, *, name):
    S = ur.shape[0]
    tm = min(256, S)

    def body(dm_ref, p_ref, g0, g1, g2, b_ref, dp0, dp1, dp2, dgl_ref, db_ref):
        i = pl.program_id(0)
        dmv = dm_ref[...]
        parts = []
        for b, (gr, dp_ref) in enumerate(((g0, dp0), (g1, dp1), (g2, dp2))):
            sl = slice(b * D_MODEL, (b + 1) * D_MODEL)
            gt = _sigmoid(gr[...] + b_ref[:, sl])
            dp_ref[...] = (dmv * gt).astype(BF16)
            dgl = dmv * p_ref[b].astype(F32) * gt * (1.0 - gt)
            dgl_ref[:, R_GL + b * D_MODEL:R_GL + (b + 1) * D_MODEL] = dgl.astype(BF16)
            parts.append(jnp.sum(dgl, axis=0, keepdims=True))
        part = jnp.concatenate(parts, axis=1)

        @pl.when(i == 0)
        def _():
            db_ref[...] = part

        @pl.when(i > 0)
        def _():
            db_ref[...] += part

    gspec = lambda b: pl.BlockSpec((tm, D_MODEL), lambda i: (i, R_GL // D_MODEL + b))
    vec = pl.BlockSpec((1, 3 * D_MODEL), lambda i: (0, 0))
    row = pl.BlockSpec((tm, D_MODEL), lambda i: (i, 0))
    outs = pl.pallas_call(
        body, name=name, grid=(S // tm,),
        in_specs=[row, pl.BlockSpec((3, tm, D_MODEL), lambda i: (0, i, 0)), gspec(0), gspec(1), gspec(2), vec],
        out_specs=[row, row, row, pl.BlockSpec((tm, NR), lambda i: (i, 0)), vec],
        out_shape=[jax.ShapeDtypeStruct((S, D_MODEL), BF16)] * 3
        + [jax.ShapeDtypeStruct((S, NR), BF16), jax.ShapeDtypeStruct((1, 3 * D_MODEL), F32)],
        compiler_params=_cp(("arbitrary",)))(dm, prods, ur, ur, ur, b_merge)
    return outs[0:3], outs[3], outs[4]


def _rope_tables(pos):
    half = ROT // 2
    S = pos.shape[0]
    inv = ROPE_THETA ** (-jnp.arange(half, dtype=F32) / half)
    per_row = LANES // half
    ang = jnp.repeat(pos.astype(F32).reshape(S // per_row, per_row), half, axis=1) * jnp.tile(inv, per_row)
    cos, sin = lax.optimization_barrier((jnp.cos(ang).reshape(S, half), jnp.sin(ang).reshape(S, half)))
    one = jnp.ones((S, LANES - ROT), F32)
    zero = jnp.zeros((S, LANES - ROT), F32)
    zh = jnp.zeros((S, half), F32)
    c = jnp.concatenate([cos, cos, one], axis=1)
    s1 = jnp.concatenate([-sin, zh, zero], axis=1)
    s2 = jnp.concatenate([zh, sin, zero], axis=1)
    return c, s1, s2


def _to_tiles(t):
    S, H = t.shape
    return t.reshape(S // LANES, LANES, H).transpose(0, 2, 1)


def _from_tiles(t):
    nt, H, _ = t.shape
    return t.transpose(1, 0, 2).reshape(H, nt * LANES)


def _local_step(x, mem, pos, tgt, g_pre, g_post, g_mem, wt, bf_pad, b_merge, w_kv, wbs, w_out, pack=None, hs=None):
    S = x.shape[0]
    T = min(512, S)
    nq = S // T
    tabs = _rope_tables(pos)

    if hs is None:
        hs = _rms_fwd(x, g_pre, name="rms_pre", dilations=DIL)
    h = hs[0]
    tabs_g = [[_to_classes(t, d) for t in tabs] for d in DIL]
    qkvs = [_proj_rope(hs[g], wt[f"A{g}"], tabs_g[g], name=f"proj_a{g}") for g in range(3)]
    ub = _mm(h, wt["B"], bt=True, out_dtype=BF16, name="proj_b", tn=1536)
    ur = _mm(h, wt["R"], bt=True, name="proj_r", tn=1792)

    outs_c, lses_c = [], []
    for g in range(3):
        o, l = _attn_a_fwd(qkvs[g], g, name=f"attn_a_fwd{g}")
        outs_c.append(o)
        lses_c.append(l)
    ya = _merge_a_fwd(outs_c, lses_c, ur, name="merge_a_fwd")

    logf = _logf(ur, bf_pad, name="logf")
    c = _from_tiles(_cumsum_lanes(_to_tiles(logf[:, :B_HEADS]), False, name="cumsum_fwd"))
    qaug, kaug = _fox_aug(ub, c, name="fox_aug")
    kt = ub[:, 512:1024].reshape(nq, T, 512).transpose(0, 2, 1)
    vt = ub[:, 1024:1536].reshape(nq, T, 512).transpose(0, 2, 1)
    ob, lse_b = _fox_fwd(qaug, kaug, vt, name="fox_fwd")
    yb = _gate_fwd(ob, ur, R_ZB, name="gate_b_fwd")

    hm = _rms_fwd(mem, g_mem, name="rms_mem")
    mkv = _mm(hm, w_kv, name="proj_mem")
    ym = _mem_fwd(ur, mkv, name="mem_fwd")

    merged, prods = _branch_fwd((ya, yb, ym), wbs, ur, b_merge, name="branch_fwd")
    out = _mm(merged, w_out, name="proj_out")
    dy, d_out, dg_post, loss_row = _post(x, out, tgt, g_post, name="post")

    dmerged = _mm(d_out, w_out, bt=True, name="d_merged")
    dw_out = _mm(merged, d_out, at=True, name="dw_out", tk=2048)
    dprods, du_r, db_merge = _branch_bwd(dmerged, prods, ur, b_merge, name="branch_bwd")
    dys, dwbs = [], []
    for i, (y, wb) in enumerate(zip((ya, yb, ym), wbs)):
        dys.append(_mm(dprods[i], wb, bt=True, name=f"d_y{i}"))
        dwbs.append(_mm(y, dprods[i], at=True, name=f"dw_branch{i}", tk=2048))

    dos_c, adjs_c, du_r = _merge_a_bwd(outs_c, lses_c, ur, dys[0], du_r, name="merge_a_bwd")
    dus_a = []
    for g, d in enumerate(DIL):
        do_c, adj_c = dos_c[g], adjs_c[g]
        du = _attn_a_dkv(qkvs[g], tabs_g[g], g, do_c, lses_c[g], adj_c, name=f"attn_a_dkv{g}")
        dus_a.append(_attn_a_dq(qkvs[g], tabs_g[g], g, do_c, lses_c[g], adj_c, du, name=f"attn_a_dq{g}"))

    dob, du_r = _gate_bwd(ob, ur, R_ZB, dys[1], du_r, name="gate_b_bwd")
    delta_b = _fox_delta(ob, dob, name="fox_delta")
    dkb, dvb, dc_k, dqt, dc_q = _fox_bwd(ub, qaug, kaug, kt, dob, lse_b, delta_b, name="fox_bwd")
    dqb = (dqt.transpose(0, 2, 1).reshape(S, A_WIDTH) * B_SCALE).astype(BF16)
    du_b = jnp.concatenate([dqb, dkb, dvb], axis=1)
    dc = dc_q.reshape(B_HEADS, S) + dc_k.reshape(B_HEADS, S)
    dlogf = _from_tiles(_cumsum_lanes(_to_tiles(dc.T), True, name="cumsum_bwd"))
    dlogf_pad = jnp.pad(dlogf.T, ((0, 0), (0, FB_PAD - B_HEADS)))
    du_r, db_forget = _dfb(ur, bf_pad, dlogf_pad, du_r, name="dfb")

    du_r, dmk, dmv = _mem_bwd(ur, mkv, dys[2], du_r, name="mem_bwd")
    dmkv = jnp.concatenate([dmk, dmv], axis=1).astype(BF16)
    dhm = _mm(dmkv, w_kv, bt=True, name="d_hm")
    dw_kv = _mm(hm, dmkv, at=True, name="dw_kv")
    dg_mem = _rms_bwd(mem, g_mem, dhm, None, name="rms_mem_bwd")

    dwt ={"R": _mm(du_r, h, at=True, name="dw_in_r", tm=1792, tk=1024),
           "B": _mm(du_b, h, at=True, name="dw_in_b", tm=1536, tk=2048)}
    for g in range(3):
        dwt[f"A{g}"] = _mm(dus_a[g], hs[g], at=True, name=f"dw_in_a{g}", tm=1536, tk=2048)
    res = dict(dwt=dwt, dw_kv=dw_kv, dwbs=dwbs, dw_out=dw_out)
    token_major = [(du_r, wt["R"]), (du_b, wt["B"]), (dus_a[0], wt["A0"])]
    if pack is None:
        dh_1 = _mm(dus_a[1], wt["A1"], name="d_h_a1", tk=1536)
        dh_2 = _mm(dus_a[2], wt["A2"], name="d_h_a2", tk=1536)
        dh = _mm_sum(token_major, name="d_h_main")
    else:
        gbig = pack(dwt, dw_kv, dwbs, dw_out)
        own_idx = _own_slabs()
        dh_1, sib = _mm(dus_a[1], wt["A1"], name="d_h_a1", tk=1536, comm=_pair_comm(gbig, (0, 1)))
        dh_2, sib = _mm(dus_a[2], wt["A2"], name="d_h_a2", tk=1536, comm=_pair_comm(gbig, (2, 3), sib))
        send = _pair_sum(gbig, sib, own_idx, 208, name="pair_sum")
        dh, recv = _mm_sum(token_major, name="d_h_main", comm=_chips_comm(send))
        res = dict(parts=[(gbig, None), (sib, 1), (recv, N_CHIP - 1)], own_idx=own_idx)
    grad_x, dg_pre = _rms_bwd(x, g_pre, dh, dy, name="rms_pre_bwd", dh_classes=[(dh_1, DIL[1]), (dh_2, DIL[2])])

    return dict(res, loss=loss_row, grad_x=grad_x, dg_pre=dg_pre, dg_post=dg_post, dg_mem=dg_mem,
                db_forget=db_forget, db_merge=db_merge)


MESH = pl.DeviceIdType.MESH
ANY = pl.BlockSpec(memory_space=pl.ANY)


def _relations():
    return [(k >> 2 & 1, k >> 1 & 1, k & 1) for k in range(1, N_DEV)]


def _coords():
    return lax.axis_index("x"), lax.axis_index("y"), lax.axis_index("c")


def _gather_comm(shard):
    R, W = shard.shape

    def plan(x_ref, out_ref, send_sems, recv_sems, local_sem):
        x, y, c = _coords()
        me, sibling = (x, y, c), (x, y, 1 - c)
        chips = [(1 - x, y), (x, 1 - y), (1 - x, 1 - y)]

        def slot(px, py, pc):
            return out_ref.at[4 * px + 2 * py + pc]

        def copy(k, block, to, src=None):
            return pltpu.make_async_remote_copy(
                src_ref=slot(*block) if src is None else src, dst_ref=slot(*block),
                send_sem=send_sems.at[k], recv_sem=recv_sems.at[k], device_id=to, device_id_type=MESH)

        mine = pltpu.make_async_copy(x_ref, slot(*me), local_sem)
        first = [copy(0, me, sibling, src=x_ref)]
        first += [copy(1 + j, me, (*chip, c), src=x_ref) for j, chip in enumerate(chips)]
        return me, sibling, chips, c, copy, mine, first

    def start(*refs):
        _, _, _, _, _, mine, first = plan(*refs)
        mine.start()
        for cp in first:
            cp.start()

    def wait(*refs):
        me, sibling, chips, c, copy, mine, first = plan(*refs)
        passed = [copy(4 + j, (*chip, c), sibling) for j, chip in enumerate(chips)]
        for j, chip in enumerate(chips):
            copy(1 + j, (*chip, c), me).wait_recv()
            passed[j].start()
        copy(0, sibling, me).wait_recv()
        for j, chip in enumerate(chips):
            copy(4 + j, (*chip, 1 - c), me).wait_recv()
        for cp in first + passed:
            cp.wait_send()
        mine.wait()

    return dict(inputs=[shard], out_shape=[jax.ShapeDtypeStruct((N_DEV, R, W), shard.dtype)],
                sems=[pltpu.SemaphoreType.DMA((N_DEV - 1,)), pltpu.SemaphoreType.DMA((N_DEV - 1,)),
                      pltpu.SemaphoreType.DMA],
                start=start, wait=wait)


N_CHIP = 4


def _pair_comm(gbig, rels, sib=None):
    _, R, W = gbig.shape

    def copies(g_ref, *rest):
        sib_ref, send_sems, recv_sems = rest[-3:]
        x, y, c = _coords()
        return [pltpu.make_async_remote_copy(
            src_ref=g_ref.at[4 * (x ^ (r >> 1)) + 2 * (y ^ (r & 1)) + (1 - c)], dst_ref=sib_ref.at[r],
            send_sem=send_sems.at[k], recv_sem=recv_sems.at[k], device_id=(x, y, 1 - c), device_id_type=MESH)
            for k, r in enumerate(rels)]

    def start(*refs):
        for cp in copies(*refs):
            cp.start()

    def wait(*refs):
        cps = copies(*refs)
        for cp in cps:
            cp.wait_recv()
        for cp in cps:
            cp.wait_send()

    return dict(inputs=[gbig] if sib is None else [gbig, sib],
                out_shape=[jax.ShapeDtypeStruct((N_CHIP, R, W), gbig.dtype)],
                alias={} if sib is None else {1: 0},
                sems=[pltpu.SemaphoreType.DMA((len(rels),)), pltpu.SemaphoreType.DMA((len(rels),))],
                start=start, wait=wait)


def _own_slabs():
    x, y, c = _coords()
    return jnp.stack([4 * (x ^ (r >> 1)) + 2 * (y ^ (r & 1)) + c for r in range(N_CHIP)]).astype(jnp.int32)


def _pair_sum(gbig, sib, own_idx, tr, *, name):
    _, R, W = gbig.shape

    def body(idx_ref, a_ref, b_ref, o_ref):
        o_ref[...] = (a_ref[...] + b_ref[...]).astype(BF16)

    return pl.pallas_call(
        body, name=name,
        grid_spec=pltpu.PrefetchScalarGridSpec(
            num_scalar_prefetch=1, grid=(N_CHIP - 1, R // tr),
            in_specs=[pl.BlockSpec((None, tr, W), lambda r, i, idx: (idx[r + 1], i, 0)),
                      pl.BlockSpec((None, tr, W), lambda r, i, idx: (r + 1, i, 0))],
            out_specs=pl.BlockSpec((None, tr, W), lambda r, i, idx: (r, i, 0))),
        out_shape=jax.ShapeDtypeStruct((N_CHIP - 1, R, W), BF16),
        compiler_params=_cp(("parallel", "parallel")))(own_idx, gbig, sib)


def _chips_comm(send):
    nb, R, W = send.shape

    def copies(b_ref, rb_ref, send_sems, recv_sems):
        x, y, c = _coords()
        return [pltpu.make_async_remote_copy(
            src_ref=b_ref.at[r - 1], dst_ref=rb_ref.at[r - 1], send_sem=send_sems.at[r - 1],
            recv_sem=recv_sems.at[r - 1], device_id=(x ^ (r >> 1), y ^ (r & 1), c), device_id_type=MESH)
            for r in range(1, N_CHIP)]

    def start(*refs):
        for cp in copies(*refs):
            cp.start()

    def wait(*refs):
        cps = copies(*refs)
        for cp in cps:
            cp.wait_recv()
        for cp in cps:
            cp.wait_send()

    return dict(inputs=[send], out_shape=[jax.ShapeDtypeStruct((nb, R, W), send.dtype)],
                sems=[pltpu.SemaphoreType.DMA((nb,)), pltpu.SemaphoreType.DMA((nb,))],
                start=start, wait=wait)


def _gather_small(gsmall, *, name):
    n = N_DEV - 1

    def body(s_ref, rs_ref, send_sems, recv_sems, local_sem):
        x, y, c = _coords()
        me = 4 * x + 2 * y + c
        mine = pltpu.make_async_copy(s_ref, rs_ref.at[me], local_sem)
        mine.start()

        def copy(k, fx, fy, fc, slot):
            return pltpu.make_async_remote_copy(
                src_ref=s_ref, dst_ref=rs_ref.at[slot], send_sem=send_sems.at[k], recv_sem=recv_sems.at[k],
                device_id=(x ^ fx, y ^ fy, c ^ fc), device_id_type=MESH)

        started = [copy(k, *rel, me) for k, rel in enumerate(_relations())]
        for cp in started:
            cp.start()
        for k, (fx, fy, fc) in enumerate(_relations()):
            copy(k, fx, fy, fc, 4 * (x ^ fx) + 2 * (y ^ fy) + (c ^ fc)).wait_recv()
        for cp in started:
            cp.wait_send()
        mine.wait()

    return pl.pallas_call(
        body, name=name, out_shape=jax.ShapeDtypeStruct((N_DEV, 1, P_SMALL), gsmall.dtype),
        in_specs=[ANY], out_specs=ANY,
        scratch_shapes=[pltpu.SemaphoreType.DMA((n,)), pltpu.SemaphoreType.DMA((n,)), pltpu.SemaphoreType.DMA],
    )(gsmall)


def _part_specs(parts, tr, row0):
    assert row0 % tr == 0
    specs = []
    for a, n_used in parts:
        if n_used is None:
            specs.append(pl.BlockSpec((1, tr, a.shape[2]), lambda i, idx: (idx[0], row0 // tr + i, 0)))
        else:
            specs.append(pl.BlockSpec((n_used, tr, a.shape[2]), lambda i, idx: (0, row0 // tr + i, 0)))
    return specs


def _part_total(refs, parts):
    g = None
    for ref, (_, n_used) in zip(refs, parts):
        for k in range(n_used or 1):
            t = ref[k].astype(F32)
            g = t if g is None else g + t
    return g


def _sum_parts(parts, idx, row0, nrows, tr, *, name):
    W = parts[0][0].shape[2]
    assert nrows % tr == 0

    def body(idx_ref, *refs):
        refs[-1][...] = _part_total(refs[:-1], parts)

    return pl.pallas_call(
        body, name=name,
        grid_spec=pltpu.PrefetchScalarGridSpec(
            num_scalar_prefetch=1, grid=(nrows // tr,), in_specs=_part_specs(parts, tr, row0),
            out_specs=pl.BlockSpec((tr, W), lambda i, idx: (i, 0))),
        out_shape=jax.ShapeDtypeStruct((nrows, W), F32),
        compiler_params=_cp(("parallel",)))(idx, *[a for a, _ in parts])


def _adamw(parts, idx, w, m, v, tr, *, name):
    R, W = w.shape
    assert R % tr == 0
    np_ = len(parts)

    def body(idx_ref, *refs):
        w_ref, m_ref, v_ref, g_ref, d_ref, nm_ref, nv_ref = refs[np_:]
        g = _part_total(refs[:np_], parts)
        mm = ADAM_B1 * m_ref[...] + (1.0 - ADAM_B1) * g
        vv = ADAM_B2 * v_ref[...] + (1.0 - ADAM_B2) * (g * g)
        m_hat = mm / (1.0 - ADAM_B1 ** ADAM_STEP)
        v_hat = vv / (1.0 - ADAM_B2 ** ADAM_STEP)
        g_ref[...] = g
        d_ref[...] = -ADAM_LR * (m_hat / (jnp.sqrt(v_hat) + ADAM_EPS) + ADAM_WD * w_ref[...])
        nm_ref[...] = mm
        nv_ref[...] = vv

    blk = pl.BlockSpec((tr, W), lambda i, idx: (i, 0))
    return pl.pallas_call(
        body, name=name,
        grid_spec=pltpu.PrefetchScalarGridSpec(
            num_scalar_prefetch=1, grid=(R // tr,), in_specs=_part_specs(parts, tr, 0) + [blk, blk, blk],
            out_specs=[blk] * 4),
        out_shape=[jax.ShapeDtypeStruct((R, W), F32)] * 4,
        compiler_params=_cp(("parallel",)))(idx, *[a for a, _ in parts], w, m, v)


def _pack_rest(w_kv, wa, wb, wm, w_out):
    return jnp.concatenate([w_kv[0], w_out[0]] + [t[0].reshape(-1, D_MODEL) for t in (wa, wb, wm)], axis=0)


def _unpack_rest(t):
    br = lambda i: t[RO_BR + 64 * i:RO_BR + 64 * (i + 1)].reshape(1, A_WIDTH, D_MODEL // N_DEV)
    return t[None, RO_KV:RO_OUT], br(0), br(1), br(2), t[None, RO_OUT:RO_BR]


def _orig_rows(gathered, a, b):
    res = []
    while a < b:
        dev, r = divmod(a, CS)
        n = min(b - a, CS - r)
        res.append(gathered[dev, RO_IN + r:RO_IN + r + n])
        a += n
    return res


def _full_weights(gathered):
    wt = {}
    for name, ranges in SEGS.items():
        rows = [p for a, b in ranges for p in _orig_rows(gathered, a, b)]
        if SEG_PAD[name]:
            rows.append(jnp.zeros((SEG_PAD[name], D_MODEL), gathered.dtype))
        wt[name] = jnp.concatenate(rows, axis=0)
    w_kv = gathered[:, RO_KV:RO_OUT].reshape(D_MODEL, D_MODEL)
    w_out = gathered[:, RO_OUT:RO_BR].reshape(D_MODEL, D_MODEL)
    wbs = [gathered[:, RO_BR + 64 * i:RO_BR + 64 * (i + 1)].reshape(N_DEV, A_WIDTH, D_MODEL // N_DEV)
           .transpose(1, 0, 2).reshape(A_WIDTH, D_MODEL) for i in range(3)]
    return wt, w_kv, wbs, w_out


def _orig_order(dwt):
    pieces = []
    for name, ranges in SEGS.items():
        o = 0
        for a, b in ranges:
            pieces.append((a, dwt[name][o:o + b - a]))
            o += b - a
    pieces.sort(key=lambda p: p[0])
    return jnp.concatenate([p[1] for p in pieces], axis=0)


def _pack_grads(dwt, dw_kv, dwbs, dw_out):
    g_in = jnp.pad(_orig_order(dwt).reshape(N_DEV, CS, D_MODEL), ((0, 0), (0, IN_ROWS - CS), (0, 0)))
    br = [t.reshape(A_WIDTH, N_DEV, D_MODEL // N_DEV).transpose(1, 0, 2).reshape(N_DEV, -1, D_MODEL) for t in dwbs]
    return jnp.concatenate([dw_kv.reshape(N_DEV, -1, D_MODEL), dw_out.reshape(N_DEV, -1, D_MODEL)] + br + [g_in],
                           axis=1)


def kernel(x, mem, positions, norm_pre_g, norm_post_g, norm_mem_g, w_in, b_forget, b_merge, w_mem_kv, w_branch_a, w_branch_b, w_branch_m, w_out, loss_target, m_norm_pre_g, m_norm_post_g, m_norm_mem_g, m_w_in, m_b_forget, m_b_merge, m_w_mem_kv, m_w_branch_a, m_w_branch_b, m_w_branch_m, m_w_out, v_norm_pre_g, v_norm_post_g, v_norm_mem_g, v_w_in, v_b_forget, v_b_merge, v_w_mem_kv, v_w_branch_a, v_w_branch_b, v_w_branch_m, v_w_out):
    w_rest = _pack_rest(w_mem_kv, w_branch_a, w_branch_b, w_branch_m, w_out)
    shard = jnp.concatenate([w_rest.astype(BF16), w_in[0].T.astype(BF16),
                             jnp.zeros((IN_ROWS - CS, D_MODEL), BF16)], axis=0)
    hs, (gathered,) = _rms_fwd(x[0], norm_pre_g, name="rms_pre_gather", dilations=DIL, comm=_gather_comm(shard))
    wt, w_kv, wbs, w_o = _full_weights(gathered)

    bf_pad = jnp.pad(b_forget, ((0, 0), (0, FB_PAD - B_HEADS)))
    r = _local_step(x[0], mem[0], positions[0], loss_target[0], norm_pre_g, norm_post_g, norm_mem_g,
                    wt, bf_pad, b_merge, w_kv, wbs, w_o, pack=_pack_grads, hs=hs)

    gsmall = jnp.concatenate([r["dg_pre"], r["dg_post"], r["dg_mem"], r["db_merge"],
                              r["db_forget"][:, :LANES], r["loss"]], axis=1)
    rsmall = _gather_small(gsmall, name="gather_small")
    parts, own_idx = r["parts"], r["own_idx"]

    m_rest = _pack_rest(m_w_mem_kv, m_w_branch_a, m_w_branch_b, m_w_branch_m, m_w_out)
    v_rest = _pack_rest(v_w_mem_kv, v_w_branch_a, v_w_branch_b, v_w_branch_m, v_w_out)
    gsum = _sum_parts(parts, own_idx, 0, ROWS, 208, name="sum_grads")
    outs_rest = [_unpack_rest(t) for t in
                 _adamw([(gsum[None], 1)], own_idx, w_rest, m_rest, v_rest, 64, name="adamw_rest")]
    g_in = gsum[RO_IN:RO_IN + CS].T
    outs_in = _adamw([(g_in[None], 1)], own_idx, w_in[0], m_w_in[0], v_w_in[0], 128, name="adamw_w_in")

    def small_vec(a, b, c, d, e):
        z = jnp.zeros((1, LANES - B_HEADS), F32)
        return jnp.concatenate([a, b, c, d, e, z, jnp.zeros((1, LANES), F32)], axis=1)

    outs_small = _adamw([(rsmall, N_DEV)], own_idx, small_vec(norm_pre_g, norm_post_g, norm_mem_g, b_merge, b_forget),
                        small_vec(m_norm_pre_g, m_norm_post_g, m_norm_mem_g, m_b_merge, m_b_forget),
                        small_vec(v_norm_pre_g, v_norm_post_g, v_norm_mem_g, v_b_merge, v_b_forget),
                        1, name="adamw_small")

    def small_parts(t):
        return [t[:, O_GPRE:O_GPRE + D_MODEL], t[:, O_GPOST:O_GPOST + D_MODEL], t[:, O_GMEM:O_GMEM + D_MODEL],
                t[:, O_BF:O_BF + B_HEADS], t[:, O_BM:O_BM + 3 * D_MODEL]]

    loss = outs_small[0][0, O_LOSS]
    result = [loss, r["grad_x"][None]]
    for rest, w_i, small in zip(outs_rest, outs_in, outs_small):
        gp, gq, gm, bf, bm = small_parts(small)
        w_k, w_a, w_b, w_m, w_ot = rest
        result += [gp, gq, gm, w_i[None], bf, bm, w_k, w_a, w_b, w_m, w_ot]
    return tuple(result)
```

```python
import jax
import jax.numpy as jnp
from jax import lax
from jax.experimental import pallas as pl
from jax.experimental.pallas import tpu as pltpu

F32 = jnp.float32
BF16 = jnp.bfloat16

N_DEV = 8
D_MODEL = 1024
N_MEM = 256
EPS = 1e-6
NEG = -1e30
ROPE_THETA = 500000.0
DIL = (1, 4, 16)
A_HEADS = 4
HEAD = 128
A_WIDTH = 512
B_HEADS = 8
B_HEAD = 64
M_HEADS = 4
ROT = 32
IN_COLS = 11272
FB_PAD = 256

SEGS = {
    "A0": ((0, 512), (1536, 2048), (3072, 3584)),
    "A1": ((512, 1024), (2048, 2560), (3584, 4096)),
    "A2": ((1024, 1536), (2560, 3072), (4096, 4608)),
    "B": ((5120, 6656),),
    "R": ((4608, 5120), (6664, 7176), (7176, 7688), (7688, 8200), (8200, 11272), (6656, 6664)),
}
SEG_PAD = {"A0": 0, "A1": 0, "A2": 0, "B": 0, "R": FB_PAD - B_HEADS}
R_ZA, R_ZB, R_QM, R_ZM, R_GL, R_FB = 0, 512, 1024, 1536, 2048, 5120
NR = R_FB + FB_PAD

ADAM_LR, ADAM_B1, ADAM_B2, ADAM_EPS, ADAM_WD, ADAM_STEP = 0.001, 0.9, 0.999, 1e-08, 0.01, 10

LANES = 128
VMEM_LIMIT = 56 * 1024 * 1024

CS = IN_COLS // N_DEV
RO_KV, RO_OUT, RO_BR, RO_IN = 0, 128, 256, 448
IN_ROWS = 1424
ROWS = RO_IN + IN_ROWS
O_GPRE, O_GPOST, O_GMEM, O_BM, O_BF, O_LOSS = 0, 1024, 2048, 3072, 6144, 6272
P_SMALL = 6400


def _cp(sem=None):
    return pltpu.CompilerParams(dimension_semantics=sem, vmem_limit_bytes=VMEM_LIMIT)


def _dot(a, b):
    return jnp.dot(a, b, preferred_element_type=F32)


def _dot_nt(a, b):
    return lax.dot_general(a, b, (((1,), (1,)), ((), ())), preferred_element_type=F32)


def _sigmoid(z):
    return 1.0 / (1.0 + jnp.exp(-z))


def _mm(a, b, *, name, at=False, bt=False, out_dtype=F32, tm=1024, tn=1024, tk=None, comm=None):
    assert not (at and bt)
    K, M = a.shape if at else a.shape[::-1]
    N = b.shape[0] if bt else b.shape[1]
    tm, tn = min(tm, M), min(tn, N)
    tk = K if tk is None else min(tk, K)
    assert M % tm == 0 and N % tn == 0 and K % tk == 0
    nk = K // tk
    grid = (M // tm, N // tn, nk)
    n_in = len(comm["inputs"]) if comm else 0
    n_out = len(comm["out_shape"]) if comm else 0

    def body(a_ref, b_ref, *rest):
        c_in, o_ref, c_out = rest[:n_in], rest[n_in], rest[n_in + 1:n_in + 1 + n_out]
        acc_ref, sems = rest[n_in + 1 + n_out], rest[n_in + 2 + n_out:]
        if comm:
            step = (pl.program_id(0) * grid[1] + pl.program_id(1)) * grid[2] + pl.program_id(2)

            @pl.when(step == 0)
            def _():
                comm["start"](*c_in, *c_out, *sems)

        av = a_ref[...].astype(BF16)
        bv = b_ref[...].astype(BF16)
        if at:
            p = lax.dot_general(av, bv, (((0,), (0,)), ((), ())), preferred_element_type=F32)
        else:
            p = _dot_nt(av, bv) if bt else _dot(av, bv)
        if nk == 1:
            o_ref[...] = p.astype(out_dtype)
        else:
            k = pl.program_id(2)

            @pl.when(k == 0)
            def _():
                acc_ref[...] = p

            @pl.when(k > 0)
            def _():
                acc_ref[...] += p

            @pl.when(k == nk - 1)
            def _():
                o_ref[...] = acc_ref[...].astype(out_dtype)

        if comm:
            @pl.when(step == grid[0] * grid[1] * grid[2] - 1)
            def _():
                comm["wait"](*c_in, *c_out, *sems)

    b_spec = (pl.BlockSpec((tn, tk), lambda i, j, k: (j, k)) if bt
              else pl.BlockSpec((tk, tn), lambda i, j, k: (k, j)))
    a_spec = (pl.BlockSpec((tk, tm), lambda i, j, k: (k, i)) if at
              else pl.BlockSpec((tm, tk), lambda i, j, k: (i, k)))
    out_spec = pl.BlockSpec((tm, tn), lambda i, j, k: (i, j))
    out_shape = jax.ShapeDtypeStruct((M, N), out_dtype)
    acc = pltpu.VMEM((tm, tn) if nk > 1 else (8, LANES), F32)
    if not comm:
        return pl.pallas_call(
            body, name=name, grid=grid, in_specs=[a_spec, b_spec], out_specs=out_spec, out_shape=out_shape,
            scratch_shapes=[acc], compiler_params=_cp(("parallel", "parallel", "arbitrary")))(a, b)
    return pl.pallas_call(
        body, name=name, grid=grid, in_specs=[a_spec, b_spec] + [ANY] * n_in,
        out_specs=[out_spec] + [ANY] * n_out, out_shape=[out_shape] + comm["out_shape"],
        input_output_aliases={2 + i: 1 + o for i, o in comm.get("alias", {}).items()},
        scratch_shapes=[acc] + comm["sems"],
        compiler_params=_cp(("arbitrary", "arbitrary", "arbitrary")))(a, b, *comm["inputs"])


def _mm_sum(pairs, *, name, tm=1024, tk=768, comm=None):
    M, N = pairs[0][0].shape[0], pairs[0][1].shape[1]
    tm = min(tm, M)
    steps = [a.shape[1] // tk for a, _ in pairs]
    assert M % tm == 0 and all(a.shape[1] % tk == 0 for a, _ in pairs)
    first = [sum(steps[:p]) for p in range(len(pairs))]
    total = sum(steps)
    grid = (M // tm, total)
    n_in = len(comm["inputs"]) if comm else 0
    n_out = len(comm["out_shape"]) if comm else 0
    npair = len(pairs)

    def body(*refs):
        ab, rest = refs[:2 * npair], refs[2 * npair:]
        c_in, o_ref, c_out = rest[:n_in], rest[n_in], rest[n_in + 1:n_in + 1 + n_out]
        acc_ref, sems = rest[n_in + 1 + n_out], rest[n_in + 2 + n_out:]
        k = pl.program_id(1)
        if comm:
            step = pl.program_id(0) * total + k

            @pl.when(step == 0)
            def _():
                comm["start"](*c_in, *c_out, *sems)

        @pl.when(k == 0)
        def _():
            acc_ref[...] = jnp.zeros((tm, N), F32)

        for p in range(npair):
            @pl.when(jnp.logical_and(k >= first[p], k < first[p] + steps[p]))
            def _(p=p):
                acc_ref[...] += _dot(ab[2 * p][...], ab[2 * p + 1][...])

        @pl.when(k == total - 1)
        def _():
            o_ref[...] = acc_ref[...]

        if comm:
            @pl.when(step == grid[0] * total - 1)
            def _():
                comm["wait"](*c_in, *c_out, *sems)

    def local(p):
        return lambda k: jnp.clip(k - first[p], 0, steps[p] - 1)

    in_specs = []
    for p in range(npair):
        in_specs += [pl.BlockSpec((tm, tk), lambda i, k, f=local(p): (i, f(k))),
                     pl.BlockSpec((tk, N), lambda i, k, f=local(p): (f(k), 0))]
    out_spec = pl.BlockSpec((tm, N), lambda i, k: (i, 0))
    out_shape = jax.ShapeDtypeStruct((M, N), F32)
    args = [t for pair in pairs for t in pair]
    if not comm:
        return pl.pallas_call(
            body, name=name, grid=grid, in_specs=in_specs, out_specs=out_spec, out_shape=out_shape,
            scratch_shapes=[pltpu.VMEM((tm, N), F32)], compiler_params=_cp(("parallel", "arbitrary")))(*args)
    return pl.pallas_call(
        body, name=name, grid=grid, in_specs=in_specs + [ANY] * n_in,
        out_specs=[out_spec] + [ANY] * n_out, out_shape=[out_shape] + comm["out_shape"],
        scratch_shapes=[pltpu.VMEM((tm, N), F32)] + comm["sems"],
        compiler_params=_cp(("arbitrary", "arbitrary")))(*args, *comm["inputs"])


def _class_spec(S, d, tm, width):
    return pl.BlockSpec((d, tm // d, width), lambda i: (0, i, 0))


def _rms_fwd(x, g, *, name, dilations=(), comm=None):
    S, D = x.shape
    tm = min(512, S)
    ds = [d for d in dilations if d > 1]
    nsteps = S // tm
    n_in = len(comm["inputs"]) if comm else 0
    n_out = len(comm["out_shape"]) if comm else 0
    n_tmp = D // LANES if ds else 0

    def body(x_ref, g_ref, *rest):
        c_in, o_ref, rest = rest[:n_in], rest[n_in], rest[n_in + 1:]
        cls, c_out, rest = rest[:len(ds)], rest[len(ds):len(ds) + n_out], rest[len(ds) + n_out:]
        tmps, sems = rest[:n_tmp], rest[n_tmp:]
        if comm:
            @pl.when(pl.program_id(0) == 0)
            def _():
                comm["start"](*c_in, *c_out, *sems)

        xv = x_ref[...]
        r = lax.rsqrt(jnp.mean(xv * xv, axis=-1, keepdims=True) + EPS)
        hv = xv * r * g_ref[...]
        o_ref[...] = hv.astype(BF16)
        if ds:
            for c, tmp in enumerate(tmps):
                tmp[...] = hv[:, c * LANES:(c + 1) * LANES]
            for c_ref, d in zip(cls, ds):
                for k in range(d):
                    c_ref[k] = jnp.concatenate([tmp[pl.ds(k, tm // d, stride=d), :] for tmp in tmps],
                                               axis=1).astype(BF16)
        if comm:
            @pl.when(pl.program_id(0) == nsteps - 1)
            def _():
                comm["wait"](*c_in, *c_out, *sems)

    row = pl.BlockSpec((tm, D), lambda i: (i, 0))
    outs = pl.pallas_call(
        body, name=name, grid=(nsteps,),
        in_specs=[row, pl.BlockSpec((1, D), lambda i: (0, 0))] + [ANY] * n_in,
        out_specs=[row] + [_class_spec(S, d, tm, D) for d in ds] + [ANY] * n_out,
        out_shape=[jax.ShapeDtypeStruct((S, D), BF16)] + [jax.ShapeDtypeStruct((d, S // d, D), BF16) for d in ds]
        + (comm["out_shape"] if comm else []),
        scratch_shapes=[pltpu.VMEM((tm, LANES), F32)] * n_tmp + (comm["sems"] if comm else []),
        compiler_params=_cp(("arbitrary",) if comm else ("parallel",)),
    )(x, g, *(comm["inputs"] if comm else []))
    rows = [outs[0]] + [o.reshape(S, D) for o in outs[1:1 + len(ds)]]
    if comm:
        return rows, list(outs[1 + len(ds):])
    return rows if ds else rows[0]


def _rms_bwd(x, g, dh, dy, *, name, dh_classes=()):
    S, D = x.shape
    tm = min(512, S)
    want_dx = dy is not None
    nc = len(dh_classes)

    def body(*refs):
        c_refs, refs = refs[:nc], refs[nc:]
        if want_dx:
            x_ref, g_ref, dh_ref, dy_ref, dx_ref, dg_ref = refs[:6]
        else:
            x_ref, g_ref, dh_ref, dg_ref = refs[:4]
        i = pl.program_id(0)
        xv = x_ref[...]
        r = lax.rsqrt(jnp.mean(xv * xv, axis=-1, keepdims=True) + EPS)
        xh = xv * r
        if nc:
            tmps = refs[-(D // LANES):]
            cols = [slice(c * LANES, (c + 1) * LANES) for c in range(D // LANES)]
            for tmp, cs in zip(tmps, cols):
                tmp[...] = dh_ref[:, cs]
            for c_ref, (_, d) in zip(c_refs, dh_classes):
                for k in range(d):
                    for tmp, cs in zip(tmps, cols):
                        tmp[pl.ds(k, tm // d, stride=d), :] += c_ref[k, :, cs]
            dhv = jnp.concatenate([tmp[...] for tmp in tmps], axis=1)
        else:
            dhv = dh_ref[...]
        part = jnp.sum(dhv * xh, axis=0, keepdims=True)

        @pl.when(i == 0)
        def _():
            dg_ref[...] = part

        @pl.when(i > 0)
        def _():
            dg_ref[...] += part

        if want_dx:
            dxh = dhv * g_ref[...]
            dx_ref[...] = dy_ref[...] + r * (dxh - xh * jnp.mean(dxh * xh, axis=-1, keepdims=True))

    row = pl.BlockSpec((tm, D), lambda i: (i, 0))
    vec = pl.BlockSpec((1, D), lambda i: (0, 0))
    c_specs = [_class_spec(S, d, tm, D) for _, d in dh_classes]
    c_args = [a.reshape(d, S // d, D) for a, d in dh_classes]
    scratch = [pltpu.VMEM((tm, LANES), F32)] * (D // LANES) if nc else []
    if want_dx:
        return pl.pallas_call(
            body, name=name, grid=(S // tm,), in_specs=c_specs + [row, vec, row, row], out_specs=[row, vec],
            out_shape=[jax.ShapeDtypeStruct((S, D), F32), jax.ShapeDtypeStruct((1, D), F32)],
            scratch_shapes=scratch, compiler_params=_cp(("arbitrary",)))(*c_args, x, g, dh, dy)
    return pl.pallas_call(
        body, name=name, grid=(S // tm,), in_specs=c_specs + [row, vec, row], out_specs=vec,
        out_shape=jax.ShapeDtypeStruct((1, D), F32),
        scratch_shapes=scratch, compiler_params=_cp(("arbitrary",)))(*c_args, x, g, dh)


def _post(x, out, tgt, g, *, name):
    S, D = x.shape
    tm = min(512, S)

    def body(x_ref, o_ref, t_ref, g_ref, dy_ref, do_ref, dg_ref, loss_ref):
        i = pl.program_id(0)
        ov = o_ref[...]
        r = lax.rsqrt(jnp.mean(ov * ov, axis=-1, keepdims=True) + EPS)
        n = ov * r
        gv = g_ref[...]
        e = (x_ref[...] + n * gv) - t_ref[...]
        lpart = 0.5 * jnp.sum(jnp.mean(e * e, axis=-1, keepdims=True), axis=0, keepdims=True)
        dy = e * (1.0 / D)
        dy_ref[...] = dy
        dn = dy * gv
        do_ref[...] = (r * (dn - n * jnp.mean(dn * n, axis=-1, keepdims=True))).astype(BF16)
        gpart = jnp.sum(dy * n, axis=0, keepdims=True)
        lrow = jnp.broadcast_to(lpart, (1, LANES))

        @pl.when(i == 0)
        def _():
            dg_ref[...] = gpart
            loss_ref[...] = lrow

        @pl.when(i > 0)
        def _():
            dg_ref[...] += gpart
            loss_ref[...] += lrow

    row = pl.BlockSpec((tm, D), lambda i: (i, 0))
    vec = pl.BlockSpec((1, D), lambda i: (0, 0))
    return pl.pallas_call(
        body, name=name, grid=(S // tm,), in_specs=[row, row, row, vec],
        out_specs=[row, row, vec, pl.BlockSpec((1, LANES), lambda i: (0, 0))],
        out_shape=[jax.ShapeDtypeStruct((S, D), F32), jax.ShapeDtypeStruct((S, D), BF16),
                   jax.ShapeDtypeStruct((1, D), F32), jax.ShapeDtypeStruct((1, LANES), F32)],
        compiler_params=_cp(("arbitrary",)))(x, out, tgt, g)


def _to_classes(t, d):
    if d == 1:
        return t
    S, C = t.shape
    return t.reshape(S // d, d, C).transpose(1, 0, 2).reshape(S, C)


def _rope(x, c, s1, s2):
    return x * c + pltpu.roll(x, LANES - ROT // 2, 1) * s1 + pltpu.roll(x, ROT // 2, 1) * s2


def _unrope(d, c, s1, s2):
    return d * c + pltpu.roll(d * s1, ROT // 2, 1) + pltpu.roll(d * s2, LANES - ROT // 2, 1)


def _a_band(qb):
    r = lax.broadcasted_iota(jnp.int32, (qb, qb + HEAD), 0)
    c = lax.broadcasted_iota(jnp.int32, (qb, qb + HEAD), 1)
    return jnp.logical_and(c >= r, c <= r + HEAD)


def _a_first_ok(qb, n):
    c = lax.broadcasted_iota(jnp.int32, (qb, qb + HEAD), 1)
    return jnp.logical_or(c >= HEAD, n > 0)


def _a_last_ok(qb, has_next):
    c = lax.broadcasted_iota(jnp.int32, (qb, qb + HEAD), 1)
    return jnp.logical_or(c < qb, has_next)


A_SCALE = HEAD ** -0.5


def _a_geometry(S, g):
    d = DIL[g]
    L = S // d
    TQ = min(512, L)
    return d, L, TQ, TQ // HEAD, L // TQ, L // HEAD


def _proj_rope(h, w, tabs, *, name):
    S, D = h.shape
    tm = min(512, S)

    def body(h_ref, w_ref, c_ref, s1_ref, s2_ref, o_ref):
        tc = (c_ref[...], s1_ref[...], s2_ref[...])
        u = _dot_nt(h_ref[...], w_ref[...])
        for j in range(3 * A_HEADS):
            sl = slice(j * HEAD, (j + 1) * HEAD)
            o_ref[:, sl] = (_rope(u[:, sl], *tc) if j < 2 * A_HEADS else u[:, sl]).astype(BF16)

    tab = pl.BlockSpec((tm, LANES), lambda i: (i, 0))
    return pl.pallas_call(
        body, name=name, grid=(S // tm,),
        in_specs=[pl.BlockSpec((tm, D), lambda i: (i, 0)), pl.BlockSpec((3 * A_WIDTH, D), lambda i: (0, 0)),
                  tab, tab, tab],
        out_specs=pl.BlockSpec((tm, 3 * A_WIDTH), lambda i: (i, 0)),
        out_shape=jax.ShapeDtypeStruct((S, 3 * A_WIDTH), BF16),
        compiler_params=_cp(("parallel",)))(h, w, *tabs)


def _attn_a_fwd(qkv, g, *, name):
    S = qkv.shape[0]
    d, L, TQ, nsub, nb, nblk = _a_geometry(S, g)

    def body(q_ref, kc_ref, kp_ref, vc_ref, vp_ref, o_ref, l_ref):
        n = pl.program_id(1)
        QB = min(2 * HEAD, TQ)
        band = _a_band(QB)
        first = jnp.logical_and(band, _a_first_ok(QB, n))
        for h in range(A_HEADS):
            hs = slice(h * HEAD, (h + 1) * HEAD)
            for hh in range(TQ // QB):
                sl = slice(hh * QB, (hh + 1) * QB)
                pv = slice(hh * QB - HEAD, hh * QB)
                kcat = jnp.concatenate([kp_ref[:, hs] if hh == 0 else kc_ref[pv, hs], kc_ref[sl, hs]], axis=0)
                vcat = jnp.concatenate([vp_ref[:, hs] if hh == 0 else vc_ref[pv, hs], vc_ref[sl, hs]], axis=0)
                s = jnp.where(first if hh == 0 else band, _dot_nt(q_ref[sl, hs], kcat) * A_SCALE, NEG)
                m = jnp.max(s, axis=-1, keepdims=True)
                p = jnp.exp(s - m)
                den = jnp.sum(p, axis=-1, keepdims=True)
                o_ref[sl, hs] = _dot(p.astype(BF16), vcat) / den
                l_ref[sl, hs] = jnp.broadcast_to(m + jnp.log(den), (QB, HEAD))

    rcur = lambda r, n: r * nb + n
    rprv = lambda r, n: r * nblk + jnp.maximum(n * nsub - 1, 0)
    cur = lambda off: pl.BlockSpec((TQ, A_WIDTH), lambda r, n: (rcur(r, n), off))
    prv = lambda off: pl.BlockSpec((HEAD, A_WIDTH), lambda r, n: (rprv(r, n), off))
    out = pl.BlockSpec((TQ, A_WIDTH), lambda r, n: (rcur(r, n), 0))
    return pl.pallas_call(
        body, name=name, grid=(d, nb),
        in_specs=[cur(0), cur(1), prv(1), cur(2), prv(2)],
        out_specs=[out, out],
        out_shape=[jax.ShapeDtypeStruct((S, A_WIDTH), F32)] * 2,
        compiler_params=_cp(("parallel", "parallel")),
    )(qkv, qkv, qkv, qkv, qkv)


def _attn_a_dq(qkv, tabs, g, do, lse, adj, du, *, name):
    S = qkv.shape[0]
    d, L, TQ, nsub, nb, nblk = _a_geometry(S, g)

    def body(q_ref, kc_ref, kp_ref, vc_ref, vp_ref, do_ref, l_ref, adj_ref, c_ref, s1_ref, s2_ref, du_ref, dq_ref):
        n = pl.program_id(1)
        QB = min(2 * HEAD, TQ)
        band = _a_band(QB)
        first = jnp.logical_and(band, _a_first_ok(QB, n))
        for h in range(A_HEADS):
            hs = slice(h * HEAD, (h + 1) * HEAD)
            for hh in range(TQ // QB):
                sl = slice(hh * QB, (hh + 1) * QB)
                pv = slice(hh * QB - HEAD, hh * QB)
                kcat = jnp.concatenate([kp_ref[:, hs] if hh == 0 else kc_ref[pv, hs], kc_ref[sl, hs]], axis=0)
                vcat = jnp.concatenate([vp_ref[:, hs] if hh == 0 else vc_ref[pv, hs], vc_ref[sl, hs]], axis=0)
                s = jnp.where(first if hh == 0 else band, _dot_nt(q_ref[sl, hs], kcat) * A_SCALE, NEG)
                p = jnp.exp(s - l_ref[sl, hs][:, :1])
                ds = p * (_dot_nt(do_ref[sl, hs], vcat) + adj_ref[sl, hs][:, :1])
                dq = _dot(ds.astype(BF16), kcat) * A_SCALE
                dq_ref[sl, hs] = _unrope(dq, c_ref[sl, :], s1_ref[sl, :], s2_ref[sl, :]).astype(BF16)

    rcur = lambda r, n: r * nb + n
    rprv = lambda r, n: r * nblk + jnp.maximum(n * nsub - 1, 0)
    cur = lambda off: pl.BlockSpec((TQ, A_WIDTH), lambda r, n: (rcur(r, n), off))
    prv = lambda off: pl.BlockSpec((HEAD, A_WIDTH), lambda r, n: (rprv(r, n), off))
    tcur = pl.BlockSpec((TQ, LANES), lambda r, n: (rcur(r, n), 0))
    blk = cur(0)
    return pl.pallas_call(
        body, name=name, grid=(d, nb),
        in_specs=[cur(0), cur(1), prv(1), cur(2), prv(2), blk, blk, blk, tcur, tcur, tcur, ANY],
        out_specs=blk,
        out_shape=jax.ShapeDtypeStruct((S, 3 * A_WIDTH), BF16),
        input_output_aliases={11: 0},
        compiler_params=_cp(("parallel", "parallel")),
    )(qkv, qkv, qkv, qkv, qkv, do, lse, adj, *tabs, du)


def _attn_a_dkv(qkv, tabs, g, do, lse, adj, *, name):
    S = qkv.shape[0]
    d, L, TQ, nsub, nb, nblk = _a_geometry(S, g)

    def body(qc_ref, qn_ref, kc_ref, vc_ref, doc_ref, don_ref, lc_ref, ln_ref, ac_ref, an_ref,
             c_ref, s1_ref, s2_ref, du_ref):
        n = pl.program_id(1)
        QB = min(2 * HEAD, TQ)
        nh = TQ // QB
        band = _a_band(QB)
        end = jnp.logical_and(band, _a_last_ok(QB, n < nb - 1))
        for h in range(A_HEADS):
            hs = slice(h * HEAD, (h + 1) * HEAD)
            for kh in range(nh):
                sl = slice(kh * QB, (kh + 1) * QB)
                nx = slice((kh + 1) * QB, (kh + 1) * QB + HEAD)
                last = kh == nh - 1
                cat = lambda cur, nxt: jnp.concatenate([cur[sl, hs], nxt[:, hs] if last else cur[nx, hs]], axis=0)
                qcat = cat(qc_ref, qn_ref)
                docat = cat(doc_ref, don_ref)
                lt = cat(lc_ref, ln_ref).T[:1, :]
                at = cat(ac_ref, an_ref).T[:1, :]
                st = jnp.where(end if last else band, _dot_nt(kc_ref[sl, hs], qcat) * A_SCALE, NEG)
                pt = jnp.exp(st - lt)
                dv_cols = slice(2 * A_WIDTH + h * HEAD, 2 * A_WIDTH + (h + 1) * HEAD)
                dk_cols = slice(A_WIDTH + h * HEAD, A_WIDTH + (h + 1) * HEAD)
                du_ref[sl, dv_cols] = _dot(pt.astype(BF16), docat).astype(BF16)
                dst = pt * (_dot_nt(vc_ref[sl, hs], docat) + at)
                dk = _dot(dst.astype(BF16), qcat) * A_SCALE
                du_ref[sl, dk_cols] = _unrope(dk, c_ref[sl, :], s1_ref[sl, :], s2_ref[sl, :]).astype(BF16)

    rcur = lambda r, n: r * nb + n
    rnxt = lambda r, n: r * nblk + jnp.minimum((n + 1) * nsub, nblk - 1)
    cur = lambda off: pl.BlockSpec((TQ, A_WIDTH), lambda r, n: (rcur(r, n), off))
    nxu = lambda off: pl.BlockSpec((HEAD, A_WIDTH), lambda r, n: (rnxt(r, n), off))
    tcur = pl.BlockSpec((TQ, LANES), lambda r, n: (rcur(r, n), 0))
    blk, bnx = cur(0), nxu(0)
    return pl.pallas_call(
        body, name=name, grid=(d, nb),
        in_specs=[cur(0), nxu(0), cur(1), cur(2), blk, bnx, blk, bnx, blk, bnx, tcur, tcur, tcur],
        out_specs=pl.BlockSpec((TQ, 3 * A_WIDTH), lambda r, n: (rcur(r, n), 0)),
        out_shape=jax.ShapeDtypeStruct((S, 3 * A_WIDTH), BF16),
        compiler_params=_cp(("parallel", "parallel")),
    )(qkv, qkv, qkv, qkv, do, do, lse, lse, adj, adj, *tabs)


def _silu_parts(z):
    sg = _sigmoid(z)
    return z * sg, sg * (1.0 + z * (1.0 - sg))


def _classes_to_tokens(c_ref, d, tm, tmps):
    if d == 1:
        return c_ref[...].astype(F32)
    for k in range(d):
        for c, tmp in enumerate(tmps):
            tmp[pl.ds(k, tm // d, stride=d), :] = c_ref[k, :, c * LANES:(c + 1) * LANES].astype(F32)
    return jnp.concatenate([tmp[...] for tmp in tmps], axis=1)


def _tokens_to_classes(val, c_ref, d, tm, tmps):
    if d == 1:
        c_ref[...] = val.astype(c_ref.dtype)
        return
    for c, tmp in enumerate(tmps):
        tmp[...] = val[:, c * LANES:(c + 1) * LANES]
    for k in range(d):
        c_ref[k] = jnp.concatenate([tmp[pl.ds(k, tm // d, stride=d), :] for tmp in tmps], axis=1).astype(c_ref.dtype)


def _group_spec(S, d, tm):
    if d == 1:
        return pl.BlockSpec((tm, A_WIDTH), lambda i: (i, 0))
    return _class_spec(S, d, tm, A_WIDTH)


def _group_view(t, d):
    return t if d == 1 else t.reshape(d, t.shape[0] // d, t.shape[1])


def _merge_a_fwd(os_, ls_, ur, *, name):
    S = ur.shape[0]
    tm = min(512, S)

    def body(o0, o1, o2, l0, l1, l2, z_ref, y_ref, *tmps):
        ls = [_classes_to_tokens(r, d, tm, tmps) for r, d in zip((l0, l1, l2), DIL)]
        ov = [_classes_to_tokens(r, d, tm, tmps) for r, d in zip((o0, o1, o2), DIL)]
        mx = jnp.maximum(jnp.maximum(ls[0], ls[1]), ls[2])
        es = [jnp.exp(l - mx) for l in ls]
        den = es[0] + es[1] + es[2]
        y = (es[0] / den) * ov[0] + (es[1] / den) * ov[1] + (es[2] / den) * ov[2]
        y_ref[...] = (y * _silu_parts(z_ref[...])[0]).astype(BF16)

    blk = pl.BlockSpec((tm, A_WIDTH), lambda i: (i, 0))
    groups = [_group_spec(S, d, tm) for d in DIL]
    return pl.pallas_call(
        body, name=name, grid=(S // tm,),
        in_specs=groups + groups + [pl.BlockSpec((tm, A_WIDTH), lambda i: (i, R_ZA // A_WIDTH))],
        out_specs=blk, out_shape=jax.ShapeDtypeStruct((S, A_WIDTH), BF16),
        scratch_shapes=[pltpu.VMEM((tm, LANES), F32)] * (A_WIDTH // LANES),
        compiler_params=_cp(("parallel",)))(*[_group_view(t, d) for t, d in zip(os_, DIL)],
                                            *[_group_view(t, d) for t, d in zip(ls_, DIL)], ur)


def _merge_a_bwd(os_, ls_, ur, dya, du_r, *, name):
    S = ur.shape[0]
    tm = min(256, S)

    def body(o0, o1, o2, l0, l1, l2, z_ref, dy_ref, du_in, d0, d1, d2, a0, a1, a2, dz_ref, *tmps):
        ls = [_classes_to_tokens(r, d, tm, tmps) for r, d in zip((l0, l1, l2), DIL)]
        ov = [_classes_to_tokens(r, d, tm, tmps) for r, d in zip((o0, o1, o2), DIL)]
        mx = jnp.maximum(jnp.maximum(ls[0], ls[1]), ls[2])
        es = [jnp.exp(l - mx) for l in ls]
        den = es[0] + es[1] + es[2]
        ws = [e / den for e in es]
        y = ws[0] * ov[0] + ws[1] * ov[1] + ws[2] * ov[2]
        sz, dsz = _silu_parts(z_ref[...])
        dyv = dy_ref[...]
        dz_ref[...] = (dyv * y * dsz).astype(BF16)
        dyp = dyv * sz
        ts = []
        for h in range(A_HEADS):
            sl = slice(h * HEAD, (h + 1) * HEAD)
            t = jnp.zeros((tm, 1), F32)
            for gi in range(3):
                t = t + ws[gi][:, sl][:, :1] * jnp.sum(dyp[:, sl] * ov[gi][:, sl], axis=-1, keepdims=True)
            ts.append(jnp.broadcast_to(t, (tm, HEAD)))
        tb = jnp.concatenate(ts, axis=1)
        for gi, (dref, aref) in enumerate(((d0, a0), (d1, a1), (d2, a2))):
            _tokens_to_classes(ws[gi] * dyp, dref, DIL[gi], tm, tmps)
            _tokens_to_classes(-ws[gi] * tb, aref, DIL[gi], tm, tmps)

    blk = pl.BlockSpec((tm, A_WIDTH), lambda i: (i, 0))
    groups = [_group_spec(S, d, tm) for d in DIL]
    shaped = lambda dt: [jax.ShapeDtypeStruct((S, A_WIDTH) if d == 1 else (d, S // d, A_WIDTH), dt) for d in DIL]
    outs = pl.pallas_call(
        body, name=name, grid=(S // tm,),
        in_specs=groups + groups + [pl.BlockSpec((tm, A_WIDTH), lambda i: (i, R_ZA // A_WIDTH)), blk, ANY],
        out_specs=groups + groups + [pl.BlockSpec((tm, A_WIDTH), lambda i: (i, R_ZA // A_WIDTH))],
        out_shape=shaped(BF16) + shaped(F32) + [jax.ShapeDtypeStruct(du_r.shape, BF16)],
        input_output_aliases={8: 6},
        scratch_shapes=[pltpu.VMEM((tm, LANES), F32)] * (A_WIDTH // LANES),
        compiler_params=_cp(("parallel",)))(*[_group_view(t, d) for t, d in zip(os_, DIL)],
                                            *[_group_view(t, d) for t, d in zip(ls_, DIL)], ur, dya, du_r)
    flat = [t.reshape(S, A_WIDTH) for t in outs[:6]]
    return flat[0:3], flat[3:6], outs[6]


def _logf(ur, bf_pad, *, name):
    S = ur.shape[0]
    tm = min(1024, S)

    def body(u_ref, b_ref, o_ref):
        z = u_ref[...] + b_ref[...]
        o_ref[...] = jnp.minimum(z, 0.0) - jnp.log(1.0 + jnp.exp(-jnp.abs(z)))

    return pl.pallas_call(
        body, name=name, grid=(S // tm,),
        in_specs=[pl.BlockSpec((tm, FB_PAD), lambda i: (i, R_FB // FB_PAD)),
                  pl.BlockSpec((1, FB_PAD), lambda i: (0, 0))],
        out_specs=pl.BlockSpec((tm, FB_PAD), lambda i: (i, 0)),
        out_shape=jax.ShapeDtypeStruct((S, FB_PAD), F32),
        compiler_params=_cp(("parallel",)))(ur, bf_pad)


def _cumsum_lanes(x, reverse, *, name):
    nt, H, _ = x.shape
    R = nt * H

    def body(x_ref, o_ref):
        v = x_ref[...].reshape(R, LANES)
        lane = lax.broadcasted_iota(jnp.int32, (R, LANES), 1)
        row = lax.broadcasted_iota(jnp.int32, (R, LANES), 0)

        def scan(t, step, idx, n, axis):
            while step < n:
                if reverse:
                    t = t + jnp.where(idx < n - step, pltpu.roll(t, n - step, axis), 0.0)
                else:
                    t = t + jnp.where(idx >= step, pltpu.roll(t, step, axis), 0.0)
                step *= 2
            return t

        v = scan(v, 1, lane, LANES, 1)
        total = jnp.broadcast_to(v[:, :1] if reverse else v[:, LANES - 1:], (R, LANES))
        carry = scan(total, H, row, R, 0) - total
        o_ref[...] = (v + carry).reshape(nt, H, LANES)

    return pl.pallas_call(
        body, name=name, out_shape=jax.ShapeDtypeStruct((nt, H, LANES), F32),
        in_specs=[pl.BlockSpec(memory_space=pltpu.VMEM)], out_specs=pl.BlockSpec(memory_space=pltpu.VMEM),
        compiler_params=_cp())(x)


B_SCALE = B_HEAD ** -0.5


def _pair_masks():
    lane = lax.broadcasted_iota(jnp.int32, (1, LANES), 1)
    row = lax.broadcasted_iota(jnp.int32, (LANES, 1), 0)
    return (lane < B_HEAD, lane >= B_HEAD), (row < B_HEAD, row >= B_HEAD)


def _causal_t(T):
    r = lax.broadcasted_iota(jnp.int32, (T, T), 0)
    c = lax.broadcasted_iota(jnp.int32, (T, T), 1)
    return r <= c


def _zero_other(x, keep):
    return jnp.where(keep, x, jnp.zeros_like(x))


def _fox_aug(ub, c, *, name):
    S = ub.shape[0]
    T = min(2048, S)

    def body(q_ref, k_ref, c_ref, qa_ref, ka_ref):
        lane = lax.broadcasted_iota(jnp.int32, (1, LANES), 1)
        q = q_ref[...] * B_SCALE
        k = k_ref[...]
        for a in range(2):
            own = (lane < B_HEAD) if a == 0 else (lane >= B_HEAD)
            o = B_HEAD if a == 0 else 0
            cv = jnp.broadcast_to(c_ref[:, a:a + 1], (T, LANES))
            hi = cv.astype(BF16)
            r1 = cv - hi.astype(F32)
            mid = r1.astype(BF16)
            lo = (r1 - mid.astype(F32)).astype(BF16)
            pieces = (hi, mid, lo)
            one = jnp.ones((T, LANES), BF16)
            qa = jnp.where(own, q, jnp.zeros_like(q))
            ka = jnp.where(own, k, jnp.zeros_like(k))
            for t in range(3):
                qa = jnp.where(lane == o + t, pieces[t], qa)
                qa = jnp.where(lane == o + 3 + t, one, qa)
                ka = jnp.where(lane == o + t, one, ka)
                ka = jnp.where(lane == o + 3 + t, -pieces[t], ka)
            qa_ref[a] = qa
            ka_ref[a] = ka

    out = pl.BlockSpec((2, T, LANES), lambda h, i: (h, i, 0))
    c_pairs = c.reshape(B_HEADS // 2, 2, S).transpose(0, 2, 1)
    return pl.pallas_call(
        body, name=name, grid=(B_HEADS // 2, S // T),
        in_specs=[pl.BlockSpec((T, LANES), lambda h, i: (i, h)), pl.BlockSpec((T, LANES), lambda h, i: (i, 4 + h)),
                  pl.BlockSpec((None, T, 2), lambda h, i: (h, i, 0))],
        out_specs=[out, out], out_shape=[jax.ShapeDtypeStruct((B_HEADS, S, LANES), BF16)] * 2,
        compiler_params=_cp(("parallel", "parallel")))(ub, ub, c_pairs)


def _fox_fwd(qaug, kaug, vt, *, name):
    S = qaug.shape[1]
    T = min(512, S)
    nq = S // T

    def body(q_ref, k_ref, vt_ref, o_ref, l_ref, m_s, l_s, acc_s, st_s):
        i = pl.program_id(1)
        _, rows = _pair_masks()
        qm = [q_ref[0], q_ref[1]]
        m_s[...] = jnp.full((2, 1, T), NEG, F32)
        l_s[...] = jnp.zeros((2, 1, T), F32)
        acc_s[...] = jnp.zeros((LANES, T), F32)

        def logits(j):
            off = pl.multiple_of(j * T, T)
            return [_dot_nt(k_ref[a, pl.ds(off, T), :], qm[a]) for a in range(2)]

        def step(j, masked, prefetch):
            nxt = logits(j + 1) if prefetch else None
            vtj = vt_ref[j]
            upd = jnp.zeros((LANES, T), F32)
            alphas = []
            for a in range(2):
                st = st_s[a]
                if masked:
                    st = jnp.where(_causal_t(T), st, NEG)
                m_old = m_s[a]
                m_new = jnp.maximum(m_old, jnp.max(st, axis=0, keepdims=True))
                alpha = jnp.exp(m_old - m_new)
                pt = jnp.exp(st - m_new)
                l_s[a] = alpha * l_s[a] + jnp.sum(pt, axis=0, keepdims=True)
                m_s[a] = m_new
                upd = upd + _dot(_zero_other(vtj, rows[a]), pt.astype(BF16))
                alphas.append(alpha)
            acc_s[...] = acc_s[...] * jnp.where(rows[0], alphas[0], alphas[1]) + upd
            if prefetch:
                st_s[0] = nxt[0]
                st_s[1] = nxt[1]

        def loop(j, carry):
            step(j, False, True)
            return carry

        first = logits(0)
        st_s[0] = first[0]
        st_s[1] = first[1]
        lax.fori_loop(0, i, loop, 0)
        step(i, True, False)
        o_ref[...] = (acc_s[...] / jnp.where(rows[0], l_s[0], l_s[1])).T
        l_ref[0] = m_s[0] + jnp.log(l_s[0])
        l_ref[1] = m_s[1] + jnp.log(l_s[1])

    stat = pl.BlockSpec((2, None, 1, T), lambda h, i: (h, i, 0, 0))
    return pl.pallas_call(
        body, name=name, grid=(B_HEADS // 2, nq),
        in_specs=[pl.BlockSpec((2, T, LANES), lambda h, i: (h, i, 0)),
                  pl.BlockSpec((2, S, LANES), lambda h, i: (h, 0, 0)),
                  pl.BlockSpec((nq, LANES, T), lambda h, i: (0, h, 0))],
        out_specs=[pl.BlockSpec((T, LANES), lambda h, i: (i, h)), stat],
        out_shape=[jax.ShapeDtypeStruct((S, A_WIDTH), F32), jax.ShapeDtypeStruct((B_HEADS, nq, 1, T), F32)],
        scratch_shapes=[pltpu.VMEM((2, 1, T), F32), pltpu.VMEM((2, 1, T), F32), pltpu.VMEM((LANES, T), F32),
                        pltpu.VMEM((2, T, T), F32)],
        compiler_params=_cp(("parallel", "parallel")),
    )(qaug, kaug, vt)


def _fox_delta(o, do, *, name):
    S = o.shape[0]
    T = min(512, S)
    nq = S // T

    per = min(4, nq)

    def body(o_ref, do_ref, d_ref):
        _, rows = _pair_masks()
        for t in range(per):
            sl = slice(t * T, (t + 1) * T)
            prod_t = (do_ref[sl, :].astype(F32) * o_ref[sl, :]).T
            d_ref[0, t] = jnp.sum(_zero_other(prod_t, rows[0]), axis=0, keepdims=True)
            d_ref[1, t] = jnp.sum(_zero_other(prod_t, rows[1]), axis=0, keepdims=True)

    tile = pl.BlockSpec((per * T, LANES), lambda h, i: (i, h))
    return pl.pallas_call(
        body, name=name, grid=(B_HEADS // 2, nq // per), in_specs=[tile, tile],
        out_specs=pl.BlockSpec((2, per, 1, T), lambda h, i: (h, i, 0, 0)),
        out_shape=jax.ShapeDtypeStruct((B_HEADS, nq, 1, T), F32),
        compiler_params=_cp(("parallel", "parallel")))(o, do)


def _fox_bwd(ub, qaug, kaug, kt, do, lse, delta, *, name):
    S = ub.shape[0]
    T = min(512, S)
    nq = S // T

    def body(k_ref, v_ref, kt_ref, q_ref, do_ref, l_ref, dl_ref,
             dk_ref, dv_ref, dck_ref, dqt_ref, dcq_ref, dk_s, dv_s, dc_s):
        j = pl.program_id(1)
        lanes, rows = _pair_masks()
        vv = v_ref[...]
        ktj = kt_ref[...]
        km = [k_ref[0], k_ref[1]]
        ktm = [_zero_other(ktj, rows[0]), _zero_other(ktj, rows[1])]
        dk_s[...] = jnp.zeros((2, T, LANES), F32)
        dv_s[...] = jnp.zeros((T, LANES), F32)
        dc_s[...] = jnp.zeros((2, T, 1), F32)

        @pl.when(j == 0)
        def _():
            dqt_ref[...] = jnp.zeros((nq, LANES, T), F32)
            dcq_ref[...] = jnp.zeros((2, nq, 1, T), F32)

        def step(i, masked):
            off = pl.multiple_of(i * T, T)
            doi = do_ref[pl.ds(off, T), :]
            upd = jnp.zeros((LANES, T), F32)
            for a in range(2):
                qi = q_ref[a, pl.ds(off, T), :]
                st = _dot_nt(km[a], qi)
                if masked:
                    st = jnp.where(_causal_t(T), st, NEG)
                pt = jnp.exp(st - l_ref[a, i])
                doa = _zero_other(doi, lanes[a])
                dv_s[...] += _dot(pt.astype(BF16), doa)
                dst = pt * (_dot_nt(vv, doa) - dl_ref[a, i])
                dsb = dst.astype(BF16)
                dk_s[a] += _dot(dsb, qi)
                upd = upd + _dot(ktm[a], dsb)
                dc_s[a] -= jnp.sum(dst, axis=-1, keepdims=True)
                dcq_ref[a, i] += jnp.sum(dst, axis=0, keepdims=True)
            dqt_ref[i] += upd

        def loop(i, carry):
            step(i, False)
            return carry

        step(j, True)
        lax.fori_loop(j + 1, nq, loop, 0)
        dk_ref[...] = jnp.where(lanes[0], dk_s[0], dk_s[1]).astype(BF16)
        dv_ref[...] = dv_s[...].astype(BF16)
        dck_ref[...] = dc_s[...]

    rowv = pl.BlockSpec((2, nq, 1, T), lambda h, j: (h, 0, 0, 0))
    tile = pl.BlockSpec((T, LANES), lambda h, j: (j, h))
    return pl.pallas_call(
        body, name=name, grid=(B_HEADS // 2, nq),
        in_specs=[pl.BlockSpec((2, T, LANES), lambda h, j: (h, j, 0)),
                  pl.BlockSpec((T, LANES), lambda h, j: (j, 8 + h)),
                  pl.BlockSpec((None, LANES, T), lambda h, j: (j, h, 0)),
                  pl.BlockSpec((2, S, LANES), lambda h, j: (h, 0, 0)),
                  pl.BlockSpec((S, LANES), lambda h, j: (0, h)),
                  rowv, rowv],
        out_specs=[tile, tile, pl.BlockSpec((2, T, 1), lambda h, j: (h, j, 0)),
                   pl.BlockSpec((nq, LANES, T), lambda h, j: (0, h, 0)), rowv],
        out_shape=[jax.ShapeDtypeStruct((S, A_WIDTH), BF16)] * 2 + [jax.ShapeDtypeStruct((B_HEADS, S, 1), F32),
                   jax.ShapeDtypeStruct((nq, A_WIDTH, T), F32), jax.ShapeDtypeStruct((B_HEADS, nq, 1, T), F32)],
        scratch_shapes=[pltpu.VMEM((2, T, LANES), F32), pltpu.VMEM((T, LANES), F32), pltpu.VMEM((2, T, 1), F32)],
        compiler_params=_cp(("parallel", "arbitrary")),
    )(kaug, ub, kt, qaug, do, lse, delta)


def _gate_fwd(o, ur, zcol, *, name):
    S = ur.shape[0]
    tm = min(1024, S)

    def body(o_ref, z_ref, y_ref):
        y_ref[...] = (o_ref[...] * _silu_parts(z_ref[...])[0]).astype(BF16)

    blk = pl.BlockSpec((tm, A_WIDTH), lambda i: (i, 0))
    return pl.pallas_call(
        body, name=name, grid=(S // tm,),
        in_specs=[blk, pl.BlockSpec((tm, A_WIDTH), lambda i: (i, zcol // A_WIDTH))],
        out_specs=blk, out_shape=jax.ShapeDtypeStruct((S, A_WIDTH), BF16),
        compiler_params=_cp(("parallel",)))(o, ur)


def _gate_bwd(o, ur, zcol, dy, du_r, *, name):
    S = ur.shape[0]
    tm = min(1024, S)

    def body(o_ref, z_ref, dy_ref, du_in, do_ref, dz_ref):
        sz, dsz = _silu_parts(z_ref[...])
        dyv = dy_ref[...]
        do_ref[...] = (dyv * sz).astype(BF16)
        dz_ref[...] = (dyv * o_ref[...] * dsz).astype(BF16)

    blk = pl.BlockSpec((tm, A_WIDTH), lambda i: (i, 0))
    gate = pl.BlockSpec((tm, A_WIDTH), lambda i: (i, zcol // A_WIDTH))
    return pl.pallas_call(
        body, name=name, grid=(S // tm,),
        in_specs=[blk, gate, blk, ANY],
        out_specs=[blk, gate],
        out_shape=[jax.ShapeDtypeStruct((S, A_WIDTH), BF16), jax.ShapeDtypeStruct(du_r.shape, BF16)],
        input_output_aliases={3: 1},
        compiler_params=_cp(("parallel",)))(o, ur, dy, du_r)


def _dfb(ur, bf_pad, dlogf_pad, du_r, *, name):
    S = ur.shape[0]
    tm = min(1024, S)

    def body(u_ref, b_ref, d_ref, du_in, o_ref, s_ref):
        i = pl.program_id(0)
        dv = d_ref[...] * _sigmoid(-(u_ref[...] + b_ref[...]))
        o_ref[...] = dv.astype(BF16)
        part = jnp.sum(dv, axis=0, keepdims=True)

        @pl.when(i == 0)
        def _():
            s_ref[...] = part

        @pl.when(i > 0)
        def _():
            s_ref[...] += part

    vec = pl.BlockSpec((1, FB_PAD), lambda i: (0, 0))
    blk = pl.BlockSpec((tm, FB_PAD), lambda i: (i, 0))
    fb = pl.BlockSpec((tm, FB_PAD), lambda i: (i, R_FB // FB_PAD))
    return pl.pallas_call(
        body, name=name, grid=(S // tm,),
        in_specs=[fb, vec, blk, ANY],
        out_specs=[fb, vec],
        out_shape=[jax.ShapeDtypeStruct(du_r.shape, BF16), jax.ShapeDtypeStruct((1, FB_PAD), F32)],
        input_output_aliases={3: 0},
        compiler_params=_cp(("arbitrary",)))(ur, bf_pad, dlogf_pad, du_r)


M_SCALE = HEAD ** -0.5


def _mem_fwd(ur, mkv, *, name):
    S = ur.shape[0]
    T = min(512, S)

    def body(q_ref, z_ref, k_ref, v_ref, y_ref):
        for h in range(M_HEADS):
            hs = slice(h * HEAD, (h + 1) * HEAD)
            s = _dot_nt(q_ref[:, hs].astype(BF16), k_ref[:, hs].astype(BF16)) * M_SCALE
            p = jnp.exp(s - jnp.max(s, axis=-1, keepdims=True))
            p = p / jnp.sum(p, axis=-1, keepdims=True)
            o = _dot(p.astype(BF16), v_ref[:, hs].astype(BF16))
            y_ref[:, hs] = (o * _silu_parts(z_ref[:, hs])[0]).astype(BF16)

    wide = lambda col: pl.BlockSpec((T, A_WIDTH), lambda i: (i, col // A_WIDTH))
    kv = lambda half: pl.BlockSpec((N_MEM, A_WIDTH), lambda i: (0, half))
    return pl.pallas_call(
        body, name=name, grid=(S // T,),
        in_specs=[wide(R_QM), wide(R_ZM), kv(0), kv(1)],
        out_specs=pl.BlockSpec((T, A_WIDTH), lambda i: (i, 0)),
        out_shape=jax.ShapeDtypeStruct((S, A_WIDTH), BF16),
        compiler_params=_cp(("parallel",)))(ur, ur, mkv, mkv)


def _mem_bwd(ur, mkv, dy, du_r, *, name):
    S = ur.shape[0]
    T = min(512, S)

    def body(q_ref, z_ref, k_ref, v_ref, dy_ref, du_in, du_ref, dk_ref, dv_ref):
        i = pl.program_id(0)

        @pl.when(i == 0)
        def _():
            dk_ref[...] = jnp.zeros((N_MEM, A_WIDTH), F32)
            dv_ref[...] = jnp.zeros((N_MEM, A_WIDTH), F32)

        for h in range(M_HEADS):
            hs = slice(h * HEAD, (h + 1) * HEAD)
            qv = q_ref[:, hs].astype(BF16)
            kv = k_ref[:, hs].astype(BF16)
            vv = v_ref[:, hs].astype(BF16)
            s = _dot_nt(qv, kv) * M_SCALE
            p = jnp.exp(s - jnp.max(s, axis=-1, keepdims=True))
            p = p / jnp.sum(p, axis=-1, keepdims=True)
            o = _dot(p.astype(BF16), vv)
            sz, dsz = _silu_parts(z_ref[:, hs])
            dyv = dy_ref[:, hs]
            du_ref[:, A_WIDTH + h * HEAD:A_WIDTH + (h + 1) * HEAD] = (dyv * o * dsz).astype(BF16)
            dov = (dyv * sz).astype(BF16)
            dp = _dot_nt(dov, vv)
            ds = p * (dp - jnp.sum(p * dp, axis=-1, keepdims=True))
            du_ref[:, hs] = (_dot(ds.astype(BF16), kv) * M_SCALE).astype(BF16)
            dv_ref[:, hs] += _dot(p.T.astype(BF16), dov)
            dk_ref[:, hs] += _dot(ds.T.astype(BF16), qv) * M_SCALE

    wide = lambda col: pl.BlockSpec((T, A_WIDTH), lambda i: (i, col // A_WIDTH))
    kv = lambda half: pl.BlockSpec((N_MEM, A_WIDTH), lambda i: (0, half))
    tile = pl.BlockSpec((T, A_WIDTH), lambda i: (i, 0))
    acc = pl.BlockSpec((N_MEM, A_WIDTH), lambda i: (0, 0))
    assert R_ZM == R_QM + A_WIDTH and R_QM % (2 * A_WIDTH) == 0
    return pl.pallas_call(
        body, name=name, grid=(S // T,),
        in_specs=[wide(R_QM), wide(R_ZM), kv(0), kv(1), tile, ANY],
        out_specs=[pl.BlockSpec((T, 2 * A_WIDTH), lambda i: (i, R_QM // (2 * A_WIDTH))), acc, acc],
        out_shape=[jax.ShapeDtypeStruct(du_r.shape, BF16)] + [jax.ShapeDtypeStruct((N_MEM, A_WIDTH), F32)] * 2,
        input_output_aliases={5: 0},
        compiler_params=_cp(("arbitrary",)))(ur, ur, mkv, mkv, dy, du_r)


def _branch_fwd(ys, wbs, ur, b_merge, *, name):
    S = ur.shape[0]
    tm, tn = min(512, S), 512
    nj = D_MODEL // tn

    def body(ya, yb, ym, wa, wb, wm, g0, g1, g2, b0, b1, b2, mg_ref, p_ref):
        acc = jnp.zeros((tm, tn), F32)
        for i, (y, w, gr, br) in enumerate(((ya, wa, g0, b0), (yb, wb, g1, b1), (ym, wm, g2, b2))):
            pr = _dot(y[...], w[...])
            p_ref[i] = pr.astype(BF16)
            acc = acc + _sigmoid(gr[...] + br[...]) * pr
        mg_ref[...] = acc.astype(BF16)

    yspec = pl.BlockSpec((tm, A_WIDTH), lambda i, j: (i, 0))
    wspec = pl.BlockSpec((A_WIDTH, tn), lambda i, j: (0, j))
    gspec = lambda b: pl.BlockSpec((tm, tn), lambda i, j: (i, (R_GL + b * D_MODEL) // tn + j))
    bspec = lambda b: pl.BlockSpec((1, tn), lambda i, j: (0, b * nj + j))
    return pl.pallas_call(
        body, name=name, grid=(S // tm, nj),
        in_specs=[yspec] * 3 + [wspec] * 3 + [gspec(0), gspec(1), gspec(2), bspec(0), bspec(1), bspec(2)],
        out_specs=[pl.BlockSpec((tm, tn), lambda i, j: (i, j)),
                   pl.BlockSpec((3, tm, tn), lambda i, j: (0, i, j))],
        out_shape=[jax.ShapeDtypeStruct((S, D_MODEL), BF16), jax.ShapeDtypeStruct((3, S, D_MODEL), BF16)],
        compiler_params=_cp(("parallel", "parallel")))(*ys, *wbs, ur, ur, ur, b_merge, b_merge, b_merge)


def _branch_bwd(dm, prods, ur, b_merge, *, name):
    S = ur.shape[0]
    tm = min(256, S)

    def body(dm_ref, p_ref, g0, g1, g2, b_ref, dp0, dp1, dp2, dgl_ref, db_ref):
        i = pl.program_id(0)
        dmv = dm_ref[...]
        parts = []
        for b, (gr, dp_ref) in enumerate(((g0, dp0), (g1, dp1), (g2, dp2))):
            sl = slice(b * D_MODEL, (b + 1) * D_MODEL)
            gt = _sigmoid(gr[...] + b_ref[:, sl])
            dp_ref[...] = (dmv * gt).astype(BF16)
            dgl = dmv * p_ref[b].astype(F32) * gt * (1.0 - gt)
            dgl_ref[:, R_GL + b * D_MODEL:R_GL + (b + 1) * D_MODEL] = dgl.astype(BF16)
            parts.append(jnp.sum(dgl, axis=0, keepdims=True))
        part = jnp.concatenate(parts, axis=1)

        @pl.when(i == 0)
        def _():
            db_ref[...] = part

        @pl.when(i > 0)
        def _():
            db_ref[...] += part

    gspec = lambda b: pl.BlockSpec((tm, D_MODEL), lambda i: (i, R_GL // D_MODEL + b))
    vec = pl.BlockSpec((1, 3 * D_MODEL), lambda i: (0, 0))
    row = pl.BlockSpec((tm, D_MODEL), lambda i: (i, 0))
    outs = pl.pallas_call(
        body, name=name, grid=(S // tm,),
        in_specs=[row, pl.BlockSpec((3, tm, D_MODEL), lambda i: (0, i, 0)), gspec(0), gspec(1), gspec(2), vec],
        out_specs=[row, row, row, pl.BlockSpec((tm, NR), lambda i: (i, 0)), vec],
        out_shape=[jax.ShapeDtypeStruct((S, D_MODEL), BF16)] * 3
        + [jax.ShapeDtypeStruct((S, NR), BF16), jax.ShapeDtypeStruct((1, 3 * D_MODEL), F32)],
        compiler_params=_cp(("arbitrary",)))(dm, prods, ur, ur, ur, b_merge)
    return outs[0:3], outs[3], outs[4]


def _rope_tables(pos):
    half = ROT // 2
    S = pos.shape[0]
    inv = ROPE_THETA ** (-jnp.arange(half, dtype=F32) / half)
    per_row = LANES // half
    ang = jnp.repeat(pos.astype(F32).reshape(S // per_row, per_row), half, axis=1) * jnp.tile(inv, per_row)
    cos, sin = lax.optimization_barrier((jnp.cos(ang).reshape(S, half), jnp.sin(ang).reshape(S, half)))
    one = jnp.ones((S, LANES - ROT), F32)
    zero = jnp.zeros((S, LANES - ROT), F32)
    zh = jnp.zeros((S, half), F32)
    c = jnp.concatenate([cos, cos, one], axis=1)
    s1 = jnp.concatenate([-sin, zh, zero], axis=1)
    s2 = jnp.concatenate([zh, sin, zero], axis=1)
    return c, s1, s2


def _to_tiles(t):
    S, H = t.shape
    return t.reshape(S // LANES, LANES, H).transpose(0, 2, 1)


def _from_tiles(t):
    nt, H, _ = t.shape
    return t.transpose(1, 0, 2).reshape(H, nt * LANES)


def _local_step(x, mem, pos, tgt, g_pre, g_post, g_mem, wt, bf_pad, b_merge, w_kv, wbs, w_out, pack=None, hs=None):
    S = x.shape[0]
    T = min(512, S)
    nq = S // T
    tabs = _rope_tables(pos)

    if hs is None:
        hs = _rms_fwd(x, g_pre, name="rms_pre", dilations=DIL)
    h = hs[0]
    tabs_g = [[_to_classes(t, d) for t in tabs] for d in DIL]
    qkvs = [_proj_rope(hs[g], wt[f"A{g}"], tabs_g[g], name=f"proj_a{g}") for g in range(3)]
    ub = _mm(h, wt["B"], bt=True, out_dtype=BF16, name="proj_b", tn=1536)
    ur = _mm(h, wt["R"], bt=True, name="proj_r", tn=1792)

    outs_c, lses_c = [], []
    for g in range(3):
        o, l = _attn_a_fwd(qkvs[g], g, name=f"attn_a_fwd{g}")
        outs_c.append(o)
        lses_c.append(l)
    ya = _merge_a_fwd(outs_c, lses_c, ur, name="merge_a_fwd")

    logf = _logf(ur, bf_pad, name="logf")
    c = _from_tiles(_cumsum_lanes(_to_tiles(logf[:, :B_HEADS]), False, name="cumsum_fwd"))
    qaug, kaug = _fox_aug(ub, c, name="fox_aug")
    kt = ub[:, 512:1024].reshape(nq, T, 512).transpose(0, 2, 1)
    vt = ub[:, 1024:1536].reshape(nq, T, 512).transpose(0, 2, 1)
    ob, lse_b = _fox_fwd(qaug, kaug, vt, name="fox_fwd")
    yb = _gate_fwd(ob, ur, R_ZB, name="gate_b_fwd")

    hm = _rms_fwd(mem, g_mem, name="rms_mem")
    mkv = _mm(hm, w_kv, name="proj_mem")
    ym = _mem_fwd(ur, mkv, name="mem_fwd")

    merged, prods = _branch_fwd((ya, yb, ym), wbs, ur, b_merge, name="branch_fwd")
    out = _mm(merged, w_out, name="proj_out")
    dy, d_out, dg_post, loss_row = _post(x, out, tgt, g_post, name="post")

    dmerged = _mm(d_out, w_out, bt=True, name="d_merged")
    dw_out = _mm(merged, d_out, at=True, name="dw_out", tk=2048)
    dprods, du_r, db_merge = _branch_bwd(dmerged, prods, ur, b_merge, name="branch_bwd")
    dys, dwbs = [], []
    for i, (y, wb) in enumerate(zip((ya, yb, ym), wbs)):
        dys.append(_mm(dprods[i], wb, bt=True, name=f"d_y{i}"))
        dwbs.append(_mm(y, dprods[i], at=True, name=f"dw_branch{i}", tk=2048))

    dos_c, adjs_c, du_r = _merge_a_bwd(outs_c, lses_c, ur, dys[0], du_r, name="merge_a_bwd")
    dus_a = []
    for g, d in enumerate(DIL):
        do_c, adj_c = dos_c[g], adjs_c[g]
        du = _attn_a_dkv(qkvs[g], tabs_g[g], g, do_c, lses_c[g], adj_c, name=f"attn_a_dkv{g}")
        dus_a.append(_attn_a_dq(qkvs[g], tabs_g[g], g, do_c, lses_c[g], adj_c, du, name=f"attn_a_dq{g}"))

    dob, du_r = _gate_bwd(ob, ur, R_ZB, dys[1], du_r, name="gate_b_bwd")
    delta_b = _fox_delta(ob, dob, name="fox_delta")
    dkb, dvb, dc_k, dqt, dc_q = _fox_bwd(ub, qaug, kaug, kt, dob, lse_b, delta_b, name="fox_bwd")
    dqb = (dqt.transpose(0, 2, 1).reshape(S, A_WIDTH) * B_SCALE).astype(BF16)
    du_b = jnp.concatenate([dqb, dkb, dvb], axis=1)
    dc = dc_q.reshape(B_HEADS, S) + dc_k.reshape(B_HEADS, S)
    dlogf = _from_tiles(_cumsum_lanes(_to_tiles(dc.T), True, name="cumsum_bwd"))
    dlogf_pad = jnp.pad(dlogf.T, ((0, 0), (0, FB_PAD - B_HEADS)))
    du_r, db_forget = _dfb(ur, bf_pad, dlogf_pad, du_r, name="dfb")

    du_r, dmk, dmv = _mem_bwd(ur, mkv, dys[2], du_r, name="mem_bwd")
    dmkv = jnp.concatenate([dmk, dmv], axis=1).astype(BF16)
    dhm = _mm(dmkv, w_kv, bt=True, name="d_hm")
    dw_kv = _mm(hm, dmkv, at=True, name="dw_kv")
    dg_mem = _rms_bwd(mem, g_mem, dhm, None, name="rms_mem_bwd")

    dwt ={"R": _mm(du_r, h, at=True, name="dw_in_r", tm=1792, tk=1024),
           "B": _mm(du_b, h, at=True, name="dw_in_b", tm=1536, tk=2048)}
    for g in range(3):
        dwt[f"A{g}"] = _mm(dus_a[g], hs[g], at=True, name=f"dw_in_a{g}", tm=1536, tk=2048)
    res = dict(dwt=dwt, dw_kv=dw_kv, dwbs=dwbs, dw_out=dw_out)
    token_major = [(du_r, wt["R"]), (du_b, wt["B"]), (dus_a[0], wt["A0"])]
    if pack is None:
        dh_1 = _mm(dus_a[1], wt["A1"], name="d_h_a1", tk=1536)
        dh_2 = _mm(dus_a[2], wt["A2"], name="d_h_a2", tk=1536)
        dh = _mm_sum(token_major, name="d_h_main")
    else:
        gbig = pack(dwt, dw_kv, dwbs, dw_out)
        own_idx = _own_slabs()
        dh_1, sib = _mm(dus_a[1], wt["A1"], name="d_h_a1", tk=1536, comm=_pair_comm(gbig, (0, 1)))
        dh_2, sib = _mm(dus_a[2], wt["A2"], name="d_h_a2", tk=1536, comm=_pair_comm(gbig, (2, 3), sib))
        send = _pair_sum(gbig, sib, own_idx, 624, name="pair_sum")
        dh, recv = _mm_sum(token_major, name="d_h_main", comm=_chips_comm(send))
        res = dict(parts=[(gbig, None), (sib, 1), (recv, N_CHIP - 1)], own_idx=own_idx)
    grad_x, dg_pre = _rms_bwd(x, g_pre, dh, dy, name="rms_pre_bwd", dh_classes=[(dh_1, DIL[1]), (dh_2, DIL[2])])

    return dict(res, loss=loss_row, grad_x=grad_x, dg_pre=dg_pre, dg_post=dg_post, dg_mem=dg_mem,
                db_forget=db_forget, db_merge=db_merge)


MESH = pl.DeviceIdType.MESH
ANY = pl.BlockSpec(memory_space=pl.ANY)


def _relations():
    return [(k >> 2 & 1, k >> 1 & 1, k & 1) for k in range(1, N_DEV)]


def _coords():
    return lax.axis_index("x"), lax.axis_index("y"), lax.axis_index("c")


def _gather_comm(shard):
    R, W = shard.shape

    def plan(x_ref, out_ref, send_sems, recv_sems, local_sem):
        x, y, c = _coords()
        me, sibling = (x, y, c), (x, y, 1 - c)
        chips = [(1 - x, y), (x, 1 - y), (1 - x, 1 - y)]

        def slot(px, py, pc):
            return out_ref.at[4 * px + 2 * py + pc]

        def copy(k, block, to, src=None):
            return pltpu.make_async_remote_copy(
                src_ref=slot(*block) if src is None else src, dst_ref=slot(*block),
                send_sem=send_sems.at[k], recv_sem=recv_sems.at[k], device_id=to, device_id_type=MESH)

        mine = pltpu.make_async_copy(x_ref, slot(*me), local_sem)
        first = [copy(0, me, sibling, src=x_ref)]
        first += [copy(1 + j, me, (*chip, c), src=x_ref) for j, chip in enumerate(chips)]
        return me, sibling, chips, c, copy, mine, first

    def start(*refs):
        _, _, _, _, _, mine, first = plan(*refs)
        mine.start()
        for cp in first:
            cp.start()

    def wait(*refs):
        me, sibling, chips, c, copy, mine, first = plan(*refs)
        passed = [copy(4 + j, (*chip, c), sibling) for j, chip in enumerate(chips)]
        for j, chip in enumerate(chips):
            copy(1 + j, (*chip, c), me).wait_recv()
            passed[j].start()
        copy(0, sibling, me).wait_recv()
        for j, chip in enumerate(chips):
            copy(4 + j, (*chip, 1 - c), me).wait_recv()
        for cp in first + passed:
            cp.wait_send()
        mine.wait()

    return dict(inputs=[shard], out_shape=[jax.ShapeDtypeStruct((N_DEV, R, W), shard.dtype)],
                sems=[pltpu.SemaphoreType.DMA((N_DEV - 1,)), pltpu.SemaphoreType.DMA((N_DEV - 1,)),
                      pltpu.SemaphoreType.DMA],
                start=start, wait=wait)


N_CHIP = 4


def _pair_comm(gbig, rels, sib=None):
    _, R, W = gbig.shape

    def copies(g_ref, *rest):
        sib_ref, send_sems, recv_sems = rest[-3:]
        x, y, c = _coords()
        return [pltpu.make_async_remote_copy(
            src_ref=g_ref.at[4 * (x ^ (r >> 1)) + 2 * (y ^ (r & 1)) + (1 - c)], dst_ref=sib_ref.at[r],
            send_sem=send_sems.at[k], recv_sem=recv_sems.at[k], device_id=(x, y, 1 - c), device_id_type=MESH)
            for k, r in enumerate(rels)]

    def start(*refs):
        for cp in copies(*refs):
            cp.start()

    def wait(*refs):
        cps = copies(*refs)
        for cp in cps:
            cp.wait_recv()
        for cp in cps:
            cp.wait_send()

    return dict(inputs=[gbig] if sib is None else [gbig, sib],
                out_shape=[jax.ShapeDtypeStruct((N_CHIP, R, W), gbig.dtype)],
                alias={} if sib is None else {1: 0},
                sems=[pltpu.SemaphoreType.DMA((len(rels),)), pltpu.SemaphoreType.DMA((len(rels),))],
                start=start, wait=wait)


def _own_slabs():
    x, y, c = _coords()
    return jnp.stack([4 * (x ^ (r >> 1)) + 2 * (y ^ (r & 1)) + c for r in range(N_CHIP)]).astype(jnp.int32)


def _pair_sum(gbig, sib, own_idx, tr, *, name):
    _, R, W = gbig.shape

    def body(idx_ref, a_ref, b_ref, o_ref):
        o_ref[...] = (a_ref[...] + b_ref[...]).astype(BF16)

    return pl.pallas_call(
        body, name=name,
        grid_spec=pltpu.PrefetchScalarGridSpec(
            num_scalar_prefetch=1, grid=(N_CHIP - 1, R // tr),
            in_specs=[pl.BlockSpec((None, tr, W), lambda r, i, idx: (idx[r + 1], i, 0)),
                      pl.BlockSpec((None, tr, W), lambda r, i, idx: (r + 1, i, 0))],
            out_specs=pl.BlockSpec((None, tr, W), lambda r, i, idx: (r, i, 0))),
        out_shape=jax.ShapeDtypeStruct((N_CHIP - 1, R, W), BF16),
        compiler_params=_cp(("parallel", "parallel")))(own_idx, gbig, sib)


def _chips_comm(send):
    nb, R, W = send.shape

    def copies(b_ref, rb_ref, send_sems, recv_sems):
        x, y, c = _coords()
        return [pltpu.make_async_remote_copy(
            src_ref=b_ref.at[r - 1], dst_ref=rb_ref.at[r - 1], send_sem=send_sems.at[r - 1],
            recv_sem=recv_sems.at[r - 1], device_id=(x ^ (r >> 1), y ^ (r & 1), c), device_id_type=MESH)
            for r in range(1, N_CHIP)]

    def start(*refs):
        for cp in copies(*refs):
            cp.start()

    def wait(*refs):
        cps = copies(*refs)
        for cp in cps:
            cp.wait_recv()
        for cp in cps:
            cp.wait_send()

    return dict(inputs=[send], out_shape=[jax.ShapeDtypeStruct((nb, R, W), send.dtype)],
                sems=[pltpu.SemaphoreType.DMA((nb,)), pltpu.SemaphoreType.DMA((nb,))],
                start=start, wait=wait)


def _gather_small(gsmall, *, name):
    n = N_DEV - 1

    def body(s_ref, rs_ref, send_sems, recv_sems, local_sem):
        x, y, c = _coords()
        me = 4 * x + 2 * y + c
        mine = pltpu.make_async_copy(s_ref, rs_ref.at[me], local_sem)
        mine.start()

        def copy(k, fx, fy, fc, slot):
            return pltpu.make_async_remote_copy(
                src_ref=s_ref, dst_ref=rs_ref.at[slot], send_sem=send_sems.at[k], recv_sem=recv_sems.at[k],
                device_id=(x ^ fx, y ^ fy, c ^ fc), device_id_type=MESH)

        started = [copy(k, *rel, me) for k, rel in enumerate(_relations())]
        for cp in started:
            cp.start()
        for k, (fx, fy, fc) in enumerate(_relations()):
            copy(k, fx, fy, fc, 4 * (x ^ fx) + 2 * (y ^ fy) + (c ^ fc)).wait_recv()
        for cp in started:
            cp.wait_send()
        mine.wait()

    return pl.pallas_call(
        body, name=name, out_shape=jax.ShapeDtypeStruct((N_DEV, 1, P_SMALL), gsmall.dtype),
        in_specs=[ANY], out_specs=ANY,
        scratch_shapes=[pltpu.SemaphoreType.DMA((n,)), pltpu.SemaphoreType.DMA((n,)), pltpu.SemaphoreType.DMA],
    )(gsmall)


def _part_specs(parts, tr, row0):
    assert row0 % tr == 0
    specs = []
    for a, n_used in parts:
        if n_used is None:
            specs.append(pl.BlockSpec((1, tr, a.shape[2]), lambda i, idx: (idx[0], row0 // tr + i, 0)))
        else:
            specs.append(pl.BlockSpec((n_used, tr, a.shape[2]), lambda i, idx: (0, row0 // tr + i, 0)))
    return specs


def _part_total(refs, parts):
    g = None
    for ref, (_, n_used) in zip(refs, parts):
        for k in range(n_used or 1):
            t = ref[k].astype(F32)
            g = t if g is None else g + t
    return g


def _sum_parts(parts, idx, row0, nrows, tr, *, name):
    W = parts[0][0].shape[2]
    assert nrows % tr == 0

    def body(idx_ref, *refs):
        refs[-1][...] = _part_total(refs[:-1], parts)

    return pl.pallas_call(
        body, name=name,
        grid_spec=pltpu.PrefetchScalarGridSpec(
            num_scalar_prefetch=1, grid=(nrows // tr,), in_specs=_part_specs(parts, tr, row0),
            out_specs=pl.BlockSpec((tr, W), lambda i, idx: (i, 0))),
        out_shape=jax.ShapeDtypeStruct((nrows, W), F32),
        compiler_params=_cp(("parallel",)))(idx, *[a for a, _ in parts])


def _adamw(parts, idx, w, m, v, tr, *, name):
    R, W = w.shape
    assert R % tr == 0
    np_ = len(parts)

    def body(idx_ref, *refs):
        w_ref, m_ref, v_ref, g_ref, d_ref, nm_ref, nv_ref = refs[np_:]
        g = _part_total(refs[:np_], parts)
        mm = ADAM_B1 * m_ref[...] + (1.0 - ADAM_B1) * g
        vv = ADAM_B2 * v_ref[...] + (1.0 - ADAM_B2) * (g * g)
        m_hat = mm / (1.0 - ADAM_B1 ** ADAM_STEP)
        v_hat = vv / (1.0 - ADAM_B2 ** ADAM_STEP)
        g_ref[...] = g
        d_ref[...] = -ADAM_LR * (m_hat / (jnp.sqrt(v_hat) + ADAM_EPS) + ADAM_WD * w_ref[...])
        nm_ref[...] = mm
        nv_ref[...] = vv

    blk = pl.BlockSpec((tr, W), lambda i, idx: (i, 0))
    return pl.pallas_call(
        body, name=name,
        grid_spec=pltpu.PrefetchScalarGridSpec(
            num_scalar_prefetch=1, grid=(R // tr,), in_specs=_part_specs(parts, tr, 0) + [blk, blk, blk],
            out_specs=[blk] * 4),
        out_shape=[jax.ShapeDtypeStruct((R, W), F32)] * 4,
        compiler_params=_cp(("parallel",)))(idx, *[a for a, _ in parts], w, m, v)


def _pack_rest(w_kv, wa, wb, wm, w_out):
    return jnp.concatenate([w_kv[0], w_out[0]] + [t[0].reshape(-1, D_MODEL) for t in (wa, wb, wm)], axis=0)


def _unpack_rest(t):
    br = lambda i: t[RO_BR + 64 * i:RO_BR + 64 * (i + 1)].reshape(1, A_WIDTH, D_MODEL // N_DEV)
    return t[None, RO_KV:RO_OUT], br(0), br(1), br(2), t[None, RO_OUT:RO_BR]


def _orig_rows(gathered, a, b):
    res = []
    while a < b:
        dev, r = divmod(a, CS)
        n = min(b - a, CS - r)
        res.append(gathered[dev, RO_IN + r:RO_IN + r + n])
        a += n
    return res


def _full_weights(gathered):
    wt = {}
    for name, ranges in SEGS.items():
        rows = [p for a, b in ranges for p in _orig_rows(gathered, a, b)]
        if SEG_PAD[name]:
            rows.append(jnp.zeros((SEG_PAD[name], D_MODEL), gathered.dtype))
        wt[name] = jnp.concatenate(rows, axis=0)
    w_kv = gathered[:, RO_KV:RO_OUT].reshape(D_MODEL, D_MODEL)
    w_out = gathered[:, RO_OUT:RO_BR].reshape(D_MODEL, D_MODEL)
    wbs = [gathered[:, RO_BR + 64 * i:RO_BR + 64 * (i + 1)].reshape(N_DEV, A_WIDTH, D_MODEL // N_DEV)
           .transpose(1, 0, 2).reshape(A_WIDTH, D_MODEL) for i in range(3)]
    return wt, w_kv, wbs, w_out


def _orig_order(dwt):
    pieces = []
    for name, ranges in SEGS.items():
        o = 0
        for a, b in ranges:
            pieces.append((a, dwt[name][o:o + b - a]))
            o += b - a
    pieces.sort(key=lambda p: p[0])
    return jnp.concatenate([p[1] for p in pieces], axis=0)


def _pack_grads(dwt, dw_kv, dwbs, dw_out):
    g_in = jnp.pad(_orig_order(dwt).reshape(N_DEV, CS, D_MODEL), ((0, 0), (0, IN_ROWS - CS), (0, 0)))
    br = [t.reshape(A_WIDTH, N_DEV, D_MODEL // N_DEV).transpose(1, 0, 2).reshape(N_DEV, -1, D_MODEL) for t in dwbs]
    return jnp.concatenate([dw_kv.reshape(N_DEV, -1, D_MODEL), dw_out.reshape(N_DEV, -1, D_MODEL)] + br + [g_in],
                           axis=1)


def kernel(x, mem, positions, norm_pre_g, norm_post_g, norm_mem_g, w_in, b_forget, b_merge, w_mem_kv, w_branch_a, w_branch_b, w_branch_m, w_out, loss_target, m_norm_pre_g, m_norm_post_g, m_norm_mem_g, m_w_in, m_b_forget, m_b_merge, m_w_mem_kv, m_w_branch_a, m_w_branch_b, m_w_branch_m, m_w_out, v_norm_pre_g, v_norm_post_g, v_norm_mem_g, v_w_in, v_b_forget, v_b_merge, v_w_mem_kv, v_w_branch_a, v_w_branch_b, v_w_branch_m, v_w_out):
    w_rest = _pack_rest(w_mem_kv, w_branch_a, w_branch_b, w_branch_m, w_out)
    shard = jnp.concatenate([w_rest.astype(BF16), w_in[0].T.astype(BF16),
                             jnp.zeros((IN_ROWS - CS, D_MODEL), BF16)], axis=0)
    hs, (gathered,) = _rms_fwd(x[0], norm_pre_g, name="rms_pre_gather", dilations=DIL, comm=_gather_comm(shard))
    wt, w_kv, wbs, w_o = _full_weights(gathered)

    bf_pad = jnp.pad(b_forget, ((0, 0), (0, FB_PAD - B_HEADS)))
    r = _local_step(x[0], mem[0], positions[0], loss_target[0], norm_pre_g, norm_post_g, norm_mem_g,
                    wt, bf_pad, b_merge, w_kv, wbs, w_o, pack=_pack_grads, hs=hs)

    gsmall = jnp.concatenate([r["dg_pre"], r["dg_post"], r["dg_mem"], r["db_merge"],
                              r["db_forget"][:, :LANES], r["loss"]], axis=1)
    rsmall = _gather_small(gsmall, name="gather_small")
    parts, own_idx = r["parts"], r["own_idx"]

    m_rest = _pack_rest(m_w_mem_kv, m_w_branch_a, m_w_branch_b, m_w_branch_m, m_w_out)
    v_rest = _pack_rest(v_w_mem_kv, v_w_branch_a, v_w_branch_b, v_w_branch_m, v_w_out)
    gsum = _sum_parts(parts, own_idx, 0, ROWS, 624, name="sum_grads")
    outs_rest = [_unpack_rest(t) for t in
                 _adamw([(gsum[None], 1)], own_idx, w_rest, m_rest, v_rest, 64, name="adamw_rest")]
    g_in = gsum[RO_IN:RO_IN + CS].T
    outs_in = _adamw([(g_in[None], 1)], own_idx, w_in[0], m_w_in[0], v_w_in[0], 128, name="adamw_w_in")

    def small_vec(a, b, c, d, e):
        z = jnp.zeros((1, LANES - B_HEADS), F32)
        return jnp.concatenate([a, b, c, d, e, z, jnp.zeros((1, LANES), F32)], axis=1)

    outs_small = _adamw([(rsmall, N_DEV)], own_idx, small_vec(norm_pre_g, norm_post_g, norm_mem_g, b_merge, b_forget),
                        small_vec(m_norm_pre_g, m_norm_post_g, m_norm_mem_g, m_b_merge, m_b_forget),
                        small_vec(v_norm_pre_g, v_norm_post_g, v_norm_mem_g, v_b_merge, v_b_forget),
                        1, name="adamw_small")

    def small_parts(t):
        return [t[:, O_GPRE:O_GPRE + D_MODEL], t[:, O_GPOST:O_GPOST + D_MODEL], t[:, O_GMEM:O_GMEM + D_MODEL],
                t[:, O_BF:O_BF + B_HEADS], t[:, O_BM:O_BM + 3 * D_MODEL]]

    loss = outs_small[0][0, O_LOSS]
    result = [loss, r["grad_x"][None]]
    for rest, w_i, small in zip(outs_rest, outs_in, outs_small):
        gp, gq, gm, bf, bm = small_parts(small)
        w_k, w_a, w_b, w_m, w_ot = rest
        result += [gp, gq, gm, w_i[None], bf, bm, w_k, w_a, w_b, w_m, w_ot]
    return tuple(result)
```

```python
import jax
import jax.numpy as jnp
from jax import lax
from jax.experimental import pallas as pl
from jax.experimental.pallas import tpu as pltpu

F32 = jnp.float32
BF16 = jnp.bfloat16

N_DEV = 8
D_MODEL = 1024
N_MEM = 256
EPS = 1e-6
NEG = -1e30
ROPE_THETA = 500000.0
DIL = (1, 4, 16)
A_HEADS = 4
HEAD = 128
A_WIDTH = 512
B_HEADS = 8
B_HEAD = 64
M_HEADS = 4
ROT = 32
IN_COLS = 11272
FB_PAD = 256

SEGS = {
    "A0": ((0, 512), (1536, 2048), (3072, 3584)),
    "A1": ((512, 1024), (2048, 2560), (3584, 4096)),
    "A2": ((1024, 1536), (2560, 3072), (4096, 4608)),
    "B": ((5120, 6656),),
    "R": ((4608, 5120), (6664, 7176), (7176, 7688), (7688, 8200), (8200, 11272), (6656, 6664)),
}
SEG_PAD = {"A0": 0, "A1": 0, "A2": 0, "B": 0, "R": FB_PAD - B_HEADS}
R_ZA, R_ZB, R_QM, R_ZM, R_GL, R_FB = 0, 512, 1024, 1536, 2048, 5120
NR = R_FB + FB_PAD

ADAM_LR, ADAM_B1, ADAM_B2, ADAM_EPS, ADAM_WD, ADAM_STEP = 0.001, 0.9, 0.999, 1e-08, 0.01, 10

LANES = 128
VMEM_LIMIT = 56 * 1024 * 1024

CS = IN_COLS // N_DEV
RO_KV, RO_OUT, RO_BR, RO_IN = 0, 128, 256, 448
IN_ROWS = 1424
ROWS = RO_IN + IN_ROWS
O_GPRE, O_GPOST, O_GMEM, O_BM, O_BF, O_LOSS = 0, 1024, 2048, 3072, 6144, 6272
P_SMALL = 6400


def _cp(sem=None):
    return pltpu.CompilerParams(dimension_semantics=sem, vmem_limit_bytes=VMEM_LIMIT)


def _dot(a, b):
    return jnp.dot(a, b, preferred_element_type=F32)


def _dot_nt(a, b):
    return lax.dot_general(a, b, (((1,), (1,)), ((), ())), preferred_element_type=F32)


def _sigmoid(z):
    return 1.0 / (1.0 + jnp.exp(-z))


def _mm(a, b, *, name, at=False, bt=False, out_dtype=F32, tm=1024, tn=1024, tk=None, comm=None):
    assert not (at and bt)
    K, M = a.shape if at else a.shape[::-1]
    N = b.shape[0] if bt else b.shape[1]
    tm, tn = min(tm, M), min(tn, N)
    tk = K if tk is None else min(tk, K)
    assert M % tm == 0 and N % tn == 0 and K % tk == 0
    nk = K // tk
    grid = (M // tm, N // tn, nk)
    n_in = len(comm["inputs"]) if comm else 0
    n_out = len(comm["out_shape"]) if comm else 0

    def body(a_ref, b_ref, *rest):
        c_in, o_ref, c_out = rest[:n_in], rest[n_in], rest[n_in + 1:n_in + 1 + n_out]
        acc_ref, sems = rest[n_in + 1 + n_out], rest[n_in + 2 + n_out:]
        if comm:
            step = (pl.program_id(0) * grid[1] + pl.program_id(1)) * grid[2] + pl.program_id(2)

            @pl.when(step == 0)
            def _():
                comm["start"](*c_in, *c_out, *sems)

        av = a_ref[...].astype(BF16)
        bv = b_ref[...].astype(BF16)
        if at:
            p = lax.dot_general(av, bv, (((0,), (0,)), ((), ())), preferred_element_type=F32)
        else:
            p = _dot_nt(av, bv) if bt else _dot(av, bv)
        if nk == 1:
            o_ref[...] = p.astype(out_dtype)
        else:
            k = pl.program_id(2)

            @pl.when(k == 0)
            def _():
                acc_ref[...] = p

            @pl.when(k > 0)
            def _():
                acc_ref[...] += p

            @pl.when(k == nk - 1)
            def _():
                o_ref[...] = acc_ref[...].astype(out_dtype)

        if comm:
            @pl.when(step == grid[0] * grid[1] * grid[2] - 1)
            def _():
                comm["wait"](*c_in, *c_out, *sems)

    b_spec = (pl.BlockSpec((tn, tk), lambda i, j, k: (j, k)) if bt
              else pl.BlockSpec((tk, tn), lambda i, j, k: (k, j)))
    a_spec = (pl.BlockSpec((tk, tm), lambda i, j, k: (k, i)) if at
              else pl.BlockSpec((tm, tk), lambda i, j, k: (i, k)))
    out_spec = pl.BlockSpec((tm, tn), lambda i, j, k: (i, j))
    out_shape = jax.ShapeDtypeStruct((M, N), out_dtype)
    acc = pltpu.VMEM((tm, tn) if nk > 1 else (8, LANES), F32)
    if not comm:
        return pl.pallas_call(
            body, name=name, grid=grid, in_specs=[a_spec, b_spec], out_specs=out_spec, out_shape=out_shape,
            scratch_shapes=[acc], compiler_params=_cp(("parallel", "parallel", "arbitrary")))(a, b)
    return pl.pallas_call(
        body, name=name, grid=grid, in_specs=[a_spec, b_spec] + [ANY] * n_in,
        out_specs=[out_spec] + [ANY] * n_out, out_shape=[out_shape] + comm["out_shape"],
        input_output_aliases={2 + i: 1 + o for i, o in comm.get("alias", {}).items()},
        scratch_shapes=[acc] + comm["sems"],
        compiler_params=_cp(("arbitrary", "arbitrary", "arbitrary")))(a, b, *comm["inputs"])


def _mm_sum(pairs, *, name, tm=1024, tk=768, comm=None):
    M, N = pairs[0][0].shape[0], pairs[0][1].shape[1]
    tm = min(tm, M)
    steps = [a.shape[1] // tk for a, _ in pairs]
    assert M % tm == 0 and all(a.shape[1] % tk == 0 for a, _ in pairs)
    first = [sum(steps[:p]) for p in range(len(pairs))]
    total = sum(steps)
    grid = (M // tm, total)
    n_in = len(comm["inputs"]) if comm else 0
    n_out = len(comm["out_shape"]) if comm else 0
    npair = len(pairs)

    def body(*refs):
        ab, rest = refs[:2 * npair], refs[2 * npair:]
        c_in, o_ref, c_out = rest[:n_in], rest[n_in], rest[n_in + 1:n_in + 1 + n_out]
        acc_ref, sems = rest[n_in + 1 + n_out], rest[n_in + 2 + n_out:]
        k = pl.program_id(1)
        if comm:
            step = pl.program_id(0) * total + k

            @pl.when(step == 0)
            def _():
                comm["start"](*c_in, *c_out, *sems)

        @pl.when(k == 0)
        def _():
            acc_ref[...] = jnp.zeros((tm, N), F32)

        for p in range(npair):
            @pl.when(jnp.logical_and(k >= first[p], k < first[p] + steps[p]))
            def _(p=p):
                acc_ref[...] += _dot(ab[2 * p][...], ab[2 * p + 1][...])

        @pl.when(k == total - 1)
        def _():
            o_ref[...] = acc_ref[...]

        if comm:
            @pl.when(step == grid[0] * total - 1)
            def _():
                comm["wait"](*c_in, *c_out, *sems)

    def local(p):
        return lambda k: jnp.clip(k - first[p], 0, steps[p] - 1)

    in_specs = []
    for p in range(npair):
        in_specs += [pl.BlockSpec((tm, tk), lambda i, k, f=local(p): (i, f(k))),
                     pl.BlockSpec((tk, N), lambda i, k, f=local(p): (f(k), 0))]
    out_spec = pl.BlockSpec((tm, N), lambda i, k: (i, 0))
    out_shape = jax.ShapeDtypeStruct((M, N), F32)
    args = [t for pair in pairs for t in pair]
    if not comm:
        return pl.pallas_call(
            body, name=name, grid=grid, in_specs=in_specs, out_specs=out_spec, out_shape=out_shape,
            scratch_shapes=[pltpu.VMEM((tm, N), F32)], compiler_params=_cp(("parallel", "arbitrary")))(*args)
    return pl.pallas_call(
        body, name=name, grid=grid, in_specs=in_specs + [ANY] * n_in,
        out_specs=[out_spec] + [ANY] * n_out, out_shape=[out_shape] + comm["out_shape"],
        scratch_shapes=[pltpu.VMEM((tm, N), F32)] + comm["sems"],
        compiler_params=_cp(("arbitrary", "arbitrary")))(*args, *comm["inputs"])


def _class_spec(S, d, tm, width):
    return pl.BlockSpec((d, tm // d, width), lambda i: (0, i, 0))


def _rms_fwd(x, g, *, name, dilations=(), comm=None):
    S, D = x.shape
    tm = min(512, S)
    ds = [d for d in dilations if d > 1]
    nsteps = S // tm
    n_in = len(comm["inputs"]) if comm else 0
    n_out = len(comm["out_shape"]) if comm else 0
    n_tmp = D // LANES if ds else 0

    def body(x_ref, g_ref, *rest):
        c_in, o_ref, rest = rest[:n_in], rest[n_in], rest[n_in + 1:]
        cls, c_out, rest = rest[:len(ds)], rest[len(ds):len(ds) + n_out], rest[len(ds) + n_out:]
        tmps, sems = rest[:n_tmp], rest[n_tmp:]
        if comm:
            @pl.when(pl.program_id(0) == 0)
            def _():
                comm["start"](*c_in, *c_out, *sems)

        xv = x_ref[...]
        r = lax.rsqrt(jnp.mean(xv * xv, axis=-1, keepdims=True) + EPS)
        hv = xv * r * g_ref[...]
        o_ref[...] = hv.astype(BF16)
        if ds:
            for c, tmp in enumerate(tmps):
                tmp[...] = hv[:, c * LANES:(c + 1) * LANES]
            for c_ref, d in zip(cls, ds):
                for k in range(d):
                    c_ref[k] = jnp.concatenate([tmp[pl.ds(k, tm // d, stride=d), :] for tmp in tmps],
                                               axis=1).astype(BF16)
        if comm:
            @pl.when(pl.program_id(0) == nsteps - 1)
            def _():
                comm["wait"](*c_in, *c_out, *sems)

    row = pl.BlockSpec((tm, D), lambda i: (i, 0))
    outs = pl.pallas_call(
        body, name=name, grid=(nsteps,),
        in_specs=[row, pl.BlockSpec((1, D), lambda i: (0, 0))] + [ANY] * n_in,
        out_specs=[row] + [_class_spec(S, d, tm, D) for d in ds] + [ANY] * n_out,
        out_shape=[jax.ShapeDtypeStruct((S, D), BF16)] + [jax.ShapeDtypeStruct((d, S // d, D), BF16) for d in ds]
        + (comm["out_shape"] if comm else []),
        scratch_shapes=[pltpu.VMEM((tm, LANES), F32)] * n_tmp + (comm["sems"] if comm else []),
        compiler_params=_cp(("arbitrary",) if comm else ("parallel",)),
    )(x, g, *(comm["inputs"] if comm else []))
    rows = [outs[0]] + [o.reshape(S, D) for o in outs[1:1 + len(ds)]]
    if comm:
        return rows, list(outs[1 + len(ds):])
    return rows if ds else rows[0]


def _rms_bwd(x, g, dh, dy, *, name, dh_classes=()):
    S, D = x.shape
    tm = min(512, S)
    want_dx = dy is not None
    nc = len(dh_classes)

    def body(*refs):
        c_refs, refs = refs[:nc], refs[nc:]
        if want_dx:
            x_ref, g_ref, dh_ref, dy_ref, dx_ref, dg_ref = refs[:6]
        else:
            x_ref, g_ref, dh_ref, dg_ref = refs[:4]
        i = pl.program_id(0)
        xv = x_ref[...]
        r = lax.rsqrt(jnp.mean(xv * xv, axis=-1, keepdims=True) + EPS)
        xh = xv * r
        if nc:
            tmps = refs[-(D // LANES):]
            cols = [slice(c * LANES, (c + 1) * LANES) for c in range(D // LANES)]
            for tmp, cs in zip(tmps, cols):
                tmp[...] = dh_ref[:, cs]
            for c_ref, (_, d) in zip(c_refs, dh_classes):
                for k in range(d):
                    for tmp, cs in zip(tmps, cols):
                        tmp[pl.ds(k, tm // d, stride=d), :] += c_ref[k, :, cs]
            dhv = jnp.concatenate([tmp[...] for tmp in tmps], axis=1)
        else:
            dhv = dh_ref[...]
        part = jnp.sum(dhv * xh, axis=0, keepdims=True)

        @pl.when(i == 0)
        def _():
            dg_ref[...] = part

        @pl.when(i > 0)
        def _():
            dg_ref[...] += part

        if want_dx:
            dxh = dhv * g_ref[...]
            dx_ref[...] = dy_ref[...] + r * (dxh - xh * jnp.mean(dxh * xh, axis=-1, keepdims=True))

    row = pl.BlockSpec((tm, D), lambda i: (i, 0))
    vec = pl.BlockSpec((1, D), lambda i: (0, 0))
    c_specs = [_class_spec(S, d, tm, D) for _, d in dh_classes]
    c_args = [a.reshape(d, S // d, D) for a, d in dh_classes]
    scratch = [pltpu.VMEM((tm, LANES), F32)] * (D // LANES) if nc else []
    if want_dx:
        return pl.pallas_call(
            body, name=name, grid=(S // tm,), in_specs=c_specs + [row, vec, row, row], out_specs=[row, vec],
            out_shape=[jax.ShapeDtypeStruct((S, D), F32), jax.ShapeDtypeStruct((1, D), F32)],
            scratch_shapes=scratch, compiler_params=_cp(("arbitrary",)))(*c_args, x, g, dh, dy)
    return pl.pallas_call(
        body, name=name, grid=(S // tm,), in_specs=c_specs + [row, vec, row], out_specs=vec,
        out_shape=jax.ShapeDtypeStruct((1, D), F32),
        scratch_shapes=scratch, compiler_params=_cp(("arbitrary",)))(*c_args, x, g, dh)


def _post(x, out, tgt, g, *, name):
    S, D = x.shape
    tm = min(512, S)

    def body(x_ref, o_ref, t_ref, g_ref, dy_ref, do_ref, dg_ref, loss_ref):
        i = pl.program_id(0)
        ov = o_ref[...]
        r = lax.rsqrt(jnp.mean(ov * ov, axis=-1, keepdims=True) + EPS)
        n = ov * r
        gv = g_ref[...]
        e = (x_ref[...] + n * gv) - t_ref[...]
        lpart = 0.5 * jnp.sum(jnp.mean(e * e, axis=-1, keepdims=True), axis=0, keepdims=True)
        dy = e * (1.0 / D)
        dy_ref[...] = dy
        dn = dy * gv
        do_ref[...] = (r * (dn - n * jnp.mean(dn * n, axis=-1, keepdims=True))).astype(BF16)
        gpart = jnp.sum(dy * n, axis=0, keepdims=True)
        lrow = jnp.broadcast_to(lpart, (1, LANES))

        @pl.when(i == 0)
        def _():
            dg_ref[...] = gpart
            loss_ref[...] = lrow

        @pl.when(i > 0)
        def _():
            dg_ref[...] += gpart
            loss_ref[...] += lrow

    row = pl.BlockSpec((tm, D), lambda i: (i, 0))
    vec = pl.BlockSpec((1, D), lambda i: (0, 0))
    return pl.pallas_call(
        body, name=name, grid=(S // tm,), in_specs=[row, row, row, vec],
        out_specs=[row, row, vec, pl.BlockSpec((1, LANES), lambda i: (0, 0))],
        out_shape=[jax.ShapeDtypeStruct((S, D), F32), jax.ShapeDtypeStruct((S, D), BF16),
                   jax.ShapeDtypeStruct((1, D), F32), jax.ShapeDtypeStruct((1, LANES), F32)],
        compiler_params=_cp(("arbitrary",)))(x, out, tgt, g)


def _to_classes(t, d):
    if d == 1:
        return t
    S, C = t.shape
    return t.reshape(S // d, d, C).transpose(1, 0, 2).reshape(S, C)


def _rope(x, c, s1, s2):
    return x * c + pltpu.roll(x, LANES - ROT // 2, 1) * s1 + pltpu.roll(x, ROT // 2, 1) * s2


def _unrope(d, c, s1, s2):
    return d * c + pltpu.roll(d * s1, ROT // 2, 1) + pltpu.roll(d * s2, LANES - ROT // 2, 1)


def _a_band(qb):
    r = lax.broadcasted_iota(jnp.int32, (qb, qb + HEAD), 0)
    c = lax.broadcasted_iota(jnp.int32, (qb, qb + HEAD), 1)
    return jnp.logical_and(c >= r, c <= r + HEAD)


def _a_first_ok(qb, n):
    c = lax.broadcasted_iota(jnp.int32, (qb, qb + HEAD), 1)
    return jnp.logical_or(c >= HEAD, n > 0)


def _a_last_ok(qb, has_next):
    c = lax.broadcasted_iota(jnp.int32, (qb, qb + HEAD), 1)
    return jnp.logical_or(c < qb, has_next)


A_SCALE = HEAD ** -0.5


def _a_geometry(S, g):
    d = DIL[g]
    L = S // d
    TQ = min(512, L)
    return d, L, TQ, TQ // HEAD, L // TQ, L // HEAD


def _proj_rope(h, w, tabs, *, name):
    S, D = h.shape
    tm = min(512, S)

    def body(h_ref, w_ref, c_ref, s1_ref, s2_ref, o_ref):
        tc = (c_ref[...], s1_ref[...], s2_ref[...])
        u = _dot_nt(h_ref[...], w_ref[...])
        for j in range(3 * A_HEADS):
            sl = slice(j * HEAD, (j + 1) * HEAD)
            o_ref[:, sl] = (_rope(u[:, sl], *tc) if j < 2 * A_HEADS else u[:, sl]).astype(BF16)

    tab = pl.BlockSpec((tm, LANES), lambda i: (i, 0))
    return pl.pallas_call(
        body, name=name, grid=(S // tm,),
        in_specs=[pl.BlockSpec((tm, D), lambda i: (i, 0)), pl.BlockSpec((3 * A_WIDTH, D), lambda i: (0, 0)),
                  tab, tab, tab],
        out_specs=pl.BlockSpec((tm, 3 * A_WIDTH), lambda i: (i, 0)),
        out_shape=jax.ShapeDtypeStruct((S, 3 * A_WIDTH), BF16),
        compiler_params=_cp(("parallel",)))(h, w, *tabs)


def _attn_a_fwd(qkv, g, *, name):
    S = qkv.shape[0]
    d, L, TQ, nsub, nb, nblk = _a_geometry(S, g)

    def body(q_ref, kc_ref, kp_ref, vc_ref, vp_ref, o_ref, l_ref):
        n = pl.program_id(1)
        QB = min(2 * HEAD, TQ)
        band = _a_band(QB)
        first = jnp.logical_and(band, _a_first_ok(QB, n))
        for h in range(A_HEADS):
            hs = slice(h * HEAD, (h + 1) * HEAD)
            for hh in range(TQ // QB):
                sl = slice(hh * QB, (hh + 1) * QB)
                pv = slice(hh * QB - HEAD, hh * QB)
                kcat = jnp.concatenate([kp_ref[:, hs] if hh == 0 else kc_ref[pv, hs], kc_ref[sl, hs]], axis=0)
                vcat = jnp.concatenate([vp_ref[:, hs] if hh == 0 else vc_ref[pv, hs], vc_ref[sl, hs]], axis=0)
                s = jnp.where(first if hh == 0 else band, _dot_nt(q_ref[sl, hs], kcat) * A_SCALE, NEG)
                m = jnp.max(s, axis=-1, keepdims=True)
                p = jnp.exp(s - m)
                den = jnp.sum(p, axis=-1, keepdims=True)
                o_ref[sl, hs] = _dot(p.astype(BF16), vcat) / den
                l_ref[sl, hs] = jnp.broadcast_to(m + jnp.log(den), (QB, HEAD))

    rcur = lambda r, n: r * nb + n
    rprv = lambda r, n: r * nblk + jnp.maximum(n * nsub - 1, 0)
    cur = lambda off: pl.BlockSpec((TQ, A_WIDTH), lambda r, n: (rcur(r, n), off))
    prv = lambda off: pl.BlockSpec((HEAD, A_WIDTH), lambda r, n: (rprv(r, n), off))
    out = pl.BlockSpec((TQ, A_WIDTH), lambda r, n: (rcur(r, n), 0))
    return pl.pallas_call(
        body, name=name, grid=(d, nb),
        in_specs=[cur(0), cur(1), prv(1), cur(2), prv(2)],
        out_specs=[out, out],
        out_shape=[jax.ShapeDtypeStruct((S, A_WIDTH), F32)] * 2,
        compiler_params=_cp(("parallel", "parallel")),
    )(qkv, qkv, qkv, qkv, qkv)


def _attn_a_dq(qkv, tabs, g, do, lse, adj, du, *, name):
    S = qkv.shape[0]
    d, L, TQ, nsub, nb, nblk = _a_geometry(S, g)

    def body(q_ref, kc_ref, kp_ref, vc_ref, vp_ref, do_ref, l_ref, adj_ref, c_ref, s1_ref, s2_ref, du_ref, dq_ref):
        n = pl.program_id(1)
        QB = min(2 * HEAD, TQ)
        band = _a_band(QB)
        first = jnp.logical_and(band, _a_first_ok(QB, n))
        for h in range(A_HEADS):
            hs = slice(h * HEAD, (h + 1) * HEAD)
            for hh in range(TQ // QB):
                sl = slice(hh * QB, (hh + 1) * QB)
                pv = slice(hh * QB - HEAD, hh * QB)
                kcat = jnp.concatenate([kp_ref[:, hs] if hh == 0 else kc_ref[pv, hs], kc_ref[sl, hs]], axis=0)
                vcat = jnp.concatenate([vp_ref[:, hs] if hh == 0 else vc_ref[pv, hs], vc_ref[sl, hs]], axis=0)
                s = jnp.where(first if hh == 0 else band, _dot_nt(q_ref[sl, hs], kcat) * A_SCALE, NEG)
                p = jnp.exp(s - l_ref[sl, hs][:, :1])
                ds = p * (_dot_nt(do_ref[sl, hs], vcat) + adj_ref[sl, hs][:, :1])
                dq = _dot(ds.astype(BF16), kcat) * A_SCALE
                dq_ref[sl, hs] = _unrope(dq, c_ref[sl, :], s1_ref[sl, :], s2_ref[sl, :]).astype(BF16)

    rcur = lambda r, n: r * nb + n
    rprv = lambda r, n: r * nblk + jnp.maximum(n * nsub - 1, 0)
    cur = lambda off: pl.BlockSpec((TQ, A_WIDTH), lambda r, n: (rcur(r, n), off))
    prv = lambda off: pl.BlockSpec((HEAD, A_WIDTH), lambda r, n: (rprv(r, n), off))
    tcur = pl.BlockSpec((TQ, LANES), lambda r, n: (rcur(r, n), 0))
    blk = cur(0)
    return pl.pallas_call(
        body, name=name, grid=(d, nb),
        in_specs=[cur(0), cur(1), prv(1), cur(2), prv(2), blk, blk, blk, tcur, tcur, tcur, ANY],
        out_specs=blk,
        out_shape=jax.ShapeDtypeStruct((S, 3 * A_WIDTH), BF16),
        input_output_aliases={11: 0},
        compiler_params=_cp(("parallel", "parallel")),
    )(qkv, qkv, qkv, qkv, qkv, do, lse, adj, *tabs, du)


def _attn_a_dkv(qkv, tabs, g, do, lse, adj, *, name):
    S = qkv.shape[0]
    d, L, TQ, nsub, nb, nblk = _a_geometry(S, g)

    def body(qc_ref, qn_ref, kc_ref, vc_ref, doc_ref, don_ref, lc_ref, ln_ref, ac_ref, an_ref,
             c_ref, s1_ref, s2_ref, du_ref):
        n = pl.program_id(1)
        QB = min(2 * HEAD, TQ)
        nh = TQ // QB
        band = _a_band(QB)
        end = jnp.logical_and(band, _a_last_ok(QB, n < nb - 1))
        for h in range(A_HEADS):
            hs = slice(h * HEAD, (h + 1) * HEAD)
            for kh in range(nh):
                sl = slice(kh * QB, (kh + 1) * QB)
                nx = slice((kh + 1) * QB, (kh + 1) * QB + HEAD)
                last = kh == nh - 1
                cat = lambda cur, nxt: jnp.concatenate([cur[sl, hs], nxt[:, hs] if last else cur[nx, hs]], axis=0)
                qcat = cat(qc_ref, qn_ref)
                docat = cat(doc_ref, don_ref)
                lt = cat(lc_ref, ln_ref).T[:1, :]
                at = cat(ac_ref, an_ref).T[:1, :]
                st = jnp.where(end if last else band, _dot_nt(kc_ref[sl, hs], qcat) * A_SCALE, NEG)
                pt = jnp.exp(st - lt)
                dv_cols = slice(2 * A_WIDTH + h * HEAD, 2 * A_WIDTH + (h + 1) * HEAD)
                dk_cols = slice(A_WIDTH + h * HEAD, A_WIDTH + (h + 1) * HEAD)
                du_ref[sl, dv_cols] = _dot(pt.astype(BF16), docat).astype(BF16)
                dst = pt * (_dot_nt(vc_ref[sl, hs], docat) + at)
                dk = _dot(dst.astype(BF16), qcat) * A_SCALE
                du_ref[sl, dk_cols] = _unrope(dk, c_ref[sl, :], s1_ref[sl, :], s2_ref[sl, :]).astype(BF16)

    rcur = lambda r, n: r * nb + n
    rnxt = lambda r, n: r * nblk + jnp.minimum((n + 1) * nsub, nblk - 1)
    cur = lambda off: pl.BlockSpec((TQ, A_WIDTH), lambda r, n: (rcur(r, n), off))
    nxu = lambda off: pl.BlockSpec((HEAD, A_WIDTH), lambda r, n: (rnxt(r, n), off))
    tcur = pl.BlockSpec((TQ, LANES), lambda r, n: (rcur(r, n), 0))
    blk, bnx = cur(0), nxu(0)
    return pl.pallas_call(
        body, name=name, grid=(d, nb),
        in_specs=[cur(0), nxu(0), cur(1), cur(2), blk, bnx, blk, bnx, blk, bnx, tcur, tcur, tcur],
        out_specs=pl.BlockSpec((TQ, 3 * A_WIDTH), lambda r, n: (rcur(r, n), 0)),
        out_shape=jax.ShapeDtypeStruct((S, 3 * A_WIDTH), BF16),
        compiler_params=_cp(("parallel", "parallel")),
    )(qkv, qkv, qkv, qkv, do, do, lse, lse, adj, adj, *tabs)


def _silu_parts(z):
    sg = _sigmoid(z)
    return z * sg, sg * (1.0 + z * (1.0 - sg))


def _classes_to_tokens(c_ref, d, tm, tmps):
    if d == 1:
        return c_ref[...].astype(F32)
    for k in range(d):
        for c, tmp in enumerate(tmps):
            tmp[pl.ds(k, tm // d, stride=d), :] = c_ref[k, :, c * LANES:(c + 1) * LANES].astype(F32)
    return jnp.concatenate([tmp[...] for tmp in tmps], axis=1)


def _tokens_to_classes(val, c_ref, d, tm, tmps):
    if d == 1:
        c_ref[...] = val.astype(c_ref.dtype)
        return
    for c, tmp in enumerate(tmps):
        tmp[...] = val[:, c * LANES:(c + 1) * LANES]
    for k in range(d):
        c_ref[k] = jnp.concatenate([tmp[pl.ds(k, tm // d, stride=d), :] for tmp in tmps], axis=1).astype(c_ref.dtype)


def _group_spec(S, d, tm):
    if d == 1:
        return pl.BlockSpec((tm, A_WIDTH), lambda i: (i, 0))
    return _class_spec(S, d, tm, A_WIDTH)


def _group_view(t, d):
    return t if d == 1 else t.reshape(d, t.shape[0] // d, t.shape[1])


def _merge_a_fwd(os_, ls_, ur, *, name):
    S = ur.shape[0]
    tm = min(512, S)

    def body(o0, o1, o2, l0, l1, l2, z_ref, y_ref, *tmps):
        ls = [_classes_to_tokens(r, d, tm, tmps) for r, d in zip((l0, l1, l2), DIL)]
        ov = [_classes_to_tokens(r, d, tm, tmps) for r, d in zip((o0, o1, o2), DIL)]
        mx = jnp.maximum(jnp.maximum(ls[0], ls[1]), ls[2])
        es = [jnp.exp(l - mx) for l in ls]
        den = es[0] + es[1] + es[2]
        y = (es[0] / den) * ov[0] + (es[1] / den) * ov[1] + (es[2] / den) * ov[2]
        y_ref[...] = (y * _silu_parts(z_ref[...])[0]).astype(BF16)

    blk = pl.BlockSpec((tm, A_WIDTH), lambda i: (i, 0))
    groups = [_group_spec(S, d, tm) for d in DIL]
    return pl.pallas_call(
        body, name=name, grid=(S // tm,),
        in_specs=groups + groups + [pl.BlockSpec((tm, A_WIDTH), lambda i: (i, R_ZA // A_WIDTH))],
        out_specs=blk, out_shape=jax.ShapeDtypeStruct((S, A_WIDTH), BF16),
        scratch_shapes=[pltpu.VMEM((tm, LANES), F32)] * (A_WIDTH // LANES),
        compiler_params=_cp(("parallel",)))(*[_group_view(t, d) for t, d in zip(os_, DIL)],
                                            *[_group_view(t, d) for t, d in zip(ls_, DIL)], ur)


def _merge_a_bwd(os_, ls_, ur, dya, du_r, *, name):
    S = ur.shape[0]
    tm = min(256, S)

    def body(o0, o1, o2, l0, l1, l2, z_ref, dy_ref, du_in, d0, d1, d2, a0, a1, a2, dz_ref, *tmps):
        ls = [_classes_to_tokens(r, d, tm, tmps) for r, d in zip((l0, l1, l2), DIL)]
        ov = [_classes_to_tokens(r, d, tm, tmps) for r, d in zip((o0, o1, o2), DIL)]
        mx = jnp.maximum(jnp.maximum(ls[0], ls[1]), ls[2])
        es = [jnp.exp(l - mx) for l in ls]
        den = es[0] + es[1] + es[2]
        ws = [e / den for e in es]
        y = ws[0] * ov[0] + ws[1] * ov[1] + ws[2] * ov[2]
        sz, dsz = _silu_parts(z_ref[...])
        dyv = dy_ref[...]
        dz_ref[...] = (dyv * y * dsz).astype(BF16)
        dyp = dyv * sz
        ts = []
        for h in range(A_HEADS):
            sl = slice(h * HEAD, (h + 1) * HEAD)
            t = jnp.zeros((tm, 1), F32)
            for gi in range(3):
                t = t + ws[gi][:, sl][:, :1] * jnp.sum(dyp[:, sl] * ov[gi][:, sl], axis=-1, keepdims=True)
            ts.append(jnp.broadcast_to(t, (tm, HEAD)))
        tb = jnp.concatenate(ts, axis=1)
        for gi, (dref, aref) in enumerate(((d0, a0), (d1, a1), (d2, a2))):
            _tokens_to_classes(ws[gi] * dyp, dref, DIL[gi], tm, tmps)
            _tokens_to_classes(-ws[gi] * tb, aref, DIL[gi], tm, tmps)

    blk = pl.BlockSpec((tm, A_WIDTH), lambda i: (i, 0))
    groups = [_group_spec(S, d, tm) for d in DIL]
    shaped = lambda dt: [jax.ShapeDtypeStruct((S, A_WIDTH) if d == 1 else (d, S // d, A_WIDTH), dt) for d in DIL]
    outs = pl.pallas_call(
        body, name=name, grid=(S // tm,),
        in_specs=groups + groups + [pl.BlockSpec((tm, A_WIDTH), lambda i: (i, R_ZA // A_WIDTH)), blk, ANY],
        out_specs=groups + groups + [pl.BlockSpec((tm, A_WIDTH), lambda i: (i, R_ZA // A_WIDTH))],
        out_shape=shaped(BF16) + shaped(F32) + [jax.ShapeDtypeStruct(du_r.shape, BF16)],
        input_output_aliases={8: 6},
        scratch_shapes=[pltpu.VMEM((tm, LANES), F32)] * (A_WIDTH // LANES),
        compiler_params=_cp(("parallel",)))(*[_group_view(t, d) for t, d in zip(os_, DIL)],
                                            *[_group_view(t, d) for t, d in zip(ls_, DIL)], ur, dya, du_r)
    flat = [t.reshape(S, A_WIDTH) for t in outs[:6]]
    return flat[0:3], flat[3:6], outs[6]


def _logf(ur, bf_pad, *, name):
    S = ur.shape[0]
    tm = min(1024, S)

    def body(u_ref, b_ref, o_ref):
        z = u_ref[...] + b_ref[...]
        o_ref[...] = jnp.minimum(z, 0.0) - jnp.log(1.0 + jnp.exp(-jnp.abs(z)))

    return pl.pallas_call(
        body, name=name, grid=(S // tm,),
        in_specs=[pl.BlockSpec((tm, FB_PAD), lambda i: (i, R_FB // FB_PAD)),
                  pl.BlockSpec((1, FB_PAD), lambda i: (0, 0))],
        out_specs=pl.BlockSpec((tm, FB_PAD), lambda i: (i, 0)),
        out_shape=jax.ShapeDtypeStruct((S, FB_PAD), F32),
        compiler_params=_cp(("parallel",)))(ur, bf_pad)


def _cumsum_lanes(x, reverse, *, name):
    nt, H, _ = x.shape
    R = nt * H

    def body(x_ref, o_ref):
        v = x_ref[...].reshape(R, LANES)
        lane = lax.broadcasted_iota(jnp.int32, (R, LANES), 1)
        row = lax.broadcasted_iota(jnp.int32, (R, LANES), 0)

        def scan(t, step, idx, n, axis):
            while step < n:
                if reverse:
                    t = t + jnp.where(idx < n - step, pltpu.roll(t, n - step, axis), 0.0)
                else:
                    t = t + jnp.where(idx >= step, pltpu.roll(t, step, axis), 0.0)
                step *= 2
            return t

        v = scan(v, 1, lane, LANES, 1)
        total = jnp.broadcast_to(v[:, :1] if reverse else v[:, LANES - 1:], (R, LANES))
        carry = scan(total, H, row, R, 0) - total
        o_ref[...] = (v + carry).reshape(nt, H, LANES)

    return pl.pallas_call(
        body, name=name, out_shape=jax.ShapeDtypeStruct((nt, H, LANES), F32),
        in_specs=[pl.BlockSpec(memory_space=pltpu.VMEM)], out_specs=pl.BlockSpec(memory_space=pltpu.VMEM),
        compiler_params=_cp())(x)


B_SCALE = B_HEAD ** -0.5


def _pair_masks():
    lane = lax.broadcasted_iota(jnp.int32, (1, LANES), 1)
    row = lax.broadcasted_iota(jnp.int32, (LANES, 1), 0)
    return (lane < B_HEAD, lane >= B_HEAD), (row < B_HEAD, row >= B_HEAD)


def _causal_t(T):
    r = lax.broadcasted_iota(jnp.int32, (T, T), 0)
    c = lax.broadcasted_iota(jnp.int32, (T, T), 1)
    return r <= c


def _zero_other(x, keep):
    return jnp.where(keep, x, jnp.zeros_like(x))


def _fox_aug(ub, c, *, name):
    S = ub.shape[0]
    T = min(2048, S)

    def body(q_ref, k_ref, c_ref, qa_ref, ka_ref):
        lane = lax.broadcasted_iota(jnp.int32, (1, LANES), 1)
        q = q_ref[...] * B_SCALE
        k = k_ref[...]
        for a in range(2):
            own = (lane < B_HEAD) if a == 0 else (lane >= B_HEAD)
            o = B_HEAD if a == 0 else 0
            cv = jnp.broadcast_to(c_ref[:, a:a + 1], (T, LANES))
            hi = cv.astype(BF16)
            r1 = cv - hi.astype(F32)
            mid = r1.astype(BF16)
            lo = (r1 - mid.astype(F32)).astype(BF16)
            pieces = (hi, mid, lo)
            one = jnp.ones((T, LANES), BF16)
            qa = jnp.where(own, q, jnp.zeros_like(q))
            ka = jnp.where(own, k, jnp.zeros_like(k))
            for t in range(3):
                qa = jnp.where(lane == o + t, pieces[t], qa)
                qa = jnp.where(lane == o + 3 + t, one, qa)
                ka = jnp.where(lane == o + t, one, ka)
                ka = jnp.where(lane == o + 3 + t, -pieces[t], ka)
            qa_ref[a] = qa
            ka_ref[a] = ka

    out = pl.BlockSpec((2, T, LANES), lambda h, i: (h, i, 0))
    c_pairs = c.reshape(B_HEADS // 2, 2, S).transpose(0, 2, 1)
    return pl.pallas_call(
        body, name=name, grid=(B_HEADS // 2, S // T),
        in_specs=[pl.BlockSpec((T, LANES), lambda h, i: (i, h)), pl.BlockSpec((T, LANES), lambda h, i: (i, 4 + h)),
                  pl.BlockSpec((None, T, 2), lambda h, i: (h, i, 0))],
        out_specs=[out, out], out_shape=[jax.ShapeDtypeStruct((B_HEADS, S, LANES), BF16)] * 2,
        compiler_params=_cp(("parallel", "parallel")))(ub, ub, c_pairs)


def _fox_fwd(qaug, kaug, vt, *, name):
    S = qaug.shape[1]
    T = min(512, S)
    nq = S // T

    def body(q_ref, k_ref, vt_ref, o_ref, l_ref, m_s, l_s, acc_s, st_s):
        i = pl.program_id(1)
        _, rows = _pair_masks()
        qm = [q_ref[0], q_ref[1]]
        m_s[...] = jnp.full((2, 1, T), NEG, F32)
        l_s[...] = jnp.zeros((2, 1, T), F32)
        acc_s[...] = jnp.zeros((LANES, T), F32)

        def logits(j):
            off = pl.multiple_of(j * T, T)
            return [_dot_nt(k_ref[a, pl.ds(off, T), :], qm[a]) for a in range(2)]

        def step(j, masked, prefetch):
            nxt = logits(j + 1) if prefetch else None
            vtj = vt_ref[j]
            vtm = [_zero_other(vtj, rows[0]), _zero_other(vtj, rows[1])]
            causal = _causal_t(T) if masked else None
            hw = min(2 * LANES, T)
            for hq in range(T // hw):
                cs = slice(hq * hw, (hq + 1) * hw)
                upd = jnp.zeros((LANES, hw), F32)
                alphas = []
                for a in range(2):
                    st = st_s[a, :, cs]
                    if masked:
                        st = jnp.where(causal[:, cs], st, NEG)
                    m_old = m_s[a, :, cs]
                    m_new = jnp.maximum(m_old, jnp.max(st, axis=0, keepdims=True))
                    alpha = jnp.exp(m_old - m_new)
                    pt = jnp.exp(st - m_new)
                    l_s[a, :, cs] = alpha * l_s[a, :, cs] + jnp.sum(pt, axis=0, keepdims=True)
                    m_s[a, :, cs] = m_new
                    upd = upd + _dot(vtm[a], pt.astype(BF16))
                    alphas.append(alpha)
                acc_s[:, cs] = acc_s[:, cs] * jnp.where(rows[0], alphas[0], alphas[1]) + upd
            if prefetch:
                st_s[0] = nxt[0]
                st_s[1] = nxt[1]

        def loop(j, carry):
            step(j, False, True)
            return carry

        first = logits(0)
        st_s[0] = first[0]
        st_s[1] = first[1]
        lax.fori_loop(0, i, loop, 0)
        step(i, True, False)
        o_ref[...] = (acc_s[...] / jnp.where(rows[0], l_s[0], l_s[1])).T
        l_ref[0] = m_s[0] + jnp.log(l_s[0])
        l_ref[1] = m_s[1] + jnp.log(l_s[1])

    stat = pl.BlockSpec((2, None, 1, T), lambda h, i: (h, i, 0, 0))
    return pl.pallas_call(
        body, name=name, grid=(B_HEADS // 2, nq),
        in_specs=[pl.BlockSpec((2, T, LANES), lambda h, i: (h, i, 0)),
                  pl.BlockSpec((2, S, LANES), lambda h, i: (h, 0, 0)),
                  pl.BlockSpec((nq, LANES, T), lambda h, i: (0, h, 0))],
        out_specs=[pl.BlockSpec((T, LANES), lambda h, i: (i, h)), stat],
        out_shape=[jax.ShapeDtypeStruct((S, A_WIDTH), F32), jax.ShapeDtypeStruct((B_HEADS, nq, 1, T), F32)],
        scratch_shapes=[pltpu.VMEM((2, 1, T), F32), pltpu.VMEM((2, 1, T), F32), pltpu.VMEM((LANES, T), F32),
                        pltpu.VMEM((2, T, T), F32)],
        compiler_params=_cp(("parallel", "parallel")),
    )(qaug, kaug, vt)


def _fox_delta(o, do, *, name):
    S = o.shape[0]
    T = min(512, S)
    nq = S // T

    per = min(4, nq)

    def body(o_ref, do_ref, d_ref):
        _, rows = _pair_masks()
        for t in range(per):
            sl = slice(t * T, (t + 1) * T)
            prod_t = (do_ref[sl, :].astype(F32) * o_ref[sl, :]).T
            d_ref[0, t] = jnp.sum(_zero_other(prod_t, rows[0]), axis=0, keepdims=True)
            d_ref[1, t] = jnp.sum(_zero_other(prod_t, rows[1]), axis=0, keepdims=True)

    tile = pl.BlockSpec((per * T, LANES), lambda h, i: (i, h))
    return pl.pallas_call(
        body, name=name, grid=(B_HEADS // 2, nq // per), in_specs=[tile, tile],
        out_specs=pl.BlockSpec((2, per, 1, T), lambda h, i: (h, i, 0, 0)),
        out_shape=jax.ShapeDtypeStruct((B_HEADS, nq, 1, T), F32),
        compiler_params=_cp(("parallel", "parallel")))(o, do)


def _fox_bwd(ub, qaug, kaug, kt, do, lse, delta, *, name):
    S = ub.shape[0]
    T = min(512, S)
    nq = S // T

    def body(k_ref, v_ref, kt_ref, q_ref, do_ref, l_ref, dl_ref,
             dk_ref, dv_ref, dck_ref, dqt_ref, dcq_ref, dk_s, dv_s, dc_s):
        j = pl.program_id(1)
        lanes, rows = _pair_masks()
        vv = v_ref[...]
        ktj = kt_ref[...]
        km = [k_ref[0], k_ref[1]]
        ktm = [_zero_other(ktj, rows[0]), _zero_other(ktj, rows[1])]
        dk_s[...] = jnp.zeros((2, T, LANES), F32)
        dv_s[...] = jnp.zeros((T, LANES), F32)
        dc_s[...] = jnp.zeros((2, T, 1), F32)

        @pl.when(j == 0)
        def _():
            dqt_ref[...] = jnp.zeros((nq, LANES, T), F32)
            dcq_ref[...] = jnp.zeros((2, nq, 1, T), F32)

        def step(i, masked):
            off = pl.multiple_of(i * T, T)
            doi = do_ref[pl.ds(off, T), :]
            upd = jnp.zeros((LANES, T), F32)
            for a in range(2):
                qi = q_ref[a, pl.ds(off, T), :]
                st = _dot_nt(km[a], qi)
                if masked:
                    st = jnp.where(_causal_t(T), st, NEG)
                pt = jnp.exp(st - l_ref[a, i])
                doa = _zero_other(doi, lanes[a])
                dv_s[...] += _dot(pt.astype(BF16), doa)
                dst = pt * (_dot_nt(vv, doa) - dl_ref[a, i])
                dsb = dst.astype(BF16)
                dk_s[a] += _dot(dsb, qi)
                upd = upd + _dot(ktm[a], dsb)
                dc_s[a] -= jnp.sum(dst, axis=-1, keepdims=True)
                dcq_ref[a, i] += jnp.sum(dst, axis=0, keepdims=True)
            dqt_ref[i] += upd

        def loop(i, carry):
            step(i, False)
            return carry

        step(j, True)
        lax.fori_loop(j + 1, nq, loop, 0)
        dk_ref[...] = jnp.where(lanes[0], dk_s[0], dk_s[1]).astype(BF16)
        dv_ref[...] = dv_s[...].astype(BF16)
        dck_ref[...] = dc_s[...]

    rowv = pl.BlockSpec((2, nq, 1, T), lambda h, j: (h, 0, 0, 0))
    tile = pl.BlockSpec((T, LANES), lambda h, j: (j, h))
    return pl.pallas_call(
        body, name=name, grid=(B_HEADS // 2, nq),
        in_specs=[pl.BlockSpec((2, T, LANES), lambda h, j: (h, j, 0)),
                  pl.BlockSpec((T, LANES), lambda h, j: (j, 8 + h)),
                  pl.BlockSpec((None, LANES, T), lambda h, j: (j, h, 0)),
                  pl.BlockSpec((2, S, LANES), lambda h, j: (h, 0, 0)),
                  pl.BlockSpec((S, LANES), lambda h, j: (0, h)),
                  rowv, rowv],
        out_specs=[tile, tile, pl.BlockSpec((2, T, 1), lambda h, j: (h, j, 0)),
                   pl.BlockSpec((nq, LANES, T), lambda h, j: (0, h, 0)), rowv],
        out_shape=[jax.ShapeDtypeStruct((S, A_WIDTH), BF16)] * 2 + [jax.ShapeDtypeStruct((B_HEADS, S, 1), F32),
                   jax.ShapeDtypeStruct((nq, A_WIDTH, T), F32), jax.ShapeDtypeStruct((B_HEADS, nq, 1, T), F32)],
        scratch_shapes=[pltpu.VMEM((2, T, LANES), F32), pltpu.VMEM((T, LANES), F32), pltpu.VMEM((2, T, 1), F32)],
        compiler_params=_cp(("parallel", "arbitrary")),
    )(kaug, ub, kt, qaug, do, lse, delta)


def _gate_fwd(o, ur, zcol, *, name):
    S = ur.shape[0]
    tm = min(1024, S)

    def body(o_ref, z_ref, y_ref):
        y_ref[...] = (o_ref[...] * _silu_parts(z_ref[...])[0]).astype(BF16)

    blk = pl.BlockSpec((tm, A_WIDTH), lambda i: (i, 0))
    return pl.pallas_call(
        body, name=name, grid=(S // tm,),
        in_specs=[blk, pl.BlockSpec((tm, A_WIDTH), lambda i: (i, zcol // A_WIDTH))],
        out_specs=blk, out_shape=jax.ShapeDtypeStruct((S, A_WIDTH), BF16),
        compiler_params=_cp(("parallel",)))(o, ur)


def _gate_bwd(o, ur, zcol, dy, du_r, *, name):
    S = ur.shape[0]
    tm = min(1024, S)

    def body(o_ref, z_ref, dy_ref, du_in, do_ref, dz_ref):
        sz, dsz = _silu_parts(z_ref[...])
        dyv = dy_ref[...]
        do_ref[...] = (dyv * sz).astype(BF16)
        dz_ref[...] = (dyv * o_ref[...] * dsz).astype(BF16)

    blk = pl.BlockSpec((tm, A_WIDTH), lambda i: (i, 0))
    gate = pl.BlockSpec((tm, A_WIDTH), lambda i: (i, zcol // A_WIDTH))
    return pl.pallas_call(
        body, name=name, grid=(S // tm,),
        in_specs=[blk, gate, blk, ANY],
        out_specs=[blk, gate],
        out_shape=[jax.ShapeDtypeStruct((S, A_WIDTH), BF16), jax.ShapeDtypeStruct(du_r.shape, BF16)],
        input_output_aliases={3: 1},
        compiler_params=_cp(("parallel",)))(o, ur, dy, du_r)


def _dfb(ur, bf_pad, dlogf_pad, du_r, *, name):
    S = ur.shape[0]
    tm = min(1024, S)

    def body(u_ref, b_ref, d_ref, du_in, o_ref, s_ref):
        i = pl.program_id(0)
        dv = d_ref[...] * _sigmoid(-(u_ref[...] + b_ref[...]))
        o_ref[...] = dv.astype(BF16)
        part = jnp.sum(dv, axis=0, keepdims=True)

        @pl.when(i == 0)
        def _():
            s_ref[...] = part

        @pl.when(i > 0)
        def _():
            s_ref[...] += part

    vec = pl.BlockSpec((1, FB_PAD), lambda i: (0, 0))
    blk = pl.BlockSpec((tm, FB_PAD), lambda i: (i, 0))
    fb = pl.BlockSpec((tm, FB_PAD), lambda i: (i, R_FB // FB_PAD))
    return pl.pallas_call(
        body, name=name, grid=(S // tm,),
        in_specs=[fb, vec, blk, ANY],
        out_specs=[fb, vec],
        out_shape=[jax.ShapeDtypeStruct(du_r.shape, BF16), jax.ShapeDtypeStruct((1, FB_PAD), F32)],
        input_output_aliases={3: 0},
        compiler_params=_cp(("arbitrary",)))(ur, bf_pad, dlogf_pad, du_r)


M_SCALE = HEAD ** -0.5


def _mem_fwd(ur, mkv, *, name):
    S = ur.shape[0]
    T = min(512, S)

    def body(q_ref, z_ref, k_ref, v_ref, y_ref):
        for h in range(M_HEADS):
            hs = slice(h * HEAD, (h + 1) * HEAD)
            s = _dot_nt(q_ref[:, hs].astype(BF16), k_ref[:, hs].astype(BF16)) * M_SCALE
            p = jnp.exp(s - jnp.max(s, axis=-1, keepdims=True))
            p = p / jnp.sum(p, axis=-1, keepdims=True)
            o = _dot(p.astype(BF16), v_ref[:, hs].astype(BF16))
            y_ref[:, hs] = (o * _silu_parts(z_ref[:, hs])[0]).astype(BF16)

    wide = lambda col: pl.BlockSpec((T, A_WIDTH), lambda i: (i, col // A_WIDTH))
    kv = lambda half: pl.BlockSpec((N_MEM, A_WIDTH), lambda i: (0, half))
    return pl.pallas_call(
        body, name=name, grid=(S // T,),
        in_specs=[wide(R_QM), wide(R_ZM), kv(0), kv(1)],
        out_specs=pl.BlockSpec((T, A_WIDTH), lambda i: (i, 0)),
        out_shape=jax.ShapeDtypeStruct((S, A_WIDTH), BF16),
        compiler_params=_cp(("parallel",)))(ur, ur, mkv, mkv)


def _mem_bwd(ur, mkv, dy, du_r, *, name):
    S = ur.shape[0]
    T = min(512, S)

    def body(q_ref, z_ref, k_ref, v_ref, dy_ref, du_in, du_ref, dk_ref, dv_ref):
        i = pl.program_id(0)

        @pl.when(i == 0)
        def _():
            dk_ref[...] = jnp.zeros((N_MEM, A_WIDTH), F32)
            dv_ref[...] = jnp.zeros((N_MEM, A_WIDTH), F32)

        for h in range(M_HEADS):
            hs = slice(h * HEAD, (h + 1) * HEAD)
            qv = q_ref[:, hs].astype(BF16)
            kv = k_ref[:, hs].astype(BF16)
            vv = v_ref[:, hs].astype(BF16)
            s = _dot_nt(qv, kv) * M_SCALE
            p = jnp.exp(s - jnp.max(s, axis=-1, keepdims=True))
            p = p / jnp.sum(p, axis=-1, keepdims=True)
            o = _dot(p.astype(BF16), vv)
            sz, dsz = _silu_parts(z_ref[:, hs])
            dyv = dy_ref[:, hs]
            du_ref[:, A_WIDTH + h * HEAD:A_WIDTH + (h + 1) * HEAD] = (dyv * o * dsz).astype(BF16)
            dov = (dyv * sz).astype(BF16)
            dp = _dot_nt(dov, vv)
            ds = p * (dp - jnp.sum(p * dp, axis=-1, keepdims=True))
            du_ref[:, hs] = (_dot(ds.astype(BF16), kv) * M_SCALE).astype(BF16)
            dv_ref[:, hs] += _dot(p.T.astype(BF16), dov)
            dk_ref[:, hs] += _dot(ds.T.astype(BF16), qv) * M_SCALE

    wide = lambda col: pl.BlockSpec((T, A_WIDTH), lambda i: (i, col // A_WIDTH))
    kv = lambda half: pl.BlockSpec((N_MEM, A_WIDTH), lambda i: (0, half))
    tile = pl.BlockSpec((T, A_WIDTH), lambda i: (i, 0))
    acc = pl.BlockSpec((N_MEM, A_WIDTH), lambda i: (0, 0))
    assert R_ZM == R_QM + A_WIDTH and R_QM % (2 * A_WIDTH) == 0
    return pl.pallas_call(
        body, name=name, grid=(S // T,),
        in_specs=[wide(R_QM), wide(R_ZM), kv(0), kv(1), tile, ANY],
        out_specs=[pl.BlockSpec((T, 2 * A_WIDTH), lambda i: (i, R_QM // (2 * A_WIDTH))), acc, acc],
        out_shape=[jax.ShapeDtypeStruct(du_r.shape, BF16)] + [jax.ShapeDtypeStruct((N_MEM, A_WIDTH), F32)] * 2,
        input_output_aliases={5: 0},
        compiler_params=_cp(("arbitrary",)))(ur, ur, mkv, mkv, dy, du_r)


def _branch_fwd(ys, wbs, ur, b_merge, *, name):
    S = ur.shape[0]
    tm, tn = min(512, S), 512
    nj = D_MODEL // tn

    def body(ya, yb, ym, wa, wb, wm, g0, g1, g2, b0, b1, b2, mg_ref, p_ref):
        acc = jnp.zeros((tm, tn), F32)
        for i, (y, w, gr, br) in enumerate(((ya, wa, g0, b0), (yb, wb, g1, b1), (ym, wm, g2, b2))):
            pr = _dot(y[...], w[...])
            p_ref[i] = pr.astype(BF16)
            acc = acc + _sigmoid(gr[...] + br[...]) * pr
        mg_ref[...] = acc.astype(BF16)

    yspec = pl.BlockSpec((tm, A_WIDTH), lambda i, j: (i, 0))
    wspec = pl.BlockSpec((A_WIDTH, tn), lambda i, j: (0, j))
    gspec = lambda b: pl.BlockSpec((tm, tn), lambda i, j: (i, (R_GL + b * D_MODEL) // tn + j))
    bspec = lambda b: pl.BlockSpec((1, tn), lambda i, j: (0, b * nj + j))
    return pl.pallas_call(
        body, name=name, grid=(S // tm, nj),
        in_specs=[yspec] * 3 + [wspec] * 3 + [gspec(0), gspec(1), gspec(2), bspec(0), bspec(1), bspec(2)],
        out_specs=[pl.BlockSpec((tm, tn), lambda i, j: (i, j)),
                   pl.BlockSpec((3, tm, tn), lambda i, j: (0, i, j))],
        out_shape=[jax.ShapeDtypeStruct((S, D_MODEL), BF16), jax.ShapeDtypeStruct((3, S, D_MODEL), BF16)],
        compiler_params=_cp(("parallel", "parallel")))(*ys, *wbs, ur, ur, ur, b_merge, b_merge, b_merge)


def _branch_bwd(dm, prods, ur, b_merge, *, name):
    S = ur.shape[0]
    tm = min(256, S)

    def body(dm_ref, p_ref, g0, g1, g2, b_ref, dp0, dp1, dp2, dgl_ref, db_ref):
        i = pl.program_id(0)
        dmv = dm_ref[...]
        parts = []
        for b, (gr, dp_ref) in enumerate(((g0, dp0), (g1, dp1), (g2, dp2))):
            sl = slice(b * D_MODEL, (b + 1) * D_MODEL)
            gt = _sigmoid(gr[...] + b_ref[:, sl])
            dp_ref[...] = (dmv * gt).astype(BF16)
            dgl = dmv * p_ref[b].astype(F32) * gt * (1.0 - gt)
            dgl_ref[:, R_GL + b * D_MODEL:R_GL + (b + 1) * D_MODEL] = dgl.astype(BF16)
            parts.append(jnp.sum(dgl, axis=0, keepdims=True))
        part = jnp.concatenate(parts, axis=1)

        @pl.when(i == 0)
        def _():
            db_ref[...] = part

        @pl.when(i > 0)
        def _():
            db_ref[...] += part

    gspec = lambda b: pl.BlockSpec((tm, D_MODEL), lambda i: (i, R_GL // D_MODEL + b))
    vec = pl.BlockSpec((1, 3 * D_MODEL), lambda i: (0, 0))
    row = pl.BlockSpec((tm, D_MODEL), lambda i: (i, 0))
    outs = pl.pallas_call(
        body, name=name, grid=(S // tm,),
        in_specs=[row, pl.BlockSpec((3, tm, D_MODEL), lambda i: (0, i, 0)), gspec(0), gspec(1), gspec(2), vec],
        out_specs=[row, row, row, pl.BlockSpec((tm, NR), lambda i: (i, 0)), vec],
        out_shape=[jax.ShapeDtypeStruct((S, D_MODEL), BF16)] * 3
        + [jax.ShapeDtypeStruct((S, NR), BF16), jax.ShapeDtypeStruct((1, 3 * D_MODEL), F32)],
        compiler_params=_cp(("arbitrary",)))(dm, prods, ur, ur, ur, b_merge)
    return outs[0:3], outs[3], outs[4]


def _rope_tables(pos):
    half = ROT // 2
    S = pos.shape[0]
    inv = ROPE_THETA ** (-jnp.arange(half, dtype=F32) / half)
    per_row = LANES // half
    ang = jnp.repeat(pos.astype(F32).reshape(S // per_row, per_row), half, axis=1) * jnp.tile(inv, per_row)
    cos, sin = lax.optimization_barrier((jnp.cos(ang).reshape(S, half), jnp.sin(ang).reshape(S, half)))
    one = jnp.ones((S, LANES - ROT), F32)
    zero = jnp.zeros((S, LANES - ROT), F32)
    zh = jnp.zeros((S, half), F32)
    c = jnp.concatenate([cos, cos, one], axis=1)
    s1 = jnp.concatenate([-sin, zh, zero], axis=1)
    s2 = jnp.concatenate([zh, sin, zero], axis=1)
    return c, s1, s2


def _to_tiles(t):
    S, H = t.shape
    return t.reshape(S // LANES, LANES, H).transpose(0, 2, 1)


def _from_tiles(t):
    nt, H, _ = t.shape
    return t.transpose(1, 0, 2).reshape(H, nt * LANES)


def _local_step(x, mem, pos, tgt, g_pre, g_post, g_mem, wt, bf_pad, b_merge, w_kv, wbs, w_out, pack=None, hs=None):
    S = x.shape[0]
    T = min(512, S)
    nq = S // T
    tabs = _rope_tables(pos)

    if hs is None:
        hs = _rms_fwd(x, g_pre, name="rms_pre", dilations=DIL)
    h = hs[0]
    tabs_g = [[_to_classes(t, d) for t in tabs] for d in DIL]
    qkvs = [_proj_rope(hs[g], wt[f"A{g}"], tabs_g[g], name=f"proj_a{g}") for g in range(3)]
    ub = _mm(h, wt["B"], bt=True, out_dtype=BF16, name="proj_b", tn=1536)
    ur = _mm(h, wt["R"], bt=True, name="proj_r", tn=1792)

    outs_c, lses_c = [], []
    for g in range(3):
        o, l = _attn_a_fwd(qkvs[g], g, name=f"attn_a_fwd{g}")
        outs_c.append(o)
        lses_c.append(l)
    ya = _merge_a_fwd(outs_c, lses_c, ur, name="merge_a_fwd")

    logf = _logf(ur, bf_pad, name="logf")
    c = _from_tiles(_cumsum_lanes(_to_tiles(logf[:, :B_HEADS]), False, name="cumsum_fwd"))
    qaug, kaug = _fox_aug(ub, c, name="fox_aug")
    kt = ub[:, 512:1024].reshape(nq, T, 512).transpose(0, 2, 1)
    vt = ub[:, 1024:1536].reshape(nq, T, 512).transpose(0, 2, 1)
    ob, lse_b = _fox_fwd(qaug, kaug, vt, name="fox_fwd")
    yb = _gate_fwd(ob, ur, R_ZB, name="gate_b_fwd")

    hm = _rms_fwd(mem, g_mem, name="rms_mem")
    mkv = _mm(hm, w_kv, name="proj_mem")
    ym = _mem_fwd(ur, mkv, name="mem_fwd")

    merged, prods = _branch_fwd((ya, yb, ym), wbs, ur, b_merge, name="branch_fwd")
    out = _mm(merged, w_out, name="proj_out")
    dy, d_out, dg_post, loss_row = _post(x, out, tgt, g_post, name="post")

    dmerged = _mm(d_out, w_out, bt=True, name="d_merged")
    dw_out = _mm(merged, d_out, at=True, name="dw_out", tk=2048)
    dprods, du_r, db_merge = _branch_bwd(dmerged, prods, ur, b_merge, name="branch_bwd")
    dys, dwbs = [], []
    for i, (y, wb) in enumerate(zip((ya, yb, ym), wbs)):
        dys.append(_mm(dprods[i], wb, bt=True, name=f"d_y{i}"))
        dwbs.append(_mm(y, dprods[i], at=True, name=f"dw_branch{i}", tk=2048))

    dos_c, adjs_c, du_r = _merge_a_bwd(outs_c, lses_c, ur, dys[0], du_r, name="merge_a_bwd")
    dus_a = []
    for g, d in enumerate(DIL):
        do_c, adj_c = dos_c[g], adjs_c[g]
        du = _attn_a_dkv(qkvs[g], tabs_g[g], g, do_c, lses_c[g], adj_c, name=f"attn_a_dkv{g}")
        dus_a.append(_attn_a_dq(qkvs[g], tabs_g[g], g, do_c, lses_c[g], adj_c, du, name=f"attn_a_dq{g}"))

    dob, du_r = _gate_bwd(ob, ur, R_ZB, dys[1], du_r, name="gate_b_bwd")
    delta_b = _fox_delta(ob, dob, name="fox_delta")
    dkb, dvb, dc_k, dqt, dc_q = _fox_bwd(ub, qaug, kaug, kt, dob, lse_b, delta_b, name="fox_bwd")
    dqb = (dqt.transpose(0, 2, 1).reshape(S, A_WIDTH) * B_SCALE).astype(BF16)
    du_b = jnp.concatenate([dqb, dkb, dvb], axis=1)
    dc = dc_q.reshape(B_HEADS, S) + dc_k.reshape(B_HEADS, S)
    dlogf = _from_tiles(_cumsum_lanes(_to_tiles(dc.T), True, name="cumsum_bwd"))
    dlogf_pad = jnp.pad(dlogf.T, ((0, 0), (0, FB_PAD - B_HEADS)))
    du_r, db_forget = _dfb(ur, bf_pad, dlogf_pad, du_r, name="dfb")

    du_r, dmk, dmv = _mem_bwd(ur, mkv, dys[2], du_r, name="mem_bwd")
    dmkv = jnp.concatenate([dmk, dmv], axis=1).astype(BF16)
    dhm = _mm(dmkv, w_kv, bt=True, name="d_hm")
    dw_kv = _mm(hm, dmkv, at=True, name="dw_kv")
    dg_mem = _rms_bwd(mem, g_mem, dhm, None, name="rms_mem_bwd")

    dwt ={"R": _mm(du_r, h, at=True, name="dw_in_r", tm=1792, tk=1024),
           "B": _mm(du_b, h, at=True, name="dw_in_b", tm=1536, tk=2048)}
    for g in range(3):
        dwt[f"A{g}"] = _mm(dus_a[g], hs[g], at=True, name=f"dw_in_a{g}", tm=1536, tk=2048)
    res = dict(dwt=dwt, dw_kv=dw_kv, dwbs=dwbs, dw_out=dw_out)
    token_major = [(du_r, wt["R"]), (du_b, wt["B"]), (dus_a[0], wt["A0"])]
    if pack is None:
        dh_1 = _mm(dus_a[1], wt["A1"], name="d_h_a1", tk=1536)
        dh_2 = _mm(dus_a[2], wt["A2"], name="d_h_a2", tk=1536)
        dh = _mm_sum(token_major, name="d_h_main")
    else:
        gbig = pack(dwt, dw_kv, dwbs, dw_out)
        own_idx = _own_slabs()
        dh_1, sib = _mm(dus_a[1], wt["A1"], name="d_h_a1", tk=1536, comm=_pair_comm(gbig, (0, 1)))
        dh_2, sib = _mm(dus_a[2], wt["A2"], name="d_h_a2", tk=1536, comm=_pair_comm(gbig, (2, 3), sib))
        send = _pair_sum(gbig, sib, own_idx, 624, name="pair_sum")
        dh, recv = _mm_sum(token_major, name="d_h_main", comm=_chips_comm(send))
        res = dict(parts=[(gbig, None), (sib, 1), (recv, N_CHIP - 1)], own_idx=own_idx)
    grad_x, dg_pre = _rms_bwd(x, g_pre, dh, dy, name="rms_pre_bwd", dh_classes=[(dh_1, DIL[1]), (dh_2, DIL[2])])

    return dict(res, loss=loss_row, grad_x=grad_x, dg_pre=dg_pre, dg_post=dg_post, dg_mem=dg_mem,
                db_forget=db_forget, db_merge=db_merge)


MESH = pl.DeviceIdType.MESH
ANY = pl.BlockSpec(memory_space=pl.ANY)


def _relations():
    return [(k >> 2 & 1, k >> 1 & 1, k & 1) for k in range(1, N_DEV)]


def _coords():
    return lax.axis_index("x"), lax.axis_index("y"), lax.axis_index("c")


def _gather_comm(shard):
    R, W = shard.shape

    def plan(x_ref, out_ref, send_sems, recv_sems, local_sem):
        x, y, c = _coords()
        me, sibling = (x, y, c), (x, y, 1 - c)
        chips = [(1 - x, y), (x, 1 - y), (1 - x, 1 - y)]

        def slot(px, py, pc):
            return out_ref.at[4 * px + 2 * py + pc]

        def copy(k, block, to, src=None):
            return pltpu.make_async_remote_copy(
                src_ref=slot(*block) if src is None else src, dst_ref=slot(*block),
                send_sem=send_sems.at[k], recv_sem=recv_sems.at[k], device_id=to, device_id_type=MESH)

        mine = pltpu.make_async_copy(x_ref, slot(*me), local_sem)
        first = [copy(0, me, sibling, src=x_ref)]
        first += [copy(1 + j, me, (*chip, c), src=x_ref) for j, chip in enumerate(chips)]
        return me, sibling, chips, c, copy, mine, first

    def start(*refs):
        _, _, _, _, _, mine, first = plan(*refs)
        mine.start()
        for cp in first:
            cp.start()

    def wait(*refs):
        me, sibling, chips, c, copy, mine, first = plan(*refs)
        passed = [copy(4 + j, (*chip, c), sibling) for j, chip in enumerate(chips)]
        for j, chip in enumerate(chips):
            copy(1 + j, (*chip, c), me).wait_recv()
            passed[j].start()
        copy(0, sibling, me).wait_recv()
        for j, chip in enumerate(chips):
            copy(4 + j, (*chip, 1 - c), me).wait_recv()
        for cp in first + passed:
            cp.wait_send()
        mine.wait()

    return dict(inputs=[shard], out_shape=[jax.ShapeDtypeStruct((N_DEV, R, W), shard.dtype)],
                sems=[pltpu.SemaphoreType.DMA((N_DEV - 1,)), pltpu.SemaphoreType.DMA((N_DEV - 1,)),
                      pltpu.SemaphoreType.DMA],
                start=start, wait=wait)


N_CHIP = 4


def _pair_comm(gbig, rels, sib=None):
    _, R, W = gbig.shape

    def copies(g_ref, *rest):
        sib_ref, send_sems, recv_sems = rest[-3:]
        x, y, c = _coords()
        return [pltpu.make_async_remote_copy(
            src_ref=g_ref.at[4 * (x ^ (r >> 1)) + 2 * (y ^ (r & 1)) + (1 - c)], dst_ref=sib_ref.at[r],
            send_sem=send_sems.at[k], recv_sem=recv_sems.at[k], device_id=(x, y, 1 - c), device_id_type=MESH)
            for k, r in enumerate(rels)]

    def start(*refs):
        for cp in copies(*refs):
            cp.start()

    def wait(*refs):
        cps = copies(*refs)
        for cp in cps:
            cp.wait_recv()
        for cp in cps:
            cp.wait_send()

    return dict(inputs=[gbig] if sib is None else [gbig, sib],
                out_shape=[jax.ShapeDtypeStruct((N_CHIP, R, W), gbig.dtype)],
                alias={} if sib is None else {1: 0},
                sems=[pltpu.SemaphoreType.DMA((len(rels),)), pltpu.SemaphoreType.DMA((len(rels),))],
                start=start, wait=wait)


def _own_slabs():
    x, y, c = _coords()
    return jnp.stack([4 * (x ^ (r >> 1)) + 2 * (y ^ (r & 1)) + c for r in range(N_CHIP)]).astype(jnp.int32)


def _pair_sum(gbig, sib, own_idx, tr, *, name):
    _, R, W = gbig.shape

    def body(idx_ref, a_ref, b_ref, o_ref):
        o_ref[...] = (a_ref[...] + b_ref[...]).astype(BF16)

    return pl.pallas_call(
        body, name=name,
        grid_spec=pltpu.PrefetchScalarGridSpec(
            num_scalar_prefetch=1, grid=(N_CHIP - 1, R // tr),
            in_specs=[pl.BlockSpec((None, tr, W), lambda r, i, idx: (idx[r + 1], i, 0)),
                      pl.BlockSpec((None, tr, W), lambda r, i, idx: (r + 1, i, 0))],
            out_specs=pl.BlockSpec((None, tr, W), lambda r, i, idx: (r, i, 0))),
        out_shape=jax.ShapeDtypeStruct((N_CHIP - 1, R, W), BF16),
        compiler_params=_cp(("parallel", "parallel")))(own_idx, gbig, sib)


def _chips_comm(send):
    nb, R, W = send.shape

    def copies(b_ref, rb_ref, send_sems, recv_sems):
        x, y, c = _coords()
        return [pltpu.make_async_remote_copy(
            src_ref=b_ref.at[r - 1], dst_ref=rb_ref.at[r - 1], send_sem=send_sems.at[r - 1],
            recv_sem=recv_sems.at[r - 1], device_id=(x ^ (r >> 1), y ^ (r & 1), c), device_id_type=MESH)
            for r in range(1, N_CHIP)]

    def start(*refs):
        for cp in copies(*refs):
            cp.start()

    def wait(*refs):
        cps = copies(*refs)
        for cp in cps:
            cp.wait_recv()
        for cp in cps:
            cp.wait_send()

    return dict(inputs=[send], out_shape=[jax.ShapeDtypeStruct((nb, R, W), send.dtype)],
                sems=[pltpu.SemaphoreType.DMA((nb,)), pltpu.SemaphoreType.DMA((nb,))],
                start=start, wait=wait)


def _gather_small(gsmall, *, name):
    n = N_DEV - 1

    def body(s_ref, rs_ref, send_sems, recv_sems, local_sem):
        x, y, c = _coords()
        me = 4 * x + 2 * y + c
        mine = pltpu.make_async_copy(s_ref, rs_ref.at[me], local_sem)
        mine.start()

        def copy(k, fx, fy, fc, slot):
            return pltpu.make_async_remote_copy(
                src_ref=s_ref, dst_ref=rs_ref.at[slot], send_sem=send_sems.at[k], recv_sem=recv_sems.at[k],
                device_id=(x ^ fx, y ^ fy, c ^ fc), device_id_type=MESH)

        started = [copy(k, *rel, me) for k, rel in enumerate(_relations())]
        for cp in started:
            cp.start()
        for k, (fx, fy, fc) in enumerate(_relations()):
            copy(k, fx, fy, fc, 4 * (x ^ fx) + 2 * (y ^ fy) + (c ^ fc)).wait_recv()
        for cp in started:
            cp.wait_send()
        mine.wait()

    return pl.pallas_call(
        body, name=name, out_shape=jax.ShapeDtypeStruct((N_DEV, 1, P_SMALL), gsmall.dtype),
        in_specs=[ANY], out_specs=ANY,
        scratch_shapes=[pltpu.SemaphoreType.DMA((n,)), pltpu.SemaphoreType.DMA((n,)), pltpu.SemaphoreType.DMA],
    )(gsmall)


def _part_specs(parts, tr, row0):
    assert row0 % tr == 0
    specs = []
    for a, n_used in parts:
        if n_used is None:
            specs.append(pl.BlockSpec((1, tr, a.shape[2]), lambda i, idx: (idx[0], row0 // tr + i, 0)))
        else:
            specs.append(pl.BlockSpec((n_used, tr, a.shape[2]), lambda i, idx: (0, row0 // tr + i, 0)))
    return specs


def _part_total(refs, parts):
    g = None
    for ref, (_, n_used) in zip(refs, parts):
        for k in range(n_used or 1):
            t = ref[k].astype(F32)
            g = t if g is None else g + t
    return g


def _sum_parts(parts, idx, row0, nrows, tr, *, name):
    W = parts[0][0].shape[2]
    assert nrows % tr == 0

    def body(idx_ref, *refs):
        refs[-1][...] = _part_total(refs[:-1], parts)

    return pl.pallas_call(
        body, name=name,
        grid_spec=pltpu.PrefetchScalarGridSpec(
            num_scalar_prefetch=1, grid=(nrows // tr,), in_specs=_part_specs(parts, tr, row0),
            out_specs=pl.BlockSpec((tr, W), lambda i, idx: (i, 0))),
        out_shape=jax.ShapeDtypeStruct((nrows, W), F32),
        compiler_params=_cp(("parallel",)))(idx, *[a for a, _ in parts])


def _adamw(parts, idx, w, m, v, tr, *, name):
    R, W = w.shape
    assert R % tr == 0
    np_ = len(parts)

    def body(idx_ref, *refs):
        w_ref, m_ref, v_ref, g_ref, d_ref, nm_ref, nv_ref = refs[np_:]
        g = _part_total(refs[:np_], parts)
        mm = ADAM_B1 * m_ref[...] + (1.0 - ADAM_B1) * g
        vv = ADAM_B2 * v_ref[...] + (1.0 - ADAM_B2) * (g * g)
        m_hat = mm / (1.0 - ADAM_B1 ** ADAM_STEP)
        v_hat = vv / (1.0 - ADAM_B2 ** ADAM_STEP)
        g_ref[...] = g
        d_ref[...] = -ADAM_LR * (m_hat / (jnp.sqrt(v_hat) + ADAM_EPS) + ADAM_WD * w_ref[...])
        nm_ref[...] = mm
        nv_ref[...] = vv

    blk = pl.BlockSpec((tr, W), lambda i, idx: (i, 0))
    return pl.pallas_call(
        body, name=name,
        grid_spec=pltpu.PrefetchScalarGridSpec(
            num_scalar_prefetch=1, grid=(R // tr,), in_specs=_part_specs(parts, tr, 0) + [blk, blk, blk],
            out_specs=[blk] * 4),
        out_shape=[jax.ShapeDtypeStruct((R, W), F32)] * 4,
        compiler_params=_cp(("parallel",)))(idx, *[a for a, _ in parts], w, m, v)


def _pack_rest(w_kv, wa, wb, wm, w_out):
    return jnp.concatenate([w_kv[0], w_out[0]] + [t[0].reshape(-1, D_MODEL) for t in (wa, wb, wm)], axis=0)


def _unpack_rest(t):
    br = lambda i: t[RO_BR + 64 * i:RO_BR + 64 * (i + 1)].reshape(1, A_WIDTH, D_MODEL // N_DEV)
    return t[None, RO_KV:RO_OUT], br(0), br(1), br(2), t[None, RO_OUT:RO_BR]


def _orig_rows(gathered, a, b):
    res = []
    while a < b:
        dev, r = divmod(a, CS)
        n = min(b - a, CS - r)
        res.append(gathered[dev, RO_IN + r:RO_IN + r + n])
        a += n
    return res


def _full_weights(gathered):
    wt = {}
    for name, ranges in SEGS.items():
        rows = [p for a, b in ranges for p in _orig_rows(gathered, a, b)]
        if SEG_PAD[name]:
            rows.append(jnp.zeros((SEG_PAD[name], D_MODEL), gathered.dtype))
        wt[name] = jnp.concatenate(rows, axis=0)
    w_kv = gathered[:, RO_KV:RO_OUT].reshape(D_MODEL, D_MODEL)
    w_out = gathered[:, RO_OUT:RO_BR].reshape(D_MODEL, D_MODEL)
    wbs = [gathered[:, RO_BR + 64 * i:RO_BR + 64 * (i + 1)].reshape(N_DEV, A_WIDTH, D_MODEL // N_DEV)
           .transpose(1, 0, 2).reshape(A_WIDTH, D_MODEL) for i in range(3)]
    return wt, w_kv, wbs, w_out


def _orig_order(dwt):
    pieces = []
    for name, ranges in SEGS.items():
        o = 0
        for a, b in ranges:
            pieces.append((a, dwt[name][o:o + b - a]))
            o += b - a
    pieces.sort(key=lambda p: p[0])
    return jnp.concatenate([p[1] for p in pieces], axis=0)


def _pack_grads(dwt, dw_kv, dwbs, dw_out):
    g_in = jnp.pad(_orig_order(dwt).reshape(N_DEV, CS, D_MODEL), ((0, 0), (0, IN_ROWS - CS), (0, 0)))
    br = [t.reshape(A_WIDTH, N_DEV, D_MODEL // N_DEV).transpose(1, 0, 2).reshape(N_DEV, -1, D_MODEL) for t in dwbs]
    return jnp.concatenate([dw_kv.reshape(N_DEV, -1, D_MODEL), dw_out.reshape(N_DEV, -1, D_MODEL)] + br + [g_in],
                           axis=1)


def kernel(x, mem, positions, norm_pre_g, norm_post_g, norm_mem_g, w_in, b_forget, b_merge, w_mem_kv, w_branch_a, w_branch_b, w_branch_m, w_out, loss_target, m_norm_pre_g, m_norm_post_g, m_norm_mem_g, m_w_in, m_b_forget, m_b_merge, m_w_mem_kv, m_w_branch_a, m_w_branch_b, m_w_branch_m, m_w_out, v_norm_pre_g, v_norm_post_g, v_norm_mem_g, v_w_in, v_b_forget, v_b_merge, v_w_mem_kv, v_w_branch_a, v_w_branch_b, v_w_branch_m, v_w_out):
    w_rest = _pack_rest(w_mem_kv, w_branch_a, w_branch_b, w_branch_m, w_out)
    shard = jnp.concatenate([w_rest.astype(BF16), w_in[0].T.astype(BF16),
                             jnp.zeros((IN_ROWS - CS, D_MODEL), BF16)], axis=0)
    hs, (gathered,) = _rms_fwd(x[0], norm_pre_g, name="rms_pre_gather", dilations=DIL, comm=_gather_comm(shard))
    wt, w_kv, wbs, w_o = _full_weights(gathered)

    bf_pad = jnp.pad(b_forget, ((0, 0), (0, FB_PAD - B_HEADS)))
    r = _local_step(x[0], mem[0], positions[0], loss_target[0], norm_pre_g, norm_post_g, norm_mem_g,
                    wt, bf_pad, b_merge, w_kv, wbs, w_o, pack=_pack_grads, hs=hs)

    gsmall = jnp.concatenate([r["dg_pre"], r["dg_post"], r["dg_mem"], r["db_merge"],
                              r["db_forget"][:, :LANES], r["loss"]], axis=1)
    rsmall = _gather_small(gsmall, name="gather_small")
    parts, own_idx = r["parts"], r["own_idx"]

    m_rest = _pack_rest(m_w_mem_kv, m_w_branch_a, m_w_branch_b, m_w_branch_m, m_w_out)
    v_rest = _pack_rest(v_w_mem_kv, v_w_branch_a, v_w_branch_b, v_w_branch_m, v_w_out)
    gsum = _sum_parts(parts, own_idx, 0, ROWS, 624, name="sum_grads")
    outs_rest = [_unpack_rest(t) for t in
                 _adamw([(gsum[None], 1)], own_idx, w_rest, m_rest, v_rest, 64, name="adamw_rest")]
    g_in = gsum[RO_IN:RO_IN + CS].T
    outs_in = _adamw([(g_in[None], 1)], own_idx, w_in[0], m_w_in[0], v_w_in[0], 128, name="adamw_w_in")

    def small_vec(a, b, c, d, e):
        z = jnp.zeros((1, LANES - B_HEADS), F32)
        return jnp.concatenate([a, b, c, d, e, z, jnp.zeros((1, LANES), F32)], axis=1)

    outs_small = _adamw([(rsmall, N_DEV)], own_idx, small_vec(norm_pre_g, norm_post_g, norm_mem_g, b_merge, b_forget),
                        small_vec(m_norm_pre_g, m_norm_post_g, m_norm_mem_g, m_b_merge, m_b_forget),
                        small_vec(v_norm_pre_g, v_norm_post_g, v_norm_mem_g, v_b_merge, v_b_forget),
                        1, name="adamw_small")

    def small_parts(t):
        return [t[:, O_GPRE:O_GPRE + D_MODEL], t[:, O_GPOST:O_GPOST + D_MODEL], t[:, O_GMEM:O_GMEM + D_MODEL],
                t[:, O_BF:O_BF + B_HEADS], t[:, O_BM:O_BM + 3 * D_MODEL]]

    loss = outs_small[0][0, O_LOSS]
    result = [loss, r["grad_x"][None]]
    for rest, w_i, small in zip(outs_rest, outs_in, outs_small):
        gp, gq, gm, bf, bm = small_parts(small)
        w_k, w_a, w_b, w_m, w_ot = rest
        result += [gp, gq, gm, w_i[None], bf, bm, w_k, w_a, w_b, w_m, w_ot]
    return tuple(result)
```

```python
import jax
import jax.numpy as jnp
from jax import lax
from jax.experimental import pallas as pl
from jax.experimental.pallas import tpu as pltpu

F32 = jnp.float32
BF16 = jnp.bfloat16

N_DEV = 8
D_MODEL = 1024
N_MEM = 256
EPS = 1e-6
NEG = -1e30
ROPE_THETA = 500000.0
DIL = (1, 4, 16)
A_HEADS = 4
HEAD = 128
A_WIDTH = 512
B_HEADS = 8
B_HEAD = 64
M_HEADS = 4
ROT = 32
IN_COLS = 11272
FB_PAD = 256

SEGS = {
    "A0": ((0, 512), (1536, 2048), (3072, 3584)),
    "A1": ((512, 1024), (2048, 2560), (3584, 4096)),
    "A2": ((1024, 1536), (2560, 3072), (4096, 4608)),
    "B": ((5120, 6656),),
    "R": ((4608, 5120), (6664, 7176), (7176, 7688), (7688, 8200), (8200, 11272), (6656, 6664)),
}
SEG_PAD = {"A0": 0, "A1": 0, "A2": 0, "B": 0, "R": FB_PAD - B_HEADS}
R_ZA, R_ZB, R_QM, R_ZM, R_GL, R_FB = 0, 512, 1024, 1536, 2048, 5120
NR = R_FB + FB_PAD

ADAM_LR, ADAM_B1, ADAM_B2, ADAM_EPS, ADAM_WD, ADAM_STEP = 0.001, 0.9, 0.999, 1e-08, 0.01, 10

LANES = 128
VMEM_LIMIT = 56 * 1024 * 1024

CS = IN_COLS // N_DEV
RO_KV, RO_OUT, RO_BR, RO_IN = 0, 128, 256, 448
IN_ROWS = 1424
ROWS = RO_IN + IN_ROWS
O_GPRE, O_GPOST, O_GMEM, O_BM, O_BF, O_LOSS = 0, 1024, 2048, 3072, 6144, 6272
P_SMALL = 6400


def _cp(sem=None):
    return pltpu.CompilerParams(dimension_semantics=sem, vmem_limit_bytes=VMEM_LIMIT)


def _dot(a, b):
    return jnp.dot(a, b, preferred_element_type=F32)


def _dot_nt(a, b):
    return lax.dot_general(a, b, (((1,), (1,)), ((), ())), preferred_element_type=F32)


def _sigmoid(z):
    return 1.0 / (1.0 + jnp.exp(-z))


def _mm(a, b, *, name, at=False, bt=False, out_dtype=F32, tm=1024, tn=1024, tk=None, comm=None):
    assert not (at and bt)
    K, M = a.shape if at else a.shape[::-1]
    N = b.shape[0] if bt else b.shape[1]
    tm, tn = min(tm, M), min(tn, N)
    tk = K if tk is None else min(tk, K)
    assert M % tm == 0 and N % tn == 0 and K % tk == 0
    nk = K // tk
    grid = (M // tm, N // tn, nk)
    n_in = len(comm["inputs"]) if comm else 0
    n_out = len(comm["out_shape"]) if comm else 0

    def body(a_ref, b_ref, *rest):
        c_in, o_ref, c_out = rest[:n_in], rest[n_in], rest[n_in + 1:n_in + 1 + n_out]
        acc_ref, sems = rest[n_in + 1 + n_out], rest[n_in + 2 + n_out:]
        if comm:
            step = (pl.program_id(0) * grid[1] + pl.program_id(1)) * grid[2] + pl.program_id(2)

            @pl.when(step == 0)
            def _():
                comm["start"](*c_in, *c_out, *sems)

        av = a_ref[...].astype(BF16)
        bv = b_ref[...].astype(BF16)
        if at:
            p = lax.dot_general(av, bv, (((0,), (0,)), ((), ())), preferred_element_type=F32)
        else:
            p = _dot_nt(av, bv) if bt else _dot(av, bv)
        if nk == 1:
            o_ref[...] = p.astype(out_dtype)
        else:
            k = pl.program_id(2)

            @pl.when(k == 0)
            def _():
                acc_ref[...] = p

            @pl.when(k > 0)
            def _():
                acc_ref[...] += p

            @pl.when(k == nk - 1)
            def _():
                o_ref[...] = acc_ref[...].astype(out_dtype)

        if comm:
            @pl.when(step == grid[0] * grid[1] * grid[2] - 1)
            def _():
                comm["wait"](*c_in, *c_out, *sems)

    b_spec = (pl.BlockSpec((tn, tk), lambda i, j, k: (j, k)) if bt
              else pl.BlockSpec((tk, tn), lambda i, j, k: (k, j)))
    a_spec = (pl.BlockSpec((tk, tm), lambda i, j, k: (k, i)) if at
              else pl.BlockSpec((tm, tk), lambda i, j, k: (i, k)))
    out_spec = pl.BlockSpec((tm, tn), lambda i, j, k: (i, j))
    out_shape = jax.ShapeDtypeStruct((M, N), out_dtype)
    acc = pltpu.VMEM((tm, tn) if nk > 1 else (8, LANES), F32)
    if not comm:
        return pl.pallas_call(
            body, name=name, grid=grid, in_specs=[a_spec, b_spec], out_specs=out_spec, out_shape=out_shape,
            scratch_shapes=[acc], compiler_params=_cp(("parallel", "parallel", "arbitrary")))(a, b)
    return pl.pallas_call(
        body, name=name, grid=grid, in_specs=[a_spec, b_spec] + [ANY] * n_in,
        out_specs=[out_spec] + [ANY] * n_out, out_shape=[out_shape] + comm["out_shape"],
        input_output_aliases={2 + i: 1 + o for i, o in comm.get("alias", {}).items()},
        scratch_shapes=[acc] + comm["sems"],
        compiler_params=_cp(("arbitrary", "arbitrary", "arbitrary")))(a, b, *comm["inputs"])


def _mm_sum(pairs, *, name, tm=1024, tk=768, comm=None):
    M, N = pairs[0][0].shape[0], pairs[0][1].shape[1]
    tm = min(tm, M)
    steps = [a.shape[1] // tk for a, _ in pairs]
    assert M % tm == 0 and all(a.shape[1] % tk == 0 for a, _ in pairs)
    first = [sum(steps[:p]) for p in range(len(pairs))]
    total = sum(steps)
    grid = (M // tm, total)
    n_in = len(comm["inputs"]) if comm else 0
    n_out = len(comm["out_shape"]) if comm else 0
    npair = len(pairs)

    def body(*refs):
        ab, rest = refs[:2 * npair], refs[2 * npair:]
        c_in, o_ref, c_out = rest[:n_in], rest[n_in], rest[n_in + 1:n_in + 1 + n_out]
        acc_ref, sems = rest[n_in + 1 + n_out], rest[n_in + 2 + n_out:]
        k = pl.program_id(1)
        if comm:
            step = pl.program_id(0) * total + k

            @pl.when(step == 0)
            def _():
                comm["start"](*c_in, *c_out, *sems)

        @pl.when(k == 0)
        def _():
            acc_ref[...] = jnp.zeros((tm, N), F32)

        for p in range(npair):
            @pl.when(jnp.logical_and(k >= first[p], k < first[p] + steps[p]))
            def _(p=p):
                acc_ref[...] += _dot(ab[2 * p][...], ab[2 * p + 1][...])

        @pl.when(k == total - 1)
        def _():
            o_ref[...] = acc_ref[...]

        if comm:
            @pl.when(step == grid[0] * total - 1)
            def _():
                comm["wait"](*c_in, *c_out, *sems)

    def local(p):
        return lambda k: jnp.clip(k - first[p], 0, steps[p] - 1)

    in_specs = []
    for p in range(npair):
        in_specs += [pl.BlockSpec((tm, tk), lambda i, k, f=local(p): (i, f(k))),
                     pl.BlockSpec((tk, N), lambda i, k, f=local(p): (f(k), 0))]
    out_spec = pl.BlockSpec((tm, N), lambda i, k: (i, 0))
    out_shape = jax.ShapeDtypeStruct((M, N), F32)
    args = [t for pair in pairs for t in pair]
    if not comm:
        return pl.pallas_call(
            body, name=name, grid=grid, in_specs=in_specs, out_specs=out_spec, out_shape=out_shape,
            scratch_shapes=[pltpu.VMEM((tm, N), F32)], compiler_params=_cp(("parallel", "arbitrary")))(*args)
    return pl.pallas_call(
        body, name=name, grid=grid, in_specs=in_specs + [ANY] * n_in,
        out_specs=[out_spec] + [ANY] * n_out, out_shape=[out_shape] + comm["out_shape"],
        scratch_shapes=[pltpu.VMEM((tm, N), F32)] + comm["sems"],
        compiler_params=_cp(("arbitrary", "arbitrary")))(*args, *comm["inputs"])


def _class_spec(S, d, tm, width):
    return pl.BlockSpec((d, tm // d, width), lambda i: (0, i, 0))


def _rms_fwd(x, g, *, name, dilations=(), comm=None):
    S, D = x.shape
    tm = min(512, S)
    ds = [d for d in dilations if d > 1]
    nsteps = S // tm
    n_in = len(comm["inputs"]) if comm else 0
    n_out = len(comm["out_shape"]) if comm else 0
    n_tmp = D // LANES if ds else 0

    def body(x_ref, g_ref, *rest):
        c_in, o_ref, rest = rest[:n_in], rest[n_in], rest[n_in + 1:]
        cls, c_out, rest = rest[:len(ds)], rest[len(ds):len(ds) + n_out], rest[len(ds) + n_out:]
        tmps, sems = rest[:n_tmp], rest[n_tmp:]
        if comm:
            @pl.when(pl.program_id(0) == 0)
            def _():
                comm["start"](*c_in, *c_out, *sems)

        xv = x_ref[...]
        r = lax.rsqrt(jnp.mean(xv * xv, axis=-1, keepdims=True) + EPS)
        hv = xv * r * g_ref[...]
        o_ref[...] = hv.astype(BF16)
        if ds:
            for c, tmp in enumerate(tmps):
                tmp[...] = hv[:, c * LANES:(c + 1) * LANES]
            for c_ref, d in zip(cls, ds):
                for k in range(d):
                    c_ref[k] = jnp.concatenate([tmp[pl.ds(k, tm // d, stride=d), :] for tmp in tmps],
                                               axis=1).astype(BF16)
        if comm:
            @pl.when(pl.program_id(0) == nsteps - 1)
            def _():
                comm["wait"](*c_in, *c_out, *sems)

    row = pl.BlockSpec((tm, D), lambda i: (i, 0))
    outs = pl.pallas_call(
        body, name=name, grid=(nsteps,),
        in_specs=[row, pl.BlockSpec((1, D), lambda i: (0, 0))] + [ANY] * n_in,
        out_specs=[row] + [_class_spec(S, d, tm, D) for d in ds] + [ANY] * n_out,
        out_shape=[jax.ShapeDtypeStruct((S, D), BF16)] + [jax.ShapeDtypeStruct((d, S // d, D), BF16) for d in ds]
        + (comm["out_shape"] if comm else []),
        scratch_shapes=[pltpu.VMEM((tm, LANES), F32)] * n_tmp + (comm["sems"] if comm else []),
        compiler_params=_cp(("arbitrary",) if comm else ("parallel",)),
    )(x, g, *(comm["inputs"] if comm else []))
    rows = [outs[0]] + [o.reshape(S, D) for o in outs[1:1 + len(ds)]]
    if comm:
        return rows, list(outs[1 + len(ds):])
    return rows if ds else rows[0]


def _rms_bwd(x, g, dh, dy, *, name, dh_classes=()):
    S, D = x.shape
    tm = min(512, S)
    want_dx = dy is not None
    nc = len(dh_classes)

    def body(*refs):
        c_refs, refs = refs[:nc], refs[nc:]
        if want_dx:
            x_ref, g_ref, dh_ref, dy_ref, dx_ref, dg_ref = refs[:6]
        else:
            x_ref, g_ref, dh_ref, dg_ref = refs[:4]
        i = pl.program_id(0)
        xv = x_ref[...]
        r = lax.rsqrt(jnp.mean(xv * xv, axis=-1, keepdims=True) + EPS)
        xh = xv * r
        if nc:
            tmps = refs[-(D // LANES):]
            cols = [slice(c * LANES, (c + 1) * LANES) for c in range(D // LANES)]
            for tmp, cs in zip(tmps, cols):
                tmp[...] = dh_ref[:, cs]
            for c_ref, (_, d) in zip(c_refs, dh_classes):
                for k in range(d):
                    for tmp, cs in zip(tmps, cols):
                        tmp[pl.ds(k, tm // d, stride=d), :] += c_ref[k, :, cs]
            dhv = jnp.concatenate([tmp[...] for tmp in tmps], axis=1)
        else:
            dhv = dh_ref[...]
        part = jnp.sum(dhv * xh, axis=0, keepdims=True)

        @pl.when(i == 0)
        def _():
            dg_ref[...] = part

        @pl.when(i > 0)
        def _():
            dg_ref[...] += part

        if want_dx:
            dxh = dhv * g_ref[...]
            dx_ref[...] = dy_ref[...] + r * (dxh - xh * jnp.mean(dxh * xh, axis=-1, keepdims=True))

    row = pl.BlockSpec((tm, D), lambda i: (i, 0))
    vec = pl.BlockSpec((1, D), lambda i: (0, 0))
    c_specs = [_class_spec(S, d, tm, D) for _, d in dh_classes]
    c_args = [a.reshape(d, S // d, D) for a, d in dh_classes]
    scratch = [pltpu.VMEM((tm, LANES), F32)] * (D // LANES) if nc else []
    if want_dx:
        return pl.pallas_call(
            body, name=name, grid=(S // tm,), in_specs=c_specs + [row, vec, row, row], out_specs=[row, vec],
            out_shape=[jax.ShapeDtypeStruct((S, D), F32), jax.ShapeDtypeStruct((1, D), F32)],
            scratch_shapes=scratch, compiler_params=_cp(("arbitrary",)))(*c_args, x, g, dh, dy)
    return pl.pallas_call(
        body, name=name, grid=(S // tm,), in_specs=c_specs + [row, vec, row], out_specs=vec,
        out_shape=jax.ShapeDtypeStruct((1, D), F32),
        scratch_shapes=scratch, compiler_params=_cp(("arbitrary",)))(*c_args, x, g, dh)


def _post(x, out, tgt, g, *, name):
    S, D = x.shape
    tm = min(512, S)

    def body(x_ref, o_ref, t_ref, g_ref, dy_ref, do_ref, dg_ref, loss_ref):
        i = pl.program_id(0)
        ov = o_ref[...]
        r = lax.rsqrt(jnp.mean(ov * ov, axis=-1, keepdims=True) + EPS)
        n = ov * r
        gv = g_ref[...]
        e = (x_ref[...] + n * gv) - t_ref[...]
        lpart = 0.5 * jnp.sum(jnp.mean(e * e, axis=-1, keepdims=True), axis=0, keepdims=True)
        dy = e * (1.0 / D)
        dy_ref[...] = dy
        dn = dy * gv
        do_ref[...] = (r * (dn - n * jnp.mean(dn * n, axis=-1, keepdims=True))).astype(BF16)
        gpart = jnp.sum(dy * n, axis=0, keepdims=True)
        lrow = jnp.broadcast_to(lpart, (1, LANES))

        @pl.when(i == 0)
        def _():
            dg_ref[...] = gpart
            loss_ref[...] = lrow

        @pl.when(i > 0)
        def _():
            dg_ref[...] += gpart
            loss_ref[...] += lrow

    row = pl.BlockSpec((tm, D), lambda i: (i, 0))
    vec = pl.BlockSpec((1, D), lambda i: (0, 0))
    return pl.pallas_call(
        body, name=name, grid=(S // tm,), in_specs=[row, row, row, vec],
        out_specs=[row, row, vec, pl.BlockSpec((1, LANES), lambda i: (0, 0))],
        out_shape=[jax.ShapeDtypeStruct((S, D), F32), jax.ShapeDtypeStruct((S, D), BF16),
                   jax.ShapeDtypeStruct((1, D), F32), jax.ShapeDtypeStruct((1, LANES), F32)],
        compiler_params=_cp(("arbitrary",)))(x, out, tgt, g)


def _to_classes(t, d):
    if d == 1:
        return t
    S, C = t.shape
    return t.reshape(S // d, d, C).transpose(1, 0, 2).reshape(S, C)


def _rope(x, c, s1, s2):
    return x * c + pltpu.roll(x, LANES - ROT // 2, 1) * s1 + pltpu.roll(x, ROT // 2, 1) * s2


def _unrope(d, c, s1, s2):
    return d * c + pltpu.roll(d * s1, ROT // 2, 1) + pltpu.roll(d * s2, LANES - ROT // 2, 1)


def _a_band(qb):
    r = lax.broadcasted_iota(jnp.int32, (qb, qb + HEAD), 0)
    c = lax.broadcasted_iota(jnp.int32, (qb, qb + HEAD), 1)
    return jnp.logical_and(c >= r, c <= r + HEAD)


def _a_first_ok(qb, n):
    c = lax.broadcasted_iota(jnp.int32, (qb, qb + HEAD), 1)
    return jnp.logical_or(c >= HEAD, n > 0)


def _a_last_ok(qb, has_next):
    c = lax.broadcasted_iota(jnp.int32, (qb, qb + HEAD), 1)
    return jnp.logical_or(c < qb, has_next)


A_SCALE = HEAD ** -0.5


def _a_geometry(S, g):
    d = DIL[g]
    L = S // d
    TQ = min(512, L)
    return d, L, TQ, TQ // HEAD, L // TQ, L // HEAD


def _proj_rope(h, w, tabs, *, name):
    S, D = h.shape
    tm = min(512, S)

    def body(h_ref, w_ref, c_ref, s1_ref, s2_ref, o_ref):
        tc = (c_ref[...], s1_ref[...], s2_ref[...])
        u = _dot_nt(h_ref[...], w_ref[...])
        for j in range(3 * A_HEADS):
            sl = slice(j * HEAD, (j + 1) * HEAD)
            if j < A_HEADS:
                o_ref[:, sl] = (_rope(u[:, sl], *tc) * A_SCALE).astype(BF16)
            else:
                o_ref[:, sl] = (_rope(u[:, sl], *tc) if j < 2 * A_HEADS else u[:, sl]).astype(BF16)

    tab = pl.BlockSpec((tm, LANES), lambda i: (i, 0))
    return pl.pallas_call(
        body, name=name, grid=(S // tm,),
        in_specs=[pl.BlockSpec((tm, D), lambda i: (i, 0)), pl.BlockSpec((3 * A_WIDTH, D), lambda i: (0, 0)),
                  tab, tab, tab],
        out_specs=pl.BlockSpec((tm, 3 * A_WIDTH), lambda i: (i, 0)),
        out_shape=jax.ShapeDtypeStruct((S, 3 * A_WIDTH), BF16),
        compiler_params=_cp(("parallel",)))(h, w, *tabs)


def _attn_a_fwd(qkv, g, *, name):
    S = qkv.shape[0]
    d, L, TQ, nsub, nb, nblk = _a_geometry(S, g)

    def body(q_ref, kc_ref, kp_ref, vc_ref, vp_ref, o_ref, l_ref):
        n = pl.program_id(1)
        QB = min(2 * HEAD, TQ)
        band = _a_band(QB)
        first = jnp.logical_and(band, _a_first_ok(QB, n))
        for h in range(A_HEADS):
            hs = slice(h * HEAD, (h + 1) * HEAD)
            for hh in range(TQ // QB):
                sl = slice(hh * QB, (hh + 1) * QB)
                pv = slice(hh * QB - HEAD, hh * QB)
                kcat = jnp.concatenate([kp_ref[:, hs] if hh == 0 else kc_ref[pv, hs], kc_ref[sl, hs]], axis=0)
                vcat = jnp.concatenate([vp_ref[:, hs] if hh == 0 else vc_ref[pv, hs], vc_ref[sl, hs]], axis=0)
                s = jnp.where(first if hh == 0 else band, _dot_nt(q_ref[sl, hs], kcat), NEG)
                m = jnp.max(s, axis=-1, keepdims=True)
                p = jnp.exp(s - m)
                den = jnp.sum(p, axis=-1, keepdims=True)
                o_ref[sl, hs] = _dot(p.astype(BF16), vcat) / den
                l_ref[sl, hs] = jnp.broadcast_to(m + jnp.log(den), (QB, HEAD))

    rcur = lambda r, n: r * nb + n
    rprv = lambda r, n: r * nblk + jnp.maximum(n * nsub - 1, 0)
    cur = lambda off: pl.BlockSpec((TQ, A_WIDTH), lambda r, n: (rcur(r, n), off))
    prv = lambda off: pl.BlockSpec((HEAD, A_WIDTH), lambda r, n: (rprv(r, n), off))
    out = pl.BlockSpec((TQ, A_WIDTH), lambda r, n: (rcur(r, n), 0))
    return pl.pallas_call(
        body, name=name, grid=(d, nb),
        in_specs=[cur(0), cur(1), prv(1), cur(2), prv(2)],
        out_specs=[out, out],
        out_shape=[jax.ShapeDtypeStruct((S, A_WIDTH), F32)] * 2,
        compiler_params=_cp(("parallel", "parallel")),
    )(qkv, qkv, qkv, qkv, qkv)


def _attn_a_dq(qkv, tabs, g, do, lse, adj, du, *, name):
    S = qkv.shape[0]
    d, L, TQ, nsub, nb, nblk = _a_geometry(S, g)

    def body(q_ref, kc_ref, kp_ref, vc_ref, vp_ref, do_ref, l_ref, adj_ref, c_ref, s1_ref, s2_ref, du_ref, dq_ref):
        n = pl.program_id(1)
        QB = min(2 * HEAD, TQ)
        band = _a_band(QB)
        first = jnp.logical_and(band, _a_first_ok(QB, n))
        for h in range(A_HEADS):
            hs = slice(h * HEAD, (h + 1) * HEAD)
            for hh in range(TQ // QB):
                sl = slice(hh * QB, (hh + 1) * QB)
                pv = slice(hh * QB - HEAD, hh * QB)
                kcat = jnp.concatenate([kp_ref[:, hs] if hh == 0 else kc_ref[pv, hs], kc_ref[sl, hs]], axis=0)
                vcat = jnp.concatenate([vp_ref[:, hs] if hh == 0 else vc_ref[pv, hs], vc_ref[sl, hs]], axis=0)
                s = jnp.where(first if hh == 0 else band, _dot_nt(q_ref[sl, hs], kcat), NEG)
                p = jnp.exp(s - l_ref[sl, hs][:, :1])
                ds = p * (_dot_nt(do_ref[sl, hs], vcat) + adj_ref[sl, hs][:, :1])
                dq = _dot(ds.astype(BF16), kcat) * A_SCALE
                dq_ref[sl, hs] = _unrope(dq, c_ref[sl, :], s1_ref[sl, :], s2_ref[sl, :]).astype(BF16)

    rcur = lambda r, n: r * nb + n
    rprv = lambda r, n: r * nblk + jnp.maximum(n * nsub - 1, 0)
    cur = lambda off: pl.BlockSpec((TQ, A_WIDTH), lambda r, n: (rcur(r, n), off))
    prv = lambda off: pl.BlockSpec((HEAD, A_WIDTH), lambda r, n: (rprv(r, n), off))
    tcur = pl.BlockSpec((TQ, LANES), lambda r, n: (rcur(r, n), 0))
    blk = cur(0)
    return pl.pallas_call(
        body, name=name, grid=(d, nb),
        in_specs=[cur(0), cur(1), prv(1), cur(2), prv(2), blk, blk, blk, tcur, tcur, tcur, ANY],
        out_specs=blk,
        out_shape=jax.ShapeDtypeStruct((S, 3 * A_WIDTH), BF16),
        input_output_aliases={11: 0},
        compiler_params=_cp(("parallel", "parallel")),
    )(qkv, qkv, qkv, qkv, qkv, do, lse, adj, *tabs, du)


def _attn_a_dkv(qkv, tabs, g, do, lse, adj, *, name):
    S = qkv.shape[0]
    d, L, TQ, nsub, nb, nblk = _a_geometry(S, g)

    def body(qc_ref, qn_ref, kc_ref, vc_ref, doc_ref, don_ref, lc_ref, ln_ref, ac_ref, an_ref,
             c_ref, s1_ref, s2_ref, du_ref):
        n = pl.program_id(1)
        QB = min(2 * HEAD, TQ)
        nh = TQ // QB
        band = _a_band(QB)
        end = jnp.logical_and(band, _a_last_ok(QB, n < nb - 1))
        for h in range(A_HEADS):
            hs = slice(h * HEAD, (h + 1) * HEAD)
            for kh in range(nh):
                sl = slice(kh * QB, (kh + 1) * QB)
                nx = slice((kh + 1) * QB, (kh + 1) * QB + HEAD)
                last = kh == nh - 1
                cat = lambda cur, nxt: jnp.concatenate([cur[sl, hs], nxt[:, hs] if last else cur[nx, hs]], axis=0)
                qcat = cat(qc_ref, qn_ref)
                docat = cat(doc_ref, don_ref)
                lt = cat(lc_ref, ln_ref).T[:1, :]
                at = cat(ac_ref, an_ref).T[:1, :]
                st = jnp.where(end if last else band, _dot_nt(kc_ref[sl, hs], qcat), NEG)
                pt = jnp.exp(st - lt)
                dv_cols = slice(2 * A_WIDTH + h * HEAD, 2 * A_WIDTH + (h + 1) * HEAD)
                dk_cols = slice(A_WIDTH + h * HEAD, A_WIDTH + (h + 1) * HEAD)
                du_ref[sl, dv_cols] = _dot(pt.astype(BF16), docat).astype(BF16)
                dst = pt * (_dot_nt(vc_ref[sl, hs], docat) + at)
                dk = _dot(dst.astype(BF16), qcat)
                du_ref[sl, dk_cols] = _unrope(dk, c_ref[sl, :], s1_ref[sl, :], s2_ref[sl, :]).astype(BF16)

    rcur = lambda r, n: r * nb + n
    rnxt = lambda r, n: r * nblk + jnp.minimum((n + 1) * nsub, nblk - 1)
    cur = lambda off: pl.BlockSpec((TQ, A_WIDTH), lambda r, n: (rcur(r, n), off))
    nxu = lambda off: pl.BlockSpec((HEAD, A_WIDTH), lambda r, n: (rnxt(r, n), off))
    tcur = pl.BlockSpec((TQ, LANES), lambda r, n: (rcur(r, n), 0))
    blk, bnx = cur(0), nxu(0)
    return pl.pallas_call(
        body, name=name, grid=(d, nb),
        in_specs=[cur(0), nxu(0), cur(1), cur(2), blk, bnx, blk, bnx, blk, bnx, tcur, tcur, tcur],
        out_specs=pl.BlockSpec((TQ, 3 * A_WIDTH), lambda r, n: (rcur(r, n), 0)),
        out_shape=jax.ShapeDtypeStruct((S, 3 * A_WIDTH), BF16),
        compiler_params=_cp(("parallel", "parallel")),
    )(qkv, qkv, qkv, qkv, do, do, lse, lse, adj, adj, *tabs)


def _silu_parts(z):
    sg = _sigmoid(z)
    return z * sg, sg * (1.0 + z * (1.0 - sg))


def _classes_to_tokens(c_ref, d, tm, tmps):
    if d == 1:
        return c_ref[...].astype(F32)
    for k in range(d):
        for c, tmp in enumerate(tmps):
            tmp[pl.ds(k, tm // d, stride=d), :] = c_ref[k, :, c * LANES:(c + 1) * LANES].astype(F32)
    return jnp.concatenate([tmp[...] for tmp in tmps], axis=1)


def _tokens_to_classes(val, c_ref, d, tm, tmps):
    if d == 1:
        c_ref[...] = val.astype(c_ref.dtype)
        return
    for c, tmp in enumerate(tmps):
        tmp[...] = val[:, c * LANES:(c + 1) * LANES]
    for k in range(d):
        c_ref[k] = jnp.concatenate([tmp[pl.ds(k, tm // d, stride=d), :] for tmp in tmps], axis=1).astype(c_ref.dtype)


def _group_spec(S, d, tm):
    if d == 1:
        return pl.BlockSpec((tm, A_WIDTH), lambda i: (i, 0))
    return _class_spec(S, d, tm, A_WIDTH)


def _group_view(t, d):
    return t if d == 1 else t.reshape(d, t.shape[0] // d, t.shape[1])


def _merge_a_fwd(os_, ls_, ur, *, name):
    S = ur.shape[0]
    tm = min(512, S)

    def body(o0, o1, o2, l0, l1, l2, z_ref, y_ref, *tmps):
        ls = [_classes_to_tokens(r, d, tm, tmps) for r, d in zip((l0, l1, l2), DIL)]
        ov = [_classes_to_tokens(r, d, tm, tmps) for r, d in zip((o0, o1, o2), DIL)]
        mx = jnp.maximum(jnp.maximum(ls[0], ls[1]), ls[2])
        es = [jnp.exp(l - mx) for l in ls]
        den = es[0] + es[1] + es[2]
        y = (es[0] / den) * ov[0] + (es[1] / den) * ov[1] + (es[2] / den) * ov[2]
        y_ref[...] = (y * _silu_parts(z_ref[...])[0]).astype(BF16)

    blk = pl.BlockSpec((tm, A_WIDTH), lambda i: (i, 0))
    groups = [_group_spec(S, d, tm) for d in DIL]
    return pl.pallas_call(
        body, name=name, grid=(S // tm,),
        in_specs=groups + groups + [pl.BlockSpec((tm, A_WIDTH), lambda i: (i, R_ZA // A_WIDTH))],
        out_specs=blk, out_shape=jax.ShapeDtypeStruct((S, A_WIDTH), BF16),
        scratch_shapes=[pltpu.VMEM((tm, LANES), F32)] * (A_WIDTH // LANES),
        compiler_params=_cp(("parallel",)))(*[_group_view(t, d) for t, d in zip(os_, DIL)],
                                            *[_group_view(t, d) for t, d in zip(ls_, DIL)], ur)


def _merge_a_bwd(os_, ls_, ur, dya, du_r, *, name):
    S = ur.shape[0]
    tm = min(256, S)

    def body(o0, o1, o2, l0, l1, l2, z_ref, dy_ref, du_in, d0, d1, d2, a0, a1, a2, dz_ref, *tmps):
        ls = [_classes_to_tokens(r, d, tm, tmps) for r, d in zip((l0, l1, l2), DIL)]
        ov = [_classes_to_tokens(r, d, tm, tmps) for r, d in zip((o0, o1, o2), DIL)]
        mx = jnp.maximum(jnp.maximum(ls[0], ls[1]), ls[2])
        es = [jnp.exp(l - mx) for l in ls]
        den = es[0] + es[1] + es[2]
        ws = [e / den for e in es]
        y = ws[0] * ov[0] + ws[1] * ov[1] + ws[2] * ov[2]
        sz, dsz = _silu_parts(z_ref[...])
        dyv = dy_ref[...]
        dz_ref[...] = (dyv * y * dsz).astype(BF16)
        dyp = dyv * sz
        ts = []
        for h in range(A_HEADS):
            sl = slice(h * HEAD, (h + 1) * HEAD)
            t = jnp.zeros((tm, 1), F32)
            for gi in range(3):
                t = t + ws[gi][:, sl][:, :1] * jnp.sum(dyp[:, sl] * ov[gi][:, sl], axis=-1, keepdims=True)
            ts.append(jnp.broadcast_to(t, (tm, HEAD)))
        tb = jnp.concatenate(ts, axis=1)
        for gi, (dref, aref) in enumerate(((d0, a0), (d1, a1), (d2, a2))):
            _tokens_to_classes(ws[gi] * dyp, dref, DIL[gi], tm, tmps)
            _tokens_to_classes(-ws[gi] * tb, aref, DIL[gi], tm, tmps)

    blk = pl.BlockSpec((tm, A_WIDTH), lambda i: (i, 0))
    groups = [_group_spec(S, d, tm) for d in DIL]
    shaped = lambda dt: [jax.ShapeDtypeStruct((S, A_WIDTH) if d == 1 else (d, S // d, A_WIDTH), dt) for d in DIL]
    outs = pl.pallas_call(
        body, name=name, grid=(S // tm,),
        in_specs=groups + groups + [pl.BlockSpec((tm, A_WIDTH), lambda i: (i, R_ZA // A_WIDTH)), blk, ANY],
        out_specs=groups + groups + [pl.BlockSpec((tm, A_WIDTH), lambda i: (i, R_ZA // A_WIDTH))],
        out_shape=shaped(BF16) + shaped(F32) + [jax.ShapeDtypeStruct(du_r.shape, BF16)],
        input_output_aliases={8: 6},
        scratch_shapes=[pltpu.VMEM((tm, LANES), F32)] * (A_WIDTH // LANES),
        compiler_params=_cp(("parallel",)))(*[_group_view(t, d) for t, d in zip(os_, DIL)],
                                            *[_group_view(t, d) for t, d in zip(ls_, DIL)], ur, dya, du_r)
    flat = [t.reshape(S, A_WIDTH) for t in outs[:6]]
    return flat[0:3], flat[3:6], outs[6]


def _logf(ur, bf_pad, *, name):
    S = ur.shape[0]
    tm = min(1024, S)

    def body(u_ref, b_ref, o_ref):
        z = u_ref[...] + b_ref[...]
        o_ref[...] = jnp.minimum(z, 0.0) - jnp.log(1.0 + jnp.exp(-jnp.abs(z)))

    return pl.pallas_call(
        body, name=name, grid=(S // tm,),
        in_specs=[pl.BlockSpec((tm, FB_PAD), lambda i: (i, R_FB // FB_PAD)),
                  pl.BlockSpec((1, FB_PAD), lambda i: (0, 0))],
        out_specs=pl.BlockSpec((tm, FB_PAD), lambda i: (i, 0)),
        out_shape=jax.ShapeDtypeStruct((S, FB_PAD), F32),
        compiler_params=_cp(("parallel",)))(ur, bf_pad)


def _cumsum_lanes(x, reverse, *, name):
    nt, H, _ = x.shape
    R = nt * H

    def body(x_ref, o_ref):
        v = x_ref[...].reshape(R, LANES)
        lane = lax.broadcasted_iota(jnp.int32, (R, LANES), 1)
        row = lax.broadcasted_iota(jnp.int32, (R, LANES), 0)

        def scan(t, step, idx, n, axis):
            while step < n:
                if reverse:
                    t = t + jnp.where(idx < n - step, pltpu.roll(t, n - step, axis), 0.0)
                else:
                    t = t + jnp.where(idx >= step, pltpu.roll(t, step, axis), 0.0)
                step *= 2
            return t

        v = scan(v, 1, lane, LANES, 1)
        total = jnp.broadcast_to(v[:, :1] if reverse else v[:, LANES - 1:], (R, LANES))
        carry = scan(total, H, row, R, 0) - total
        o_ref[...] = (v + carry).reshape(nt, H, LANES)

    return pl.pallas_call(
        body, name=name, out_shape=jax.ShapeDtypeStruct((nt, H, LANES), F32),
        in_specs=[pl.BlockSpec(memory_space=pltpu.VMEM)], out_specs=pl.BlockSpec(memory_space=pltpu.VMEM),
        compiler_params=_cp())(x)


B_SCALE = B_HEAD ** -0.5


def _pair_masks():
    lane = lax.broadcasted_iota(jnp.int32, (1, LANES), 1)
    row = lax.broadcasted_iota(jnp.int32, (LANES, 1), 0)
    return (lane < B_HEAD, lane >= B_HEAD), (row < B_HEAD, row >= B_HEAD)


def _causal_t(T):
    r = lax.broadcasted_iota(jnp.int32, (T, T), 0)
    c = lax.broadcasted_iota(jnp.int32, (T, T), 1)
    return r <= c


def _zero_other(x, keep):
    return jnp.where(keep, x, jnp.zeros_like(x))


def _fox_aug(ub, c, *, name):
    S = ub.shape[0]
    T = min(2048, S)

    def body(q_ref, k_ref, c_ref, qa_ref, ka_ref):
        lane = lax.broadcasted_iota(jnp.int32, (1, LANES), 1)
        q = q_ref[...] * B_SCALE
        k = k_ref[...]
        for a in range(2):
            own = (lane < B_HEAD) if a == 0 else (lane >= B_HEAD)
            o = B_HEAD if a == 0 else 0
            cv = jnp.broadcast_to(c_ref[:, a:a + 1], (T, LANES))
            hi = cv.astype(BF16)
            r1 = cv - hi.astype(F32)
            mid = r1.astype(BF16)
            lo = (r1 - mid.astype(F32)).astype(BF16)
            pieces = (hi, mid, lo)
            one = jnp.ones((T, LANES), BF16)
            qa = jnp.where(own, q, jnp.zeros_like(q))
            ka = jnp.where(own, k, jnp.zeros_like(k))
            for t in range(3):
                qa = jnp.where(lane == o + t, pieces[t], qa)
                qa = jnp.where(lane == o + 3 + t, one, qa)
                ka = jnp.where(lane == o + t, one, ka)
                ka = jnp.where(lane == o + 3 + t, -pieces[t], ka)
            qa_ref[a] = qa
            ka_ref[a] = ka

    out = pl.BlockSpec((2, T, LANES), lambda h, i: (h, i, 0))
    c_pairs = c.reshape(B_HEADS // 2, 2, S).transpose(0, 2, 1)
    return pl.pallas_call(
        body, name=name, grid=(B_HEADS // 2, S // T),
        in_specs=[pl.BlockSpec((T, LANES), lambda h, i: (i, h)), pl.BlockSpec((T, LANES), lambda h, i: (i, 4 + h)),
                  pl.BlockSpec((None, T, 2), lambda h, i: (h, i, 0))],
        out_specs=[out, out], out_shape=[jax.ShapeDtypeStruct((B_HEADS, S, LANES), BF16)] * 2,
        compiler_params=_cp(("parallel", "parallel")))(ub, ub, c_pairs)


def _fox_fwd(qaug, kaug, vt, *, name):
    S = qaug.shape[1]
    T = min(512, S)
    nq = S // T

    def body(q_ref, k_ref, vt_ref, o_ref, l_ref, m_s, l_s, acc_s, st_s):
        i = pl.program_id(1)
        _, rows = _pair_masks()
        qm = [q_ref[0], q_ref[1]]
        m_s[...] = jnp.full((2, 1, T), NEG, F32)
        l_s[...] = jnp.zeros((2, 1, T), F32)
        acc_s[...] = jnp.zeros((LANES, T), F32)

        def logits(j):
            off = pl.multiple_of(j * T, T)
            return [_dot_nt(k_ref[a, pl.ds(off, T), :], qm[a]) for a in range(2)]

        def step(j, masked, prefetch):
            nxt = logits(j + 1) if prefetch else None
            vtj = vt_ref[j]
            vtm = [_zero_other(vtj, rows[0]), _zero_other(vtj, rows[1])]
            causal = _causal_t(T) if masked else None
            hw = min(2 * LANES, T)
            for hq in range(T // hw):
                cs = slice(hq * hw, (hq + 1) * hw)
                upd = jnp.zeros((LANES, hw), F32)
                alphas = []
                for a in range(2):
                    st = st_s[a, :, cs]
                    if masked:
                        st = jnp.where(causal[:, cs], st, NEG)
                    m_old = m_s[a, :, cs]
                    m_new = jnp.maximum(m_old, jnp.max(st, axis=0, keepdims=True))
                    alpha = jnp.exp(m_old - m_new)
                    pt = jnp.exp(st - m_new)
                    l_s[a, :, cs] = alpha * l_s[a, :, cs] + jnp.sum(pt, axis=0, keepdims=True)
                    m_s[a, :, cs] = m_new
                    upd = upd + _dot(vtm[a], pt.astype(BF16))
                    alphas.append(alpha)
                acc_s[:, cs] = acc_s[:, cs] * jnp.where(rows[0], alphas[0], alphas[1]) + upd
            if prefetch:
                st_s[0] = nxt[0]
                st_s[1] = nxt[1]

        def loop(j, carry):
            step(j, False, True)
            return carry

        first = logits(0)
        st_s[0] = first[0]
        st_s[1] = first[1]
        lax.fori_loop(0, i, loop, 0)
        step(i, True, False)
        o_ref[...] = (acc_s[...] / jnp.where(rows[0], l_s[0], l_s[1])).T
        l_ref[0] = m_s[0] + jnp.log(l_s[0])
        l_ref[1] = m_s[1] + jnp.log(l_s[1])

    stat = pl.BlockSpec((2, None, 1, T), lambda h, i: (h, i, 0, 0))
    return pl.pallas_call(
        body, name=name, grid=(B_HEADS // 2, nq),
        in_specs=[pl.BlockSpec((2, T, LANES), lambda h, i: (h, i, 0)),
                  pl.BlockSpec((2, S, LANES), lambda h, i: (h, 0, 0)),
                  pl.BlockSpec((nq, LANES, T), lambda h, i: (0, h, 0))],
        out_specs=[pl.BlockSpec((T, LANES), lambda h, i: (i, h)), stat],
        out_shape=[jax.ShapeDtypeStruct((S, A_WIDTH), F32), jax.ShapeDtypeStruct((B_HEADS, nq, 1, T), F32)],
        scratch_shapes=[pltpu.VMEM((2, 1, T), F32), pltpu.VMEM((2, 1, T), F32), pltpu.VMEM((LANES, T), F32),
                        pltpu.VMEM((2, T, T), F32)],
        compiler_params=_cp(("parallel", "parallel")),
    )(qaug, kaug, vt)


def _fox_delta(o, do, *, name):
    S = o.shape[0]
    T = min(512, S)
    nq = S // T

    per = min(4, nq)

    def body(o_ref, do_ref, d_ref):
        _, rows = _pair_masks()
        for t in range(per):
            sl = slice(t * T, (t + 1) * T)
            prod_t = (do_ref[sl, :].astype(F32) * o_ref[sl, :]).T
            d_ref[0, t] = jnp.sum(_zero_other(prod_t, rows[0]), axis=0, keepdims=True)
            d_ref[1, t] = jnp.sum(_zero_other(prod_t, rows[1]), axis=0, keepdims=True)

    tile = pl.BlockSpec((per * T, LANES), lambda h, i: (i, h))
    return pl.pallas_call(
        body, name=name, grid=(B_HEADS // 2, nq // per), in_specs=[tile, tile],
        out_specs=pl.BlockSpec((2, per, 1, T), lambda h, i: (h, i, 0, 0)),
        out_shape=jax.ShapeDtypeStruct((B_HEADS, nq, 1, T), F32),
        compiler_params=_cp(("parallel", "parallel")))(o, do)


def _fox_bwd(ub, qaug, kaug, kt, do, lse, delta, *, name):
    S = ub.shape[0]
    T = min(512, S)
    nq = S // T

    def body(k_ref, v_ref, kt_ref, q_ref, do_ref, l_ref, dl_ref,
             dk_ref, dv_ref, dck_ref, dqt_ref, dcq_ref, dk_s, dv_s, dc_s):
        j = pl.program_id(1)
        lanes, rows = _pair_masks()
        vv = v_ref[...]
        ktj = kt_ref[...]
        km = [k_ref[0], k_ref[1]]
        ktm = [_zero_other(ktj, rows[0]), _zero_other(ktj, rows[1])]
        dk_s[...] = jnp.zeros((2, T, LANES), F32)
        dv_s[...] = jnp.zeros((T, LANES), F32)
        dc_s[...] = jnp.zeros((2, T, 1), F32)

        @pl.when(j == 0)
        def _():
            dqt_ref[...] = jnp.zeros((nq, LANES, T), F32)
            dcq_ref[...] = jnp.zeros((2, nq, 1, T), F32)

        def step(i, masked):
            off = pl.multiple_of(i * T, T)
            doi = do_ref[pl.ds(off, T), :]
            upd = jnp.zeros((LANES, T), F32)
            for a in range(2):
                qi = q_ref[a, pl.ds(off, T), :]
                st = _dot_nt(km[a], qi)
                if masked:
                    st = jnp.where(_causal_t(T), st, NEG)
                pt = jnp.exp(st - l_ref[a, i])
                doa = _zero_other(doi, lanes[a])
                dv_s[...] += _dot(pt.astype(BF16), doa)
                dst = pt * (_dot_nt(vv, doa) - dl_ref[a, i])
                dsb = dst.astype(BF16)
                dk_s[a] += _dot(dsb, qi)
                upd = upd + _dot(ktm[a], dsb)
                dc_s[a] -= jnp.sum(dst, axis=-1, keepdims=True)
                dcq_ref[a, i] += jnp.sum(dst, axis=0, keepdims=True)
            dqt_ref[i] += upd

        def loop(i, carry):
            step(i, False)
            return carry

        step(j, True)
        lax.fori_loop(j + 1, nq, loop, 0)
        dk_ref[...] = jnp.where(lanes[0], dk_s[0], dk_s[1]).astype(BF16)
        dv_ref[...] = dv_s[...].astype(BF16)
        dck_ref[...] = dc_s[...]

    rowv = pl.BlockSpec((2, nq, 1, T), lambda h, j: (h, 0, 0, 0))
    tile = pl.BlockSpec((T, LANES), lambda h, j: (j, h))
    return pl.pallas_call(
        body, name=name, grid=(B_HEADS // 2, nq),
        in_specs=[pl.BlockSpec((2, T, LANES), lambda h, j: (h, j, 0)),
                  pl.BlockSpec((T, LANES), lambda h, j: (j, 8 + h)),
                  pl.BlockSpec((None, LANES, T), lambda h, j: (j, h, 0)),
                  pl.BlockSpec((2, S, LANES), lambda h, j: (h, 0, 0)),
                  pl.BlockSpec((S, LANES), lambda h, j: (0, h)),
                  rowv, rowv],
        out_specs=[tile, tile, pl.BlockSpec((2, T, 1), lambda h, j: (h, j, 0)),
                   pl.BlockSpec((nq, LANES, T), lambda h, j: (0, h, 0)), rowv],
        out_shape=[jax.ShapeDtypeStruct((S, A_WIDTH), BF16)] * 2 + [jax.ShapeDtypeStruct((B_HEADS, S, 1), F32),
                   jax.ShapeDtypeStruct((nq, A_WIDTH, T), F32), jax.ShapeDtypeStruct((B_HEADS, nq, 1, T), F32)],
        scratch_shapes=[pltpu.VMEM((2, T, LANES), F32), pltpu.VMEM((T, LANES), F32), pltpu.VMEM((2, T, 1), F32)],
        compiler_params=_cp(("parallel", "arbitrary")),
    )(kaug, ub, kt, qaug, do, lse, delta)


def _gate_fwd(o, ur, zcol, *, name):
    S = ur.shape[0]
    tm = min(1024, S)

    def body(o_ref, z_ref, y_ref):
        y_ref[...] = (o_ref[...] * _silu_parts(z_ref[...])[0]).astype(BF16)

    blk = pl.BlockSpec((tm, A_WIDTH), lambda i: (i, 0))
    return pl.pallas_call(
        body, name=name, grid=(S // tm,),
        in_specs=[blk, pl.BlockSpec((tm, A_WIDTH), lambda i: (i, zcol // A_WIDTH))],
        out_specs=blk, out_shape=jax.ShapeDtypeStruct((S, A_WIDTH), BF16),
        compiler_params=_cp(("parallel",)))(o, ur)


def _gate_bwd(o, ur, zcol, dy, du_r, *, name):
    S = ur.shape[0]
    tm = min(1024, S)

    def body(o_ref, z_ref, dy_ref, du_in, do_ref, dz_ref):
        sz, dsz = _silu_parts(z_ref[...])
        dyv = dy_ref[...]
        do_ref[...] = (dyv * sz).astype(BF16)
        dz_ref[...] = (dyv * o_ref[...] * dsz).astype(BF16)

    blk = pl.BlockSpec((tm, A_WIDTH), lambda i: (i, 0))
    gate = pl.BlockSpec((tm, A_WIDTH), lambda i: (i, zcol // A_WIDTH))
    return pl.pallas_call(
        body, name=name, grid=(S // tm,),
        in_specs=[blk, gate, blk, ANY],
        out_specs=[blk, gate],
        out_shape=[jax.ShapeDtypeStruct((S, A_WIDTH), BF16), jax.ShapeDtypeStruct(du_r.shape, BF16)],
        input_output_aliases={3: 1},
        compiler_params=_cp(("parallel",)))(o, ur, dy, du_r)


def _dfb(ur, bf_pad, dlogf_pad, du_r, *, name):
    S = ur.shape[0]
    tm = min(1024, S)

    def body(u_ref, b_ref, d_ref, du_in, o_ref, s_ref):
        i = pl.program_id(0)
        dv = d_ref[...] * _sigmoid(-(u_ref[...] + b_ref[...]))
        o_ref[...] = dv.astype(BF16)
        part = jnp.sum(dv, axis=0, keepdims=True)

        @pl.when(i == 0)
        def _():
            s_ref[...] = part

        @pl.when(i > 0)
        def _():
            s_ref[...] += part

    vec = pl.BlockSpec((1, FB_PAD), lambda i: (0, 0))
    blk = pl.BlockSpec((tm, FB_PAD), lambda i: (i, 0))
    fb = pl.BlockSpec((tm, FB_PAD), lambda i: (i, R_FB // FB_PAD))
    return pl.pallas_call(
        body, name=name, grid=(S // tm,),
        in_specs=[fb, vec, blk, ANY],
        out_specs=[fb, vec],
        out_shape=[jax.ShapeDtypeStruct(du_r.shape, BF16), jax.ShapeDtypeStruct((1, FB_PAD), F32)],
        input_output_aliases={3: 0},
        compiler_params=_cp(("arbitrary",)))(ur, bf_pad, dlogf_pad, du_r)


M_SCALE = HEAD ** -0.5


def _mem_fwd(ur, mkv, *, name):
    S = ur.shape[0]
    T = min(512, S)

    def body(q_ref, z_ref, k_ref, v_ref, y_ref):
        for h in range(M_HEADS):
            hs = slice(h * HEAD, (h + 1) * HEAD)
            s = _dot_nt(q_ref[:, hs].astype(BF16), k_ref[:, hs].astype(BF16)) * M_SCALE
            p = jnp.exp(s - jnp.max(s, axis=-1, keepdims=True))
            p = p / jnp.sum(p, axis=-1, keepdims=True)
            o = _dot(p.astype(BF16), v_ref[:, hs].astype(BF16))
            y_ref[:, hs] = (o * _silu_parts(z_ref[:, hs])[0]).astype(BF16)

    wide = lambda col: pl.BlockSpec((T, A_WIDTH), lambda i: (i, col // A_WIDTH))
    kv = lambda half: pl.BlockSpec((N_MEM, A_WIDTH), lambda i: (0, half))
    return pl.pallas_call(
        body, name=name, grid=(S // T,),
        in_specs=[wide(R_QM), wide(R_ZM), kv(0), kv(1)],
        out_specs=pl.BlockSpec((T, A_WIDTH), lambda i: (i, 0)),
        out_shape=jax.ShapeDtypeStruct((S, A_WIDTH), BF16),
        compiler_params=_cp(("parallel",)))(ur, ur, mkv, mkv)


def _mem_bwd(ur, mkv, dy, du_r, *, name):
    S = ur.shape[0]
    T = min(512, S)

    def body(q_ref, z_ref, k_ref, v_ref, dy_ref, du_in, du_ref, dk_ref, dv_ref):
        i = pl.program_id(0)

        @pl.when(i == 0)
        def _():
            dk_ref[...] = jnp.zeros((N_MEM, A_WIDTH), F32)
            dv_ref[...] = jnp.zeros((N_MEM, A_WIDTH), F32)

        for h in range(M_HEADS):
            hs = slice(h * HEAD, (h + 1) * HEAD)
            qv = q_ref[:, hs].astype(BF16)
            kv = k_ref[:, hs].astype(BF16)
            vv = v_ref[:, hs].astype(BF16)
            s = _dot_nt(qv, kv) * M_SCALE
            p = jnp.exp(s - jnp.max(s, axis=-1, keepdims=True))
            p = p / jnp.sum(p, axis=-1, keepdims=True)
            o = _dot(p.astype(BF16), vv)
            sz, dsz = _silu_parts(z_ref[:, hs])
            dyv = dy_ref[:, hs]
            du_ref[:, A_WIDTH + h * HEAD:A_WIDTH + (h + 1) * HEAD] = (dyv * o * dsz).astype(BF16)
            dov = (dyv * sz).astype(BF16)
            dp = _dot_nt(dov, vv)
            ds = p * (dp - jnp.sum(p * dp, axis=-1, keepdims=True))
            du_ref[:, hs] = (_dot(ds.astype(BF16), kv) * M_SCALE).astype(BF16)
            dv_ref[:, hs] += _dot(p.T.astype(BF16), dov)
            dk_ref[:, hs] += _dot(ds.T.astype(BF16), qv) * M_SCALE

    wide = lambda col: pl.BlockSpec((T, A_WIDTH), lambda i: (i, col // A_WIDTH))
    kv = lambda half: pl.BlockSpec((N_MEM, A_WIDTH), lambda i: (0, half))
    tile = pl.BlockSpec((T, A_WIDTH), lambda i: (i, 0))
    acc = pl.BlockSpec((N_MEM, A_WIDTH), lambda i: (0, 0))
    assert R_ZM == R_QM + A_WIDTH and R_QM % (2 * A_WIDTH) == 0
    return pl.pallas_call(
        body, name=name, grid=(S // T,),
        in_specs=[wide(R_QM), wide(R_ZM), kv(0), kv(1), tile, ANY],
        out_specs=[pl.BlockSpec((T, 2 * A_WIDTH), lambda i: (i, R_QM // (2 * A_WIDTH))), acc, acc],
        out_shape=[jax.ShapeDtypeStruct(du_r.shape, BF16)] + [jax.ShapeDtypeStruct((N_MEM, A_WIDTH), F32)] * 2,
        input_output_aliases={5: 0},
        compiler_params=_cp(("arbitrary",)))(ur, ur, mkv, mkv, dy, du_r)


def _branch_fwd(ys, wbs, ur, b_merge, *, name):
    S = ur.shape[0]
    tm, tn = min(512, S), 512
    nj = D_MODEL // tn

    def body(ya, yb, ym, wa, wb, wm, g0, g1, g2, b0, b1, b2, mg_ref, p_ref):
        acc = jnp.zeros((tm, tn), F32)
        for i, (y, w, gr, br) in enumerate(((ya, wa, g0, b0), (yb, wb, g1, b1), (ym, wm, g2, b2))):
            pr = _dot(y[...], w[...])
            p_ref[i] = pr.astype(BF16)
            acc = acc + _sigmoid(gr[...] + br[...]) * pr
        mg_ref[...] = acc.astype(BF16)

    yspec = pl.BlockSpec((tm, A_WIDTH), lambda i, j: (i, 0))
    wspec = pl.BlockSpec((A_WIDTH, tn), lambda i, j: (0, j))
    gspec = lambda b: pl.BlockSpec((tm, tn), lambda i, j: (i, (R_GL + b * D_MODEL) // tn + j))
    bspec = lambda b: pl.BlockSpec((1, tn), lambda i, j: (0, b * nj + j))
    return pl.pallas_call(
        body, name=name, grid=(S // tm, nj),
        in_specs=[yspec] * 3 + [wspec] * 3 + [gspec(0), gspec(1), gspec(2), bspec(0), bspec(1), bspec(2)],
        out_specs=[pl.BlockSpec((tm, tn), lambda i, j: (i, j)),
                   pl.BlockSpec((3, tm, tn), lambda i, j: (0, i, j))],
        out_shape=[jax.ShapeDtypeStruct((S, D_MODEL), BF16), jax.ShapeDtypeStruct((3, S, D_MODEL), BF16)],
        compiler_params=_cp(("parallel", "parallel")))(*ys, *wbs, ur, ur, ur, b_merge, b_merge, b_merge)


def _branch_bwd(dm, prods, ur, b_merge, *, name):
    S = ur.shape[0]
    tm = min(256, S)

    def body(dm_ref, p_ref, g0, g1, g2, b_ref, dp0, dp1, dp2, dgl_ref, db_ref):
        i = pl.program_id(0)
        dmv = dm_ref[...]
        parts = []
        for b, (gr, dp_ref) in enumerate(((g0, dp0), (g1, dp1), (g2, dp2))):
            sl = slice(b * D_MODEL, (b + 1) * D_MODEL)
            gt = _sigmoid(gr[...] + b_ref[:, sl])
            dp_ref[...] = (dmv * gt).astype(BF16)
            dgl = dmv * p_ref[b].astype(F32) * gt * (1.0 - gt)
            dgl_ref[:, R_GL + b * D_MODEL:R_GL + (b + 1) * D_MODEL] = dgl.astype(BF16)
            parts.append(jnp.sum(dgl, axis=0, keepdims=True))
        part = jnp.concatenate(parts, axis=1)

        @pl.when(i == 0)
        def _():
            db_ref[...] = part

        @pl.when(i > 0)
        def _():
            db_ref[...] += part

    gspec = lambda b: pl.BlockSpec((tm, D_MODEL), lambda i: (i, R_GL // D_MODEL + b))
    vec = pl.BlockSpec((1, 3 * D_MODEL), lambda i: (0, 0))
    row = pl.BlockSpec((tm, D_MODEL), lambda i: (i, 0))
    outs = pl.pallas_call(
        body, name=name, grid=(S // tm,),
        in_specs=[row, pl.BlockSpec((3, tm, D_MODEL), lambda i: (0, i, 0)), gspec(0), gspec(1), gspec(2), vec],
        out_specs=[row, row, row, pl.BlockSpec((tm, NR), lambda i: (i, 0)), vec],
        out_shape=[jax.ShapeDtypeStruct((S, D_MODEL), BF16)] * 3
        + [jax.ShapeDtypeStruct((S, NR), BF16), jax.ShapeDtypeStruct((1, 3 * D_MODEL), F32)],
        compiler_params=_cp(("arbitrary",)))(dm, prods, ur, ur, ur, b_merge)
    return outs[0:3], outs[3], outs[4]


def _rope_tables(pos):
    half = ROT // 2
    S = pos.shape[0]
    inv = ROPE_THETA ** (-jnp.arange(half, dtype=F32) / half)
    per_row = LANES // half
    ang = jnp.repeat(pos.astype(F32).reshape(S // per_row, per_row), half, axis=1) * jnp.tile(inv, per_row)
    cos, sin = lax.optimization_barrier((jnp.cos(ang).reshape(S, half), jnp.sin(ang).reshape(S, half)))
    one = jnp.ones((S, LANES - ROT), F32)
    zero = jnp.zeros((S, LANES - ROT), F32)
    zh = jnp.zeros((S, half), F32)
    c = jnp.concatenate([cos, cos, one], axis=1)
    s1 = jnp.concatenate([-sin, zh, zero], axis=1)
    s2 = jnp.concatenate([zh, sin, zero], axis=1)
    return c, s1, s2


def _to_tiles(t):
    S, H = t.shape
    return t.reshape(S // LANES, LANES, H).transpose(0, 2, 1)


def _from_tiles(t):
    nt, H, _ = t.shape
    return t.transpose(1, 0, 2).reshape(H, nt * LANES)


def _local_step(x, mem, pos, tgt, g_pre, g_post, g_mem, wt, bf_pad, b_merge, w_kv, wbs, w_out, pack=None, hs=None):
    S = x.shape[0]
    T = min(512, S)
    nq = S // T
    tabs = _rope_tables(pos)

    if hs is None:
        hs = _rms_fwd(x, g_pre, name="rms_pre", dilations=DIL)
    h = hs[0]
    tabs_g = [[_to_classes(t, d) for t in tabs] for d in DIL]
    qkvs = [_proj_rope(hs[g], wt[f"A{g}"], tabs_g[g], name=f"proj_a{g}") for g in range(3)]
    ub = _mm(h, wt["B"], bt=True, out_dtype=BF16, name="proj_b", tn=1536)
    ur = _mm(h, wt["R"], bt=True, name="proj_r", tn=1792)

    outs_c, lses_c = [], []
    for g in range(3):
        o, l = _attn_a_fwd(qkvs[g], g, name=f"attn_a_fwd{g}")
        outs_c.append(o)
        lses_c.append(l)
    ya = _merge_a_fwd(outs_c, lses_c, ur, name="merge_a_fwd")

    logf = _logf(ur, bf_pad, name="logf")
    c = _from_tiles(_cumsum_lanes(_to_tiles(logf[:, :B_HEADS]), False, name="cumsum_fwd"))
    qaug, kaug = _fox_aug(ub, c, name="fox_aug")
    kt = ub[:, 512:1024].reshape(nq, T, 512).transpose(0, 2, 1)
    vt = ub[:, 1024:1536].reshape(nq, T, 512).transpose(0, 2, 1)
    ob, lse_b = _fox_fwd(qaug, kaug, vt, name="fox_fwd")
    yb = _gate_fwd(ob, ur, R_ZB, name="gate_b_fwd")

    hm = _rms_fwd(mem, g_mem, name="rms_mem")
    mkv = _mm(hm, w_kv, name="proj_mem")
    ym = _mem_fwd(ur, mkv, name="mem_fwd")

    merged, prods = _branch_fwd((ya, yb, ym), wbs, ur, b_merge, name="branch_fwd")
    out = _mm(merged, w_out, name="proj_out")
    dy, d_out, dg_post, loss_row = _post(x, out, tgt, g_post, name="post")

    dmerged = _mm(d_out, w_out, bt=True, name="d_merged")
    dw_out = _mm(merged, d_out, at=True, name="dw_out", tk=2048)
    dprods, du_r, db_merge = _branch_bwd(dmerged, prods, ur, b_merge, name="branch_bwd")
    dys, dwbs = [], []
    for i, (y, wb) in enumerate(zip((ya, yb, ym), wbs)):
        dys.append(_mm(dprods[i], wb, bt=True, name=f"d_y{i}"))
        dwbs.append(_mm(y, dprods[i], at=True, name=f"dw_branch{i}", tk=2048))

    dos_c, adjs_c, du_r = _merge_a_bwd(outs_c, lses_c, ur, dys[0], du_r, name="merge_a_bwd")
    dus_a = []
    for g, d in enumerate(DIL):
        do_c, adj_c = dos_c[g], adjs_c[g]
        du = _attn_a_dkv(qkvs[g], tabs_g[g], g, do_c, lses_c[g], adj_c, name=f"attn_a_dkv{g}")
        dus_a.append(_attn_a_dq(qkvs[g], tabs_g[g], g, do_c, lses_c[g], adj_c, du, name=f"attn_a_dq{g}"))

    dob, du_r = _gate_bwd(ob, ur, R_ZB, dys[1], du_r, name="gate_b_bwd")
    delta_b = _fox_delta(ob, dob, name="fox_delta")
    dkb, dvb, dc_k, dqt, dc_q = _fox_bwd(ub, qaug, kaug, kt, dob, lse_b, delta_b, name="fox_bwd")
    dqb = (dqt.transpose(0, 2, 1).reshape(S, A_WIDTH) * B_SCALE).astype(BF16)
    du_b = jnp.concatenate([dqb, dkb, dvb], axis=1)
    dc = dc_q.reshape(B_HEADS, S) + dc_k.reshape(B_HEADS, S)
    dlogf = _from_tiles(_cumsum_lanes(_to_tiles(dc.T), True, name="cumsum_bwd"))
    dlogf_pad = jnp.pad(dlogf.T, ((0, 0), (0, FB_PAD - B_HEADS)))
    du_r, db_forget = _dfb(ur, bf_pad, dlogf_pad, du_r, name="dfb")

    du_r, dmk, dmv = _mem_bwd(ur, mkv, dys[2], du_r, name="mem_bwd")
    dmkv = jnp.concatenate([dmk, dmv], axis=1).astype(BF16)
    dhm = _mm(dmkv, w_kv, bt=True, name="d_hm")
    dw_kv = _mm(hm, dmkv, at=True, name="dw_kv")
    dg_mem = _rms_bwd(mem, g_mem, dhm, None, name="rms_mem_bwd")

    dwt ={"R": _mm(du_r, h, at=True, name="dw_in_r", tm=1792, tk=1024),
           "B": _mm(du_b, h, at=True, name="dw_in_b", tm=1536, tk=2048)}
    for g in range(3):
        dwt[f"A{g}"] = _mm(dus_a[g], hs[g], at=True, name=f"dw_in_a{g}", tm=1536, tk=2048)
    res = dict(dwt=dwt, dw_kv=dw_kv, dwbs=dwbs, dw_out=dw_out)
    token_major = [(du_r, wt["R"]), (du_b, wt["B"]), (dus_a[0], wt["A0"])]
    if pack is None:
        dh_1 = _mm(dus_a[1], wt["A1"], name="d_h_a1", tk=1536)
        dh_2 = _mm(dus_a[2], wt["A2"], name="d_h_a2", tk=1536)
        dh = _mm_sum(token_major, name="d_h_main")
    else:
        gbig = pack(dwt, dw_kv, dwbs, dw_out)
        own_idx = _own_slabs()
        dh_1, sib = _mm(dus_a[1], wt["A1"], name="d_h_a1", tk=1536, comm=_pair_comm(gbig, (0, 1)))
        dh_2, sib = _mm(dus_a[2], wt["A2"], name="d_h_a2", tk=1536, comm=_pair_comm(gbig, (2, 3), sib))
        send = _pair_sum(gbig, sib, own_idx, 624, name="pair_sum")
        dh, recv = _mm_sum(token_major, name="d_h_main", comm=_chips_comm(send))
        res = dict(parts=[(gbig, None), (sib, 1), (recv, N_CHIP - 1)], own_idx=own_idx)
    grad_x, dg_pre = _rms_bwd(x, g_pre, dh, dy, name="rms_pre_bwd", dh_classes=[(dh_1, DIL[1]), (dh_2, DIL[2])])

    return dict(res, loss=loss_row, grad_x=grad_x, dg_pre=dg_pre, dg_post=dg_post, dg_mem=dg_mem,
                db_forget=db_forget, db_merge=db_merge)


MESH = pl.DeviceIdType.MESH
ANY = pl.BlockSpec(memory_space=pl.ANY)


def _relations():
    return [(k >> 2 & 1, k >> 1 & 1, k & 1) for k in range(1, N_DEV)]


def _coords():
    return lax.axis_index("x"), lax.axis_index("y"), lax.axis_index("c")


def _gather_comm(shard):
    R, W = shard.shape

    def plan(x_ref, out_ref, send_sems, recv_sems, local_sem):
        x, y, c = _coords()
        me, sibling = (x, y, c), (x, y, 1 - c)
        chips = [(1 - x, y), (x, 1 - y), (1 - x, 1 - y)]

        def slot(px, py, pc):
            return out_ref.at[4 * px + 2 * py + pc]

        def copy(k, block, to, src=None):
            return pltpu.make_async_remote_copy(
                src_ref=slot(*block) if src is None else src, dst_ref=slot(*block),
                send_sem=send_sems.at[k], recv_sem=recv_sems.at[k], device_id=to, device_id_type=MESH)

        mine = pltpu.make_async_copy(x_ref, slot(*me), local_sem)
        first = [copy(0, me, sibling, src=x_ref)]
        first += [copy(1 + j, me, (*chip, c), src=x_ref) for j, chip in enumerate(chips)]
        return me, sibling, chips, c, copy, mine, first

    def start(*refs):
        _, _, _, _, _, mine, first = plan(*refs)
        mine.start()
        for cp in first:
            cp.start()

    def wait(*refs):
        me, sibling, chips, c, copy, mine, first = plan(*refs)
        passed = [copy(4 + j, (*chip, c), sibling) for j, chip in enumerate(chips)]
        for j, chip in enumerate(chips):
            copy(1 + j, (*chip, c), me).wait_recv()
            passed[j].start()
        copy(0, sibling, me).wait_recv()
        for j, chip in enumerate(chips):
            copy(4 + j, (*chip, 1 - c), me).wait_recv()
        for cp in first + passed:
            cp.wait_send()
        mine.wait()

    return dict(inputs=[shard], out_shape=[jax.ShapeDtypeStruct((N_DEV, R, W), shard.dtype)],
                sems=[pltpu.SemaphoreType.DMA((N_DEV - 1,)), pltpu.SemaphoreType.DMA((N_DEV - 1,)),
                      pltpu.SemaphoreType.DMA],
                start=start, wait=wait)


N_CHIP = 4


def _pair_comm(gbig, rels, sib=None):
    _, R, W = gbig.shape

    def copies(g_ref, *rest):
        sib_ref, send_sems, recv_sems = rest[-3:]
        x, y, c = _coords()
        return [pltpu.make_async_remote_copy(
            src_ref=g_ref.at[4 * (x ^ (r >> 1)) + 2 * (y ^ (r & 1)) + (1 - c)], dst_ref=sib_ref.at[r],
            send_sem=send_sems.at[k], recv_sem=recv_sems.at[k], device_id=(x, y, 1 - c), device_id_type=MESH)
            for k, r in enumerate(rels)]

    def start(*refs):
        for cp in copies(*refs):
            cp.start()

    def wait(*refs):
        cps = copies(*refs)
        for cp in cps:
            cp.wait_recv()
        for cp in cps:
            cp.wait_send()

    return dict(inputs=[gbig] if sib is None else [gbig, sib],
                out_shape=[jax.ShapeDtypeStruct((N_CHIP, R, W), gbig.dtype)],
                alias={} if sib is None else {1: 0},
                sems=[pltpu.SemaphoreType.DMA((len(rels),)), pltpu.SemaphoreType.DMA((len(rels),))],
                start=start, wait=wait)


def _own_slabs():
    x, y, c = _coords()
    return jnp.stack([4 * (x ^ (r >> 1)) + 2 * (y ^ (r & 1)) + c for r in range(N_CHIP)]).astype(jnp.int32)


def _pair_sum(gbig, sib, own_idx, tr, *, name):
    _, R, W = gbig.shape

    def body(idx_ref, a_ref, b_ref, o_ref):
        o_ref[...] = (a_ref[...] + b_ref[...]).astype(BF16)

    return pl.pallas_call(
        body, name=name,
        grid_spec=pltpu.PrefetchScalarGridSpec(
            num_scalar_prefetch=1, grid=(N_CHIP - 1, R // tr),
            in_specs=[pl.BlockSpec((None, tr, W), lambda r, i, idx: (idx[r + 1], i, 0)),
                      pl.BlockSpec((None, tr, W), lambda r, i, idx: (r + 1, i, 0))],
            out_specs=pl.BlockSpec((None, tr, W), lambda r, i, idx: (r, i, 0))),
        out_shape=jax.ShapeDtypeStruct((N_CHIP - 1, R, W), BF16),
        compiler_params=_cp(("parallel", "parallel")))(own_idx, gbig, sib)


def _chips_comm(send):
    nb, R, W = send.shape

    def copies(b_ref, rb_ref, send_sems, recv_sems):
        x, y, c = _coords()
        return [pltpu.make_async_remote_copy(
            src_ref=b_ref.at[r - 1], dst_ref=rb_ref.at[r - 1], send_sem=send_sems.at[r - 1],
            recv_sem=recv_sems.at[r - 1], device_id=(x ^ (r >> 1), y ^ (r & 1), c), device_id_type=MESH)
            for r in range(1, N_CHIP)]

    def start(*refs):
        for cp in copies(*refs):
            cp.start()

    def wait(*refs):
        cps = copies(*refs)
        for cp in cps:
            cp.wait_recv()
        for cp in cps:
            cp.wait_send()

    return dict(inputs=[send], out_shape=[jax.ShapeDtypeStruct((nb, R, W), send.dtype)],
                sems=[pltpu.SemaphoreType.DMA((nb,)), pltpu.SemaphoreType.DMA((nb,))],
                start=start, wait=wait)


def _gather_small(gsmall, *, name):
    n = N_DEV - 1

    def body(s_ref, rs_ref, send_sems, recv_sems, local_sem):
        x, y, c = _coords()
        me = 4 * x + 2 * y + c
        mine = pltpu.make_async_copy(s_ref, rs_ref.at[me], local_sem)
        mine.start()

        def copy(k, fx, fy, fc, slot):
            return pltpu.make_async_remote_copy(
                src_ref=s_ref, dst_ref=rs_ref.at[slot], send_sem=send_sems.at[k], recv_sem=recv_sems.at[k],
                device_id=(x ^ fx, y ^ fy, c ^ fc), device_id_type=MESH)

        started = [copy(k, *rel, me) for k, rel in enumerate(_relations())]
        for cp in started:
            cp.start()
        for k, (fx, fy, fc) in enumerate(_relations()):
            copy(k, fx, fy, fc, 4 * (x ^ fx) + 2 * (y ^ fy) + (c ^ fc)).wait_recv()
        for cp in started:
            cp.wait_send()
        mine.wait()

    return pl.pallas_call(
        body, name=name, out_shape=jax.ShapeDtypeStruct((N_DEV, 1, P_SMALL), gsmall.dtype),
        in_specs=[ANY], out_specs=ANY,
        scratch_shapes=[pltpu.SemaphoreType.DMA((n,)), pltpu.SemaphoreType.DMA((n,)), pltpu.SemaphoreType.DMA],
    )(gsmall)


def _part_specs(parts, tr, row0):
    assert row0 % tr == 0
    specs = []
    for a, n_used in parts:
        if n_used is None:
            specs.append(pl.BlockSpec((1, tr, a.shape[2]), lambda i, idx: (idx[0], row0 // tr + i, 0)))
        else:
            specs.append(pl.BlockSpec((n_used, tr, a.shape[2]), lambda i, idx: (0, row0 // tr + i, 0)))
    return specs


def _part_total(refs, parts):
    g = None
    for ref, (_, n_used) in zip(refs, parts):
        for k in range(n_used or 1):
            t = ref[k].astype(F32)
            g = t if g is None else g + t
    return g


def _sum_parts(parts, idx, row0, nrows, tr, *, name):
    W = parts[0][0].shape[2]
    assert nrows % tr == 0

    def body(idx_ref, *refs):
        refs[-1][...] = _part_total(refs[:-1], parts)

    return pl.pallas_call(
        body, name=name,
        grid_spec=pltpu.PrefetchScalarGridSpec(
            num_scalar_prefetch=1, grid=(nrows // tr,), in_specs=_part_specs(parts, tr, row0),
            out_specs=pl.BlockSpec((tr, W), lambda i, idx: (i, 0))),
        out_shape=jax.ShapeDtypeStruct((nrows, W), F32),
        compiler_params=_cp(("parallel",)))(idx, *[a for a, _ in parts])


def _adamw(parts, idx, w, m, v, tr, *, name):
    R, W = w.shape
    assert R % tr == 0
    np_ = len(parts)

    def body(idx_ref, *refs):
        w_ref, m_ref, v_ref, g_ref, d_ref, nm_ref, nv_ref = refs[np_:]
        g = _part_total(refs[:np_], parts)
        mm = ADAM_B1 * m_ref[...] + (1.0 - ADAM_B1) * g
        vv = ADAM_B2 * v_ref[...] + (1.0 - ADAM_B2) * (g * g)
        m_hat = mm / (1.0 - ADAM_B1 ** ADAM_STEP)
        v_hat = vv / (1.0 - ADAM_B2 ** ADAM_STEP)
        g_ref[...] = g
        d_ref[...] = -ADAM_LR * (m_hat / (jnp.sqrt(v_hat) + ADAM_EPS) + ADAM_WD * w_ref[...])
        nm_ref[...] = mm
        nv_ref[...] = vv

    blk = pl.BlockSpec((tr, W), lambda i, idx: (i, 0))
    return pl.pallas_call(
        body, name=name,
        grid_spec=pltpu.PrefetchScalarGridSpec(
            num_scalar_prefetch=1, grid=(R // tr,), in_specs=_part_specs(parts, tr, 0) + [blk, blk, blk],
            out_specs=[blk] * 4),
        out_shape=[jax.ShapeDtypeStruct((R, W), F32)] * 4,
        compiler_params=_cp(("parallel",)))(idx, *[a for a, _ in parts], w, m, v)


def _pack_rest(w_kv, wa, wb, wm, w_out):
    return jnp.concatenate([w_kv[0], w_out[0]] + [t[0].reshape(-1, D_MODEL) for t in (wa, wb, wm)], axis=0)


def _unpack_rest(t):
    br = lambda i: t[RO_BR + 64 * i:RO_BR + 64 * (i + 1)].reshape(1, A_WIDTH, D_MODEL // N_DEV)
    return t[None, RO_KV:RO_OUT], br(0), br(1), br(2), t[None, RO_OUT:RO_BR]


def _orig_rows(gathered, a, b):
    res = []
    while a < b:
        dev, r = divmod(a, CS)
        n = min(b - a, CS - r)
        res.append(gathered[dev, RO_IN + r:RO_IN + r + n])
        a += n
    return res


def _full_weights(gathered):
    wt = {}
    for name, ranges in SEGS.items():
        rows = [p for a, b in ranges for p in _orig_rows(gathered, a, b)]
        if SEG_PAD[name]:
            rows.append(jnp.zeros((SEG_PAD[name], D_MODEL), gathered.dtype))
        wt[name] = jnp.concatenate(rows, axis=0)
    w_kv = gathered[:, RO_KV:RO_OUT].reshape(D_MODEL, D_MODEL)
    w_out = gathered[:, RO_OUT:RO_BR].reshape(D_MODEL, D_MODEL)
    wbs = [gathered[:, RO_BR + 64 * i:RO_BR + 64 * (i + 1)].reshape(N_DEV, A_WIDTH, D_MODEL // N_DEV)
           .transpose(1, 0, 2).reshape(A_WIDTH, D_MODEL) for i in range(3)]
    return wt, w_kv, wbs, w_out


def _orig_order(dwt):
    pieces = []
    for name, ranges in SEGS.items():
        o = 0
        for a, b in ranges:
            pieces.append((a, dwt[name][o:o + b - a]))
            o += b - a
    pieces.sort(key=lambda p: p[0])
    return jnp.concatenate([p[1] for p in pieces], axis=0)


def _pack_grads(dwt, dw_kv, dwbs, dw_out):
    g_in = jnp.pad(_orig_order(dwt).reshape(N_DEV, CS, D_MODEL), ((0, 0), (0, IN_ROWS - CS), (0, 0)))
    br = [t.reshape(A_WIDTH, N_DEV, D_MODEL // N_DEV).transpose(1, 0, 2).reshape(N_DEV, -1, D_MODEL) for t in dwbs]
    return jnp.concatenate([dw_kv.reshape(N_DEV, -1, D_MODEL), dw_out.reshape(N_DEV, -1, D_MODEL)] + br + [g_in],
                           axis=1)


def kernel(x, mem, positions, norm_pre_g, norm_post_g, norm_mem_g, w_in, b_forget, b_merge, w_mem_kv, w_branch_a, w_branch_b, w_branch_m, w_out, loss_target, m_norm_pre_g, m_norm_post_g, m_norm_mem_g, m_w_in, m_b_forget, m_b_merge, m_w_mem_kv, m_w_branch_a, m_w_branch_b, m_w_branch_m, m_w_out, v_norm_pre_g, v_norm_post_g, v_norm_mem_g, v_w_in, v_b_forget, v_b_merge, v_w_mem_kv, v_w_branch_a, v_w_branch_b, v_w_branch_m, v_w_out):
    w_rest = _pack_rest(w_mem_kv, w_branch_a, w_branch_b, w_branch_m, w_out)
    shard = jnp.concatenate([w_rest.astype(BF16), w_in[0].T.astype(BF16),
                             jnp.zeros((IN_ROWS - CS, D_MODEL), BF16)], axis=0)
    hs, (gathered,) = _rms_fwd(x[0], norm_pre_g, name="rms_pre_gather", dilations=DIL, comm=_gather_comm(shard))
    wt, w_kv, wbs, w_o = _full_weights(gathered)

    bf_pad = jnp.pad(b_forget, ((0, 0), (0, FB_PAD - B_HEADS)))
    r = _local_step(x[0], mem[0], positions[0], loss_target[0], norm_pre_g, norm_post_g, norm_mem_g,
                    wt, bf_pad, b_merge, w_kv, wbs, w_o, pack=_pack_grads, hs=hs)

    gsmall = jnp.concatenate([r["dg_pre"], r["dg_post"], r["dg_mem"], r["db_merge"],
                              r["db_forget"][:, :LANES], r["loss"]], axis=1)
    rsmall = _gather_small(gsmall, name="gather_small")
    parts, own_idx = r["parts"], r["own_idx"]

    m_rest = _pack_rest(m_w_mem_kv, m_w_branch_a, m_w_branch_b, m_w_branch_m, m_w_out)
    v_rest = _pack_rest(v_w_mem_kv, v_w_branch_a, v_w_branch_b, v_w_branch_m, v_w_out)
    gsum = _sum_parts(parts, own_idx, 0, ROWS, 624, name="sum_grads")
    outs_rest = [_unpack_rest(t) for t in
                 _adamw([(gsum[None], 1)], own_idx, w_rest, m_rest, v_rest, 64, name="adamw_rest")]
    g_in = gsum[RO_IN:RO_IN + CS].T
    outs_in = _adamw([(g_in[None], 1)], own_idx, w_in[0], m_w_in[0], v_w_in[0], 128, name="adamw_w_in")

    def small_vec(a, b, c, d, e):
        z = jnp.zeros((1, LANES - B_HEADS), F32)
        return jnp.concatenate([a, b, c, d, e, z, jnp.zeros((1, LANES), F32)], axis=1)

    outs_small = _adamw([(rsmall, N_DEV)], own_idx, small_vec(norm_pre_g, norm_post_g, norm_mem_g, b_merge, b_forget),
                        small_vec(m_norm_pre_g, m_norm_post_g, m_norm_mem_g, m_b_merge, m_b_forget),
                        small_vec(v_norm_pre_g, v_norm_post_g, v_norm_mem_g, v_b_merge, v_b_forget),
                        1, name="adamw_small")

    def small_parts(t):
        return [t[:, O_GPRE:O_GPRE + D_MODEL], t[:, O_GPOST:O_GPOST + D_MODEL], t[:, O_GMEM:O_GMEM + D_MODEL],
                t[:, O_BF:O_BF + B_HEADS], t[:, O_BM:O_BM + 3 * D_MODEL]]

    loss = outs_small[0][0, O_LOSS]
    result = [loss, r["grad_x"][None]]
    for rest, w_i, small in zip(outs_rest, outs_in, outs_small):
        gp, gq, gm, bf, bm = small_parts(small)
        w_k, w_a, w_b, w_m, w_ot = rest
        result += [gp, gq, gm, w_i[None], bf, bm, w_k, w_a, w_b, w_m, w_ot]
    return tuple(result)
```
